```python
import math
import jax, jax.numpy as jnp
from jax import lax
import numpy as np

D_MODEL = 1024
BATCH = 8
SEQ = 4096
DEPTH = 1

N_META = 16
HEAD_DIM = 64
SWA_Q_HEADS = D_MODEL // (2 * HEAD_DIM)
SWA_KV_HEADS = max(SWA_Q_HEADS // 4, 1)
SWA_GROUP = SWA_Q_HEADS // SWA_KV_HEADS
FOX_HEADS = D_MODEL // (2 * HEAD_DIM)
SWA_Q_W = SWA_Q_HEADS * HEAD_DIM
SWA_KV_W = SWA_KV_HEADS * HEAD_DIM
FOX_W = FOX_HEADS * HEAD_DIM
D_MIX = SWA_Q_W + FOX_W
OFF_QA = SWA_Q_W
OFF_KA = OFF_QA + SWA_KV_W
OFF_VA = OFF_KA + SWA_KV_W
OFF_QB = OFF_VA + FOX_W
OFF_KB = OFF_QB + FOX_W
OFF_VB = OFF_KB + FOX_W
D_PROJ = OFF_VB + FOX_HEADS
WINDOW = 128
BLOCK = 128
N_BUCKETS = 32
MAX_DISTANCE = 128
D_FF = -(-8 * D_MODEL // (3 * 256)) * 256
EPS = 1e-6
NEG_INF = -1e30

kernel_name = "hymba_swa_sink_fox_t5bias_sandwich"


def rms_norm(x, g):
    xf = x.astype(jnp.float32)
    y = xf * lax.rsqrt(jnp.mean(xf * xf, axis=-1, keepdims=True) + EPS)
    return (y * g.astype(jnp.float32)).astype(x.dtype)


def t5_bucket(dist):
    n = jnp.maximum(dist, 0).astype(jnp.int32)
    max_exact = N_BUCKETS // 2
    nf = jnp.maximum(n, 1).astype(jnp.float32)
    large = max_exact + (jnp.log(nf / max_exact) / math.log(MAX_DISTANCE / max_exact)
                         * (N_BUCKETS - max_exact)).astype(jnp.int32)
    large = jnp.minimum(large, N_BUCKETS - 1)
    return jnp.where(n < max_exact, n, large)


def softmax_with_sink(s, sink):
    sink_r = sink.reshape((1,) + sink.shape + (1,) * (s.ndim - 3))
    col = jnp.broadcast_to(sink_r, s.shape[:-1] + (1,))
    p = jax.nn.softmax(jnp.concatenate([s, col], axis=-1), axis=-1)
    return p[..., :-1]


def swa_sink_attention(q, k, v, sinks, rel_bias):
    B, L = q.shape[0], q.shape[1]
    n_blk = (L - N_META) // BLOCK
    scale = HEAD_DIM ** -0.5
    sink = sinks.astype(jnp.float32).reshape(SWA_KV_HEADS, SWA_GROUP)
    tab = rel_bias.astype(jnp.float32)
    mi = jnp.arange(N_META)
    km, vm = k[:, :N_META], v[:, :N_META]

    qm = q[:, :N_META].reshape(B, N_META, SWA_KV_HEADS, SWA_GROUP, HEAD_DIM)
    d_mm = mi[:, None] - mi[None, :]
    b_mm = tab[t5_bucket(d_mm)].transpose(2, 0, 1).reshape(SWA_KV_HEADS, SWA_GROUP, N_META, N_META)
    s_mm = jnp.einsum('bqhgd,bkhd->bhgqk', qm, km, preferred_element_type=jnp.float32) * scale + b_mm
    s_mm = jnp.where(d_mm >= 0, s_mm, NEG_INF)
    p_mm = softmax_with_sink(s_mm, sink).astype(v.dtype)
    o_meta = jnp.einsum('bhgqk,bkhd->bqhgd', p_mm, vm).reshape(B, N_META, SWA_Q_HEADS, HEAD_DIM)

    qr = q[:, N_META:].reshape(B, n_blk, BLOCK, SWA_KV_HEADS, SWA_GROUP, HEAD_DIM)
    kr = k[:, N_META:].reshape(B, n_blk, BLOCK, SWA_KV_HEADS, HEAD_DIM)
    vr = v[:, N_META:].reshape(B, n_blk, BLOCK, SWA_KV_HEADS, HEAD_DIM)

    def with_prev(t):
        prev = jnp.pad(t, ((0, 0), (1, 0), (0, 0), (0, 0), (0, 0)))[:, :-1]
        return jnp.concatenate([prev, t], axis=2)

    kw, vw = with_prev(kr), with_prev(vr)
    qi = jnp.arange(BLOCK)[:, None]
    ki = jnp.arange(2 * BLOCK)[None, :]
    d_w = qi + BLOCK - ki
    b_w = tab[t5_bucket(d_w)].transpose(2, 0, 1).reshape(SWA_KV_HEADS, SWA_GROUP, 1, BLOCK, 2 * BLOCK)
    blk = jnp.arange(n_blk)[:, None, None]
    valid_w = ((d_w >= 0) & (d_w < WINDOW))[None] & ((blk > 0) | (ki >= BLOCK)[None])
    s_w = jnp.einsum('bnqhgd,bnkhd->bhgnqk', qr, kw, preferred_element_type=jnp.float32) * scale + b_w
    s_w = jnp.where(valid_w, s_w, NEG_INF)
    d_m = N_META + blk * BLOCK + qi[None] - mi[None, None, :]
    b_m = tab[t5_bucket(d_m)].transpose(3, 0, 1, 2).reshape(SWA_KV_HEADS, SWA_GROUP, n_blk, BLOCK, N_META)
    s_m = jnp.einsum('bnqhgd,bmhd->bhgnqm', qr, km, preferred_element_type=jnp.float32) * scale + b_m
    p = softmax_with_sink(jnp.concatenate([s_m, s_w], axis=-1), sink).astype(v.dtype)
    o = (jnp.einsum('bhgnqm,bmhd->bnqhgd', p[..., :N_META], vm)
         + jnp.einsum('bhgnqk,bnkhd->bnqhgd', p[..., N_META:], vw))
    o_real = o.reshape(B, n_blk * BLOCK, SWA_Q_HEADS, HEAD_DIM)
    return jnp.concatenate([o_meta, o_real], axis=1)


def forgetting_attention(q, k, v, f_logit):
    B, L = q.shape[0], q.shape[1]
    n_blk = (L - N_META) // BLOCK
    scale = HEAD_DIM ** -0.5
    cum = jnp.cumsum(jax.nn.log_sigmoid(f_logit.astype(jnp.float32)), axis=1).transpose(0, 2, 1)

    mi = jnp.arange(N_META)
    s = jnp.einsum('bqhd,bkhd->bhqk', q[:, :N_META], k[:, :N_META], preferred_element_type=jnp.float32) * scale
    s = s + cum[:, :, :N_META, None] - cum[:, :, None, :N_META]
    s = jnp.where(mi[:, None] >= mi[None, :], s, NEG_INF)
    p = jax.nn.softmax(s, axis=-1).astype(v.dtype)
    o_meta = jnp.einsum('bhqk,bkhd->bqhd', p, v[:, :N_META])

    qb = q[:, N_META:].reshape(B, n_blk, BLOCK, FOX_HEADS, HEAD_DIM).transpose(1, 0, 2, 3, 4)
    cb = cum[:, :, N_META:].reshape(B, FOX_HEADS, n_blk, BLOCK).transpose(2, 0, 1, 3)
    k_pos = jnp.arange(L)

    def one_block(args):
        q_blk, c_blk, b = args
        q_pos = N_META + b * BLOCK + jnp.arange(BLOCK)
        sb = jnp.einsum('bqhd,bkhd->bhqk', q_blk, k, preferred_element_type=jnp.float32) * scale
        sb = sb + c_blk[..., None] - cum[:, :, None, :]
        sb = jnp.where(k_pos[None, :] <= q_pos[:, None], sb, NEG_INF)
        pb = jax.nn.softmax(sb, axis=-1).astype(v.dtype)
        return jnp.einsum('bhqk,bkhd->bqhd', pb, v)

    o = lax.map(one_block, (qb, cb, jnp.arange(n_blk)))
    o_real = o.transpose(1, 0, 2, 3, 4).reshape(B, n_blk * BLOCK, FOX_HEADS, HEAD_DIM)
    return jnp.concatenate([o_meta, o_real], axis=1)


def _fwd_setup_inputs(seed: int = 0) -> dict:
    key = jax.random.key(seed)
    ks = jax.random.split(key, 14)
    f32 = jnp.float32
    gain = lambda k: 1.0 + 0.05 * jax.random.normal(k, (DEPTH, D_MODEL), f32)
    return {
        "x": jax.random.normal(ks[0], (BATCH, SEQ, D_MODEL), f32),
        "meta_tokens": jax.random.normal(ks[1], (N_META, D_MODEL), f32),
        "rel_bias": 0.5 * jax.random.normal(ks[2], (N_BUCKETS, SWA_Q_HEADS), f32),
        "ln_pre_mix": gain(ks[3]),
        "ln_post_mix": gain(ks[4]),
        "ln_pre_ffn": gain(ks[5]),
        "ln_post_ffn": gain(ks[6]),
        "w_in": jax.random.normal(ks[7], (DEPTH, D_MODEL, D_PROJ), f32) * D_MODEL ** -0.5,
        "b_forget": jax.random.uniform(ks[8], (DEPTH, FOX_HEADS), f32, minval=1.0, maxval=5.0),
        "sinks": 0.5 * jax.random.normal(ks[9], (DEPTH, SWA_Q_HEADS), f32),
        "w_out": jax.random.normal(ks[10], (DEPTH, D_MIX, D_MODEL), f32) * D_MIX ** -0.5,
        "w_gate_up": jax.random.normal(ks[11], (DEPTH, D_MODEL, 2 * D_FF), f32) * D_MODEL ** -0.5,
        "w_down": jax.random.normal(ks[12], (DEPTH, D_FF, D_MODEL), f32) * D_FF ** -0.5,
    }


def _fwd_reference(x, meta_tokens, rel_bias, ln_pre_mix, ln_post_mix, ln_pre_ffn, ln_post_ffn,
              w_in, b_forget, sinks, w_out, w_gate_up, w_down):
    B = x.shape[0]
    meta = jnp.broadcast_to(meta_tokens[None].astype(x.dtype), (B, N_META, D_MODEL))
    h = jnp.concatenate([meta, x], axis=1)
    L = h.shape[1]
    for layer in range(DEPTH):
        hn = rms_norm(h, ln_pre_mix[layer])
        proj = jnp.einsum('bld,dc->blc', hn, w_in[layer])
        q_a = proj[..., :OFF_QA].reshape(B, L, SWA_Q_HEADS, HEAD_DIM)
        k_a = proj[..., OFF_QA:OFF_KA].reshape(B, L, SWA_KV_HEADS, HEAD_DIM)
        v_a = proj[..., OFF_KA:OFF_VA].reshape(B, L, SWA_KV_HEADS, HEAD_DIM)
        q_b = proj[..., OFF_VA:OFF_QB].reshape(B, L, FOX_HEADS, HEAD_DIM)
        k_b = proj[..., OFF_QB:OFF_KB].reshape(B, L, FOX_HEADS, HEAD_DIM)
        v_b = proj[..., OFF_KB:OFF_VB].reshape(B, L, FOX_HEADS, HEAD_DIM)
        f_b = proj[..., OFF_VB:] + b_forget[layer].astype(proj.dtype)
        o_a = swa_sink_attention(q_a, k_a, v_a, sinks[layer], rel_bias)
        o_b = forgetting_attention(q_b, k_b, v_b, f_b)
        mix = jnp.concatenate([o_a.reshape(B, L, SWA_Q_W), o_b.reshape(B, L, FOX_W)], axis=-1)
        h = h + rms_norm(jnp.einsum('blc,cd->bld', mix, w_out[layer]), ln_post_mix[layer])
        hn = rms_norm(h, ln_pre_ffn[layer])
        gu = jnp.einsum('bld,df->blf', hn, w_gate_up[layer])
        ff = jnp.einsum('blf,fd->bld', jax.nn.silu(gu[..., :D_FF]) * gu[..., D_FF:], w_down[layer])
        h = h + rms_norm(ff, ln_post_ffn[layer])
    return h[:, N_META:]


import jax as _jax
import jax.numpy as _jnp

TWIN_FORMAT = 'train_step'
FWD_PARAMS = ['x', 'meta_tokens', 'rel_bias', 'ln_pre_mix', 'ln_post_mix', 'ln_pre_ffn', 'ln_post_ffn', 'w_in', 'b_forget', 'sinks', 'w_out', 'w_gate_up', 'w_down']
TWIN_WEIGHTS = ['meta_tokens', 'rel_bias', 'ln_pre_mix', 'ln_post_mix', 'ln_pre_ffn', 'ln_post_ffn', 'w_in', 'b_forget', 'sinks', 'w_out', 'w_gate_up', 'w_down']
TWIN_DIFF_INPUT = 'x'
TWIN_INPUTS = ['x', 'meta_tokens', 'rel_bias', 'ln_pre_mix', 'ln_post_mix', 'ln_pre_ffn', 'ln_post_ffn', 'w_in', 'b_forget', 'sinks', 'w_out', 'w_gate_up', 'w_down', 'loss_target', 'm_meta_tokens', 'm_rel_bias', 'm_ln_pre_mix', 'm_ln_post_mix', 'm_ln_pre_ffn', 'm_ln_post_ffn', 'm_w_in', 'm_b_forget', 'm_sinks', 'm_w_out', 'm_w_gate_up', 'm_w_down', 'v_meta_tokens', 'v_rel_bias', 'v_ln_pre_mix', 'v_ln_post_mix', 'v_ln_pre_ffn', 'v_ln_post_ffn', 'v_w_in', 'v_b_forget', 'v_sinks', 'v_w_out', 'v_w_gate_up', 'v_w_down']
TWIN_OUTPUTS = ['loss', 'grad_x', 'grad_meta_tokens', 'grad_rel_bias', 'grad_ln_pre_mix', 'grad_ln_post_mix', 'grad_ln_pre_ffn', 'grad_ln_post_ffn', 'grad_w_in', 'grad_b_forget', 'grad_sinks', 'grad_w_out', 'grad_w_gate_up', 'grad_w_down', 'delta_meta_tokens', 'delta_rel_bias', 'delta_ln_pre_mix', 'delta_ln_post_mix', 'delta_ln_pre_ffn', 'delta_ln_post_ffn', 'delta_w_in', 'delta_b_forget', 'delta_sinks', 'delta_w_out', 'delta_w_gate_up', 'delta_w_down', 'new_m_meta_tokens', 'new_m_rel_bias', 'new_m_ln_pre_mix', 'new_m_ln_post_mix', 'new_m_ln_pre_ffn', 'new_m_ln_post_ffn', 'new_m_w_in', 'new_m_b_forget', 'new_m_sinks', 'new_m_w_out', 'new_m_w_gate_up', 'new_m_w_down', 'new_v_meta_tokens', 'new_v_rel_bias', 'new_v_ln_pre_mix', 'new_v_ln_post_mix', 'new_v_ln_pre_ffn', 'new_v_ln_post_ffn', 'new_v_w_in', 'new_v_b_forget', 'new_v_sinks', 'new_v_w_out', 'new_v_w_gate_up', 'new_v_w_down']
TWIN_LEAF_KINDS = {'loss': 'loss', 'grad_x': 'grad_x', 'grad_meta_tokens': 'grad_w', 'grad_rel_bias': 'grad_w', 'grad_ln_pre_mix': 'grad_w', 'grad_ln_post_mix': 'grad_w', 'grad_ln_pre_ffn': 'grad_w', 'grad_ln_post_ffn': 'grad_w', 'grad_w_in': 'grad_w', 'grad_b_forget': 'grad_w', 'grad_sinks': 'grad_w', 'grad_w_out': 'grad_w', 'grad_w_gate_up': 'grad_w', 'grad_w_down': 'grad_w', 'delta_meta_tokens': 'delta_w', 'delta_rel_bias': 'delta_w', 'delta_ln_pre_mix': 'delta_w', 'delta_ln_post_mix': 'delta_w', 'delta_ln_pre_ffn': 'delta_w', 'delta_ln_post_ffn': 'delta_w', 'delta_w_in': 'delta_w', 'delta_b_forget': 'delta_w', 'delta_sinks': 'delta_w', 'delta_w_out': 'delta_w', 'delta_w_gate_up': 'delta_w', 'delta_w_down': 'delta_w', 'new_m_meta_tokens': 'new_m', 'new_m_rel_bias': 'new_m', 'new_m_ln_pre_mix': 'new_m', 'new_m_ln_post_mix': 'new_m', 'new_m_ln_pre_ffn': 'new_m', 'new_m_ln_post_ffn': 'new_m', 'new_m_w_in': 'new_m', 'new_m_b_forget': 'new_m', 'new_m_sinks': 'new_m', 'new_m_w_out': 'new_m', 'new_m_w_gate_up': 'new_m', 'new_m_w_down': 'new_m', 'new_v_meta_tokens': 'new_v', 'new_v_rel_bias': 'new_v', 'new_v_ln_pre_mix': 'new_v', 'new_v_ln_post_mix': 'new_v', 'new_v_ln_pre_ffn': 'new_v', 'new_v_ln_post_ffn': 'new_v', 'new_v_w_in': 'new_v', 'new_v_b_forget': 'new_v', 'new_v_sinks': 'new_v', 'new_v_w_out': 'new_v', 'new_v_w_gate_up': 'new_v', 'new_v_w_down': 'new_v'}


def _forward(args):
    return _fwd_reference(*[args[k] for k in FWD_PARAMS])


def _output_shape():
    def fwd():
        inp = _fwd_setup_inputs(0)
        return _fwd_reference(*[inp[k] for k in FWD_PARAMS])
    out = _jax.eval_shape(fwd)
    return out.shape, out.dtype

N_MICROBATCH = 1
ADAM_LR = 0.001
ADAM_B1 = 0.9
ADAM_B2 = 0.999
ADAM_EPS = 1e-08
ADAM_WD = 0.01
ADAM_STEP = 10
PER_EXAMPLE_BATCH_AXIS = {'x': 0, 'loss_target': 0}
SHARED_INPUTS = []
_WEIGHT_DTYPES = {'meta_tokens': _jnp.float32, 'rel_bias': _jnp.float32, 'ln_pre_mix': _jnp.float32, 'ln_post_mix': _jnp.float32, 'ln_pre_ffn': _jnp.float32, 'ln_post_ffn': _jnp.float32, 'w_in': _jnp.float32, 'b_forget': _jnp.float32, 'sinks': _jnp.float32, 'w_out': _jnp.float32, 'w_gate_up': _jnp.float32, 'w_down': _jnp.float32}
MOMENT_SCALE = {'meta_tokens': 4.125696e-02, 'rel_bias': 3.846072e-01, 'ln_pre_mix': 7.520218e-01, 'ln_post_mix': 3.197812e+01, 'ln_pre_ffn': 6.349895e-01, 'ln_post_ffn': 3.200197e+01, 'w_in': 5.120380e-01, 'b_forget': 5.145161e+00, 'sinks': 3.517687e-02, 'w_out': 6.903657e-01, 'w_gate_up': 2.779514e-01, 'w_down': 5.770451e-01}


def _to_microbatches(a, axis):
    t = _jnp.moveaxis(a, axis, 0)
    t = t.reshape((N_MICROBATCH, t.shape[0] // N_MICROBATCH) + t.shape[1:])
    return _jnp.moveaxis(t, 1, axis + 1)


def setup_inputs(seed: int = 0) -> dict:
    inp = _fwd_setup_inputs(seed)
    key = _jax.random.fold_in(_jax.random.key(seed), 7919)
    shape, _ = _output_shape()
    out = dict(inp)
    out["loss_target"] = _jax.random.normal(_jax.random.fold_in(key, 0), shape, _jnp.float32)
    for i, name in enumerate(TWIN_WEIGHTS):
        w = inp[name].astype(_jnp.float32)
        if MOMENT_SCALE is None:
            s = _jnp.sqrt(_jnp.mean(_jnp.square(w)) + 1e-30)
        else:
            s = MOMENT_SCALE[name]
        km, kv = _jax.random.split(_jax.random.fold_in(key, i + 1))
        out[name] = w
        out["m_" + name] = s * _jax.random.normal(km, w.shape, _jnp.float32)
        out["v_" + name] = (s * s) * _jax.random.uniform(kv, w.shape, _jnp.float32, 0.5, 1.5)
    if N_MICROBATCH > 1:
        for name, axis in PER_EXAMPLE_BATCH_AXIS.items():
            out[name] = _to_microbatches(out[name], axis)
    return {'x': out['x'], 'meta_tokens': out['meta_tokens'], 'rel_bias': out['rel_bias'], 'ln_pre_mix': out['ln_pre_mix'], 'ln_post_mix': out['ln_post_mix'], 'ln_pre_ffn': out['ln_pre_ffn'], 'ln_post_ffn': out['ln_post_ffn'], 'w_in': out['w_in'], 'b_forget': out['b_forget'], 'sinks': out['sinks'], 'w_out': out['w_out'], 'w_gate_up': out['w_gate_up'], 'w_down': out['w_down'], 'loss_target': out['loss_target'], 'm_meta_tokens': out['m_meta_tokens'], 'm_rel_bias': out['m_rel_bias'], 'm_ln_pre_mix': out['m_ln_pre_mix'], 'm_ln_post_mix': out['m_ln_post_mix'], 'm_ln_pre_ffn': out['m_ln_pre_ffn'], 'm_ln_post_ffn': out['m_ln_post_ffn'], 'm_w_in': out['m_w_in'], 'm_b_forget': out['m_b_forget'], 'm_sinks': out['m_sinks'], 'm_w_out': out['m_w_out'], 'm_w_gate_up': out['m_w_gate_up'], 'm_w_down': out['m_w_down'], 'v_meta_tokens': out['v_meta_tokens'], 'v_rel_bias': out['v_rel_bias'], 'v_ln_pre_mix': out['v_ln_pre_mix'], 'v_ln_post_mix': out['v_ln_post_mix'], 'v_ln_pre_ffn': out['v_ln_pre_ffn'], 'v_ln_post_ffn': out['v_ln_post_ffn'], 'v_w_in': out['v_w_in'], 'v_b_forget': out['v_b_forget'], 'v_sinks': out['v_sinks'], 'v_w_out': out['v_w_out'], 'v_w_gate_up': out['v_w_gate_up'], 'v_w_down': out['v_w_down']}


def _loss(weights, diff, rest, loss_target):
    with _jax.named_scope("forward"):
        args = {**rest, TWIN_DIFF_INPUT: diff, **{k: w.astype(_WEIGHT_DTYPES[k]) for k, w in weights.items()}}
        y = _forward(args)
    with _jax.named_scope("loss_head"):
        err = _jnp.square(y.astype(_jnp.float32) - loss_target)
        return 0.5 * _jnp.sum(_jnp.mean(err, axis=-1)) if err.ndim else 0.5 * err


def _adamw(w, g, m, v):
    m = ADAM_B1 * m + (1.0 - ADAM_B1) * g
    v = ADAM_B2 * v + (1.0 - ADAM_B2) * _jnp.square(g)
    m_hat = m / (1.0 - ADAM_B1 ** ADAM_STEP)
    v_hat = v / (1.0 - ADAM_B2 ** ADAM_STEP)
    delta = -ADAM_LR * (m_hat / (_jnp.sqrt(v_hat) + ADAM_EPS) + ADAM_WD * w)
    return delta, m, v


def reference(x, meta_tokens, rel_bias, ln_pre_mix, ln_post_mix, ln_pre_ffn, ln_post_ffn, w_in, b_forget, sinks, w_out, w_gate_up, w_down, loss_target, m_meta_tokens, m_rel_bias, m_ln_pre_mix, m_ln_post_mix, m_ln_pre_ffn, m_ln_post_ffn, m_w_in, m_b_forget, m_sinks, m_w_out, m_w_gate_up, m_w_down, v_meta_tokens, v_rel_bias, v_ln_pre_mix, v_ln_post_mix, v_ln_pre_ffn, v_ln_post_ffn, v_w_in, v_b_forget, v_sinks, v_w_out, v_w_gate_up, v_w_down):
    given = dict(x=x, meta_tokens=meta_tokens, rel_bias=rel_bias, ln_pre_mix=ln_pre_mix, ln_post_mix=ln_post_mix, ln_pre_ffn=ln_pre_ffn, ln_post_ffn=ln_post_ffn, w_in=w_in, b_forget=b_forget, sinks=sinks, w_out=w_out, w_gate_up=w_gate_up, w_down=w_down, loss_target=loss_target, m_meta_tokens=m_meta_tokens, m_rel_bias=m_rel_bias, m_ln_pre_mix=m_ln_pre_mix, m_ln_post_mix=m_ln_post_mix, m_ln_pre_ffn=m_ln_pre_ffn, m_ln_post_ffn=m_ln_post_ffn, m_w_in=m_w_in, m_b_forget=m_b_forget, m_sinks=m_sinks, m_w_out=m_w_out, m_w_gate_up=m_w_gate_up, m_w_down=m_w_down, v_meta_tokens=v_meta_tokens, v_rel_bias=v_rel_bias, v_ln_pre_mix=v_ln_pre_mix, v_ln_post_mix=v_ln_post_mix, v_ln_pre_ffn=v_ln_pre_ffn, v_ln_post_ffn=v_ln_post_ffn, v_w_in=v_w_in, v_b_forget=v_b_forget, v_sinks=v_sinks, v_w_out=v_w_out, v_w_gate_up=v_w_gate_up, v_w_down=v_w_down)
    weights = {n: given[n] for n in TWIN_WEIGHTS}
    shared = {n: given[n] for n in SHARED_INPUTS}
    per_example = {n: given[n] for n in ['x']}
    grad_fn = _jax.value_and_grad(_loss, argnums=(0, 1))

    def one_microbatch(ex, loss_target):
        ex = dict(ex)
        diff = ex.pop(TWIN_DIFF_INPUT)
        return grad_fn(weights, diff, {**shared, **ex}, loss_target)

    if N_MICROBATCH == 1:
        loss, (grad_w, grad_x) = one_microbatch(per_example, given["loss_target"])
    else:
        def body(carry, xs):
            loss_sum, grad_sum = carry
            l_k, (gw_k, gx_k) = one_microbatch(xs[0], xs[1])
            with _jax.named_scope("update"):
                return (loss_sum + l_k, _jax.tree.map(_jnp.add, grad_sum, gw_k)), gx_k

        init = (_jnp.zeros((), _jnp.float32), _jax.tree.map(_jnp.zeros_like, weights))
        (loss, grad_w), grad_x = _jax.lax.scan(body, init, (per_example, given["loss_target"]))
    with _jax.named_scope("update"):
        delta_w, new_m, new_v = {}, {}, {}
        for n in TWIN_WEIGHTS:
            delta_w[n], new_m[n], new_v[n] = _adamw(weights[n], grad_w[n], given["m_" + n], given["v_" + n])
    return (loss, grad_x, *[grad_w[n] for n in TWIN_WEIGHTS], *[delta_w[n] for n in TWIN_WEIGHTS],
            *[new_m[n] for n in TWIN_WEIGHTS], *[new_v[n] for n in TWIN_WEIGHTS])
```

```python
import math

import numpy as np
import jax
import jax.numpy as jnp
from jax import lax
from jax.experimental import pallas as pl
from jax.experimental.pallas import tpu as pltpu

F32 = jnp.float32
BF16 = jnp.bfloat16
MESH = pl.DeviceIdType.MESH
SDS = jax.ShapeDtypeStruct

D_MODEL = 1024
SEQ = 4096
N_META = 16
N_HEADS = 8
HALF = 64
D_FF = 2816
N_BUCKETS = 32
EPS = 1e-6
NEG = -1e30
SCALE = 0.125
PAD_ROWS = 112
ROW0 = PAD_ROWS + N_META
LP = ROW0 + SEQ
BLK = 128
NBLK = LP // BLK
TM = 384
NT = LP // TM
D_PROJ = 2312
D_PROJ_P = 2432
D_QKV = 2304
FF_T = 1408
VMEM_LIMIT = 56 * 1024 * 1024

ADAM_LR = 0.001
ADAM_B1 = 0.9
ADAM_B2 = 0.999
ADAM_EPS = 1e-08
ADAM_WD = 0.01
ADAM_STEP = 10

QA, QB, KB, VB = 0, 1, 2, 3
KA, VA = 16, 17

NT_DIMS = (((1,), (1,)), ((), ()))
TN_DIMS = (((0,), (0,)), ((), ()))


def _cparams(sem):
    return pltpu.CompilerParams(dimension_semantics=sem, vmem_limit_bytes=VMEM_LIMIT)


def _t5_bucket_np(d):
    n = np.maximum(d, 0).astype(np.int32)
    nf = np.maximum(n, 1).astype(np.float32)
    large = 16 + (np.log(nf / np.float32(16)) / np.float32(math.log(8.0)) * np.float32(16)).astype(np.int32)
    large = np.minimum(large, N_BUCKETS - 1)
    return np.where(n < 16, n, large).astype(np.int32)


def _bucket_tables():
    qi = np.arange(BLK)[:, None]
    ki = np.arange(BLK)[None, :]
    return np.stack([_t5_bucket_np(qi - ki), _t5_bucket_np(qi - ki + BLK)])


def _rms(x):
    return lax.rsqrt(jnp.mean(x * x, axis=-1, keepdims=True) + EPS)


def _rms_bwd(n, r, gdy):
    return r * (gdy - n * jnp.mean(n * gdy, axis=-1, keepdims=True))


def _pre_mix(h0, gain, w_in_b):
    def body(h_ref, g_ref, w_ref, hn_ref, proj_ref, f_ref):
        x = h_ref[...]
        hn = (x * _rms(x) * g_ref[...]).astype(BF16)
        hn_ref[...] = hn
        p = jnp.dot(hn, w_ref[...], preferred_element_type=F32)
        proj_ref[...] = p[:, :D_QKV].astype(BF16)
        f_ref[...] = p[:, D_QKV:]

    return pl.pallas_call(
        body, grid=(NT,),
        in_specs=[pl.BlockSpec((TM, D_MODEL), lambda i: (i, 0)),
                  pl.BlockSpec((1, D_MODEL), lambda i: (0, 0)),
                  pl.BlockSpec((D_MODEL, D_PROJ_P), lambda i: (0, 0))],
        out_specs=[pl.BlockSpec((TM, D_MODEL), lambda i: (i, 0)),
                   pl.BlockSpec((TM, D_QKV), lambda i: (i, 0)),
                   pl.BlockSpec((TM, BLK), lambda i: (i, 0))],
        out_shape=[SDS((LP, D_MODEL), BF16), SDS((LP, D_QKV), BF16), SDS((LP, BLK), F32)],
        compiler_params=_cparams(("parallel",)), name="pre_mix")(h0, gain, w_in_b)


def _attn_out(o_a, o_b, w_out_b, h0, g_post, g_pre_ffn):
    def body(oa_ref, ob_ref, w_ref, h0_ref, gp_ref, gf_ref, a_ref, h1_ref, hn2_ref):
        a = (jnp.dot(oa_ref[...], w_ref[0:512, :], preferred_element_type=F32)
             + jnp.dot(ob_ref[...], w_ref[512:1024, :], preferred_element_type=F32))
        a_ref[...] = a
        h1 = h0_ref[...] + a * _rms(a) * gp_ref[...]
        h1_ref[...] = h1
        hn2_ref[...] = (h1 * _rms(h1) * gf_ref[...]).astype(BF16)

    row = lambda w: pl.BlockSpec((TM, w), lambda i: (i, 0))
    vec = pl.BlockSpec((1, D_MODEL), lambda i: (0, 0))
    return pl.pallas_call(
        body, grid=(NT,),
        in_specs=[row(512), row(512), pl.BlockSpec((D_MODEL, D_MODEL), lambda i: (0, 0)), row(D_MODEL), vec, vec],
        out_specs=[row(D_MODEL), row(D_MODEL), row(D_MODEL)],
        out_shape=[SDS((LP, D_MODEL), F32), SDS((LP, D_MODEL), F32), SDS((LP, D_MODEL), BF16)],
        compiler_params=_cparams(("parallel",)), name="attn_out")(o_a, o_b, w_out_b, h0, g_post, g_pre_ffn)


def _ffn_up(hn2, w_gu_b):
    def body(x_ref, wg_ref, wu_ref, g_ref, u_ref, act_ref):
        x = x_ref[...]
        g = jnp.dot(x, wg_ref[0], preferred_element_type=F32)
        u = jnp.dot(x, wu_ref[0], preferred_element_type=F32)
        g_ref[...] = g.astype(BF16)
        u_ref[...] = u.astype(BF16)
        act_ref[...] = (g * (1.0 / (1.0 + jnp.exp(-g))) * u).astype(BF16)

    out = pl.BlockSpec((TM, FF_T), lambda j, i: (i, j))
    return pl.pallas_call(
        body, grid=(2, NT),
        in_specs=[pl.BlockSpec((TM, D_MODEL), lambda j, i: (i, 0)),
                  pl.BlockSpec((1, D_MODEL, FF_T), lambda j, i: (j, 0, 0)),
                  pl.BlockSpec((1, D_MODEL, FF_T), lambda j, i: (j + 2, 0, 0))],
        out_specs=[out, out, out],
        out_shape=[SDS((LP, D_FF), BF16)] * 3,
        compiler_params=_cparams(("parallel", "parallel")), name="ffn_up")(hn2, w_gu_b, w_gu_b)


def _ffn_down_loss(act, w_dn_b, h1, tgt_p, g_post_ffn):
    def body(act_ref, w_ref, h1_ref, t_ref, g_ref, dff_ref, dy_ref, loss_ref, dg_ref):
        i = pl.program_id(0)

        @pl.when(i == 0)
        def _():
            loss_ref[...] = jnp.zeros_like(loss_ref)
            dg_ref[...] = jnp.zeros_like(dg_ref)

        ff = jnp.dot(act_ref[...], w_ref[...], preferred_element_type=F32)
        r = _rms(ff)
        n = ff * r
        g = g_ref[...]
        y = h1_ref[...] + n * g
        rows = i * TM + lax.broadcasted_iota(jnp.int32, (TM, D_MODEL), 0)
        diff = jnp.where(rows >= ROW0, y - t_ref[...], 0.0)
        loss_ref[...] += 0.5 * jnp.sum(diff * diff) / D_MODEL
        dy = diff / D_MODEL
        dy_ref[...] = dy
        dg_ref[...] += jnp.sum(dy * n, axis=0, keepdims=True)
        dff_ref[...] = _rms_bwd(n, r, g * dy).astype(BF16)

    row = pl.BlockSpec((TM, D_MODEL), lambda i: (i, 0))
    return pl.pallas_call(
        body, grid=(NT,),
        in_specs=[pl.BlockSpec((TM, D_FF), lambda i: (i, 0)), pl.BlockSpec((D_FF, D_MODEL), lambda i: (0, 0)),
                  row, row, pl.BlockSpec((1, D_MODEL), lambda i: (0, 0))],
        out_specs=[row, row, pl.BlockSpec((8, BLK), lambda i: (0, 0)), pl.BlockSpec((1, D_MODEL), lambda i: (0, 0))],
        out_shape=[SDS((LP, D_MODEL), BF16), SDS((LP, D_MODEL), F32), SDS((8, BLK), F32), SDS((1, D_MODEL), F32)],
        compiler_params=_cparams(("arbitrary",)), name="ffn_down_loss")(act, w_dn_b, h1, tgt_p, g_post_ffn)


def _ffn_down_bwd(dff, w_dn_b, g, u):
    def body(d_ref, w_ref, g_ref, u_ref, dg_ref, du_ref):
        dact = lax.dot_general(d_ref[...], w_ref[...], NT_DIMS, preferred_element_type=F32)
        gg = g_ref[...].astype(F32)
        sig = 1.0 / (1.0 + jnp.exp(-gg))
        dg_ref[...] = (dact * u_ref[...].astype(F32) * sig * (1.0 + gg * (1.0 - sig))).astype(BF16)
        du_ref[...] = (dact * gg * sig).astype(BF16)

    blk = pl.BlockSpec((TM, FF_T), lambda j, i: (i, j))
    return pl.pallas_call(
        body, grid=(2, NT),
        in_specs=[pl.BlockSpec((TM, D_MODEL), lambda j, i: (i, 0)),
                  pl.BlockSpec((FF_T, D_MODEL), lambda j, i: (j, 0)), blk, blk],
        out_specs=[blk, blk],
        out_shape=[SDS((LP, D_FF), BF16)] * 2,
        compiler_params=_cparams(("parallel", "parallel")), name="ffn_down_bwd")(dff, w_dn_b, g, u)


def _ffn_up_bwd(dg, du, w_gu_b, h1, a, dy, g_pre_ffn, g_post_mix):
    def body(dg_ref, du_ref, w_ref, h1_ref, a_ref, dy_ref, gf_ref, gp_ref,
             dh1_ref, da_ref, dgf_ref, dgp_ref, acc):
        i = pl.program_id(0)
        s = pl.program_id(1)

        @pl.when((i == 0) & (s == 0))
        def _():
            dgf_ref[...] = jnp.zeros_like(dgf_ref)
            dgp_ref[...] = jnp.zeros_like(dgp_ref)

        @pl.when(s == 0)
        def _():
            acc[...] = jnp.zeros_like(acc)

        @pl.when(s < 2)
        def _():
            acc[...] += lax.dot_general(dg_ref[...], w_ref[0], NT_DIMS, preferred_element_type=F32)

        @pl.when(s >= 2)
        def _():
            acc[...] += lax.dot_general(du_ref[...], w_ref[0], NT_DIMS, preferred_element_type=F32)

        @pl.when(s == 3)
        def _():
            dhn2 = acc[...]
            h1 = h1_ref[...]
            r2 = _rms(h1)
            n2 = h1 * r2
            dgf_ref[...] += jnp.sum(dhn2 * n2, axis=0, keepdims=True)
            dh1 = dy_ref[...] + _rms_bwd(n2, r2, gf_ref[...] * dhn2)
            dh1_ref[...] = dh1
            av = a_ref[...]
            ra = _rms(av)
            na = av * ra
            dgp_ref[...] += jnp.sum(dh1 * na, axis=0, keepdims=True)
            da_ref[...] = _rms_bwd(na, ra, gp_ref[...] * dh1).astype(BF16)

    row = pl.BlockSpec((TM, D_MODEL), lambda i, s: (i, 0))
    vec = pl.BlockSpec((1, D_MODEL), lambda i, s: (0, 0))
    return pl.pallas_call(
        body, grid=(NT, 4),
        in_specs=[pl.BlockSpec((TM, FF_T), lambda i, s: (i, jnp.minimum(s, 1))),
                  pl.BlockSpec((TM, FF_T), lambda i, s: (i, jnp.maximum(s - 2, 0))),
                  pl.BlockSpec((1, D_MODEL, FF_T), lambda i, s: (s, 0, 0)),
                  row, row, row, vec, vec],
        out_specs=[row, row, vec, vec],
        out_shape=[SDS((LP, D_MODEL), F32), SDS((LP, D_MODEL), BF16), SDS((1, D_MODEL), F32), SDS((1, D_MODEL), F32)],
        scratch_shapes=[pltpu.VMEM((TM, D_MODEL), F32)],
        compiler_params=_cparams(("arbitrary", "arbitrary")), name="ffn_up_bwd",
    )(dg, du, w_gu_b, h1, a, dy, g_pre_ffn, g_post_mix)


def _attn_out_bwd(da, w_out_b):
    def body(d_ref, w_ref, o_ref):
        o_ref[...] = lax.dot_general(d_ref[...], w_ref[...], NT_DIMS, preferred_element_type=F32).astype(BF16)

    row = pl.BlockSpec((TM, D_MODEL), lambda i: (i, 0))
    return pl.pallas_call(
        body, grid=(NT,),
        in_specs=[row, pl.BlockSpec((D_MODEL, D_MODEL), lambda i: (0, 0))],
        out_specs=row, out_shape=SDS((LP, D_MODEL), BF16),
        compiler_params=_cparams(("parallel",)), name="attn_out_bwd")(da, w_out_b)


def _pre_mix_bwd(dq_a, dq_b, dk_b, dv_b, dk_a, dv_a, df, w_in_b, h0, dh1, g_pre_mix):
    def body(qa_ref, qb_ref, kb_ref, vb_ref, ka_ref, va_ref, f_ref, w_ref, h0_ref, dh1_ref, g_ref,
             dproj_ref, dh0_ref, dg_ref):
        i = pl.program_id(0)

        @pl.when(i == 0)
        def _():
            dg_ref[...] = jnp.zeros_like(dg_ref)

        dproj = jnp.concatenate(
            [qa_ref[...], (qb_ref[...] * SCALE).astype(BF16), kb_ref[...], vb_ref[...],
             ka_ref[...].astype(BF16), va_ref[...].astype(BF16), f_ref[...].astype(BF16)], axis=1)
        dproj_ref[...] = dproj
        dhn = lax.dot_general(dproj, w_ref[...], NT_DIMS, preferred_element_type=F32)
        x = h0_ref[...]
        r = _rms(x)
        n = x * r
        dg_ref[...] += jnp.sum(dhn * n, axis=0, keepdims=True)
        dh0_ref[...] = dh1_ref[...] + _rms_bwd(n, r, g_ref[...] * dhn)

    row = lambda w: pl.BlockSpec((TM, w), lambda i: (i, 0))
    vec = pl.BlockSpec((1, D_MODEL), lambda i: (0, 0))
    return pl.pallas_call(
        body, grid=(NT,),
        in_specs=[row(512), row(512), row(512), row(512), row(BLK), row(BLK), row(BLK),
                  pl.BlockSpec((D_MODEL, D_PROJ_P), lambda i: (0, 0)), row(D_MODEL), row(D_MODEL), vec],
        out_specs=[row(D_PROJ_P), row(D_MODEL), vec],
        out_shape=[SDS((LP, D_PROJ_P), BF16), SDS((LP, D_MODEL), F32), SDS((1, D_MODEL), F32)],
        compiler_params=_cparams(("arbitrary",)), name="pre_mix_bwd",
    )(dq_a, dq_b, dk_b, dv_b, dk_a, dv_a, df, w_in_b, h0, dh1, g_pre_mix)


def _mm_tn(a, b, tm, name):
    m_total = a.shape[1]
    n = b.shape[1]

    def body(a_ref, b_ref, o_ref):
        @pl.when(pl.program_id(1) == 0)
        def _():
            o_ref[...] = jnp.zeros_like(o_ref)
        o_ref[...] += lax.dot_general(a_ref[...], b_ref[...], TN_DIMS, preferred_element_type=F32)

    return pl.pallas_call(
        body, grid=(m_total // tm, NT),
        in_specs=[pl.BlockSpec((TM, tm), lambda mi, k: (k, mi)),
                  pl.BlockSpec((TM, n), lambda mi, k: (k, 0))],
        out_specs=pl.BlockSpec((tm, n), lambda mi, k: (mi, 0)),
        out_shape=SDS((m_total, n), F32),
        compiler_params=_cparams(("parallel", "arbitrary")), name=name)(a, b)


def _dw_gate_up(hn2, dg, du):
    def body(a_ref, dg_ref, du_ref, o_ref):
        s = pl.program_id(0)

        @pl.when(pl.program_id(1) == 0)
        def _():
            o_ref[...] = jnp.zeros_like(o_ref)

        @pl.when(s < 2)
        def _():
            o_ref[0] += lax.dot_general(a_ref[...], dg_ref[...], TN_DIMS, preferred_element_type=F32)

        @pl.when(s >= 2)
        def _():
            o_ref[0] += lax.dot_general(a_ref[...], du_ref[...], TN_DIMS, preferred_element_type=F32)

    return pl.pallas_call(
        body, grid=(4, NT),
        in_specs=[pl.BlockSpec((TM, D_MODEL), lambda s, k: (k, 0)),
                  pl.BlockSpec((TM, FF_T), lambda s, k: (k, jnp.minimum(s, 1))),
                  pl.BlockSpec((TM, FF_T), lambda s, k: (k, jnp.maximum(s - 2, 0)))],
        out_specs=pl.BlockSpec((1, D_MODEL, FF_T), lambda s, k: (s, 0, 0)),
        out_shape=SDS((4, D_MODEL, FF_T), F32),
        compiler_params=_cparams(("parallel", "arbitrary")), name="dw_gate_up")(hn2, dg, du)


def _split3(x):
    hi = x.astype(BF16)
    r1 = x - hi.astype(F32)
    mid = r1.astype(BF16)
    lo = (r1 - mid.astype(F32)).astype(BF16)
    return hi, mid, lo


def _tri_matmul(tri, x):
    hi, mid, lo = _split3(x)
    dot = lambda t: jnp.dot(tri, t, preferred_element_type=F32)
    return dot(hi) + dot(mid) + dot(lo)


def _forget_cumsum(f, b_forget_p):
    def body(f_ref, b_ref, cum_ref, carry):
        i = pl.program_id(0)

        @pl.when(i == 0)
        def _():
            carry[...] = jnp.zeros_like(carry)

        z = f_ref[...] + b_ref[...]
        ls = jnp.minimum(z, 0.0) - jnp.log(1.0 + jnp.exp(-jnp.abs(z)))
        rows = i * BLK + lax.broadcasted_iota(jnp.int32, (BLK, BLK), 0)
        ls = jnp.where(rows >= PAD_ROWS, ls, 0.0)
        r = lax.broadcasted_iota(jnp.int32, (BLK, BLK), 0)
        c = lax.broadcasted_iota(jnp.int32, (BLK, BLK), 1)
        tri = (c <= r).astype(BF16)
        cum = _tri_matmul(tri, ls) + carry[...]
        cum_ref[...] = cum
        carry[...] = cum[BLK - 1:BLK, :]

    return pl.pallas_call(
        body, grid=(NBLK,),
        in_specs=[pl.BlockSpec((BLK, BLK), lambda i: (i, 0)), pl.BlockSpec((1, BLK), lambda i: (0, 0))],
        out_specs=pl.BlockSpec((BLK, BLK), lambda i: (i, 0)),
        out_shape=SDS((LP, BLK), F32),
        scratch_shapes=[pltpu.VMEM((1, BLK), F32)],
        compiler_params=_cparams(("arbitrary",)), name="forget_cumsum")(f, b_forget_p)


def _forget_cumsum_bwd(dcum, f, b_forget_p):
    def body(d_ref, f_ref, b_ref, df_ref, db_ref, carry):
        i = pl.program_id(0)

        @pl.when(i == 0)
        def _():
            carry[...] = jnp.zeros_like(carry)
            db_ref[...] = jnp.zeros_like(db_ref)

        blk = NBLK - 1 - i
        r = lax.broadcasted_iota(jnp.int32, (BLK, BLK), 0)
        c = lax.broadcasted_iota(jnp.int32, (BLK, BLK), 1)
        tri = (c >= r).astype(BF16)
        d = d_ref[...]
        dls = _tri_matmul(tri, d) + carry[...]
        carry[...] = dls[0:1, :]
        z = f_ref[...] + b_ref[...]
        rows = blk * BLK + r
        df = jnp.where(rows >= PAD_ROWS, dls / (1.0 + jnp.exp(z)), 0.0)
        df_ref[...] = df
        db_ref[...] += jnp.sum(df, axis=0, keepdims=True)

    rev = pl.BlockSpec((BLK, BLK), lambda i: (NBLK - 1 - i, 0))
    vec = pl.BlockSpec((1, BLK), lambda i: (0, 0))
    return pl.pallas_call(
        body, grid=(NBLK,),
        in_specs=[rev, rev, vec],
        out_specs=[rev, vec],
        out_shape=[SDS((LP, BLK), F32), SDS((1, BLK), F32)],
        scratch_shapes=[pltpu.VMEM((1, BLK), F32)],
        compiler_params=_cparams(("arbitrary",)), name="forget_cumsum_bwd")(dcum, f, b_forget_p)


def _lane_half(rows):
    return lax.broadcasted_iota(jnp.int32, (rows, BLK), 1) // HALF


def _fox_valid(qi, kj):
    qrow = qi * TM + lax.broadcasted_iota(jnp.int32, (TM, TM), 0)
    krow = kj * TM + lax.broadcasted_iota(jnp.int32, (TM, TM), 1)
    return (krow <= qrow) & ((krow >= PAD_ROWS) | (qrow < PAD_ROWS))


def _fox_fwd(proj, ck_t):
    def body(q_ref, k_ref, v_ref, ck_ref, o_ref, lse_ref, m_s, l_s, acc_s):
        qi = pl.program_id(0)
        kj = pl.program_id(1)

        @pl.when(kj == 0)
        def _():
            m_s[...] = jnp.full_like(m_s, NEG)
            l_s[...] = jnp.zeros_like(l_s)
            acc_s[...] = jnp.zeros_like(acc_s)

        def tile(masked):
            valid = _fox_valid(qi, kj) if masked else None
            half_k = _lane_half(TM)
            for pp in range(4):
                cols = slice(pp * BLK, (pp + 1) * BLK)
                qs = (q_ref[:, cols].astype(F32) * SCALE).astype(BF16)
                kp = k_ref[:, cols]
                vp = v_ref[:, cols]
                pv = jnp.zeros((TM, BLK), F32)
                alphas = []
                for e in range(2):
                    h = 2 * pp + e
                    ke = jnp.where(half_k == e, kp, jnp.zeros_like(kp))
                    ve = jnp.where(half_k == e, vp, jnp.zeros_like(vp))
                    t = lax.dot_general(qs, ke, NT_DIMS, preferred_element_type=F32) - ck_ref[h]
                    if masked:
                        t = jnp.where(valid, t, NEG)
                    m_prev = m_s[h]
                    m_new = jnp.maximum(m_prev, jnp.max(t, axis=1, keepdims=True))
                    p = jnp.exp(t - m_new)
                    alpha = jnp.exp(m_prev - m_new)
                    l_s[h] = alpha * l_s[h] + jnp.sum(p, axis=1, keepdims=True)
                    m_s[h] = m_new
                    pv = pv + jnp.dot(p.astype(BF16), ve, preferred_element_type=F32)
                    alphas.append(alpha)
                a_lane = jnp.where(half_k == 0, alphas[0], alphas[1])
                acc_s[pp] = acc_s[pp] * a_lane + pv

        @pl.when((kj < qi) & (kj > 0))
        def _():
            tile(False)

        @pl.when((kj <= qi) & ((kj == qi) | (kj == 0)))
        def _():
            tile(True)

        @pl.when(kj == qi)
        def _():
            half_q = _lane_half(TM)
            for pp in range(4):
                inv = jnp.where(half_q == 0, 1.0 / l_s[2 * pp], 1.0 / l_s[2 * pp + 1])
                o_ref[:, pp * BLK:(pp + 1) * BLK] = (acc_s[pp] * inv).astype(BF16)
            for h in range(N_HEADS):
                lse_ref[h] = m_s[h] + jnp.log(l_s[h])

    kv = lambda blk: pl.BlockSpec((TM, 512), lambda qi, kj: (jnp.minimum(kj, qi), blk))
    return pl.pallas_call(
        body, grid=(NT, NT),
        in_specs=[pl.BlockSpec((TM, 512), lambda qi, kj: (qi, QB)), kv(KB), kv(VB),
                  pl.BlockSpec((N_HEADS, 1, TM), lambda qi, kj: (0, 0, jnp.minimum(kj, qi)))],
        out_specs=[pl.BlockSpec((TM, 512), lambda qi, kj: (qi, 0)),
                   pl.BlockSpec((N_HEADS, TM, 1), lambda qi, kj: (0, qi, 0))],
        out_shape=[SDS((LP, 512), BF16), SDS((N_HEADS, LP, 1), F32)],
        scratch_shapes=[pltpu.VMEM((N_HEADS, TM, 1), F32), pltpu.VMEM((N_HEADS, TM, 1), F32),
                        pltpu.VMEM((4, TM, BLK), F32)],
        compiler_params=_cparams(("parallel", "arbitrary")), name="fox_fwd")(proj, proj, proj, ck_t)


def _fox_bwd(proj, o_b, dmix, lse, ck_t):
    def body(q_ref, k_ref, v_ref, o_ref, do_ref, lse_ref, ck_ref, dq_ref, dk_ref, dv_ref, dck_ref, dcq_ref,
             dk_s, dv_s, dck_s):
        kj = pl.program_id(0)
        qi = pl.program_id(1)

        @pl.when((kj == 0) & (qi == 0))
        def _():
            dq_ref[...] = jnp.zeros_like(dq_ref)
            dcq_ref[...] = jnp.zeros_like(dcq_ref)

        @pl.when(qi == 0)
        def _():
            dk_s[...] = jnp.zeros_like(dk_s)
            dv_s[...] = jnp.zeros_like(dv_s)
            dck_s[...] = jnp.zeros_like(dck_s)

        def tile(masked):
            valid = _fox_valid(qi, kj) if masked else None
            half = _lane_half(TM)
            q0 = pl.multiple_of(qi * TM, TM)
            lane = lax.broadcasted_iota(jnp.int32, (TM, BLK), 1)
            row_sums = jnp.zeros((TM, BLK), F32)
            for pp in range(4):
                cols = slice(pp * BLK, (pp + 1) * BLK)
                qs = (q_ref[:, cols].astype(F32) * SCALE).astype(BF16)
                kp = k_ref[:, cols]
                vp = v_ref[:, cols]
                dop = do_ref[:, cols]
                prod = dop.astype(F32) * o_ref[:, cols].astype(F32)
                d0 = jnp.sum(jnp.where(half == 0, prod, 0.0), axis=1, keepdims=True)
                d1 = jnp.sum(prod, axis=1, keepdims=True) - d0
                dq = jnp.zeros((TM, BLK), F32)
                dks, dvs = [], []
                for e in range(2):
                    h = 2 * pp + e
                    ke = jnp.where(half == e, kp, jnp.zeros_like(kp))
                    ve = jnp.where(half == e, vp, jnp.zeros_like(vp))
                    t = lax.dot_general(qs, ke, NT_DIMS, preferred_element_type=F32) - ck_ref[h] - lse_ref[h]
                    if masked:
                        t = jnp.where(valid, t, NEG)
                    p = jnp.exp(t)
                    dp = lax.dot_general(dop, ve, NT_DIMS, preferred_element_type=F32)
                    ds = p * (dp - (d0 if e == 0 else d1))
                    dck_s[h] += jnp.sum(ds, axis=0, keepdims=True)
                    row_sums = jnp.where(lane == h, jnp.sum(ds, axis=1, keepdims=True), row_sums)
                    ds_b = ds.astype(BF16)
                    dq = dq + jnp.dot(ds_b, ke, preferred_element_type=F32)
                    dks.append(lax.dot_general(ds_b, qs, TN_DIMS, preferred_element_type=F32))
                    dvs.append(lax.dot_general(p.astype(BF16), dop, TN_DIMS, preferred_element_type=F32))
                dq_ref[pl.ds(q0, TM), cols] += dq
                dk_s[pp] += jnp.where(half == 0, dks[0], dks[1])
                dv_s[pp] += jnp.where(half == 0, dvs[0], dvs[1])
            dcq_ref[pl.ds(q0, TM), :] += row_sums

        @pl.when((qi > kj) & (kj > 0))
        def _():
            tile(False)

        @pl.when((qi >= kj) & ((qi == kj) | (kj == 0)))
        def _():
            tile(True)

        @pl.when(qi == NT - 1)
        def _():
            for pp in range(4):
                cols = slice(pp * BLK, (pp + 1) * BLK)
                dk_ref[:, cols] = dk_s[pp].astype(BF16)
                dv_ref[:, cols] = dv_s[pp].astype(BF16)
            dck_ref[...] = dck_s[...]

    qrow = lambda blk, arr_cols=512: pl.BlockSpec((TM, 512), lambda kj, qi: (jnp.maximum(qi, kj), blk))
    krow = lambda blk: pl.BlockSpec((TM, 512), lambda kj, qi: (kj, blk))
    return pl.pallas_call(
        body, grid=(NT, NT),
        in_specs=[qrow(QB), krow(KB), krow(VB), qrow(0), qrow(1),
                  pl.BlockSpec((N_HEADS, TM, 1), lambda kj, qi: (0, jnp.maximum(qi, kj), 0)),
                  pl.BlockSpec((N_HEADS, 1, TM), lambda kj, qi: (0, 0, kj))],
        out_specs=[pl.BlockSpec((LP, 512), lambda kj, qi: (0, 0)),
                   pl.BlockSpec((TM, 512), lambda kj, qi: (kj, 0)),
                   pl.BlockSpec((TM, 512), lambda kj, qi: (kj, 0)),
                   pl.BlockSpec((N_HEADS, 1, TM), lambda kj, qi: (0, 0, kj)),
                   pl.BlockSpec((LP, BLK), lambda kj, qi: (0, 0))],
        out_shape=[SDS((LP, 512), F32), SDS((LP, 512), BF16), SDS((LP, 512), BF16), SDS((N_HEADS, 1, LP), F32),
                   SDS((LP, BLK), F32)],
        scratch_shapes=[pltpu.VMEM((4, TM, BLK), F32), pltpu.VMEM((4, TM, BLK), F32),
                        pltpu.VMEM((N_HEADS, 1, TM), F32)],
        compiler_params=_cparams(("arbitrary", "arbitrary")), name="fox_bwd",
    )(proj, proj, proj, o_b, dmix, lse, ck_t)


def _build_bias(tab_ref, bkt_ref, bias_cur, bias_prev):
    bc = bkt_ref[0]
    bp = bkt_ref[1]
    for h in range(N_HEADS):
        def step(b, carry, h=h):
            t = tab_ref[b, h]
            return jnp.where(bc == b, t, carry[0]), jnp.where(bp == b, t, carry[1])
        zero = jnp.zeros((BLK, BLK), F32)
        cur, prev = lax.fori_loop(0, N_BUCKETS, step, (zero, zero))
        bias_cur[h] = cur
        bias_prev[h] = prev


def _kv_variants(ref):
    x = ref[...].astype(F32)
    swapped = pltpu.roll(x, HALF, 1)
    half = _lane_half(BLK)
    out = {}
    for e in range(2):
        for g in range(2):
            src = x if e == g else swapped
            out[(e, g)] = jnp.where(half == e, src, 0.0).astype(BF16)
    return out


def _swa_masks(i):
    qrow = lax.broadcasted_iota(jnp.int32, (BLK, BLK), 0)
    kcol = lax.broadcasted_iota(jnp.int32, (BLK, BLK), 1)
    big = 4 * BLK
    valid_cur = (kcol <= qrow) & (kcol >= jnp.where(i > 0, 0, PAD_ROWS))
    valid_prev = kcol > qrow + jnp.where(i >= 2, 0, big)
    valid_meta = kcol >= jnp.where(i >= 1, PAD_ROWS, big)
    return valid_cur, valid_prev, valid_meta


def _swa_scores(i, h, qs, kvar, bias_cur, bias_prev, tab_ref, masks):
    e, g = h % 2, h // 4
    first = jnp.full((BLK, BLK), i, jnp.int32) == 1
    biases = (bias_cur[h], bias_prev[h], jnp.where(first, bias_prev[h], tab_ref[N_BUCKETS - 1, h]))
    out = []
    for x in range(3):
        s = lax.dot_general(qs, kvar[x][(e, g)], NT_DIMS, preferred_element_type=F32) + biases[x]
        out.append(jnp.where(masks[x], s, NEG))
    return out


def _swa_fwd(proj, rel_bias, sinks, bkt):
    def body(tab_ref, sink_ref, bkt_ref, q_ref, kc_ref, kp_ref, km_ref, vc_ref, vp_ref, vm_ref,
             o_ref, lse_ref, bias_cur, bias_prev):
        i = pl.program_id(0)

        @pl.when(i == 0)
        def _():
            _build_bias(tab_ref, bkt_ref, bias_cur, bias_prev)

        masks = _swa_masks(i)
        kvar = [_kv_variants(r) for r in (kc_ref, kp_ref, km_ref)]
        vvar = [_kv_variants(r) for r in (vc_ref, vp_ref, vm_ref)]
        for pp in range(4):
            cols = slice(pp * BLK, (pp + 1) * BLK)
            qs = (q_ref[:, cols].astype(F32) * SCALE).astype(BF16)
            o_pair = jnp.zeros((BLK, BLK), F32)
            for e in range(2):
                h = 2 * pp + e
                g = h // 4
                s = _swa_scores(i, h, qs, kvar, bias_cur, bias_prev, tab_ref, masks)
                sink = sink_ref[0, h]
                m = jnp.maximum(jnp.maximum(jnp.max(s[0], axis=1, keepdims=True), jnp.max(s[1], axis=1, keepdims=True)),
                                jnp.maximum(jnp.max(s[2], axis=1, keepdims=True), sink))
                p = [jnp.exp(sx - m) for sx in s]
                denom = (jnp.sum(p[0], axis=1, keepdims=True) + jnp.sum(p[1], axis=1, keepdims=True)
                         + jnp.sum(p[2], axis=1, keepdims=True) + jnp.exp(sink - m))
                pv = (jnp.dot(p[0].astype(BF16), vvar[0][(e, g)], preferred_element_type=F32)
                      + jnp.dot(p[1].astype(BF16), vvar[1][(e, g)], preferred_element_type=F32)
                      + jnp.dot(p[2].astype(BF16), vvar[2][(e, g)], preferred_element_type=F32))
                o_pair = o_pair + pv * (1.0 / denom)
                lse_ref[h] = m + jnp.log(denom)
            o_ref[:, cols] = o_pair.astype(BF16)

    smem = pl.BlockSpec(memory_space=pltpu.SMEM)
    kv = lambda col, which: pl.BlockSpec(
        (BLK, BLK), {0: lambda i: (i, col), 1: lambda i: (jnp.maximum(i - 1, 0), col), 2: lambda i: (0, col)}[which])
    return pl.pallas_call(
        body, grid=(NBLK,),
        in_specs=[smem, smem, pl.BlockSpec((2, BLK, BLK), lambda i: (0, 0, 0)),
                  pl.BlockSpec((BLK, 512), lambda i: (i, QA)),
                  kv(KA, 0), kv(KA, 1), kv(KA, 2), kv(VA, 0), kv(VA, 1), kv(VA, 2)],
        out_specs=[pl.BlockSpec((BLK, 512), lambda i: (i, 0)),
                   pl.BlockSpec((N_HEADS, BLK, 1), lambda i: (0, i, 0))],
        out_shape=[SDS((LP, 512), BF16), SDS((N_HEADS, LP, 1), F32)],
        scratch_shapes=[pltpu.VMEM((N_HEADS, BLK, BLK), F32), pltpu.VMEM((N_HEADS, BLK, BLK), F32)],
        compiler_params=_cparams(("arbitrary",)), name="swa_fwd",
    )(rel_bias, sinks, bkt, proj, proj, proj, proj, proj, proj, proj)


def _swa_bwd(proj, o_a, dmix, lse, rel_bias, sinks, bkt):
    def body(tab_ref, sink_ref, bkt_ref, q_ref, kc_ref, kp_ref, km_ref, vc_ref, vp_ref, vm_ref,
             o_ref, do_ref, lse_ref, dq_ref, dk_ref, dv_ref, dbias_ref, dsink_ref,
             bias_cur, bias_prev, acc_cur, acc_prev, acc_far, dsk):
        i = pl.program_id(0)

        @pl.when(i == 0)
        def _():
            _build_bias(tab_ref, bkt_ref, bias_cur, bias_prev)
            dk_ref[...] = jnp.zeros_like(dk_ref)
            dv_ref[...] = jnp.zeros_like(dv_ref)
            acc_cur[...] = jnp.zeros_like(acc_cur)
            acc_prev[...] = jnp.zeros_like(acc_prev)
            acc_far[...] = jnp.zeros_like(acc_far)
            dsk[...] = jnp.zeros_like(dsk)

        masks = _swa_masks(i)
        half = _lane_half(BLK)
        first = jnp.full((BLK, BLK), i, jnp.int32) == 1
        kvar = [_kv_variants(r) for r in (kc_ref, kp_ref, km_ref)]
        vvar = [_kv_variants(r) for r in (vc_ref, vp_ref, vm_ref)]
        dk_blk = [jnp.zeros((BLK, BLK), F32) for _ in range(3)]
        dv_blk = [jnp.zeros((BLK, BLK), F32) for _ in range(3)]
        for pp in range(4):
            cols = slice(pp * BLK, (pp + 1) * BLK)
            qs = (q_ref[:, cols].astype(F32) * SCALE).astype(BF16)
            dop = do_ref[:, cols]
            prod = dop.astype(F32) * o_ref[:, cols].astype(F32)
            d0 = jnp.sum(jnp.where(half == 0, prod, 0.0), axis=1, keepdims=True)
            d1 = jnp.sum(prod, axis=1, keepdims=True) - d0
            dq = jnp.zeros((BLK, BLK), F32)
            for e in range(2):
                h = 2 * pp + e
                g = h // 4
                dd = d0 if e == 0 else d1
                lse_h = lse_ref[h]
                s = _swa_scores(i, h, qs, kvar, bias_cur, bias_prev, tab_ref, masks)
                dsk[h] += -jnp.exp(sink_ref[0, h] - lse_h) * dd
                ds_all = []
                for x in range(3):
                    p = jnp.exp(s[x] - lse_h)
                    dp = lax.dot_general(dop, vvar[x][(e, g)], NT_DIMS, preferred_element_type=F32)
                    ds = p * (dp - dd)
                    ds_all.append(ds)
                    ds_b = ds.astype(BF16)
                    dq = dq + jnp.dot(ds_b, kvar[x][(e, g)], preferred_element_type=F32)
                    dkv = lax.dot_general(ds_b, qs, TN_DIMS, preferred_element_type=F32)
                    dvv = lax.dot_general(p.astype(BF16), dop, TN_DIMS, preferred_element_type=F32)
                    if e != g:
                        dkv = pltpu.roll(dkv, HALF, 1)
                        dvv = pltpu.roll(dvv, HALF, 1)
                    dk_blk[x] = dk_blk[x] + jnp.where(half == g, dkv, 0.0)
                    dv_blk[x] = dv_blk[x] + jnp.where(half == g, dvv, 0.0)
                acc_cur[h] += ds_all[0]
                acc_prev[h] += ds_all[1] + jnp.where(first, ds_all[2], 0.0)
                acc_far[h] += jnp.where(first, 0.0, ds_all[2])
            dq_ref[:, cols] = (dq * SCALE).astype(BF16)

        cur0 = pl.multiple_of(i * BLK, BLK)
        prev0 = pl.multiple_of(jnp.maximum(i - 1, 0) * BLK, BLK)
        dk_ref[pl.ds(cur0, BLK), :] += dk_blk[0]
        dv_ref[pl.ds(cur0, BLK), :] += dv_blk[0]
        dk_ref[pl.ds(prev0, BLK), :] += dk_blk[1]
        dv_ref[pl.ds(prev0, BLK), :] += dv_blk[1]
        dk_ref[0:BLK, :] += dk_blk[2]
        dv_ref[0:BLK, :] += dv_blk[2]

        @pl.when(i == NBLK - 1)
        def _():
            bc = bkt_ref[0]
            bp = bkt_ref[1]
            lane = lax.broadcasted_iota(jnp.int32, (1, BLK), 1)

            def per_bucket(b, carry):
                row = jnp.zeros((1, BLK), F32)
                for h in range(N_HEADS):
                    val = (jnp.sum(jnp.where(bc == b, acc_cur[h], 0.0), keepdims=True)
                           + jnp.sum(jnp.where(bp == b, acc_prev[h], 0.0), keepdims=True))
                    row = jnp.where(lane == h, val, row)
                dbias_ref[pl.ds(b, 1), :] = row
                return carry

            lax.fori_loop(0, N_BUCKETS, per_bucket, 0)
            far = jnp.zeros((1, BLK), F32)
            dsr = jnp.zeros((1, BLK), F32)
            for h in range(N_HEADS):
                far = jnp.where(lane == h, jnp.sum(acc_far[h], keepdims=True), far)
                dsr = jnp.where(lane == h, jnp.sum(dsk[h], keepdims=True), dsr)
            dbias_ref[N_BUCKETS - 1:N_BUCKETS, :] += far
            dsink_ref[...] = dsr

    smem = pl.BlockSpec(memory_space=pltpu.SMEM)
    kv = lambda col, which: pl.BlockSpec(
        (BLK, BLK), {0: lambda i: (i, col), 1: lambda i: (jnp.maximum(i - 1, 0), col), 2: lambda i: (0, col)}[which])
    blk512 = lambda col: pl.BlockSpec((BLK, 512), lambda i: (i, col))
    full = lambda r, c: pl.BlockSpec((r, c), lambda i: (0, 0))
    acc = pltpu.VMEM((N_HEADS, BLK, BLK), F32)
    return pl.pallas_call(
        body, grid=(NBLK,),
        in_specs=[smem, smem, pl.BlockSpec((2, BLK, BLK), lambda i: (0, 0, 0)), blk512(QA),
                  kv(KA, 0), kv(KA, 1), kv(KA, 2), kv(VA, 0), kv(VA, 1), kv(VA, 2),
                  blk512(0), blk512(0), pl.BlockSpec((N_HEADS, BLK, 1), lambda i: (0, i, 0))],
        out_specs=[blk512(0), full(LP, BLK), full(LP, BLK), full(N_BUCKETS, BLK), full(1, BLK)],
        out_shape=[SDS((LP, 512), BF16), SDS((LP, BLK), F32), SDS((LP, BLK), F32),
                   SDS((N_BUCKETS, BLK), F32), SDS((1, BLK), F32)],
        scratch_shapes=[acc, acc, acc, acc, acc, pltpu.VMEM((N_HEADS, BLK, 1), F32)],
        compiler_params=_cparams(("arbitrary",)), name="swa_bwd",
    )(rel_bias, sinks, bkt, proj, proj, proj, proj, proj, proj, proj, o_a, dmix, lse)


def _local_step(x, tgt, meta, rel_bias, g_pre_mix, g_post_mix, g_pre_ffn, g_post_ffn, b_forget, sinks,
                w_in_b, w_out_b, w_gu_b, w_dn_b):
    bkt = jnp.asarray(_bucket_tables())
    h0 = jnp.concatenate([jnp.zeros((PAD_ROWS, D_MODEL), F32), meta, x], axis=0)
    tgt_p = jnp.concatenate([jnp.zeros((ROW0, D_MODEL), F32), tgt], axis=0)
    b_p = jnp.pad(b_forget, ((0, 0), (0, BLK - N_HEADS)))

    hn1, proj, f = _pre_mix(h0, g_pre_mix, w_in_b)
    o_a, lse_a = _swa_fwd(proj, rel_bias, sinks, bkt)
    cum = _forget_cumsum(f, b_p)
    ck_t = cum[:, :N_HEADS].T.reshape(N_HEADS, 1, LP)
    o_b, lse_b = _fox_fwd(proj, ck_t)
    a, h1, hn2 = _attn_out(o_a, o_b, w_out_b, h0, g_post_mix, g_pre_ffn)
    g, u, act = _ffn_up(hn2, w_gu_b)
    dff, dy, loss_blk, dg_post_ffn = _ffn_down_loss(act, w_dn_b, h1, tgt_p, g_post_ffn)

    dw_dn = _mm_tn(act, dff, FF_T, "dw_down")
    dg, du = _ffn_down_bwd(dff, w_dn_b, g, u)
    dw_gu = _dw_gate_up(hn2, dg, du)
    dh1, da, dg_pre_ffn, dg_post_mix = _ffn_up_bwd(dg, du, w_gu_b, h1, a, dy, g_pre_ffn, g_post_mix)
    dw_out = jnp.concatenate([_mm_tn(o_a, da, 512, "dw_out_a"), _mm_tn(o_b, da, 512, "dw_out_b")], axis=0)
    dmix = _attn_out_bwd(da, w_out_b)
    dq_b, dk_b, dv_b, dck, dcq = _fox_bwd(proj, o_b, dmix, lse_b, ck_t)
    dq_a, dk_a, dv_a, dbias, dsink = _swa_bwd(proj, o_a, dmix, lse_a, rel_bias, sinks, bkt)
    dcum = dcq - jnp.pad(dck.reshape(N_HEADS, LP).T, ((0, 0), (0, BLK - N_HEADS)))
    df, db = _forget_cumsum_bwd(dcum, f, b_p)
    dproj, dh0, dg_pre_mix = _pre_mix_bwd(dq_a, dq_b, dk_b, dv_b, dk_a, dv_a, df, w_in_b, h0, dh1, g_pre_mix)
    dw_in = _mm_tn(hn1, dproj, 512, "dw_in")

    return dict(loss=loss_blk[0, 0], grad_x=dh0[ROW0:], meta=dh0[PAD_ROWS:ROW0],
                rel_bias=dbias[:, :N_HEADS], ln_pre_mix=dg_pre_mix, ln_post_mix=dg_post_mix,
                ln_pre_ffn=dg_pre_ffn, ln_post_ffn=dg_post_ffn, b_forget=db[:, :N_HEADS],
                sinks=dsink[:, :N_HEADS], w_in=dw_in, w_out=dw_out, w_gate_up=dw_gu, w_down=dw_dn)


HBM_SPEC = pl.BlockSpec(memory_space=pltpu.HBM)
N_SMALL = 24


def _place():
    x, y, c = lax.axis_index("x"), lax.axis_index("y"), lax.axis_index("c")
    return x, y, c, [(1 - x, y), (x, 1 - y), (1 - x, 1 - y)]


def _gather_over_chips(shards):
    n = len(shards)

    def body(*refs):
        ins, outs = refs[:n], refs[n:2 * n]
        send_sems, recv_sems, local_sems = refs[2 * n:]
        x, y, c, others = _place()
        chip = 2 * x + y
        sibling = (x, y, 1 - c)

        def rc(a, k, src, dst, to):
            return pltpu.make_async_remote_copy(src_ref=src, dst_ref=dst, send_sem=send_sems.at[6 * a + k],
                                                recv_sem=recv_sems.at[6 * a + k], device_id=to, device_id_type=MESH)

        own = [pltpu.make_async_copy(ins[a], outs[a].at[chip], local_sems.at[a]) for a in range(n)]
        for cp in own:
            cp.start()
        first = [rc(a, j, ins[a].at[c], outs[a].at[chip, c], (ox, oy, c))
                 for a in range(n) for j, (ox, oy) in enumerate(others)]
        for cp in first:
            cp.start()
        passed = []
        for a in range(n):
            for j, (ox, oy) in enumerate(others):
                landed = outs[a].at[2 * ox + oy, c]
                rc(a, j, landed, landed, sibling).wait_recv()
                cp = rc(a, 3 + j, landed, landed, sibling)
                cp.start()
                passed.append(cp)
        for a in range(n):
            for j, (ox, oy) in enumerate(others):
                other_half = outs[a].at[2 * ox + oy, 1 - c]
                rc(a, 3 + j, other_half, other_half, sibling).wait_recv()
        for cp in first + passed:
            cp.wait_send()
        for cp in own:
            cp.wait()

    return pl.pallas_call(
        body, in_specs=[HBM_SPEC] * n, out_specs=[HBM_SPEC] * n,
        out_shape=[SDS((4,) + s.shape, s.dtype) for s in shards],
        scratch_shapes=[pltpu.SemaphoreType.DMA((6 * n,)), pltpu.SemaphoreType.DMA((6 * n,)),
                        pltpu.SemaphoreType.DMA((n,))],
        name="gather_weights")(*shards)


def _swap_halves(grads):
    n = len(grads)

    def body(*refs):
        ins, outs = refs[:n], refs[n:2 * n]
        send_sems, recv_sems = refs[2 * n:]
        x, y, c, _ = _place()
        copies = [pltpu.make_async_remote_copy(
            src_ref=ins[a].at[s, 1 - c], dst_ref=outs[a].at[s], send_sem=send_sems.at[4 * a + s],
            recv_sem=recv_sems.at[4 * a + s], device_id=(x, y, 1 - c), device_id_type=MESH)
            for a in range(n) for s in range(4)]
        for cp in copies:
            cp.start()
        for cp in copies:
            cp.wait()

    return pl.pallas_call(
        body, in_specs=[HBM_SPEC] * n, out_specs=[HBM_SPEC] * n,
        out_shape=[SDS((4,) + g.shape[2:], g.dtype) for g in grads],
        scratch_shapes=[pltpu.SemaphoreType.DMA((4 * n,)), pltpu.SemaphoreType.DMA((4 * n,))],
        name="swap_halves")(*grads)


def _pair_sum(g, got, c_arr, name):
    rh, cc = got.shape[1:]

    def body(c_ref, g_ref, p_ref, o_ref):
        o_ref[0] = (g_ref[0, 0] + p_ref[0]).astype(BF16)

    grid_spec = pltpu.PrefetchScalarGridSpec(
        num_scalar_prefetch=1, grid=(4,),
        in_specs=[pl.BlockSpec((1, 1, rh, cc), lambda s, c_ref: (s, c_ref[0], 0, 0)),
                  pl.BlockSpec((1, rh, cc), lambda s, c_ref: (s, 0, 0))],
        out_specs=pl.BlockSpec((1, rh, cc), lambda s, c_ref: (s, 0, 0)))
    return pl.pallas_call(body, grid_spec=grid_spec, out_shape=SDS((4, rh, cc), BF16),
                          compiler_params=_cparams(("parallel",)), name=name)(c_arr, g, got)


def _exchange_over_chips(parts, small):
    n = len(parts)

    def body(*refs):
        ins, small_ref = refs[:n], refs[n]
        outs, small_out = refs[n + 1:2 * n + 1], refs[2 * n + 1]
        send_sems, recv_sems, local_sems, ssend, srecv = refs[2 * n + 2:]
        x, y, c, others = _place()
        chip = 2 * x + y
        me = 4 * x + 2 * y + c
        own = [pltpu.make_async_copy(ins[a].at[chip], outs[a].at[chip], local_sems.at[a]) for a in range(n)]
        own.append(pltpu.make_async_copy(small_ref, small_out.at[me], local_sems.at[n]))
        for cp in own:
            cp.start()
        peers = [(x, y, 1 - c)] + [(ox, oy, c) for ox, oy in others] + [(ox, oy, 1 - c) for ox, oy in others]
        small_copies = [pltpu.make_async_remote_copy(
            src_ref=small_ref, dst_ref=small_out.at[me], send_sem=ssend.at[k], recv_sem=srecv.at[k],
            device_id=peer, device_id_type=MESH) for k, peer in enumerate(peers)]
        for cp in small_copies:
            cp.start()
        copies = [pltpu.make_async_remote_copy(
            src_ref=ins[a].at[2 * ox + oy], dst_ref=outs[a].at[chip], send_sem=send_sems.at[3 * a + j],
            recv_sem=recv_sems.at[3 * a + j], device_id=(ox, oy, c), device_id_type=MESH)
            for a in range(n) for j, (ox, oy) in enumerate(others)]
        for cp in copies:
            cp.start()
        for cp in small_copies + copies:
            cp.wait()
        for cp in own:
            cp.wait()

    return pl.pallas_call(
        body, in_specs=[HBM_SPEC] * (n + 1), out_specs=[HBM_SPEC] * (n + 1),
        out_shape=[SDS(p.shape, p.dtype) for p in parts] + [SDS((8,) + small.shape, small.dtype)],
        scratch_shapes=[pltpu.SemaphoreType.DMA((3 * n,)), pltpu.SemaphoreType.DMA((3 * n,)),
                        pltpu.SemaphoreType.DMA((n + 1,)), pltpu.SemaphoreType.DMA((7,)),
                        pltpu.SemaphoreType.DMA((7,))],
        name="exchange_grads")(*parts, small)


def _chip_sum(p, name):
    rh, cc = p.shape[1:]
    tr = rh // 2

    def body(p_ref, o_ref):
        o_ref[...] = ((p_ref[0].astype(F32) + p_ref[1].astype(F32)) + p_ref[2].astype(F32)) + p_ref[3].astype(F32)

    return pl.pallas_call(
        body, grid=(2,), in_specs=[pl.BlockSpec((4, tr, cc), lambda i: (0, i, 0))],
        out_specs=pl.BlockSpec((tr, cc), lambda i: (i, 0)), out_shape=SDS((rh, cc), F32),
        compiler_params=_cparams(("parallel",)), name=name)(p)


def _device_sum(p):
    def body(p_ref, o_ref):
        acc = p_ref[0]
        for k in range(1, 8):
            acc = acc + p_ref[k]
        o_ref[...] = acc

    return pl.pallas_call(body, out_shape=SDS(p.shape[1:], F32), name="small_sum")(p)


def _join_halves(halves):
    n = len(halves)

    def body(*refs):
        ins, outs = refs[:n], refs[n:2 * n]
        send_sems, recv_sems, local_sems = refs[2 * n:]
        x, y, c, _ = _place()
        own = [pltpu.make_async_copy(ins[a], outs[a].at[c], local_sems.at[a]) for a in range(n)]
        for cp in own:
            cp.start()
        copies = [pltpu.make_async_remote_copy(
            src_ref=ins[a], dst_ref=outs[a].at[c], send_sem=send_sems.at[a], recv_sem=recv_sems.at[a],
            device_id=(x, y, 1 - c), device_id_type=MESH) for a in range(n)]
        for cp in copies:
            cp.start()
        for cp in copies:
            cp.wait()
        for cp in own:
            cp.wait()

    return pl.pallas_call(
        body, in_specs=[HBM_SPEC] * n, out_specs=[HBM_SPEC] * n,
        out_shape=[SDS((2,) + h.shape, h.dtype) for h in halves],
        scratch_shapes=[pltpu.SemaphoreType.DMA((n,)), pltpu.SemaphoreType.DMA((n,)), pltpu.SemaphoreType.DMA((n,))],
        name="join_halves")(*halves)


def _adamw(w, g, m, v, name):
    rows, cols = w.shape
    tr = rows if rows <= 352 else (256 if rows % 256 == 0 else 352)

    def body(w_ref, g_ref, m_ref, v_ref, d_ref, nm_ref, nv_ref):
        gg = g_ref[...]
        nm = ADAM_B1 * m_ref[...] + (1.0 - ADAM_B1) * gg
        nv = ADAM_B2 * v_ref[...] + (1.0 - ADAM_B2) * (gg * gg)
        nm_ref[...] = nm
        nv_ref[...] = nv
        m_hat = nm / (1.0 - ADAM_B1 ** ADAM_STEP)
        v_hat = nv / (1.0 - ADAM_B2 ** ADAM_STEP)
        d_ref[...] = -ADAM_LR * (m_hat / (jnp.sqrt(v_hat) + ADAM_EPS) + ADAM_WD * w_ref[...])

    blk = pl.BlockSpec((tr, cols), lambda i: (i, 0))
    return pl.pallas_call(
        body, grid=(rows // tr,), in_specs=[blk] * 4, out_specs=[blk] * 3,
        out_shape=[SDS((rows, cols), F32)] * 3,
        compiler_params=_cparams(("parallel",)), name=name)(w, g, m, v)


def _pack_small(pre_mix, post_mix, pre_ffn, post_ffn, rel_bias, b_forget, sinks):
    def at(row, v):
        return jnp.pad(v, ((row, 7 - row), (0, D_MODEL - v.shape[1])))
    return (at(0, pre_mix) + at(1, post_mix) + at(2, pre_ffn) + at(3, post_ffn)
            + at(4, rel_bias.reshape(1, N_BUCKETS * N_HEADS)) + at(5, jnp.concatenate([b_forget, sinks], axis=1)))


def _unpack_small(p):
    return dict(ln_pre_mix=p[0:1], ln_post_mix=p[1:2], ln_pre_ffn=p[2:3], ln_post_ffn=p[3:4],
                rel_bias=p[4, :N_BUCKETS * N_HEADS].reshape(N_BUCKETS, N_HEADS),
                b_forget=p[5:6, 0:N_HEADS], sinks=p[5:6, N_HEADS:2 * N_HEADS])


WEIGHTS = ("meta_tokens", "rel_bias", "ln_pre_mix", "ln_post_mix", "ln_pre_ffn", "ln_post_ffn",
           "w_in", "b_forget", "sinks", "w_out", "w_gate_up", "w_down")


def kernel(x, meta_tokens, rel_bias, ln_pre_mix, ln_post_mix, ln_pre_ffn, ln_post_ffn, w_in, b_forget, sinks, w_out, w_gate_up, w_down, loss_target, m_meta_tokens, m_rel_bias, m_ln_pre_mix, m_ln_post_mix, m_ln_pre_ffn, m_ln_post_ffn, m_w_in, m_b_forget, m_sinks, m_w_out, m_w_gate_up, m_w_down, v_meta_tokens, v_rel_bias, v_ln_pre_mix, v_ln_post_mix, v_ln_pre_ffn, v_ln_post_ffn, v_w_in, v_b_forget, v_sinks, v_w_out, v_w_gate_up, v_w_down):
    xi, yi, ci = lax.axis_index("x"), lax.axis_index("y"), lax.axis_index("c")
    chip = 2 * xi + yi
    c_arr = jnp.reshape(ci, (1,)).astype(jnp.int32)

    def halves(w, dtype):
        return w.astype(dtype).reshape(2, w.shape[0] // 2, w.shape[1])

    gw_in, gw_out, gw_gu, gw_dn, g_meta = _gather_over_chips(
        [halves(w_in[0], BF16), halves(w_out[0], BF16), halves(w_gate_up[0], BF16), halves(w_down[0], BF16),
         halves(meta_tokens, F32)])
    w_in_all = gw_in.reshape(4, D_MODEL, D_PROJ // 4).transpose(1, 0, 2).reshape(D_MODEL, D_PROJ)
    w_in_b = jnp.concatenate(
        [w_in_all[:, 0:512], w_in_all[:, 768:1280], w_in_all[:, 1280:1792], w_in_all[:, 1792:2304],
         w_in_all[:, 512:640], w_in_all[:, 640:768], w_in_all[:, 2304:2312],
         jnp.zeros((D_MODEL, D_PROJ_P - D_PROJ), BF16)], axis=1)
    meta_all = g_meta.reshape(4, N_META, D_MODEL // 4).transpose(1, 0, 2).reshape(N_META, D_MODEL)

    loc = _local_step(x[0], loss_target[0], meta_all, rel_bias, ln_pre_mix, ln_post_mix, ln_pre_ffn, ln_post_ffn,
                      b_forget, sinks, w_in_b, gw_out.reshape(D_MODEL, D_MODEL),
                      gw_gu.reshape(4, D_MODEL, FF_T), gw_dn.reshape(D_FF, D_MODEL))

    n = loc["w_in"]
    dw_in = jnp.concatenate([n[:, 0:512], n[:, 2048:2176], n[:, 2176:2304], n[:, 512:1024], n[:, 1024:1536],
                             n[:, 1536:2048], n[:, 2304:2312]], axis=1)
    dw_in = dw_in.reshape(D_MODEL, 4, D_PROJ // 4).transpose(1, 0, 2)
    grads = [dw_in.reshape(4, 2, 512, D_PROJ // 4), loc["w_out"].reshape(4, 2, 128, D_MODEL),
             loc["w_gate_up"].reshape(4, 2, 512, FF_T), loc["w_down"].reshape(4, 2, 352, D_MODEL)]
    small = jnp.concatenate(
        [_pack_small(loc["ln_pre_mix"], loc["ln_post_mix"], loc["ln_pre_ffn"], loc["ln_post_ffn"],
                     loc["rel_bias"], loc["b_forget"], loc["sinks"]), loc["meta"]], axis=0)

    got = _swap_halves(grads)
    parts = [_pair_sum(g, p, c_arr, "pair_sum_%d" % a) for a, (g, p) in enumerate(zip(grads, got))]
    *landed, small_all = _exchange_over_chips(parts, small)
    mine = [_chip_sum(p, "chip_sum_%d" % a) for a, p in enumerate(landed)]
    small_sum = _device_sum(small_all)
    full = _join_halves(mine)
    g_w_in = full[0].reshape(D_MODEL, D_PROJ // 4)
    g_w_out = full[1].reshape(D_MODEL // 4, D_MODEL)
    g_w_gu = full[2].reshape(D_MODEL, FF_T)
    g_w_dn = full[3].reshape(D_FF // 4, D_MODEL)
    g_meta_tokens = lax.dynamic_slice(small_sum[8:N_SMALL], (0, chip * (D_MODEL // 4)), (N_META, D_MODEL // 4))
    g_small = small_sum[0:8]

    grad = _unpack_small(g_small)
    grad.update(meta_tokens=g_meta_tokens, w_in=g_w_in[None], w_out=g_w_out[None], w_gate_up=g_w_gu[None],
                w_down=g_w_dn[None])

    delta, new_m, new_v = {}, {}, {}
    big = dict(w_in=(w_in, m_w_in, v_w_in, g_w_in), w_out=(w_out, m_w_out, v_w_out, g_w_out),
               w_gate_up=(w_gate_up, m_w_gate_up, v_w_gate_up, g_w_gu), w_down=(w_down, m_w_down, v_w_down, g_w_dn))
    for name, (w, m, v, g) in big.items():
        d, nm, nv = _adamw(w[0], g, m[0], v[0], "adamw_" + name)
        delta[name], new_m[name], new_v[name] = d[None], nm[None], nv[None]
    delta["meta_tokens"], new_m["meta_tokens"], new_v["meta_tokens"] = _adamw(
        meta_tokens, g_meta_tokens, m_meta_tokens, v_meta_tokens, "adamw_meta")
    d, nm, nv = _adamw(
        _pack_small(ln_pre_mix, ln_post_mix, ln_pre_ffn, ln_post_ffn, rel_bias, b_forget, sinks), g_small,
        _pack_small(m_ln_pre_mix, m_ln_post_mix, m_ln_pre_ffn, m_ln_post_ffn, m_rel_bias, m_b_forget, m_sinks),
        _pack_small(v_ln_pre_mix, v_ln_post_mix, v_ln_pre_ffn, v_ln_post_ffn, v_rel_bias, v_b_forget, v_sinks),
        "adamw_small")
    delta.update(_unpack_small(d))
    new_m.update(_unpack_small(nm))
    new_v.update(_unpack_small(nv))

    loss = lax.psum(loc["loss"], ("x", "y", "c"))
    return (loss, loc["grad_x"][None], *[grad[k] for k in WEIGHTS], *[delta[k] for k in WEIGHTS],
            *[new_m[k] for k in WEIGHTS], *[new_v[k] for k in WEIGHTS])
```

```python
import math

import numpy as np
import jax
import jax.numpy as jnp
from jax import lax
from jax.experimental import pallas as pl
from jax.experimental.pallas import tpu as pltpu

F32 = jnp.float32
BF16 = jnp.bfloat16
MESH = pl.DeviceIdType.MESH
SDS = jax.ShapeDtypeStruct

D_MODEL = 1024
SEQ = 4096
N_META = 16
N_HEADS = 8
HALF = 64
D_FF = 2816
N_BUCKETS = 32
EPS = 1e-6
NEG = -1e30
SCALE = 0.125
PAD_ROWS = 112
ROW0 = PAD_ROWS + N_META
LP = ROW0 + SEQ
BLK = 128
NBLK = LP // BLK
TM = 384
NT = LP // TM
D_PROJ = 2312
D_PROJ_P = 2432
D_QKV = 2304
FF_T = 1408
VMEM_LIMIT = 56 * 1024 * 1024

ADAM_LR = 0.001
ADAM_B1 = 0.9
ADAM_B2 = 0.999
ADAM_EPS = 1e-08
ADAM_WD = 0.01
ADAM_STEP = 10

QA, QB, KB, VB = 0, 1, 2, 3
KA, VA = 16, 17

NT_DIMS = (((1,), (1,)), ((), ()))
TN_DIMS = (((0,), (0,)), ((), ()))


def _cparams(sem):
    return pltpu.CompilerParams(dimension_semantics=sem, vmem_limit_bytes=VMEM_LIMIT)


def _t5_bucket_np(d):
    n = np.maximum(d, 0).astype(np.int32)
    nf = np.maximum(n, 1).astype(np.float32)
    large = 16 + (np.log(nf / np.float32(16)) / np.float32(math.log(8.0)) * np.float32(16)).astype(np.int32)
    large = np.minimum(large, N_BUCKETS - 1)
    return np.where(n < 16, n, large).astype(np.int32)


def _bucket_tables():
    qi = np.arange(BLK)[:, None]
    ki = np.arange(BLK)[None, :]
    return np.stack([_t5_bucket_np(qi - ki), _t5_bucket_np(qi - ki + BLK)])


def _rms(x):
    return lax.rsqrt(jnp.mean(x * x, axis=-1, keepdims=True) + EPS)


def _rms_bwd(n, r, gdy):
    return r * (gdy - n * jnp.mean(n * gdy, axis=-1, keepdims=True))


def _pre_mix(h0, gain, w_in_b):
    def body(h_ref, g_ref, w_ref, hn_ref, proj_ref, f_ref):
        x = h_ref[...]
        hn = (x * _rms(x) * g_ref[...]).astype(BF16)
        hn_ref[...] = hn
        p = jnp.dot(hn, w_ref[...], preferred_element_type=F32)
        proj_ref[...] = p[:, :D_QKV].astype(BF16)
        f_ref[...] = p[:, D_QKV:]

    return pl.pallas_call(
        body, grid=(NT,),
        in_specs=[pl.BlockSpec((TM, D_MODEL), lambda i: (i, 0)),
                  pl.BlockSpec((1, D_MODEL), lambda i: (0, 0)),
                  pl.BlockSpec((D_MODEL, D_PROJ_P), lambda i: (0, 0))],
        out_specs=[pl.BlockSpec((TM, D_MODEL), lambda i: (i, 0)),
                   pl.BlockSpec((TM, D_QKV), lambda i: (i, 0)),
                   pl.BlockSpec((TM, BLK), lambda i: (i, 0))],
        out_shape=[SDS((LP, D_MODEL), BF16), SDS((LP, D_QKV), BF16), SDS((LP, BLK), F32)],
        compiler_params=_cparams(("parallel",)), name="pre_mix")(h0, gain, w_in_b)


def _attn_out(o_a, o_b, w_out_b, h0, g_post, g_pre_ffn):
    def body(oa_ref, ob_ref, w_ref, h0_ref, gp_ref, gf_ref, a_ref, h1_ref, hn2_ref):
        a = (jnp.dot(oa_ref[...], w_ref[0:512, :], preferred_element_type=F32)
             + jnp.dot(ob_ref[...], w_ref[512:1024, :], preferred_element_type=F32))
        a_ref[...] = a
        h1 = h0_ref[...] + a * _rms(a) * gp_ref[...]
        h1_ref[...] = h1
        hn2_ref[...] = (h1 * _rms(h1) * gf_ref[...]).astype(BF16)

    row = lambda w: pl.BlockSpec((TM, w), lambda i: (i, 0))
    vec = pl.BlockSpec((1, D_MODEL), lambda i: (0, 0))
    return pl.pallas_call(
        body, grid=(NT,),
        in_specs=[row(512), row(512), pl.BlockSpec((D_MODEL, D_MODEL), lambda i: (0, 0)), row(D_MODEL), vec, vec],
        out_specs=[row(D_MODEL), row(D_MODEL), row(D_MODEL)],
        out_shape=[SDS((LP, D_MODEL), F32), SDS((LP, D_MODEL), F32), SDS((LP, D_MODEL), BF16)],
        compiler_params=_cparams(("parallel",)), name="attn_out")(o_a, o_b, w_out_b, h0, g_post, g_pre_ffn)


def _ffn_up(hn2, w_gu_b):
    def body(x_ref, wg_ref, wu_ref, g_ref, u_ref, act_ref):
        x = x_ref[...]
        g = jnp.dot(x, wg_ref[0], preferred_element_type=F32)
        u = jnp.dot(x, wu_ref[0], preferred_element_type=F32)
        g_ref[...] = g.astype(BF16)
        u_ref[...] = u.astype(BF16)
        act_ref[...] = (g * (1.0 / (1.0 + jnp.exp(-g))) * u).astype(BF16)

    out = pl.BlockSpec((TM, FF_T), lambda j, i: (i, j))
    return pl.pallas_call(
        body, grid=(2, NT),
        in_specs=[pl.BlockSpec((TM, D_MODEL), lambda j, i: (i, 0)),
                  pl.BlockSpec((1, D_MODEL, FF_T), lambda j, i: (j, 0, 0)),
                  pl.BlockSpec((1, D_MODEL, FF_T), lambda j, i: (j + 2, 0, 0))],
        out_specs=[out, out, out],
        out_shape=[SDS((LP, D_FF), BF16)] * 3,
        compiler_params=_cparams(("parallel", "parallel")), name="ffn_up")(hn2, w_gu_b, w_gu_b)


def _ffn_down_loss(act, w_dn_b, h1, tgt_p, g_post_ffn):
    def body(act_ref, w_ref, h1_ref, t_ref, g_ref, dff_ref, dy_ref, loss_ref, dg_ref):
        i = pl.program_id(0)

        @pl.when(i == 0)
        def _():
            loss_ref[...] = jnp.zeros_like(loss_ref)
            dg_ref[...] = jnp.zeros_like(dg_ref)

        ff = jnp.dot(act_ref[...], w_ref[...], preferred_element_type=F32)
        r = _rms(ff)
        n = ff * r
        g = g_ref[...]
        y = h1_ref[...] + n * g
        rows = i * TM + lax.broadcasted_iota(jnp.int32, (TM, D_MODEL), 0)
        diff = jnp.where(rows >= ROW0, y - t_ref[...], 0.0)
        loss_ref[...] += 0.5 * jnp.sum(diff * diff) / D_MODEL
        dy = diff / D_MODEL
        dy_ref[...] = dy
        dg_ref[...] += jnp.sum(dy * n, axis=0, keepdims=True)
        dff_ref[...] = _rms_bwd(n, r, g * dy).astype(BF16)

    row = pl.BlockSpec((TM, D_MODEL), lambda i: (i, 0))
    return pl.pallas_call(
        body, grid=(NT,),
        in_specs=[pl.BlockSpec((TM, D_FF), lambda i: (i, 0)), pl.BlockSpec((D_FF, D_MODEL), lambda i: (0, 0)),
                  row, row, pl.BlockSpec((1, D_MODEL), lambda i: (0, 0))],
        out_specs=[row, row, pl.BlockSpec((8, BLK), lambda i: (0, 0)), pl.BlockSpec((1, D_MODEL), lambda i: (0, 0))],
        out_shape=[SDS((LP, D_MODEL), BF16), SDS((LP, D_MODEL), F32), SDS((8, BLK), F32), SDS((1, D_MODEL), F32)],
        compiler_params=_cparams(("arbitrary",)), name="ffn_down_loss")(act, w_dn_b, h1, tgt_p, g_post_ffn)


def _ffn_down_bwd(dff, w_dn_b, g, u):
    def body(d_ref, w_ref, g_ref, u_ref, dg_ref, du_ref):
        dact = lax.dot_general(d_ref[...], w_ref[...], NT_DIMS, preferred_element_type=F32)
        gg = g_ref[...].astype(F32)
        sig = 1.0 / (1.0 + jnp.exp(-gg))
        dg_ref[...] = (dact * u_ref[...].astype(F32) * sig * (1.0 + gg * (1.0 - sig))).astype(BF16)
        du_ref[...] = (dact * gg * sig).astype(BF16)

    blk = pl.BlockSpec((TM, FF_T), lambda j, i: (i, j))
    return pl.pallas_call(
        body, grid=(2, NT),
        in_specs=[pl.BlockSpec((TM, D_MODEL), lambda j, i: (i, 0)),
                  pl.BlockSpec((FF_T, D_MODEL), lambda j, i: (j, 0)), blk, blk],
        out_specs=[blk, blk],
        out_shape=[SDS((LP, D_FF), BF16)] * 2,
        compiler_params=_cparams(("parallel", "parallel")), name="ffn_down_bwd")(dff, w_dn_b, g, u)


def _ffn_up_bwd(dg, du, w_gu_b, h1, a, dy, g_pre_ffn, g_post_mix):
    def body(dg_ref, du_ref, w_ref, h1_ref, a_ref, dy_ref, gf_ref, gp_ref,
             dh1_ref, da_ref, dgf_ref, dgp_ref, acc):
        i = pl.program_id(0)
        s = pl.program_id(1)

        @pl.when((i == 0) & (s == 0))
        def _():
            dgf_ref[...] = jnp.zeros_like(dgf_ref)
            dgp_ref[...] = jnp.zeros_like(dgp_ref)

        @pl.when(s == 0)
        def _():
            acc[...] = jnp.zeros_like(acc)

        @pl.when(s < 2)
        def _():
            acc[...] += lax.dot_general(dg_ref[...], w_ref[0], NT_DIMS, preferred_element_type=F32)

        @pl.when(s >= 2)
        def _():
            acc[...] += lax.dot_general(du_ref[...], w_ref[0], NT_DIMS, preferred_element_type=F32)

        @pl.when(s == 3)
        def _():
            dhn2 = acc[...]
            h1 = h1_ref[...]
            r2 = _rms(h1)
            n2 = h1 * r2
            dgf_ref[...] += jnp.sum(dhn2 * n2, axis=0, keepdims=True)
            dh1 = dy_ref[...] + _rms_bwd(n2, r2, gf_ref[...] * dhn2)
            dh1_ref[...] = dh1
            av = a_ref[...]
            ra = _rms(av)
            na = av * ra
            dgp_ref[...] += jnp.sum(dh1 * na, axis=0, keepdims=True)
            da_ref[...] = _rms_bwd(na, ra, gp_ref[...] * dh1).astype(BF16)

    row = pl.BlockSpec((TM, D_MODEL), lambda i, s: (i, 0))
    vec = pl.BlockSpec((1, D_MODEL), lambda i, s: (0, 0))
    return pl.pallas_call(
        body, grid=(NT, 4),
        in_specs=[pl.BlockSpec((TM, FF_T), lambda i, s: (i, jnp.minimum(s, 1))),
                  pl.BlockSpec((TM, FF_T), lambda i, s: (i, jnp.maximum(s - 2, 0))),
                  pl.BlockSpec((1, D_MODEL, FF_T), lambda i, s: (s, 0, 0)),
                  row, row, row, vec, vec],
        out_specs=[row, row, vec, vec],
        out_shape=[SDS((LP, D_MODEL), F32), SDS((LP, D_MODEL), BF16), SDS((1, D_MODEL), F32), SDS((1, D_MODEL), F32)],
        scratch_shapes=[pltpu.VMEM((TM, D_MODEL), F32)],
        compiler_params=_cparams(("arbitrary", "arbitrary")), name="ffn_up_bwd",
    )(dg, du, w_gu_b, h1, a, dy, g_pre_ffn, g_post_mix)


def _attn_out_bwd(da, w_out_b):
    def body(d_ref, w_ref, o_ref):
        o_ref[...] = lax.dot_general(d_ref[...], w_ref[...], NT_DIMS, preferred_element_type=F32).astype(BF16)

    row = pl.BlockSpec((TM, D_MODEL), lambda i: (i, 0))
    return pl.pallas_call(
        body, grid=(NT,),
        in_specs=[row, pl.BlockSpec((D_MODEL, D_MODEL), lambda i: (0, 0))],
        out_specs=row, out_shape=SDS((LP, D_MODEL), BF16),
        compiler_params=_cparams(("parallel",)), name="attn_out_bwd")(da, w_out_b)


def _pre_mix_bwd(dq_a, dq_b, dk_b, dv_b, dk_a, dv_a, df, w_in_b, h0, dh1, g_pre_mix):
    def body(qa_ref, qb_ref, kb_ref, vb_ref, ka_ref, va_ref, f_ref, w_ref, h0_ref, dh1_ref, g_ref,
             dproj_ref, dh0_ref, dg_ref):
        i = pl.program_id(0)

        @pl.when(i == 0)
        def _():
            dg_ref[...] = jnp.zeros_like(dg_ref)

        dproj = jnp.concatenate(
            [qa_ref[...], (qb_ref[...] * SCALE).astype(BF16), kb_ref[...], vb_ref[...],
             ka_ref[...].astype(BF16), va_ref[...].astype(BF16), f_ref[...].astype(BF16)], axis=1)
        dproj_ref[...] = dproj
        dhn = lax.dot_general(dproj, w_ref[...], NT_DIMS, preferred_element_type=F32)
        x = h0_ref[...]
        r = _rms(x)
        n = x * r
        dg_ref[...] += jnp.sum(dhn * n, axis=0, keepdims=True)
        dh0_ref[...] = dh1_ref[...] + _rms_bwd(n, r, g_ref[...] * dhn)

    row = lambda w: pl.BlockSpec((TM, w), lambda i: (i, 0))
    vec = pl.BlockSpec((1, D_MODEL), lambda i: (0, 0))
    return pl.pallas_call(
        body, grid=(NT,),
        in_specs=[row(512), row(512), row(512), row(512), row(BLK), row(BLK), row(BLK),
                  pl.BlockSpec((D_MODEL, D_PROJ_P), lambda i: (0, 0)), row(D_MODEL), row(D_MODEL), vec],
        out_specs=[row(D_PROJ_P), row(D_MODEL), vec],
        out_shape=[SDS((LP, D_PROJ_P), BF16), SDS((LP, D_MODEL), F32), SDS((1, D_MODEL), F32)],
        compiler_params=_cparams(("arbitrary",)), name="pre_mix_bwd",
    )(dq_a, dq_b, dk_b, dv_b, dk_a, dv_a, df, w_in_b, h0, dh1, g_pre_mix)


def _mm_tn(a, b, tm, name):
    m_total = a.shape[1]
    n = b.shape[1]

    def body(a_ref, b_ref, o_ref):
        @pl.when(pl.program_id(1) == 0)
        def _():
            o_ref[...] = jnp.zeros_like(o_ref)
        o_ref[...] += lax.dot_general(a_ref[...], b_ref[...], TN_DIMS, preferred_element_type=F32)

    return pl.pallas_call(
        body, grid=(m_total // tm, NT),
        in_specs=[pl.BlockSpec((TM, tm), lambda mi, k: (k, mi)),
                  pl.BlockSpec((TM, n), lambda mi, k: (k, 0))],
        out_specs=pl.BlockSpec((tm, n), lambda mi, k: (mi, 0)),
        out_shape=SDS((m_total, n), F32),
        compiler_params=_cparams(("parallel", "arbitrary")), name=name)(a, b)


def _dw_gate_up(hn2, dg, du):
    def body(a_ref, dg_ref, du_ref, o_ref):
        s = pl.program_id(0)

        @pl.when(pl.program_id(1) == 0)
        def _():
            o_ref[...] = jnp.zeros_like(o_ref)

        @pl.when(s < 2)
        def _():
            o_ref[0] += lax.dot_general(a_ref[...], dg_ref[...], TN_DIMS, preferred_element_type=F32)

        @pl.when(s >= 2)
        def _():
            o_ref[0] += lax.dot_general(a_ref[...], du_ref[...], TN_DIMS, preferred_element_type=F32)

    return pl.pallas_call(
        body, grid=(4, NT),
        in_specs=[pl.BlockSpec((TM, D_MODEL), lambda s, k: (k, 0)),
                  pl.BlockSpec((TM, FF_T), lambda s, k: (k, jnp.minimum(s, 1))),
                  pl.BlockSpec((TM, FF_T), lambda s, k: (k, jnp.maximum(s - 2, 0)))],
        out_specs=pl.BlockSpec((1, D_MODEL, FF_T), lambda s, k: (s, 0, 0)),
        out_shape=SDS((4, D_MODEL, FF_T), F32),
        compiler_params=_cparams(("parallel", "arbitrary")), name="dw_gate_up")(hn2, dg, du)


def _split3(x):
    hi = x.astype(BF16)
    r1 = x - hi.astype(F32)
    mid = r1.astype(BF16)
    lo = (r1 - mid.astype(F32)).astype(BF16)
    return hi, mid, lo


def _tri_matmul(tri, x):
    hi, mid, lo = _split3(x)
    dot = lambda t: jnp.dot(tri, t, preferred_element_type=F32)
    return dot(hi) + dot(mid) + dot(lo)


def _forget_cumsum(f, b_forget_p):
    def body(f_ref, b_ref, cum_ref, carry):
        i = pl.program_id(0)

        @pl.when(i == 0)
        def _():
            carry[...] = jnp.zeros_like(carry)

        z = f_ref[...] + b_ref[...]
        ls = jnp.minimum(z, 0.0) - jnp.log(1.0 + jnp.exp(-jnp.abs(z)))
        rows = i * BLK + lax.broadcasted_iota(jnp.int32, (BLK, BLK), 0)
        ls = jnp.where(rows >= PAD_ROWS, ls, 0.0)
        r = lax.broadcasted_iota(jnp.int32, (BLK, BLK), 0)
        c = lax.broadcasted_iota(jnp.int32, (BLK, BLK), 1)
        tri = (c <= r).astype(BF16)
        cum = _tri_matmul(tri, ls) + carry[...]
        cum_ref[...] = cum
        carry[...] = cum[BLK - 1:BLK, :]

    return pl.pallas_call(
        body, grid=(NBLK,),
        in_specs=[pl.BlockSpec((BLK, BLK), lambda i: (i, 0)), pl.BlockSpec((1, BLK), lambda i: (0, 0))],
        out_specs=pl.BlockSpec((BLK, BLK), lambda i: (i, 0)),
        out_shape=SDS((LP, BLK), F32),
        scratch_shapes=[pltpu.VMEM((1, BLK), F32)],
        compiler_params=_cparams(("arbitrary",)), name="forget_cumsum")(f, b_forget_p)


def _forget_cumsum_bwd(dcum, f, b_forget_p):
    def body(d_ref, f_ref, b_ref, df_ref, db_ref, carry):
        i = pl.program_id(0)

        @pl.when(i == 0)
        def _():
            carry[...] = jnp.zeros_like(carry)
            db_ref[...] = jnp.zeros_like(db_ref)

        blk = NBLK - 1 - i
        r = lax.broadcasted_iota(jnp.int32, (BLK, BLK), 0)
        c = lax.broadcasted_iota(jnp.int32, (BLK, BLK), 1)
        tri = (c >= r).astype(BF16)
        d = d_ref[...]
        dls = _tri_matmul(tri, d) + carry[...]
        carry[...] = dls[0:1, :]
        z = f_ref[...] + b_ref[...]
        rows = blk * BLK + r
        df = jnp.where(rows >= PAD_ROWS, dls / (1.0 + jnp.exp(z)), 0.0)
        df_ref[...] = df
        db_ref[...] += jnp.sum(df, axis=0, keepdims=True)

    rev = pl.BlockSpec((BLK, BLK), lambda i: (NBLK - 1 - i, 0))
    vec = pl.BlockSpec((1, BLK), lambda i: (0, 0))
    return pl.pallas_call(
        body, grid=(NBLK,),
        in_specs=[rev, rev, vec],
        out_specs=[rev, vec],
        out_shape=[SDS((LP, BLK), F32), SDS((1, BLK), F32)],
        scratch_shapes=[pltpu.VMEM((1, BLK), F32)],
        compiler_params=_cparams(("arbitrary",)), name="forget_cumsum_bwd")(dcum, f, b_forget_p)


def _lane_half(rows):
    return lax.broadcasted_iota(jnp.int32, (rows, BLK), 1) // HALF


def _fox_valid(qi, kj):
    qrow = qi * TM + lax.broadcasted_iota(jnp.int32, (TM, TM), 0)
    krow = kj * TM + lax.broadcasted_iota(jnp.int32, (TM, TM), 1)
    return (krow <= qrow) & ((krow >= PAD_ROWS) | (qrow < PAD_ROWS))


def _fox_fwd(proj, ck_t):
    def body(q_ref, k_ref, v_ref, ck_ref, o_ref, lse_ref, m_s, l_s, acc_s):
        qi = pl.program_id(0)
        kj = pl.program_id(1)

        @pl.when(kj == 0)
        def _():
            m_s[...] = jnp.full_like(m_s, NEG)
            l_s[...] = jnp.zeros_like(l_s)
            acc_s[...] = jnp.zeros_like(acc_s)

        def tile(masked):
            valid = _fox_valid(qi, kj) if masked else None
            half_k = _lane_half(TM)
            for pp in range(4):
                cols = slice(pp * BLK, (pp + 1) * BLK)
                qs = (q_ref[:, cols].astype(F32) * SCALE).astype(BF16)
                kp = k_ref[:, cols]
                vp = v_ref[:, cols]
                pv = jnp.zeros((TM, BLK), F32)
                alphas = []
                for e in range(2):
                    h = 2 * pp + e
                    ke = jnp.where(half_k == e, kp, jnp.zeros_like(kp))
                    ve = jnp.where(half_k == e, vp, jnp.zeros_like(vp))
                    t = lax.dot_general(qs, ke, NT_DIMS, preferred_element_type=F32) - ck_ref[h]
                    if masked:
                        t = jnp.where(valid, t, NEG)
                    m_prev = m_s[h]
                    m_new = jnp.maximum(m_prev, jnp.max(t, axis=1, keepdims=True))
                    p = jnp.exp(t - m_new)
                    alpha = jnp.exp(m_prev - m_new)
                    l_s[h] = alpha * l_s[h] + jnp.sum(p, axis=1, keepdims=True)
                    m_s[h] = m_new
                    pv = pv + jnp.dot(p.astype(BF16), ve, preferred_element_type=F32)
                    alphas.append(alpha)
                a_lane = jnp.where(half_k == 0, alphas[0], alphas[1])
                acc_s[pp] = acc_s[pp] * a_lane + pv

        @pl.when((kj < qi) & (kj > 0))
        def _():
            tile(False)

        @pl.when((kj <= qi) & ((kj == qi) | (kj == 0)))
        def _():
            tile(True)

        @pl.when(kj == qi)
        def _():
            half_q = _lane_half(TM)
            for pp in range(4):
                inv = jnp.where(half_q == 0, 1.0 / l_s[2 * pp], 1.0 / l_s[2 * pp + 1])
                o_ref[:, pp * BLK:(pp + 1) * BLK] = (acc_s[pp] * inv).astype(BF16)
            for h in range(N_HEADS):
                lse_ref[h] = m_s[h] + jnp.log(l_s[h])

    kv = lambda blk: pl.BlockSpec((TM, 512), lambda qi, kj: (jnp.minimum(kj, qi), blk))
    return pl.pallas_call(
        body, grid=(NT, NT),
        in_specs=[pl.BlockSpec((TM, 512), lambda qi, kj: (qi, QB)), kv(KB), kv(VB),
                  pl.BlockSpec((N_HEADS, 1, TM), lambda qi, kj: (0, 0, jnp.minimum(kj, qi)))],
        out_specs=[pl.BlockSpec((TM, 512), lambda qi, kj: (qi, 0)),
                   pl.BlockSpec((N_HEADS, TM, 1), lambda qi, kj: (0, qi, 0))],
        out_shape=[SDS((LP, 512), BF16), SDS((N_HEADS, LP, 1), F32)],
        scratch_shapes=[pltpu.VMEM((N_HEADS, TM, 1), F32), pltpu.VMEM((N_HEADS, TM, 1), F32),
                        pltpu.VMEM((4, TM, BLK), F32)],
        compiler_params=_cparams(("parallel", "arbitrary")), name="fox_fwd")(proj, proj, proj, ck_t)


def _fox_bwd(proj, o_b, dmix, lse, ck_t):
    def body(q_ref, k_ref, v_ref, o_ref, do_ref, lse_ref, ck_ref, dq_ref, dk_ref, dv_ref, dck_ref, dcq_ref,
             dk_s, dv_s, dck_s):
        kj = pl.program_id(0)
        qi = pl.program_id(1)

        @pl.when((kj == 0) & (qi == 0))
        def _():
            dq_ref[...] = jnp.zeros_like(dq_ref)
            dcq_ref[...] = jnp.zeros_like(dcq_ref)

        @pl.when(qi == 0)
        def _():
            dk_s[...] = jnp.zeros_like(dk_s)
            dv_s[...] = jnp.zeros_like(dv_s)
            dck_s[...] = jnp.zeros_like(dck_s)

        def tile(masked):
            valid = _fox_valid(qi, kj) if masked else None
            half = _lane_half(TM)
            q0 = pl.multiple_of(qi * TM, TM)
            lane = lax.broadcasted_iota(jnp.int32, (TM, BLK), 1)
            row_sums = jnp.zeros((TM, BLK), F32)
            for pp in range(4):
                cols = slice(pp * BLK, (pp + 1) * BLK)
                qs = (q_ref[:, cols].astype(F32) * SCALE).astype(BF16)
                kp = k_ref[:, cols]
                vp = v_ref[:, cols]
                dop = do_ref[:, cols]
                prod = dop.astype(F32) * o_ref[:, cols].astype(F32)
                d0 = jnp.sum(jnp.where(half == 0, prod, 0.0), axis=1, keepdims=True)
                d1 = jnp.sum(prod, axis=1, keepdims=True) - d0
                dq = jnp.zeros((TM, BLK), F32)
                dks, dvs = [], []
                for e in range(2):
                    h = 2 * pp + e
                    ke = jnp.where(half == e, kp, jnp.zeros_like(kp))
                    ve = jnp.where(half == e, vp, jnp.zeros_like(vp))
                    t = lax.dot_general(qs, ke, NT_DIMS, preferred_element_type=F32) - ck_ref[h] - lse_ref[h]
                    if masked:
                        t = jnp.where(valid, t, NEG)
                    p = jnp.exp(t)
                    dp = lax.dot_general(dop, ve, NT_DIMS, preferred_element_type=F32)
                    ds = p * (dp - (d0 if e == 0 else d1))
                    dck_s[h] += jnp.sum(ds, axis=0, keepdims=True)
                    row_sums = jnp.where(lane == h, jnp.sum(ds, axis=1, keepdims=True), row_sums)
                    ds_b = ds.astype(BF16)
                    dq = dq + jnp.dot(ds_b, ke, preferred_element_type=F32)
                    dks.append(lax.dot_general(ds_b, qs, TN_DIMS, preferred_element_type=F32))
                    dvs.append(lax.dot_general(p.astype(BF16), dop, TN_DIMS, preferred_element_type=F32))
                dq_ref[pl.ds(q0, TM), cols] += dq
                dk_s[pp] += jnp.where(half == 0, dks[0], dks[1])
                dv_s[pp] += jnp.where(half == 0, dvs[0], dvs[1])
            dcq_ref[pl.ds(q0, TM), :] += row_sums

        @pl.when((qi > kj) & (kj > 0))
        def _():
            tile(False)

        @pl.when((qi >= kj) & ((qi == kj) | (kj == 0)))
        def _():
            tile(True)

        @pl.when(qi == NT - 1)
        def _():
            for pp in range(4):
                cols = slice(pp * BLK, (pp + 1) * BLK)
                dk_ref[:, cols] = dk_s[pp].astype(BF16)
                dv_ref[:, cols] = dv_s[pp].astype(BF16)
            dck_ref[...] = dck_s[...]

    qrow = lambda blk, arr_cols=512: pl.BlockSpec((TM, 512), lambda kj, qi: (jnp.maximum(qi, kj), blk))
    krow = lambda blk: pl.BlockSpec((TM, 512), lambda kj, qi: (kj, blk))
    return pl.pallas_call(
        body, grid=(NT, NT),
        in_specs=[qrow(QB), krow(KB), krow(VB), qrow(0), qrow(1),
                  pl.BlockSpec((N_HEADS, TM, 1), lambda kj, qi: (0, jnp.maximum(qi, kj), 0)),
                  pl.BlockSpec((N_HEADS, 1, TM), lambda kj, qi: (0, 0, kj))],
        out_specs=[pl.BlockSpec((LP, 512), lambda kj, qi: (0, 0)),
                   pl.BlockSpec((TM, 512), lambda kj, qi: (kj, 0)),
                   pl.BlockSpec((TM, 512), lambda kj, qi: (kj, 0)),
                   pl.BlockSpec((N_HEADS, 1, TM), lambda kj, qi: (0, 0, kj)),
                   pl.BlockSpec((LP, BLK), lambda kj, qi: (0, 0))],
        out_shape=[SDS((LP, 512), F32), SDS((LP, 512), BF16), SDS((LP, 512), BF16), SDS((N_HEADS, 1, LP), F32),
                   SDS((LP, BLK), F32)],
        scratch_shapes=[pltpu.VMEM((4, TM, BLK), F32), pltpu.VMEM((4, TM, BLK), F32),
                        pltpu.VMEM((N_HEADS, 1, TM), F32)],
        compiler_params=_cparams(("arbitrary", "arbitrary")), name="fox_bwd",
    )(proj, proj, proj, o_b, dmix, lse, ck_t)


def _build_bias(tab_ref, bkt_ref, bias_cur, bias_prev):
    bc = bkt_ref[0]
    bp = bkt_ref[1]
    for h in range(N_HEADS):
        def step(b, carry, h=h):
            t = tab_ref[b, h]
            return jnp.where(bc == b, t, carry[0]), jnp.where(bp == b, t, carry[1])
        zero = jnp.zeros((BLK, BLK), F32)
        cur, prev = lax.fori_loop(0, N_BUCKETS, step, (zero, zero))
        bias_cur[h] = cur
        bias_prev[h] = prev


def _kv_variants(ref):
    x = ref[...].astype(F32)
    swapped = pltpu.roll(x, HALF, 1)
    half = _lane_half(BLK)
    out = {}
    for e in range(2):
        for g in range(2):
            src = x if e == g else swapped
            out[(e, g)] = jnp.where(half == e, src, 0.0).astype(BF16)
    return out


def _swa_masks(i):
    qrow = lax.broadcasted_iota(jnp.int32, (BLK, BLK), 0)
    kcol = lax.broadcasted_iota(jnp.int32, (BLK, BLK), 1)
    big = 4 * BLK
    valid_cur = (kcol <= qrow) & (kcol >= jnp.where(i > 0, 0, PAD_ROWS))
    valid_prev = kcol > qrow + jnp.where(i >= 2, 0, big)
    valid_meta = kcol >= jnp.where(i >= 1, PAD_ROWS, big)
    return valid_cur, valid_prev, valid_meta


def _swa_scores(i, h, qs, kvar, bias_cur, bias_prev, tab_ref, masks):
    e, g = h % 2, h // 4
    first = jnp.full((BLK, BLK), i, jnp.int32) == 1
    biases = (bias_cur[h], bias_prev[h], jnp.where(first, bias_prev[h], tab_ref[N_BUCKETS - 1, h]))
    out = []
    for x in range(3):
        s = lax.dot_general(qs, kvar[x][(e, g)], NT_DIMS, preferred_element_type=F32) + biases[x]
        out.append(jnp.where(masks[x], s, NEG))
    return out


def _swa_fwd(proj, rel_bias, sinks, bkt):
    def body(tab_ref, sink_ref, bkt_ref, q_ref, kc_ref, kp_ref, km_ref, vc_ref, vp_ref, vm_ref,
             o_ref, lse_ref, bias_cur, bias_prev):
        i = pl.program_id(0)

        @pl.when(i == 0)
        def _():
            _build_bias(tab_ref, bkt_ref, bias_cur, bias_prev)

        masks = _swa_masks(i)
        kvar = [_kv_variants(r) for r in (kc_ref, kp_ref, km_ref)]
        vvar = [_kv_variants(r) for r in (vc_ref, vp_ref, vm_ref)]
        for pp in range(4):
            cols = slice(pp * BLK, (pp + 1) * BLK)
            qs = (q_ref[:, cols].astype(F32) * SCALE).astype(BF16)
            o_pair = jnp.zeros((BLK, BLK), F32)
            for e in range(2):
                h = 2 * pp + e
                g = h // 4
                s = _swa_scores(i, h, qs, kvar, bias_cur, bias_prev, tab_ref, masks)
                sink = sink_ref[0, h]
                m = jnp.maximum(jnp.maximum(jnp.max(s[0], axis=1, keepdims=True), jnp.max(s[1], axis=1, keepdims=True)),
                                jnp.maximum(jnp.max(s[2], axis=1, keepdims=True), sink))
                p = [jnp.exp(sx - m) for sx in s]
                denom = (jnp.sum(p[0], axis=1, keepdims=True) + jnp.sum(p[1], axis=1, keepdims=True)
                         + jnp.sum(p[2], axis=1, keepdims=True) + jnp.exp(sink - m))
                pv = (jnp.dot(p[0].astype(BF16), vvar[0][(e, g)], preferred_element_type=F32)
                      + jnp.dot(p[1].astype(BF16), vvar[1][(e, g)], preferred_element_type=F32)
                      + jnp.dot(p[2].astype(BF16), vvar[2][(e, g)], preferred_element_type=F32))
                o_pair = o_pair + pv * (1.0 / denom)
                lse_ref[h] = m + jnp.log(denom)
            o_ref[:, cols] = o_pair.astype(BF16)

    smem = pl.BlockSpec(memory_space=pltpu.SMEM)
    kv = lambda col, which: pl.BlockSpec(
        (BLK, BLK), {0: lambda i: (i, col), 1: lambda i: (jnp.maximum(i - 1, 0), col), 2: lambda i: (0, col)}[which])
    return pl.pallas_call(
        body, grid=(NBLK,),
        in_specs=[smem, smem, pl.BlockSpec((2, BLK, BLK), lambda i: (0, 0, 0)),
                  pl.BlockSpec((BLK, 512), lambda i: (i, QA)),
                  kv(KA, 0), kv(KA, 1), kv(KA, 2), kv(VA, 0), kv(VA, 1), kv(VA, 2)],
        out_specs=[pl.BlockSpec((BLK, 512), lambda i: (i, 0)),
                   pl.BlockSpec((N_HEADS, BLK, 1), lambda i: (0, i, 0))],
        out_shape=[SDS((LP, 512), BF16), SDS((N_HEADS, LP, 1), F32)],
        scratch_shapes=[pltpu.VMEM((N_HEADS, BLK, BLK), F32), pltpu.VMEM((N_HEADS, BLK, BLK), F32)],
        compiler_params=_cparams(("arbitrary",)), name="swa_fwd",
    )(rel_bias, sinks, bkt, proj, proj, proj, proj, proj, proj, proj)


def _swa_bwd(proj, o_a, dmix, lse, rel_bias, sinks, bkt):
    def body(tab_ref, sink_ref, bkt_ref, q_ref, kc_ref, kp_ref, km_ref, vc_ref, vp_ref, vm_ref,
             o_ref, do_ref, lse_ref, dq_ref, dk_ref, dv_ref, dbias_ref, dsink_ref,
             bias_cur, bias_prev, acc_cur, acc_prev, acc_far, dsk):
        i = pl.program_id(0)

        @pl.when(i == 0)
        def _():
            _build_bias(tab_ref, bkt_ref, bias_cur, bias_prev)
            dk_ref[...] = jnp.zeros_like(dk_ref)
            dv_ref[...] = jnp.zeros_like(dv_ref)
            acc_cur[...] = jnp.zeros_like(acc_cur)
            acc_prev[...] = jnp.zeros_like(acc_prev)
            acc_far[...] = jnp.zeros_like(acc_far)
            dsk[...] = jnp.zeros_like(dsk)

        masks = _swa_masks(i)
        half = _lane_half(BLK)
        first = jnp.full((BLK, BLK), i, jnp.int32) == 1
        kvar = [_kv_variants(r) for r in (kc_ref, kp_ref, km_ref)]
        vvar = [_kv_variants(r) for r in (vc_ref, vp_ref, vm_ref)]
        dk_blk = [jnp.zeros((BLK, BLK), F32) for _ in range(3)]
        dv_blk = [jnp.zeros((BLK, BLK), F32) for _ in range(3)]
        for pp in range(4):
            cols = slice(pp * BLK, (pp + 1) * BLK)
            qs = (q_ref[:, cols].astype(F32) * SCALE).astype(BF16)
            dop = do_ref[:, cols]
            prod = dop.astype(F32) * o_ref[:, cols].astype(F32)
            d0 = jnp.sum(jnp.where(half == 0, prod, 0.0), axis=1, keepdims=True)
            d1 = jnp.sum(prod, axis=1, keepdims=True) - d0
            dq = jnp.zeros((BLK, BLK), F32)
            for e in range(2):
                h = 2 * pp + e
                g = h // 4
                dd = d0 if e == 0 else d1
                lse_h = lse_ref[h]
                s = _swa_scores(i, h, qs, kvar, bias_cur, bias_prev, tab_ref, masks)
                dsk[h] += -jnp.exp(sink_ref[0, h] - lse_h) * dd
                ds_all = []
                for x in range(3):
                    p = jnp.exp(s[x] - lse_h)
                    dp = lax.dot_general(dop, vvar[x][(e, g)], NT_DIMS, preferred_element_type=F32)
                    ds = p * (dp - dd)
                    ds_all.append(ds)
                    ds_b = ds.astype(BF16)
                    dq = dq + jnp.dot(ds_b, kvar[x][(e, g)], preferred_element_type=F32)
                    dkv = lax.dot_general(ds_b, qs, TN_DIMS, preferred_element_type=F32)
                    dvv = lax.dot_general(p.astype(BF16), dop, TN_DIMS, preferred_element_type=F32)
                    if e != g:
                        dkv = pltpu.roll(dkv, HALF, 1)
                        dvv = pltpu.roll(dvv, HALF, 1)
                    dk_blk[x] = dk_blk[x] + jnp.where(half == g, dkv, 0.0)
                    dv_blk[x] = dv_blk[x] + jnp.where(half == g, dvv, 0.0)
                acc_cur[h] += ds_all[0]
                acc_prev[h] += ds_all[1] + jnp.where(first, ds_all[2], 0.0)
                acc_far[h] += jnp.where(first, 0.0, ds_all[2])
            dq_ref[:, cols] = (dq * SCALE).astype(BF16)

        cur0 = pl.multiple_of(i * BLK, BLK)
        prev0 = pl.multiple_of(jnp.maximum(i - 1, 0) * BLK, BLK)
        dk_ref[pl.ds(cur0, BLK), :] += dk_blk[0]
        dv_ref[pl.ds(cur0, BLK), :] += dv_blk[0]
        dk_ref[pl.ds(prev0, BLK), :] += dk_blk[1]
        dv_ref[pl.ds(prev0, BLK), :] += dv_blk[1]
        dk_ref[0:BLK, :] += dk_blk[2]
        dv_ref[0:BLK, :] += dv_blk[2]

        @pl.when(i == NBLK - 1)
        def _():
            bc = bkt_ref[0]
            bp = bkt_ref[1]
            lane = lax.broadcasted_iota(jnp.int32, (1, BLK), 1)

            def per_bucket(b, carry):
                row = jnp.zeros((1, BLK), F32)
                for h in range(N_HEADS):
                    val = (jnp.sum(jnp.where(bc == b, acc_cur[h], 0.0), keepdims=True)
                           + jnp.sum(jnp.where(bp == b, acc_prev[h], 0.0), keepdims=True))
                    row = jnp.where(lane == h, val, row)
                dbias_ref[pl.ds(b, 1), :] = row
                return carry

            lax.fori_loop(0, N_BUCKETS, per_bucket, 0)
            far = jnp.zeros((1, BLK), F32)
            dsr = jnp.zeros((1, BLK), F32)
            for h in range(N_HEADS):
                far = jnp.where(lane == h, jnp.sum(acc_far[h], keepdims=True), far)
                dsr = jnp.where(lane == h, jnp.sum(dsk[h], keepdims=True), dsr)
            dbias_ref[N_BUCKETS - 1:N_BUCKETS, :] += far
            dsink_ref[...] = dsr

    smem = pl.BlockSpec(memory_space=pltpu.SMEM)
    kv = lambda col, which: pl.BlockSpec(
        (BLK, BLK), {0: lambda i: (i, col), 1: lambda i: (jnp.maximum(i - 1, 0), col), 2: lambda i: (0, col)}[which])
    blk512 = lambda col: pl.BlockSpec((BLK, 512), lambda i: (i, col))
    full = lambda r, c: pl.BlockSpec((r, c), lambda i: (0, 0))
    acc = pltpu.VMEM((N_HEADS, BLK, BLK), F32)
    return pl.pallas_call(
        body, grid=(NBLK,),
        in_specs=[smem, smem, pl.BlockSpec((2, BLK, BLK), lambda i: (0, 0, 0)), blk512(QA),
                  kv(KA, 0), kv(KA, 1), kv(KA, 2), kv(VA, 0), kv(VA, 1), kv(VA, 2),
                  blk512(0), blk512(0), pl.BlockSpec((N_HEADS, BLK, 1), lambda i: (0, i, 0))],
        out_specs=[blk512(0), full(LP, BLK), full(LP, BLK), full(N_BUCKETS, BLK), full(1, BLK)],
        out_shape=[SDS((LP, 512), BF16), SDS((LP, BLK), F32), SDS((LP, BLK), F32),
                   SDS((N_BUCKETS, BLK), F32), SDS((1, BLK), F32)],
        scratch_shapes=[acc, acc, acc, acc, acc, pltpu.VMEM((N_HEADS, BLK, 1), F32)],
        compiler_params=_cparams(("arbitrary",)), name="swa_bwd",
    )(rel_bias, sinks, bkt, proj, proj, proj, proj, proj, proj, proj, o_a, dmix, lse)


def _local_step(x, tgt, meta, rel_bias, g_pre_mix, g_post_mix, g_pre_ffn, g_post_ffn, b_forget, sinks,
                w_in_b, w_out_b, w_gu_b, w_dn_b):
    bkt = jnp.asarray(_bucket_tables())
    h0 = jnp.concatenate([jnp.zeros((PAD_ROWS, D_MODEL), F32), meta, x], axis=0)
    tgt_p = jnp.concatenate([jnp.zeros((ROW0, D_MODEL), F32), tgt], axis=0)
    b_p = jnp.pad(b_forget, ((0, 0), (0, BLK - N_HEADS)))

    hn1, proj, f = _pre_mix(h0, g_pre_mix, w_in_b)
    o_a, lse_a = _swa_fwd(proj, rel_bias, sinks, bkt)
    cum = _forget_cumsum(f, b_p)
    ck_t = cum[:, :N_HEADS].T.reshape(N_HEADS, 1, LP)
    o_b, lse_b = _fox_fwd(proj, ck_t)
    a, h1, hn2 = _attn_out(o_a, o_b, w_out_b, h0, g_post_mix, g_pre_ffn)
    g, u, act = _ffn_up(hn2, w_gu_b)
    dff, dy, loss_blk, dg_post_ffn = _ffn_down_loss(act, w_dn_b, h1, tgt_p, g_post_ffn)

    dw_dn = _mm_tn(act, dff, FF_T, "dw_down")
    dg, du = _ffn_down_bwd(dff, w_dn_b, g, u)
    dw_gu = _dw_gate_up(hn2, dg, du)
    dh1, da, dg_pre_ffn, dg_post_mix = _ffn_up_bwd(dg, du, w_gu_b, h1, a, dy, g_pre_ffn, g_post_mix)
    dw_out = jnp.concatenate([_mm_tn(o_a, da, 512, "dw_out_a"), _mm_tn(o_b, da, 512, "dw_out_b")], axis=0)
    dmix = _attn_out_bwd(da, w_out_b)
    dq_b, dk_b, dv_b, dck, dcq = _fox_bwd(proj, o_b, dmix, lse_b, ck_t)
    dq_a, dk_a, dv_a, dbias, dsink = _swa_bwd(proj, o_a, dmix, lse_a, rel_bias, sinks, bkt)
    dcum = dcq - jnp.pad(dck.reshape(N_HEADS, LP).T, ((0, 0), (0, BLK - N_HEADS)))
    df, db = _forget_cumsum_bwd(dcum, f, b_p)
    dproj, dh0, dg_pre_mix = _pre_mix_bwd(dq_a, dq_b, dk_b, dv_b, dk_a, dv_a, df, w_in_b, h0, dh1, g_pre_mix)
    dw_in = _mm_tn(hn1, dproj, 512, "dw_in")

    return dict(loss=loss_blk[0, 0], grad_x=dh0[ROW0:], meta=dh0[PAD_ROWS:ROW0],
                rel_bias=dbias[:, :N_HEADS], ln_pre_mix=dg_pre_mix, ln_post_mix=dg_post_mix,
                ln_pre_ffn=dg_pre_ffn, ln_post_ffn=dg_post_ffn, b_forget=db[:, :N_HEADS],
                sinks=dsink[:, :N_HEADS], w_in=dw_in, w_out=dw_out, w_gate_up=dw_gu, w_down=dw_dn)


HBM_SPEC = pl.BlockSpec(memory_space=pltpu.HBM)
N_SMALL = 24
LOSS_ROW = 6


def _place():
    x, y, c = lax.axis_index("x"), lax.axis_index("y"), lax.axis_index("c")
    return x, y, c, [(1 - x, y), (x, 1 - y), (1 - x, 1 - y)]


def _gather_over_chips(shards):
    n = len(shards)

    def body(*refs):
        ins, outs = refs[:n], refs[n:2 * n]
        send_sems, recv_sems = refs[2 * n:]
        x, y, c, others = _place()
        chip = 2 * x + y
        sibling = (x, y, 1 - c)

        def rc(a, k, src, dst, to):
            return pltpu.make_async_remote_copy(src_ref=src, dst_ref=dst, send_sem=send_sems.at[6 * a + k],
                                                recv_sem=recv_sems.at[6 * a + k], device_id=to, device_id_type=MESH)

        first = [rc(a, j, ins[a].at[c], outs[a].at[chip, c], (ox, oy, c))
                 for a in range(n) for j, (ox, oy) in enumerate(others)]
        for cp in first:
            cp.start()
        passed = []
        for a in range(n):
            for j, (ox, oy) in enumerate(others):
                landed = outs[a].at[2 * ox + oy, c]
                rc(a, j, landed, landed, sibling).wait_recv()
                cp = rc(a, 3 + j, landed, landed, sibling)
                cp.start()
                passed.append(cp)
        for a in range(n):
            for j, (ox, oy) in enumerate(others):
                other_half = outs[a].at[2 * ox + oy, 1 - c]
                rc(a, 3 + j, other_half, other_half, sibling).wait_recv()
        for cp in first + passed:
            cp.wait_send()

    return pl.pallas_call(
        body, in_specs=[HBM_SPEC] * n, out_specs=[HBM_SPEC] * n,
        out_shape=[SDS((4,) + s.shape, s.dtype) for s in shards],
        scratch_shapes=[pltpu.SemaphoreType.DMA((6 * n,)), pltpu.SemaphoreType.DMA((6 * n,))],
        name="gather_weights")(*shards)


def _swap_halves(grads):
    n = len(grads)

    def body(*refs):
        ins, outs = refs[:n], refs[n:2 * n]
        send_sems, recv_sems = refs[2 * n:]
        x, y, c, _ = _place()
        copies = [pltpu.make_async_remote_copy(
            src_ref=ins[a].at[s, 1 - c], dst_ref=outs[a].at[s], send_sem=send_sems.at[4 * a + s],
            recv_sem=recv_sems.at[4 * a + s], device_id=(x, y, 1 - c), device_id_type=MESH)
            for a in range(n) for s in range(4)]
        for cp in copies:
            cp.start()
        for cp in copies:
            cp.wait()

    return pl.pallas_call(
        body, in_specs=[HBM_SPEC] * n, out_specs=[HBM_SPEC] * n,
        out_shape=[SDS((4,) + g.shape[2:], g.dtype) for g in grads],
        scratch_shapes=[pltpu.SemaphoreType.DMA((4 * n,)), pltpu.SemaphoreType.DMA((4 * n,))],
        name="swap_halves")(*grads)


def _pair_sum(g, got, c_arr, name):
    rh, cc = got.shape[1:]

    def body(c_ref, g_ref, p_ref, o_ref):
        o_ref[0] = (g_ref[0, 0] + p_ref[0]).astype(BF16)

    grid_spec = pltpu.PrefetchScalarGridSpec(
        num_scalar_prefetch=1, grid=(4,),
        in_specs=[pl.BlockSpec((1, 1, rh, cc), lambda s, c_ref: (s, c_ref[0], 0, 0)),
                  pl.BlockSpec((1, rh, cc), lambda s, c_ref: (s, 0, 0))],
        out_specs=pl.BlockSpec((1, rh, cc), lambda s, c_ref: (s, 0, 0)))
    return pl.pallas_call(body, grid_spec=grid_spec, out_shape=SDS((4, rh, cc), BF16),
                          compiler_params=_cparams(("parallel",)), name=name)(c_arr, g, got)


def _exchange_over_chips(parts, small):
    n = len(parts)

    def body(*refs):
        ins, small_ref = refs[:n], refs[n]
        outs, small_out = refs[n + 1:2 * n + 1], refs[2 * n + 1]
        send_sems, recv_sems, local_sem, ssend, srecv = refs[2 * n + 2:]
        x, y, c, others = _place()
        me = 4 * x + 2 * y + c
        own = pltpu.make_async_copy(small_ref, small_out.at[me], local_sem)
        own.start()
        peers = [(x, y, 1 - c)] + [(ox, oy, c) for ox, oy in others] + [(ox, oy, 1 - c) for ox, oy in others]
        small_copies = [pltpu.make_async_remote_copy(
            src_ref=small_ref, dst_ref=small_out.at[me], send_sem=ssend.at[k], recv_sem=srecv.at[k],
            device_id=peer, device_id_type=MESH) for k, peer in enumerate(peers)]
        for cp in small_copies:
            cp.start()
        copies = [pltpu.make_async_remote_copy(
            src_ref=ins[a].at[2 * ox + oy], dst_ref=outs[a].at[j], send_sem=send_sems.at[3 * a + j],
            recv_sem=recv_sems.at[3 * a + j], device_id=(ox, oy, c), device_id_type=MESH)
            for a in range(n) for j, (ox, oy) in enumerate(others)]
        for cp in copies:
            cp.start()
        for cp in small_copies + copies:
            cp.wait()
        own.wait()

    return pl.pallas_call(
        body, in_specs=[HBM_SPEC] * (n + 1), out_specs=[HBM_SPEC] * (n + 1),
        out_shape=[SDS((3,) + p.shape[1:], p.dtype) for p in parts] + [SDS((8,) + small.shape, small.dtype)],
        scratch_shapes=[pltpu.SemaphoreType.DMA((3 * n,)), pltpu.SemaphoreType.DMA((3 * n,)),
                        pltpu.SemaphoreType.DMA, pltpu.SemaphoreType.DMA((7,)), pltpu.SemaphoreType.DMA((7,))],
        name="exchange_grads")(*parts, small)


def _chip_sum(parts, landed, chip_arr, name):
    rh, cc = landed.shape[1:]
    tr = rh // 2

    def body(chip_ref, own_ref, p_ref, o_ref):
        o_ref[...] = ((own_ref[0].astype(F32) + p_ref[0].astype(F32)) + p_ref[1].astype(F32)) + p_ref[2].astype(F32)

    grid_spec = pltpu.PrefetchScalarGridSpec(
        num_scalar_prefetch=1, grid=(2,),
        in_specs=[pl.BlockSpec((1, tr, cc), lambda i, chip_ref: (chip_ref[0], i, 0)),
                  pl.BlockSpec((3, tr, cc), lambda i, chip_ref: (0, i, 0))],
        out_specs=pl.BlockSpec((tr, cc), lambda i, chip_ref: (i, 0)))
    return pl.pallas_call(body, grid_spec=grid_spec, out_shape=SDS((rh, cc), F32),
                          compiler_params=_cparams(("parallel",)), name=name)(chip_arr, parts, landed)


def _device_sum(p):
    def body(p_ref, o_ref):
        acc = p_ref[0]
        for k in range(1, 8):
            acc = acc + p_ref[k]
        o_ref[...] = acc

    return pl.pallas_call(body, out_shape=SDS(p.shape[1:], F32), name="small_sum")(p)


def _join_halves(halves):
    n = len(halves)

    def body(*refs):
        ins, outs = refs[:n], refs[n:2 * n]
        send_sems, recv_sems = refs[2 * n:]
        x, y, c, _ = _place()
        copies = [pltpu.make_async_remote_copy(
            src_ref=ins[a], dst_ref=outs[a], send_sem=send_sems.at[a], recv_sem=recv_sems.at[a],
            device_id=(x, y, 1 - c), device_id_type=MESH) for a in range(n)]
        for cp in copies:
            cp.start()
        for cp in copies:
            cp.wait()

    return pl.pallas_call(
        body, in_specs=[HBM_SPEC] * n, out_specs=[HBM_SPEC] * n,
        out_shape=[SDS(h.shape, h.dtype) for h in halves],
        scratch_shapes=[pltpu.SemaphoreType.DMA((n,)), pltpu.SemaphoreType.DMA((n,))],
        name="join_halves")(*halves)


def _adamw(w, g, m, v, name):
    rows, cols = w.shape
    tr = rows if rows <= 352 else (256 if rows % 256 == 0 else 352)

    def body(w_ref, g_ref, m_ref, v_ref, d_ref, nm_ref, nv_ref):
        gg = g_ref[...]
        nm = ADAM_B1 * m_ref[...] + (1.0 - ADAM_B1) * gg
        nv = ADAM_B2 * v_ref[...] + (1.0 - ADAM_B2) * (gg * gg)
        nm_ref[...] = nm
        nv_ref[...] = nv
        m_hat = nm / (1.0 - ADAM_B1 ** ADAM_STEP)
        v_hat = nv / (1.0 - ADAM_B2 ** ADAM_STEP)
        d_ref[...] = -ADAM_LR * (m_hat / (jnp.sqrt(v_hat) + ADAM_EPS) + ADAM_WD * w_ref[...])

    blk = pl.BlockSpec((tr, cols), lambda i: (i, 0))
    return pl.pallas_call(
        body, grid=(rows // tr,), in_specs=[blk] * 4, out_specs=[blk] * 3,
        out_shape=[SDS((rows, cols), F32)] * 3,
        compiler_params=_cparams(("parallel",)), name=name)(w, g, m, v)


def _pack_small(pre_mix, post_mix, pre_ffn, post_ffn, rel_bias, b_forget, sinks):
    def at(row, v):
        return jnp.pad(v, ((row, 7 - row), (0, D_MODEL - v.shape[1])))
    return (at(0, pre_mix) + at(1, post_mix) + at(2, pre_ffn) + at(3, post_ffn)
            + at(4, rel_bias.reshape(1, N_BUCKETS * N_HEADS)) + at(5, jnp.concatenate([b_forget, sinks], axis=1)))


def _unpack_small(p):
    return dict(ln_pre_mix=p[0:1], ln_post_mix=p[1:2], ln_pre_ffn=p[2:3], ln_post_ffn=p[3:4],
                rel_bias=p[4, :N_BUCKETS * N_HEADS].reshape(N_BUCKETS, N_HEADS),
                b_forget=p[5:6, 0:N_HEADS], sinks=p[5:6, N_HEADS:2 * N_HEADS])


WEIGHTS = ("meta_tokens", "rel_bias", "ln_pre_mix", "ln_post_mix", "ln_pre_ffn", "ln_post_ffn",
           "w_in", "b_forget", "sinks", "w_out", "w_gate_up", "w_down")


def kernel(x, meta_tokens, rel_bias, ln_pre_mix, ln_post_mix, ln_pre_ffn, ln_post_ffn, w_in, b_forget, sinks, w_out, w_gate_up, w_down, loss_target, m_meta_tokens, m_rel_bias, m_ln_pre_mix, m_ln_post_mix, m_ln_pre_ffn, m_ln_post_ffn, m_w_in, m_b_forget, m_sinks, m_w_out, m_w_gate_up, m_w_down, v_meta_tokens, v_rel_bias, v_ln_pre_mix, v_ln_post_mix, v_ln_pre_ffn, v_ln_post_ffn, v_w_in, v_b_forget, v_sinks, v_w_out, v_w_gate_up, v_w_down):
    xi, yi, ci = lax.axis_index("x"), lax.axis_index("y"), lax.axis_index("c")
    chip = 2 * xi + yi
    c_arr = jnp.reshape(ci, (1,)).astype(jnp.int32)

    def halves(w, dtype):
        return w.astype(dtype).reshape(2, w.shape[0] // 2, w.shape[1])

    shards = [halves(w_in[0], BF16), halves(w_out[0], BF16), halves(w_gate_up[0], BF16), halves(w_down[0], BF16),
              halves(meta_tokens, F32)]
    gw_in, gw_out, gw_gu, gw_dn, g_meta = [
        lax.dynamic_update_slice(got, own[None], (chip, 0, 0, 0))
        for got, own in zip(_gather_over_chips(shards), shards)]
    w_in_all = gw_in.reshape(4, D_MODEL, D_PROJ // 4).transpose(1, 0, 2).reshape(D_MODEL, D_PROJ)
    w_in_b = jnp.concatenate(
        [w_in_all[:, 0:512], w_in_all[:, 768:1280], w_in_all[:, 1280:1792], w_in_all[:, 1792:2304],
         w_in_all[:, 512:640], w_in_all[:, 640:768], w_in_all[:, 2304:2312],
         jnp.zeros((D_MODEL, D_PROJ_P - D_PROJ), BF16)], axis=1)
    meta_all = g_meta.reshape(4, N_META, D_MODEL // 4).transpose(1, 0, 2).reshape(N_META, D_MODEL)

    loc = _local_step(x[0], loss_target[0], meta_all, rel_bias, ln_pre_mix, ln_post_mix, ln_pre_ffn, ln_post_ffn,
                      b_forget, sinks, w_in_b, gw_out.reshape(D_MODEL, D_MODEL),
                      gw_gu.reshape(4, D_MODEL, FF_T), gw_dn.reshape(D_FF, D_MODEL))

    n = loc["w_in"]
    dw_in = jnp.concatenate([n[:, 0:512], n[:, 2048:2176], n[:, 2176:2304], n[:, 512:1024], n[:, 1024:1536],
                             n[:, 1536:2048], n[:, 2304:2312]], axis=1)
    dw_in = dw_in.reshape(D_MODEL, 4, D_PROJ // 4).transpose(1, 0, 2)
    grads = [dw_in.reshape(4, 2, 512, D_PROJ // 4), loc["w_out"].reshape(4, 2, 128, D_MODEL),
             loc["w_gate_up"].reshape(4, 2, 512, FF_T), loc["w_down"].reshape(4, 2, 352, D_MODEL)]
    small = jnp.concatenate(
        [_pack_small(loc["ln_pre_mix"], loc["ln_post_mix"], loc["ln_pre_ffn"], loc["ln_post_ffn"],
                     loc["rel_bias"], loc["b_forget"], loc["sinks"])
         + jnp.pad(loc["loss"].reshape(1, 1), ((LOSS_ROW, 7 - LOSS_ROW), (0, D_MODEL - 1))), loc["meta"]], axis=0)

    got = _swap_halves(grads)
    parts = [_pair_sum(g, p, c_arr, "pair_sum_%d" % a) for a, (g, p) in enumerate(zip(grads, got))]
    *landed, small_all = _exchange_over_chips(parts, small)
    chip_arr = jnp.reshape(chip, (1,)).astype(jnp.int32)
    mine = [_chip_sum(p, l, chip_arr, "chip_sum_%d" % a) for a, (p, l) in enumerate(zip(parts, landed))]
    small_sum = _device_sum(small_all)
    theirs = _join_halves(mine)
    full = [jnp.where(ci == 0, jnp.concatenate([m, t], axis=0), jnp.concatenate([t, m], axis=0))
            for m, t in zip(mine, theirs)]
    g_w_in, g_w_out, g_w_gu, g_w_dn = full
    g_meta_tokens = lax.dynamic_slice(small_sum[8:N_SMALL], (0, chip * (D_MODEL // 4)), (N_META, D_MODEL // 4))
    g_small = small_sum[0:8]

    grad = _unpack_small(g_small)
    grad.update(meta_tokens=g_meta_tokens, w_in=g_w_in[None], w_out=g_w_out[None], w_gate_up=g_w_gu[None],
                w_down=g_w_dn[None])

    delta, new_m, new_v = {}, {}, {}
    big = dict(w_in=(w_in, m_w_in, v_w_in, g_w_in), w_out=(w_out, m_w_out, v_w_out, g_w_out),
               w_gate_up=(w_gate_up, m_w_gate_up, v_w_gate_up, g_w_gu), w_down=(w_down, m_w_down, v_w_down, g_w_dn))
    for name, (w, m, v, g) in big.items():
        d, nm, nv = _adamw(w[0], g, m[0], v[0], "adamw_" + name)
        delta[name], new_m[name], new_v[name] = d[None], nm[None], nv[None]
    delta["meta_tokens"], new_m["meta_tokens"], new_v["meta_tokens"] = _adamw(
        meta_tokens, g_meta_tokens, m_meta_tokens, v_meta_tokens, "adamw_meta")
    d, nm, nv = _adamw(
        _pack_small(ln_pre_mix, ln_post_mix, ln_pre_ffn, ln_post_ffn, rel_bias, b_forget, sinks), g_small,
        _pack_small(m_ln_pre_mix, m_ln_post_mix, m_ln_pre_ffn, m_ln_post_ffn, m_rel_bias, m_b_forget, m_sinks),
        _pack_small(v_ln_pre_mix, v_ln_post_mix, v_ln_pre_ffn, v_ln_post_ffn, v_rel_bias, v_b_forget, v_sinks),
        "adamw_small")
    delta.update(_unpack_small(d))
    new_m.update(_unpack_small(nm))
    new_v.update(_unpack_small(nv))

    loss = small_sum[LOSS_ROW, 0]
    return (loss,loc["grad_x"][None], *[grad[k] for k in WEIGHTS], *[delta[k] for k in WEIGHTS],
            *[new_m[k] for k in WEIGHTS], *[new_v[k] for k in WEIGHTS])
```

```python
import math

import numpy as np
import jax
import jax.numpy as jnp
from jax import lax
from jax.experimental import pallas as pl
from jax.experimental.pallas import tpu as pltpu

F32 = jnp.float32
BF16 = jnp.bfloat16
MESH = pl.DeviceIdType.MESH
SDS = jax.ShapeDtypeStruct

D_MODEL = 1024
SEQ = 4096
N_META = 16
N_HEADS = 8
HALF = 64
D_FF = 2816
N_BUCKETS = 32
EPS = 1e-6
NEG = -1e30
SCALE = 0.125
PAD_ROWS = 112
ROW0 = PAD_ROWS + N_META
LP = ROW0 + SEQ
BLK = 128
NBLK = LP // BLK
TM = 384
NT = LP // TM
D_PROJ = 2312
D_PROJ_P = 2432
D_QKV = 2304
FF_T = 1408
VMEM_LIMIT = 56 * 1024 * 1024

ADAM_LR = 0.001
ADAM_B1 = 0.9
ADAM_B2 = 0.999
ADAM_EPS = 1e-08
ADAM_WD = 0.01
ADAM_STEP = 10

QA, QB, KB, VB = 0, 1, 2, 3
KA, VA = 16, 17

NT_DIMS = (((1,), (1,)), ((), ()))
TN_DIMS = (((0,), (0,)), ((), ()))


def _cparams(sem):
    return pltpu.CompilerParams(dimension_semantics=sem, vmem_limit_bytes=VMEM_LIMIT)


def _t5_bucket_np(d):
    n = np.maximum(d, 0).astype(np.int32)
    nf = np.maximum(n, 1).astype(np.float32)
    large = 16 + (np.log(nf / np.float32(16)) / np.float32(math.log(8.0)) * np.float32(16)).astype(np.int32)
    large = np.minimum(large, N_BUCKETS - 1)
    return np.where(n < 16, n, large).astype(np.int32)


def _bucket_tables():
    qi = np.arange(BLK)[:, None]
    ki = np.arange(BLK)[None, :]
    return np.stack([_t5_bucket_np(qi - ki), _t5_bucket_np(qi - ki + BLK)])


def _rms(x):
    return lax.rsqrt(jnp.mean(x * x, axis=-1, keepdims=True) + EPS)


def _rms_bwd(n, r, gdy):
    return r * (gdy - n * jnp.mean(n * gdy, axis=-1, keepdims=True))


def _pre_mix(h0, gain, w_in_b):
    def body(h_ref, g_ref, w_ref, hn_ref, proj_ref, f_ref):
        x = h_ref[...]
        hn = (x * _rms(x) * g_ref[...]).astype(BF16)
        hn_ref[...] = hn
        p = jnp.dot(hn, w_ref[...], preferred_element_type=F32)
        proj_ref[...] = p[:, :D_QKV].astype(BF16)
        f_ref[...] = p[:, D_QKV:]

    return pl.pallas_call(
        body, grid=(NT,),
        in_specs=[pl.BlockSpec((TM, D_MODEL), lambda i: (i, 0)),
                  pl.BlockSpec((1, D_MODEL), lambda i: (0, 0)),
                  pl.BlockSpec((D_MODEL, D_PROJ_P), lambda i: (0, 0))],
        out_specs=[pl.BlockSpec((TM, D_MODEL), lambda i: (i, 0)),
                   pl.BlockSpec((TM, D_QKV), lambda i: (i, 0)),
                   pl.BlockSpec((TM, BLK), lambda i: (i, 0))],
        out_shape=[SDS((LP, D_MODEL), BF16), SDS((LP, D_QKV), BF16), SDS((LP, BLK), F32)],
        compiler_params=_cparams(("parallel",)), name="pre_mix")(h0, gain, w_in_b)


def _attn_out(o_a, o_b, w_out_b, h0, g_post, g_pre_ffn):
    def body(oa_ref, ob_ref, w_ref, h0_ref, gp_ref, gf_ref, a_ref, h1_ref, hn2_ref):
        a = (jnp.dot(oa_ref[...], w_ref[0:512, :], preferred_element_type=F32)
             + jnp.dot(ob_ref[...], w_ref[512:1024, :], preferred_element_type=F32))
        a_ref[...] = a
        h1 = h0_ref[...] + a * _rms(a) * gp_ref[...]
        h1_ref[...] = h1
        hn2_ref[...] = (h1 * _rms(h1) * gf_ref[...]).astype(BF16)

    row = lambda w: pl.BlockSpec((TM, w), lambda i: (i, 0))
    vec = pl.BlockSpec((1, D_MODEL), lambda i: (0, 0))
    return pl.pallas_call(
        body, grid=(NT,),
        in_specs=[row(512), row(512), pl.BlockSpec((D_MODEL, D_MODEL), lambda i: (0, 0)), row(D_MODEL), vec, vec],
        out_specs=[row(D_MODEL), row(D_MODEL), row(D_MODEL)],
        out_shape=[SDS((LP, D_MODEL), F32), SDS((LP, D_MODEL), F32), SDS((LP, D_MODEL), BF16)],
        compiler_params=_cparams(("parallel",)), name="attn_out")(o_a, o_b, w_out_b, h0, g_post, g_pre_ffn)


def _ffn_up(hn2, w_gu_b):
    def body(x_ref, wg_ref, wu_ref, g_ref, u_ref, act_ref):
        x = x_ref[...]
        g = jnp.dot(x, wg_ref[0], preferred_element_type=F32)
        u = jnp.dot(x, wu_ref[0], preferred_element_type=F32)
        g_ref[...] = g.astype(BF16)
        u_ref[...] = u.astype(BF16)
        act_ref[...] = (g * (1.0 / (1.0 + jnp.exp(-g))) * u).astype(BF16)

    out = pl.BlockSpec((TM, FF_T), lambda j, i: (i, j))
    return pl.pallas_call(
        body, grid=(2, NT),
        in_specs=[pl.BlockSpec((TM, D_MODEL), lambda j, i: (i, 0)),
                  pl.BlockSpec((1, D_MODEL, FF_T), lambda j, i: (j, 0, 0)),
                  pl.BlockSpec((1, D_MODEL, FF_T), lambda j, i: (j + 2, 0, 0))],
        out_specs=[out, out, out],
        out_shape=[SDS((LP, D_FF), BF16)] * 3,
        compiler_params=_cparams(("parallel", "parallel")), name="ffn_up")(hn2, w_gu_b, w_gu_b)


def _ffn_down_loss(act, w_dn_b, h1, tgt_p, g_post_ffn):
    def body(act_ref, w_ref, h1_ref, t_ref, g_ref, dff_ref, dy_ref, loss_ref, dg_ref):
        i = pl.program_id(0)

        @pl.when(i == 0)
        def _():
            loss_ref[...] = jnp.zeros_like(loss_ref)
            dg_ref[...] = jnp.zeros_like(dg_ref)

        ff = jnp.dot(act_ref[...], w_ref[...], preferred_element_type=F32)
        r = _rms(ff)
        n = ff * r
        g = g_ref[...]
        y = h1_ref[...] + n * g
        rows = i * TM + lax.broadcasted_iota(jnp.int32, (TM, D_MODEL), 0)
        diff = jnp.where(rows >= ROW0, y - t_ref[...], 0.0)
        loss_ref[...] += 0.5 * jnp.sum(diff * diff) / D_MODEL
        dy = diff / D_MODEL
        dy_ref[...] = dy
        dg_ref[...] += jnp.sum(dy * n, axis=0, keepdims=True)
        dff_ref[...] = _rms_bwd(n, r, g * dy).astype(BF16)

    row = pl.BlockSpec((TM, D_MODEL), lambda i: (i, 0))
    return pl.pallas_call(
        body, grid=(NT,),
        in_specs=[pl.BlockSpec((TM, D_FF), lambda i: (i, 0)), pl.BlockSpec((D_FF, D_MODEL), lambda i: (0, 0)),
                  row, row, pl.BlockSpec((1, D_MODEL), lambda i: (0, 0))],
        out_specs=[row, row, pl.BlockSpec((8, BLK), lambda i: (0, 0)), pl.BlockSpec((1, D_MODEL), lambda i: (0, 0))],
        out_shape=[SDS((LP, D_MODEL), BF16), SDS((LP, D_MODEL), F32), SDS((8, BLK), F32), SDS((1, D_MODEL), F32)],
        compiler_params=_cparams(("arbitrary",)), name="ffn_down_loss")(act, w_dn_b, h1, tgt_p, g_post_ffn)


def _ffn_down_bwd(dff, w_dn_b, g, u):
    def body(d_ref, w_ref, g_ref, u_ref, dg_ref, du_ref):
        dact = lax.dot_general(d_ref[...], w_ref[...], NT_DIMS, preferred_element_type=F32)
        gg = g_ref[...].astype(F32)
        sig = 1.0 / (1.0 + jnp.exp(-gg))
        dg_ref[...] = (dact * u_ref[...].astype(F32) * sig * (1.0 + gg * (1.0 - sig))).astype(BF16)
        du_ref[...] = (dact * gg * sig).astype(BF16)

    blk = pl.BlockSpec((TM, FF_T), lambda j, i: (i, j))
    return pl.pallas_call(
        body, grid=(2, NT),
        in_specs=[pl.BlockSpec((TM, D_MODEL), lambda j, i: (i, 0)),
                  pl.BlockSpec((FF_T, D_MODEL), lambda j, i: (j, 0)), blk, blk],
        out_specs=[blk, blk],
        out_shape=[SDS((LP, D_FF), BF16)] * 2,
        compiler_params=_cparams(("parallel", "parallel")), name="ffn_down_bwd")(dff, w_dn_b, g, u)


def _ffn_up_bwd(dg, du, w_gu_b, h1, a, dy, g_pre_ffn, g_post_mix):
    def body(dg_ref, du_ref, w_ref, h1_ref, a_ref, dy_ref, gf_ref, gp_ref,
             dh1_ref, da_ref, dgf_ref, dgp_ref, acc):
        i = pl.program_id(0)
        s = pl.program_id(1)

        @pl.when((i == 0) & (s == 0))
        def _():
            dgf_ref[...] = jnp.zeros_like(dgf_ref)
            dgp_ref[...] = jnp.zeros_like(dgp_ref)

        @pl.when(s == 0)
        def _():
            acc[...] = jnp.zeros_like(acc)

        @pl.when(s < 2)
        def _():
            acc[...] += lax.dot_general(dg_ref[...], w_ref[0], NT_DIMS, preferred_element_type=F32)

        @pl.when(s >= 2)
        def _():
            acc[...] += lax.dot_general(du_ref[...], w_ref[0], NT_DIMS, preferred_element_type=F32)

        @pl.when(s == 3)
        def _():
            dhn2 = acc[...]
            h1 = h1_ref[...]
            r2 = _rms(h1)
            n2 = h1 * r2
            dgf_ref[...] += jnp.sum(dhn2 * n2, axis=0, keepdims=True)
            dh1 = dy_ref[...] + _rms_bwd(n2, r2, gf_ref[...] * dhn2)
            dh1_ref[...] = dh1
            av = a_ref[...]
            ra = _rms(av)
            na = av * ra
            dgp_ref[...] += jnp.sum(dh1 * na, axis=0, keepdims=True)
            da_ref[...] = _rms_bwd(na, ra, gp_ref[...] * dh1).astype(BF16)

    row = pl.BlockSpec((TM, D_MODEL), lambda i, s: (i, 0))
    vec = pl.BlockSpec((1, D_MODEL), lambda i, s: (0, 0))
    return pl.pallas_call(
        body, grid=(NT, 4),
        in_specs=[pl.BlockSpec((TM, FF_T), lambda i, s: (i, jnp.minimum(s, 1))),
                  pl.BlockSpec((TM, FF_T), lambda i, s: (i, jnp.maximum(s - 2, 0))),
                  pl.BlockSpec((1, D_MODEL, FF_T), lambda i, s: (s, 0, 0)),
                  row, row, row, vec, vec],
        out_specs=[row, row, vec, vec],
        out_shape=[SDS((LP, D_MODEL), F32), SDS((LP, D_MODEL), BF16), SDS((1, D_MODEL), F32), SDS((1, D_MODEL), F32)],
        scratch_shapes=[pltpu.VMEM((TM, D_MODEL), F32)],
        compiler_params=_cparams(("arbitrary", "arbitrary")), name="ffn_up_bwd",
    )(dg, du, w_gu_b, h1, a, dy, g_pre_ffn, g_post_mix)


def _attn_out_bwd(da, w_out_b):
    def body(d_ref, w_ref, o_ref):
        o_ref[...] = lax.dot_general(d_ref[...], w_ref[...], NT_DIMS, preferred_element_type=F32).astype(BF16)

    row = pl.BlockSpec((TM, D_MODEL), lambda i: (i, 0))
    return pl.pallas_call(
        body, grid=(NT,),
        in_specs=[row, pl.BlockSpec((D_MODEL, D_MODEL), lambda i: (0, 0))],
        out_specs=row, out_shape=SDS((LP, D_MODEL), BF16),
        compiler_params=_cparams(("parallel",)), name="attn_out_bwd")(da, w_out_b)


def _pre_mix_bwd(dq_a, dq_b, dk_b, dv_b, dk_a, dv_a, df, w_in_b, h0, dh1, g_pre_mix):
    def body(qa_ref, qb_ref, kb_ref, vb_ref, ka_ref, va_ref, f_ref, w_ref, h0_ref, dh1_ref, g_ref,
             dproj_ref, dh0_ref, dg_ref):
        i = pl.program_id(0)

        @pl.when(i == 0)
        def _():
            dg_ref[...] = jnp.zeros_like(dg_ref)

        dproj = jnp.concatenate(
            [qa_ref[...], (qb_ref[...] * SCALE).astype(BF16), kb_ref[...], vb_ref[...],
             ka_ref[...].astype(BF16), va_ref[...].astype(BF16), f_ref[...].astype(BF16)], axis=1)
        dproj_ref[...] = dproj
        dhn = lax.dot_general(dproj, w_ref[...], NT_DIMS, preferred_element_type=F32)
        x = h0_ref[...]
        r = _rms(x)
        n = x * r
        dg_ref[...] += jnp.sum(dhn * n, axis=0, keepdims=True)
        dh0_ref[...] = dh1_ref[...] + _rms_bwd(n, r, g_ref[...] * dhn)

    row = lambda w: pl.BlockSpec((TM, w), lambda i: (i, 0))
    vec = pl.BlockSpec((1, D_MODEL), lambda i: (0, 0))
    return pl.pallas_call(
        body, grid=(NT,),
        in_specs=[row(512), row(512), row(512), row(512), row(BLK), row(BLK), row(BLK),
                  pl.BlockSpec((D_MODEL, D_PROJ_P), lambda i: (0, 0)), row(D_MODEL), row(D_MODEL), vec],
        out_specs=[row(D_PROJ_P), row(D_MODEL), vec],
        out_shape=[SDS((LP, D_PROJ_P), BF16), SDS((LP, D_MODEL), F32), SDS((1, D_MODEL), F32)],
        compiler_params=_cparams(("arbitrary",)), name="pre_mix_bwd",
    )(dq_a, dq_b, dk_b, dv_b, dk_a, dv_a, df, w_in_b, h0, dh1, g_pre_mix)


def _mm_tn(a, b, tm, name):
    m_total = a.shape[1]
    n = b.shape[1]

    def body(a_ref, b_ref, o_ref):
        @pl.when(pl.program_id(1) == 0)
        def _():
            o_ref[...] = jnp.zeros_like(o_ref)
        o_ref[...] += lax.dot_general(a_ref[...], b_ref[...], TN_DIMS, preferred_element_type=F32)

    return pl.pallas_call(
        body, grid=(m_total // tm, NT),
        in_specs=[pl.BlockSpec((TM, tm), lambda mi, k: (k, mi)),
                  pl.BlockSpec((TM, n), lambda mi, k: (k, 0))],
        out_specs=pl.BlockSpec((tm, n), lambda mi, k: (mi, 0)),
        out_shape=SDS((m_total, n), F32),
        compiler_params=_cparams(("parallel", "arbitrary")), name=name)(a, b)


def _dw_gate_up(hn2, dg, du):
    def body(a_ref, dg_ref, du_ref, o_ref):
        s = pl.program_id(0)

        @pl.when(pl.program_id(1) == 0)
        def _():
            o_ref[...] = jnp.zeros_like(o_ref)

        @pl.when(s < 2)
        def _():
            o_ref[0] += lax.dot_general(a_ref[...], dg_ref[...], TN_DIMS, preferred_element_type=F32)

        @pl.when(s >= 2)
        def _():
            o_ref[0] += lax.dot_general(a_ref[...], du_ref[...], TN_DIMS, preferred_element_type=F32)

    return pl.pallas_call(
        body, grid=(4, NT),
        in_specs=[pl.BlockSpec((TM, D_MODEL), lambda s, k: (k, 0)),
                  pl.BlockSpec((TM, FF_T), lambda s, k: (k, jnp.minimum(s, 1))),
                  pl.BlockSpec((TM, FF_T), lambda s, k: (k, jnp.maximum(s - 2, 0)))],
        out_specs=pl.BlockSpec((1, D_MODEL, FF_T), lambda s, k: (s, 0, 0)),
        out_shape=SDS((4, D_MODEL, FF_T), F32),
        compiler_params=_cparams(("parallel", "arbitrary")), name="dw_gate_up")(hn2, dg, du)


def _split3(x):
    hi = x.astype(BF16)
    r1 = x - hi.astype(F32)
    mid = r1.astype(BF16)
    lo = (r1 - mid.astype(F32)).astype(BF16)
    return hi, mid, lo


def _tri_matmul(tri, x):
    hi, mid, lo = _split3(x)
    dot = lambda t: jnp.dot(tri, t, preferred_element_type=F32)
    return dot(hi) + dot(mid) + dot(lo)


def _forget_cumsum(f, b_forget_p):
    def body(f_ref, b_ref, cum_ref, carry):
        i = pl.program_id(0)

        @pl.when(i == 0)
        def _():
            carry[...] = jnp.zeros_like(carry)

        z = f_ref[...] + b_ref[...]
        ls = jnp.minimum(z, 0.0) - jnp.log(1.0 + jnp.exp(-jnp.abs(z)))
        rows = i * BLK + lax.broadcasted_iota(jnp.int32, (BLK, BLK), 0)
        ls = jnp.where(rows >= PAD_ROWS, ls, 0.0)
        r = lax.broadcasted_iota(jnp.int32, (BLK, BLK), 0)
        c = lax.broadcasted_iota(jnp.int32, (BLK, BLK), 1)
        tri = (c <= r).astype(BF16)
        cum = _tri_matmul(tri, ls) + carry[...]
        cum_ref[...] = cum
        carry[...] = cum[BLK - 1:BLK, :]

    return pl.pallas_call(
        body, grid=(NBLK,),
        in_specs=[pl.BlockSpec((BLK, BLK), lambda i: (i, 0)), pl.BlockSpec((1, BLK), lambda i: (0, 0))],
        out_specs=pl.BlockSpec((BLK, BLK), lambda i: (i, 0)),
        out_shape=SDS((LP, BLK), F32),
        scratch_shapes=[pltpu.VMEM((1, BLK), F32)],
        compiler_params=_cparams(("arbitrary",)), name="forget_cumsum")(f, b_forget_p)


def _forget_cumsum_bwd(dcum, f, b_forget_p):
    def body(d_ref, f_ref, b_ref, df_ref, db_ref, carry):
        i = pl.program_id(0)

        @pl.when(i == 0)
        def _():
            carry[...] = jnp.zeros_like(carry)
            db_ref[...] = jnp.zeros_like(db_ref)

        blk = NBLK - 1 - i
        r = lax.broadcasted_iota(jnp.int32, (BLK, BLK), 0)
        c = lax.broadcasted_iota(jnp.int32, (BLK, BLK), 1)
        tri = (c >= r).astype(BF16)
        d = d_ref[...]
        dls = _tri_matmul(tri, d) + carry[...]
        carry[...] = dls[0:1, :]
        z = f_ref[...] + b_ref[...]
        rows = blk * BLK + r
        df = jnp.where(rows >= PAD_ROWS, dls / (1.0 + jnp.exp(z)), 0.0)
        df_ref[...] = df
        db_ref[...] += jnp.sum(df, axis=0, keepdims=True)

    rev = pl.BlockSpec((BLK, BLK), lambda i: (NBLK - 1 - i, 0))
    vec = pl.BlockSpec((1, BLK), lambda i: (0, 0))
    return pl.pallas_call(
        body, grid=(NBLK,),
        in_specs=[rev, rev, vec],
        out_specs=[rev, vec],
        out_shape=[SDS((LP, BLK), F32), SDS((1, BLK), F32)],
        scratch_shapes=[pltpu.VMEM((1, BLK), F32)],
        compiler_params=_cparams(("arbitrary",)), name="forget_cumsum_bwd")(dcum, f, b_forget_p)


def _lane_half(rows):
    return lax.broadcasted_iota(jnp.int32, (rows, BLK), 1) // HALF


def _fox_valid(qi, kj):
    qrow = qi * TM + lax.broadcasted_iota(jnp.int32, (TM, TM), 0)
    krow = kj * TM + lax.broadcasted_iota(jnp.int32, (TM, TM), 1)
    return (krow <= qrow) & ((krow >= PAD_ROWS) | (qrow < PAD_ROWS))


class _Rider:
    def __init__(self, operands, out_shapes, sem_counts, first, middle, last):
        self.operands, self.out_shapes, self.sem_counts = list(operands), list(out_shapes), list(sem_counts)
        self.first, self.middle, self.last = first, middle, last

    def scratch(self):
        return [pltpu.SemaphoreType.DMA((k,)) for k in self.sem_counts]

    def split(self, refs, n_in, n_out, n_scratch):
        a, b = len(self.operands), len(self.out_shapes)
        ins, mine_in = refs[:n_in], refs[n_in:n_in + a]
        outs, mine_out = refs[n_in + a:n_in + a + n_out], refs[n_in + a + n_out:n_in + a + n_out + b]
        rest = refs[n_in + a + n_out + b:]
        return ins, outs, rest[:n_scratch], (mine_in, mine_out, rest[n_scratch:])

    def at_steps(self, mine, is_first, is_middle, is_last):
        for cond, fn in ((is_first, self.first), (is_middle, self.middle), (is_last, self.last)):
            pl.when(cond)(lambda fn=fn: fn(*mine))


HBM_SPEC = pl.BlockSpec(memory_space=pltpu.HBM)


def _fox_fwd(proj, ck_t, rider):
    def body(*refs):
        (q_ref, k_ref, v_ref, ck_ref), (o_ref, lse_ref), (m_s, l_s, acc_s), mine = rider.split(refs, 4, 2, 3)
        qi = pl.program_id(0)
        kj = pl.program_id(1)
        rider.at_steps(mine, (qi == 0) & (kj == 0), (qi == NT // 2) & (kj == 0), (qi == NT - 1) & (kj == NT - 1))

        @pl.when(kj == 0)
        def _():
            m_s[...] = jnp.full_like(m_s, NEG)
            l_s[...] = jnp.zeros_like(l_s)
            acc_s[...] = jnp.zeros_like(acc_s)

        def tile(masked):
            valid = _fox_valid(qi, kj) if masked else None
            half_k = _lane_half(TM)
            for pp in range(4):
                cols = slice(pp * BLK, (pp + 1) * BLK)
                qs = (q_ref[:, cols].astype(F32) * SCALE).astype(BF16)
                kp = k_ref[:, cols]
                vp = v_ref[:, cols]
                pv = jnp.zeros((TM, BLK), F32)
                alphas = []
                for e in range(2):
                    h = 2 * pp + e
                    ke = jnp.where(half_k == e, kp, jnp.zeros_like(kp))
                    ve = jnp.where(half_k == e, vp, jnp.zeros_like(vp))
                    t = lax.dot_general(qs, ke, NT_DIMS, preferred_element_type=F32) - ck_ref[h]
                    if masked:
                        t = jnp.where(valid, t, NEG)
                    m_prev = m_s[h]
                    m_new = jnp.maximum(m_prev, jnp.max(t, axis=1, keepdims=True))
                    p = jnp.exp(t - m_new)
                    alpha = jnp.exp(m_prev - m_new)
                    l_s[h] = alpha * l_s[h] + jnp.sum(p, axis=1, keepdims=True)
                    m_s[h] = m_new
                    pv = pv + jnp.dot(p.astype(BF16), ve, preferred_element_type=F32)
                    alphas.append(alpha)
                a_lane = jnp.where(half_k == 0, alphas[0], alphas[1])
                acc_s[pp] = acc_s[pp] * a_lane + pv

        @pl.when((kj < qi) & (kj > 0))
        def _():
            tile(False)

        @pl.when((kj <= qi) & ((kj == qi) | (kj == 0)))
        def _():
            tile(True)

        @pl.when(kj == qi)
        def _():
            half_q = _lane_half(TM)
            for pp in range(4):
                inv = jnp.where(half_q == 0, 1.0 / l_s[2 * pp], 1.0 / l_s[2 * pp + 1])
                o_ref[:, pp * BLK:(pp + 1) * BLK] = (acc_s[pp] * inv).astype(BF16)
            for h in range(N_HEADS):
                lse_ref[h] = m_s[h] + jnp.log(l_s[h])

    kv = lambda blk: pl.BlockSpec((TM, 512), lambda qi, kj: (jnp.minimum(kj, qi), blk))
    o_b, lse, *carried = pl.pallas_call(
        body, grid=(NT, NT),
        in_specs=[pl.BlockSpec((TM, 512), lambda qi, kj: (qi, QB)), kv(KB), kv(VB),
                  pl.BlockSpec((N_HEADS, 1, TM), lambda qi, kj: (0, 0, jnp.minimum(kj, qi)))]
        + [HBM_SPEC] * len(rider.operands),
        out_specs=[pl.BlockSpec((TM, 512), lambda qi, kj: (qi, 0)),
                   pl.BlockSpec((N_HEADS, TM, 1), lambda qi, kj: (0, qi, 0))] + [HBM_SPEC] * len(rider.out_shapes),
        out_shape=[SDS((LP, 512), BF16), SDS((N_HEADS, LP, 1), F32)] + rider.out_shapes,
        scratch_shapes=[pltpu.VMEM((N_HEADS, TM, 1), F32), pltpu.VMEM((N_HEADS, TM, 1), F32),
                        pltpu.VMEM((4, TM, BLK), F32)] + rider.scratch(),
        compiler_params=_cparams(("arbitrary", "arbitrary")), name="fox_fwd",
    )(proj, proj, proj, ck_t, *rider.operands)
    return o_b, lse, carried


def _fox_bwd(proj, o_b, dmix, lse, ck_t, rider):
    def body(*refs):
        ((q_ref, k_ref, v_ref, o_ref, do_ref, lse_ref, ck_ref), (dq_ref, dk_ref, dv_ref, dck_ref, dcq_ref),
         (dk_s, dv_s, dck_s), mine) = rider.split(refs, 7, 5, 3)
        kj = pl.program_id(0)
        qi = pl.program_id(1)
        rider.at_steps(mine, (kj == 0) & (qi == 0), (kj == NT // 2) & (qi == 0), (kj == NT - 1) & (qi == NT - 1))

        @pl.when((kj == 0) & (qi == 0))
        def _():
            dq_ref[...] = jnp.zeros_like(dq_ref)
            dcq_ref[...] = jnp.zeros_like(dcq_ref)

        @pl.when(qi == 0)
        def _():
            dk_s[...] = jnp.zeros_like(dk_s)
            dv_s[...] = jnp.zeros_like(dv_s)
            dck_s[...] = jnp.zeros_like(dck_s)

        def tile(masked):
            valid = _fox_valid(qi, kj) if masked else None
            half = _lane_half(TM)
            q0 = pl.multiple_of(qi * TM, TM)
            lane = lax.broadcasted_iota(jnp.int32, (TM, BLK), 1)
            row_sums = jnp.zeros((TM, BLK), F32)
            for pp in range(4):
                cols = slice(pp * BLK, (pp + 1) * BLK)
                qs = (q_ref[:, cols].astype(F32) * SCALE).astype(BF16)
                kp = k_ref[:, cols]
                vp = v_ref[:, cols]
                dop = do_ref[:, cols]
                prod = dop.astype(F32) * o_ref[:, cols].astype(F32)
                d0 = jnp.sum(jnp.where(half == 0, prod, 0.0), axis=1, keepdims=True)
                d1 = jnp.sum(prod, axis=1, keepdims=True) - d0
                dq = jnp.zeros((TM, BLK), F32)
                dks, dvs = [], []
                for e in range(2):
                    h = 2 * pp + e
                    ke = jnp.where(half == e, kp, jnp.zeros_like(kp))
                    ve = jnp.where(half == e, vp, jnp.zeros_like(vp))
                    t = lax.dot_general(qs, ke, NT_DIMS, preferred_element_type=F32) - ck_ref[h] - lse_ref[h]
                    if masked:
                        t = jnp.where(valid, t, NEG)
                    p = jnp.exp(t)
                    dp = lax.dot_general(dop, ve, NT_DIMS, preferred_element_type=F32)
                    ds = p * (dp - (d0 if e == 0 else d1))
                    dck_s[h] += jnp.sum(ds, axis=0, keepdims=True)
                    row_sums = jnp.where(lane == h, jnp.sum(ds, axis=1, keepdims=True), row_sums)
                    ds_b = ds.astype(BF16)
                    dq = dq + jnp.dot(ds_b, ke, preferred_element_type=F32)
                    dks.append(lax.dot_general(ds_b, qs, TN_DIMS, preferred_element_type=F32))
                    dvs.append(lax.dot_general(p.astype(BF16), dop, TN_DIMS, preferred_element_type=F32))
                dq_ref[pl.ds(q0, TM), cols] += dq
                dk_s[pp] += jnp.where(half == 0, dks[0], dks[1])
                dv_s[pp] += jnp.where(half == 0, dvs[0], dvs[1])
            dcq_ref[pl.ds(q0, TM), :] += row_sums

        @pl.when((qi > kj) & (kj > 0))
        def _():
            tile(False)

        @pl.when((qi >= kj) & ((qi == kj) | (kj == 0)))
        def _():
            tile(True)

        @pl.when(qi == NT - 1)
        def _():
            for pp in range(4):
                cols = slice(pp * BLK, (pp + 1) * BLK)
                dk_ref[:, cols] = dk_s[pp].astype(BF16)
                dv_ref[:, cols] = dv_s[pp].astype(BF16)
            dck_ref[...] = dck_s[...]

    qrow = lambda blk: pl.BlockSpec((TM, 512), lambda kj, qi: (jnp.maximum(qi, kj), blk))
    krow = lambda blk: pl.BlockSpec((TM, 512), lambda kj, qi: (kj, blk))
    dq, dk, dv, dck, dcq, *carried = pl.pallas_call(
        body, grid=(NT, NT),
        in_specs=[qrow(QB), krow(KB), krow(VB), qrow(0), qrow(1),
                  pl.BlockSpec((N_HEADS, TM, 1), lambda kj, qi: (0, jnp.maximum(qi, kj), 0)),
                  pl.BlockSpec((N_HEADS, 1, TM), lambda kj, qi: (0, 0, kj))] + [HBM_SPEC] * len(rider.operands),
        out_specs=[pl.BlockSpec((LP, 512), lambda kj, qi: (0, 0)),
                   pl.BlockSpec((TM, 512), lambda kj, qi: (kj, 0)),
                   pl.BlockSpec((TM, 512), lambda kj, qi: (kj, 0)),
                   pl.BlockSpec((N_HEADS, 1, TM), lambda kj, qi: (0, 0, kj)),
                   pl.BlockSpec((LP, BLK), lambda kj, qi: (0, 0))] + [HBM_SPEC] * len(rider.out_shapes),
        out_shape=[SDS((LP, 512), F32), SDS((LP, 512), BF16), SDS((LP, 512), BF16), SDS((N_HEADS, 1, LP), F32),
                   SDS((LP, BLK), F32)] + rider.out_shapes,
        scratch_shapes=[pltpu.VMEM((4, TM, BLK), F32), pltpu.VMEM((4, TM, BLK), F32),
                        pltpu.VMEM((N_HEADS, 1, TM), F32)] + rider.scratch(),
        compiler_params=_cparams(("arbitrary", "arbitrary")), name="fox_bwd",
    )(proj, proj, proj, o_b, dmix, lse, ck_t, *rider.operands)
    return dq, dk, dv, dck, dcq, carried


def _build_bias(tab_ref, bkt_ref, bias_cur, bias_prev):
    bc = bkt_ref[0]
    bp = bkt_ref[1]
    for h in range(N_HEADS):
        def step(b, carry, h=h):
            t = tab_ref[b, h]
            return jnp.where(bc == b, t, carry[0]), jnp.where(bp == b, t, carry[1])
        zero = jnp.zeros((BLK, BLK), F32)
        cur, prev = lax.fori_loop(0, N_BUCKETS, step, (zero, zero))
        bias_cur[h] = cur
        bias_prev[h] = prev


def _kv_variants(ref):
    x = ref[...].astype(F32)
    swapped = pltpu.roll(x, HALF, 1)
    half = _lane_half(BLK)
    out = {}
    for e in range(2):
        for g in range(2):
            src = x if e == g else swapped
            out[(e, g)] = jnp.where(half == e, src, 0.0).astype(BF16)
    return out


def _swa_masks(i):
    qrow = lax.broadcasted_iota(jnp.int32, (BLK, BLK), 0)
    kcol = lax.broadcasted_iota(jnp.int32, (BLK, BLK), 1)
    big = 4 * BLK
    valid_cur = (kcol <= qrow) & (kcol >= jnp.where(i > 0, 0, PAD_ROWS))
    valid_prev = kcol > qrow + jnp.where(i >= 2, 0, big)
    valid_meta = kcol >= jnp.where(i >= 1, PAD_ROWS, big)
    return valid_cur, valid_prev, valid_meta


def _swa_scores(i, h, qs, kvar, bias_cur, bias_prev, tab_ref, masks):
    e, g = h % 2, h // 4
    first = jnp.full((BLK, BLK), i, jnp.int32) == 1
    biases = (bias_cur[h], bias_prev[h], jnp.where(first, bias_prev[h], tab_ref[N_BUCKETS - 1, h]))
    out = []
    for x in range(3):
        s = lax.dot_general(qs, kvar[x][(e, g)], NT_DIMS, preferred_element_type=F32) + biases[x]
        out.append(jnp.where(masks[x], s, NEG))
    return out


def _swa_fwd(proj, rel_bias, sinks, bkt):
    def body(tab_ref, sink_ref, bkt_ref, q_ref, kc_ref, kp_ref, km_ref, vc_ref, vp_ref, vm_ref,
             o_ref, lse_ref, bias_cur, bias_prev):
        i = pl.program_id(0)

        @pl.when(i == 0)
        def _():
            _build_bias(tab_ref, bkt_ref, bias_cur, bias_prev)

        masks = _swa_masks(i)
        kvar = [_kv_variants(r) for r in (kc_ref, kp_ref, km_ref)]
        vvar = [_kv_variants(r) for r in (vc_ref, vp_ref, vm_ref)]
        for pp in range(4):
            cols = slice(pp * BLK, (pp + 1) * BLK)
            qs = (q_ref[:, cols].astype(F32) * SCALE).astype(BF16)
            o_pair = jnp.zeros((BLK, BLK), F32)
            for e in range(2):
                h = 2 * pp + e
                g = h // 4
                s = _swa_scores(i, h, qs, kvar, bias_cur, bias_prev, tab_ref, masks)
                sink = sink_ref[0, h]
                m = jnp.maximum(jnp.maximum(jnp.max(s[0], axis=1, keepdims=True), jnp.max(s[1], axis=1, keepdims=True)),
                                jnp.maximum(jnp.max(s[2], axis=1, keepdims=True), sink))
                p = [jnp.exp(sx - m) for sx in s]
                denom = (jnp.sum(p[0], axis=1, keepdims=True) + jnp.sum(p[1], axis=1, keepdims=True)
                         + jnp.sum(p[2], axis=1, keepdims=True) + jnp.exp(sink - m))
                pv = (jnp.dot(p[0].astype(BF16), vvar[0][(e, g)], preferred_element_type=F32)
                      + jnp.dot(p[1].astype(BF16), vvar[1][(e, g)], preferred_element_type=F32)
                      + jnp.dot(p[2].astype(BF16), vvar[2][(e, g)], preferred_element_type=F32))
                o_pair = o_pair + pv * (1.0 / denom)
                lse_ref[h] = m + jnp.log(denom)
            o_ref[:, cols] = o_pair.astype(BF16)

    smem = pl.BlockSpec(memory_space=pltpu.SMEM)
    kv = lambda col, which: pl.BlockSpec(
        (BLK, BLK), {0: lambda i: (i, col), 1: lambda i: (jnp.maximum(i - 1, 0), col), 2: lambda i: (0, col)}[which])
    return pl.pallas_call(
        body, grid=(NBLK,),
        in_specs=[smem, smem, pl.BlockSpec((2, BLK, BLK), lambda i: (0, 0, 0)),
                  pl.BlockSpec((BLK, 512), lambda i: (i, QA)),
                  kv(KA, 0), kv(KA, 1), kv(KA, 2), kv(VA, 0), kv(VA, 1), kv(VA, 2)],
        out_specs=[pl.BlockSpec((BLK, 512), lambda i: (i, 0)),
                   pl.BlockSpec((N_HEADS, BLK, 1), lambda i: (0, i, 0))],
        out_shape=[SDS((LP, 512), BF16), SDS((N_HEADS, LP, 1), F32)],
        scratch_shapes=[pltpu.VMEM((N_HEADS, BLK, BLK), F32), pltpu.VMEM((N_HEADS, BLK, BLK), F32)],
        compiler_params=_cparams(("arbitrary",)), name="swa_fwd",
    )(rel_bias, sinks, bkt, proj, proj, proj, proj, proj, proj, proj)


def _swa_bwd(proj, o_a, dmix, lse, rel_bias, sinks, bkt):
    def body(tab_ref, sink_ref, bkt_ref, q_ref, kc_ref, kp_ref, km_ref, vc_ref, vp_ref, vm_ref,
             o_ref, do_ref, lse_ref, dq_ref, dk_ref, dv_ref, dbias_ref, dsink_ref,
             bias_cur, bias_prev, acc_cur, acc_prev, acc_far, dsk):
        i = pl.program_id(0)

        @pl.when(i == 0)
        def _():
            _build_bias(tab_ref, bkt_ref, bias_cur, bias_prev)
            dk_ref[...] = jnp.zeros_like(dk_ref)
            dv_ref[...] = jnp.zeros_like(dv_ref)
            acc_cur[...] = jnp.zeros_like(acc_cur)
            acc_prev[...] = jnp.zeros_like(acc_prev)
            acc_far[...] = jnp.zeros_like(acc_far)
            dsk[...] = jnp.zeros_like(dsk)

        masks = _swa_masks(i)
        half = _lane_half(BLK)
        first = jnp.full((BLK, BLK), i, jnp.int32) == 1
        kvar = [_kv_variants(r) for r in (kc_ref, kp_ref, km_ref)]
        vvar = [_kv_variants(r) for r in (vc_ref, vp_ref, vm_ref)]
        dk_blk = [jnp.zeros((BLK, BLK), F32) for _ in range(3)]
        dv_blk = [jnp.zeros((BLK, BLK), F32) for _ in range(3)]
        for pp in range(4):
            cols = slice(pp * BLK, (pp + 1) * BLK)
            qs = (q_ref[:, cols].astype(F32) * SCALE).astype(BF16)
            dop = do_ref[:, cols]
            prod = dop.astype(F32) * o_ref[:, cols].astype(F32)
            d0 = jnp.sum(jnp.where(half == 0, prod, 0.0), axis=1, keepdims=True)
            d1 = jnp.sum(prod, axis=1, keepdims=True) - d0
            dq = jnp.zeros((BLK, BLK), F32)
            for e in range(2):
                h = 2 * pp + e
                g = h // 4
                dd = d0 if e == 0 else d1
                lse_h = lse_ref[h]
                s = _swa_scores(i, h, qs, kvar, bias_cur, bias_prev, tab_ref, masks)
                dsk[h] += -jnp.exp(sink_ref[0, h] - lse_h) * dd
                ds_all = []
                for x in range(3):
                    p = jnp.exp(s[x] - lse_h)
                    dp = lax.dot_general(dop, vvar[x][(e, g)], NT_DIMS, preferred_element_type=F32)
                    ds = p * (dp - dd)
                    ds_all.append(ds)
                    ds_b = ds.astype(BF16)
                    dq = dq + jnp.dot(ds_b, kvar[x][(e, g)], preferred_element_type=F32)
                    dkv = lax.dot_general(ds_b, qs, TN_DIMS, preferred_element_type=F32)
                    dvv = lax.dot_general(p.astype(BF16), dop, TN_DIMS, preferred_element_type=F32)
                    if e != g:
                        dkv = pltpu.roll(dkv, HALF, 1)
                        dvv = pltpu.roll(dvv, HALF, 1)
                    dk_blk[x] = dk_blk[x] + jnp.where(half == g, dkv, 0.0)
                    dv_blk[x] = dv_blk[x] + jnp.where(half == g, dvv, 0.0)
                acc_cur[h] += ds_all[0]
                acc_prev[h] += ds_all[1] + jnp.where(first, ds_all[2], 0.0)
                acc_far[h] += jnp.where(first, 0.0, ds_all[2])
            dq_ref[:, cols] = (dq * SCALE).astype(BF16)

        cur0 = pl.multiple_of(i * BLK, BLK)
        prev0 = pl.multiple_of(jnp.maximum(i - 1, 0) * BLK, BLK)
        dk_ref[pl.ds(cur0, BLK), :] += dk_blk[0]
        dv_ref[pl.ds(cur0, BLK), :] += dv_blk[0]
        dk_ref[pl.ds(prev0, BLK), :] += dk_blk[1]
        dv_ref[pl.ds(prev0, BLK), :] += dv_blk[1]
        dk_ref[0:BLK, :] += dk_blk[2]
        dv_ref[0:BLK, :] += dv_blk[2]

        @pl.when(i == NBLK - 1)
        def _():
            bc = bkt_ref[0]
            bp = bkt_ref[1]
            lane = lax.broadcasted_iota(jnp.int32, (1, BLK), 1)

            def per_bucket(b, carry):
                row = jnp.zeros((1, BLK), F32)
                for h in range(N_HEADS):
                    val = (jnp.sum(jnp.where(bc == b, acc_cur[h], 0.0), keepdims=True)
                           + jnp.sum(jnp.where(bp == b, acc_prev[h], 0.0), keepdims=True))
                    row = jnp.where(lane == h, val, row)
                dbias_ref[pl.ds(b, 1), :] = row
                return carry

            lax.fori_loop(0, N_BUCKETS, per_bucket, 0)
            far = jnp.zeros((1, BLK), F32)
            dsr = jnp.zeros((1, BLK), F32)
            for h in range(N_HEADS):
                far = jnp.where(lane == h, jnp.sum(acc_far[h], keepdims=True), far)
                dsr = jnp.where(lane == h, jnp.sum(dsk[h], keepdims=True), dsr)
            dbias_ref[N_BUCKETS - 1:N_BUCKETS, :] += far
            dsink_ref[...] = dsr

    smem = pl.BlockSpec(memory_space=pltpu.SMEM)
    kv = lambda col, which: pl.BlockSpec(
        (BLK, BLK), {0: lambda i: (i, col), 1: lambda i: (jnp.maximum(i - 1, 0), col), 2: lambda i: (0, col)}[which])
    blk512 = lambda col: pl.BlockSpec((BLK, 512), lambda i: (i, col))
    full = lambda r, c: pl.BlockSpec((r, c), lambda i: (0, 0))
    acc = pltpu.VMEM((N_HEADS, BLK, BLK), F32)
    return pl.pallas_call(
        body, grid=(NBLK,),
        in_specs=[smem, smem, pl.BlockSpec((2, BLK, BLK), lambda i: (0, 0, 0)), blk512(QA),
                  kv(KA, 0), kv(KA, 1), kv(KA, 2), kv(VA, 0), kv(VA, 1), kv(VA, 2),
                  blk512(0), blk512(0), pl.BlockSpec((N_HEADS, BLK, 1), lambda i: (0, i, 0))],
        out_specs=[blk512(0), full(LP, BLK), full(LP, BLK), full(N_BUCKETS, BLK), full(1, BLK)],
        out_shape=[SDS((LP, 512), BF16), SDS((LP, BLK), F32), SDS((LP, BLK), F32),
                   SDS((N_BUCKETS, BLK), F32), SDS((1, BLK), F32)],
        scratch_shapes=[acc, acc, acc, acc, acc, pltpu.VMEM((N_HEADS, BLK, 1), F32)],
        compiler_params=_cparams(("arbitrary",)), name="swa_bwd",
    )(rel_bias, sinks, bkt, proj, proj, proj, proj, proj, proj, proj, o_a, dmix, lse)


def _local_step(x, tgt, meta, rel_bias, g_pre_mix, g_post_mix, g_pre_ffn, g_post_ffn, b_forget, sinks,
                w_in_b, w_out_b, ffn_rider, ffn_weights, early_grads):
    bkt = jnp.asarray(_bucket_tables())
    h0 = jnp.concatenate([jnp.zeros((PAD_ROWS, D_MODEL), F32), meta, x], axis=0)
    tgt_p = jnp.concatenate([jnp.zeros((ROW0, D_MODEL), F32), tgt], axis=0)
    b_p = jnp.pad(b_forget, ((0, 0), (0, BLK - N_HEADS)))

    hn1, proj, f = _pre_mix(h0, g_pre_mix, w_in_b)
    o_a, lse_a = _swa_fwd(proj, rel_bias, sinks, bkt)
    cum = _forget_cumsum(f, b_p)
    ck_t = cum[:, :N_HEADS].T.reshape(N_HEADS, 1, LP)
    o_b, lse_b, carried = _fox_fwd(proj, ck_t, ffn_rider)
    w_gu_b, w_dn_b = ffn_weights(carried)
    a, h1, hn2 = _attn_out(o_a, o_b, w_out_b, h0, g_post_mix, g_pre_ffn)
    g, u, act = _ffn_up(hn2, w_gu_b)
    dff, dy, loss_blk, dg_post_ffn = _ffn_down_loss(act, w_dn_b, h1, tgt_p, g_post_ffn)

    dw_dn = _mm_tn(act, dff, FF_T, "dw_down")
    dg, du = _ffn_down_bwd(dff, w_dn_b, g, u)
    dw_gu = _dw_gate_up(hn2, dg, du)
    dh1, da, dg_pre_ffn, dg_post_mix = _ffn_up_bwd(dg, du, w_gu_b, h1, a, dy, g_pre_ffn, g_post_mix)
    dw_out = jnp.concatenate([_mm_tn(o_a, da, 512, "dw_out_a"), _mm_tn(o_b, da, 512, "dw_out_b")], axis=0)
    dmix = _attn_out_bwd(da, w_out_b)
    dq_b, dk_b, dv_b, dck, dcq, landed = _fox_bwd(proj, o_b, dmix, lse_b, ck_t, early_grads(dw_gu, dw_dn, dw_out))
    dq_a, dk_a, dv_a, dbias, dsink = _swa_bwd(proj, o_a, dmix, lse_a, rel_bias, sinks, bkt)
    dcum = dcq - jnp.pad(dck.reshape(N_HEADS, LP).T, ((0, 0), (0, BLK - N_HEADS)))
    df, db = _forget_cumsum_bwd(dcum, f, b_p)
    dproj, dh0, dg_pre_mix = _pre_mix_bwd(dq_a, dq_b, dk_b, dv_b, dk_a, dv_a, df, w_in_b, h0, dh1, g_pre_mix)
    dw_in = _mm_tn(hn1, dproj, 512, "dw_in")

    return dict(loss=loss_blk[0, 0], grad_x=dh0[ROW0:], meta=dh0[PAD_ROWS:ROW0],
                rel_bias=dbias[:, :N_HEADS], ln_pre_mix=dg_pre_mix, ln_post_mix=dg_post_mix,
                ln_pre_ffn=dg_pre_ffn, ln_post_ffn=dg_post_ffn, b_forget=db[:, :N_HEADS],
                sinks=dsink[:, :N_HEADS], w_in=dw_in, w_out=dw_out, w_gate_up=dw_gu, w_down=dw_dn,
                landed=landed)


N_SMALL = 24
LOSS_ROW = 6


def _place():
    x, y, c = lax.axis_index("x"), lax.axis_index("y"), lax.axis_index("c")
    return x, y, c, [(1 - x, y), (x, 1 - y), (1 - x, 1 - y)]


def _run_alone(rider, name):
    a, b = len(rider.operands), len(rider.out_shapes)

    def body(*refs):
        mine = (refs[:a], refs[a:a + b], refs[a + b:])
        rider.first(*mine)
        rider.middle(*mine)
        rider.last(*mine)

    return pl.pallas_call(body, in_specs=[HBM_SPEC] * a, out_specs=[HBM_SPEC] * b, out_shape=rider.out_shapes,
                          scratch_shapes=rider.scratch(), name=name)(*rider.operands)


def _gather_rider(shards):
    n = len(shards)

    def copies(ins, outs, sems):
        send_sems, recv_sems = sems
        x, y, c, others = _place()
        chip = 2 * x + y
        sibling = (x, y, 1 - c)

        def rc(a, k, src, dst, to):
            return pltpu.make_async_remote_copy(src_ref=src, dst_ref=dst, send_sem=send_sems.at[6 * a + k],
                                                recv_sem=recv_sems.at[6 * a + k], device_id=to, device_id_type=MESH)

        pairs = [(a, j, ox, oy) for a in range(n) for j, (ox, oy) in enumerate(others)]
        sent = [rc(a, j, ins[a].at[c], outs[a].at[chip, c], (ox, oy, c)) for a, j, ox, oy in pairs]
        landed = [rc(a, j, outs[a].at[2 * ox + oy, c], outs[a].at[2 * ox + oy, c], sibling) for a, j, ox, oy in pairs]
        passed = [rc(a, 3 + j, outs[a].at[2 * ox + oy, c], outs[a].at[2 * ox + oy, c], sibling)
                  for a, j, ox, oy in pairs]
        arriving = [rc(a, 3 + j, outs[a].at[2 * ox + oy, 1 - c], outs[a].at[2 * ox + oy, 1 - c], sibling)
                    for a, j, ox, oy in pairs]
        return sent, landed, passed, arriving

    def first(*mine):
        for cp in copies(*mine)[0]:
            cp.start()

    def middle(*mine):
        _, landed, passed, _ = copies(*mine)
        for got, cp in zip(landed, passed):
            got.wait_recv()
            cp.start()

    def last(*mine):
        sent, _, passed, arriving = copies(*mine)
        for cp in arriving:
            cp.wait_recv()
        for cp in sent + passed:
            cp.wait_send()

    return _Rider(shards, [SDS((4,) + s.shape, s.dtype) for s in shards], [6 * n, 6 * n], first, middle, last)


def _swap_halves(grads, name):
    n = len(grads)

    def body(*refs):
        ins, outs = refs[:n], refs[n:2 * n]
        send_sems, recv_sems = refs[2 * n:]
        x, y, c, _ = _place()
        copies = [pltpu.make_async_remote_copy(
            src_ref=ins[a].at[s, 1 - c], dst_ref=outs[a].at[s], send_sem=send_sems.at[4 * a + s],
            recv_sem=recv_sems.at[4 * a + s], device_id=(x, y, 1 - c), device_id_type=MESH)
            for a in range(n) for s in range(4)]
        for cp in copies:
            cp.start()
        for cp in copies:
            cp.wait()

    return pl.pallas_call(
        body, in_specs=[HBM_SPEC] * n, out_specs=[HBM_SPEC] * n,
        out_shape=[SDS((4,) + g.shape[2:], g.dtype) for g in grads],
        scratch_shapes=[pltpu.SemaphoreType.DMA((4 * n,)), pltpu.SemaphoreType.DMA((4 * n,))],
        name=name)(*grads)


def _pair_sum(g, got, c_arr, name):
    rh, cc = got.shape[1:]

    def body(c_ref, g_ref, p_ref, o_ref):
        o_ref[0] = (g_ref[0, 0] + p_ref[0]).astype(BF16)

    grid_spec = pltpu.PrefetchScalarGridSpec(
        num_scalar_prefetch=1, grid=(4,),
        in_specs=[pl.BlockSpec((1, 1, rh, cc), lambda s, c_ref: (s, c_ref[0], 0, 0)),
                  pl.BlockSpec((1, rh, cc), lambda s, c_ref: (s, 0, 0))],
        out_specs=pl.BlockSpec((1, rh, cc), lambda s, c_ref: (s, 0, 0)))
    return pl.pallas_call(body, grid_spec=grid_spec, out_shape=SDS((4, rh, cc), BF16),
                          compiler_params=_cparams(("parallel",)), name=name)(c_arr, g, got)


def _exchange_rider(parts, small=None):
    n = len(parts)

    def copies(ins, outs, sems):
        x, y, c, others = _place()
        out = [pltpu.make_async_remote_copy(
            src_ref=ins[a].at[2 * ox + oy], dst_ref=outs[a].at[j], send_sem=sems[0].at[3 * a + j],
            recv_sem=sems[1].at[3 * a + j], device_id=(ox, oy, c), device_id_type=MESH)
            for a in range(n) for j, (ox, oy) in enumerate(others)]
        own = []
        if small is not None:
            me = 4 * x + 2 * y + c
            peers = [(x, y, 1 - c)] + [(ox, oy, c) for ox, oy in others] + [(ox, oy, 1 - c) for ox, oy in others]
            out += [pltpu.make_async_remote_copy(
                src_ref=ins[n], dst_ref=outs[n].at[me], send_sem=sems[2].at[k], recv_sem=sems[3].at[k],
                device_id=peer, device_id_type=MESH) for k, peer in enumerate(peers)]
            own = [pltpu.make_async_copy(ins[n], outs[n].at[me], sems[4].at[0])]
        return out, own

    def first(*mine):
        out, own = copies(*mine)
        for cp in own + out:
            cp.start()

    def middle(*mine):
        pass

    def last(*mine):
        out, own = copies(*mine)
        for cp in out + own:
            cp.wait()

    shapes = [SDS((3,) + p.shape[1:], p.dtype) for p in parts]
    if small is None:
        return _Rider(parts, shapes, [3 * n, 3 * n], first, middle, last)
    return _Rider(parts + [small], shapes + [SDS((8,) + small.shape, small.dtype)], [3 * n, 3 * n, 7, 7, 1],
                  first, middle, last)


def _chip_sum(parts, landed, chip_arr, name):
    rh, cc = landed.shape[1:]
    tr = rh // 2

    def body(chip_ref, own_ref, p_ref, o_ref):
        o_ref[...] = ((own_ref[0].astype(F32) + p_ref[0].astype(F32)) + p_ref[1].astype(F32)) + p_ref[2].astype(F32)

    grid_spec = pltpu.PrefetchScalarGridSpec(
        num_scalar_prefetch=1, grid=(2,),
        in_specs=[pl.BlockSpec((1, tr, cc), lambda i, chip_ref: (chip_ref[0], i, 0)),
                  pl.BlockSpec((3, tr, cc), lambda i, chip_ref: (0, i, 0))],
        out_specs=pl.BlockSpec((tr, cc), lambda i, chip_ref: (i, 0)))
    return pl.pallas_call(body, grid_spec=grid_spec, out_shape=SDS((rh, cc), F32),
                          compiler_params=_cparams(("parallel",)), name=name)(chip_arr, parts, landed)


def _device_sum(p):
    def body(p_ref, o_ref):
        acc = p_ref[0]
        for k in range(1, 8):
            acc = acc + p_ref[k]
        o_ref[...] = acc

    return pl.pallas_call(body, out_shape=SDS(p.shape[1:], F32), name="small_sum")(p)


def _join_halves(halves):
    n = len(halves)

    def body(*refs):
        ins, outs = refs[:n], refs[n:2 * n]
        send_sems, recv_sems = refs[2 * n:]
        x, y, c, _ = _place()
        copies = [pltpu.make_async_remote_copy(
            src_ref=ins[a], dst_ref=outs[a], send_sem=send_sems.at[a], recv_sem=recv_sems.at[a],
            device_id=(x, y, 1 - c), device_id_type=MESH) for a in range(n)]
        for cp in copies:
            cp.start()
        for cp in copies:
            cp.wait()

    return pl.pallas_call(
        body, in_specs=[HBM_SPEC] * n, out_specs=[HBM_SPEC] * n,
        out_shape=[SDS(h.shape, h.dtype) for h in halves],
        scratch_shapes=[pltpu.SemaphoreType.DMA((n,)), pltpu.SemaphoreType.DMA((n,))],
        name="join_halves")(*halves)


def _adamw(w, g, m, v, name):
    rows, cols = w.shape
    tr = rows if rows <= 352 else (256 if rows % 256 == 0 else 352)

    def body(w_ref, g_ref, m_ref, v_ref, d_ref, nm_ref, nv_ref):
        gg = g_ref[...]
        nm = ADAM_B1 * m_ref[...] + (1.0 - ADAM_B1) * gg
        nv = ADAM_B2 * v_ref[...] + (1.0 - ADAM_B2) * (gg * gg)
        nm_ref[...] = nm
        nv_ref[...] = nv
        m_hat = nm / (1.0 - ADAM_B1 ** ADAM_STEP)
        v_hat = nv / (1.0 - ADAM_B2 ** ADAM_STEP)
        d_ref[...] = -ADAM_LR * (m_hat / (jnp.sqrt(v_hat) + ADAM_EPS) + ADAM_WD * w_ref[...])

    blk = pl.BlockSpec((tr, cols), lambda i: (i, 0))
    return pl.pallas_call(
        body, grid=(rows // tr,), in_specs=[blk] * 4, out_specs=[blk] * 3,
        out_shape=[SDS((rows, cols), F32)] * 3,
        compiler_params=_cparams(("parallel",)), name=name)(w, g, m, v)


def _pack_small(pre_mix, post_mix, pre_ffn, post_ffn, rel_bias, b_forget, sinks):
    def at(row, v):
        return jnp.pad(v, ((row, 7 - row), (0, D_MODEL - v.shape[1])))
    return (at(0, pre_mix) + at(1, post_mix) + at(2, pre_ffn) + at(3, post_ffn)
            + at(4, rel_bias.reshape(1, N_BUCKETS * N_HEADS)) + at(5, jnp.concatenate([b_forget, sinks], axis=1)))


def _unpack_small(p):
    return dict(ln_pre_mix=p[0:1], ln_post_mix=p[1:2], ln_pre_ffn=p[2:3], ln_post_ffn=p[3:4],
                rel_bias=p[4, :N_BUCKETS * N_HEADS].reshape(N_BUCKETS, N_HEADS),
                b_forget=p[5:6, 0:N_HEADS], sinks=p[5:6, N_HEADS:2 * N_HEADS])


WEIGHTS = ("meta_tokens", "rel_bias", "ln_pre_mix", "ln_post_mix", "ln_pre_ffn", "ln_post_ffn",
           "w_in", "b_forget", "sinks", "w_out", "w_gate_up", "w_down")


def kernel(x, meta_tokens, rel_bias, ln_pre_mix, ln_post_mix, ln_pre_ffn, ln_post_ffn, w_in, b_forget, sinks, w_out, w_gate_up, w_down, loss_target, m_meta_tokens, m_rel_bias, m_ln_pre_mix, m_ln_post_mix, m_ln_pre_ffn, m_ln_post_ffn, m_w_in, m_b_forget, m_sinks, m_w_out, m_w_gate_up, m_w_down, v_meta_tokens, v_rel_bias, v_ln_pre_mix, v_ln_post_mix, v_ln_pre_ffn, v_ln_post_ffn, v_w_in, v_b_forget, v_sinks, v_w_out, v_w_gate_up, v_w_down):
    xi, yi, ci = lax.axis_index("x"), lax.axis_index("y"), lax.axis_index("c")
    chip = 2 * xi + yi
    c_arr = jnp.reshape(ci, (1,)).astype(jnp.int32)

    def halves(w, dtype):
        return w.astype(dtype).reshape(2, w.shape[0] // 2, w.shape[1])

    def with_own(gathered, shards):
        return [lax.dynamic_update_slice(got, own[None], (chip, 0, 0, 0)) for got, own in zip(gathered, shards)]

    shards = [halves(w_in[0], BF16), halves(w_out[0], BF16), halves(meta_tokens, F32)]
    gw_in, gw_out, g_meta = with_own(_run_alone(_gather_rider(shards), "gather_mixer_weights"), shards)
    ffn_shards = [halves(w_gate_up[0], BF16), halves(w_down[0], BF16)]

    def ffn_weights(carried):
        gw_gu, gw_dn = with_own(carried, ffn_shards)
        return gw_gu.reshape(4, D_MODEL, FF_T), gw_dn.reshape(D_FF, D_MODEL)

    early = {}

    def early_grads(dw_gu, dw_dn, dw_out):
        grads = [dw_out.reshape(4, 2, 128, D_MODEL), dw_gu.reshape(4, 2, 512, FF_T), dw_dn.reshape(4, 2, 352, D_MODEL)]
        got = _swap_halves(grads, "swap_halves_early")
        early["parts"] = [_pair_sum(g, p, c_arr, "pair_sum_%d" % a) for a, (g, p) in enumerate(zip(grads, got))]
        return _exchange_rider(early["parts"])
    w_in_all = gw_in.reshape(4, D_MODEL, D_PROJ // 4).transpose(1, 0, 2).reshape(D_MODEL, D_PROJ)
    w_in_b = jnp.concatenate(
        [w_in_all[:, 0:512], w_in_all[:, 768:1280], w_in_all[:, 1280:1792], w_in_all[:, 1792:2304],
         w_in_all[:, 512:640], w_in_all[:, 640:768], w_in_all[:, 2304:2312],
         jnp.zeros((D_MODEL, D_PROJ_P - D_PROJ), BF16)], axis=1)
    meta_all = g_meta.reshape(4, N_META, D_MODEL // 4).transpose(1, 0, 2).reshape(N_META, D_MODEL)

    loc = _local_step(x[0], loss_target[0], meta_all, rel_bias, ln_pre_mix, ln_post_mix, ln_pre_ffn, ln_post_ffn,
                      b_forget, sinks, w_in_b, gw_out.reshape(D_MODEL, D_MODEL),
                      _gather_rider(ffn_shards), ffn_weights, early_grads)

    n = loc["w_in"]
    dw_in = jnp.concatenate([n[:, 0:512], n[:, 2048:2176], n[:, 2176:2304], n[:, 512:1024], n[:, 1024:1536],
                             n[:, 1536:2048], n[:, 2304:2312]], axis=1)
    dw_in = dw_in.reshape(D_MODEL, 4, D_PROJ // 4).transpose(1, 0, 2).reshape(4, 2, 512, D_PROJ // 4)
    small = jnp.concatenate(
        [_pack_small(loc["ln_pre_mix"], loc["ln_post_mix"], loc["ln_pre_ffn"], loc["ln_post_ffn"],
                     loc["rel_bias"], loc["b_forget"], loc["sinks"])
         + jnp.pad(loc["loss"].reshape(1, 1), ((LOSS_ROW, 7 - LOSS_ROW), (0, D_MODEL - 1))), loc["meta"]], axis=0)

    (got_in,) = _swap_halves([dw_in], "swap_halves_late")
    part_in = _pair_sum(dw_in, got_in, c_arr, "pair_sum_late")
    landed_in, small_all = _run_alone(_exchange_rider([part_in], small), "exchange_late")
    chip_arr = jnp.reshape(chip, (1,)).astype(jnp.int32)
    parts = [part_in] + early["parts"]
    landed = [landed_in] + loc["landed"]
    mine = [_chip_sum(p, l, chip_arr, "chip_sum_%d" % a) for a, (p, l) in enumerate(zip(parts, landed))]
    small_sum = _device_sum(small_all)
    theirs = _join_halves(mine)
    full = [jnp.where(ci == 0, jnp.concatenate([m, t], axis=0), jnp.concatenate([t, m], axis=0))
            for m, t in zip(mine, theirs)]
    g_w_in, g_w_out, g_w_gu, g_w_dn = full
    g_meta_tokens = lax.dynamic_slice(small_sum[8:N_SMALL], (0, chip * (D_MODEL // 4)), (N_META, D_MODEL // 4))
    g_small = small_sum[0:8]

    grad = _unpack_small(g_small)
    grad.update(meta_tokens=g_meta_tokens, w_in=g_w_in[None], w_out=g_w_out[None], w_gate_up=g_w_gu[None],
                w_down=g_w_dn[None])

    delta, new_m, new_v = {}, {}, {}
    big = dict(w_in=(w_in, m_w_in, v_w_in, g_w_in), w_out=(w_out, m_w_out, v_w_out, g_w_out),
               w_gate_up=(w_gate_up, m_w_gate_up, v_w_gate_up, g_w_gu), w_down=(w_down, m_w_down, v_w_down, g_w_dn))
    for name, (w, m, v, g) in big.items():
        d, nm, nv = _adamw(w[0], g, m[0], v[0], "adamw_" + name)
        delta[name], new_m[name], new_v[name] = d[None], nm[None], nv[None]
    delta["meta_tokens"], new_m["meta_tokens"], new_v["meta_tokens"] = _adamw(
        meta_tokens, g_meta_tokens, m_meta_tokens, v_meta_tokens, "adamw_meta")
    d, nm, nv = _adamw(
        _pack_small(ln_pre_mix, ln_post_mix, ln_pre_ffn, ln_post_ffn, rel_bias, b_forget, sinks), g_small,
        _pack_small(m_ln_pre_mix, m_ln_post_mix, m_ln_pre_ffn, m_ln_post_ffn, m_rel_bias, m_b_forget, m_sinks),
        _pack_small(v_ln_pre_mix, v_ln_post_mix, v_ln_pre_ffn, v_ln_post_ffn, v_rel_bias, v_b_forget, v_sinks),
        "adamw_small")
    delta.update(_unpack_small(d))
    new_m.update(_unpack_small(nm))
    new_v.update(_unpack_small(nv))

    loss = small_sum[LOSS_ROW, 0]
    return (loss,loc["grad_x"][None], *[grad[k] for k in WEIGHTS], *[delta[k] for k in WEIGHTS],
            *[new_m[k] for k in WEIGHTS], *[new_v[k] for k in WEIGHTS])
```

```python
import math

import numpy as np
import jax
import jax.numpy as jnp
from jax import lax
from jax.experimental import pallas as pl
from jax.experimental.pallas import tpu as pltpu

F32 = jnp.float32
BF16 = jnp.bfloat16
MESH = pl.DeviceIdType.MESH
SDS = jax.ShapeDtypeStruct

D_MODEL = 1024
SEQ = 4096
N_META = 16
N_HEADS = 8
HALF = 64
D_FF = 2816
N_BUCKETS = 32
EPS = 1e-6
NEG = -1e30
SCALE = 0.125
PAD_ROWS = 112
ROW0 = PAD_ROWS + N_META
LP = ROW0 + SEQ
BLK = 128
NBLK = LP // BLK
TM = 384
NT = LP // TM
D_PROJ = 2312
D_PROJ_P = 2432
D_QKV = 2304
FF_T = 1408
VMEM_LIMIT = 56 * 1024 * 1024

ADAM_LR = 0.001
ADAM_B1 = 0.9
ADAM_B2 = 0.999
ADAM_EPS = 1e-08
ADAM_WD = 0.01
ADAM_STEP = 10

QA, QB, KB, VB = 0, 1, 2, 3
KA, VA = 16, 17

NT_DIMS = (((1,), (1,)), ((), ()))
TN_DIMS = (((0,), (0,)), ((), ()))


def _cparams(sem):
    return pltpu.CompilerParams(dimension_semantics=sem, vmem_limit_bytes=VMEM_LIMIT)


def _t5_bucket_np(d):
    n = np.maximum(d, 0).astype(np.int32)
    nf = np.maximum(n, 1).astype(np.float32)
    large = 16 + (np.log(nf / np.float32(16)) / np.float32(math.log(8.0)) * np.float32(16)).astype(np.int32)
    large = np.minimum(large, N_BUCKETS - 1)
    return np.where(n < 16, n, large).astype(np.int32)


def _bucket_tables():
    qi = np.arange(BLK)[:, None]
    ki = np.arange(BLK)[None, :]
    return np.stack([_t5_bucket_np(qi - ki), _t5_bucket_np(qi - ki + BLK)])


def _rms(x):
    return lax.rsqrt(jnp.mean(x * x, axis=-1, keepdims=True) + EPS)


def _rms_bwd(n, r, gdy):
    return r * (gdy - n * jnp.mean(n * gdy, axis=-1, keepdims=True))


def _pre_mix(h0, gain, w_in_b):
    def body(h_ref, g_ref, w_ref, hn_ref, proj_ref, f_ref):
        x = h_ref[...]
        hn = (x * _rms(x) * g_ref[...]).astype(BF16)
        hn_ref[...] = hn
        p = jnp.dot(hn, w_ref[...], preferred_element_type=F32)
        proj_ref[...] = p[:, :D_QKV].astype(BF16)
        f_ref[...] = p[:, D_QKV:]

    return pl.pallas_call(
        body, grid=(NT,),
        in_specs=[pl.BlockSpec((TM, D_MODEL), lambda i: (i, 0)),
                  pl.BlockSpec((1, D_MODEL), lambda i: (0, 0)),
                  pl.BlockSpec((D_MODEL, D_PROJ_P), lambda i: (0, 0))],
        out_specs=[pl.BlockSpec((TM, D_MODEL), lambda i: (i, 0)),
                   pl.BlockSpec((TM, D_QKV), lambda i: (i, 0)),
                   pl.BlockSpec((TM, BLK), lambda i: (i, 0))],
        out_shape=[SDS((LP, D_MODEL), BF16), SDS((LP, D_QKV), BF16), SDS((LP, BLK), F32)],
        compiler_params=_cparams(("parallel",)), name="pre_mix")(h0, gain, w_in_b)


def _attn_out(o_a, o_b, w_out_b, h0, g_post, g_pre_ffn):
    def body(oa_ref, ob_ref, w_ref, h0_ref, gp_ref, gf_ref, a_ref, h1_ref, hn2_ref):
        a = (jnp.dot(oa_ref[...], w_ref[0:512, :], preferred_element_type=F32)
             + jnp.dot(ob_ref[...], w_ref[512:1024, :], preferred_element_type=F32))
        a_ref[...] = a
        h1 = h0_ref[...] + a * _rms(a) * gp_ref[...]
        h1_ref[...] = h1
        hn2_ref[...] = (h1 * _rms(h1) * gf_ref[...]).astype(BF16)

    row = lambda w: pl.BlockSpec((TM, w), lambda i: (i, 0))
    vec = pl.BlockSpec((1, D_MODEL), lambda i: (0, 0))
    return pl.pallas_call(
        body, grid=(NT,),
        in_specs=[row(512), row(512), pl.BlockSpec((D_MODEL, D_MODEL), lambda i: (0, 0)), row(D_MODEL), vec, vec],
        out_specs=[row(D_MODEL), row(D_MODEL), row(D_MODEL)],
        out_shape=[SDS((LP, D_MODEL), F32), SDS((LP, D_MODEL), F32), SDS((LP, D_MODEL), BF16)],
        compiler_params=_cparams(("parallel",)), name="attn_out")(o_a, o_b, w_out_b, h0, g_post, g_pre_ffn)


def _ffn_up(hn2, w_gu_b):
    def body(x_ref, wg_ref, wu_ref, g_ref, u_ref, act_ref):
        x = x_ref[...]
        g = jnp.dot(x, wg_ref[0], preferred_element_type=F32)
        u = jnp.dot(x, wu_ref[0], preferred_element_type=F32)
        g_ref[...] = g.astype(BF16)
        u_ref[...] = u.astype(BF16)
        act_ref[...] = (g * (1.0 / (1.0 + jnp.exp(-g))) * u).astype(BF16)

    out = pl.BlockSpec((TM, FF_T), lambda j, i: (i, j))
    return pl.pallas_call(
        body, grid=(2, NT),
        in_specs=[pl.BlockSpec((TM, D_MODEL), lambda j, i: (i, 0)),
                  pl.BlockSpec((1, D_MODEL, FF_T), lambda j, i: (j, 0, 0)),
                  pl.BlockSpec((1, D_MODEL, FF_T), lambda j, i: (j + 2, 0, 0))],
        out_specs=[out, out, out],
        out_shape=[SDS((LP, D_FF), BF16)] * 3,
        compiler_params=_cparams(("parallel", "parallel")), name="ffn_up")(hn2, w_gu_b, w_gu_b)


def _ffn_down_loss(act, w_dn_b, h1, tgt_p, g_post_ffn):
    def body(act_ref, w_ref, h1_ref, t_ref, g_ref, dff_ref, dy_ref, loss_ref, dg_ref):
        i = pl.program_id(0)

        @pl.when(i == 0)
        def _():
            loss_ref[...] = jnp.zeros_like(loss_ref)
            dg_ref[...] = jnp.zeros_like(dg_ref)

        ff = jnp.dot(act_ref[...], w_ref[...], preferred_element_type=F32)
        r = _rms(ff)
        n = ff * r
        g = g_ref[...]
        y = h1_ref[...] + n * g
        rows = i * TM + lax.broadcasted_iota(jnp.int32, (TM, D_MODEL), 0)
        diff = jnp.where(rows >= ROW0, y - t_ref[...], 0.0)
        loss_ref[...] += 0.5 * jnp.sum(diff * diff) / D_MODEL
        dy = diff / D_MODEL
        dy_ref[...] = dy
        dg_ref[...] += jnp.sum(dy * n, axis=0, keepdims=True)
        dff_ref[...] = _rms_bwd(n, r, g * dy).astype(BF16)

    row = pl.BlockSpec((TM, D_MODEL), lambda i: (i, 0))
    return pl.pallas_call(
        body, grid=(NT,),
        in_specs=[pl.BlockSpec((TM, D_FF), lambda i: (i, 0)), pl.BlockSpec((D_FF, D_MODEL), lambda i: (0, 0)),
                  row, row, pl.BlockSpec((1, D_MODEL), lambda i: (0, 0))],
        out_specs=[row, row, pl.BlockSpec((8, BLK), lambda i: (0, 0)), pl.BlockSpec((1, D_MODEL), lambda i: (0, 0))],
        out_shape=[SDS((LP, D_MODEL), BF16), SDS((LP, D_MODEL), F32), SDS((8, BLK), F32), SDS((1, D_MODEL), F32)],
        compiler_params=_cparams(("arbitrary",)), name="ffn_down_loss")(act, w_dn_b, h1, tgt_p, g_post_ffn)


def _ffn_down_bwd(dff, w_dn_b, g, u):
    def body(d_ref, w_ref, g_ref, u_ref, dg_ref, du_ref):
        dact = lax.dot_general(d_ref[...], w_ref[...], NT_DIMS, preferred_element_type=F32)
        gg = g_ref[...].astype(F32)
        sig = 1.0 / (1.0 + jnp.exp(-gg))
        dg_ref[...] = (dact * u_ref[...].astype(F32) * sig * (1.0 + gg * (1.0 - sig))).astype(BF16)
        du_ref[...] = (dact * gg * sig).astype(BF16)

    blk = pl.BlockSpec((TM, FF_T), lambda j, i: (i, j))
    return pl.pallas_call(
        body, grid=(2, NT),
        in_specs=[pl.BlockSpec((TM, D_MODEL), lambda j, i: (i, 0)),
                  pl.BlockSpec((FF_T, D_MODEL), lambda j, i: (j, 0)), blk, blk],
        out_specs=[blk, blk],
        out_shape=[SDS((LP, D_FF), BF16)] * 2,
        compiler_params=_cparams(("parallel", "parallel")), name="ffn_down_bwd")(dff, w_dn_b, g, u)


def _ffn_up_bwd(dg, du, w_gu_b, h1, a, dy, g_pre_ffn, g_post_mix):
    def body(dg_ref, du_ref, w_ref, h1_ref, a_ref, dy_ref, gf_ref, gp_ref,
             dh1_ref, da_ref, dgf_ref, dgp_ref, acc):
        i = pl.program_id(0)
        s = pl.program_id(1)

        @pl.when((i == 0) & (s == 0))
        def _():
            dgf_ref[...] = jnp.zeros_like(dgf_ref)
            dgp_ref[...] = jnp.zeros_like(dgp_ref)

        @pl.when(s == 0)
        def _():
            acc[...] = jnp.zeros_like(acc)

        @pl.when(s < 2)
        def _():
            acc[...] += lax.dot_general(dg_ref[...], w_ref[0], NT_DIMS, preferred_element_type=F32)

        @pl.when(s >= 2)
        def _():
            acc[...] += lax.dot_general(du_ref[...], w_ref[0], NT_DIMS, preferred_element_type=F32)

        @pl.when(s == 3)
        def _():
            dhn2 = acc[...]
            h1 = h1_ref[...]
            r2 = _rms(h1)
            n2 = h1 * r2
            dgf_ref[...] += jnp.sum(dhn2 * n2, axis=0, keepdims=True)
            dh1 = dy_ref[...] + _rms_bwd(n2, r2, gf_ref[...] * dhn2)
            dh1_ref[...] = dh1
            av = a_ref[...]
            ra = _rms(av)
            na = av * ra
            dgp_ref[...] += jnp.sum(dh1 * na, axis=0, keepdims=True)
            da_ref[...] = _rms_bwd(na, ra, gp_ref[...] * dh1).astype(BF16)

    row = pl.BlockSpec((TM, D_MODEL), lambda i, s: (i, 0))
    vec = pl.BlockSpec((1, D_MODEL), lambda i, s: (0, 0))
    return pl.pallas_call(
        body, grid=(NT, 4),
        in_specs=[pl.BlockSpec((TM, FF_T), lambda i, s: (i, jnp.minimum(s, 1))),
                  pl.BlockSpec((TM, FF_T), lambda i, s: (i, jnp.maximum(s - 2, 0))),
                  pl.BlockSpec((1, D_MODEL, FF_T), lambda i, s: (s, 0, 0)),
                  row, row, row, vec, vec],
        out_specs=[row, row, vec, vec],
        out_shape=[SDS((LP, D_MODEL), F32), SDS((LP, D_MODEL), BF16), SDS((1, D_MODEL), F32), SDS((1, D_MODEL), F32)],
        scratch_shapes=[pltpu.VMEM((TM, D_MODEL), F32)],
        compiler_params=_cparams(("arbitrary", "arbitrary")), name="ffn_up_bwd",
    )(dg, du, w_gu_b, h1, a, dy, g_pre_ffn, g_post_mix)


def _attn_out_bwd(da, w_out_b):
    def body(d_ref, w_ref, o_ref):
        o_ref[...] = lax.dot_general(d_ref[...], w_ref[...], NT_DIMS, preferred_element_type=F32).astype(BF16)

    row = pl.BlockSpec((TM, D_MODEL), lambda i: (i, 0))
    return pl.pallas_call(
        body, grid=(NT,),
        in_specs=[row, pl.BlockSpec((D_MODEL, D_MODEL), lambda i: (0, 0))],
        out_specs=row, out_shape=SDS((LP, D_MODEL), BF16),
        compiler_params=_cparams(("parallel",)), name="attn_out_bwd")(da, w_out_b)


def _pre_mix_bwd(dq_a, dq_b, dk_b, dv_b, dk_a, dv_a, df, w_in_b, h0, dh1, g_pre_mix):
    def body(qa_ref, qb_ref, kb_ref, vb_ref, ka_ref, va_ref, f_ref, w_ref, h0_ref, dh1_ref, g_ref,
             dproj_ref, dh0_ref, dg_ref):
        i = pl.program_id(0)

        @pl.when(i == 0)
        def _():
            dg_ref[...] = jnp.zeros_like(dg_ref)

        dproj = jnp.concatenate(
            [qa_ref[...], (qb_ref[...] * SCALE).astype(BF16), kb_ref[...], vb_ref[...],
             ka_ref[...].astype(BF16), va_ref[...].astype(BF16), f_ref[...].astype(BF16)], axis=1)
        dproj_ref[...] = dproj
        dhn = lax.dot_general(dproj, w_ref[...], NT_DIMS, preferred_element_type=F32)
        x = h0_ref[...]
        r = _rms(x)
        n = x * r
        dg_ref[...] += jnp.sum(dhn * n, axis=0, keepdims=True)
        dh0_ref[...] = dh1_ref[...] + _rms_bwd(n, r, g_ref[...] * dhn)

    row = lambda w: pl.BlockSpec((TM, w), lambda i: (i, 0))
    vec = pl.BlockSpec((1, D_MODEL), lambda i: (0, 0))
    return pl.pallas_call(
        body, grid=(NT,),
        in_specs=[row(512), row(512), row(512), row(512), row(BLK), row(BLK), row(BLK),
                  pl.BlockSpec((D_MODEL, D_PROJ_P), lambda i: (0, 0)), row(D_MODEL), row(D_MODEL), vec],
        out_specs=[row(D_PROJ_P), row(D_MODEL), vec],
        out_shape=[SDS((LP, D_PROJ_P), BF16), SDS((LP, D_MODEL), F32), SDS((1, D_MODEL), F32)],
        compiler_params=_cparams(("arbitrary",)), name="pre_mix_bwd",
    )(dq_a, dq_b, dk_b, dv_b, dk_a, dv_a, df, w_in_b, h0, dh1, g_pre_mix)


def _mm_tn(a, b, tm, name):
    m_total = a.shape[1]
    n = b.shape[1]

    def body(a_ref, b_ref, o_ref):
        @pl.when(pl.program_id(1) == 0)
        def _():
            o_ref[...] = jnp.zeros_like(o_ref)
        o_ref[...] += lax.dot_general(a_ref[...], b_ref[...], TN_DIMS, preferred_element_type=F32)

    return pl.pallas_call(
        body, grid=(m_total // tm, NT),
        in_specs=[pl.BlockSpec((TM, tm), lambda mi, k: (k, mi)),
                  pl.BlockSpec((TM, n), lambda mi, k: (k, 0))],
        out_specs=pl.BlockSpec((tm, n), lambda mi, k: (mi, 0)),
        out_shape=SDS((m_total, n), F32),
        compiler_params=_cparams(("parallel", "arbitrary")), name=name)(a, b)


def _dw_gate_up(hn2, dg, du):
    def body(a_ref, dg_ref, du_ref, o_ref):
        s = pl.program_id(0)

        @pl.when(pl.program_id(1) == 0)
        def _():
            o_ref[...] = jnp.zeros_like(o_ref)

        @pl.when(s < 2)
        def _():
            o_ref[0] += lax.dot_general(a_ref[...], dg_ref[...], TN_DIMS, preferred_element_type=F32)

        @pl.when(s >= 2)
        def _():
            o_ref[0] += lax.dot_general(a_ref[...], du_ref[...], TN_DIMS, preferred_element_type=F32)

    return pl.pallas_call(
        body, grid=(4, NT),
        in_specs=[pl.BlockSpec((TM, D_MODEL), lambda s, k: (k, 0)),
                  pl.BlockSpec((TM, FF_T), lambda s, k: (k, jnp.minimum(s, 1))),
                  pl.BlockSpec((TM, FF_T), lambda s, k: (k, jnp.maximum(s - 2, 0)))],
        out_specs=pl.BlockSpec((1, D_MODEL, FF_T), lambda s, k: (s, 0, 0)),
        out_shape=SDS((4, D_MODEL, FF_T), F32),
        compiler_params=_cparams(("parallel", "arbitrary")), name="dw_gate_up")(hn2, dg, du)


def _split3(x):
    hi = x.astype(BF16)
    r1 = x - hi.astype(F32)
    mid = r1.astype(BF16)
    lo = (r1 - mid.astype(F32)).astype(BF16)
    return hi, mid, lo


def _tri_matmul(tri, x):
    hi, mid, lo = _split3(x)
    dot = lambda t: jnp.dot(tri, t, preferred_element_type=F32)
    return dot(hi) + dot(mid) + dot(lo)


def _forget_cumsum(f, b_forget_p):
    def body(f_ref, b_ref, cum_ref, carry):
        i = pl.program_id(0)

        @pl.when(i == 0)
        def _():
            carry[...] = jnp.zeros_like(carry)

        z = f_ref[...] + b_ref[...]
        ls = jnp.minimum(z, 0.0) - jnp.log(1.0 + jnp.exp(-jnp.abs(z)))
        rows = i * BLK + lax.broadcasted_iota(jnp.int32, (BLK, BLK), 0)
        ls = jnp.where(rows >= PAD_ROWS, ls, 0.0)
        r = lax.broadcasted_iota(jnp.int32, (BLK, BLK), 0)
        c = lax.broadcasted_iota(jnp.int32, (BLK, BLK), 1)
        tri = (c <= r).astype(BF16)
        cum = _tri_matmul(tri, ls) + carry[...]
        cum_ref[...] = cum
        carry[...] = cum[BLK - 1:BLK, :]

    return pl.pallas_call(
        body, grid=(NBLK,),
        in_specs=[pl.BlockSpec((BLK, BLK), lambda i: (i, 0)), pl.BlockSpec((1, BLK), lambda i: (0, 0))],
        out_specs=pl.BlockSpec((BLK, BLK), lambda i: (i, 0)),
        out_shape=SDS((LP, BLK), F32),
        scratch_shapes=[pltpu.VMEM((1, BLK), F32)],
        compiler_params=_cparams(("arbitrary",)), name="forget_cumsum")(f, b_forget_p)


def _forget_cumsum_bwd(dcum, f, b_forget_p):
    def body(d_ref, f_ref, b_ref, df_ref, db_ref, carry):
        i = pl.program_id(0)

        @pl.when(i == 0)
        def _():
            carry[...] = jnp.zeros_like(carry)
            db_ref[...] = jnp.zeros_like(db_ref)

        blk = NBLK - 1 - i
        r = lax.broadcasted_iota(jnp.int32, (BLK, BLK), 0)
        c = lax.broadcasted_iota(jnp.int32, (BLK, BLK), 1)
        tri = (c >= r).astype(BF16)
        d = d_ref[...]
        dls = _tri_matmul(tri, d) + carry[...]
        carry[...] = dls[0:1, :]
        z = f_ref[...] + b_ref[...]
        rows = blk * BLK + r
        df = jnp.where(rows >= PAD_ROWS, dls / (1.0 + jnp.exp(z)), 0.0)
        df_ref[...] = df
        db_ref[...] += jnp.sum(df, axis=0, keepdims=True)

    rev = pl.BlockSpec((BLK, BLK), lambda i: (NBLK - 1 - i, 0))
    vec = pl.BlockSpec((1, BLK), lambda i: (0, 0))
    return pl.pallas_call(
        body, grid=(NBLK,),
        in_specs=[rev, rev, vec],
        out_specs=[rev, vec],
        out_shape=[SDS((LP, BLK), F32), SDS((1, BLK), F32)],
        scratch_shapes=[pltpu.VMEM((1, BLK), F32)],
        compiler_params=_cparams(("arbitrary",)), name="forget_cumsum_bwd")(dcum, f, b_forget_p)


def _lane_half(rows):
    return lax.broadcasted_iota(jnp.int32, (rows, BLK), 1) // HALF


def _fox_valid(qi, kj):
    qrow = qi * TM + lax.broadcasted_iota(jnp.int32, (TM, TM), 0)
    krow = kj * TM + lax.broadcasted_iota(jnp.int32, (TM, TM), 1)
    return (krow <= qrow) & ((krow >= PAD_ROWS) | (qrow < PAD_ROWS))


class _Rider:
    def __init__(self, operands, out_shapes, sem_counts, first, middle, last):
        self.operands, self.out_shapes, self.sem_counts = list(operands), list(out_shapes), list(sem_counts)
        self.first, self.middle, self.last = first, middle, last

    def scratch(self):
        return [pltpu.SemaphoreType.DMA((k,)) for k in self.sem_counts]

    def split(self, refs, n_in, n_out, n_scratch):
        a, b = len(self.operands), len(self.out_shapes)
        ins, mine_in = refs[:n_in], refs[n_in:n_in + a]
        outs, mine_out = refs[n_in + a:n_in + a + n_out], refs[n_in + a + n_out:n_in + a + n_out + b]
        rest = refs[n_in + a + n_out + b:]
        return ins, outs, rest[:n_scratch], (mine_in, mine_out, rest[n_scratch:])

    def at_steps(self, mine, is_first, is_middle, is_last):
        for cond, fn in ((is_first, self.first), (is_middle, self.middle), (is_last, self.last)):
            pl.when(cond)(lambda fn=fn: fn(*mine))


HBM_SPEC = pl.BlockSpec(memory_space=pltpu.HBM)


N_AUG = 4
QCH = 128
KSUB = 384


def _fox_prep(proj, cum):
    def body(q_ref, k_ref, v_ref, c_ref, qa_ref, ka_ref, vm_ref, kt_ref, vt_ref):
        half = _lane_half(BLK)
        lane = lax.broadcasted_iota(jnp.int32, (BLK, BLK), 1)
        for pp in range(4):
            cols = slice(pp * BLK, (pp + 1) * BLK)
            qs = q_ref[:, cols].astype(F32) * SCALE
            kp = k_ref[:, cols].astype(F32)
            vp = v_ref[:, cols]
            vt_ref[cols, :] = vp.astype(F32).T.astype(BF16)
            for e in range(2):
                h = 2 * pp + e
                a = (1 - e) * HALF
                blk = slice(h * BLK, (h + 1) * BLK)
                hi, mid, lo = _split3(-c_ref[:, h:h + 1])
                q_aug = jnp.where(half == e, qs, jnp.where((lane >= a) & (lane < a + 3), 1.0, 0.0))
                k_aug = jnp.where(half == e, kp, jnp.where(
                    lane == a, hi.astype(F32), jnp.where(lane == a + 1, mid.astype(F32), jnp.where(
                        lane == a + 2, lo.astype(F32), jnp.where(lane == a + 3, 1.0, 0.0)))))
                qa_ref[:, blk] = q_aug.astype(BF16)
                ka_ref[:, blk] = k_aug.astype(BF16)
                kt_ref[blk, :] = k_aug.T.astype(BF16)
                vm_ref[:, blk] = jnp.where(half == e, vp, jnp.zeros_like(vp))

    row = lambda blk: pl.BlockSpec((BLK, 512), lambda i: (i, blk))
    wide = pl.BlockSpec((BLK, 1024), lambda i: (i, 0))
    return pl.pallas_call(
        body, grid=(NBLK,),
        in_specs=[row(QB), row(KB), row(VB), pl.BlockSpec((BLK, BLK), lambda i: (i, 0))],
        out_specs=[wide, wide, wide, pl.BlockSpec((1024, BLK), lambda i: (0, i)),
                   pl.BlockSpec((512, BLK), lambda i: (0, i))],
        out_shape=[SDS((LP, 1024), BF16)] * 3 + [SDS((1024, LP), BF16), SDS((512, LP), BF16)],
        compiler_params=_cparams(("parallel",)), name="fox_prep")(proj, proj, proj, cum)


def _over_keys(reduce, x):
    slabs = x.reshape(x.shape[0] // HALF, HALF, x.shape[1])
    return reduce(reduce(slabs, axis=0), axis=0, keepdims=True)


def _fox_valid_t(qi, kj, c, r):
    krow = kj * TM + r * KSUB + lax.broadcasted_iota(jnp.int32, (KSUB, QCH), 0)
    qrow = qi * TM + c * QCH + lax.broadcasted_iota(jnp.int32, (KSUB, QCH), 1)
    return (krow <= qrow) & ((krow >= PAD_ROWS) | (qrow < PAD_ROWS))


def _fox_fwd(q_aug, k_aug, v_t, rider):
    def body(*refs):
        (q_ref, k_ref, vt_ref), (o_ref, lse_ref), (m_s, l_s, acc_s), mine = rider.split(refs, 3, 2, 3)
        qi = pl.program_id(0)
        kj = pl.program_id(1)
        rider.at_steps(mine, (qi == 0) & (kj == 0), (qi == NT // 2) & (kj == 0), (qi == NT - 1) & (kj == NT - 1))

        @pl.when(kj == 0)
        def _():
            m_s[...] = jnp.full_like(m_s, NEG)
            l_s[...] = jnp.zeros_like(l_s)
            acc_s[...] = jnp.zeros_like(acc_s)

        def tile(masked):
            steps = [(h, c, r) for h in range(N_HEADS) for c in range(TM // QCH) for r in range(TM // KSUB)]

            def scores(h, c, r):
                blk = slice(h * BLK, (h + 1) * BLK)
                return lax.dot_general(k_ref[r * KSUB:(r + 1) * KSUB, blk], q_ref[c * QCH:(c + 1) * QCH, blk],
                                       NT_DIMS, preferred_element_type=F32)

            ahead = scores(*steps[0])
            for n, (h, c, r) in enumerate(steps):
                s_t = ahead
                if n + 1 < len(steps):
                    ahead = scores(*steps[n + 1])
                cs = slice(c * QCH, (c + 1) * QCH)
                if masked:
                    s_t = jnp.where(_fox_valid_t(qi, kj, c, r), s_t, NEG)
                m_prev = m_s[h, :, cs]
                m_new = jnp.maximum(m_prev, _over_keys(jnp.max, s_t))
                p_t = jnp.exp(s_t - m_new)
                alpha = jnp.exp(m_prev - m_new)
                l_s[h, :, cs] = alpha * l_s[h, :, cs] + _over_keys(jnp.sum, p_t)
                m_s[h, :, cs] = m_new
                vt = vt_ref[h * HALF:(h + 1) * HALF, r * KSUB:(r + 1) * KSUB]
                acc_s[h, :, cs] = acc_s[h, :, cs] * alpha + jnp.dot(vt, p_t.astype(BF16),
                                                                    preferred_element_type=F32)

        @pl.when((kj < qi) & (kj > 0))
        def _():
            tile(False)

        @pl.when((kj <= qi) & ((kj == qi) | (kj == 0)))
        def _():
            tile(True)

        @pl.when(kj == qi)
        def _():
            for pp in range(4):
                both = jnp.concatenate([acc_s[2 * pp] * (1.0 / l_s[2 * pp]),
                                        acc_s[2 * pp + 1] * (1.0 / l_s[2 * pp + 1])], axis=0)
                o_ref[:, pp * BLK:(pp + 1) * BLK] = both.T.astype(BF16)
            for h in range(N_HEADS):
                lse_ref[h] = m_s[h] + jnp.log(l_s[h])

    o_b, lse, *carried = pl.pallas_call(
        body, grid=(NT, NT),
        in_specs=[pl.BlockSpec((TM, 1024), lambda qi, kj: (qi, 0)),
                  pl.BlockSpec((TM, 1024), lambda qi, kj: (jnp.minimum(kj, qi), 0)),
                  pl.BlockSpec((512, TM), lambda qi, kj: (0, jnp.minimum(kj, qi)))]
        + [HBM_SPEC] * len(rider.operands),
        out_specs=[pl.BlockSpec((TM, 512), lambda qi, kj: (qi, 0)),
                   pl.BlockSpec((N_HEADS, 1, TM), lambda qi, kj: (0, 0, qi))] + [HBM_SPEC] * len(rider.out_shapes),
        out_shape=[SDS((LP, 512), BF16), SDS((N_HEADS, 1, LP), F32)] + rider.out_shapes,
        scratch_shapes=[pltpu.VMEM((N_HEADS, 1, TM), F32), pltpu.VMEM((N_HEADS, 1, TM), F32),
                        pltpu.VMEM((N_HEADS, HALF, TM), F32)] + rider.scratch(),
        compiler_params=_cparams(("arbitrary", "arbitrary")), name="fox_fwd",
    )(q_aug, k_aug, v_t, *rider.operands)
    return o_b, lse, carried


def _fox_bwd(proj, o_b, dmix, lse, ck_t, rider):
    def body(*refs):
        ((q_ref, k_ref, v_ref, o_ref, do_ref, lse_ref, ck_ref), (dq_ref, dk_ref, dv_ref, dck_ref, dcq_ref),
         (dk_s, dv_s, dck_s), mine) = rider.split(refs, 7, 5, 3)
        kj = pl.program_id(0)
        qi = pl.program_id(1)
        rider.at_steps(mine, (kj == 0) & (qi == 0), (kj == NT // 2) & (qi == 0), (kj == NT - 1) & (qi == NT - 1))

        @pl.when((kj == 0) & (qi == 0))
        def _():
            dq_ref[...] = jnp.zeros_like(dq_ref)
            dcq_ref[...] = jnp.zeros_like(dcq_ref)

        @pl.when(qi == 0)
        def _():
            dk_s[...] = jnp.zeros_like(dk_s)
            dv_s[...] = jnp.zeros_like(dv_s)
            dck_s[...] = jnp.zeros_like(dck_s)

        def tile(masked):
            valid = _fox_valid(qi, kj) if masked else None
            half = _lane_half(TM)
            q0 = pl.multiple_of(qi * TM, TM)
            lane = lax.broadcasted_iota(jnp.int32, (TM, BLK), 1)
            row_sums = jnp.zeros((TM, BLK), F32)
            for pp in range(4):
                cols = slice(pp * BLK, (pp + 1) * BLK)
                qs = (q_ref[:, cols].astype(F32) * SCALE).astype(BF16)
                kp = k_ref[:, cols]
                vp = v_ref[:, cols]
                dop = do_ref[:, cols]
                prod = dop.astype(F32) * o_ref[:, cols].astype(F32)
                d0 = jnp.sum(jnp.where(half == 0, prod, 0.0), axis=1, keepdims=True)
                d1 = jnp.sum(prod, axis=1, keepdims=True) - d0
                dq = jnp.zeros((TM, BLK), F32)
                dks, dvs = [], []
                for e in range(2):
                    h = 2 * pp + e
                    ke = jnp.where(half == e, kp, jnp.zeros_like(kp))
                    ve = jnp.where(half == e, vp, jnp.zeros_like(vp))
                    t = lax.dot_general(qs, ke, NT_DIMS, preferred_element_type=F32) - ck_ref[h] - lse_ref[h]
                    if masked:
                        t = jnp.where(valid, t, NEG)
                    p = jnp.exp(t)
                    dp = lax.dot_general(dop, ve, NT_DIMS, preferred_element_type=F32)
                    ds = p * (dp - (d0 if e == 0 else d1))
                    dck_s[h] += jnp.sum(ds, axis=0, keepdims=True)
                    row_sums = jnp.where(lane == h, jnp.sum(ds, axis=1, keepdims=True), row_sums)
                    ds_b = ds.astype(BF16)
                    dq = dq + jnp.dot(ds_b, ke, preferred_element_type=F32)
                    dks.append(lax.dot_general(ds_b, qs, TN_DIMS, preferred_element_type=F32))
                    dvs.append(lax.dot_general(p.astype(BF16), dop, TN_DIMS, preferred_element_type=F32))
                dq_ref[pl.ds(q0, TM), cols] += dq
                dk_s[pp] += jnp.where(half == 0, dks[0], dks[1])
                dv_s[pp] += jnp.where(half == 0, dvs[0], dvs[1])
            dcq_ref[pl.ds(q0, TM), :] += row_sums

        @pl.when((qi > kj) & (kj > 0))
        def _():
            tile(False)

        @pl.when((qi >= kj) & ((qi == kj) | (kj == 0)))
        def _():
            tile(True)

        @pl.when(qi == NT - 1)
        def _():
            for pp in range(4):
                cols = slice(pp * BLK, (pp + 1) * BLK)
                dk_ref[:, cols] = dk_s[pp].astype(BF16)
                dv_ref[:, cols] = dv_s[pp].astype(BF16)
            dck_ref[...] = dck_s[...]

    qrow = lambda blk: pl.BlockSpec((TM, 512), lambda kj, qi: (jnp.maximum(qi, kj), blk))
    krow = lambda blk: pl.BlockSpec((TM, 512), lambda kj, qi: (kj, blk))
    dq, dk, dv, dck, dcq, *carried = pl.pallas_call(
        body, grid=(NT, NT),
        in_specs=[qrow(QB), krow(KB), krow(VB), qrow(0), qrow(1),
                  pl.BlockSpec((N_HEADS, TM, 1), lambda kj, qi: (0, jnp.maximum(qi, kj), 0)),
                  pl.BlockSpec((N_HEADS, 1, TM), lambda kj, qi: (0, 0, kj))] + [HBM_SPEC] * len(rider.operands),
        out_specs=[pl.BlockSpec((LP, 512), lambda kj, qi: (0, 0)),
                   pl.BlockSpec((TM, 512), lambda kj, qi: (kj, 0)),
                   pl.BlockSpec((TM, 512), lambda kj, qi: (kj, 0)),
                   pl.BlockSpec((N_HEADS, 1, TM), lambda kj, qi: (0, 0, kj)),
                   pl.BlockSpec((LP, BLK), lambda kj, qi: (0, 0))] + [HBM_SPEC] * len(rider.out_shapes),
        out_shape=[SDS((LP, 512), F32), SDS((LP, 512), BF16), SDS((LP, 512), BF16), SDS((N_HEADS, 1, LP), F32),
                   SDS((LP, BLK), F32)] + rider.out_shapes,
        scratch_shapes=[pltpu.VMEM((4, TM, BLK), F32), pltpu.VMEM((4, TM, BLK), F32),
                        pltpu.VMEM((N_HEADS, 1, TM), F32)] + rider.scratch(),
        compiler_params=_cparams(("arbitrary", "arbitrary")), name="fox_bwd",
    )(proj, proj, proj, o_b, dmix, lse, ck_t, *rider.operands)
    return dq, dk, dv, dck, dcq, carried


def _build_bias(tab_ref, bkt_ref, bias_cur, bias_prev):
    bc = bkt_ref[0]
    bp = bkt_ref[1]
    for h in range(N_HEADS):
        def step(b, carry, h=h):
            t = tab_ref[b, h]
            return jnp.where(bc == b, t, carry[0]), jnp.where(bp == b, t, carry[1])
        zero = jnp.zeros((BLK, BLK), F32)
        cur, prev = lax.fori_loop(0, N_BUCKETS, step, (zero, zero))
        bias_cur[h] = cur
        bias_prev[h] = prev


def _kv_variants(ref):
    x = ref[...].astype(F32)
    swapped = pltpu.roll(x, HALF, 1)
    half = _lane_half(BLK)
    out = {}
    for e in range(2):
        for g in range(2):
            src = x if e == g else swapped
            out[(e, g)] = jnp.where(half == e, src, 0.0).astype(BF16)
    return out


def _swa_masks(i):
    qrow = lax.broadcasted_iota(jnp.int32, (BLK, BLK), 0)
    kcol = lax.broadcasted_iota(jnp.int32, (BLK, BLK), 1)
    big = 4 * BLK
    valid_cur = (kcol <= qrow) & (kcol >= jnp.where(i > 0, 0, PAD_ROWS))
    valid_prev = kcol > qrow + jnp.where(i >= 2, 0, big)
    valid_meta = kcol >= jnp.where(i >= 1, PAD_ROWS, big)
    return valid_cur, valid_prev, valid_meta


def _swa_scores(i, h, qs, kvar, bias_cur, bias_prev, tab_ref, masks):
    e, g = h % 2, h // 4
    first = jnp.full((BLK, BLK), i, jnp.int32) == 1
    biases = (bias_cur[h], bias_prev[h], jnp.where(first, bias_prev[h], tab_ref[N_BUCKETS - 1, h]))
    out = []
    for x in range(3):
        s = lax.dot_general(qs, kvar[x][(e, g)], NT_DIMS, preferred_element_type=F32) + biases[x]
        out.append(jnp.where(masks[x], s, NEG))
    return out


def _swa_fwd(proj, rel_bias, sinks, bkt):
    def body(tab_ref, sink_ref, bkt_ref, q_ref, kc_ref, kp_ref, km_ref, vc_ref, vp_ref, vm_ref,
             o_ref, lse_ref, bias_cur, bias_prev):
        i = pl.program_id(0)

        @pl.when(i == 0)
        def _():
            _build_bias(tab_ref, bkt_ref, bias_cur, bias_prev)

        masks = _swa_masks(i)
        kvar = [_kv_variants(r) for r in (kc_ref, kp_ref, km_ref)]
        vvar = [_kv_variants(r) for r in (vc_ref, vp_ref, vm_ref)]
        for pp in range(4):
            cols = slice(pp * BLK, (pp + 1) * BLK)
            qs = (q_ref[:, cols].astype(F32) * SCALE).astype(BF16)
            o_pair = jnp.zeros((BLK, BLK), F32)
            for e in range(2):
                h = 2 * pp + e
                g = h // 4
                s = _swa_scores(i, h, qs, kvar, bias_cur, bias_prev, tab_ref, masks)
                sink = sink_ref[0, h]
                m = jnp.maximum(jnp.maximum(jnp.max(s[0], axis=1, keepdims=True), jnp.max(s[1], axis=1, keepdims=True)),
                                jnp.maximum(jnp.max(s[2], axis=1, keepdims=True), sink))
                p = [jnp.exp(sx - m) for sx in s]
                denom = (jnp.sum(p[0], axis=1, keepdims=True) + jnp.sum(p[1], axis=1, keepdims=True)
                         + jnp.sum(p[2], axis=1, keepdims=True) + jnp.exp(sink - m))
                pv = (jnp.dot(p[0].astype(BF16), vvar[0][(e, g)], preferred_element_type=F32)
                      + jnp.dot(p[1].astype(BF16), vvar[1][(e, g)], preferred_element_type=F32)
                      + jnp.dot(p[2].astype(BF16), vvar[2][(e, g)], preferred_element_type=F32))
                o_pair = o_pair + pv * (1.0 / denom)
                lse_ref[h] = m + jnp.log(denom)
            o_ref[:, cols] = o_pair.astype(BF16)

    smem = pl.BlockSpec(memory_space=pltpu.SMEM)
    kv = lambda col, which: pl.BlockSpec(
        (BLK, BLK), {0: lambda i: (i, col), 1: lambda i: (jnp.maximum(i - 1, 0), col), 2: lambda i: (0, col)}[which])
    return pl.pallas_call(
        body, grid=(NBLK,),
        in_specs=[smem, smem, pl.BlockSpec((2, BLK, BLK), lambda i: (0, 0, 0)),
                  pl.BlockSpec((BLK, 512), lambda i: (i, QA)),
                  kv(KA, 0), kv(KA, 1), kv(KA, 2), kv(VA, 0), kv(VA, 1), kv(VA, 2)],
        out_specs=[pl.BlockSpec((BLK, 512), lambda i: (i, 0)),
                   pl.BlockSpec((N_HEADS, BLK, 1), lambda i: (0, i, 0))],
        out_shape=[SDS((LP, 512), BF16), SDS((N_HEADS, LP, 1), F32)],
        scratch_shapes=[pltpu.VMEM((N_HEADS, BLK, BLK), F32), pltpu.VMEM((N_HEADS, BLK, BLK), F32)],
        compiler_params=_cparams(("arbitrary",)), name="swa_fwd",
    )(rel_bias, sinks, bkt, proj, proj, proj, proj, proj, proj, proj)


def _swa_bwd(proj, o_a, dmix, lse, rel_bias, sinks, bkt):
    def body(tab_ref, sink_ref, bkt_ref, q_ref, kc_ref, kp_ref, km_ref, vc_ref, vp_ref, vm_ref,
             o_ref, do_ref, lse_ref, dq_ref, dk_ref, dv_ref, dbias_ref, dsink_ref,
             bias_cur, bias_prev, acc_cur, acc_prev, acc_far, dsk):
        i = pl.program_id(0)

        @pl.when(i == 0)
        def _():
            _build_bias(tab_ref, bkt_ref, bias_cur, bias_prev)
            dk_ref[...] = jnp.zeros_like(dk_ref)
            dv_ref[...] = jnp.zeros_like(dv_ref)
            acc_cur[...] = jnp.zeros_like(acc_cur)
            acc_prev[...] = jnp.zeros_like(acc_prev)
            acc_far[...] = jnp.zeros_like(acc_far)
            dsk[...] = jnp.zeros_like(dsk)

        masks = _swa_masks(i)
        half = _lane_half(BLK)
        first = jnp.full((BLK, BLK), i, jnp.int32) == 1
        kvar = [_kv_variants(r) for r in (kc_ref, kp_ref, km_ref)]
        vvar = [_kv_variants(r) for r in (vc_ref, vp_ref, vm_ref)]
        dk_blk = [jnp.zeros((BLK, BLK), F32) for _ in range(3)]
        dv_blk = [jnp.zeros((BLK, BLK), F32) for _ in range(3)]
        for pp in range(4):
            cols = slice(pp * BLK, (pp + 1) * BLK)
            qs = (q_ref[:, cols].astype(F32) * SCALE).astype(BF16)
            dop = do_ref[:, cols]
            prod = dop.astype(F32) * o_ref[:, cols].astype(F32)
            d0 = jnp.sum(jnp.where(half == 0, prod, 0.0), axis=1, keepdims=True)
            d1 = jnp.sum(prod, axis=1, keepdims=True) - d0
            dq = jnp.zeros((BLK, BLK), F32)
            for e in range(2):
                h = 2 * pp + e
                g = h // 4
                dd = d0 if e == 0 else d1
                lse_h = lse_ref[h]
                s = _swa_scores(i, h, qs, kvar, bias_cur, bias_prev, tab_ref, masks)
                dsk[h] += -jnp.exp(sink_ref[0, h] - lse_h) * dd
                ds_all = []
                for x in range(3):
                    p = jnp.exp(s[x] - lse_h)
                    dp = lax.dot_general(dop, vvar[x][(e, g)], NT_DIMS, preferred_element_type=F32)
                    ds = p * (dp - dd)
                    ds_all.append(ds)
                    ds_b = ds.astype(BF16)
                    dq = dq + jnp.dot(ds_b, kvar[x][(e, g)], preferred_element_type=F32)
                    dkv = lax.dot_general(ds_b, qs, TN_DIMS, preferred_element_type=F32)
                    dvv = lax.dot_general(p.astype(BF16), dop, TN_DIMS, preferred_element_type=F32)
                    if e != g:
                        dkv = pltpu.roll(dkv, HALF, 1)
                        dvv = pltpu.roll(dvv, HALF, 1)
                    dk_blk[x] = dk_blk[x] + jnp.where(half == g, dkv, 0.0)
                    dv_blk[x] = dv_blk[x] + jnp.where(half == g, dvv, 0.0)
                acc_cur[h] += ds_all[0]
                acc_prev[h] += ds_all[1] + jnp.where(first, ds_all[2], 0.0)
                acc_far[h] += jnp.where(first, 0.0, ds_all[2])
            dq_ref[:, cols] = (dq * SCALE).astype(BF16)

        cur0 = pl.multiple_of(i * BLK, BLK)
        prev0 = pl.multiple_of(jnp.maximum(i - 1, 0) * BLK, BLK)
        dk_ref[pl.ds(cur0, BLK), :] += dk_blk[0]
        dv_ref[pl.ds(cur0, BLK), :] += dv_blk[0]
        dk_ref[pl.ds(prev0, BLK), :] += dk_blk[1]
        dv_ref[pl.ds(prev0, BLK), :] += dv_blk[1]
        dk_ref[0:BLK, :] += dk_blk[2]
        dv_ref[0:BLK, :] += dv_blk[2]

        @pl.when(i == NBLK - 1)
        def _():
            bc = bkt_ref[0]
            bp = bkt_ref[1]
            lane = lax.broadcasted_iota(jnp.int32, (1, BLK), 1)

            def per_bucket(b, carry):
                row = jnp.zeros((1, BLK), F32)
                for h in range(N_HEADS):
                    val = (jnp.sum(jnp.where(bc == b, acc_cur[h], 0.0), keepdims=True)
                           + jnp.sum(jnp.where(bp == b, acc_prev[h], 0.0), keepdims=True))
                    row = jnp.where(lane == h, val, row)
                dbias_ref[pl.ds(b, 1), :] = row
                return carry

            lax.fori_loop(0, N_BUCKETS, per_bucket, 0)
            far = jnp.zeros((1, BLK), F32)
            dsr = jnp.zeros((1, BLK), F32)
            for h in range(N_HEADS):
                far = jnp.where(lane == h, jnp.sum(acc_far[h], keepdims=True), far)
                dsr = jnp.where(lane == h, jnp.sum(dsk[h], keepdims=True), dsr)
            dbias_ref[N_BUCKETS - 1:N_BUCKETS, :] += far
            dsink_ref[...] = dsr

    smem = pl.BlockSpec(memory_space=pltpu.SMEM)
    kv = lambda col, which: pl.BlockSpec(
        (BLK, BLK), {0: lambda i: (i, col), 1: lambda i: (jnp.maximum(i - 1, 0), col), 2: lambda i: (0, col)}[which])
    blk512 = lambda col: pl.BlockSpec((BLK, 512), lambda i: (i, col))
    full = lambda r, c: pl.BlockSpec((r, c), lambda i: (0, 0))
    acc = pltpu.VMEM((N_HEADS, BLK, BLK), F32)
    return pl.pallas_call(
        body, grid=(NBLK,),
        in_specs=[smem, smem, pl.BlockSpec((2, BLK, BLK), lambda i: (0, 0, 0)), blk512(QA),
                  kv(KA, 0), kv(KA, 1), kv(KA, 2), kv(VA, 0), kv(VA, 1), kv(VA, 2),
                  blk512(0), blk512(0), pl.BlockSpec((N_HEADS, BLK, 1), lambda i: (0, i, 0))],
        out_specs=[blk512(0), full(LP, BLK), full(LP, BLK), full(N_BUCKETS, BLK), full(1, BLK)],
        out_shape=[SDS((LP, 512), BF16), SDS((LP, BLK), F32), SDS((LP, BLK), F32),
                   SDS((N_BUCKETS, BLK), F32), SDS((1, BLK), F32)],
        scratch_shapes=[acc, acc, acc, acc, acc, pltpu.VMEM((N_HEADS, BLK, 1), F32)],
        compiler_params=_cparams(("arbitrary",)), name="swa_bwd",
    )(rel_bias, sinks, bkt, proj, proj, proj, proj, proj, proj, proj, o_a, dmix, lse)


def _local_step(x, tgt, meta, rel_bias, g_pre_mix, g_post_mix, g_pre_ffn, g_post_ffn, b_forget, sinks,
                w_in_b, w_out_b, ffn_rider, ffn_weights, early_grads):
    bkt = jnp.asarray(_bucket_tables())
    h0 = jnp.concatenate([jnp.zeros((PAD_ROWS, D_MODEL), F32), meta, x], axis=0)
    tgt_p = jnp.concatenate([jnp.zeros((ROW0, D_MODEL), F32), tgt], axis=0)
    b_p = jnp.pad(b_forget, ((0, 0), (0, BLK - N_HEADS)))

    hn1, proj, f = _pre_mix(h0, g_pre_mix, w_in_b)
    o_a, lse_a = _swa_fwd(proj, rel_bias, sinks, bkt)
    cum = _forget_cumsum(f, b_p)
    ck_t = cum[:, :N_HEADS].T.reshape(N_HEADS, 1, LP)
    q_aug, k_aug, v_m, k_t, v_t = _fox_prep(proj, cum)
    o_b, lse_row, carried = _fox_fwd(q_aug, k_aug, v_t, ffn_rider)
    lse_b = lse_row.reshape(N_HEADS, LP, 1)
    w_gu_b, w_dn_b = ffn_weights(carried)
    a, h1, hn2 = _attn_out(o_a, o_b, w_out_b, h0, g_post_mix, g_pre_ffn)
    g, u, act = _ffn_up(hn2, w_gu_b)
    dff, dy, loss_blk, dg_post_ffn = _ffn_down_loss(act, w_dn_b, h1, tgt_p, g_post_ffn)

    dw_dn = _mm_tn(act, dff, FF_T, "dw_down")
    dg, du = _ffn_down_bwd(dff, w_dn_b, g, u)
    dw_gu = _dw_gate_up(hn2, dg, du)
    dh1, da, dg_pre_ffn, dg_post_mix = _ffn_up_bwd(dg, du, w_gu_b, h1, a, dy, g_pre_ffn, g_post_mix)
    dw_out = jnp.concatenate([_mm_tn(o_a, da, 512, "dw_out_a"), _mm_tn(o_b, da, 512, "dw_out_b")], axis=0)
    dmix = _attn_out_bwd(da, w_out_b)
    dq_b, dk_b, dv_b, dck, dcq, landed = _fox_bwd(proj, o_b, dmix, lse_b, ck_t, early_grads(dw_gu, dw_dn, dw_out))
    dq_a, dk_a, dv_a, dbias, dsink = _swa_bwd(proj, o_a, dmix, lse_a, rel_bias, sinks, bkt)
    dcum = dcq - jnp.pad(dck.reshape(N_HEADS, LP).T, ((0, 0), (0, BLK - N_HEADS)))
    df, db = _forget_cumsum_bwd(dcum, f, b_p)
    dproj, dh0, dg_pre_mix = _pre_mix_bwd(dq_a, dq_b, dk_b, dv_b, dk_a, dv_a, df, w_in_b, h0, dh1, g_pre_mix)
    dw_in = _mm_tn(hn1, dproj, 512, "dw_in")

    return dict(loss=loss_blk[0, 0], grad_x=dh0[ROW0:], meta=dh0[PAD_ROWS:ROW0],
                rel_bias=dbias[:, :N_HEADS], ln_pre_mix=dg_pre_mix, ln_post_mix=dg_post_mix,
                ln_pre_ffn=dg_pre_ffn, ln_post_ffn=dg_post_ffn, b_forget=db[:, :N_HEADS],
                sinks=dsink[:, :N_HEADS], w_in=dw_in, w_out=dw_out, w_gate_up=dw_gu, w_down=dw_dn,
                landed=landed)


N_SMALL = 24
LOSS_ROW = 6


def _place():
    x, y, c = lax.axis_index("x"), lax.axis_index("y"), lax.axis_index("c")
    return x, y, c, [(1 - x, y), (x, 1 - y), (1 - x, 1 - y)]


def _run_alone(rider, name):
    a, b = len(rider.operands), len(rider.out_shapes)

    def body(*refs):
        mine = (refs[:a], refs[a:a + b], refs[a + b:])
        rider.first(*mine)
        rider.middle(*mine)
        rider.last(*mine)

    return pl.pallas_call(body, in_specs=[HBM_SPEC] * a, out_specs=[HBM_SPEC] * b, out_shape=rider.out_shapes,
                          scratch_shapes=rider.scratch(), name=name)(*rider.operands)


def _gather_rider(shards):
    n = len(shards)

    def copies(ins, outs, sems):
        send_sems, recv_sems = sems
        x, y, c, others = _place()
        chip = 2 * x + y
        sibling = (x, y, 1 - c)

        def rc(a, k, src, dst, to):
            return pltpu.make_async_remote_copy(src_ref=src, dst_ref=dst, send_sem=send_sems.at[6 * a + k],
                                                recv_sem=recv_sems.at[6 * a + k], device_id=to, device_id_type=MESH)

        pairs = [(a, j, ox, oy) for a in range(n) for j, (ox, oy) in enumerate(others)]
        sent = [rc(a, j, ins[a].at[c], outs[a].at[chip, c], (ox, oy, c)) for a, j, ox, oy in pairs]
        landed = [rc(a, j, outs[a].at[2 * ox + oy, c], outs[a].at[2 * ox + oy, c], sibling) for a, j, ox, oy in pairs]
        passed = [rc(a, 3 + j, outs[a].at[2 * ox + oy, c], outs[a].at[2 * ox + oy, c], sibling)
                  for a, j, ox, oy in pairs]
        arriving = [rc(a, 3 + j, outs[a].at[2 * ox + oy, 1 - c], outs[a].at[2 * ox + oy, 1 - c], sibling)
                    for a, j, ox, oy in pairs]
        return sent, landed, passed, arriving

    def first(*mine):
        for cp in copies(*mine)[0]:
            cp.start()

    def middle(*mine):
        _, landed, passed, _ = copies(*mine)
        for got, cp in zip(landed, passed):
            got.wait_recv()
            cp.start()

    def last(*mine):
        sent, _, passed, arriving = copies(*mine)
        for cp in arriving:
            cp.wait_recv()
        for cp in sent + passed:
            cp.wait_send()

    return _Rider(shards, [SDS((4,) + s.shape, s.dtype) for s in shards], [6 * n, 6 * n], first, middle, last)


def _swap_halves(grads, name):
    n = len(grads)

    def body(*refs):
        ins, outs = refs[:n], refs[n:2 * n]
        send_sems, recv_sems = refs[2 * n:]
        x, y, c, _ = _place()
        copies = [pltpu.make_async_remote_copy(
            src_ref=ins[a].at[s, 1 - c], dst_ref=outs[a].at[s], send_sem=send_sems.at[4 * a + s],
            recv_sem=recv_sems.at[4 * a + s], device_id=(x, y, 1 - c), device_id_type=MESH)
            for a in range(n) for s in range(4)]
        for cp in copies:
            cp.start()
        for cp in copies:
            cp.wait()

    return pl.pallas_call(
        body, in_specs=[HBM_SPEC] * n, out_specs=[HBM_SPEC] * n,
        out_shape=[SDS((4,) + g.shape[2:], g.dtype) for g in grads],
        scratch_shapes=[pltpu.SemaphoreType.DMA((4 * n,)), pltpu.SemaphoreType.DMA((4 * n,))],
        name=name)(*grads)


def _pair_sum(g, got, c_arr, name):
    rh, cc = got.shape[1:]

    def body(c_ref, g_ref, p_ref, o_ref):
        o_ref[0] = (g_ref[0, 0] + p_ref[0]).astype(BF16)

    grid_spec = pltpu.PrefetchScalarGridSpec(
        num_scalar_prefetch=1, grid=(4,),
        in_specs=[pl.BlockSpec((1, 1, rh, cc), lambda s, c_ref: (s, c_ref[0], 0, 0)),
                  pl.BlockSpec((1, rh, cc), lambda s, c_ref: (s, 0, 0))],
        out_specs=pl.BlockSpec((1, rh, cc), lambda s, c_ref: (s, 0, 0)))
    return pl.pallas_call(body, grid_spec=grid_spec, out_shape=SDS((4, rh, cc), BF16),
                          compiler_params=_cparams(("parallel",)), name=name)(c_arr, g, got)


def _exchange_rider(parts, small=None):
    n = len(parts)

    def copies(ins, outs, sems):
        x, y, c, others = _place()
        out = [pltpu.make_async_remote_copy(
            src_ref=ins[a].at[2 * ox + oy], dst_ref=outs[a].at[j], send_sem=sems[0].at[3 * a + j],
            recv_sem=sems[1].at[3 * a + j], device_id=(ox, oy, c), device_id_type=MESH)
            for a in range(n) for j, (ox, oy) in enumerate(others)]
        own = []
        if small is not None:
            me = 4 * x + 2 * y + c
            peers = [(x, y, 1 - c)] + [(ox, oy, c) for ox, oy in others] + [(ox, oy, 1 - c) for ox, oy in others]
            out += [pltpu.make_async_remote_copy(
                src_ref=ins[n], dst_ref=outs[n].at[me], send_sem=sems[2].at[k], recv_sem=sems[3].at[k],
                device_id=peer, device_id_type=MESH) for k, peer in enumerate(peers)]
            own = [pltpu.make_async_copy(ins[n], outs[n].at[me], sems[4].at[0])]
        return out, own

    def first(*mine):
        out, own = copies(*mine)
        for cp in own + out:
            cp.start()

    def middle(*mine):
        pass

    def last(*mine):
        out, own = copies(*mine)
        for cp in out + own:
            cp.wait()

    shapes = [SDS((3,) + p.shape[1:], p.dtype) for p in parts]
    if small is None:
        return _Rider(parts, shapes, [3 * n, 3 * n], first, middle, last)
    return _Rider(parts + [small], shapes + [SDS((8,) + small.shape, small.dtype)], [3 * n, 3 * n, 7, 7, 1],
                  first, middle, last)


def _chip_sum(parts, landed, chip_arr, name):
    rh, cc = landed.shape[1:]
    tr = rh // 2

    def body(chip_ref, own_ref, p_ref, o_ref):
        o_ref[...] = ((own_ref[0].astype(F32) + p_ref[0].astype(F32)) + p_ref[1].astype(F32)) + p_ref[2].astype(F32)

    grid_spec = pltpu.PrefetchScalarGridSpec(
        num_scalar_prefetch=1, grid=(2,),
        in_specs=[pl.BlockSpec((1, tr, cc), lambda i, chip_ref: (chip_ref[0], i, 0)),
                  pl.BlockSpec((3, tr, cc), lambda i, chip_ref: (0, i, 0))],
        out_specs=pl.BlockSpec((tr, cc), lambda i, chip_ref: (i, 0)))
    return pl.pallas_call(body, grid_spec=grid_spec, out_shape=SDS((rh, cc), F32),
                          compiler_params=_cparams(("parallel",)), name=name)(chip_arr, parts, landed)


def _device_sum(p):
    def body(p_ref, o_ref):
        acc = p_ref[0]
        for k in range(1, 8):
            acc = acc + p_ref[k]
        o_ref[...] = acc

    return pl.pallas_call(body, out_shape=SDS(p.shape[1:], F32), name="small_sum")(p)


def _join_halves(halves):
    n = len(halves)

    def body(*refs):
        ins, outs = refs[:n], refs[n:2 * n]
        send_sems, recv_sems = refs[2 * n:]
        x, y, c, _ = _place()
        copies = [pltpu.make_async_remote_copy(
            src_ref=ins[a], dst_ref=outs[a], send_sem=send_sems.at[a], recv_sem=recv_sems.at[a],
            device_id=(x, y, 1 - c), device_id_type=MESH) for a in range(n)]
        for cp in copies:
            cp.start()
        for cp in copies:
            cp.wait()

    return pl.pallas_call(
        body, in_specs=[HBM_SPEC] * n, out_specs=[HBM_SPEC] * n,
        out_shape=[SDS(h.shape, h.dtype) for h in halves],
        scratch_shapes=[pltpu.SemaphoreType.DMA((n,)), pltpu.SemaphoreType.DMA((n,))],
        name="join_halves")(*halves)


def _adamw(w, g, m, v, name):
    rows, cols = w.shape
    tr = rows if rows <= 352 else (256 if rows % 256 == 0 else 352)

    def body(w_ref, g_ref, m_ref, v_ref, d_ref, nm_ref, nv_ref):
        gg = g_ref[...]
        nm = ADAM_B1 * m_ref[...] + (1.0 - ADAM_B1) * gg
        nv = ADAM_B2 * v_ref[...] + (1.0 - ADAM_B2) * (gg * gg)
        nm_ref[...] = nm
        nv_ref[...] = nv
        m_hat = nm / (1.0 - ADAM_B1 ** ADAM_STEP)
        v_hat = nv / (1.0 - ADAM_B2 ** ADAM_STEP)
        d_ref[...] = -ADAM_LR * (m_hat / (jnp.sqrt(v_hat) + ADAM_EPS) + ADAM_WD * w_ref[...])

    blk = pl.BlockSpec((tr, cols), lambda i: (i, 0))
    return pl.pallas_call(
        body, grid=(rows // tr,), in_specs=[blk] * 4, out_specs=[blk] * 3,
        out_shape=[SDS((rows, cols), F32)] * 3,
        compiler_params=_cparams(("parallel",)), name=name)(w, g, m, v)


def _pack_small(pre_mix, post_mix, pre_ffn, post_ffn, rel_bias, b_forget, sinks):
    def at(row, v):
        return jnp.pad(v, ((row, 7 - row), (0, D_MODEL - v.shape[1])))
    return (at(0, pre_mix) + at(1, post_mix) + at(2, pre_ffn) + at(3, post_ffn)
            + at(4, rel_bias.reshape(1, N_BUCKETS * N_HEADS)) + at(5, jnp.concatenate([b_forget, sinks], axis=1)))


def _unpack_small(p):
    return dict(ln_pre_mix=p[0:1], ln_post_mix=p[1:2], ln_pre_ffn=p[2:3], ln_post_ffn=p[3:4],
                rel_bias=p[4, :N_BUCKETS * N_HEADS].reshape(N_BUCKETS, N_HEADS),
                b_forget=p[5:6, 0:N_HEADS], sinks=p[5:6, N_HEADS:2 * N_HEADS])


WEIGHTS = ("meta_tokens", "rel_bias", "ln_pre_mix", "ln_post_mix", "ln_pre_ffn", "ln_post_ffn",
           "w_in", "b_forget", "sinks", "w_out", "w_gate_up", "w_down")


def kernel(x, meta_tokens, rel_bias, ln_pre_mix, ln_post_mix, ln_pre_ffn, ln_post_ffn, w_in, b_forget, sinks, w_out, w_gate_up, w_down, loss_target, m_meta_tokens, m_rel_bias, m_ln_pre_mix, m_ln_post_mix, m_ln_pre_ffn, m_ln_post_ffn, m_w_in, m_b_forget, m_sinks, m_w_out, m_w_gate_up, m_w_down, v_meta_tokens, v_rel_bias, v_ln_pre_mix, v_ln_post_mix, v_ln_pre_ffn, v_ln_post_ffn, v_w_in, v_b_forget, v_sinks, v_w_out, v_w_gate_up, v_w_down):
    xi, yi, ci = lax.axis_index("x"), lax.axis_index("y"), lax.axis_index("c")
    chip = 2 * xi + yi
    c_arr = jnp.reshape(ci, (1,)).astype(jnp.int32)

    def halves(w, dtype):
        return w.astype(dtype).reshape(2, w.shape[0] // 2, w.shape[1])

    def with_own(gathered, shards):
        return [lax.dynamic_update_slice(got, own[None], (chip, 0, 0, 0)) for got, own in zip(gathered, shards)]

    shards = [halves(w_in[0], BF16), halves(w_out[0], BF16), halves(meta_tokens, F32)]
    gw_in, gw_out, g_meta = with_own(_run_alone(_gather_rider(shards), "gather_mixer_weights"), shards)
    ffn_shards = [halves(w_gate_up[0], BF16), halves(w_down[0], BF16)]

    def ffn_weights(carried):
        gw_gu, gw_dn = with_own(carried, ffn_shards)
        return gw_gu.reshape(4, D_MODEL, FF_T), gw_dn.reshape(D_FF, D_MODEL)

    early = {}

    def early_grads(dw_gu, dw_dn, dw_out):
        grads = [dw_out.reshape(4, 2, 128, D_MODEL), dw_gu.reshape(4, 2, 512, FF_T), dw_dn.reshape(4, 2, 352, D_MODEL)]
        got = _swap_halves(grads, "swap_halves_early")
        early["parts"] = [_pair_sum(g, p, c_arr, "pair_sum_%d" % a) for a, (g, p) in enumerate(zip(grads, got))]
        return _exchange_rider(early["parts"])
    w_in_all = gw_in.reshape(4, D_MODEL, D_PROJ // 4).transpose(1, 0, 2).reshape(D_MODEL, D_PROJ)
    w_in_b = jnp.concatenate(
        [w_in_all[:, 0:512], w_in_all[:, 768:1280], w_in_all[:, 1280:1792], w_in_all[:, 1792:2304],
         w_in_all[:, 512:640], w_in_all[:, 640:768], w_in_all[:, 2304:2312],
         jnp.zeros((D_MODEL, D_PROJ_P - D_PROJ), BF16)], axis=1)
    meta_all = g_meta.reshape(4, N_META, D_MODEL // 4).transpose(1, 0, 2).reshape(N_META, D_MODEL)

    loc = _local_step(x[0], loss_target[0], meta_all, rel_bias, ln_pre_mix, ln_post_mix, ln_pre_ffn, ln_post_ffn,
                      b_forget, sinks, w_in_b, gw_out.reshape(D_MODEL, D_MODEL),
                      _gather_rider(ffn_shards), ffn_weights, early_grads)

    n = loc["w_in"]
    dw_in = jnp.concatenate([n[:, 0:512], n[:, 2048:2176], n[:, 2176:2304], n[:, 512:1024], n[:, 1024:1536],
                             n[:, 1536:2048], n[:, 2304:2312]], axis=1)
    dw_in = dw_in.reshape(D_MODEL, 4, D_PROJ // 4).transpose(1, 0, 2).reshape(4, 2, 512, D_PROJ // 4)
    small = jnp.concatenate(
        [_pack_small(loc["ln_pre_mix"], loc["ln_post_mix"], loc["ln_pre_ffn"], loc["ln_post_ffn"],
                     loc["rel_bias"], loc["b_forget"], loc["sinks"])
         + jnp.pad(loc["loss"].reshape(1, 1), ((LOSS_ROW, 7 - LOSS_ROW), (0, D_MODEL - 1))), loc["meta"]], axis=0)

    (got_in,) = _swap_halves([dw_in], "swap_halves_late")
    part_in = _pair_sum(dw_in, got_in, c_arr, "pair_sum_late")
    landed_in, small_all = _run_alone(_exchange_rider([part_in], small), "exchange_late")
    chip_arr = jnp.reshape(chip, (1,)).astype(jnp.int32)
    parts = [part_in] + early["parts"]
    landed = [landed_in] + loc["landed"]
    mine = [_chip_sum(p, l, chip_arr, "chip_sum_%d" % a) for a, (p, l) in enumerate(zip(parts, landed))]
    small_sum = _device_sum(small_all)
    theirs = _join_halves(mine)
    full = [jnp.where(ci == 0, jnp.concatenate([m, t], axis=0), jnp.concatenate([t, m], axis=0))
            for m, t in zip(mine, theirs)]
    g_w_in, g_w_out, g_w_gu, g_w_dn = full
    g_meta_tokens = lax.dynamic_slice(small_sum[8:N_SMALL], (0, chip * (D_MODEL // 4)), (N_META, D_MODEL // 4))
    g_small = small_sum[0:8]

    grad = _unpack_small(g_small)
    grad.update(meta_tokens=g_meta_tokens, w_in=g_w_in[None], w_out=g_w_out[None], w_gate_up=g_w_gu[None],
                w_down=g_w_dn[None])

    delta, new_m, new_v = {}, {}, {}
    big = dict(w_in=(w_in, m_w_in, v_w_in, g_w_in), w_out=(w_out, m_w_out, v_w_out, g_w_out),
               w_gate_up=(w_gate_up, m_w_gate_up, v_w_gate_up, g_w_gu), w_down=(w_down, m_w_down, v_w_down, g_w_dn))
    for name, (w, m, v, g) in big.items():
        d, nm, nv = _adamw(w[0], g, m[0], v[0], "adamw_" + name)
        delta[name], new_m[name], new_v[name] = d[None], nm[None], nv[None]
    delta["meta_tokens"], new_m["meta_tokens"], new_v["meta_tokens"] = _adamw(
        meta_tokens, g_meta_tokens, m_meta_tokens, v_meta_tokens, "adamw_meta")
    d, nm, nv = _adamw(
        _pack_small(ln_pre_mix, ln_post_mix, ln_pre_ffn, ln_post_ffn, rel_bias, b_forget, sinks), g_small,
        _pack_small(m_ln_pre_mix, m_ln_post_mix, m_ln_pre_ffn, m_ln_post_ffn, m_rel_bias, m_b_forget, m_sinks),
        _pack_small(v_ln_pre_mix, v_ln_post_mix, v_ln_pre_ffn, v_ln_post_ffn, v_rel_bias, v_b_forget, v_sinks),
        "adamw_small")
    delta.update(_unpack_small(d))
    new_m.update(_unpack_small(nm))
    new_v.update(_unpack_small(nv))

    loss = small_sum[LOSS_ROW, 0]
    return (loss,loc["grad_x"][None], *[grad[k] for k in WEIGHTS], *[delta[k] for k in WEIGHTS],
            *[new_m[k] for k in WEIGHTS], *[new_v[k] for k in WEIGHTS])
```

```python
import math

import numpy as np
import jax
import jax.numpy as jnp
from jax import lax
from jax.experimental import pallas as pl
from jax.experimental.pallas import tpu as pltpu

F32 = jnp.float32
BF16 = jnp.bfloat16
MESH = pl.DeviceIdType.MESH
SDS = jax.ShapeDtypeStruct

D_MODEL = 1024
SEQ = 4096
N_META = 16
N_HEADS = 8
HALF = 64
D_FF = 2816
N_BUCKETS = 32
EPS = 1e-6
NEG = -1e30
SCALE = 0.125
PAD_ROWS = 112
ROW0 = PAD_ROWS + N_META
LP = ROW0 + SEQ
BLK = 128
NBLK = LP // BLK
TM = 384
NT = LP // TM
TM_PURE = LP // 2
TM_MID = LP // 4
TM_EPI = LP // 6
TN = 256
D_PROJ = 2312
D_PROJ_P = 2432
D_QKV = 2304
FF_T = 1408
VMEM_LIMIT = 56 * 1024 * 1024

ADAM_LR = 0.001
ADAM_B1 = 0.9
ADAM_B2 = 0.999
ADAM_EPS = 1e-08
ADAM_WD = 0.01
ADAM_STEP = 10

QA, QB, KB, VB = 0, 1, 2, 3
KA, VA = 16, 17

NT_DIMS = (((1,), (1,)), ((), ()))
TN_DIMS = (((0,), (0,)), ((), ()))


def _cparams(sem):
    return pltpu.CompilerParams(dimension_semantics=sem, vmem_limit_bytes=VMEM_LIMIT)


def _t5_bucket_np(d):
    n = np.maximum(d, 0).astype(np.int32)
    nf = np.maximum(n, 1).astype(np.float32)
    large = 16 + (np.log(nf / np.float32(16)) / np.float32(math.log(8.0)) * np.float32(16)).astype(np.int32)
    large = np.minimum(large, N_BUCKETS - 1)
    return np.where(n < 16, n, large).astype(np.int32)


def _bucket_tables():
    qi = np.arange(BLK)[:, None]
    ki = np.arange(BLK)[None, :]
    return np.stack([_t5_bucket_np(qi - ki), _t5_bucket_np(qi - ki + BLK)])


def _rms(x):
    return lax.rsqrt(jnp.mean(x * x, axis=-1, keepdims=True) + EPS)


def _rms_bwd(n, r, gdy):
    return r * (gdy - n * jnp.mean(n * gdy, axis=-1, keepdims=True))


def _pre_mix(h0, gain, w_in_b):
    half = D_QKV // 2

    def body(h_ref, g_ref, w_ref, hn_ref, proj_ref, f_ref):
        x = h_ref[...]
        hn = (x * _rms(x) * g_ref[...]).astype(BF16)
        hn_ref[...] = hn
        proj_ref[:, :half] = jnp.dot(hn, w_ref[:, :half], preferred_element_type=F32).astype(BF16)
        p = jnp.dot(hn, w_ref[:, half:], preferred_element_type=F32)
        proj_ref[:, half:] = p[:, :half].astype(BF16)
        f_ref[...] = p[:, half:]

    return pl.pallas_call(
        body, grid=(LP // TM_MID,),
        in_specs=[pl.BlockSpec((TM_MID, D_MODEL), lambda i: (i, 0)),
                  pl.BlockSpec((1, D_MODEL), lambda i: (0, 0)),
                  pl.BlockSpec((D_MODEL, D_PROJ_P), lambda i: (0, 0))],
        out_specs=[pl.BlockSpec((TM_MID, D_MODEL), lambda i: (i, 0)),
                   pl.BlockSpec((TM_MID, D_QKV), lambda i: (i, 0)),
                   pl.BlockSpec((TM_MID, BLK), lambda i: (i, 0))],
        out_shape=[SDS((LP, D_MODEL), BF16), SDS((LP, D_QKV), BF16), SDS((LP, BLK), F32)],
        compiler_params=_cparams(("parallel",)), name="pre_mix")(h0, gain, w_in_b)


def _attn_out(o_a, o_b, w_out_b, h0, g_post, g_pre_ffn):
    def body(oa_ref, ob_ref, w_ref, h0_ref, gp_ref, gf_ref, a_ref, h1_ref, hn2_ref):
        a = (jnp.dot(oa_ref[...], w_ref[0:512, :], preferred_element_type=F32)
             + jnp.dot(ob_ref[...], w_ref[512:1024, :], preferred_element_type=F32))
        a_ref[...] = a
        h1 = h0_ref[...] + a * _rms(a) * gp_ref[...]
        h1_ref[...] = h1
        hn2_ref[...] = (h1 * _rms(h1) * gf_ref[...]).astype(BF16)

    row = lambda w: pl.BlockSpec((TM_EPI, w), lambda i: (i, 0))
    vec = pl.BlockSpec((1, D_MODEL), lambda i: (0, 0))
    return pl.pallas_call(
        body, grid=(LP // TM_EPI,),
        in_specs=[row(512), row(512), pl.BlockSpec((D_MODEL, D_MODEL), lambda i: (0, 0)), row(D_MODEL), vec, vec],
        out_specs=[row(D_MODEL), row(D_MODEL), row(D_MODEL)],
        out_shape=[SDS((LP, D_MODEL), F32), SDS((LP, D_MODEL), F32), SDS((LP, D_MODEL), BF16)],
        compiler_params=_cparams(("parallel",)), name="attn_out")(o_a, o_b, w_out_b, h0, g_post, g_pre_ffn)


def _ffn_up(hn2, w_gu_b):
    def body(x_ref, wg_ref, wu_ref, g_ref, u_ref, act_ref):
        x = x_ref[...]
        g = jnp.dot(x, wg_ref[...], preferred_element_type=F32)
        u = jnp.dot(x, wu_ref[...], preferred_element_type=F32)
        g_ref[...] = g.astype(BF16)
        u_ref[...] = u.astype(BF16)
        act_ref[...] = (g * (1.0 / (1.0 + jnp.exp(-g))) * u).astype(BF16)

    out = pl.BlockSpec((TM_PURE, TN), lambda i, j: (i, j))
    return pl.pallas_call(
        body, grid=(LP // TM_PURE, D_FF // TN),
        in_specs=[pl.BlockSpec((TM_PURE, D_MODEL), lambda i, j: (i, 0)),
                  pl.BlockSpec((D_MODEL, TN), lambda i, j: (0, j)),
                  pl.BlockSpec((D_MODEL, TN), lambda i, j: (0, j + D_FF // TN))],
        out_specs=[out, out, out],
        out_shape=[SDS((LP, D_FF), BF16)] * 3,
        compiler_params=_cparams(("parallel", "parallel")), name="ffn_up")(hn2, w_gu_b, w_gu_b)


def _ffn_down_loss(act, w_dn_b, h1, tgt_p, g_post_ffn):
    def body(act_ref, w_ref, h1_ref, t_ref, g_ref, dff_ref, dy_ref, loss_ref, dg_ref):
        i = pl.program_id(0)

        @pl.when(i == 0)
        def _():
            loss_ref[...] = jnp.zeros_like(loss_ref)
            dg_ref[...] = jnp.zeros_like(dg_ref)

        ff = jnp.dot(act_ref[...], w_ref[...], preferred_element_type=F32)
        r = _rms(ff)
        n = ff * r
        g = g_ref[...]
        y = h1_ref[...] + n * g
        rows = i * TM + lax.broadcasted_iota(jnp.int32, (TM, D_MODEL), 0)
        diff = jnp.where(rows >= ROW0, y - t_ref[...], 0.0)
        loss_ref[...] += 0.5 * jnp.sum(diff * diff) / D_MODEL
        dy = diff / D_MODEL
        dy_ref[...] = dy
        dg_ref[...] += jnp.sum(dy * n, axis=0, keepdims=True)
        dff_ref[...] = _rms_bwd(n, r, g * dy).astype(BF16)

    row = pl.BlockSpec((TM, D_MODEL), lambda i: (i, 0))
    return pl.pallas_call(
        body, grid=(NT,),
        in_specs=[pl.BlockSpec((TM, D_FF), lambda i: (i, 0)), pl.BlockSpec((D_FF, D_MODEL), lambda i: (0, 0)),
                  row, row, pl.BlockSpec((1, D_MODEL), lambda i: (0, 0))],
        out_specs=[row, row, pl.BlockSpec((8, BLK), lambda i: (0, 0)), pl.BlockSpec((1, D_MODEL), lambda i: (0, 0))],
        out_shape=[SDS((LP, D_MODEL), BF16), SDS((LP, D_MODEL), F32), SDS((8, BLK), F32), SDS((1, D_MODEL), F32)],
        compiler_params=_cparams(("arbitrary",)), name="ffn_down_loss")(act, w_dn_b, h1, tgt_p, g_post_ffn)


def _ffn_down_bwd(dff, w_dn_b, g, u):
    def body(d_ref, w_ref, g_ref, u_ref, dg_ref, du_ref):
        dact = lax.dot_general(d_ref[...], w_ref[...], NT_DIMS, preferred_element_type=F32)
        gg = g_ref[...].astype(F32)
        sig = 1.0 / (1.0 + jnp.exp(-gg))
        dg_ref[...] = (dact * u_ref[...].astype(F32) * sig * (1.0 + gg * (1.0 - sig))).astype(BF16)
        du_ref[...] = (dact * gg * sig).astype(BF16)

    blk = pl.BlockSpec((TM_PURE, TN), lambda i, j: (i, j))
    return pl.pallas_call(
        body, grid=(LP // TM_PURE, D_FF // TN),
        in_specs=[pl.BlockSpec((TM_PURE, D_MODEL), lambda i, j: (i, 0)),
                  pl.BlockSpec((TN, D_MODEL), lambda i, j: (j, 0)), blk, blk],
        out_specs=[blk, blk],
        out_shape=[SDS((LP, D_FF), BF16)] * 2,
        compiler_params=_cparams(("parallel", "parallel")), name="ffn_down_bwd")(dff, w_dn_b, g, u)


def _ffn_up_bwd(dg, du, w_gu_b, h1, a, dy, g_pre_ffn, g_post_mix):
    def body(dg_ref, du_ref, w_ref, h1_ref, a_ref, dy_ref, gf_ref, gp_ref,
             dh1_ref, da_ref, dgf_ref, dgp_ref, acc):
        i = pl.program_id(0)
        s = pl.program_id(1)

        @pl.when((i == 0) & (s == 0))
        def _():
            dgf_ref[...] = jnp.zeros_like(dgf_ref)
            dgp_ref[...] = jnp.zeros_like(dgp_ref)

        @pl.when(s == 0)
        def _():
            acc[...] = jnp.zeros_like(acc)

        @pl.when(s < 2)
        def _():
            acc[...] += lax.dot_general(dg_ref[...], w_ref[...], NT_DIMS, preferred_element_type=F32)

        @pl.when(s >= 2)
        def _():
            acc[...] += lax.dot_general(du_ref[...], w_ref[...], NT_DIMS, preferred_element_type=F32)

        @pl.when(s == 3)
        def _():
            dhn2 = acc[...]
            h1 = h1_ref[...]
            r2 = _rms(h1)
            n2 = h1 * r2
            dgf_ref[...] += jnp.sum(dhn2 * n2, axis=0, keepdims=True)
            dh1 = dy_ref[...] + _rms_bwd(n2, r2, gf_ref[...] * dhn2)
            dh1_ref[...] = dh1
            av = a_ref[...]
            ra = _rms(av)
            na = av * ra
            dgp_ref[...] += jnp.sum(dh1 * na, axis=0, keepdims=True)
            da_ref[...] = _rms_bwd(na, ra, gp_ref[...] * dh1).astype(BF16)

    row = pl.BlockSpec((TM_EPI, D_MODEL), lambda i, s: (i, 0))
    vec = pl.BlockSpec((1, D_MODEL), lambda i, s: (0, 0))
    return pl.pallas_call(
        body, grid=(LP // TM_EPI, 4),
        in_specs=[pl.BlockSpec((TM_EPI, FF_T), lambda i, s: (i, jnp.minimum(s, 1))),
                  pl.BlockSpec((TM_EPI, FF_T), lambda i, s: (i, jnp.maximum(s - 2, 0))),
                  pl.BlockSpec((D_MODEL, FF_T), lambda i, s: (0, s)),
                  row, row, row, vec, vec],
        out_specs=[row, row, vec, vec],
        out_shape=[SDS((LP, D_MODEL), F32), SDS((LP, D_MODEL), BF16), SDS((1, D_MODEL), F32), SDS((1, D_MODEL), F32)],
        scratch_shapes=[pltpu.VMEM((TM_EPI, D_MODEL), F32)],
        compiler_params=_cparams(("arbitrary", "arbitrary")), name="ffn_up_bwd",
    )(dg, du, w_gu_b, h1, a, dy, g_pre_ffn, g_post_mix)


def _attn_out_bwd(da, w_out_b):
    def body(d_ref, w_ref, o_ref):
        o_ref[...] = lax.dot_general(d_ref[...], w_ref[...], NT_DIMS, preferred_element_type=F32).astype(BF16)

    row = pl.BlockSpec((TM_PURE, D_MODEL), lambda i: (i, 0))
    return pl.pallas_call(
        body, grid=(LP // TM_PURE,),
        in_specs=[row, pl.BlockSpec((D_MODEL, D_MODEL), lambda i: (0, 0))],
        out_specs=row, out_shape=SDS((LP, D_MODEL), BF16),
        compiler_params=_cparams(("parallel",)), name="attn_out_bwd")(da, w_out_b)


def _pre_mix_bwd(dq_a, dq_b, dk_b, dv_b, dk_a, dv_a, df, w_in_b, h0, dh1, g_pre_mix):
    def body(qa_ref, qb_ref, kb_ref, vb_ref, ka_ref, va_ref, f_ref, w_ref, h0_ref, dh1_ref, g_ref,
             dproj_ref, dh0_ref, dg_ref):
        i = pl.program_id(0)

        @pl.when(i == 0)
        def _():
            dg_ref[...] = jnp.zeros_like(dg_ref)

        dproj = jnp.concatenate(
            [qa_ref[...], (qb_ref[...] * SCALE).astype(BF16), kb_ref[...], vb_ref[...],
             ka_ref[...].astype(BF16), va_ref[...].astype(BF16), f_ref[...].astype(BF16)], axis=1)
        dproj_ref[...] = dproj
        dhn = lax.dot_general(dproj, w_ref[...], NT_DIMS, preferred_element_type=F32)
        x = h0_ref[...]
        r = _rms(x)
        n = x * r
        dg_ref[...] += jnp.sum(dhn * n, axis=0, keepdims=True)
        dh0_ref[...] = dh1_ref[...] + _rms_bwd(n, r, g_ref[...] * dhn)

    row = lambda w: pl.BlockSpec((TM_EPI, w), lambda i: (i, 0))
    vec = pl.BlockSpec((1, D_MODEL), lambda i: (0, 0))
    return pl.pallas_call(
        body, grid=(LP // TM_EPI,),
        in_specs=[row(512), row(512), row(512), row(512), row(BLK), row(BLK), row(BLK),
                  pl.BlockSpec((D_MODEL, D_PROJ_P), lambda i: (0, 0)), row(D_MODEL), row(D_MODEL), vec],
        out_specs=[row(D_PROJ_P), row(D_MODEL), vec],
        out_shape=[SDS((LP, D_PROJ_P), BF16), SDS((LP, D_MODEL), F32), SDS((1, D_MODEL), F32)],
        compiler_params=_cparams(("arbitrary",)), name="pre_mix_bwd",
    )(dq_a, dq_b, dk_b, dv_b, dk_a, dv_a, df, w_in_b, h0, dh1, g_pre_mix)


def _mm_tn(parts, b, tm, name):
    widths = [p.shape[1] for p in parts]
    m_total = sum(widths)
    n = b.shape[1]
    whole = len(parts) > 1
    assert (tm == m_total) if whole else (m_total % tm == 0)

    def body(*refs):
        a_refs, b_ref, o_ref = refs[:-2], refs[-2], refs[-1]

        @pl.when(pl.program_id(1) == 0)
        def _():
            o_ref[...] = jnp.zeros_like(o_ref)
        a = a_refs[0][...] if not whole else jnp.concatenate([r[...] for r in a_refs], axis=1)
        o_ref[...] += lax.dot_general(a, b_ref[...], TN_DIMS, preferred_element_type=F32)

    a_specs = ([pl.BlockSpec((TM_MID, w), lambda mi, k: (k, 0)) for w in widths] if whole
               else [pl.BlockSpec((TM_MID, tm), lambda mi, k: (k, mi))])
    return pl.pallas_call(
        body, grid=(m_total // tm, LP // TM_MID),
        in_specs=a_specs + [pl.BlockSpec((TM_MID, n), lambda mi, k: (k, 0))],
        out_specs=pl.BlockSpec((tm, n), lambda mi, k: (mi, 0)),
        out_shape=SDS((m_total, n), F32),
        compiler_params=_cparams(("parallel", "arbitrary")), name=name)(*parts, b)


def _dw_gate_up(hn2, dg, du):
    def body(a_ref, dg_ref, du_ref, o_ref):
        s = pl.program_id(0)

        @pl.when(pl.program_id(1) == 0)
        def _():
            o_ref[...] = jnp.zeros_like(o_ref)

        @pl.when(s < 2)
        def _():
            o_ref[0] += lax.dot_general(a_ref[...], dg_ref[...], TN_DIMS, preferred_element_type=F32)

        @pl.when(s >= 2)
        def _():
            o_ref[0] += lax.dot_general(a_ref[...], du_ref[...], TN_DIMS, preferred_element_type=F32)

    return pl.pallas_call(
        body, grid=(4, LP // TM_MID),
        in_specs=[pl.BlockSpec((TM_MID, D_MODEL), lambda s, k: (k, 0)),
                  pl.BlockSpec((TM_MID, FF_T), lambda s, k: (k, jnp.minimum(s, 1))),
                  pl.BlockSpec((TM_MID, FF_T), lambda s, k: (k, jnp.maximum(s - 2, 0)))],
        out_specs=pl.BlockSpec((1, D_MODEL, FF_T), lambda s, k: (s, 0, 0)),
        out_shape=SDS((4, D_MODEL, FF_T), F32),
        compiler_params=_cparams(("parallel", "arbitrary")), name="dw_gate_up")(hn2, dg, du)


def _split3(x):
    hi = x.astype(BF16)
    r1 = x - hi.astype(F32)
    mid = r1.astype(BF16)
    lo = (r1 - mid.astype(F32)).astype(BF16)
    return hi, mid, lo


def _tri_matmul(tri, x):
    hi, mid, lo = _split3(x)
    dot = lambda t: jnp.dot(tri, t, preferred_element_type=F32)
    return dot(hi) + dot(mid) + dot(lo)


def _forget_cumsum(f, b_forget_p):
    def body(f_ref, b_ref, cum_ref, carry):
        i = pl.program_id(0)

        @pl.when(i == 0)
        def _():
            carry[...] = jnp.zeros_like(carry)

        z = f_ref[...] + b_ref[...]
        ls = jnp.minimum(z, 0.0) - jnp.log(1.0 + jnp.exp(-jnp.abs(z)))
        rows = i * BLK + lax.broadcasted_iota(jnp.int32, (BLK, BLK), 0)
        ls = jnp.where(rows >= PAD_ROWS, ls, 0.0)
        r = lax.broadcasted_iota(jnp.int32, (BLK, BLK), 0)
        c = lax.broadcasted_iota(jnp.int32, (BLK, BLK), 1)
        tri = (c <= r).astype(BF16)
        cum = _tri_matmul(tri, ls) + carry[...]
        cum_ref[...] = cum
        carry[...] = cum[BLK - 1:BLK, :]

    return pl.pallas_call(
        body, grid=(NBLK,),
        in_specs=[pl.BlockSpec((BLK, BLK), lambda i: (i, 0)), pl.BlockSpec((1, BLK), lambda i: (0, 0))],
        out_specs=pl.BlockSpec((BLK, BLK), lambda i: (i, 0)),
        out_shape=SDS((LP, BLK), F32),
        scratch_shapes=[pltpu.VMEM((1, BLK), F32)],
        compiler_params=_cparams(("arbitrary",)), name="forget_cumsum")(f, b_forget_p)


def _forget_cumsum_bwd(dcum, f, b_forget_p):
    def body(d_ref, f_ref, b_ref, df_ref, db_ref, carry):
        i = pl.program_id(0)

        @pl.when(i == 0)
        def _():
            carry[...] = jnp.zeros_like(carry)
            db_ref[...] = jnp.zeros_like(db_ref)

        blk = NBLK - 1 - i
        r = lax.broadcasted_iota(jnp.int32, (BLK, BLK), 0)
        c = lax.broadcasted_iota(jnp.int32, (BLK, BLK), 1)
        tri = (c >= r).astype(BF16)
        d = d_ref[...]
        dls = _tri_matmul(tri, d) + carry[...]
        carry[...] = dls[0:1, :]
        z = f_ref[...] + b_ref[...]
        rows = blk * BLK + r
        df = jnp.where(rows >= PAD_ROWS, dls / (1.0 + jnp.exp(z)), 0.0)
        df_ref[...] = df
        db_ref[...] += jnp.sum(df, axis=0, keepdims=True)

    rev = pl.BlockSpec((BLK, BLK), lambda i: (NBLK - 1 - i, 0))
    vec = pl.BlockSpec((1, BLK), lambda i: (0, 0))
    return pl.pallas_call(
        body, grid=(NBLK,),
        in_specs=[rev, rev, vec],
        out_specs=[rev, vec],
        out_shape=[SDS((LP, BLK), F32), SDS((1, BLK), F32)],
        scratch_shapes=[pltpu.VMEM((1, BLK), F32)],
        compiler_params=_cparams(("arbitrary",)), name="forget_cumsum_bwd")(dcum, f, b_forget_p)


def _lane_half(rows):
    return lax.broadcasted_iota(jnp.int32, (rows, BLK), 1) // HALF


def _fox_valid(qi, kj):
    qrow = qi * TM + lax.broadcasted_iota(jnp.int32, (TM, TM), 0)
    krow = kj * TM + lax.broadcasted_iota(jnp.int32, (TM, TM), 1)
    return (krow <= qrow) & ((krow >= PAD_ROWS) | (qrow < PAD_ROWS))


class _Rider:
    def __init__(self, operands, out_shapes, sem_counts, first, middle, last):
        self.operands, self.out_shapes, self.sem_counts = list(operands), list(out_shapes), list(sem_counts)
        self.first, self.middle, self.last = first, middle, last

    def scratch(self):
        return [pltpu.SemaphoreType.DMA((k,)) for k in self.sem_counts]

    def split(self, refs, n_in, n_out, n_scratch):
        a, b = len(self.operands), len(self.out_shapes)
        ins, mine_in = refs[:n_in], refs[n_in:n_in + a]
        outs, mine_out = refs[n_in + a:n_in + a + n_out], refs[n_in + a + n_out:n_in + a + n_out + b]
        rest = refs[n_in + a + n_out + b:]
        return ins, outs, rest[:n_scratch], (mine_in, mine_out, rest[n_scratch:])

    def at_steps(self, mine, is_first, is_middle, is_last):
        for cond, fn in ((is_first, self.first), (is_middle, self.middle), (is_last, self.last)):
            pl.when(cond)(lambda fn=fn: fn(*mine))


HBM_SPEC = pl.BlockSpec(memory_space=pltpu.HBM)


N_AUG = 4
QCH = 128
KSUB = 384


def _fox_prep(proj, cum):
    def body(q_ref, k_ref, v_ref, c_ref, qa_ref, ka_ref, vm_ref, kt_ref, vt_ref):
        half = _lane_half(BLK)
        lane = lax.broadcasted_iota(jnp.int32, (BLK, BLK), 1)
        for pp in range(4):
            cols = slice(pp * BLK, (pp + 1) * BLK)
            qs = q_ref[:, cols].astype(F32) * SCALE
            kp = k_ref[:, cols].astype(F32)
            vp = v_ref[:, cols]
            vt_ref[cols, :] = vp.astype(F32).T.astype(BF16)
            for e in range(2):
                h = 2 * pp + e
                a = (1 - e) * HALF
                blk = slice(h * BLK, (h + 1) * BLK)
                hi, mid, lo = _split3(-c_ref[:, h:h + 1])
                q_aug = jnp.where(half == e, qs, jnp.where((lane >= a) & (lane < a + 3), 1.0, 0.0))
                k_aug = jnp.where(half == e, kp, jnp.where(
                    lane == a, hi.astype(F32), jnp.where(lane == a + 1, mid.astype(F32), jnp.where(
                        lane == a + 2, lo.astype(F32), jnp.where(lane == a + 3, 1.0, 0.0)))))
                qa_ref[:, blk] = q_aug.astype(BF16)
                ka_ref[:, blk] = k_aug.astype(BF16)
                kt_ref[blk, :] = k_aug.T.astype(BF16)
                vm_ref[:, blk] = jnp.where(half == e, vp, jnp.zeros_like(vp))

    row = lambda blk: pl.BlockSpec((BLK, 512), lambda i: (i, blk))
    wide = pl.BlockSpec((BLK, 1024), lambda i: (i, 0))
    return pl.pallas_call(
        body, grid=(NBLK,),
        in_specs=[row(QB), row(KB), row(VB), pl.BlockSpec((BLK, BLK), lambda i: (i, 0))],
        out_specs=[wide, wide, wide, pl.BlockSpec((1024, BLK), lambda i: (0, i)),
                   pl.BlockSpec((512, BLK), lambda i: (0, i))],
        out_shape=[SDS((LP, 1024), BF16)] * 3 + [SDS((1024, LP), BF16), SDS((512, LP), BF16)],
        compiler_params=_cparams(("parallel",)), name="fox_prep")(proj, proj, proj, cum)


def _over_keys(reduce, x):
    slabs = x.reshape(x.shape[0] // HALF, HALF, x.shape[1])
    return reduce(reduce(slabs, axis=0), axis=0, keepdims=True)


def _fox_valid_t(qi, kj, c, r):
    krow = kj * TM + r * KSUB + lax.broadcasted_iota(jnp.int32, (KSUB, QCH), 0)
    qrow = qi * TM + c * QCH + lax.broadcasted_iota(jnp.int32, (KSUB, QCH), 1)
    return (krow <= qrow) & ((krow >= PAD_ROWS) | (qrow < PAD_ROWS))


def _fox_fwd(q_aug, k_aug, v_t, rider):
    def body(*refs):
        (q_ref, k_ref, vt_ref), (o_ref, lse_ref), (m_s, l_s, acc_s), mine = rider.split(refs, 3, 2, 3)
        qi = pl.program_id(0)
        kj = pl.program_id(1)
        rider.at_steps(mine, (qi == 0) & (kj == 0), (qi == NT // 2) & (kj == 0), (qi == NT - 1) & (kj == NT - 1))

        @pl.when(kj == 0)
        def _():
            m_s[...] = jnp.full_like(m_s, NEG)
            l_s[...] = jnp.zeros_like(l_s)
            acc_s[...] = jnp.zeros_like(acc_s)

        def tile(masked):
            steps = [(h, c, r) for h in range(N_HEADS) for c in range(TM // QCH) for r in range(TM // KSUB)]

            def scores(h, c, r):
                blk = slice(h * BLK, (h + 1) * BLK)
                return lax.dot_general(k_ref[r * KSUB:(r + 1) * KSUB, blk], q_ref[c * QCH:(c + 1) * QCH, blk],
                                       NT_DIMS, preferred_element_type=F32)

            ahead = scores(*steps[0])
            for n, (h, c, r) in enumerate(steps):
                s_t = ahead
                if n + 1 < len(steps):
                    ahead = scores(*steps[n + 1])
                cs = slice(c * QCH, (c + 1) * QCH)
                if masked:
                    s_t = jnp.where(_fox_valid_t(qi, kj, c, r), s_t, NEG)
                m_prev = m_s[h, :, cs]
                m_new = jnp.maximum(m_prev, _over_keys(jnp.max, s_t))
                p_t = jnp.exp(s_t - m_new)
                alpha = jnp.exp(m_prev - m_new)
                l_s[h, :, cs] = alpha * l_s[h, :, cs] + _over_keys(jnp.sum, p_t)
                m_s[h, :, cs] = m_new
                vt = vt_ref[h * HALF:(h + 1) * HALF, r * KSUB:(r + 1) * KSUB]
                acc_s[h, :, cs] = acc_s[h, :, cs] * alpha + jnp.dot(vt, p_t.astype(BF16),
                                                                    preferred_element_type=F32)

        @pl.when((kj < qi) & (kj > 0))
        def _():
            tile(False)

        @pl.when((kj <= qi) & ((kj == qi) | (kj == 0)))
        def _():
            tile(True)

        @pl.when(kj == qi)
        def _():
            for pp in range(4):
                both = jnp.concatenate([acc_s[2 * pp] * (1.0 / l_s[2 * pp]),
                                        acc_s[2 * pp + 1] * (1.0 / l_s[2 * pp + 1])], axis=0)
                o_ref[:, pp * BLK:(pp + 1) * BLK] = both.T.astype(BF16)
            for h in range(N_HEADS):
                lse_ref[h] = m_s[h] + jnp.log(l_s[h])

    o_b, lse, *carried = pl.pallas_call(
        body, grid=(NT, NT),
        in_specs=[pl.BlockSpec((TM, 1024), lambda qi, kj: (qi, 0)),
                  pl.BlockSpec((TM, 1024), lambda qi, kj: (jnp.minimum(kj, qi), 0)),
                  pl.BlockSpec((512, TM), lambda qi, kj: (0, jnp.minimum(kj, qi)))]
        + [HBM_SPEC] * len(rider.operands),
        out_specs=[pl.BlockSpec((TM, 512), lambda qi, kj: (qi, 0)),
                   pl.BlockSpec((N_HEADS, 1, TM), lambda qi, kj: (0, 0, qi))] + [HBM_SPEC] * len(rider.out_shapes),
        out_shape=[SDS((LP, 512), BF16), SDS((N_HEADS, 1, LP), F32)] + rider.out_shapes,
        scratch_shapes=[pltpu.VMEM((N_HEADS, 1, TM), F32), pltpu.VMEM((N_HEADS, 1, TM), F32),
                        pltpu.VMEM((N_HEADS, HALF, TM), F32)] + rider.scratch(),
        compiler_params=_cparams(("arbitrary", "arbitrary")), name="fox_fwd",
    )(q_aug, k_aug, v_t, *rider.operands)
    return o_b, lse, carried


def _fox_bwd(proj, o_b, dmix, lse, ck_t, rider):
    def body(*refs):
        ((q_ref, k_ref, v_ref, o_ref, do_ref, lse_ref, ck_ref), (dq_ref, dk_ref, dv_ref, dck_ref, dcq_ref),
         (dk_s, dv_s, dck_s), mine) = rider.split(refs, 7, 5, 3)
        kj = pl.program_id(0)
        qi = pl.program_id(1)
        rider.at_steps(mine, (kj == 0) & (qi == 0), (kj == NT // 2) & (qi == 0), (kj == NT - 1) & (qi == NT - 1))

        @pl.when((kj == 0) & (qi == 0))
        def _():
            dq_ref[...] = jnp.zeros_like(dq_ref)
            dcq_ref[...] = jnp.zeros_like(dcq_ref)

        @pl.when(qi == 0)
        def _():
            dk_s[...] = jnp.zeros_like(dk_s)
            dv_s[...] = jnp.zeros_like(dv_s)
            dck_s[...] = jnp.zeros_like(dck_s)

        def tile(masked):
            valid = _fox_valid(qi, kj) if masked else None
            half = _lane_half(TM)
            q0 = pl.multiple_of(qi * TM, TM)
            lane = lax.broadcasted_iota(jnp.int32, (TM, BLK), 1)
            row_sums = jnp.zeros((TM, BLK), F32)
            for pp in range(4):
                cols = slice(pp * BLK, (pp + 1) * BLK)
                qs = (q_ref[:, cols].astype(F32) * SCALE).astype(BF16)
                kp = k_ref[:, cols]
                vp = v_ref[:, cols]
                dop = do_ref[:, cols]
                prod = dop.astype(F32) * o_ref[:, cols].astype(F32)
                d0 = jnp.sum(jnp.where(half == 0, prod, 0.0), axis=1, keepdims=True)
                d1 = jnp.sum(prod, axis=1, keepdims=True) - d0
                dq = jnp.zeros((TM, BLK), F32)
                dks, dvs = [], []
                for e in range(2):
                    h = 2 * pp + e
                    ke = jnp.where(half == e, kp, jnp.zeros_like(kp))
                    ve = jnp.where(half == e, vp, jnp.zeros_like(vp))
                    t = lax.dot_general(qs, ke, NT_DIMS, preferred_element_type=F32) - ck_ref[h] - lse_ref[h]
                    if masked:
                        t = jnp.where(valid, t, NEG)
                    p = jnp.exp(t)
                    dp = lax.dot_general(dop, ve, NT_DIMS, preferred_element_type=F32)
                    ds = p * (dp - (d0 if e == 0 else d1))
                    dck_s[h] += jnp.sum(ds, axis=0, keepdims=True)
                    row_sums = jnp.where(lane == h, jnp.sum(ds, axis=1, keepdims=True), row_sums)
                    ds_b = ds.astype(BF16)
                    dq = dq + jnp.dot(ds_b, ke, preferred_element_type=F32)
                    dks.append(lax.dot_general(ds_b, qs, TN_DIMS, preferred_element_type=F32))
                    dvs.append(lax.dot_general(p.astype(BF16), dop, TN_DIMS, preferred_element_type=F32))
                dq_ref[pl.ds(q0, TM), cols] += dq
                dk_s[pp] += jnp.where(half == 0, dks[0], dks[1])
                dv_s[pp] += jnp.where(half == 0, dvs[0], dvs[1])
            dcq_ref[pl.ds(q0, TM), :] += row_sums

        @pl.when((qi > kj) & (kj > 0))
        def _():
            tile(False)

        @pl.when((qi >= kj) & ((qi == kj) | (kj == 0)))
        def _():
            tile(True)

        @pl.when(qi == NT - 1)
        def _():
            for pp in range(4):
                cols = slice(pp * BLK, (pp + 1) * BLK)
                dk_ref[:, cols] = dk_s[pp].astype(BF16)
                dv_ref[:, cols] = dv_s[pp].astype(BF16)
            dck_ref[...] = dck_s[...]

    qrow = lambda blk: pl.BlockSpec((TM, 512), lambda kj, qi: (jnp.maximum(qi, kj), blk))
    krow = lambda blk: pl.BlockSpec((TM, 512), lambda kj, qi: (kj, blk))
    dq, dk, dv, dck, dcq, *carried = pl.pallas_call(
        body, grid=(NT, NT),
        in_specs=[qrow(QB), krow(KB), krow(VB), qrow(0), qrow(1),
                  pl.BlockSpec((N_HEADS, TM, 1), lambda kj, qi: (0, jnp.maximum(qi, kj), 0)),
                  pl.BlockSpec((N_HEADS, 1, TM), lambda kj, qi: (0, 0, kj))] + [HBM_SPEC] * len(rider.operands),
        out_specs=[pl.BlockSpec((LP, 512), lambda kj, qi: (0, 0)),
                   pl.BlockSpec((TM, 512), lambda kj, qi: (kj, 0)),
                   pl.BlockSpec((TM, 512), lambda kj, qi: (kj, 0)),
                   pl.BlockSpec((N_HEADS, 1, TM), lambda kj, qi: (0, 0, kj)),
                   pl.BlockSpec((LP, BLK), lambda kj, qi: (0, 0))] + [HBM_SPEC] * len(rider.out_shapes),
        out_shape=[SDS((LP, 512), F32), SDS((LP, 512), BF16), SDS((LP, 512), BF16), SDS((N_HEADS, 1, LP), F32),
                   SDS((LP, BLK), F32)] + rider.out_shapes,
        scratch_shapes=[pltpu.VMEM((4, TM, BLK), F32), pltpu.VMEM((4, TM, BLK), F32),
                        pltpu.VMEM((N_HEADS, 1, TM), F32)] + rider.scratch(),
        compiler_params=_cparams(("arbitrary", "arbitrary")), name="fox_bwd",
    )(proj, proj, proj, o_b, dmix, lse, ck_t, *rider.operands)
    return dq, dk, dv, dck, dcq, carried


def _build_bias(tab_ref, bkt_ref, bias_cur, bias_prev):
    bc = bkt_ref[0]
    bp = bkt_ref[1]
    for h in range(N_HEADS):
        def step(b, carry, h=h):
            t = tab_ref[b, h]
            return jnp.where(bc == b, t, carry[0]), jnp.where(bp == b, t, carry[1])
        zero = jnp.zeros((BLK, BLK), F32)
        cur, prev = lax.fori_loop(0, N_BUCKETS, step, (zero, zero))
        bias_cur[h] = cur
        bias_prev[h] = prev


def _kv_variants(ref):
    x = ref[...].astype(F32)
    swapped = pltpu.roll(x, HALF, 1)
    half = _lane_half(BLK)
    out = {}
    for e in range(2):
        for g in range(2):
            src = x if e == g else swapped
            out[(e, g)] = jnp.where(half == e, src, 0.0).astype(BF16)
    return out


def _swa_masks(i):
    qrow = lax.broadcasted_iota(jnp.int32, (BLK, BLK), 0)
    kcol = lax.broadcasted_iota(jnp.int32, (BLK, BLK), 1)
    big = 4 * BLK
    valid_cur = (kcol <= qrow) & (kcol >= jnp.where(i > 0, 0, PAD_ROWS))
    valid_prev = kcol > qrow + jnp.where(i >= 2, 0, big)
    valid_meta = kcol >= jnp.where(i >= 1, PAD_ROWS, big)
    return valid_cur, valid_prev, valid_meta


def _swa_scores(i, h, qs, kvar, bias_cur, bias_prev, tab_ref, masks):
    e, g = h % 2, h // 4
    first = jnp.full((BLK, BLK), i, jnp.int32) == 1
    biases = (bias_cur[h], bias_prev[h], jnp.where(first, bias_prev[h], tab_ref[N_BUCKETS - 1, h]))
    out = []
    for x in range(3):
        s = lax.dot_general(qs, kvar[x][(e, g)], NT_DIMS, preferred_element_type=F32) + biases[x]
        out.append(jnp.where(masks[x], s, NEG))
    return out


def _swa_fwd(proj, rel_bias, sinks, bkt):
    def body(tab_ref, sink_ref, bkt_ref, q_ref, kc_ref, kp_ref, km_ref, vc_ref, vp_ref, vm_ref,
             o_ref, lse_ref, bias_cur, bias_prev):
        i = pl.program_id(0)

        @pl.when(i == 0)
        def _():
            _build_bias(tab_ref, bkt_ref, bias_cur, bias_prev)

        masks = _swa_masks(i)
        kvar = [_kv_variants(r) for r in (kc_ref, kp_ref, km_ref)]
        vvar = [_kv_variants(r) for r in (vc_ref, vp_ref, vm_ref)]
        for pp in range(4):
            cols = slice(pp * BLK, (pp + 1) * BLK)
            qs = (q_ref[:, cols].astype(F32) * SCALE).astype(BF16)
            o_pair = jnp.zeros((BLK, BLK), F32)
            for e in range(2):
                h = 2 * pp + e
                g = h // 4
                s = _swa_scores(i, h, qs, kvar, bias_cur, bias_prev, tab_ref, masks)
                sink = sink_ref[0, h]
                m = jnp.maximum(jnp.maximum(jnp.max(s[0], axis=1, keepdims=True), jnp.max(s[1], axis=1, keepdims=True)),
                                jnp.maximum(jnp.max(s[2], axis=1, keepdims=True), sink))
                p = [jnp.exp(sx - m) for sx in s]
                denom = (jnp.sum(p[0], axis=1, keepdims=True) + jnp.sum(p[1], axis=1, keepdims=True)
                         + jnp.sum(p[2], axis=1, keepdims=True) + jnp.exp(sink - m))
                pv = (jnp.dot(p[0].astype(BF16), vvar[0][(e, g)], preferred_element_type=F32)
                      + jnp.dot(p[1].astype(BF16), vvar[1][(e, g)], preferred_element_type=F32)
                      + jnp.dot(p[2].astype(BF16), vvar[2][(e, g)], preferred_element_type=F32))
                o_pair = o_pair + pv * (1.0 / denom)
                lse_ref[h] = m + jnp.log(denom)
            o_ref[:, cols] = o_pair.astype(BF16)

    smem = pl.BlockSpec(memory_space=pltpu.SMEM)
    kv = lambda col, which: pl.BlockSpec(
        (BLK, BLK), {0: lambda i: (i, col), 1: lambda i: (jnp.maximum(i - 1, 0), col), 2: lambda i: (0, col)}[which])
    return pl.pallas_call(
        body, grid=(NBLK,),
        in_specs=[smem, smem, pl.BlockSpec((2, BLK, BLK), lambda i: (0, 0, 0)),
                  pl.BlockSpec((BLK, 512), lambda i: (i, QA)),
                  kv(KA, 0), kv(KA, 1), kv(KA, 2), kv(VA, 0), kv(VA, 1), kv(VA, 2)],
        out_specs=[pl.BlockSpec((BLK, 512), lambda i: (i, 0)),
                   pl.BlockSpec((N_HEADS, BLK, 1), lambda i: (0, i, 0))],
        out_shape=[SDS((LP, 512), BF16), SDS((N_HEADS, LP, 1), F32)],
        scratch_shapes=[pltpu.VMEM((N_HEADS, BLK, BLK), F32), pltpu.VMEM((N_HEADS, BLK, BLK), F32)],
        compiler_params=_cparams(("arbitrary",)), name="swa_fwd",
    )(rel_bias, sinks, bkt, proj, proj, proj, proj, proj, proj, proj)


def _swa_bwd(proj, o_a, dmix, lse, rel_bias, sinks, bkt):
    def body(tab_ref, sink_ref, bkt_ref, q_ref, kc_ref, kp_ref, km_ref, vc_ref, vp_ref, vm_ref,
             o_ref, do_ref, lse_ref, dq_ref, dk_ref, dv_ref, dbias_ref, dsink_ref,
             bias_cur, bias_prev, acc_cur, acc_prev, acc_far, dsk):
        i = pl.program_id(0)

        @pl.when(i == 0)
        def _():
            _build_bias(tab_ref, bkt_ref, bias_cur, bias_prev)
            dk_ref[...] = jnp.zeros_like(dk_ref)
            dv_ref[...] = jnp.zeros_like(dv_ref)
            acc_cur[...] = jnp.zeros_like(acc_cur)
            acc_prev[...] = jnp.zeros_like(acc_prev)
            acc_far[...] = jnp.zeros_like(acc_far)
            dsk[...] = jnp.zeros_like(dsk)

        masks = _swa_masks(i)
        half = _lane_half(BLK)
        first = jnp.full((BLK, BLK), i, jnp.int32) == 1
        kvar = [_kv_variants(r) for r in (kc_ref, kp_ref, km_ref)]
        vvar = [_kv_variants(r) for r in (vc_ref, vp_ref, vm_ref)]
        dk_blk = [jnp.zeros((BLK, BLK), F32) for _ in range(3)]
        dv_blk = [jnp.zeros((BLK, BLK), F32) for _ in range(3)]
        for pp in range(4):
            cols = slice(pp * BLK, (pp + 1) * BLK)
            qs = (q_ref[:, cols].astype(F32) * SCALE).astype(BF16)
            dop = do_ref[:, cols]
            prod = dop.astype(F32) * o_ref[:, cols].astype(F32)
            d0 = jnp.sum(jnp.where(half == 0, prod, 0.0), axis=1, keepdims=True)
            d1 = jnp.sum(prod, axis=1, keepdims=True) - d0
            dq = jnp.zeros((BLK, BLK), F32)
            for e in range(2):
                h = 2 * pp + e
                g = h // 4
                dd = d0 if e == 0 else d1
                lse_h = lse_ref[h]
                s = _swa_scores(i, h, qs, kvar, bias_cur, bias_prev, tab_ref, masks)
                dsk[h] += -jnp.exp(sink_ref[0, h] - lse_h) * dd
                ds_all = []
                for x in range(3):
                    p = jnp.exp(s[x] - lse_h)
                    dp = lax.dot_general(dop, vvar[x][(e, g)], NT_DIMS, preferred_element_type=F32)
                    ds = p * (dp - dd)
                    ds_all.append(ds)
                    ds_b = ds.astype(BF16)
                    dq = dq + jnp.dot(ds_b, kvar[x][(e, g)], preferred_element_type=F32)
                    dkv = lax.dot_general(ds_b, qs, TN_DIMS, preferred_element_type=F32)
                    dvv = lax.dot_general(p.astype(BF16), dop, TN_DIMS, preferred_element_type=F32)
                    if e != g:
                        dkv = pltpu.roll(dkv, HALF, 1)
                        dvv = pltpu.roll(dvv, HALF, 1)
                    dk_blk[x] = dk_blk[x] + jnp.where(half == g, dkv, 0.0)
                    dv_blk[x] = dv_blk[x] + jnp.where(half == g, dvv, 0.0)
                acc_cur[h] += ds_all[0]
                acc_prev[h] += ds_all[1] + jnp.where(first, ds_all[2], 0.0)
                acc_far[h] += jnp.where(first, 0.0, ds_all[2])
            dq_ref[:, cols] = (dq * SCALE).astype(BF16)

        cur0 = pl.multiple_of(i * BLK, BLK)
        prev0 = pl.multiple_of(jnp.maximum(i - 1, 0) * BLK, BLK)
        dk_ref[pl.ds(cur0, BLK), :] += dk_blk[0]
        dv_ref[pl.ds(cur0, BLK), :] += dv_blk[0]
        dk_ref[pl.ds(prev0, BLK), :] += dk_blk[1]
        dv_ref[pl.ds(prev0, BLK), :] += dv_blk[1]
        dk_ref[0:BLK, :] += dk_blk[2]
        dv_ref[0:BLK, :] += dv_blk[2]

        @pl.when(i == NBLK - 1)
        def _():
            bc = bkt_ref[0]
            bp = bkt_ref[1]
            lane = lax.broadcasted_iota(jnp.int32, (1, BLK), 1)

            def per_bucket(b, carry):
                row = jnp.zeros((1, BLK), F32)
                for h in range(N_HEADS):
                    val = (jnp.sum(jnp.where(bc == b, acc_cur[h], 0.0), keepdims=True)
                           + jnp.sum(jnp.where(bp == b, acc_prev[h], 0.0), keepdims=True))
                    row = jnp.where(lane == h, val, row)
                dbias_ref[pl.ds(b, 1), :] = row
                return carry

            lax.fori_loop(0, N_BUCKETS, per_bucket, 0)
            far = jnp.zeros((1, BLK), F32)
            dsr = jnp.zeros((1, BLK), F32)
            for h in range(N_HEADS):
                far = jnp.where(lane == h, jnp.sum(acc_far[h], keepdims=True), far)
                dsr = jnp.where(lane == h, jnp.sum(dsk[h], keepdims=True), dsr)
            dbias_ref[N_BUCKETS - 1:N_BUCKETS, :] += far
            dsink_ref[...] = dsr

    smem = pl.BlockSpec(memory_space=pltpu.SMEM)
    kv = lambda col, which: pl.BlockSpec(
        (BLK, BLK), {0: lambda i: (i, col), 1: lambda i: (jnp.maximum(i - 1, 0), col), 2: lambda i: (0, col)}[which])
    blk512 = lambda col: pl.BlockSpec((BLK, 512), lambda i: (i, col))
    full = lambda r, c: pl.BlockSpec((r, c), lambda i: (0, 0))
    acc = pltpu.VMEM((N_HEADS, BLK, BLK), F32)
    return pl.pallas_call(
        body, grid=(NBLK,),
        in_specs=[smem, smem, pl.BlockSpec((2, BLK, BLK), lambda i: (0, 0, 0)), blk512(QA),
                  kv(KA, 0), kv(KA, 1), kv(KA, 2), kv(VA, 0), kv(VA, 1), kv(VA, 2),
                  blk512(0), blk512(0), pl.BlockSpec((N_HEADS, BLK, 1), lambda i: (0, i, 0))],
        out_specs=[blk512(0), full(LP, BLK), full(LP, BLK), full(N_BUCKETS, BLK), full(1, BLK)],
        out_shape=[SDS((LP, 512), BF16), SDS((LP, BLK), F32), SDS((LP, BLK), F32),
                   SDS((N_BUCKETS, BLK), F32), SDS((1, BLK), F32)],
        scratch_shapes=[acc, acc, acc, acc, acc, pltpu.VMEM((N_HEADS, BLK, 1), F32)],
        compiler_params=_cparams(("arbitrary",)), name="swa_bwd",
    )(rel_bias, sinks, bkt, proj, proj, proj, proj, proj, proj, proj, o_a, dmix, lse)


def _local_step(x, tgt, meta, rel_bias, g_pre_mix, g_post_mix, g_pre_ffn, g_post_ffn, b_forget, sinks,
                w_in_b, w_out_b, ffn_rider, ffn_weights, early_grads):
    bkt = jnp.asarray(_bucket_tables())
    h0 = jnp.concatenate([jnp.zeros((PAD_ROWS, D_MODEL), F32), meta, x], axis=0)
    tgt_p = jnp.concatenate([jnp.zeros((ROW0, D_MODEL), F32), tgt], axis=0)
    b_p = jnp.pad(b_forget, ((0, 0), (0, BLK - N_HEADS)))

    hn1, proj, f = _pre_mix(h0, g_pre_mix, w_in_b)
    o_a, lse_a = _swa_fwd(proj, rel_bias, sinks, bkt)
    cum = _forget_cumsum(f, b_p)
    ck_t = cum[:, :N_HEADS].T.reshape(N_HEADS, 1, LP)
    q_aug, k_aug, v_m, k_t, v_t = _fox_prep(proj, cum)
    o_b, lse_row, carried = _fox_fwd(q_aug, k_aug, v_t, ffn_rider)
    lse_b = lse_row.reshape(N_HEADS, LP, 1)
    w_gu_b, w_dn_b = ffn_weights(carried)
    a, h1, hn2 = _attn_out(o_a, o_b, w_out_b, h0, g_post_mix, g_pre_ffn)
    g, u, act = _ffn_up(hn2, w_gu_b)
    dff, dy, loss_blk, dg_post_ffn = _ffn_down_loss(act, w_dn_b, h1, tgt_p, g_post_ffn)

    dw_dn = _mm_tn([act], dff, FF_T, "dw_down")
    dg, du = _ffn_down_bwd(dff, w_dn_b, g, u)
    dw_gu = _dw_gate_up(hn2, dg, du)
    dh1, da, dg_pre_ffn, dg_post_mix = _ffn_up_bwd(dg, du, w_gu_b, h1, a, dy, g_pre_ffn, g_post_mix)
    dw_out = _mm_tn([o_a, o_b], da, D_MODEL, "dw_out")
    dmix = _attn_out_bwd(da, w_out_b)
    dq_b, dk_b, dv_b, dck, dcq, landed = _fox_bwd(proj, o_b, dmix, lse_b, ck_t, early_grads(dw_gu, dw_dn, dw_out))
    dq_a, dk_a, dv_a, dbias, dsink = _swa_bwd(proj, o_a, dmix, lse_a, rel_bias, sinks, bkt)
    dcum = dcq - jnp.pad(dck.reshape(N_HEADS, LP).T, ((0, 0), (0, BLK - N_HEADS)))
    df, db = _forget_cumsum_bwd(dcum, f, b_p)
    dproj, dh0, dg_pre_mix = _pre_mix_bwd(dq_a, dq_b, dk_b, dv_b, dk_a, dv_a, df, w_in_b, h0, dh1, g_pre_mix)
    dw_in = _mm_tn([hn1], dproj, D_MODEL, "dw_in")

    return dict(loss=loss_blk[0, 0], grad_x=dh0[ROW0:], meta=dh0[PAD_ROWS:ROW0],
                rel_bias=dbias[:, :N_HEADS], ln_pre_mix=dg_pre_mix, ln_post_mix=dg_post_mix,
                ln_pre_ffn=dg_pre_ffn, ln_post_ffn=dg_post_ffn, b_forget=db[:, :N_HEADS],
                sinks=dsink[:, :N_HEADS], w_in=dw_in, w_out=dw_out, w_gate_up=dw_gu, w_down=dw_dn,
                landed=landed)


N_SMALL = 24
LOSS_ROW = 6


def _place():
    x, y, c = lax.axis_index("x"), lax.axis_index("y"), lax.axis_index("c")
    return x, y, c, [(1 - x, y), (x, 1 - y), (1 - x, 1 - y)]


def _run_alone(rider, name):
    a, b = len(rider.operands), len(rider.out_shapes)

    def body(*refs):
        mine = (refs[:a], refs[a:a + b], refs[a + b:])
        rider.first(*mine)
        rider.middle(*mine)
        rider.last(*mine)

    return pl.pallas_call(body, in_specs=[HBM_SPEC] * a, out_specs=[HBM_SPEC] * b, out_shape=rider.out_shapes,
                          scratch_shapes=rider.scratch(), name=name)(*rider.operands)


def _gather_rider(shards):
    n = len(shards)

    def copies(ins, outs, sems):
        send_sems, recv_sems = sems
        x, y, c, others = _place()
        chip = 2 * x + y
        sibling = (x, y, 1 - c)

        def rc(a, k, src, dst, to):
            return pltpu.make_async_remote_copy(src_ref=src, dst_ref=dst, send_sem=send_sems.at[6 * a + k],
                                                recv_sem=recv_sems.at[6 * a + k], device_id=to, device_id_type=MESH)

        pairs = [(a, j, ox, oy) for a in range(n) for j, (ox, oy) in enumerate(others)]
        sent = [rc(a, j, ins[a].at[c], outs[a].at[chip, c], (ox, oy, c)) for a, j, ox, oy in pairs]
        landed = [rc(a, j, outs[a].at[2 * ox + oy, c], outs[a].at[2 * ox + oy, c], sibling) for a, j, ox, oy in pairs]
        passed = [rc(a, 3 + j, outs[a].at[2 * ox + oy, c], outs[a].at[2 * ox + oy, c], sibling)
                  for a, j, ox, oy in pairs]
        arriving = [rc(a, 3 + j, outs[a].at[2 * ox + oy, 1 - c], outs[a].at[2 * ox + oy, 1 - c], sibling)
                    for a, j, ox, oy in pairs]
        return sent, landed, passed, arriving

    def first(*mine):
        for cp in copies(*mine)[0]:
            cp.start()

    def middle(*mine):
        _, landed, passed, _ = copies(*mine)
        for got, cp in zip(landed, passed):
            got.wait_recv()
            cp.start()

    def last(*mine):
        sent, _, passed, arriving = copies(*mine)
        for cp in arriving:
            cp.wait_recv()
        for cp in sent + passed:
            cp.wait_send()

    return _Rider(shards, [SDS((4,) + s.shape, s.dtype) for s in shards], [6 * n, 6 * n], first, middle, last)


def _swap_halves(grads, name):
    n = len(grads)

    def body(*refs):
        ins, outs = refs[:n], refs[n:2 * n]
        send_sems, recv_sems = refs[2 * n:]
        x, y, c, _ = _place()
        copies = [pltpu.make_async_remote_copy(
            src_ref=ins[a].at[s, 1 - c], dst_ref=outs[a].at[s], send_sem=send_sems.at[4 * a + s],
            recv_sem=recv_sems.at[4 * a + s], device_id=(x, y, 1 - c), device_id_type=MESH)
            for a in range(n) for s in range(4)]
        for cp in copies:
            cp.start()
        for cp in copies:
            cp.wait()

    return pl.pallas_call(
        body, in_specs=[HBM_SPEC] * n, out_specs=[HBM_SPEC] * n,
        out_shape=[SDS((4,) + g.shape[2:], g.dtype) for g in grads],
        scratch_shapes=[pltpu.SemaphoreType.DMA((4 * n,)), pltpu.SemaphoreType.DMA((4 * n,))],
        name=name)(*grads)


def _pair_sum(g, got, c_arr, name):
    rh, cc = got.shape[1:]

    def body(c_ref, g_ref, p_ref, o_ref):
        o_ref[0] = (g_ref[0, 0] + p_ref[0]).astype(BF16)

    grid_spec = pltpu.PrefetchScalarGridSpec(
        num_scalar_prefetch=1, grid=(4,),
        in_specs=[pl.BlockSpec((1, 1, rh, cc), lambda s, c_ref: (s, c_ref[0], 0, 0)),
                  pl.BlockSpec((1, rh, cc), lambda s, c_ref: (s, 0, 0))],
        out_specs=pl.BlockSpec((1, rh, cc), lambda s, c_ref: (s, 0, 0)))
    return pl.pallas_call(body, grid_spec=grid_spec, out_shape=SDS((4, rh, cc), BF16),
                          compiler_params=_cparams(("parallel",)), name=name)(c_arr, g, got)


def _exchange_rider(parts, small=None):
    n = len(parts)

    def copies(ins, outs, sems):
        x, y, c, others = _place()
        out = [pltpu.make_async_remote_copy(
            src_ref=ins[a].at[2 * ox + oy], dst_ref=outs[a].at[j], send_sem=sems[0].at[3 * a + j],
            recv_sem=sems[1].at[3 * a + j], device_id=(ox, oy, c), device_id_type=MESH)
            for a in range(n) for j, (ox, oy) in enumerate(others)]
        own = []
        if small is not None:
            me = 4 * x + 2 * y + c
            peers = [(x, y, 1 - c)] + [(ox, oy, c) for ox, oy in others] + [(ox, oy, 1 - c) for ox, oy in others]
            out += [pltpu.make_async_remote_copy(
                src_ref=ins[n], dst_ref=outs[n].at[me], send_sem=sems[2].at[k], recv_sem=sems[3].at[k],
                device_id=peer, device_id_type=MESH) for k, peer in enumerate(peers)]
            own = [pltpu.make_async_copy(ins[n], outs[n].at[me], sems[4].at[0])]
        return out, own

    def first(*mine):
        out, own = copies(*mine)
        for cp in own + out:
            cp.start()

    def middle(*mine):
        pass

    def last(*mine):
        out, own = copies(*mine)
        for cp in out + own:
            cp.wait()

    shapes = [SDS((3,) + p.shape[1:], p.dtype) for p in parts]
    if small is None:
        return _Rider(parts, shapes, [3 * n, 3 * n], first, middle, last)
    return _Rider(parts + [small], shapes + [SDS((8,) + small.shape, small.dtype)], [3 * n, 3 * n, 7, 7, 1],
                  first, middle, last)


def _chip_sum(parts, landed, chip_arr, name):
    rh, cc = landed.shape[1:]
    tr = rh // 2

    def body(chip_ref, own_ref, p_ref, o_ref):
        o_ref[...] = ((own_ref[0].astype(F32) + p_ref[0].astype(F32)) + p_ref[1].astype(F32)) + p_ref[2].astype(F32)

    grid_spec = pltpu.PrefetchScalarGridSpec(
        num_scalar_prefetch=1, grid=(2,),
        in_specs=[pl.BlockSpec((1, tr, cc), lambda i, chip_ref: (chip_ref[0], i, 0)),
                  pl.BlockSpec((3, tr, cc), lambda i, chip_ref: (0, i, 0))],
        out_specs=pl.BlockSpec((tr, cc), lambda i, chip_ref: (i, 0)))
    return pl.pallas_call(body, grid_spec=grid_spec, out_shape=SDS((rh, cc), F32),
                          compiler_params=_cparams(("parallel",)), name=name)(chip_arr, parts, landed)


def _device_sum(p):
    def body(p_ref, o_ref):
        acc = p_ref[0]
        for k in range(1, 8):
            acc = acc + p_ref[k]
        o_ref[...] = acc

    return pl.pallas_call(body, out_shape=SDS(p.shape[1:], F32), name="small_sum")(p)


def _join_halves(halves):
    n = len(halves)

    def body(*refs):
        ins, outs = refs[:n], refs[n:2 * n]
        send_sems, recv_sems = refs[2 * n:]
        x, y, c, _ = _place()
        copies = [pltpu.make_async_remote_copy(
            src_ref=ins[a], dst_ref=outs[a], send_sem=send_sems.at[a], recv_sem=recv_sems.at[a],
            device_id=(x, y, 1 - c), device_id_type=MESH) for a in range(n)]
        for cp in copies:
            cp.start()
        for cp in copies:
            cp.wait()

    return pl.pallas_call(
        body, in_specs=[HBM_SPEC] * n, out_specs=[HBM_SPEC] * n,
        out_shape=[SDS(h.shape, h.dtype) for h in halves],
        scratch_shapes=[pltpu.SemaphoreType.DMA((n,)), pltpu.SemaphoreType.DMA((n,))],
        name="join_halves")(*halves)


def _adamw(w, g, m, v, name):
    rows, cols = w.shape
    tr = rows if rows <= 352 else (256 if rows % 256 == 0 else 352)

    def body(w_ref, g_ref, m_ref, v_ref, d_ref, nm_ref, nv_ref):
        gg = g_ref[...]
        nm = ADAM_B1 * m_ref[...] + (1.0 - ADAM_B1) * gg
        nv = ADAM_B2 * v_ref[...] + (1.0 - ADAM_B2) * (gg * gg)
        nm_ref[...] = nm
        nv_ref[...] = nv
        m_hat = nm / (1.0 - ADAM_B1 ** ADAM_STEP)
        v_hat = nv / (1.0 - ADAM_B2 ** ADAM_STEP)
        d_ref[...] = -ADAM_LR * (m_hat / (jnp.sqrt(v_hat) + ADAM_EPS) + ADAM_WD * w_ref[...])

    blk = pl.BlockSpec((tr, cols), lambda i: (i, 0))
    return pl.pallas_call(
        body, grid=(rows // tr,), in_specs=[blk] * 4, out_specs=[blk] * 3,
        out_shape=[SDS((rows, cols), F32)] * 3,
        compiler_params=_cparams(("parallel",)), name=name)(w, g, m, v)


def _pack_small(pre_mix, post_mix, pre_ffn, post_ffn, rel_bias, b_forget, sinks):
    def at(row, v):
        return jnp.pad(v, ((row, 7 - row), (0, D_MODEL - v.shape[1])))
    return (at(0, pre_mix) + at(1, post_mix) + at(2, pre_ffn) + at(3, post_ffn)
            + at(4, rel_bias.reshape(1, N_BUCKETS * N_HEADS)) + at(5, jnp.concatenate([b_forget, sinks], axis=1)))


def _unpack_small(p):
    return dict(ln_pre_mix=p[0:1], ln_post_mix=p[1:2], ln_pre_ffn=p[2:3], ln_post_ffn=p[3:4],
                rel_bias=p[4, :N_BUCKETS * N_HEADS].reshape(N_BUCKETS, N_HEADS),
                b_forget=p[5:6, 0:N_HEADS], sinks=p[5:6, N_HEADS:2 * N_HEADS])


WEIGHTS = ("meta_tokens", "rel_bias", "ln_pre_mix", "ln_post_mix", "ln_pre_ffn", "ln_post_ffn",
           "w_in", "b_forget", "sinks", "w_out", "w_gate_up", "w_down")


def kernel(x, meta_tokens, rel_bias, ln_pre_mix, ln_post_mix, ln_pre_ffn, ln_post_ffn, w_in, b_forget, sinks, w_out, w_gate_up, w_down, loss_target, m_meta_tokens, m_rel_bias, m_ln_pre_mix, m_ln_post_mix, m_ln_pre_ffn, m_ln_post_ffn, m_w_in, m_b_forget, m_sinks, m_w_out, m_w_gate_up, m_w_down, v_meta_tokens, v_rel_bias, v_ln_pre_mix, v_ln_post_mix, v_ln_pre_ffn, v_ln_post_ffn, v_w_in, v_b_forget, v_sinks, v_w_out, v_w_gate_up, v_w_down):
    xi, yi, ci = lax.axis_index("x"), lax.axis_index("y"), lax.axis_index("c")
    chip = 2 * xi + yi
    c_arr = jnp.reshape(ci, (1,)).astype(jnp.int32)

    def halves(w, dtype):
        return w.astype(dtype).reshape(2, w.shape[0] // 2, w.shape[1])

    def with_own(gathered, shards):
        return [lax.dynamic_update_slice(got, own[None], (chip, 0, 0, 0)) for got, own in zip(gathered, shards)]

    shards = [halves(w_in[0], BF16), halves(w_out[0], BF16), halves(meta_tokens, F32)]
    gw_in, gw_out, g_meta = with_own(_run_alone(_gather_rider(shards), "gather_mixer_weights"), shards)
    ffn_shards = [halves(w_gate_up[0], BF16), halves(w_down[0], BF16)]

    def ffn_weights(carried):
        gw_gu, gw_dn = with_own(carried, ffn_shards)
        w_gu_b = gw_gu.reshape(4, D_MODEL, FF_T).transpose(1, 0, 2).reshape(D_MODEL, 2 * D_FF)
        return w_gu_b, gw_dn.reshape(D_FF, D_MODEL)

    early = {}

    def early_grads(dw_gu, dw_dn, dw_out):
        grads = [dw_out.reshape(4, 2, 128, D_MODEL), dw_gu.reshape(4, 2, 512, FF_T), dw_dn.reshape(4, 2, 352, D_MODEL)]
        got = _swap_halves(grads, "swap_halves_early")
        early["parts"] = [_pair_sum(g, p, c_arr, "pair_sum_%d" % a) for a, (g, p) in enumerate(zip(grads, got))]
        return _exchange_rider(early["parts"])
    w_in_all = gw_in.reshape(4, D_MODEL, D_PROJ // 4).transpose(1, 0, 2).reshape(D_MODEL, D_PROJ)
    w_in_b = jnp.concatenate(
        [w_in_all[:, 0:512], w_in_all[:, 768:1280], w_in_all[:, 1280:1792], w_in_all[:, 1792:2304],
         w_in_all[:, 512:640], w_in_all[:, 640:768], w_in_all[:, 2304:2312],
         jnp.zeros((D_MODEL, D_PROJ_P - D_PROJ), BF16)], axis=1)
    meta_all = g_meta.reshape(4, N_META, D_MODEL // 4).transpose(1, 0, 2).reshape(N_META, D_MODEL)

    loc = _local_step(x[0], loss_target[0], meta_all, rel_bias, ln_pre_mix, ln_post_mix, ln_pre_ffn, ln_post_ffn,
                      b_forget, sinks, w_in_b, gw_out.reshape(D_MODEL, D_MODEL),
                      _gather_rider(ffn_shards), ffn_weights, early_grads)

    n = loc["w_in"]
    dw_in = jnp.concatenate([n[:, 0:512], n[:, 2048:2176], n[:, 2176:2304], n[:, 512:1024], n[:, 1024:1536],
                             n[:, 1536:2048], n[:, 2304:2312]], axis=1)
    dw_in = dw_in.reshape(D_MODEL, 4, D_PROJ // 4).transpose(1, 0, 2).reshape(4, 2, 512, D_PROJ // 4)
    small = jnp.concatenate(
        [_pack_small(loc["ln_pre_mix"], loc["ln_post_mix"], loc["ln_pre_ffn"], loc["ln_post_ffn"],
                     loc["rel_bias"], loc["b_forget"], loc["sinks"])
         + jnp.pad(loc["loss"].reshape(1, 1), ((LOSS_ROW, 7 - LOSS_ROW), (0, D_MODEL - 1))), loc["meta"]], axis=0)

    (got_in,) = _swap_halves([dw_in], "swap_halves_late")
    part_in = _pair_sum(dw_in, got_in, c_arr, "pair_sum_late")
    landed_in, small_all = _run_alone(_exchange_rider([part_in], small), "exchange_late")
    chip_arr = jnp.reshape(chip, (1,)).astype(jnp.int32)
    parts = [part_in] + early["parts"]
    landed = [landed_in] + loc["landed"]
    mine = [_chip_sum(p, l, chip_arr, "chip_sum_%d" % a) for a, (p, l) in enumerate(zip(parts, landed))]
    small_sum = _device_sum(small_all)
    theirs = _join_halves(mine)
    full = [jnp.where(ci == 0, jnp.concatenate([m, t], axis=0), jnp.concatenate([t, m], axis=0))
            for m, t in zip(mine, theirs)]
    g_w_in, g_w_out, g_w_gu, g_w_dn = full
    g_meta_tokens = lax.dynamic_slice(small_sum[8:N_SMALL], (0, chip * (D_MODEL // 4)), (N_META, D_MODEL // 4))
    g_small = small_sum[0:8]

    grad = _unpack_small(g_small)
    grad.update(meta_tokens=g_meta_tokens, w_in=g_w_in[None], w_out=g_w_out[None], w_gate_up=g_w_gu[None],
                w_down=g_w_dn[None])

    delta, new_m, new_v = {}, {}, {}
    big = dict(w_in=(w_in, m_w_in, v_w_in, g_w_in), w_out=(w_out, m_w_out, v_w_out, g_w_out),
               w_gate_up=(w_gate_up, m_w_gate_up, v_w_gate_up, g_w_gu), w_down=(w_down, m_w_down, v_w_down, g_w_dn))
    for name, (w, m, v, g) in big.items():
        d, nm, nv = _adamw(w[0], g, m[0], v[0], "adamw_" + name)
        delta[name], new_m[name], new_v[name] = d[None], nm[None], nv[None]
    delta["meta_tokens"], new_m["meta_tokens"], new_v["meta_tokens"] = _adamw(
        meta_tokens, g_meta_tokens, m_meta_tokens, v_meta_tokens, "adamw_meta")
    d, nm, nv = _adamw(
        _pack_small(ln_pre_mix, ln_post_mix, ln_pre_ffn, ln_post_ffn, rel_bias, b_forget, sinks), g_small,
        _pack_small(m_ln_pre_mix, m_ln_post_mix, m_ln_pre_ffn, m_ln_post_ffn, m_rel_bias, m_b_forget, m_sinks),
        _pack_small(v_ln_pre_mix, v_ln_post_mix, v_ln_pre_ffn, v_ln_post_ffn, v_rel_bias, v_b_forget, v_sinks),
        "adamw_small")
    delta.update(_unpack_small(d))
    new_m.update(_unpack_small(nm))
    new_v.update(_unpack_small(nv))

    loss = small_sum[LOSS_ROW, 0]
    return (loss,loc["grad_x"][None], *[grad[k] for k in WEIGHTS], *[delta[k] for k in WEIGHTS],
            *[new_m[k] for k in WEIGHTS], *[new_v[k] for k in WEIGHTS])
```

```python
import math

import numpy as np
import jax
import jax.numpy as jnp
from jax import lax
from jax.experimental import pallas as pl
from jax.experimental.pallas import tpu as pltpu

F32 = jnp.float32
BF16 = jnp.bfloat16
MESH = pl.DeviceIdType.MESH
SDS = jax.ShapeDtypeStruct

D_MODEL = 1024
SEQ = 4096
N_META = 16
N_HEADS = 8
HALF = 64
D_FF = 2816
N_BUCKETS = 32
EPS = 1e-6
NEG = -1e30
SCALE = 0.125
PAD_ROWS = 112
ROW0 = PAD_ROWS + N_META
LP = ROW0 + SEQ
BLK = 128
NBLK = LP // BLK
TM = 384
NT = LP // TM
TM_PURE = LP // 2
TM_MID = LP // 4
TM_EPI = LP // 6
TN = 256
D_PROJ = 2312
D_PROJ_P = 2432
D_QKV = 2304
FF_T = 1408
VMEM_LIMIT = 56 * 1024 * 1024

ADAM_LR = 0.001
ADAM_B1 = 0.9
ADAM_B2 = 0.999
ADAM_EPS = 1e-08
ADAM_WD = 0.01
ADAM_STEP = 10

QA, QB, KB, VB = 0, 1, 2, 3
KA, VA = 16, 17

NT_DIMS = (((1,), (1,)), ((), ()))
TN_DIMS = (((0,), (0,)), ((), ()))


def _cparams(sem):
    return pltpu.CompilerParams(dimension_semantics=sem, vmem_limit_bytes=VMEM_LIMIT)


def _t5_bucket_np(d):
    n = np.maximum(d, 0).astype(np.int32)
    nf = np.maximum(n, 1).astype(np.float32)
    large = 16 + (np.log(nf / np.float32(16)) / np.float32(math.log(8.0)) * np.float32(16)).astype(np.int32)
    large = np.minimum(large, N_BUCKETS - 1)
    return np.where(n < 16, n, large).astype(np.int32)


def _bucket_tables():
    qi = np.arange(BLK)[:, None]
    ki = np.arange(BLK)[None, :]
    return np.stack([_t5_bucket_np(qi - ki), _t5_bucket_np(qi - ki + BLK)])


def _rms(x):
    return lax.rsqrt(jnp.mean(x * x, axis=-1, keepdims=True) + EPS)


def _rms_bwd(n, r, gdy):
    return r * (gdy - n * jnp.mean(n * gdy, axis=-1, keepdims=True))


def _pre_mix(h0, gain, w_in_b):
    half = D_QKV // 2

    def body(h_ref, g_ref, w_ref, hn_ref, proj_ref, f_ref):
        x = h_ref[...]
        hn = (x * _rms(x) * g_ref[...]).astype(BF16)
        hn_ref[...] = hn
        proj_ref[:, :half] = jnp.dot(hn, w_ref[:, :half], preferred_element_type=F32).astype(BF16)
        p = jnp.dot(hn, w_ref[:, half:], preferred_element_type=F32)
        proj_ref[:, half:] = p[:, :half].astype(BF16)
        f_ref[...] = p[:, half:]

    return pl.pallas_call(
        body, grid=(LP // TM_MID,),
        in_specs=[pl.BlockSpec((TM_MID, D_MODEL), lambda i: (i, 0)),
                  pl.BlockSpec((1, D_MODEL), lambda i: (0, 0)),
                  pl.BlockSpec((D_MODEL, D_PROJ_P), lambda i: (0, 0))],
        out_specs=[pl.BlockSpec((TM_MID, D_MODEL), lambda i: (i, 0)),
                   pl.BlockSpec((TM_MID, D_QKV), lambda i: (i, 0)),
                   pl.BlockSpec((TM_MID, BLK), lambda i: (i, 0))],
        out_shape=[SDS((LP, D_MODEL), BF16), SDS((LP, D_QKV), BF16), SDS((LP, BLK), F32)],
        compiler_params=_cparams(("parallel",)), name="pre_mix")(h0, gain, w_in_b)


def _attn_out(o_a, o_b, w_out_b, h0, g_post, g_pre_ffn):
    def body(oa_ref, ob_ref, w_ref, h0_ref, gp_ref, gf_ref, a_ref, h1_ref, hn2_ref):
        a = (jnp.dot(oa_ref[...], w_ref[0:512, :], preferred_element_type=F32)
             + jnp.dot(ob_ref[...], w_ref[512:1024, :], preferred_element_type=F32))
        a_ref[...] = a
        h1 = h0_ref[...] + a * _rms(a) * gp_ref[...]
        h1_ref[...] = h1
        hn2_ref[...] = (h1 * _rms(h1) * gf_ref[...]).astype(BF16)

    row = lambda w: pl.BlockSpec((TM_EPI, w), lambda i: (i, 0))
    vec = pl.BlockSpec((1, D_MODEL), lambda i: (0, 0))
    return pl.pallas_call(
        body, grid=(LP // TM_EPI,),
        in_specs=[row(512), row(512), pl.BlockSpec((D_MODEL, D_MODEL), lambda i: (0, 0)), row(D_MODEL), vec, vec],
        out_specs=[row(D_MODEL), row(D_MODEL), row(D_MODEL)],
        out_shape=[SDS((LP, D_MODEL), F32), SDS((LP, D_MODEL), F32), SDS((LP, D_MODEL), BF16)],
        compiler_params=_cparams(("parallel",)), name="attn_out")(o_a, o_b, w_out_b, h0, g_post, g_pre_ffn)


def _ffn_up(hn2, w_gu_b):
    def body(x_ref, wg_ref, wu_ref, g_ref, u_ref, act_ref):
        x = x_ref[...]
        g = jnp.dot(x, wg_ref[...], preferred_element_type=F32)
        u = jnp.dot(x, wu_ref[...], preferred_element_type=F32)
        g_ref[...] = g.astype(BF16)
        u_ref[...] = u.astype(BF16)
        act_ref[...] = (g * (1.0 / (1.0 + jnp.exp(-g))) * u).astype(BF16)

    out = pl.BlockSpec((TM_PURE, TN), lambda i, j: (i, j))
    return pl.pallas_call(
        body, grid=(LP // TM_PURE, D_FF // TN),
        in_specs=[pl.BlockSpec((TM_PURE, D_MODEL), lambda i, j: (i, 0)),
                  pl.BlockSpec((D_MODEL, TN), lambda i, j: (0, j)),
                  pl.BlockSpec((D_MODEL, TN), lambda i, j: (0, j + D_FF // TN))],
        out_specs=[out, out, out],
        out_shape=[SDS((LP, D_FF), BF16)] * 3,
        compiler_params=_cparams(("parallel", "parallel")), name="ffn_up")(hn2, w_gu_b, w_gu_b)


def _ffn_down_loss(act, w_dn_b, h1, tgt_p, g_post_ffn):
    def body(act_ref, w_ref, h1_ref, t_ref, g_ref, dff_ref, dy_ref, loss_ref, dg_ref):
        i = pl.program_id(0)

        @pl.when(i == 0)
        def _():
            loss_ref[...] = jnp.zeros_like(loss_ref)
            dg_ref[...] = jnp.zeros_like(dg_ref)

        ff = jnp.dot(act_ref[...], w_ref[...], preferred_element_type=F32)
        r = _rms(ff)
        n = ff * r
        g = g_ref[...]
        y = h1_ref[...] + n * g
        rows = i * TM + lax.broadcasted_iota(jnp.int32, (TM, D_MODEL), 0)
        diff = jnp.where(rows >= ROW0, y - t_ref[...], 0.0)
        loss_ref[...] += 0.5 * jnp.sum(diff * diff) / D_MODEL
        dy = diff / D_MODEL
        dy_ref[...] = dy
        dg_ref[...] += jnp.sum(dy * n, axis=0, keepdims=True)
        dff_ref[...] = _rms_bwd(n, r, g * dy).astype(BF16)

    row = pl.BlockSpec((TM, D_MODEL), lambda i: (i, 0))
    return pl.pallas_call(
        body, grid=(NT,),
        in_specs=[pl.BlockSpec((TM, D_FF), lambda i: (i, 0)), pl.BlockSpec((D_FF, D_MODEL), lambda i: (0, 0)),
                  row, row, pl.BlockSpec((1, D_MODEL), lambda i: (0, 0))],
        out_specs=[row, row, pl.BlockSpec((8, BLK), lambda i: (0, 0)), pl.BlockSpec((1, D_MODEL), lambda i: (0, 0))],
        out_shape=[SDS((LP, D_MODEL), BF16), SDS((LP, D_MODEL), F32), SDS((8, BLK), F32), SDS((1, D_MODEL), F32)],
        compiler_params=_cparams(("arbitrary",)), name="ffn_down_loss")(act, w_dn_b, h1, tgt_p, g_post_ffn)


def _ffn_down_bwd(dff, w_dn_b, g, u):
    def body(d_ref, w_ref, g_ref, u_ref, dg_ref, du_ref):
        dact = lax.dot_general(d_ref[...], w_ref[...], NT_DIMS, preferred_element_type=F32)
        gg = g_ref[...].astype(F32)
        sig = 1.0 / (1.0 + jnp.exp(-gg))
        dg_ref[...] = (dact * u_ref[...].astype(F32) * sig * (1.0 + gg * (1.0 - sig))).astype(BF16)
        du_ref[...] = (dact * gg * sig).astype(BF16)

    blk = pl.BlockSpec((TM_PURE, TN), lambda i, j: (i, j))
    return pl.pallas_call(
        body, grid=(LP // TM_PURE, D_FF // TN),
        in_specs=[pl.BlockSpec((TM_PURE, D_MODEL), lambda i, j: (i, 0)),
                  pl.BlockSpec((TN, D_MODEL), lambda i, j: (j, 0)), blk, blk],
        out_specs=[blk, blk],
        out_shape=[SDS((LP, D_FF), BF16)] * 2,
        compiler_params=_cparams(("parallel", "parallel")), name="ffn_down_bwd")(dff, w_dn_b, g, u)


def _ffn_up_bwd(dg, du, w_gu_b, h1, a, dy, g_pre_ffn, g_post_mix):
    def body(dg_ref, du_ref, w_ref, h1_ref, a_ref, dy_ref, gf_ref, gp_ref,
             dh1_ref, da_ref, dgf_ref, dgp_ref, acc):
        i = pl.program_id(0)
        s = pl.program_id(1)

        @pl.when((i == 0) & (s == 0))
        def _():
            dgf_ref[...] = jnp.zeros_like(dgf_ref)
            dgp_ref[...] = jnp.zeros_like(dgp_ref)

        @pl.when(s == 0)
        def _():
            acc[...] = jnp.zeros_like(acc)

        @pl.when(s < 2)
        def _():
            acc[...] += lax.dot_general(dg_ref[...], w_ref[...], NT_DIMS, preferred_element_type=F32)

        @pl.when(s >= 2)
        def _():
            acc[...] += lax.dot_general(du_ref[...], w_ref[...], NT_DIMS, preferred_element_type=F32)

        @pl.when(s == 3)
        def _():
            dhn2 = acc[...]
            h1 = h1_ref[...]
            r2 = _rms(h1)
            n2 = h1 * r2
            dgf_ref[...] += jnp.sum(dhn2 * n2, axis=0, keepdims=True)
            dh1 = dy_ref[...] + _rms_bwd(n2, r2, gf_ref[...] * dhn2)
            dh1_ref[...] = dh1
            av = a_ref[...]
            ra = _rms(av)
            na = av * ra
            dgp_ref[...] += jnp.sum(dh1 * na, axis=0, keepdims=True)
            da_ref[...] = _rms_bwd(na, ra, gp_ref[...] * dh1).astype(BF16)

    row = pl.BlockSpec((TM_EPI, D_MODEL), lambda i, s: (i, 0))
    vec = pl.BlockSpec((1, D_MODEL), lambda i, s: (0, 0))
    return pl.pallas_call(
        body, grid=(LP // TM_EPI, 4),
        in_specs=[pl.BlockSpec((TM_EPI, FF_T), lambda i, s: (i, jnp.minimum(s, 1))),
                  pl.BlockSpec((TM_EPI, FF_T), lambda i, s: (i, jnp.maximum(s - 2, 0))),
                  pl.BlockSpec((D_MODEL, FF_T), lambda i, s: (0, s)),
                  row, row, row, vec, vec],
        out_specs=[row, row, vec, vec],
        out_shape=[SDS((LP, D_MODEL), F32), SDS((LP, D_MODEL), BF16), SDS((1, D_MODEL), F32), SDS((1, D_MODEL), F32)],
        scratch_shapes=[pltpu.VMEM((TM_EPI, D_MODEL), F32)],
        compiler_params=_cparams(("arbitrary", "arbitrary")), name="ffn_up_bwd",
    )(dg, du, w_gu_b, h1, a, dy, g_pre_ffn, g_post_mix)


def _attn_out_bwd(da, w_out_b):
    def body(d_ref, w_ref, o_ref):
        o_ref[...] = lax.dot_general(d_ref[...], w_ref[...], NT_DIMS, preferred_element_type=F32).astype(BF16)

    row = pl.BlockSpec((TM_PURE, D_MODEL), lambda i: (i, 0))
    return pl.pallas_call(
        body, grid=(LP // TM_PURE,),
        in_specs=[row, pl.BlockSpec((D_MODEL, D_MODEL), lambda i: (0, 0))],
        out_specs=row, out_shape=SDS((LP, D_MODEL), BF16),
        compiler_params=_cparams(("parallel",)), name="attn_out_bwd")(da, w_out_b)


def _pre_mix_bwd(dq_a, dq_b, dk_b, dv_b, dk_a, dv_a, df, w_in_b, h0, dh1, g_pre_mix):
    def body(qa_ref, qb_ref, kb_ref, vb_ref, ka_ref, va_ref, f_ref, w_ref, h0_ref, dh1_ref, g_ref,
             dproj_ref, dh0_ref, dg_ref):
        i = pl.program_id(0)

        @pl.when(i == 0)
        def _():
            dg_ref[...] = jnp.zeros_like(dg_ref)

        dproj = jnp.concatenate(
            [qa_ref[...], (qb_ref[...] * SCALE).astype(BF16), kb_ref[...], vb_ref[...],
             ka_ref[...].astype(BF16), va_ref[...].astype(BF16), f_ref[...].astype(BF16)], axis=1)
        dproj_ref[...] = dproj
        dhn = lax.dot_general(dproj, w_ref[...], NT_DIMS, preferred_element_type=F32)
        x = h0_ref[...]
        r = _rms(x)
        n = x * r
        dg_ref[...] += jnp.sum(dhn * n, axis=0, keepdims=True)
        dh0_ref[...] = dh1_ref[...] + _rms_bwd(n, r, g_ref[...] * dhn)

    row = lambda w: pl.BlockSpec((TM_EPI, w), lambda i: (i, 0))
    vec = pl.BlockSpec((1, D_MODEL), lambda i: (0, 0))
    return pl.pallas_call(
        body, grid=(LP // TM_EPI,),
        in_specs=[row(512), row(512), row(512), row(512), row(BLK), row(BLK), row(BLK),
                  pl.BlockSpec((D_MODEL, D_PROJ_P), lambda i: (0, 0)), row(D_MODEL), row(D_MODEL), vec],
        out_specs=[row(D_PROJ_P), row(D_MODEL), vec],
        out_shape=[SDS((LP, D_PROJ_P), BF16), SDS((LP, D_MODEL), F32), SDS((1, D_MODEL), F32)],
        compiler_params=_cparams(("arbitrary",)), name="pre_mix_bwd",
    )(dq_a, dq_b, dk_b, dv_b, dk_a, dv_a, df, w_in_b, h0, dh1, g_pre_mix)


def _mm_tn(parts, b, tm, name):
    widths = [p.shape[1] for p in parts]
    m_total = sum(widths)
    n = b.shape[1]
    whole = len(parts) > 1
    assert (tm == m_total) if whole else (m_total % tm == 0)

    def body(*refs):
        a_refs, b_ref, o_ref = refs[:-2], refs[-2], refs[-1]

        @pl.when(pl.program_id(1) == 0)
        def _():
            o_ref[...] = jnp.zeros_like(o_ref)
        a = a_refs[0][...] if not whole else jnp.concatenate([r[...] for r in a_refs], axis=1)
        o_ref[...] += lax.dot_general(a, b_ref[...], TN_DIMS, preferred_element_type=F32)

    a_specs = ([pl.BlockSpec((TM_MID, w), lambda mi, k: (k, 0)) for w in widths] if whole
               else [pl.BlockSpec((TM_MID, tm), lambda mi, k: (k, mi))])
    return pl.pallas_call(
        body, grid=(m_total // tm, LP // TM_MID),
        in_specs=a_specs + [pl.BlockSpec((TM_MID, n), lambda mi, k: (k, 0))],
        out_specs=pl.BlockSpec((tm, n), lambda mi, k: (mi, 0)),
        out_shape=SDS((m_total, n), F32),
        compiler_params=_cparams(("parallel", "arbitrary")), name=name)(*parts, b)


def _dw_gate_up(hn2, dg, du):
    def body(a_ref, dg_ref, du_ref, o_ref):
        s = pl.program_id(0)

        @pl.when(pl.program_id(1) == 0)
        def _():
            o_ref[...] = jnp.zeros_like(o_ref)

        @pl.when(s < 2)
        def _():
            o_ref[0] += lax.dot_general(a_ref[...], dg_ref[...], TN_DIMS, preferred_element_type=F32)

        @pl.when(s >= 2)
        def _():
            o_ref[0] += lax.dot_general(a_ref[...], du_ref[...], TN_DIMS, preferred_element_type=F32)

    return pl.pallas_call(
        body, grid=(4, LP // TM_MID),
        in_specs=[pl.BlockSpec((TM_MID, D_MODEL), lambda s, k: (k, 0)),
                  pl.BlockSpec((TM_MID, FF_T), lambda s, k: (k, jnp.minimum(s, 1))),
                  pl.BlockSpec((TM_MID, FF_T), lambda s, k: (k, jnp.maximum(s - 2, 0)))],
        out_specs=pl.BlockSpec((1, D_MODEL, FF_T), lambda s, k: (s, 0, 0)),
        out_shape=SDS((4, D_MODEL, FF_T), F32),
        compiler_params=_cparams(("parallel", "arbitrary")), name="dw_gate_up")(hn2, dg, du)


def _split3(x):
    hi = x.astype(BF16)
    r1 = x - hi.astype(F32)
    mid = r1.astype(BF16)
    lo = (r1 - mid.astype(F32)).astype(BF16)
    return hi, mid, lo


def _tri_matmul(tri, x):
    hi, mid, lo = _split3(x)
    dot = lambda t: jnp.dot(tri, t, preferred_element_type=F32)
    return dot(hi) + dot(mid) + dot(lo)


def _forget_cumsum(f, b_forget_p):
    def body(f_ref, b_ref, cum_ref, carry):
        i = pl.program_id(0)

        @pl.when(i == 0)
        def _():
            carry[...] = jnp.zeros_like(carry)

        z = f_ref[...] + b_ref[...]
        ls = jnp.minimum(z, 0.0) - jnp.log(1.0 + jnp.exp(-jnp.abs(z)))
        rows = i * BLK + lax.broadcasted_iota(jnp.int32, (BLK, BLK), 0)
        ls = jnp.where(rows >= PAD_ROWS, ls, 0.0)
        r = lax.broadcasted_iota(jnp.int32, (BLK, BLK), 0)
        c = lax.broadcasted_iota(jnp.int32, (BLK, BLK), 1)
        tri = (c <= r).astype(BF16)
        cum = _tri_matmul(tri, ls) + carry[...]
        cum_ref[...] = cum
        carry[...] = cum[BLK - 1:BLK, :]

    return pl.pallas_call(
        body, grid=(NBLK,),
        in_specs=[pl.BlockSpec((BLK, BLK), lambda i: (i, 0)), pl.BlockSpec((1, BLK), lambda i: (0, 0))],
        out_specs=pl.BlockSpec((BLK, BLK), lambda i: (i, 0)),
        out_shape=SDS((LP, BLK), F32),
        scratch_shapes=[pltpu.VMEM((1, BLK), F32)],
        compiler_params=_cparams(("arbitrary",)), name="forget_cumsum")(f, b_forget_p)


def _forget_cumsum_bwd(dcum, f, b_forget_p):
    def body(d_ref, f_ref, b_ref, df_ref, db_ref, carry):
        i = pl.program_id(0)

        @pl.when(i == 0)
        def _():
            carry[...] = jnp.zeros_like(carry)
            db_ref[...] = jnp.zeros_like(db_ref)

        blk = NBLK - 1 - i
        r = lax.broadcasted_iota(jnp.int32, (BLK, BLK), 0)
        c = lax.broadcasted_iota(jnp.int32, (BLK, BLK), 1)
        tri = (c >= r).astype(BF16)
        d = d_ref[...]
        dls = _tri_matmul(tri, d) + carry[...]
        carry[...] = dls[0:1, :]
        z = f_ref[...] + b_ref[...]
        rows = blk * BLK + r
        df = jnp.where(rows >= PAD_ROWS, dls / (1.0 + jnp.exp(z)), 0.0)
        df_ref[...] = df
        db_ref[...] += jnp.sum(df, axis=0, keepdims=True)

    rev = pl.BlockSpec((BLK, BLK), lambda i: (NBLK - 1 - i, 0))
    vec = pl.BlockSpec((1, BLK), lambda i: (0, 0))
    return pl.pallas_call(
        body, grid=(NBLK,),
        in_specs=[rev, rev, vec],
        out_specs=[rev, vec],
        out_shape=[SDS((LP, BLK), F32), SDS((1, BLK), F32)],
        scratch_shapes=[pltpu.VMEM((1, BLK), F32)],
        compiler_params=_cparams(("arbitrary",)), name="forget_cumsum_bwd")(dcum, f, b_forget_p)


def _lane_half(rows):
    return lax.broadcasted_iota(jnp.int32, (rows, BLK), 1) // HALF


def _fox_valid(qi, kj):
    qrow = qi * TM + lax.broadcasted_iota(jnp.int32, (TM, TM), 0)
    krow = kj * TM + lax.broadcasted_iota(jnp.int32, (TM, TM), 1)
    return (krow <= qrow) & ((krow >= PAD_ROWS) | (qrow < PAD_ROWS))


class _Rider:
    def __init__(self, operands, out_shapes, sem_counts, first, middle, last):
        self.operands, self.out_shapes, self.sem_counts = list(operands), list(out_shapes), list(sem_counts)
        self.first, self.middle, self.last = first, middle, last

    def scratch(self):
        return [pltpu.SemaphoreType.DMA((k,)) for k in self.sem_counts]

    def split(self, refs, n_in, n_out, n_scratch):
        a, b = len(self.operands), len(self.out_shapes)
        ins, mine_in = refs[:n_in], refs[n_in:n_in + a]
        outs, mine_out = refs[n_in + a:n_in + a + n_out], refs[n_in + a + n_out:n_in + a + n_out + b]
        rest = refs[n_in + a + n_out + b:]
        return ins, outs, rest[:n_scratch], (mine_in, mine_out, rest[n_scratch:])

    def at_steps(self, mine, is_first, is_middle, is_last):
        for cond, fn in ((is_first, self.first), (is_middle, self.middle), (is_last, self.last)):
            pl.when(cond)(lambda fn=fn: fn(*mine))


HBM_SPEC = pl.BlockSpec(memory_space=pltpu.HBM)


N_AUG = 4
QCH = 128
KSUB = 384


def _fox_prep(proj, cum):
    def body(q_ref, k_ref, v_ref, c_ref, qa_ref, ka_ref, vt_ref):
        half = _lane_half(BLK)
        lane = lax.broadcasted_iota(jnp.int32, (BLK, BLK), 1)
        for pp in range(4):
            cols = slice(pp * BLK, (pp + 1) * BLK)
            qs = q_ref[:, cols].astype(F32) * SCALE
            kp = k_ref[:, cols].astype(F32)
            vp = v_ref[:, cols]
            vt_ref[cols, :] = vp.astype(F32).T.astype(BF16)
            for e in range(2):
                h = 2 * pp + e
                a = (1 - e) * HALF
                blk = slice(h * BLK, (h + 1) * BLK)
                hi, mid, lo = _split3(-c_ref[:, h:h + 1])
                q_aug = jnp.where(half == e, qs, jnp.where((lane >= a) & (lane < a + 3), 1.0, 0.0))
                k_aug = jnp.where(half == e, kp, jnp.where(
                    lane == a, hi.astype(F32), jnp.where(lane == a + 1, mid.astype(F32), jnp.where(
                        lane == a + 2, lo.astype(F32), jnp.where(lane == a + 3, 1.0, 0.0)))))
                qa_ref[:, blk] = q_aug.astype(BF16)
                ka_ref[:, blk] = k_aug.astype(BF16)

    row = lambda blk: pl.BlockSpec((BLK, 512), lambda i: (i, blk))
    wide = pl.BlockSpec((BLK, 1024), lambda i: (i, 0))
    return pl.pallas_call(
        body, grid=(NBLK,),
        in_specs=[row(QB), row(KB), row(VB), pl.BlockSpec((BLK, BLK), lambda i: (i, 0))],
        out_specs=[wide, wide, pl.BlockSpec((512, BLK), lambda i: (0, i))],
        out_shape=[SDS((LP, 1024), BF16)] * 2 + [SDS((512, LP), BF16)],
        compiler_params=_cparams(("parallel",)), name="fox_prep")(proj, proj, proj, cum)


def _over_keys(reduce, x):
    slabs = x.reshape(x.shape[0] // HALF, HALF, x.shape[1])
    return reduce(reduce(slabs, axis=0), axis=0, keepdims=True)


def _fox_valid_t(qi, kj, c, r):
    krow = kj * TM + r * KSUB + lax.broadcasted_iota(jnp.int32, (KSUB, QCH), 0)
    qrow = qi * TM + c * QCH + lax.broadcasted_iota(jnp.int32, (KSUB, QCH), 1)
    return (krow <= qrow) & ((krow >= PAD_ROWS) | (qrow < PAD_ROWS))


def _fox_fwd(q_aug, k_aug, v_t, rider):
    pairs = [(qi, kj) for qi in range(NT) for kj in range(qi + 1)]
    n_pairs = len(pairs)

    def body(qi_ref, kj_ref, *refs):
        (q_ref, k_ref, vt_ref), (o_ref, lse_ref), (m_s, l_s, acc_s), mine = rider.split(refs, 3, 2, 3)
        n = pl.program_id(0)
        qi = qi_ref[n]
        kj = kj_ref[n]
        rider.at_steps(mine, n == 0, n == n_pairs // 2, n == n_pairs - 1)

        @pl.when(kj == 0)
        def _():
            m_s[...] = jnp.full_like(m_s, NEG)
            l_s[...] = jnp.zeros_like(l_s)
            acc_s[...] = jnp.zeros_like(acc_s)

        def tile(masked):
            steps = [(h, c, r) for h in range(N_HEADS) for c in range(TM // QCH) for r in range(TM // KSUB)]

            def scores(h, c, r):
                blk = slice(h * BLK, (h + 1) * BLK)
                return lax.dot_general(k_ref[r * KSUB:(r + 1) * KSUB, blk], q_ref[c * QCH:(c + 1) * QCH, blk],
                                       NT_DIMS, preferred_element_type=F32)

            ahead = scores(*steps[0])
            for n, (h, c, r) in enumerate(steps):
                s_t = ahead
                if n + 1 < len(steps):
                    ahead = scores(*steps[n + 1])
                cs = slice(c * QCH, (c + 1) * QCH)
                if masked:
                    s_t = jnp.where(_fox_valid_t(qi, kj, c, r), s_t, NEG)
                m_prev = m_s[h, :, cs]
                m_new = jnp.maximum(m_prev, _over_keys(jnp.max, s_t))
                p_t = jnp.exp(s_t - m_new)
                alpha = jnp.exp(m_prev - m_new)
                l_s[h, :, cs] = alpha * l_s[h, :, cs] + _over_keys(jnp.sum, p_t)
                m_s[h, :, cs] = m_new
                vt = vt_ref[h * HALF:(h + 1) * HALF, r * KSUB:(r + 1) * KSUB]
                acc_s[h, :, cs] = acc_s[h, :, cs] * alpha + jnp.dot(vt, p_t.astype(BF16),
                                                                    preferred_element_type=F32)

        @pl.when((kj < qi) & (kj > 0))
        def _():
            tile(False)

        @pl.when((kj == qi) | (kj == 0))
        def _():
            tile(True)

        @pl.when(kj == qi)
        def _():
            for pp in range(4):
                both = jnp.concatenate([acc_s[2 * pp] * (1.0 / l_s[2 * pp]),
                                        acc_s[2 * pp + 1] * (1.0 / l_s[2 * pp + 1])], axis=0)
                o_ref[:, pp * BLK:(pp + 1) * BLK] = both.T.astype(BF16)
            for h in range(N_HEADS):
                lse_ref[h] = m_s[h] + jnp.log(l_s[h])

    grid_spec = pltpu.PrefetchScalarGridSpec(
        num_scalar_prefetch=2, grid=(n_pairs,),
        in_specs=[pl.BlockSpec((TM, 1024), lambda n, qi, kj: (qi[n], 0)),
                  pl.BlockSpec((TM, 1024), lambda n, qi, kj: (kj[n], 0)),
                  pl.BlockSpec((512, TM), lambda n, qi, kj: (0, kj[n]))] + [HBM_SPEC] * len(rider.operands),
        out_specs=[pl.BlockSpec((TM, 512), lambda n, qi, kj: (qi[n], 0)),
                   pl.BlockSpec((N_HEADS, 1, TM), lambda n, qi, kj: (0, 0, qi[n]))]
        + [HBM_SPEC] * len(rider.out_shapes),
        scratch_shapes=[pltpu.VMEM((N_HEADS, 1, TM), F32), pltpu.VMEM((N_HEADS, 1, TM), F32),
                        pltpu.VMEM((N_HEADS, HALF, TM), F32)] + rider.scratch())
    o_b, lse, *carried = pl.pallas_call(
        body, grid_spec=grid_spec,
        out_shape=[SDS((LP, 512), BF16), SDS((N_HEADS, 1, LP), F32)] + rider.out_shapes,
        compiler_params=_cparams(("arbitrary",)), name="fox_fwd",
    )(jnp.asarray([p[0] for p in pairs], jnp.int32), jnp.asarray([p[1] for p in pairs], jnp.int32),
      q_aug, k_aug, v_t, *rider.operands)
    return o_b, lse, carried


def _fox_bwd(proj, o_b, dmix, lse, ck_t, rider):
    pairs = [(kj, qi) for kj in range(NT) for qi in range(kj, NT)]
    n_pairs = len(pairs)

    def body(kj_ref, qi_ref, *refs):
        ((q_ref, k_ref, v_ref, o_ref, do_ref, lse_ref, ck_ref), (dq_ref, dk_ref, dv_ref, dck_ref, dcq_ref),
         (dk_s, dv_s, dck_s), mine) = rider.split(refs, 7, 5, 3)
        n = pl.program_id(0)
        kj = kj_ref[n]
        qi = qi_ref[n]
        rider.at_steps(mine, n == 0, n == n_pairs // 2, n == n_pairs - 1)

        @pl.when(n == 0)
        def _():
            dq_ref[...] = jnp.zeros_like(dq_ref)
            dcq_ref[...] = jnp.zeros_like(dcq_ref)

        @pl.when(qi == kj)
        def _():
            dk_s[...] = jnp.zeros_like(dk_s)
            dv_s[...] = jnp.zeros_like(dv_s)
            dck_s[...] = jnp.zeros_like(dck_s)

        def tile(masked):
            valid = _fox_valid(qi, kj) if masked else None
            half = _lane_half(TM)
            q0 = pl.multiple_of(qi * TM, TM)
            lane = lax.broadcasted_iota(jnp.int32, (TM, BLK), 1)
            row_sums = jnp.zeros((TM, BLK), F32)
            for pp in range(4):
                cols = slice(pp * BLK, (pp + 1) * BLK)
                qs = (q_ref[:, cols].astype(F32) * SCALE).astype(BF16)
                kp = k_ref[:, cols]
                vp = v_ref[:, cols]
                dop = do_ref[:, cols]
                prod = dop.astype(F32) * o_ref[:, cols].astype(F32)
                d0 = jnp.sum(jnp.where(half == 0, prod, 0.0), axis=1, keepdims=True)
                d1 = jnp.sum(prod, axis=1, keepdims=True) - d0
                dq = jnp.zeros((TM, BLK), F32)
                dks, dvs = [], []
                for e in range(2):
                    h = 2 * pp + e
                    ke = jnp.where(half == e, kp, jnp.zeros_like(kp))
                    ve = jnp.where(half == e, vp, jnp.zeros_like(vp))
                    t = lax.dot_general(qs, ke, NT_DIMS, preferred_element_type=F32) - ck_ref[h] - lse_ref[h]
                    if masked:
                        t = jnp.where(valid, t, NEG)
                    p = jnp.exp(t)
                    dp = lax.dot_general(dop, ve, NT_DIMS, preferred_element_type=F32)
                    ds = p * (dp - (d0 if e == 0 else d1))
                    dck_s[h] += jnp.sum(ds, axis=0, keepdims=True)
                    row_sums = jnp.where(lane == h, jnp.sum(ds, axis=1, keepdims=True), row_sums)
                    ds_b = ds.astype(BF16)
                    dq = dq + jnp.dot(ds_b, ke, preferred_element_type=F32)
                    dks.append(lax.dot_general(ds_b, qs, TN_DIMS, preferred_element_type=F32))
                    dvs.append(lax.dot_general(p.astype(BF16), dop, TN_DIMS, preferred_element_type=F32))
                dq_ref[pl.ds(q0, TM), cols] += dq
                dk_s[pp] += jnp.where(half == 0, dks[0], dks[1])
                dv_s[pp] += jnp.where(half == 0, dvs[0], dvs[1])
            dcq_ref[pl.ds(q0, TM), :] += row_sums

        @pl.when((qi > kj) & (kj > 0))
        def _():
            tile(False)

        @pl.when((qi == kj) | (kj == 0))
        def _():
            tile(True)

        @pl.when(qi == NT - 1)
        def _():
            for pp in range(4):
                cols = slice(pp * BLK, (pp + 1) * BLK)
                dk_ref[:, cols] = dk_s[pp].astype(BF16)
                dv_ref[:, cols] = dv_s[pp].astype(BF16)
            dck_ref[...] = dck_s[...]

    qrow = lambda blk: pl.BlockSpec((TM, 512), lambda n, kj, qi: (qi[n], blk))
    krow = lambda blk: pl.BlockSpec((TM, 512), lambda n, kj, qi: (kj[n], blk))
    grid_spec = pltpu.PrefetchScalarGridSpec(
        num_scalar_prefetch=2, grid=(n_pairs,),
        in_specs=[qrow(QB), krow(KB), krow(VB), qrow(0), qrow(1),
                  pl.BlockSpec((N_HEADS, TM, 1), lambda n, kj, qi: (0, qi[n], 0)),
                  pl.BlockSpec((N_HEADS, 1, TM), lambda n, kj, qi: (0, 0, kj[n]))] + [HBM_SPEC] * len(rider.operands),
        out_specs=[pl.BlockSpec((LP, 512), lambda n, kj, qi: (0, 0)),
                   pl.BlockSpec((TM, 512), lambda n, kj, qi: (kj[n], 0)),
                   pl.BlockSpec((TM, 512), lambda n, kj, qi: (kj[n], 0)),
                   pl.BlockSpec((N_HEADS, 1, TM), lambda n, kj, qi: (0, 0, kj[n])),
                   pl.BlockSpec((LP, BLK), lambda n, kj, qi: (0, 0))] + [HBM_SPEC] * len(rider.out_shapes),
        scratch_shapes=[pltpu.VMEM((4, TM, BLK), F32), pltpu.VMEM((4, TM, BLK), F32),
                        pltpu.VMEM((N_HEADS, 1, TM), F32)] + rider.scratch())
    dq, dk, dv, dck, dcq, *carried = pl.pallas_call(
        body, grid_spec=grid_spec,
        out_shape=[SDS((LP, 512), F32), SDS((LP, 512), BF16), SDS((LP, 512), BF16), SDS((N_HEADS, 1, LP), F32),
                   SDS((LP, BLK), F32)] + rider.out_shapes,
        compiler_params=_cparams(("arbitrary",)), name="fox_bwd",
    )(jnp.asarray([p[0] for p in pairs], jnp.int32), jnp.asarray([p[1] for p in pairs], jnp.int32),
      proj, proj, proj, o_b, dmix, lse, ck_t, *rider.operands)
    return dq, dk, dv, dck, dcq, carried


def _build_bias(tab_ref, bkt_ref, bias_cur, bias_prev):
    bc = bkt_ref[0]
    bp = bkt_ref[1]
    for h in range(N_HEADS):
        def step(b, carry, h=h):
            t = tab_ref[b, h]
            return jnp.where(bc == b, t, carry[0]), jnp.where(bp == b, t, carry[1])
        zero = jnp.zeros((BLK, BLK), F32)
        cur, prev = lax.fori_loop(0, N_BUCKETS, step, (zero, zero))
        bias_cur[h] = cur
        bias_prev[h] = prev


def _kv_variants(ref):
    x = ref[...].astype(F32)
    swapped = pltpu.roll(x, HALF, 1)
    half = _lane_half(BLK)
    out = {}
    for e in range(2):
        for g in range(2):
            src = x if e == g else swapped
            out[(e, g)] = jnp.where(half == e, src, 0.0).astype(BF16)
    return out


def _swa_masks(i):
    qrow = lax.broadcasted_iota(jnp.int32, (BLK, BLK), 0)
    kcol = lax.broadcasted_iota(jnp.int32, (BLK, BLK), 1)
    big = 4 * BLK
    valid_cur = (kcol <= qrow) & (kcol >= jnp.where(i > 0, 0, PAD_ROWS))
    valid_prev = kcol > qrow + jnp.where(i >= 2, 0, big)
    valid_meta = kcol >= jnp.where(i >= 1, PAD_ROWS, big)
    return valid_cur, valid_prev, valid_meta


def _swa_scores(i, h, qs, kvar, bias_cur, bias_prev, tab_ref, masks):
    e, g = h % 2, h // 4
    first = jnp.full((BLK, BLK), i, jnp.int32) == 1
    biases = (bias_cur[h], bias_prev[h], jnp.where(first, bias_prev[h], tab_ref[N_BUCKETS - 1, h]))
    out = []
    for x in range(3):
        s = lax.dot_general(qs, kvar[x][(e, g)], NT_DIMS, preferred_element_type=F32) + biases[x]
        out.append(jnp.where(masks[x], s, NEG))
    return out


def _swa_fwd(proj, rel_bias, sinks, bkt):
    def body(tab_ref, sink_ref, bkt_ref, q_ref, kc_ref, kp_ref, km_ref, vc_ref, vp_ref, vm_ref,
             o_ref, lse_ref, bias_cur, bias_prev):
        i = pl.program_id(0)

        @pl.when(i == 0)
        def _():
            _build_bias(tab_ref, bkt_ref, bias_cur, bias_prev)

        masks = _swa_masks(i)
        kvar = [_kv_variants(r) for r in (kc_ref, kp_ref, km_ref)]
        vvar = [_kv_variants(r) for r in (vc_ref, vp_ref, vm_ref)]
        for pp in range(4):
            cols = slice(pp * BLK, (pp + 1) * BLK)
            qs = (q_ref[:, cols].astype(F32) * SCALE).astype(BF16)
            o_pair = jnp.zeros((BLK, BLK), F32)
            for e in range(2):
                h = 2 * pp + e
                g = h // 4
                s = _swa_scores(i, h, qs, kvar, bias_cur, bias_prev, tab_ref, masks)
                sink = sink_ref[0, h]
                m = jnp.maximum(jnp.maximum(jnp.max(s[0], axis=1, keepdims=True), jnp.max(s[1], axis=1, keepdims=True)),
                                jnp.maximum(jnp.max(s[2], axis=1, keepdims=True), sink))
                p = [jnp.exp(sx - m) for sx in s]
                denom = (jnp.sum(p[0], axis=1, keepdims=True) + jnp.sum(p[1], axis=1, keepdims=True)
                         + jnp.sum(p[2], axis=1, keepdims=True) + jnp.exp(sink - m))
                pv = (jnp.dot(p[0].astype(BF16), vvar[0][(e, g)], preferred_element_type=F32)
                      + jnp.dot(p[1].astype(BF16), vvar[1][(e, g)], preferred_element_type=F32)
                      + jnp.dot(p[2].astype(BF16), vvar[2][(e, g)], preferred_element_type=F32))
                o_pair = o_pair + pv * (1.0 / denom)
                lse_ref[h] = m + jnp.log(denom)
            o_ref[:, cols] = o_pair.astype(BF16)

    smem = pl.BlockSpec(memory_space=pltpu.SMEM)
    kv = lambda col, which: pl.BlockSpec(
        (BLK, BLK), {0: lambda i: (i, col), 1: lambda i: (jnp.maximum(i - 1, 0), col), 2: lambda i: (0, col)}[which])
    return pl.pallas_call(
        body, grid=(NBLK,),
        in_specs=[smem, smem, pl.BlockSpec((2, BLK, BLK), lambda i: (0, 0, 0)),
                  pl.BlockSpec((BLK, 512), lambda i: (i, QA)),
                  kv(KA, 0), kv(KA, 1), kv(KA, 2), kv(VA, 0), kv(VA, 1), kv(VA, 2)],
        out_specs=[pl.BlockSpec((BLK, 512), lambda i: (i, 0)),
                   pl.BlockSpec((N_HEADS, BLK, 1), lambda i: (0, i, 0))],
        out_shape=[SDS((LP, 512), BF16), SDS((N_HEADS, LP, 1), F32)],
        scratch_shapes=[pltpu.VMEM((N_HEADS, BLK, BLK), F32), pltpu.VMEM((N_HEADS, BLK, BLK), F32)],
        compiler_params=_cparams(("arbitrary",)), name="swa_fwd",
    )(rel_bias, sinks, bkt, proj, proj, proj, proj, proj, proj, proj)


def _swa_bwd(proj, o_a, dmix, lse, rel_bias, sinks, bkt):
    def body(tab_ref, sink_ref, bkt_ref, q_ref, kc_ref, kp_ref, km_ref, vc_ref, vp_ref, vm_ref,
             o_ref, do_ref, lse_ref, dq_ref, dk_ref, dv_ref, dbias_ref, dsink_ref,
             bias_cur, bias_prev, acc_cur, acc_prev, acc_far, dsk):
        i = pl.program_id(0)

        @pl.when(i == 0)
        def _():
            _build_bias(tab_ref, bkt_ref, bias_cur, bias_prev)
            dk_ref[...] = jnp.zeros_like(dk_ref)
            dv_ref[...] = jnp.zeros_like(dv_ref)
            acc_cur[...] = jnp.zeros_like(acc_cur)
            acc_prev[...] = jnp.zeros_like(acc_prev)
            acc_far[...] = jnp.zeros_like(acc_far)
            dsk[...] = jnp.zeros_like(dsk)

        masks = _swa_masks(i)
        half = _lane_half(BLK)
        first = jnp.full((BLK, BLK), i, jnp.int32) == 1
        kvar = [_kv_variants(r) for r in (kc_ref, kp_ref, km_ref)]
        vvar = [_kv_variants(r) for r in (vc_ref, vp_ref, vm_ref)]
        dk_blk = [jnp.zeros((BLK, BLK), F32) for _ in range(3)]
        dv_blk = [jnp.zeros((BLK, BLK), F32) for _ in range(3)]
        for pp in range(4):
            cols = slice(pp * BLK, (pp + 1) * BLK)
            qs = (q_ref[:, cols].astype(F32) * SCALE).astype(BF16)
            dop = do_ref[:, cols]
            prod = dop.astype(F32) * o_ref[:, cols].astype(F32)
            d0 = jnp.sum(jnp.where(half == 0, prod, 0.0), axis=1, keepdims=True)
            d1 = jnp.sum(prod, axis=1, keepdims=True) - d0
            dq = jnp.zeros((BLK, BLK), F32)
            for e in range(2):
                h = 2 * pp + e
                g = h // 4
                dd = d0 if e == 0 else d1
                lse_h = lse_ref[h]
                s = _swa_scores(i, h, qs, kvar, bias_cur, bias_prev, tab_ref, masks)
                dsk[h] += -jnp.exp(sink_ref[0, h] - lse_h) * dd
                ds_all = []
                for x in range(3):
                    p = jnp.exp(s[x] - lse_h)
                    dp = lax.dot_general(dop, vvar[x][(e, g)], NT_DIMS, preferred_element_type=F32)
                    ds = p * (dp - dd)
                    ds_all.append(ds)
                    ds_b = ds.astype(BF16)
                    dq = dq + jnp.dot(ds_b, kvar[x][(e, g)], preferred_element_type=F32)
                    dkv = lax.dot_general(ds_b, qs, TN_DIMS, preferred_element_type=F32)
                    dvv = lax.dot_general(p.astype(BF16), dop, TN_DIMS, preferred_element_type=F32)
                    if e != g:
                        dkv = pltpu.roll(dkv, HALF, 1)
                        dvv = pltpu.roll(dvv, HALF, 1)
                    dk_blk[x] = dk_blk[x] + jnp.where(half == g, dkv, 0.0)
                    dv_blk[x] = dv_blk[x] + jnp.where(half == g, dvv, 0.0)
                acc_cur[h] += ds_all[0]
                acc_prev[h] += ds_all[1] + jnp.where(first, ds_all[2], 0.0)
                acc_far[h] += jnp.where(first, 0.0, ds_all[2])
            dq_ref[:, cols] = (dq * SCALE).astype(BF16)

        cur0 = pl.multiple_of(i * BLK, BLK)
        prev0 = pl.multiple_of(jnp.maximum(i - 1, 0) * BLK, BLK)
        dk_ref[pl.ds(cur0, BLK), :] += dk_blk[0]
        dv_ref[pl.ds(cur0, BLK), :] += dv_blk[0]
        dk_ref[pl.ds(prev0, BLK), :] += dk_blk[1]
        dv_ref[pl.ds(prev0, BLK), :] += dv_blk[1]
        dk_ref[0:BLK, :] += dk_blk[2]
        dv_ref[0:BLK, :] += dv_blk[2]

        @pl.when(i == NBLK - 1)
        def _():
            bc = bkt_ref[0]
            bp = bkt_ref[1]
            lane = lax.broadcasted_iota(jnp.int32, (1, BLK), 1)

            def per_bucket(b, carry):
                row = jnp.zeros((1, BLK), F32)
                for h in range(N_HEADS):
                    val = (jnp.sum(jnp.where(bc == b, acc_cur[h], 0.0), keepdims=True)
                           + jnp.sum(jnp.where(bp == b, acc_prev[h], 0.0), keepdims=True))
                    row = jnp.where(lane == h, val, row)
                dbias_ref[pl.ds(b, 1), :] = row
                return carry

            lax.fori_loop(0, N_BUCKETS, per_bucket, 0)
            far = jnp.zeros((1, BLK), F32)
            dsr = jnp.zeros((1, BLK), F32)
            for h in range(N_HEADS):
                far = jnp.where(lane == h, jnp.sum(acc_far[h], keepdims=True), far)
                dsr = jnp.where(lane == h, jnp.sum(dsk[h], keepdims=True), dsr)
            dbias_ref[N_BUCKETS - 1:N_BUCKETS, :] += far
            dsink_ref[...] = dsr

    smem = pl.BlockSpec(memory_space=pltpu.SMEM)
    kv = lambda col, which: pl.BlockSpec(
        (BLK, BLK), {0: lambda i: (i, col), 1: lambda i: (jnp.maximum(i - 1, 0), col), 2: lambda i: (0, col)}[which])
    blk512 = lambda col: pl.BlockSpec((BLK, 512), lambda i: (i, col))
    full = lambda r, c: pl.BlockSpec((r, c), lambda i: (0, 0))
    acc = pltpu.VMEM((N_HEADS, BLK, BLK), F32)
    return pl.pallas_call(
        body, grid=(NBLK,),
        in_specs=[smem, smem, pl.BlockSpec((2, BLK, BLK), lambda i: (0, 0, 0)), blk512(QA),
                  kv(KA, 0), kv(KA, 1), kv(KA, 2), kv(VA, 0), kv(VA, 1), kv(VA, 2),
                  blk512(0), blk512(0), pl.BlockSpec((N_HEADS, BLK, 1), lambda i: (0, i, 0))],
        out_specs=[blk512(0), full(LP, BLK), full(LP, BLK), full(N_BUCKETS, BLK), full(1, BLK)],
        out_shape=[SDS((LP, 512), BF16), SDS((LP, BLK), F32), SDS((LP, BLK), F32),
                   SDS((N_BUCKETS, BLK), F32), SDS((1, BLK), F32)],
        scratch_shapes=[acc, acc, acc, acc, acc, pltpu.VMEM((N_HEADS, BLK, 1), F32)],
        compiler_params=_cparams(("arbitrary",)), name="swa_bwd",
    )(rel_bias, sinks, bkt, proj, proj, proj, proj, proj, proj, proj, o_a, dmix, lse)


def _local_step(x, tgt, meta, rel_bias, g_pre_mix, g_post_mix, g_pre_ffn, g_post_ffn, b_forget, sinks,
                w_in_b, w_out_b, ffn_rider, ffn_weights, early_grads):
    bkt = jnp.asarray(_bucket_tables())
    h0 = jnp.concatenate([jnp.zeros((PAD_ROWS, D_MODEL), F32), meta, x], axis=0)
    tgt_p = jnp.concatenate([jnp.zeros((ROW0, D_MODEL), F32), tgt], axis=0)
    b_p = jnp.pad(b_forget, ((0, 0), (0, BLK - N_HEADS)))

    hn1, proj, f = _pre_mix(h0, g_pre_mix, w_in_b)
    o_a, lse_a = _swa_fwd(proj, rel_bias, sinks, bkt)
    cum = _forget_cumsum(f, b_p)
    ck_t = cum[:, :N_HEADS].T.reshape(N_HEADS, 1, LP)
    q_aug, k_aug, v_t = _fox_prep(proj, cum)
    o_b, lse_row, carried = _fox_fwd(q_aug, k_aug, v_t, ffn_rider)
    lse_b = lse_row.reshape(N_HEADS, LP, 1)
    w_gu_b, w_dn_b = ffn_weights(carried)
    a, h1, hn2 = _attn_out(o_a, o_b, w_out_b, h0, g_post_mix, g_pre_ffn)
    g, u, act = _ffn_up(hn2, w_gu_b)
    dff, dy, loss_blk, dg_post_ffn = _ffn_down_loss(act, w_dn_b, h1, tgt_p, g_post_ffn)

    dw_dn = _mm_tn([act], dff, FF_T, "dw_down")
    dg, du = _ffn_down_bwd(dff, w_dn_b, g, u)
    dw_gu = _dw_gate_up(hn2, dg, du)
    dh1, da, dg_pre_ffn, dg_post_mix = _ffn_up_bwd(dg, du, w_gu_b, h1, a, dy, g_pre_ffn, g_post_mix)
    dw_out = _mm_tn([o_a, o_b], da, D_MODEL, "dw_out")
    dmix = _attn_out_bwd(da, w_out_b)
    dq_b, dk_b, dv_b, dck, dcq, landed = _fox_bwd(proj, o_b, dmix, lse_b, ck_t, early_grads(dw_gu, dw_dn, dw_out))
    dq_a, dk_a, dv_a, dbias, dsink = _swa_bwd(proj, o_a, dmix, lse_a, rel_bias, sinks, bkt)
    dcum = dcq - jnp.pad(dck.reshape(N_HEADS, LP).T, ((0, 0), (0, BLK - N_HEADS)))
    df, db = _forget_cumsum_bwd(dcum, f, b_p)
    dproj, dh0, dg_pre_mix = _pre_mix_bwd(dq_a, dq_b, dk_b, dv_b, dk_a, dv_a, df, w_in_b, h0, dh1, g_pre_mix)
    dw_in = _mm_tn([hn1], dproj, D_MODEL, "dw_in")

    return dict(loss=loss_blk[0, 0], grad_x=dh0[ROW0:], meta=dh0[PAD_ROWS:ROW0],
                rel_bias=dbias[:, :N_HEADS], ln_pre_mix=dg_pre_mix, ln_post_mix=dg_post_mix,
                ln_pre_ffn=dg_pre_ffn, ln_post_ffn=dg_post_ffn, b_forget=db[:, :N_HEADS],
                sinks=dsink[:, :N_HEADS], w_in=dw_in, w_out=dw_out, w_gate_up=dw_gu, w_down=dw_dn,
                landed=landed)


N_SMALL = 24
LOSS_ROW = 6


def _place():
    x, y, c = lax.axis_index("x"), lax.axis_index("y"), lax.axis_index("c")
    return x, y, c, [(1 - x, y), (x, 1 - y), (1 - x, 1 - y)]


def _run_alone(rider, name):
    a, b = len(rider.operands), len(rider.out_shapes)

    def body(*refs):
        mine = (refs[:a], refs[a:a + b], refs[a + b:])
        rider.first(*mine)
        rider.middle(*mine)
        rider.last(*mine)

    return pl.pallas_call(body, in_specs=[HBM_SPEC] * a, out_specs=[HBM_SPEC] * b, out_shape=rider.out_shapes,
                          scratch_shapes=rider.scratch(), name=name)(*rider.operands)


def _gather_rider(shards, own_too):
    n = len(shards)

    def own_copies(ins, outs, sems):
        x, y, _, _ = _place()
        return [pltpu.make_async_copy(ins[a], outs[a].at[2 * x + y], sems[2].at[a]) for a in range(n)] if own_too else []

    def copies(ins, outs, sems):
        send_sems, recv_sems = sems[:2]
        x, y, c, others = _place()
        chip = 2 * x + y
        sibling = (x, y, 1 - c)

        def rc(a, k, src, dst, to):
            return pltpu.make_async_remote_copy(src_ref=src, dst_ref=dst, send_sem=send_sems.at[6 * a + k],
                                                recv_sem=recv_sems.at[6 * a + k], device_id=to, device_id_type=MESH)

        pairs = [(a, j, ox, oy) for a in range(n) for j, (ox, oy) in enumerate(others)]
        return dict(
            sent=lambda: [rc(a, j, ins[a].at[c], outs[a].at[chip, c], (ox, oy, c)) for a, j, ox, oy in pairs],
            landed=lambda: [rc(a, j, outs[a].at[2 * ox + oy, c], outs[a].at[2 * ox + oy, c], sibling)
                            for a, j, ox, oy in pairs],
            passed=lambda: [rc(a, 3 + j, outs[a].at[2 * ox + oy, c], outs[a].at[2 * ox + oy, c], sibling)
                            for a, j, ox, oy in pairs],
            arriving=lambda: [rc(a, 3 + j, outs[a].at[2 * ox + oy, 1 - c], outs[a].at[2 * ox + oy, 1 - c], sibling)
                              for a, j, ox, oy in pairs])

    def first(*mine):
        for cp in copies(*mine)["sent"]() + own_copies(*mine):
            cp.start()

    def middle(*mine):
        kinds = copies(*mine)
        for got, cp in zip(kinds["landed"](), kinds["passed"]()):
            got.wait_recv()
            cp.start()

    def last(*mine):
        kinds = copies(*mine)
        for cp in kinds["arriving"]():
            cp.wait_recv()
        for cp in kinds["sent"]() + kinds["passed"]():
            cp.wait_send()
        for cp in own_copies(*mine):
            cp.wait()

    return _Rider(shards, [SDS((4,) + s.shape, s.dtype) for s in shards], [6 * n, 6 * n] + [n] * own_too,
                  first, middle, last)


def _swap_halves(grads, name):
    n = len(grads)

    def body(*refs):
        ins, outs = refs[:n], refs[n:2 * n]
        send_sems, recv_sems = refs[2 * n:]
        x, y, c, _ = _place()
        copies = [pltpu.make_async_remote_copy(
            src_ref=ins[a].at[s, 1 - c], dst_ref=outs[a].at[s], send_sem=send_sems.at[4 * a + s],
            recv_sem=recv_sems.at[4 * a + s], device_id=(x, y, 1 - c), device_id_type=MESH)
            for a in range(n) for s in range(4)]
        for cp in copies:
            cp.start()
        for cp in copies:
            cp.wait()

    return pl.pallas_call(
        body, in_specs=[HBM_SPEC] * n, out_specs=[HBM_SPEC] * n,
        out_shape=[SDS((4,) + g.shape[2:], g.dtype) for g in grads],
        scratch_shapes=[pltpu.SemaphoreType.DMA((4 * n,)), pltpu.SemaphoreType.DMA((4 * n,))],
        name=name)(*grads)


def _pair_sum(g, got, c_arr, name):
    rh, cc = got.shape[1:]

    def body(c_ref, g_ref, p_ref, o_ref):
        o_ref[0] = (g_ref[0, 0] + p_ref[0]).astype(BF16)

    grid_spec = pltpu.PrefetchScalarGridSpec(
        num_scalar_prefetch=1, grid=(4,),
        in_specs=[pl.BlockSpec((1, 1, rh, cc), lambda s, c_ref: (s, c_ref[0], 0, 0)),
                  pl.BlockSpec((1, rh, cc), lambda s, c_ref: (s, 0, 0))],
        out_specs=pl.BlockSpec((1, rh, cc), lambda s, c_ref: (s, 0, 0)))
    return pl.pallas_call(body, grid_spec=grid_spec, out_shape=SDS((4, rh, cc), BF16),
                          compiler_params=_cparams(("parallel",)), name=name)(c_arr, g, got)


def _exchange_rider(parts, small=None):
    n = len(parts)

    def copies(ins, outs, sems):
        x, y, c, others = _place()
        out = [pltpu.make_async_remote_copy(
            src_ref=ins[a].at[2 * ox + oy], dst_ref=outs[a].at[j], send_sem=sems[0].at[3 * a + j],
            recv_sem=sems[1].at[3 * a + j], device_id=(ox, oy, c), device_id_type=MESH)
            for a in range(n) for j, (ox, oy) in enumerate(others)]
        own = []
        if small is not None:
            me = 4 * x + 2 * y + c
            peers = [(x, y, 1 - c)] + [(ox, oy, c) for ox, oy in others] + [(ox, oy, 1 - c) for ox, oy in others]
            out += [pltpu.make_async_remote_copy(
                src_ref=ins[n], dst_ref=outs[n].at[me], send_sem=sems[2].at[k], recv_sem=sems[3].at[k],
                device_id=peer, device_id_type=MESH) for k, peer in enumerate(peers)]
            own = [pltpu.make_async_copy(ins[n], outs[n].at[me], sems[4].at[0])]
        return out, own

    def first(*mine):
        out, own = copies(*mine)
        for cp in own + out:
            cp.start()

    def middle(*mine):
        pass

    def last(*mine):
        out, own = copies(*mine)
        for cp in out + own:
            cp.wait()

    shapes = [SDS((3,) + p.shape[1:], p.dtype) for p in parts]
    if small is None:
        return _Rider(parts, shapes, [3 * n, 3 * n], first, middle, last)
    return _Rider(parts + [small], shapes + [SDS((8,) + small.shape, small.dtype)], [3 * n, 3 * n, 7, 7, 1],
                  first, middle, last)


def _chip_sum(parts, landed, chip_arr, name):
    rh, cc = landed.shape[1:]
    tr = rh // 2

    def body(chip_ref, own_ref, p_ref, o_ref):
        o_ref[...] = ((own_ref[0].astype(F32) + p_ref[0].astype(F32)) + p_ref[1].astype(F32)) + p_ref[2].astype(F32)

    grid_spec = pltpu.PrefetchScalarGridSpec(
        num_scalar_prefetch=1, grid=(2,),
        in_specs=[pl.BlockSpec((1, tr, cc), lambda i, chip_ref: (chip_ref[0], i, 0)),
                  pl.BlockSpec((3, tr, cc), lambda i, chip_ref: (0, i, 0))],
        out_specs=pl.BlockSpec((tr, cc), lambda i, chip_ref: (i, 0)))
    return pl.pallas_call(body, grid_spec=grid_spec, out_shape=SDS((rh, cc), F32),
                          compiler_params=_cparams(("parallel",)), name=name)(chip_arr, parts, landed)


def _device_sum(p):
    def body(p_ref, o_ref):
        acc = p_ref[0]
        for k in range(1, 8):
            acc = acc + p_ref[k]
        o_ref[...] = acc

    return pl.pallas_call(body, out_shape=SDS(p.shape[1:], F32), name="small_sum")(p)


def _join_halves(halves):
    n = len(halves)

    def body(*refs):
        ins, outs = refs[:n], refs[n:2 * n]
        send_sems, recv_sems = refs[2 * n:]
        x, y, c, _ = _place()
        copies = [pltpu.make_async_remote_copy(
            src_ref=ins[a], dst_ref=outs[a], send_sem=send_sems.at[a], recv_sem=recv_sems.at[a],
            device_id=(x, y, 1 - c), device_id_type=MESH) for a in range(n)]
        for cp in copies:
            cp.start()
        for cp in copies:
            cp.wait()

    return pl.pallas_call(
        body, in_specs=[HBM_SPEC] * n, out_specs=[HBM_SPEC] * n,
        out_shape=[SDS(h.shape, h.dtype) for h in halves],
        scratch_shapes=[pltpu.SemaphoreType.DMA((n,)), pltpu.SemaphoreType.DMA((n,))],
        name="join_halves")(*halves)


def _adamw(w, g, m, v, name):
    rows, cols = w.shape
    tr = rows if rows <= 352 else (256 if rows % 256 == 0 else 352)

    def body(w_ref, g_ref, m_ref, v_ref, d_ref, nm_ref, nv_ref):
        gg = g_ref[...]
        nm = ADAM_B1 * m_ref[...] + (1.0 - ADAM_B1) * gg
        nv = ADAM_B2 * v_ref[...] + (1.0 - ADAM_B2) * (gg * gg)
        nm_ref[...] = nm
        nv_ref[...] = nv
        m_hat = nm / (1.0 - ADAM_B1 ** ADAM_STEP)
        v_hat = nv / (1.0 - ADAM_B2 ** ADAM_STEP)
        d_ref[...] = -ADAM_LR * (m_hat / (jnp.sqrt(v_hat) + ADAM_EPS) + ADAM_WD * w_ref[...])

    blk = pl.BlockSpec((tr, cols), lambda i: (i, 0))
    return pl.pallas_call(
        body, grid=(rows // tr,), in_specs=[blk] * 4, out_specs=[blk] * 3,
        out_shape=[SDS((rows, cols), F32)] * 3,
        compiler_params=_cparams(("parallel",)), name=name)(w, g, m, v)


def _adamw_halves(w, mine, theirs, m, v, c_arr, name):
    rows, cols = w.shape
    rh = rows // 2
    tr = rh if rh <= 352 else 256
    nh = rh // tr

    def body(c_ref, w_ref, mine_ref, theirs_ref, m_ref, v_ref, g_ref, d_ref, nm_ref, nv_ref):
        own = jnp.full((tr, cols), pl.program_id(0), jnp.int32) == c_ref[0]
        gg = jnp.where(own, mine_ref[...], theirs_ref[...])
        g_ref[...] = gg
        nm = ADAM_B1 * m_ref[...] + (1.0 - ADAM_B1) * gg
        nv = ADAM_B2 * v_ref[...] + (1.0 - ADAM_B2) * (gg * gg)
        nm_ref[...] = nm
        nv_ref[...] = nv
        m_hat = nm / (1.0 - ADAM_B1 ** ADAM_STEP)
        v_hat = nv / (1.0 - ADAM_B2 ** ADAM_STEP)
        d_ref[...] = -ADAM_LR * (m_hat / (jnp.sqrt(v_hat) + ADAM_EPS) + ADAM_WD * w_ref[...])

    whole = pl.BlockSpec((tr, cols), lambda hh, i, c_ref: (hh * nh + i, 0))
    part = pl.BlockSpec((tr, cols), lambda hh, i, c_ref: (i, 0))
    grid_spec = pltpu.PrefetchScalarGridSpec(
        num_scalar_prefetch=1, grid=(2, nh), in_specs=[whole, part, part, whole, whole], out_specs=[whole] * 4)
    return pl.pallas_call(body, grid_spec=grid_spec, out_shape=[SDS((rows, cols), F32)] * 4,
                          compiler_params=_cparams(("parallel", "parallel")), name=name)(c_arr, w, mine, theirs, m, v)


def _pack_small(pre_mix, post_mix, pre_ffn, post_ffn, rel_bias, b_forget, sinks):
    def at(row, v):
        return jnp.pad(v, ((row, 7 - row), (0, D_MODEL - v.shape[1])))
    return (at(0, pre_mix) + at(1, post_mix) + at(2, pre_ffn) + at(3, post_ffn)
            + at(4, rel_bias.reshape(1, N_BUCKETS * N_HEADS)) + at(5, jnp.concatenate([b_forget, sinks], axis=1)))


def _unpack_small(p):
    return dict(ln_pre_mix=p[0:1], ln_post_mix=p[1:2], ln_pre_ffn=p[2:3], ln_post_ffn=p[3:4],
                rel_bias=p[4, :N_BUCKETS * N_HEADS].reshape(N_BUCKETS, N_HEADS),
                b_forget=p[5:6, 0:N_HEADS], sinks=p[5:6, N_HEADS:2 * N_HEADS])


WEIGHTS = ("meta_tokens", "rel_bias", "ln_pre_mix", "ln_post_mix", "ln_pre_ffn", "ln_post_ffn",
           "w_in", "b_forget", "sinks", "w_out", "w_gate_up", "w_down")


def kernel(x, meta_tokens, rel_bias, ln_pre_mix, ln_post_mix, ln_pre_ffn, ln_post_ffn, w_in, b_forget, sinks, w_out, w_gate_up, w_down, loss_target, m_meta_tokens, m_rel_bias, m_ln_pre_mix, m_ln_post_mix, m_ln_pre_ffn, m_ln_post_ffn, m_w_in, m_b_forget, m_sinks, m_w_out, m_w_gate_up, m_w_down, v_meta_tokens, v_rel_bias, v_ln_pre_mix, v_ln_post_mix, v_ln_pre_ffn, v_ln_post_ffn, v_w_in, v_b_forget, v_sinks, v_w_out, v_w_gate_up, v_w_down):
    xi, yi, ci = lax.axis_index("x"), lax.axis_index("y"), lax.axis_index("c")
    chip = 2 * xi + yi
    c_arr = jnp.reshape(ci, (1,)).astype(jnp.int32)

    def halves(w, dtype):
        return w.astype(dtype).reshape(2, w.shape[0] // 2, w.shape[1])

    def with_own(gathered, shards):
        return [lax.dynamic_update_slice(got, own[None], (chip, 0, 0, 0)) for got, own in zip(gathered, shards)]

    shards = [halves(w_in[0], BF16), halves(w_out[0], BF16), halves(meta_tokens, F32)]
    gw_in, gw_out, g_meta = with_own(_run_alone(_gather_rider(shards, False), "gather_mixer_weights"), shards)
    ffn_shards = [halves(w_gate_up[0], BF16), halves(w_down[0], BF16)]

    def ffn_weights(carried):
        gw_gu, gw_dn = carried
        w_gu_b = gw_gu.reshape(4, D_MODEL, FF_T).transpose(1, 0, 2).reshape(D_MODEL, 2 * D_FF)
        return w_gu_b, gw_dn.reshape(D_FF, D_MODEL)

    early = {}

    def early_grads(dw_gu, dw_dn, dw_out):
        grads = [dw_out.reshape(4, 2, 128, D_MODEL), dw_gu.reshape(4, 2, 512, FF_T), dw_dn.reshape(4, 2, 352, D_MODEL)]
        got = _swap_halves(grads, "swap_halves_early")
        early["parts"] = [_pair_sum(g, p, c_arr, "pair_sum_%d" % a) for a, (g, p) in enumerate(zip(grads, got))]
        return _exchange_rider(early["parts"])
    w_in_all = gw_in.reshape(4, D_MODEL, D_PROJ // 4).transpose(1, 0, 2).reshape(D_MODEL, D_PROJ)
    w_in_b = jnp.concatenate(
        [w_in_all[:, 0:512], w_in_all[:, 768:1280], w_in_all[:, 1280:1792], w_in_all[:, 1792:2304],
         w_in_all[:, 512:640], w_in_all[:, 640:768], w_in_all[:, 2304:2312],
         jnp.zeros((D_MODEL, D_PROJ_P - D_PROJ), BF16)], axis=1)
    meta_all = g_meta.reshape(4, N_META, D_MODEL // 4).transpose(1, 0, 2).reshape(N_META, D_MODEL)

    loc = _local_step(x[0], loss_target[0], meta_all, rel_bias, ln_pre_mix, ln_post_mix, ln_pre_ffn, ln_post_ffn,
                      b_forget, sinks, w_in_b, gw_out.reshape(D_MODEL, D_MODEL),
                      _gather_rider(ffn_shards, True), ffn_weights, early_grads)

    n = loc["w_in"]
    dw_in = jnp.concatenate([n[:, 0:512], n[:, 2048:2176], n[:, 2176:2304], n[:, 512:1024], n[:, 1024:1536],
                             n[:, 1536:2048], n[:, 2304:2312]], axis=1)
    dw_in = dw_in.reshape(D_MODEL, 4, D_PROJ // 4).transpose(1, 0, 2).reshape(4, 2, 512, D_PROJ // 4)
    small = jnp.concatenate(
        [_pack_small(loc["ln_pre_mix"], loc["ln_post_mix"], loc["ln_pre_ffn"], loc["ln_post_ffn"],
                     loc["rel_bias"], loc["b_forget"], loc["sinks"])
         + jnp.pad(loc["loss"].reshape(1, 1), ((LOSS_ROW, 7 - LOSS_ROW), (0, D_MODEL - 1))), loc["meta"]], axis=0)

    (got_in,) = _swap_halves([dw_in], "swap_halves_late")
    part_in = _pair_sum(dw_in, got_in, c_arr, "pair_sum_late")
    landed_in, small_all = _run_alone(_exchange_rider([part_in], small), "exchange_late")
    chip_arr = jnp.reshape(chip, (1,)).astype(jnp.int32)
    parts = [part_in] + early["parts"]
    landed = [landed_in] + loc["landed"]
    mine = [_chip_sum(p, l, chip_arr, "chip_sum_%d" % a) for a, (p, l) in enumerate(zip(parts, landed))]
    small_sum = _device_sum(small_all)
    theirs = _join_halves(mine)
    g_meta_tokens = lax.dynamic_slice(small_sum[8:N_SMALL], (0, chip * (D_MODEL // 4)), (N_META, D_MODEL // 4))
    g_small = small_sum[0:8]

    grad = _unpack_small(g_small)
    grad.update(meta_tokens=g_meta_tokens)
    delta, new_m, new_v = {}, {}, {}
    big = dict(w_in=(w_in, m_w_in, v_w_in), w_out=(w_out, m_w_out, v_w_out),
               w_gate_up=(w_gate_up, m_w_gate_up, v_w_gate_up), w_down=(w_down, m_w_down, v_w_down))
    for (name, (w, m, v)), g_mine, g_theirs in zip(big.items(), mine, theirs):
        g, d, nm, nv = _adamw_halves(w[0], g_mine, g_theirs, m[0], v[0], c_arr, "adamw_" + name)
        grad[name], delta[name], new_m[name], new_v[name] = g[None], d[None], nm[None], nv[None]
    delta["meta_tokens"], new_m["meta_tokens"], new_v["meta_tokens"] = _adamw(
        meta_tokens, g_meta_tokens, m_meta_tokens, v_meta_tokens, "adamw_meta")
    d, nm, nv = _adamw(
        _pack_small(ln_pre_mix, ln_post_mix, ln_pre_ffn, ln_post_ffn, rel_bias, b_forget, sinks), g_small,
        _pack_small(m_ln_pre_mix, m_ln_post_mix, m_ln_pre_ffn, m_ln_post_ffn, m_rel_bias, m_b_forget, m_sinks),
        _pack_small(v_ln_pre_mix, v_ln_post_mix, v_ln_pre_ffn, v_ln_post_ffn, v_rel_bias, v_b_forget, v_sinks),
        "adamw_small")
    delta.update(_unpack_small(d))
    new_m.update(_unpack_small(nm))
    new_v.update(_unpack_small(nv))

    loss = small_sum[LOSS_ROW, 0]
    return (loss,loc["grad_x"][None], *[grad[k] for k in WEIGHTS], *[delta[k] for k in WEIGHTS],
            *[new_m[k] for k in WEIGHTS], *[new_v[k] for k in WEIGHTS])
```

```python
import math

import numpy as np
import jax
import jax.numpy as jnp
from jax import lax
from jax.experimental import pallas as pl
from jax.experimental.pallas import tpu as pltpu

F32 = jnp.float32
BF16 = jnp.bfloat16
MESH = pl.DeviceIdType.MESH
SDS = jax.ShapeDtypeStruct

D_MODEL = 1024
SEQ = 4096
N_META = 16
N_HEADS = 8
HALF = 64
D_FF = 2816
N_BUCKETS = 32
EPS = 1e-6
NEG = -1e30
SCALE = 0.125
PAD_ROWS = 112
ROW0 = PAD_ROWS + N_META
LP = ROW0 + SEQ
BLK = 128
NBLK = LP // BLK
TM = 384
NT = LP // TM
TM_PURE = LP // 2
TM_MID = LP // 4
TM_EPI = LP // 6
TN = 256
D_PROJ = 2312
D_PROJ_P = 2432
D_QKV = 2304
FF_T = 1408
VMEM_LIMIT = 56 * 1024 * 1024

ADAM_LR = 0.001
ADAM_B1 = 0.9
ADAM_B2 = 0.999
ADAM_EPS = 1e-08
ADAM_WD = 0.01
ADAM_STEP = 10

QA, QB, KB, VB = 0, 1, 2, 3
KA, VA = 16, 17

NT_DIMS = (((1,), (1,)), ((), ()))
TN_DIMS = (((0,), (0,)), ((), ()))


def _cparams(sem):
    return pltpu.CompilerParams(dimension_semantics=sem, vmem_limit_bytes=VMEM_LIMIT)


def _t5_bucket_np(d):
    n = np.maximum(d, 0).astype(np.int32)
    nf = np.maximum(n, 1).astype(np.float32)
    large = 16 + (np.log(nf / np.float32(16)) / np.float32(math.log(8.0)) * np.float32(16)).astype(np.int32)
    large = np.minimum(large, N_BUCKETS - 1)
    return np.where(n < 16, n, large).astype(np.int32)


def _bucket_tables():
    qi = np.arange(BLK)[:, None]
    ki = np.arange(BLK)[None, :]
    return np.stack([_t5_bucket_np(qi - ki), _t5_bucket_np(qi - ki + BLK)])


def _rms(x):
    return lax.rsqrt(jnp.mean(x * x, axis=-1, keepdims=True) + EPS)


def _rms_bwd(n, r, gdy):
    return r * (gdy - n * jnp.mean(n * gdy, axis=-1, keepdims=True))


def _pre_mix(h0, gain, w_in_b):
    half = D_QKV // 2

    def body(h_ref, g_ref, w_ref, hn_ref, proj_ref, f_ref):
        x = h_ref[...]
        hn = (x * _rms(x) * g_ref[...]).astype(BF16)
        hn_ref[...] = hn
        proj_ref[:, :half] = jnp.dot(hn, w_ref[:, :half], preferred_element_type=F32).astype(BF16)
        p = jnp.dot(hn, w_ref[:, half:], preferred_element_type=F32)
        proj_ref[:, half:] = p[:, :half].astype(BF16)
        f_ref[...] = p[:, half:]

    return pl.pallas_call(
        body, grid=(LP // TM_MID,),
        in_specs=[pl.BlockSpec((TM_MID, D_MODEL), lambda i: (i, 0)),
                  pl.BlockSpec((1, D_MODEL), lambda i: (0, 0)),
                  pl.BlockSpec((D_MODEL, D_PROJ_P), lambda i: (0, 0))],
        out_specs=[pl.BlockSpec((TM_MID, D_MODEL), lambda i: (i, 0)),
                   pl.BlockSpec((TM_MID, D_QKV), lambda i: (i, 0)),
                   pl.BlockSpec((TM_MID, BLK), lambda i: (i, 0))],
        out_shape=[SDS((LP, D_MODEL), BF16), SDS((LP, D_QKV), BF16), SDS((LP, BLK), F32)],
        compiler_params=_cparams(("parallel",)), name="pre_mix")(h0, gain, w_in_b)


def _attn_out(o_a, o_b, w_out_b, h0, g_post, g_pre_ffn):
    def body(oa_ref, ob_ref, w_ref, h0_ref, gp_ref, gf_ref, a_ref, h1_ref, hn2_ref):
        a = (jnp.dot(oa_ref[...], w_ref[0:512, :], preferred_element_type=F32)
             + jnp.dot(ob_ref[...], w_ref[512:1024, :], preferred_element_type=F32))
        a_ref[...] = a
        h1 = h0_ref[...] + a * _rms(a) * gp_ref[...]
        h1_ref[...] = h1
        hn2_ref[...] = (h1 * _rms(h1) * gf_ref[...]).astype(BF16)

    row = lambda w: pl.BlockSpec((TM_EPI, w), lambda i: (i, 0))
    vec = pl.BlockSpec((1, D_MODEL), lambda i: (0, 0))
    return pl.pallas_call(
        body, grid=(LP // TM_EPI,),
        in_specs=[row(512), row(512), pl.BlockSpec((D_MODEL, D_MODEL), lambda i: (0, 0)), row(D_MODEL), vec, vec],
        out_specs=[row(D_MODEL), row(D_MODEL), row(D_MODEL)],
        out_shape=[SDS((LP, D_MODEL), F32), SDS((LP, D_MODEL), F32), SDS((LP, D_MODEL), BF16)],
        compiler_params=_cparams(("parallel",)), name="attn_out")(o_a, o_b, w_out_b, h0, g_post, g_pre_ffn)


def _ffn_up(hn2, w_gu_b):
    def body(x_ref, wg_ref, wu_ref, g_ref, u_ref, act_ref):
        x = x_ref[...]
        g = jnp.dot(x, wg_ref[...], preferred_element_type=F32)
        u = jnp.dot(x, wu_ref[...], preferred_element_type=F32)
        g_ref[...] = g.astype(BF16)
        u_ref[...] = u.astype(BF16)
        act_ref[...] = (g * (1.0 / (1.0 + jnp.exp(-g))) * u).astype(BF16)

    out = pl.BlockSpec((TM_PURE, TN), lambda i, j: (i, j))
    return pl.pallas_call(
        body, grid=(LP // TM_PURE, D_FF // TN),
        in_specs=[pl.BlockSpec((TM_PURE, D_MODEL), lambda i, j: (i, 0)),
                  pl.BlockSpec((D_MODEL, TN), lambda i, j: (0, j)),
                  pl.BlockSpec((D_MODEL, TN), lambda i, j: (0, j + D_FF // TN))],
        out_specs=[out, out, out],
        out_shape=[SDS((LP, D_FF), BF16)] * 3,
        compiler_params=_cparams(("parallel", "parallel")), name="ffn_up")(hn2, w_gu_b, w_gu_b)


def _ffn_down_loss(act, w_dn_b, h1, tgt_p, g_post_ffn):
    def body(act_ref, w_ref, h1_ref, t_ref, g_ref, dff_ref, dy_ref, loss_ref, dg_ref):
        i = pl.program_id(0)

        @pl.when(i == 0)
        def _():
            loss_ref[...] = jnp.zeros_like(loss_ref)
            dg_ref[...] = jnp.zeros_like(dg_ref)

        ff = jnp.dot(act_ref[...], w_ref[...], preferred_element_type=F32)
        r = _rms(ff)
        n = ff * r
        g = g_ref[...]
        y = h1_ref[...] + n * g
        rows = i * TM + lax.broadcasted_iota(jnp.int32, (TM, D_MODEL), 0)
        diff = jnp.where(rows >= ROW0, y - t_ref[...], 0.0)
        loss_ref[...] += 0.5 * jnp.sum(diff * diff) / D_MODEL
        dy = diff / D_MODEL
        dy_ref[...] = dy
        dg_ref[...] += jnp.sum(dy * n, axis=0, keepdims=True)
        dff_ref[...] = _rms_bwd(n, r, g * dy).astype(BF16)

    row = pl.BlockSpec((TM, D_MODEL), lambda i: (i, 0))
    return pl.pallas_call(
        body, grid=(NT,),
        in_specs=[pl.BlockSpec((TM, D_FF), lambda i: (i, 0)), pl.BlockSpec((D_FF, D_MODEL), lambda i: (0, 0)),
                  row, row, pl.BlockSpec((1, D_MODEL), lambda i: (0, 0))],
        out_specs=[row, row, pl.BlockSpec((8, BLK), lambda i: (0, 0)), pl.BlockSpec((1, D_MODEL), lambda i: (0, 0))],
        out_shape=[SDS((LP, D_MODEL), BF16), SDS((LP, D_MODEL), F32), SDS((8, BLK), F32), SDS((1, D_MODEL), F32)],
        compiler_params=_cparams(("arbitrary",)), name="ffn_down_loss")(act, w_dn_b, h1, tgt_p, g_post_ffn)


def _ffn_down_bwd(dff, w_dn_b, g, u):
    def body(d_ref, w_ref, g_ref, u_ref, dg_ref, du_ref):
        dact = lax.dot_general(d_ref[...], w_ref[...], NT_DIMS, preferred_element_type=F32)
        gg = g_ref[...].astype(F32)
        sig = 1.0 / (1.0 + jnp.exp(-gg))
        dg_ref[...] = (dact * u_ref[...].astype(F32) * sig * (1.0 + gg * (1.0 - sig))).astype(BF16)
        du_ref[...] = (dact * gg * sig).astype(BF16)

    blk = pl.BlockSpec((TM_PURE, TN), lambda i, j: (i, j))
    return pl.pallas_call(
        body, grid=(LP // TM_PURE, D_FF // TN),
        in_specs=[pl.BlockSpec((TM_PURE, D_MODEL), lambda i, j: (i, 0)),
                  pl.BlockSpec((TN, D_MODEL), lambda i, j: (j, 0)), blk, blk],
        out_specs=[blk, blk],
        out_shape=[SDS((LP, D_FF), BF16)] * 2,
        compiler_params=_cparams(("parallel", "parallel")), name="ffn_down_bwd")(dff, w_dn_b, g, u)


def _ffn_up_bwd(dg, du, w_gu_b, h1, a, dy, g_pre_ffn, g_post_mix):
    def body(dg_ref, du_ref, w_ref, h1_ref, a_ref, dy_ref, gf_ref, gp_ref,
             dh1_ref, da_ref, dgf_ref, dgp_ref, acc):
        i = pl.program_id(0)
        s = pl.program_id(1)

        @pl.when((i == 0) & (s == 0))
        def _():
            dgf_ref[...] = jnp.zeros_like(dgf_ref)
            dgp_ref[...] = jnp.zeros_like(dgp_ref)

        @pl.when(s == 0)
        def _():
            acc[...] = jnp.zeros_like(acc)

        @pl.when(s < 2)
        def _():
            acc[...] += lax.dot_general(dg_ref[...], w_ref[...], NT_DIMS, preferred_element_type=F32)

        @pl.when(s >= 2)
        def _():
            acc[...] += lax.dot_general(du_ref[...], w_ref[...], NT_DIMS, preferred_element_type=F32)

        @pl.when(s == 3)
        def _():
            dhn2 = acc[...]
            h1 = h1_ref[...]
            r2 = _rms(h1)
            n2 = h1 * r2
            dgf_ref[...] += jnp.sum(dhn2 * n2, axis=0, keepdims=True)
            dh1 = dy_ref[...] + _rms_bwd(n2, r2, gf_ref[...] * dhn2)
            dh1_ref[...] = dh1
            av = a_ref[...]
            ra = _rms(av)
            na = av * ra
            dgp_ref[...] += jnp.sum(dh1 * na, axis=0, keepdims=True)
            da_ref[...] = _rms_bwd(na, ra, gp_ref[...] * dh1).astype(BF16)

    row = pl.BlockSpec((TM_EPI, D_MODEL), lambda i, s: (i, 0))
    vec = pl.BlockSpec((1, D_MODEL), lambda i, s: (0, 0))
    return pl.pallas_call(
        body, grid=(LP // TM_EPI, 4),
        in_specs=[pl.BlockSpec((TM_EPI, FF_T), lambda i, s: (i, jnp.minimum(s, 1))),
                  pl.BlockSpec((TM_EPI, FF_T), lambda i, s: (i, jnp.maximum(s - 2, 0))),
                  pl.BlockSpec((D_MODEL, FF_T), lambda i, s: (0, s)),
                  row, row, row, vec, vec],
        out_specs=[row, row, vec, vec],
        out_shape=[SDS((LP, D_MODEL), F32), SDS((LP, D_MODEL), BF16), SDS((1, D_MODEL), F32), SDS((1, D_MODEL), F32)],
        scratch_shapes=[pltpu.VMEM((TM_EPI, D_MODEL), F32)],
        compiler_params=_cparams(("arbitrary", "arbitrary")), name="ffn_up_bwd",
    )(dg, du, w_gu_b, h1, a, dy, g_pre_ffn, g_post_mix)


def _attn_out_bwd(da, w_out_b):
    def body(d_ref, w_ref, o_ref):
        o_ref[...] = lax.dot_general(d_ref[...], w_ref[...], NT_DIMS, preferred_element_type=F32).astype(BF16)

    row = pl.BlockSpec((TM_PURE, D_MODEL), lambda i: (i, 0))
    return pl.pallas_call(
        body, grid=(LP // TM_PURE,),
        in_specs=[row, pl.BlockSpec((D_MODEL, D_MODEL), lambda i: (0, 0))],
        out_specs=row, out_shape=SDS((LP, D_MODEL), BF16),
        compiler_params=_cparams(("parallel",)), name="attn_out_bwd")(da, w_out_b)


def _pre_mix_bwd(dq_a, dq_b, dk_b, dv_b, dk_a, dv_a, df, w_in_b, h0, dh1, g_pre_mix):
    def body(qa_ref, qb_ref, kb_ref, vb_ref, ka_ref, va_ref, f_ref, w_ref, h0_ref, dh1_ref, g_ref,
             dproj_ref, dh0_ref, dg_ref):
        i = pl.program_id(0)

        @pl.when(i == 0)
        def _():
            dg_ref[...] = jnp.zeros_like(dg_ref)

        dproj = jnp.concatenate(
            [qa_ref[...], (qb_ref[...] * SCALE).astype(BF16), kb_ref[...], vb_ref[...],
             ka_ref[...].astype(BF16), va_ref[...].astype(BF16), f_ref[...].astype(BF16)], axis=1)
        dproj_ref[...] = dproj
        dhn = lax.dot_general(dproj, w_ref[...], NT_DIMS, preferred_element_type=F32)
        x = h0_ref[...]
        r = _rms(x)
        n = x * r
        dg_ref[...] += jnp.sum(dhn * n, axis=0, keepdims=True)
        dh0_ref[...] = dh1_ref[...] + _rms_bwd(n, r, g_ref[...] * dhn)

    row = lambda w: pl.BlockSpec((TM_EPI, w), lambda i: (i, 0))
    vec = pl.BlockSpec((1, D_MODEL), lambda i: (0, 0))
    return pl.pallas_call(
        body, grid=(LP // TM_EPI,),
        in_specs=[row(512), row(512), row(512), row(512), row(BLK), row(BLK), row(BLK),
                  pl.BlockSpec((D_MODEL, D_PROJ_P), lambda i: (0, 0)), row(D_MODEL), row(D_MODEL), vec],
        out_specs=[row(D_PROJ_P), row(D_MODEL), vec],
        out_shape=[SDS((LP, D_PROJ_P), BF16), SDS((LP, D_MODEL), F32), SDS((1, D_MODEL), F32)],
        compiler_params=_cparams(("arbitrary",)), name="pre_mix_bwd",
    )(dq_a, dq_b, dk_b, dv_b, dk_a, dv_a, df, w_in_b, h0, dh1, g_pre_mix)


def _mm_tn(parts, b, tm, name):
    widths = [p.shape[1] for p in parts]
    m_total = sum(widths)
    n = b.shape[1]
    whole = len(parts) > 1
    assert (tm == m_total) if whole else (m_total % tm == 0)

    def body(*refs):
        a_refs, b_ref, o_ref = refs[:-2], refs[-2], refs[-1]

        @pl.when(pl.program_id(1) == 0)
        def _():
            o_ref[...] = jnp.zeros_like(o_ref)
        a = a_refs[0][...] if not whole else jnp.concatenate([r[...] for r in a_refs], axis=1)
        o_ref[...] += lax.dot_general(a, b_ref[...], TN_DIMS, preferred_element_type=F32)

    a_specs = ([pl.BlockSpec((TM_MID, w), lambda mi, k: (k, 0)) for w in widths] if whole
               else [pl.BlockSpec((TM_MID, tm), lambda mi, k: (k, mi))])
    return pl.pallas_call(
        body, grid=(m_total // tm, LP // TM_MID),
        in_specs=a_specs + [pl.BlockSpec((TM_MID, n), lambda mi, k: (k, 0))],
        out_specs=pl.BlockSpec((tm, n), lambda mi, k: (mi, 0)),
        out_shape=SDS((m_total, n), F32),
        compiler_params=_cparams(("parallel", "arbitrary")), name=name)(*parts, b)


def _dw_gate_up(hn2, dg, du):
    def body(a_ref, dg_ref, du_ref, o_ref):
        s = pl.program_id(0)

        @pl.when(pl.program_id(1) == 0)
        def _():
            o_ref[...] = jnp.zeros_like(o_ref)

        @pl.when(s < 2)
        def _():
            o_ref[0] += lax.dot_general(a_ref[...], dg_ref[...], TN_DIMS, preferred_element_type=F32)

        @pl.when(s >= 2)
        def _():
            o_ref[0] += lax.dot_general(a_ref[...], du_ref[...], TN_DIMS, preferred_element_type=F32)

    return pl.pallas_call(
        body, grid=(4, LP // TM_MID),
        in_specs=[pl.BlockSpec((TM_MID, D_MODEL), lambda s, k: (k, 0)),
                  pl.BlockSpec((TM_MID, FF_T), lambda s, k: (k, jnp.minimum(s, 1))),
                  pl.BlockSpec((TM_MID, FF_T), lambda s, k: (k, jnp.maximum(s - 2, 0)))],
        out_specs=pl.BlockSpec((1, D_MODEL, FF_T), lambda s, k: (s, 0, 0)),
        out_shape=SDS((4, D_MODEL, FF_T), F32),
        compiler_params=_cparams(("parallel", "arbitrary")), name="dw_gate_up")(hn2, dg, du)


def _split3(x):
    hi = x.astype(BF16)
    r1 = x - hi.astype(F32)
    mid = r1.astype(BF16)
    lo = (r1 - mid.astype(F32)).astype(BF16)
    return hi, mid, lo


def _tri_matmul(tri, x):
    hi, mid, lo = _split3(x)
    dot = lambda t: jnp.dot(tri, t, preferred_element_type=F32)
    return dot(hi) + dot(mid) + dot(lo)


def _forget_cumsum(f, b_forget_p):
    def body(f_ref, b_ref, cum_ref, carry):
        i = pl.program_id(0)

        @pl.when(i == 0)
        def _():
            carry[...] = jnp.zeros_like(carry)

        z = f_ref[...] + b_ref[...]
        ls = jnp.minimum(z, 0.0) - jnp.log(1.0 + jnp.exp(-jnp.abs(z)))
        rows = i * BLK + lax.broadcasted_iota(jnp.int32, (BLK, BLK), 0)
        ls = jnp.where(rows >= PAD_ROWS, ls, 0.0)
        r = lax.broadcasted_iota(jnp.int32, (BLK, BLK), 0)
        c = lax.broadcasted_iota(jnp.int32, (BLK, BLK), 1)
        tri = (c <= r).astype(BF16)
        cum = _tri_matmul(tri, ls) + carry[...]
        cum_ref[...] = cum
        carry[...] = cum[BLK - 1:BLK, :]

    return pl.pallas_call(
        body, grid=(NBLK,),
        in_specs=[pl.BlockSpec((BLK, BLK), lambda i: (i, 0)), pl.BlockSpec((1, BLK), lambda i: (0, 0))],
        out_specs=pl.BlockSpec((BLK, BLK), lambda i: (i, 0)),
        out_shape=SDS((LP, BLK), F32),
        scratch_shapes=[pltpu.VMEM((1, BLK), F32)],
        compiler_params=_cparams(("arbitrary",)), name="forget_cumsum")(f, b_forget_p)


def _forget_cumsum_bwd(dcum, f, b_forget_p):
    def body(d_ref, f_ref, b_ref, df_ref, db_ref, carry):
        i = pl.program_id(0)

        @pl.when(i == 0)
        def _():
            carry[...] = jnp.zeros_like(carry)
            db_ref[...] = jnp.zeros_like(db_ref)

        blk = NBLK - 1 - i
        r = lax.broadcasted_iota(jnp.int32, (BLK, BLK), 0)
        c = lax.broadcasted_iota(jnp.int32, (BLK, BLK), 1)
        tri = (c >= r).astype(BF16)
        d = d_ref[...]
        dls = _tri_matmul(tri, d) + carry[...]
        carry[...] = dls[0:1, :]
        z = f_ref[...] + b_ref[...]
        rows = blk * BLK + r
        df = jnp.where(rows >= PAD_ROWS, dls / (1.0 + jnp.exp(z)), 0.0)
        df_ref[...] = df
        db_ref[...] += jnp.sum(df, axis=0, keepdims=True)

    rev = pl.BlockSpec((BLK, BLK), lambda i: (NBLK - 1 - i, 0))
    vec = pl.BlockSpec((1, BLK), lambda i: (0, 0))
    return pl.pallas_call(
        body, grid=(NBLK,),
        in_specs=[rev, rev, vec],
        out_specs=[rev, vec],
        out_shape=[SDS((LP, BLK), F32), SDS((1, BLK), F32)],
        scratch_shapes=[pltpu.VMEM((1, BLK), F32)],
        compiler_params=_cparams(("arbitrary",)), name="forget_cumsum_bwd")(dcum, f, b_forget_p)


def _lane_half(rows):
    return lax.broadcasted_iota(jnp.int32, (rows, BLK), 1) // HALF


def _fox_valid(qi, kj):
    qrow = qi * TM + lax.broadcasted_iota(jnp.int32, (TM, TM), 0)
    krow = kj * TM + lax.broadcasted_iota(jnp.int32, (TM, TM), 1)
    return (krow <= qrow) & ((krow >= PAD_ROWS) | (qrow < PAD_ROWS))


class _Rider:
    def __init__(self, operands, out_shapes, sem_counts, first, middle, last):
        self.operands, self.out_shapes, self.sem_counts = list(operands), list(out_shapes), list(sem_counts)
        self.first, self.middle, self.last = first, middle, last

    def scratch(self):
        return [pltpu.SemaphoreType.DMA((k,)) for k in self.sem_counts]

    def split(self, refs, n_in, n_out, n_scratch):
        a, b = len(self.operands), len(self.out_shapes)
        ins, mine_in = refs[:n_in], refs[n_in:n_in + a]
        outs, mine_out = refs[n_in + a:n_in + a + n_out], refs[n_in + a + n_out:n_in + a + n_out + b]
        rest = refs[n_in + a + n_out + b:]
        return ins, outs, rest[:n_scratch], (mine_in, mine_out, rest[n_scratch:])

    def at_steps(self, mine, is_first, is_middle, is_last):
        for cond, fn in ((is_first, self.first), (is_middle, self.middle), (is_last, self.last)):
            pl.when(cond)(lambda fn=fn: fn(*mine))


HBM_SPEC = pl.BlockSpec(memory_space=pltpu.HBM)


N_AUG = 4
QCH = 128
KSUB = 384


def _fox_prep(proj, cum):
    def body(q_ref, k_ref, v_ref, c_ref, qa_ref, ka_ref, vt_ref):
        half = _lane_half(BLK)
        lane = lax.broadcasted_iota(jnp.int32, (BLK, BLK), 1)
        for pp in range(4):
            cols = slice(pp * BLK, (pp + 1) * BLK)
            qs = q_ref[:, cols].astype(F32) * SCALE
            kp = k_ref[:, cols].astype(F32)
            vp = v_ref[:, cols]
            vt_ref[cols, :] = vp.astype(F32).T.astype(BF16)
            for e in range(2):
                h = 2 * pp + e
                a = (1 - e) * HALF
                blk = slice(h * BLK, (h + 1) * BLK)
                hi, mid, lo = _split3(-c_ref[:, h:h + 1])
                q_aug = jnp.where(half == e, qs, jnp.where((lane >= a) & (lane < a + 3), 1.0, 0.0))
                k_aug = jnp.where(half == e, kp, jnp.where(
                    lane == a, hi.astype(F32), jnp.where(lane == a + 1, mid.astype(F32), jnp.where(
                        lane == a + 2, lo.astype(F32), jnp.where(lane == a + 3, 1.0, 0.0)))))
                qa_ref[:, blk] = q_aug.astype(BF16)
                ka_ref[:, blk] = k_aug.astype(BF16)

    row = lambda blk: pl.BlockSpec((BLK, 512), lambda i: (i, blk))
    wide = pl.BlockSpec((BLK, 1024), lambda i: (i, 0))
    return pl.pallas_call(
        body, grid=(NBLK,),
        in_specs=[row(QB), row(KB), row(VB), pl.BlockSpec((BLK, BLK), lambda i: (i, 0))],
        out_specs=[wide, wide, pl.BlockSpec((512, BLK), lambda i: (0, i))],
        out_shape=[SDS((LP, 1024), BF16)] * 2 + [SDS((512, LP), BF16)],
        compiler_params=_cparams(("parallel",)), name="fox_prep")(proj, proj, proj, cum)


def _over_keys(reduce, x):
    slabs = x.reshape(x.shape[0] // HALF, HALF, x.shape[1])
    return reduce(reduce(slabs, axis=0), axis=0, keepdims=True)


def _fox_valid_t(qi, kj, c, r):
    krow = kj * TM + r * KSUB + lax.broadcasted_iota(jnp.int32, (KSUB, QCH), 0)
    qrow = qi * TM + c * QCH + lax.broadcasted_iota(jnp.int32, (KSUB, QCH), 1)
    return (krow <= qrow) & ((krow >= PAD_ROWS) | (qrow < PAD_ROWS))


def _fox_fwd(q_aug, k_aug, v_t, rider):
    pairs = [(qi, kj) for qi in range(NT) for kj in range(qi + 1)]
    n_pairs = len(pairs)

    def body(qi_ref, kj_ref, *refs):
        (q_ref, k_ref, vt_ref), (o_ref, lse_ref), (m_s, l_s, acc_s), mine = rider.split(refs, 3, 2, 3)
        n = pl.program_id(0)
        qi = qi_ref[n]
        kj = kj_ref[n]
        rider.at_steps(mine, n == 0, n == n_pairs // 2, n == n_pairs - 1)

        @pl.when(kj == 0)
        def _():
            m_s[...] = jnp.full_like(m_s, NEG)
            l_s[...] = jnp.zeros_like(l_s)
            acc_s[...] = jnp.zeros_like(acc_s)

        def tile(masked):
            steps = [(h, c, r) for h in range(N_HEADS) for c in range(TM // QCH) for r in range(TM // KSUB)]

            def scores(h, c, r):
                blk = slice(h * BLK, (h + 1) * BLK)
                return lax.dot_general(k_ref[r * KSUB:(r + 1) * KSUB, blk], q_ref[c * QCH:(c + 1) * QCH, blk],
                                       NT_DIMS, preferred_element_type=F32)

            ahead = scores(*steps[0])
            for n, (h, c, r) in enumerate(steps):
                s_t = ahead
                if n + 1 < len(steps):
                    ahead = scores(*steps[n + 1])
                cs = slice(c * QCH, (c + 1) * QCH)
                if masked:
                    s_t = jnp.where(_fox_valid_t(qi, kj, c, r), s_t, NEG)
                m_prev = m_s[h, :, cs]
                m_new = jnp.maximum(m_prev, _over_keys(jnp.max, s_t))
                p_t = jnp.exp(s_t - m_new)
                alpha = jnp.exp(m_prev - m_new)
                l_s[h, :, cs] = alpha * l_s[h, :, cs] + _over_keys(jnp.sum, p_t)
                m_s[h, :, cs] = m_new
                vt = vt_ref[h * HALF:(h + 1) * HALF, r * KSUB:(r + 1) * KSUB]
                acc_s[h, :, cs] = acc_s[h, :, cs] * alpha + jnp.dot(vt, p_t.astype(BF16),
                                                                    preferred_element_type=F32)

        @pl.when((kj < qi) & (kj > 0))
        def _():
            tile(False)

        @pl.when((kj == qi) | (kj == 0))
        def _():
            tile(True)

        @pl.when(kj == qi)
        def _():
            for pp in range(4):
                both = jnp.concatenate([acc_s[2 * pp] * (1.0 / l_s[2 * pp]),
                                        acc_s[2 * pp + 1] * (1.0 / l_s[2 * pp + 1])], axis=0)
                o_ref[:, pp * BLK:(pp + 1) * BLK] = both.T.astype(BF16)
            for h in range(N_HEADS):
                lse_ref[h] = m_s[h] + jnp.log(l_s[h])

    grid_spec = pltpu.PrefetchScalarGridSpec(
        num_scalar_prefetch=2, grid=(n_pairs,),
        in_specs=[pl.BlockSpec((TM, 1024), lambda n, qi, kj: (qi[n], 0)),
                  pl.BlockSpec((TM, 1024), lambda n, qi, kj: (kj[n], 0)),
                  pl.BlockSpec((512, TM), lambda n, qi, kj: (0, kj[n]))] + [HBM_SPEC] * len(rider.operands),
        out_specs=[pl.BlockSpec((TM, 512), lambda n, qi, kj: (qi[n], 0)),
                   pl.BlockSpec((N_HEADS, 1, TM), lambda n, qi, kj: (0, 0, qi[n]))]
        + [HBM_SPEC] * len(rider.out_shapes),
        scratch_shapes=[pltpu.VMEM((N_HEADS, 1, TM), F32), pltpu.VMEM((N_HEADS, 1, TM), F32),
                        pltpu.VMEM((N_HEADS, HALF, TM), F32)] + rider.scratch())
    o_b, lse, *carried = pl.pallas_call(
        body, grid_spec=grid_spec,
        out_shape=[SDS((LP, 512), BF16), SDS((N_HEADS, 1, LP), F32)] + rider.out_shapes,
        compiler_params=_cparams(("arbitrary",)), name="fox_fwd",
    )(jnp.asarray([p[0] for p in pairs], jnp.int32), jnp.asarray([p[1] for p in pairs], jnp.int32),
      q_aug, k_aug, v_t, *rider.operands)
    return o_b, lse, carried


def _fox_bwd(proj, o_b, dmix, lse, ck_t, rider):
    pairs = [(kj, qi) for kj in range(NT) for qi in range(kj, NT)]
    n_pairs = len(pairs)

    def body(kj_ref, qi_ref, *refs):
        ((q_ref, k_ref, v_ref, o_ref, do_ref, lse_ref, ck_ref), (dq_ref, dk_ref, dv_ref, dck_ref, dcq_ref),
         (dk_s, dv_s, dck_s), mine) = rider.split(refs, 7, 5, 3)
        n = pl.program_id(0)
        kj = kj_ref[n]
        qi = qi_ref[n]
        rider.at_steps(mine, n == 0, n == n_pairs // 2, n == n_pairs - 1)

        @pl.when(n == 0)
        def _():
            dq_ref[...] = jnp.zeros_like(dq_ref)
            dcq_ref[...] = jnp.zeros_like(dcq_ref)

        @pl.when(qi == kj)
        def _():
            dk_s[...] = jnp.zeros_like(dk_s)
            dv_s[...] = jnp.zeros_like(dv_s)
            dck_s[...] = jnp.zeros_like(dck_s)

        def tile(masked):
            valid = _fox_valid(qi, kj) if masked else None
            half = _lane_half(TM)
            q0 = pl.multiple_of(qi * TM, TM)
            lane = lax.broadcasted_iota(jnp.int32, (TM, BLK), 1)
            row_sums = jnp.zeros((TM, BLK), F32)
            for pp in range(4):
                cols = slice(pp * BLK, (pp + 1) * BLK)
                qs = (q_ref[:, cols].astype(F32) * SCALE).astype(BF16)
                kp = k_ref[:, cols]
                vp = v_ref[:, cols]
                dop = do_ref[:, cols]
                prod = dop.astype(F32) * o_ref[:, cols].astype(F32)
                d0 = jnp.sum(jnp.where(half == 0, prod, 0.0), axis=1, keepdims=True)
                d1 = jnp.sum(prod, axis=1, keepdims=True) - d0
                dq = jnp.zeros((TM, BLK), F32)
                dks, dvs = [], []
                for e in range(2):
                    h = 2 * pp + e
                    ke = jnp.where(half == e, kp, jnp.zeros_like(kp))
                    ve = jnp.where(half == e, vp, jnp.zeros_like(vp))
                    t = lax.dot_general(qs, ke, NT_DIMS, preferred_element_type=F32) - ck_ref[h] - lse_ref[h]
                    if masked:
                        t = jnp.where(valid, t, NEG)
                    p = jnp.exp(t)
                    dp = lax.dot_general(dop, ve, NT_DIMS, preferred_element_type=F32)
                    ds = p * (dp - (d0 if e == 0 else d1))
                    dck_s[h] += jnp.sum(ds, axis=0, keepdims=True)
                    row_sums = jnp.where(lane == h, jnp.sum(ds, axis=1, keepdims=True), row_sums)
                    ds_b = ds.astype(BF16)
                    dq = dq + jnp.dot(ds_b, ke, preferred_element_type=F32)
                    dks.append(lax.dot_general(ds_b, qs, TN_DIMS, preferred_element_type=F32))
                    dvs.append(lax.dot_general(p.astype(BF16), dop, TN_DIMS, preferred_element_type=F32))
                dq_ref[pl.ds(q0, TM), cols] += dq
                dk_s[pp] += jnp.where(half == 0, dks[0], dks[1])
                dv_s[pp] += jnp.where(half == 0, dvs[0], dvs[1])
            dcq_ref[pl.ds(q0, TM), :] += row_sums

        @pl.when((qi > kj) & (kj > 0))
        def _():
            tile(False)

        @pl.when((qi == kj) | (kj == 0))
        def _():
            tile(True)

        @pl.when(qi == NT - 1)
        def _():
            for pp in range(4):
                cols = slice(pp * BLK, (pp + 1) * BLK)
                dk_ref[:, cols] = dk_s[pp].astype(BF16)
                dv_ref[:, cols] = dv_s[pp].astype(BF16)
            dck_ref[...] = dck_s[...]

    qrow = lambda blk: pl.BlockSpec((TM, 512), lambda n, kj, qi: (qi[n], blk))
    krow = lambda blk: pl.BlockSpec((TM, 512), lambda n, kj, qi: (kj[n], blk))
    grid_spec = pltpu.PrefetchScalarGridSpec(
        num_scalar_prefetch=2, grid=(n_pairs,),
        in_specs=[qrow(QB), krow(KB), krow(VB), qrow(0), qrow(1),
                  pl.BlockSpec((N_HEADS, TM, 1), lambda n, kj, qi: (0, qi[n], 0)),
                  pl.BlockSpec((N_HEADS, 1, TM), lambda n, kj, qi: (0, 0, kj[n]))] + [HBM_SPEC] * len(rider.operands),
        out_specs=[pl.BlockSpec((LP, 512), lambda n, kj, qi: (0, 0)),
                   pl.BlockSpec((TM, 512), lambda n, kj, qi: (kj[n], 0)),
                   pl.BlockSpec((TM, 512), lambda n, kj, qi: (kj[n], 0)),
                   pl.BlockSpec((N_HEADS, 1, TM), lambda n, kj, qi: (0, 0, kj[n])),
                   pl.BlockSpec((LP, BLK), lambda n, kj, qi: (0, 0))] + [HBM_SPEC] * len(rider.out_shapes),
        scratch_shapes=[pltpu.VMEM((4, TM, BLK), F32), pltpu.VMEM((4, TM, BLK), F32),
                        pltpu.VMEM((N_HEADS, 1, TM), F32)] + rider.scratch())
    dq, dk, dv, dck, dcq, *carried = pl.pallas_call(
        body, grid_spec=grid_spec,
        out_shape=[SDS((LP, 512), F32), SDS((LP, 512), BF16), SDS((LP, 512), BF16), SDS((N_HEADS, 1, LP), F32),
                   SDS((LP, BLK), F32)] + rider.out_shapes,
        compiler_params=_cparams(("arbitrary",)), name="fox_bwd",
    )(jnp.asarray([p[0] for p in pairs], jnp.int32), jnp.asarray([p[1] for p in pairs], jnp.int32),
      proj, proj, proj, o_b, dmix, lse, ck_t, *rider.operands)
    return dq, dk, dv, dck, dcq, carried


N_SEG = 3
N_KEY = N_SEG * BLK
GROUP = 4
QW = GROUP * BLK


def _bucket_tables_t():
    return np.ascontiguousarray(_bucket_tables().transpose(0, 2, 1))


def _stack_heads(ref, g, scale):
    half = _lane_half(BLK)
    out = []
    for pair in range(2):
        x = ref[:, (2 * g + pair) * BLK:(2 * g + pair + 1) * BLK].astype(F32) * scale
        swapped = pltpu.roll(x, HALF, 1)
        for e in range(2):
            out.append(jnp.where(half == g, x if e == g else swapped, 0.0).astype(BF16))
    return jnp.concatenate(out, axis=0)


def _unstack_heads(x_t, g, ref, scale):
    for pair in range(2):
        both = jnp.concatenate([x_t[:, (2 * pair) * BLK:(2 * pair + 1) * BLK],
                                x_t[:, (2 * pair + 1) * BLK:(2 * pair + 2) * BLK]], axis=0)
        ref[:, (2 * g + pair) * BLK:(2 * g + pair + 1) * BLK] = (both.T * scale).astype(ref.dtype)


def _swa_tables(tab_ref, sink_ref, bkt_ref, tbl, sink_row):
    kk = lax.broadcasted_iota(jnp.int32, (BLK, BLK), 0)
    qq = lax.broadcasted_iota(jnp.int32, (BLK, BLK), 1)
    neg = jnp.full((BLK, BLK), NEG, F32)
    lane = lax.broadcasted_iota(jnp.int32, (1, QW), 1) // BLK
    for g in range(2):
        row = jnp.zeros((1, QW), F32)
        for hh in range(GROUP):
            h = GROUP * g + hh
            cols = slice(hh * BLK, (hh + 1) * BLK)
            row = jnp.where(lane == hh, sink_ref[0, h], row)

            def step(b, carry, h=h):
                t = tab_ref[b, h]
                return jnp.where(bkt_ref[0] == b, t, carry[0]), jnp.where(bkt_ref[1] == b, t, carry[1])
            zero = jnp.zeros((BLK, BLK), F32)
            cur, prev = lax.fori_loop(0, N_BUCKETS, step, (zero, zero))
            far = jnp.full((BLK, BLK), tab_ref[N_BUCKETS - 1, h], F32)
            causal = jnp.where(kk <= qq, cur, neg)
            segments = [
                (neg, neg, jnp.where(kk >= PAD_ROWS, causal, neg)),
                (jnp.where(kk >= PAD_ROWS, prev, neg), neg, causal),
                (jnp.where(kk >= PAD_ROWS, far, neg), jnp.where(kk > qq, prev, neg), causal)]
            for case in range(3):
                for seg in range(N_SEG):
                    tbl[case, g, seg * BLK:(seg + 1) * BLK, cols] = segments[case][seg]
        sink_row[g] = row


def _swa_prep(proj):
    def body(k_ref, v_ref, kt_ref, vt_ref):
        kt_ref[...] = k_ref[...].astype(F32).T.astype(BF16)
        vt_ref[...] = v_ref[...].astype(F32).T.astype(BF16)

    col = pl.BlockSpec((BLK, BLK), lambda i: (0, i))
    return pl.pallas_call(
        body, grid=(NBLK,),
        in_specs=[pl.BlockSpec((BLK, BLK), lambda i: (i, KA)), pl.BlockSpec((BLK, BLK), lambda i: (i, VA))],
        out_specs=[col, col], out_shape=[SDS((BLK, LP), BF16)] * 2,
        compiler_params=_cparams(("parallel",)), name="swa_prep")(proj, proj)


def _seg_specs(rows_major, col):
    idx = [lambda i: 0, lambda i: jnp.maximum(i - 1, 0), lambda i: i]
    if rows_major:
        return [pl.BlockSpec((BLK, BLK), lambda i, f=f: (f(i), col)) for f in idx]
    return [pl.BlockSpec((BLK, BLK), lambda i, f=f: (0, f(i))) for f in idx]


def _swa_fwd(proj, vt_a, rel_bias, sinks, bkt_t):
    def body(tab_ref, sink_ref, bkt_ref, q_ref, km_ref, kp_ref, kc_ref, vm_ref, vp_ref, vc_ref,
             o_ref, lse_ref, tbl, sink_row):
        i = pl.program_id(0)

        @pl.when(i == 0)
        def _():
            _swa_tables(tab_ref, sink_ref, bkt_ref, tbl, sink_row)

        case = jnp.minimum(i, 2)
        k_cat = jnp.concatenate([km_ref[...], kp_ref[...], kc_ref[...]], axis=0)
        vt_cat = jnp.concatenate([vm_ref[...], vp_ref[...], vc_ref[...]], axis=1)
        for g in range(2):
            s_t = lax.dot_general(k_cat, _stack_heads(q_ref, g, SCALE), NT_DIMS,
                                  preferred_element_type=F32) + tbl[case, g]
            sink = sink_row[g]
            m = jnp.maximum(_over_keys(jnp.max, s_t), sink)
            p_t = jnp.exp(s_t - m)
            l = _over_keys(jnp.sum, p_t) + jnp.exp(sink - m)
            o_t = jnp.dot(vt_cat[g * HALF:(g + 1) * HALF, :], p_t.astype(BF16), preferred_element_type=F32)
            _unstack_heads(o_t * (1.0 / l), g, o_ref, 1.0)
            lse = m + jnp.log(l)
            for hh in range(GROUP):
                lse_ref[GROUP * g + hh] = lse[:, hh * BLK:(hh + 1) * BLK]

    smem = pl.BlockSpec(memory_space=pltpu.SMEM)
    return pl.pallas_call(
        body, grid=(NBLK,),
        in_specs=[smem, smem, pl.BlockSpec((2, BLK, BLK), lambda i: (0, 0, 0)),
                  pl.BlockSpec((BLK, 512), lambda i: (i, QA))] + _seg_specs(True, KA) + _seg_specs(False, 0),
        out_specs=[pl.BlockSpec((BLK, 512), lambda i: (i, 0)),
                   pl.BlockSpec((N_HEADS, 1, BLK), lambda i: (0, 0, i))],
        out_shape=[SDS((LP, 512), BF16), SDS((N_HEADS, 1, LP), F32)],
        scratch_shapes=[pltpu.VMEM((3, 2, N_KEY, QW), F32), pltpu.VMEM((2, 1, QW), F32)],
        compiler_params=_cparams(("arbitrary",)), name="swa_fwd",
    )(rel_bias, sinks, bkt_t, proj, proj, proj, proj, vt_a, vt_a, vt_a)


def _swa_bwd(proj, kt_a, o_a, dmix, lse, rel_bias, sinks, bkt_t):
    def body(tab_ref, sink_ref, bkt_ref, q_ref, km_ref, kp_ref, kc_ref, vm_ref, vp_ref, vc_ref,
             tm_ref, tp_ref, tc_ref, o_ref, do_ref, lse_ref,
             dq_ref, dk_ref, dv_ref, dbias_ref, dsink_ref, tbl, sink_row, acc, dsk):
        i = pl.program_id(0)

        @pl.when(i == 0)
        def _():
            _swa_tables(tab_ref, sink_ref, bkt_ref, tbl, sink_row)
            dk_ref[...] = jnp.zeros_like(dk_ref)
            dv_ref[...] = jnp.zeros_like(dv_ref)
            acc[...] = jnp.zeros_like(acc)
            dsk[...] = jnp.zeros_like(dsk)

        case = jnp.minimum(i, 2)
        first = jnp.full((BLK, QW), i, jnp.int32) == 1
        k_cat = jnp.concatenate([km_ref[...], kp_ref[...], kc_ref[...]], axis=0)
        v_cat = jnp.concatenate([vm_ref[...], vp_ref[...], vc_ref[...]], axis=0)
        kt_cat = jnp.concatenate([tm_ref[...], tp_ref[...], tc_ref[...]], axis=1)
        dk_cat = jnp.zeros((N_KEY, BLK), F32)
        dv_cat = jnp.zeros((N_KEY, BLK), F32)
        for g in range(2):
            d_parts = []
            for pair in range(2):
                cols = slice((2 * g + pair) * BLK, (2 * g + pair + 1) * BLK)
                prod_t = (do_ref[:, cols].astype(F32) * o_ref[:, cols].astype(F32)).T
                d_parts += [jnp.sum(prod_t[:HALF], axis=0, keepdims=True),
                            jnp.sum(prod_t[HALF:], axis=0, keepdims=True)]
            d_row = jnp.concatenate(d_parts, axis=1)
            lse_row = jnp.concatenate([lse_ref[GROUP * g + hh] for hh in range(GROUP)], axis=1)
            q_st = _stack_heads(q_ref, g, SCALE)
            do_st = _stack_heads(do_ref, g, 1.0)
            s_t = lax.dot_general(k_cat, q_st, NT_DIMS, preferred_element_type=F32) + tbl[case, g]
            p_t = jnp.exp(s_t - lse_row)
            dp_t = lax.dot_general(v_cat, do_st, NT_DIMS, preferred_element_type=F32)
            ds_t = p_t * (dp_t - d_row)
            dsk[g] += -jnp.exp(sink_row[g] - lse_row) * d_row
            acc[g, 0:BLK] += jnp.where(first, 0.0, ds_t[0:BLK])
            acc[g, BLK:2 * BLK] += jnp.where(first, ds_t[0:BLK], ds_t[BLK:2 * BLK])
            acc[g, 2 * BLK:N_KEY] += ds_t[2 * BLK:N_KEY]
            ds_b = ds_t.astype(BF16)
            dk_cat = dk_cat + jnp.dot(ds_b, q_st, preferred_element_type=F32)
            dv_cat = dv_cat + jnp.dot(p_t.astype(BF16), do_st, preferred_element_type=F32)
            dq_t = jnp.dot(kt_cat[g * HALF:(g + 1) * HALF, :], ds_b, preferred_element_type=F32)
            _unstack_heads(dq_t, g, dq_ref, SCALE)

        prev0 = pl.multiple_of(jnp.maximum(i - 1, 0) * BLK, BLK)
        cur0 = pl.multiple_of(i * BLK, BLK)
        for ref, cat in ((dk_ref, dk_cat), (dv_ref, dv_cat)):
            ref[0:BLK, :] += cat[0:BLK]
            ref[pl.ds(prev0, BLK), :] += cat[BLK:2 * BLK]
            ref[pl.ds(cur0, BLK), :] += cat[2 * BLK:N_KEY]

        @pl.when(i == NBLK - 1)
        def _():
            lane = lax.broadcasted_iota(jnp.int32, (1, BLK), 1)

            def per_bucket(b, carry):
                row = jnp.zeros((1, BLK), F32)
                for h in range(N_HEADS):
                    g, cols = h // GROUP, slice((h % GROUP) * BLK, (h % GROUP + 1) * BLK)
                    val = (jnp.sum(jnp.where(bkt_ref[0] == b, acc[g, 2 * BLK:N_KEY, cols], 0.0), keepdims=True)
                           + jnp.sum(jnp.where(bkt_ref[1] == b, acc[g, BLK:2 * BLK, cols], 0.0), keepdims=True))
                    row = jnp.where(lane == h, val, row)
                dbias_ref[pl.ds(b, 1), :] = row
                return carry

            lax.fori_loop(0, N_BUCKETS, per_bucket, 0)
            far = jnp.zeros((1, BLK), F32)
            dsr = jnp.zeros((1, BLK), F32)
            for h in range(N_HEADS):
                g, cols = h // GROUP, slice((h % GROUP) * BLK, (h % GROUP + 1) * BLK)
                far = jnp.where(lane == h, jnp.sum(acc[g, 0:BLK, cols], keepdims=True), far)
                dsr = jnp.where(lane == h, jnp.sum(dsk[g, :, cols], keepdims=True), dsr)
            dbias_ref[N_BUCKETS - 1:N_BUCKETS, :] += far
            dsink_ref[...] = dsr

    smem = pl.BlockSpec(memory_space=pltpu.SMEM)
    blk512 = lambda col: pl.BlockSpec((BLK, 512), lambda i: (i, col))
    full = lambda r, c: pl.BlockSpec((r, c), lambda i: (0, 0))
    return pl.pallas_call(
        body, grid=(NBLK,),
        in_specs=[smem, smem, pl.BlockSpec((2, BLK, BLK), lambda i: (0, 0, 0)), blk512(QA)]
        + _seg_specs(True, KA) + _seg_specs(True, VA) + _seg_specs(False, 0)
        + [blk512(0), blk512(0), pl.BlockSpec((N_HEADS, 1, BLK), lambda i: (0, 0, i))],
        out_specs=[blk512(0), full(LP, BLK), full(LP, BLK), full(N_BUCKETS, BLK), full(1, BLK)],
        out_shape=[SDS((LP, 512), BF16), SDS((LP, BLK), F32), SDS((LP, BLK), F32),
                   SDS((N_BUCKETS, BLK), F32), SDS((1, BLK), F32)],
        scratch_shapes=[pltpu.VMEM((3, 2, N_KEY, QW), F32), pltpu.VMEM((2, 1, QW), F32),
                        pltpu.VMEM((2, N_KEY, QW), F32), pltpu.VMEM((2, 1, QW), F32)],
        compiler_params=_cparams(("arbitrary",)), name="swa_bwd",
    )(rel_bias, sinks, bkt_t, proj, proj, proj, proj, proj, proj, proj, kt_a, kt_a, kt_a, o_a, dmix, lse)


def _local_step(x, tgt, meta, rel_bias, g_pre_mix, g_post_mix, g_pre_ffn, g_post_ffn, b_forget, sinks,
                w_in_b, w_out_b, ffn_rider, ffn_weights, early_grads):
    bkt_t = jnp.asarray(_bucket_tables_t())
    h0 = jnp.concatenate([jnp.zeros((PAD_ROWS, D_MODEL), F32), meta, x], axis=0)
    tgt_p = jnp.concatenate([jnp.zeros((ROW0, D_MODEL), F32), tgt], axis=0)
    b_p = jnp.pad(b_forget, ((0, 0), (0, BLK - N_HEADS)))

    hn1, proj, f = _pre_mix(h0, g_pre_mix, w_in_b)
    kt_a, vt_a = _swa_prep(proj)
    o_a, lse_a = _swa_fwd(proj, vt_a, rel_bias, sinks, bkt_t)
    cum = _forget_cumsum(f, b_p)
    ck_t = cum[:, :N_HEADS].T.reshape(N_HEADS, 1, LP)
    q_aug, k_aug, v_t = _fox_prep(proj, cum)
    o_b, lse_row, carried = _fox_fwd(q_aug, k_aug, v_t, ffn_rider)
    lse_b = lse_row.reshape(N_HEADS, LP, 1)
    w_gu_b, w_dn_b = ffn_weights(carried)
    a, h1, hn2 = _attn_out(o_a, o_b, w_out_b, h0, g_post_mix, g_pre_ffn)
    g, u, act = _ffn_up(hn2, w_gu_b)
    dff, dy, loss_blk, dg_post_ffn = _ffn_down_loss(act, w_dn_b, h1, tgt_p, g_post_ffn)

    dw_dn = _mm_tn([act], dff, FF_T, "dw_down")
    dg, du = _ffn_down_bwd(dff, w_dn_b, g, u)
    dw_gu = _dw_gate_up(hn2, dg, du)
    dh1, da, dg_pre_ffn, dg_post_mix = _ffn_up_bwd(dg, du, w_gu_b, h1, a, dy, g_pre_ffn, g_post_mix)
    dw_out = _mm_tn([o_a, o_b], da, D_MODEL, "dw_out")
    dmix = _attn_out_bwd(da, w_out_b)
    dq_b, dk_b, dv_b, dck, dcq, landed = _fox_bwd(proj, o_b, dmix, lse_b, ck_t, early_grads(dw_gu, dw_dn, dw_out))
    dq_a, dk_a, dv_a, dbias, dsink = _swa_bwd(proj, kt_a, o_a, dmix, lse_a, rel_bias, sinks, bkt_t)
    dcum = dcq - jnp.pad(dck.reshape(N_HEADS, LP).T, ((0, 0), (0, BLK - N_HEADS)))
    df, db = _forget_cumsum_bwd(dcum, f, b_p)
    dproj, dh0, dg_pre_mix = _pre_mix_bwd(dq_a, dq_b, dk_b, dv_b, dk_a, dv_a, df, w_in_b, h0, dh1, g_pre_mix)
    dw_in = _mm_tn([hn1], dproj, D_MODEL, "dw_in")

    return dict(loss=loss_blk[0, 0], grad_x=dh0[ROW0:], meta=dh0[PAD_ROWS:ROW0],
                rel_bias=dbias[:, :N_HEADS], ln_pre_mix=dg_pre_mix, ln_post_mix=dg_post_mix,
                ln_pre_ffn=dg_pre_ffn, ln_post_ffn=dg_post_ffn, b_forget=db[:, :N_HEADS],
                sinks=dsink[:, :N_HEADS], w_in=dw_in, w_out=dw_out, w_gate_up=dw_gu, w_down=dw_dn,
                landed=landed)


N_SMALL = 24
LOSS_ROW = 6


def _place():
    x, y, c = lax.axis_index("x"), lax.axis_index("y"), lax.axis_index("c")
    return x, y, c, [(1 - x, y), (x, 1 - y), (1 - x, 1 - y)]


def _run_alone(rider, name):
    a, b = len(rider.operands), len(rider.out_shapes)

    def body(*refs):
        mine = (refs[:a], refs[a:a + b], refs[a + b:])
        rider.first(*mine)
        rider.middle(*mine)
        rider.last(*mine)

    return pl.pallas_call(body, in_specs=[HBM_SPEC] * a, out_specs=[HBM_SPEC] * b, out_shape=rider.out_shapes,
                          scratch_shapes=rider.scratch(), name=name)(*rider.operands)


def _gather_rider(shards, own_too):
    n = len(shards)

    def own_copies(ins, outs, sems):
        x, y, _, _ = _place()
        return [pltpu.make_async_copy(ins[a], outs[a].at[2 * x + y], sems[2].at[a]) for a in range(n)] if own_too else []

    def copies(ins, outs, sems):
        send_sems, recv_sems = sems[:2]
        x, y, c, others = _place()
        chip = 2 * x + y
        sibling = (x, y, 1 - c)

        def rc(a, k, src, dst, to):
            return pltpu.make_async_remote_copy(src_ref=src, dst_ref=dst, send_sem=send_sems.at[6 * a + k],
                                                recv_sem=recv_sems.at[6 * a + k], device_id=to, device_id_type=MESH)

        pairs = [(a, j, ox, oy) for a in range(n) for j, (ox, oy) in enumerate(others)]
        return dict(
            sent=lambda: [rc(a, j, ins[a].at[c], outs[a].at[chip, c], (ox, oy, c)) for a, j, ox, oy in pairs],
            landed=lambda: [rc(a, j, outs[a].at[2 * ox + oy, c], outs[a].at[2 * ox + oy, c], sibling)
                            for a, j, ox, oy in pairs],
            passed=lambda: [rc(a, 3 + j, outs[a].at[2 * ox + oy, c], outs[a].at[2 * ox + oy, c], sibling)
                            for a, j, ox, oy in pairs],
            arriving=lambda: [rc(a, 3 + j, outs[a].at[2 * ox + oy, 1 - c], outs[a].at[2 * ox + oy, 1 - c], sibling)
                              for a, j, ox, oy in pairs])

    def first(*mine):
        for cp in copies(*mine)["sent"]() + own_copies(*mine):
            cp.start()

    def middle(*mine):
        kinds = copies(*mine)
        for got, cp in zip(kinds["landed"](), kinds["passed"]()):
            got.wait_recv()
            cp.start()

    def last(*mine):
        kinds = copies(*mine)
        for cp in kinds["arriving"]():
            cp.wait_recv()
        for cp in kinds["sent"]() + kinds["passed"]():
            cp.wait_send()
        for cp in own_copies(*mine):
            cp.wait()

    return _Rider(shards, [SDS((4,) + s.shape, s.dtype) for s in shards], [6 * n, 6 * n] + [n] * own_too,
                  first, middle, last)


def _swap_halves(grads, name):
    n = len(grads)

    def body(*refs):
        ins, outs = refs[:n], refs[n:2 * n]
        send_sems, recv_sems = refs[2 * n:]
        x, y, c, _ = _place()
        copies = [pltpu.make_async_remote_copy(
            src_ref=ins[a].at[s, 1 - c], dst_ref=outs[a].at[s], send_sem=send_sems.at[4 * a + s],
            recv_sem=recv_sems.at[4 * a + s], device_id=(x, y, 1 - c), device_id_type=MESH)
            for a in range(n) for s in range(4)]
        for cp in copies:
            cp.start()
        for cp in copies:
            cp.wait()

    return pl.pallas_call(
        body, in_specs=[HBM_SPEC] * n, out_specs=[HBM_SPEC] * n,
        out_shape=[SDS((4,) + g.shape[2:], g.dtype) for g in grads],
        scratch_shapes=[pltpu.SemaphoreType.DMA((4 * n,)), pltpu.SemaphoreType.DMA((4 * n,))],
        name=name)(*grads)


def _pair_sum(g, got, c_arr, name):
    rh, cc = got.shape[1:]

    def body(c_ref, g_ref, p_ref, o_ref):
        o_ref[0] = (g_ref[0, 0] + p_ref[0]).astype(BF16)

    grid_spec = pltpu.PrefetchScalarGridSpec(
        num_scalar_prefetch=1, grid=(4,),
        in_specs=[pl.BlockSpec((1, 1, rh, cc), lambda s, c_ref: (s, c_ref[0], 0, 0)),
                  pl.BlockSpec((1, rh, cc), lambda s, c_ref: (s, 0, 0))],
        out_specs=pl.BlockSpec((1, rh, cc), lambda s, c_ref: (s, 0, 0)))
    return pl.pallas_call(body, grid_spec=grid_spec, out_shape=SDS((4, rh, cc), BF16),
                          compiler_params=_cparams(("parallel",)), name=name)(c_arr, g, got)


def _exchange_rider(parts, small=None):
    n = len(parts)

    def copies(ins, outs, sems):
        x, y, c, others = _place()
        out = [pltpu.make_async_remote_copy(
            src_ref=ins[a].at[2 * ox + oy], dst_ref=outs[a].at[j], send_sem=sems[0].at[3 * a + j],
            recv_sem=sems[1].at[3 * a + j], device_id=(ox, oy, c), device_id_type=MESH)
            for a in range(n) for j, (ox, oy) in enumerate(others)]
        own = []
        if small is not None:
            me = 4 * x + 2 * y + c
            peers = [(x, y, 1 - c)] + [(ox, oy, c) for ox, oy in others] + [(ox, oy, 1 - c) for ox, oy in others]
            out += [pltpu.make_async_remote_copy(
                src_ref=ins[n], dst_ref=outs[n].at[me], send_sem=sems[2].at[k], recv_sem=sems[3].at[k],
                device_id=peer, device_id_type=MESH) for k, peer in enumerate(peers)]
            own = [pltpu.make_async_copy(ins[n], outs[n].at[me], sems[4].at[0])]
        return out, own

    def first(*mine):
        out, own = copies(*mine)
        for cp in own + out:
            cp.start()

    def middle(*mine):
        pass

    def last(*mine):
        out, own = copies(*mine)
        for cp in out + own:
            cp.wait()

    shapes = [SDS((3,) + p.shape[1:], p.dtype) for p in parts]
    if small is None:
        return _Rider(parts, shapes, [3 * n, 3 * n], first, middle, last)
    return _Rider(parts + [small], shapes + [SDS((8,) + small.shape, small.dtype)], [3 * n, 3 * n, 7, 7, 1],
                  first, middle, last)


def _chip_sum(parts, landed, chip_arr, name):
    rh, cc = landed.shape[1:]
    tr = rh // 2

    def body(chip_ref, own_ref, p_ref, o_ref):
        o_ref[...] = ((own_ref[0].astype(F32) + p_ref[0].astype(F32)) + p_ref[1].astype(F32)) + p_ref[2].astype(F32)

    grid_spec = pltpu.PrefetchScalarGridSpec(
        num_scalar_prefetch=1, grid=(2,),
        in_specs=[pl.BlockSpec((1, tr, cc), lambda i, chip_ref: (chip_ref[0], i, 0)),
                  pl.BlockSpec((3, tr, cc), lambda i, chip_ref: (0, i, 0))],
        out_specs=pl.BlockSpec((tr, cc), lambda i, chip_ref: (i, 0)))
    return pl.pallas_call(body, grid_spec=grid_spec, out_shape=SDS((rh, cc), F32),
                          compiler_params=_cparams(("parallel",)), name=name)(chip_arr, parts, landed)


def _device_sum(p):
    def body(p_ref, o_ref):
        acc = p_ref[0]
        for k in range(1, 8):
            acc = acc + p_ref[k]
        o_ref[...] = acc

    return pl.pallas_call(body, out_shape=SDS(p.shape[1:], F32), name="small_sum")(p)


def _join_halves(halves):
    n = len(halves)

    def body(*refs):
        ins, outs = refs[:n], refs[n:2 * n]
        send_sems, recv_sems = refs[2 * n:]
        x, y, c, _ = _place()
        copies = [pltpu.make_async_remote_copy(
            src_ref=ins[a], dst_ref=outs[a], send_sem=send_sems.at[a], recv_sem=recv_sems.at[a],
            device_id=(x, y, 1 - c), device_id_type=MESH) for a in range(n)]
        for cp in copies:
            cp.start()
        for cp in copies:
            cp.wait()

    return pl.pallas_call(
        body, in_specs=[HBM_SPEC] * n, out_specs=[HBM_SPEC] * n,
        out_shape=[SDS(h.shape, h.dtype) for h in halves],
        scratch_shapes=[pltpu.SemaphoreType.DMA((n,)), pltpu.SemaphoreType.DMA((n,))],
        name="join_halves")(*halves)


def _adamw(w, g, m, v, name):
    rows, cols = w.shape
    tr = rows if rows <= 352 else (256 if rows % 256 == 0 else 352)

    def body(w_ref, g_ref, m_ref, v_ref, d_ref, nm_ref, nv_ref):
        gg = g_ref[...]
        nm = ADAM_B1 * m_ref[...] + (1.0 - ADAM_B1) * gg
        nv = ADAM_B2 * v_ref[...] + (1.0 - ADAM_B2) * (gg * gg)
        nm_ref[...] = nm
        nv_ref[...] = nv
        m_hat = nm / (1.0 - ADAM_B1 ** ADAM_STEP)
        v_hat = nv / (1.0 - ADAM_B2 ** ADAM_STEP)
        d_ref[...] = -ADAM_LR * (m_hat / (jnp.sqrt(v_hat) + ADAM_EPS) + ADAM_WD * w_ref[...])

    blk = pl.BlockSpec((tr, cols), lambda i: (i, 0))
    return pl.pallas_call(
        body, grid=(rows // tr,), in_specs=[blk] * 4, out_specs=[blk] * 3,
        out_shape=[SDS((rows, cols), F32)] * 3,
        compiler_params=_cparams(("parallel",)), name=name)(w, g, m, v)


def _adamw_halves(w, mine, theirs, m, v, c_arr, name):
    rows, cols = w.shape
    rh = rows // 2
    tr = rh if rh <= 352 else 256
    nh = rh // tr

    def body(c_ref, w_ref, mine_ref, theirs_ref, m_ref, v_ref, g_ref, d_ref, nm_ref, nv_ref):
        own = jnp.full((tr, cols), pl.program_id(0), jnp.int32) == c_ref[0]
        gg = jnp.where(own, mine_ref[...], theirs_ref[...])
        g_ref[...] = gg
        nm = ADAM_B1 * m_ref[...] + (1.0 - ADAM_B1) * gg
        nv = ADAM_B2 * v_ref[...] + (1.0 - ADAM_B2) * (gg * gg)
        nm_ref[...] = nm
        nv_ref[...] = nv
        m_hat = nm / (1.0 - ADAM_B1 ** ADAM_STEP)
        v_hat = nv / (1.0 - ADAM_B2 ** ADAM_STEP)
        d_ref[...] = -ADAM_LR * (m_hat / (jnp.sqrt(v_hat) + ADAM_EPS) + ADAM_WD * w_ref[...])

    whole = pl.BlockSpec((tr, cols), lambda hh, i, c_ref: (hh * nh + i, 0))
    part = pl.BlockSpec((tr, cols), lambda hh, i, c_ref: (i, 0))
    grid_spec = pltpu.PrefetchScalarGridSpec(
        num_scalar_prefetch=1, grid=(2, nh), in_specs=[whole, part, part, whole, whole], out_specs=[whole] * 4)
    return pl.pallas_call(body, grid_spec=grid_spec, out_shape=[SDS((rows, cols), F32)] * 4,
                          compiler_params=_cparams(("parallel", "parallel")), name=name)(c_arr, w, mine, theirs, m, v)


def _pack_small(pre_mix, post_mix, pre_ffn, post_ffn, rel_bias, b_forget, sinks):
    def at(row, v):
        return jnp.pad(v, ((row, 7 - row), (0, D_MODEL - v.shape[1])))
    return (at(0, pre_mix) + at(1, post_mix) + at(2, pre_ffn) + at(3, post_ffn)
            + at(4, rel_bias.reshape(1, N_BUCKETS * N_HEADS)) + at(5, jnp.concatenate([b_forget, sinks], axis=1)))


def _unpack_small(p):
    return dict(ln_pre_mix=p[0:1], ln_post_mix=p[1:2], ln_pre_ffn=p[2:3], ln_post_ffn=p[3:4],
                rel_bias=p[4, :N_BUCKETS * N_HEADS].reshape(N_BUCKETS, N_HEADS),
                b_forget=p[5:6, 0:N_HEADS], sinks=p[5:6, N_HEADS:2 * N_HEADS])


WEIGHTS = ("meta_tokens", "rel_bias", "ln_pre_mix", "ln_post_mix", "ln_pre_ffn", "ln_post_ffn",
           "w_in", "b_forget", "sinks", "w_out", "w_gate_up", "w_down")


def kernel(x, meta_tokens, rel_bias, ln_pre_mix, ln_post_mix, ln_pre_ffn, ln_post_ffn, w_in, b_forget, sinks, w_out, w_gate_up, w_down, loss_target, m_meta_tokens, m_rel_bias, m_ln_pre_mix, m_ln_post_mix, m_ln_pre_ffn, m_ln_post_ffn, m_w_in, m_b_forget, m_sinks, m_w_out, m_w_gate_up, m_w_down, v_meta_tokens, v_rel_bias, v_ln_pre_mix, v_ln_post_mix, v_ln_pre_ffn, v_ln_post_ffn, v_w_in, v_b_forget, v_sinks, v_w_out, v_w_gate_up, v_w_down):
    xi, yi, ci = lax.axis_index("x"), lax.axis_index("y"), lax.axis_index("c")
    chip = 2 * xi + yi
    c_arr = jnp.reshape(ci, (1,)).astype(jnp.int32)

    def halves(w, dtype):
        return w.astype(dtype).reshape(2, w.shape[0] // 2, w.shape[1])

    def with_own(gathered, shards):
        return [lax.dynamic_update_slice(got, own[None], (chip, 0, 0, 0)) for got, own in zip(gathered, shards)]

    shards = [halves(w_in[0], BF16), halves(w_out[0], BF16), halves(meta_tokens, F32)]
    gw_in, gw_out, g_meta = with_own(_run_alone(_gather_rider(shards, False), "gather_mixer_weights"), shards)
    ffn_shards = [halves(w_gate_up[0], BF16), halves(w_down[0], BF16)]

    def ffn_weights(carried):
        gw_gu, gw_dn = carried
        w_gu_b = gw_gu.reshape(4, D_MODEL, FF_T).transpose(1, 0, 2).reshape(D_MODEL, 2 * D_FF)
        return w_gu_b, gw_dn.reshape(D_FF, D_MODEL)

    early = {}

    def early_grads(dw_gu, dw_dn, dw_out):
        grads = [dw_out.reshape(4, 2, 128, D_MODEL), dw_gu.reshape(4, 2, 512, FF_T), dw_dn.reshape(4, 2, 352, D_MODEL)]
        got = _swap_halves(grads, "swap_halves_early")
        early["parts"] = [_pair_sum(g, p, c_arr, "pair_sum_%d" % a) for a, (g, p) in enumerate(zip(grads, got))]
        return _exchange_rider(early["parts"])
    w_in_all = gw_in.reshape(4, D_MODEL, D_PROJ // 4).transpose(1, 0, 2).reshape(D_MODEL, D_PROJ)
    w_in_b = jnp.concatenate(
        [w_in_all[:, 0:512], w_in_all[:, 768:1280], w_in_all[:, 1280:1792], w_in_all[:, 1792:2304],
         w_in_all[:, 512:640], w_in_all[:, 640:768], w_in_all[:, 2304:2312],
         jnp.zeros((D_MODEL, D_PROJ_P - D_PROJ), BF16)], axis=1)
    meta_all = g_meta.reshape(4, N_META, D_MODEL // 4).transpose(1, 0, 2).reshape(N_META, D_MODEL)

    loc = _local_step(x[0], loss_target[0], meta_all, rel_bias, ln_pre_mix, ln_post_mix, ln_pre_ffn, ln_post_ffn,
                      b_forget, sinks, w_in_b, gw_out.reshape(D_MODEL, D_MODEL),
                      _gather_rider(ffn_shards, True), ffn_weights, early_grads)

    n = loc["w_in"]
    dw_in = jnp.concatenate([n[:, 0:512], n[:, 2048:2176], n[:, 2176:2304], n[:, 512:1024], n[:, 1024:1536],
                             n[:, 1536:2048], n[:, 2304:2312]], axis=1)
    dw_in = dw_in.reshape(D_MODEL, 4, D_PROJ // 4).transpose(1, 0, 2).reshape(4, 2, 512, D_PROJ // 4)
    small = jnp.concatenate(
        [_pack_small(loc["ln_pre_mix"], loc["ln_post_mix"], loc["ln_pre_ffn"], loc["ln_post_ffn"],
                     loc["rel_bias"], loc["b_forget"], loc["sinks"])
         + jnp.pad(loc["loss"].reshape(1, 1), ((LOSS_ROW, 7 - LOSS_ROW), (0, D_MODEL - 1))), loc["meta"]], axis=0)

    (got_in,) = _swap_halves([dw_in], "swap_halves_late")
    part_in = _pair_sum(dw_in, got_in, c_arr, "pair_sum_late")
    landed_in, small_all = _run_alone(_exchange_rider([part_in], small), "exchange_late")
    chip_arr = jnp.reshape(chip, (1,)).astype(jnp.int32)
    parts = [part_in] + early["parts"]
    landed = [landed_in] + loc["landed"]
    mine = [_chip_sum(p, l, chip_arr, "chip_sum_%d" % a) for a, (p, l) in enumerate(zip(parts, landed))]
    small_sum = _device_sum(small_all)
    theirs = _join_halves(mine)
    g_meta_tokens = lax.dynamic_slice(small_sum[8:N_SMALL], (0, chip * (D_MODEL // 4)), (N_META, D_MODEL // 4))
    g_small = small_sum[0:8]

    grad = _unpack_small(g_small)
    grad.update(meta_tokens=g_meta_tokens)
    delta, new_m, new_v = {}, {}, {}
    big = dict(w_in=(w_in, m_w_in, v_w_in), w_out=(w_out, m_w_out, v_w_out),
               w_gate_up=(w_gate_up, m_w_gate_up, v_w_gate_up), w_down=(w_down, m_w_down, v_w_down))
    for (name, (w, m, v)), g_mine, g_theirs in zip(big.items(), mine, theirs):
        g, d, nm, nv = _adamw_halves(w[0], g_mine, g_theirs, m[0], v[0], c_arr, "adamw_" + name)
        grad[name], delta[name], new_m[name], new_v[name] = g[None], d[None], nm[None], nv[None]
    delta["meta_tokens"], new_m["meta_tokens"], new_v["meta_tokens"] = _adamw(
        meta_tokens, g_meta_tokens, m_meta_tokens, v_meta_tokens, "adamw_meta")
    d, nm, nv = _adamw(
        _pack_small(ln_pre_mix, ln_post_mix, ln_pre_ffn, ln_post_ffn, rel_bias, b_forget, sinks), g_small,
        _pack_small(m_ln_pre_mix, m_ln_post_mix, m_ln_pre_ffn, m_ln_post_ffn, m_rel_bias, m_b_forget, m_sinks),
        _pack_small(v_ln_pre_mix, v_ln_post_mix, v_ln_pre_ffn, v_ln_post_ffn, v_rel_bias, v_b_forget, v_sinks),
        "adamw_small")
    delta.update(_unpack_small(d))
    new_m.update(_unpack_small(nm))
    new_v.update(_unpack_small(nv))

    loss = small_sum[LOSS_ROW, 0]
    return (loss,loc["grad_x"][None], *[grad[k] for k in WEIGHTS], *[delta[k] for k in WEIGHTS],
            *[new_m[k] for k in WEIGHTS], *[new_v[k] for k in WEIGHTS])
```

```python
import math

import numpy as np
import jax
import jax.numpy as jnp
from jax import lax
from jax.experimental import pallas as pl
from jax.experimental.pallas import tpu as pltpu

F32 = jnp.float32
BF16 = jnp.bfloat16
MESH = pl.DeviceIdType.MESH
SDS = jax.ShapeDtypeStruct

D_MODEL = 1024
SEQ = 4096
N_META = 16
N_HEADS = 8
HALF = 64
D_FF = 2816
N_BUCKETS = 32
EPS = 1e-6
NEG = -1e30
SCALE = 0.125
PAD_ROWS = 112
ROW0 = PAD_ROWS + N_META
LP = ROW0 + SEQ
BLK = 128
NBLK = LP // BLK
TM = 384
NT = LP // TM
TM_PURE = LP // 2
TM_MID = LP // 4
TM_EPI = LP // 6
TN = 256
D_PROJ = 2312
D_PROJ_P = 2432
D_QKV = 2304
FF_T = 1408
VMEM_LIMIT = 56 * 1024 * 1024

ADAM_LR = 0.001
ADAM_B1 = 0.9
ADAM_B2 = 0.999
ADAM_EPS = 1e-08
ADAM_WD = 0.01
ADAM_STEP = 10

QA, QB, KB, VB = 0, 1, 2, 3
KA, VA = 16, 17

NT_DIMS = (((1,), (1,)), ((), ()))
TN_DIMS = (((0,), (0,)), ((), ()))


def _cparams(sem):
    return pltpu.CompilerParams(dimension_semantics=sem, vmem_limit_bytes=VMEM_LIMIT)


def _t5_bucket_np(d):
    n = np.maximum(d, 0).astype(np.int32)
    nf = np.maximum(n, 1).astype(np.float32)
    large = 16 + (np.log(nf / np.float32(16)) / np.float32(math.log(8.0)) * np.float32(16)).astype(np.int32)
    large = np.minimum(large, N_BUCKETS - 1)
    return np.where(n < 16, n, large).astype(np.int32)


def _bucket_tables():
    qi = np.arange(BLK)[:, None]
    ki = np.arange(BLK)[None, :]
    return np.stack([_t5_bucket_np(qi - ki), _t5_bucket_np(qi - ki + BLK)])


def _rms(x):
    return lax.rsqrt(jnp.mean(x * x, axis=-1, keepdims=True) + EPS)


def _rms_bwd(n, r, gdy):
    return r * (gdy - n * jnp.mean(n * gdy, axis=-1, keepdims=True))


def _pre_mix(h0, gain, w_in_b):
    half = D_QKV // 2

    def body(h_ref, g_ref, w_ref, hn_ref, proj_ref, f_ref):
        x = h_ref[...]
        hn = (x * _rms(x) * g_ref[...]).astype(BF16)
        hn_ref[...] = hn
        proj_ref[:, :half] = jnp.dot(hn, w_ref[:, :half], preferred_element_type=F32).astype(BF16)
        p = jnp.dot(hn, w_ref[:, half:], preferred_element_type=F32)
        proj_ref[:, half:] = p[:, :half].astype(BF16)
        f_ref[...] = p[:, half:]

    return pl.pallas_call(
        body, grid=(LP // TM_MID,),
        in_specs=[pl.BlockSpec((TM_MID, D_MODEL), lambda i: (i, 0)),
                  pl.BlockSpec((1, D_MODEL), lambda i: (0, 0)),
                  pl.BlockSpec((D_MODEL, D_PROJ_P), lambda i: (0, 0))],
        out_specs=[pl.BlockSpec((TM_MID, D_MODEL), lambda i: (i, 0)),
                   pl.BlockSpec((TM_MID, D_QKV), lambda i: (i, 0)),
                   pl.BlockSpec((TM_MID, BLK), lambda i: (i, 0))],
        out_shape=[SDS((LP, D_MODEL), BF16), SDS((LP, D_QKV), BF16), SDS((LP, BLK), F32)],
        compiler_params=_cparams(("parallel",)), name="pre_mix")(h0, gain, w_in_b)


def _attn_out(o_a, o_b, w_out_b, h0, g_post, g_pre_ffn):
    def body(oa_ref, ob_ref, w_ref, h0_ref, gp_ref, gf_ref, a_ref, h1_ref, hn2_ref):
        a = (jnp.dot(oa_ref[...], w_ref[0:512, :], preferred_element_type=F32)
             + jnp.dot(ob_ref[...], w_ref[512:1024, :], preferred_element_type=F32))
        a_ref[...] = a
        h1 = h0_ref[...] + a * _rms(a) * gp_ref[...]
        h1_ref[...] = h1
        hn2_ref[...] = (h1 * _rms(h1) * gf_ref[...]).astype(BF16)

    row = lambda w: pl.BlockSpec((TM_EPI, w), lambda i: (i, 0))
    vec = pl.BlockSpec((1, D_MODEL), lambda i: (0, 0))
    return pl.pallas_call(
        body, grid=(LP // TM_EPI,),
        in_specs=[row(512), row(512), pl.BlockSpec((D_MODEL, D_MODEL), lambda i: (0, 0)), row(D_MODEL), vec, vec],
        out_specs=[row(D_MODEL), row(D_MODEL), row(D_MODEL)],
        out_shape=[SDS((LP, D_MODEL), F32), SDS((LP, D_MODEL), F32), SDS((LP, D_MODEL), BF16)],
        compiler_params=_cparams(("parallel",)), name="attn_out")(o_a, o_b, w_out_b, h0, g_post, g_pre_ffn)


def _ffn_up(hn2, w_gu_b):
    def body(x_ref, wg_ref, wu_ref, g_ref, u_ref, act_ref):
        x = x_ref[...]
        g = jnp.dot(x, wg_ref[...], preferred_element_type=F32)
        u = jnp.dot(x, wu_ref[...], preferred_element_type=F32)
        g_ref[...] = g.astype(BF16)
        u_ref[...] = u.astype(BF16)
        act_ref[...] = (g * (1.0 / (1.0 + jnp.exp(-g))) * u).astype(BF16)

    out = pl.BlockSpec((TM_PURE, TN), lambda i, j: (i, j))
    return pl.pallas_call(
        body, grid=(LP // TM_PURE, D_FF // TN),
        in_specs=[pl.BlockSpec((TM_PURE, D_MODEL), lambda i, j: (i, 0)),
                  pl.BlockSpec((D_MODEL, TN), lambda i, j: (0, j)),
                  pl.BlockSpec((D_MODEL, TN), lambda i, j: (0, j + D_FF // TN))],
        out_specs=[out, out, out],
        out_shape=[SDS((LP, D_FF), BF16)] * 3,
        compiler_params=_cparams(("parallel", "parallel")), name="ffn_up")(hn2, w_gu_b, w_gu_b)


def _ffn_down_loss(act, w_dn_b, h1, tgt, g_post_ffn):
    def body(act_ref, w_ref, h1_ref, t0_ref, t1_ref, t2_ref, g_ref, dff_ref, dy_ref, loss_ref, dg_ref):
        i = pl.program_id(0)
        target = jnp.concatenate([t0_ref[...], t1_ref[...], t2_ref[...]], axis=0)

        @pl.when(i == 0)
        def _():
            loss_ref[...] = jnp.zeros_like(loss_ref)
            dg_ref[...] = jnp.zeros_like(dg_ref)

        ff = jnp.dot(act_ref[...], w_ref[...], preferred_element_type=F32)
        r = _rms(ff)
        n = ff * r
        g = g_ref[...]
        y = h1_ref[...] + n * g
        rows = i * TM + lax.broadcasted_iota(jnp.int32, (TM, D_MODEL), 0)
        diff = jnp.where(rows >= ROW0, y - target, 0.0)
        loss_ref[...] += 0.5 * jnp.sum(diff * diff) / D_MODEL
        dy = diff / D_MODEL
        dy_ref[...] = dy
        dg_ref[...] += jnp.sum(dy * n, axis=0, keepdims=True)
        dff_ref[...] = _rms_bwd(n, r, g * dy).astype(BF16)

    row = pl.BlockSpec((TM, D_MODEL), lambda i: (i, 0))
    tblk = lambda j: pl.BlockSpec((BLK, D_MODEL), lambda i: (jnp.maximum(3 * i - 1 + j, 0), 0))
    return pl.pallas_call(
        body, grid=(NT,),
        in_specs=[pl.BlockSpec((TM, D_FF), lambda i: (i, 0)), pl.BlockSpec((D_FF, D_MODEL), lambda i: (0, 0)),
                  row, tblk(0), tblk(1), tblk(2), pl.BlockSpec((1, D_MODEL), lambda i: (0, 0))],
        out_specs=[row, row, pl.BlockSpec((8, BLK), lambda i: (0, 0)), pl.BlockSpec((1, D_MODEL), lambda i: (0, 0))],
        out_shape=[SDS((LP, D_MODEL), BF16), SDS((LP, D_MODEL), F32), SDS((8, BLK), F32), SDS((1, D_MODEL), F32)],
        compiler_params=_cparams(("arbitrary",)), name="ffn_down_loss")(act, w_dn_b, h1, tgt, tgt, tgt, g_post_ffn)


def _ffn_down_bwd(dff, w_dn_b, g, u):
    def body(d_ref, w_ref, g_ref, u_ref, dg_ref, du_ref):
        dact = lax.dot_general(d_ref[...], w_ref[...], NT_DIMS, preferred_element_type=F32)
        gg = g_ref[...].astype(F32)
        sig = 1.0 / (1.0 + jnp.exp(-gg))
        dg_ref[...] = (dact * u_ref[...].astype(F32) * sig * (1.0 + gg * (1.0 - sig))).astype(BF16)
        du_ref[...] = (dact * gg * sig).astype(BF16)

    blk = pl.BlockSpec((TM_PURE, TN), lambda i, j: (i, j))
    return pl.pallas_call(
        body, grid=(LP // TM_PURE, D_FF // TN),
        in_specs=[pl.BlockSpec((TM_PURE, D_MODEL), lambda i, j: (i, 0)),
                  pl.BlockSpec((TN, D_MODEL), lambda i, j: (j, 0)), blk, blk],
        out_specs=[blk, blk],
        out_shape=[SDS((LP, D_FF), BF16)] * 2,
        compiler_params=_cparams(("parallel", "parallel")), name="ffn_down_bwd")(dff, w_dn_b, g, u)


def _ffn_up_bwd(dg, du, w_gu_b, h1, a, dy, g_pre_ffn, g_post_mix, rider):
    n_rows = LP // TM_EPI

    def body(*refs):
        ((dg_ref, du_ref, w_ref, h1_ref, a_ref, dy_ref, gf_ref, gp_ref), (dh1_ref, da_ref, dgf_ref, dgp_ref),
         (acc,), mine) = rider.split(refs, 8, 4, 1)
        i = pl.program_id(0)
        s = pl.program_id(1)
        rider.at_steps(mine, (i == 0) & (s == 0), (i == n_rows // 2) & (s == 0), (i == n_rows - 1) & (s == 3))

        @pl.when((i == 0) & (s == 0))
        def _():
            dgf_ref[...] = jnp.zeros_like(dgf_ref)
            dgp_ref[...] = jnp.zeros_like(dgp_ref)

        @pl.when(s == 0)
        def _():
            acc[...] = jnp.zeros_like(acc)

        @pl.when(s < 2)
        def _():
            acc[...] += lax.dot_general(dg_ref[...], w_ref[...], NT_DIMS, preferred_element_type=F32)

        @pl.when(s >= 2)
        def _():
            acc[...] += lax.dot_general(du_ref[...], w_ref[...], NT_DIMS, preferred_element_type=F32)

        @pl.when(s == 3)
        def _():
            dhn2 = acc[...]
            h1 = h1_ref[...]
            r2 = _rms(h1)
            n2 = h1 * r2
            dgf_ref[...] += jnp.sum(dhn2 * n2, axis=0, keepdims=True)
            dh1 = dy_ref[...] + _rms_bwd(n2, r2, gf_ref[...] * dhn2)
            dh1_ref[...] = dh1
            av = a_ref[...]
            ra = _rms(av)
            na = av * ra
            dgp_ref[...] += jnp.sum(dh1 * na, axis=0, keepdims=True)
            da_ref[...] = _rms_bwd(na, ra, gp_ref[...] * dh1).astype(BF16)

    row = pl.BlockSpec((TM_EPI, D_MODEL), lambda i, s: (i, 0))
    vec = pl.BlockSpec((1, D_MODEL), lambda i, s: (0, 0))
    dh1, da, dgf, dgp, *carried = pl.pallas_call(
        body, grid=(n_rows, 4),
        in_specs=[pl.BlockSpec((TM_EPI, FF_T), lambda i, s: (i, jnp.minimum(s, 1))),
                  pl.BlockSpec((TM_EPI, FF_T), lambda i, s: (i, jnp.maximum(s - 2, 0))),
                  pl.BlockSpec((D_MODEL, FF_T), lambda i, s: (0, s)),
                  row, row, row, vec, vec] + [HBM_SPEC] * len(rider.operands),
        out_specs=[row, row, vec, vec] + [HBM_SPEC] * len(rider.out_shapes),
        out_shape=[SDS((LP, D_MODEL), F32), SDS((LP, D_MODEL), BF16), SDS((1, D_MODEL), F32),
                   SDS((1, D_MODEL), F32)] + rider.out_shapes,
        scratch_shapes=[pltpu.VMEM((TM_EPI, D_MODEL), F32)] + rider.scratch(),
        compiler_params=_cparams(("arbitrary", "arbitrary")), name="ffn_up_bwd",
    )(dg, du, w_gu_b, h1, a, dy, g_pre_ffn, g_post_mix, *rider.operands)
    return dh1, da, dgf, dgp, carried


def _attn_out_bwd(da, w_out_b):
    def body(d_ref, w_ref, o_ref):
        o_ref[...] = lax.dot_general(d_ref[...], w_ref[...], NT_DIMS, preferred_element_type=F32).astype(BF16)

    row = pl.BlockSpec((TM_PURE, D_MODEL), lambda i: (i, 0))
    return pl.pallas_call(
        body, grid=(LP // TM_PURE,),
        in_specs=[row, pl.BlockSpec((D_MODEL, D_MODEL), lambda i: (0, 0))],
        out_specs=row, out_shape=SDS((LP, D_MODEL), BF16),
        compiler_params=_cparams(("parallel",)), name="attn_out_bwd")(da, w_out_b)


def _pre_mix_bwd(dq_a, dq_b, dk_b, dv_b, dk_a, dv_a, df, w_in_b, h0, dh1, g_pre_mix):
    def body(qa_ref, qb_ref, kb_ref, vb_ref, ka_ref, va_ref, f_ref, w_ref, h0_ref, dh1_ref, g_ref,
             dproj_ref, dh0_ref, dg_ref):
        i = pl.program_id(0)

        @pl.when(i == 0)
        def _():
            dg_ref[...] = jnp.zeros_like(dg_ref)

        dproj = jnp.concatenate(
            [qa_ref[...], (qb_ref[...] * SCALE).astype(BF16), kb_ref[...], vb_ref[...],
             ka_ref[...].astype(BF16), va_ref[...].astype(BF16), f_ref[...].astype(BF16)], axis=1)
        dproj_ref[...] = dproj
        dhn = lax.dot_general(dproj, w_ref[...], NT_DIMS, preferred_element_type=F32)
        x = h0_ref[...]
        r = _rms(x)
        n = x * r
        dg_ref[...] += jnp.sum(dhn * n, axis=0, keepdims=True)
        dh0_ref[...] = dh1_ref[...] + _rms_bwd(n, r, g_ref[...] * dhn)

    row = lambda w: pl.BlockSpec((TM_EPI, w), lambda i: (i, 0))
    vec = pl.BlockSpec((1, D_MODEL), lambda i: (0, 0))
    return pl.pallas_call(
        body, grid=(LP // TM_EPI,),
        in_specs=[row(512), row(512), row(512), row(512), row(BLK), row(BLK), row(BLK),
                  pl.BlockSpec((D_MODEL, D_PROJ_P), lambda i: (0, 0)), row(D_MODEL), row(D_MODEL), vec],
        out_specs=[row(D_PROJ_P), row(D_MODEL), vec],
        out_shape=[SDS((LP, D_PROJ_P), BF16), SDS((LP, D_MODEL), F32), SDS((1, D_MODEL), F32)],
        compiler_params=_cparams(("arbitrary",)), name="pre_mix_bwd",
    )(dq_a, dq_b, dk_b, dv_b, dk_a, dv_a, df, w_in_b, h0, dh1, g_pre_mix)


def _mm_tn(parts, b, tm, name):
    widths = [p.shape[1] for p in parts]
    m_total = sum(widths)
    n = b.shape[1]
    whole = len(parts) > 1
    assert (tm == m_total) if whole else (m_total % tm == 0)

    def body(*refs):
        a_refs, b_ref, o_ref = refs[:-2], refs[-2], refs[-1]

        @pl.when(pl.program_id(1) == 0)
        def _():
            o_ref[...] = jnp.zeros_like(o_ref)
        a = a_refs[0][...] if not whole else jnp.concatenate([r[...] for r in a_refs], axis=1)
        o_ref[...] += lax.dot_general(a, b_ref[...], TN_DIMS, preferred_element_type=F32)

    a_specs = ([pl.BlockSpec((TM_MID, w), lambda mi, k: (k, 0)) for w in widths] if whole
               else [pl.BlockSpec((TM_MID, tm), lambda mi, k: (k, mi))])
    return pl.pallas_call(
        body, grid=(m_total // tm, LP // TM_MID),
        in_specs=a_specs + [pl.BlockSpec((TM_MID, n), lambda mi, k: (k, 0))],
        out_specs=pl.BlockSpec((tm, n), lambda mi, k: (mi, 0)),
        out_shape=SDS((m_total, n), F32),
        compiler_params=_cparams(("parallel", "arbitrary")), name=name)(*parts, b)


def _dw_gate_up(hn2, dg, du):
    def body(a_ref, dg_ref, du_ref, o_ref):
        s = pl.program_id(0)

        @pl.when(pl.program_id(1) == 0)
        def _():
            o_ref[...] = jnp.zeros_like(o_ref)

        @pl.when(s < 2)
        def _():
            o_ref[0] += lax.dot_general(a_ref[...], dg_ref[...], TN_DIMS, preferred_element_type=F32)

        @pl.when(s >= 2)
        def _():
            o_ref[0] += lax.dot_general(a_ref[...], du_ref[...], TN_DIMS, preferred_element_type=F32)

    return pl.pallas_call(
        body, grid=(4, LP // TM_MID),
        in_specs=[pl.BlockSpec((TM_MID, D_MODEL), lambda s, k: (k, 0)),
                  pl.BlockSpec((TM_MID, FF_T), lambda s, k: (k, jnp.minimum(s, 1))),
                  pl.BlockSpec((TM_MID, FF_T), lambda s, k: (k, jnp.maximum(s - 2, 0)))],
        out_specs=pl.BlockSpec((1, D_MODEL, FF_T), lambda s, k: (s, 0, 0)),
        out_shape=SDS((4, D_MODEL, FF_T), F32),
        compiler_params=_cparams(("parallel", "arbitrary")), name="dw_gate_up")(hn2, dg, du)


def _split3(x):
    hi = x.astype(BF16)
    r1 = x - hi.astype(F32)
    mid = r1.astype(BF16)
    lo = (r1 - mid.astype(F32)).astype(BF16)
    return hi, mid, lo


def _tri_matmul(tri, x):
    hi, mid, lo = _split3(x)
    dot = lambda t: jnp.dot(tri, t, preferred_element_type=F32)
    return dot(hi) + dot(mid) + dot(lo)


def _forget_cumsum(f, b_forget_p):
    def body(f_ref, b_ref, cum_ref, carry):
        i = pl.program_id(0)

        @pl.when(i == 0)
        def _():
            carry[...] = jnp.zeros_like(carry)

        z = f_ref[...] + b_ref[...]
        ls = jnp.minimum(z, 0.0) - jnp.log(1.0 + jnp.exp(-jnp.abs(z)))
        rows = i * TM + lax.broadcasted_iota(jnp.int32, (TM, BLK), 0)
        ls = jnp.where(rows >= PAD_ROWS, ls, 0.0)
        r = lax.broadcasted_iota(jnp.int32, (TM, TM), 0)
        c = lax.broadcasted_iota(jnp.int32, (TM, TM), 1)
        tri = (c <= r).astype(BF16)
        cum = _tri_matmul(tri, ls) + carry[...]
        cum_ref[...] = cum
        carry[...] = cum[TM - 1:TM, :]

    return pl.pallas_call(
        body, grid=(NT,),
        in_specs=[pl.BlockSpec((TM, BLK), lambda i: (i, 0)), pl.BlockSpec((1, BLK), lambda i: (0, 0))],
        out_specs=pl.BlockSpec((TM, BLK), lambda i: (i, 0)),
        out_shape=SDS((LP, BLK), F32),
        scratch_shapes=[pltpu.VMEM((1, BLK), F32)],
        compiler_params=_cparams(("arbitrary",)), name="forget_cumsum")(f, b_forget_p)


def _forget_cumsum_bwd(dcum, f, b_forget_p):
    def body(d_ref, f_ref, b_ref, df_ref, db_ref, carry):
        i = pl.program_id(0)

        @pl.when(i == 0)
        def _():
            carry[...] = jnp.zeros_like(carry)
            db_ref[...] = jnp.zeros_like(db_ref)

        blk = NT - 1 - i
        r = lax.broadcasted_iota(jnp.int32, (TM, TM), 0)
        c = lax.broadcasted_iota(jnp.int32, (TM, TM), 1)
        tri = (c >= r).astype(BF16)
        d = d_ref[...]
        dls = _tri_matmul(tri, d) + carry[...]
        carry[...] = dls[0:1, :]
        z = f_ref[...] + b_ref[...]
        rows = blk * TM + lax.broadcasted_iota(jnp.int32, (TM, BLK), 0)
        df = jnp.where(rows >= PAD_ROWS, dls / (1.0 + jnp.exp(z)), 0.0)
        df_ref[...] = df
        db_ref[...] += jnp.sum(df, axis=0, keepdims=True)

    rev = pl.BlockSpec((TM, BLK), lambda i: (NT - 1 - i, 0))
    vec = pl.BlockSpec((1, BLK), lambda i: (0, 0))
    return pl.pallas_call(
        body, grid=(NT,),
        in_specs=[rev, rev, vec],
        out_specs=[rev, vec],
        out_shape=[SDS((LP, BLK), F32), SDS((1, BLK), F32)],
        scratch_shapes=[pltpu.VMEM((1, BLK), F32)],
        compiler_params=_cparams(("arbitrary",)), name="forget_cumsum_bwd")(dcum, f, b_forget_p)


def _lane_half(rows):
    return lax.broadcasted_iota(jnp.int32, (rows, BLK), 1) // HALF


def _fox_valid(qi, kj):
    qrow = qi * TM + lax.broadcasted_iota(jnp.int32, (TM, TM), 0)
    krow = kj * TM + lax.broadcasted_iota(jnp.int32, (TM, TM), 1)
    return (krow <= qrow) & ((krow >= PAD_ROWS) | (qrow < PAD_ROWS))


class _Rider:
    def __init__(self, operands, out_shapes, sem_counts, first, middle, last):
        self.operands, self.out_shapes, self.sem_counts = list(operands), list(out_shapes), list(sem_counts)
        self.first, self.middle, self.last = first, middle, last

    def scratch(self):
        return [pltpu.SemaphoreType.DMA((k,)) for k in self.sem_counts]

    def split(self, refs, n_in, n_out, n_scratch):
        a, b = len(self.operands), len(self.out_shapes)
        ins, mine_in = refs[:n_in], refs[n_in:n_in + a]
        outs, mine_out = refs[n_in + a:n_in + a + n_out], refs[n_in + a + n_out:n_in + a + n_out + b]
        rest = refs[n_in + a + n_out + b:]
        return ins, outs, rest[:n_scratch], (mine_in, mine_out, rest[n_scratch:])

    def at_steps(self, mine, is_first, is_middle, is_last):
        for cond, fn in ((is_first, self.first), (is_middle, self.middle), (is_last, self.last)):
            pl.when(cond)(lambda fn=fn: fn(*mine))


HBM_SPEC = pl.BlockSpec(memory_space=pltpu.HBM)


N_AUG = 4
QCH = 128
KSUB = 384


def _fox_prep(proj, cum):
    def body(q_ref, k_ref, v_ref, c_ref, qa_ref, ka_ref, vt_ref):
        half = _lane_half(TM)
        lane = lax.broadcasted_iota(jnp.int32, (TM, BLK), 1)
        for pp in range(4):
            cols = slice(pp * BLK, (pp + 1) * BLK)
            qs = q_ref[:, cols].astype(F32) * SCALE
            kp = k_ref[:, cols].astype(F32)
            vp = v_ref[:, cols]
            vt_ref[cols, :] = vp.astype(F32).T.astype(BF16)
            for e in range(2):
                h = 2 * pp + e
                a = (1 - e) * HALF
                blk = slice(h * BLK, (h + 1) * BLK)
                hi, mid, lo = _split3(-c_ref[:, h:h + 1])
                q_aug = jnp.where(half == e, qs, jnp.where((lane >= a) & (lane < a + 3), 1.0, 0.0))
                k_aug = jnp.where(half == e, kp, jnp.where(
                    lane == a, hi.astype(F32), jnp.where(lane == a + 1, mid.astype(F32), jnp.where(
                        lane == a + 2, lo.astype(F32), jnp.where(lane == a + 3, 1.0, 0.0)))))
                qa_ref[:, blk] = q_aug.astype(BF16)
                ka_ref[:, blk] = k_aug.astype(BF16)

    row = lambda blk: pl.BlockSpec((TM, 512), lambda i: (i, blk))
    wide = pl.BlockSpec((TM, 1024), lambda i: (i, 0))
    return pl.pallas_call(
        body, grid=(NT,),
        in_specs=[row(QB), row(KB), row(VB), pl.BlockSpec((TM, BLK), lambda i: (i, 0))],
        out_specs=[wide, wide, pl.BlockSpec((512, TM), lambda i: (0, i))],
        out_shape=[SDS((LP, 1024), BF16)] * 2 + [SDS((512, LP), BF16)],
        compiler_params=_cparams(("parallel",)), name="fox_prep")(proj, proj, proj, cum)


def _over_keys(reduce, x):
    slabs = x.reshape(x.shape[0] // HALF, HALF, x.shape[1])
    return reduce(reduce(slabs, axis=0), axis=0, keepdims=True)


def _fox_valid_t(qi, kj, c, r):
    krow = kj * TM + r * KSUB + lax.broadcasted_iota(jnp.int32, (KSUB, QCH), 0)
    qrow = qi * TM + c * QCH + lax.broadcasted_iota(jnp.int32, (KSUB, QCH), 1)
    return (krow <= qrow) & ((krow >= PAD_ROWS) | (qrow < PAD_ROWS))


def _fox_fwd(q_aug, k_aug, v_t, rider):
    pairs = [(qi, kj) for qi in range(NT) for kj in range(qi + 1)]
    n_pairs = len(pairs)

    def body(qi_ref, kj_ref, *refs):
        (q_ref, k_ref, vt_ref), (o_ref, lse_ref), (m_s, l_s, acc_s), mine = rider.split(refs, 3, 2, 3)
        n = pl.program_id(0)
        qi = qi_ref[n]
        kj = kj_ref[n]
        rider.at_steps(mine, n == 0, n == n_pairs // 2, n == n_pairs - 1)

        @pl.when(kj == 0)
        def _():
            m_s[...] = jnp.full_like(m_s, NEG)
            l_s[...] = jnp.zeros_like(l_s)
            acc_s[...] = jnp.zeros_like(acc_s)

        def tile(masked):
            steps = [(h, c, r) for h in range(N_HEADS) for c in range(TM // QCH) for r in range(TM // KSUB)]

            def scores(h, c, r):
                blk = slice(h * BLK, (h + 1) * BLK)
                return lax.dot_general(k_ref[r * KSUB:(r + 1) * KSUB, blk], q_ref[c * QCH:(c + 1) * QCH, blk],
                                       NT_DIMS, preferred_element_type=F32)

            ahead = scores(*steps[0])
            for n, (h, c, r) in enumerate(steps):
                s_t = ahead
                if n + 1 < len(steps):
                    ahead = scores(*steps[n + 1])
                cs = slice(c * QCH, (c + 1) * QCH)
                if masked:
                    s_t = jnp.where(_fox_valid_t(qi, kj, c, r), s_t, NEG)
                m_prev = m_s[h, :, cs]
                m_new = jnp.maximum(m_prev, _over_keys(jnp.max, s_t))
                p_t = jnp.exp(s_t - m_new)
                alpha = jnp.exp(m_prev - m_new)
                l_s[h, :, cs] = alpha * l_s[h, :, cs] + _over_keys(jnp.sum, p_t)
                m_s[h, :, cs] = m_new
                vt = vt_ref[h * HALF:(h + 1) * HALF, r * KSUB:(r + 1) * KSUB]
                acc_s[h, :, cs] = acc_s[h, :, cs] * alpha + jnp.dot(vt, p_t.astype(BF16),
                                                                    preferred_element_type=F32)

        @pl.when((kj < qi) & (kj > 0))
        def _():
            tile(False)

        @pl.when((kj == qi) | (kj == 0))
        def _():
            tile(True)

        @pl.when(kj == qi)
        def _():
            for pp in range(4):
                both = jnp.concatenate([acc_s[2 * pp] * (1.0 / l_s[2 * pp]),
                                        acc_s[2 * pp + 1] * (1.0 / l_s[2 * pp + 1])], axis=0)
                o_ref[:, pp * BLK:(pp + 1) * BLK] = both.T.astype(BF16)
            for h in range(N_HEADS):
                lse_ref[h] = m_s[h] + jnp.log(l_s[h])

    grid_spec = pltpu.PrefetchScalarGridSpec(
        num_scalar_prefetch=2, grid=(n_pairs,),
        in_specs=[pl.BlockSpec((TM, 1024), lambda n, qi, kj: (qi[n], 0)),
                  pl.BlockSpec((TM, 1024), lambda n, qi, kj: (kj[n], 0)),
                  pl.BlockSpec((512, TM), lambda n, qi, kj: (0, kj[n]))] + [HBM_SPEC] * len(rider.operands),
        out_specs=[pl.BlockSpec((TM, 512), lambda n, qi, kj: (qi[n], 0)),
                   pl.BlockSpec((N_HEADS, 1, TM), lambda n, qi, kj: (0, 0, qi[n]))]
        + [HBM_SPEC] * len(rider.out_shapes),
        scratch_shapes=[pltpu.VMEM((N_HEADS, 1, TM), F32), pltpu.VMEM((N_HEADS, 1, TM), F32),
                        pltpu.VMEM((N_HEADS, HALF, TM), F32)] + rider.scratch())
    o_b, lse, *carried = pl.pallas_call(
        body, grid_spec=grid_spec,
        out_shape=[SDS((LP, 512), BF16), SDS((N_HEADS, 1, LP), F32)] + rider.out_shapes,
        compiler_params=_cparams(("arbitrary",)), name="fox_fwd",
    )(jnp.asarray([p[0] for p in pairs], jnp.int32), jnp.asarray([p[1] for p in pairs], jnp.int32),
      q_aug, k_aug, v_t, *rider.operands)
    return o_b, lse, carried


def _fox_bwd(proj, o_b, dmix, lse, ck_t, rider):
    pairs = [(kj, qi) for kj in range(NT) for qi in range(kj, NT)]
    n_pairs = len(pairs)

    def body(kj_ref, qi_ref, *refs):
        ((q_ref, k_ref, v_ref, o_ref, do_ref, lse_ref, ck_ref), (dq_ref, dk_ref, dv_ref, dck_ref, dcq_ref),
         (dk_s, dv_s, dck_s), mine) = rider.split(refs, 7, 5, 3)
        n = pl.program_id(0)
        kj = kj_ref[n]
        qi = qi_ref[n]
        rider.at_steps(mine, n == 0, n == n_pairs // 2, n == n_pairs - 1)

        @pl.when(n == 0)
        def _():
            dq_ref[...] = jnp.zeros_like(dq_ref)
            dcq_ref[...] = jnp.zeros_like(dcq_ref)

        @pl.when(qi == kj)
        def _():
            dk_s[...] = jnp.zeros_like(dk_s)
            dv_s[...] = jnp.zeros_like(dv_s)
            dck_s[...] = jnp.zeros_like(dck_s)

        def tile(masked):
            valid = _fox_valid(qi, kj) if masked else None
            half = _lane_half(TM)
            q0 = pl.multiple_of(qi * TM, TM)
            lane = lax.broadcasted_iota(jnp.int32, (TM, BLK), 1)
            row_sums = jnp.zeros((TM, BLK), F32)
            for pp in range(4):
                cols = slice(pp * BLK, (pp + 1) * BLK)
                qs = (q_ref[:, cols].astype(F32) * SCALE).astype(BF16)
                kp = k_ref[:, cols]
                vp = v_ref[:, cols]
                dop = do_ref[:, cols]
                prod = dop.astype(F32) * o_ref[:, cols].astype(F32)
                d0 = jnp.sum(jnp.where(half == 0, prod, 0.0), axis=1, keepdims=True)
                d1 = jnp.sum(prod, axis=1, keepdims=True) - d0
                dq = jnp.zeros((TM, BLK), F32)
                dks, dvs = [], []
                for e in range(2):
                    h = 2 * pp + e
                    ke = jnp.where(half == e, kp, jnp.zeros_like(kp))
                    ve = jnp.where(half == e, vp, jnp.zeros_like(vp))
                    t = lax.dot_general(qs, ke, NT_DIMS, preferred_element_type=F32) - ck_ref[h] - lse_ref[h]
                    if masked:
                        t = jnp.where(valid, t, NEG)
                    p = jnp.exp(t)
                    dp = lax.dot_general(dop, ve, NT_DIMS, preferred_element_type=F32)
                    ds = p * (dp - (d0 if e == 0 else d1))
                    dck_s[h] += jnp.sum(ds, axis=0, keepdims=True)
                    row_sums = jnp.where(lane == h, jnp.sum(ds, axis=1, keepdims=True), row_sums)
                    ds_b = ds.astype(BF16)
                    dq = dq + jnp.dot(ds_b, ke, preferred_element_type=F32)
                    dks.append(lax.dot_general(ds_b, qs, TN_DIMS, preferred_element_type=F32))
                    dvs.append(lax.dot_general(p.astype(BF16), dop, TN_DIMS, preferred_element_type=F32))
                dq_ref[pl.ds(q0, TM), cols] += dq
                dk_s[pp] += jnp.where(half == 0, dks[0], dks[1])
                dv_s[pp] += jnp.where(half == 0, dvs[0], dvs[1])
            dcq_ref[pl.ds(q0, TM), :] += row_sums

        @pl.when((qi > kj) & (kj > 0))
        def _():
            tile(False)

        @pl.when((qi == kj) | (kj == 0))
        def _():
            tile(True)

        @pl.when(qi == NT - 1)
        def _():
            for pp in range(4):
                cols = slice(pp * BLK, (pp + 1) * BLK)
                dk_ref[:, cols] = dk_s[pp].astype(BF16)
                dv_ref[:, cols] = dv_s[pp].astype(BF16)
            dck_ref[...] = dck_s[...]

    qrow = lambda blk: pl.BlockSpec((TM, 512), lambda n, kj, qi: (qi[n], blk))
    krow = lambda blk: pl.BlockSpec((TM, 512), lambda n, kj, qi: (kj[n], blk))
    grid_spec = pltpu.PrefetchScalarGridSpec(
        num_scalar_prefetch=2, grid=(n_pairs,),
        in_specs=[qrow(QB), krow(KB), krow(VB), qrow(0), qrow(1),
                  pl.BlockSpec((N_HEADS, TM, 1), lambda n, kj, qi: (0, qi[n], 0)),
                  pl.BlockSpec((N_HEADS, 1, TM), lambda n, kj, qi: (0, 0, kj[n]))] + [HBM_SPEC] * len(rider.operands),
        out_specs=[pl.BlockSpec((LP, 512), lambda n, kj, qi: (0, 0)),
                   pl.BlockSpec((TM, 512), lambda n, kj, qi: (kj[n], 0)),
                   pl.BlockSpec((TM, 512), lambda n, kj, qi: (kj[n], 0)),
                   pl.BlockSpec((N_HEADS, 1, TM), lambda n, kj, qi: (0, 0, kj[n])),
                   pl.BlockSpec((LP, BLK), lambda n, kj, qi: (0, 0))] + [HBM_SPEC] * len(rider.out_shapes),
        scratch_shapes=[pltpu.VMEM((4, TM, BLK), F32), pltpu.VMEM((4, TM, BLK), F32),
                        pltpu.VMEM((N_HEADS, 1, TM), F32)] + rider.scratch())
    dq, dk, dv, dck, dcq, *carried = pl.pallas_call(
        body, grid_spec=grid_spec,
        out_shape=[SDS((LP, 512), F32), SDS((LP, 512), BF16), SDS((LP, 512), BF16), SDS((N_HEADS, 1, LP), F32),
                   SDS((LP, BLK), F32)] + rider.out_shapes,
        compiler_params=_cparams(("arbitrary",)), name="fox_bwd",
    )(jnp.asarray([p[0] for p in pairs], jnp.int32), jnp.asarray([p[1] for p in pairs], jnp.int32),
      proj, proj, proj, o_b, dmix, lse, ck_t, *rider.operands)
    return dq, dk, dv, dck, dcq, carried


N_SEG = 3
N_KEY = N_SEG * BLK
GROUP = 4
QW = GROUP * BLK


def _bucket_tables_t():
    return np.ascontiguousarray(_bucket_tables().transpose(0, 2, 1))


def _stack_heads(ref, g, scale):
    half = _lane_half(BLK)
    out = []
    for pair in range(2):
        x = ref[:, (2 * g + pair) * BLK:(2 * g + pair + 1) * BLK].astype(F32) * scale
        swapped = pltpu.roll(x, HALF, 1)
        for e in range(2):
            out.append(jnp.where(half == g, x if e == g else swapped, 0.0).astype(BF16))
    return jnp.concatenate(out, axis=0)


def _unstack_heads(x_t, g, ref, scale):
    for pair in range(2):
        both = jnp.concatenate([x_t[:, (2 * pair) * BLK:(2 * pair + 1) * BLK],
                                x_t[:, (2 * pair + 1) * BLK:(2 * pair + 2) * BLK]], axis=0)
        ref[:, (2 * g + pair) * BLK:(2 * g + pair + 1) * BLK] = (both.T * scale).astype(ref.dtype)


def _swa_tables(tab_ref, sink_ref, bkt_ref, tbl, sink_row):
    kk = lax.broadcasted_iota(jnp.int32, (BLK, BLK), 0)
    qq = lax.broadcasted_iota(jnp.int32, (BLK, BLK), 1)
    neg = jnp.full((BLK, BLK), NEG, F32)
    lane = lax.broadcasted_iota(jnp.int32, (1, QW), 1) // BLK
    for g in range(2):
        row = jnp.zeros((1, QW), F32)
        for hh in range(GROUP):
            h = GROUP * g + hh
            cols = slice(hh * BLK, (hh + 1) * BLK)
            row = jnp.where(lane == hh, sink_ref[0, h], row)

            def step(b, carry, h=h):
                t = tab_ref[b, h]
                return jnp.where(bkt_ref[0] == b, t, carry[0]), jnp.where(bkt_ref[1] == b, t, carry[1])
            zero = jnp.zeros((BLK, BLK), F32)
            cur, prev = lax.fori_loop(0, N_BUCKETS, step, (zero, zero))
            far = jnp.full((BLK, BLK), tab_ref[N_BUCKETS - 1, h], F32)
            causal = jnp.where(kk <= qq, cur, neg)
            segments = [
                (neg, neg, jnp.where(kk >= PAD_ROWS, causal, neg)),
                (jnp.where(kk >= PAD_ROWS, prev, neg), neg, causal),
                (jnp.where(kk >= PAD_ROWS, far, neg), jnp.where(kk > qq, prev, neg), causal)]
            for case in range(3):
                for seg in range(N_SEG):
                    tbl[case, g, seg * BLK:(seg + 1) * BLK, cols] = segments[case][seg]
        sink_row[g] = row


def _swa_prep(proj):
    rows = LP // 3

    def body(k_ref, v_ref, kt_ref, vt_ref):
        kt_ref[...] = k_ref[...].astype(F32).T.astype(BF16)
        vt_ref[...] = v_ref[...].astype(F32).T.astype(BF16)

    col = pl.BlockSpec((BLK, rows), lambda i: (0, i))
    return pl.pallas_call(
        body, grid=(3,),
        in_specs=[pl.BlockSpec((rows, BLK), lambda i: (i, KA)), pl.BlockSpec((rows, BLK), lambda i: (i, VA))],
        out_specs=[col, col], out_shape=[SDS((BLK, LP), BF16)] * 2,
        compiler_params=_cparams(("parallel",)), name="swa_prep")(proj, proj)


def _seg_specs(rows_major, col):
    idx = [lambda i: 0, lambda i: jnp.maximum(i - 1, 0), lambda i: i]
    if rows_major:
        return [pl.BlockSpec((BLK, BLK), lambda i, f=f: (f(i), col)) for f in idx]
    return [pl.BlockSpec((BLK, BLK), lambda i, f=f: (0, f(i))) for f in idx]


def _swa_fwd(proj, vt_a, rel_bias, sinks, bkt_t):
    def body(tab_ref, sink_ref, bkt_ref, q_ref, km_ref, kp_ref, kc_ref, vm_ref, vp_ref, vc_ref,
             o_ref, lse_ref, tbl, sink_row):
        i = pl.program_id(0)

        @pl.when(i == 0)
        def _():
            _swa_tables(tab_ref, sink_ref, bkt_ref, tbl, sink_row)

        case = jnp.minimum(i, 2)
        k_cat = jnp.concatenate([km_ref[...], kp_ref[...], kc_ref[...]], axis=0)
        vt_cat = jnp.concatenate([vm_ref[...], vp_ref[...], vc_ref[...]], axis=1)
        for g in range(2):
            s_t = lax.dot_general(k_cat, _stack_heads(q_ref, g, SCALE), NT_DIMS,
                                  preferred_element_type=F32) + tbl[case, g]
            sink = sink_row[g]
            m = jnp.maximum(_over_keys(jnp.max, s_t), sink)
            p_t = jnp.exp(s_t - m)
            l = _over_keys(jnp.sum, p_t) + jnp.exp(sink - m)
            o_t = jnp.dot(vt_cat[g * HALF:(g + 1) * HALF, :], p_t.astype(BF16), preferred_element_type=F32)
            _unstack_heads(o_t * (1.0 / l), g, o_ref, 1.0)
            lse = m + jnp.log(l)
            for hh in range(GROUP):
                lse_ref[GROUP * g + hh] = lse[:, hh * BLK:(hh + 1) * BLK]

    smem = pl.BlockSpec(memory_space=pltpu.SMEM)
    return pl.pallas_call(
        body, grid=(NBLK,),
        in_specs=[smem, smem, pl.BlockSpec((2, BLK, BLK), lambda i: (0, 0, 0)),
                  pl.BlockSpec((BLK, 512), lambda i: (i, QA))] + _seg_specs(True, KA) + _seg_specs(False, 0),
        out_specs=[pl.BlockSpec((BLK, 512), lambda i: (i, 0)),
                   pl.BlockSpec((N_HEADS, 1, BLK), lambda i: (0, 0, i))],
        out_shape=[SDS((LP, 512), BF16), SDS((N_HEADS, 1, LP), F32)],
        scratch_shapes=[pltpu.VMEM((3, 2, N_KEY, QW), F32), pltpu.VMEM((2, 1, QW), F32)],
        compiler_params=_cparams(("arbitrary",)), name="swa_fwd",
    )(rel_bias, sinks, bkt_t, proj, proj, proj, proj, vt_a, vt_a, vt_a)


def _swa_bwd(proj, kt_a, o_a, dmix, lse, rel_bias, sinks, bkt_t):
    def body(tab_ref, sink_ref, bkt_ref, q_ref, km_ref, kp_ref, kc_ref, vm_ref, vp_ref, vc_ref,
             tm_ref, tp_ref, tc_ref, o_ref, do_ref, lse_ref,
             dq_ref, dk_ref, dv_ref, dbias_ref, dsink_ref, tbl, sink_row, acc, dsk):
        i = pl.program_id(0)

        @pl.when(i == 0)
        def _():
            _swa_tables(tab_ref, sink_ref, bkt_ref, tbl, sink_row)
            dk_ref[...] = jnp.zeros_like(dk_ref)
            dv_ref[...] = jnp.zeros_like(dv_ref)
            acc[...] = jnp.zeros_like(acc)
            dsk[...] = jnp.zeros_like(dsk)

        case = jnp.minimum(i, 2)
        first = jnp.full((BLK, QW), i, jnp.int32) == 1
        k_cat = jnp.concatenate([km_ref[...], kp_ref[...], kc_ref[...]], axis=0)
        v_cat = jnp.concatenate([vm_ref[...], vp_ref[...], vc_ref[...]], axis=0)
        kt_cat = jnp.concatenate([tm_ref[...], tp_ref[...], tc_ref[...]], axis=1)
        dk_cat = jnp.zeros((N_KEY, BLK), F32)
        dv_cat = jnp.zeros((N_KEY, BLK), F32)
        for g in range(2):
            d_parts = []
            for pair in range(2):
                cols = slice((2 * g + pair) * BLK, (2 * g + pair + 1) * BLK)
                prod_t = (do_ref[:, cols].astype(F32) * o_ref[:, cols].astype(F32)).T
                d_parts += [jnp.sum(prod_t[:HALF], axis=0, keepdims=True),
                            jnp.sum(prod_t[HALF:], axis=0, keepdims=True)]
            d_row = jnp.concatenate(d_parts, axis=1)
            lse_row = jnp.concatenate([lse_ref[GROUP * g + hh] for hh in range(GROUP)], axis=1)
            q_st = _stack_heads(q_ref, g, SCALE)
            do_st = _stack_heads(do_ref, g, 1.0)
            s_t = lax.dot_general(k_cat, q_st, NT_DIMS, preferred_element_type=F32) + tbl[case, g]
            p_t = jnp.exp(s_t - lse_row)
            dp_t = lax.dot_general(v_cat, do_st, NT_DIMS, preferred_element_type=F32)
            ds_t = p_t * (dp_t - d_row)
            dsk[g] += -jnp.exp(sink_row[g] - lse_row) * d_row
            acc[g, 0:BLK] += jnp.where(first, 0.0, ds_t[0:BLK])
            acc[g, BLK:2 * BLK] += jnp.where(first, ds_t[0:BLK], ds_t[BLK:2 * BLK])
            acc[g, 2 * BLK:N_KEY] += ds_t[2 * BLK:N_KEY]
            ds_b = ds_t.astype(BF16)
            dk_cat = dk_cat + jnp.dot(ds_b, q_st, preferred_element_type=F32)
            dv_cat = dv_cat + jnp.dot(p_t.astype(BF16), do_st, preferred_element_type=F32)
            dq_t = jnp.dot(kt_cat[g * HALF:(g + 1) * HALF, :], ds_b, preferred_element_type=F32)
            _unstack_heads(dq_t, g, dq_ref, SCALE)

        prev0 = pl.multiple_of(jnp.maximum(i - 1, 0) * BLK, BLK)
        cur0 = pl.multiple_of(i * BLK, BLK)
        for ref, cat in ((dk_ref, dk_cat), (dv_ref, dv_cat)):
            ref[0:BLK, :] += cat[0:BLK]
            ref[pl.ds(prev0, BLK), :] += cat[BLK:2 * BLK]
            ref[pl.ds(cur0, BLK), :] += cat[2 * BLK:N_KEY]

        @pl.when(i == NBLK - 1)
        def _():
            lane = lax.broadcasted_iota(jnp.int32, (1, BLK), 1)

            def per_bucket(b, carry):
                row = jnp.zeros((1, BLK), F32)
                for h in range(N_HEADS):
                    g, cols = h // GROUP, slice((h % GROUP) * BLK, (h % GROUP + 1) * BLK)
                    val = (jnp.sum(jnp.where(bkt_ref[0] == b, acc[g, 2 * BLK:N_KEY, cols], 0.0), keepdims=True)
                           + jnp.sum(jnp.where(bkt_ref[1] == b, acc[g, BLK:2 * BLK, cols], 0.0), keepdims=True))
                    row = jnp.where(lane == h, val, row)
                dbias_ref[pl.ds(b, 1), :] = row
                return carry

            lax.fori_loop(0, N_BUCKETS, per_bucket, 0)
            far = jnp.zeros((1, BLK), F32)
            dsr = jnp.zeros((1, BLK), F32)
            for h in range(N_HEADS):
                g, cols = h // GROUP, slice((h % GROUP) * BLK, (h % GROUP + 1) * BLK)
                far = jnp.where(lane == h, jnp.sum(acc[g, 0:BLK, cols], keepdims=True), far)
                dsr = jnp.where(lane == h, jnp.sum(dsk[g, :, cols], keepdims=True), dsr)
            dbias_ref[N_BUCKETS - 1:N_BUCKETS, :] += far
            dsink_ref[...] = dsr

    smem = pl.BlockSpec(memory_space=pltpu.SMEM)
    blk512 = lambda col: pl.BlockSpec((BLK, 512), lambda i: (i, col))
    full = lambda r, c: pl.BlockSpec((r, c), lambda i: (0, 0))
    return pl.pallas_call(
        body, grid=(NBLK,),
        in_specs=[smem, smem, pl.BlockSpec((2, BLK, BLK), lambda i: (0, 0, 0)), blk512(QA)]
        + _seg_specs(True, KA) + _seg_specs(True, VA) + _seg_specs(False, 0)
        + [blk512(0), blk512(0), pl.BlockSpec((N_HEADS, 1, BLK), lambda i: (0, 0, i))],
        out_specs=[blk512(0), full(LP, BLK), full(LP, BLK), full(N_BUCKETS, BLK), full(1, BLK)],
        out_shape=[SDS((LP, 512), BF16), SDS((LP, BLK), F32), SDS((LP, BLK), F32),
                   SDS((N_BUCKETS, BLK), F32), SDS((1, BLK), F32)],
        scratch_shapes=[pltpu.VMEM((3, 2, N_KEY, QW), F32), pltpu.VMEM((2, 1, QW), F32),
                        pltpu.VMEM((2, N_KEY, QW), F32), pltpu.VMEM((2, 1, QW), F32)],
        compiler_params=_cparams(("arbitrary",)), name="swa_bwd",
    )(rel_bias, sinks, bkt_t, proj, proj, proj, proj, proj, proj, proj, kt_a, kt_a, kt_a, o_a, dmix, lse)


def _local_step(x, tgt, meta, rel_bias, g_pre_mix, g_post_mix, g_pre_ffn, g_post_ffn, b_forget, sinks,
                w_in_b, w_out_b, ffn_rider, ffn_weights, ffn_grads, early_grads):
    bkt_t = jnp.asarray(_bucket_tables_t())
    h0 = jnp.concatenate([jnp.zeros((PAD_ROWS, D_MODEL), F32), meta, x], axis=0)
    b_p = jnp.pad(b_forget, ((0, 0), (0, BLK - N_HEADS)))

    hn1, proj, f = _pre_mix(h0, g_pre_mix, w_in_b)
    kt_a, vt_a = _swa_prep(proj)
    o_a, lse_a = _swa_fwd(proj, vt_a, rel_bias, sinks, bkt_t)
    cum = _forget_cumsum(f, b_p)
    ck_t = cum[:, :N_HEADS].T.reshape(N_HEADS, 1, LP)
    q_aug, k_aug, v_t = _fox_prep(proj, cum)
    o_b, lse_row, carried = _fox_fwd(q_aug, k_aug, v_t, ffn_rider)
    lse_b = lse_row.reshape(N_HEADS, LP, 1)
    w_gu_b, w_dn_b = ffn_weights(carried)
    a, h1, hn2 = _attn_out(o_a, o_b, w_out_b, h0, g_post_mix, g_pre_ffn)
    g, u, act = _ffn_up(hn2, w_gu_b)
    dff, dy, loss_blk, dg_post_ffn = _ffn_down_loss(act, w_dn_b, h1, tgt, g_post_ffn)

    dw_dn = _mm_tn([act], dff, FF_T, "dw_down")
    dg, du = _ffn_down_bwd(dff, w_dn_b, g, u)
    dw_gu = _dw_gate_up(hn2, dg, du)
    dh1, da, dg_pre_ffn, dg_post_mix, swapped = _ffn_up_bwd(dg, du, w_gu_b, h1, a, dy, g_pre_ffn, g_post_mix,
                                                           ffn_grads(dw_gu, dw_dn))
    dw_out = _mm_tn([o_a, o_b], da, D_MODEL, "dw_out")
    dmix = _attn_out_bwd(da, w_out_b)
    dq_b, dk_b, dv_b, dck, dcq, landed = _fox_bwd(proj, o_b, dmix, lse_b, ck_t, early_grads(swapped, dw_out))
    dq_a, dk_a, dv_a, dbias, dsink = _swa_bwd(proj, kt_a, o_a, dmix, lse_a, rel_bias, sinks, bkt_t)
    dcum = dcq - jnp.pad(dck.reshape(N_HEADS, LP).T, ((0, 0), (0, BLK - N_HEADS)))
    df, db = _forget_cumsum_bwd(dcum, f, b_p)
    dproj, dh0, dg_pre_mix = _pre_mix_bwd(dq_a, dq_b, dk_b, dv_b, dk_a, dv_a, df, w_in_b, h0, dh1, g_pre_mix)
    dw_in = _mm_tn([hn1], dproj, D_MODEL, "dw_in")

    return dict(loss=loss_blk[0, 0], grad_x=dh0[ROW0:], meta=dh0[PAD_ROWS:ROW0],
                rel_bias=dbias[:, :N_HEADS], ln_pre_mix=dg_pre_mix, ln_post_mix=dg_post_mix,
                ln_pre_ffn=dg_pre_ffn, ln_post_ffn=dg_post_ffn, b_forget=db[:, :N_HEADS],
                sinks=dsink[:, :N_HEADS], w_in=dw_in, w_out=dw_out, w_gate_up=dw_gu, w_down=dw_dn,
                landed=landed)


N_SMALL = 24
LOSS_ROW = 6


def _place():
    x, y, c = lax.axis_index("x"), lax.axis_index("y"), lax.axis_index("c")
    return x, y, c, [(1 - x, y), (x, 1 - y), (1 - x, 1 - y)]


def _run_alone(rider, name):
    a, b = len(rider.operands), len(rider.out_shapes)

    def body(*refs):
        mine = (refs[:a], refs[a:a + b], refs[a + b:])
        rider.first(*mine)
        rider.middle(*mine)
        rider.last(*mine)

    return pl.pallas_call(body, in_specs=[HBM_SPEC] * a, out_specs=[HBM_SPEC] * b, out_shape=rider.out_shapes,
                          scratch_shapes=rider.scratch(), name=name)(*rider.operands)


def _gather_rider(shards, own_too):
    n = len(shards)

    def own_copies(ins, outs, sems):
        x, y, _, _ = _place()
        return [pltpu.make_async_copy(ins[a], outs[a].at[2 * x + y], sems[2].at[a]) for a in range(n)] if own_too else []

    def copies(ins, outs, sems):
        send_sems, recv_sems = sems[:2]
        x, y, c, others = _place()
        chip = 2 * x + y
        sibling = (x, y, 1 - c)

        def rc(a, k, src, dst, to):
            return pltpu.make_async_remote_copy(src_ref=src, dst_ref=dst, send_sem=send_sems.at[6 * a + k],
                                                recv_sem=recv_sems.at[6 * a + k], device_id=to, device_id_type=MESH)

        pairs = [(a, j, ox, oy) for a in range(n) for j, (ox, oy) in enumerate(others)]
        return dict(
            sent=lambda: [rc(a, j, ins[a].at[c], outs[a].at[chip, c], (ox, oy, c)) for a, j, ox, oy in pairs],
            landed=lambda: [rc(a, j, outs[a].at[2 * ox + oy, c], outs[a].at[2 * ox + oy, c], sibling)
                            for a, j, ox, oy in pairs],
            passed=lambda: [rc(a, 3 + j, outs[a].at[2 * ox + oy, c], outs[a].at[2 * ox + oy, c], sibling)
                            for a, j, ox, oy in pairs],
            arriving=lambda: [rc(a, 3 + j, outs[a].at[2 * ox + oy, 1 - c], outs[a].at[2 * ox + oy, 1 - c], sibling)
                              for a, j, ox, oy in pairs])

    def first(*mine):
        for cp in copies(*mine)["sent"]() + own_copies(*mine):
            cp.start()

    def middle(*mine):
        kinds = copies(*mine)
        for got, cp in zip(kinds["landed"](), kinds["passed"]()):
            got.wait_recv()
            cp.start()

    def last(*mine):
        kinds = copies(*mine)
        for cp in kinds["arriving"]():
            cp.wait_recv()
        for cp in kinds["sent"]() + kinds["passed"]():
            cp.wait_send()
        for cp in own_copies(*mine):
            cp.wait()

    return _Rider(shards, [SDS((4,) + s.shape, s.dtype) for s in shards], [6 * n, 6 * n] + [n] * own_too,
                  first, middle, last)


def _swap_rider(grads):
    n = len(grads)

    def copies(ins, outs, sems):
        x, y, c, _ = _place()
        return [pltpu.make_async_remote_copy(
            src_ref=ins[a].at[s, 1 - c], dst_ref=outs[a].at[s], send_sem=sems[0].at[4 * a + s],
            recv_sem=sems[1].at[4 * a + s], device_id=(x, y, 1 - c), device_id_type=MESH)
            for a in range(n) for s in range(4)]

    def first(*mine):
        for cp in copies(*mine):
            cp.start()

    def middle(*mine):
        pass

    def last(*mine):
        for cp in copies(*mine):
            cp.wait()

    return _Rider(grads, [SDS((4,) + g.shape[2:], g.dtype) for g in grads], [4 * n, 4 * n], first, middle, last)


def _pair_sum(g, got, c_arr, name):
    rh, cc = got.shape[1:]

    def body(c_ref, g_ref, p_ref, o_ref):
        o_ref[0] = (g_ref[0, 0] + p_ref[0]).astype(BF16)

    grid_spec = pltpu.PrefetchScalarGridSpec(
        num_scalar_prefetch=1, grid=(4,),
        in_specs=[pl.BlockSpec((1, 1, rh, cc), lambda s, c_ref: (s, c_ref[0], 0, 0)),
                  pl.BlockSpec((1, rh, cc), lambda s, c_ref: (s, 0, 0))],
        out_specs=pl.BlockSpec((1, rh, cc), lambda s, c_ref: (s, 0, 0)))
    return pl.pallas_call(body, grid_spec=grid_spec, out_shape=SDS((4, rh, cc), BF16),
                          compiler_params=_cparams(("parallel",)), name=name)(c_arr, g, got)


def _exchange_rider(parts, small=None):
    n = len(parts)

    def copies(ins, outs, sems):
        x, y, c, others = _place()
        out = [pltpu.make_async_remote_copy(
            src_ref=ins[a].at[2 * ox + oy], dst_ref=outs[a].at[j], send_sem=sems[0].at[3 * a + j],
            recv_sem=sems[1].at[3 * a + j], device_id=(ox, oy, c), device_id_type=MESH)
            for a in range(n) for j, (ox, oy) in enumerate(others)]
        own = []
        if small is not None:
            me = 4 * x + 2 * y + c
            peers = [(x, y, 1 - c)] + [(ox, oy, c) for ox, oy in others] + [(ox, oy, 1 - c) for ox, oy in others]
            out += [pltpu.make_async_remote_copy(
                src_ref=ins[n], dst_ref=outs[n].at[me], send_sem=sems[2].at[k], recv_sem=sems[3].at[k],
                device_id=peer, device_id_type=MESH) for k, peer in enumerate(peers)]
            own = [pltpu.make_async_copy(ins[n], outs[n].at[me], sems[4].at[0])]
        return out, own

    def first(*mine):
        out, own = copies(*mine)
        for cp in own + out:
            cp.start()

    def middle(*mine):
        pass

    def last(*mine):
        out, own = copies(*mine)
        for cp in out + own:
            cp.wait()

    shapes = [SDS((3,) + p.shape[1:], p.dtype) for p in parts]
    if small is None:
        return _Rider(parts, shapes, [3 * n, 3 * n], first, middle, last)
    return _Rider(parts + [small], shapes + [SDS((8,) + small.shape, small.dtype)], [3 * n, 3 * n, 7, 7, 1],
                  first, middle, last)


def _chip_sum(parts, landed, chip_arr, name):
    rh, cc = landed.shape[1:]
    tr = rh // 2

    def body(chip_ref, own_ref, p_ref, o_ref):
        o_ref[...] = ((own_ref[0].astype(F32) + p_ref[0].astype(F32)) + p_ref[1].astype(F32)) + p_ref[2].astype(F32)

    grid_spec = pltpu.PrefetchScalarGridSpec(
        num_scalar_prefetch=1, grid=(2,),
        in_specs=[pl.BlockSpec((1, tr, cc), lambda i, chip_ref: (chip_ref[0], i, 0)),
                  pl.BlockSpec((3, tr, cc), lambda i, chip_ref: (0, i, 0))],
        out_specs=pl.BlockSpec((tr, cc), lambda i, chip_ref: (i, 0)))
    return pl.pallas_call(body, grid_spec=grid_spec, out_shape=SDS((rh, cc), F32),
                          compiler_params=_cparams(("parallel",)), name=name)(chip_arr, parts, landed)


def _device_sum(p):
    def body(p_ref, o_ref):
        acc = p_ref[0]
        for k in range(1, 8):
            acc = acc + p_ref[k]
        o_ref[...] = acc

    return pl.pallas_call(body, out_shape=SDS(p.shape[1:], F32), name="small_sum")(p)


def _join_halves(halves):
    n = len(halves)

    def body(*refs):
        ins, outs = refs[:n], refs[n:2 * n]
        send_sems, recv_sems = refs[2 * n:]
        x, y, c, _ = _place()
        copies = [pltpu.make_async_remote_copy(
            src_ref=ins[a], dst_ref=outs[a], send_sem=send_sems.at[a], recv_sem=recv_sems.at[a],
            device_id=(x, y, 1 - c), device_id_type=MESH) for a in range(n)]
        for cp in copies:
            cp.start()
        for cp in copies:
            cp.wait()

    return pl.pallas_call(
        body, in_specs=[HBM_SPEC] * n, out_specs=[HBM_SPEC] * n,
        out_shape=[SDS(h.shape, h.dtype) for h in halves],
        scratch_shapes=[pltpu.SemaphoreType.DMA((n,)), pltpu.SemaphoreType.DMA((n,))],
        name="join_halves")(*halves)


def _adamw(w, g, m, v, name):
    rows, cols = w.shape
    tr = rows if rows <= 352 else (256 if rows % 256 == 0 else 352)

    def body(w_ref, g_ref, m_ref, v_ref, d_ref, nm_ref, nv_ref):
        gg = g_ref[...]
        nm = ADAM_B1 * m_ref[...] + (1.0 - ADAM_B1) * gg
        nv = ADAM_B2 * v_ref[...] + (1.0 - ADAM_B2) * (gg * gg)
        nm_ref[...] = nm
        nv_ref[...] = nv
        m_hat = nm / (1.0 - ADAM_B1 ** ADAM_STEP)
        v_hat = nv / (1.0 - ADAM_B2 ** ADAM_STEP)
        d_ref[...] = -ADAM_LR * (m_hat / (jnp.sqrt(v_hat) + ADAM_EPS) + ADAM_WD * w_ref[...])

    blk = pl.BlockSpec((tr, cols), lambda i: (i, 0))
    return pl.pallas_call(
        body, grid=(rows // tr,), in_specs=[blk] * 4, out_specs=[blk] * 3,
        out_shape=[SDS((rows, cols), F32)] * 3,
        compiler_params=_cparams(("parallel",)), name=name)(w, g, m, v)


def _adamw_halves(w, mine, theirs, m, v, c_arr, name):
    rows, cols = w.shape
    rh = rows // 2
    tr = rh if rh <= 352 else 256
    nh = rh // tr

    def body(c_ref, w_ref, mine_ref, theirs_ref, m_ref, v_ref, g_ref, d_ref, nm_ref, nv_ref):
        own = jnp.full((tr, cols), pl.program_id(0), jnp.int32) == c_ref[0]
        gg = jnp.where(own, mine_ref[...], theirs_ref[...])
        g_ref[...] = gg
        nm = ADAM_B1 * m_ref[...] + (1.0 - ADAM_B1) * gg
        nv = ADAM_B2 * v_ref[...] + (1.0 - ADAM_B2) * (gg * gg)
        nm_ref[...] = nm
        nv_ref[...] = nv
        m_hat = nm / (1.0 - ADAM_B1 ** ADAM_STEP)
        v_hat = nv / (1.0 - ADAM_B2 ** ADAM_STEP)
        d_ref[...] = -ADAM_LR * (m_hat / (jnp.sqrt(v_hat) + ADAM_EPS) + ADAM_WD * w_ref[...])

    whole = pl.BlockSpec((tr, cols), lambda hh, i, c_ref: (hh * nh + i, 0))
    part = pl.BlockSpec((tr, cols), lambda hh, i, c_ref: (i, 0))
    grid_spec = pltpu.PrefetchScalarGridSpec(
        num_scalar_prefetch=1, grid=(2, nh), in_specs=[whole, part, part, whole, whole], out_specs=[whole] * 4)
    return pl.pallas_call(body, grid_spec=grid_spec, out_shape=[SDS((rows, cols), F32)] * 4,
                          compiler_params=_cparams(("parallel", "parallel")), name=name)(c_arr, w, mine, theirs, m, v)


def _pack_small(pre_mix, post_mix, pre_ffn, post_ffn, rel_bias, b_forget, sinks):
    def at(row, v):
        return jnp.pad(v, ((row, 7 - row), (0, D_MODEL - v.shape[1])))
    return (at(0, pre_mix) + at(1, post_mix) + at(2, pre_ffn) + at(3, post_ffn)
            + at(4, rel_bias.reshape(1, N_BUCKETS * N_HEADS)) + at(5, jnp.concatenate([b_forget, sinks], axis=1)))


def _unpack_small(p):
    return dict(ln_pre_mix=p[0:1], ln_post_mix=p[1:2], ln_pre_ffn=p[2:3], ln_post_ffn=p[3:4],
                rel_bias=p[4, :N_BUCKETS * N_HEADS].reshape(N_BUCKETS, N_HEADS),
                b_forget=p[5:6, 0:N_HEADS], sinks=p[5:6, N_HEADS:2 * N_HEADS])


WEIGHTS = ("meta_tokens", "rel_bias", "ln_pre_mix", "ln_post_mix", "ln_pre_ffn", "ln_post_ffn",
           "w_in", "b_forget", "sinks", "w_out", "w_gate_up", "w_down")


def kernel(x, meta_tokens, rel_bias, ln_pre_mix, ln_post_mix, ln_pre_ffn, ln_post_ffn, w_in, b_forget, sinks, w_out, w_gate_up, w_down, loss_target, m_meta_tokens, m_rel_bias, m_ln_pre_mix, m_ln_post_mix, m_ln_pre_ffn, m_ln_post_ffn, m_w_in, m_b_forget, m_sinks, m_w_out, m_w_gate_up, m_w_down, v_meta_tokens, v_rel_bias, v_ln_pre_mix, v_ln_post_mix, v_ln_pre_ffn, v_ln_post_ffn, v_w_in, v_b_forget, v_sinks, v_w_out, v_w_gate_up, v_w_down):
    xi, yi, ci = lax.axis_index("x"), lax.axis_index("y"), lax.axis_index("c")
    chip = 2 * xi + yi
    c_arr = jnp.reshape(ci, (1,)).astype(jnp.int32)

    def halves(w, dtype):
        return w.astype(dtype).reshape(2, w.shape[0] // 2, w.shape[1])

    def with_own(gathered, shards):
        return [lax.dynamic_update_slice(got, own[None], (chip, 0, 0, 0)) for got, own in zip(gathered, shards)]

    shards = [halves(w_in[0], BF16), halves(w_out[0], BF16), halves(meta_tokens, F32)]
    gw_in, gw_out, g_meta = with_own(_run_alone(_gather_rider(shards, False), "gather_mixer_weights"), shards)
    ffn_shards = [halves(w_gate_up[0], BF16), halves(w_down[0], BF16)]

    def ffn_weights(carried):
        gw_gu, gw_dn = carried
        w_gu_b = gw_gu.reshape(4, D_MODEL, FF_T).transpose(1, 0, 2).reshape(D_MODEL, 2 * D_FF)
        return w_gu_b, gw_dn.reshape(D_FF, D_MODEL)

    early = {}

    def ffn_grads(dw_gu, dw_dn):
        early["grads"] = [dw_gu.reshape(4, 2, 512, FF_T), dw_dn.reshape(4, 2, 352, D_MODEL)]
        return _swap_rider(early["grads"])

    def early_grads(swapped, dw_out):
        g_out = dw_out.reshape(4, 2, 128, D_MODEL)
        grads = [g_out] + early["grads"]
        got = list(_run_alone(_swap_rider([g_out]), "swap_halves_out")) + list(swapped)
        early["parts"] = [_pair_sum(g, p, c_arr, "pair_sum_%d" % a) for a, (g, p) in enumerate(zip(grads, got))]
        return _exchange_rider(early["parts"])
    w_in_all = gw_in.reshape(4, D_MODEL, D_PROJ // 4).transpose(1, 0, 2).reshape(D_MODEL, D_PROJ)
    w_in_b = jnp.concatenate(
        [w_in_all[:, 0:512], w_in_all[:, 768:1280], w_in_all[:, 1280:1792], w_in_all[:, 1792:2304],
         w_in_all[:, 512:640], w_in_all[:, 640:768], w_in_all[:, 2304:2312],
         jnp.zeros((D_MODEL, D_PROJ_P - D_PROJ), BF16)], axis=1)
    meta_all = g_meta.reshape(4, N_META, D_MODEL // 4).transpose(1, 0, 2).reshape(N_META, D_MODEL)

    loc = _local_step(x[0], loss_target[0], meta_all, rel_bias, ln_pre_mix, ln_post_mix, ln_pre_ffn, ln_post_ffn,
                      b_forget, sinks, w_in_b, gw_out.reshape(D_MODEL, D_MODEL),
                      _gather_rider(ffn_shards, True), ffn_weights, ffn_grads, early_grads)

    n = loc["w_in"]
    dw_in = jnp.concatenate([n[:, 0:512], n[:, 2048:2176], n[:, 2176:2304], n[:, 512:1024], n[:, 1024:1536],
                             n[:, 1536:2048], n[:, 2304:2312]], axis=1)
    dw_in = dw_in.reshape(D_MODEL, 4, D_PROJ // 4).transpose(1, 0, 2).reshape(4, 2, 512, D_PROJ // 4)
    small = jnp.concatenate(
        [_pack_small(loc["ln_pre_mix"], loc["ln_post_mix"], loc["ln_pre_ffn"], loc["ln_post_ffn"],
                     loc["rel_bias"], loc["b_forget"], loc["sinks"])
         + jnp.pad(loc["loss"].reshape(1, 1), ((LOSS_ROW, 7 - LOSS_ROW), (0, D_MODEL - 1))), loc["meta"]], axis=0)

    (got_in,) = _run_alone(_swap_rider([dw_in]), "swap_halves_late")
    part_in = _pair_sum(dw_in, got_in, c_arr, "pair_sum_late")
    landed_in, small_all = _run_alone(_exchange_rider([part_in], small), "exchange_late")
    chip_arr = jnp.reshape(chip, (1,)).astype(jnp.int32)
    parts = [part_in] + early["parts"]
    landed = [landed_in] + loc["landed"]
    mine = [_chip_sum(p, l, chip_arr, "chip_sum_%d" % a) for a, (p, l) in enumerate(zip(parts, landed))]
    small_sum = _device_sum(small_all)
    theirs = _join_halves(mine)
    g_meta_tokens = lax.dynamic_slice(small_sum[8:N_SMALL], (0, chip * (D_MODEL // 4)), (N_META, D_MODEL // 4))
    g_small = small_sum[0:8]

    grad = _unpack_small(g_small)
    grad.update(meta_tokens=g_meta_tokens)
    delta, new_m, new_v = {}, {}, {}
    big = dict(w_in=(w_in, m_w_in, v_w_in), w_out=(w_out, m_w_out, v_w_out),
               w_gate_up=(w_gate_up, m_w_gate_up, v_w_gate_up), w_down=(w_down, m_w_down, v_w_down))
    for (name, (w, m, v)), g_mine, g_theirs in zip(big.items(), mine, theirs):
        g, d, nm, nv = _adamw_halves(w[0], g_mine, g_theirs, m[0], v[0], c_arr, "adamw_" + name)
        grad[name], delta[name], new_m[name], new_v[name] = g[None], d[None], nm[None], nv[None]
    delta["meta_tokens"], new_m["meta_tokens"], new_v["meta_tokens"] = _adamw(
        meta_tokens, g_meta_tokens, m_meta_tokens, v_meta_tokens, "adamw_meta")
    d, nm, nv = _adamw(
        _pack_small(ln_pre_mix, ln_post_mix, ln_pre_ffn, ln_post_ffn, rel_bias, b_forget, sinks), g_small,
        _pack_small(m_ln_pre_mix, m_ln_post_mix, m_ln_pre_ffn, m_ln_post_ffn, m_rel_bias, m_b_forget, m_sinks),
        _pack_small(v_ln_pre_mix, v_ln_post_mix, v_ln_pre_ffn, v_ln_post_ffn, v_rel_bias, v_b_forget, v_sinks),
        "adamw_small")
    delta.update(_unpack_small(d))
    new_m.update(_unpack_small(nm))
    new_v.update(_unpack_small(nv))

    loss = small_sum[LOSS_ROW, 0]
    return (loss,loc["grad_x"][None], *[grad[k] for k in WEIGHTS], *[delta[k] for k in WEIGHTS],
            *[new_m[k] for k in WEIGHTS], *[new_v[k] for k in WEIGHTS])
```

```python
import math

import numpy as np
import jax
import jax.numpy as jnp
from jax import lax
from jax.experimental import pallas as pl
from jax.experimental.pallas import tpu as pltpu

F32 = jnp.float32
BF16 = jnp.bfloat16
MESH = pl.DeviceIdType.MESH
SDS = jax.ShapeDtypeStruct

D_MODEL = 1024
SEQ = 4096
N_META = 16
N_HEADS = 8
HALF = 64
D_FF = 2816
N_BUCKETS = 32
EPS = 1e-6
NEG = -1e30
SCALE = 0.125
PAD_ROWS = 112
ROW0 = PAD_ROWS + N_META
LP = ROW0 + SEQ
BLK = 128
NBLK = LP // BLK
TM = 384
NT = LP // TM
TM_PURE = LP // 2
TM_MID = LP // 4
TM_EPI = LP // 6
TN = 256
D_PROJ = 2312
D_PROJ_P = 2432
D_QKV = 2304
FF_T = 1408
VMEM_LIMIT = 56 * 1024 * 1024

ADAM_LR = 0.001
ADAM_B1 = 0.9
ADAM_B2 = 0.999
ADAM_EPS = 1e-08
ADAM_WD = 0.01
ADAM_STEP = 10

QA, QB, KB, VB = 0, 1, 2, 3
KA, VA = 16, 17

NT_DIMS = (((1,), (1,)), ((), ()))
TN_DIMS = (((0,), (0,)), ((), ()))


def _cparams(sem):
    return pltpu.CompilerParams(dimension_semantics=sem, vmem_limit_bytes=VMEM_LIMIT)


def _t5_bucket_np(d):
    n = np.maximum(d, 0).astype(np.int32)
    nf = np.maximum(n, 1).astype(np.float32)
    large = 16 + (np.log(nf / np.float32(16)) / np.float32(math.log(8.0)) * np.float32(16)).astype(np.int32)
    large = np.minimum(large, N_BUCKETS - 1)
    return np.where(n < 16, n, large).astype(np.int32)


def _bucket_tables():
    qi = np.arange(BLK)[:, None]
    ki = np.arange(BLK)[None, :]
    return np.stack([_t5_bucket_np(qi - ki), _t5_bucket_np(qi - ki + BLK)])


def _rms(x):
    return lax.rsqrt(jnp.mean(x * x, axis=-1, keepdims=True) + EPS)


def _rms_bwd(n, r, gdy):
    return r * (gdy - n * jnp.mean(n * gdy, axis=-1, keepdims=True))


def _pre_mix(h0, gain, w_in_b):
    half = D_QKV // 2

    def body(h_ref, g_ref, w_ref, hn_ref, proj_ref, f_ref):
        x = h_ref[...]
        hn = (x * _rms(x) * g_ref[...]).astype(BF16)
        hn_ref[...] = hn
        proj_ref[:, :half] = jnp.dot(hn, w_ref[:, :half], preferred_element_type=F32).astype(BF16)
        p = jnp.dot(hn, w_ref[:, half:], preferred_element_type=F32)
        proj_ref[:, half:] = p[:, :half].astype(BF16)
        f_ref[...] = p[:, half:]

    return pl.pallas_call(
        body, grid=(LP // TM_MID,),
        in_specs=[pl.BlockSpec((TM_MID, D_MODEL), lambda i: (i, 0)),
                  pl.BlockSpec((1, D_MODEL), lambda i: (0, 0)),
                  pl.BlockSpec((D_MODEL, D_PROJ_P), lambda i: (0, 0))],
        out_specs=[pl.BlockSpec((TM_MID, D_MODEL), lambda i: (i, 0)),
                   pl.BlockSpec((TM_MID, D_QKV), lambda i: (i, 0)),
                   pl.BlockSpec((TM_MID, BLK), lambda i: (i, 0))],
        out_shape=[SDS((LP, D_MODEL), BF16), SDS((LP, D_QKV), BF16), SDS((LP, BLK), F32)],
        compiler_params=_cparams(("parallel",)), name="pre_mix")(h0, gain, w_in_b)


def _attn_out(o_a, o_b, w_out_b, h0, g_post, g_pre_ffn):
    def body(oa_ref, ob_ref, w_ref, h0_ref, gp_ref, gf_ref, a_ref, h1_ref, hn2_ref):
        a = (jnp.dot(oa_ref[...], w_ref[0:512, :], preferred_element_type=F32)
             + jnp.dot(ob_ref[...], w_ref[512:1024, :], preferred_element_type=F32))
        a_ref[...] = a
        h1 = h0_ref[...] + a * _rms(a) * gp_ref[...]
        h1_ref[...] = h1
        hn2_ref[...] = (h1 * _rms(h1) * gf_ref[...]).astype(BF16)

    row = lambda w: pl.BlockSpec((TM_EPI, w), lambda i: (i, 0))
    vec = pl.BlockSpec((1, D_MODEL), lambda i: (0, 0))
    return pl.pallas_call(
        body, grid=(LP // TM_EPI,),
        in_specs=[row(512), row(512), pl.BlockSpec((D_MODEL, D_MODEL), lambda i: (0, 0)), row(D_MODEL), vec, vec],
        out_specs=[row(D_MODEL), row(D_MODEL), row(D_MODEL)],
        out_shape=[SDS((LP, D_MODEL), F32), SDS((LP, D_MODEL), F32), SDS((LP, D_MODEL), BF16)],
        compiler_params=_cparams(("parallel",)), name="attn_out")(o_a, o_b, w_out_b, h0, g_post, g_pre_ffn)


def _ffn_up(hn2, w_gu_b):
    def body(x_ref, wg_ref, wu_ref, g_ref, u_ref, act_ref):
        x = x_ref[...]
        g = jnp.dot(x, wg_ref[...], preferred_element_type=F32)
        u = jnp.dot(x, wu_ref[...], preferred_element_type=F32)
        g_ref[...] = g.astype(BF16)
        u_ref[...] = u.astype(BF16)
        act_ref[...] = (g * (1.0 / (1.0 + jnp.exp(-g))) * u).astype(BF16)

    out = pl.BlockSpec((TM_PURE, TN), lambda i, j: (i, j))
    return pl.pallas_call(
        body, grid=(LP // TM_PURE, D_FF // TN),
        in_specs=[pl.BlockSpec((TM_PURE, D_MODEL), lambda i, j: (i, 0)),
                  pl.BlockSpec((D_MODEL, TN), lambda i, j: (0, j)),
                  pl.BlockSpec((D_MODEL, TN), lambda i, j: (0, j + D_FF // TN))],
        out_specs=[out, out, out],
        out_shape=[SDS((LP, D_FF), BF16)] * 3,
        compiler_params=_cparams(("parallel", "parallel")), name="ffn_up")(hn2, w_gu_b, w_gu_b)


def _ffn_down_loss(act, w_dn_b, h1, tgt, g_post_ffn):
    def body(act_ref, w_ref, h1_ref, t0_ref, t1_ref, t2_ref, g_ref, dff_ref, dy_ref, loss_ref, dg_ref):
        i = pl.program_id(0)
        target = jnp.concatenate([t0_ref[...], t1_ref[...], t2_ref[...]], axis=0)

        @pl.when(i == 0)
        def _():
            loss_ref[...] = jnp.zeros_like(loss_ref)
            dg_ref[...] = jnp.zeros_like(dg_ref)

        ff = jnp.dot(act_ref[...], w_ref[...], preferred_element_type=F32)
        r = _rms(ff)
        n = ff * r
        g = g_ref[...]
        y = h1_ref[...] + n * g
        rows = i * TM + lax.broadcasted_iota(jnp.int32, (TM, D_MODEL), 0)
        diff = jnp.where(rows >= ROW0, y - target, 0.0)
        loss_ref[...] += 0.5 * jnp.sum(diff * diff) / D_MODEL
        dy = diff / D_MODEL
        dy_ref[...] = dy
        dg_ref[...] += jnp.sum(dy * n, axis=0, keepdims=True)
        dff_ref[...] = _rms_bwd(n, r, g * dy).astype(BF16)

    row = pl.BlockSpec((TM, D_MODEL), lambda i: (i, 0))
    tblk = lambda j: pl.BlockSpec((BLK, D_MODEL), lambda i: (jnp.maximum(3 * i - 1 + j, 0), 0))
    return pl.pallas_call(
        body, grid=(NT,),
        in_specs=[pl.BlockSpec((TM, D_FF), lambda i: (i, 0)), pl.BlockSpec((D_FF, D_MODEL), lambda i: (0, 0)),
                  row, tblk(0), tblk(1), tblk(2), pl.BlockSpec((1, D_MODEL), lambda i: (0, 0))],
        out_specs=[row, row, pl.BlockSpec((8, BLK), lambda i: (0, 0)), pl.BlockSpec((1, D_MODEL), lambda i: (0, 0))],
        out_shape=[SDS((LP, D_MODEL), BF16), SDS((LP, D_MODEL), F32), SDS((8, BLK), F32), SDS((1, D_MODEL), F32)],
        compiler_params=_cparams(("arbitrary",)), name="ffn_down_loss")(act, w_dn_b, h1, tgt, tgt, tgt, g_post_ffn)


def _ffn_down_bwd(dff, w_dn_b, g, u):
    def body(d_ref, w_ref, g_ref, u_ref, dg_ref, du_ref):
        dact = lax.dot_general(d_ref[...], w_ref[...], NT_DIMS, preferred_element_type=F32)
        gg = g_ref[...].astype(F32)
        sig = 1.0 / (1.0 + jnp.exp(-gg))
        dg_ref[...] = (dact * u_ref[...].astype(F32) * sig * (1.0 + gg * (1.0 - sig))).astype(BF16)
        du_ref[...] = (dact * gg * sig).astype(BF16)

    blk = pl.BlockSpec((TM_PURE, TN), lambda i, j: (i, j))
    return pl.pallas_call(
        body, grid=(LP // TM_PURE, D_FF // TN),
        in_specs=[pl.BlockSpec((TM_PURE, D_MODEL), lambda i, j: (i, 0)),
                  pl.BlockSpec((TN, D_MODEL), lambda i, j: (j, 0)), blk, blk],
        out_specs=[blk, blk],
        out_shape=[SDS((LP, D_FF), BF16)] * 2,
        compiler_params=_cparams(("parallel", "parallel")), name="ffn_down_bwd")(dff, w_dn_b, g, u)


def _ffn_up_bwd(dg, du, w_gu_b, h1, a, dy, g_pre_ffn, g_post_mix, rider):
    n_rows = LP // TM_EPI

    def body(*refs):
        ((dg_ref, du_ref, w_ref, h1_ref, a_ref, dy_ref, gf_ref, gp_ref), (dh1_ref, da_ref, dgf_ref, dgp_ref),
         (acc,), mine) = rider.split(refs, 8, 4, 1)
        i = pl.program_id(0)
        s = pl.program_id(1)
        rider.at_steps(mine, (i == 0) & (s == 0), (i == n_rows // 2) & (s == 0), (i == n_rows - 1) & (s == 3))

        @pl.when((i == 0) & (s == 0))
        def _():
            dgf_ref[...] = jnp.zeros_like(dgf_ref)
            dgp_ref[...] = jnp.zeros_like(dgp_ref)

        @pl.when(s == 0)
        def _():
            acc[...] = jnp.zeros_like(acc)

        @pl.when(s < 2)
        def _():
            acc[...] += lax.dot_general(dg_ref[...], w_ref[...], NT_DIMS, preferred_element_type=F32)

        @pl.when(s >= 2)
        def _():
            acc[...] += lax.dot_general(du_ref[...], w_ref[...], NT_DIMS, preferred_element_type=F32)

        @pl.when(s == 3)
        def _():
            dhn2 = acc[...]
            h1 = h1_ref[...]
            r2 = _rms(h1)
            n2 = h1 * r2
            dgf_ref[...] += jnp.sum(dhn2 * n2, axis=0, keepdims=True)
            dh1 = dy_ref[...] + _rms_bwd(n2, r2, gf_ref[...] * dhn2)
            dh1_ref[...] = dh1
            av = a_ref[...]
            ra = _rms(av)
            na = av * ra
            dgp_ref[...] += jnp.sum(dh1 * na, axis=0, keepdims=True)
            da_ref[...] = _rms_bwd(na, ra, gp_ref[...] * dh1).astype(BF16)

    row = pl.BlockSpec((TM_EPI, D_MODEL), lambda i, s: (i, 0))
    vec = pl.BlockSpec((1, D_MODEL), lambda i, s: (0, 0))
    dh1, da, dgf, dgp, *carried = pl.pallas_call(
        body, grid=(n_rows, 4),
        in_specs=[pl.BlockSpec((TM_EPI, FF_T), lambda i, s: (i, jnp.minimum(s, 1))),
                  pl.BlockSpec((TM_EPI, FF_T), lambda i, s: (i, jnp.maximum(s - 2, 0))),
                  pl.BlockSpec((D_MODEL, FF_T), lambda i, s: (0, s)),
                  row, row, row, vec, vec] + [HBM_SPEC] * len(rider.operands),
        out_specs=[row, row, vec, vec] + [HBM_SPEC] * len(rider.out_shapes),
        out_shape=[SDS((LP, D_MODEL), F32), SDS((LP, D_MODEL), BF16), SDS((1, D_MODEL), F32),
                   SDS((1, D_MODEL), F32)] + rider.out_shapes,
        scratch_shapes=[pltpu.VMEM((TM_EPI, D_MODEL), F32)] + rider.scratch(),
        compiler_params=_cparams(("arbitrary", "arbitrary")), name="ffn_up_bwd",
    )(dg, du, w_gu_b, h1, a, dy, g_pre_ffn, g_post_mix, *rider.operands)
    return dh1, da, dgf, dgp, carried


def _attn_out_bwd(da, w_out_b):
    def body(d_ref, w_ref, o_ref):
        o_ref[...] = lax.dot_general(d_ref[...], w_ref[...], NT_DIMS, preferred_element_type=F32).astype(BF16)

    row = pl.BlockSpec((TM_PURE, D_MODEL), lambda i: (i, 0))
    return pl.pallas_call(
        body, grid=(LP // TM_PURE,),
        in_specs=[row, pl.BlockSpec((D_MODEL, D_MODEL), lambda i: (0, 0))],
        out_specs=row, out_shape=SDS((LP, D_MODEL), BF16),
        compiler_params=_cparams(("parallel",)), name="attn_out_bwd")(da, w_out_b)


def _pre_mix_bwd(dq_a, dq_b, dk_b, dv_b, dk_a, dv_a, df, w_in_b, h0, dh1, g_pre_mix):
    def body(qa_ref, qb_ref, kb_ref, vb_ref, ka_ref, va_ref, f_ref, w_ref, h0_ref, dh1_ref, g_ref,
             dproj_ref, dh0_ref, dg_ref):
        i = pl.program_id(0)

        @pl.when(i == 0)
        def _():
            dg_ref[...] = jnp.zeros_like(dg_ref)

        dproj = jnp.concatenate(
            [qa_ref[...], (qb_ref[...] * SCALE).astype(BF16), kb_ref[...], vb_ref[...],
             ka_ref[...].astype(BF16), va_ref[...].astype(BF16), f_ref[...].astype(BF16)], axis=1)
        dproj_ref[...] = dproj
        dhn = lax.dot_general(dproj, w_ref[...], NT_DIMS, preferred_element_type=F32)
        x = h0_ref[...]
        r = _rms(x)
        n = x * r
        dg_ref[...] += jnp.sum(dhn * n, axis=0, keepdims=True)
        dh0_ref[...] = dh1_ref[...] + _rms_bwd(n, r, g_ref[...] * dhn)

    row = lambda w: pl.BlockSpec((TM_EPI, w), lambda i: (i, 0))
    vec = pl.BlockSpec((1, D_MODEL), lambda i: (0, 0))
    return pl.pallas_call(
        body, grid=(LP // TM_EPI,),
        in_specs=[row(512), row(512), row(512), row(512), row(BLK), row(BLK), row(BLK),
                  pl.BlockSpec((D_MODEL, D_PROJ_P), lambda i: (0, 0)), row(D_MODEL), row(D_MODEL), vec],
        out_specs=[row(D_PROJ_P), row(D_MODEL), vec],
        out_shape=[SDS((LP, D_PROJ_P), BF16), SDS((LP, D_MODEL), F32), SDS((1, D_MODEL), F32)],
        compiler_params=_cparams(("arbitrary",)), name="pre_mix_bwd",
    )(dq_a, dq_b, dk_b, dv_b, dk_a, dv_a, df, w_in_b, h0, dh1, g_pre_mix)


def _mm_tn(parts, b, tm, name):
    widths = [p.shape[1] for p in parts]
    m_total = sum(widths)
    n = b.shape[1]
    whole = len(parts) > 1
    assert (tm == m_total) if whole else (m_total % tm == 0)

    def body(*refs):
        a_refs, b_ref, o_ref = refs[:-2], refs[-2], refs[-1]

        @pl.when(pl.program_id(1) == 0)
        def _():
            o_ref[...] = jnp.zeros_like(o_ref)
        a = a_refs[0][...] if not whole else jnp.concatenate([r[...] for r in a_refs], axis=1)
        o_ref[...] += lax.dot_general(a, b_ref[...], TN_DIMS, preferred_element_type=F32)

    a_specs = ([pl.BlockSpec((TM_MID, w), lambda mi, k: (k, 0)) for w in widths] if whole
               else [pl.BlockSpec((TM_MID, tm), lambda mi, k: (k, mi))])
    return pl.pallas_call(
        body, grid=(m_total // tm, LP // TM_MID),
        in_specs=a_specs + [pl.BlockSpec((TM_MID, n), lambda mi, k: (k, 0))],
        out_specs=pl.BlockSpec((tm, n), lambda mi, k: (mi, 0)),
        out_shape=SDS((m_total, n), F32),
        compiler_params=_cparams(("parallel", "arbitrary")), name=name)(*parts, b)


def _dw_gate_up(hn2, dg, du):
    def body(a_ref, dg_ref, du_ref, o_ref):
        s = pl.program_id(0)

        @pl.when(pl.program_id(1) == 0)
        def _():
            o_ref[...] = jnp.zeros_like(o_ref)

        @pl.when(s < 2)
        def _():
            o_ref[0] += lax.dot_general(a_ref[...], dg_ref[...], TN_DIMS, preferred_element_type=F32)

        @pl.when(s >= 2)
        def _():
            o_ref[0] += lax.dot_general(a_ref[...], du_ref[...], TN_DIMS, preferred_element_type=F32)

    return pl.pallas_call(
        body, grid=(4, LP // TM_MID),
        in_specs=[pl.BlockSpec((TM_MID, D_MODEL), lambda s, k: (k, 0)),
                  pl.BlockSpec((TM_MID, FF_T), lambda s, k: (k, jnp.minimum(s, 1))),
                  pl.BlockSpec((TM_MID, FF_T), lambda s, k: (k, jnp.maximum(s - 2, 0)))],
        out_specs=pl.BlockSpec((1, D_MODEL, FF_T), lambda s, k: (s, 0, 0)),
        out_shape=SDS((4, D_MODEL, FF_T), F32),
        compiler_params=_cparams(("parallel", "arbitrary")), name="dw_gate_up")(hn2, dg, du)


def _split3(x):
    hi = x.astype(BF16)
    r1 = x - hi.astype(F32)
    mid = r1.astype(BF16)
    lo = (r1 - mid.astype(F32)).astype(BF16)
    return hi, mid, lo


def _tri_matmul(tri, x):
    hi, mid, lo = _split3(x)
    dot = lambda t: jnp.dot(tri, t, preferred_element_type=F32)
    return dot(hi) + dot(mid) + dot(lo)


def _forget_cumsum(f, b_forget_p):
    def body(f_ref, b_ref, cum_ref, carry):
        i = pl.program_id(0)

        @pl.when(i == 0)
        def _():
            carry[...] = jnp.zeros_like(carry)

        z = f_ref[...] + b_ref[...]
        ls = jnp.minimum(z, 0.0) - jnp.log(1.0 + jnp.exp(-jnp.abs(z)))
        rows = i * TM + lax.broadcasted_iota(jnp.int32, (TM, BLK), 0)
        ls = jnp.where(rows >= PAD_ROWS, ls, 0.0)
        r = lax.broadcasted_iota(jnp.int32, (TM, TM), 0)
        c = lax.broadcasted_iota(jnp.int32, (TM, TM), 1)
        tri = (c <= r).astype(BF16)
        cum = _tri_matmul(tri, ls) + carry[...]
        cum_ref[...] = cum
        carry[...] = cum[TM - 1:TM, :]

    return pl.pallas_call(
        body, grid=(NT,),
        in_specs=[pl.BlockSpec((TM, BLK), lambda i: (i, 0)), pl.BlockSpec((1, BLK), lambda i: (0, 0))],
        out_specs=pl.BlockSpec((TM, BLK), lambda i: (i, 0)),
        out_shape=SDS((LP, BLK), F32),
        scratch_shapes=[pltpu.VMEM((1, BLK), F32)],
        compiler_params=_cparams(("arbitrary",)), name="forget_cumsum")(f, b_forget_p)


def _forget_cumsum_bwd(dcum, f, b_forget_p):
    def body(d_ref, f_ref, b_ref, df_ref, db_ref, carry):
        i = pl.program_id(0)

        @pl.when(i == 0)
        def _():
            carry[...] = jnp.zeros_like(carry)
            db_ref[...] = jnp.zeros_like(db_ref)

        blk = NT - 1 - i
        r = lax.broadcasted_iota(jnp.int32, (TM, TM), 0)
        c = lax.broadcasted_iota(jnp.int32, (TM, TM), 1)
        tri = (c >= r).astype(BF16)
        d = d_ref[...]
        dls = _tri_matmul(tri, d) + carry[...]
        carry[...] = dls[0:1, :]
        z = f_ref[...] + b_ref[...]
        rows = blk * TM + lax.broadcasted_iota(jnp.int32, (TM, BLK), 0)
        df = jnp.where(rows >= PAD_ROWS, dls / (1.0 + jnp.exp(z)), 0.0)
        df_ref[...] = df
        db_ref[...] += jnp.sum(df, axis=0, keepdims=True)

    rev = pl.BlockSpec((TM, BLK), lambda i: (NT - 1 - i, 0))
    vec = pl.BlockSpec((1, BLK), lambda i: (0, 0))
    return pl.pallas_call(
        body, grid=(NT,),
        in_specs=[rev, rev, vec],
        out_specs=[rev, vec],
        out_shape=[SDS((LP, BLK), F32), SDS((1, BLK), F32)],
        scratch_shapes=[pltpu.VMEM((1, BLK), F32)],
        compiler_params=_cparams(("arbitrary",)), name="forget_cumsum_bwd")(dcum, f, b_forget_p)


def _lane_half(rows):
    return lax.broadcasted_iota(jnp.int32, (rows, BLK), 1) // HALF


def _fox_valid(qi, kj):
    qrow = qi * TM + lax.broadcasted_iota(jnp.int32, (TM, TM), 0)
    krow = kj * TM + lax.broadcasted_iota(jnp.int32, (TM, TM), 1)
    return (krow <= qrow) & ((krow >= PAD_ROWS) | (qrow < PAD_ROWS))


class _Rider:
    def __init__(self, operands, out_shapes, sem_counts, first, middle, last):
        self.operands, self.out_shapes, self.sem_counts = list(operands), list(out_shapes), list(sem_counts)
        self.first, self.middle, self.last = first, middle, last

    def scratch(self):
        return [pltpu.SemaphoreType.DMA((k,)) for k in self.sem_counts]

    def split(self, refs, n_in, n_out, n_scratch):
        a, b = len(self.operands), len(self.out_shapes)
        ins, mine_in = refs[:n_in], refs[n_in:n_in + a]
        outs, mine_out = refs[n_in + a:n_in + a + n_out], refs[n_in + a + n_out:n_in + a + n_out + b]
        rest = refs[n_in + a + n_out + b:]
        return ins, outs, rest[:n_scratch], (mine_in, mine_out, rest[n_scratch:])

    def at_steps(self, mine, is_first, is_middle, is_last):
        for cond, fn in ((is_first, self.first), (is_middle, self.middle), (is_last, self.last)):
            pl.when(cond)(lambda fn=fn: fn(*mine))


HBM_SPEC = pl.BlockSpec(memory_space=pltpu.HBM)


N_AUG = 4
QCH = 128
KSUB = 384
AHEAD = 5
AHEAD_BWD = 1


def _fox_prep(proj, cum):
    def body(q_ref, k_ref, v_ref, c_ref, qa_ref, ka_ref, vt_ref):
        half = _lane_half(TM)
        lane = lax.broadcasted_iota(jnp.int32, (TM, BLK), 1)
        for pp in range(4):
            cols = slice(pp * BLK, (pp + 1) * BLK)
            qs = q_ref[:, cols].astype(F32) * SCALE
            kp = k_ref[:, cols].astype(F32)
            vp = v_ref[:, cols]
            vt_ref[cols, :] = vp.astype(F32).T.astype(BF16)
            for e in range(2):
                h = 2 * pp + e
                a = (1 - e) * HALF
                blk = slice(h * BLK, (h + 1) * BLK)
                hi, mid, lo = _split3(-c_ref[:, h:h + 1])
                q_aug = jnp.where(half == e, qs, jnp.where((lane >= a) & (lane < a + 3), 1.0, 0.0))
                k_aug = jnp.where(half == e, kp, jnp.where(
                    lane == a, hi.astype(F32), jnp.where(lane == a + 1, mid.astype(F32), jnp.where(
                        lane == a + 2, lo.astype(F32), jnp.where(lane == a + 3, 1.0, 0.0)))))
                qa_ref[:, blk] = q_aug.astype(BF16)
                ka_ref[:, blk] = k_aug.astype(BF16)

    row = lambda blk: pl.BlockSpec((TM, 512), lambda i: (i, blk))
    wide = pl.BlockSpec((TM, 1024), lambda i: (i, 0))
    return pl.pallas_call(
        body, grid=(NT,),
        in_specs=[row(QB), row(KB), row(VB), pl.BlockSpec((TM, BLK), lambda i: (i, 0))],
        out_specs=[wide, wide, pl.BlockSpec((512, TM), lambda i: (0, i))],
        out_shape=[SDS((LP, 1024), BF16)] * 2 + [SDS((512, LP), BF16)],
        compiler_params=_cparams(("parallel",)), name="fox_prep")(proj, proj, proj, cum)


def _over_keys(reduce, x):
    slabs = x.reshape(x.shape[0] // HALF, HALF, x.shape[1])
    return reduce(reduce(slabs, axis=0), axis=0, keepdims=True)


def _fox_valid_t(qi, kj, c, r):
    krow = kj * TM + r * KSUB + lax.broadcasted_iota(jnp.int32, (KSUB, QCH), 0)
    qrow = qi * TM + c * QCH + lax.broadcasted_iota(jnp.int32, (KSUB, QCH), 1)
    return (krow <= qrow) & ((krow >= PAD_ROWS) | (qrow < PAD_ROWS))


def _fox_fwd(q_aug, k_aug, v_t, rider):
    pairs = [(qi, kj) for qi in range(NT) for kj in range(qi + 1)]
    n_pairs = len(pairs)

    def body(qi_ref, kj_ref, *refs):
        (q_ref, k_ref, vt_ref), (o_ref, lse_ref), (m_s, l_s, acc_s), mine = rider.split(refs, 3, 2, 3)
        n = pl.program_id(0)
        qi = qi_ref[n]
        kj = kj_ref[n]
        rider.at_steps(mine, n == 0, n == n_pairs // 2, n == n_pairs - 1)

        @pl.when(kj == 0)
        def _():
            m_s[...] = jnp.full_like(m_s, NEG)
            l_s[...] = jnp.zeros_like(l_s)
            acc_s[...] = jnp.zeros_like(acc_s)

        def tile(masked):
            steps = [(h, c, r) for h in range(N_HEADS) for c in range(TM // QCH) for r in range(TM // KSUB)]

            def scores(h, c, r):
                blk = slice(h * BLK, (h + 1) * BLK)
                return lax.dot_general(k_ref[r * KSUB:(r + 1) * KSUB, blk], q_ref[c * QCH:(c + 1) * QCH, blk],
                                       NT_DIMS, preferred_element_type=F32)

            ahead = [scores(*st) for st in steps[:AHEAD]]
            for n, (h, c, r) in enumerate(steps):
                s_t = ahead.pop(0)
                if n + AHEAD < len(steps):
                    ahead.append(scores(*steps[n + AHEAD]))
                cs = slice(c * QCH, (c + 1) * QCH)
                if masked:
                    s_t = jnp.where(_fox_valid_t(qi, kj, c, r), s_t, NEG)
                m_prev = m_s[h, :, cs]
                m_new = jnp.maximum(m_prev, _over_keys(jnp.max, s_t))
                p_t = jnp.exp(s_t - m_new)
                alpha = jnp.exp(m_prev - m_new)
                l_s[h, :, cs] = alpha * l_s[h, :, cs] + _over_keys(jnp.sum, p_t)
                m_s[h, :, cs] = m_new
                vt = vt_ref[h * HALF:(h + 1) * HALF, r * KSUB:(r + 1) * KSUB]
                acc_s[h, :, cs] = acc_s[h, :, cs] * alpha + jnp.dot(vt, p_t.astype(BF16),
                                                                    preferred_element_type=F32)

        @pl.when((kj < qi) & (kj > 0))
        def _():
            tile(False)

        @pl.when((kj == qi) | (kj == 0))
        def _():
            tile(True)

        @pl.when(kj == qi)
        def _():
            for pp in range(4):
                both = jnp.concatenate([acc_s[2 * pp] * (1.0 / l_s[2 * pp]),
                                        acc_s[2 * pp + 1] * (1.0 / l_s[2 * pp + 1])], axis=0)
                o_ref[:, pp * BLK:(pp + 1) * BLK] = both.T.astype(BF16)
            for h in range(N_HEADS):
                lse_ref[h] = m_s[h] + jnp.log(l_s[h])

    grid_spec = pltpu.PrefetchScalarGridSpec(
        num_scalar_prefetch=2, grid=(n_pairs,),
        in_specs=[pl.BlockSpec((TM, 1024), lambda n, qi, kj: (qi[n], 0)),
                  pl.BlockSpec((TM, 1024), lambda n, qi, kj: (kj[n], 0)),
                  pl.BlockSpec((512, TM), lambda n, qi, kj: (0, kj[n]))] + [HBM_SPEC] * len(rider.operands),
        out_specs=[pl.BlockSpec((TM, 512), lambda n, qi, kj: (qi[n], 0)),
                   pl.BlockSpec((N_HEADS, 1, TM), lambda n, qi, kj: (0, 0, qi[n]))]
        + [HBM_SPEC] * len(rider.out_shapes),
        scratch_shapes=[pltpu.VMEM((N_HEADS, 1, TM), F32), pltpu.VMEM((N_HEADS, 1, TM), F32),
                        pltpu.VMEM((N_HEADS, HALF, TM), F32)] + rider.scratch())
    o_b, lse, *carried = pl.pallas_call(
        body, grid_spec=grid_spec,
        out_shape=[SDS((LP, 512), BF16), SDS((N_HEADS, 1, LP), F32)] + rider.out_shapes,
        compiler_params=_cparams(("arbitrary",)), name="fox_fwd",
    )(jnp.asarray([p[0] for p in pairs], jnp.int32), jnp.asarray([p[1] for p in pairs], jnp.int32),
      q_aug, k_aug, v_t, *rider.operands)
    return o_b, lse, carried


def _fox_bwd(proj, o_b, dmix, lse, ck_t, rider):
    pairs = [(kj, qi) for kj in range(NT) for qi in range(kj, NT)]
    n_pairs = len(pairs)

    def body(kj_ref, qi_ref, *refs):
        ((q_ref, k_ref, v_ref, o_ref, do_ref, lse_ref, ck_ref), (dq_ref, dk_ref, dv_ref, dck_ref, dcq_ref),
         (dk_s, dv_s, dck_s), mine) = rider.split(refs, 7, 5, 3)
        n = pl.program_id(0)
        kj = kj_ref[n]
        qi = qi_ref[n]
        rider.at_steps(mine, n == 0, n == n_pairs // 2, n == n_pairs - 1)

        @pl.when(n == 0)
        def _():
            dq_ref[...] = jnp.zeros_like(dq_ref)
            dcq_ref[...] = jnp.zeros_like(dcq_ref)

        @pl.when(qi == kj)
        def _():
            dk_s[...] = jnp.zeros_like(dk_s)
            dv_s[...] = jnp.zeros_like(dv_s)
            dck_s[...] = jnp.zeros_like(dck_s)

        def tile(masked):
            valid = _fox_valid(qi, kj) if masked else None
            half = _lane_half(TM)
            q0 = pl.multiple_of(qi * TM, TM)
            lane = lax.broadcasted_iota(jnp.int32, (TM, BLK), 1)
            row_sums = jnp.zeros((TM, BLK), F32)
            pair_ops = {}

            def operands(pp):
                if pp not in pair_ops:
                    cols = slice(pp * BLK, (pp + 1) * BLK)
                    pair_ops[pp] = ((q_ref[:, cols].astype(F32) * SCALE).astype(BF16), k_ref[:, cols],
                                    v_ref[:, cols], do_ref[:, cols])
                return pair_ops[pp]

            def scores(pp, e):
                qs, kp, vp, dop = operands(pp)
                ke = jnp.where(half == e, kp, jnp.zeros_like(kp))
                ve = jnp.where(half == e, vp, jnp.zeros_like(vp))
                return (lax.dot_general(qs, ke, NT_DIMS, preferred_element_type=F32),
                        lax.dot_general(dop, ve, NT_DIMS, preferred_element_type=F32), ke)

            steps = [(pp, e) for pp in range(4) for e in range(2)]
            ahead = [scores(*st) for st in steps[:AHEAD_BWD]]
            for n, (pp, e) in enumerate(steps):
                raw, dp, ke = ahead.pop(0)
                if n + AHEAD_BWD < len(steps):
                    ahead.append(scores(*steps[n + AHEAD_BWD]))
                h = 2 * pp + e
                cols = slice(pp * BLK, (pp + 1) * BLK)
                qs, kp, vp, dop = operands(pp)
                if e == 0:
                    prod = dop.astype(F32) * o_ref[:, cols].astype(F32)
                    d0 = jnp.sum(jnp.where(half == 0, prod, 0.0), axis=1, keepdims=True)
                    d1 = jnp.sum(prod, axis=1, keepdims=True) - d0
                    dq = jnp.zeros((TM, BLK), F32)
                    dks, dvs = [], []
                t = raw - ck_ref[h] - lse_ref[h]
                if masked:
                    t = jnp.where(valid, t, NEG)
                p = jnp.exp(t)
                ds = p * (dp - (d0 if e == 0 else d1))
                dck_s[h] += jnp.sum(ds, axis=0, keepdims=True)
                row_sums = jnp.where(lane == h, jnp.sum(ds, axis=1, keepdims=True), row_sums)
                ds_b = ds.astype(BF16)
                dq = dq + jnp.dot(ds_b, ke, preferred_element_type=F32)
                dks.append(lax.dot_general(ds_b, qs, TN_DIMS, preferred_element_type=F32))
                dvs.append(lax.dot_general(p.astype(BF16), dop, TN_DIMS, preferred_element_type=F32))
                if e == 1:
                    dq_ref[pl.ds(q0, TM), cols] += dq
                    dk_s[pp] += jnp.where(half == 0, dks[0], dks[1])
                    dv_s[pp] += jnp.where(half == 0, dvs[0], dvs[1])
            dcq_ref[pl.ds(q0, TM), :] += row_sums

        @pl.when((qi > kj) & (kj > 0))
        def _():
            tile(False)

        @pl.when((qi == kj) | (kj == 0))
        def _():
            tile(True)

        @pl.when(qi == NT - 1)
        def _():
            for pp in range(4):
                cols = slice(pp * BLK, (pp + 1) * BLK)
                dk_ref[:, cols] = dk_s[pp].astype(BF16)
                dv_ref[:, cols] = dv_s[pp].astype(BF16)
            dck_ref[...] = dck_s[...]

    qrow = lambda blk: pl.BlockSpec((TM, 512), lambda n, kj, qi: (qi[n], blk))
    krow = lambda blk: pl.BlockSpec((TM, 512), lambda n, kj, qi: (kj[n], blk))
    grid_spec = pltpu.PrefetchScalarGridSpec(
        num_scalar_prefetch=2, grid=(n_pairs,),
        in_specs=[qrow(QB), krow(KB), krow(VB), qrow(0), qrow(1),
                  pl.BlockSpec((N_HEADS, TM, 1), lambda n, kj, qi: (0, qi[n], 0)),
                  pl.BlockSpec((N_HEADS, 1, TM), lambda n, kj, qi: (0, 0, kj[n]))] + [HBM_SPEC] * len(rider.operands),
        out_specs=[pl.BlockSpec((LP, 512), lambda n, kj, qi: (0, 0)),
                   pl.BlockSpec((TM, 512), lambda n, kj, qi: (kj[n], 0)),
                   pl.BlockSpec((TM, 512), lambda n, kj, qi: (kj[n], 0)),
                   pl.BlockSpec((N_HEADS, 1, TM), lambda n, kj, qi: (0, 0, kj[n])),
                   pl.BlockSpec((LP, BLK), lambda n, kj, qi: (0, 0))] + [HBM_SPEC] * len(rider.out_shapes),
        scratch_shapes=[pltpu.VMEM((4, TM, BLK), F32), pltpu.VMEM((4, TM, BLK), F32),
                        pltpu.VMEM((N_HEADS, 1, TM), F32)] + rider.scratch())
    dq, dk, dv, dck, dcq, *carried = pl.pallas_call(
        body, grid_spec=grid_spec,
        out_shape=[SDS((LP, 512), F32), SDS((LP, 512), BF16), SDS((LP, 512), BF16), SDS((N_HEADS, 1, LP), F32),
                   SDS((LP, BLK), F32)] + rider.out_shapes,
        compiler_params=_cparams(("arbitrary",)), name="fox_bwd",
    )(jnp.asarray([p[0] for p in pairs], jnp.int32), jnp.asarray([p[1] for p in pairs], jnp.int32),
      proj, proj, proj, o_b, dmix, lse, ck_t, *rider.operands)
    return dq, dk, dv, dck, dcq, carried


N_SEG = 3
N_KEY = N_SEG * BLK
GROUP = 4
QW = GROUP * BLK


def _bucket_tables_t():
    return np.ascontiguousarray(_bucket_tables().transpose(0, 2, 1))


def _stack_heads(ref, g, scale):
    half = _lane_half(BLK)
    out = []
    for pair in range(2):
        x = ref[:, (2 * g + pair) * BLK:(2 * g + pair + 1) * BLK].astype(F32) * scale
        swapped = pltpu.roll(x, HALF, 1)
        for e in range(2):
            out.append(jnp.where(half == g, x if e == g else swapped, 0.0).astype(BF16))
    return jnp.concatenate(out, axis=0)


def _unstack_heads(x_t, g, ref, scale):
    for pair in range(2):
        both = jnp.concatenate([x_t[:, (2 * pair) * BLK:(2 * pair + 1) * BLK],
                                x_t[:, (2 * pair + 1) * BLK:(2 * pair + 2) * BLK]], axis=0)
        ref[:, (2 * g + pair) * BLK:(2 * g + pair + 1) * BLK] = (both.T * scale).astype(ref.dtype)


def _swa_tables(tab_ref, sink_ref, bkt_ref, tbl, sink_row):
    kk = lax.broadcasted_iota(jnp.int32, (BLK, BLK), 0)
    qq = lax.broadcasted_iota(jnp.int32, (BLK, BLK), 1)
    neg = jnp.full((BLK, BLK), NEG, F32)
    lane = lax.broadcasted_iota(jnp.int32, (1, QW), 1) // BLK
    for g in range(2):
        row = jnp.zeros((1, QW), F32)
        for hh in range(GROUP):
            h = GROUP * g + hh
            cols = slice(hh * BLK, (hh + 1) * BLK)
            row = jnp.where(lane == hh, sink_ref[0, h], row)

            def step(b, carry, h=h):
                t = tab_ref[b, h]
                return jnp.where(bkt_ref[0] == b, t, carry[0]), jnp.where(bkt_ref[1] == b, t, carry[1])
            zero = jnp.zeros((BLK, BLK), F32)
            cur, prev = lax.fori_loop(0, N_BUCKETS, step, (zero, zero))
            far = jnp.full((BLK, BLK), tab_ref[N_BUCKETS - 1, h], F32)
            causal = jnp.where(kk <= qq, cur, neg)
            segments = [
                (neg, neg, jnp.where(kk >= PAD_ROWS, causal, neg)),
                (jnp.where(kk >= PAD_ROWS, prev, neg), neg, causal),
                (jnp.where(kk >= PAD_ROWS, far, neg), jnp.where(kk > qq, prev, neg), causal)]
            for case in range(3):
                for seg in range(N_SEG):
                    tbl[case, g, seg * BLK:(seg + 1) * BLK, cols] = segments[case][seg]
        sink_row[g] = row


def _swa_prep(proj):
    rows = LP // 3

    def body(k_ref, v_ref, kt_ref, vt_ref):
        kt_ref[...] = k_ref[...].astype(F32).T.astype(BF16)
        vt_ref[...] = v_ref[...].astype(F32).T.astype(BF16)

    col = pl.BlockSpec((BLK, rows), lambda i: (0, i))
    return pl.pallas_call(
        body, grid=(3,),
        in_specs=[pl.BlockSpec((rows, BLK), lambda i: (i, KA)), pl.BlockSpec((rows, BLK), lambda i: (i, VA))],
        out_specs=[col, col], out_shape=[SDS((BLK, LP), BF16)] * 2,
        compiler_params=_cparams(("parallel",)), name="swa_prep")(proj, proj)


def _seg_specs(rows_major, col):
    idx = [lambda i: 0, lambda i: jnp.maximum(i - 1, 0), lambda i: i]
    if rows_major:
        return [pl.BlockSpec((BLK, BLK), lambda i, f=f: (f(i), col)) for f in idx]
    return [pl.BlockSpec((BLK, BLK), lambda i, f=f: (0, f(i))) for f in idx]


def _swa_fwd(proj, vt_a, rel_bias, sinks, bkt_t):
    def body(tab_ref, sink_ref, bkt_ref, q_ref, km_ref, kp_ref, kc_ref, vm_ref, vp_ref, vc_ref,
             o_ref, lse_ref, tbl, sink_row):
        i = pl.program_id(0)

        @pl.when(i == 0)
        def _():
            _swa_tables(tab_ref, sink_ref, bkt_ref, tbl, sink_row)

        case = jnp.minimum(i, 2)
        k_cat = jnp.concatenate([km_ref[...], kp_ref[...], kc_ref[...]], axis=0)
        vt_cat = jnp.concatenate([vm_ref[...], vp_ref[...], vc_ref[...]], axis=1)
        raw = [lax.dot_general(k_cat, _stack_heads(q_ref, g, SCALE), NT_DIMS, preferred_element_type=F32)
               for g in range(2)]
        for g in range(2):
            s_t = raw[g] + tbl[case, g]
            sink = sink_row[g]
            m = jnp.maximum(_over_keys(jnp.max, s_t), sink)
            p_t = jnp.exp(s_t - m)
            l = _over_keys(jnp.sum, p_t) + jnp.exp(sink - m)
            o_t = jnp.dot(vt_cat[g * HALF:(g + 1) * HALF, :], p_t.astype(BF16), preferred_element_type=F32)
            _unstack_heads(o_t * (1.0 / l), g, o_ref, 1.0)
            lse = m + jnp.log(l)
            for hh in range(GROUP):
                lse_ref[GROUP * g + hh] = lse[:, hh * BLK:(hh + 1) * BLK]

    smem = pl.BlockSpec(memory_space=pltpu.SMEM)
    return pl.pallas_call(
        body, grid=(NBLK,),
        in_specs=[smem, smem, pl.BlockSpec((2, BLK, BLK), lambda i: (0, 0, 0)),
                  pl.BlockSpec((BLK, 512), lambda i: (i, QA))] + _seg_specs(True, KA) + _seg_specs(False, 0),
        out_specs=[pl.BlockSpec((BLK, 512), lambda i: (i, 0)),
                   pl.BlockSpec((N_HEADS, 1, BLK), lambda i: (0, 0, i))],
        out_shape=[SDS((LP, 512), BF16), SDS((N_HEADS, 1, LP), F32)],
        scratch_shapes=[pltpu.VMEM((3, 2, N_KEY, QW), F32), pltpu.VMEM((2, 1, QW), F32)],
        compiler_params=_cparams(("arbitrary",)), name="swa_fwd",
    )(rel_bias, sinks, bkt_t, proj, proj, proj, proj, vt_a, vt_a, vt_a)


def _swa_bwd(proj, kt_a, o_a, dmix, lse, rel_bias, sinks, bkt_t):
    def body(tab_ref, sink_ref, bkt_ref, q_ref, km_ref, kp_ref, kc_ref, vm_ref, vp_ref, vc_ref,
             tm_ref, tp_ref, tc_ref, o_ref, do_ref, lse_ref,
             dq_ref, dk_ref, dv_ref, dbias_ref, dsink_ref, tbl, sink_row, acc, dsk):
        i = pl.program_id(0)

        @pl.when(i == 0)
        def _():
            _swa_tables(tab_ref, sink_ref, bkt_ref, tbl, sink_row)
            dk_ref[...] = jnp.zeros_like(dk_ref)
            dv_ref[...] = jnp.zeros_like(dv_ref)
            acc[...] = jnp.zeros_like(acc)
            dsk[...] = jnp.zeros_like(dsk)

        case = jnp.minimum(i, 2)
        first = jnp.full((BLK, QW), i, jnp.int32) == 1
        k_cat = jnp.concatenate([km_ref[...], kp_ref[...], kc_ref[...]], axis=0)
        v_cat = jnp.concatenate([vm_ref[...], vp_ref[...], vc_ref[...]], axis=0)
        kt_cat = jnp.concatenate([tm_ref[...], tp_ref[...], tc_ref[...]], axis=1)
        dk_cat = jnp.zeros((N_KEY, BLK), F32)
        dv_cat = jnp.zeros((N_KEY, BLK), F32)
        for g in range(2):
            d_parts = []
            for pair in range(2):
                cols = slice((2 * g + pair) * BLK, (2 * g + pair + 1) * BLK)
                prod_t = (do_ref[:, cols].astype(F32) * o_ref[:, cols].astype(F32)).T
                d_parts += [jnp.sum(prod_t[:HALF], axis=0, keepdims=True),
                            jnp.sum(prod_t[HALF:], axis=0, keepdims=True)]
            d_row = jnp.concatenate(d_parts, axis=1)
            lse_row = jnp.concatenate([lse_ref[GROUP * g + hh] for hh in range(GROUP)], axis=1)
            q_st = _stack_heads(q_ref, g, SCALE)
            do_st = _stack_heads(do_ref, g, 1.0)
            s_t = lax.dot_general(k_cat, q_st, NT_DIMS, preferred_element_type=F32) + tbl[case, g]
            p_t = jnp.exp(s_t - lse_row)
            dp_t = lax.dot_general(v_cat, do_st, NT_DIMS, preferred_element_type=F32)
            ds_t = p_t * (dp_t - d_row)
            dsk[g] += -jnp.exp(sink_row[g] - lse_row) * d_row
            acc[g, 0:BLK] += jnp.where(first, 0.0, ds_t[0:BLK])
            acc[g, BLK:2 * BLK] += jnp.where(first, ds_t[0:BLK], ds_t[BLK:2 * BLK])
            acc[g, 2 * BLK:N_KEY] += ds_t[2 * BLK:N_KEY]
            ds_b = ds_t.astype(BF16)
            dk_cat = dk_cat + jnp.dot(ds_b, q_st, preferred_element_type=F32)
            dv_cat = dv_cat + jnp.dot(p_t.astype(BF16), do_st, preferred_element_type=F32)
            dq_t = jnp.dot(kt_cat[g * HALF:(g + 1) * HALF, :], ds_b, preferred_element_type=F32)
            _unstack_heads(dq_t, g, dq_ref, SCALE)

        prev0 = pl.multiple_of(jnp.maximum(i - 1, 0) * BLK, BLK)
        cur0 = pl.multiple_of(i * BLK, BLK)
        for ref, cat in ((dk_ref, dk_cat), (dv_ref, dv_cat)):
            ref[0:BLK, :] += cat[0:BLK]
            ref[pl.ds(prev0, BLK), :] += cat[BLK:2 * BLK]
            ref[pl.ds(cur0, BLK), :] += cat[2 * BLK:N_KEY]

        @pl.when(i == NBLK - 1)
        def _():
            lane = lax.broadcasted_iota(jnp.int32, (1, BLK), 1)

            def per_bucket(b, carry):
                row = jnp.zeros((1, BLK), F32)
                for h in range(N_HEADS):
                    g, cols = h // GROUP, slice((h % GROUP) * BLK, (h % GROUP + 1) * BLK)
                    val = (jnp.sum(jnp.where(bkt_ref[0] == b, acc[g, 2 * BLK:N_KEY, cols], 0.0), keepdims=True)
                           + jnp.sum(jnp.where(bkt_ref[1] == b, acc[g, BLK:2 * BLK, cols], 0.0), keepdims=True))
                    row = jnp.where(lane == h, val, row)
                dbias_ref[pl.ds(b, 1), :] = row
                return carry

            lax.fori_loop(0, N_BUCKETS, per_bucket, 0)
            far = jnp.zeros((1, BLK), F32)
            dsr = jnp.zeros((1, BLK), F32)
            for h in range(N_HEADS):
                g, cols = h // GROUP, slice((h % GROUP) * BLK, (h % GROUP + 1) * BLK)
                far = jnp.where(lane == h, jnp.sum(acc[g, 0:BLK, cols], keepdims=True), far)
                dsr = jnp.where(lane == h, jnp.sum(dsk[g, :, cols], keepdims=True), dsr)
            dbias_ref[N_BUCKETS - 1:N_BUCKETS, :] += far
            dsink_ref[...] = dsr

    smem = pl.BlockSpec(memory_space=pltpu.SMEM)
    blk512 = lambda col: pl.BlockSpec((BLK, 512), lambda i: (i, col))
    full = lambda r, c: pl.BlockSpec((r, c), lambda i: (0, 0))
    return pl.pallas_call(
        body, grid=(NBLK,),
        in_specs=[smem, smem, pl.BlockSpec((2, BLK, BLK), lambda i: (0, 0, 0)), blk512(QA)]
        + _seg_specs(True, KA) + _seg_specs(True, VA) + _seg_specs(False, 0)
        + [blk512(0), blk512(0), pl.BlockSpec((N_HEADS, 1, BLK), lambda i: (0, 0, i))],
        out_specs=[blk512(0), full(LP, BLK), full(LP, BLK), full(N_BUCKETS, BLK), full(1, BLK)],
        out_shape=[SDS((LP, 512), BF16), SDS((LP, BLK), F32), SDS((LP, BLK), F32),
                   SDS((N_BUCKETS, BLK), F32), SDS((1, BLK), F32)],
        scratch_shapes=[pltpu.VMEM((3, 2, N_KEY, QW), F32), pltpu.VMEM((2, 1, QW), F32),
                        pltpu.VMEM((2, N_KEY, QW), F32), pltpu.VMEM((2, 1, QW), F32)],
        compiler_params=_cparams(("arbitrary",)), name="swa_bwd",
    )(rel_bias, sinks, bkt_t, proj, proj, proj, proj, proj, proj, proj, kt_a, kt_a, kt_a, o_a, dmix, lse)


def _local_step(x, tgt, meta, rel_bias, g_pre_mix, g_post_mix, g_pre_ffn, g_post_ffn, b_forget, sinks,
                w_in_b, w_out_b, ffn_rider, ffn_weights, ffn_grads, early_grads):
    bkt_t = jnp.asarray(_bucket_tables_t())
    h0 = jnp.concatenate([jnp.zeros((PAD_ROWS, D_MODEL), F32), meta, x], axis=0)
    b_p = jnp.pad(b_forget, ((0, 0), (0, BLK - N_HEADS)))

    hn1, proj, f = _pre_mix(h0, g_pre_mix, w_in_b)
    kt_a, vt_a = _swa_prep(proj)
    o_a, lse_a = _swa_fwd(proj, vt_a, rel_bias, sinks, bkt_t)
    cum = _forget_cumsum(f, b_p)
    ck_t = cum[:, :N_HEADS].T.reshape(N_HEADS, 1, LP)
    q_aug, k_aug, v_t = _fox_prep(proj, cum)
    o_b, lse_row, carried = _fox_fwd(q_aug, k_aug, v_t, ffn_rider)
    lse_b = lse_row.reshape(N_HEADS, LP, 1)
    w_gu_b, w_dn_b = ffn_weights(carried)
    a, h1, hn2 = _attn_out(o_a, o_b, w_out_b, h0, g_post_mix, g_pre_ffn)
    g, u, act = _ffn_up(hn2, w_gu_b)
    dff, dy, loss_blk, dg_post_ffn = _ffn_down_loss(act, w_dn_b, h1, tgt, g_post_ffn)

    dw_dn = _mm_tn([act], dff, FF_T, "dw_down")
    dg, du = _ffn_down_bwd(dff, w_dn_b, g, u)
    dw_gu = _dw_gate_up(hn2, dg, du)
    dh1, da, dg_pre_ffn, dg_post_mix, swapped = _ffn_up_bwd(dg, du, w_gu_b, h1, a, dy, g_pre_ffn, g_post_mix,
                                                           ffn_grads(dw_gu, dw_dn))
    dw_out = _mm_tn([o_a, o_b], da, D_MODEL, "dw_out")
    dmix = _attn_out_bwd(da, w_out_b)
    dq_b, dk_b, dv_b, dck, dcq, landed = _fox_bwd(proj, o_b, dmix, lse_b, ck_t, early_grads(swapped, dw_out))
    dq_a, dk_a, dv_a, dbias, dsink = _swa_bwd(proj, kt_a, o_a, dmix, lse_a, rel_bias, sinks, bkt_t)
    dcum = dcq - jnp.pad(dck.reshape(N_HEADS, LP).T, ((0, 0), (0, BLK - N_HEADS)))
    df, db = _forget_cumsum_bwd(dcum, f, b_p)
    dproj, dh0, dg_pre_mix = _pre_mix_bwd(dq_a, dq_b, dk_b, dv_b, dk_a, dv_a, df, w_in_b, h0, dh1, g_pre_mix)
    dw_in = _mm_tn([hn1], dproj, D_MODEL, "dw_in")

    return dict(loss=loss_blk[0, 0], grad_x=dh0[ROW0:], meta=dh0[PAD_ROWS:ROW0],
                rel_bias=dbias[:, :N_HEADS], ln_pre_mix=dg_pre_mix, ln_post_mix=dg_post_mix,
                ln_pre_ffn=dg_pre_ffn, ln_post_ffn=dg_post_ffn, b_forget=db[:, :N_HEADS],
                sinks=dsink[:, :N_HEADS], w_in=dw_in, w_out=dw_out, w_gate_up=dw_gu, w_down=dw_dn,
                landed=landed)


N_SMALL = 24
LOSS_ROW = 6


def _place():
    x, y, c = lax.axis_index("x"), lax.axis_index("y"), lax.axis_index("c")
    return x, y, c, [(1 - x, y), (x, 1 - y), (1 - x, 1 - y)]


def _run_alone(rider, name):
    a, b = len(rider.operands), len(rider.out_shapes)

    def body(*refs):
        mine = (refs[:a], refs[a:a + b], refs[a + b:])
        rider.first(*mine)
        rider.middle(*mine)
        rider.last(*mine)

    return pl.pallas_call(body, in_specs=[HBM_SPEC] * a, out_specs=[HBM_SPEC] * b, out_shape=rider.out_shapes,
                          scratch_shapes=rider.scratch(), name=name)(*rider.operands)


def _gather_rider(shards, own_too):
    n = len(shards)

    def own_copies(ins, outs, sems):
        x, y, _, _ = _place()
        return [pltpu.make_async_copy(ins[a], outs[a].at[2 * x + y], sems[2].at[a]) for a in range(n)] if own_too else []

    def copies(ins, outs, sems):
        send_sems, recv_sems = sems[:2]
        x, y, c, others = _place()
        chip = 2 * x + y
        sibling = (x, y, 1 - c)

        def rc(a, k, src, dst, to):
            return pltpu.make_async_remote_copy(src_ref=src, dst_ref=dst, send_sem=send_sems.at[6 * a + k],
                                                recv_sem=recv_sems.at[6 * a + k], device_id=to, device_id_type=MESH)

        pairs = [(a, j, ox, oy) for a in range(n) for j, (ox, oy) in enumerate(others)]
        return dict(
            sent=lambda: [rc(a, j, ins[a].at[c], outs[a].at[chip, c], (ox, oy, c)) for a, j, ox, oy in pairs],
            landed=lambda: [rc(a, j, outs[a].at[2 * ox + oy, c], outs[a].at[2 * ox + oy, c], sibling)
                            for a, j, ox, oy in pairs],
            passed=lambda: [rc(a, 3 + j, outs[a].at[2 * ox + oy, c], outs[a].at[2 * ox + oy, c], sibling)
                            for a, j, ox, oy in pairs],
            arriving=lambda: [rc(a, 3 + j, outs[a].at[2 * ox + oy, 1 - c], outs[a].at[2 * ox + oy, 1 - c], sibling)
                              for a, j, ox, oy in pairs])

    def first(*mine):
        for cp in copies(*mine)["sent"]() + own_copies(*mine):
            cp.start()

    def middle(*mine):
        kinds = copies(*mine)
        for got, cp in zip(kinds["landed"](), kinds["passed"]()):
            got.wait_recv()
            cp.start()

    def last(*mine):
        kinds = copies(*mine)
        for cp in kinds["arriving"]():
            cp.wait_recv()
        for cp in kinds["sent"]() + kinds["passed"]():
            cp.wait_send()
        for cp in own_copies(*mine):
            cp.wait()

    return _Rider(shards, [SDS((4,) + s.shape, s.dtype) for s in shards], [6 * n, 6 * n] + [n] * own_too,
                  first, middle, last)


def _swap_rider(grads):
    n = len(grads)

    def copies(ins, outs, sems):
        x, y, c, _ = _place()
        return [pltpu.make_async_remote_copy(
            src_ref=ins[a].at[s, 1 - c], dst_ref=outs[a].at[s], send_sem=sems[0].at[4 * a + s],
            recv_sem=sems[1].at[4 * a + s], device_id=(x, y, 1 - c), device_id_type=MESH)
            for a in range(n) for s in range(4)]

    def first(*mine):
        for cp in copies(*mine):
            cp.start()

    def middle(*mine):
        pass

    def last(*mine):
        for cp in copies(*mine):
            cp.wait()

    return _Rider(grads, [SDS((4,) + g.shape[2:], g.dtype) for g in grads], [4 * n, 4 * n], first, middle, last)


def _pair_sum(g, got, c_arr, name):
    rh, cc = got.shape[1:]

    def body(c_ref, g_ref, p_ref, o_ref):
        o_ref[0] = (g_ref[0, 0] + p_ref[0]).astype(BF16)

    grid_spec = pltpu.PrefetchScalarGridSpec(
        num_scalar_prefetch=1, grid=(4,),
        in_specs=[pl.BlockSpec((1, 1, rh, cc), lambda s, c_ref: (s, c_ref[0], 0, 0)),
                  pl.BlockSpec((1, rh, cc), lambda s, c_ref: (s, 0, 0))],
        out_specs=pl.BlockSpec((1, rh, cc), lambda s, c_ref: (s, 0, 0)))
    return pl.pallas_call(body, grid_spec=grid_spec, out_shape=SDS((4, rh, cc), BF16),
                          compiler_params=_cparams(("parallel",)), name=name)(c_arr, g, got)


def _exchange_rider(parts, small=None):
    n = len(parts)

    def copies(ins, outs, sems):
        x, y, c, others = _place()
        out = [pltpu.make_async_remote_copy(
            src_ref=ins[a].at[2 * ox + oy], dst_ref=outs[a].at[j], send_sem=sems[0].at[3 * a + j],
            recv_sem=sems[1].at[3 * a + j], device_id=(ox, oy, c), device_id_type=MESH)
            for a in range(n) for j, (ox, oy) in enumerate(others)]
        own = []
        if small is not None:
            me = 4 * x + 2 * y + c
            peers = [(x, y, 1 - c)] + [(ox, oy, c) for ox, oy in others] + [(ox, oy, 1 - c) for ox, oy in others]
            out += [pltpu.make_async_remote_copy(
                src_ref=ins[n], dst_ref=outs[n].at[me], send_sem=sems[2].at[k], recv_sem=sems[3].at[k],
                device_id=peer, device_id_type=MESH) for k, peer in enumerate(peers)]
            own = [pltpu.make_async_copy(ins[n], outs[n].at[me], sems[4].at[0])]
        return out, own

    def first(*mine):
        out, own = copies(*mine)
        for cp in own + out:
            cp.start()

    def middle(*mine):
        pass

    def last(*mine):
        out, own = copies(*mine)
        for cp in out + own:
            cp.wait()

    shapes = [SDS((3,) + p.shape[1:], p.dtype) for p in parts]
    if small is None:
        return _Rider(parts, shapes, [3 * n, 3 * n], first, middle, last)
    return _Rider(parts + [small], shapes + [SDS((8,) + small.shape, small.dtype)], [3 * n, 3 * n, 7, 7, 1],
                  first, middle, last)


def _chip_sum(parts, landed, chip_arr, name):
    rh, cc = landed.shape[1:]
    tr = rh // 2

    def body(chip_ref, own_ref, p_ref, o_ref):
        o_ref[...] = ((own_ref[0].astype(F32) + p_ref[0].astype(F32)) + p_ref[1].astype(F32)) + p_ref[2].astype(F32)

    grid_spec = pltpu.PrefetchScalarGridSpec(
        num_scalar_prefetch=1, grid=(2,),
        in_specs=[pl.BlockSpec((1, tr, cc), lambda i, chip_ref: (chip_ref[0], i, 0)),
                  pl.BlockSpec((3, tr, cc), lambda i, chip_ref: (0, i, 0))],
        out_specs=pl.BlockSpec((tr, cc), lambda i, chip_ref: (i, 0)))
    return pl.pallas_call(body, grid_spec=grid_spec, out_shape=SDS((rh, cc), F32),
                          compiler_params=_cparams(("parallel",)), name=name)(chip_arr, parts, landed)


def _device_sum(p):
    def body(p_ref, o_ref):
        acc = p_ref[0]
        for k in range(1, 8):
            acc = acc + p_ref[k]
        o_ref[...] = acc

    return pl.pallas_call(body, out_shape=SDS(p.shape[1:], F32), name="small_sum")(p)


def _join_halves(halves):
    n = len(halves)

    def body(*refs):
        ins, outs = refs[:n], refs[n:2 * n]
        send_sems, recv_sems = refs[2 * n:]
        x, y, c, _ = _place()
        copies = [pltpu.make_async_remote_copy(
            src_ref=ins[a], dst_ref=outs[a], send_sem=send_sems.at[a], recv_sem=recv_sems.at[a],
            device_id=(x, y, 1 - c), device_id_type=MESH) for a in range(n)]
        for cp in copies:
            cp.start()
        for cp in copies:
            cp.wait()

    return pl.pallas_call(
        body, in_specs=[HBM_SPEC] * n, out_specs=[HBM_SPEC] * n,
        out_shape=[SDS(h.shape, h.dtype) for h in halves],
        scratch_shapes=[pltpu.SemaphoreType.DMA((n,)), pltpu.SemaphoreType.DMA((n,))],
        name="join_halves")(*halves)


def _adamw(w, g, m, v, name):
    rows, cols = w.shape
    tr = rows if rows <= 352 else (256 if rows % 256 == 0 else 352)

    def body(w_ref, g_ref, m_ref, v_ref, d_ref, nm_ref, nv_ref):
        gg = g_ref[...]
        nm = ADAM_B1 * m_ref[...] + (1.0 - ADAM_B1) * gg
        nv = ADAM_B2 * v_ref[...] + (1.0 - ADAM_B2) * (gg * gg)
        nm_ref[...] = nm
        nv_ref[...] = nv
        m_hat = nm / (1.0 - ADAM_B1 ** ADAM_STEP)
        v_hat = nv / (1.0 - ADAM_B2 ** ADAM_STEP)
        d_ref[...] = -ADAM_LR * (m_hat / (jnp.sqrt(v_hat) + ADAM_EPS) + ADAM_WD * w_ref[...])

    blk = pl.BlockSpec((tr, cols), lambda i: (i, 0))
    return pl.pallas_call(
        body, grid=(rows // tr,), in_specs=[blk] * 4, out_specs=[blk] * 3,
        out_shape=[SDS((rows, cols), F32)] * 3,
        compiler_params=_cparams(("parallel",)), name=name)(w, g, m, v)


def _adamw_halves(w, mine, theirs, m, v, c_arr, name):
    rows, cols = w.shape
    rh = rows // 2
    tr = rh if rh <= 352 else 256
    nh = rh // tr

    def body(c_ref, w_ref, mine_ref, theirs_ref, m_ref, v_ref, g_ref, d_ref, nm_ref, nv_ref):
        own = jnp.full((tr, cols), pl.program_id(0), jnp.int32) == c_ref[0]
        gg = jnp.where(own, mine_ref[...], theirs_ref[...])
        g_ref[...] = gg
        nm = ADAM_B1 * m_ref[...] + (1.0 - ADAM_B1) * gg
        nv = ADAM_B2 * v_ref[...] + (1.0 - ADAM_B2) * (gg * gg)
        nm_ref[...] = nm
        nv_ref[...] = nv
        m_hat = nm / (1.0 - ADAM_B1 ** ADAM_STEP)
        v_hat = nv / (1.0 - ADAM_B2 ** ADAM_STEP)
        d_ref[...] = -ADAM_LR * (m_hat / (jnp.sqrt(v_hat) + ADAM_EPS) + ADAM_WD * w_ref[...])

    whole = pl.BlockSpec((tr, cols), lambda hh, i, c_ref: (hh * nh + i, 0))
    part = pl.BlockSpec((tr, cols), lambda hh, i, c_ref: (i, 0))
    grid_spec = pltpu.PrefetchScalarGridSpec(
        num_scalar_prefetch=1, grid=(2, nh), in_specs=[whole, part, part, whole, whole], out_specs=[whole] * 4)
    return pl.pallas_call(body, grid_spec=grid_spec, out_shape=[SDS((rows, cols), F32)] * 4,
                          compiler_params=_cparams(("parallel", "parallel")), name=name)(c_arr, w, mine, theirs, m, v)


def _pack_small(pre_mix, post_mix, pre_ffn, post_ffn, rel_bias, b_forget, sinks):
    def at(row, v):
        return jnp.pad(v, ((row, 7 - row), (0, D_MODEL - v.shape[1])))
    return (at(0, pre_mix) + at(1, post_mix) + at(2, pre_ffn) + at(3, post_ffn)
            + at(4, rel_bias.reshape(1, N_BUCKETS * N_HEADS)) + at(5, jnp.concatenate([b_forget, sinks], axis=1)))


def _unpack_small(p):
    return dict(ln_pre_mix=p[0:1], ln_post_mix=p[1:2], ln_pre_ffn=p[2:3], ln_post_ffn=p[3:4],
                rel_bias=p[4, :N_BUCKETS * N_HEADS].reshape(N_BUCKETS, N_HEADS),
                b_forget=p[5:6, 0:N_HEADS], sinks=p[5:6, N_HEADS:2 * N_HEADS])


WEIGHTS = ("meta_tokens", "rel_bias", "ln_pre_mix", "ln_post_mix", "ln_pre_ffn", "ln_post_ffn",
           "w_in", "b_forget", "sinks", "w_out", "w_gate_up", "w_down")


def kernel(x, meta_tokens, rel_bias, ln_pre_mix, ln_post_mix, ln_pre_ffn, ln_post_ffn, w_in, b_forget, sinks, w_out, w_gate_up, w_down, loss_target, m_meta_tokens, m_rel_bias, m_ln_pre_mix, m_ln_post_mix, m_ln_pre_ffn, m_ln_post_ffn, m_w_in, m_b_forget, m_sinks, m_w_out, m_w_gate_up, m_w_down, v_meta_tokens, v_rel_bias, v_ln_pre_mix, v_ln_post_mix, v_ln_pre_ffn, v_ln_post_ffn, v_w_in, v_b_forget, v_sinks, v_w_out, v_w_gate_up, v_w_down):
    xi, yi, ci = lax.axis_index("x"), lax.axis_index("y"), lax.axis_index("c")
    chip = 2 * xi + yi
    c_arr = jnp.reshape(ci, (1,)).astype(jnp.int32)

    def halves(w, dtype):
        return w.astype(dtype).reshape(2, w.shape[0] // 2, w.shape[1])

    def with_own(gathered, shards):
        return [lax.dynamic_update_slice(got, own[None], (chip, 0, 0, 0)) for got, own in zip(gathered, shards)]

    shards = [halves(w_in[0], BF16), halves(w_out[0], BF16), halves(meta_tokens, F32)]
    gw_in, gw_out, g_meta = with_own(_run_alone(_gather_rider(shards, False), "gather_mixer_weights"), shards)
    ffn_shards = [halves(w_gate_up[0], BF16), halves(w_down[0], BF16)]

    def ffn_weights(carried):
        gw_gu, gw_dn = carried
        w_gu_b = gw_gu.reshape(4, D_MODEL, FF_T).transpose(1, 0, 2).reshape(D_MODEL, 2 * D_FF)
        return w_gu_b, gw_dn.reshape(D_FF, D_MODEL)

    early = {}

    def ffn_grads(dw_gu, dw_dn):
        early["grads"] = [dw_gu.reshape(4, 2, 512, FF_T), dw_dn.reshape(4, 2, 352, D_MODEL)]
        return _swap_rider(early["grads"])

    def early_grads(swapped, dw_out):
        g_out = dw_out.reshape(4, 2, 128, D_MODEL)
        grads = [g_out] + early["grads"]
        got = list(_run_alone(_swap_rider([g_out]), "swap_halves_out")) + list(swapped)
        early["parts"] = [_pair_sum(g, p, c_arr, "pair_sum_%d" % a) for a, (g, p) in enumerate(zip(grads, got))]
        return _exchange_rider(early["parts"])
    w_in_all = gw_in.reshape(4, D_MODEL, D_PROJ // 4).transpose(1, 0, 2).reshape(D_MODEL, D_PROJ)
    w_in_b = jnp.concatenate(
        [w_in_all[:, 0:512], w_in_all[:, 768:1280], w_in_all[:, 1280:1792], w_in_all[:, 1792:2304],
         w_in_all[:, 512:640], w_in_all[:, 640:768], w_in_all[:, 2304:2312],
         jnp.zeros((D_MODEL, D_PROJ_P - D_PROJ), BF16)], axis=1)
    meta_all = g_meta.reshape(4, N_META, D_MODEL // 4).transpose(1, 0, 2).reshape(N_META, D_MODEL)

    loc = _local_step(x[0], loss_target[0], meta_all, rel_bias, ln_pre_mix, ln_post_mix, ln_pre_ffn, ln_post_ffn,
                      b_forget, sinks, w_in_b, gw_out.reshape(D_MODEL, D_MODEL),
                      _gather_rider(ffn_shards, True), ffn_weights, ffn_grads, early_grads)

    n = loc["w_in"]
    dw_in = jnp.concatenate([n[:, 0:512], n[:, 2048:2176], n[:, 2176:2304], n[:, 512:1024], n[:, 1024:1536],
                             n[:, 1536:2048], n[:, 2304:2312]], axis=1)
    dw_in = dw_in.reshape(D_MODEL, 4, D_PROJ // 4).transpose(1, 0, 2).reshape(4, 2, 512, D_PROJ // 4)
    small = jnp.concatenate(
        [_pack_small(loc["ln_pre_mix"], loc["ln_post_mix"], loc["ln_pre_ffn"], loc["ln_post_ffn"],
                     loc["rel_bias"], loc["b_forget"], loc["sinks"])
         + jnp.pad(loc["loss"].reshape(1, 1), ((LOSS_ROW, 7 - LOSS_ROW), (0, D_MODEL - 1))), loc["meta"]], axis=0)

    (got_in,) = _run_alone(_swap_rider([dw_in]), "swap_halves_late")
    part_in = _pair_sum(dw_in, got_in, c_arr, "pair_sum_late")
    landed_in, small_all = _run_alone(_exchange_rider([part_in], small), "exchange_late")
    chip_arr = jnp.reshape(chip, (1,)).astype(jnp.int32)
    parts = [part_in] + early["parts"]
    landed = [landed_in] + loc["landed"]
    mine = [_chip_sum(p, l, chip_arr, "chip_sum_%d" % a) for a, (p, l) in enumerate(zip(parts, landed))]
    small_sum = _device_sum(small_all)
    theirs = _join_halves(mine)
    g_meta_tokens = lax.dynamic_slice(small_sum[8:N_SMALL], (0, chip * (D_MODEL // 4)), (N_META, D_MODEL // 4))
    g_small = small_sum[0:8]

    grad = _unpack_small(g_small)
    grad.update(meta_tokens=g_meta_tokens)
    delta, new_m, new_v = {}, {}, {}
    big = dict(w_in=(w_in, m_w_in, v_w_in), w_out=(w_out, m_w_out, v_w_out),
               w_gate_up=(w_gate_up, m_w_gate_up, v_w_gate_up), w_down=(w_down, m_w_down, v_w_down))
    for (name, (w, m, v)), g_mine, g_theirs in zip(big.items(), mine, theirs):
        g, d, nm, nv = _adamw_halves(w[0], g_mine, g_theirs, m[0], v[0], c_arr, "adamw_" + name)
        grad[name], delta[name], new_m[name], new_v[name] = g[None], d[None], nm[None], nv[None]
    delta["meta_tokens"], new_m["meta_tokens"], new_v["meta_tokens"] = _adamw(
        meta_tokens, g_meta_tokens, m_meta_tokens, v_meta_tokens, "adamw_meta")
    d, nm, nv = _adamw(
        _pack_small(ln_pre_mix, ln_post_mix, ln_pre_ffn, ln_post_ffn, rel_bias, b_forget, sinks), g_small,
        _pack_small(m_ln_pre_mix, m_ln_post_mix, m_ln_pre_ffn, m_ln_post_ffn, m_rel_bias, m_b_forget, m_sinks),
        _pack_small(v_ln_pre_mix, v_ln_post_mix, v_ln_pre_ffn, v_ln_post_ffn, v_rel_bias, v_b_forget, v_sinks),
        "adamw_small")
    delta.update(_unpack_small(d))
    new_m.update(_unpack_small(nm))
    new_v.update(_unpack_small(nv))

    loss = small_sum[LOSS_ROW, 0]
    return (loss,loc["grad_x"][None], *[grad[k] for k in WEIGHTS], *[delta[k] for k in WEIGHTS],
            *[new_m[k] for k in WEIGHTS], *[new_v[k] for k in WEIGHTS])
```

```python
import math

import numpy as np
import jax
import jax.numpy as jnp
from jax import lax
from jax.experimental import pallas as pl
from jax.experimental.pallas import tpu as pltpu

F32 = jnp.float32
BF16 = jnp.bfloat16
MESH = pl.DeviceIdType.MESH
SDS = jax.ShapeDtypeStruct

D_MODEL = 1024
SEQ = 4096
N_META = 16
N_HEADS = 8
HALF = 64
D_FF = 2816
N_BUCKETS = 32
EPS = 1e-6
NEG = -1e30
SCALE = 0.125
PAD_ROWS = 112
ROW0 = PAD_ROWS + N_META
LP = ROW0 + SEQ
BLK = 128
NBLK = LP // BLK
TM = 384
NT = LP // TM
TM_PURE = LP // 2
TM_MID = LP // 4
TM_EPI = LP // 6
TN = 256
D_PROJ = 2312
D_PROJ_P = 2432
D_QKV = 2304
FF_T = 1408
VMEM_LIMIT = 56 * 1024 * 1024

ADAM_LR = 0.001
ADAM_B1 = 0.9
ADAM_B2 = 0.999
ADAM_EPS = 1e-08
ADAM_WD = 0.01
ADAM_STEP = 10

QA, QB, KB, VB = 0, 1, 2, 3
KA, VA = 16, 17

NT_DIMS = (((1,), (1,)), ((), ()))
TN_DIMS = (((0,), (0,)), ((), ()))


def _cparams(sem):
    return pltpu.CompilerParams(dimension_semantics=sem, vmem_limit_bytes=VMEM_LIMIT)


def _t5_bucket_np(d):
    n = np.maximum(d, 0).astype(np.int32)
    nf = np.maximum(n, 1).astype(np.float32)
    large = 16 + (np.log(nf / np.float32(16)) / np.float32(math.log(8.0)) * np.float32(16)).astype(np.int32)
    large = np.minimum(large, N_BUCKETS - 1)
    return np.where(n < 16, n, large).astype(np.int32)


def _bucket_tables():
    qi = np.arange(BLK)[:, None]
    ki = np.arange(BLK)[None, :]
    return np.stack([_t5_bucket_np(qi - ki), _t5_bucket_np(qi - ki + BLK)])


def _rms(x):
    return lax.rsqrt(jnp.mean(x * x, axis=-1, keepdims=True) + EPS)


def _rms_bwd(n, r, gdy):
    return r * (gdy - n * jnp.mean(n * gdy, axis=-1, keepdims=True))


def _pre_mix(h0, gain, w_in_b):
    half = D_QKV // 2

    def body(h_ref, g_ref, w_ref, hn_ref, proj_ref, f_ref):
        x = h_ref[...]
        hn = (x * _rms(x) * g_ref[...]).astype(BF16)
        hn_ref[...] = hn
        proj_ref[:, :half] = jnp.dot(hn, w_ref[:, :half], preferred_element_type=F32).astype(BF16)
        p = jnp.dot(hn, w_ref[:, half:], preferred_element_type=F32)
        proj_ref[:, half:] = p[:, :half].astype(BF16)
        f_ref[...] = p[:, half:]

    return pl.pallas_call(
        body, grid=(LP // TM_MID,),
        in_specs=[pl.BlockSpec((TM_MID, D_MODEL), lambda i: (i, 0)),
                  pl.BlockSpec((1, D_MODEL), lambda i: (0, 0)),
                  pl.BlockSpec((D_MODEL, D_PROJ_P), lambda i: (0, 0))],
        out_specs=[pl.BlockSpec((TM_MID, D_MODEL), lambda i: (i, 0)),
                   pl.BlockSpec((TM_MID, D_QKV), lambda i: (i, 0)),
                   pl.BlockSpec((TM_MID, BLK), lambda i: (i, 0))],
        out_shape=[SDS((LP, D_MODEL), BF16), SDS((LP, D_QKV), BF16), SDS((LP, BLK), F32)],
        compiler_params=_cparams(("parallel",)), name="pre_mix")(h0, gain, w_in_b)


def _attn_out(o_a, o_b, w_out_b, h0, g_post, g_pre_ffn):
    def body(oa_ref, ob_ref, w_ref, h0_ref, gp_ref, gf_ref, a_ref, h1_ref, hn2_ref):
        a = (jnp.dot(oa_ref[...], w_ref[0:512, :], preferred_element_type=F32)
             + jnp.dot(ob_ref[...], w_ref[512:1024, :], preferred_element_type=F32))
        a_ref[...] = a
        h1 = h0_ref[...] + a * _rms(a) * gp_ref[...]
        h1_ref[...] = h1
        hn2_ref[...] = (h1 * _rms(h1) * gf_ref[...]).astype(BF16)

    row = lambda w: pl.BlockSpec((TM_EPI, w), lambda i: (i, 0))
    vec = pl.BlockSpec((1, D_MODEL), lambda i: (0, 0))
    return pl.pallas_call(
        body, grid=(LP // TM_EPI,),
        in_specs=[row(512), row(512), pl.BlockSpec((D_MODEL, D_MODEL), lambda i: (0, 0)), row(D_MODEL), vec, vec],
        out_specs=[row(D_MODEL), row(D_MODEL), row(D_MODEL)],
        out_shape=[SDS((LP, D_MODEL), F32), SDS((LP, D_MODEL), F32), SDS((LP, D_MODEL), BF16)],
        compiler_params=_cparams(("parallel",)), name="attn_out")(o_a, o_b, w_out_b, h0, g_post, g_pre_ffn)


def _ffn_up(hn2, w_gu_b):
    def body(x_ref, wg_ref, wu_ref, g_ref, u_ref, act_ref):
        x = x_ref[...]
        g = jnp.dot(x, wg_ref[...], preferred_element_type=F32)
        u = jnp.dot(x, wu_ref[...], preferred_element_type=F32)
        g_ref[...] = g.astype(BF16)
        u_ref[...] = u.astype(BF16)
        act_ref[...] = (g * (1.0 / (1.0 + jnp.exp(-g))) * u).astype(BF16)

    out = pl.BlockSpec((TM_PURE, TN), lambda i, j: (i, j))
    return pl.pallas_call(
        body, grid=(LP // TM_PURE, D_FF // TN),
        in_specs=[pl.BlockSpec((TM_PURE, D_MODEL), lambda i, j: (i, 0)),
                  pl.BlockSpec((D_MODEL, TN), lambda i, j: (0, j)),
                  pl.BlockSpec((D_MODEL, TN), lambda i, j: (0, j + D_FF // TN))],
        out_specs=[out, out, out],
        out_shape=[SDS((LP, D_FF), BF16)] * 3,
        compiler_params=_cparams(("parallel", "parallel")), name="ffn_up")(hn2, w_gu_b, w_gu_b)


def _ffn_down_loss(act, w_dn_b, h1, tgt, g_post_ffn):
    def body(act_ref, w_ref, h1_ref, t0_ref, t1_ref, t2_ref, g_ref, dff_ref, dy_ref, loss_ref, dg_ref):
        i = pl.program_id(0)
        target = jnp.concatenate([t0_ref[...], t1_ref[...], t2_ref[...]], axis=0)

        @pl.when(i == 0)
        def _():
            loss_ref[...] = jnp.zeros_like(loss_ref)
            dg_ref[...] = jnp.zeros_like(dg_ref)

        ff = jnp.dot(act_ref[...], w_ref[...], preferred_element_type=F32)
        r = _rms(ff)
        n = ff * r
        g = g_ref[...]
        y = h1_ref[...] + n * g
        rows = i * TM + lax.broadcasted_iota(jnp.int32, (TM, D_MODEL), 0)
        diff = jnp.where(rows >= ROW0, y - target, 0.0)
        loss_ref[...] += 0.5 * jnp.sum(diff * diff) / D_MODEL
        dy = diff / D_MODEL
        dy_ref[...] = dy
        dg_ref[...] += jnp.sum(dy * n, axis=0, keepdims=True)
        dff_ref[...] = _rms_bwd(n, r, g * dy).astype(BF16)

    row = pl.BlockSpec((TM, D_MODEL), lambda i: (i, 0))
    tblk = lambda j: pl.BlockSpec((BLK, D_MODEL), lambda i: (jnp.maximum(3 * i - 1 + j, 0), 0))
    return pl.pallas_call(
        body, grid=(NT,),
        in_specs=[pl.BlockSpec((TM, D_FF), lambda i: (i, 0)), pl.BlockSpec((D_FF, D_MODEL), lambda i: (0, 0)),
                  row, tblk(0), tblk(1), tblk(2), pl.BlockSpec((1, D_MODEL), lambda i: (0, 0))],
        out_specs=[row, row, pl.BlockSpec((8, BLK), lambda i: (0, 0)), pl.BlockSpec((1, D_MODEL), lambda i: (0, 0))],
        out_shape=[SDS((LP, D_MODEL), BF16), SDS((LP, D_MODEL), F32), SDS((8, BLK), F32), SDS((1, D_MODEL), F32)],
        compiler_params=_cparams(("arbitrary",)), name="ffn_down_loss")(act, w_dn_b, h1, tgt, tgt, tgt, g_post_ffn)


def _ffn_down_bwd(dff, w_dn_b, g, u):
    def body(d_ref, w_ref, g_ref, u_ref, dg_ref, du_ref):
        dact = lax.dot_general(d_ref[...], w_ref[...], NT_DIMS, preferred_element_type=F32)
        gg = g_ref[...].astype(F32)
        sig = 1.0 / (1.0 + jnp.exp(-gg))
        dg_ref[...] = (dact * u_ref[...].astype(F32) * sig * (1.0 + gg * (1.0 - sig))).astype(BF16)
        du_ref[...] = (dact * gg * sig).astype(BF16)

    blk = pl.BlockSpec((TM_PURE, TN), lambda i, j: (i, j))
    return pl.pallas_call(
        body, grid=(LP // TM_PURE, D_FF // TN),
        in_specs=[pl.BlockSpec((TM_PURE, D_MODEL), lambda i, j: (i, 0)),
                  pl.BlockSpec((TN, D_MODEL), lambda i, j: (j, 0)), blk, blk],
        out_specs=[blk, blk],
        out_shape=[SDS((LP, D_FF), BF16)] * 2,
        compiler_params=_cparams(("parallel", "parallel")), name="ffn_down_bwd")(dff, w_dn_b, g, u)


def _ffn_up_bwd(dg, du, w_gu_b, h1, a, dy, g_pre_ffn, g_post_mix, rider):
    n_rows = LP // TM_EPI

    def body(*refs):
        ((dg_ref, du_ref, w_ref, h1_ref, a_ref, dy_ref, gf_ref, gp_ref), (dh1_ref, da_ref, dgf_ref, dgp_ref),
         (acc,), mine) = rider.split(refs, 8, 4, 1)
        i = pl.program_id(0)
        s = pl.program_id(1)
        rider.at_steps(mine, (i == 0) & (s == 0), (i == n_rows // 2) & (s == 0), (i == n_rows - 1) & (s == 3))

        @pl.when((i == 0) & (s == 0))
        def _():
            dgf_ref[...] = jnp.zeros_like(dgf_ref)
            dgp_ref[...] = jnp.zeros_like(dgp_ref)

        @pl.when(s == 0)
        def _():
            acc[...] = jnp.zeros_like(acc)

        @pl.when(s < 2)
        def _():
            acc[...] += lax.dot_general(dg_ref[...], w_ref[...], NT_DIMS, preferred_element_type=F32)

        @pl.when(s >= 2)
        def _():
            acc[...] += lax.dot_general(du_ref[...], w_ref[...], NT_DIMS, preferred_element_type=F32)

        @pl.when(s == 3)
        def _():
            dhn2 = acc[...]
            h1 = h1_ref[...]
            r2 = _rms(h1)
            n2 = h1 * r2
            dgf_ref[...] += jnp.sum(dhn2 * n2, axis=0, keepdims=True)
            dh1 = dy_ref[...] + _rms_bwd(n2, r2, gf_ref[...] * dhn2)
            dh1_ref[...] = dh1
            av = a_ref[...]
            ra = _rms(av)
            na = av * ra
            dgp_ref[...] += jnp.sum(dh1 * na, axis=0, keepdims=True)
            da_ref[...] = _rms_bwd(na, ra, gp_ref[...] * dh1).astype(BF16)

    row = pl.BlockSpec((TM_EPI, D_MODEL), lambda i, s: (i, 0))
    vec = pl.BlockSpec((1, D_MODEL), lambda i, s: (0, 0))
    dh1, da, dgf, dgp, *carried = pl.pallas_call(
        body, grid=(n_rows, 4),
        in_specs=[pl.BlockSpec((TM_EPI, FF_T), lambda i, s: (i, jnp.minimum(s, 1))),
                  pl.BlockSpec((TM_EPI, FF_T), lambda i, s: (i, jnp.maximum(s - 2, 0))),
                  pl.BlockSpec((D_MODEL, FF_T), lambda i, s: (0, s)),
                  row, row, row, vec, vec] + [HBM_SPEC] * len(rider.operands),
        out_specs=[row, row, vec, vec] + [HBM_SPEC] * len(rider.out_shapes),
        out_shape=[SDS((LP, D_MODEL), F32), SDS((LP, D_MODEL), BF16), SDS((1, D_MODEL), F32),
                   SDS((1, D_MODEL), F32)] + rider.out_shapes,
        scratch_shapes=[pltpu.VMEM((TM_EPI, D_MODEL), F32)] + rider.scratch(),
        compiler_params=_cparams(("arbitrary", "arbitrary")), name="ffn_up_bwd",
    )(dg, du, w_gu_b, h1, a, dy, g_pre_ffn, g_post_mix, *rider.operands)
    return dh1, da, dgf, dgp, carried


def _attn_out_bwd(da, w_out_b):
    def body(d_ref, w_ref, o_ref):
        o_ref[...] = lax.dot_general(d_ref[...], w_ref[...], NT_DIMS, preferred_element_type=F32).astype(BF16)

    row = pl.BlockSpec((TM_PURE, D_MODEL), lambda i: (i, 0))
    return pl.pallas_call(
        body, grid=(LP // TM_PURE,),
        in_specs=[row, pl.BlockSpec((D_MODEL, D_MODEL), lambda i: (0, 0))],
        out_specs=row, out_shape=SDS((LP, D_MODEL), BF16),
        compiler_params=_cparams(("parallel",)), name="attn_out_bwd")(da, w_out_b)


def _pre_mix_bwd(dq_a, dq_b, dk_b, dv_b, dk_a, dv_a, df, w_in_b, h0, dh1, g_pre_mix):
    def body(qa_ref, qb_ref, kb_ref, vb_ref, ka_ref, va_ref, f_ref, w_ref, h0_ref, dh1_ref, g_ref,
             dproj_ref, dh0_ref, dg_ref):
        i = pl.program_id(0)

        @pl.when(i == 0)
        def _():
            dg_ref[...] = jnp.zeros_like(dg_ref)

        dproj = jnp.concatenate(
            [qa_ref[...], (qb_ref[...] * SCALE).astype(BF16), kb_ref[...], vb_ref[...],
             ka_ref[...].astype(BF16), va_ref[...].astype(BF16), f_ref[...].astype(BF16)], axis=1)
        dproj_ref[...] = dproj
        dhn = lax.dot_general(dproj, w_ref[...], NT_DIMS, preferred_element_type=F32)
        x = h0_ref[...]
        r = _rms(x)
        n = x * r
        dg_ref[...] += jnp.sum(dhn * n, axis=0, keepdims=True)
        dh0_ref[...] = dh1_ref[...] + _rms_bwd(n, r, g_ref[...] * dhn)

    row = lambda w: pl.BlockSpec((TM_EPI, w), lambda i: (i, 0))
    vec = pl.BlockSpec((1, D_MODEL), lambda i: (0, 0))
    return pl.pallas_call(
        body, grid=(LP // TM_EPI,),
        in_specs=[row(512), row(512), row(512), row(512), row(BLK), row(BLK), row(BLK),
                  pl.BlockSpec((D_MODEL, D_PROJ_P), lambda i: (0, 0)), row(D_MODEL), row(D_MODEL), vec],
        out_specs=[row(D_PROJ_P), row(D_MODEL), vec],
        out_shape=[SDS((LP, D_PROJ_P), BF16), SDS((LP, D_MODEL), F32), SDS((1, D_MODEL), F32)],
        compiler_params=_cparams(("arbitrary",)), name="pre_mix_bwd",
    )(dq_a, dq_b, dk_b, dv_b, dk_a, dv_a, df, w_in_b, h0, dh1, g_pre_mix)


def _mm_tn(parts, b, tm, name):
    widths = [p.shape[1] for p in parts]
    m_total = sum(widths)
    n = b.shape[1]
    whole = len(parts) > 1
    assert (tm == m_total) if whole else (m_total % tm == 0)

    def body(*refs):
        a_refs, b_ref, o_ref = refs[:-2], refs[-2], refs[-1]

        @pl.when(pl.program_id(1) == 0)
        def _():
            o_ref[...] = jnp.zeros_like(o_ref)
        a = a_refs[0][...] if not whole else jnp.concatenate([r[...] for r in a_refs], axis=1)
        o_ref[...] += lax.dot_general(a, b_ref[...], TN_DIMS, preferred_element_type=F32)

    a_specs = ([pl.BlockSpec((TM_MID, w), lambda mi, k: (k, 0)) for w in widths] if whole
               else [pl.BlockSpec((TM_MID, tm), lambda mi, k: (k, mi))])
    return pl.pallas_call(
        body, grid=(m_total // tm, LP // TM_MID),
        in_specs=a_specs + [pl.BlockSpec((TM_MID, n), lambda mi, k: (k, 0))],
        out_specs=pl.BlockSpec((tm, n), lambda mi, k: (mi, 0)),
        out_shape=SDS((m_total, n), F32),
        compiler_params=_cparams(("parallel", "arbitrary")), name=name)(*parts, b)


def _dw_gate_up(hn2, dg, du):
    def body(a_ref, dg_ref, du_ref, o_ref):
        s = pl.program_id(0)

        @pl.when(pl.program_id(1) == 0)
        def _():
            o_ref[...] = jnp.zeros_like(o_ref)

        @pl.when(s < 2)
        def _():
            o_ref[0] += lax.dot_general(a_ref[...], dg_ref[...], TN_DIMS, preferred_element_type=F32)

        @pl.when(s >= 2)
        def _():
            o_ref[0] += lax.dot_general(a_ref[...], du_ref[...], TN_DIMS, preferred_element_type=F32)

    return pl.pallas_call(
        body, grid=(4, LP // TM_MID),
        in_specs=[pl.BlockSpec((TM_MID, D_MODEL), lambda s, k: (k, 0)),
                  pl.BlockSpec((TM_MID, FF_T), lambda s, k: (k, jnp.minimum(s, 1))),
                  pl.BlockSpec((TM_MID, FF_T), lambda s, k: (k, jnp.maximum(s - 2, 0)))],
        out_specs=pl.BlockSpec((1, D_MODEL, FF_T), lambda s, k: (s, 0, 0)),
        out_shape=SDS((4, D_MODEL, FF_T), F32),
        compiler_params=_cparams(("parallel", "arbitrary")), name="dw_gate_up")(hn2, dg, du)


def _split3(x):
    hi = x.astype(BF16)
    r1 = x - hi.astype(F32)
    mid = r1.astype(BF16)
    lo = (r1 - mid.astype(F32)).astype(BF16)
    return hi, mid, lo


def _tri_matmul(tri, x):
    hi, mid, lo = _split3(x)
    dot = lambda t: jnp.dot(tri, t, preferred_element_type=F32)
    return dot(hi) + dot(mid) + dot(lo)


def _forget_cumsum(f, b_forget_p):
    def body(f_ref, b_ref, cum_ref, carry):
        i = pl.program_id(0)

        @pl.when(i == 0)
        def _():
            carry[...] = jnp.zeros_like(carry)

        z = f_ref[...] + b_ref[...]
        ls = jnp.minimum(z, 0.0) - jnp.log(1.0 + jnp.exp(-jnp.abs(z)))
        rows = i * TM + lax.broadcasted_iota(jnp.int32, (TM, BLK), 0)
        ls = jnp.where(rows >= PAD_ROWS, ls, 0.0)
        r = lax.broadcasted_iota(jnp.int32, (TM, TM), 0)
        c = lax.broadcasted_iota(jnp.int32, (TM, TM), 1)
        tri = (c <= r).astype(BF16)
        cum = _tri_matmul(tri, ls) + carry[...]
        cum_ref[...] = cum
        carry[...] = cum[TM - 1:TM, :]

    return pl.pallas_call(
        body, grid=(NT,),
        in_specs=[pl.BlockSpec((TM, BLK), lambda i: (i, 0)), pl.BlockSpec((1, BLK), lambda i: (0, 0))],
        out_specs=pl.BlockSpec((TM, BLK), lambda i: (i, 0)),
        out_shape=SDS((LP, BLK), F32),
        scratch_shapes=[pltpu.VMEM((1, BLK), F32)],
        compiler_params=_cparams(("arbitrary",)), name="forget_cumsum")(f, b_forget_p)


def _forget_cumsum_bwd(dcum, f, b_forget_p):
    def body(d_ref, f_ref, b_ref, df_ref, db_ref, carry):
        i = pl.program_id(0)

        @pl.when(i == 0)
        def _():
            carry[...] = jnp.zeros_like(carry)
            db_ref[...] = jnp.zeros_like(db_ref)

        blk = NT - 1 - i
        r = lax.broadcasted_iota(jnp.int32, (TM, TM), 0)
        c = lax.broadcasted_iota(jnp.int32, (TM, TM), 1)
        tri = (c >= r).astype(BF16)
        d = d_ref[...]
        dls = _tri_matmul(tri, d) + carry[...]
        carry[...] = dls[0:1, :]
        z = f_ref[...] + b_ref[...]
        rows = blk * TM + lax.broadcasted_iota(jnp.int32, (TM, BLK), 0)
        df = jnp.where(rows >= PAD_ROWS, dls / (1.0 + jnp.exp(z)), 0.0)
        df_ref[...] = df
        db_ref[...] += jnp.sum(df, axis=0, keepdims=True)

    rev = pl.BlockSpec((TM, BLK), lambda i: (NT - 1 - i, 0))
    vec = pl.BlockSpec((1, BLK), lambda i: (0, 0))
    return pl.pallas_call(
        body, grid=(NT,),
        in_specs=[rev, rev, vec],
        out_specs=[rev, vec],
        out_shape=[SDS((LP, BLK), F32), SDS((1, BLK), F32)],
        scratch_shapes=[pltpu.VMEM((1, BLK), F32)],
        compiler_params=_cparams(("arbitrary",)), name="forget_cumsum_bwd")(dcum, f, b_forget_p)


def _lane_half(rows):
    return lax.broadcasted_iota(jnp.int32, (rows, BLK), 1) // HALF


def _fox_valid(qi, kj):
    qrow = qi * TM + lax.broadcasted_iota(jnp.int32, (TM, TM), 0)
    krow = kj * TM + lax.broadcasted_iota(jnp.int32, (TM, TM), 1)
    return (krow <= qrow) & ((krow >= PAD_ROWS) | (qrow < PAD_ROWS))


class _Rider:
    def __init__(self, operands, out_shapes, sem_counts, first, middle, last):
        self.operands, self.out_shapes, self.sem_counts = list(operands), list(out_shapes), list(sem_counts)
        self.first, self.middle, self.last = first, middle, last

    def scratch(self):
        return [pltpu.SemaphoreType.DMA((k,)) for k in self.sem_counts]

    def split(self, refs, n_in, n_out, n_scratch):
        a, b = len(self.operands), len(self.out_shapes)
        ins, mine_in = refs[:n_in], refs[n_in:n_in + a]
        outs, mine_out = refs[n_in + a:n_in + a + n_out], refs[n_in + a + n_out:n_in + a + n_out + b]
        rest = refs[n_in + a + n_out + b:]
        return ins, outs, rest[:n_scratch], (mine_in, mine_out, rest[n_scratch:])

    def at_steps(self, mine, is_first, is_middle, is_last):
        for cond, fn in ((is_first, self.first), (is_middle, self.middle), (is_last, self.last)):
            pl.when(cond)(lambda fn=fn: fn(*mine))


HBM_SPEC = pl.BlockSpec(memory_space=pltpu.HBM)


N_AUG = 4
QCH = 128
KSUB = 384
AHEAD = 5
AHEAD_BWD = 1


def _fox_prep(proj, cum):
    def body(q_ref, k_ref, v_ref, c_ref, qa_ref, ka_ref, vt_ref):
        half = _lane_half(TM)
        lane = lax.broadcasted_iota(jnp.int32, (TM, BLK), 1)
        for pp in range(4):
            cols = slice(pp * BLK, (pp + 1) * BLK)
            qs = q_ref[:, cols].astype(F32) * SCALE
            kp = k_ref[:, cols].astype(F32)
            vp = v_ref[:, cols]
            vt_ref[cols, :] = vp.astype(F32).T.astype(BF16)
            for e in range(2):
                h = 2 * pp + e
                a = (1 - e) * HALF
                blk = slice(h * BLK, (h + 1) * BLK)
                hi, mid, lo = _split3(-c_ref[:, h:h + 1])
                q_aug = jnp.where(half == e, qs, jnp.where((lane >= a) & (lane < a + 3), 1.0, 0.0))
                k_aug = jnp.where(half == e, kp, jnp.where(
                    lane == a, hi.astype(F32), jnp.where(lane == a + 1, mid.astype(F32), jnp.where(
                        lane == a + 2, lo.astype(F32), jnp.where(lane == a + 3, 1.0, 0.0)))))
                qa_ref[:, blk] = q_aug.astype(BF16)
                ka_ref[:, blk] = k_aug.astype(BF16)

    row = lambda blk: pl.BlockSpec((TM, 512), lambda i: (i, blk))
    wide = pl.BlockSpec((TM, 1024), lambda i: (i, 0))
    return pl.pallas_call(
        body, grid=(NT,),
        in_specs=[row(QB), row(KB), row(VB), pl.BlockSpec((TM, BLK), lambda i: (i, 0))],
        out_specs=[wide, wide, pl.BlockSpec((512, TM), lambda i: (0, i))],
        out_shape=[SDS((LP, 1024), BF16)] * 2 + [SDS((512, LP), BF16)],
        compiler_params=_cparams(("parallel",)), name="fox_prep")(proj, proj, proj, cum)


def _over_keys(reduce, x):
    slabs = x.reshape(x.shape[0] // HALF, HALF, x.shape[1])
    return reduce(reduce(slabs, axis=0), axis=0, keepdims=True)


def _fox_valid_t(qi, kj, c, r):
    krow = kj * TM + r * KSUB + lax.broadcasted_iota(jnp.int32, (KSUB, QCH), 0)
    qrow = qi * TM + c * QCH + lax.broadcasted_iota(jnp.int32, (KSUB, QCH), 1)
    return (krow <= qrow) & ((krow >= PAD_ROWS) | (qrow < PAD_ROWS))


def _fox_fwd(q_aug, k_aug, v_t, rider):
    pairs = [(qi, kj) for qi in range(NT) for kj in range(qi + 1)]
    n_pairs = len(pairs)

    def body(qi_ref, kj_ref, *refs):
        (q_ref, k_ref, vt_ref), (o_ref, lse_ref), (m_s, l_s, acc_s), mine = rider.split(refs, 3, 2, 3)
        n = pl.program_id(0)
        qi = qi_ref[n]
        kj = kj_ref[n]
        rider.at_steps(mine, n == 0, n == n_pairs // 2, n == n_pairs - 1)

        @pl.when(kj == 0)
        def _():
            m_s[...] = jnp.full_like(m_s, NEG)
            l_s[...] = jnp.zeros_like(l_s)
            acc_s[...] = jnp.zeros_like(acc_s)

        def tile(masked):
            steps = [(h, c, r) for h in range(N_HEADS) for c in range(TM // QCH) for r in range(TM // KSUB)]

            def scores(h, c, r):
                blk = slice(h * BLK, (h + 1) * BLK)
                return lax.dot_general(k_ref[r * KSUB:(r + 1) * KSUB, blk], q_ref[c * QCH:(c + 1) * QCH, blk],
                                       NT_DIMS, preferred_element_type=F32)

            ahead = [scores(*st) for st in steps[:AHEAD]]
            for n, (h, c, r) in enumerate(steps):
                s_t = ahead.pop(0)
                if n + AHEAD < len(steps):
                    ahead.append(scores(*steps[n + AHEAD]))
                cs = slice(c * QCH, (c + 1) * QCH)
                if masked:
                    s_t = jnp.where(_fox_valid_t(qi, kj, c, r), s_t, NEG)
                m_prev = m_s[h, :, cs]
                m_new = jnp.maximum(m_prev, _over_keys(jnp.max, s_t))
                p_t = jnp.exp(s_t - m_new)
                alpha = jnp.exp(m_prev - m_new)
                l_s[h, :, cs] = alpha * l_s[h, :, cs] + _over_keys(jnp.sum, p_t)
                m_s[h, :, cs] = m_new
                vt = vt_ref[h * HALF:(h + 1) * HALF, r * KSUB:(r + 1) * KSUB]
                acc_s[h, :, cs] = acc_s[h, :, cs] * alpha + jnp.dot(vt, p_t.astype(BF16),
                                                                    preferred_element_type=F32)

        @pl.when((kj < qi) & (kj > 0))
        def _():
            tile(False)

        @pl.when((kj == qi) | (kj == 0))
        def _():
            tile(True)

        @pl.when(kj == qi)
        def _():
            for pp in range(4):
                both = jnp.concatenate([acc_s[2 * pp] * (1.0 / l_s[2 * pp]),
                                        acc_s[2 * pp + 1] * (1.0 / l_s[2 * pp + 1])], axis=0)
                o_ref[:, pp * BLK:(pp + 1) * BLK] = both.T.astype(BF16)
            for h in range(N_HEADS):
                lse_ref[h] = m_s[h] + jnp.log(l_s[h])

    grid_spec = pltpu.PrefetchScalarGridSpec(
        num_scalar_prefetch=2, grid=(n_pairs,),
        in_specs=[pl.BlockSpec((TM, 1024), lambda n, qi, kj: (qi[n], 0)),
                  pl.BlockSpec((TM, 1024), lambda n, qi, kj: (kj[n], 0)),
                  pl.BlockSpec((512, TM), lambda n, qi, kj: (0, kj[n]))] + [HBM_SPEC] * len(rider.operands),
        out_specs=[pl.BlockSpec((TM, 512), lambda n, qi, kj: (qi[n], 0)),
                   pl.BlockSpec((N_HEADS, 1, TM), lambda n, qi, kj: (0, 0, qi[n]))]
        + [HBM_SPEC] * len(rider.out_shapes),
        scratch_shapes=[pltpu.VMEM((N_HEADS, 1, TM), F32), pltpu.VMEM((N_HEADS, 1, TM), F32),
                        pltpu.VMEM((N_HEADS, HALF, TM), F32)] + rider.scratch())
    o_b, lse, *carried = pl.pallas_call(
        body, grid_spec=grid_spec,
        out_shape=[SDS((LP, 512), BF16), SDS((N_HEADS, 1, LP), F32)] + rider.out_shapes,
        compiler_params=_cparams(("arbitrary",)), name="fox_fwd",
    )(jnp.asarray([p[0] for p in pairs], jnp.int32), jnp.asarray([p[1] for p in pairs], jnp.int32),
      q_aug, k_aug, v_t, *rider.operands)
    return o_b, lse, carried


def _fox_bwd(proj, o_b, dmix, lse, ck_t, rider):
    pairs = [(kj, qi) for kj in range(NT) for qi in range(kj, NT)]
    n_pairs = len(pairs)

    def body(kj_ref, qi_ref, *refs):
        ((q_ref, k_ref, v_ref, o_ref, do_ref, lse_ref, ck_ref), (dq_ref, dk_ref, dv_ref, dck_ref, dcq_ref),
         (dk_s, dv_s, dck_s), mine) = rider.split(refs, 7, 5, 3)
        n = pl.program_id(0)
        kj = kj_ref[n]
        qi = qi_ref[n]
        rider.at_steps(mine, n == 0, n == n_pairs // 2, n == n_pairs - 1)

        @pl.when(n == 0)
        def _():
            dq_ref[...] = jnp.zeros_like(dq_ref)
            dcq_ref[...] = jnp.zeros_like(dcq_ref)

        @pl.when(qi == kj)
        def _():
            dk_s[...] = jnp.zeros_like(dk_s)
            dv_s[...] = jnp.zeros_like(dv_s)
            dck_s[...] = jnp.zeros_like(dck_s)

        def tile(masked):
            valid = _fox_valid(qi, kj) if masked else None
            half = _lane_half(TM)
            q0 = pl.multiple_of(qi * TM, TM)
            lane = lax.broadcasted_iota(jnp.int32, (TM, BLK), 1)
            row_sums = jnp.zeros((TM, BLK), F32)
            pair_ops = {}

            def operands(pp):
                if pp not in pair_ops:
                    cols = slice(pp * BLK, (pp + 1) * BLK)
                    pair_ops[pp] = ((q_ref[:, cols].astype(F32) * SCALE).astype(BF16), k_ref[:, cols],
                                    v_ref[:, cols], do_ref[:, cols])
                return pair_ops[pp]

            def scores(pp, e):
                qs, kp, vp, dop = operands(pp)
                ke = jnp.where(half == e, kp, jnp.zeros_like(kp))
                ve = jnp.where(half == e, vp, jnp.zeros_like(vp))
                return (lax.dot_general(qs, ke, NT_DIMS, preferred_element_type=F32),
                        lax.dot_general(dop, ve, NT_DIMS, preferred_element_type=F32), ke)

            steps = [(pp, e) for pp in range(4) for e in range(2)]
            ahead = [scores(*st) for st in steps[:AHEAD_BWD]]
            for n, (pp, e) in enumerate(steps):
                raw, dp, ke = ahead.pop(0)
                if n + AHEAD_BWD < len(steps):
                    ahead.append(scores(*steps[n + AHEAD_BWD]))
                h = 2 * pp + e
                cols = slice(pp * BLK, (pp + 1) * BLK)
                qs, kp, vp, dop = operands(pp)
                if e == 0:
                    prod = dop.astype(F32) * o_ref[:, cols].astype(F32)
                    d0 = jnp.sum(jnp.where(half == 0, prod, 0.0), axis=1, keepdims=True)
                    d1 = jnp.sum(prod, axis=1, keepdims=True) - d0
                    dq = jnp.zeros((TM, BLK), F32)
                    dks, dvs = [], []
                t = raw - ck_ref[h] - lse_ref[h]
                if masked:
                    t = jnp.where(valid, t, NEG)
                p = jnp.exp(t)
                ds = p * (dp - (d0 if e == 0 else d1))
                dck_s[h] += jnp.sum(ds, axis=0, keepdims=True)
                row_sums = jnp.where(lane == h, jnp.sum(ds, axis=1, keepdims=True), row_sums)
                ds_b = ds.astype(BF16)
                dq = dq + jnp.dot(ds_b, ke, preferred_element_type=F32)
                dks.append(lax.dot_general(ds_b, qs, TN_DIMS, preferred_element_type=F32))
                dvs.append(lax.dot_general(p.astype(BF16), dop, TN_DIMS, preferred_element_type=F32))
                if e == 1:
                    dq_ref[pl.ds(q0, TM), cols] += dq
                    dk_s[pp] += jnp.where(half == 0, dks[0], dks[1])
                    dv_s[pp] += jnp.where(half == 0, dvs[0], dvs[1])
            dcq_ref[pl.ds(q0, TM), :] += row_sums

        @pl.when((qi > kj) & (kj > 0))
        def _():
            tile(False)

        @pl.when((qi == kj) | (kj == 0))
        def _():
            tile(True)

        @pl.when(qi == NT - 1)
        def _():
            for pp in range(4):
                cols = slice(pp * BLK, (pp + 1) * BLK)
                dk_ref[:, cols] = dk_s[pp].astype(BF16)
                dv_ref[:, cols] = dv_s[pp].astype(BF16)
            dck_ref[...] = dck_s[...]

    qrow = lambda blk: pl.BlockSpec((TM, 512), lambda n, kj, qi: (qi[n], blk))
    krow = lambda blk: pl.BlockSpec((TM, 512), lambda n, kj, qi: (kj[n], blk))
    grid_spec = pltpu.PrefetchScalarGridSpec(
        num_scalar_prefetch=2, grid=(n_pairs,),
        in_specs=[qrow(QB), krow(KB), krow(VB), qrow(0), qrow(1),
                  pl.BlockSpec((N_HEADS, TM, 1), lambda n, kj, qi: (0, qi[n], 0)),
                  pl.BlockSpec((N_HEADS, 1, TM), lambda n, kj, qi: (0, 0, kj[n]))] + [HBM_SPEC] * len(rider.operands),
        out_specs=[pl.BlockSpec((LP, 512), lambda n, kj, qi: (0, 0)),
                   pl.BlockSpec((TM, 512), lambda n, kj, qi: (kj[n], 0)),
                   pl.BlockSpec((TM, 512), lambda n, kj, qi: (kj[n], 0)),
                   pl.BlockSpec((N_HEADS, 1, TM), lambda n, kj, qi: (0, 0, kj[n])),
                   pl.BlockSpec((LP, BLK), lambda n, kj, qi: (0, 0))] + [HBM_SPEC] * len(rider.out_shapes),
        scratch_shapes=[pltpu.VMEM((4, TM, BLK), F32), pltpu.VMEM((4, TM, BLK), F32),
                        pltpu.VMEM((N_HEADS, 1, TM), F32)] + rider.scratch())
    dq, dk, dv, dck, dcq, *carried = pl.pallas_call(
        body, grid_spec=grid_spec,
        out_shape=[SDS((LP, 512), F32), SDS((LP, 512), BF16), SDS((LP, 512), BF16), SDS((N_HEADS, 1, LP), F32),
                   SDS((LP, BLK), F32)] + rider.out_shapes,
        compiler_params=_cparams(("arbitrary",)), name="fox_bwd",
    )(jnp.asarray([p[0] for p in pairs], jnp.int32), jnp.asarray([p[1] for p in pairs], jnp.int32),
      proj, proj, proj, o_b, dmix, lse, ck_t, *rider.operands)
    return dq, dk, dv, dck, dcq, carried


N_SEG = 3
N_KEY = N_SEG * BLK
GROUP = 4
QW = GROUP * BLK


def _bucket_tables_t():
    return np.ascontiguousarray(_bucket_tables().transpose(0, 2, 1))


def _stack_heads(ref, g, scale):
    half = _lane_half(BLK)
    out = []
    for pair in range(2):
        x = ref[:, (2 * g + pair) * BLK:(2 * g + pair + 1) * BLK].astype(F32) * scale
        swapped = pltpu.roll(x, HALF, 1)
        for e in range(2):
            out.append(jnp.where(half == g, x if e == g else swapped, 0.0).astype(BF16))
    return jnp.concatenate(out, axis=0)


def _unstack_heads(x_t, g, ref, scale):
    for pair in range(2):
        both = jnp.concatenate([x_t[:, (2 * pair) * BLK:(2 * pair + 1) * BLK],
                                x_t[:, (2 * pair + 1) * BLK:(2 * pair + 2) * BLK]], axis=0)
        ref[:, (2 * g + pair) * BLK:(2 * g + pair + 1) * BLK] = (both.T * scale).astype(ref.dtype)


def _swa_tables(tab_ref, sink_ref, bkt_ref, tbl, sink_row):
    kk = lax.broadcasted_iota(jnp.int32, (BLK, BLK), 0)
    qq = lax.broadcasted_iota(jnp.int32, (BLK, BLK), 1)
    neg = jnp.full((BLK, BLK), NEG, F32)
    lane = lax.broadcasted_iota(jnp.int32, (1, QW), 1) // BLK
    for g in range(2):
        row = jnp.zeros((1, QW), F32)
        for hh in range(GROUP):
            h = GROUP * g + hh
            cols = slice(hh * BLK, (hh + 1) * BLK)
            row = jnp.where(lane == hh, sink_ref[0, h], row)

            def step(b, carry, h=h):
                t = tab_ref[b, h]
                return jnp.where(bkt_ref[0] == b, t, carry[0]), jnp.where(bkt_ref[1] == b, t, carry[1])
            zero = jnp.zeros((BLK, BLK), F32)
            cur, prev = lax.fori_loop(0, N_BUCKETS, step, (zero, zero))
            far = jnp.full((BLK, BLK), tab_ref[N_BUCKETS - 1, h], F32)
            causal = jnp.where(kk <= qq, cur, neg)
            segments = [
                (neg, neg, jnp.where(kk >= PAD_ROWS, causal, neg)),
                (jnp.where(kk >= PAD_ROWS, prev, neg), neg, causal),
                (jnp.where(kk >= PAD_ROWS, far, neg), jnp.where(kk > qq, prev, neg), causal)]
            for case in range(3):
                for seg in range(N_SEG):
                    tbl[case, g, seg * BLK:(seg + 1) * BLK, cols] = segments[case][seg]
        sink_row[g] = row


def _swa_prep(proj):
    rows = LP // 3

    def body(k_ref, v_ref, kt_ref, vt_ref):
        kt_ref[...] = k_ref[...].astype(F32).T.astype(BF16)
        vt_ref[...] = v_ref[...].astype(F32).T.astype(BF16)

    col = pl.BlockSpec((BLK, rows), lambda i: (0, i))
    return pl.pallas_call(
        body, grid=(3,),
        in_specs=[pl.BlockSpec((rows, BLK), lambda i: (i, KA)), pl.BlockSpec((rows, BLK), lambda i: (i, VA))],
        out_specs=[col, col], out_shape=[SDS((BLK, LP), BF16)] * 2,
        compiler_params=_cparams(("parallel",)), name="swa_prep")(proj, proj)


def _seg_specs(rows_major, col):
    idx = [lambda i: 0, lambda i: jnp.maximum(i - 1, 0), lambda i: i]
    if rows_major:
        return [pl.BlockSpec((BLK, BLK), lambda i, f=f: (f(i), col)) for f in idx]
    return [pl.BlockSpec((BLK, BLK), lambda i, f=f: (0, f(i))) for f in idx]


def _swa_fwd(proj, vt_a, rel_bias, sinks, bkt_t, rider):
    def body(*refs):
        ((tab_ref, sink_ref, bkt_ref, q_ref, km_ref, kp_ref, kc_ref, vm_ref, vp_ref, vc_ref), (o_ref, lse_ref),
         (tbl, sink_row), mine) = rider.split(refs, 10, 2, 2)
        i = pl.program_id(0)
        rider.at_steps(mine, i == 0, i == NBLK // 2, i == NBLK - 1)

        @pl.when(i == 0)
        def _():
            _swa_tables(tab_ref, sink_ref, bkt_ref, tbl, sink_row)

        case = jnp.minimum(i, 2)
        k_cat = jnp.concatenate([km_ref[...], kp_ref[...], kc_ref[...]], axis=0)
        vt_cat = jnp.concatenate([vm_ref[...], vp_ref[...], vc_ref[...]], axis=1)
        raw = [lax.dot_general(k_cat, _stack_heads(q_ref, g, SCALE), NT_DIMS, preferred_element_type=F32)
               for g in range(2)]
        for g in range(2):
            s_t = raw[g] + tbl[case, g]
            sink = sink_row[g]
            m = jnp.maximum(_over_keys(jnp.max, s_t), sink)
            p_t = jnp.exp(s_t - m)
            l = _over_keys(jnp.sum, p_t) + jnp.exp(sink - m)
            o_t = jnp.dot(vt_cat[g * HALF:(g + 1) * HALF, :], p_t.astype(BF16), preferred_element_type=F32)
            _unstack_heads(o_t * (1.0 / l), g, o_ref, 1.0)
            lse = m + jnp.log(l)
            for hh in range(GROUP):
                lse_ref[GROUP * g + hh] = lse[:, hh * BLK:(hh + 1) * BLK]

    smem = pl.BlockSpec(memory_space=pltpu.SMEM)
    o_a, lse, *carried = pl.pallas_call(
        body, grid=(NBLK,),
        in_specs=[smem, smem, pl.BlockSpec((2, BLK, BLK), lambda i: (0, 0, 0)),
                  pl.BlockSpec((BLK, 512), lambda i: (i, QA))] + _seg_specs(True, KA) + _seg_specs(False, 0)
        + [HBM_SPEC] * len(rider.operands),
        out_specs=[pl.BlockSpec((BLK, 512), lambda i: (i, 0)),
                   pl.BlockSpec((N_HEADS, 1, BLK), lambda i: (0, 0, i))] + [HBM_SPEC] * len(rider.out_shapes),
        out_shape=[SDS((LP, 512), BF16), SDS((N_HEADS, 1, LP), F32)] + rider.out_shapes,
        scratch_shapes=[pltpu.VMEM((3, 2, N_KEY, QW), F32), pltpu.VMEM((2, 1, QW), F32)] + rider.scratch(),
        compiler_params=_cparams(("arbitrary",)), name="swa_fwd",
    )(rel_bias, sinks, bkt_t, proj, proj, proj, proj, vt_a, vt_a, vt_a, *rider.operands)
    return o_a, lse, carried


def _swa_bwd(proj, kt_a, o_a, dmix, lse, rel_bias, sinks, bkt_t):
    def body(tab_ref, sink_ref, bkt_ref, q_ref, km_ref, kp_ref, kc_ref, vm_ref, vp_ref, vc_ref,
             tm_ref, tp_ref, tc_ref, o_ref, do_ref, lse_ref,
             dq_ref, dk_ref, dv_ref, dbias_ref, dsink_ref, tbl, sink_row, acc, dsk):
        i = pl.program_id(0)

        @pl.when(i == 0)
        def _():
            _swa_tables(tab_ref, sink_ref, bkt_ref, tbl, sink_row)
            dk_ref[...] = jnp.zeros_like(dk_ref)
            dv_ref[...] = jnp.zeros_like(dv_ref)
            acc[...] = jnp.zeros_like(acc)
            dsk[...] = jnp.zeros_like(dsk)

        case = jnp.minimum(i, 2)
        first = jnp.full((BLK, QW), i, jnp.int32) == 1
        k_cat = jnp.concatenate([km_ref[...], kp_ref[...], kc_ref[...]], axis=0)
        v_cat = jnp.concatenate([vm_ref[...], vp_ref[...], vc_ref[...]], axis=0)
        kt_cat = jnp.concatenate([tm_ref[...], tp_ref[...], tc_ref[...]], axis=1)
        dk_cat = jnp.zeros((N_KEY, BLK), F32)
        dv_cat = jnp.zeros((N_KEY, BLK), F32)
        for g in range(2):
            d_parts = []
            for pair in range(2):
                cols = slice((2 * g + pair) * BLK, (2 * g + pair + 1) * BLK)
                prod_t = (do_ref[:, cols].astype(F32) * o_ref[:, cols].astype(F32)).T
                d_parts += [jnp.sum(prod_t[:HALF], axis=0, keepdims=True),
                            jnp.sum(prod_t[HALF:], axis=0, keepdims=True)]
            d_row = jnp.concatenate(d_parts, axis=1)
            lse_row = jnp.concatenate([lse_ref[GROUP * g + hh] for hh in range(GROUP)], axis=1)
            q_st = _stack_heads(q_ref, g, SCALE)
            do_st = _stack_heads(do_ref, g, 1.0)
            s_t = lax.dot_general(k_cat, q_st, NT_DIMS, preferred_element_type=F32) + tbl[case, g]
            p_t = jnp.exp(s_t - lse_row)
            dp_t = lax.dot_general(v_cat, do_st, NT_DIMS, preferred_element_type=F32)
            ds_t = p_t * (dp_t - d_row)
            dsk[g] += -jnp.exp(sink_row[g] - lse_row) * d_row
            acc[g, 0:BLK] += jnp.where(first, 0.0, ds_t[0:BLK])
            acc[g, BLK:2 * BLK] += jnp.where(first, ds_t[0:BLK], ds_t[BLK:2 * BLK])
            acc[g, 2 * BLK:N_KEY] += ds_t[2 * BLK:N_KEY]
            ds_b = ds_t.astype(BF16)
            dk_cat = dk_cat + jnp.dot(ds_b, q_st, preferred_element_type=F32)
            dv_cat = dv_cat + jnp.dot(p_t.astype(BF16), do_st, preferred_element_type=F32)
            dq_t = jnp.dot(kt_cat[g * HALF:(g + 1) * HALF, :], ds_b, preferred_element_type=F32)
            _unstack_heads(dq_t, g, dq_ref, SCALE)

        prev0 = pl.multiple_of(jnp.maximum(i - 1, 0) * BLK, BLK)
        cur0 = pl.multiple_of(i * BLK, BLK)
        for ref, cat in ((dk_ref, dk_cat), (dv_ref, dv_cat)):
            ref[0:BLK, :] += cat[0:BLK]
            ref[pl.ds(prev0, BLK), :] += cat[BLK:2 * BLK]
            ref[pl.ds(cur0, BLK), :] += cat[2 * BLK:N_KEY]

        @pl.when(i == NBLK - 1)
        def _():
            lane = lax.broadcasted_iota(jnp.int32, (1, BLK), 1)

            def per_bucket(b, carry):
                row = jnp.zeros((1, BLK), F32)
                for h in range(N_HEADS):
                    g, cols = h // GROUP, slice((h % GROUP) * BLK, (h % GROUP + 1) * BLK)
                    val = (jnp.sum(jnp.where(bkt_ref[0] == b, acc[g, 2 * BLK:N_KEY, cols], 0.0), keepdims=True)
                           + jnp.sum(jnp.where(bkt_ref[1] == b, acc[g, BLK:2 * BLK, cols], 0.0), keepdims=True))
                    row = jnp.where(lane == h, val, row)
                dbias_ref[pl.ds(b, 1), :] = row
                return carry

            lax.fori_loop(0, N_BUCKETS, per_bucket, 0)
            far = jnp.zeros((1, BLK), F32)
            dsr = jnp.zeros((1, BLK), F32)
            for h in range(N_HEADS):
                g, cols = h // GROUP, slice((h % GROUP) * BLK, (h % GROUP + 1) * BLK)
                far = jnp.where(lane == h, jnp.sum(acc[g, 0:BLK, cols], keepdims=True), far)
                dsr = jnp.where(lane == h, jnp.sum(dsk[g, :, cols], keepdims=True), dsr)
            dbias_ref[N_BUCKETS - 1:N_BUCKETS, :] += far
            dsink_ref[...] = dsr

    smem = pl.BlockSpec(memory_space=pltpu.SMEM)
    blk512 = lambda col: pl.BlockSpec((BLK, 512), lambda i: (i, col))
    full = lambda r, c: pl.BlockSpec((r, c), lambda i: (0, 0))
    return pl.pallas_call(
        body, grid=(NBLK,),
        in_specs=[smem, smem, pl.BlockSpec((2, BLK, BLK), lambda i: (0, 0, 0)), blk512(QA)]
        + _seg_specs(True, KA) + _seg_specs(True, VA) + _seg_specs(False, 0)
        + [blk512(0), blk512(0), pl.BlockSpec((N_HEADS, 1, BLK), lambda i: (0, 0, i))],
        out_specs=[blk512(0), full(LP, BLK), full(LP, BLK), full(N_BUCKETS, BLK), full(1, BLK)],
        out_shape=[SDS((LP, 512), BF16), SDS((LP, BLK), F32), SDS((LP, BLK), F32),
                   SDS((N_BUCKETS, BLK), F32), SDS((1, BLK), F32)],
        scratch_shapes=[pltpu.VMEM((3, 2, N_KEY, QW), F32), pltpu.VMEM((2, 1, QW), F32),
                        pltpu.VMEM((2, N_KEY, QW), F32), pltpu.VMEM((2, 1, QW), F32)],
        compiler_params=_cparams(("arbitrary",)), name="swa_bwd",
    )(rel_bias, sinks, bkt_t, proj, proj, proj, proj, proj, proj, proj, kt_a, kt_a, kt_a, o_a, dmix, lse)


def _local_step(x, tgt, meta, rel_bias, g_pre_mix, g_post_mix, g_pre_ffn, g_post_ffn, b_forget, sinks,
                w_in_b, out_rider, out_weight, ffn_rider, ffn_weights, ffn_grads, early_grads):
    bkt_t = jnp.asarray(_bucket_tables_t())
    h0 = jnp.concatenate([jnp.zeros((PAD_ROWS, D_MODEL), F32), meta, x], axis=0)
    b_p = jnp.pad(b_forget, ((0, 0), (0, BLK - N_HEADS)))

    hn1, proj, f = _pre_mix(h0, g_pre_mix, w_in_b)
    kt_a, vt_a = _swa_prep(proj)
    o_a, lse_a, carried_out = _swa_fwd(proj, vt_a, rel_bias, sinks, bkt_t, out_rider)
    w_out_b = out_weight(carried_out)
    cum = _forget_cumsum(f, b_p)
    ck_t = cum[:, :N_HEADS].T.reshape(N_HEADS, 1, LP)
    q_aug, k_aug, v_t = _fox_prep(proj, cum)
    o_b, lse_row, carried = _fox_fwd(q_aug, k_aug, v_t, ffn_rider)
    lse_b = lse_row.reshape(N_HEADS, LP, 1)
    w_gu_b, w_dn_b = ffn_weights(carried)
    a, h1, hn2 = _attn_out(o_a, o_b, w_out_b, h0, g_post_mix, g_pre_ffn)
    g, u, act = _ffn_up(hn2, w_gu_b)
    dff, dy, loss_blk, dg_post_ffn = _ffn_down_loss(act, w_dn_b, h1, tgt, g_post_ffn)

    dw_dn = _mm_tn([act], dff, FF_T, "dw_down")
    dg, du = _ffn_down_bwd(dff, w_dn_b, g, u)
    dw_gu = _dw_gate_up(hn2, dg, du)
    dh1, da, dg_pre_ffn, dg_post_mix, swapped = _ffn_up_bwd(dg, du, w_gu_b, h1, a, dy, g_pre_ffn, g_post_mix,
                                                           ffn_grads(dw_gu, dw_dn))
    dw_out = _mm_tn([o_a, o_b], da, D_MODEL, "dw_out")
    dmix = _attn_out_bwd(da, w_out_b)
    dq_b, dk_b, dv_b, dck, dcq, landed = _fox_bwd(proj, o_b, dmix, lse_b, ck_t, early_grads(swapped, dw_out))
    dq_a, dk_a, dv_a, dbias, dsink = _swa_bwd(proj, kt_a, o_a, dmix, lse_a, rel_bias, sinks, bkt_t)
    dcum = dcq - jnp.pad(dck.reshape(N_HEADS, LP).T, ((0, 0), (0, BLK - N_HEADS)))
    df, db = _forget_cumsum_bwd(dcum, f, b_p)
    dproj, dh0, dg_pre_mix = _pre_mix_bwd(dq_a, dq_b, dk_b, dv_b, dk_a, dv_a, df, w_in_b, h0, dh1, g_pre_mix)
    dw_in = _mm_tn([hn1], dproj, D_MODEL, "dw_in")

    return dict(loss=loss_blk[0, 0], grad_x=dh0[ROW0:], meta=dh0[PAD_ROWS:ROW0],
                rel_bias=dbias[:, :N_HEADS], ln_pre_mix=dg_pre_mix, ln_post_mix=dg_post_mix,
                ln_pre_ffn=dg_pre_ffn, ln_post_ffn=dg_post_ffn, b_forget=db[:, :N_HEADS],
                sinks=dsink[:, :N_HEADS], w_in=dw_in, w_out=dw_out, w_gate_up=dw_gu, w_down=dw_dn,
                landed=landed)


N_SMALL = 24
LOSS_ROW = 6


def _place():
    x, y, c = lax.axis_index("x"), lax.axis_index("y"), lax.axis_index("c")
    return x, y, c, [(1 - x, y), (x, 1 - y), (1 - x, 1 - y)]


def _run_alone(rider, name):
    a, b = len(rider.operands), len(rider.out_shapes)

    def body(*refs):
        mine = (refs[:a], refs[a:a + b], refs[a + b:])
        rider.first(*mine)
        rider.middle(*mine)
        rider.last(*mine)

    return pl.pallas_call(body, in_specs=[HBM_SPEC] * a, out_specs=[HBM_SPEC] * b, out_shape=rider.out_shapes,
                          scratch_shapes=rider.scratch(), name=name)(*rider.operands)


def _gather_rider(shards, own_too):
    n = len(shards)

    def own_copies(ins, outs, sems):
        x, y, _, _ = _place()
        return [pltpu.make_async_copy(ins[a], outs[a].at[2 * x + y], sems[2].at[a]) for a in range(n)] if own_too else []

    def copies(ins, outs, sems):
        send_sems, recv_sems = sems[:2]
        x, y, c, others = _place()
        chip = 2 * x + y
        sibling = (x, y, 1 - c)

        def rc(a, k, src, dst, to):
            return pltpu.make_async_remote_copy(src_ref=src, dst_ref=dst, send_sem=send_sems.at[6 * a + k],
                                                recv_sem=recv_sems.at[6 * a + k], device_id=to, device_id_type=MESH)

        pairs = [(a, j, ox, oy) for a in range(n) for j, (ox, oy) in enumerate(others)]
        return dict(
            sent=lambda: [rc(a, j, ins[a].at[c], outs[a].at[chip, c], (ox, oy, c)) for a, j, ox, oy in pairs],
            landed=lambda: [rc(a, j, outs[a].at[2 * ox + oy, c], outs[a].at[2 * ox + oy, c], sibling)
                            for a, j, ox, oy in pairs],
            passed=lambda: [rc(a, 3 + j, outs[a].at[2 * ox + oy, c], outs[a].at[2 * ox + oy, c], sibling)
                            for a, j, ox, oy in pairs],
            arriving=lambda: [rc(a, 3 + j, outs[a].at[2 * ox + oy, 1 - c], outs[a].at[2 * ox + oy, 1 - c], sibling)
                              for a, j, ox, oy in pairs])

    def first(*mine):
        for cp in copies(*mine)["sent"]() + own_copies(*mine):
            cp.start()

    def middle(*mine):
        kinds = copies(*mine)
        for got, cp in zip(kinds["landed"](), kinds["passed"]()):
            got.wait_recv()
            cp.start()

    def last(*mine):
        kinds = copies(*mine)
        for cp in kinds["arriving"]():
            cp.wait_recv()
        for cp in kinds["sent"]() + kinds["passed"]():
            cp.wait_send()
        for cp in own_copies(*mine):
            cp.wait()

    return _Rider(shards, [SDS((4,) + s.shape, s.dtype) for s in shards], [6 * n, 6 * n] + [n] * own_too,
                  first, middle, last)


def _swap_rider(grads):
    n = len(grads)

    def copies(ins, outs, sems):
        x, y, c, _ = _place()
        return [pltpu.make_async_remote_copy(
            src_ref=ins[a].at[s, 1 - c], dst_ref=outs[a].at[s], send_sem=sems[0].at[4 * a + s],
            recv_sem=sems[1].at[4 * a + s], device_id=(x, y, 1 - c), device_id_type=MESH)
            for a in range(n) for s in range(4)]

    def first(*mine):
        for cp in copies(*mine):
            cp.start()

    def middle(*mine):
        pass

    def last(*mine):
        for cp in copies(*mine):
            cp.wait()

    return _Rider(grads, [SDS((4,) + g.shape[2:], g.dtype) for g in grads], [4 * n, 4 * n], first, middle, last)


def _pair_sum(g, got, c_arr, name):
    rh, cc = got.shape[1:]

    def body(c_ref, g_ref, p_ref, o_ref):
        o_ref[0] = (g_ref[0, 0] + p_ref[0]).astype(BF16)

    grid_spec = pltpu.PrefetchScalarGridSpec(
        num_scalar_prefetch=1, grid=(4,),
        in_specs=[pl.BlockSpec((1, 1, rh, cc), lambda s, c_ref: (s, c_ref[0], 0, 0)),
                  pl.BlockSpec((1, rh, cc), lambda s, c_ref: (s, 0, 0))],
        out_specs=pl.BlockSpec((1, rh, cc), lambda s, c_ref: (s, 0, 0)))
    return pl.pallas_call(body, grid_spec=grid_spec, out_shape=SDS((4, rh, cc), BF16),
                          compiler_params=_cparams(("parallel",)), name=name)(c_arr, g, got)


def _exchange_rider(parts, small=None):
    n = len(parts)

    def copies(ins, outs, sems):
        x, y, c, others = _place()
        out = [pltpu.make_async_remote_copy(
            src_ref=ins[a].at[2 * ox + oy], dst_ref=outs[a].at[j], send_sem=sems[0].at[3 * a + j],
            recv_sem=sems[1].at[3 * a + j], device_id=(ox, oy, c), device_id_type=MESH)
            for a in range(n) for j, (ox, oy) in enumerate(others)]
        own = []
        if small is not None:
            me = 4 * x + 2 * y + c
            peers = [(x, y, 1 - c)] + [(ox, oy, c) for ox, oy in others] + [(ox, oy, 1 - c) for ox, oy in others]
            out += [pltpu.make_async_remote_copy(
                src_ref=ins[n], dst_ref=outs[n].at[me], send_sem=sems[2].at[k], recv_sem=sems[3].at[k],
                device_id=peer, device_id_type=MESH) for k, peer in enumerate(peers)]
            own = [pltpu.make_async_copy(ins[n], outs[n].at[me], sems[4].at[0])]
        return out, own

    def first(*mine):
        out, own = copies(*mine)
        for cp in own + out:
            cp.start()

    def middle(*mine):
        pass

    def last(*mine):
        out, own = copies(*mine)
        for cp in out + own:
            cp.wait()

    shapes = [SDS((3,) + p.shape[1:], p.dtype) for p in parts]
    if small is None:
        return _Rider(parts, shapes, [3 * n, 3 * n], first, middle, last)
    return _Rider(parts + [small], shapes + [SDS((8,) + small.shape, small.dtype)], [3 * n, 3 * n, 7, 7, 1],
                  first, middle, last)


def _chip_sum(parts, landed, chip_arr, name):
    rh, cc = landed.shape[1:]
    tr = rh // 2

    def body(chip_ref, own_ref, p_ref, o_ref):
        o_ref[...] = ((own_ref[0].astype(F32) + p_ref[0].astype(F32)) + p_ref[1].astype(F32)) + p_ref[2].astype(F32)

    grid_spec = pltpu.PrefetchScalarGridSpec(
        num_scalar_prefetch=1, grid=(2,),
        in_specs=[pl.BlockSpec((1, tr, cc), lambda i, chip_ref: (chip_ref[0], i, 0)),
                  pl.BlockSpec((3, tr, cc), lambda i, chip_ref: (0, i, 0))],
        out_specs=pl.BlockSpec((tr, cc), lambda i, chip_ref: (i, 0)))
    return pl.pallas_call(body, grid_spec=grid_spec, out_shape=SDS((rh, cc), F32),
                          compiler_params=_cparams(("parallel",)), name=name)(chip_arr, parts, landed)


def _device_sum(p):
    def body(p_ref, o_ref):
        acc = p_ref[0]
        for k in range(1, 8):
            acc = acc + p_ref[k]
        o_ref[...] = acc

    return pl.pallas_call(body, out_shape=SDS(p.shape[1:], F32), name="small_sum")(p)


def _join_halves(halves):
    n = len(halves)

    def body(*refs):
        ins, outs = refs[:n], refs[n:2 * n]
        send_sems, recv_sems = refs[2 * n:]
        x, y, c, _ = _place()
        copies = [pltpu.make_async_remote_copy(
            src_ref=ins[a], dst_ref=outs[a], send_sem=send_sems.at[a], recv_sem=recv_sems.at[a],
            device_id=(x, y, 1 - c), device_id_type=MESH) for a in range(n)]
        for cp in copies:
            cp.start()
        for cp in copies:
            cp.wait()

    return pl.pallas_call(
        body, in_specs=[HBM_SPEC] * n, out_specs=[HBM_SPEC] * n,
        out_shape=[SDS(h.shape, h.dtype) for h in halves],
        scratch_shapes=[pltpu.SemaphoreType.DMA((n,)), pltpu.SemaphoreType.DMA((n,))],
        name="join_halves")(*halves)


def _adamw(w, g, m, v, name):
    rows, cols = w.shape
    tr = rows if rows <= 352 else (256 if rows % 256 == 0 else 352)

    def body(w_ref, g_ref, m_ref, v_ref, d_ref, nm_ref, nv_ref):
        gg = g_ref[...]
        nm = ADAM_B1 * m_ref[...] + (1.0 - ADAM_B1) * gg
        nv = ADAM_B2 * v_ref[...] + (1.0 - ADAM_B2) * (gg * gg)
        nm_ref[...] = nm
        nv_ref[...] = nv
        m_hat = nm / (1.0 - ADAM_B1 ** ADAM_STEP)
        v_hat = nv / (1.0 - ADAM_B2 ** ADAM_STEP)
        d_ref[...] = -ADAM_LR * (m_hat / (jnp.sqrt(v_hat) + ADAM_EPS) + ADAM_WD * w_ref[...])

    blk = pl.BlockSpec((tr, cols), lambda i: (i, 0))
    return pl.pallas_call(
        body, grid=(rows // tr,), in_specs=[blk] * 4, out_specs=[blk] * 3,
        out_shape=[SDS((rows, cols), F32)] * 3,
        compiler_params=_cparams(("parallel",)), name=name)(w, g, m, v)


def _adamw_halves(w, mine, theirs, m, v, c_arr, name):
    rows, cols = w.shape
    rh = rows // 2
    tr = rh if rh <= 352 else 256
    nh = rh // tr

    def body(c_ref, w_ref, mine_ref, theirs_ref, m_ref, v_ref, g_ref, d_ref, nm_ref, nv_ref):
        own = jnp.full((tr, cols), pl.program_id(0), jnp.int32) == c_ref[0]
        gg = jnp.where(own, mine_ref[...], theirs_ref[...])
        g_ref[...] = gg
        nm = ADAM_B1 * m_ref[...] + (1.0 - ADAM_B1) * gg
        nv = ADAM_B2 * v_ref[...] + (1.0 - ADAM_B2) * (gg * gg)
        nm_ref[...] = nm
        nv_ref[...] = nv
        m_hat = nm / (1.0 - ADAM_B1 ** ADAM_STEP)
        v_hat = nv / (1.0 - ADAM_B2 ** ADAM_STEP)
        d_ref[...] = -ADAM_LR * (m_hat / (jnp.sqrt(v_hat) + ADAM_EPS) + ADAM_WD * w_ref[...])

    whole = pl.BlockSpec((tr, cols), lambda hh, i, c_ref: (hh * nh + i, 0))
    part = pl.BlockSpec((tr, cols), lambda hh, i, c_ref: (i, 0))
    grid_spec = pltpu.PrefetchScalarGridSpec(
        num_scalar_prefetch=1, grid=(2, nh), in_specs=[whole, part, part, whole, whole], out_specs=[whole] * 4)
    return pl.pallas_call(body, grid_spec=grid_spec, out_shape=[SDS((rows, cols), F32)] * 4,
                          compiler_params=_cparams(("parallel", "parallel")), name=name)(c_arr, w, mine, theirs, m, v)


def _pack_small(pre_mix, post_mix, pre_ffn, post_ffn, rel_bias, b_forget, sinks):
    def at(row, v):
        return jnp.pad(v, ((row, 7 - row), (0, D_MODEL - v.shape[1])))
    return (at(0, pre_mix) + at(1, post_mix) + at(2, pre_ffn) + at(3, post_ffn)
            + at(4, rel_bias.reshape(1, N_BUCKETS * N_HEADS)) + at(5, jnp.concatenate([b_forget, sinks], axis=1)))


def _unpack_small(p):
    return dict(ln_pre_mix=p[0:1], ln_post_mix=p[1:2], ln_pre_ffn=p[2:3], ln_post_ffn=p[3:4],
                rel_bias=p[4, :N_BUCKETS * N_HEADS].reshape(N_BUCKETS, N_HEADS),
                b_forget=p[5:6, 0:N_HEADS], sinks=p[5:6, N_HEADS:2 * N_HEADS])


WEIGHTS = ("meta_tokens", "rel_bias", "ln_pre_mix", "ln_post_mix", "ln_pre_ffn", "ln_post_ffn",
           "w_in", "b_forget", "sinks", "w_out", "w_gate_up", "w_down")


def kernel(x, meta_tokens, rel_bias, ln_pre_mix, ln_post_mix, ln_pre_ffn, ln_post_ffn, w_in, b_forget, sinks, w_out, w_gate_up, w_down, loss_target, m_meta_tokens, m_rel_bias, m_ln_pre_mix, m_ln_post_mix, m_ln_pre_ffn, m_ln_post_ffn, m_w_in, m_b_forget, m_sinks, m_w_out, m_w_gate_up, m_w_down, v_meta_tokens, v_rel_bias, v_ln_pre_mix, v_ln_post_mix, v_ln_pre_ffn, v_ln_post_ffn, v_w_in, v_b_forget, v_sinks, v_w_out, v_w_gate_up, v_w_down):
    xi, yi, ci = lax.axis_index("x"), lax.axis_index("y"), lax.axis_index("c")
    chip = 2 * xi + yi
    c_arr = jnp.reshape(ci, (1,)).astype(jnp.int32)

    def halves(w, dtype):
        return w.astype(dtype).reshape(2, w.shape[0] // 2, w.shape[1])

    def with_own(gathered, shards):
        return [lax.dynamic_update_slice(got, own[None], (chip, 0, 0, 0)) for got, own in zip(gathered, shards)]

    shards = [halves(w_in[0], BF16), halves(meta_tokens, F32)]
    gw_in, g_meta = with_own(_run_alone(_gather_rider(shards, False), "gather_mixer_weights"), shards)
    out_shards = [halves(w_out[0], BF16)]
    ffn_shards = [halves(w_gate_up[0], BF16), halves(w_down[0], BF16)]

    def ffn_weights(carried):
        gw_gu, gw_dn = carried
        w_gu_b = gw_gu.reshape(4, D_MODEL, FF_T).transpose(1, 0, 2).reshape(D_MODEL, 2 * D_FF)
        return w_gu_b, gw_dn.reshape(D_FF, D_MODEL)

    early = {}

    def ffn_grads(dw_gu, dw_dn):
        early["grads"] = [dw_gu.reshape(4, 2, 512, FF_T), dw_dn.reshape(4, 2, 352, D_MODEL)]
        return _swap_rider(early["grads"])

    def early_grads(swapped, dw_out):
        g_out = dw_out.reshape(4, 2, 128, D_MODEL)
        grads = [g_out] + early["grads"]
        got = list(_run_alone(_swap_rider([g_out]), "swap_halves_out")) + list(swapped)
        early["parts"] = [_pair_sum(g, p, c_arr, "pair_sum_%d" % a) for a, (g, p) in enumerate(zip(grads, got))]
        return _exchange_rider(early["parts"])
    w_in_all = gw_in.reshape(4, D_MODEL, D_PROJ // 4).transpose(1, 0, 2).reshape(D_MODEL, D_PROJ)
    w_in_b = jnp.concatenate(
        [w_in_all[:, 0:512], w_in_all[:, 768:1280], w_in_all[:, 1280:1792], w_in_all[:, 1792:2304],
         w_in_all[:, 512:640], w_in_all[:, 640:768], w_in_all[:, 2304:2312],
         jnp.zeros((D_MODEL, D_PROJ_P - D_PROJ), BF16)], axis=1)
    meta_all = g_meta.reshape(4, N_META, D_MODEL // 4).transpose(1, 0, 2).reshape(N_META, D_MODEL)

    loc = _local_step(x[0], loss_target[0], meta_all, rel_bias, ln_pre_mix, ln_post_mix, ln_pre_ffn, ln_post_ffn,
                      b_forget, sinks, w_in_b, _gather_rider(out_shards, True),
                      lambda carried: carried[0].reshape(D_MODEL, D_MODEL),
                      _gather_rider(ffn_shards, True), ffn_weights, ffn_grads, early_grads)

    n = loc["w_in"]
    dw_in = jnp.concatenate([n[:, 0:512], n[:, 2048:2176], n[:, 2176:2304], n[:, 512:1024], n[:, 1024:1536],
                             n[:, 1536:2048], n[:, 2304:2312]], axis=1)
    dw_in = dw_in.reshape(D_MODEL, 4, D_PROJ // 4).transpose(1, 0, 2).reshape(4, 2, 512, D_PROJ // 4)
    small = jnp.concatenate(
        [_pack_small(loc["ln_pre_mix"], loc["ln_post_mix"], loc["ln_pre_ffn"], loc["ln_post_ffn"],
                     loc["rel_bias"], loc["b_forget"], loc["sinks"])
         + jnp.pad(loc["loss"].reshape(1, 1), ((LOSS_ROW, 7 - LOSS_ROW), (0, D_MODEL - 1))), loc["meta"]], axis=0)

    (got_in,) = _run_alone(_swap_rider([dw_in]), "swap_halves_late")
    part_in = _pair_sum(dw_in, got_in, c_arr, "pair_sum_late")
    landed_in, small_all = _run_alone(_exchange_rider([part_in], small), "exchange_late")
    chip_arr = jnp.reshape(chip, (1,)).astype(jnp.int32)
    parts = [part_in] + early["parts"]
    landed = [landed_in] + loc["landed"]
    mine = [_chip_sum(p, l, chip_arr, "chip_sum_%d" % a) for a, (p, l) in enumerate(zip(parts, landed))]
    small_sum = _device_sum(small_all)
    theirs = _join_halves(mine)
    g_meta_tokens = lax.dynamic_slice(small_sum[8:N_SMALL], (0, chip * (D_MODEL // 4)), (N_META, D_MODEL // 4))
    g_small = small_sum[0:8]

    grad = _unpack_small(g_small)
    grad.update(meta_tokens=g_meta_tokens)
    delta, new_m, new_v = {}, {}, {}
    big = dict(w_in=(w_in, m_w_in, v_w_in), w_out=(w_out, m_w_out, v_w_out),
               w_gate_up=(w_gate_up, m_w_gate_up, v_w_gate_up), w_down=(w_down, m_w_down, v_w_down))
    for (name, (w, m, v)), g_mine, g_theirs in zip(big.items(), mine, theirs):
        g, d, nm, nv = _adamw_halves(w[0], g_mine, g_theirs, m[0], v[0], c_arr, "adamw_" + name)
        grad[name], delta[name], new_m[name], new_v[name] = g[None], d[None], nm[None], nv[None]
    delta["meta_tokens"], new_m["meta_tokens"], new_v["meta_tokens"] = _adamw(
        meta_tokens, g_meta_tokens, m_meta_tokens, v_meta_tokens, "adamw_meta")
    d, nm, nv = _adamw(
        _pack_small(ln_pre_mix, ln_post_mix, ln_pre_ffn, ln_post_ffn, rel_bias, b_forget, sinks), g_small,
        _pack_small(m_ln_pre_mix, m_ln_post_mix, m_ln_pre_ffn, m_ln_post_ffn, m_rel_bias, m_b_forget, m_sinks),
        _pack_small(v_ln_pre_mix, v_ln_post_mix, v_ln_pre_ffn, v_ln_post_ffn, v_rel_bias, v_b_forget, v_sinks),
        "adamw_small")
    delta.update(_unpack_small(d))
    new_m.update(_unpack_small(nm))
    new_v.update(_unpack_small(nv))

    loss = small_sum[LOSS_ROW, 0]
    return (loss,loc["grad_x"][None], *[grad[k] for k in WEIGHTS], *[delta[k] for k in WEIGHTS],
            *[new_m[k] for k in WEIGHTS], *[new_v[k] for k in WEIGHTS])
```

```python
import math

import numpy as np
import jax
import jax.numpy as jnp
from jax import lax
from jax.experimental import pallas as pl
from jax.experimental.pallas import tpu as pltpu

F32 = jnp.float32
BF16 = jnp.bfloat16
MESH = pl.DeviceIdType.MESH
SDS = jax.ShapeDtypeStruct

D_MODEL = 1024
SEQ = 4096
N_META = 16
N_HEADS = 8
HALF = 64
D_FF = 2816
N_BUCKETS = 32
EPS = 1e-6
NEG = -1e30
SCALE = 0.125
PAD_ROWS = 112
ROW0 = PAD_ROWS + N_META
LP = ROW0 + SEQ
BLK = 128
NBLK = LP // BLK
TM = 384
NT = LP // TM
TM_PURE = LP // 2
TM_MID = LP // 4
TM_EPI = LP // 6
TN = 256
D_PROJ = 2312
D_PROJ_P = 2432
D_QKV = 2304
FF_T = 1408
VMEM_LIMIT = 56 * 1024 * 1024

ADAM_LR = 0.001
ADAM_B1 = 0.9
ADAM_B2 = 0.999
ADAM_EPS = 1e-08
ADAM_WD = 0.01
ADAM_STEP = 10

QA, QB, KB, VB = 0, 1, 2, 3
KA, VA = 16, 17

NT_DIMS = (((1,), (1,)), ((), ()))
TN_DIMS = (((0,), (0,)), ((), ()))


def _cparams(sem):
    return pltpu.CompilerParams(dimension_semantics=sem, vmem_limit_bytes=VMEM_LIMIT)


def _t5_bucket_np(d):
    n = np.maximum(d, 0).astype(np.int32)
    nf = np.maximum(n, 1).astype(np.float32)
    large = 16 + (np.log(nf / np.float32(16)) / np.float32(math.log(8.0)) * np.float32(16)).astype(np.int32)
    large = np.minimum(large, N_BUCKETS - 1)
    return np.where(n < 16, n, large).astype(np.int32)


def _bucket_tables():
    qi = np.arange(BLK)[:, None]
    ki = np.arange(BLK)[None, :]
    return np.stack([_t5_bucket_np(qi - ki), _t5_bucket_np(qi - ki + BLK)])


def _rms(x):
    return lax.rsqrt(jnp.mean(x * x, axis=-1, keepdims=True) + EPS)


def _rms_bwd(n, r, gdy):
    return r * (gdy - n * jnp.mean(n * gdy, axis=-1, keepdims=True))


def _pre_mix(h0, gain, w_in_b):
    half = D_QKV // 2

    def body(h_ref, g_ref, w_ref, hn_ref, proj_ref, f_ref):
        x = h_ref[...]
        hn = (x * _rms(x) * g_ref[...]).astype(BF16)
        hn_ref[...] = hn
        proj_ref[:, :half] = jnp.dot(hn, w_ref[:, :half], preferred_element_type=F32).astype(BF16)
        p = jnp.dot(hn, w_ref[:, half:], preferred_element_type=F32)
        proj_ref[:, half:] = p[:, :half].astype(BF16)
        f_ref[...] = p[:, half:]

    return pl.pallas_call(
        body, grid=(LP // TM_MID,),
        in_specs=[pl.BlockSpec((TM_MID, D_MODEL), lambda i: (i, 0)),
                  pl.BlockSpec((1, D_MODEL), lambda i: (0, 0)),
                  pl.BlockSpec((D_MODEL, D_PROJ_P), lambda i: (0, 0))],
        out_specs=[pl.BlockSpec((TM_MID, D_MODEL), lambda i: (i, 0)),
                   pl.BlockSpec((TM_MID, D_QKV), lambda i: (i, 0)),
                   pl.BlockSpec((TM_MID, BLK), lambda i: (i, 0))],
        out_shape=[SDS((LP, D_MODEL), BF16), SDS((LP, D_QKV), BF16), SDS((LP, BLK), F32)],
        compiler_params=_cparams(("parallel",)), name="pre_mix")(h0, gain, w_in_b)


def _attn_out(o_a, o_b, w_out_b, h0, g_post, g_pre_ffn):
    def body(oa_ref, ob_ref, w_ref, h0_ref, gp_ref, gf_ref, a_ref, h1_ref, hn2_ref):
        a = (jnp.dot(oa_ref[...], w_ref[0:512, :], preferred_element_type=F32)
             + jnp.dot(ob_ref[...], w_ref[512:1024, :], preferred_element_type=F32))
        a_ref[...] = a
        h1 = h0_ref[...] + a * _rms(a) * gp_ref[...]
        h1_ref[...] = h1
        hn2_ref[...] = (h1 * _rms(h1) * gf_ref[...]).astype(BF16)

    row = lambda w: pl.BlockSpec((TM_EPI, w), lambda i: (i, 0))
    vec = pl.BlockSpec((1, D_MODEL), lambda i: (0, 0))
    return pl.pallas_call(
        body, grid=(LP // TM_EPI,),
        in_specs=[row(512), row(512), pl.BlockSpec((D_MODEL, D_MODEL), lambda i: (0, 0)), row(D_MODEL), vec, vec],
        out_specs=[row(D_MODEL), row(D_MODEL), row(D_MODEL)],
        out_shape=[SDS((LP, D_MODEL), F32), SDS((LP, D_MODEL), F32), SDS((LP, D_MODEL), BF16)],
        compiler_params=_cparams(("parallel",)), name="attn_out")(o_a, o_b, w_out_b, h0, g_post, g_pre_ffn)


def _ffn_up(hn2, w_gu_b):
    def body(x_ref, wg_ref, wu_ref, g_ref, u_ref, act_ref):
        x = x_ref[...]
        g = jnp.dot(x, wg_ref[...], preferred_element_type=F32)
        u = jnp.dot(x, wu_ref[...], preferred_element_type=F32)
        g_ref[...] = g.astype(BF16)
        u_ref[...] = u.astype(BF16)
        act_ref[...] = (g * (1.0 / (1.0 + jnp.exp(-g))) * u).astype(BF16)

    out = pl.BlockSpec((TM_PURE, TN), lambda i, j: (i, j))
    return pl.pallas_call(
        body, grid=(LP // TM_PURE, D_FF // TN),
        in_specs=[pl.BlockSpec((TM_PURE, D_MODEL), lambda i, j: (i, 0)),
                  pl.BlockSpec((D_MODEL, TN), lambda i, j: (0, j)),
                  pl.BlockSpec((D_MODEL, TN), lambda i, j: (0, j + D_FF // TN))],
        out_specs=[out, out, out],
        out_shape=[SDS((LP, D_FF), BF16)] * 3,
        compiler_params=_cparams(("parallel", "parallel")), name="ffn_up")(hn2, w_gu_b, w_gu_b)


def _ffn_down_loss(act, w_dn_b, h1, tgt, g_post_ffn):
    def body(act_ref, w_ref, h1_ref, t0_ref, t1_ref, t2_ref, g_ref, dff_ref, dy_ref, loss_ref, dg_ref):
        i = pl.program_id(0)
        target = jnp.concatenate([t0_ref[...], t1_ref[...], t2_ref[...]], axis=0)

        @pl.when(i == 0)
        def _():
            loss_ref[...] = jnp.zeros_like(loss_ref)
            dg_ref[...] = jnp.zeros_like(dg_ref)

        ff = jnp.dot(act_ref[...], w_ref[...], preferred_element_type=F32)
        r = _rms(ff)
        n = ff * r
        g = g_ref[...]
        y = h1_ref[...] + n * g
        rows = i * TM + lax.broadcasted_iota(jnp.int32, (TM, D_MODEL), 0)
        diff = jnp.where(rows >= ROW0, y - target, 0.0)
        loss_ref[...] += 0.5 * jnp.sum(diff * diff) / D_MODEL
        dy = diff / D_MODEL
        dy_ref[...] = dy
        dg_ref[...] += jnp.sum(dy * n, axis=0, keepdims=True)
        dff_ref[...] = _rms_bwd(n, r, g * dy).astype(BF16)

    row = pl.BlockSpec((TM, D_MODEL), lambda i: (i, 0))
    tblk = lambda j: pl.BlockSpec((BLK, D_MODEL), lambda i: (jnp.maximum(3 * i - 1 + j, 0), 0))
    return pl.pallas_call(
        body, grid=(NT,),
        in_specs=[pl.BlockSpec((TM, D_FF), lambda i: (i, 0)), pl.BlockSpec((D_FF, D_MODEL), lambda i: (0, 0)),
                  row, tblk(0), tblk(1), tblk(2), pl.BlockSpec((1, D_MODEL), lambda i: (0, 0))],
        out_specs=[row, row, pl.BlockSpec((8, BLK), lambda i: (0, 0)), pl.BlockSpec((1, D_MODEL), lambda i: (0, 0))],
        out_shape=[SDS((LP, D_MODEL), BF16), SDS((LP, D_MODEL), F32), SDS((8, BLK), F32), SDS((1, D_MODEL), F32)],
        compiler_params=_cparams(("arbitrary",)), name="ffn_down_loss")(act, w_dn_b, h1, tgt, tgt, tgt, g_post_ffn)


def _ffn_down_bwd(dff, w_dn_b, g, u):
    def body(d_ref, w_ref, g_ref, u_ref, dg_ref, du_ref):
        dact = lax.dot_general(d_ref[...], w_ref[...], NT_DIMS, preferred_element_type=F32)
        gg = g_ref[...].astype(F32)
        sig = 1.0 / (1.0 + jnp.exp(-gg))
        dg_ref[...] = (dact * u_ref[...].astype(F32) * sig * (1.0 + gg * (1.0 - sig))).astype(BF16)
        du_ref[...] = (dact * gg * sig).astype(BF16)

    blk = pl.BlockSpec((TM_PURE, TN), lambda i, j: (i, j))
    return pl.pallas_call(
        body, grid=(LP // TM_PURE, D_FF // TN),
        in_specs=[pl.BlockSpec((TM_PURE, D_MODEL), lambda i, j: (i, 0)),
                  pl.BlockSpec((TN, D_MODEL), lambda i, j: (j, 0)), blk, blk],
        out_specs=[blk, blk],
        out_shape=[SDS((LP, D_FF), BF16)] * 2,
        compiler_params=_cparams(("parallel", "parallel")), name="ffn_down_bwd")(dff, w_dn_b, g, u)


def _ffn_up_bwd(dg, du, w_gu_b, h1, a, dy, g_pre_ffn, g_post_mix):
    def body(dg_ref, du_ref, w_ref, h1_ref, a_ref, dy_ref, gf_ref, gp_ref,
             dh1_ref, da_ref, dgf_ref, dgp_ref, acc):
        i = pl.program_id(0)
        s = pl.program_id(1)

        @pl.when((i == 0) & (s == 0))
        def _():
            dgf_ref[...] = jnp.zeros_like(dgf_ref)
            dgp_ref[...] = jnp.zeros_like(dgp_ref)

        @pl.when(s == 0)
        def _():
            acc[...] = jnp.zeros_like(acc)

        @pl.when(s < 2)
        def _():
            acc[...] += lax.dot_general(dg_ref[...], w_ref[...], NT_DIMS, preferred_element_type=F32)

        @pl.when(s >= 2)
        def _():
            acc[...] += lax.dot_general(du_ref[...], w_ref[...], NT_DIMS, preferred_element_type=F32)

        @pl.when(s == 3)
        def _():
            dhn2 = acc[...]
            h1 = h1_ref[...]
            r2 = _rms(h1)
            n2 = h1 * r2
            dgf_ref[...] += jnp.sum(dhn2 * n2, axis=0, keepdims=True)
            dh1 = dy_ref[...] + _rms_bwd(n2, r2, gf_ref[...] * dhn2)
            dh1_ref[...] = dh1
            av = a_ref[...]
            ra = _rms(av)
            na = av * ra
            dgp_ref[...] += jnp.sum(dh1 * na, axis=0, keepdims=True)
            da_ref[...] = _rms_bwd(na, ra, gp_ref[...] * dh1).astype(BF16)

    row = pl.BlockSpec((TM_EPI, D_MODEL), lambda i, s: (i, 0))
    vec = pl.BlockSpec((1, D_MODEL), lambda i, s: (0, 0))
    return pl.pallas_call(
        body, grid=(LP // TM_EPI, 4),
        in_specs=[pl.BlockSpec((TM_EPI, FF_T), lambda i, s: (i, jnp.minimum(s, 1))),
                  pl.BlockSpec((TM_EPI, FF_T), lambda i, s: (i, jnp.maximum(s - 2, 0))),
                  pl.BlockSpec((D_MODEL, FF_T), lambda i, s: (0, s)),
                  row, row, row, vec, vec],
        out_specs=[row, row, vec, vec],
        out_shape=[SDS((LP, D_MODEL), F32), SDS((LP, D_MODEL), BF16), SDS((1, D_MODEL), F32), SDS((1, D_MODEL), F32)],
        scratch_shapes=[pltpu.VMEM((TM_EPI, D_MODEL), F32)],
        compiler_params=_cparams(("arbitrary", "arbitrary")), name="ffn_up_bwd",
    )(dg, du, w_gu_b, h1, a, dy, g_pre_ffn, g_post_mix)


def _attn_out_bwd(da, w_out_b):
    def body(d_ref, w_ref, o_ref):
        o_ref[...] = lax.dot_general(d_ref[...], w_ref[...], NT_DIMS, preferred_element_type=F32).astype(BF16)

    row = pl.BlockSpec((TM_PURE, D_MODEL), lambda i: (i, 0))
    return pl.pallas_call(
        body, grid=(LP // TM_PURE,),
        in_specs=[row, pl.BlockSpec((D_MODEL, D_MODEL), lambda i: (0, 0))],
        out_specs=row, out_shape=SDS((LP, D_MODEL), BF16),
        compiler_params=_cparams(("parallel",)), name="attn_out_bwd")(da, w_out_b)


def _pre_mix_bwd(dq_a, dq_b, dk_b, dv_b, dk_a, dv_a, df, w_in_b, h0, dh1, g_pre_mix):
    def body(qa_ref, qb_ref, kb_ref, vb_ref, ka_ref, va_ref, f_ref, w_ref, h0_ref, dh1_ref, g_ref,
             dproj_ref, dh0_ref, dg_ref):
        i = pl.program_id(0)

        @pl.when(i == 0)
        def _():
            dg_ref[...] = jnp.zeros_like(dg_ref)

        dproj = jnp.concatenate(
            [qa_ref[...], (qb_ref[...] * SCALE).astype(BF16), kb_ref[...], vb_ref[...],
             ka_ref[...].astype(BF16), va_ref[...].astype(BF16), f_ref[...].astype(BF16)], axis=1)
        dproj_ref[...] = dproj
        dhn = lax.dot_general(dproj, w_ref[...], NT_DIMS, preferred_element_type=F32)
        x = h0_ref[...]
        r = _rms(x)
        n = x * r
        dg_ref[...] += jnp.sum(dhn * n, axis=0, keepdims=True)
        dh0_ref[...] = dh1_ref[...] + _rms_bwd(n, r, g_ref[...] * dhn)

    row = lambda w: pl.BlockSpec((TM_EPI, w), lambda i: (i, 0))
    vec = pl.BlockSpec((1, D_MODEL), lambda i: (0, 0))
    return pl.pallas_call(
        body, grid=(LP // TM_EPI,),
        in_specs=[row(512), row(512), row(512), row(512), row(BLK), row(BLK), row(BLK),
                  pl.BlockSpec((D_MODEL, D_PROJ_P), lambda i: (0, 0)), row(D_MODEL), row(D_MODEL), vec],
        out_specs=[row(D_PROJ_P), row(D_MODEL), vec],
        out_shape=[SDS((LP, D_PROJ_P), BF16), SDS((LP, D_MODEL), F32), SDS((1, D_MODEL), F32)],
        compiler_params=_cparams(("arbitrary",)), name="pre_mix_bwd",
    )(dq_a, dq_b, dk_b, dv_b, dk_a, dv_a, df, w_in_b, h0, dh1, g_pre_mix)


def _mm_tn(parts, b, tm, name, out_dtype=F32):
    widths = [p.shape[1] for p in parts]
    m_total = sum(widths)
    n = b.shape[1]
    whole = len(parts) > 1
    n_k = LP // TM_MID
    assert (tm == m_total) if whole else (m_total % tm == 0)

    def body(*refs):
        a_refs, b_ref, o_ref, acc = refs[:-3], refs[-3], refs[-2], refs[-1]
        k = pl.program_id(1)

        @pl.when(k == 0)
        def _():
            acc[...] = jnp.zeros_like(acc)
        a = a_refs[0][...] if not whole else jnp.concatenate([r[...] for r in a_refs], axis=1)
        acc[...] += lax.dot_general(a, b_ref[...], TN_DIMS, preferred_element_type=F32)

        @pl.when(k == n_k - 1)
        def _():
            o_ref[...] = acc[...].astype(out_dtype)

    a_specs = ([pl.BlockSpec((TM_MID, w), lambda mi, k: (k, 0)) for w in widths] if whole
               else [pl.BlockSpec((TM_MID, tm), lambda mi, k: (k, mi))])
    return pl.pallas_call(
        body, grid=(m_total // tm, n_k),
        in_specs=a_specs + [pl.BlockSpec((TM_MID, n), lambda mi, k: (k, 0))],
        out_specs=pl.BlockSpec((tm, n), lambda mi, k: (mi, 0)),
        out_shape=SDS((m_total, n), out_dtype),
        scratch_shapes=[pltpu.VMEM((tm, n), F32)],
        compiler_params=_cparams(("parallel", "arbitrary")), name=name)(*parts, b)


def _dw_gate_up(hn2, dg, du):
    n_k = LP // TM_MID

    def body(a_ref, dg_ref, du_ref, o_ref, acc):
        s = pl.program_id(0)
        k = pl.program_id(1)

        @pl.when(k == 0)
        def _():
            acc[...] = jnp.zeros_like(acc)

        @pl.when(s < 2)
        def _():
            acc[...] += lax.dot_general(a_ref[...], dg_ref[...], TN_DIMS, preferred_element_type=F32)

        @pl.when(s >= 2)
        def _():
            acc[...] += lax.dot_general(a_ref[...], du_ref[...], TN_DIMS, preferred_element_type=F32)

        @pl.when(k == n_k - 1)
        def _():
            o_ref[0] = acc[...].astype(BF16)

    return pl.pallas_call(
        body, grid=(4, n_k),
        in_specs=[pl.BlockSpec((TM_MID, D_MODEL), lambda s, k: (k, 0)),
                  pl.BlockSpec((TM_MID, FF_T), lambda s, k: (k, jnp.minimum(s, 1))),
                  pl.BlockSpec((TM_MID, FF_T), lambda s, k: (k, jnp.maximum(s - 2, 0)))],
        out_specs=pl.BlockSpec((1, D_MODEL, FF_T), lambda s, k: (s, 0, 0)),
        out_shape=SDS((4, D_MODEL, FF_T), BF16),
        scratch_shapes=[pltpu.VMEM((D_MODEL, FF_T), F32)],
        compiler_params=_cparams(("parallel", "arbitrary")), name="dw_gate_up")(hn2, dg, du)


def _split3(x):
    hi = x.astype(BF16)
    r1 = x - hi.astype(F32)
    mid = r1.astype(BF16)
    lo = (r1 - mid.astype(F32)).astype(BF16)
    return hi, mid, lo


def _tri_matmul(tri, x):
    hi, mid, lo = _split3(x)
    dot = lambda t: jnp.dot(tri, t, preferred_element_type=F32)
    return dot(hi) + dot(mid) + dot(lo)


def _forget_cumsum(f, b_forget_p):
    def body(f_ref, b_ref, cum_ref, carry):
        i = pl.program_id(0)

        @pl.when(i == 0)
        def _():
            carry[...] = jnp.zeros_like(carry)

        z = f_ref[...] + b_ref[...]
        ls = jnp.minimum(z, 0.0) - jnp.log(1.0 + jnp.exp(-jnp.abs(z)))
        rows = i * TM + lax.broadcasted_iota(jnp.int32, (TM, BLK), 0)
        ls = jnp.where(rows >= PAD_ROWS, ls, 0.0)
        r = lax.broadcasted_iota(jnp.int32, (TM, TM), 0)
        c = lax.broadcasted_iota(jnp.int32, (TM, TM), 1)
        tri = (c <= r).astype(BF16)
        cum = _tri_matmul(tri, ls) + carry[...]
        cum_ref[...] = cum
        carry[...] = cum[TM - 1:TM, :]

    return pl.pallas_call(
        body, grid=(NT,),
        in_specs=[pl.BlockSpec((TM, BLK), lambda i: (i, 0)), pl.BlockSpec((1, BLK), lambda i: (0, 0))],
        out_specs=pl.BlockSpec((TM, BLK), lambda i: (i, 0)),
        out_shape=SDS((LP, BLK), F32),
        scratch_shapes=[pltpu.VMEM((1, BLK), F32)],
        compiler_params=_cparams(("arbitrary",)), name="forget_cumsum")(f, b_forget_p)


def _forget_cumsum_bwd(dcum, f, b_forget_p):
    def body(d_ref, f_ref, b_ref, df_ref, db_ref, carry):
        i = pl.program_id(0)

        @pl.when(i == 0)
        def _():
            carry[...] = jnp.zeros_like(carry)
            db_ref[...] = jnp.zeros_like(db_ref)

        blk = NT - 1 - i
        r = lax.broadcasted_iota(jnp.int32, (TM, TM), 0)
        c = lax.broadcasted_iota(jnp.int32, (TM, TM), 1)
        tri = (c >= r).astype(BF16)
        d = d_ref[...]
        dls = _tri_matmul(tri, d) + carry[...]
        carry[...] = dls[0:1, :]
        z = f_ref[...] + b_ref[...]
        rows = blk * TM + lax.broadcasted_iota(jnp.int32, (TM, BLK), 0)
        df = jnp.where(rows >= PAD_ROWS, dls / (1.0 + jnp.exp(z)), 0.0)
        df_ref[...] = df
        db_ref[...] += jnp.sum(df, axis=0, keepdims=True)

    rev = pl.BlockSpec((TM, BLK), lambda i: (NT - 1 - i, 0))
    vec = pl.BlockSpec((1, BLK), lambda i: (0, 0))
    return pl.pallas_call(
        body, grid=(NT,),
        in_specs=[rev, rev, vec],
        out_specs=[rev, vec],
        out_shape=[SDS((LP, BLK), F32), SDS((1, BLK), F32)],
        scratch_shapes=[pltpu.VMEM((1, BLK), F32)],
        compiler_params=_cparams(("arbitrary",)), name="forget_cumsum_bwd")(dcum, f, b_forget_p)


def _lane_half(rows):
    return lax.broadcasted_iota(jnp.int32, (rows, BLK), 1) // HALF


def _fox_valid(qi, kj):
    qrow = qi * TM + lax.broadcasted_iota(jnp.int32, (TM, TM), 0)
    krow = kj * TM + lax.broadcasted_iota(jnp.int32, (TM, TM), 1)
    return (krow <= qrow) & ((krow >= PAD_ROWS) | (qrow < PAD_ROWS))


class _Rider:
    def __init__(self, operands, out_shapes, sem_counts, first, middle, last):
        self.operands, self.out_shapes, self.sem_counts = list(operands), list(out_shapes), list(sem_counts)
        self.first, self.middle, self.last = first, middle, last

    def scratch(self):
        return [pltpu.SemaphoreType.DMA((k,)) for k in self.sem_counts]

    def split(self, refs, n_in, n_out, n_scratch):
        a, b = len(self.operands), len(self.out_shapes)
        ins, mine_in = refs[:n_in], refs[n_in:n_in + a]
        outs, mine_out = refs[n_in + a:n_in + a + n_out], refs[n_in + a + n_out:n_in + a + n_out + b]
        rest = refs[n_in + a + n_out + b:]
        return ins, outs, rest[:n_scratch], (mine_in, mine_out, rest[n_scratch:])

    def at_steps(self, mine, is_first, is_middle, is_last):
        for cond, fn in ((is_first, self.first), (is_middle, self.middle), (is_last, self.last)):
            pl.when(cond)(lambda fn=fn: fn(*mine))


HBM_SPEC = pl.BlockSpec(memory_space=pltpu.HBM)


N_AUG = 4
QCH = 128
KSUB = 384
AHEAD = 5
AHEAD_BWD = 1


def _fox_prep(proj, cum):
    def body(q_ref, k_ref, v_ref, c_ref, qa_ref, ka_ref, vt_ref):
        half = _lane_half(TM)
        lane = lax.broadcasted_iota(jnp.int32, (TM, BLK), 1)
        for pp in range(4):
            cols = slice(pp * BLK, (pp + 1) * BLK)
            qs = q_ref[:, cols].astype(F32) * SCALE
            kp = k_ref[:, cols].astype(F32)
            vp = v_ref[:, cols]
            vt_ref[cols, :] = vp.astype(F32).T.astype(BF16)
            for e in range(2):
                h = 2 * pp + e
                a = (1 - e) * HALF
                blk = slice(h * BLK, (h + 1) * BLK)
                hi, mid, lo = _split3(-c_ref[:, h:h + 1])
                q_aug = jnp.where(half == e, qs, jnp.where((lane >= a) & (lane < a + 3), 1.0, 0.0))
                k_aug = jnp.where(half == e, kp, jnp.where(
                    lane == a, hi.astype(F32), jnp.where(lane == a + 1, mid.astype(F32), jnp.where(
                        lane == a + 2, lo.astype(F32), jnp.where(lane == a + 3, 1.0, 0.0)))))
                qa_ref[:, blk] = q_aug.astype(BF16)
                ka_ref[:, blk] = k_aug.astype(BF16)

    row = lambda blk: pl.BlockSpec((TM, 512), lambda i: (i, blk))
    wide = pl.BlockSpec((TM, 1024), lambda i: (i, 0))
    return pl.pallas_call(
        body, grid=(NT,),
        in_specs=[row(QB), row(KB), row(VB), pl.BlockSpec((TM, BLK), lambda i: (i, 0))],
        out_specs=[wide, wide, pl.BlockSpec((512, TM), lambda i: (0, i))],
        out_shape=[SDS((LP, 1024), BF16)] * 2 + [SDS((512, LP), BF16)],
        compiler_params=_cparams(("parallel",)), name="fox_prep")(proj, proj, proj, cum)


def _over_keys(reduce, x):
    slabs = x.reshape(x.shape[0] // HALF, HALF, x.shape[1])
    return reduce(reduce(slabs, axis=0), axis=0, keepdims=True)


def _fox_valid_t(qi, kj, c, r):
    krow = kj * TM + r * KSUB + lax.broadcasted_iota(jnp.int32, (KSUB, QCH), 0)
    qrow = qi * TM + c * QCH + lax.broadcasted_iota(jnp.int32, (KSUB, QCH), 1)
    return (krow <= qrow) & ((krow >= PAD_ROWS) | (qrow < PAD_ROWS))


def _fox_fwd(q_aug, k_aug, v_t, rider):
    pairs = [(qi, kj) for qi in range(NT) for kj in range(qi + 1)]
    n_pairs = len(pairs)

    def body(qi_ref, kj_ref, *refs):
        (q_ref, k_ref, vt_ref), (o_ref, lse_ref), (m_s, l_s, acc_s), mine = rider.split(refs, 3, 2, 3)
        n = pl.program_id(0)
        qi = qi_ref[n]
        kj = kj_ref[n]
        rider.at_steps(mine, n == 0, n == n_pairs // 2, n == n_pairs - 1)

        @pl.when(kj == 0)
        def _():
            m_s[...] = jnp.full_like(m_s, NEG)
            l_s[...] = jnp.zeros_like(l_s)
            acc_s[...] = jnp.zeros_like(acc_s)

        def tile(masked):
            steps = [(h, c, r) for h in range(N_HEADS) for c in range(TM // QCH) for r in range(TM // KSUB)]

            def scores(h, c, r):
                blk = slice(h * BLK, (h + 1) * BLK)
                return lax.dot_general(k_ref[r * KSUB:(r + 1) * KSUB, blk], q_ref[c * QCH:(c + 1) * QCH, blk],
                                       NT_DIMS, preferred_element_type=F32)

            ahead = [scores(*st) for st in steps[:AHEAD]]
            for n, (h, c, r) in enumerate(steps):
                s_t = ahead.pop(0)
                if n + AHEAD < len(steps):
                    ahead.append(scores(*steps[n + AHEAD]))
                cs = slice(c * QCH, (c + 1) * QCH)
                if masked:
                    s_t = jnp.where(_fox_valid_t(qi, kj, c, r), s_t, NEG)
                m_prev = m_s[h, :, cs]
                m_new = jnp.maximum(m_prev, _over_keys(jnp.max, s_t))
                p_t = jnp.exp(s_t - m_new)
                alpha = jnp.exp(m_prev - m_new)
                l_s[h, :, cs] = alpha * l_s[h, :, cs] + _over_keys(jnp.sum, p_t)
                m_s[h, :, cs] = m_new
                vt = vt_ref[h * HALF:(h + 1) * HALF, r * KSUB:(r + 1) * KSUB]
                acc_s[h, :, cs] = acc_s[h, :, cs] * alpha + jnp.dot(vt, p_t.astype(BF16),
                                                                    preferred_element_type=F32)

        @pl.when((kj < qi) & (kj > 0))
        def _():
            tile(False)

        @pl.when((kj == qi) | (kj == 0))
        def _():
            tile(True)

        @pl.when(kj == qi)
        def _():
            for pp in range(4):
                both = jnp.concatenate([acc_s[2 * pp] * (1.0 / l_s[2 * pp]),
                                        acc_s[2 * pp + 1] * (1.0 / l_s[2 * pp + 1])], axis=0)
                o_ref[:, pp * BLK:(pp + 1) * BLK] = both.T.astype(BF16)
            for h in range(N_HEADS):
                lse_ref[h] = m_s[h] + jnp.log(l_s[h])

    grid_spec = pltpu.PrefetchScalarGridSpec(
        num_scalar_prefetch=2, grid=(n_pairs,),
        in_specs=[pl.BlockSpec((TM, 1024), lambda n, qi, kj: (qi[n], 0)),
                  pl.BlockSpec((TM, 1024), lambda n, qi, kj: (kj[n], 0)),
                  pl.BlockSpec((512, TM), lambda n, qi, kj: (0, kj[n]))] + [HBM_SPEC] * len(rider.operands),
        out_specs=[pl.BlockSpec((TM, 512), lambda n, qi, kj: (qi[n], 0)),
                   pl.BlockSpec((N_HEADS, 1, TM), lambda n, qi, kj: (0, 0, qi[n]))]
        + [HBM_SPEC] * len(rider.out_shapes),
        scratch_shapes=[pltpu.VMEM((N_HEADS, 1, TM), F32), pltpu.VMEM((N_HEADS, 1, TM), F32),
                        pltpu.VMEM((N_HEADS, HALF, TM), F32)] + rider.scratch())
    o_b, lse, *carried = pl.pallas_call(
        body, grid_spec=grid_spec,
        out_shape=[SDS((LP, 512), BF16), SDS((N_HEADS, 1, LP), F32)] + rider.out_shapes,
        compiler_params=_cparams(("arbitrary",)), name="fox_fwd",
    )(jnp.asarray([p[0] for p in pairs], jnp.int32), jnp.asarray([p[1] for p in pairs], jnp.int32),
      q_aug, k_aug, v_t, *rider.operands)
    return o_b, lse, carried


def _fox_bwd(proj, o_b, dmix, lse, ck_t, rider):
    pairs = [(kj, qi) for kj in range(NT) for qi in range(kj, NT)]
    n_pairs = len(pairs)

    def body(kj_ref, qi_ref, *refs):
        ((q_ref, k_ref, v_ref, o_ref, do_ref, lse_ref, ck_ref), (dq_ref, dk_ref, dv_ref, dck_ref, dcq_ref),
         (dk_s, dv_s, dck_s), mine) = rider.split(refs, 7, 5, 3)
        n = pl.program_id(0)
        kj = kj_ref[n]
        qi = qi_ref[n]
        rider.at_steps(mine, n == 0, n == n_pairs // 2, n == n_pairs - 1)

        @pl.when(n == 0)
        def _():
            dq_ref[...] = jnp.zeros_like(dq_ref)
            dcq_ref[...] = jnp.zeros_like(dcq_ref)

        @pl.when(qi == kj)
        def _():
            dk_s[...] = jnp.zeros_like(dk_s)
            dv_s[...] = jnp.zeros_like(dv_s)
            dck_s[...] = jnp.zeros_like(dck_s)

        def tile(masked):
            valid = _fox_valid(qi, kj) if masked else None
            half = _lane_half(TM)
            q0 = pl.multiple_of(qi * TM, TM)
            lane = lax.broadcasted_iota(jnp.int32, (TM, BLK), 1)
            row_sums = jnp.zeros((TM, BLK), F32)
            pair_ops = {}

            def operands(pp):
                if pp not in pair_ops:
                    cols = slice(pp * BLK, (pp + 1) * BLK)
                    pair_ops[pp] = ((q_ref[:, cols].astype(F32) * SCALE).astype(BF16), k_ref[:, cols],
                                    v_ref[:, cols], do_ref[:, cols])
                return pair_ops[pp]

            def scores(pp, e):
                qs, kp, vp, dop = operands(pp)
                ke = jnp.where(half == e, kp, jnp.zeros_like(kp))
                ve = jnp.where(half == e, vp, jnp.zeros_like(vp))
                return (lax.dot_general(qs, ke, NT_DIMS, preferred_element_type=F32),
                        lax.dot_general(dop, ve, NT_DIMS, preferred_element_type=F32), ke)

            steps = [(pp, e) for pp in range(4) for e in range(2)]
            ahead = [scores(*st) for st in steps[:AHEAD_BWD]]
            for n, (pp, e) in enumerate(steps):
                raw, dp, ke = ahead.pop(0)
                if n + AHEAD_BWD < len(steps):
                    ahead.append(scores(*steps[n + AHEAD_BWD]))
                h = 2 * pp + e
                cols = slice(pp * BLK, (pp + 1) * BLK)
                qs, kp, vp, dop = operands(pp)
                if e == 0:
                    prod = dop.astype(F32) * o_ref[:, cols].astype(F32)
                    d0 = jnp.sum(jnp.where(half == 0, prod, 0.0), axis=1, keepdims=True)
                    d1 = jnp.sum(prod, axis=1, keepdims=True) - d0
                    dq = jnp.zeros((TM, BLK), F32)
                    dks, dvs = [], []
                t = raw - ck_ref[h] - lse_ref[h]
                if masked:
                    t = jnp.where(valid, t, NEG)
                p = jnp.exp(t)
                ds = p * (dp - (d0 if e == 0 else d1))
                dck_s[h] += jnp.sum(ds, axis=0, keepdims=True)
                row_sums = jnp.where(lane == h, jnp.sum(ds, axis=1, keepdims=True), row_sums)
                ds_b = ds.astype(BF16)
                dq = dq + jnp.dot(ds_b, ke, preferred_element_type=F32)
                dks.append(lax.dot_general(ds_b, qs, TN_DIMS, preferred_element_type=F32))
                dvs.append(lax.dot_general(p.astype(BF16), dop, TN_DIMS, preferred_element_type=F32))
                if e == 1:
                    dq_ref[pl.ds(q0, TM), cols] += dq
                    dk_s[pp] += jnp.where(half == 0, dks[0], dks[1])
                    dv_s[pp] += jnp.where(half == 0, dvs[0], dvs[1])
            dcq_ref[pl.ds(q0, TM), :] += row_sums

        @pl.when((qi > kj) & (kj > 0))
        def _():
            tile(False)

        @pl.when((qi == kj) | (kj == 0))
        def _():
            tile(True)

        @pl.when(qi == NT - 1)
        def _():
            for pp in range(4):
                cols = slice(pp * BLK, (pp + 1) * BLK)
                dk_ref[:, cols] = dk_s[pp].astype(BF16)
                dv_ref[:, cols] = dv_s[pp].astype(BF16)
            dck_ref[...] = dck_s[...]

    qrow = lambda blk: pl.BlockSpec((TM, 512), lambda n, kj, qi: (qi[n], blk))
    krow = lambda blk: pl.BlockSpec((TM, 512), lambda n, kj, qi: (kj[n], blk))
    grid_spec = pltpu.PrefetchScalarGridSpec(
        num_scalar_prefetch=2, grid=(n_pairs,),
        in_specs=[qrow(QB), krow(KB), krow(VB), qrow(0), qrow(1),
                  pl.BlockSpec((N_HEADS, TM, 1), lambda n, kj, qi: (0, qi[n], 0)),
                  pl.BlockSpec((N_HEADS, 1, TM), lambda n, kj, qi: (0, 0, kj[n]))] + [HBM_SPEC] * len(rider.operands),
        out_specs=[pl.BlockSpec((LP, 512), lambda n, kj, qi: (0, 0)),
                   pl.BlockSpec((TM, 512), lambda n, kj, qi: (kj[n], 0)),
                   pl.BlockSpec((TM, 512), lambda n, kj, qi: (kj[n], 0)),
                   pl.BlockSpec((N_HEADS, 1, TM), lambda n, kj, qi: (0, 0, kj[n])),
                   pl.BlockSpec((LP, BLK), lambda n, kj, qi: (0, 0))] + [HBM_SPEC] * len(rider.out_shapes),
        scratch_shapes=[pltpu.VMEM((4, TM, BLK), F32), pltpu.VMEM((4, TM, BLK), F32),
                        pltpu.VMEM((N_HEADS, 1, TM), F32)] + rider.scratch())
    dq, dk, dv, dck, dcq, *carried = pl.pallas_call(
        body, grid_spec=grid_spec,
        out_shape=[SDS((LP, 512), F32), SDS((LP, 512), BF16), SDS((LP, 512), BF16), SDS((N_HEADS, 1, LP), F32),
                   SDS((LP, BLK), F32)] + rider.out_shapes,
        compiler_params=_cparams(("arbitrary",)), name="fox_bwd",
    )(jnp.asarray([p[0] for p in pairs], jnp.int32), jnp.asarray([p[1] for p in pairs], jnp.int32),
      proj, proj, proj, o_b, dmix, lse, ck_t, *rider.operands)
    return dq, dk, dv, dck, dcq, carried


N_SEG = 3
N_KEY = N_SEG * BLK
GROUP = 4
QW = GROUP * BLK


def _bucket_tables_t():
    return np.ascontiguousarray(_bucket_tables().transpose(0, 2, 1))


def _stack_heads(ref, g, scale):
    half = _lane_half(BLK)
    out = []
    for pair in range(2):
        x = ref[:, (2 * g + pair) * BLK:(2 * g + pair + 1) * BLK].astype(F32) * scale
        swapped = pltpu.roll(x, HALF, 1)
        for e in range(2):
            out.append(jnp.where(half == g, x if e == g else swapped, 0.0).astype(BF16))
    return jnp.concatenate(out, axis=0)


def _unstack_heads(x_t, g, ref, scale):
    for pair in range(2):
        both = jnp.concatenate([x_t[:, (2 * pair) * BLK:(2 * pair + 1) * BLK],
                                x_t[:, (2 * pair + 1) * BLK:(2 * pair + 2) * BLK]], axis=0)
        ref[:, (2 * g + pair) * BLK:(2 * g + pair + 1) * BLK] = (both.T * scale).astype(ref.dtype)


def _swa_tables(tab_ref, sink_ref, bkt_ref, tbl, sink_row):
    kk = lax.broadcasted_iota(jnp.int32, (BLK, BLK), 0)
    qq = lax.broadcasted_iota(jnp.int32, (BLK, BLK), 1)
    neg = jnp.full((BLK, BLK), NEG, F32)
    lane = lax.broadcasted_iota(jnp.int32, (1, QW), 1) // BLK
    for g in range(2):
        row = jnp.zeros((1, QW), F32)
        for hh in range(GROUP):
            h = GROUP * g + hh
            cols = slice(hh * BLK, (hh + 1) * BLK)
            row = jnp.where(lane == hh, sink_ref[0, h], row)

            def step(b, carry, h=h):
                t = tab_ref[b, h]
                return jnp.where(bkt_ref[0] == b, t, carry[0]), jnp.where(bkt_ref[1] == b, t, carry[1])
            zero = jnp.zeros((BLK, BLK), F32)
            cur, prev = lax.fori_loop(0, N_BUCKETS, step, (zero, zero))
            far = jnp.full((BLK, BLK), tab_ref[N_BUCKETS - 1, h], F32)
            causal = jnp.where(kk <= qq, cur, neg)
            segments = [
                (neg, neg, jnp.where(kk >= PAD_ROWS, causal, neg)),
                (jnp.where(kk >= PAD_ROWS, prev, neg), neg, causal),
                (jnp.where(kk >= PAD_ROWS, far, neg), jnp.where(kk > qq, prev, neg), causal)]
            for case in range(3):
                for seg in range(N_SEG):
                    tbl[case, g, seg * BLK:(seg + 1) * BLK, cols] = segments[case][seg]
        sink_row[g] = row


def _swa_prep(proj):
    rows = LP // 3

    def body(k_ref, v_ref, kt_ref, vt_ref):
        kt_ref[...] = k_ref[...].astype(F32).T.astype(BF16)
        vt_ref[...] = v_ref[...].astype(F32).T.astype(BF16)

    col = pl.BlockSpec((BLK, rows), lambda i: (0, i))
    return pl.pallas_call(
        body, grid=(3,),
        in_specs=[pl.BlockSpec((rows, BLK), lambda i: (i, KA)), pl.BlockSpec((rows, BLK), lambda i: (i, VA))],
        out_specs=[col, col], out_shape=[SDS((BLK, LP), BF16)] * 2,
        compiler_params=_cparams(("parallel",)), name="swa_prep")(proj, proj)


def _seg_specs(rows_major, col):
    idx = [lambda i: 0, lambda i: jnp.maximum(i - 1, 0), lambda i: i]
    if rows_major:
        return [pl.BlockSpec((BLK, BLK), lambda i, f=f: (f(i), col)) for f in idx]
    return [pl.BlockSpec((BLK, BLK), lambda i, f=f: (0, f(i))) for f in idx]


def _swa_fwd(proj, vt_a, rel_bias, sinks, bkt_t, rider):
    def body(*refs):
        ((tab_ref, sink_ref, bkt_ref, q_ref, km_ref, kp_ref, kc_ref, vm_ref, vp_ref, vc_ref), (o_ref, lse_ref),
         (tbl, sink_row), mine) = rider.split(refs, 10, 2, 2)
        i = pl.program_id(0)
        rider.at_steps(mine, i == 0, i == NBLK // 2, i == NBLK - 1)

        @pl.when(i == 0)
        def _():
            _swa_tables(tab_ref, sink_ref, bkt_ref, tbl, sink_row)

        case = jnp.minimum(i, 2)
        k_cat = jnp.concatenate([km_ref[...], kp_ref[...], kc_ref[...]], axis=0)
        vt_cat = jnp.concatenate([vm_ref[...], vp_ref[...], vc_ref[...]], axis=1)
        raw = [lax.dot_general(k_cat, _stack_heads(q_ref, g, SCALE), NT_DIMS, preferred_element_type=F32)
               for g in range(2)]
        for g in range(2):
            s_t = raw[g] + tbl[case, g]
            sink = sink_row[g]
            m = jnp.maximum(_over_keys(jnp.max, s_t), sink)
            p_t = jnp.exp(s_t - m)
            l = _over_keys(jnp.sum, p_t) + jnp.exp(sink - m)
            o_t = jnp.dot(vt_cat[g * HALF:(g + 1) * HALF, :], p_t.astype(BF16), preferred_element_type=F32)
            _unstack_heads(o_t * (1.0 / l), g, o_ref, 1.0)
            lse = m + jnp.log(l)
            for hh in range(GROUP):
                lse_ref[GROUP * g + hh] = lse[:, hh * BLK:(hh + 1) * BLK]

    smem = pl.BlockSpec(memory_space=pltpu.SMEM)
    o_a, lse, *carried = pl.pallas_call(
        body, grid=(NBLK,),
        in_specs=[smem, smem, pl.BlockSpec((2, BLK, BLK), lambda i: (0, 0, 0)),
                  pl.BlockSpec((BLK, 512), lambda i: (i, QA))] + _seg_specs(True, KA) + _seg_specs(False, 0)
        + [HBM_SPEC] * len(rider.operands),
        out_specs=[pl.BlockSpec((BLK, 512), lambda i: (i, 0)),
                   pl.BlockSpec((N_HEADS, 1, BLK), lambda i: (0, 0, i))] + [HBM_SPEC] * len(rider.out_shapes),
        out_shape=[SDS((LP, 512), BF16), SDS((N_HEADS, 1, LP), F32)] + rider.out_shapes,
        scratch_shapes=[pltpu.VMEM((3, 2, N_KEY, QW), F32), pltpu.VMEM((2, 1, QW), F32)] + rider.scratch(),
        compiler_params=_cparams(("arbitrary",)), name="swa_fwd",
    )(rel_bias, sinks, bkt_t, proj, proj, proj, proj, vt_a, vt_a, vt_a, *rider.operands)
    return o_a, lse, carried


def _swa_bwd(proj, kt_a, o_a, dmix, lse, rel_bias, sinks, bkt_t):
    def body(tab_ref, sink_ref, bkt_ref, q_ref, km_ref, kp_ref, kc_ref, vm_ref, vp_ref, vc_ref,
             tm_ref, tp_ref, tc_ref, o_ref, do_ref, lse_ref,
             dq_ref, dk_ref, dv_ref, dbias_ref, dsink_ref, tbl, sink_row, acc, dsk):
        i = pl.program_id(0)

        @pl.when(i == 0)
        def _():
            _swa_tables(tab_ref, sink_ref, bkt_ref, tbl, sink_row)
            dk_ref[...] = jnp.zeros_like(dk_ref)
            dv_ref[...] = jnp.zeros_like(dv_ref)
            acc[...] = jnp.zeros_like(acc)
            dsk[...] = jnp.zeros_like(dsk)

        case = jnp.minimum(i, 2)
        first = jnp.full((BLK, QW), i, jnp.int32) == 1
        k_cat = jnp.concatenate([km_ref[...], kp_ref[...], kc_ref[...]], axis=0)
        v_cat = jnp.concatenate([vm_ref[...], vp_ref[...], vc_ref[...]], axis=0)
        kt_cat = jnp.concatenate([tm_ref[...], tp_ref[...], tc_ref[...]], axis=1)
        dk_cat = jnp.zeros((N_KEY, BLK), F32)
        dv_cat = jnp.zeros((N_KEY, BLK), F32)
        for g in range(2):
            d_parts = []
            for pair in range(2):
                cols = slice((2 * g + pair) * BLK, (2 * g + pair + 1) * BLK)
                prod_t = (do_ref[:, cols].astype(F32) * o_ref[:, cols].astype(F32)).T
                d_parts += [jnp.sum(prod_t[:HALF], axis=0, keepdims=True),
                            jnp.sum(prod_t[HALF:], axis=0, keepdims=True)]
            d_row = jnp.concatenate(d_parts, axis=1)
            lse_row = jnp.concatenate([lse_ref[GROUP * g + hh] for hh in range(GROUP)], axis=1)
            q_st = _stack_heads(q_ref, g, SCALE)
            do_st = _stack_heads(do_ref, g, 1.0)
            s_t = lax.dot_general(k_cat, q_st, NT_DIMS, preferred_element_type=F32) + tbl[case, g]
            p_t = jnp.exp(s_t - lse_row)
            dp_t = lax.dot_general(v_cat, do_st, NT_DIMS, preferred_element_type=F32)
            ds_t = p_t * (dp_t - d_row)
            dsk[g] += -jnp.exp(sink_row[g] - lse_row) * d_row
            acc[g, 0:BLK] += jnp.where(first, 0.0, ds_t[0:BLK])
            acc[g, BLK:2 * BLK] += jnp.where(first, ds_t[0:BLK], ds_t[BLK:2 * BLK])
            acc[g, 2 * BLK:N_KEY] += ds_t[2 * BLK:N_KEY]
            ds_b = ds_t.astype(BF16)
            dk_cat = dk_cat + jnp.dot(ds_b, q_st, preferred_element_type=F32)
            dv_cat = dv_cat + jnp.dot(p_t.astype(BF16), do_st, preferred_element_type=F32)
            dq_t = jnp.dot(kt_cat[g * HALF:(g + 1) * HALF, :], ds_b, preferred_element_type=F32)
            _unstack_heads(dq_t, g, dq_ref, SCALE)

        prev0 = pl.multiple_of(jnp.maximum(i - 1, 0) * BLK, BLK)
        cur0 = pl.multiple_of(i * BLK, BLK)
        for ref, cat in ((dk_ref, dk_cat), (dv_ref, dv_cat)):
            ref[0:BLK, :] += cat[0:BLK]
            ref[pl.ds(prev0, BLK), :] += cat[BLK:2 * BLK]
            ref[pl.ds(cur0, BLK), :] += cat[2 * BLK:N_KEY]

        @pl.when(i == NBLK - 1)
        def _():
            lane = lax.broadcasted_iota(jnp.int32, (1, BLK), 1)

            def per_bucket(b, carry):
                row = jnp.zeros((1, BLK), F32)
                for h in range(N_HEADS):
                    g, cols = h // GROUP, slice((h % GROUP) * BLK, (h % GROUP + 1) * BLK)
                    val = (jnp.sum(jnp.where(bkt_ref[0] == b, acc[g, 2 * BLK:N_KEY, cols], 0.0), keepdims=True)
                           + jnp.sum(jnp.where(bkt_ref[1] == b, acc[g, BLK:2 * BLK, cols], 0.0), keepdims=True))
                    row = jnp.where(lane == h, val, row)
                dbias_ref[pl.ds(b, 1), :] = row
                return carry

            lax.fori_loop(0, N_BUCKETS, per_bucket, 0)
            far = jnp.zeros((1, BLK), F32)
            dsr = jnp.zeros((1, BLK), F32)
            for h in range(N_HEADS):
                g, cols = h // GROUP, slice((h % GROUP) * BLK, (h % GROUP + 1) * BLK)
                far = jnp.where(lane == h, jnp.sum(acc[g, 0:BLK, cols], keepdims=True), far)
                dsr = jnp.where(lane == h, jnp.sum(dsk[g, :, cols], keepdims=True), dsr)
            dbias_ref[N_BUCKETS - 1:N_BUCKETS, :] += far
            dsink_ref[...] = dsr

    smem = pl.BlockSpec(memory_space=pltpu.SMEM)
    blk512 = lambda col: pl.BlockSpec((BLK, 512), lambda i: (i, col))
    full = lambda r, c: pl.BlockSpec((r, c), lambda i: (0, 0))
    return pl.pallas_call(
        body, grid=(NBLK,),
        in_specs=[smem, smem, pl.BlockSpec((2, BLK, BLK), lambda i: (0, 0, 0)), blk512(QA)]
        + _seg_specs(True, KA) + _seg_specs(True, VA) + _seg_specs(False, 0)
        + [blk512(0), blk512(0), pl.BlockSpec((N_HEADS, 1, BLK), lambda i: (0, 0, i))],
        out_specs=[blk512(0), full(LP, BLK), full(LP, BLK), full(N_BUCKETS, BLK), full(1, BLK)],
        out_shape=[SDS((LP, 512), BF16), SDS((LP, BLK), F32), SDS((LP, BLK), F32),
                   SDS((N_BUCKETS, BLK), F32), SDS((1, BLK), F32)],
        scratch_shapes=[pltpu.VMEM((3, 2, N_KEY, QW), F32), pltpu.VMEM((2, 1, QW), F32),
                        pltpu.VMEM((2, N_KEY, QW), F32), pltpu.VMEM((2, 1, QW), F32)],
        compiler_params=_cparams(("arbitrary",)), name="swa_bwd",
    )(rel_bias, sinks, bkt_t, proj, proj, proj, proj, proj, proj, proj, kt_a, kt_a, kt_a, o_a, dmix, lse)


def _local_step(x, tgt, meta, rel_bias, g_pre_mix, g_post_mix, g_pre_ffn, g_post_ffn, b_forget, sinks,
                w_in_b, out_rider, out_weight, ffn_rider, ffn_weights, early_grads):
    bkt_t = jnp.asarray(_bucket_tables_t())
    h0 = jnp.concatenate([jnp.zeros((PAD_ROWS, D_MODEL), F32), meta, x], axis=0)
    b_p = jnp.pad(b_forget, ((0, 0), (0, BLK - N_HEADS)))

    hn1, proj, f = _pre_mix(h0, g_pre_mix, w_in_b)
    kt_a, vt_a = _swa_prep(proj)
    o_a, lse_a, carried_out = _swa_fwd(proj, vt_a, rel_bias, sinks, bkt_t, out_rider)
    w_out_b = out_weight(carried_out)
    cum = _forget_cumsum(f, b_p)
    ck_t = cum[:, :N_HEADS].T.reshape(N_HEADS, 1, LP)
    q_aug, k_aug, v_t = _fox_prep(proj, cum)
    o_b, lse_row, carried = _fox_fwd(q_aug, k_aug, v_t, ffn_rider)
    lse_b = lse_row.reshape(N_HEADS, LP, 1)
    w_gu_b, w_dn_b = ffn_weights(carried)
    a, h1, hn2 = _attn_out(o_a, o_b, w_out_b, h0, g_post_mix, g_pre_ffn)
    g, u, act = _ffn_up(hn2, w_gu_b)
    dff, dy, loss_blk, dg_post_ffn = _ffn_down_loss(act, w_dn_b, h1, tgt, g_post_ffn)

    dw_dn = _mm_tn([act], dff, FF_T, "dw_down", BF16)
    dg, du = _ffn_down_bwd(dff, w_dn_b, g, u)
    dw_gu = _dw_gate_up(hn2, dg, du)
    dh1, da, dg_pre_ffn, dg_post_mix = _ffn_up_bwd(dg, du, w_gu_b, h1, a, dy, g_pre_ffn, g_post_mix)
    dw_out = _mm_tn([o_a, o_b], da, D_MODEL, "dw_out", BF16)
    dmix = _attn_out_bwd(da, w_out_b)
    dq_b, dk_b, dv_b, dck, dcq, landed = _fox_bwd(proj, o_b, dmix, lse_b, ck_t, early_grads(dw_gu, dw_dn, dw_out))
    dq_a, dk_a, dv_a, dbias, dsink = _swa_bwd(proj, kt_a, o_a, dmix, lse_a, rel_bias, sinks, bkt_t)
    dcum = dcq - jnp.pad(dck.reshape(N_HEADS, LP).T, ((0, 0), (0, BLK - N_HEADS)))
    df, db = _forget_cumsum_bwd(dcum, f, b_p)
    dproj, dh0, dg_pre_mix = _pre_mix_bwd(dq_a, dq_b, dk_b, dv_b, dk_a, dv_a, df, w_in_b, h0, dh1, g_pre_mix)
    dw_in = _mm_tn([hn1], dproj, D_MODEL, "dw_in")

    return dict(loss=loss_blk[0, 0], grad_x=dh0[ROW0:], meta=dh0[PAD_ROWS:ROW0],
                rel_bias=dbias[:, :N_HEADS], ln_pre_mix=dg_pre_mix, ln_post_mix=dg_post_mix,
                ln_pre_ffn=dg_pre_ffn, ln_post_ffn=dg_post_ffn, b_forget=db[:, :N_HEADS],
                sinks=dsink[:, :N_HEADS], w_in=dw_in, w_out=dw_out, w_gate_up=dw_gu, w_down=dw_dn,
                landed=landed)


N_SMALL = 24
LOSS_ROW = 6


def _place():
    x, y, c = lax.axis_index("x"), lax.axis_index("y"), lax.axis_index("c")
    return x, y, c, [(1 - x, y), (x, 1 - y), (1 - x, 1 - y)]


def _run_alone(rider, name):
    a, b = len(rider.operands), len(rider.out_shapes)

    def body(*refs):
        mine = (refs[:a], refs[a:a + b], refs[a + b:])
        rider.first(*mine)
        rider.middle(*mine)
        rider.last(*mine)

    return pl.pallas_call(body, in_specs=[HBM_SPEC] * a, out_specs=[HBM_SPEC] * b, out_shape=rider.out_shapes,
                          scratch_shapes=rider.scratch(), name=name)(*rider.operands)


def _gather_rider(shards, own_too):
    n = len(shards)

    def own_copies(ins, outs, sems):
        x, y, _, _ = _place()
        return [pltpu.make_async_copy(ins[a], outs[a].at[2 * x + y], sems[2].at[a]) for a in range(n)] if own_too else []

    def copies(ins, outs, sems):
        send_sems, recv_sems = sems[:2]
        x, y, c, others = _place()
        chip = 2 * x + y
        sibling = (x, y, 1 - c)

        def rc(a, k, src, dst, to):
            return pltpu.make_async_remote_copy(src_ref=src, dst_ref=dst, send_sem=send_sems.at[6 * a + k],
                                                recv_sem=recv_sems.at[6 * a + k], device_id=to, device_id_type=MESH)

        pairs = [(a, j, ox, oy) for a in range(n) for j, (ox, oy) in enumerate(others)]
        return dict(
            sent=lambda: [rc(a, j, ins[a].at[c], outs[a].at[chip, c], (ox, oy, c)) for a, j, ox, oy in pairs],
            landed=lambda: [rc(a, j, outs[a].at[2 * ox + oy, c], outs[a].at[2 * ox + oy, c], sibling)
                            for a, j, ox, oy in pairs],
            passed=lambda: [rc(a, 3 + j, outs[a].at[2 * ox + oy, c], outs[a].at[2 * ox + oy, c], sibling)
                            for a, j, ox, oy in pairs],
            arriving=lambda: [rc(a, 3 + j, outs[a].at[2 * ox + oy, 1 - c], outs[a].at[2 * ox + oy, 1 - c], sibling)
                              for a, j, ox, oy in pairs])

    def first(*mine):
        for cp in copies(*mine)["sent"]() + own_copies(*mine):
            cp.start()

    def middle(*mine):
        kinds = copies(*mine)
        for got, cp in zip(kinds["landed"](), kinds["passed"]()):
            got.wait_recv()
            cp.start()

    def last(*mine):
        kinds = copies(*mine)
        for cp in kinds["arriving"]():
            cp.wait_recv()
        for cp in kinds["sent"]() + kinds["passed"]():
            cp.wait_send()
        for cp in own_copies(*mine):
            cp.wait()

    return _Rider(shards, [SDS((4,) + s.shape, s.dtype) for s in shards], [6 * n, 6 * n] + [n] * own_too,
                  first, middle, last)


def _swap_rider(grads):
    n = len(grads)

    def copies(ins, outs, sems):
        x, y, c, _ = _place()
        return [pltpu.make_async_remote_copy(
            src_ref=ins[a].at[s, 1 - c], dst_ref=outs[a].at[s], send_sem=sems[0].at[4 * a + s],
            recv_sem=sems[1].at[4 * a + s], device_id=(x, y, 1 - c), device_id_type=MESH)
            for a in range(n) for s in range(4)]

    def first(*mine):
        for cp in copies(*mine):
            cp.start()

    def middle(*mine):
        pass

    def last(*mine):
        for cp in copies(*mine):
            cp.wait()

    return _Rider(grads, [SDS((4,) + g.shape[2:], g.dtype) for g in grads], [4 * n, 4 * n], first, middle, last)


def _pair_sum(g, got, c_arr, name):
    rh, cc = got.shape[1:]

    def body(c_ref, g_ref, p_ref, o_ref):
        o_ref[0] = (g_ref[0, 0] + p_ref[0]).astype(BF16)

    grid_spec = pltpu.PrefetchScalarGridSpec(
        num_scalar_prefetch=1, grid=(4,),
        in_specs=[pl.BlockSpec((1, 1, rh, cc), lambda s, c_ref: (s, c_ref[0], 0, 0)),
                  pl.BlockSpec((1, rh, cc), lambda s, c_ref: (s, 0, 0))],
        out_specs=pl.BlockSpec((1, rh, cc), lambda s, c_ref: (s, 0, 0)))
    return pl.pallas_call(body, grid_spec=grid_spec, out_shape=SDS((4, rh, cc), BF16),
                          compiler_params=_cparams(("parallel",)), name=name)(c_arr, g, got)


def _exchange_rider(parts, small=None):
    n = len(parts)

    def copies(ins, outs, sems):
        x, y, c, others = _place()
        out = [pltpu.make_async_remote_copy(
            src_ref=ins[a].at[2 * ox + oy], dst_ref=outs[a].at[j], send_sem=sems[0].at[3 * a + j],
            recv_sem=sems[1].at[3 * a + j], device_id=(ox, oy, c), device_id_type=MESH)
            for a in range(n) for j, (ox, oy) in enumerate(others)]
        own = []
        if small is not None:
            me = 4 * x + 2 * y + c
            peers = [(x, y, 1 - c)] + [(ox, oy, c) for ox, oy in others] + [(ox, oy, 1 - c) for ox, oy in others]
            out += [pltpu.make_async_remote_copy(
                src_ref=ins[n], dst_ref=outs[n].at[me], send_sem=sems[2].at[k], recv_sem=sems[3].at[k],
                device_id=peer, device_id_type=MESH) for k, peer in enumerate(peers)]
            own = [pltpu.make_async_copy(ins[n], outs[n].at[me], sems[4].at[0])]
        return out, own

    def first(*mine):
        out, own = copies(*mine)
        for cp in own + out:
            cp.start()

    def middle(*mine):
        pass

    def last(*mine):
        out, own = copies(*mine)
        for cp in out + own:
            cp.wait()

    shapes = [SDS((3,) + p.shape[1:], p.dtype) for p in parts]
    if small is None:
        return _Rider(parts, shapes, [3 * n, 3 * n], first, middle, last)
    return _Rider(parts + [small], shapes + [SDS((8,) + small.shape, small.dtype)], [3 * n, 3 * n, 7, 7, 1],
                  first, middle, last)


def _direct_rider(grads):
    n = len(grads)

    def copies(ins, outs, sems):
        x, y, c, others = _place()
        peers = [(x, y, 1 - c)] + [(ox, oy, c) for ox, oy in others] + [(ox, oy, 1 - c) for ox, oy in others]
        return [pltpu.make_async_remote_copy(
            src_ref=ins[a].at[2 * px + py, pc], dst_ref=outs[a].at[k], send_sem=sems[0].at[7 * a + k],
            recv_sem=sems[1].at[7 * a + k], device_id=(px, py, pc), device_id_type=MESH)
            for a in range(n) for k, (px, py, pc) in enumerate(peers)]

    def first(*mine):
        for cp in copies(*mine):
            cp.start()

    def middle(*mine):
        pass

    def last(*mine):
        for cp in copies(*mine):
            cp.wait()

    return _Rider(grads, [SDS((7,) + g.shape[2:], g.dtype) for g in grads], [7 * n, 7 * n], first, middle, last)


def _owner_sum(grads, landed, own_arr, name):
    rh, cc = landed.shape[1:]
    tr = rh // 2

    def body(own_ref, g_ref, p_ref, o_ref):
        total = g_ref[0, 0].astype(F32)
        for k in range(7):
            total = total + p_ref[k].astype(F32)
        o_ref[...] = total

    grid_spec = pltpu.PrefetchScalarGridSpec(
        num_scalar_prefetch=1, grid=(2,),
        in_specs=[pl.BlockSpec((1, 1, tr, cc), lambda i, own: (own[0], own[1], i, 0)),
                  pl.BlockSpec((7, tr, cc), lambda i, own: (0, i, 0))],
        out_specs=pl.BlockSpec((tr, cc), lambda i, own: (i, 0)))
    return pl.pallas_call(body, grid_spec=grid_spec, out_shape=SDS((rh, cc), F32),
                          compiler_params=_cparams(("parallel",)), name=name)(own_arr, grads, landed)


def _chip_sum(parts, landed, chip_arr, name):
    rh, cc = landed.shape[1:]
    tr = rh // 2

    def body(chip_ref, own_ref, p_ref, o_ref):
        o_ref[...] = ((own_ref[0].astype(F32) + p_ref[0].astype(F32)) + p_ref[1].astype(F32)) + p_ref[2].astype(F32)

    grid_spec = pltpu.PrefetchScalarGridSpec(
        num_scalar_prefetch=1, grid=(2,),
        in_specs=[pl.BlockSpec((1, tr, cc), lambda i, chip_ref: (chip_ref[0], i, 0)),
                  pl.BlockSpec((3, tr, cc), lambda i, chip_ref: (0, i, 0))],
        out_specs=pl.BlockSpec((tr, cc), lambda i, chip_ref: (i, 0)))
    return pl.pallas_call(body, grid_spec=grid_spec, out_shape=SDS((rh, cc), F32),
                          compiler_params=_cparams(("parallel",)), name=name)(chip_arr, parts, landed)


def _device_sum(p):
    def body(p_ref, o_ref):
        acc = p_ref[0]
        for k in range(1, 8):
            acc = acc + p_ref[k]
        o_ref[...] = acc

    return pl.pallas_call(body, out_shape=SDS(p.shape[1:], F32), name="small_sum")(p)


def _join_halves(halves):
    n = len(halves)

    def body(*refs):
        ins, outs = refs[:n], refs[n:2 * n]
        send_sems, recv_sems = refs[2 * n:]
        x, y, c, _ = _place()
        copies = [pltpu.make_async_remote_copy(
            src_ref=ins[a], dst_ref=outs[a], send_sem=send_sems.at[a], recv_sem=recv_sems.at[a],
            device_id=(x, y, 1 - c), device_id_type=MESH) for a in range(n)]
        for cp in copies:
            cp.start()
        for cp in copies:
            cp.wait()

    return pl.pallas_call(
        body, in_specs=[HBM_SPEC] * n, out_specs=[HBM_SPEC] * n,
        out_shape=[SDS(h.shape, h.dtype) for h in halves],
        scratch_shapes=[pltpu.SemaphoreType.DMA((n,)), pltpu.SemaphoreType.DMA((n,))],
        name="join_halves")(*halves)


def _adamw(w, g, m, v, name):
    rows, cols = w.shape
    tr = rows if rows <= 352 else (256 if rows % 256 == 0 else 352)

    def body(w_ref, g_ref, m_ref, v_ref, d_ref, nm_ref, nv_ref):
        gg = g_ref[...]
        nm = ADAM_B1 * m_ref[...] + (1.0 - ADAM_B1) * gg
        nv = ADAM_B2 * v_ref[...] + (1.0 - ADAM_B2) * (gg * gg)
        nm_ref[...] = nm
        nv_ref[...] = nv
        m_hat = nm / (1.0 - ADAM_B1 ** ADAM_STEP)
        v_hat = nv / (1.0 - ADAM_B2 ** ADAM_STEP)
        d_ref[...] = -ADAM_LR * (m_hat / (jnp.sqrt(v_hat) + ADAM_EPS) + ADAM_WD * w_ref[...])

    blk = pl.BlockSpec((tr, cols), lambda i: (i, 0))
    return pl.pallas_call(
        body, grid=(rows // tr,), in_specs=[blk] * 4, out_specs=[blk] * 3,
        out_shape=[SDS((rows, cols), F32)] * 3,
        compiler_params=_cparams(("parallel",)), name=name)(w, g, m, v)


def _adamw_halves(w, mine, theirs, m, v, c_arr, name):
    rows, cols = w.shape
    rh = rows // 2
    tr = rh if rh <= 352 else 256
    nh = rh // tr

    def body(c_ref, w_ref, mine_ref, theirs_ref, m_ref, v_ref, g_ref, d_ref, nm_ref, nv_ref):
        own = jnp.full((tr, cols), pl.program_id(0), jnp.int32) == c_ref[0]
        gg = jnp.where(own, mine_ref[...], theirs_ref[...])
        g_ref[...] = gg
        nm = ADAM_B1 * m_ref[...] + (1.0 - ADAM_B1) * gg
        nv = ADAM_B2 * v_ref[...] + (1.0 - ADAM_B2) * (gg * gg)
        nm_ref[...] = nm
        nv_ref[...] = nv
        m_hat = nm / (1.0 - ADAM_B1 ** ADAM_STEP)
        v_hat = nv / (1.0 - ADAM_B2 ** ADAM_STEP)
        d_ref[...] = -ADAM_LR * (m_hat / (jnp.sqrt(v_hat) + ADAM_EPS) + ADAM_WD * w_ref[...])

    whole = pl.BlockSpec((tr, cols), lambda hh, i, c_ref: (hh * nh + i, 0))
    part = pl.BlockSpec((tr, cols), lambda hh, i, c_ref: (i, 0))
    grid_spec = pltpu.PrefetchScalarGridSpec(
        num_scalar_prefetch=1, grid=(2, nh), in_specs=[whole, part, part, whole, whole], out_specs=[whole] * 4)
    return pl.pallas_call(body, grid_spec=grid_spec, out_shape=[SDS((rows, cols), F32)] * 4,
                          compiler_params=_cparams(("parallel", "parallel")), name=name)(c_arr, w, mine, theirs, m, v)


def _pack_small(pre_mix, post_mix, pre_ffn, post_ffn, rel_bias, b_forget, sinks):
    def at(row, v):
        return jnp.pad(v, ((row, 7 - row), (0, D_MODEL - v.shape[1])))
    return (at(0, pre_mix) + at(1, post_mix) + at(2, pre_ffn) + at(3, post_ffn)
            + at(4, rel_bias.reshape(1, N_BUCKETS * N_HEADS)) + at(5, jnp.concatenate([b_forget, sinks], axis=1)))


def _unpack_small(p):
    return dict(ln_pre_mix=p[0:1], ln_post_mix=p[1:2], ln_pre_ffn=p[2:3], ln_post_ffn=p[3:4],
                rel_bias=p[4, :N_BUCKETS * N_HEADS].reshape(N_BUCKETS, N_HEADS),
                b_forget=p[5:6, 0:N_HEADS], sinks=p[5:6, N_HEADS:2 * N_HEADS])


WEIGHTS = ("meta_tokens", "rel_bias", "ln_pre_mix", "ln_post_mix", "ln_pre_ffn", "ln_post_ffn",
           "w_in", "b_forget", "sinks", "w_out", "w_gate_up", "w_down")


def kernel(x, meta_tokens, rel_bias, ln_pre_mix, ln_post_mix, ln_pre_ffn, ln_post_ffn, w_in, b_forget, sinks, w_out, w_gate_up, w_down, loss_target, m_meta_tokens, m_rel_bias, m_ln_pre_mix, m_ln_post_mix, m_ln_pre_ffn, m_ln_post_ffn, m_w_in, m_b_forget, m_sinks, m_w_out, m_w_gate_up, m_w_down, v_meta_tokens, v_rel_bias, v_ln_pre_mix, v_ln_post_mix, v_ln_pre_ffn, v_ln_post_ffn, v_w_in, v_b_forget, v_sinks, v_w_out, v_w_gate_up, v_w_down):
    xi, yi, ci = lax.axis_index("x"), lax.axis_index("y"), lax.axis_index("c")
    chip = 2 * xi + yi
    c_arr = jnp.reshape(ci, (1,)).astype(jnp.int32)

    def halves(w, dtype):
        return w.astype(dtype).reshape(2, w.shape[0] // 2, w.shape[1])

    def with_own(gathered, shards):
        return [lax.dynamic_update_slice(got, own[None], (chip, 0, 0, 0)) for got, own in zip(gathered, shards)]

    shards = [halves(w_in[0], BF16), halves(meta_tokens, F32)]
    gw_in, g_meta = with_own(_run_alone(_gather_rider(shards, False), "gather_mixer_weights"), shards)
    out_shards = [halves(w_out[0], BF16)]
    ffn_shards = [halves(w_gate_up[0], BF16), halves(w_down[0], BF16)]

    def ffn_weights(carried):
        gw_gu, gw_dn = carried
        w_gu_b = gw_gu.reshape(4, D_MODEL, FF_T).transpose(1, 0, 2).reshape(D_MODEL, 2 * D_FF)
        return w_gu_b, gw_dn.reshape(D_FF, D_MODEL)

    early = {}

    def early_grads(dw_gu, dw_dn, dw_out):
        early["grads"] = [dw_out.reshape(4, 2, 128, D_MODEL), dw_gu.reshape(4, 2, 512, FF_T),
                          dw_dn.reshape(4, 2, 352, D_MODEL)]
        return _direct_rider(early["grads"])
    w_in_all = gw_in.reshape(4, D_MODEL, D_PROJ // 4).transpose(1, 0, 2).reshape(D_MODEL, D_PROJ)
    w_in_b = jnp.concatenate(
        [w_in_all[:, 0:512], w_in_all[:, 768:1280], w_in_all[:, 1280:1792], w_in_all[:, 1792:2304],
         w_in_all[:, 512:640], w_in_all[:, 640:768], w_in_all[:, 2304:2312],
         jnp.zeros((D_MODEL, D_PROJ_P - D_PROJ), BF16)], axis=1)
    meta_all = g_meta.reshape(4, N_META, D_MODEL // 4).transpose(1, 0, 2).reshape(N_META, D_MODEL)

    loc = _local_step(x[0], loss_target[0], meta_all, rel_bias, ln_pre_mix, ln_post_mix, ln_pre_ffn, ln_post_ffn,
                      b_forget, sinks, w_in_b, _gather_rider(out_shards, True),
                      lambda carried: carried[0].reshape(D_MODEL, D_MODEL),
                      _gather_rider(ffn_shards, True), ffn_weights, early_grads)

    n = loc["w_in"]
    dw_in = jnp.concatenate([n[:, 0:512], n[:, 2048:2176], n[:, 2176:2304], n[:, 512:1024], n[:, 1024:1536],
                             n[:, 1536:2048], n[:, 2304:2312]], axis=1)
    dw_in = dw_in.reshape(D_MODEL, 4, D_PROJ // 4).transpose(1, 0, 2).reshape(4, 2, 512, D_PROJ // 4)
    small = jnp.concatenate(
        [_pack_small(loc["ln_pre_mix"], loc["ln_post_mix"], loc["ln_pre_ffn"], loc["ln_post_ffn"],
                     loc["rel_bias"], loc["b_forget"], loc["sinks"])
         + jnp.pad(loc["loss"].reshape(1, 1), ((LOSS_ROW, 7 - LOSS_ROW), (0, D_MODEL - 1))), loc["meta"]], axis=0)

    (got_in,) = _run_alone(_swap_rider([dw_in]), "swap_halves_late")
    part_in = _pair_sum(dw_in, got_in, c_arr, "pair_sum_late")
    landed_in, small_all = _run_alone(_exchange_rider([part_in], small), "exchange_late")
    chip_arr = jnp.reshape(chip, (1,)).astype(jnp.int32)
    own_arr = jnp.stack([chip, ci]).astype(jnp.int32)
    mine = [_chip_sum(part_in, landed_in, chip_arr, "chip_sum_in")] + [
        _owner_sum(g, l, own_arr, "owner_sum_%d" % a) for a, (g, l) in enumerate(zip(early["grads"], loc["landed"]))]
    small_sum = _device_sum(small_all)
    theirs = _join_halves(mine)
    g_meta_tokens = lax.dynamic_slice(small_sum[8:N_SMALL], (0, chip * (D_MODEL // 4)), (N_META, D_MODEL // 4))
    g_small = small_sum[0:8]

    grad = _unpack_small(g_small)
    grad.update(meta_tokens=g_meta_tokens)
    delta, new_m, new_v = {}, {}, {}
    big = dict(w_in=(w_in, m_w_in, v_w_in), w_out=(w_out, m_w_out, v_w_out),
               w_gate_up=(w_gate_up, m_w_gate_up, v_w_gate_up), w_down=(w_down, m_w_down, v_w_down))
    for (name, (w, m, v)), g_mine, g_theirs in zip(big.items(), mine, theirs):
        g, d, nm, nv = _adamw_halves(w[0], g_mine, g_theirs, m[0], v[0], c_arr, "adamw_" + name)
        grad[name], delta[name], new_m[name], new_v[name] = g[None], d[None], nm[None], nv[None]
    delta["meta_tokens"], new_m["meta_tokens"], new_v["meta_tokens"] = _adamw(
        meta_tokens, g_meta_tokens, m_meta_tokens, v_meta_tokens, "adamw_meta")
    d, nm, nv = _adamw(
        _pack_small(ln_pre_mix, ln_post_mix, ln_pre_ffn, ln_post_ffn, rel_bias, b_forget, sinks), g_small,
        _pack_small(m_ln_pre_mix, m_ln_post_mix, m_ln_pre_ffn, m_ln_post_ffn, m_rel_bias, m_b_forget, m_sinks),
        _pack_small(v_ln_pre_mix, v_ln_post_mix, v_ln_pre_ffn, v_ln_post_ffn, v_rel_bias, v_b_forget, v_sinks),
        "adamw_small")
    delta.update(_unpack_small(d))
    new_m.update(_unpack_small(nm))
    new_v.update(_unpack_small(nv))

    loss = small_sum[LOSS_ROW, 0]
    return (loss,loc["grad_x"][None], *[grad[k] for k in WEIGHTS], *[delta[k] for k in WEIGHTS],
            *[new_m[k] for k in WEIGHTS], *[new_v[k] for k in WEIGHTS])
```

```python
import math

import numpy as np
import jax
import jax.numpy as jnp
from jax import lax
from jax.experimental import pallas as pl
from jax.experimental.pallas import tpu as pltpu

F32 = jnp.float32
BF16 = jnp.bfloat16
MESH = pl.DeviceIdType.MESH
SDS = jax.ShapeDtypeStruct

D_MODEL = 1024
SEQ = 4096
N_META = 16
N_HEADS = 8
HALF = 64
D_FF = 2816
N_BUCKETS = 32
EPS = 1e-6
NEG = -1e30
SCALE = 0.125
PAD_ROWS = 112
ROW0 = PAD_ROWS + N_META
LP = ROW0 + SEQ
BLK = 128
NBLK = LP // BLK
TM = 384
NT = LP // TM
TM_PURE = LP // 2
TM_MID = LP // 4
TM_EPI = LP // 6
TN = 256
D_PROJ = 2312
D_PROJ_P = 2432
D_QKV = 2304
FF_T = 1408
VMEM_LIMIT = 56 * 1024 * 1024

ADAM_LR = 0.001
ADAM_B1 = 0.9
ADAM_B2 = 0.999
ADAM_EPS = 1e-08
ADAM_WD = 0.01
ADAM_STEP = 10

QA, QB, KB, VB = 0, 1, 2, 3
KA, VA = 16, 17

NT_DIMS = (((1,), (1,)), ((), ()))
TN_DIMS = (((0,), (0,)), ((), ()))


def _cparams(sem):
    return pltpu.CompilerParams(dimension_semantics=sem, vmem_limit_bytes=VMEM_LIMIT)


def _t5_bucket_np(d):
    n = np.maximum(d, 0).astype(np.int32)
    nf = np.maximum(n, 1).astype(np.float32)
    large = 16 + (np.log(nf / np.float32(16)) / np.float32(math.log(8.0)) * np.float32(16)).astype(np.int32)
    large = np.minimum(large, N_BUCKETS - 1)
    return np.where(n < 16, n, large).astype(np.int32)


def _bucket_tables():
    qi = np.arange(BLK)[:, None]
    ki = np.arange(BLK)[None, :]
    return np.stack([_t5_bucket_np(qi - ki), _t5_bucket_np(qi - ki + BLK)])


def _rms(x):
    return lax.rsqrt(jnp.mean(x * x, axis=-1, keepdims=True) + EPS)


def _rms_bwd(n, r, gdy):
    return r * (gdy - n * jnp.mean(n * gdy, axis=-1, keepdims=True))


def _pre_mix(h0, gain, w_in_b):
    half = D_QKV // 2

    def body(h_ref, g_ref, w_ref, hn_ref, proj_ref, f_ref):
        x = h_ref[...]
        hn = (x * _rms(x) * g_ref[...]).astype(BF16)
        hn_ref[...] = hn
        proj_ref[:, :half] = jnp.dot(hn, w_ref[:, :half], preferred_element_type=F32).astype(BF16)
        p = jnp.dot(hn, w_ref[:, half:], preferred_element_type=F32)
        proj_ref[:, half:] = p[:, :half].astype(BF16)
        f_ref[...] = p[:, half:]

    return pl.pallas_call(
        body, grid=(LP // TM_MID,),
        in_specs=[pl.BlockSpec((TM_MID, D_MODEL), lambda i: (i, 0)),
                  pl.BlockSpec((1, D_MODEL), lambda i: (0, 0)),
                  pl.BlockSpec((D_MODEL, D_PROJ_P), lambda i: (0, 0))],
        out_specs=[pl.BlockSpec((TM_MID, D_MODEL), lambda i: (i, 0)),
                   pl.BlockSpec((TM_MID, D_QKV), lambda i: (i, 0)),
                   pl.BlockSpec((TM_MID, BLK), lambda i: (i, 0))],
        out_shape=[SDS((LP, D_MODEL), BF16), SDS((LP, D_QKV), BF16), SDS((LP, BLK), F32)],
        compiler_params=_cparams(("parallel",)), name="pre_mix")(h0, gain, w_in_b)


def _attn_out(o_a, o_b, w_out_b, h0, g_post, g_pre_ffn):
    def body(oa_ref, ob_ref, w_ref, h0_ref, gp_ref, gf_ref, a_ref, h1_ref, hn2_ref):
        a = (jnp.dot(oa_ref[...], w_ref[0:512, :], preferred_element_type=F32)
             + jnp.dot(ob_ref[...], w_ref[512:1024, :], preferred_element_type=F32))
        a_ref[...] = a
        h1 = h0_ref[...] + a * _rms(a) * gp_ref[...]
        h1_ref[...] = h1
        hn2_ref[...] = (h1 * _rms(h1) * gf_ref[...]).astype(BF16)

    row = lambda w: pl.BlockSpec((TM_EPI, w), lambda i: (i, 0))
    vec = pl.BlockSpec((1, D_MODEL), lambda i: (0, 0))
    return pl.pallas_call(
        body, grid=(LP // TM_EPI,),
        in_specs=[row(512), row(512), pl.BlockSpec((D_MODEL, D_MODEL), lambda i: (0, 0)), row(D_MODEL), vec, vec],
        out_specs=[row(D_MODEL), row(D_MODEL), row(D_MODEL)],
        out_shape=[SDS((LP, D_MODEL), F32), SDS((LP, D_MODEL), F32), SDS((LP, D_MODEL), BF16)],
        compiler_params=_cparams(("parallel",)), name="attn_out")(o_a, o_b, w_out_b, h0, g_post, g_pre_ffn)


def _ffn_up(hn2, w_gu_b):
    def body(x_ref, wg_ref, wu_ref, g_ref, u_ref, act_ref):
        x = x_ref[...]
        g = jnp.dot(x, wg_ref[...], preferred_element_type=F32)
        u = jnp.dot(x, wu_ref[...], preferred_element_type=F32)
        g_ref[...] = g.astype(BF16)
        u_ref[...] = u.astype(BF16)
        act_ref[...] = (g * (1.0 / (1.0 + jnp.exp(-g))) * u).astype(BF16)

    out = pl.BlockSpec((TM_PURE, TN), lambda i, j: (i, j))
    return pl.pallas_call(
        body, grid=(LP // TM_PURE, D_FF // TN),
        in_specs=[pl.BlockSpec((TM_PURE, D_MODEL), lambda i, j: (i, 0)),
                  pl.BlockSpec((D_MODEL, TN), lambda i, j: (0, j)),
                  pl.BlockSpec((D_MODEL, TN), lambda i, j: (0, j + D_FF // TN))],
        out_specs=[out, out, out],
        out_shape=[SDS((LP, D_FF), BF16)] * 3,
        compiler_params=_cparams(("parallel", "parallel")), name="ffn_up")(hn2, w_gu_b, w_gu_b)


def _ffn_down_loss(act, w_dn_b, h1, tgt, g_post_ffn):
    def body(act_ref, w_ref, h1_ref, t0_ref, t1_ref, t2_ref, g_ref, dff_ref, dy_ref, loss_ref, dg_ref):
        i = pl.program_id(0)
        target = jnp.concatenate([t0_ref[...], t1_ref[...], t2_ref[...]], axis=0)

        @pl.when(i == 0)
        def _():
            loss_ref[...] = jnp.zeros_like(loss_ref)
            dg_ref[...] = jnp.zeros_like(dg_ref)

        ff = jnp.dot(act_ref[...], w_ref[...], preferred_element_type=F32)
        r = _rms(ff)
        n = ff * r
        g = g_ref[...]
        y = h1_ref[...] + n * g
        rows = i * TM + lax.broadcasted_iota(jnp.int32, (TM, D_MODEL), 0)
        diff = jnp.where(rows >= ROW0, y - target, 0.0)
        loss_ref[...] += 0.5 * jnp.sum(diff * diff) / D_MODEL
        dy = diff / D_MODEL
        dy_ref[...] = dy
        dg_ref[...] += jnp.sum(dy * n, axis=0, keepdims=True)
        dff_ref[...] = _rms_bwd(n, r, g * dy).astype(BF16)

    row = pl.BlockSpec((TM, D_MODEL), lambda i: (i, 0))
    tblk = lambda j: pl.BlockSpec((BLK, D_MODEL), lambda i: (jnp.maximum(3 * i - 1 + j, 0), 0))
    return pl.pallas_call(
        body, grid=(NT,),
        in_specs=[pl.BlockSpec((TM, D_FF), lambda i: (i, 0)), pl.BlockSpec((D_FF, D_MODEL), lambda i: (0, 0)),
                  row, tblk(0), tblk(1), tblk(2), pl.BlockSpec((1, D_MODEL), lambda i: (0, 0))],
        out_specs=[row, row, pl.BlockSpec((8, BLK), lambda i: (0, 0)), pl.BlockSpec((1, D_MODEL), lambda i: (0, 0))],
        out_shape=[SDS((LP, D_MODEL), BF16), SDS((LP, D_MODEL), F32), SDS((8, BLK), F32), SDS((1, D_MODEL), F32)],
        compiler_params=_cparams(("arbitrary",)), name="ffn_down_loss")(act, w_dn_b, h1, tgt, tgt, tgt, g_post_ffn)


def _ffn_down_bwd(dff, w_dn_b, g, u):
    def body(d_ref, w_ref, g_ref, u_ref, dg_ref, du_ref):
        dact = lax.dot_general(d_ref[...], w_ref[...], NT_DIMS, preferred_element_type=F32)
        gg = g_ref[...].astype(F32)
        sig = 1.0 / (1.0 + jnp.exp(-gg))
        dg_ref[...] = (dact * u_ref[...].astype(F32) * sig * (1.0 + gg * (1.0 - sig))).astype(BF16)
        du_ref[...] = (dact * gg * sig).astype(BF16)

    blk = pl.BlockSpec((TM_PURE, TN), lambda i, j: (i, j))
    return pl.pallas_call(
        body, grid=(LP // TM_PURE, D_FF // TN),
        in_specs=[pl.BlockSpec((TM_PURE, D_MODEL), lambda i, j: (i, 0)),
                  pl.BlockSpec((TN, D_MODEL), lambda i, j: (j, 0)), blk, blk],
        out_specs=[blk, blk],
        out_shape=[SDS((LP, D_FF), BF16)] * 2,
        compiler_params=_cparams(("parallel", "parallel")), name="ffn_down_bwd")(dff, w_dn_b, g, u)


def _ffn_up_bwd(dg, du, w_gu_b, h1, a, dy, g_pre_ffn, g_post_mix):
    def body(dg_ref, du_ref, w_ref, h1_ref, a_ref, dy_ref, gf_ref, gp_ref,
             dh1_ref, da_ref, dgf_ref, dgp_ref, acc):
        i = pl.program_id(0)
        s = pl.program_id(1)

        @pl.when((i == 0) & (s == 0))
        def _():
            dgf_ref[...] = jnp.zeros_like(dgf_ref)
            dgp_ref[...] = jnp.zeros_like(dgp_ref)

        @pl.when(s == 0)
        def _():
            acc[...] = jnp.zeros_like(acc)

        @pl.when(s < 2)
        def _():
            acc[...] += lax.dot_general(dg_ref[...], w_ref[...], NT_DIMS, preferred_element_type=F32)

        @pl.when(s >= 2)
        def _():
            acc[...] += lax.dot_general(du_ref[...], w_ref[...], NT_DIMS, preferred_element_type=F32)

        @pl.when(s == 3)
        def _():
            dhn2 = acc[...]
            h1 = h1_ref[...]
            r2 = _rms(h1)
            n2 = h1 * r2
            dgf_ref[...] += jnp.sum(dhn2 * n2, axis=0, keepdims=True)
            dh1 = dy_ref[...] + _rms_bwd(n2, r2, gf_ref[...] * dhn2)
            dh1_ref[...] = dh1
            av = a_ref[...]
            ra = _rms(av)
            na = av * ra
            dgp_ref[...] += jnp.sum(dh1 * na, axis=0, keepdims=True)
            da_ref[...] = _rms_bwd(na, ra, gp_ref[...] * dh1).astype(BF16)

    row = pl.BlockSpec((TM_EPI, D_MODEL), lambda i, s: (i, 0))
    vec = pl.BlockSpec((1, D_MODEL), lambda i, s: (0, 0))
    return pl.pallas_call(
        body, grid=(LP // TM_EPI, 4),
        in_specs=[pl.BlockSpec((TM_EPI, FF_T), lambda i, s: (i, jnp.minimum(s, 1))),
                  pl.BlockSpec((TM_EPI, FF_T), lambda i, s: (i, jnp.maximum(s - 2, 0))),
                  pl.BlockSpec((D_MODEL, FF_T), lambda i, s: (0, s)),
                  row, row, row, vec, vec],
        out_specs=[row, row, vec, vec],
        out_shape=[SDS((LP, D_MODEL), F32), SDS((LP, D_MODEL), BF16), SDS((1, D_MODEL), F32), SDS((1, D_MODEL), F32)],
        scratch_shapes=[pltpu.VMEM((TM_EPI, D_MODEL), F32)],
        compiler_params=_cparams(("arbitrary", "arbitrary")), name="ffn_up_bwd",
    )(dg, du, w_gu_b, h1, a, dy, g_pre_ffn, g_post_mix)


def _attn_out_bwd(da, w_out_b):
    def body(d_ref, w_ref, o_ref):
        o_ref[...] = lax.dot_general(d_ref[...], w_ref[...], NT_DIMS, preferred_element_type=F32).astype(BF16)

    row = pl.BlockSpec((TM_PURE, D_MODEL), lambda i: (i, 0))
    return pl.pallas_call(
        body, grid=(LP // TM_PURE,),
        in_specs=[row, pl.BlockSpec((D_MODEL, D_MODEL), lambda i: (0, 0))],
        out_specs=row, out_shape=SDS((LP, D_MODEL), BF16),
        compiler_params=_cparams(("parallel",)), name="attn_out_bwd")(da, w_out_b)


def _pre_mix_bwd(dq_a, dq_b, dk_b, dv_b, dk_a, dv_a, df, w_in_b, h0, dh1, g_pre_mix):
    def body(qa_ref, qb_ref, kb_ref, vb_ref, ka_ref, va_ref, f_ref, w_ref, h0_ref, dh1_ref, g_ref,
             dproj_ref, dh0_ref, dg_ref):
        i = pl.program_id(0)

        @pl.when(i == 0)
        def _():
            dg_ref[...] = jnp.zeros_like(dg_ref)

        dproj = jnp.concatenate(
            [qa_ref[...], (qb_ref[...] * SCALE).astype(BF16), kb_ref[...], vb_ref[...],
             ka_ref[...].astype(BF16), va_ref[...].astype(BF16), f_ref[...].astype(BF16)], axis=1)
        dproj_ref[...] = dproj
        dhn = lax.dot_general(dproj, w_ref[...], NT_DIMS, preferred_element_type=F32)
        x = h0_ref[...]
        r = _rms(x)
        n = x * r
        dg_ref[...] += jnp.sum(dhn * n, axis=0, keepdims=True)
        dh0_ref[...] = dh1_ref[...] + _rms_bwd(n, r, g_ref[...] * dhn)

    row = lambda w: pl.BlockSpec((TM_EPI, w), lambda i: (i, 0))
    vec = pl.BlockSpec((1, D_MODEL), lambda i: (0, 0))
    return pl.pallas_call(
        body, grid=(LP // TM_EPI,),
        in_specs=[row(512), row(512), row(512), row(512), row(BLK), row(BLK), row(BLK),
                  pl.BlockSpec((D_MODEL, D_PROJ_P), lambda i: (0, 0)), row(D_MODEL), row(D_MODEL), vec],
        out_specs=[row(D_PROJ_P), row(D_MODEL), vec],
        out_shape=[SDS((LP, D_PROJ_P), BF16), SDS((LP, D_MODEL), F32), SDS((1, D_MODEL), F32)],
        compiler_params=_cparams(("arbitrary",)), name="pre_mix_bwd",
    )(dq_a, dq_b, dk_b, dv_b, dk_a, dv_a, df, w_in_b, h0, dh1, g_pre_mix)


def _mm_tn(parts, b, tm, name, out_dtype=F32):
    widths = [p.shape[1] for p in parts]
    m_total = sum(widths)
    n = b.shape[1]
    whole = len(parts) > 1
    n_k = LP // TM_MID
    assert (tm == m_total) if whole else (m_total % tm == 0)

    def body(*refs):
        a_refs, b_ref, o_ref, acc = refs[:-3], refs[-3], refs[-2], refs[-1]
        k = pl.program_id(1)

        @pl.when(k == 0)
        def _():
            acc[...] = jnp.zeros_like(acc)
        a = a_refs[0][...] if not whole else jnp.concatenate([r[...] for r in a_refs], axis=1)
        acc[...] += lax.dot_general(a, b_ref[...], TN_DIMS, preferred_element_type=F32)

        @pl.when(k == n_k - 1)
        def _():
            o_ref[...] = acc[...].astype(out_dtype)

    a_specs = ([pl.BlockSpec((TM_MID, w), lambda mi, k: (k, 0)) for w in widths] if whole
               else [pl.BlockSpec((TM_MID, tm), lambda mi, k: (k, mi))])
    return pl.pallas_call(
        body, grid=(m_total // tm, n_k),
        in_specs=a_specs + [pl.BlockSpec((TM_MID, n), lambda mi, k: (k, 0))],
        out_specs=pl.BlockSpec((tm, n), lambda mi, k: (mi, 0)),
        out_shape=SDS((m_total, n), out_dtype),
        scratch_shapes=[pltpu.VMEM((tm, n), F32)],
        compiler_params=_cparams(("parallel", "arbitrary")), name=name)(*parts, b)


def _dw_gate_up(hn2, dg, du):
    n_k = LP // TM_MID

    def body(a_ref, dg_ref, du_ref, o_ref, acc):
        s = pl.program_id(0)
        k = pl.program_id(1)

        @pl.when(k == 0)
        def _():
            acc[...] = jnp.zeros_like(acc)

        @pl.when(s < 2)
        def _():
            acc[...] += lax.dot_general(a_ref[...], dg_ref[...], TN_DIMS, preferred_element_type=F32)

        @pl.when(s >= 2)
        def _():
            acc[...] += lax.dot_general(a_ref[...], du_ref[...], TN_DIMS, preferred_element_type=F32)

        @pl.when(k == n_k - 1)
        def _():
            o_ref[0] = acc[...].astype(BF16)

    return pl.pallas_call(
        body, grid=(4, n_k),
        in_specs=[pl.BlockSpec((TM_MID, D_MODEL), lambda s, k: (k, 0)),
                  pl.BlockSpec((TM_MID, FF_T), lambda s, k: (k, jnp.minimum(s, 1))),
                  pl.BlockSpec((TM_MID, FF_T), lambda s, k: (k, jnp.maximum(s - 2, 0)))],
        out_specs=pl.BlockSpec((1, D_MODEL, FF_T), lambda s, k: (s, 0, 0)),
        out_shape=SDS((4, D_MODEL, FF_T), BF16),
        scratch_shapes=[pltpu.VMEM((D_MODEL, FF_T), F32)],
        compiler_params=_cparams(("parallel", "arbitrary")), name="dw_gate_up")(hn2, dg, du)


def _split3(x):
    hi = x.astype(BF16)
    r1 = x - hi.astype(F32)
    mid = r1.astype(BF16)
    lo = (r1 - mid.astype(F32)).astype(BF16)
    return hi, mid, lo


def _tri_matmul(tri, x):
    hi, mid, lo = _split3(x)
    dot = lambda t: jnp.dot(tri, t, preferred_element_type=F32)
    return dot(hi) + dot(mid) + dot(lo)


def _forget_cumsum(f, b_forget_p):
    def body(f_ref, b_ref, cum_ref, carry):
        i = pl.program_id(0)

        @pl.when(i == 0)
        def _():
            carry[...] = jnp.zeros_like(carry)

        z = f_ref[...] + b_ref[...]
        ls = jnp.minimum(z, 0.0) - jnp.log(1.0 + jnp.exp(-jnp.abs(z)))
        rows = i * TM + lax.broadcasted_iota(jnp.int32, (TM, BLK), 0)
        ls = jnp.where(rows >= PAD_ROWS, ls, 0.0)
        r = lax.broadcasted_iota(jnp.int32, (TM, TM), 0)
        c = lax.broadcasted_iota(jnp.int32, (TM, TM), 1)
        tri = (c <= r).astype(BF16)
        cum = _tri_matmul(tri, ls) + carry[...]
        cum_ref[...] = cum
        carry[...] = cum[TM - 1:TM, :]

    return pl.pallas_call(
        body, grid=(NT,),
        in_specs=[pl.BlockSpec((TM, BLK), lambda i: (i, 0)), pl.BlockSpec((1, BLK), lambda i: (0, 0))],
        out_specs=pl.BlockSpec((TM, BLK), lambda i: (i, 0)),
        out_shape=SDS((LP, BLK), F32),
        scratch_shapes=[pltpu.VMEM((1, BLK), F32)],
        compiler_params=_cparams(("arbitrary",)), name="forget_cumsum")(f, b_forget_p)


def _forget_cumsum_bwd(dcum, f, b_forget_p):
    def body(d_ref, f_ref, b_ref, df_ref, db_ref, carry):
        i = pl.program_id(0)

        @pl.when(i == 0)
        def _():
            carry[...] = jnp.zeros_like(carry)
            db_ref[...] = jnp.zeros_like(db_ref)

        blk = NT - 1 - i
        r = lax.broadcasted_iota(jnp.int32, (TM, TM), 0)
        c = lax.broadcasted_iota(jnp.int32, (TM, TM), 1)
        tri = (c >= r).astype(BF16)
        d = d_ref[...]
        dls = _tri_matmul(tri, d) + carry[...]
        carry[...] = dls[0:1, :]
        z = f_ref[...] + b_ref[...]
        rows = blk * TM + lax.broadcasted_iota(jnp.int32, (TM, BLK), 0)
        df = jnp.where(rows >= PAD_ROWS, dls / (1.0 + jnp.exp(z)), 0.0)
        df_ref[...] = df
        db_ref[...] += jnp.sum(df, axis=0, keepdims=True)

    rev = pl.BlockSpec((TM, BLK), lambda i: (NT - 1 - i, 0))
    vec = pl.BlockSpec((1, BLK), lambda i: (0, 0))
    return pl.pallas_call(
        body, grid=(NT,),
        in_specs=[rev, rev, vec],
        out_specs=[rev, vec],
        out_shape=[SDS((LP, BLK), F32), SDS((1, BLK), F32)],
        scratch_shapes=[pltpu.VMEM((1, BLK), F32)],
        compiler_params=_cparams(("arbitrary",)), name="forget_cumsum_bwd")(dcum, f, b_forget_p)


def _lane_half(rows):
    return lax.broadcasted_iota(jnp.int32, (rows, BLK), 1) // HALF


def _fox_valid(qi, kj):
    qrow = qi * TM + lax.broadcasted_iota(jnp.int32, (TM, TM), 0)
    krow = kj * TM + lax.broadcasted_iota(jnp.int32, (TM, TM), 1)
    return (krow <= qrow) & ((krow >= PAD_ROWS) | (qrow < PAD_ROWS))


class _Rider:
    def __init__(self, operands, out_shapes, sem_counts, first, middle, last):
        self.operands, self.out_shapes, self.sem_counts = list(operands), list(out_shapes), list(sem_counts)
        self.first, self.middle, self.last = first, middle, last

    def scratch(self):
        return [pltpu.SemaphoreType.DMA((k,)) for k in self.sem_counts]

    def split(self, refs, n_in, n_out, n_scratch):
        a, b = len(self.operands), len(self.out_shapes)
        ins, mine_in = refs[:n_in], refs[n_in:n_in + a]
        outs, mine_out = refs[n_in + a:n_in + a + n_out], refs[n_in + a + n_out:n_in + a + n_out + b]
        rest = refs[n_in + a + n_out + b:]
        return ins, outs, rest[:n_scratch], (mine_in, mine_out, rest[n_scratch:])

    def at_steps(self, mine, is_first, is_middle, is_last):
        for cond, fn in ((is_first, self.first), (is_middle, self.middle), (is_last, self.last)):
            pl.when(cond)(lambda fn=fn: fn(*mine))


HBM_SPEC = pl.BlockSpec(memory_space=pltpu.HBM)


N_AUG = 4
QCH = 128
KSUB = 384
AHEAD = 5
AHEAD_BWD = 1


def _fox_prep(proj, cum):
    def body(q_ref, k_ref, v_ref, c_ref, qa_ref, ka_ref, vt_ref):
        half = _lane_half(TM)
        lane = lax.broadcasted_iota(jnp.int32, (TM, BLK), 1)
        for pp in range(4):
            cols = slice(pp * BLK, (pp + 1) * BLK)
            qs = q_ref[:, cols].astype(F32) * SCALE
            kp = k_ref[:, cols].astype(F32)
            vp = v_ref[:, cols]
            vt_ref[cols, :] = vp.astype(F32).T.astype(BF16)
            for e in range(2):
                h = 2 * pp + e
                a = (1 - e) * HALF
                blk = slice(h * BLK, (h + 1) * BLK)
                hi, mid, lo = _split3(-c_ref[:, h:h + 1])
                q_aug = jnp.where(half == e, qs, jnp.where((lane >= a) & (lane < a + 3), 1.0, 0.0))
                k_aug = jnp.where(half == e, kp, jnp.where(
                    lane == a, hi.astype(F32), jnp.where(lane == a + 1, mid.astype(F32), jnp.where(
                        lane == a + 2, lo.astype(F32), jnp.where(lane == a + 3, 1.0, 0.0)))))
                qa_ref[:, blk] = q_aug.astype(BF16)
                ka_ref[:, blk] = k_aug.astype(BF16)

    row = lambda blk: pl.BlockSpec((TM, 512), lambda i: (i, blk))
    wide = pl.BlockSpec((TM, 1024), lambda i: (i, 0))
    return pl.pallas_call(
        body, grid=(NT,),
        in_specs=[row(QB), row(KB), row(VB), pl.BlockSpec((TM, BLK), lambda i: (i, 0))],
        out_specs=[wide, wide, pl.BlockSpec((512, TM), lambda i: (0, i))],
        out_shape=[SDS((LP, 1024), BF16)] * 2 + [SDS((512, LP), BF16)],
        compiler_params=_cparams(("parallel",)), name="fox_prep")(proj, proj, proj, cum)


def _over_keys(reduce, x):
    slabs = x.reshape(x.shape[0] // HALF, HALF, x.shape[1])
    return reduce(reduce(slabs, axis=0), axis=0, keepdims=True)


def _fox_valid_t(qi, kj, c, r):
    krow = kj * TM + r * KSUB + lax.broadcasted_iota(jnp.int32, (KSUB, QCH), 0)
    qrow = qi * TM + c * QCH + lax.broadcasted_iota(jnp.int32, (KSUB, QCH), 1)
    return (krow <= qrow) & ((krow >= PAD_ROWS) | (qrow < PAD_ROWS))


def _fox_fwd(q_aug, k_aug, v_t, rider):
    pairs = [(qi, kj) for qi in range(NT) for kj in range(qi + 1)]
    n_pairs = len(pairs)

    def body(qi_ref, kj_ref, *refs):
        (q_ref, k_ref, vt_ref), (o_ref, lse_ref), (m_s, l_s, acc_s), mine = rider.split(refs, 3, 2, 3)
        n = pl.program_id(0)
        qi = qi_ref[n]
        kj = kj_ref[n]
        rider.at_steps(mine, n == 0, n == n_pairs // 2, n == n_pairs - 1)

        @pl.when(kj == 0)
        def _():
            m_s[...] = jnp.full_like(m_s, NEG)
            l_s[...] = jnp.zeros_like(l_s)
            acc_s[...] = jnp.zeros_like(acc_s)

        def tile(masked):
            steps = [(h, c, r) for h in range(N_HEADS) for c in range(TM // QCH) for r in range(TM // KSUB)]

            def scores(h, c, r):
                blk = slice(h * BLK, (h + 1) * BLK)
                return lax.dot_general(k_ref[r * KSUB:(r + 1) * KSUB, blk], q_ref[c * QCH:(c + 1) * QCH, blk],
                                       NT_DIMS, preferred_element_type=F32)

            ahead = [scores(*st) for st in steps[:AHEAD]]
            for n, (h, c, r) in enumerate(steps):
                s_t = ahead.pop(0)
                if n + AHEAD < len(steps):
                    ahead.append(scores(*steps[n + AHEAD]))
                cs = slice(c * QCH, (c + 1) * QCH)
                if masked:
                    s_t = jnp.where(_fox_valid_t(qi, kj, c, r), s_t, NEG)
                m_prev = m_s[h, :, cs]
                m_new = jnp.maximum(m_prev, _over_keys(jnp.max, s_t))
                p_t = jnp.exp(s_t - m_new)
                alpha = jnp.exp(m_prev - m_new)
                l_s[h, :, cs] = alpha * l_s[h, :, cs] + _over_keys(jnp.sum, p_t)
                m_s[h, :, cs] = m_new
                vt = vt_ref[h * HALF:(h + 1) * HALF, r * KSUB:(r + 1) * KSUB]
                acc_s[h, :, cs] = acc_s[h, :, cs] * alpha + jnp.dot(vt, p_t.astype(BF16),
                                                                    preferred_element_type=F32)

        @pl.when((kj < qi) & (kj > 0))
        def _():
            tile(False)

        @pl.when((kj == qi) | (kj == 0))
        def _():
            tile(True)

        @pl.when(kj == qi)
        def _():
            for pp in range(4):
                both = jnp.concatenate([acc_s[2 * pp] * (1.0 / l_s[2 * pp]),
                                        acc_s[2 * pp + 1] * (1.0 / l_s[2 * pp + 1])], axis=0)
                o_ref[:, pp * BLK:(pp + 1) * BLK] = both.T.astype(BF16)
            for h in range(N_HEADS):
                lse_ref[h] = m_s[h] + jnp.log(l_s[h])

    grid_spec = pltpu.PrefetchScalarGridSpec(
        num_scalar_prefetch=2, grid=(n_pairs,),
        in_specs=[pl.BlockSpec((TM, 1024), lambda n, qi, kj: (qi[n], 0)),
                  pl.BlockSpec((TM, 1024), lambda n, qi, kj: (kj[n], 0)),
                  pl.BlockSpec((512, TM), lambda n, qi, kj: (0, kj[n]))] + [HBM_SPEC] * len(rider.operands),
        out_specs=[pl.BlockSpec((TM, 512), lambda n, qi, kj: (qi[n], 0)),
                   pl.BlockSpec((N_HEADS, 1, TM), lambda n, qi, kj: (0, 0, qi[n]))]
        + [HBM_SPEC] * len(rider.out_shapes),
        scratch_shapes=[pltpu.VMEM((N_HEADS, 1, TM), F32), pltpu.VMEM((N_HEADS, 1, TM), F32),
                        pltpu.VMEM((N_HEADS, HALF, TM), F32)] + rider.scratch())
    o_b, lse, *carried = pl.pallas_call(
        body, grid_spec=grid_spec,
        out_shape=[SDS((LP, 512), BF16), SDS((N_HEADS, 1, LP), F32)] + rider.out_shapes,
        compiler_params=_cparams(("arbitrary",)), name="fox_fwd",
    )(jnp.asarray([p[0] for p in pairs], jnp.int32), jnp.asarray([p[1] for p in pairs], jnp.int32),
      q_aug, k_aug, v_t, *rider.operands)
    return o_b, lse, carried


def _fox_bwd(proj, o_b, dmix, lse, ck_t, rider):
    pairs = [(kj, qi) for kj in range(NT) for qi in range(kj, NT)]
    n_pairs = len(pairs)

    def body(kj_ref, qi_ref, *refs):
        ((q_ref, k_ref, v_ref, o_ref, do_ref, lse_ref, ck_ref), (dq_ref, dk_ref, dv_ref, dck_ref, dcq_ref),
         (dk_s, dv_s, dck_s), mine) = rider.split(refs, 7, 5, 3)
        n = pl.program_id(0)
        kj = kj_ref[n]
        qi = qi_ref[n]
        rider.at_steps(mine, n == 0, n == n_pairs // 2, n == n_pairs - 1)

        @pl.when(n == 0)
        def _():
            dq_ref[...] = jnp.zeros_like(dq_ref)
            dcq_ref[...] = jnp.zeros_like(dcq_ref)

        @pl.when(qi == kj)
        def _():
            dk_s[...] = jnp.zeros_like(dk_s)
            dv_s[...] = jnp.zeros_like(dv_s)
            dck_s[...] = jnp.zeros_like(dck_s)

        def tile(masked):
            valid = _fox_valid(qi, kj) if masked else None
            half = _lane_half(TM)
            q0 = pl.multiple_of(qi * TM, TM)
            lane = lax.broadcasted_iota(jnp.int32, (TM, BLK), 1)
            row_sums = jnp.zeros((TM, BLK), F32)
            pair_ops = {}

            def operands(pp):
                if pp not in pair_ops:
                    cols = slice(pp * BLK, (pp + 1) * BLK)
                    pair_ops[pp] = ((q_ref[:, cols].astype(F32) * SCALE).astype(BF16), k_ref[:, cols],
                                    v_ref[:, cols], do_ref[:, cols])
                return pair_ops[pp]

            def scores(pp, e):
                qs, kp, vp, dop = operands(pp)
                ke = jnp.where(half == e, kp, jnp.zeros_like(kp))
                ve = jnp.where(half == e, vp, jnp.zeros_like(vp))
                return (lax.dot_general(qs, ke, NT_DIMS, preferred_element_type=F32),
                        lax.dot_general(dop, ve, NT_DIMS, preferred_element_type=F32), ke)

            steps = [(pp, e) for pp in range(4) for e in range(2)]
            ahead = [scores(*st) for st in steps[:AHEAD_BWD]]
            for n, (pp, e) in enumerate(steps):
                raw, dp, ke = ahead.pop(0)
                if n + AHEAD_BWD < len(steps):
                    ahead.append(scores(*steps[n + AHEAD_BWD]))
                h = 2 * pp + e
                cols = slice(pp * BLK, (pp + 1) * BLK)
                qs, kp, vp, dop = operands(pp)
                if e == 0:
                    prod = dop.astype(F32) * o_ref[:, cols].astype(F32)
                    d0 = jnp.sum(jnp.where(half == 0, prod, 0.0), axis=1, keepdims=True)
                    d1 = jnp.sum(prod, axis=1, keepdims=True) - d0
                    dq = jnp.zeros((TM, BLK), F32)
                    dks, dvs = [], []
                t = raw - ck_ref[h] - lse_ref[h]
                if masked:
                    t = jnp.where(valid, t, NEG)
                p = jnp.exp(t)
                ds = p * (dp - (d0 if e == 0 else d1))
                dck_s[h] += jnp.sum(ds, axis=0, keepdims=True)
                row_sums = jnp.where(lane == h, jnp.sum(ds, axis=1, keepdims=True), row_sums)
                ds_b = ds.astype(BF16)
                dq = dq + jnp.dot(ds_b, ke, preferred_element_type=F32)
                dks.append(lax.dot_general(ds_b, qs, TN_DIMS, preferred_element_type=F32))
                dvs.append(lax.dot_general(p.astype(BF16), dop, TN_DIMS, preferred_element_type=F32))
                if e == 1:
                    dq_ref[pl.ds(q0, TM), cols] += dq
                    dk_s[pp] += jnp.where(half == 0, dks[0], dks[1])
                    dv_s[pp] += jnp.where(half == 0, dvs[0], dvs[1])
            dcq_ref[pl.ds(q0, TM), :] += row_sums

        @pl.when((qi > kj) & (kj > 0))
        def _():
            tile(False)

        @pl.when((qi == kj) | (kj == 0))
        def _():
            tile(True)

        @pl.when(qi == NT - 1)
        def _():
            for pp in range(4):
                cols = slice(pp * BLK, (pp + 1) * BLK)
                dk_ref[:, cols] = dk_s[pp].astype(BF16)
                dv_ref[:, cols] = dv_s[pp].astype(BF16)
            dck_ref[...] = dck_s[...]

    qrow = lambda blk: pl.BlockSpec((TM, 512), lambda n, kj, qi: (qi[n], blk))
    krow = lambda blk: pl.BlockSpec((TM, 512), lambda n, kj, qi: (kj[n], blk))
    grid_spec = pltpu.PrefetchScalarGridSpec(
        num_scalar_prefetch=2, grid=(n_pairs,),
        in_specs=[qrow(QB), krow(KB), krow(VB), qrow(0), qrow(1),
                  pl.BlockSpec((N_HEADS, TM, 1), lambda n, kj, qi: (0, qi[n], 0)),
                  pl.BlockSpec((N_HEADS, 1, TM), lambda n, kj, qi: (0, 0, kj[n]))] + [HBM_SPEC] * len(rider.operands),
        out_specs=[pl.BlockSpec((LP, 512), lambda n, kj, qi: (0, 0)),
                   pl.BlockSpec((TM, 512), lambda n, kj, qi: (kj[n], 0)),
                   pl.BlockSpec((TM, 512), lambda n, kj, qi: (kj[n], 0)),
                   pl.BlockSpec((N_HEADS, 1, TM), lambda n, kj, qi: (0, 0, kj[n])),
                   pl.BlockSpec((LP, BLK), lambda n, kj, qi: (0, 0))] + [HBM_SPEC] * len(rider.out_shapes),
        scratch_shapes=[pltpu.VMEM((4, TM, BLK), F32), pltpu.VMEM((4, TM, BLK), F32),
                        pltpu.VMEM((N_HEADS, 1, TM), F32)] + rider.scratch())
    dq, dk, dv, dck, dcq, *carried = pl.pallas_call(
        body, grid_spec=grid_spec,
        out_shape=[SDS((LP, 512), F32), SDS((LP, 512), BF16), SDS((LP, 512), BF16), SDS((N_HEADS, 1, LP), F32),
                   SDS((LP, BLK), F32)] + rider.out_shapes,
        compiler_params=_cparams(("arbitrary",)), name="fox_bwd",
    )(jnp.asarray([p[0] for p in pairs], jnp.int32), jnp.asarray([p[1] for p in pairs], jnp.int32),
      proj, proj, proj, o_b, dmix, lse, ck_t, *rider.operands)
    return dq, dk, dv, dck, dcq, carried


N_SEG = 3
N_KEY = N_SEG * BLK
GROUP = 4
QW = GROUP * BLK


def _bucket_tables_t():
    return np.ascontiguousarray(_bucket_tables().transpose(0, 2, 1))


def _stack_heads(ref, g, scale):
    half = _lane_half(BLK)
    out = []
    for pair in range(2):
        x = ref[:, (2 * g + pair) * BLK:(2 * g + pair + 1) * BLK].astype(F32) * scale
        swapped = pltpu.roll(x, HALF, 1)
        for e in range(2):
            out.append(jnp.where(half == g, x if e == g else swapped, 0.0).astype(BF16))
    return jnp.concatenate(out, axis=0)


def _unstack_heads(x_t, g, ref, scale):
    for pair in range(2):
        both = jnp.concatenate([x_t[:, (2 * pair) * BLK:(2 * pair + 1) * BLK],
                                x_t[:, (2 * pair + 1) * BLK:(2 * pair + 2) * BLK]], axis=0)
        ref[:, (2 * g + pair) * BLK:(2 * g + pair + 1) * BLK] = (both.T * scale).astype(ref.dtype)


def _swa_tables(tab_ref, sink_ref, bkt_ref, tbl, sink_row):
    kk = lax.broadcasted_iota(jnp.int32, (BLK, BLK), 0)
    qq = lax.broadcasted_iota(jnp.int32, (BLK, BLK), 1)
    neg = jnp.full((BLK, BLK), NEG, F32)
    lane = lax.broadcasted_iota(jnp.int32, (1, QW), 1) // BLK
    for g in range(2):
        row = jnp.zeros((1, QW), F32)
        for hh in range(GROUP):
            h = GROUP * g + hh
            cols = slice(hh * BLK, (hh + 1) * BLK)
            row = jnp.where(lane == hh, sink_ref[0, h], row)

            def step(b, carry, h=h):
                t = tab_ref[b, h]
                return jnp.where(bkt_ref[0] == b, t, carry[0]), jnp.where(bkt_ref[1] == b, t, carry[1])
            zero = jnp.zeros((BLK, BLK), F32)
            cur, prev = lax.fori_loop(0, N_BUCKETS, step, (zero, zero))
            far = jnp.full((BLK, BLK), tab_ref[N_BUCKETS - 1, h], F32)
            causal = jnp.where(kk <= qq, cur, neg)
            segments = [
                (neg, neg, jnp.where(kk >= PAD_ROWS, causal, neg)),
                (jnp.where(kk >= PAD_ROWS, prev, neg), neg, causal),
                (jnp.where(kk >= PAD_ROWS, far, neg), jnp.where(kk > qq, prev, neg), causal)]
            for case in range(3):
                for seg in range(N_SEG):
                    tbl[case, g, seg * BLK:(seg + 1) * BLK, cols] = segments[case][seg]
        sink_row[g] = row


def _swa_prep(proj):
    rows = LP // 3

    def body(k_ref, v_ref, kt_ref, vt_ref):
        kt_ref[...] = k_ref[...].astype(F32).T.astype(BF16)
        vt_ref[...] = v_ref[...].astype(F32).T.astype(BF16)

    col = pl.BlockSpec((BLK, rows), lambda i: (0, i))
    return pl.pallas_call(
        body, grid=(3,),
        in_specs=[pl.BlockSpec((rows, BLK), lambda i: (i, KA)), pl.BlockSpec((rows, BLK), lambda i: (i, VA))],
        out_specs=[col, col], out_shape=[SDS((BLK, LP), BF16)] * 2,
        compiler_params=_cparams(("parallel",)), name="swa_prep")(proj, proj)


def _seg_specs(rows_major, col):
    idx = [lambda i: 0, lambda i: jnp.maximum(i - 1, 0), lambda i: i]
    if rows_major:
        return [pl.BlockSpec((BLK, BLK), lambda i, f=f: (f(i), col)) for f in idx]
    return [pl.BlockSpec((BLK, BLK), lambda i, f=f: (0, f(i))) for f in idx]


def _swa_fwd(proj, vt_a, rel_bias, sinks, bkt_t, rider):
    def body(*refs):
        ((tab_ref, sink_ref, bkt_ref, q_ref, km_ref, kp_ref, kc_ref, vm_ref, vp_ref, vc_ref), (o_ref, lse_ref),
         (tbl, sink_row), mine) = rider.split(refs, 10, 2, 2)
        i = pl.program_id(0)
        rider.at_steps(mine, i == 0, i == NBLK // 2, i == NBLK - 1)

        @pl.when(i == 0)
        def _():
            _swa_tables(tab_ref, sink_ref, bkt_ref, tbl, sink_row)

        case = jnp.minimum(i, 2)
        k_cat = jnp.concatenate([km_ref[...], kp_ref[...], kc_ref[...]], axis=0)
        vt_cat = jnp.concatenate([vm_ref[...], vp_ref[...], vc_ref[...]], axis=1)
        raw = [lax.dot_general(k_cat, _stack_heads(q_ref, g, SCALE), NT_DIMS, preferred_element_type=F32)
               for g in range(2)]
        for g in range(2):
            s_t = raw[g] + tbl[case, g]
            sink = sink_row[g]
            m = jnp.maximum(_over_keys(jnp.max, s_t), sink)
            p_t = jnp.exp(s_t - m)
            l = _over_keys(jnp.sum, p_t) + jnp.exp(sink - m)
            o_t = jnp.dot(vt_cat[g * HALF:(g + 1) * HALF, :], p_t.astype(BF16), preferred_element_type=F32)
            _unstack_heads(o_t * (1.0 / l), g, o_ref, 1.0)
            lse = m + jnp.log(l)
            for hh in range(GROUP):
                lse_ref[GROUP * g + hh] = lse[:, hh * BLK:(hh + 1) * BLK]

    smem = pl.BlockSpec(memory_space=pltpu.SMEM)
    o_a, lse, *carried = pl.pallas_call(
        body, grid=(NBLK,),
        in_specs=[smem, smem, pl.BlockSpec((2, BLK, BLK), lambda i: (0, 0, 0)),
                  pl.BlockSpec((BLK, 512), lambda i: (i, QA))] + _seg_specs(True, KA) + _seg_specs(False, 0)
        + [HBM_SPEC] * len(rider.operands),
        out_specs=[pl.BlockSpec((BLK, 512), lambda i: (i, 0)),
                   pl.BlockSpec((N_HEADS, 1, BLK), lambda i: (0, 0, i))] + [HBM_SPEC] * len(rider.out_shapes),
        out_shape=[SDS((LP, 512), BF16), SDS((N_HEADS, 1, LP), F32)] + rider.out_shapes,
        scratch_shapes=[pltpu.VMEM((3, 2, N_KEY, QW), F32), pltpu.VMEM((2, 1, QW), F32)] + rider.scratch(),
        compiler_params=_cparams(("arbitrary",)), name="swa_fwd",
    )(rel_bias, sinks, bkt_t, proj, proj, proj, proj, vt_a, vt_a, vt_a, *rider.operands)
    return o_a, lse, carried


def _swa_bwd(proj, kt_a, o_a, dmix, lse, rel_bias, sinks, bkt_t):
    def body(tab_ref, sink_ref, bkt_ref, q_ref, km_ref, kp_ref, kc_ref, vm_ref, vp_ref, vc_ref,
             tm_ref, tp_ref, tc_ref, o_ref, do_ref, lse_ref,
             dq_ref, dk_ref, dv_ref, dbias_ref, dsink_ref, tbl, sink_row, acc, dsk):
        i = pl.program_id(0)

        @pl.when(i == 0)
        def _():
            _swa_tables(tab_ref, sink_ref, bkt_ref, tbl, sink_row)
            dk_ref[...] = jnp.zeros_like(dk_ref)
            dv_ref[...] = jnp.zeros_like(dv_ref)
            acc[...] = jnp.zeros_like(acc)
            dsk[...] = jnp.zeros_like(dsk)

        case = jnp.minimum(i, 2)
        first = jnp.full((BLK, QW), i, jnp.int32) == 1
        k_cat = jnp.concatenate([km_ref[...], kp_ref[...], kc_ref[...]], axis=0)
        v_cat = jnp.concatenate([vm_ref[...], vp_ref[...], vc_ref[...]], axis=0)
        kt_cat = jnp.concatenate([tm_ref[...], tp_ref[...], tc_ref[...]], axis=1)
        dk_cat = jnp.zeros((N_KEY, BLK), F32)
        dv_cat = jnp.zeros((N_KEY, BLK), F32)
        for g in range(2):
            d_parts = []
            for pair in range(2):
                cols = slice((2 * g + pair) * BLK, (2 * g + pair + 1) * BLK)
                prod_t = (do_ref[:, cols].astype(F32) * o_ref[:, cols].astype(F32)).T
                d_parts += [jnp.sum(prod_t[:HALF], axis=0, keepdims=True),
                            jnp.sum(prod_t[HALF:], axis=0, keepdims=True)]
            d_row = jnp.concatenate(d_parts, axis=1)
            lse_row = jnp.concatenate([lse_ref[GROUP * g + hh] for hh in range(GROUP)], axis=1)
            q_st = _stack_heads(q_ref, g, SCALE)
            do_st = _stack_heads(do_ref, g, 1.0)
            s_t = lax.dot_general(k_cat, q_st, NT_DIMS, preferred_element_type=F32) + tbl[case, g]
            p_t = jnp.exp(s_t - lse_row)
            dp_t = lax.dot_general(v_cat, do_st, NT_DIMS, preferred_element_type=F32)
            ds_t = p_t * (dp_t - d_row)
            dsk[g] += -jnp.exp(sink_row[g] - lse_row) * d_row
            acc[g, 0:BLK] += jnp.where(first, 0.0, ds_t[0:BLK])
            acc[g, BLK:2 * BLK] += jnp.where(first, ds_t[0:BLK], ds_t[BLK:2 * BLK])
            acc[g, 2 * BLK:N_KEY] += ds_t[2 * BLK:N_KEY]
            ds_b = ds_t.astype(BF16)
            dk_cat = dk_cat + jnp.dot(ds_b, q_st, preferred_element_type=F32)
            dv_cat = dv_cat + jnp.dot(p_t.astype(BF16), do_st, preferred_element_type=F32)
            dq_t = jnp.dot(kt_cat[g * HALF:(g + 1) * HALF, :], ds_b, preferred_element_type=F32)
            _unstack_heads(dq_t, g, dq_ref, SCALE)

        prev0 = pl.multiple_of(jnp.maximum(i - 1, 0) * BLK, BLK)
        cur0 = pl.multiple_of(i * BLK, BLK)
        for ref, cat in ((dk_ref, dk_cat), (dv_ref, dv_cat)):
            ref[0:BLK, :] += cat[0:BLK]
            ref[pl.ds(prev0, BLK), :] += cat[BLK:2 * BLK]
            ref[pl.ds(cur0, BLK), :] += cat[2 * BLK:N_KEY]

        @pl.when(i == NBLK - 1)
        def _():
            lane = lax.broadcasted_iota(jnp.int32, (1, BLK), 1)

            def per_bucket(b, carry):
                row = jnp.zeros((1, BLK), F32)
                for h in range(N_HEADS):
                    g, cols = h // GROUP, slice((h % GROUP) * BLK, (h % GROUP + 1) * BLK)
                    val = (jnp.sum(jnp.where(bkt_ref[0] == b, acc[g, 2 * BLK:N_KEY, cols], 0.0), keepdims=True)
                           + jnp.sum(jnp.where(bkt_ref[1] == b, acc[g, BLK:2 * BLK, cols], 0.0), keepdims=True))
                    row = jnp.where(lane == h, val, row)
                dbias_ref[pl.ds(b, 1), :] = row
                return carry

            lax.fori_loop(0, N_BUCKETS, per_bucket, 0)
            far = jnp.zeros((1, BLK), F32)
            dsr = jnp.zeros((1, BLK), F32)
            for h in range(N_HEADS):
                g, cols = h // GROUP, slice((h % GROUP) * BLK, (h % GROUP + 1) * BLK)
                far = jnp.where(lane == h, jnp.sum(acc[g, 0:BLK, cols], keepdims=True), far)
                dsr = jnp.where(lane == h, jnp.sum(dsk[g, :, cols], keepdims=True), dsr)
            dbias_ref[N_BUCKETS - 1:N_BUCKETS, :] += far
            dsink_ref[...] = dsr

    smem = pl.BlockSpec(memory_space=pltpu.SMEM)
    blk512 = lambda col: pl.BlockSpec((BLK, 512), lambda i: (i, col))
    full = lambda r, c: pl.BlockSpec((r, c), lambda i: (0, 0))
    return pl.pallas_call(
        body, grid=(NBLK,),
        in_specs=[smem, smem, pl.BlockSpec((2, BLK, BLK), lambda i: (0, 0, 0)), blk512(QA)]
        + _seg_specs(True, KA) + _seg_specs(True, VA) + _seg_specs(False, 0)
        + [blk512(0), blk512(0), pl.BlockSpec((N_HEADS, 1, BLK), lambda i: (0, 0, i))],
        out_specs=[blk512(0), full(LP, BLK), full(LP, BLK), full(N_BUCKETS, BLK), full(1, BLK)],
        out_shape=[SDS((LP, 512), BF16), SDS((LP, BLK), F32), SDS((LP, BLK), F32),
                   SDS((N_BUCKETS, BLK), F32), SDS((1, BLK), F32)],
        scratch_shapes=[pltpu.VMEM((3, 2, N_KEY, QW), F32), pltpu.VMEM((2, 1, QW), F32),
                        pltpu.VMEM((2, N_KEY, QW), F32), pltpu.VMEM((2, 1, QW), F32)],
        compiler_params=_cparams(("arbitrary",)), name="swa_bwd",
    )(rel_bias, sinks, bkt_t, proj, proj, proj, proj, proj, proj, proj, kt_a, kt_a, kt_a, o_a, dmix, lse)


def _local_step(x, tgt, meta, rel_bias, g_pre_mix, g_post_mix, g_pre_ffn, g_post_ffn, b_forget, sinks,
                w_in_b, out_rider, out_weight, ffn_rider, ffn_weights, early_grads):
    bkt_t = jnp.asarray(_bucket_tables_t())
    h0 = jnp.concatenate([jnp.zeros((PAD_ROWS, D_MODEL), F32), meta, x], axis=0)
    b_p = jnp.pad(b_forget, ((0, 0), (0, BLK - N_HEADS)))

    hn1, proj, f = _pre_mix(h0, g_pre_mix, w_in_b)
    kt_a, vt_a = _swa_prep(proj)
    o_a, lse_a, carried_out = _swa_fwd(proj, vt_a, rel_bias, sinks, bkt_t, out_rider)
    w_out_b = out_weight(carried_out)
    cum = _forget_cumsum(f, b_p)
    ck_t = cum[:, :N_HEADS].T.reshape(N_HEADS, 1, LP)
    q_aug, k_aug, v_t = _fox_prep(proj, cum)
    o_b, lse_row, carried = _fox_fwd(q_aug, k_aug, v_t, ffn_rider)
    lse_b = lse_row.reshape(N_HEADS, LP, 1)
    w_gu_b, w_dn_b = ffn_weights(carried)
    a, h1, hn2 = _attn_out(o_a, o_b, w_out_b, h0, g_post_mix, g_pre_ffn)
    g, u, act = _ffn_up(hn2, w_gu_b)
    dff, dy, loss_blk, dg_post_ffn = _ffn_down_loss(act, w_dn_b, h1, tgt, g_post_ffn)

    dw_dn = _mm_tn([act], dff, FF_T, "dw_down", BF16)
    dg, du = _ffn_down_bwd(dff, w_dn_b, g, u)
    dw_gu = _dw_gate_up(hn2, dg, du)
    dh1, da, dg_pre_ffn, dg_post_mix = _ffn_up_bwd(dg, du, w_gu_b, h1, a, dy, g_pre_ffn, g_post_mix)
    dw_out = _mm_tn([o_a, o_b], da, D_MODEL, "dw_out", BF16)
    dmix = _attn_out_bwd(da, w_out_b)
    dq_b, dk_b, dv_b, dck, dcq, landed = _fox_bwd(proj, o_b, dmix, lse_b, ck_t, early_grads(dw_gu, dw_dn, dw_out))
    dq_a, dk_a, dv_a, dbias, dsink = _swa_bwd(proj, kt_a, o_a, dmix, lse_a, rel_bias, sinks, bkt_t)
    dcum = dcq - jnp.pad(dck.reshape(N_HEADS, LP).T, ((0, 0), (0, BLK - N_HEADS)))
    df, db = _forget_cumsum_bwd(dcum, f, b_p)
    dproj, dh0, dg_pre_mix = _pre_mix_bwd(dq_a, dq_b, dk_b, dv_b, dk_a, dv_a, df, w_in_b, h0, dh1, g_pre_mix)
    dw_in = _mm_tn([hn1], dproj, D_MODEL, "dw_in", BF16)

    return dict(loss=loss_blk[0, 0], grad_x=dh0[ROW0:], meta=dh0[PAD_ROWS:ROW0],
                rel_bias=dbias[:, :N_HEADS], ln_pre_mix=dg_pre_mix, ln_post_mix=dg_post_mix,
                ln_pre_ffn=dg_pre_ffn, ln_post_ffn=dg_post_ffn, b_forget=db[:, :N_HEADS],
                sinks=dsink[:, :N_HEADS], w_in=dw_in, w_out=dw_out, w_gate_up=dw_gu, w_down=dw_dn,
                landed=landed)


N_SMALL = 24
LOSS_ROW = 6


def _place():
    x, y, c = lax.axis_index("x"), lax.axis_index("y"), lax.axis_index("c")
    return x, y, c, [(1 - x, y), (x, 1 - y), (1 - x, 1 - y)]


def _run_alone(rider, name):
    a, b = len(rider.operands), len(rider.out_shapes)

    def body(*refs):
        mine = (refs[:a], refs[a:a + b], refs[a + b:])
        rider.first(*mine)
        rider.middle(*mine)
        rider.last(*mine)

    return pl.pallas_call(body, in_specs=[HBM_SPEC] * a, out_specs=[HBM_SPEC] * b, out_shape=rider.out_shapes,
                          scratch_shapes=rider.scratch(), name=name)(*rider.operands)


def _gather_rider(shards, own_too, by_columns=()):
    n = len(shards)

    def slot(a, outs, chip, h):
        if a in by_columns:
            cols = shards[a].shape[2]
            return outs[a].at[h, :, pl.ds(pl.multiple_of(chip * cols, BLK), cols)]
        return outs[a].at[chip, h]

    def own_copies(ins, outs, sems):
        x, y, _, _ = _place()
        if not own_too:
            return []
        return [pltpu.make_async_copy(ins[a].at[h], slot(a, outs, 2 * x + y, h), sems[2].at[2 * a + h])
                for a in range(n) for h in range(2)]

    def copies(ins, outs, sems):
        send_sems, recv_sems = sems[:2]
        x, y, c, others = _place()
        chip = 2 * x + y
        sibling = (x, y, 1 - c)

        def rc(a, k, src, dst, to):
            return pltpu.make_async_remote_copy(src_ref=src, dst_ref=dst, send_sem=send_sems.at[6 * a + k],
                                                recv_sem=recv_sems.at[6 * a + k], device_id=to, device_id_type=MESH)

        pairs = [(a, j, ox, oy) for a in range(n) for j, (ox, oy) in enumerate(others)]
        there = lambda a, ox, oy, h: slot(a, outs, 2 * ox + oy, h)
        return dict(
            sent=lambda: [rc(a, j, ins[a].at[c], slot(a, outs, chip, c), (ox, oy, c)) for a, j, ox, oy in pairs],
            landed=lambda: [rc(a, j, there(a, ox, oy, c), there(a, ox, oy, c), sibling) for a, j, ox, oy in pairs],
            passed=lambda: [rc(a, 3 + j, there(a, ox, oy, c), there(a, ox, oy, c), sibling)
                            for a, j, ox, oy in pairs],
            arriving=lambda: [rc(a, 3 + j, there(a, ox, oy, 1 - c), there(a, ox, oy, 1 - c), sibling)
                              for a, j, ox, oy in pairs])

    def first(*mine):
        for cp in copies(*mine)["sent"]() + own_copies(*mine):
            cp.start()

    def middle(*mine):
        kinds = copies(*mine)
        for got, cp in zip(kinds["landed"](), kinds["passed"]()):
            got.wait_recv()
            cp.start()

    def last(*mine):
        kinds = copies(*mine)
        for cp in kinds["arriving"]():
            cp.wait_recv()
        for cp in kinds["sent"]() + kinds["passed"]():
            cp.wait_send()
        for cp in own_copies(*mine):
            cp.wait()

    shapes = [SDS((2, s.shape[1], 4 * s.shape[2]) if a in by_columns else (4,) + s.shape, s.dtype)
              for a, s in enumerate(shards)]
    return _Rider(shards, shapes, [6 * n, 6 * n] + [2 * n] * own_too, first, middle, last)


def _swap_rider(grads):
    n = len(grads)

    def copies(ins, outs, sems):
        x, y, c, _ = _place()
        return [pltpu.make_async_remote_copy(
            src_ref=ins[a].at[s, 1 - c], dst_ref=outs[a].at[s], send_sem=sems[0].at[4 * a + s],
            recv_sem=sems[1].at[4 * a + s], device_id=(x, y, 1 - c), device_id_type=MESH)
            for a in range(n) for s in range(4)]

    def first(*mine):
        for cp in copies(*mine):
            cp.start()

    def middle(*mine):
        pass

    def last(*mine):
        for cp in copies(*mine):
            cp.wait()

    return _Rider(grads, [SDS((4,) + g.shape[2:], g.dtype) for g in grads], [4 * n, 4 * n], first, middle, last)


def _pair_sum(g, got, c_arr, name):
    rh, cc = got.shape[1:]

    def body(c_ref, g_ref, p_ref, o_ref):
        o_ref[0] = (g_ref[0, 0].astype(F32) + p_ref[0].astype(F32)).astype(BF16)

    grid_spec = pltpu.PrefetchScalarGridSpec(
        num_scalar_prefetch=1, grid=(4,),
        in_specs=[pl.BlockSpec((1, 1, rh, cc), lambda s, c_ref: (s, c_ref[0], 0, 0)),
                  pl.BlockSpec((1, rh, cc), lambda s, c_ref: (s, 0, 0))],
        out_specs=pl.BlockSpec((1, rh, cc), lambda s, c_ref: (s, 0, 0)))
    return pl.pallas_call(body, grid_spec=grid_spec, out_shape=SDS((4, rh, cc), BF16),
                          compiler_params=_cparams(("parallel",)), name=name)(c_arr, g, got)


def _exchange_rider(parts, small=None):
    n = len(parts)

    def copies(ins, outs, sems):
        x, y, c, others = _place()
        out = [pltpu.make_async_remote_copy(
            src_ref=ins[a].at[2 * ox + oy], dst_ref=outs[a].at[j], send_sem=sems[0].at[3 * a + j],
            recv_sem=sems[1].at[3 * a + j], device_id=(ox, oy, c), device_id_type=MESH)
            for a in range(n) for j, (ox, oy) in enumerate(others)]
        own = []
        if small is not None:
            me = 4 * x + 2 * y + c
            peers = [(x, y, 1 - c)] + [(ox, oy, c) for ox, oy in others] + [(ox, oy, 1 - c) for ox, oy in others]
            out += [pltpu.make_async_remote_copy(
                src_ref=ins[n], dst_ref=outs[n].at[me], send_sem=sems[2].at[k], recv_sem=sems[3].at[k],
                device_id=peer, device_id_type=MESH) for k, peer in enumerate(peers)]
            own = [pltpu.make_async_copy(ins[n], outs[n].at[me], sems[4].at[0])]
        return out, own

    def first(*mine):
        out, own = copies(*mine)
        for cp in own + out:
            cp.start()

    def middle(*mine):
        pass

    def last(*mine):
        out, own = copies(*mine)
        for cp in out + own:
            cp.wait()

    shapes = [SDS((3,) + p.shape[1:], p.dtype) for p in parts]
    if small is None:
        return _Rider(parts, shapes, [3 * n, 3 * n], first, middle, last)
    return _Rider(parts + [small], shapes + [SDS((8,) + small.shape, small.dtype)], [3 * n, 3 * n, 7, 7, 1],
                  first, middle, last)


def _direct_rider(grads):
    n = len(grads)

    def copies(ins, outs, sems):
        x, y, c, others = _place()
        peers = [(x, y, 1 - c)] + [(ox, oy, c) for ox, oy in others] + [(ox, oy, 1 - c) for ox, oy in others]
        return [pltpu.make_async_remote_copy(
            src_ref=ins[a].at[2 * px + py, pc], dst_ref=outs[a].at[k], send_sem=sems[0].at[7 * a + k],
            recv_sem=sems[1].at[7 * a + k], device_id=(px, py, pc), device_id_type=MESH)
            for a in range(n) for k, (px, py, pc) in enumerate(peers)]

    def first(*mine):
        for cp in copies(*mine):
            cp.start()

    def middle(*mine):
        pass

    def last(*mine):
        for cp in copies(*mine):
            cp.wait()

    return _Rider(grads, [SDS((7,) + g.shape[2:], g.dtype) for g in grads], [7 * n, 7 * n], first, middle, last)


def _owner_sum(grads, landed, own_arr, name):
    rh, cc = landed.shape[1:]
    tr = rh // 2

    def body(own_ref, g_ref, p_ref, o_ref):
        total = g_ref[0, 0].astype(F32)
        for k in range(7):
            total = total + p_ref[k].astype(F32)
        o_ref[...] = total

    grid_spec = pltpu.PrefetchScalarGridSpec(
        num_scalar_prefetch=1, grid=(2,),
        in_specs=[pl.BlockSpec((1, 1, tr, cc), lambda i, own: (own[0], own[1], i, 0)),
                  pl.BlockSpec((7, tr, cc), lambda i, own: (0, i, 0))],
        out_specs=pl.BlockSpec((tr, cc), lambda i, own: (i, 0)))
    return pl.pallas_call(body, grid_spec=grid_spec, out_shape=SDS((rh, cc), F32),
                          compiler_params=_cparams(("parallel",)), name=name)(own_arr, grads, landed)


def _chip_sum(parts, landed, chip_arr, name):
    rh, cc = landed.shape[1:]
    tr = rh // 2

    def body(chip_ref, own_ref, p_ref, o_ref):
        o_ref[...] = ((own_ref[0].astype(F32) + p_ref[0].astype(F32)) + p_ref[1].astype(F32)) + p_ref[2].astype(F32)

    grid_spec = pltpu.PrefetchScalarGridSpec(
        num_scalar_prefetch=1, grid=(2,),
        in_specs=[pl.BlockSpec((1, tr, cc), lambda i, chip_ref: (chip_ref[0], i, 0)),
                  pl.BlockSpec((3, tr, cc), lambda i, chip_ref: (0, i, 0))],
        out_specs=pl.BlockSpec((tr, cc), lambda i, chip_ref: (i, 0)))
    return pl.pallas_call(body, grid_spec=grid_spec, out_shape=SDS((rh, cc), F32),
                          compiler_params=_cparams(("parallel",)), name=name)(chip_arr, parts, landed)


def _device_sum(p):
    def body(p_ref, o_ref):
        acc = p_ref[0]
        for k in range(1, 8):
            acc = acc + p_ref[k]
        o_ref[...] = acc

    return pl.pallas_call(body, out_shape=SDS(p.shape[1:], F32), name="small_sum")(p)


def _join_halves(halves):
    n = len(halves)

    def body(*refs):
        ins, outs = refs[:n], refs[n:2 * n]
        send_sems, recv_sems = refs[2 * n:]
        x, y, c, _ = _place()
        copies = [pltpu.make_async_remote_copy(
            src_ref=ins[a], dst_ref=outs[a], send_sem=send_sems.at[a], recv_sem=recv_sems.at[a],
            device_id=(x, y, 1 - c), device_id_type=MESH) for a in range(n)]
        for cp in copies:
            cp.start()
        for cp in copies:
            cp.wait()

    return pl.pallas_call(
        body, in_specs=[HBM_SPEC] * n, out_specs=[HBM_SPEC] * n,
        out_shape=[SDS(h.shape, h.dtype) for h in halves],
        scratch_shapes=[pltpu.SemaphoreType.DMA((n,)), pltpu.SemaphoreType.DMA((n,))],
        name="join_halves")(*halves)


def _adamw(w, g, m, v, name):
    rows, cols = w.shape
    tr = rows if rows <= 352 else (256 if rows % 256 == 0 else 352)

    def body(w_ref, g_ref, m_ref, v_ref, d_ref, nm_ref, nv_ref):
        gg = g_ref[...]
        nm = ADAM_B1 * m_ref[...] + (1.0 - ADAM_B1) * gg
        nv = ADAM_B2 * v_ref[...] + (1.0 - ADAM_B2) * (gg * gg)
        nm_ref[...] = nm
        nv_ref[...] = nv
        m_hat = nm / (1.0 - ADAM_B1 ** ADAM_STEP)
        v_hat = nv / (1.0 - ADAM_B2 ** ADAM_STEP)
        d_ref[...] = -ADAM_LR * (m_hat / (jnp.sqrt(v_hat) + ADAM_EPS) + ADAM_WD * w_ref[...])

    blk = pl.BlockSpec((tr, cols), lambda i: (i, 0))
    return pl.pallas_call(
        body, grid=(rows // tr,), in_specs=[blk] * 4, out_specs=[blk] * 3,
        out_shape=[SDS((rows, cols), F32)] * 3,
        compiler_params=_cparams(("parallel",)), name=name)(w, g, m, v)


def _adamw_halves(w, mine, theirs, m, v, c_arr, name):
    rows, cols = w.shape
    rh = rows // 2
    tr = rh if rh <= 352 else 256
    nh = rh // tr

    def body(c_ref, w_ref, mine_ref, theirs_ref, m_ref, v_ref, g_ref, d_ref, nm_ref, nv_ref):
        own = jnp.full((tr, cols), pl.program_id(0), jnp.int32) == c_ref[0]
        gg = jnp.where(own, mine_ref[...], theirs_ref[...])
        g_ref[...] = gg
        nm = ADAM_B1 * m_ref[...] + (1.0 - ADAM_B1) * gg
        nv = ADAM_B2 * v_ref[...] + (1.0 - ADAM_B2) * (gg * gg)
        nm_ref[...] = nm
        nv_ref[...] = nv
        m_hat = nm / (1.0 - ADAM_B1 ** ADAM_STEP)
        v_hat = nv / (1.0 - ADAM_B2 ** ADAM_STEP)
        d_ref[...] = -ADAM_LR * (m_hat / (jnp.sqrt(v_hat) + ADAM_EPS) + ADAM_WD * w_ref[...])

    whole = pl.BlockSpec((tr, cols), lambda hh, i, c_ref: (hh * nh + i, 0))
    part = pl.BlockSpec((tr, cols), lambda hh, i, c_ref: (i, 0))
    grid_spec = pltpu.PrefetchScalarGridSpec(
        num_scalar_prefetch=1, grid=(2, nh), in_specs=[whole, part, part, whole, whole], out_specs=[whole] * 4)
    return pl.pallas_call(body, grid_spec=grid_spec, out_shape=[SDS((rows, cols), F32)] * 4,
                          compiler_params=_cparams(("parallel", "parallel")), name=name)(c_arr, w, mine, theirs, m, v)


def _pack_small(pre_mix, post_mix, pre_ffn, post_ffn, rel_bias, b_forget, sinks):
    def at(row, v):
        return jnp.pad(v, ((row, 7 - row), (0, D_MODEL - v.shape[1])))
    return (at(0, pre_mix) + at(1, post_mix) + at(2, pre_ffn) + at(3, post_ffn)
            + at(4, rel_bias.reshape(1, N_BUCKETS * N_HEADS)) + at(5, jnp.concatenate([b_forget, sinks], axis=1)))


def _unpack_small(p):
    return dict(ln_pre_mix=p[0:1], ln_post_mix=p[1:2], ln_pre_ffn=p[2:3], ln_post_ffn=p[3:4],
                rel_bias=p[4, :N_BUCKETS * N_HEADS].reshape(N_BUCKETS, N_HEADS),
                b_forget=p[5:6, 0:N_HEADS], sinks=p[5:6, N_HEADS:2 * N_HEADS])


WEIGHTS = ("meta_tokens", "rel_bias", "ln_pre_mix", "ln_post_mix", "ln_pre_ffn", "ln_post_ffn",
           "w_in", "b_forget", "sinks", "w_out", "w_gate_up", "w_down")


def kernel(x, meta_tokens, rel_bias, ln_pre_mix, ln_post_mix, ln_pre_ffn, ln_post_ffn, w_in, b_forget, sinks, w_out, w_gate_up, w_down, loss_target, m_meta_tokens, m_rel_bias, m_ln_pre_mix, m_ln_post_mix, m_ln_pre_ffn, m_ln_post_ffn, m_w_in, m_b_forget, m_sinks, m_w_out, m_w_gate_up, m_w_down, v_meta_tokens, v_rel_bias, v_ln_pre_mix, v_ln_post_mix, v_ln_pre_ffn, v_ln_post_ffn, v_w_in, v_b_forget, v_sinks, v_w_out, v_w_gate_up, v_w_down):
    xi, yi, ci = lax.axis_index("x"), lax.axis_index("y"), lax.axis_index("c")
    chip = 2 * xi + yi
    c_arr = jnp.reshape(ci, (1,)).astype(jnp.int32)

    def halves(w, dtype):
        return w.astype(dtype).reshape(2, w.shape[0] // 2, w.shape[1])

    def with_own(gathered, shards):
        return [lax.dynamic_update_slice(got, own[None], (chip, 0, 0, 0)) for got, own in zip(gathered, shards)]

    shards = [halves(w_in[0], BF16), halves(meta_tokens, F32)]
    gw_in, g_meta = with_own(_run_alone(_gather_rider(shards, False), "gather_mixer_weights"), shards)
    out_shards = [halves(w_out[0], BF16)]
    ffn_shards = [halves(w_gate_up[0], BF16), halves(w_down[0], BF16)]

    def ffn_weights(carried):
        gw_gu, gw_dn = carried
        return gw_gu.reshape(D_MODEL, 2 * D_FF), gw_dn.reshape(D_FF, D_MODEL)

    early = {}

    def early_grads(dw_gu, dw_dn, dw_out):
        early["grads"] = [dw_out.reshape(4, 2, 128, D_MODEL), dw_gu.reshape(4, 2, 512, FF_T),
                          dw_dn.reshape(4, 2, 352, D_MODEL)]
        return _direct_rider(early["grads"])
    w_in_all = gw_in.reshape(4, D_MODEL, D_PROJ // 4).transpose(1, 0, 2).reshape(D_MODEL, D_PROJ)
    w_in_b = jnp.concatenate(
        [w_in_all[:, 0:512], w_in_all[:, 768:1280], w_in_all[:, 1280:1792], w_in_all[:, 1792:2304],
         w_in_all[:, 512:640], w_in_all[:, 640:768], w_in_all[:, 2304:2312],
         jnp.zeros((D_MODEL, D_PROJ_P - D_PROJ), BF16)], axis=1)
    meta_all = g_meta.reshape(4, N_META, D_MODEL // 4).transpose(1, 0, 2).reshape(N_META, D_MODEL)

    loc = _local_step(x[0], loss_target[0], meta_all, rel_bias, ln_pre_mix, ln_post_mix, ln_pre_ffn, ln_post_ffn,
                      b_forget, sinks, w_in_b, _gather_rider(out_shards, True),
                      lambda carried: carried[0].reshape(D_MODEL, D_MODEL),
                      _gather_rider(ffn_shards, True, by_columns=(0,)), ffn_weights, early_grads)

    n = loc["w_in"]
    dw_in = jnp.concatenate([n[:, 0:512], n[:, 2048:2176], n[:, 2176:2304], n[:, 512:1024], n[:, 1024:1536],
                             n[:, 1536:2048], n[:, 2304:2312]], axis=1)
    dw_in = dw_in.reshape(D_MODEL, 4, D_PROJ // 4).transpose(1, 0, 2).reshape(4, 2, 512, D_PROJ // 4)
    small = jnp.concatenate(
        [_pack_small(loc["ln_pre_mix"], loc["ln_post_mix"], loc["ln_pre_ffn"], loc["ln_post_ffn"],
                     loc["rel_bias"], loc["b_forget"], loc["sinks"])
         + jnp.pad(loc["loss"].reshape(1, 1), ((LOSS_ROW, 7 - LOSS_ROW), (0, D_MODEL - 1))), loc["meta"]], axis=0)

    (got_in,) = _run_alone(_swap_rider([dw_in]), "swap_halves_late")
    part_in = _pair_sum(dw_in, got_in, c_arr, "pair_sum_late")
    landed_in, small_all = _run_alone(_exchange_rider([part_in], small), "exchange_late")
    chip_arr = jnp.reshape(chip, (1,)).astype(jnp.int32)
    own_arr = jnp.stack([chip, ci]).astype(jnp.int32)
    mine = [_chip_sum(part_in, landed_in, chip_arr, "chip_sum_in")] + [
        _owner_sum(g, l, own_arr, "owner_sum_%d" % a) for a, (g, l) in enumerate(zip(early["grads"], loc["landed"]))]
    small_sum = _device_sum(small_all)
    theirs = _join_halves(mine)
    g_meta_tokens = lax.dynamic_slice(small_sum[8:N_SMALL], (0, chip * (D_MODEL // 4)), (N_META, D_MODEL // 4))
    g_small = small_sum[0:8]

    grad = _unpack_small(g_small)
    grad.update(meta_tokens=g_meta_tokens)
    delta, new_m, new_v = {}, {}, {}
    big = dict(w_in=(w_in, m_w_in, v_w_in), w_out=(w_out, m_w_out, v_w_out),
               w_gate_up=(w_gate_up, m_w_gate_up, v_w_gate_up), w_down=(w_down, m_w_down, v_w_down))
    for (name, (w, m, v)), g_mine, g_theirs in zip(big.items(), mine, theirs):
        g, d, nm, nv = _adamw_halves(w[0], g_mine, g_theirs, m[0], v[0], c_arr, "adamw_" + name)
        grad[name], delta[name], new_m[name], new_v[name] = g[None], d[None], nm[None], nv[None]
    delta["meta_tokens"], new_m["meta_tokens"], new_v["meta_tokens"] = _adamw(
        meta_tokens, g_meta_tokens, m_meta_tokens, v_meta_tokens, "adamw_meta")
    d, nm, nv = _adamw(
        _pack_small(ln_pre_mix, ln_post_mix, ln_pre_ffn, ln_post_ffn, rel_bias, b_forget, sinks), g_small,
        _pack_small(m_ln_pre_mix, m_ln_post_mix, m_ln_pre_ffn, m_ln_post_ffn, m_rel_bias, m_b_forget, m_sinks),
        _pack_small(v_ln_pre_mix, v_ln_post_mix, v_ln_pre_ffn, v_ln_post_ffn, v_rel_bias, v_b_forget, v_sinks),
        "adamw_small")
    delta.update(_unpack_small(d))
    new_m.update(_unpack_small(nm))
    new_v.update(_unpack_small(nv))

    loss = small_sum[LOSS_ROW, 0]
    return (loss,loc["grad_x"][None], *[grad[k] for k in WEIGHTS], *[delta[k] for k in WEIGHTS],
            *[new_m[k] for k in WEIGHTS], *[new_v[k] for k in WEIGHTS])
```

```python
import math

import numpy as np
import jax
import jax.numpy as jnp
from jax import lax
from jax.experimental import pallas as pl
from jax.experimental.pallas import tpu as pltpu

F32 = jnp.float32
BF16 = jnp.bfloat16
MESH = pl.DeviceIdType.MESH
SDS = jax.ShapeDtypeStruct

D_MODEL = 1024
SEQ = 4096
N_META = 16
N_HEADS = 8
HALF = 64
D_FF = 2816
N_BUCKETS = 32
EPS = 1e-6
NEG = -1e30
SCALE = 0.125
PAD_ROWS = 112
ROW0 = PAD_ROWS + N_META
LP = ROW0 + SEQ
BLK = 128
NBLK = LP // BLK
TM = 384
NT = LP // TM
TM_PURE = LP // 2
TM_MID = LP // 4
TM_EPI = LP // 6
TN = 256
D_PROJ = 2312
D_PROJ_P = 2432
D_QKV = 2304
FF_T = 1408
VMEM_LIMIT = 56 * 1024 * 1024

ADAM_LR = 0.001
ADAM_B1 = 0.9
ADAM_B2 = 0.999
ADAM_EPS = 1e-08
ADAM_WD = 0.01
ADAM_STEP = 10

QA = 0
KA, VA = 4, 5
QB, KB, VB = 3, 5, 7
W2 = 256

NT_DIMS = (((1,), (1,)), ((), ()))
TN_DIMS = (((0,), (0,)), ((), ()))


def _cparams(sem):
    return pltpu.CompilerParams(dimension_semantics=sem, vmem_limit_bytes=VMEM_LIMIT)


def _t5_bucket_np(d):
    n = np.maximum(d, 0).astype(np.int32)
    nf = np.maximum(n, 1).astype(np.float32)
    large = 16 + (np.log(nf / np.float32(16)) / np.float32(math.log(8.0)) * np.float32(16)).astype(np.int32)
    large = np.minimum(large, N_BUCKETS - 1)
    return np.where(n < 16, n, large).astype(np.int32)


def _bucket_tables():
    qi = np.arange(BLK)[:, None]
    ki = np.arange(BLK)[None, :]
    return np.stack([_t5_bucket_np(qi - ki), _t5_bucket_np(qi - ki + BLK)])


def _rms(x):
    return lax.rsqrt(jnp.mean(x * x, axis=-1, keepdims=True) + EPS)


def _rms_bwd(n, r, gdy):
    return r * (gdy - n * jnp.mean(n * gdy, axis=-1, keepdims=True))


def _pre_mix(h0, gain, w_in_b):
    half = D_QKV // 2

    def body(h_ref, g_ref, w_ref, hn_ref, proj_ref, f_ref):
        x = h_ref[...]
        hn = (x * _rms(x) * g_ref[...]).astype(BF16)
        hn_ref[...] = hn
        proj_ref[:, :half] = jnp.dot(hn, w_ref[:, :half], preferred_element_type=F32).astype(BF16)
        p = jnp.dot(hn, w_ref[:, half:], preferred_element_type=F32)
        proj_ref[:, half:] = p[:, :half].astype(BF16)
        f_ref[...] = p[:, half:]

    return pl.pallas_call(
        body, grid=(LP // TM_MID,),
        in_specs=[pl.BlockSpec((TM_MID, D_MODEL), lambda i: (i, 0)),
                  pl.BlockSpec((1, D_MODEL), lambda i: (0, 0)),
                  pl.BlockSpec((D_MODEL, D_PROJ_P), lambda i: (0, 0))],
        out_specs=[pl.BlockSpec((TM_MID, D_MODEL), lambda i: (i, 0)),
                   pl.BlockSpec((TM_MID, D_QKV), lambda i: (i, 0)),
                   pl.BlockSpec((TM_MID, BLK), lambda i: (i, 0))],
        out_shape=[SDS((LP, D_MODEL), BF16), SDS((LP, D_QKV), BF16), SDS((LP, BLK), F32)],
        compiler_params=_cparams(("parallel",)), name="pre_mix")(h0, gain, w_in_b)


def _attn_out(o_a, o_b, w_out_b, h0, g_post, g_pre_ffn):
    def body(oa_ref, ob_ref, w_ref, h0_ref, gp_ref, gf_ref, a_ref, h1_ref, hn2_ref):
        a = (jnp.dot(oa_ref[...], w_ref[0:512, :], preferred_element_type=F32)
             + jnp.dot(ob_ref[...], w_ref[512:1024, :], preferred_element_type=F32))
        a_ref[...] = a
        h1 = h0_ref[...] + a * _rms(a) * gp_ref[...]
        h1_ref[...] = h1
        hn2_ref[...] = (h1 * _rms(h1) * gf_ref[...]).astype(BF16)

    row = lambda w: pl.BlockSpec((TM_EPI, w), lambda i: (i, 0))
    vec = pl.BlockSpec((1, D_MODEL), lambda i: (0, 0))
    return pl.pallas_call(
        body, grid=(LP // TM_EPI,),
        in_specs=[row(512), row(512), pl.BlockSpec((D_MODEL, D_MODEL), lambda i: (0, 0)), row(D_MODEL), vec, vec],
        out_specs=[row(D_MODEL), row(D_MODEL), row(D_MODEL)],
        out_shape=[SDS((LP, D_MODEL), F32), SDS((LP, D_MODEL), F32), SDS((LP, D_MODEL), BF16)],
        compiler_params=_cparams(("parallel",)), name="attn_out")(o_a, o_b, w_out_b, h0, g_post, g_pre_ffn)


def _ffn_up(hn2, w_gu_b):
    def body(x_ref, wg_ref, wu_ref, g_ref, u_ref, act_ref):
        x = x_ref[...]
        g = jnp.dot(x, wg_ref[...], preferred_element_type=F32)
        u = jnp.dot(x, wu_ref[...], preferred_element_type=F32)
        g_ref[...] = g.astype(BF16)
        u_ref[...] = u.astype(BF16)
        act_ref[...] = (g * (1.0 / (1.0 + jnp.exp(-g))) * u).astype(BF16)

    out = pl.BlockSpec((TM_PURE, TN), lambda i, j: (i, j))
    return pl.pallas_call(
        body, grid=(LP // TM_PURE, D_FF // TN),
        in_specs=[pl.BlockSpec((TM_PURE, D_MODEL), lambda i, j: (i, 0)),
                  pl.BlockSpec((D_MODEL, TN), lambda i, j: (0, j)),
                  pl.BlockSpec((D_MODEL, TN), lambda i, j: (0, j + D_FF // TN))],
        out_specs=[out, out, out],
        out_shape=[SDS((LP, D_FF), BF16)] * 3,
        compiler_params=_cparams(("parallel", "parallel")), name="ffn_up")(hn2, w_gu_b, w_gu_b)


def _ffn_down_loss(act, w_dn_b, h1, tgt, g_post_ffn):
    def body(act_ref, w_ref, h1_ref, t0_ref, t1_ref, t2_ref, g_ref, dff_ref, dy_ref, loss_ref, dg_ref):
        i = pl.program_id(0)
        target = jnp.concatenate([t0_ref[...], t1_ref[...], t2_ref[...]], axis=0)

        @pl.when(i == 0)
        def _():
            loss_ref[...] = jnp.zeros_like(loss_ref)
            dg_ref[...] = jnp.zeros_like(dg_ref)

        ff = jnp.dot(act_ref[...], w_ref[...], preferred_element_type=F32)
        r = _rms(ff)
        n = ff * r
        g = g_ref[...]
        y = h1_ref[...] + n * g
        rows = i * TM + lax.broadcasted_iota(jnp.int32, (TM, D_MODEL), 0)
        diff = jnp.where(rows >= ROW0, y - target, 0.0)
        loss_ref[...] += 0.5 * jnp.sum(diff * diff) / D_MODEL
        dy = diff / D_MODEL
        dy_ref[...] = dy
        dg_ref[...] += jnp.sum(dy * n, axis=0, keepdims=True)
        dff_ref[...] = _rms_bwd(n, r, g * dy).astype(BF16)

    row = pl.BlockSpec((TM, D_MODEL), lambda i: (i, 0))
    tblk = lambda j: pl.BlockSpec((BLK, D_MODEL), lambda i: (jnp.maximum(3 * i - 1 + j, 0), 0))
    return pl.pallas_call(
        body, grid=(NT,),
        in_specs=[pl.BlockSpec((TM, D_FF), lambda i: (i, 0)), pl.BlockSpec((D_FF, D_MODEL), lambda i: (0, 0)),
                  row, tblk(0), tblk(1), tblk(2), pl.BlockSpec((1, D_MODEL), lambda i: (0, 0))],
        out_specs=[row, row, pl.BlockSpec((8, BLK), lambda i: (0, 0)), pl.BlockSpec((1, D_MODEL), lambda i: (0, 0))],
        out_shape=[SDS((LP, D_MODEL), BF16), SDS((LP, D_MODEL), F32), SDS((8, BLK), F32), SDS((1, D_MODEL), F32)],
        compiler_params=_cparams(("arbitrary",)), name="ffn_down_loss")(act, w_dn_b, h1, tgt, tgt, tgt, g_post_ffn)


def _ffn_down_bwd(dff, w_dn_b, g, u):
    def body(d_ref, w_ref, g_ref, u_ref, dg_ref, du_ref):
        dact = lax.dot_general(d_ref[...], w_ref[...], NT_DIMS, preferred_element_type=F32)
        gg = g_ref[...].astype(F32)
        sig = 1.0 / (1.0 + jnp.exp(-gg))
        dg_ref[...] = (dact * u_ref[...].astype(F32) * sig * (1.0 + gg * (1.0 - sig))).astype(BF16)
        du_ref[...] = (dact * gg * sig).astype(BF16)

    blk = pl.BlockSpec((TM_PURE, TN), lambda i, j: (i, j))
    return pl.pallas_call(
        body, grid=(LP // TM_PURE, D_FF // TN),
        in_specs=[pl.BlockSpec((TM_PURE, D_MODEL), lambda i, j: (i, 0)),
                  pl.BlockSpec((TN, D_MODEL), lambda i, j: (j, 0)), blk, blk],
        out_specs=[blk, blk],
        out_shape=[SDS((LP, D_FF), BF16)] * 2,
        compiler_params=_cparams(("parallel", "parallel")), name="ffn_down_bwd")(dff, w_dn_b, g, u)


def _ffn_up_bwd(dg, du, w_gu_b, h1, a, dy, g_pre_ffn, g_post_mix):
    def body(dg_ref, du_ref, w_ref, h1_ref, a_ref, dy_ref, gf_ref, gp_ref,
             dh1_ref, da_ref, dgf_ref, dgp_ref, acc):
        i = pl.program_id(0)
        s = pl.program_id(1)

        @pl.when((i == 0) & (s == 0))
        def _():
            dgf_ref[...] = jnp.zeros_like(dgf_ref)
            dgp_ref[...] = jnp.zeros_like(dgp_ref)

        @pl.when(s == 0)
        def _():
            acc[...] = jnp.zeros_like(acc)

        @pl.when(s < 2)
        def _():
            acc[...] += lax.dot_general(dg_ref[...], w_ref[...], NT_DIMS, preferred_element_type=F32)

        @pl.when(s >= 2)
        def _():
            acc[...] += lax.dot_general(du_ref[...], w_ref[...], NT_DIMS, preferred_element_type=F32)

        @pl.when(s == 3)
        def _():
            dhn2 = acc[...]
            h1 = h1_ref[...]
            r2 = _rms(h1)
            n2 = h1 * r2
            dgf_ref[...] += jnp.sum(dhn2 * n2, axis=0, keepdims=True)
            dh1 = dy_ref[...] + _rms_bwd(n2, r2, gf_ref[...] * dhn2)
            dh1_ref[...] = dh1
            av = a_ref[...]
            ra = _rms(av)
            na = av * ra
            dgp_ref[...] += jnp.sum(dh1 * na, axis=0, keepdims=True)
            da_ref[...] = _rms_bwd(na, ra, gp_ref[...] * dh1).astype(BF16)

    row = pl.BlockSpec((TM_EPI, D_MODEL), lambda i, s: (i, 0))
    vec = pl.BlockSpec((1, D_MODEL), lambda i, s: (0, 0))
    return pl.pallas_call(
        body, grid=(LP // TM_EPI, 4),
        in_specs=[pl.BlockSpec((TM_EPI, FF_T), lambda i, s: (i, jnp.minimum(s, 1))),
                  pl.BlockSpec((TM_EPI, FF_T), lambda i, s: (i, jnp.maximum(s - 2, 0))),
                  pl.BlockSpec((D_MODEL, FF_T), lambda i, s: (0, s)),
                  row, row, row, vec, vec],
        out_specs=[row, row, vec, vec],
        out_shape=[SDS((LP, D_MODEL), F32), SDS((LP, D_MODEL), BF16), SDS((1, D_MODEL), F32), SDS((1, D_MODEL), F32)],
        scratch_shapes=[pltpu.VMEM((TM_EPI, D_MODEL), F32)],
        compiler_params=_cparams(("arbitrary", "arbitrary")), name="ffn_up_bwd",
    )(dg, du, w_gu_b, h1, a, dy, g_pre_ffn, g_post_mix)


def _attn_out_bwd(da, w_out_b):
    def body(d_ref, w_ref, o_ref):
        o_ref[...] = lax.dot_general(d_ref[...], w_ref[...], NT_DIMS, preferred_element_type=F32).astype(BF16)

    row = pl.BlockSpec((TM_PURE, D_MODEL), lambda i: (i, 0))
    return pl.pallas_call(
        body, grid=(LP // TM_PURE,),
        in_specs=[row, pl.BlockSpec((D_MODEL, D_MODEL), lambda i: (0, 0))],
        out_specs=row, out_shape=SDS((LP, D_MODEL), BF16),
        compiler_params=_cparams(("parallel",)), name="attn_out_bwd")(da, w_out_b)


def _pre_mix_bwd(dq_a, dq_b, dk_b, dv_b, dk_a, dv_a, df, w_in_b, h0, dh1, g_pre_mix):
    def body(qa_ref, qb_ref, kb_ref, vb_ref, ka_ref, va_ref, f_ref, w_ref, h0_ref, dh1_ref, g_ref,
             dproj_ref, dh0_ref, dg_ref):
        i = pl.program_id(0)

        @pl.when(i == 0)
        def _():
            dg_ref[...] = jnp.zeros_like(dg_ref)

        dproj = jnp.concatenate(
            [qa_ref[...], ka_ref[...].astype(BF16), va_ref[...].astype(BF16), (qb_ref[...] * SCALE).astype(BF16),
             kb_ref[...], vb_ref[...], f_ref[...].astype(BF16)], axis=1)
        dproj_ref[...] = dproj
        dhn = lax.dot_general(dproj, w_ref[...], NT_DIMS, preferred_element_type=F32)
        x = h0_ref[...]
        r = _rms(x)
        n = x * r
        dg_ref[...] += jnp.sum(dhn * n, axis=0, keepdims=True)
        dh0_ref[...] = dh1_ref[...] + _rms_bwd(n, r, g_ref[...] * dhn)

    row = lambda w: pl.BlockSpec((TM_EPI, w), lambda i: (i, 0))
    vec = pl.BlockSpec((1, D_MODEL), lambda i: (0, 0))
    return pl.pallas_call(
        body, grid=(LP // TM_EPI,),
        in_specs=[row(512), row(512), row(512), row(512), row(BLK), row(BLK), row(BLK),
                  pl.BlockSpec((D_MODEL, D_PROJ_P), lambda i: (0, 0)), row(D_MODEL), row(D_MODEL), vec],
        out_specs=[row(D_PROJ_P), row(D_MODEL), vec],
        out_shape=[SDS((LP, D_PROJ_P), BF16), SDS((LP, D_MODEL), F32), SDS((1, D_MODEL), F32)],
        compiler_params=_cparams(("arbitrary",)), name="pre_mix_bwd",
    )(dq_a, dq_b, dk_b, dv_b, dk_a, dv_a, df, w_in_b, h0, dh1, g_pre_mix)


def _mm_tn(parts, b, tm, name, out_dtype=F32):
    widths = [p.shape[1] for p in parts]
    m_total = sum(widths)
    n = b.shape[1]
    whole = len(parts) > 1
    n_k = LP // TM_MID
    assert (tm == m_total) if whole else (m_total % tm == 0)

    def body(*refs):
        a_refs, b_ref, o_ref, acc = refs[:-3], refs[-3], refs[-2], refs[-1]
        k = pl.program_id(1)

        @pl.when(k == 0)
        def _():
            acc[...] = jnp.zeros_like(acc)
        a = a_refs[0][...] if not whole else jnp.concatenate([r[...] for r in a_refs], axis=1)
        acc[...] += lax.dot_general(a, b_ref[...], TN_DIMS, preferred_element_type=F32)

        @pl.when(k == n_k - 1)
        def _():
            o_ref[...] = acc[...].astype(out_dtype)

    a_specs = ([pl.BlockSpec((TM_MID, w), lambda mi, k: (k, 0)) for w in widths] if whole
               else [pl.BlockSpec((TM_MID, tm), lambda mi, k: (k, mi))])
    return pl.pallas_call(
        body, grid=(m_total // tm, n_k),
        in_specs=a_specs + [pl.BlockSpec((TM_MID, n), lambda mi, k: (k, 0))],
        out_specs=pl.BlockSpec((tm, n), lambda mi, k: (mi, 0)),
        out_shape=SDS((m_total, n), out_dtype),
        scratch_shapes=[pltpu.VMEM((tm, n), F32)],
        compiler_params=_cparams(("parallel", "arbitrary")), name=name)(*parts, b)


def _dw_gate_up(hn2, dg, du):
    n_k = LP // TM_MID

    def body(a_ref, dg_ref, du_ref, o_ref, acc):
        s = pl.program_id(0)
        k = pl.program_id(1)

        @pl.when(k == 0)
        def _():
            acc[...] = jnp.zeros_like(acc)

        @pl.when(s < 2)
        def _():
            acc[...] += lax.dot_general(a_ref[...], dg_ref[...], TN_DIMS, preferred_element_type=F32)

        @pl.when(s >= 2)
        def _():
            acc[...] += lax.dot_general(a_ref[...], du_ref[...], TN_DIMS, preferred_element_type=F32)

        @pl.when(k == n_k - 1)
        def _():
            o_ref[0] = acc[...].astype(BF16)

    return pl.pallas_call(
        body, grid=(4, n_k),
        in_specs=[pl.BlockSpec((TM_MID, D_MODEL), lambda s, k: (k, 0)),
                  pl.BlockSpec((TM_MID, FF_T), lambda s, k: (k, jnp.minimum(s, 1))),
                  pl.BlockSpec((TM_MID, FF_T), lambda s, k: (k, jnp.maximum(s - 2, 0)))],
        out_specs=pl.BlockSpec((1, D_MODEL, FF_T), lambda s, k: (s, 0, 0)),
        out_shape=SDS((4, D_MODEL, FF_T), BF16),
        scratch_shapes=[pltpu.VMEM((D_MODEL, FF_T), F32)],
        compiler_params=_cparams(("parallel", "arbitrary")), name="dw_gate_up")(hn2, dg, du)


def _split3(x):
    hi = x.astype(BF16)
    r1 = x - hi.astype(F32)
    mid = r1.astype(BF16)
    lo = (r1 - mid.astype(F32)).astype(BF16)
    return hi, mid, lo


def _tri_matmul(tri, x):
    hi, mid, lo = _split3(x)
    dot = lambda t: jnp.dot(tri, t, preferred_element_type=F32)
    return dot(hi) + dot(mid) + dot(lo)


def _forget_cumsum(f, b_forget_p):
    def body(f_ref, b_ref, cum_ref, carry):
        i = pl.program_id(0)

        @pl.when(i == 0)
        def _():
            carry[...] = jnp.zeros_like(carry)

        z = f_ref[...] + b_ref[...]
        ls = jnp.minimum(z, 0.0) - jnp.log(1.0 + jnp.exp(-jnp.abs(z)))
        rows = i * TM + lax.broadcasted_iota(jnp.int32, (TM, BLK), 0)
        ls = jnp.where(rows >= PAD_ROWS, ls, 0.0)
        r = lax.broadcasted_iota(jnp.int32, (TM, TM), 0)
        c = lax.broadcasted_iota(jnp.int32, (TM, TM), 1)
        tri = (c <= r).astype(BF16)
        cum = _tri_matmul(tri, ls) + carry[...]
        cum_ref[...] = cum
        carry[...] = cum[TM - 1:TM, :]

    return pl.pallas_call(
        body, grid=(NT,),
        in_specs=[pl.BlockSpec((TM, BLK), lambda i: (i, 0)), pl.BlockSpec((1, BLK), lambda i: (0, 0))],
        out_specs=pl.BlockSpec((TM, BLK), lambda i: (i, 0)),
        out_shape=SDS((LP, BLK), F32),
        scratch_shapes=[pltpu.VMEM((1, BLK), F32)],
        compiler_params=_cparams(("arbitrary",)), name="forget_cumsum")(f, b_forget_p)


def _forget_cumsum_bwd(dcum, f, b_forget_p):
    def body(d_ref, f_ref, b_ref, df_ref, db_ref, carry):
        i = pl.program_id(0)

        @pl.when(i == 0)
        def _():
            carry[...] = jnp.zeros_like(carry)
            db_ref[...] = jnp.zeros_like(db_ref)

        blk = NT - 1 - i
        r = lax.broadcasted_iota(jnp.int32, (TM, TM), 0)
        c = lax.broadcasted_iota(jnp.int32, (TM, TM), 1)
        tri = (c >= r).astype(BF16)
        d = d_ref[...]
        dls = _tri_matmul(tri, d) + carry[...]
        carry[...] = dls[0:1, :]
        z = f_ref[...] + b_ref[...]
        rows = blk * TM + lax.broadcasted_iota(jnp.int32, (TM, BLK), 0)
        df = jnp.where(rows >= PAD_ROWS, dls / (1.0 + jnp.exp(z)), 0.0)
        df_ref[...] = df
        db_ref[...] += jnp.sum(df, axis=0, keepdims=True)

    rev = pl.BlockSpec((TM, BLK), lambda i: (NT - 1 - i, 0))
    vec = pl.BlockSpec((1, BLK), lambda i: (0, 0))
    return pl.pallas_call(
        body, grid=(NT,),
        in_specs=[rev, rev, vec],
        out_specs=[rev, vec],
        out_shape=[SDS((LP, BLK), F32), SDS((1, BLK), F32)],
        scratch_shapes=[pltpu.VMEM((1, BLK), F32)],
        compiler_params=_cparams(("arbitrary",)), name="forget_cumsum_bwd")(dcum, f, b_forget_p)


def _lane_half(rows):
    return lax.broadcasted_iota(jnp.int32, (rows, BLK), 1) // HALF


def _fox_valid(qi, kj):
    qrow = qi * TM + lax.broadcasted_iota(jnp.int32, (TM, TM), 0)
    krow = kj * TM + lax.broadcasted_iota(jnp.int32, (TM, TM), 1)
    return (krow <= qrow) & ((krow >= PAD_ROWS) | (qrow < PAD_ROWS))


class _Rider:
    def __init__(self, operands, out_shapes, sem_counts, first, middle, last):
        self.operands, self.out_shapes, self.sem_counts = list(operands), list(out_shapes), list(sem_counts)
        self.first, self.middle, self.last = first, middle, last

    def scratch(self):
        return [pltpu.SemaphoreType.DMA((k,)) for k in self.sem_counts]

    def split(self, refs, n_in, n_out, n_scratch):
        a, b = len(self.operands), len(self.out_shapes)
        ins, mine_in = refs[:n_in], refs[n_in:n_in + a]
        outs, mine_out = refs[n_in + a:n_in + a + n_out], refs[n_in + a + n_out:n_in + a + n_out + b]
        rest = refs[n_in + a + n_out + b:]
        return ins, outs, rest[:n_scratch], (mine_in, mine_out, rest[n_scratch:])

    def at_steps(self, mine, is_first, is_middle, is_last):
        for cond, fn in ((is_first, self.first), (is_middle, self.middle), (is_last, self.last)):
            pl.when(cond)(lambda fn=fn: fn(*mine))


HBM_SPEC = pl.BlockSpec(memory_space=pltpu.HBM)


N_AUG = 4
QCH = 128
KSUB = 384
AHEAD = 5
AHEAD_BWD = 1


def _fox_prep(proj, cum):
    def body(q0_ref, q1_ref, k0_ref, k1_ref, v0_ref, v1_ref, c_ref, qa_ref, ka_ref, vt_ref):
        half = _lane_half(TM)
        lane = lax.broadcasted_iota(jnp.int32, (TM, BLK), 1)
        for pp in range(4):
            cols = slice(pp * BLK, (pp + 1) * BLK)
            q_ref, k_ref, v_ref = ((q0_ref, k0_ref, v0_ref), (q1_ref, k1_ref, v1_ref))[pp // 2]
            part = slice((pp % 2) * BLK, (pp % 2 + 1) * BLK)
            qs = q_ref[:, part].astype(F32) * SCALE
            kp = k_ref[:, part].astype(F32)
            vp = v_ref[:, part]
            vt_ref[cols, :] = vp.astype(F32).T.astype(BF16)
            for e in range(2):
                h = 2 * pp + e
                a = (1 - e) * HALF
                blk = slice(h * BLK, (h + 1) * BLK)
                hi, mid, lo = _split3(-c_ref[:, h:h + 1])
                q_aug = jnp.where(half == e, qs, jnp.where((lane >= a) & (lane < a + 3), 1.0, 0.0))
                k_aug = jnp.where(half == e, kp, jnp.where(
                    lane == a, hi.astype(F32), jnp.where(lane == a + 1, mid.astype(F32), jnp.where(
                        lane == a + 2, lo.astype(F32), jnp.where(lane == a + 3, 1.0, 0.0)))))
                qa_ref[:, blk] = q_aug.astype(BF16)
                ka_ref[:, blk] = k_aug.astype(BF16)

    row = lambda blk: pl.BlockSpec((TM, W2), lambda i: (i, blk))
    wide = pl.BlockSpec((TM, 1024), lambda i: (i, 0))
    return pl.pallas_call(
        body, grid=(NT,),
        in_specs=[row(QB), row(QB + 1), row(KB), row(KB + 1), row(VB), row(VB + 1),
                  pl.BlockSpec((TM, BLK), lambda i: (i, 0))],
        out_specs=[wide, wide, pl.BlockSpec((512, TM), lambda i: (0, i))],
        out_shape=[SDS((LP, 1024), BF16)] * 2 + [SDS((512, LP), BF16)],
        compiler_params=_cparams(("parallel",)), name="fox_prep")(proj, proj, proj, proj, proj, proj, cum)


def _over_keys(reduce, x):
    slabs = x.reshape(x.shape[0] // HALF, HALF, x.shape[1])
    return reduce(reduce(slabs, axis=0), axis=0, keepdims=True)


def _fox_valid_t(qi, kj, c, r):
    krow = kj * TM + r * KSUB + lax.broadcasted_iota(jnp.int32, (KSUB, QCH), 0)
    qrow = qi * TM + c * QCH + lax.broadcasted_iota(jnp.int32, (KSUB, QCH), 1)
    return (krow <= qrow) & ((krow >= PAD_ROWS) | (qrow < PAD_ROWS))


def _fox_fwd(q_aug, k_aug, v_t, rider):
    pairs = [(qi, kj) for qi in range(NT) for kj in range(qi + 1)]
    n_pairs = len(pairs)

    def body(qi_ref, kj_ref, *refs):
        (q_ref, k_ref, vt_ref), (o_ref, lse_ref), (m_s, l_s, acc_s), mine = rider.split(refs, 3, 2, 3)
        n = pl.program_id(0)
        qi = qi_ref[n]
        kj = kj_ref[n]
        rider.at_steps(mine, n == 0, n == n_pairs // 2, n == n_pairs - 1)

        @pl.when(kj == 0)
        def _():
            m_s[...] = jnp.full_like(m_s, NEG)
            l_s[...] = jnp.zeros_like(l_s)
            acc_s[...] = jnp.zeros_like(acc_s)

        def tile(masked):
            steps = [(h, c, r) for h in range(N_HEADS) for c in range(TM // QCH) for r in range(TM // KSUB)]

            def scores(h, c, r):
                blk = slice(h * BLK, (h + 1) * BLK)
                return lax.dot_general(k_ref[r * KSUB:(r + 1) * KSUB, blk], q_ref[c * QCH:(c + 1) * QCH, blk],
                                       NT_DIMS, preferred_element_type=F32)

            ahead = [scores(*st) for st in steps[:AHEAD]]
            for n, (h, c, r) in enumerate(steps):
                s_t = ahead.pop(0)
                if n + AHEAD < len(steps):
                    ahead.append(scores(*steps[n + AHEAD]))
                cs = slice(c * QCH, (c + 1) * QCH)
                if masked:
                    s_t = jnp.where(_fox_valid_t(qi, kj, c, r), s_t, NEG)
                m_prev = m_s[h, :, cs]
                m_new = jnp.maximum(m_prev, _over_keys(jnp.max, s_t))
                p_t = jnp.exp(s_t - m_new)
                alpha = jnp.exp(m_prev - m_new)
                l_s[h, :, cs] = alpha * l_s[h, :, cs] + _over_keys(jnp.sum, p_t)
                m_s[h, :, cs] = m_new
                vt = vt_ref[h * HALF:(h + 1) * HALF, r * KSUB:(r + 1) * KSUB]
                acc_s[h, :, cs] = acc_s[h, :, cs] * alpha + jnp.dot(vt, p_t.astype(BF16),
                                                                    preferred_element_type=F32)

        @pl.when((kj < qi) & (kj > 0))
        def _():
            tile(False)

        @pl.when((kj == qi) | (kj == 0))
        def _():
            tile(True)

        @pl.when(kj == qi)
        def _():
            for pp in range(4):
                both = jnp.concatenate([acc_s[2 * pp] * (1.0 / l_s[2 * pp]),
                                        acc_s[2 * pp + 1] * (1.0 / l_s[2 * pp + 1])], axis=0)
                o_ref[:, pp * BLK:(pp + 1) * BLK] = both.T.astype(BF16)
            for h in range(N_HEADS):
                lse_ref[h] = m_s[h] + jnp.log(l_s[h])

    grid_spec = pltpu.PrefetchScalarGridSpec(
        num_scalar_prefetch=2, grid=(n_pairs,),
        in_specs=[pl.BlockSpec((TM, 1024), lambda n, qi, kj: (qi[n], 0)),
                  pl.BlockSpec((TM, 1024), lambda n, qi, kj: (kj[n], 0)),
                  pl.BlockSpec((512, TM), lambda n, qi, kj: (0, kj[n]))] + [HBM_SPEC] * len(rider.operands),
        out_specs=[pl.BlockSpec((TM, 512), lambda n, qi, kj: (qi[n], 0)),
                   pl.BlockSpec((N_HEADS, 1, TM), lambda n, qi, kj: (0, 0, qi[n]))]
        + [HBM_SPEC] * len(rider.out_shapes),
        scratch_shapes=[pltpu.VMEM((N_HEADS, 1, TM), F32), pltpu.VMEM((N_HEADS, 1, TM), F32),
                        pltpu.VMEM((N_HEADS, HALF, TM), F32)] + rider.scratch())
    o_b, lse, *carried = pl.pallas_call(
        body, grid_spec=grid_spec,
        out_shape=[SDS((LP, 512), BF16), SDS((N_HEADS, 1, LP), F32)] + rider.out_shapes,
        compiler_params=_cparams(("arbitrary",)), name="fox_fwd",
    )(jnp.asarray([p[0] for p in pairs], jnp.int32), jnp.asarray([p[1] for p in pairs], jnp.int32),
      q_aug, k_aug, v_t, *rider.operands)
    return o_b, lse, carried


def _fox_bwd(proj, o_b, dmix, lse, ck_t, rider):
    pairs = [(kj, qi) for kj in range(NT) for qi in range(kj, NT)]
    n_pairs = len(pairs)

    def body(kj_ref, qi_ref, *refs):
        ((q0_ref, q1_ref, k0_ref, k1_ref, v0_ref, v1_ref, o_ref, do_ref, lse_ref, ck_ref),
         (dq_ref, dk_ref, dv_ref, dck_ref, dcq_ref), (dk_s, dv_s, dck_s), mine) = rider.split(refs, 10, 5, 3)
        n = pl.program_id(0)
        kj = kj_ref[n]
        qi = qi_ref[n]
        rider.at_steps(mine, n == 0, n == n_pairs // 2, n == n_pairs - 1)

        @pl.when(n == 0)
        def _():
            dq_ref[...] = jnp.zeros_like(dq_ref)
            dcq_ref[...] = jnp.zeros_like(dcq_ref)

        @pl.when(qi == kj)
        def _():
            dk_s[...] = jnp.zeros_like(dk_s)
            dv_s[...] = jnp.zeros_like(dv_s)
            dck_s[...] = jnp.zeros_like(dck_s)

        def tile(masked):
            valid = _fox_valid(qi, kj) if masked else None
            half = _lane_half(TM)
            q0 = pl.multiple_of(qi * TM, TM)
            lane = lax.broadcasted_iota(jnp.int32, (TM, BLK), 1)
            row_sums = jnp.zeros((TM, BLK), F32)
            pair_ops = {}

            def operands(pp):
                if pp not in pair_ops:
                    cols = slice(pp * BLK, (pp + 1) * BLK)
                    q_ref, k_ref, v_ref = ((q0_ref, k0_ref, v0_ref), (q1_ref, k1_ref, v1_ref))[pp // 2]
                    part = slice((pp % 2) * BLK, (pp % 2 + 1) * BLK)
                    pair_ops[pp] = ((q_ref[:, part].astype(F32) * SCALE).astype(BF16), k_ref[:, part],
                                    v_ref[:, part], do_ref[:, cols])
                return pair_ops[pp]

            def scores(pp, e):
                qs, kp, vp, dop = operands(pp)
                ke = jnp.where(half == e, kp, jnp.zeros_like(kp))
                ve = jnp.where(half == e, vp, jnp.zeros_like(vp))
                return (lax.dot_general(qs, ke, NT_DIMS, preferred_element_type=F32),
                        lax.dot_general(dop, ve, NT_DIMS, preferred_element_type=F32), ke)

            steps = [(pp, e) for pp in range(4) for e in range(2)]
            ahead = [scores(*st) for st in steps[:AHEAD_BWD]]
            for n, (pp, e) in enumerate(steps):
                raw, dp, ke = ahead.pop(0)
                if n + AHEAD_BWD < len(steps):
                    ahead.append(scores(*steps[n + AHEAD_BWD]))
                h = 2 * pp + e
                cols = slice(pp * BLK, (pp + 1) * BLK)
                qs, kp, vp, dop = operands(pp)
                if e == 0:
                    prod = dop.astype(F32) * o_ref[:, cols].astype(F32)
                    d0 = jnp.sum(jnp.where(half == 0, prod, 0.0), axis=1, keepdims=True)
                    d1 = jnp.sum(prod, axis=1, keepdims=True) - d0
                    dq = jnp.zeros((TM, BLK), F32)
                    dks, dvs = [], []
                t = raw - ck_ref[h] - lse_ref[h]
                if masked:
                    t = jnp.where(valid, t, NEG)
                p = jnp.exp(t)
                ds = p * (dp - (d0 if e == 0 else d1))
                dck_s[h] += jnp.sum(ds, axis=0, keepdims=True)
                row_sums = jnp.where(lane == h, jnp.sum(ds, axis=1, keepdims=True), row_sums)
                ds_b = ds.astype(BF16)
                dq = dq + jnp.dot(ds_b, ke, preferred_element_type=F32)
                dks.append(lax.dot_general(ds_b, qs, TN_DIMS, preferred_element_type=F32))
                dvs.append(lax.dot_general(p.astype(BF16), dop, TN_DIMS, preferred_element_type=F32))
                if e == 1:
                    dq_ref[pl.ds(q0, TM), cols] += dq
                    dk_s[pp] += jnp.where(half == 0, dks[0], dks[1])
                    dv_s[pp] += jnp.where(half == 0, dvs[0], dvs[1])
            dcq_ref[pl.ds(q0, TM), :] += row_sums

        @pl.when((qi > kj) & (kj > 0))
        def _():
            tile(False)

        @pl.when((qi == kj) | (kj == 0))
        def _():
            tile(True)

        @pl.when(qi == NT - 1)
        def _():
            for pp in range(4):
                cols = slice(pp * BLK, (pp + 1) * BLK)
                dk_ref[:, cols] = dk_s[pp].astype(BF16)
                dv_ref[:, cols] = dv_s[pp].astype(BF16)
            dck_ref[...] = dck_s[...]

    qrow = lambda blk, w=512: pl.BlockSpec((TM, w), lambda n, kj, qi: (qi[n], blk))
    krow = lambda blk: pl.BlockSpec((TM, W2), lambda n, kj, qi: (kj[n], blk))
    grid_spec = pltpu.PrefetchScalarGridSpec(
        num_scalar_prefetch=2, grid=(n_pairs,),
        in_specs=[qrow(QB, W2), qrow(QB + 1, W2), krow(KB), krow(KB + 1), krow(VB), krow(VB + 1), qrow(0), qrow(1),
                  pl.BlockSpec((N_HEADS, TM, 1), lambda n, kj, qi: (0, qi[n], 0)),
                  pl.BlockSpec((N_HEADS, 1, TM), lambda n, kj, qi: (0, 0, kj[n]))] + [HBM_SPEC] * len(rider.operands),
        out_specs=[pl.BlockSpec((LP, 512), lambda n, kj, qi: (0, 0)),
                   pl.BlockSpec((TM, 512), lambda n, kj, qi: (kj[n], 0)),
                   pl.BlockSpec((TM, 512), lambda n, kj, qi: (kj[n], 0)),
                   pl.BlockSpec((N_HEADS, 1, TM), lambda n, kj, qi: (0, 0, kj[n])),
                   pl.BlockSpec((LP, BLK), lambda n, kj, qi: (0, 0))] + [HBM_SPEC] * len(rider.out_shapes),
        scratch_shapes=[pltpu.VMEM((4, TM, BLK), F32), pltpu.VMEM((4, TM, BLK), F32),
                        pltpu.VMEM((N_HEADS, 1, TM), F32)] + rider.scratch())
    dq, dk, dv, dck, dcq, *carried = pl.pallas_call(
        body, grid_spec=grid_spec,
        out_shape=[SDS((LP, 512), F32), SDS((LP, 512), BF16), SDS((LP, 512), BF16), SDS((N_HEADS, 1, LP), F32),
                   SDS((LP, BLK), F32)] + rider.out_shapes,
        compiler_params=_cparams(("arbitrary",)), name="fox_bwd",
    )(jnp.asarray([p[0] for p in pairs], jnp.int32), jnp.asarray([p[1] for p in pairs], jnp.int32),
      proj, proj, proj, proj, proj, proj, o_b, dmix, lse, ck_t, *rider.operands)
    return dq, dk, dv, dck, dcq, carried


N_SEG = 3
N_KEY = N_SEG * BLK
GROUP = 4
QW = GROUP * BLK


def _bucket_tables_t():
    return np.ascontiguousarray(_bucket_tables().transpose(0, 2, 1))


def _stack_heads(ref, g, scale):
    half = _lane_half(BLK)
    out = []
    for pair in range(2):
        x = ref[:, (2 * g + pair) * BLK:(2 * g + pair + 1) * BLK].astype(F32) * scale
        swapped = pltpu.roll(x, HALF, 1)
        for e in range(2):
            out.append(jnp.where(half == g, x if e == g else swapped, 0.0).astype(BF16))
    return jnp.concatenate(out, axis=0)


def _unstack_heads(x_t, g, ref, scale):
    for pair in range(2):
        both = jnp.concatenate([x_t[:, (2 * pair) * BLK:(2 * pair + 1) * BLK],
                                x_t[:, (2 * pair + 1) * BLK:(2 * pair + 2) * BLK]], axis=0)
        ref[:, (2 * g + pair) * BLK:(2 * g + pair + 1) * BLK] = (both.T * scale).astype(ref.dtype)


def _swa_tables(tab_ref, sink_ref, bkt_ref, tbl, sink_row):
    kk = lax.broadcasted_iota(jnp.int32, (BLK, BLK), 0)
    qq = lax.broadcasted_iota(jnp.int32, (BLK, BLK), 1)
    neg = jnp.full((BLK, BLK), NEG, F32)
    lane = lax.broadcasted_iota(jnp.int32, (1, QW), 1) // BLK
    for g in range(2):
        row = jnp.zeros((1, QW), F32)
        for hh in range(GROUP):
            h = GROUP * g + hh
            cols = slice(hh * BLK, (hh + 1) * BLK)
            row = jnp.where(lane == hh, sink_ref[0, h], row)

            def step(b, carry, h=h):
                t = tab_ref[b, h]
                return jnp.where(bkt_ref[0] == b, t, carry[0]), jnp.where(bkt_ref[1] == b, t, carry[1])
            zero = jnp.zeros((BLK, BLK), F32)
            cur, prev = lax.fori_loop(0, N_BUCKETS, step, (zero, zero))
            far = jnp.full((BLK, BLK), tab_ref[N_BUCKETS - 1, h], F32)
            causal = jnp.where(kk <= qq, cur, neg)
            segments = [
                (neg, neg, jnp.where(kk >= PAD_ROWS, causal, neg)),
                (jnp.where(kk >= PAD_ROWS, prev, neg), neg, causal),
                (jnp.where(kk >= PAD_ROWS, far, neg), jnp.where(kk > qq, prev, neg), causal)]
            for case in range(3):
                for seg in range(N_SEG):
                    tbl[case, g, seg * BLK:(seg + 1) * BLK, cols] = segments[case][seg]
        sink_row[g] = row


def _swa_prep(proj):
    rows = LP // 3

    def body(k_ref, v_ref, kt_ref, vt_ref):
        kt_ref[...] = k_ref[...].astype(F32).T.astype(BF16)
        vt_ref[...] = v_ref[...].astype(F32).T.astype(BF16)

    col = pl.BlockSpec((BLK, rows), lambda i: (0, i))
    return pl.pallas_call(
        body, grid=(3,),
        in_specs=[pl.BlockSpec((rows, BLK), lambda i: (i, KA)), pl.BlockSpec((rows, BLK), lambda i: (i, VA))],
        out_specs=[col, col], out_shape=[SDS((BLK, LP), BF16)] * 2,
        compiler_params=_cparams(("parallel",)), name="swa_prep")(proj, proj)


def _seg_specs(rows_major, col):
    idx = [lambda i: 0, lambda i: jnp.maximum(i - 1, 0), lambda i: i]
    if rows_major:
        return [pl.BlockSpec((BLK, BLK), lambda i, f=f: (f(i), col)) for f in idx]
    return [pl.BlockSpec((BLK, BLK), lambda i, f=f: (0, f(i))) for f in idx]


def _swa_fwd(proj, vt_a, rel_bias, sinks, bkt_t, rider):
    def body(*refs):
        ((tab_ref, sink_ref, bkt_ref, q_ref, km_ref, kp_ref, kc_ref, vm_ref, vp_ref, vc_ref), (o_ref, lse_ref),
         (tbl, sink_row), mine) = rider.split(refs, 10, 2, 2)
        i = pl.program_id(0)
        rider.at_steps(mine, i == 0, i == NBLK // 2, i == NBLK - 1)

        @pl.when(i == 0)
        def _():
            _swa_tables(tab_ref, sink_ref, bkt_ref, tbl, sink_row)

        case = jnp.minimum(i, 2)
        k_cat = jnp.concatenate([km_ref[...], kp_ref[...], kc_ref[...]], axis=0)
        vt_cat = jnp.concatenate([vm_ref[...], vp_ref[...], vc_ref[...]], axis=1)
        raw = [lax.dot_general(k_cat, _stack_heads(q_ref, g, SCALE), NT_DIMS, preferred_element_type=F32)
               for g in range(2)]
        for g in range(2):
            s_t = raw[g] + tbl[case, g]
            sink = sink_row[g]
            m = jnp.maximum(_over_keys(jnp.max, s_t), sink)
            p_t = jnp.exp(s_t - m)
            l = _over_keys(jnp.sum, p_t) + jnp.exp(sink - m)
            o_t = jnp.dot(vt_cat[g * HALF:(g + 1) * HALF, :], p_t.astype(BF16), preferred_element_type=F32)
            _unstack_heads(o_t * (1.0 / l), g, o_ref, 1.0)
            lse = m + jnp.log(l)
            for hh in range(GROUP):
                lse_ref[GROUP * g + hh] = lse[:, hh * BLK:(hh + 1) * BLK]

    smem = pl.BlockSpec(memory_space=pltpu.SMEM)
    o_a, lse, *carried = pl.pallas_call(
        body, grid=(NBLK,),
        in_specs=[smem, smem, pl.BlockSpec((2, BLK, BLK), lambda i: (0, 0, 0)),
                  pl.BlockSpec((BLK, 512), lambda i: (i, QA))] + _seg_specs(True, KA) + _seg_specs(False, 0)
        + [HBM_SPEC] * len(rider.operands),
        out_specs=[pl.BlockSpec((BLK, 512), lambda i: (i, 0)),
                   pl.BlockSpec((N_HEADS, 1, BLK), lambda i: (0, 0, i))] + [HBM_SPEC] * len(rider.out_shapes),
        out_shape=[SDS((LP, 512), BF16), SDS((N_HEADS, 1, LP), F32)] + rider.out_shapes,
        scratch_shapes=[pltpu.VMEM((3, 2, N_KEY, QW), F32), pltpu.VMEM((2, 1, QW), F32)] + rider.scratch(),
        compiler_params=_cparams(("arbitrary",)), name="swa_fwd",
    )(rel_bias, sinks, bkt_t, proj, proj, proj, proj, vt_a, vt_a, vt_a, *rider.operands)
    return o_a, lse, carried


def _swa_bwd(proj, kt_a, o_a, dmix, lse, rel_bias, sinks, bkt_t):
    def body(tab_ref, sink_ref, bkt_ref, q_ref, km_ref, kp_ref, kc_ref, vm_ref, vp_ref, vc_ref,
             tm_ref, tp_ref, tc_ref, o_ref, do_ref, lse_ref,
             dq_ref, dk_ref, dv_ref, dbias_ref, dsink_ref, tbl, sink_row, acc, dsk):
        i = pl.program_id(0)

        @pl.when(i == 0)
        def _():
            _swa_tables(tab_ref, sink_ref, bkt_ref, tbl, sink_row)
            dk_ref[...] = jnp.zeros_like(dk_ref)
            dv_ref[...] = jnp.zeros_like(dv_ref)
            acc[...] = jnp.zeros_like(acc)
            dsk[...] = jnp.zeros_like(dsk)

        case = jnp.minimum(i, 2)
        first = jnp.full((BLK, QW), i, jnp.int32) == 1
        k_cat = jnp.concatenate([km_ref[...], kp_ref[...], kc_ref[...]], axis=0)
        v_cat = jnp.concatenate([vm_ref[...], vp_ref[...], vc_ref[...]], axis=0)
        kt_cat = jnp.concatenate([tm_ref[...], tp_ref[...], tc_ref[...]], axis=1)
        dk_cat = jnp.zeros((N_KEY, BLK), F32)
        dv_cat = jnp.zeros((N_KEY, BLK), F32)
        for g in range(2):
            d_parts = []
            for pair in range(2):
                cols = slice((2 * g + pair) * BLK, (2 * g + pair + 1) * BLK)
                prod_t = (do_ref[:, cols].astype(F32) * o_ref[:, cols].astype(F32)).T
                d_parts += [jnp.sum(prod_t[:HALF], axis=0, keepdims=True),
                            jnp.sum(prod_t[HALF:], axis=0, keepdims=True)]
            d_row = jnp.concatenate(d_parts, axis=1)
            lse_row = jnp.concatenate([lse_ref[GROUP * g + hh] for hh in range(GROUP)], axis=1)
            q_st = _stack_heads(q_ref, g, SCALE)
            do_st = _stack_heads(do_ref, g, 1.0)
            s_t = lax.dot_general(k_cat, q_st, NT_DIMS, preferred_element_type=F32) + tbl[case, g]
            p_t = jnp.exp(s_t - lse_row)
            dp_t = lax.dot_general(v_cat, do_st, NT_DIMS, preferred_element_type=F32)
            ds_t = p_t * (dp_t - d_row)
            dsk[g] += -jnp.exp(sink_row[g] - lse_row) * d_row
            acc[g, 0:BLK] += jnp.where(first, 0.0, ds_t[0:BLK])
            acc[g, BLK:2 * BLK] += jnp.where(first, ds_t[0:BLK], ds_t[BLK:2 * BLK])
            acc[g, 2 * BLK:N_KEY] += ds_t[2 * BLK:N_KEY]
            ds_b = ds_t.astype(BF16)
            dk_cat = dk_cat + jnp.dot(ds_b, q_st, preferred_element_type=F32)
            dv_cat = dv_cat + jnp.dot(p_t.astype(BF16), do_st, preferred_element_type=F32)
            dq_t = jnp.dot(kt_cat[g * HALF:(g + 1) * HALF, :], ds_b, preferred_element_type=F32)
            _unstack_heads(dq_t, g, dq_ref, SCALE)

        prev0 = pl.multiple_of(jnp.maximum(i - 1, 0) * BLK, BLK)
        cur0 = pl.multiple_of(i * BLK, BLK)
        for ref, cat in ((dk_ref, dk_cat), (dv_ref, dv_cat)):
            ref[0:BLK, :] += cat[0:BLK]
            ref[pl.ds(prev0, BLK), :] += cat[BLK:2 * BLK]
            ref[pl.ds(cur0, BLK), :] += cat[2 * BLK:N_KEY]

        @pl.when(i == NBLK - 1)
        def _():
            lane = lax.broadcasted_iota(jnp.int32, (1, BLK), 1)

            def per_bucket(b, carry):
                row = jnp.zeros((1, BLK), F32)
                for h in range(N_HEADS):
                    g, cols = h // GROUP, slice((h % GROUP) * BLK, (h % GROUP + 1) * BLK)
                    val = (jnp.sum(jnp.where(bkt_ref[0] == b, acc[g, 2 * BLK:N_KEY, cols], 0.0), keepdims=True)
                           + jnp.sum(jnp.where(bkt_ref[1] == b, acc[g, BLK:2 * BLK, cols], 0.0), keepdims=True))
                    row = jnp.where(lane == h, val, row)
                dbias_ref[pl.ds(b, 1), :] = row
                return carry

            lax.fori_loop(0, N_BUCKETS, per_bucket, 0)
            far = jnp.zeros((1, BLK), F32)
            dsr = jnp.zeros((1, BLK), F32)
            for h in range(N_HEADS):
                g, cols = h // GROUP, slice((h % GROUP) * BLK, (h % GROUP + 1) * BLK)
                far = jnp.where(lane == h, jnp.sum(acc[g, 0:BLK, cols], keepdims=True), far)
                dsr = jnp.where(lane == h, jnp.sum(dsk[g, :, cols], keepdims=True), dsr)
            dbias_ref[N_BUCKETS - 1:N_BUCKETS, :] += far
            dsink_ref[...] = dsr

    smem = pl.BlockSpec(memory_space=pltpu.SMEM)
    blk512 = lambda col: pl.BlockSpec((BLK, 512), lambda i: (i, col))
    full = lambda r, c: pl.BlockSpec((r, c), lambda i: (0, 0))
    return pl.pallas_call(
        body, grid=(NBLK,),
        in_specs=[smem, smem, pl.BlockSpec((2, BLK, BLK), lambda i: (0, 0, 0)), blk512(QA)]
        + _seg_specs(True, KA) + _seg_specs(True, VA) + _seg_specs(False, 0)
        + [blk512(0), blk512(0), pl.BlockSpec((N_HEADS, 1, BLK), lambda i: (0, 0, i))],
        out_specs=[blk512(0), full(LP, BLK), full(LP, BLK), full(N_BUCKETS, BLK), full(1, BLK)],
        out_shape=[SDS((LP, 512), BF16), SDS((LP, BLK), F32), SDS((LP, BLK), F32),
                   SDS((N_BUCKETS, BLK), F32), SDS((1, BLK), F32)],
        scratch_shapes=[pltpu.VMEM((3, 2, N_KEY, QW), F32), pltpu.VMEM((2, 1, QW), F32),
                        pltpu.VMEM((2, N_KEY, QW), F32), pltpu.VMEM((2, 1, QW), F32)],
        compiler_params=_cparams(("arbitrary",)), name="swa_bwd",
    )(rel_bias, sinks, bkt_t, proj, proj, proj, proj, proj, proj, proj, kt_a, kt_a, kt_a, o_a, dmix, lse)


def _local_step(x, tgt, meta, rel_bias, g_pre_mix, g_post_mix, g_pre_ffn, g_post_ffn, b_forget, sinks,
                w_in_b, out_rider, out_weight, ffn_rider, ffn_weights, early_grads):
    bkt_t = jnp.asarray(_bucket_tables_t())
    h0 = jnp.concatenate([jnp.zeros((PAD_ROWS, D_MODEL), F32), meta, x], axis=0)
    b_p = jnp.pad(b_forget, ((0, 0), (0, BLK - N_HEADS)))

    hn1, proj, f = _pre_mix(h0, g_pre_mix, w_in_b)
    kt_a, vt_a = _swa_prep(proj)
    o_a, lse_a, carried_out = _swa_fwd(proj, vt_a, rel_bias, sinks, bkt_t, out_rider)
    w_out_b = out_weight(carried_out)
    cum = _forget_cumsum(f, b_p)
    ck_t = cum[:, :N_HEADS].T.reshape(N_HEADS, 1, LP)
    q_aug, k_aug, v_t = _fox_prep(proj, cum)
    o_b, lse_row, carried = _fox_fwd(q_aug, k_aug, v_t, ffn_rider)
    lse_b = lse_row.reshape(N_HEADS, LP, 1)
    w_gu_b, w_dn_b = ffn_weights(carried)
    a, h1, hn2 = _attn_out(o_a, o_b, w_out_b, h0, g_post_mix, g_pre_ffn)
    g, u, act = _ffn_up(hn2, w_gu_b)
    dff, dy, loss_blk, dg_post_ffn = _ffn_down_loss(act, w_dn_b, h1, tgt, g_post_ffn)

    dw_dn = _mm_tn([act], dff, FF_T, "dw_down", BF16)
    dg, du = _ffn_down_bwd(dff, w_dn_b, g, u)
    dw_gu = _dw_gate_up(hn2, dg, du)
    dh1, da, dg_pre_ffn, dg_post_mix = _ffn_up_bwd(dg, du, w_gu_b, h1, a, dy, g_pre_ffn, g_post_mix)
    dw_out = _mm_tn([o_a, o_b], da, D_MODEL, "dw_out", BF16)
    dmix = _attn_out_bwd(da, w_out_b)
    dq_b, dk_b, dv_b, dck, dcq, landed = _fox_bwd(proj, o_b, dmix, lse_b, ck_t, early_grads(dw_gu, dw_dn, dw_out))
    dq_a, dk_a, dv_a, dbias, dsink = _swa_bwd(proj, kt_a, o_a, dmix, lse_a, rel_bias, sinks, bkt_t)
    dcum = dcq - jnp.pad(dck.reshape(N_HEADS, LP).T, ((0, 0), (0, BLK - N_HEADS)))
    df, db = _forget_cumsum_bwd(dcum, f, b_p)
    dproj, dh0, dg_pre_mix = _pre_mix_bwd(dq_a, dq_b, dk_b, dv_b, dk_a, dv_a, df, w_in_b, h0, dh1, g_pre_mix)
    dw_in = _mm_tn([hn1], dproj, D_MODEL, "dw_in", BF16)

    return dict(loss=loss_blk[0, 0], grad_x=dh0[ROW0:], meta=dh0[PAD_ROWS:ROW0],
                rel_bias=dbias[:, :N_HEADS], ln_pre_mix=dg_pre_mix, ln_post_mix=dg_post_mix,
                ln_pre_ffn=dg_pre_ffn, ln_post_ffn=dg_post_ffn, b_forget=db[:, :N_HEADS],
                sinks=dsink[:, :N_HEADS], w_in=dw_in, w_out=dw_out, w_gate_up=dw_gu, w_down=dw_dn,
                landed=landed)


N_SMALL = 24
LOSS_ROW = 6


def _place():
    x, y, c = lax.axis_index("x"), lax.axis_index("y"), lax.axis_index("c")
    return x, y, c, [(1 - x, y), (x, 1 - y), (1 - x, 1 - y)]


def _run_alone(rider, name):
    a, b = len(rider.operands), len(rider.out_shapes)

    def body(*refs):
        mine = (refs[:a], refs[a:a + b], refs[a + b:])
        rider.first(*mine)
        rider.middle(*mine)
        rider.last(*mine)

    return pl.pallas_call(body, in_specs=[HBM_SPEC] * a, out_specs=[HBM_SPEC] * b, out_shape=rider.out_shapes,
                          scratch_shapes=rider.scratch(), name=name)(*rider.operands)


def _gather_rider(shards, own_too, by_columns=()):
    n = len(shards)

    def slot(a, outs, chip, h):
        if a in by_columns:
            cols = shards[a].shape[2]
            return outs[a].at[h, :, pl.ds(pl.multiple_of(chip * cols, BLK), cols)]
        return outs[a].at[chip, h]

    def own_copies(ins, outs, sems):
        x, y, _, _ = _place()
        if not own_too:
            return []
        return [pltpu.make_async_copy(ins[a].at[h], slot(a, outs, 2 * x + y, h), sems[2].at[2 * a + h])
                for a in range(n) for h in range(2)]

    def copies(ins, outs, sems):
        send_sems, recv_sems = sems[:2]
        x, y, c, others = _place()
        chip = 2 * x + y
        sibling = (x, y, 1 - c)

        def rc(a, k, src, dst, to):
            return pltpu.make_async_remote_copy(src_ref=src, dst_ref=dst, send_sem=send_sems.at[6 * a + k],
                                                recv_sem=recv_sems.at[6 * a + k], device_id=to, device_id_type=MESH)

        pairs = [(a, j, ox, oy) for a in range(n) for j, (ox, oy) in enumerate(others)]
        there = lambda a, ox, oy, h: slot(a, outs, 2 * ox + oy, h)
        return dict(
            sent=lambda: [rc(a, j, ins[a].at[c], slot(a, outs, chip, c), (ox, oy, c)) for a, j, ox, oy in pairs],
            landed=lambda: [rc(a, j, there(a, ox, oy, c), there(a, ox, oy, c), sibling) for a, j, ox, oy in pairs],
            passed=lambda: [rc(a, 3 + j, there(a, ox, oy, c), there(a, ox, oy, c), sibling)
                            for a, j, ox, oy in pairs],
            arriving=lambda: [rc(a, 3 + j, there(a, ox, oy, 1 - c), there(a, ox, oy, 1 - c), sibling)
                              for a, j, ox, oy in pairs])

    def first(*mine):
        for cp in copies(*mine)["sent"]() + own_copies(*mine):
            cp.start()

    def middle(*mine):
        kinds = copies(*mine)
        for got, cp in zip(kinds["landed"](), kinds["passed"]()):
            got.wait_recv()
            cp.start()

    def last(*mine):
        kinds = copies(*mine)
        for cp in kinds["arriving"]():
            cp.wait_recv()
        for cp in kinds["sent"]() + kinds["passed"]():
            cp.wait_send()
        for cp in own_copies(*mine):
            cp.wait()

    shapes = [SDS((2, s.shape[1], 4 * s.shape[2]) if a in by_columns else (4,) + s.shape, s.dtype)
              for a, s in enumerate(shards)]
    return _Rider(shards, shapes, [6 * n, 6 * n] + [2 * n] * own_too, first, middle, last)


def _swap_rider(grads):
    n = len(grads)

    def copies(ins, outs, sems):
        x, y, c, _ = _place()
        return [pltpu.make_async_remote_copy(
            src_ref=ins[a].at[s, 1 - c], dst_ref=outs[a].at[s], send_sem=sems[0].at[4 * a + s],
            recv_sem=sems[1].at[4 * a + s], device_id=(x, y, 1 - c), device_id_type=MESH)
            for a in range(n) for s in range(4)]

    def first(*mine):
        for cp in copies(*mine):
            cp.start()

    def middle(*mine):
        pass

    def last(*mine):
        for cp in copies(*mine):
            cp.wait()

    return _Rider(grads, [SDS((4,) + g.shape[2:], g.dtype) for g in grads], [4 * n, 4 * n], first, middle, last)


def _pair_sum(g, got, c_arr, name):
    rh, cc = got.shape[1:]

    def body(c_ref, g_ref, p_ref, o_ref):
        o_ref[0] = (g_ref[0, 0].astype(F32) + p_ref[0].astype(F32)).astype(BF16)

    grid_spec = pltpu.PrefetchScalarGridSpec(
        num_scalar_prefetch=1, grid=(4,),
        in_specs=[pl.BlockSpec((1, 1, rh, cc), lambda s, c_ref: (s, c_ref[0], 0, 0)),
                  pl.BlockSpec((1, rh, cc), lambda s, c_ref: (s, 0, 0))],
        out_specs=pl.BlockSpec((1, rh, cc), lambda s, c_ref: (s, 0, 0)))
    return pl.pallas_call(body, grid_spec=grid_spec, out_shape=SDS((4, rh, cc), BF16),
                          compiler_params=_cparams(("parallel",)), name=name)(c_arr, g, got)


def _exchange_rider(parts, small=None):
    n = len(parts)

    def copies(ins, outs, sems):
        x, y, c, others = _place()
        out = [pltpu.make_async_remote_copy(
            src_ref=ins[a].at[2 * ox + oy], dst_ref=outs[a].at[j], send_sem=sems[0].at[3 * a + j],
            recv_sem=sems[1].at[3 * a + j], device_id=(ox, oy, c), device_id_type=MESH)
            for a in range(n) for j, (ox, oy) in enumerate(others)]
        own = []
        if small is not None:
            me = 4 * x + 2 * y + c
            peers = [(x, y, 1 - c)] + [(ox, oy, c) for ox, oy in others] + [(ox, oy, 1 - c) for ox, oy in others]
            out += [pltpu.make_async_remote_copy(
                src_ref=ins[n], dst_ref=outs[n].at[me], send_sem=sems[2].at[k], recv_sem=sems[3].at[k],
                device_id=peer, device_id_type=MESH) for k, peer in enumerate(peers)]
            own = [pltpu.make_async_copy(ins[n], outs[n].at[me], sems[4].at[0])]
        return out, own

    def first(*mine):
        out, own = copies(*mine)
        for cp in own + out:
            cp.start()

    def middle(*mine):
        pass

    def last(*mine):
        out, own = copies(*mine)
        for cp in out + own:
            cp.wait()

    shapes = [SDS((3,) + p.shape[1:], p.dtype) for p in parts]
    if small is None:
        return _Rider(parts, shapes, [3 * n, 3 * n], first, middle, last)
    return _Rider(parts + [small], shapes + [SDS((8,) + small.shape, small.dtype)], [3 * n, 3 * n, 7, 7, 1],
                  first, middle, last)


def _direct_rider(grads):
    n = len(grads)

    def copies(ins, outs, sems):
        x, y, c, others = _place()
        peers = [(x, y, 1 - c)] + [(ox, oy, c) for ox, oy in others] + [(ox, oy, 1 - c) for ox, oy in others]
        return [pltpu.make_async_remote_copy(
            src_ref=ins[a].at[2 * px + py, pc], dst_ref=outs[a].at[k], send_sem=sems[0].at[7 * a + k],
            recv_sem=sems[1].at[7 * a + k], device_id=(px, py, pc), device_id_type=MESH)
            for a in range(n) for k, (px, py, pc) in enumerate(peers)]

    def first(*mine):
        for cp in copies(*mine):
            cp.start()

    def middle(*mine):
        pass

    def last(*mine):
        for cp in copies(*mine):
            cp.wait()

    return _Rider(grads, [SDS((7,) + g.shape[2:], g.dtype) for g in grads], [7 * n, 7 * n], first, middle, last)


def _owner_sum(grads, landed, own_arr, name):
    rh, cc = landed.shape[1:]
    tr = rh // 2

    def body(own_ref, g_ref, p_ref, o_ref):
        total = g_ref[0, 0].astype(F32)
        for k in range(7):
            total = total + p_ref[k].astype(F32)
        o_ref[...] = total

    grid_spec = pltpu.PrefetchScalarGridSpec(
        num_scalar_prefetch=1, grid=(2,),
        in_specs=[pl.BlockSpec((1, 1, tr, cc), lambda i, own: (own[0], own[1], i, 0)),
                  pl.BlockSpec((7, tr, cc), lambda i, own: (0, i, 0))],
        out_specs=pl.BlockSpec((tr, cc), lambda i, own: (i, 0)))
    return pl.pallas_call(body, grid_spec=grid_spec, out_shape=SDS((rh, cc), F32),
                          compiler_params=_cparams(("parallel",)), name=name)(own_arr, grads, landed)


def _chip_sum(parts, landed, chip_arr, name):
    rh, cc = landed.shape[1:]
    tr = rh // 2

    def body(chip_ref, own_ref, p_ref, o_ref):
        o_ref[...] = ((own_ref[0].astype(F32) + p_ref[0].astype(F32)) + p_ref[1].astype(F32)) + p_ref[2].astype(F32)

    grid_spec = pltpu.PrefetchScalarGridSpec(
        num_scalar_prefetch=1, grid=(2,),
        in_specs=[pl.BlockSpec((1, tr, cc), lambda i, chip_ref: (chip_ref[0], i, 0)),
                  pl.BlockSpec((3, tr, cc), lambda i, chip_ref: (0, i, 0))],
        out_specs=pl.BlockSpec((tr, cc), lambda i, chip_ref: (i, 0)))
    return pl.pallas_call(body, grid_spec=grid_spec, out_shape=SDS((rh, cc), F32),
                          compiler_params=_cparams(("parallel",)), name=name)(chip_arr, parts, landed)


def _device_sum(p):
    def body(p_ref, o_ref):
        acc = p_ref[0]
        for k in range(1, 8):
            acc = acc + p_ref[k]
        o_ref[...] = acc

    return pl.pallas_call(body, out_shape=SDS(p.shape[1:], F32), name="small_sum")(p)


def _join_halves(halves):
    n = len(halves)

    def body(*refs):
        ins, outs = refs[:n], refs[n:2 * n]
        send_sems, recv_sems = refs[2 * n:]
        x, y, c, _ = _place()
        copies = [pltpu.make_async_remote_copy(
            src_ref=ins[a], dst_ref=outs[a], send_sem=send_sems.at[a], recv_sem=recv_sems.at[a],
            device_id=(x, y, 1 - c), device_id_type=MESH) for a in range(n)]
        for cp in copies:
            cp.start()
        for cp in copies:
            cp.wait()

    return pl.pallas_call(
        body, in_specs=[HBM_SPEC] * n, out_specs=[HBM_SPEC] * n,
        out_shape=[SDS(h.shape, h.dtype) for h in halves],
        scratch_shapes=[pltpu.SemaphoreType.DMA((n,)), pltpu.SemaphoreType.DMA((n,))],
        name="join_halves")(*halves)


def _adamw(w, g, m, v, name):
    rows, cols = w.shape
    tr = rows if rows <= 352 else (256 if rows % 256 == 0 else 352)

    def body(w_ref, g_ref, m_ref, v_ref, d_ref, nm_ref, nv_ref):
        gg = g_ref[...]
        nm = ADAM_B1 * m_ref[...] + (1.0 - ADAM_B1) * gg
        nv = ADAM_B2 * v_ref[...] + (1.0 - ADAM_B2) * (gg * gg)
        nm_ref[...] = nm
        nv_ref[...] = nv
        m_hat = nm / (1.0 - ADAM_B1 ** ADAM_STEP)
        v_hat = nv / (1.0 - ADAM_B2 ** ADAM_STEP)
        d_ref[...] = -ADAM_LR * (m_hat / (jnp.sqrt(v_hat) + ADAM_EPS) + ADAM_WD * w_ref[...])

    blk = pl.BlockSpec((tr, cols), lambda i: (i, 0))
    return pl.pallas_call(
        body, grid=(rows // tr,), in_specs=[blk] * 4, out_specs=[blk] * 3,
        out_shape=[SDS((rows, cols), F32)] * 3,
        compiler_params=_cparams(("parallel",)), name=name)(w, g, m, v)


def _adamw_halves(w, mine, theirs, m, v, c_arr, name):
    rows, cols = w.shape
    rh = rows // 2
    tr = rh if rh <= 352 else 256
    nh = rh // tr

    def body(c_ref, w_ref, mine_ref, theirs_ref, m_ref, v_ref, g_ref, d_ref, nm_ref, nv_ref):
        own = jnp.full((tr, cols), pl.program_id(0), jnp.int32) == c_ref[0]
        gg = jnp.where(own, mine_ref[...], theirs_ref[...])
        g_ref[...] = gg
        nm = ADAM_B1 * m_ref[...] + (1.0 - ADAM_B1) * gg
        nv = ADAM_B2 * v_ref[...] + (1.0 - ADAM_B2) * (gg * gg)
        nm_ref[...] = nm
        nv_ref[...] = nv
        m_hat = nm / (1.0 - ADAM_B1 ** ADAM_STEP)
        v_hat = nv / (1.0 - ADAM_B2 ** ADAM_STEP)
        d_ref[...] = -ADAM_LR * (m_hat / (jnp.sqrt(v_hat) + ADAM_EPS) + ADAM_WD * w_ref[...])

    whole = pl.BlockSpec((tr, cols), lambda hh, i, c_ref: (hh * nh + i, 0))
    part = pl.BlockSpec((tr, cols), lambda hh, i, c_ref: (i, 0))
    grid_spec = pltpu.PrefetchScalarGridSpec(
        num_scalar_prefetch=1, grid=(2, nh), in_specs=[whole, part, part, whole, whole], out_specs=[whole] * 4)
    return pl.pallas_call(body, grid_spec=grid_spec, out_shape=[SDS((rows, cols), F32)] * 4,
                          compiler_params=_cparams(("parallel", "parallel")), name=name)(c_arr, w, mine, theirs, m, v)


def _pack_small(pre_mix, post_mix, pre_ffn, post_ffn, rel_bias, b_forget, sinks):
    def at(row, v):
        return jnp.pad(v, ((row, 7 - row), (0, D_MODEL - v.shape[1])))
    return (at(0, pre_mix) + at(1, post_mix) + at(2, pre_ffn) + at(3, post_ffn)
            + at(4, rel_bias.reshape(1, N_BUCKETS * N_HEADS)) + at(5, jnp.concatenate([b_forget, sinks], axis=1)))


def _unpack_small(p):
    return dict(ln_pre_mix=p[0:1], ln_post_mix=p[1:2], ln_pre_ffn=p[2:3], ln_post_ffn=p[3:4],
                rel_bias=p[4, :N_BUCKETS * N_HEADS].reshape(N_BUCKETS, N_HEADS),
                b_forget=p[5:6, 0:N_HEADS], sinks=p[5:6, N_HEADS:2 * N_HEADS])


WEIGHTS = ("meta_tokens", "rel_bias", "ln_pre_mix", "ln_post_mix", "ln_pre_ffn", "ln_post_ffn",
           "w_in", "b_forget", "sinks", "w_out", "w_gate_up", "w_down")


def kernel(x, meta_tokens, rel_bias, ln_pre_mix, ln_post_mix, ln_pre_ffn, ln_post_ffn, w_in, b_forget, sinks, w_out, w_gate_up, w_down, loss_target, m_meta_tokens, m_rel_bias, m_ln_pre_mix, m_ln_post_mix, m_ln_pre_ffn, m_ln_post_ffn, m_w_in, m_b_forget, m_sinks, m_w_out, m_w_gate_up, m_w_down, v_meta_tokens, v_rel_bias, v_ln_pre_mix, v_ln_post_mix, v_ln_pre_ffn, v_ln_post_ffn, v_w_in, v_b_forget, v_sinks, v_w_out, v_w_gate_up, v_w_down):
    xi, yi, ci = lax.axis_index("x"), lax.axis_index("y"), lax.axis_index("c")
    chip = 2 * xi + yi
    c_arr = jnp.reshape(ci, (1,)).astype(jnp.int32)

    def halves(w, dtype):
        return w.astype(dtype).reshape(2, w.shape[0] // 2, w.shape[1])

    def with_own(gathered, shards):
        return [lax.dynamic_update_slice(got, own[None], (chip, 0, 0, 0)) for got, own in zip(gathered, shards)]

    shards = [halves(w_in[0], BF16), halves(meta_tokens, F32)]
    gw_in, g_meta = with_own(_run_alone(_gather_rider(shards, False), "gather_mixer_weights"), shards)
    out_shards = [halves(w_out[0], BF16)]
    ffn_shards = [halves(w_gate_up[0], BF16), halves(w_down[0], BF16)]

    def ffn_weights(carried):
        gw_gu, gw_dn = carried
        return gw_gu.reshape(D_MODEL, 2 * D_FF), gw_dn.reshape(D_FF, D_MODEL)

    early = {}

    def early_grads(dw_gu, dw_dn, dw_out):
        early["grads"] = [dw_out.reshape(4, 2, 128, D_MODEL), dw_gu.reshape(4, 2, 512, FF_T),
                          dw_dn.reshape(4, 2, 352, D_MODEL)]
        return _direct_rider(early["grads"])
    w_in_all = gw_in.reshape(4, D_MODEL, D_PROJ // 4).transpose(1, 0, 2).reshape(D_MODEL, D_PROJ)
    w_in_b = jnp.pad(w_in_all, ((0, 0), (0, D_PROJ_P - D_PROJ)))
    meta_all = g_meta.reshape(4, N_META, D_MODEL // 4).transpose(1, 0, 2).reshape(N_META, D_MODEL)

    loc = _local_step(x[0], loss_target[0], meta_all, rel_bias, ln_pre_mix, ln_post_mix, ln_pre_ffn, ln_post_ffn,
                      b_forget, sinks, w_in_b, _gather_rider(out_shards, True),
                      lambda carried: carried[0].reshape(D_MODEL, D_MODEL),
                      _gather_rider(ffn_shards, True, by_columns=(0,)), ffn_weights, early_grads)

    dw_in = loc["w_in"][:, :D_PROJ].reshape(D_MODEL, 4, D_PROJ // 4).transpose(1, 0, 2)
    dw_in = dw_in.reshape(4, 2, 512, D_PROJ // 4)
    small = jnp.concatenate(
        [_pack_small(loc["ln_pre_mix"], loc["ln_post_mix"], loc["ln_pre_ffn"], loc["ln_post_ffn"],
                     loc["rel_bias"], loc["b_forget"], loc["sinks"])
         + jnp.pad(loc["loss"].reshape(1, 1), ((LOSS_ROW, 7 - LOSS_ROW), (0, D_MODEL - 1))), loc["meta"]], axis=0)

    (got_in,) = _run_alone(_swap_rider([dw_in]), "swap_halves_late")
    part_in = _pair_sum(dw_in, got_in, c_arr, "pair_sum_late")
    landed_in, small_all = _run_alone(_exchange_rider([part_in], small), "exchange_late")
    chip_arr = jnp.reshape(chip, (1,)).astype(jnp.int32)
    own_arr = jnp.stack([chip, ci]).astype(jnp.int32)
    mine = [_chip_sum(part_in, landed_in, chip_arr, "chip_sum_in")] + [
        _owner_sum(g, l, own_arr, "owner_sum_%d" % a) for a, (g, l) in enumerate(zip(early["grads"], loc["landed"]))]
    small_sum = _device_sum(small_all)
    theirs = _join_halves(mine)
    g_meta_tokens = lax.dynamic_slice(small_sum[8:N_SMALL], (0, chip * (D_MODEL // 4)), (N_META, D_MODEL // 4))
    g_small = small_sum[0:8]

    grad = _unpack_small(g_small)
    grad.update(meta_tokens=g_meta_tokens)
    delta, new_m, new_v = {}, {}, {}
    big = dict(w_in=(w_in, m_w_in, v_w_in), w_out=(w_out, m_w_out, v_w_out),
               w_gate_up=(w_gate_up, m_w_gate_up, v_w_gate_up), w_down=(w_down, m_w_down, v_w_down))
    for (name, (w, m, v)), g_mine, g_theirs in zip(big.items(), mine, theirs):
        g, d, nm, nv = _adamw_halves(w[0], g_mine, g_theirs, m[0], v[0], c_arr, "adamw_" + name)
        grad[name], delta[name], new_m[name], new_v[name] = g[None], d[None], nm[None], nv[None]
    delta["meta_tokens"], new_m["meta_tokens"], new_v["meta_tokens"] = _adamw(
        meta_tokens, g_meta_tokens, m_meta_tokens, v_meta_tokens, "adamw_meta")
    d, nm, nv = _adamw(
        _pack_small(ln_pre_mix, ln_post_mix, ln_pre_ffn, ln_post_ffn, rel_bias, b_forget, sinks), g_small,
        _pack_small(m_ln_pre_mix, m_ln_post_mix, m_ln_pre_ffn, m_ln_post_ffn, m_rel_bias, m_b_forget, m_sinks),
        _pack_small(v_ln_pre_mix, v_ln_post_mix, v_ln_pre_ffn, v_ln_post_ffn, v_rel_bias, v_b_forget, v_sinks),
        "adamw_small")
    delta.update(_unpack_small(d))
    new_m.update(_unpack_small(nm))
    new_v.update(_unpack_small(nv))

    loss = small_sum[LOSS_ROW, 0]
    return (loss,loc["grad_x"][None], *[grad[k] for k in WEIGHTS], *[delta[k] for k in WEIGHTS],
            *[new_m[k] for k in WEIGHTS], *[new_v[k] for k in WEIGHTS])
```

```python
import math

import numpy as np
import jax
import jax.numpy as jnp
from jax import lax
from jax.experimental import pallas as pl
from jax.experimental.pallas import tpu as pltpu

F32 = jnp.float32
BF16 = jnp.bfloat16
MESH = pl.DeviceIdType.MESH
SDS = jax.ShapeDtypeStruct

D_MODEL = 1024
SEQ = 4096
N_META = 16
N_HEADS = 8
HALF = 64
D_FF = 2816
N_BUCKETS = 32
EPS = 1e-6
NEG = -1e30
SCALE = 0.125
PAD_ROWS = 112
ROW0 = PAD_ROWS + N_META
LP = ROW0 + SEQ
BLK = 128
NBLK = LP // BLK
TM = 384
NT = LP // TM
TM_PURE = LP // 2
TM_MID = LP // 4
TM_EPI = LP // 6
TN = 256
D_PROJ = 2312
D_PROJ_P = 2432
D_QKV = 2304
FF_T = 1408
VMEM_LIMIT = 56 * 1024 * 1024

ADAM_LR = 0.001
ADAM_B1 = 0.9
ADAM_B2 = 0.999
ADAM_EPS = 1e-08
ADAM_WD = 0.01
ADAM_STEP = 10

QA = 0
KA, VA = 4, 5
QB, KB, VB = 3, 5, 7
W2 = 256

NT_DIMS = (((1,), (1,)), ((), ()))
TN_DIMS = (((0,), (0,)), ((), ()))


def _cparams(sem):
    return pltpu.CompilerParams(dimension_semantics=sem, vmem_limit_bytes=VMEM_LIMIT)


def _t5_bucket_np(d):
    n = np.maximum(d, 0).astype(np.int32)
    nf = np.maximum(n, 1).astype(np.float32)
    large = 16 + (np.log(nf / np.float32(16)) / np.float32(math.log(8.0)) * np.float32(16)).astype(np.int32)
    large = np.minimum(large, N_BUCKETS - 1)
    return np.where(n < 16, n, large).astype(np.int32)


def _bucket_tables():
    qi = np.arange(BLK)[:, None]
    ki = np.arange(BLK)[None, :]
    return np.stack([_t5_bucket_np(qi - ki), _t5_bucket_np(qi - ki + BLK)])


def _rms(x):
    return lax.rsqrt(jnp.mean(x * x, axis=-1, keepdims=True) + EPS)


def _rms_bwd(n, r, gdy):
    return r * (gdy - n * jnp.mean(n * gdy, axis=-1, keepdims=True))


def _pre_mix(h0, gain, w_in_b):
    half = D_QKV // 2

    def body(h_ref, g_ref, w_ref, hn_ref, proj_ref, f_ref):
        x = h_ref[...]
        hn = (x * _rms(x) * g_ref[...]).astype(BF16)
        hn_ref[...] = hn
        proj_ref[:, :half] = jnp.dot(hn, w_ref[:, :half], preferred_element_type=F32).astype(BF16)
        p = jnp.dot(hn, w_ref[:, half:], preferred_element_type=F32)
        proj_ref[:, half:] = p[:, :half].astype(BF16)
        f_ref[...] = p[:, half:]

    return pl.pallas_call(
        body, grid=(LP // TM_MID,),
        in_specs=[pl.BlockSpec((TM_MID, D_MODEL), lambda i: (i, 0)),
                  pl.BlockSpec((1, D_MODEL), lambda i: (0, 0)),
                  pl.BlockSpec((D_MODEL, D_PROJ_P), lambda i: (0, 0))],
        out_specs=[pl.BlockSpec((TM_MID, D_MODEL), lambda i: (i, 0)),
                   pl.BlockSpec((TM_MID, D_QKV), lambda i: (i, 0)),
                   pl.BlockSpec((TM_MID, BLK), lambda i: (i, 0))],
        out_shape=[SDS((LP, D_MODEL), BF16), SDS((LP, D_QKV), BF16), SDS((LP, BLK), F32)],
        compiler_params=_cparams(("parallel",)), name="pre_mix")(h0, gain, w_in_b)


def _attn_out(o_a, o_b, w_out_b, h0, g_post, g_pre_ffn):
    def body(oa_ref, ob_ref, w_ref, h0_ref, gp_ref, gf_ref, a_ref, h1_ref, hn2_ref):
        a = (jnp.dot(oa_ref[...], w_ref[0:512, :], preferred_element_type=F32)
             + jnp.dot(ob_ref[...], w_ref[512:1024, :], preferred_element_type=F32))
        a_ref[...] = a
        h1 = h0_ref[...] + a * _rms(a) * gp_ref[...]
        h1_ref[...] = h1
        hn2_ref[...] = (h1 * _rms(h1) * gf_ref[...]).astype(BF16)

    row = lambda w: pl.BlockSpec((TM_EPI, w), lambda i: (i, 0))
    vec = pl.BlockSpec((1, D_MODEL), lambda i: (0, 0))
    return pl.pallas_call(
        body, grid=(LP // TM_EPI,),
        in_specs=[row(512), row(512), pl.BlockSpec((D_MODEL, D_MODEL), lambda i: (0, 0)), row(D_MODEL), vec, vec],
        out_specs=[row(D_MODEL), row(D_MODEL), row(D_MODEL)],
        out_shape=[SDS((LP, D_MODEL), F32), SDS((LP, D_MODEL), F32), SDS((LP, D_MODEL), BF16)],
        compiler_params=_cparams(("parallel",)), name="attn_out")(o_a, o_b, w_out_b, h0, g_post, g_pre_ffn)


def _ffn_up(hn2, w_gu_b):
    def body(x_ref, wg_ref, wu_ref, g_ref, u_ref, act_ref):
        x = x_ref[...]
        g = jnp.dot(x, wg_ref[...], preferred_element_type=F32)
        u = jnp.dot(x, wu_ref[...], preferred_element_type=F32)
        g_ref[...] = g.astype(BF16)
        u_ref[...] = u.astype(BF16)
        act_ref[...] = (g * (1.0 / (1.0 + jnp.exp(-g))) * u).astype(BF16)

    out = pl.BlockSpec((TM_PURE, TN), lambda i, j: (i, j))
    return pl.pallas_call(
        body, grid=(LP // TM_PURE, D_FF // TN),
        in_specs=[pl.BlockSpec((TM_PURE, D_MODEL), lambda i, j: (i, 0)),
                  pl.BlockSpec((D_MODEL, TN), lambda i, j: (0, j)),
                  pl.BlockSpec((D_MODEL, TN), lambda i, j: (0, j + D_FF // TN))],
        out_specs=[out, out, out],
        out_shape=[SDS((LP, D_FF), BF16)] * 3,
        compiler_params=_cparams(("parallel", "parallel")), name="ffn_up")(hn2, w_gu_b, w_gu_b)


def _ffn_down_loss(act, w_dn_b, h1, tgt, g_post_ffn):
    def body(act_ref, w_ref, h1_ref, t0_ref, t1_ref, t2_ref, g_ref, dff_ref, dy_ref, loss_ref, dg_ref):
        i = pl.program_id(0)
        target = jnp.concatenate([t0_ref[...], t1_ref[...], t2_ref[...]], axis=0)

        @pl.when(i == 0)
        def _():
            loss_ref[...] = jnp.zeros_like(loss_ref)
            dg_ref[...] = jnp.zeros_like(dg_ref)

        ff = jnp.dot(act_ref[...], w_ref[...], preferred_element_type=F32)
        r = _rms(ff)
        n = ff * r
        g = g_ref[...]
        y = h1_ref[...] + n * g
        rows = i * TM + lax.broadcasted_iota(jnp.int32, (TM, D_MODEL), 0)
        diff = jnp.where(rows >= ROW0, y - target, 0.0)
        loss_ref[...] += 0.5 * jnp.sum(diff * diff) / D_MODEL
        dy = diff / D_MODEL
        dy_ref[...] = dy
        dg_ref[...] += jnp.sum(dy * n, axis=0, keepdims=True)
        dff_ref[...] = _rms_bwd(n, r, g * dy).astype(BF16)

    row = pl.BlockSpec((TM, D_MODEL), lambda i: (i, 0))
    tblk = lambda j: pl.BlockSpec((BLK, D_MODEL), lambda i: (jnp.maximum(3 * i - 1 + j, 0), 0))
    return pl.pallas_call(
        body, grid=(NT,),
        in_specs=[pl.BlockSpec((TM, D_FF), lambda i: (i, 0)), pl.BlockSpec((D_FF, D_MODEL), lambda i: (0, 0)),
                  row, tblk(0), tblk(1), tblk(2), pl.BlockSpec((1, D_MODEL), lambda i: (0, 0))],
        out_specs=[row, row, pl.BlockSpec((8, BLK), lambda i: (0, 0)), pl.BlockSpec((1, D_MODEL), lambda i: (0, 0))],
        out_shape=[SDS((LP, D_MODEL), BF16), SDS((LP, D_MODEL), F32), SDS((8, BLK), F32), SDS((1, D_MODEL), F32)],
        compiler_params=_cparams(("arbitrary",)), name="ffn_down_loss")(act, w_dn_b, h1, tgt, tgt, tgt, g_post_ffn)


def _ffn_down_bwd(dff, w_dn_b, g, u):
    def body(d_ref, w_ref, g_ref, u_ref, dg_ref, du_ref):
        dact = lax.dot_general(d_ref[...], w_ref[...], NT_DIMS, preferred_element_type=F32)
        gg = g_ref[...].astype(F32)
        sig = 1.0 / (1.0 + jnp.exp(-gg))
        dg_ref[...] = (dact * u_ref[...].astype(F32) * sig * (1.0 + gg * (1.0 - sig))).astype(BF16)
        du_ref[...] = (dact * gg * sig).astype(BF16)

    blk = pl.BlockSpec((TM_PURE, TN), lambda i, j: (i, j))
    return pl.pallas_call(
        body, grid=(LP // TM_PURE, D_FF // TN),
        in_specs=[pl.BlockSpec((TM_PURE, D_MODEL), lambda i, j: (i, 0)),
                  pl.BlockSpec((TN, D_MODEL), lambda i, j: (j, 0)), blk, blk],
        out_specs=[blk, blk],
        out_shape=[SDS((LP, D_FF), BF16)] * 2,
        compiler_params=_cparams(("parallel", "parallel")), name="ffn_down_bwd")(dff, w_dn_b, g, u)


def _ffn_up_bwd(dg, du, w_gu_b, h1, a, dy, g_pre_ffn, g_post_mix):
    def body(dg_ref, du_ref, w_ref, h1_ref, a_ref, dy_ref, gf_ref, gp_ref,
             dh1_ref, da_ref, dgf_ref, dgp_ref, acc):
        i = pl.program_id(0)
        s = pl.program_id(1)

        @pl.when((i == 0) & (s == 0))
        def _():
            dgf_ref[...] = jnp.zeros_like(dgf_ref)
            dgp_ref[...] = jnp.zeros_like(dgp_ref)

        @pl.when(s == 0)
        def _():
            acc[...] = jnp.zeros_like(acc)

        @pl.when(s < 2)
        def _():
            acc[...] += lax.dot_general(dg_ref[...], w_ref[...], NT_DIMS, preferred_element_type=F32)

        @pl.when(s >= 2)
        def _():
            acc[...] += lax.dot_general(du_ref[...], w_ref[...], NT_DIMS, preferred_element_type=F32)

        @pl.when(s == 3)
        def _():
            dhn2 = acc[...]
            h1 = h1_ref[...]
            r2 = _rms(h1)
            n2 = h1 * r2
            dgf_ref[...] += jnp.sum(dhn2 * n2, axis=0, keepdims=True)
            dh1 = dy_ref[...] + _rms_bwd(n2, r2, gf_ref[...] * dhn2)
            dh1_ref[...] = dh1
            av = a_ref[...]
            ra = _rms(av)
            na = av * ra
            dgp_ref[...] += jnp.sum(dh1 * na, axis=0, keepdims=True)
            da_ref[...] = _rms_bwd(na, ra, gp_ref[...] * dh1).astype(BF16)

    row = pl.BlockSpec((TM_EPI, D_MODEL), lambda i, s: (i, 0))
    vec = pl.BlockSpec((1, D_MODEL), lambda i, s: (0, 0))
    return pl.pallas_call(
        body, grid=(LP // TM_EPI, 4),
        in_specs=[pl.BlockSpec((TM_EPI, FF_T), lambda i, s: (i, jnp.minimum(s, 1))),
                  pl.BlockSpec((TM_EPI, FF_T), lambda i, s: (i, jnp.maximum(s - 2, 0))),
                  pl.BlockSpec((D_MODEL, FF_T), lambda i, s: (0, s)),
                  row, row, row, vec, vec],
        out_specs=[row, row, vec, vec],
        out_shape=[SDS((LP, D_MODEL), F32), SDS((LP, D_MODEL), BF16), SDS((1, D_MODEL), F32), SDS((1, D_MODEL), F32)],
        scratch_shapes=[pltpu.VMEM((TM_EPI, D_MODEL), F32)],
        compiler_params=_cparams(("arbitrary", "arbitrary")), name="ffn_up_bwd",
    )(dg, du, w_gu_b, h1, a, dy, g_pre_ffn, g_post_mix)


def _attn_out_bwd(da, w_out_b):
    def body(d_ref, w_ref, o_ref):
        o_ref[...] = lax.dot_general(d_ref[...], w_ref[...], NT_DIMS, preferred_element_type=F32).astype(BF16)

    row = pl.BlockSpec((TM_PURE, D_MODEL), lambda i: (i, 0))
    return pl.pallas_call(
        body, grid=(LP // TM_PURE,),
        in_specs=[row, pl.BlockSpec((D_MODEL, D_MODEL), lambda i: (0, 0))],
        out_specs=row, out_shape=SDS((LP, D_MODEL), BF16),
        compiler_params=_cparams(("parallel",)), name="attn_out_bwd")(da, w_out_b)


def _pre_mix_bwd(dq_a, dq_b, dk_b, dv_b, dk_a, dv_a, df, w_in_b, h0, dh1, g_pre_mix):
    def body(qa_ref, qb_ref, kb_ref, vb_ref, ka_ref, va_ref, f_ref, w_ref, h0_ref, dh1_ref, g_ref,
             dproj_ref, dh0_ref, dg_ref):
        i = pl.program_id(0)

        @pl.when(i == 0)
        def _():
            dg_ref[...] = jnp.zeros_like(dg_ref)

        dproj = jnp.concatenate(
            [qa_ref[...], ka_ref[...].astype(BF16), va_ref[...].astype(BF16), (qb_ref[...] * SCALE).astype(BF16),
             kb_ref[...], vb_ref[...], f_ref[...].astype(BF16)], axis=1)
        dproj_ref[...] = dproj
        dhn = lax.dot_general(dproj, w_ref[...], NT_DIMS, preferred_element_type=F32)
        x = h0_ref[...]
        r = _rms(x)
        n = x * r
        dg_ref[...] += jnp.sum(dhn * n, axis=0, keepdims=True)
        dh0_ref[...] = dh1_ref[...] + _rms_bwd(n, r, g_ref[...] * dhn)

    row = lambda w: pl.BlockSpec((TM_EPI, w), lambda i: (i, 0))
    vec = pl.BlockSpec((1, D_MODEL), lambda i: (0, 0))
    return pl.pallas_call(
        body, grid=(LP // TM_EPI,),
        in_specs=[row(512), row(512), row(512), row(512), row(BLK), row(BLK), row(BLK),
                  pl.BlockSpec((D_MODEL, D_PROJ_P), lambda i: (0, 0)), row(D_MODEL), row(D_MODEL), vec],
        out_specs=[row(D_PROJ_P), row(D_MODEL), vec],
        out_shape=[SDS((LP, D_PROJ_P), BF16), SDS((LP, D_MODEL), F32), SDS((1, D_MODEL), F32)],
        compiler_params=_cparams(("arbitrary",)), name="pre_mix_bwd",
    )(dq_a, dq_b, dk_b, dv_b, dk_a, dv_a, df, w_in_b, h0, dh1, g_pre_mix)


def _mm_tn(parts, b, tm, name, out_dtype=F32):
    widths = [p.shape[1] for p in parts]
    m_total = sum(widths)
    n = b.shape[1]
    whole = len(parts) > 1
    n_k = LP // TM_MID
    assert (tm == m_total) if whole else (m_total % tm == 0)

    def body(*refs):
        a_refs, b_ref, o_ref, acc = refs[:-3], refs[-3], refs[-2], refs[-1]
        k = pl.program_id(1)

        @pl.when(k == 0)
        def _():
            acc[...] = jnp.zeros_like(acc)
        a = a_refs[0][...] if not whole else jnp.concatenate([r[...] for r in a_refs], axis=1)
        acc[...] += lax.dot_general(a, b_ref[...], TN_DIMS, preferred_element_type=F32)

        @pl.when(k == n_k - 1)
        def _():
            o_ref[...] = acc[...].astype(out_dtype)

    a_specs = ([pl.BlockSpec((TM_MID, w), lambda mi, k: (k, 0)) for w in widths] if whole
               else [pl.BlockSpec((TM_MID, tm), lambda mi, k: (k, mi))])
    return pl.pallas_call(
        body, grid=(m_total // tm, n_k),
        in_specs=a_specs + [pl.BlockSpec((TM_MID, n), lambda mi, k: (k, 0))],
        out_specs=pl.BlockSpec((tm, n), lambda mi, k: (mi, 0)),
        out_shape=SDS((m_total, n), out_dtype),
        scratch_shapes=[pltpu.VMEM((tm, n), F32)],
        compiler_params=_cparams(("parallel", "arbitrary")), name=name)(*parts, b)


def _dw_gate_up(hn2, dg, du):
    n_k = LP // TM_MID

    def body(a_ref, dg_ref, du_ref, o_ref, acc):
        s = pl.program_id(0)
        k = pl.program_id(1)

        @pl.when(k == 0)
        def _():
            acc[...] = jnp.zeros_like(acc)

        @pl.when(s < 2)
        def _():
            acc[...] += lax.dot_general(a_ref[...], dg_ref[...], TN_DIMS, preferred_element_type=F32)

        @pl.when(s >= 2)
        def _():
            acc[...] += lax.dot_general(a_ref[...], du_ref[...], TN_DIMS, preferred_element_type=F32)

        @pl.when(k == n_k - 1)
        def _():
            o_ref[0] = acc[...].astype(BF16)

    return pl.pallas_call(
        body, grid=(4, n_k),
        in_specs=[pl.BlockSpec((TM_MID, D_MODEL), lambda s, k: (k, 0)),
                  pl.BlockSpec((TM_MID, FF_T), lambda s, k: (k, jnp.minimum(s, 1))),
                  pl.BlockSpec((TM_MID, FF_T), lambda s, k: (k, jnp.maximum(s - 2, 0)))],
        out_specs=pl.BlockSpec((1, D_MODEL, FF_T), lambda s, k: (s, 0, 0)),
        out_shape=SDS((4, D_MODEL, FF_T), BF16),
        scratch_shapes=[pltpu.VMEM((D_MODEL, FF_T), F32)],
        compiler_params=_cparams(("parallel", "arbitrary")), name="dw_gate_up")(hn2, dg, du)


def _split3(x):
    hi = x.astype(BF16)
    r1 = x - hi.astype(F32)
    mid = r1.astype(BF16)
    lo = (r1 - mid.astype(F32)).astype(BF16)
    return hi, mid, lo


def _tri_matmul(tri, x):
    hi, mid, lo = _split3(x)
    dot = lambda t: jnp.dot(tri, t, preferred_element_type=F32)
    return dot(hi) + dot(mid) + dot(lo)


def _forget_cumsum(f, b_forget_p):
    def body(f_ref, b_ref, cum_ref, carry):
        i = pl.program_id(0)

        @pl.when(i == 0)
        def _():
            carry[...] = jnp.zeros_like(carry)

        z = f_ref[...] + b_ref[...]
        ls = jnp.minimum(z, 0.0) - jnp.log(1.0 + jnp.exp(-jnp.abs(z)))
        rows = i * TM + lax.broadcasted_iota(jnp.int32, (TM, BLK), 0)
        ls = jnp.where(rows >= PAD_ROWS, ls, 0.0)
        r = lax.broadcasted_iota(jnp.int32, (TM, TM), 0)
        c = lax.broadcasted_iota(jnp.int32, (TM, TM), 1)
        tri = (c <= r).astype(BF16)
        cum = _tri_matmul(tri, ls) + carry[...]
        cum_ref[...] = cum
        carry[...] = cum[TM - 1:TM, :]

    return pl.pallas_call(
        body, grid=(NT,),
        in_specs=[pl.BlockSpec((TM, BLK), lambda i: (i, 0)), pl.BlockSpec((1, BLK), lambda i: (0, 0))],
        out_specs=pl.BlockSpec((TM, BLK), lambda i: (i, 0)),
        out_shape=SDS((LP, BLK), F32),
        scratch_shapes=[pltpu.VMEM((1, BLK), F32)],
        compiler_params=_cparams(("arbitrary",)), name="forget_cumsum")(f, b_forget_p)


def _forget_cumsum_bwd(dcum, f, b_forget_p):
    def body(d_ref, f_ref, b_ref, df_ref, db_ref, carry):
        i = pl.program_id(0)

        @pl.when(i == 0)
        def _():
            carry[...] = jnp.zeros_like(carry)
            db_ref[...] = jnp.zeros_like(db_ref)

        blk = NT - 1 - i
        r = lax.broadcasted_iota(jnp.int32, (TM, TM), 0)
        c = lax.broadcasted_iota(jnp.int32, (TM, TM), 1)
        tri = (c >= r).astype(BF16)
        d = d_ref[...]
        dls = _tri_matmul(tri, d) + carry[...]
        carry[...] = dls[0:1, :]
        z = f_ref[...] + b_ref[...]
        rows = blk * TM + lax.broadcasted_iota(jnp.int32, (TM, BLK), 0)
        df = jnp.where(rows >= PAD_ROWS, dls / (1.0 + jnp.exp(z)), 0.0)
        df_ref[...] = df
        db_ref[...] += jnp.sum(df, axis=0, keepdims=True)

    rev = pl.BlockSpec((TM, BLK), lambda i: (NT - 1 - i, 0))
    vec = pl.BlockSpec((1, BLK), lambda i: (0, 0))
    return pl.pallas_call(
        body, grid=(NT,),
        in_specs=[rev, rev, vec],
        out_specs=[rev, vec],
        out_shape=[SDS((LP, BLK), F32), SDS((1, BLK), F32)],
        scratch_shapes=[pltpu.VMEM((1, BLK), F32)],
        compiler_params=_cparams(("arbitrary",)), name="forget_cumsum_bwd")(dcum, f, b_forget_p)


def _lane_half(rows):
    return lax.broadcasted_iota(jnp.int32, (rows, BLK), 1) // HALF


def _fox_valid(qi, kj):
    qrow = qi * TM + lax.broadcasted_iota(jnp.int32, (TM, TM), 0)
    krow = kj * TM + lax.broadcasted_iota(jnp.int32, (TM, TM), 1)
    return (krow <= qrow) & ((krow >= PAD_ROWS) | (qrow < PAD_ROWS))


class _Rider:
    def __init__(self, operands, out_shapes, sem_counts, first, middle, last):
        self.operands, self.out_shapes, self.sem_counts = list(operands), list(out_shapes), list(sem_counts)
        self.first, self.middle, self.last = first, middle, last

    def scratch(self):
        return [pltpu.SemaphoreType.DMA((k,)) for k in self.sem_counts]

    def split(self, refs, n_in, n_out, n_scratch):
        a, b = len(self.operands), len(self.out_shapes)
        ins, mine_in = refs[:n_in], refs[n_in:n_in + a]
        outs, mine_out = refs[n_in + a:n_in + a + n_out], refs[n_in + a + n_out:n_in + a + n_out + b]
        rest = refs[n_in + a + n_out + b:]
        return ins, outs, rest[:n_scratch], (mine_in, mine_out, rest[n_scratch:])

    def at_steps(self, mine, is_first, is_middle, is_last):
        for cond, fn in ((is_first, self.first), (is_middle, self.middle), (is_last, self.last)):
            pl.when(cond)(lambda fn=fn: fn(*mine))


HBM_SPEC = pl.BlockSpec(memory_space=pltpu.HBM)


N_AUG = 4
QCHUNKS = ((0, 128), (128, 128), (256, 128))
KSUB = 384
AHEAD = 5
AHEAD_BWD = 1


def _fox_prep(proj, cum):
    def body(q0_ref, q1_ref, k0_ref, k1_ref, v0_ref, v1_ref, c_ref, qa_ref, ka_ref, vt_ref):
        half = _lane_half(TM)
        lane = lax.broadcasted_iota(jnp.int32, (TM, BLK), 1)
        for pp in range(4):
            cols = slice(pp * BLK, (pp + 1) * BLK)
            q_ref, k_ref, v_ref = ((q0_ref, k0_ref, v0_ref), (q1_ref, k1_ref, v1_ref))[pp // 2]
            part = slice((pp % 2) * BLK, (pp % 2 + 1) * BLK)
            qs = q_ref[:, part].astype(F32) * SCALE
            kp = k_ref[:, part].astype(F32)
            vp = v_ref[:, part]
            vt_ref[cols, :] = vp.astype(F32).T.astype(BF16)
            for e in range(2):
                h = 2 * pp + e
                a = (1 - e) * HALF
                blk = slice(h * BLK, (h + 1) * BLK)
                hi, mid, lo = _split3(-c_ref[:, h:h + 1])
                q_aug = jnp.where(half == e, qs, jnp.where((lane >= a) & (lane < a + 3), 1.0, 0.0))
                k_aug = jnp.where(half == e, kp, jnp.where(
                    lane == a, hi.astype(F32), jnp.where(lane == a + 1, mid.astype(F32), jnp.where(
                        lane == a + 2, lo.astype(F32), jnp.where(lane == a + 3, 1.0, 0.0)))))
                qa_ref[blk, :] = q_aug.T.astype(BF16)
                ka_ref[:, blk] = k_aug.astype(BF16)

    row = lambda blk: pl.BlockSpec((TM, W2), lambda i: (i, blk))
    wide = pl.BlockSpec((TM, 1024), lambda i: (i, 0))
    return pl.pallas_call(
        body, grid=(NT,),
        in_specs=[row(QB), row(QB + 1), row(KB), row(KB + 1), row(VB), row(VB + 1),
                  pl.BlockSpec((TM, BLK), lambda i: (i, 0))],
        out_specs=[pl.BlockSpec((1024, TM), lambda i: (0, i)), wide, pl.BlockSpec((512, TM), lambda i: (0, i))],
        out_shape=[SDS((1024, LP), BF16), SDS((LP, 1024), BF16), SDS((512, LP), BF16)],
        compiler_params=_cparams(("parallel",)), name="fox_prep")(proj, proj, proj, proj, proj, proj, cum)


def _over_keys(reduce, x):
    slabs = x.reshape(x.shape[0] // HALF, HALF, x.shape[1])
    return reduce(reduce(slabs, axis=0), axis=0, keepdims=True)


def _fox_valid_t(qi, kj, c, r):
    krow = kj * TM + r * KSUB + lax.broadcasted_iota(jnp.int32, (KSUB, c[1]), 0)
    qrow = qi * TM + c[0] + lax.broadcasted_iota(jnp.int32, (KSUB, c[1]), 1)
    return (krow <= qrow) & ((krow >= PAD_ROWS) | (qrow < PAD_ROWS))


def _fox_fwd(q_aug, k_aug, v_t, rider):
    pairs = [(qi, kj) for qi in range(NT) for kj in range(qi + 1)]
    n_pairs = len(pairs)

    def body(qi_ref, kj_ref, *refs):
        (q_ref, k_ref, vt_ref), (o_ref, lse_ref), (m_s, l_s, acc_s), mine = rider.split(refs, 3, 2, 3)
        n = pl.program_id(0)
        qi = qi_ref[n]
        kj = kj_ref[n]
        rider.at_steps(mine, n == 0, n == n_pairs // 2, n == n_pairs - 1)

        @pl.when(kj == 0)
        def _():
            m_s[...] = jnp.full_like(m_s, NEG)
            l_s[...] = jnp.zeros_like(l_s)
            acc_s[...] = jnp.zeros_like(acc_s)

        def tile(masked):
            steps = [(h, c, r) for h in range(N_HEADS) for c in QCHUNKS for r in range(TM // KSUB)]

            def scores(h, c, r):
                blk = slice(h * BLK, (h + 1) * BLK)
                return jnp.dot(k_ref[r * KSUB:(r + 1) * KSUB, blk], q_ref[blk, c[0]:c[0] + c[1]],
                               preferred_element_type=F32)

            ahead = [scores(*st) for st in steps[:AHEAD]]
            for n, (h, c, r) in enumerate(steps):
                s_t = ahead.pop(0)
                if n + AHEAD < len(steps):
                    ahead.append(scores(*steps[n + AHEAD]))
                cs = slice(c[0], c[0] + c[1])
                if masked:
                    s_t = jnp.where(_fox_valid_t(qi, kj, c, r), s_t, NEG)
                m_prev = m_s[h, :, cs]
                m_new = jnp.maximum(m_prev, _over_keys(jnp.max, s_t))
                p_t = jnp.exp(s_t - m_new)
                alpha = jnp.exp(m_prev - m_new)
                l_s[h, :, cs] = alpha * l_s[h, :, cs] + _over_keys(jnp.sum, p_t)
                m_s[h, :, cs] = m_new
                vt = vt_ref[h * HALF:(h + 1) * HALF, r * KSUB:(r + 1) * KSUB]
                acc_s[h, :, cs] = acc_s[h, :, cs] * alpha + jnp.dot(vt, p_t.astype(BF16),
                                                                    preferred_element_type=F32)

        @pl.when((kj < qi) & (kj > 0))
        def _():
            tile(False)

        @pl.when((kj == qi) | (kj == 0))
        def _():
            tile(True)

        @pl.when(kj == qi)
        def _():
            for pp in range(4):
                both = jnp.concatenate([acc_s[2 * pp] * (1.0 / l_s[2 * pp]),
                                        acc_s[2 * pp + 1] * (1.0 / l_s[2 * pp + 1])], axis=0)
                o_ref[:, pp * BLK:(pp + 1) * BLK] = both.T.astype(BF16)
            for h in range(N_HEADS):
                lse_ref[h] = m_s[h] + jnp.log(l_s[h])

    grid_spec = pltpu.PrefetchScalarGridSpec(
        num_scalar_prefetch=2, grid=(n_pairs,),
        in_specs=[pl.BlockSpec((1024, TM), lambda n, qi, kj: (0, qi[n])),
                  pl.BlockSpec((TM, 1024), lambda n, qi, kj: (kj[n], 0)),
                  pl.BlockSpec((512, TM), lambda n, qi, kj: (0, kj[n]))] + [HBM_SPEC] * len(rider.operands),
        out_specs=[pl.BlockSpec((TM, 512), lambda n, qi, kj: (qi[n], 0)),
                   pl.BlockSpec((N_HEADS, 1, TM), lambda n, qi, kj: (0, 0, qi[n]))]
        + [HBM_SPEC] * len(rider.out_shapes),
        scratch_shapes=[pltpu.VMEM((N_HEADS, 1, TM), F32), pltpu.VMEM((N_HEADS, 1, TM), F32),
                        pltpu.VMEM((N_HEADS, HALF, TM), F32)] + rider.scratch())
    o_b, lse, *carried = pl.pallas_call(
        body, grid_spec=grid_spec,
        out_shape=[SDS((LP, 512), BF16), SDS((N_HEADS, 1, LP), F32)] + rider.out_shapes,
        compiler_params=_cparams(("arbitrary",)), name="fox_fwd",
    )(jnp.asarray([p[0] for p in pairs], jnp.int32), jnp.asarray([p[1] for p in pairs], jnp.int32),
      q_aug, k_aug, v_t, *rider.operands)
    return o_b, lse, carried


def _fox_bwd(proj, o_b, dmix, lse, ck_t, rider):
    pairs = [(kj, qi) for kj in range(NT) for qi in range(kj, NT)]
    n_pairs = len(pairs)

    def body(kj_ref, qi_ref, *refs):
        ((q0_ref, q1_ref, k0_ref, k1_ref, v0_ref, v1_ref, o_ref, do_ref, lse_ref, ck_ref),
         (dq_ref, dk_ref, dv_ref, dck_ref, dcq_ref), (dk_s, dv_s, dck_s), mine) = rider.split(refs, 10, 5, 3)
        n = pl.program_id(0)
        kj = kj_ref[n]
        qi = qi_ref[n]
        rider.at_steps(mine, n == 0, n == n_pairs // 2, n == n_pairs - 1)

        @pl.when(n == 0)
        def _():
            dq_ref[...] = jnp.zeros_like(dq_ref)
            dcq_ref[...] = jnp.zeros_like(dcq_ref)

        @pl.when(qi == kj)
        def _():
            dk_s[...] = jnp.zeros_like(dk_s)
            dv_s[...] = jnp.zeros_like(dv_s)
            dck_s[...] = jnp.zeros_like(dck_s)

        def tile(masked):
            valid = _fox_valid(qi, kj) if masked else None
            half = _lane_half(TM)
            q0 = pl.multiple_of(qi * TM, TM)
            lane = lax.broadcasted_iota(jnp.int32, (TM, BLK), 1)
            row_sums = jnp.zeros((TM, BLK), F32)
            pair_ops = {}

            def operands(pp):
                if pp not in pair_ops:
                    cols = slice(pp * BLK, (pp + 1) * BLK)
                    q_ref, k_ref, v_ref = ((q0_ref, k0_ref, v0_ref), (q1_ref, k1_ref, v1_ref))[pp // 2]
                    part = slice((pp % 2) * BLK, (pp % 2 + 1) * BLK)
                    pair_ops[pp] = ((q_ref[:, part].astype(F32) * SCALE).astype(BF16), k_ref[:, part],
                                    v_ref[:, part], do_ref[:, cols])
                return pair_ops[pp]

            def scores(pp, e):
                qs, kp, vp, dop = operands(pp)
                ke = jnp.where(half == e, kp, jnp.zeros_like(kp))
                ve = jnp.where(half == e, vp, jnp.zeros_like(vp))
                return (lax.dot_general(qs, ke, NT_DIMS, preferred_element_type=F32),
                        lax.dot_general(dop, ve, NT_DIMS, preferred_element_type=F32), ke)

            steps = [(pp, e) for pp in range(4) for e in range(2)]
            ahead = [scores(*st) for st in steps[:AHEAD_BWD]]
            for n, (pp, e) in enumerate(steps):
                raw, dp, ke = ahead.pop(0)
                if n + AHEAD_BWD < len(steps):
                    ahead.append(scores(*steps[n + AHEAD_BWD]))
                h = 2 * pp + e
                cols = slice(pp * BLK, (pp + 1) * BLK)
                qs, kp, vp, dop = operands(pp)
                if e == 0:
                    prod = dop.astype(F32) * o_ref[:, cols].astype(F32)
                    d0 = jnp.sum(jnp.where(half == 0, prod, 0.0), axis=1, keepdims=True)
                    d1 = jnp.sum(prod, axis=1, keepdims=True) - d0
                    dq = jnp.zeros((TM, BLK), F32)
                    dks, dvs = [], []
                t = raw - ck_ref[h] - lse_ref[h]
                if masked:
                    t = jnp.where(valid, t, NEG)
                p = jnp.exp(t)
                ds = p * (dp - (d0 if e == 0 else d1))
                dck_s[h] += jnp.sum(ds, axis=0, keepdims=True)
                row_sums = jnp.where(lane == h, jnp.sum(ds, axis=1, keepdims=True), row_sums)
                ds_b = ds.astype(BF16)
                dq = dq + jnp.dot(ds_b, ke, preferred_element_type=F32)
                dks.append(lax.dot_general(ds_b, qs, TN_DIMS, preferred_element_type=F32))
                dvs.append(lax.dot_general(p.astype(BF16), dop, TN_DIMS, preferred_element_type=F32))
                if e == 1:
                    dq_ref[pl.ds(q0, TM), cols] += dq
                    dk_s[pp] += jnp.where(half == 0, dks[0], dks[1])
                    dv_s[pp] += jnp.where(half == 0, dvs[0], dvs[1])
            dcq_ref[pl.ds(q0, TM), :] += row_sums

        @pl.when((qi > kj) & (kj > 0))
        def _():
            tile(False)

        @pl.when((qi == kj) | (kj == 0))
        def _():
            tile(True)

        @pl.when(qi == NT - 1)
        def _():
            for pp in range(4):
                cols = slice(pp * BLK, (pp + 1) * BLK)
                dk_ref[:, cols] = dk_s[pp].astype(BF16)
                dv_ref[:, cols] = dv_s[pp].astype(BF16)
            dck_ref[...] = dck_s[...]

    qrow = lambda blk, w=512: pl.BlockSpec((TM, w), lambda n, kj, qi: (qi[n], blk))
    krow = lambda blk: pl.BlockSpec((TM, W2), lambda n, kj, qi: (kj[n], blk))
    grid_spec = pltpu.PrefetchScalarGridSpec(
        num_scalar_prefetch=2, grid=(n_pairs,),
        in_specs=[qrow(QB, W2), qrow(QB + 1, W2), krow(KB), krow(KB + 1), krow(VB), krow(VB + 1), qrow(0), qrow(1),
                  pl.BlockSpec((N_HEADS, TM, 1), lambda n, kj, qi: (0, qi[n], 0)),
                  pl.BlockSpec((N_HEADS, 1, TM), lambda n, kj, qi: (0, 0, kj[n]))] + [HBM_SPEC] * len(rider.operands),
        out_specs=[pl.BlockSpec((LP, 512), lambda n, kj, qi: (0, 0)),
                   pl.BlockSpec((TM, 512), lambda n, kj, qi: (kj[n], 0)),
                   pl.BlockSpec((TM, 512), lambda n, kj, qi: (kj[n], 0)),
                   pl.BlockSpec((N_HEADS, 1, TM), lambda n, kj, qi: (0, 0, kj[n])),
                   pl.BlockSpec((LP, BLK), lambda n, kj, qi: (0, 0))] + [HBM_SPEC] * len(rider.out_shapes),
        scratch_shapes=[pltpu.VMEM((4, TM, BLK), F32), pltpu.VMEM((4, TM, BLK), F32),
                        pltpu.VMEM((N_HEADS, 1, TM), F32)] + rider.scratch())
    dq, dk, dv, dck, dcq, *carried = pl.pallas_call(
        body, grid_spec=grid_spec,
        out_shape=[SDS((LP, 512), F32), SDS((LP, 512), BF16), SDS((LP, 512), BF16), SDS((N_HEADS, 1, LP), F32),
                   SDS((LP, BLK), F32)] + rider.out_shapes,
        compiler_params=_cparams(("arbitrary",)), name="fox_bwd",
    )(jnp.asarray([p[0] for p in pairs], jnp.int32), jnp.asarray([p[1] for p in pairs], jnp.int32),
      proj, proj, proj, proj, proj, proj, o_b, dmix, lse, ck_t, *rider.operands)
    return dq, dk, dv, dck, dcq, carried


N_SEG = 3
N_KEY = N_SEG * BLK
GROUP = 4
QW = GROUP * BLK


def _bucket_tables_t():
    return np.ascontiguousarray(_bucket_tables().transpose(0, 2, 1))


def _stack_heads(ref, g, scale):
    half = _lane_half(BLK)
    out = []
    for pair in range(2):
        x = ref[:, (2 * g + pair) * BLK:(2 * g + pair + 1) * BLK].astype(F32) * scale
        swapped = pltpu.roll(x, HALF, 1)
        for e in range(2):
            out.append(jnp.where(half == g, x if e == g else swapped, 0.0).astype(BF16))
    return jnp.concatenate(out, axis=0)


def _unstack_heads(x_t, g, ref, scale):
    for pair in range(2):
        both = jnp.concatenate([x_t[:, (2 * pair) * BLK:(2 * pair + 1) * BLK],
                                x_t[:, (2 * pair + 1) * BLK:(2 * pair + 2) * BLK]], axis=0)
        ref[:, (2 * g + pair) * BLK:(2 * g + pair + 1) * BLK] = (both.T * scale).astype(ref.dtype)


def _swa_tables(tab_ref, sink_ref, bkt_ref, tbl, sink_row):
    kk = lax.broadcasted_iota(jnp.int32, (BLK, BLK), 0)
    qq = lax.broadcasted_iota(jnp.int32, (BLK, BLK), 1)
    neg = jnp.full((BLK, BLK), NEG, F32)
    lane = lax.broadcasted_iota(jnp.int32, (1, QW), 1) // BLK
    for g in range(2):
        row = jnp.zeros((1, QW), F32)
        for hh in range(GROUP):
            h = GROUP * g + hh
            cols = slice(hh * BLK, (hh + 1) * BLK)
            row = jnp.where(lane == hh, sink_ref[0, h], row)

            def step(b, carry, h=h):
                t = tab_ref[b, h]
                return jnp.where(bkt_ref[0] == b, t, carry[0]), jnp.where(bkt_ref[1] == b, t, carry[1])
            zero = jnp.zeros((BLK, BLK), F32)
            cur, prev = lax.fori_loop(0, N_BUCKETS, step, (zero, zero))
            far = jnp.full((BLK, BLK), tab_ref[N_BUCKETS - 1, h], F32)
            causal = jnp.where(kk <= qq, cur, neg)
            segments = [
                (neg, neg, jnp.where(kk >= PAD_ROWS, causal, neg)),
                (jnp.where(kk >= PAD_ROWS, prev, neg), neg, causal),
                (jnp.where(kk >= PAD_ROWS, far, neg), jnp.where(kk > qq, prev, neg), causal)]
            for case in range(3):
                for seg in range(N_SEG):
                    tbl[case, g, seg * BLK:(seg + 1) * BLK, cols] = segments[case][seg]
        sink_row[g] = row


def _swa_prep(proj):
    rows = LP // 3

    def body(k_ref, v_ref, kt_ref, vt_ref):
        kt_ref[...] = k_ref[...].astype(F32).T.astype(BF16)
        vt_ref[...] = v_ref[...].astype(F32).T.astype(BF16)

    col = pl.BlockSpec((BLK, rows), lambda i: (0, i))
    return pl.pallas_call(
        body, grid=(3,),
        in_specs=[pl.BlockSpec((rows, BLK), lambda i: (i, KA)), pl.BlockSpec((rows, BLK), lambda i: (i, VA))],
        out_specs=[col, col], out_shape=[SDS((BLK, LP), BF16)] * 2,
        compiler_params=_cparams(("parallel",)), name="swa_prep")(proj, proj)


def _seg_specs(rows_major, col):
    idx = [lambda i: 0, lambda i: jnp.maximum(i - 1, 0), lambda i: i]
    if rows_major:
        return [pl.BlockSpec((BLK, BLK), lambda i, f=f: (f(i), col)) for f in idx]
    return [pl.BlockSpec((BLK, BLK), lambda i, f=f: (0, f(i))) for f in idx]


def _swa_fwd(proj, vt_a, rel_bias, sinks, bkt_t, rider):
    def body(*refs):
        ((tab_ref, sink_ref, bkt_ref, q_ref, km_ref, kp_ref, kc_ref, vm_ref, vp_ref, vc_ref), (o_ref, lse_ref),
         (tbl, sink_row), mine) = rider.split(refs, 10, 2, 2)
        i = pl.program_id(0)
        rider.at_steps(mine, i == 0, i == NBLK // 2, i == NBLK - 1)

        @pl.when(i == 0)
        def _():
            _swa_tables(tab_ref, sink_ref, bkt_ref, tbl, sink_row)

        case = jnp.minimum(i, 2)
        k_cat = jnp.concatenate([km_ref[...], kp_ref[...], kc_ref[...]], axis=0)
        vt_cat = jnp.concatenate([vm_ref[...], vp_ref[...], vc_ref[...]], axis=1)
        raw = [lax.dot_general(k_cat, _stack_heads(q_ref, g, SCALE), NT_DIMS, preferred_element_type=F32)
               for g in range(2)]
        for g in range(2):
            s_t = raw[g] + tbl[case, g]
            sink = sink_row[g]
            m = jnp.maximum(_over_keys(jnp.max, s_t), sink)
            p_t = jnp.exp(s_t - m)
            l = _over_keys(jnp.sum, p_t) + jnp.exp(sink - m)
            o_t = jnp.dot(vt_cat[g * HALF:(g + 1) * HALF, :], p_t.astype(BF16), preferred_element_type=F32)
            _unstack_heads(o_t * (1.0 / l), g, o_ref, 1.0)
            lse = m + jnp.log(l)
            for hh in range(GROUP):
                lse_ref[GROUP * g + hh] = lse[:, hh * BLK:(hh + 1) * BLK]

    smem = pl.BlockSpec(memory_space=pltpu.SMEM)
    o_a, lse, *carried = pl.pallas_call(
        body, grid=(NBLK,),
        in_specs=[smem, smem, pl.BlockSpec((2, BLK, BLK), lambda i: (0, 0, 0)),
                  pl.BlockSpec((BLK, 512), lambda i: (i, QA))] + _seg_specs(True, KA) + _seg_specs(False, 0)
        + [HBM_SPEC] * len(rider.operands),
        out_specs=[pl.BlockSpec((BLK, 512), lambda i: (i, 0)),
                   pl.BlockSpec((N_HEADS, 1, BLK), lambda i: (0, 0, i))] + [HBM_SPEC] * len(rider.out_shapes),
        out_shape=[SDS((LP, 512), BF16), SDS((N_HEADS, 1, LP), F32)] + rider.out_shapes,
        scratch_shapes=[pltpu.VMEM((3, 2, N_KEY, QW), F32), pltpu.VMEM((2, 1, QW), F32)] + rider.scratch(),
        compiler_params=_cparams(("arbitrary",)), name="swa_fwd",
    )(rel_bias, sinks, bkt_t, proj, proj, proj, proj, vt_a, vt_a, vt_a, *rider.operands)
    return o_a, lse, carried


def _swa_bwd(proj, kt_a, o_a, dmix, lse, rel_bias, sinks, bkt_t):
    def body(tab_ref, sink_ref, bkt_ref, q_ref, km_ref, kp_ref, kc_ref, vm_ref, vp_ref, vc_ref,
             tm_ref, tp_ref, tc_ref, o_ref, do_ref, lse_ref,
             dq_ref, dk_ref, dv_ref, dbias_ref, dsink_ref, tbl, sink_row, acc, dsk):
        i = pl.program_id(0)

        @pl.when(i == 0)
        def _():
            _swa_tables(tab_ref, sink_ref, bkt_ref, tbl, sink_row)
            dk_ref[...] = jnp.zeros_like(dk_ref)
            dv_ref[...] = jnp.zeros_like(dv_ref)
            acc[...] = jnp.zeros_like(acc)
            dsk[...] = jnp.zeros_like(dsk)

        case = jnp.minimum(i, 2)
        first = jnp.full((BLK, QW), i, jnp.int32) == 1
        k_cat = jnp.concatenate([km_ref[...], kp_ref[...], kc_ref[...]], axis=0)
        v_cat = jnp.concatenate([vm_ref[...], vp_ref[...], vc_ref[...]], axis=0)
        kt_cat = jnp.concatenate([tm_ref[...], tp_ref[...], tc_ref[...]], axis=1)
        dk_cat = jnp.zeros((N_KEY, BLK), F32)
        dv_cat = jnp.zeros((N_KEY, BLK), F32)
        for g in range(2):
            d_parts = []
            for pair in range(2):
                cols = slice((2 * g + pair) * BLK, (2 * g + pair + 1) * BLK)
                prod_t = (do_ref[:, cols].astype(F32) * o_ref[:, cols].astype(F32)).T
                d_parts += [jnp.sum(prod_t[:HALF], axis=0, keepdims=True),
                            jnp.sum(prod_t[HALF:], axis=0, keepdims=True)]
            d_row = jnp.concatenate(d_parts, axis=1)
            lse_row = jnp.concatenate([lse_ref[GROUP * g + hh] for hh in range(GROUP)], axis=1)
            q_st = _stack_heads(q_ref, g, SCALE)
            do_st = _stack_heads(do_ref, g, 1.0)
            s_t = lax.dot_general(k_cat, q_st, NT_DIMS, preferred_element_type=F32) + tbl[case, g]
            p_t = jnp.exp(s_t - lse_row)
            dp_t = lax.dot_general(v_cat, do_st, NT_DIMS, preferred_element_type=F32)
            ds_t = p_t * (dp_t - d_row)
            dsk[g] += -jnp.exp(sink_row[g] - lse_row) * d_row
            acc[g, 0:BLK] += jnp.where(first, 0.0, ds_t[0:BLK])
            acc[g, BLK:2 * BLK] += jnp.where(first, ds_t[0:BLK], ds_t[BLK:2 * BLK])
            acc[g, 2 * BLK:N_KEY] += ds_t[2 * BLK:N_KEY]
            ds_b = ds_t.astype(BF16)
            dk_cat = dk_cat + jnp.dot(ds_b, q_st, preferred_element_type=F32)
            dv_cat = dv_cat + jnp.dot(p_t.astype(BF16), do_st, preferred_element_type=F32)
            dq_t = jnp.dot(kt_cat[g * HALF:(g + 1) * HALF, :], ds_b, preferred_element_type=F32)
            _unstack_heads(dq_t, g, dq_ref, SCALE)

        prev0 = pl.multiple_of(jnp.maximum(i - 1, 0) * BLK, BLK)
        cur0 = pl.multiple_of(i * BLK, BLK)
        for ref, cat in ((dk_ref, dk_cat), (dv_ref, dv_cat)):
            ref[0:BLK, :] += cat[0:BLK]
            ref[pl.ds(prev0, BLK), :] += cat[BLK:2 * BLK]
            ref[pl.ds(cur0, BLK), :] += cat[2 * BLK:N_KEY]

        @pl.when(i == NBLK - 1)
        def _():
            lane = lax.broadcasted_iota(jnp.int32, (1, BLK), 1)

            def per_bucket(b, carry):
                row = jnp.zeros((1, BLK), F32)
                for h in range(N_HEADS):
                    g, cols = h // GROUP, slice((h % GROUP) * BLK, (h % GROUP + 1) * BLK)
                    val = (jnp.sum(jnp.where(bkt_ref[0] == b, acc[g, 2 * BLK:N_KEY, cols], 0.0), keepdims=True)
                           + jnp.sum(jnp.where(bkt_ref[1] == b, acc[g, BLK:2 * BLK, cols], 0.0), keepdims=True))
                    row = jnp.where(lane == h, val, row)
                dbias_ref[pl.ds(b, 1), :] = row
                return carry

            lax.fori_loop(0, N_BUCKETS, per_bucket, 0)
            far = jnp.zeros((1, BLK), F32)
            dsr = jnp.zeros((1, BLK), F32)
            for h in range(N_HEADS):
                g, cols = h // GROUP, slice((h % GROUP) * BLK, (h % GROUP + 1) * BLK)
                far = jnp.where(lane == h, jnp.sum(acc[g, 0:BLK, cols], keepdims=True), far)
                dsr = jnp.where(lane == h, jnp.sum(dsk[g, :, cols], keepdims=True), dsr)
            dbias_ref[N_BUCKETS - 1:N_BUCKETS, :] += far
            dsink_ref[...] = dsr

    smem = pl.BlockSpec(memory_space=pltpu.SMEM)
    blk512 = lambda col: pl.BlockSpec((BLK, 512), lambda i: (i, col))
    full = lambda r, c: pl.BlockSpec((r, c), lambda i: (0, 0))
    return pl.pallas_call(
        body, grid=(NBLK,),
        in_specs=[smem, smem, pl.BlockSpec((2, BLK, BLK), lambda i: (0, 0, 0)), blk512(QA)]
        + _seg_specs(True, KA) + _seg_specs(True, VA) + _seg_specs(False, 0)
        + [blk512(0), blk512(0), pl.BlockSpec((N_HEADS, 1, BLK), lambda i: (0, 0, i))],
        out_specs=[blk512(0), full(LP, BLK), full(LP, BLK), full(N_BUCKETS, BLK), full(1, BLK)],
        out_shape=[SDS((LP, 512), BF16), SDS((LP, BLK), F32), SDS((LP, BLK), F32),
                   SDS((N_BUCKETS, BLK), F32), SDS((1, BLK), F32)],
        scratch_shapes=[pltpu.VMEM((3, 2, N_KEY, QW), F32), pltpu.VMEM((2, 1, QW), F32),
                        pltpu.VMEM((2, N_KEY, QW), F32), pltpu.VMEM((2, 1, QW), F32)],
        compiler_params=_cparams(("arbitrary",)), name="swa_bwd",
    )(rel_bias, sinks, bkt_t, proj, proj, proj, proj, proj, proj, proj, kt_a, kt_a, kt_a, o_a, dmix, lse)


def _local_step(x, tgt, meta, rel_bias, g_pre_mix, g_post_mix, g_pre_ffn, g_post_ffn, b_forget, sinks,
                w_in_b, out_rider, out_weight, ffn_rider, ffn_weights, early_grads):
    bkt_t = jnp.asarray(_bucket_tables_t())
    h0 = jnp.concatenate([jnp.zeros((PAD_ROWS, D_MODEL), F32), meta, x], axis=0)
    b_p = jnp.pad(b_forget, ((0, 0), (0, BLK - N_HEADS)))

    hn1, proj, f = _pre_mix(h0, g_pre_mix, w_in_b)
    kt_a, vt_a = _swa_prep(proj)
    o_a, lse_a, carried_out = _swa_fwd(proj, vt_a, rel_bias, sinks, bkt_t, out_rider)
    w_out_b = out_weight(carried_out)
    cum = _forget_cumsum(f, b_p)
    ck_t = cum[:, :N_HEADS].T.reshape(N_HEADS, 1, LP)
    q_aug, k_aug, v_t = _fox_prep(proj, cum)
    o_b, lse_row, carried = _fox_fwd(q_aug, k_aug, v_t, ffn_rider)
    lse_b = lse_row.reshape(N_HEADS, LP, 1)
    w_gu_b, w_dn_b = ffn_weights(carried)
    a, h1, hn2 = _attn_out(o_a, o_b, w_out_b, h0, g_post_mix, g_pre_ffn)
    g, u, act = _ffn_up(hn2, w_gu_b)
    dff, dy, loss_blk, dg_post_ffn = _ffn_down_loss(act, w_dn_b, h1, tgt, g_post_ffn)

    dw_dn = _mm_tn([act], dff, FF_T, "dw_down", BF16)
    dg, du = _ffn_down_bwd(dff, w_dn_b, g, u)
    dw_gu = _dw_gate_up(hn2, dg, du)
    dh1, da, dg_pre_ffn, dg_post_mix = _ffn_up_bwd(dg, du, w_gu_b, h1, a, dy, g_pre_ffn, g_post_mix)
    dw_out = _mm_tn([o_a, o_b], da, D_MODEL, "dw_out", BF16)
    dmix = _attn_out_bwd(da, w_out_b)
    dq_b, dk_b, dv_b, dck, dcq, landed = _fox_bwd(proj, o_b, dmix, lse_b, ck_t, early_grads(dw_gu, dw_dn, dw_out))
    dq_a, dk_a, dv_a, dbias, dsink = _swa_bwd(proj, kt_a, o_a, dmix, lse_a, rel_bias, sinks, bkt_t)
    dcum = dcq - jnp.pad(dck.reshape(N_HEADS, LP).T, ((0, 0), (0, BLK - N_HEADS)))
    df, db = _forget_cumsum_bwd(dcum, f, b_p)
    dproj, dh0, dg_pre_mix = _pre_mix_bwd(dq_a, dq_b, dk_b, dv_b, dk_a, dv_a, df, w_in_b, h0, dh1, g_pre_mix)
    dw_in = _mm_tn([hn1], dproj, D_MODEL, "dw_in", BF16)

    return dict(loss=loss_blk[0, 0], grad_x=dh0[ROW0:], meta=dh0[PAD_ROWS:ROW0],
                rel_bias=dbias[:, :N_HEADS], ln_pre_mix=dg_pre_mix, ln_post_mix=dg_post_mix,
                ln_pre_ffn=dg_pre_ffn, ln_post_ffn=dg_post_ffn, b_forget=db[:, :N_HEADS],
                sinks=dsink[:, :N_HEADS], w_in=dw_in, w_out=dw_out, w_gate_up=dw_gu, w_down=dw_dn,
                landed=landed)


N_SMALL = 24
LOSS_ROW = 6


def _place():
    x, y, c = lax.axis_index("x"), lax.axis_index("y"), lax.axis_index("c")
    return x, y, c, [(1 - x, y), (x, 1 - y), (1 - x, 1 - y)]


def _run_alone(rider, name):
    a, b = len(rider.operands), len(rider.out_shapes)

    def body(*refs):
        mine = (refs[:a], refs[a:a + b], refs[a + b:])
        rider.first(*mine)
        rider.middle(*mine)
        rider.last(*mine)

    return pl.pallas_call(body, in_specs=[HBM_SPEC] * a, out_specs=[HBM_SPEC] * b, out_shape=rider.out_shapes,
                          scratch_shapes=rider.scratch(), name=name)(*rider.operands)


def _gather_rider(shards, own_too, by_columns=()):
    n = len(shards)

    def slot(a, outs, chip, h):
        if a in by_columns:
            cols = shards[a].shape[2]
            return outs[a].at[h, :, pl.ds(pl.multiple_of(chip * cols, BLK), cols)]
        return outs[a].at[chip, h]

    def own_copies(ins, outs, sems):
        x, y, _, _ = _place()
        if not own_too:
            return []
        return [pltpu.make_async_copy(ins[a].at[h], slot(a, outs, 2 * x + y, h), sems[2].at[2 * a + h])
                for a in range(n) for h in range(2)]

    def copies(ins, outs, sems):
        send_sems, recv_sems = sems[:2]
        x, y, c, others = _place()
        chip = 2 * x + y
        sibling = (x, y, 1 - c)

        def rc(a, k, src, dst, to):
            return pltpu.make_async_remote_copy(src_ref=src, dst_ref=dst, send_sem=send_sems.at[6 * a + k],
                                                recv_sem=recv_sems.at[6 * a + k], device_id=to, device_id_type=MESH)

        pairs = [(a, j, ox, oy) for a in range(n) for j, (ox, oy) in enumerate(others)]
        there = lambda a, ox, oy, h: slot(a, outs, 2 * ox + oy, h)
        return dict(
            sent=lambda: [rc(a, j, ins[a].at[c], slot(a, outs, chip, c), (ox, oy, c)) for a, j, ox, oy in pairs],
            landed=lambda: [rc(a, j, there(a, ox, oy, c), there(a, ox, oy, c), sibling) for a, j, ox, oy in pairs],
            passed=lambda: [rc(a, 3 + j, there(a, ox, oy, c), there(a, ox, oy, c), sibling)
                            for a, j, ox, oy in pairs],
            arriving=lambda: [rc(a, 3 + j, there(a, ox, oy, 1 - c), there(a, ox, oy, 1 - c), sibling)
                              for a, j, ox, oy in pairs])

    def first(*mine):
        for cp in copies(*mine)["sent"]() + own_copies(*mine):
            cp.start()

    def middle(*mine):
        kinds = copies(*mine)
        for got, cp in zip(kinds["landed"](), kinds["passed"]()):
            got.wait_recv()
            cp.start()

    def last(*mine):
        kinds = copies(*mine)
        for cp in kinds["arriving"]():
            cp.wait_recv()
        for cp in kinds["sent"]() + kinds["passed"]():
            cp.wait_send()
        for cp in own_copies(*mine):
            cp.wait()

    shapes = [SDS((2, s.shape[1], 4 * s.shape[2]) if a in by_columns else (4,) + s.shape, s.dtype)
              for a, s in enumerate(shards)]
    return _Rider(shards, shapes, [6 * n, 6 * n] + [2 * n] * own_too, first, middle, last)


def _swap_rider(grads):
    n = len(grads)

    def copies(ins, outs, sems):
        x, y, c, _ = _place()
        return [pltpu.make_async_remote_copy(
            src_ref=ins[a].at[s, 1 - c], dst_ref=outs[a].at[s], send_sem=sems[0].at[4 * a + s],
            recv_sem=sems[1].at[4 * a + s], device_id=(x, y, 1 - c), device_id_type=MESH)
            for a in range(n) for s in range(4)]

    def first(*mine):
        for cp in copies(*mine):
            cp.start()

    def middle(*mine):
        pass

    def last(*mine):
        for cp in copies(*mine):
            cp.wait()

    return _Rider(grads, [SDS((4,) + g.shape[2:], g.dtype) for g in grads], [4 * n, 4 * n], first, middle, last)


def _pair_sum(g, got, c_arr, name):
    rh, cc = got.shape[1:]

    def body(c_ref, g_ref, p_ref, o_ref):
        o_ref[0] = (g_ref[0, 0].astype(F32) + p_ref[0].astype(F32)).astype(BF16)

    grid_spec = pltpu.PrefetchScalarGridSpec(
        num_scalar_prefetch=1, grid=(4,),
        in_specs=[pl.BlockSpec((1, 1, rh, cc), lambda s, c_ref: (s, c_ref[0], 0, 0)),
                  pl.BlockSpec((1, rh, cc), lambda s, c_ref: (s, 0, 0))],
        out_specs=pl.BlockSpec((1, rh, cc), lambda s, c_ref: (s, 0, 0)))
    return pl.pallas_call(body, grid_spec=grid_spec, out_shape=SDS((4, rh, cc), BF16),
                          compiler_params=_cparams(("parallel",)), name=name)(c_arr, g, got)


def _exchange_rider(parts, small=None):
    n = len(parts)

    def copies(ins, outs, sems):
        x, y, c, others = _place()
        out = [pltpu.make_async_remote_copy(
            src_ref=ins[a].at[2 * ox + oy], dst_ref=outs[a].at[j], send_sem=sems[0].at[3 * a + j],
            recv_sem=sems[1].at[3 * a + j], device_id=(ox, oy, c), device_id_type=MESH)
            for a in range(n) for j, (ox, oy) in enumerate(others)]
        own = []
        if small is not None:
            me = 4 * x + 2 * y + c
            peers = [(x, y, 1 - c)] + [(ox, oy, c) for ox, oy in others] + [(ox, oy, 1 - c) for ox, oy in others]
            out += [pltpu.make_async_remote_copy(
                src_ref=ins[n], dst_ref=outs[n].at[me], send_sem=sems[2].at[k], recv_sem=sems[3].at[k],
                device_id=peer, device_id_type=MESH) for k, peer in enumerate(peers)]
            own = [pltpu.make_async_copy(ins[n], outs[n].at[me], sems[4].at[0])]
        return out, own

    def first(*mine):
        out, own = copies(*mine)
        for cp in own + out:
            cp.start()

    def middle(*mine):
        pass

    def last(*mine):
        out, own = copies(*mine)
        for cp in out + own:
            cp.wait()

    shapes = [SDS((3,) + p.shape[1:], p.dtype) for p in parts]
    if small is None:
        return _Rider(parts, shapes, [3 * n, 3 * n], first, middle, last)
    return _Rider(parts + [small], shapes + [SDS((8,) + small.shape, small.dtype)], [3 * n, 3 * n, 7, 7, 1],
                  first, middle, last)


def _direct_rider(grads):
    n = len(grads)

    def copies(ins, outs, sems):
        x, y, c, others = _place()
        peers = [(x, y, 1 - c)] + [(ox, oy, c) for ox, oy in others] + [(ox, oy, 1 - c) for ox, oy in others]
        return [pltpu.make_async_remote_copy(
            src_ref=ins[a].at[2 * px + py, pc], dst_ref=outs[a].at[k], send_sem=sems[0].at[7 * a + k],
            recv_sem=sems[1].at[7 * a + k], device_id=(px, py, pc), device_id_type=MESH)
            for a in range(n) for k, (px, py, pc) in enumerate(peers)]

    def first(*mine):
        for cp in copies(*mine):
            cp.start()

    def middle(*mine):
        pass

    def last(*mine):
        for cp in copies(*mine):
            cp.wait()

    return _Rider(grads, [SDS((7,) + g.shape[2:], g.dtype) for g in grads], [7 * n, 7 * n], first, middle, last)


def _owner_sum(grads, landed, own_arr, after, name):
    rh, cc = landed.shape[1:]
    tr = rh // 2

    def body(own_ref, g_ref, p_ref, after_ref, o_ref):
        total = g_ref[0, 0].astype(F32)
        for k in range(7):
            total = total + p_ref[k].astype(F32)
        o_ref[...] = total

    grid_spec = pltpu.PrefetchScalarGridSpec(
        num_scalar_prefetch=1, grid=(2,),
        in_specs=[pl.BlockSpec((1, 1, tr, cc), lambda i, own: (own[0], own[1], i, 0)),
                  pl.BlockSpec((7, tr, cc), lambda i, own: (0, i, 0)), pl.BlockSpec(memory_space=pl.ANY)],
        out_specs=pl.BlockSpec((tr, cc), lambda i, own: (i, 0)))
    return pl.pallas_call(body, grid_spec=grid_spec, out_shape=SDS((rh, cc), F32),
                          compiler_params=_cparams(("parallel",)), name=name)(own_arr, grads, landed, after)


SEM_SPEC = pl.BlockSpec(memory_space=pltpu.SEMAPHORE)
N_LATE = 10


def _late_copies(part_ref, landed_ref, small_ref, all_ref, send_sems, recv_sems):
    x, y, c, others = _place()
    me = 4 * x + 2 * y + c
    peers = [(x, y, 1 - c)] + [(ox, oy, c) for ox, oy in others] + [(ox, oy, 1 - c) for ox, oy in others]
    big = [pltpu.make_async_remote_copy(
        src_ref=part_ref.at[2 * ox + oy], dst_ref=landed_ref.at[j], send_sem=send_sems.at[j], recv_sem=recv_sems.at[j],
        device_id=(ox, oy, c), device_id_type=MESH) for j, (ox, oy) in enumerate(others)]
    small = [pltpu.make_async_remote_copy(
        src_ref=small_ref, dst_ref=all_ref.at[me], send_sem=send_sems.at[3 + k], recv_sem=recv_sems.at[3 + k],
        device_id=peer, device_id_type=MESH) for k, peer in enumerate(peers)]
    return big + small


def _late_exchange_start(part, small):
    def body(part_ref, landed_ref, small_ref, all_ref, send_sems, recv_sems, part_o, landed_o, small_o, all_o, token):
        for cp in _late_copies(part_ref, landed_ref, small_ref, all_ref, send_sems, recv_sems):
            cp.start()
        token[...] = jnp.zeros_like(token)

    hbm = lambda a: pltpu.HBM(a.shape, a.dtype)
    landed = lax.empty((3,) + part.shape[1:], part.dtype)
    everyone = lax.empty((8,) + small.shape, small.dtype)
    operands = [pltpu.with_memory_space_constraint(a, pltpu.HBM) for a in (part, landed, small, everyone)]
    return pl.pallas_call(
        body, name="late_exchange_start",
        out_shape=(pltpu.SemaphoreType.DMA((N_LATE,)), pltpu.SemaphoreType.DMA((N_LATE,)),
                   hbm(part), hbm(landed), hbm(small), hbm(everyone), SDS((8, BLK), F32)),
        in_specs=[HBM_SPEC] * 4,
        out_specs=(SEM_SPEC, SEM_SPEC, HBM_SPEC, HBM_SPEC, HBM_SPEC, HBM_SPEC, pl.BlockSpec(memory_space=pltpu.VMEM)),
        input_output_aliases={0: 2, 1: 3, 2: 4, 3: 5},
        compiler_params=pltpu.CompilerParams(has_side_effects=pltpu.SideEffectType.DATAFLOW_SIDE_EFFECTING),
    )(*operands)


def _late_exchange_wait(send_sems, recv_sems, part, landed, small, everyone, after):
    def body(part_ref, landed_ref, small_ref, all_ref, send_sems, recv_sems, after_ref, part_o, landed_o, small_o, all_o):
        for cp in _late_copies(part_ref, landed_ref, small_ref, all_ref, send_sems, recv_sems):
            cp.wait_send()
            cp.wait_recv()

    hbm = lambda a: pltpu.HBM(a.shape, a.dtype)
    out = pl.pallas_call(
        body, name="late_exchange_wait",
        out_shape=(hbm(part), hbm(landed), hbm(small), hbm(everyone)),
        in_specs=[HBM_SPEC] * 4 + [SEM_SPEC, SEM_SPEC, pl.BlockSpec(memory_space=pl.ANY)],
        out_specs=(HBM_SPEC,) * 4, input_output_aliases={0: 0, 1: 1, 2: 2, 3: 3},
        compiler_params=pltpu.CompilerParams(has_side_effects=pltpu.SideEffectType.DATAFLOW_SIDE_EFFECTING),
    )(part, landed, small, everyone, send_sems, recv_sems, after)
    return out[0], out[1], out[3]


def _chip_sum(parts, landed, chip_arr, name):
    rh, cc = landed.shape[1:]
    tr = rh // 2

    def body(chip_ref, own_ref, p_ref, o_ref):
        o_ref[...] = ((own_ref[0].astype(F32) + p_ref[0].astype(F32)) + p_ref[1].astype(F32)) + p_ref[2].astype(F32)

    grid_spec = pltpu.PrefetchScalarGridSpec(
        num_scalar_prefetch=1, grid=(2,),
        in_specs=[pl.BlockSpec((1, tr, cc), lambda i, chip_ref: (chip_ref[0], i, 0)),
                  pl.BlockSpec((3, tr, cc), lambda i, chip_ref: (0, i, 0))],
        out_specs=pl.BlockSpec((tr, cc), lambda i, chip_ref: (i, 0)))
    return pl.pallas_call(body, grid_spec=grid_spec, out_shape=SDS((rh, cc), F32),
                          compiler_params=_cparams(("parallel",)), name=name)(chip_arr, parts, landed)


def _device_sum(p):
    def body(p_ref, o_ref):
        acc = p_ref[0]
        for k in range(1, 8):
            acc = acc + p_ref[k]
        o_ref[...] = acc

    return pl.pallas_call(body, out_shape=SDS(p.shape[1:], F32), name="small_sum")(p)


def _join_halves(halves, name):
    n = len(halves)

    def body(*refs):
        ins, outs = refs[:n], refs[n:2 * n]
        send_sems, recv_sems = refs[2 * n:]
        x, y, c, _ = _place()
        copies = [pltpu.make_async_remote_copy(
            src_ref=ins[a], dst_ref=outs[a], send_sem=send_sems.at[a], recv_sem=recv_sems.at[a],
            device_id=(x, y, 1 - c), device_id_type=MESH) for a in range(n)]
        for cp in copies:
            cp.start()
        for cp in copies:
            cp.wait()

    return pl.pallas_call(
        body, in_specs=[HBM_SPEC] * n, out_specs=[HBM_SPEC] * n,
        out_shape=[SDS(h.shape, h.dtype) for h in halves],
        scratch_shapes=[pltpu.SemaphoreType.DMA((n,)), pltpu.SemaphoreType.DMA((n,))],
        name=name)(*halves)


def _adamw(w, g, m, v, name):
    rows, cols = w.shape
    tr = rows if rows <= 352 else (256 if rows % 256 == 0 else 352)

    def body(w_ref, g_ref, m_ref, v_ref, d_ref, nm_ref, nv_ref):
        gg = g_ref[...]
        nm = ADAM_B1 * m_ref[...] + (1.0 - ADAM_B1) * gg
        nv = ADAM_B2 * v_ref[...] + (1.0 - ADAM_B2) * (gg * gg)
        nm_ref[...] = nm
        nv_ref[...] = nv
        m_hat = nm / (1.0 - ADAM_B1 ** ADAM_STEP)
        v_hat = nv / (1.0 - ADAM_B2 ** ADAM_STEP)
        d_ref[...] = -ADAM_LR * (m_hat / (jnp.sqrt(v_hat) + ADAM_EPS) + ADAM_WD * w_ref[...])

    blk = pl.BlockSpec((tr, cols), lambda i: (i, 0))
    return pl.pallas_call(
        body, grid=(rows // tr,), in_specs=[blk] * 4, out_specs=[blk] * 3,
        out_shape=[SDS((rows, cols), F32)] * 3,
        compiler_params=_cparams(("parallel",)), name=name)(w, g, m, v)


def _adamw_halves(w, mine, theirs, m, v, c_arr, name):
    rows, cols = w.shape
    rh = rows // 2
    tr = rh if rh <= 352 else 256
    nh = rh // tr

    def body(c_ref, w_ref, mine_ref, theirs_ref, m_ref, v_ref, g_ref, d_ref, nm_ref, nv_ref):
        own = jnp.full((tr, cols), pl.program_id(0), jnp.int32) == c_ref[0]
        gg = jnp.where(own, mine_ref[...], theirs_ref[...])
        g_ref[...] = gg
        nm = ADAM_B1 * m_ref[...] + (1.0 - ADAM_B1) * gg
        nv = ADAM_B2 * v_ref[...] + (1.0 - ADAM_B2) * (gg * gg)
        nm_ref[...] = nm
        nv_ref[...] = nv
        m_hat = nm / (1.0 - ADAM_B1 ** ADAM_STEP)
        v_hat = nv / (1.0 - ADAM_B2 ** ADAM_STEP)
        d_ref[...] = -ADAM_LR * (m_hat / (jnp.sqrt(v_hat) + ADAM_EPS) + ADAM_WD * w_ref[...])

    whole = pl.BlockSpec((tr, cols), lambda hh, i, c_ref: (hh * nh + i, 0))
    part = pl.BlockSpec((tr, cols), lambda hh, i, c_ref: (i, 0))
    grid_spec = pltpu.PrefetchScalarGridSpec(
        num_scalar_prefetch=1, grid=(2, nh), in_specs=[whole, part, part, whole, whole], out_specs=[whole] * 4)
    return pl.pallas_call(body, grid_spec=grid_spec, out_shape=[SDS((rows, cols), F32)] * 4,
                          compiler_params=_cparams(("parallel", "parallel")), name=name)(c_arr, w, mine, theirs, m, v)


def _pack_small(pre_mix, post_mix, pre_ffn, post_ffn, rel_bias, b_forget, sinks):
    def at(row, v):
        return jnp.pad(v, ((row, 7 - row), (0, D_MODEL - v.shape[1])))
    return (at(0, pre_mix) + at(1, post_mix) + at(2, pre_ffn) + at(3, post_ffn)
            + at(4, rel_bias.reshape(1, N_BUCKETS * N_HEADS)) + at(5, jnp.concatenate([b_forget, sinks], axis=1)))


def _unpack_small(p):
    return dict(ln_pre_mix=p[0:1], ln_post_mix=p[1:2], ln_pre_ffn=p[2:3], ln_post_ffn=p[3:4],
                rel_bias=p[4, :N_BUCKETS * N_HEADS].reshape(N_BUCKETS, N_HEADS),
                b_forget=p[5:6, 0:N_HEADS], sinks=p[5:6, N_HEADS:2 * N_HEADS])


WEIGHTS = ("meta_tokens", "rel_bias", "ln_pre_mix", "ln_post_mix", "ln_pre_ffn", "ln_post_ffn",
           "w_in", "b_forget", "sinks", "w_out", "w_gate_up", "w_down")


def kernel(x, meta_tokens, rel_bias, ln_pre_mix, ln_post_mix, ln_pre_ffn, ln_post_ffn, w_in, b_forget, sinks, w_out, w_gate_up, w_down, loss_target, m_meta_tokens, m_rel_bias, m_ln_pre_mix, m_ln_post_mix, m_ln_pre_ffn, m_ln_post_ffn, m_w_in, m_b_forget, m_sinks, m_w_out, m_w_gate_up, m_w_down, v_meta_tokens, v_rel_bias, v_ln_pre_mix, v_ln_post_mix, v_ln_pre_ffn, v_ln_post_ffn, v_w_in, v_b_forget, v_sinks, v_w_out, v_w_gate_up, v_w_down):
    xi, yi, ci = lax.axis_index("x"), lax.axis_index("y"), lax.axis_index("c")
    chip = 2 * xi + yi
    c_arr = jnp.reshape(ci, (1,)).astype(jnp.int32)

    def halves(w, dtype):
        return w.astype(dtype).reshape(2, w.shape[0] // 2, w.shape[1])

    def with_own(gathered, shards):
        return [lax.dynamic_update_slice(got, own[None], (chip, 0, 0, 0)) for got, own in zip(gathered, shards)]

    shards = [halves(w_in[0], BF16), halves(meta_tokens, F32)]
    gw_in, g_meta = with_own(_run_alone(_gather_rider(shards, False), "gather_mixer_weights"), shards)
    out_shards = [halves(w_out[0], BF16)]
    ffn_shards = [halves(w_gate_up[0], BF16), halves(w_down[0], BF16)]

    def ffn_weights(carried):
        gw_gu, gw_dn = carried
        return gw_gu.reshape(D_MODEL, 2 * D_FF), gw_dn.reshape(D_FF, D_MODEL)

    early = {}

    def early_grads(dw_gu, dw_dn, dw_out):
        early["grads"] = [dw_out.reshape(4, 2, 128, D_MODEL), dw_gu.reshape(4, 2, 512, FF_T),
                          dw_dn.reshape(4, 2, 352, D_MODEL)]
        return _direct_rider(early["grads"])
    w_in_all = gw_in.reshape(4, D_MODEL, D_PROJ // 4).transpose(1, 0, 2).reshape(D_MODEL, D_PROJ)
    w_in_b = jnp.pad(w_in_all, ((0, 0), (0, D_PROJ_P - D_PROJ)))
    meta_all = g_meta.reshape(4, N_META, D_MODEL // 4).transpose(1, 0, 2).reshape(N_META, D_MODEL)

    loc = _local_step(x[0], loss_target[0], meta_all, rel_bias, ln_pre_mix, ln_post_mix, ln_pre_ffn, ln_post_ffn,
                      b_forget, sinks, w_in_b, _gather_rider(out_shards, True),
                      lambda carried: carried[0].reshape(D_MODEL, D_MODEL),
                      _gather_rider(ffn_shards, True, by_columns=(0,)), ffn_weights, early_grads)

    dw_in = loc["w_in"][:, :D_PROJ].reshape(D_MODEL, 4, D_PROJ // 4).transpose(1, 0, 2)
    dw_in = dw_in.reshape(4, 2, 512, D_PROJ // 4)
    small = jnp.concatenate(
        [_pack_small(loc["ln_pre_mix"], loc["ln_post_mix"], loc["ln_pre_ffn"], loc["ln_post_ffn"],
                     loc["rel_bias"], loc["b_forget"], loc["sinks"])
         + jnp.pad(loc["loss"].reshape(1, 1), ((LOSS_ROW, 7 - LOSS_ROW), (0, D_MODEL - 1))), loc["meta"]], axis=0)

    (got_in,) = _run_alone(_swap_rider([dw_in]), "swap_halves_late")
    part_in = _pair_sum(dw_in, got_in, c_arr, "pair_sum_late")
    send_sems, recv_sems, part_sent, landing, small_sent, everyone, token = _late_exchange_start(part_in, small)
    chip_arr = jnp.reshape(chip, (1,)).astype(jnp.int32)
    own_arr = jnp.stack([chip, ci]).astype(jnp.int32)
    grad, delta, new_m, new_v = {}, {}, {}, {}

    def update(names, mine):
        theirs = _join_halves(mine, "join_" + names[0])
        big = dict(w_in=(w_in, m_w_in, v_w_in), w_out=(w_out, m_w_out, v_w_out),
                   w_gate_up=(w_gate_up, m_w_gate_up, v_w_gate_up), w_down=(w_down, m_w_down, v_w_down))
        for name, g_mine, g_theirs in zip(names, mine, theirs):
            w, m, v = big[name]
            g, d, nm, nv = _adamw_halves(w[0], g_mine, g_theirs, m[0], v[0], c_arr, "adamw_" + name)
            grad[name], delta[name], new_m[name], new_v[name] = g[None], d[None], nm[None], nv[None]

    update(("w_out", "w_gate_up", "w_down"),
           [_owner_sum(g, l, own_arr, token, "owner_sum_%d" % a)
            for a, (g, l) in enumerate(zip(early["grads"], loc["landed"]))])
    part_back, landed_in, small_all = _late_exchange_wait(send_sems, recv_sems, part_sent, landing, small_sent,
                                                         everyone, new_v["w_down"])
    update(("w_in",), [_chip_sum(part_back, landed_in, chip_arr, "chip_sum_in")])
    me = 4 * xi + 2 * yi + ci
    small_sum = _device_sum(lax.dynamic_update_slice(small_all, small[None], (me, 0, 0)))
    g_meta_tokens = lax.dynamic_slice(small_sum[8:N_SMALL], (0, chip * (D_MODEL // 4)), (N_META, D_MODEL // 4))
    g_small = small_sum[0:8]
    grad.update(_unpack_small(g_small))
    grad.update(meta_tokens=g_meta_tokens)
    delta["meta_tokens"], new_m["meta_tokens"], new_v["meta_tokens"] = _adamw(
        meta_tokens, g_meta_tokens, m_meta_tokens, v_meta_tokens, "adamw_meta")
    d, nm, nv = _adamw(
        _pack_small(ln_pre_mix, ln_post_mix, ln_pre_ffn, ln_post_ffn, rel_bias, b_forget, sinks), g_small,
        _pack_small(m_ln_pre_mix, m_ln_post_mix, m_ln_pre_ffn, m_ln_post_ffn, m_rel_bias, m_b_forget, m_sinks),
        _pack_small(v_ln_pre_mix, v_ln_post_mix, v_ln_pre_ffn, v_ln_post_ffn, v_rel_bias, v_b_forget, v_sinks),
        "adamw_small")
    delta.update(_unpack_small(d))
    new_m.update(_unpack_small(nm))
    new_v.update(_unpack_small(nv))

    loss = small_sum[LOSS_ROW, 0]
    return (loss,loc["grad_x"][None], *[grad[k] for k in WEIGHTS], *[delta[k] for k in WEIGHTS],
            *[new_m[k] for k in WEIGHTS], *[new_v[k] for k in WEIGHTS])
```

```python
import math

import numpy as np
import jax
import jax.numpy as jnp
from jax import lax
from jax.experimental import pallas as pl
from jax.experimental.pallas import tpu as pltpu

F32 = jnp.float32
BF16 = jnp.bfloat16
MESH = pl.DeviceIdType.MESH
SDS = jax.ShapeDtypeStruct

D_MODEL = 1024
SEQ = 4096
N_META = 16
N_HEADS = 8
HALF = 64
D_FF = 2816
N_BUCKETS = 32
EPS = 1e-6
NEG = -1e30
SCALE = 0.125
PAD_ROWS = 112
ROW0 = PAD_ROWS + N_META
LP = ROW0 + SEQ
BLK = 128
NBLK = LP // BLK
TM = 384
NT = LP // TM
TM_PURE = LP // 2
TM_MID = LP // 4
TM_EPI = LP // 6
TN = 256
D_PROJ = 2312
D_PROJ_P = 2432
D_QKV = 2304
FF_T = 1408
VMEM_LIMIT = 56 * 1024 * 1024

ADAM_LR = 0.001
ADAM_B1 = 0.9
ADAM_B2 = 0.999
ADAM_EPS = 1e-08
ADAM_WD = 0.01
ADAM_STEP = 10

QA = 0
KA, VA = 4, 5
QB, KB, VB = 3, 5, 7
W2 = 256

NT_DIMS = (((1,), (1,)), ((), ()))
TN_DIMS = (((0,), (0,)), ((), ()))


def _cparams(sem):
    return pltpu.CompilerParams(dimension_semantics=sem, vmem_limit_bytes=VMEM_LIMIT)


def _t5_bucket_np(d):
    n = np.maximum(d, 0).astype(np.int32)
    nf = np.maximum(n, 1).astype(np.float32)
    large = 16 + (np.log(nf / np.float32(16)) / np.float32(math.log(8.0)) * np.float32(16)).astype(np.int32)
    large = np.minimum(large, N_BUCKETS - 1)
    return np.where(n < 16, n, large).astype(np.int32)


def _bucket_tables():
    qi = np.arange(BLK)[:, None]
    ki = np.arange(BLK)[None, :]
    return np.stack([_t5_bucket_np(qi - ki), _t5_bucket_np(qi - ki + BLK)])


def _rms(x):
    return lax.rsqrt(jnp.mean(x * x, axis=-1, keepdims=True) + EPS)


def _rms_bwd(n, r, gdy):
    return r * (gdy - n * jnp.mean(n * gdy, axis=-1, keepdims=True))


def _pre_mix(h0, gain, w_in_b):
    half = D_QKV // 2

    def body(h_ref, g_ref, w_ref, hn_ref, proj_ref, f_ref):
        x = h_ref[...]
        hn = (x * _rms(x) * g_ref[...]).astype(BF16)
        hn_ref[...] = hn
        proj_ref[:, :half] = jnp.dot(hn, w_ref[:, :half], preferred_element_type=F32).astype(BF16)
        p = jnp.dot(hn, w_ref[:, half:], preferred_element_type=F32)
        proj_ref[:, half:] = p[:, :half].astype(BF16)
        f_ref[...] = p[:, half:]

    return pl.pallas_call(
        body, grid=(LP // TM_MID,),
        in_specs=[pl.BlockSpec((TM_MID, D_MODEL), lambda i: (i, 0)),
                  pl.BlockSpec((1, D_MODEL), lambda i: (0, 0)),
                  pl.BlockSpec((D_MODEL, D_PROJ_P), lambda i: (0, 0))],
        out_specs=[pl.BlockSpec((TM_MID, D_MODEL), lambda i: (i, 0)),
                   pl.BlockSpec((TM_MID, D_QKV), lambda i: (i, 0)),
                   pl.BlockSpec((TM_MID, BLK), lambda i: (i, 0))],
        out_shape=[SDS((LP, D_MODEL), BF16), SDS((LP, D_QKV), BF16), SDS((LP, BLK), F32)],
        compiler_params=_cparams(("parallel",)), name="pre_mix")(h0, gain, w_in_b)


def _attn_out(o_a, o_b, w_out_b, h0, g_post, g_pre_ffn):
    def body(oa_ref, ob_ref, w_ref, h0_ref, gp_ref, gf_ref, a_ref, h1_ref, hn2_ref):
        a = (jnp.dot(oa_ref[...], w_ref[0:512, :], preferred_element_type=F32)
             + jnp.dot(ob_ref[...], w_ref[512:1024, :], preferred_element_type=F32))
        a_ref[...] = a
        h1 = h0_ref[...] + a * _rms(a) * gp_ref[...]
        h1_ref[...] = h1
        hn2_ref[...] = (h1 * _rms(h1) * gf_ref[...]).astype(BF16)

    row = lambda w: pl.BlockSpec((TM_EPI, w), lambda i: (i, 0))
    vec = pl.BlockSpec((1, D_MODEL), lambda i: (0, 0))
    return pl.pallas_call(
        body, grid=(LP // TM_EPI,),
        in_specs=[row(512), row(512), pl.BlockSpec((D_MODEL, D_MODEL), lambda i: (0, 0)), row(D_MODEL), vec, vec],
        out_specs=[row(D_MODEL), row(D_MODEL), row(D_MODEL)],
        out_shape=[SDS((LP, D_MODEL), F32), SDS((LP, D_MODEL), F32), SDS((LP, D_MODEL), BF16)],
        compiler_params=_cparams(("parallel",)), name="attn_out")(o_a, o_b, w_out_b, h0, g_post, g_pre_ffn)


def _ffn_up(hn2, w_gu_b):
    def body(x_ref, wg_ref, wu_ref, g_ref, u_ref, act_ref):
        x = x_ref[...]
        g = jnp.dot(x, wg_ref[...], preferred_element_type=F32)
        u = jnp.dot(x, wu_ref[...], preferred_element_type=F32)
        g_ref[...] = g.astype(BF16)
        u_ref[...] = u.astype(BF16)
        act_ref[...] = (g * (1.0 / (1.0 + jnp.exp(-g))) * u).astype(BF16)

    out = pl.BlockSpec((TM_PURE, TN), lambda i, j: (i, j))
    return pl.pallas_call(
        body, grid=(LP // TM_PURE, D_FF // TN),
        in_specs=[pl.BlockSpec((TM_PURE, D_MODEL), lambda i, j: (i, 0)),
                  pl.BlockSpec((D_MODEL, TN), lambda i, j: (0, j)),
                  pl.BlockSpec((D_MODEL, TN), lambda i, j: (0, j + D_FF // TN))],
        out_specs=[out, out, out],
        out_shape=[SDS((LP, D_FF), BF16)] * 3,
        compiler_params=_cparams(("parallel", "parallel")), name="ffn_up")(hn2, w_gu_b, w_gu_b)


def _ffn_down_loss(act, w_dn_b, h1, tgt, g_post_ffn):
    def body(act_ref, w_ref, h1_ref, t0_ref, t1_ref, t2_ref, g_ref, dff_ref, dy_ref, loss_ref, dg_ref):
        i = pl.program_id(0)
        target = jnp.concatenate([t0_ref[...], t1_ref[...], t2_ref[...]], axis=0)

        @pl.when(i == 0)
        def _():
            loss_ref[...] = jnp.zeros_like(loss_ref)
            dg_ref[...] = jnp.zeros_like(dg_ref)

        ff = jnp.dot(act_ref[...], w_ref[...], preferred_element_type=F32)
        r = _rms(ff)
        n = ff * r
        g = g_ref[...]
        y = h1_ref[...] + n * g
        rows = i * TM + lax.broadcasted_iota(jnp.int32, (TM, D_MODEL), 0)
        diff = jnp.where(rows >= ROW0, y - target, 0.0)
        loss_ref[...] += 0.5 * jnp.sum(diff * diff) / D_MODEL
        dy = diff / D_MODEL
        dy_ref[...] = dy
        dg_ref[...] += jnp.sum(dy * n, axis=0, keepdims=True)
        dff_ref[...] = _rms_bwd(n, r, g * dy).astype(BF16)

    row = pl.BlockSpec((TM, D_MODEL), lambda i: (i, 0))
    tblk = lambda j: pl.BlockSpec((BLK, D_MODEL), lambda i: (jnp.maximum(3 * i - 1 + j, 0), 0))
    return pl.pallas_call(
        body, grid=(NT,),
        in_specs=[pl.BlockSpec((TM, D_FF), lambda i: (i, 0)), pl.BlockSpec((D_FF, D_MODEL), lambda i: (0, 0)),
                  row, tblk(0), tblk(1), tblk(2), pl.BlockSpec((1, D_MODEL), lambda i: (0, 0))],
        out_specs=[row, row, pl.BlockSpec((8, BLK), lambda i: (0, 0)), pl.BlockSpec((1, D_MODEL), lambda i: (0, 0))],
        out_shape=[SDS((LP, D_MODEL), BF16), SDS((LP, D_MODEL), F32), SDS((8, BLK), F32), SDS((1, D_MODEL), F32)],
        compiler_params=_cparams(("arbitrary",)), name="ffn_down_loss")(act, w_dn_b, h1, tgt, tgt, tgt, g_post_ffn)


def _ffn_down_bwd(dff, w_dn_b, g, u):
    def body(d_ref, w_ref, g_ref, u_ref, dg_ref, du_ref):
        dact = lax.dot_general(d_ref[...], w_ref[...], NT_DIMS, preferred_element_type=F32)
        gg = g_ref[...].astype(F32)
        sig = 1.0 / (1.0 + jnp.exp(-gg))
        dg_ref[...] = (dact * u_ref[...].astype(F32) * sig * (1.0 + gg * (1.0 - sig))).astype(BF16)
        du_ref[...] = (dact * gg * sig).astype(BF16)

    blk = pl.BlockSpec((TM_PURE, TN), lambda i, j: (i, j))
    return pl.pallas_call(
        body, grid=(LP // TM_PURE, D_FF // TN),
        in_specs=[pl.BlockSpec((TM_PURE, D_MODEL), lambda i, j: (i, 0)),
                  pl.BlockSpec((TN, D_MODEL), lambda i, j: (j, 0)), blk, blk],
        out_specs=[blk, blk],
        out_shape=[SDS((LP, D_FF), BF16)] * 2,
        compiler_params=_cparams(("parallel", "parallel")), name="ffn_down_bwd")(dff, w_dn_b, g, u)


def _ffn_up_bwd(dg, du, w_gu_b, h1, a, dy, g_pre_ffn, g_post_mix):
    def body(dg_ref, du_ref, w_ref, h1_ref, a_ref, dy_ref, gf_ref, gp_ref,
             dh1_ref, da_ref, dgf_ref, dgp_ref, acc):
        i = pl.program_id(0)
        s = pl.program_id(1)

        @pl.when((i == 0) & (s == 0))
        def _():
            dgf_ref[...] = jnp.zeros_like(dgf_ref)
            dgp_ref[...] = jnp.zeros_like(dgp_ref)

        @pl.when(s == 0)
        def _():
            acc[...] = jnp.zeros_like(acc)

        @pl.when(s < 2)
        def _():
            acc[...] += lax.dot_general(dg_ref[...], w_ref[...], NT_DIMS, preferred_element_type=F32)

        @pl.when(s >= 2)
        def _():
            acc[...] += lax.dot_general(du_ref[...], w_ref[...], NT_DIMS, preferred_element_type=F32)

        @pl.when(s == 3)
        def _():
            dhn2 = acc[...]
            h1 = h1_ref[...]
            r2 = _rms(h1)
            n2 = h1 * r2
            dgf_ref[...] += jnp.sum(dhn2 * n2, axis=0, keepdims=True)
            dh1 = dy_ref[...] + _rms_bwd(n2, r2, gf_ref[...] * dhn2)
            dh1_ref[...] = dh1
            av = a_ref[...]
            ra = _rms(av)
            na = av * ra
            dgp_ref[...] += jnp.sum(dh1 * na, axis=0, keepdims=True)
            da_ref[...] = _rms_bwd(na, ra, gp_ref[...] * dh1).astype(BF16)

    row = pl.BlockSpec((TM_EPI, D_MODEL), lambda i, s: (i, 0))
    vec = pl.BlockSpec((1, D_MODEL), lambda i, s: (0, 0))
    return pl.pallas_call(
        body, grid=(LP // TM_EPI, 4),
        in_specs=[pl.BlockSpec((TM_EPI, FF_T), lambda i, s: (i, jnp.minimum(s, 1))),
                  pl.BlockSpec((TM_EPI, FF_T), lambda i, s: (i, jnp.maximum(s - 2, 0))),
                  pl.BlockSpec((D_MODEL, FF_T), lambda i, s: (0, s)),
                  row, row, row, vec, vec],
        out_specs=[row, row, vec, vec],
        out_shape=[SDS((LP, D_MODEL), F32), SDS((LP, D_MODEL), BF16), SDS((1, D_MODEL), F32), SDS((1, D_MODEL), F32)],
        scratch_shapes=[pltpu.VMEM((TM_EPI, D_MODEL), F32)],
        compiler_params=_cparams(("arbitrary", "arbitrary")), name="ffn_up_bwd",
    )(dg, du, w_gu_b, h1, a, dy, g_pre_ffn, g_post_mix)


def _attn_out_bwd(da, w_out_b):
    def body(d_ref, w_ref, o_ref):
        o_ref[...] = lax.dot_general(d_ref[...], w_ref[...], NT_DIMS, preferred_element_type=F32).astype(BF16)

    row = pl.BlockSpec((TM_PURE, D_MODEL), lambda i: (i, 0))
    return pl.pallas_call(
        body, grid=(LP // TM_PURE,),
        in_specs=[row, pl.BlockSpec((D_MODEL, D_MODEL), lambda i: (0, 0))],
        out_specs=row, out_shape=SDS((LP, D_MODEL), BF16),
        compiler_params=_cparams(("parallel",)), name="attn_out_bwd")(da, w_out_b)


def _pre_mix_bwd(dq_a, dq_b, dk_b, dv_b, dk_a, dv_a, df, w_in_b, h0, dh1, g_pre_mix):
    def body(qa_ref, qb_ref, kb_ref, vb_ref, ka_ref, va_ref, f_ref, w_ref, h0_ref, dh1_ref, g_ref,
             dproj_ref, dh0_ref, dg_ref):
        i = pl.program_id(0)

        @pl.when(i == 0)
        def _():
            dg_ref[...] = jnp.zeros_like(dg_ref)

        dproj = jnp.concatenate(
            [qa_ref[...], ka_ref[...].astype(BF16), va_ref[...].astype(BF16), (qb_ref[...] * SCALE).astype(BF16),
             kb_ref[...], vb_ref[...], f_ref[...].astype(BF16)], axis=1)
        dproj_ref[...] = dproj
        dhn = lax.dot_general(dproj, w_ref[...], NT_DIMS, preferred_element_type=F32)
        x = h0_ref[...]
        r = _rms(x)
        n = x * r
        dg_ref[...] += jnp.sum(dhn * n, axis=0, keepdims=True)
        dh0_ref[...] = dh1_ref[...] + _rms_bwd(n, r, g_ref[...] * dhn)

    row = lambda w: pl.BlockSpec((TM_EPI, w), lambda i: (i, 0))
    vec = pl.BlockSpec((1, D_MODEL), lambda i: (0, 0))
    return pl.pallas_call(
        body, grid=(LP // TM_EPI,),
        in_specs=[row(512), row(512), row(512), row(512), row(BLK), row(BLK), row(BLK),
                  pl.BlockSpec((D_MODEL, D_PROJ_P), lambda i: (0, 0)), row(D_MODEL), row(D_MODEL), vec],
        out_specs=[row(D_PROJ_P), row(D_MODEL), vec],
        out_shape=[SDS((LP, D_PROJ_P), BF16), SDS((LP, D_MODEL), F32), SDS((1, D_MODEL), F32)],
        compiler_params=_cparams(("arbitrary",)), name="pre_mix_bwd",
    )(dq_a, dq_b, dk_b, dv_b, dk_a, dv_a, df, w_in_b, h0, dh1, g_pre_mix)


def _mm_tn(parts, b, tm, name, out_dtype=F32):
    widths = [p.shape[1] for p in parts]
    m_total = sum(widths)
    n = b.shape[1]
    whole = len(parts) > 1
    n_k = LP // TM_MID
    assert (tm == m_total) if whole else (m_total % tm == 0)

    def body(*refs):
        a_refs, b_ref, o_ref, acc = refs[:-3], refs[-3], refs[-2], refs[-1]
        k = pl.program_id(1)

        @pl.when(k == 0)
        def _():
            acc[...] = jnp.zeros_like(acc)
        a = a_refs[0][...] if not whole else jnp.concatenate([r[...] for r in a_refs], axis=1)
        acc[...] += lax.dot_general(a, b_ref[...], TN_DIMS, preferred_element_type=F32)

        @pl.when(k == n_k - 1)
        def _():
            o_ref[...] = acc[...].astype(out_dtype)

    a_specs = ([pl.BlockSpec((TM_MID, w), lambda mi, k: (k, 0)) for w in widths] if whole
               else [pl.BlockSpec((TM_MID, tm), lambda mi, k: (k, mi))])
    return pl.pallas_call(
        body, grid=(m_total // tm, n_k),
        in_specs=a_specs + [pl.BlockSpec((TM_MID, n), lambda mi, k: (k, 0))],
        out_specs=pl.BlockSpec((tm, n), lambda mi, k: (mi, 0)),
        out_shape=SDS((m_total, n), out_dtype),
        scratch_shapes=[pltpu.VMEM((tm, n), F32)],
        compiler_params=_cparams(("parallel", "arbitrary")), name=name)(*parts, b)


def _dw_gate_up(hn2, dg, du):
    n_k = LP // TM_MID

    def body(a_ref, dg_ref, du_ref, o_ref, acc):
        s = pl.program_id(0)
        k = pl.program_id(1)

        @pl.when(k == 0)
        def _():
            acc[...] = jnp.zeros_like(acc)

        @pl.when(s < 2)
        def _():
            acc[...] += lax.dot_general(a_ref[...], dg_ref[...], TN_DIMS, preferred_element_type=F32)

        @pl.when(s >= 2)
        def _():
            acc[...] += lax.dot_general(a_ref[...], du_ref[...], TN_DIMS, preferred_element_type=F32)

        @pl.when(k == n_k - 1)
        def _():
            o_ref[0] = acc[...].astype(BF16)

    return pl.pallas_call(
        body, grid=(4, n_k),
        in_specs=[pl.BlockSpec((TM_MID, D_MODEL), lambda s, k: (k, 0)),
                  pl.BlockSpec((TM_MID, FF_T), lambda s, k: (k, jnp.minimum(s, 1))),
                  pl.BlockSpec((TM_MID, FF_T), lambda s, k: (k, jnp.maximum(s - 2, 0)))],
        out_specs=pl.BlockSpec((1, D_MODEL, FF_T), lambda s, k: (s, 0, 0)),
        out_shape=SDS((4, D_MODEL, FF_T), BF16),
        scratch_shapes=[pltpu.VMEM((D_MODEL, FF_T), F32)],
        compiler_params=_cparams(("parallel", "arbitrary")), name="dw_gate_up")(hn2, dg, du)


def _split3(x):
    hi = x.astype(BF16)
    r1 = x - hi.astype(F32)
    mid = r1.astype(BF16)
    lo = (r1 - mid.astype(F32)).astype(BF16)
    return hi, mid, lo


def _tri_matmul(tri, x):
    hi, mid, lo = _split3(x)
    dot = lambda t: jnp.dot(tri, t, preferred_element_type=F32)
    return dot(hi) + dot(mid) + dot(lo)


def _forget_cumsum(f, b_forget_p):
    def body(f_ref, b_ref, cum_ref, carry):
        i = pl.program_id(0)

        @pl.when(i == 0)
        def _():
            carry[...] = jnp.zeros_like(carry)

        z = f_ref[...] + b_ref[...]
        ls = jnp.minimum(z, 0.0) - jnp.log(1.0 + jnp.exp(-jnp.abs(z)))
        rows = i * TM + lax.broadcasted_iota(jnp.int32, (TM, BLK), 0)
        ls = jnp.where(rows >= PAD_ROWS, ls, 0.0)
        r = lax.broadcasted_iota(jnp.int32, (TM, TM), 0)
        c = lax.broadcasted_iota(jnp.int32, (TM, TM), 1)
        tri = (c <= r).astype(BF16)
        cum = _tri_matmul(tri, ls) + carry[...]
        cum_ref[...] = cum
        carry[...] = cum[TM - 1:TM, :]

    return pl.pallas_call(
        body, grid=(NT,),
        in_specs=[pl.BlockSpec((TM, BLK), lambda i: (i, 0)), pl.BlockSpec((1, BLK), lambda i: (0, 0))],
        out_specs=pl.BlockSpec((TM, BLK), lambda i: (i, 0)),
        out_shape=SDS((LP, BLK), F32),
        scratch_shapes=[pltpu.VMEM((1, BLK), F32)],
        compiler_params=_cparams(("arbitrary",)), name="forget_cumsum")(f, b_forget_p)


def _forget_cumsum_bwd(dcum, f, b_forget_p):
    def body(d_ref, f_ref, b_ref, df_ref, db_ref, carry):
        i = pl.program_id(0)

        @pl.when(i == 0)
        def _():
            carry[...] = jnp.zeros_like(carry)
            db_ref[...] = jnp.zeros_like(db_ref)

        blk = NT - 1 - i
        r = lax.broadcasted_iota(jnp.int32, (TM, TM), 0)
        c = lax.broadcasted_iota(jnp.int32, (TM, TM), 1)
        tri = (c >= r).astype(BF16)
        d = d_ref[...]
        dls = _tri_matmul(tri, d) + carry[...]
        carry[...] = dls[0:1, :]
        z = f_ref[...] + b_ref[...]
        rows = blk * TM + lax.broadcasted_iota(jnp.int32, (TM, BLK), 0)
        df = jnp.where(rows >= PAD_ROWS, dls / (1.0 + jnp.exp(z)), 0.0)
        df_ref[...] = df
        db_ref[...] += jnp.sum(df, axis=0, keepdims=True)

    rev = pl.BlockSpec((TM, BLK), lambda i: (NT - 1 - i, 0))
    vec = pl.BlockSpec((1, BLK), lambda i: (0, 0))
    return pl.pallas_call(
        body, grid=(NT,),
        in_specs=[rev, rev, vec],
        out_specs=[rev, vec],
        out_shape=[SDS((LP, BLK), F32), SDS((1, BLK), F32)],
        scratch_shapes=[pltpu.VMEM((1, BLK), F32)],
        compiler_params=_cparams(("arbitrary",)), name="forget_cumsum_bwd")(dcum, f, b_forget_p)


def _lane_half(rows):
    return lax.broadcasted_iota(jnp.int32, (rows, BLK), 1) // HALF


def _fox_valid(qi, kj):
    qrow = qi * TM + lax.broadcasted_iota(jnp.int32, (TM, TM), 0)
    krow = kj * TM + lax.broadcasted_iota(jnp.int32, (TM, TM), 1)
    return (krow <= qrow) & ((krow >= PAD_ROWS) | (qrow < PAD_ROWS))


class _Rider:
    def __init__(self, operands, out_shapes, sem_counts, first, middle, last):
        self.operands, self.out_shapes, self.sem_counts = list(operands), list(out_shapes), list(sem_counts)
        self.first, self.middle, self.last = first, middle, last

    def scratch(self):
        return [pltpu.SemaphoreType.DMA((k,)) for k in self.sem_counts]

    def split(self, refs, n_in, n_out, n_scratch):
        a, b = len(self.operands), len(self.out_shapes)
        ins, mine_in = refs[:n_in], refs[n_in:n_in + a]
        outs, mine_out = refs[n_in + a:n_in + a + n_out], refs[n_in + a + n_out:n_in + a + n_out + b]
        rest = refs[n_in + a + n_out + b:]
        return ins, outs, rest[:n_scratch], (mine_in, mine_out, rest[n_scratch:])

    def at_steps(self, mine, is_first, is_middle, is_last):
        for cond, fn in ((is_first, self.first), (is_middle, self.middle), (is_last, self.last)):
            pl.when(cond)(lambda fn=fn: fn(*mine))


HBM_SPEC = pl.BlockSpec(memory_space=pltpu.HBM)


N_AUG = 4
QCHUNKS = ((0, 128), (128, 128), (256, 128))
KSUB = 384
AHEAD = 5
AHEAD_BWD = 1


def _fox_prep(proj, cum):
    def body(q0_ref, q1_ref, k0_ref, k1_ref, v0_ref, v1_ref, c_ref, qa_ref, ka_ref, vt_ref):
        half = _lane_half(TM)
        lane = lax.broadcasted_iota(jnp.int32, (TM, BLK), 1)
        for pp in range(4):
            cols = slice(pp * BLK, (pp + 1) * BLK)
            q_ref, k_ref, v_ref = ((q0_ref, k0_ref, v0_ref), (q1_ref, k1_ref, v1_ref))[pp // 2]
            part = slice((pp % 2) * BLK, (pp % 2 + 1) * BLK)
            qs = q_ref[:, part].astype(F32) * SCALE
            kp = k_ref[:, part].astype(F32)
            vp = v_ref[:, part]
            vt_ref[cols, :] = vp.astype(F32).T.astype(BF16)
            for e in range(2):
                h = 2 * pp + e
                a = (1 - e) * HALF
                blk = slice(h * BLK, (h + 1) * BLK)
                hi, mid, lo = _split3(-c_ref[:, h:h + 1])
                q_aug = jnp.where(half == e, qs, jnp.where((lane >= a) & (lane < a + 3), 1.0, 0.0))
                k_aug = jnp.where(half == e, kp, jnp.where(
                    lane == a, hi.astype(F32), jnp.where(lane == a + 1, mid.astype(F32), jnp.where(
                        lane == a + 2, lo.astype(F32), jnp.where(lane == a + 3, 1.0, 0.0)))))
                qa_ref[blk, :] = q_aug.T.astype(BF16)
                ka_ref[:, blk] = k_aug.astype(BF16)

    row = lambda blk: pl.BlockSpec((TM, W2), lambda i: (i, blk))
    wide = pl.BlockSpec((TM, 1024), lambda i: (i, 0))
    return pl.pallas_call(
        body, grid=(NT,),
        in_specs=[row(QB), row(QB + 1), row(KB), row(KB + 1), row(VB), row(VB + 1),
                  pl.BlockSpec((TM, BLK), lambda i: (i, 0))],
        out_specs=[pl.BlockSpec((1024, TM), lambda i: (0, i)), wide, pl.BlockSpec((512, TM), lambda i: (0, i))],
        out_shape=[SDS((1024, LP), BF16), SDS((LP, 1024), BF16), SDS((512, LP), BF16)],
        compiler_params=_cparams(("parallel",)), name="fox_prep")(proj, proj, proj, proj, proj, proj, cum)


def _over_keys(reduce, x):
    slabs = x.reshape(x.shape[0] // HALF, HALF, x.shape[1])
    return reduce(reduce(slabs, axis=0), axis=0, keepdims=True)


def _fox_valid_t(qi, kj, c, r):
    krow = kj * TM + r * KSUB + lax.broadcasted_iota(jnp.int32, (KSUB, c[1]), 0)
    qrow = qi * TM + c[0] + lax.broadcasted_iota(jnp.int32, (KSUB, c[1]), 1)
    return (krow <= qrow) & ((krow >= PAD_ROWS) | (qrow < PAD_ROWS))


def _fox_fwd(q_aug, k_aug, v_t, rider):
    pairs = [(qi, kj) for qi in range(NT) for kj in range(qi + 1)]
    n_pairs = len(pairs)

    def body(qi_ref, kj_ref, *refs):
        (q_ref, k_ref, vt_ref), (o_ref, lse_ref), (m_s, l_s, acc_s), mine = rider.split(refs, 3, 2, 3)
        n = pl.program_id(0)
        qi = qi_ref[n]
        kj = kj_ref[n]
        rider.at_steps(mine, n == 0, n == n_pairs // 2, n == n_pairs - 1)

        @pl.when(kj == 0)
        def _():
            m_s[...] = jnp.full_like(m_s, NEG)
            l_s[...] = jnp.zeros_like(l_s)
            acc_s[...] = jnp.zeros_like(acc_s)

        def tile(masked):
            steps = [(h, c, r) for h in range(N_HEADS) for c in QCHUNKS for r in range(TM // KSUB)]

            def scores(h, c, r):
                blk = slice(h * BLK, (h + 1) * BLK)
                return jnp.dot(k_ref[r * KSUB:(r + 1) * KSUB, blk], q_ref[blk, c[0]:c[0] + c[1]],
                               preferred_element_type=F32)

            ahead = [scores(*st) for st in steps[:AHEAD]]
            for n, (h, c, r) in enumerate(steps):
                s_t = ahead.pop(0)
                if n + AHEAD < len(steps):
                    ahead.append(scores(*steps[n + AHEAD]))
                cs = slice(c[0], c[0] + c[1])
                if masked:
                    s_t = jnp.where(_fox_valid_t(qi, kj, c, r), s_t, NEG)
                m_prev = m_s[h, :, cs]
                m_new = jnp.maximum(m_prev, _over_keys(jnp.max, s_t))
                p_t = jnp.exp(s_t - m_new)
                alpha = jnp.exp(m_prev - m_new)
                l_s[h, :, cs] = alpha * l_s[h, :, cs] + _over_keys(jnp.sum, p_t)
                m_s[h, :, cs] = m_new
                vt = vt_ref[h * HALF:(h + 1) * HALF, r * KSUB:(r + 1) * KSUB]
                acc_s[h, :, cs] = acc_s[h, :, cs] * alpha + jnp.dot(vt, p_t.astype(BF16),
                                                                    preferred_element_type=F32)

        @pl.when((kj < qi) & (kj > 0))
        def _():
            tile(False)

        @pl.when((kj == qi) | (kj == 0))
        def _():
            tile(True)

        @pl.when(kj == qi)
        def _():
            for pp in range(4):
                both = jnp.concatenate([acc_s[2 * pp] * (1.0 / l_s[2 * pp]),
                                        acc_s[2 * pp + 1] * (1.0 / l_s[2 * pp + 1])], axis=0)
                o_ref[:, pp * BLK:(pp + 1) * BLK] = both.T.astype(BF16)
            for h in range(N_HEADS):
                lse_ref[h] = m_s[h] + jnp.log(l_s[h])

    grid_spec = pltpu.PrefetchScalarGridSpec(
        num_scalar_prefetch=2, grid=(n_pairs,),
        in_specs=[pl.BlockSpec((1024, TM), lambda n, qi, kj: (0, qi[n])),
                  pl.BlockSpec((TM, 1024), lambda n, qi, kj: (kj[n], 0)),
                  pl.BlockSpec((512, TM), lambda n, qi, kj: (0, kj[n]))] + [HBM_SPEC] * len(rider.operands),
        out_specs=[pl.BlockSpec((TM, 512), lambda n, qi, kj: (qi[n], 0)),
                   pl.BlockSpec((N_HEADS, 1, TM), lambda n, qi, kj: (0, 0, qi[n]))]
        + [HBM_SPEC] * len(rider.out_shapes),
        scratch_shapes=[pltpu.VMEM((N_HEADS, 1, TM), F32), pltpu.VMEM((N_HEADS, 1, TM), F32),
                        pltpu.VMEM((N_HEADS, HALF, TM), F32)] + rider.scratch())
    o_b, lse, *carried = pl.pallas_call(
        body, grid_spec=grid_spec,
        out_shape=[SDS((LP, 512), BF16), SDS((N_HEADS, 1, LP), F32)] + rider.out_shapes,
        compiler_params=_cparams(("arbitrary",)), name="fox_fwd",
    )(jnp.asarray([p[0] for p in pairs], jnp.int32), jnp.asarray([p[1] for p in pairs], jnp.int32),
      q_aug, k_aug, v_t, *rider.operands)
    return o_b, lse, carried


def _fox_bwd(proj, o_b, dmix, lse, ck_t, rider):
    pairs = [(kj, qi) for kj in range(NT) for qi in range(kj, NT)]
    n_pairs = len(pairs)

    def body(kj_ref, qi_ref, *refs):
        ((q0_ref, q1_ref, k0_ref, k1_ref, v0_ref, v1_ref, o_ref, do_ref, lse_ref, ck_ref),
         (dq_ref, dk_ref, dv_ref, dck_ref, dcq_ref), (dk_s, dv_s, dck_s), mine) = rider.split(refs, 10, 5, 3)
        n = pl.program_id(0)
        kj = kj_ref[n]
        qi = qi_ref[n]
        rider.at_steps(mine, n == 0, n == n_pairs // 2, n == n_pairs - 1)

        @pl.when(n == 0)
        def _():
            dq_ref[...] = jnp.zeros_like(dq_ref)
            dcq_ref[...] = jnp.zeros_like(dcq_ref)

        @pl.when(qi == kj)
        def _():
            dk_s[...] = jnp.zeros_like(dk_s)
            dv_s[...] = jnp.zeros_like(dv_s)
            dck_s[...] = jnp.zeros_like(dck_s)

        def tile(masked):
            valid = _fox_valid(qi, kj) if masked else None
            half = _lane_half(TM)
            q0 = pl.multiple_of(qi * TM, TM)
            lane = lax.broadcasted_iota(jnp.int32, (TM, BLK), 1)
            row_sums = jnp.zeros((TM, BLK), F32)
            pair_ops = {}

            def operands(pp):
                if pp not in pair_ops:
                    cols = slice(pp * BLK, (pp + 1) * BLK)
                    q_ref, k_ref, v_ref = ((q0_ref, k0_ref, v0_ref), (q1_ref, k1_ref, v1_ref))[pp // 2]
                    part = slice((pp % 2) * BLK, (pp % 2 + 1) * BLK)
                    pair_ops[pp] = ((q_ref[:, part].astype(F32) * SCALE).astype(BF16), k_ref[:, part],
                                    v_ref[:, part], do_ref[:, cols])
                return pair_ops[pp]

            def scores(pp, e):
                qs, kp, vp, dop = operands(pp)
                ke = jnp.where(half == e, kp, jnp.zeros_like(kp))
                ve = jnp.where(half == e, vp, jnp.zeros_like(vp))
                return (lax.dot_general(qs, ke, NT_DIMS, preferred_element_type=F32),
                        lax.dot_general(dop, ve, NT_DIMS, preferred_element_type=F32), ke)

            steps = [(pp, e) for pp in range(4) for e in range(2)]
            ahead = [scores(*st) for st in steps[:AHEAD_BWD]]
            for n, (pp, e) in enumerate(steps):
                raw, dp, ke = ahead.pop(0)
                if n + AHEAD_BWD < len(steps):
                    ahead.append(scores(*steps[n + AHEAD_BWD]))
                h = 2 * pp + e
                cols = slice(pp * BLK, (pp + 1) * BLK)
                qs, kp, vp, dop = operands(pp)
                if e == 0:
                    prod = dop.astype(F32) * o_ref[:, cols].astype(F32)
                    d0 = jnp.sum(jnp.where(half == 0, prod, 0.0), axis=1, keepdims=True)
                    d1 = jnp.sum(prod, axis=1, keepdims=True) - d0
                    dq = jnp.zeros((TM, BLK), F32)
                    dks, dvs = [], []
                t = raw - ck_ref[h] - lse_ref[h]
                if masked:
                    t = jnp.where(valid, t, NEG)
                p = jnp.exp(t)
                ds = p * (dp - (d0 if e == 0 else d1))
                dck_s[h] += jnp.sum(ds, axis=0, keepdims=True)
                row_sums = jnp.where(lane == h, jnp.sum(ds, axis=1, keepdims=True), row_sums)
                ds_b = ds.astype(BF16)
                dq = dq + jnp.dot(ds_b, ke, preferred_element_type=F32)
                dks.append(lax.dot_general(ds_b, qs, TN_DIMS, preferred_element_type=F32))
                dvs.append(lax.dot_general(p.astype(BF16), dop, TN_DIMS, preferred_element_type=F32))
                if e == 1:
                    dq_ref[pl.ds(q0, TM), cols] += dq
                    dk_s[pp] += jnp.where(half == 0, dks[0], dks[1])
                    dv_s[pp] += jnp.where(half == 0, dvs[0], dvs[1])
            dcq_ref[pl.ds(q0, TM), :] += row_sums

        @pl.when((qi > kj) & (kj > 0))
        def _():
            tile(False)

        @pl.when((qi == kj) | (kj == 0))
        def _():
            tile(True)

        @pl.when(qi == NT - 1)
        def _():
            for pp in range(4):
                cols = slice(pp * BLK, (pp + 1) * BLK)
                dk_ref[:, cols] = dk_s[pp].astype(BF16)
                dv_ref[:, cols] = dv_s[pp].astype(BF16)
            dck_ref[...] = dck_s[...]

    qrow = lambda blk, w=512: pl.BlockSpec((TM, w), lambda n, kj, qi: (qi[n], blk))
    krow = lambda blk: pl.BlockSpec((TM, W2), lambda n, kj, qi: (kj[n], blk))
    grid_spec = pltpu.PrefetchScalarGridSpec(
        num_scalar_prefetch=2, grid=(n_pairs,),
        in_specs=[qrow(QB, W2), qrow(QB + 1, W2), krow(KB), krow(KB + 1), krow(VB), krow(VB + 1), qrow(0), qrow(1),
                  pl.BlockSpec((N_HEADS, TM, 1), lambda n, kj, qi: (0, qi[n], 0)),
                  pl.BlockSpec((N_HEADS, 1, TM), lambda n, kj, qi: (0, 0, kj[n]))] + [HBM_SPEC] * len(rider.operands),
        out_specs=[pl.BlockSpec((LP, 512), lambda n, kj, qi: (0, 0)),
                   pl.BlockSpec((TM, 512), lambda n, kj, qi: (kj[n], 0)),
                   pl.BlockSpec((TM, 512), lambda n, kj, qi: (kj[n], 0)),
                   pl.BlockSpec((N_HEADS, 1, TM), lambda n, kj, qi: (0, 0, kj[n])),
                   pl.BlockSpec((LP, BLK), lambda n, kj, qi: (0, 0))] + [HBM_SPEC] * len(rider.out_shapes),
        scratch_shapes=[pltpu.VMEM((4, TM, BLK), F32), pltpu.VMEM((4, TM, BLK), F32),
                        pltpu.VMEM((N_HEADS, 1, TM), F32)] + rider.scratch())
    dq, dk, dv, dck, dcq, *carried = pl.pallas_call(
        body, grid_spec=grid_spec,
        out_shape=[SDS((LP, 512), F32), SDS((LP, 512), BF16), SDS((LP, 512), BF16), SDS((N_HEADS, 1, LP), F32),
                   SDS((LP, BLK), F32)] + rider.out_shapes,
        compiler_params=_cparams(("arbitrary",)), name="fox_bwd",
    )(jnp.asarray([p[0] for p in pairs], jnp.int32), jnp.asarray([p[1] for p in pairs], jnp.int32),
      proj, proj, proj, proj, proj, proj, o_b, dmix, lse, ck_t, *rider.operands)
    return dq, dk, dv, dck, dcq, carried


N_SEG = 3
N_KEY = N_SEG * BLK
GROUP = 4
QW = GROUP * BLK


def _bucket_tables_t():
    return np.ascontiguousarray(_bucket_tables().transpose(0, 2, 1))


def _stack_heads(ref, g, scale):
    half = _lane_half(BLK)
    out = []
    for pair in range(2):
        x = ref[:, (2 * g + pair) * BLK:(2 * g + pair + 1) * BLK].astype(F32) * scale
        swapped = pltpu.roll(x, HALF, 1)
        for e in range(2):
            out.append(jnp.where(half == g, x if e == g else swapped, 0.0).astype(BF16))
    return jnp.concatenate(out, axis=0)


def _unstack_heads(x_t, g, ref, scale):
    for pair in range(2):
        both = jnp.concatenate([x_t[:, (2 * pair) * BLK:(2 * pair + 1) * BLK],
                                x_t[:, (2 * pair + 1) * BLK:(2 * pair + 2) * BLK]], axis=0)
        ref[:, (2 * g + pair) * BLK:(2 * g + pair + 1) * BLK] = (both.T * scale).astype(ref.dtype)


def _swa_tables(tab_ref, sink_ref, bkt_ref, tbl, sink_row):
    kk = lax.broadcasted_iota(jnp.int32, (BLK, BLK), 0)
    qq = lax.broadcasted_iota(jnp.int32, (BLK, BLK), 1)
    neg = jnp.full((BLK, BLK), NEG, F32)
    lane = lax.broadcasted_iota(jnp.int32, (1, QW), 1) // BLK
    for g in range(2):
        row = jnp.zeros((1, QW), F32)
        for hh in range(GROUP):
            h = GROUP * g + hh
            cols = slice(hh * BLK, (hh + 1) * BLK)
            row = jnp.where(lane == hh, sink_ref[0, h], row)

            def step(b, carry, h=h):
                t = tab_ref[b, h]
                return jnp.where(bkt_ref[0] == b, t, carry[0]), jnp.where(bkt_ref[1] == b, t, carry[1])
            zero = jnp.zeros((BLK, BLK), F32)
            cur, prev = lax.fori_loop(0, N_BUCKETS, step, (zero, zero))
            far = jnp.full((BLK, BLK), tab_ref[N_BUCKETS - 1, h], F32)
            causal = jnp.where(kk <= qq, cur, neg)
            segments = [
                (neg, neg, jnp.where(kk >= PAD_ROWS, causal, neg)),
                (jnp.where(kk >= PAD_ROWS, prev, neg), neg, causal),
                (jnp.where(kk >= PAD_ROWS, far, neg), jnp.where(kk > qq, prev, neg), causal)]
            for case in range(3):
                for seg in range(N_SEG):
                    tbl[case, g, seg * BLK:(seg + 1) * BLK, cols] = segments[case][seg]
        sink_row[g] = row


def _swa_prep(proj):
    rows = LP // 3

    def body(k_ref, v_ref, kt_ref, vt_ref):
        kt_ref[...] = k_ref[...].astype(F32).T.astype(BF16)
        vt_ref[...] = v_ref[...].astype(F32).T.astype(BF16)

    col = pl.BlockSpec((BLK, rows), lambda i: (0, i))
    return pl.pallas_call(
        body, grid=(3,),
        in_specs=[pl.BlockSpec((rows, BLK), lambda i: (i, KA)), pl.BlockSpec((rows, BLK), lambda i: (i, VA))],
        out_specs=[col, col], out_shape=[SDS((BLK, LP), BF16)] * 2,
        compiler_params=_cparams(("parallel",)), name="swa_prep")(proj, proj)


def _seg_specs(rows_major, col):
    idx = [lambda i: 0, lambda i: jnp.maximum(i - 1, 0), lambda i: i]
    if rows_major:
        return [pl.BlockSpec((BLK, BLK), lambda i, f=f: (f(i), col)) for f in idx]
    return [pl.BlockSpec((BLK, BLK), lambda i, f=f: (0, f(i))) for f in idx]


def _swa_fwd(proj, vt_a, rel_bias, sinks, bkt_t, rider):
    def body(*refs):
        ((tab_ref, sink_ref, bkt_ref, q_ref, km_ref, kp_ref, kc_ref, vm_ref, vp_ref, vc_ref), (o_ref, lse_ref),
         (tbl, sink_row), mine) = rider.split(refs, 10, 2, 2)
        i = pl.program_id(0)
        rider.at_steps(mine, i == 0, i == NBLK // 2, i == NBLK - 1)

        @pl.when(i == 0)
        def _():
            _swa_tables(tab_ref, sink_ref, bkt_ref, tbl, sink_row)

        case = jnp.minimum(i, 2)
        k_cat = jnp.concatenate([km_ref[...], kp_ref[...], kc_ref[...]], axis=0)
        vt_cat = jnp.concatenate([vm_ref[...], vp_ref[...], vc_ref[...]], axis=1)
        raw = [lax.dot_general(k_cat, _stack_heads(q_ref, g, SCALE), NT_DIMS, preferred_element_type=F32)
               for g in range(2)]
        for g in range(2):
            s_t = raw[g] + tbl[case, g]
            sink = sink_row[g]
            m = jnp.maximum(_over_keys(jnp.max, s_t), sink)
            p_t = jnp.exp(s_t - m)
            l = _over_keys(jnp.sum, p_t) + jnp.exp(sink - m)
            o_t = jnp.dot(vt_cat[g * HALF:(g + 1) * HALF, :], p_t.astype(BF16), preferred_element_type=F32)
            _unstack_heads(o_t * (1.0 / l), g, o_ref, 1.0)
            lse = m + jnp.log(l)
            for hh in range(GROUP):
                lse_ref[GROUP * g + hh] = lse[:, hh * BLK:(hh + 1) * BLK]

    smem = pl.BlockSpec(memory_space=pltpu.SMEM)
    o_a, lse, *carried = pl.pallas_call(
        body, grid=(NBLK,),
        in_specs=[smem, smem, pl.BlockSpec((2, BLK, BLK), lambda i: (0, 0, 0)),
                  pl.BlockSpec((BLK, 512), lambda i: (i, QA))] + _seg_specs(True, KA) + _seg_specs(False, 0)
        + [HBM_SPEC] * len(rider.operands),
        out_specs=[pl.BlockSpec((BLK, 512), lambda i: (i, 0)),
                   pl.BlockSpec((N_HEADS, 1, BLK), lambda i: (0, 0, i))] + [HBM_SPEC] * len(rider.out_shapes),
        out_shape=[SDS((LP, 512), BF16), SDS((N_HEADS, 1, LP), F32)] + rider.out_shapes,
        scratch_shapes=[pltpu.VMEM((3, 2, N_KEY, QW), F32), pltpu.VMEM((2, 1, QW), F32)] + rider.scratch(),
        compiler_params=_cparams(("arbitrary",)), name="swa_fwd",
    )(rel_bias, sinks, bkt_t, proj, proj, proj, proj, vt_a, vt_a, vt_a, *rider.operands)
    return o_a, lse, carried


def _swa_bwd(proj, kt_a, o_a, dmix, lse, rel_bias, sinks, bkt_t):
    def body(tab_ref, sink_ref, bkt_ref, q_ref, km_ref, kp_ref, kc_ref, vm_ref, vp_ref, vc_ref,
             tm_ref, tp_ref, tc_ref, o_ref, do_ref, lse_ref,
             dq_ref, dk_ref, dv_ref, dbias_ref, dsink_ref, tbl, sink_row, acc, dsk):
        i = pl.program_id(0)

        @pl.when(i == 0)
        def _():
            _swa_tables(tab_ref, sink_ref, bkt_ref, tbl, sink_row)
            dk_ref[...] = jnp.zeros_like(dk_ref)
            dv_ref[...] = jnp.zeros_like(dv_ref)
            acc[...] = jnp.zeros_like(acc)
            dsk[...] = jnp.zeros_like(dsk)

        case = jnp.minimum(i, 2)
        first = jnp.full((BLK, QW), i, jnp.int32) == 1
        k_cat = jnp.concatenate([km_ref[...], kp_ref[...], kc_ref[...]], axis=0)
        v_cat = jnp.concatenate([vm_ref[...], vp_ref[...], vc_ref[...]], axis=0)
        kt_cat = jnp.concatenate([tm_ref[...], tp_ref[...], tc_ref[...]], axis=1)
        dk_cat = jnp.zeros((N_KEY, BLK), F32)
        dv_cat = jnp.zeros((N_KEY, BLK), F32)
        for g in range(2):
            d_parts = []
            for pair in range(2):
                cols = slice((2 * g + pair) * BLK, (2 * g + pair + 1) * BLK)
                prod_t = (do_ref[:, cols].astype(F32) * o_ref[:, cols].astype(F32)).T
                d_parts += [jnp.sum(prod_t[:HALF], axis=0, keepdims=True),
                            jnp.sum(prod_t[HALF:], axis=0, keepdims=True)]
            d_row = jnp.concatenate(d_parts, axis=1)
            lse_row = jnp.concatenate([lse_ref[GROUP * g + hh] for hh in range(GROUP)], axis=1)
            q_st = _stack_heads(q_ref, g, SCALE)
            do_st = _stack_heads(do_ref, g, 1.0)
            s_t = lax.dot_general(k_cat, q_st, NT_DIMS, preferred_element_type=F32) + tbl[case, g]
            p_t = jnp.exp(s_t - lse_row)
            dp_t = lax.dot_general(v_cat, do_st, NT_DIMS, preferred_element_type=F32)
            ds_t = p_t * (dp_t - d_row)
            dsk[g] += -jnp.exp(sink_row[g] - lse_row) * d_row
            acc[g, 0:BLK] += jnp.where(first, 0.0, ds_t[0:BLK])
            acc[g, BLK:2 * BLK] += jnp.where(first, ds_t[0:BLK], ds_t[BLK:2 * BLK])
            acc[g, 2 * BLK:N_KEY] += ds_t[2 * BLK:N_KEY]
            ds_b = ds_t.astype(BF16)
            dk_cat = dk_cat + jnp.dot(ds_b, q_st, preferred_element_type=F32)
            dv_cat = dv_cat + jnp.dot(p_t.astype(BF16), do_st, preferred_element_type=F32)
            dq_t = jnp.dot(kt_cat[g * HALF:(g + 1) * HALF, :], ds_b, preferred_element_type=F32)
            _unstack_heads(dq_t, g, dq_ref, SCALE)

        prev0 = pl.multiple_of(jnp.maximum(i - 1, 0) * BLK, BLK)
        cur0 = pl.multiple_of(i * BLK, BLK)
        for ref, cat in ((dk_ref, dk_cat), (dv_ref, dv_cat)):
            ref[0:BLK, :] += cat[0:BLK]
            ref[pl.ds(prev0, BLK), :] += cat[BLK:2 * BLK]
            ref[pl.ds(cur0, BLK), :] += cat[2 * BLK:N_KEY]

        @pl.when(i == NBLK - 1)
        def _():
            lane = lax.broadcasted_iota(jnp.int32, (1, BLK), 1)

            def per_bucket(b, carry):
                row = jnp.zeros((1, BLK), F32)
                for h in range(N_HEADS):
                    g, cols = h // GROUP, slice((h % GROUP) * BLK, (h % GROUP + 1) * BLK)
                    val = (jnp.sum(jnp.where(bkt_ref[0] == b, acc[g, 2 * BLK:N_KEY, cols], 0.0), keepdims=True)
                           + jnp.sum(jnp.where(bkt_ref[1] == b, acc[g, BLK:2 * BLK, cols], 0.0), keepdims=True))
                    row = jnp.where(lane == h, val, row)
                dbias_ref[pl.ds(b, 1), :] = row
                return carry

            lax.fori_loop(0, N_BUCKETS, per_bucket, 0)
            far = jnp.zeros((1, BLK), F32)
            dsr = jnp.zeros((1, BLK), F32)
            for h in range(N_HEADS):
                g, cols = h // GROUP, slice((h % GROUP) * BLK, (h % GROUP + 1) * BLK)
                far = jnp.where(lane == h, jnp.sum(acc[g, 0:BLK, cols], keepdims=True), far)
                dsr = jnp.where(lane == h, jnp.sum(dsk[g, :, cols], keepdims=True), dsr)
            dbias_ref[N_BUCKETS - 1:N_BUCKETS, :] += far
            dsink_ref[...] = dsr

    smem = pl.BlockSpec(memory_space=pltpu.SMEM)
    blk512 = lambda col: pl.BlockSpec((BLK, 512), lambda i: (i, col))
    full = lambda r, c: pl.BlockSpec((r, c), lambda i: (0, 0))
    return pl.pallas_call(
        body, grid=(NBLK,),
        in_specs=[smem, smem, pl.BlockSpec((2, BLK, BLK), lambda i: (0, 0, 0)), blk512(QA)]
        + _seg_specs(True, KA) + _seg_specs(True, VA) + _seg_specs(False, 0)
        + [blk512(0), blk512(0), pl.BlockSpec((N_HEADS, 1, BLK), lambda i: (0, 0, i))],
        out_specs=[blk512(0), full(LP, BLK), full(LP, BLK), full(N_BUCKETS, BLK), full(1, BLK)],
        out_shape=[SDS((LP, 512), BF16), SDS((LP, BLK), F32), SDS((LP, BLK), F32),
                   SDS((N_BUCKETS, BLK), F32), SDS((1, BLK), F32)],
        scratch_shapes=[pltpu.VMEM((3, 2, N_KEY, QW), F32), pltpu.VMEM((2, 1, QW), F32),
                        pltpu.VMEM((2, N_KEY, QW), F32), pltpu.VMEM((2, 1, QW), F32)],
        compiler_params=_cparams(("arbitrary",)), name="swa_bwd",
    )(rel_bias, sinks, bkt_t, proj, proj, proj, proj, proj, proj, proj, kt_a, kt_a, kt_a, o_a, dmix, lse)


def _local_step(x, tgt, meta, rel_bias, g_pre_mix, g_post_mix, g_pre_ffn, g_post_ffn, b_forget, sinks,
                w_in_b, out_rider, out_weight, ffn_rider, ffn_weights, early_grads):
    bkt_t = jnp.asarray(_bucket_tables_t())
    h0 = jnp.concatenate([jnp.zeros((PAD_ROWS, D_MODEL), F32), meta, x], axis=0)
    b_p = jnp.pad(b_forget, ((0, 0), (0, BLK - N_HEADS)))

    hn1, proj, f = _pre_mix(h0, g_pre_mix, w_in_b)
    kt_a, vt_a = _swa_prep(proj)
    o_a, lse_a, carried_out = _swa_fwd(proj, vt_a, rel_bias, sinks, bkt_t, out_rider)
    w_out_b = out_weight(carried_out)
    cum = _forget_cumsum(f, b_p)
    ck_t = cum[:, :N_HEADS].T.reshape(N_HEADS, 1, LP)
    q_aug, k_aug, v_t = _fox_prep(proj, cum)
    o_b, lse_row, carried = _fox_fwd(q_aug, k_aug, v_t, ffn_rider)
    lse_b = lse_row.reshape(N_HEADS, LP, 1)
    w_gu_b, w_dn_b = ffn_weights(carried)
    a, h1, hn2 = _attn_out(o_a, o_b, w_out_b, h0, g_post_mix, g_pre_ffn)
    g, u, act = _ffn_up(hn2, w_gu_b)
    dff, dy, loss_blk, dg_post_ffn = _ffn_down_loss(act, w_dn_b, h1, tgt, g_post_ffn)

    dw_dn = _mm_tn([act], dff, FF_T, "dw_down", BF16)
    dg, du = _ffn_down_bwd(dff, w_dn_b, g, u)
    dw_gu = _dw_gate_up(hn2, dg, du)
    dh1, da, dg_pre_ffn, dg_post_mix = _ffn_up_bwd(dg, du, w_gu_b, h1, a, dy, g_pre_ffn, g_post_mix)
    dw_out = _mm_tn([o_a, o_b], da, D_MODEL, "dw_out", BF16)
    dmix = _attn_out_bwd(da, w_out_b)
    dq_b, dk_b, dv_b, dck, dcq, landed = _fox_bwd(proj, o_b, dmix, lse_b, ck_t, early_grads(dw_gu, dw_dn, dw_out))
    dq_a, dk_a, dv_a, dbias, dsink = _swa_bwd(proj, kt_a, o_a, dmix, lse_a, rel_bias, sinks, bkt_t)
    dcum = dcq - jnp.pad(dck.reshape(N_HEADS, LP).T, ((0, 0), (0, BLK - N_HEADS)))
    df, db = _forget_cumsum_bwd(dcum, f, b_p)
    dproj, dh0, dg_pre_mix = _pre_mix_bwd(dq_a, dq_b, dk_b, dv_b, dk_a, dv_a, df, w_in_b, h0, dh1, g_pre_mix)
    dw_in = _mm_tn([hn1], dproj, D_MODEL, "dw_in", BF16)

    return dict(loss=loss_blk[0, 0], grad_x=dh0[ROW0:], meta=dh0[PAD_ROWS:ROW0],
                rel_bias=dbias[:, :N_HEADS], ln_pre_mix=dg_pre_mix, ln_post_mix=dg_post_mix,
                ln_pre_ffn=dg_pre_ffn, ln_post_ffn=dg_post_ffn, b_forget=db[:, :N_HEADS],
                sinks=dsink[:, :N_HEADS], w_in=dw_in, w_out=dw_out, w_gate_up=dw_gu, w_down=dw_dn,
                landed=landed)


N_SMALL = 24
LOSS_ROW = 6


def _place():
    x, y, c = lax.axis_index("x"), lax.axis_index("y"), lax.axis_index("c")
    return x, y, c, [(1 - x, y), (x, 1 - y), (1 - x, 1 - y)]


def _run_alone(rider, name):
    a, b = len(rider.operands), len(rider.out_shapes)

    def body(*refs):
        mine = (refs[:a], refs[a:a + b], refs[a + b:])
        rider.first(*mine)
        rider.middle(*mine)
        rider.last(*mine)

    return pl.pallas_call(body, in_specs=[HBM_SPEC] * a, out_specs=[HBM_SPEC] * b, out_shape=rider.out_shapes,
                          scratch_shapes=rider.scratch(), name=name)(*rider.operands)


def _gather_rider(shards, own_too, by_columns=()):
    n = len(shards)

    def slot(a, outs, chip, h):
        if a in by_columns:
            cols = shards[a].shape[2]
            return outs[a].at[h, :, pl.ds(pl.multiple_of(chip * cols, BLK), cols)]
        return outs[a].at[chip, h]

    def own_copies(ins, outs, sems):
        x, y, _, _ = _place()
        if not own_too:
            return []
        return [pltpu.make_async_copy(ins[a].at[h], slot(a, outs, 2 * x + y, h), sems[2].at[2 * a + h])
                for a in range(n) for h in range(2)]

    def copies(ins, outs, sems):
        send_sems, recv_sems = sems[:2]
        x, y, c, others = _place()
        chip = 2 * x + y
        sibling = (x, y, 1 - c)

        def rc(a, k, src, dst, to):
            return pltpu.make_async_remote_copy(src_ref=src, dst_ref=dst, send_sem=send_sems.at[6 * a + k],
                                                recv_sem=recv_sems.at[6 * a + k], device_id=to, device_id_type=MESH)

        pairs = [(a, j, ox, oy) for a in range(n) for j, (ox, oy) in enumerate(others)]
        there = lambda a, ox, oy, h: slot(a, outs, 2 * ox + oy, h)
        return dict(
            sent=lambda: [rc(a, j, ins[a].at[c], slot(a, outs, chip, c), (ox, oy, c)) for a, j, ox, oy in pairs],
            landed=lambda: [rc(a, j, there(a, ox, oy, c), there(a, ox, oy, c), sibling) for a, j, ox, oy in pairs],
            passed=lambda: [rc(a, 3 + j, there(a, ox, oy, c), there(a, ox, oy, c), sibling)
                            for a, j, ox, oy in pairs],
            arriving=lambda: [rc(a, 3 + j, there(a, ox, oy, 1 - c), there(a, ox, oy, 1 - c), sibling)
                              for a, j, ox, oy in pairs])

    def first(*mine):
        for cp in copies(*mine)["sent"]() + own_copies(*mine):
            cp.start()

    def middle(*mine):
        kinds = copies(*mine)
        for got, cp in zip(kinds["landed"](), kinds["passed"]()):
            got.wait_recv()
            cp.start()

    def last(*mine):
        kinds = copies(*mine)
        for cp in kinds["arriving"]():
            cp.wait_recv()
        for cp in kinds["sent"]() + kinds["passed"]():
            cp.wait_send()
        for cp in own_copies(*mine):
            cp.wait()

    shapes = [SDS((2, s.shape[1], 4 * s.shape[2]) if a in by_columns else (4,) + s.shape, s.dtype)
              for a, s in enumerate(shards)]
    return _Rider(shards, shapes, [6 * n, 6 * n] + [2 * n] * own_too, first, middle, last)


def _swap_rider(grads):
    n = len(grads)

    def copies(ins, outs, sems):
        x, y, c, _ = _place()
        return [pltpu.make_async_remote_copy(
            src_ref=ins[a].at[s, 1 - c], dst_ref=outs[a].at[s], send_sem=sems[0].at[4 * a + s],
            recv_sem=sems[1].at[4 * a + s], device_id=(x, y, 1 - c), device_id_type=MESH)
            for a in range(n) for s in range(4)]

    def first(*mine):
        for cp in copies(*mine):
            cp.start()

    def middle(*mine):
        pass

    def last(*mine):
        for cp in copies(*mine):
            cp.wait()

    return _Rider(grads, [SDS((4,) + g.shape[2:], g.dtype) for g in grads], [4 * n, 4 * n], first, middle, last)


def _pair_sum(g, got, c_arr, name):
    rh, cc = got.shape[1:]

    def body(c_ref, g_ref, p_ref, o_ref):
        o_ref[0] = (g_ref[0, 0].astype(F32) + p_ref[0].astype(F32)).astype(BF16)

    grid_spec = pltpu.PrefetchScalarGridSpec(
        num_scalar_prefetch=1, grid=(4,),
        in_specs=[pl.BlockSpec((1, 1, rh, cc), lambda s, c_ref: (s, c_ref[0], 0, 0)),
                  pl.BlockSpec((1, rh, cc), lambda s, c_ref: (s, 0, 0))],
        out_specs=pl.BlockSpec((1, rh, cc), lambda s, c_ref: (s, 0, 0)))
    return pl.pallas_call(body, grid_spec=grid_spec, out_shape=SDS((4, rh, cc), BF16),
                          compiler_params=_cparams(("parallel",)), name=name)(c_arr, g, got)


def _direct_rider(grads):
    n = len(grads)

    def copies(ins, outs, sems):
        x, y, c, others = _place()
        peers = [(x, y, 1 - c)] + [(ox, oy, c) for ox, oy in others] + [(ox, oy, 1 - c) for ox, oy in others]
        return [pltpu.make_async_remote_copy(
            src_ref=ins[a].at[2 * px + py, pc], dst_ref=outs[a].at[k], send_sem=sems[0].at[7 * a + k],
            recv_sem=sems[1].at[7 * a + k], device_id=(px, py, pc), device_id_type=MESH)
            for a in range(n) for k, (px, py, pc) in enumerate(peers)]

    def first(*mine):
        for cp in copies(*mine):
            cp.start()

    def middle(*mine):
        pass

    def last(*mine):
        for cp in copies(*mine):
            cp.wait()

    return _Rider(grads, [SDS((7,) + g.shape[2:], g.dtype) for g in grads], [7 * n, 7 * n], first, middle, last)


def _owner_sum(grads, landed, own_arr, after, name):
    rh, cc = landed.shape[1:]
    tr = rh // 2

    def body(own_ref, g_ref, p_ref, after_ref, o_ref):
        total = g_ref[0, 0].astype(F32)
        for k in range(7):
            total = total + p_ref[k].astype(F32)
        o_ref[...] = total

    grid_spec = pltpu.PrefetchScalarGridSpec(
        num_scalar_prefetch=1, grid=(2,),
        in_specs=[pl.BlockSpec((1, 1, tr, cc), lambda i, own: (own[0], own[1], i, 0)),
                  pl.BlockSpec((7, tr, cc), lambda i, own: (0, i, 0)), pl.BlockSpec(memory_space=pl.ANY)],
        out_specs=pl.BlockSpec((tr, cc), lambda i, own: (i, 0)))
    return pl.pallas_call(body, grid_spec=grid_spec, out_shape=SDS((rh, cc), F32),
                          compiler_params=_cparams(("parallel",)), name=name)(own_arr, grads, landed, after)


SEM_SPEC = pl.BlockSpec(memory_space=pltpu.SEMAPHORE)
N_LATE = 10


def _late_copies(part_ref, landed_ref, small_ref, all_ref, send_sems, recv_sems):
    x, y, c, others = _place()
    me = 4 * x + 2 * y + c
    peers = [(x, y, 1 - c)] + [(ox, oy, c) for ox, oy in others] + [(ox, oy, 1 - c) for ox, oy in others]
    big = [pltpu.make_async_remote_copy(
        src_ref=part_ref.at[2 * ox + oy], dst_ref=landed_ref.at[j], send_sem=send_sems.at[j], recv_sem=recv_sems.at[j],
        device_id=(ox, oy, c), device_id_type=MESH) for j, (ox, oy) in enumerate(others)]
    small = [pltpu.make_async_remote_copy(
        src_ref=small_ref, dst_ref=all_ref.at[me], send_sem=send_sems.at[3 + k], recv_sem=recv_sems.at[3 + k],
        device_id=peer, device_id_type=MESH) for k, peer in enumerate(peers)]
    return big + small


def _late_exchange_start(part, small):
    def body(part_ref, landed_ref, small_ref, all_ref, send_sems, recv_sems, part_o, landed_o, small_o, all_o, token):
        for cp in _late_copies(part_ref, landed_ref, small_ref, all_ref, send_sems, recv_sems):
            cp.start()
        token[...] = jnp.zeros_like(token)

    hbm = lambda a: pltpu.HBM(a.shape, a.dtype)
    landed = lax.empty((3,) + part.shape[1:], part.dtype)
    everyone = lax.empty((8,) + small.shape, small.dtype)
    operands = [pltpu.with_memory_space_constraint(a, pltpu.HBM) for a in (part, landed, small, everyone)]
    return pl.pallas_call(
        body, name="late_exchange_start",
        out_shape=(pltpu.SemaphoreType.DMA((N_LATE,)), pltpu.SemaphoreType.DMA((N_LATE,)),
                   hbm(part), hbm(landed), hbm(small), hbm(everyone), SDS((8, BLK), F32)),
        in_specs=[HBM_SPEC] * 4,
        out_specs=(SEM_SPEC, SEM_SPEC, HBM_SPEC, HBM_SPEC, HBM_SPEC, HBM_SPEC, pl.BlockSpec(memory_space=pltpu.VMEM)),
        input_output_aliases={0: 2, 1: 3, 2: 4, 3: 5},
        compiler_params=pltpu.CompilerParams(has_side_effects=pltpu.SideEffectType.DATAFLOW_SIDE_EFFECTING),
    )(*operands)


def _late_exchange_wait(send_sems, recv_sems, part, landed, small, everyone, after):
    def body(part_ref, landed_ref, small_ref, all_ref, send_sems, recv_sems, after_ref, part_o, landed_o, small_o, all_o):
        for cp in _late_copies(part_ref, landed_ref, small_ref, all_ref, send_sems, recv_sems):
            cp.wait_send()
            cp.wait_recv()

    hbm = lambda a: pltpu.HBM(a.shape, a.dtype)
    out = pl.pallas_call(
        body, name="late_exchange_wait",
        out_shape=(hbm(part), hbm(landed), hbm(small), hbm(everyone)),
        in_specs=[HBM_SPEC] * 4 + [SEM_SPEC, SEM_SPEC, pl.BlockSpec(memory_space=pl.ANY)],
        out_specs=(HBM_SPEC,) * 4, input_output_aliases={0: 0, 1: 1, 2: 2, 3: 3},
        compiler_params=pltpu.CompilerParams(has_side_effects=pltpu.SideEffectType.DATAFLOW_SIDE_EFFECTING),
    )(part, landed, small, everyone, send_sems, recv_sems, after)
    return out[0], out[1], out[3]


def _chip_sum(parts, landed, chip_arr, name):
    rh, cc = landed.shape[1:]
    tr = rh // 2

    def body(chip_ref, own_ref, p_ref, o_ref):
        o_ref[...] = ((own_ref[0].astype(F32) + p_ref[0].astype(F32)) + p_ref[1].astype(F32)) + p_ref[2].astype(F32)

    grid_spec = pltpu.PrefetchScalarGridSpec(
        num_scalar_prefetch=1, grid=(2,),
        in_specs=[pl.BlockSpec((1, tr, cc), lambda i, chip_ref: (chip_ref[0], i, 0)),
                  pl.BlockSpec((3, tr, cc), lambda i, chip_ref: (0, i, 0))],
        out_specs=pl.BlockSpec((tr, cc), lambda i, chip_ref: (i, 0)))
    return pl.pallas_call(body, grid_spec=grid_spec, out_shape=SDS((rh, cc), F32),
                          compiler_params=_cparams(("parallel",)), name=name)(chip_arr, parts, landed)


def _device_sum(p):
    def body(p_ref, o_ref):
        acc = p_ref[0]
        for k in range(1, 8):
            acc = acc + p_ref[k]
        o_ref[...] = acc

    return pl.pallas_call(body, out_shape=SDS(p.shape[1:], F32), name="small_sum")(p)


def _join_halves(halves, name):
    n = len(halves)

    def body(*refs):
        ins, outs = refs[:n], refs[n:2 * n]
        send_sems, recv_sems = refs[2 * n:]
        x, y, c, _ = _place()
        copies = [pltpu.make_async_remote_copy(
            src_ref=ins[a], dst_ref=outs[a], send_sem=send_sems.at[a], recv_sem=recv_sems.at[a],
            device_id=(x, y, 1 - c), device_id_type=MESH) for a in range(n)]
        for cp in copies:
            cp.start()
        for cp in copies:
            cp.wait()

    return pl.pallas_call(
        body, in_specs=[HBM_SPEC] * n, out_specs=[HBM_SPEC] * n,
        out_shape=[SDS(h.shape, h.dtype) for h in halves],
        scratch_shapes=[pltpu.SemaphoreType.DMA((n,)), pltpu.SemaphoreType.DMA((n,))],
        name=name)(*halves)


def _adamw(w, g, m, v, name, tr=None):
    rows, cols = w.shape
    tr = tr or rows
    assert rows % tr == 0

    def body(w_ref, g_ref, m_ref, v_ref, d_ref, nm_ref, nv_ref):
        gg = g_ref[...]
        nm = ADAM_B1 * m_ref[...] + (1.0 - ADAM_B1) * gg
        nv = ADAM_B2 * v_ref[...] + (1.0 - ADAM_B2) * (gg * gg)
        nm_ref[...] = nm
        nv_ref[...] = nv
        m_hat = nm / (1.0 - ADAM_B1 ** ADAM_STEP)
        v_hat = nv / (1.0 - ADAM_B2 ** ADAM_STEP)
        d_ref[...] = -ADAM_LR * (m_hat / (jnp.sqrt(v_hat) + ADAM_EPS) + ADAM_WD * w_ref[...])

    blk = pl.BlockSpec((tr, cols), lambda i: (i, 0))
    return pl.pallas_call(
        body, grid=(rows // tr,), in_specs=[blk] * 4, out_specs=[blk] * 3,
        out_shape=[SDS((rows, cols), F32)] * 3,
        compiler_params=_cparams(("parallel",)), name=name)(w, g, m, v)


def _adamw_halves(w, mine, theirs, m, v, c_arr, name):
    rows, cols = w.shape
    rh = rows // 2
    tr = rh if rh <= 352 else 256
    nh = rh // tr

    def body(c_ref, w_ref, mine_ref, theirs_ref, m_ref, v_ref, g_ref, d_ref, nm_ref, nv_ref):
        own = jnp.full((tr, cols), pl.program_id(0), jnp.int32) == c_ref[0]
        gg = jnp.where(own, mine_ref[...], theirs_ref[...])
        g_ref[...] = gg
        nm = ADAM_B1 * m_ref[...] + (1.0 - ADAM_B1) * gg
        nv = ADAM_B2 * v_ref[...] + (1.0 - ADAM_B2) * (gg * gg)
        nm_ref[...] = nm
        nv_ref[...] = nv
        m_hat = nm / (1.0 - ADAM_B1 ** ADAM_STEP)
        v_hat = nv / (1.0 - ADAM_B2 ** ADAM_STEP)
        d_ref[...] = -ADAM_LR * (m_hat / (jnp.sqrt(v_hat) + ADAM_EPS) + ADAM_WD * w_ref[...])

    whole = pl.BlockSpec((tr, cols), lambda hh, i, c_ref: (hh * nh + i, 0))
    part = pl.BlockSpec((tr, cols), lambda hh, i, c_ref: (i, 0))
    grid_spec = pltpu.PrefetchScalarGridSpec(
        num_scalar_prefetch=1, grid=(2, nh), in_specs=[whole, part, part, whole, whole], out_specs=[whole] * 4)
    return pl.pallas_call(body, grid_spec=grid_spec, out_shape=[SDS((rows, cols), F32)] * 4,
                          compiler_params=_cparams(("parallel", "parallel")), name=name)(c_arr, w, mine, theirs, m, v)


def _pack_small(pre_mix, post_mix, pre_ffn, post_ffn, rel_bias, b_forget, sinks):
    def at(row, v):
        return jnp.pad(v, ((row, 7 - row), (0, D_MODEL - v.shape[1])))
    return (at(0, pre_mix) + at(1, post_mix) + at(2, pre_ffn) + at(3, post_ffn)
            + at(4, rel_bias.reshape(1, N_BUCKETS * N_HEADS)) + at(5, jnp.concatenate([b_forget, sinks], axis=1)))


def _unpack_small(p):
    return dict(ln_pre_mix=p[0:1], ln_post_mix=p[1:2], ln_pre_ffn=p[2:3], ln_post_ffn=p[3:4],
                rel_bias=p[4, :N_BUCKETS * N_HEADS].reshape(N_BUCKETS, N_HEADS),
                b_forget=p[5:6, 0:N_HEADS], sinks=p[5:6, N_HEADS:2 * N_HEADS])


WEIGHTS = ("meta_tokens", "rel_bias", "ln_pre_mix", "ln_post_mix", "ln_pre_ffn", "ln_post_ffn",
           "w_in", "b_forget", "sinks", "w_out", "w_gate_up", "w_down")


def kernel(x, meta_tokens, rel_bias, ln_pre_mix, ln_post_mix, ln_pre_ffn, ln_post_ffn, w_in, b_forget, sinks, w_out, w_gate_up, w_down, loss_target, m_meta_tokens, m_rel_bias, m_ln_pre_mix, m_ln_post_mix, m_ln_pre_ffn, m_ln_post_ffn, m_w_in, m_b_forget, m_sinks, m_w_out, m_w_gate_up, m_w_down, v_meta_tokens, v_rel_bias, v_ln_pre_mix, v_ln_post_mix, v_ln_pre_ffn, v_ln_post_ffn, v_w_in, v_b_forget, v_sinks, v_w_out, v_w_gate_up, v_w_down):
    xi, yi, ci = lax.axis_index("x"), lax.axis_index("y"), lax.axis_index("c")
    chip = 2 * xi + yi
    c_arr = jnp.reshape(ci, (1,)).astype(jnp.int32)

    def halves(w, dtype):
        return w.astype(dtype).reshape(2, w.shape[0] // 2, w.shape[1])

    def with_own(gathered, shards):
        return [lax.dynamic_update_slice(got, own[None], (chip, 0, 0, 0)) for got, own in zip(gathered, shards)]

    shards = [halves(w_in[0], BF16), halves(meta_tokens, F32)]
    gw_in, g_meta = with_own(_run_alone(_gather_rider(shards, False), "gather_mixer_weights"), shards)
    out_shards = [halves(w_out[0], BF16)]
    ffn_shards = [halves(w_gate_up[0], BF16), halves(w_down[0], BF16)]

    def ffn_weights(carried):
        gw_gu, gw_dn = carried
        return gw_gu.reshape(D_MODEL, 2 * D_FF), gw_dn.reshape(D_FF, D_MODEL)

    early = {}

    def early_grads(dw_gu, dw_dn, dw_out):
        early["grads"] = [dw_out.reshape(4, 2, 128, D_MODEL), dw_gu.reshape(4, 2, 512, FF_T),
                          dw_dn.reshape(4, 2, 352, D_MODEL)]
        return _direct_rider(early["grads"])
    w_in_all = gw_in.reshape(4, D_MODEL, D_PROJ // 4).transpose(1, 0, 2).reshape(D_MODEL, D_PROJ)
    w_in_b = jnp.pad(w_in_all, ((0, 0), (0, D_PROJ_P - D_PROJ)))
    meta_all = g_meta.reshape(4, N_META, D_MODEL // 4).transpose(1, 0, 2).reshape(N_META, D_MODEL)

    loc = _local_step(x[0], loss_target[0], meta_all, rel_bias, ln_pre_mix, ln_post_mix, ln_pre_ffn, ln_post_ffn,
                      b_forget, sinks, w_in_b, _gather_rider(out_shards, True),
                      lambda carried: carried[0].reshape(D_MODEL, D_MODEL),
                      _gather_rider(ffn_shards, True, by_columns=(0,)), ffn_weights, early_grads)

    dw_in = loc["w_in"][:, :D_PROJ].reshape(D_MODEL, 4, D_PROJ // 4).transpose(1, 0, 2)
    dw_in = dw_in.reshape(4, 2, 512, D_PROJ // 4)
    small = jnp.concatenate(
        [_pack_small(loc["ln_pre_mix"], loc["ln_post_mix"], loc["ln_pre_ffn"], loc["ln_post_ffn"],
                     loc["rel_bias"], loc["b_forget"], loc["sinks"])
         + jnp.pad(loc["loss"].reshape(1, 1), ((LOSS_ROW, 7 - LOSS_ROW), (0, D_MODEL - 1))), loc["meta"]], axis=0)

    (got_in,) = _run_alone(_swap_rider([dw_in]), "swap_halves_late")
    part_in = _pair_sum(dw_in, got_in, c_arr, "pair_sum_late")
    send_sems, recv_sems, part_sent, landing, small_sent, everyone, token = _late_exchange_start(part_in, small)
    chip_arr = jnp.reshape(chip, (1,)).astype(jnp.int32)
    own_arr = jnp.stack([chip, ci]).astype(jnp.int32)
    grad, delta, new_m, new_v = {}, {}, {}, {}

    def update(names, mine):
        theirs = _join_halves(mine, "join_" + names[0])
        big = dict(w_in=(w_in, m_w_in, v_w_in), w_out=(w_out, m_w_out, v_w_out),
                   w_gate_up=(w_gate_up, m_w_gate_up, v_w_gate_up), w_down=(w_down, m_w_down, v_w_down))
        for name, g_mine, g_theirs in zip(names, mine, theirs):
            w, m, v = big[name]
            g, d, nm, nv = _adamw_halves(w[0], g_mine, g_theirs, m[0], v[0], c_arr, "adamw_" + name)
            grad[name], delta[name], new_m[name], new_v[name] = g[None], d[None], nm[None], nv[None]

    update(("w_out", "w_gate_up", "w_down"),
           [_owner_sum(g, l, own_arr, token, "owner_sum_%d" % a)
            for a, (g, l) in enumerate(zip(early["grads"], loc["landed"]))])
    part_back, landed_in, small_all = _late_exchange_wait(send_sems, recv_sems, part_sent, landing, small_sent,
                                                         everyone, new_v["w_down"])
    mine_in = _chip_sum(part_back, landed_in, chip_arr, "chip_sum_in")
    (theirs_in,) = _join_halves([mine_in], "join_w_in")
    g_w_in = jnp.where(ci == 0, jnp.concatenate([mine_in, theirs_in], axis=0),
                       jnp.concatenate([theirs_in, mine_in], axis=0))
    view = lambda a: jnp.transpose(a).reshape(D_PROJ // 4 * 8, BLK)
    back = lambda a: jnp.transpose(a.reshape(D_PROJ // 4, D_MODEL))[None]
    d, nm, nv = _adamw(view(w_in[0]), view(g_w_in), view(m_w_in[0]), view(v_w_in[0]), "adamw_w_in",
                       tr=D_PROJ // 4 * 4)
    grad["w_in"], delta["w_in"], new_m["w_in"], new_v["w_in"] = g_w_in[None], back(d), back(nm), back(nv)
    me = 4 * xi + 2 * yi + ci
    small_sum = _device_sum(lax.dynamic_update_slice(small_all, small[None], (me, 0, 0)))
    g_meta_tokens = lax.dynamic_slice(small_sum[8:N_SMALL], (0, chip * (D_MODEL // 4)), (N_META, D_MODEL // 4))
    g_small = small_sum[0:8]
    grad.update(_unpack_small(g_small))
    grad.update(meta_tokens=g_meta_tokens)
    delta["meta_tokens"], new_m["meta_tokens"], new_v["meta_tokens"] = _adamw(
        meta_tokens, g_meta_tokens, m_meta_tokens, v_meta_tokens, "adamw_meta")
    d, nm, nv = _adamw(
        _pack_small(ln_pre_mix, ln_post_mix, ln_pre_ffn, ln_post_ffn, rel_bias, b_forget, sinks), g_small,
        _pack_small(m_ln_pre_mix, m_ln_post_mix, m_ln_pre_ffn, m_ln_post_ffn, m_rel_bias, m_b_forget, m_sinks),
        _pack_small(v_ln_pre_mix, v_ln_post_mix, v_ln_pre_ffn, v_ln_post_ffn, v_rel_bias, v_b_forget, v_sinks),
        "adamw_small")
    delta.update(_unpack_small(d))
    new_m.update(_unpack_small(nm))
    new_v.update(_unpack_small(nv))

    loss = small_sum[LOSS_ROW, 0]
    return (loss,loc["grad_x"][None], *[grad[k] for k in WEIGHTS], *[delta[k] for k in WEIGHTS],
            *[new_m[k] for k in WEIGHTS], *[new_v[k] for k in WEIGHTS])
```

```python
import math

import numpy as np
import jax
import jax.numpy as jnp
from jax import lax
from jax.experimental import pallas as pl
from jax.experimental.pallas import tpu as pltpu

F32 = jnp.float32
BF16 = jnp.bfloat16
MESH = pl.DeviceIdType.MESH
SDS = jax.ShapeDtypeStruct

D_MODEL = 1024
SEQ = 4096
N_META = 16
N_HEADS = 8
HALF = 64
D_FF = 2816
N_BUCKETS = 32
EPS = 1e-6
NEG = -1e30
SCALE = 0.125
PAD_ROWS = 112
ROW0 = PAD_ROWS + N_META
LP = ROW0 + SEQ
BLK = 128
NBLK = LP // BLK
TM = 384
NT = LP // TM
TM_PURE = LP // 2
TM_MID = LP // 4
TM_EPI = LP // 6
TN = 256
D_PROJ = 2312
D_PROJ_P = 2432
D_QKV = 2304
FF_T = 1408
VMEM_LIMIT = 56 * 1024 * 1024

ADAM_LR = 0.001
ADAM_B1 = 0.9
ADAM_B2 = 0.999
ADAM_EPS = 1e-08
ADAM_WD = 0.01
ADAM_STEP = 10

QA = 0
KA, VA = 4, 5
QB, KB, VB = 3, 5, 7
W2 = 256

NT_DIMS = (((1,), (1,)), ((), ()))
TN_DIMS = (((0,), (0,)), ((), ()))


def _cparams(sem):
    return pltpu.CompilerParams(dimension_semantics=sem, vmem_limit_bytes=VMEM_LIMIT)


def _t5_bucket_np(d):
    n = np.maximum(d, 0).astype(np.int32)
    nf = np.maximum(n, 1).astype(np.float32)
    large = 16 + (np.log(nf / np.float32(16)) / np.float32(math.log(8.0)) * np.float32(16)).astype(np.int32)
    large = np.minimum(large, N_BUCKETS - 1)
    return np.where(n < 16, n, large).astype(np.int32)


def _bucket_tables():
    qi = np.arange(BLK)[:, None]
    ki = np.arange(BLK)[None, :]
    return np.stack([_t5_bucket_np(qi - ki), _t5_bucket_np(qi - ki + BLK)])


def _rms(x):
    return lax.rsqrt(jnp.mean(x * x, axis=-1, keepdims=True) + EPS)


def _rms_bwd(n, r, gdy):
    return r * (gdy - n * jnp.mean(n * gdy, axis=-1, keepdims=True))


def _pre_mix(h0, gain, w_in_b):
    half = D_QKV // 2

    def body(h_ref, g_ref, w_ref, hn_ref, proj_ref, f_ref):
        x = h_ref[...]
        hn = (x * _rms(x) * g_ref[...]).astype(BF16)
        hn_ref[...] = hn
        proj_ref[:, :half] = jnp.dot(hn, w_ref[:, :half], preferred_element_type=F32).astype(BF16)
        p = jnp.dot(hn, w_ref[:, half:], preferred_element_type=F32)
        proj_ref[:, half:] = p[:, :half].astype(BF16)
        f_ref[...] = p[:, half:]

    return pl.pallas_call(
        body, grid=(LP // TM_MID,),
        in_specs=[pl.BlockSpec((TM_MID, D_MODEL), lambda i: (i, 0)),
                  pl.BlockSpec((1, D_MODEL), lambda i: (0, 0)),
                  pl.BlockSpec((D_MODEL, D_PROJ_P), lambda i: (0, 0))],
        out_specs=[pl.BlockSpec((TM_MID, D_MODEL), lambda i: (i, 0)),
                   pl.BlockSpec((TM_MID, D_QKV), lambda i: (i, 0)),
                   pl.BlockSpec((TM_MID, BLK), lambda i: (i, 0))],
        out_shape=[SDS((LP, D_MODEL), BF16), SDS((LP, D_QKV), BF16), SDS((LP, BLK), F32)],
        compiler_params=_cparams(("parallel",)), name="pre_mix")(h0, gain, w_in_b)


def _attn_out(o_a, o_b, w_out_b, h0, g_post, g_pre_ffn):
    def body(oa_ref, ob_ref, w_ref, h0_ref, gp_ref, gf_ref, a_ref, h1_ref, hn2_ref):
        a = (jnp.dot(oa_ref[...], w_ref[0:512, :], preferred_element_type=F32)
             + jnp.dot(ob_ref[...], w_ref[512:1024, :], preferred_element_type=F32))
        a_ref[...] = a
        h1 = h0_ref[...] + a * _rms(a) * gp_ref[...]
        h1_ref[...] = h1
        hn2_ref[...] = (h1 * _rms(h1) * gf_ref[...]).astype(BF16)

    row = lambda w: pl.BlockSpec((TM_EPI, w), lambda i: (i, 0))
    vec = pl.BlockSpec((1, D_MODEL), lambda i: (0, 0))
    return pl.pallas_call(
        body, grid=(LP // TM_EPI,),
        in_specs=[row(512), row(512), pl.BlockSpec((D_MODEL, D_MODEL), lambda i: (0, 0)), row(D_MODEL), vec, vec],
        out_specs=[row(D_MODEL), row(D_MODEL), row(D_MODEL)],
        out_shape=[SDS((LP, D_MODEL), F32), SDS((LP, D_MODEL), F32), SDS((LP, D_MODEL), BF16)],
        compiler_params=_cparams(("parallel",)), name="attn_out")(o_a, o_b, w_out_b, h0, g_post, g_pre_ffn)


def _ffn_up(hn2, w_gu_b):
    def body(x_ref, wg_ref, wu_ref, g_ref, u_ref, act_ref):
        x = x_ref[...]
        g = jnp.dot(x, wg_ref[...], preferred_element_type=F32)
        u = jnp.dot(x, wu_ref[...], preferred_element_type=F32)
        g_ref[...] = g.astype(BF16)
        u_ref[...] = u.astype(BF16)
        act_ref[...] = (g * (1.0 / (1.0 + jnp.exp(-g))) * u).astype(BF16)

    out = pl.BlockSpec((TM_PURE, TN), lambda i, j: (i, j))
    return pl.pallas_call(
        body, grid=(LP // TM_PURE, D_FF // TN),
        in_specs=[pl.BlockSpec((TM_PURE, D_MODEL), lambda i, j: (i, 0)),
                  pl.BlockSpec((D_MODEL, TN), lambda i, j: (0, j)),
                  pl.BlockSpec((D_MODEL, TN), lambda i, j: (0, j + D_FF // TN))],
        out_specs=[out, out, out],
        out_shape=[SDS((LP, D_FF), BF16)] * 3,
        compiler_params=_cparams(("parallel", "parallel")), name="ffn_up")(hn2, w_gu_b, w_gu_b)


def _ffn_down_loss(act, w_dn_b, h1, tgt, g_post_ffn):
    def body(act_ref, w_ref, h1_ref, t0_ref, t1_ref, t2_ref, g_ref, dff_ref, dy_ref, loss_ref, dg_ref):
        i = pl.program_id(0)
        target = jnp.concatenate([t0_ref[...], t1_ref[...], t2_ref[...]], axis=0)

        @pl.when(i == 0)
        def _():
            loss_ref[...] = jnp.zeros_like(loss_ref)
            dg_ref[...] = jnp.zeros_like(dg_ref)

        ff = jnp.dot(act_ref[...], w_ref[...], preferred_element_type=F32)
        r = _rms(ff)
        n = ff * r
        g = g_ref[...]
        y = h1_ref[...] + n * g
        rows = i * TM + lax.broadcasted_iota(jnp.int32, (TM, D_MODEL), 0)
        diff = jnp.where(rows >= ROW0, y - target, 0.0)
        loss_ref[...] += 0.5 * jnp.sum(diff * diff) / D_MODEL
        dy = diff / D_MODEL
        dy_ref[...] = dy
        dg_ref[...] += jnp.sum(dy * n, axis=0, keepdims=True)
        dff_ref[...] = _rms_bwd(n, r, g * dy).astype(BF16)

    row = pl.BlockSpec((TM, D_MODEL), lambda i: (i, 0))
    tblk = lambda j: pl.BlockSpec((BLK, D_MODEL), lambda i: (jnp.maximum(3 * i - 1 + j, 0), 0))
    return pl.pallas_call(
        body, grid=(NT,),
        in_specs=[pl.BlockSpec((TM, D_FF), lambda i: (i, 0)), pl.BlockSpec((D_FF, D_MODEL), lambda i: (0, 0)),
                  row, tblk(0), tblk(1), tblk(2), pl.BlockSpec((1, D_MODEL), lambda i: (0, 0))],
        out_specs=[row, row, pl.BlockSpec((8, BLK), lambda i: (0, 0)), pl.BlockSpec((1, D_MODEL), lambda i: (0, 0))],
        out_shape=[SDS((LP, D_MODEL), BF16), SDS((LP, D_MODEL), F32), SDS((8, BLK), F32), SDS((1, D_MODEL), F32)],
        compiler_params=_cparams(("arbitrary",)), name="ffn_down_loss")(act, w_dn_b, h1, tgt, tgt, tgt, g_post_ffn)


def _ffn_down_bwd(dff, w_dn_b, g, u):
    def body(d_ref, w_ref, g_ref, u_ref, dg_ref, du_ref):
        dact = lax.dot_general(d_ref[...], w_ref[...], NT_DIMS, preferred_element_type=F32)
        gg = g_ref[...].astype(F32)
        sig = 1.0 / (1.0 + jnp.exp(-gg))
        dg_ref[...] = (dact * u_ref[...].astype(F32) * sig * (1.0 + gg * (1.0 - sig))).astype(BF16)
        du_ref[...] = (dact * gg * sig).astype(BF16)

    blk = pl.BlockSpec((TM_PURE, TN), lambda i, j: (i, j))
    return pl.pallas_call(
        body, grid=(LP // TM_PURE, D_FF // TN),
        in_specs=[pl.BlockSpec((TM_PURE, D_MODEL), lambda i, j: (i, 0)),
                  pl.BlockSpec((TN, D_MODEL), lambda i, j: (j, 0)), blk, blk],
        out_specs=[blk, blk],
        out_shape=[SDS((LP, D_FF), BF16)] * 2,
        compiler_params=_cparams(("parallel", "parallel")), name="ffn_down_bwd")(dff, w_dn_b, g, u)


def _ffn_up_bwd(dg, du, w_gu_b, h1, a, dy, g_pre_ffn, g_post_mix):
    def body(dg_ref, du_ref, w_ref, h1_ref, a_ref, dy_ref, gf_ref, gp_ref,
             dh1_ref, da_ref, dgf_ref, dgp_ref, acc):
        i = pl.program_id(0)
        s = pl.program_id(1)

        @pl.when((i == 0) & (s == 0))
        def _():
            dgf_ref[...] = jnp.zeros_like(dgf_ref)
            dgp_ref[...] = jnp.zeros_like(dgp_ref)

        @pl.when(s == 0)
        def _():
            acc[...] = jnp.zeros_like(acc)

        @pl.when(s < 2)
        def _():
            acc[...] += lax.dot_general(dg_ref[...], w_ref[...], NT_DIMS, preferred_element_type=F32)

        @pl.when(s >= 2)
        def _():
            acc[...] += lax.dot_general(du_ref[...], w_ref[...], NT_DIMS, preferred_element_type=F32)

        @pl.when(s == 3)
        def _():
            dhn2 = acc[...]
            h1 = h1_ref[...]
            r2 = _rms(h1)
            n2 = h1 * r2
            dgf_ref[...] += jnp.sum(dhn2 * n2, axis=0, keepdims=True)
            dh1 = dy_ref[...] + _rms_bwd(n2, r2, gf_ref[...] * dhn2)
            dh1_ref[...] = dh1
            av = a_ref[...]
            ra = _rms(av)
            na = av * ra
            dgp_ref[...] += jnp.sum(dh1 * na, axis=0, keepdims=True)
            da_ref[...] = _rms_bwd(na, ra, gp_ref[...] * dh1).astype(BF16)

    row = pl.BlockSpec((TM_EPI, D_MODEL), lambda i, s: (i, 0))
    vec = pl.BlockSpec((1, D_MODEL), lambda i, s: (0, 0))
    return pl.pallas_call(
        body, grid=(LP // TM_EPI, 4),
        in_specs=[pl.BlockSpec((TM_EPI, FF_T), lambda i, s: (i, jnp.minimum(s, 1))),
                  pl.BlockSpec((TM_EPI, FF_T), lambda i, s: (i, jnp.maximum(s - 2, 0))),
                  pl.BlockSpec((D_MODEL, FF_T), lambda i, s: (0, s)),
                  row, row, row, vec, vec],
        out_specs=[row, row, vec, vec],
        out_shape=[SDS((LP, D_MODEL), F32), SDS((LP, D_MODEL), BF16), SDS((1, D_MODEL), F32), SDS((1, D_MODEL), F32)],
        scratch_shapes=[pltpu.VMEM((TM_EPI, D_MODEL), F32)],
        compiler_params=_cparams(("arbitrary", "arbitrary")), name="ffn_up_bwd",
    )(dg, du, w_gu_b, h1, a, dy, g_pre_ffn, g_post_mix)


def _attn_out_bwd(da, w_out_b):
    def body(d_ref, w_ref, o_ref):
        o_ref[...] = lax.dot_general(d_ref[...], w_ref[...], NT_DIMS, preferred_element_type=F32).astype(BF16)

    row = pl.BlockSpec((TM_PURE, D_MODEL), lambda i: (i, 0))
    return pl.pallas_call(
        body, grid=(LP // TM_PURE,),
        in_specs=[row, pl.BlockSpec((D_MODEL, D_MODEL), lambda i: (0, 0))],
        out_specs=row, out_shape=SDS((LP, D_MODEL), BF16),
        compiler_params=_cparams(("parallel",)), name="attn_out_bwd")(da, w_out_b)


def _pre_mix_bwd(dq_a, dq_b, dk_b, dv_b, dk_a, dv_a, df, w_in_b, h0, dh1, g_pre_mix):
    def body(qa_ref, qb_ref, kb_ref, vb_ref, ka_ref, va_ref, f_ref, w_ref, h0_ref, dh1_ref, g_ref,
             dproj_ref, dh0_ref, dg_ref):
        i = pl.program_id(0)

        @pl.when(i == 0)
        def _():
            dg_ref[...] = jnp.zeros_like(dg_ref)

        dproj = jnp.concatenate(
            [qa_ref[...], ka_ref[...].astype(BF16), va_ref[...].astype(BF16), (qb_ref[...] * SCALE).astype(BF16),
             kb_ref[...], vb_ref[...], f_ref[...].astype(BF16)], axis=1)
        dproj_ref[...] = dproj
        dhn = lax.dot_general(dproj, w_ref[...], NT_DIMS, preferred_element_type=F32)
        x = h0_ref[...]
        r = _rms(x)
        n = x * r
        dg_ref[...] += jnp.sum(dhn * n, axis=0, keepdims=True)
        dh0_ref[...] = dh1_ref[...] + _rms_bwd(n, r, g_ref[...] * dhn)

    row = lambda w: pl.BlockSpec((TM_EPI, w), lambda i: (i, 0))
    vec = pl.BlockSpec((1, D_MODEL), lambda i: (0, 0))
    return pl.pallas_call(
        body, grid=(LP // TM_EPI,),
        in_specs=[row(512), row(512), row(512), row(512), row(BLK), row(BLK), row(BLK),
                  pl.BlockSpec((D_MODEL, D_PROJ_P), lambda i: (0, 0)), row(D_MODEL), row(D_MODEL), vec],
        out_specs=[row(D_PROJ_P), row(D_MODEL), vec],
        out_shape=[SDS((LP, D_PROJ_P), BF16), SDS((LP, D_MODEL), F32), SDS((1, D_MODEL), F32)],
        compiler_params=_cparams(("arbitrary",)), name="pre_mix_bwd",
    )(dq_a, dq_b, dk_b, dv_b, dk_a, dv_a, df, w_in_b, h0, dh1, g_pre_mix)


def _mm_tn(parts, b, tm, name, out_dtype=F32):
    widths = [p.shape[1] for p in parts]
    m_total = sum(widths)
    n = b.shape[1]
    whole = len(parts) > 1
    n_k = LP // TM_MID
    assert (tm == m_total) if whole else (m_total % tm == 0)

    def body(*refs):
        a_refs, b_ref, o_ref, acc = refs[:-3], refs[-3], refs[-2], refs[-1]
        k = pl.program_id(1)

        @pl.when(k == 0)
        def _():
            acc[...] = jnp.zeros_like(acc)
        a = a_refs[0][...] if not whole else jnp.concatenate([r[...] for r in a_refs], axis=1)
        acc[...] += lax.dot_general(a, b_ref[...], TN_DIMS, preferred_element_type=F32)

        @pl.when(k == n_k - 1)
        def _():
            o_ref[...] = acc[...].astype(out_dtype)

    a_specs = ([pl.BlockSpec((TM_MID, w), lambda mi, k: (k, 0)) for w in widths] if whole
               else [pl.BlockSpec((TM_MID, tm), lambda mi, k: (k, mi))])
    return pl.pallas_call(
        body, grid=(m_total // tm, n_k),
        in_specs=a_specs + [pl.BlockSpec((TM_MID, n), lambda mi, k: (k, 0))],
        out_specs=pl.BlockSpec((tm, n), lambda mi, k: (mi, 0)),
        out_shape=SDS((m_total, n), out_dtype),
        scratch_shapes=[pltpu.VMEM((tm, n), F32)],
        compiler_params=_cparams(("parallel", "arbitrary")), name=name)(*parts, b)


def _dw_gate_up(hn2, dg, du):
    n_k = LP // TM_MID

    def body(a_ref, dg_ref, du_ref, o_ref, acc):
        s = pl.program_id(0)
        k = pl.program_id(1)

        @pl.when(k == 0)
        def _():
            acc[...] = jnp.zeros_like(acc)

        @pl.when(s < 2)
        def _():
            acc[...] += lax.dot_general(a_ref[...], dg_ref[...], TN_DIMS, preferred_element_type=F32)

        @pl.when(s >= 2)
        def _():
            acc[...] += lax.dot_general(a_ref[...], du_ref[...], TN_DIMS, preferred_element_type=F32)

        @pl.when(k == n_k - 1)
        def _():
            o_ref[0] = acc[...].astype(BF16)

    return pl.pallas_call(
        body, grid=(4, n_k),
        in_specs=[pl.BlockSpec((TM_MID, D_MODEL), lambda s, k: (k, 0)),
                  pl.BlockSpec((TM_MID, FF_T), lambda s, k: (k, jnp.minimum(s, 1))),
                  pl.BlockSpec((TM_MID, FF_T), lambda s, k: (k, jnp.maximum(s - 2, 0)))],
        out_specs=pl.BlockSpec((1, D_MODEL, FF_T), lambda s, k: (s, 0, 0)),
        out_shape=SDS((4, D_MODEL, FF_T), BF16),
        scratch_shapes=[pltpu.VMEM((D_MODEL, FF_T), F32)],
        compiler_params=_cparams(("parallel", "arbitrary")), name="dw_gate_up")(hn2, dg, du)


def _split3(x):
    hi = x.astype(BF16)
    r1 = x - hi.astype(F32)
    mid = r1.astype(BF16)
    lo = (r1 - mid.astype(F32)).astype(BF16)
    return hi, mid, lo


def _tri_matmul(tri, x):
    hi, mid, lo = _split3(x)
    dot = lambda t: jnp.dot(tri, t, preferred_element_type=F32)
    return dot(hi) + dot(mid) + dot(lo)


def _forget_cumsum(f, b_forget_p):
    def body(f_ref, b_ref, cum_ref, carry):
        i = pl.program_id(0)

        @pl.when(i == 0)
        def _():
            carry[...] = jnp.zeros_like(carry)

        z = f_ref[...] + b_ref[...]
        ls = jnp.minimum(z, 0.0) - jnp.log(1.0 + jnp.exp(-jnp.abs(z)))
        rows = i * TM + lax.broadcasted_iota(jnp.int32, (TM, BLK), 0)
        ls = jnp.where(rows >= PAD_ROWS, ls, 0.0)
        r = lax.broadcasted_iota(jnp.int32, (TM, TM), 0)
        c = lax.broadcasted_iota(jnp.int32, (TM, TM), 1)
        tri = (c <= r).astype(BF16)
        cum = _tri_matmul(tri, ls) + carry[...]
        cum_ref[...] = cum
        carry[...] = cum[TM - 1:TM, :]

    return pl.pallas_call(
        body, grid=(NT,),
        in_specs=[pl.BlockSpec((TM, BLK), lambda i: (i, 0)), pl.BlockSpec((1, BLK), lambda i: (0, 0))],
        out_specs=pl.BlockSpec((TM, BLK), lambda i: (i, 0)),
        out_shape=SDS((LP, BLK), F32),
        scratch_shapes=[pltpu.VMEM((1, BLK), F32)],
        compiler_params=_cparams(("arbitrary",)), name="forget_cumsum")(f, b_forget_p)


def _forget_cumsum_bwd(dcum, f, b_forget_p):
    def body(d_ref, f_ref, b_ref, df_ref, db_ref, carry):
        i = pl.program_id(0)

        @pl.when(i == 0)
        def _():
            carry[...] = jnp.zeros_like(carry)
            db_ref[...] = jnp.zeros_like(db_ref)

        blk = NT - 1 - i
        r = lax.broadcasted_iota(jnp.int32, (TM, TM), 0)
        c = lax.broadcasted_iota(jnp.int32, (TM, TM), 1)
        tri = (c >= r).astype(BF16)
        d = d_ref[...]
        dls = _tri_matmul(tri, d) + carry[...]
        carry[...] = dls[0:1, :]
        z = f_ref[...] + b_ref[...]
        rows = blk * TM + lax.broadcasted_iota(jnp.int32, (TM, BLK), 0)
        df = jnp.where(rows >= PAD_ROWS, dls / (1.0 + jnp.exp(z)), 0.0)
        df_ref[...] = df
        db_ref[...] += jnp.sum(df, axis=0, keepdims=True)

    rev = pl.BlockSpec((TM, BLK), lambda i: (NT - 1 - i, 0))
    vec = pl.BlockSpec((1, BLK), lambda i: (0, 0))
    return pl.pallas_call(
        body, grid=(NT,),
        in_specs=[rev, rev, vec],
        out_specs=[rev, vec],
        out_shape=[SDS((LP, BLK), F32), SDS((1, BLK), F32)],
        scratch_shapes=[pltpu.VMEM((1, BLK), F32)],
        compiler_params=_cparams(("arbitrary",)), name="forget_cumsum_bwd")(dcum, f, b_forget_p)


def _lane_half(rows):
    return lax.broadcasted_iota(jnp.int32, (rows, BLK), 1) // HALF


def _fox_valid(qi, kj):
    qrow = qi * TM + lax.broadcasted_iota(jnp.int32, (TM, TM), 0)
    krow = kj * TM + lax.broadcasted_iota(jnp.int32, (TM, TM), 1)
    return (krow <= qrow) & ((krow >= PAD_ROWS) | (qrow < PAD_ROWS))


class _Rider:
    def __init__(self, operands, out_shapes, sem_counts, first, middle, last):
        self.operands, self.out_shapes, self.sem_counts = list(operands), list(out_shapes), list(sem_counts)
        self.first, self.middle, self.last = first, middle, last

    def scratch(self):
        return [pltpu.SemaphoreType.DMA((k,)) for k in self.sem_counts]

    def split(self, refs, n_in, n_out, n_scratch):
        a, b = len(self.operands), len(self.out_shapes)
        ins, mine_in = refs[:n_in], refs[n_in:n_in + a]
        outs, mine_out = refs[n_in + a:n_in + a + n_out], refs[n_in + a + n_out:n_in + a + n_out + b]
        rest = refs[n_in + a + n_out + b:]
        return ins, outs, rest[:n_scratch], (mine_in, mine_out, rest[n_scratch:])

    def at_steps(self, mine, is_first, is_middle, is_last):
        for cond, fn in ((is_first, self.first), (is_middle, self.middle), (is_last, self.last)):
            pl.when(cond)(lambda fn=fn: fn(*mine))


HBM_SPEC = pl.BlockSpec(memory_space=pltpu.HBM)


N_AUG = 4
QCHUNKS = ((0, 128), (128, 128), (256, 128))
KSUB = 384
AHEAD = 5
AHEAD_BWD = 1


def _fox_prep(proj, cum):
    def body(q0_ref, q1_ref, k0_ref, k1_ref, v0_ref, v1_ref, c_ref, qa_ref, ka_ref, vt_ref):
        half = _lane_half(TM)
        lane = lax.broadcasted_iota(jnp.int32, (TM, BLK), 1)
        for pp in range(4):
            cols = slice(pp * BLK, (pp + 1) * BLK)
            q_ref, k_ref, v_ref = ((q0_ref, k0_ref, v0_ref), (q1_ref, k1_ref, v1_ref))[pp // 2]
            part = slice((pp % 2) * BLK, (pp % 2 + 1) * BLK)
            qs = q_ref[:, part].astype(F32) * SCALE
            kp = k_ref[:, part].astype(F32)
            vp = v_ref[:, part]
            vt_ref[cols, :] = vp.astype(F32).T.astype(BF16)
            for e in range(2):
                h = 2 * pp + e
                a = (1 - e) * HALF
                blk = slice(h * BLK, (h + 1) * BLK)
                hi, mid, lo = _split3(-c_ref[:, h:h + 1])
                q_aug = jnp.where(half == e, qs, jnp.where((lane >= a) & (lane < a + 3), 1.0, 0.0))
                k_aug = jnp.where(half == e, kp, jnp.where(
                    lane == a, hi.astype(F32), jnp.where(lane == a + 1, mid.astype(F32), jnp.where(
                        lane == a + 2, lo.astype(F32), jnp.where(lane == a + 3, 1.0, 0.0)))))
                qa_ref[blk, :] = q_aug.T.astype(BF16)
                ka_ref[:, blk] = k_aug.astype(BF16)

    row = lambda blk: pl.BlockSpec((TM, W2), lambda i: (i, blk))
    wide = pl.BlockSpec((TM, 1024), lambda i: (i, 0))
    return pl.pallas_call(
        body, grid=(NT,),
        in_specs=[row(QB), row(QB + 1), row(KB), row(KB + 1), row(VB), row(VB + 1),
                  pl.BlockSpec((TM, BLK), lambda i: (i, 0))],
        out_specs=[pl.BlockSpec((1024, TM), lambda i: (0, i)), wide, pl.BlockSpec((512, TM), lambda i: (0, i))],
        out_shape=[SDS((1024, LP), BF16), SDS((LP, 1024), BF16), SDS((512, LP), BF16)],
        compiler_params=_cparams(("parallel",)), name="fox_prep")(proj, proj, proj, proj, proj, proj, cum)


def _over_keys(reduce, x):
    slabs = x.reshape(x.shape[0] // HALF, HALF, x.shape[1])
    return reduce(reduce(slabs, axis=0), axis=0, keepdims=True)


def _fox_valid_t(qi, kj, c, r):
    krow = kj * TM + r * KSUB + lax.broadcasted_iota(jnp.int32, (KSUB, c[1]), 0)
    qrow = qi * TM + c[0] + lax.broadcasted_iota(jnp.int32, (KSUB, c[1]), 1)
    return (krow <= qrow) & ((krow >= PAD_ROWS) | (qrow < PAD_ROWS))


def _fox_fwd(q_aug, k_aug, v_t, rider):
    pairs = [(qi, kj) for qi in range(NT) for kj in range(qi + 1)]
    n_pairs = len(pairs)

    def body(qi_ref, kj_ref, *refs):
        (q_ref, k_ref, vt_ref), (o_ref, lse_ref), (m_s, l_s, acc_s), mine = rider.split(refs, 3, 2, 3)
        n = pl.program_id(0)
        qi = qi_ref[n]
        kj = kj_ref[n]
        rider.at_steps(mine, n == 0, n == n_pairs // 2, n == n_pairs - 1)

        @pl.when(kj == 0)
        def _():
            m_s[...] = jnp.full_like(m_s, NEG)
            l_s[...] = jnp.zeros_like(l_s)
            acc_s[...] = jnp.zeros_like(acc_s)

        def tile(masked):
            steps = [(h, c, r) for h in range(N_HEADS) for c in QCHUNKS for r in range(TM // KSUB)]

            def scores(h, c, r):
                blk = slice(h * BLK, (h + 1) * BLK)
                return jnp.dot(k_ref[r * KSUB:(r + 1) * KSUB, blk], q_ref[blk, c[0]:c[0] + c[1]],
                               preferred_element_type=F32)

            ahead = [scores(*st) for st in steps[:AHEAD]]
            for n, (h, c, r) in enumerate(steps):
                s_t = ahead.pop(0)
                if n + AHEAD < len(steps):
                    ahead.append(scores(*steps[n + AHEAD]))
                cs = slice(c[0], c[0] + c[1])
                if masked:
                    s_t = jnp.where(_fox_valid_t(qi, kj, c, r), s_t, NEG)
                m_prev = m_s[h, :, cs]
                m_new = jnp.maximum(m_prev, _over_keys(jnp.max, s_t))
                p_t = jnp.exp(s_t - m_new)
                alpha = jnp.exp(m_prev - m_new)
                l_s[h, :, cs] = alpha * l_s[h, :, cs] + _over_keys(jnp.sum, p_t)
                m_s[h, :, cs] = m_new
                vt = vt_ref[h * HALF:(h + 1) * HALF, r * KSUB:(r + 1) * KSUB]
                acc_s[h, :, cs] = acc_s[h, :, cs] * alpha + jnp.dot(vt, p_t.astype(BF16),
                                                                    preferred_element_type=F32)

        @pl.when((kj < qi) & (kj > 0))
        def _():
            tile(False)

        @pl.when((kj == qi) | (kj == 0))
        def _():
            tile(True)

        @pl.when(kj == qi)
        def _():
            for pp in range(4):
                both = jnp.concatenate([acc_s[2 * pp] * (1.0 / l_s[2 * pp]),
                                        acc_s[2 * pp + 1] * (1.0 / l_s[2 * pp + 1])], axis=0)
                o_ref[:, pp * BLK:(pp + 1) * BLK] = both.T.astype(BF16)
            for h in range(N_HEADS):
                lse_ref[h] = m_s[h] + jnp.log(l_s[h])

    grid_spec = pltpu.PrefetchScalarGridSpec(
        num_scalar_prefetch=2, grid=(n_pairs,),
        in_specs=[pl.BlockSpec((1024, TM), lambda n, qi, kj: (0, qi[n])),
                  pl.BlockSpec((TM, 1024), lambda n, qi, kj: (kj[n], 0)),
                  pl.BlockSpec((512, TM), lambda n, qi, kj: (0, kj[n]))] + [HBM_SPEC] * len(rider.operands),
        out_specs=[pl.BlockSpec((TM, 512), lambda n, qi, kj: (qi[n], 0)),
                   pl.BlockSpec((N_HEADS, 1, TM), lambda n, qi, kj: (0, 0, qi[n]))]
        + [HBM_SPEC] * len(rider.out_shapes),
        scratch_shapes=[pltpu.VMEM((N_HEADS, 1, TM), F32), pltpu.VMEM((N_HEADS, 1, TM), F32),
                        pltpu.VMEM((N_HEADS, HALF, TM), F32)] + rider.scratch())
    o_b, lse, *carried = pl.pallas_call(
        body, grid_spec=grid_spec,
        out_shape=[SDS((LP, 512), BF16), SDS((N_HEADS, 1, LP), F32)] + rider.out_shapes,
        compiler_params=_cparams(("arbitrary",)), name="fox_fwd",
    )(jnp.asarray([p[0] for p in pairs], jnp.int32), jnp.asarray([p[1] for p in pairs], jnp.int32),
      q_aug, k_aug, v_t, *rider.operands)
    return o_b, lse, carried


def _fox_bwd(proj, o_b, dmix, lse, ck_t, rider):
    pairs = [(kj, qi) for kj in range(NT) for qi in range(kj, NT)]
    n_pairs = len(pairs)

    def body(kj_ref, qi_ref, *refs):
        ((q0_ref, q1_ref, k0_ref, k1_ref, v0_ref, v1_ref, o_ref, do_ref, lse_ref, ck_ref),
         (dq_ref, dk_ref, dv_ref, dck_ref, dcq_ref), (dk_s, dv_s, dck_s), mine) = rider.split(refs, 10, 5, 3)
        n = pl.program_id(0)
        kj = kj_ref[n]
        qi = qi_ref[n]
        rider.at_steps(mine, n == 0, n == n_pairs // 2, n == n_pairs - 1)

        @pl.when(n == 0)
        def _():
            dq_ref[...] = jnp.zeros_like(dq_ref)
            dcq_ref[...] = jnp.zeros_like(dcq_ref)

        @pl.when(qi == kj)
        def _():
            dk_s[...] = jnp.zeros_like(dk_s)
            dv_s[...] = jnp.zeros_like(dv_s)
            dck_s[...] = jnp.zeros_like(dck_s)

        def tile(masked):
            valid = _fox_valid(qi, kj) if masked else None
            half = _lane_half(TM)
            q0 = pl.multiple_of(qi * TM, TM)
            lane = lax.broadcasted_iota(jnp.int32, (TM, BLK), 1)
            row_sums = jnp.zeros((TM, BLK), F32)
            pair_ops = {}

            def operands(pp):
                if pp not in pair_ops:
                    cols = slice(pp * BLK, (pp + 1) * BLK)
                    q_ref, k_ref, v_ref = ((q0_ref, k0_ref, v0_ref), (q1_ref, k1_ref, v1_ref))[pp // 2]
                    part = slice((pp % 2) * BLK, (pp % 2 + 1) * BLK)
                    pair_ops[pp] = ((q_ref[:, part].astype(F32) * SCALE).astype(BF16), k_ref[:, part],
                                    v_ref[:, part], do_ref[:, cols])
                return pair_ops[pp]

            def scores(pp, e):
                qs, kp, vp, dop = operands(pp)
                ke = jnp.where(half == e, kp, jnp.zeros_like(kp))
                ve = jnp.where(half == e, vp, jnp.zeros_like(vp))
                return (lax.dot_general(qs, ke, NT_DIMS, preferred_element_type=F32),
                        lax.dot_general(dop, ve, NT_DIMS, preferred_element_type=F32), ke)

            steps = [(pp, e) for pp in range(4) for e in range(2)]
            ahead = [scores(*st) for st in steps[:AHEAD_BWD]]
            for n, (pp, e) in enumerate(steps):
                raw, dp, ke = ahead.pop(0)
                if n + AHEAD_BWD < len(steps):
                    ahead.append(scores(*steps[n + AHEAD_BWD]))
                h = 2 * pp + e
                cols = slice(pp * BLK, (pp + 1) * BLK)
                qs, kp, vp, dop = operands(pp)
                if e == 0:
                    prod = dop.astype(F32) * o_ref[:, cols].astype(F32)
                    d0 = jnp.sum(jnp.where(half == 0, prod, 0.0), axis=1, keepdims=True)
                    d1 = jnp.sum(prod, axis=1, keepdims=True) - d0
                    dq = jnp.zeros((TM, BLK), F32)
                    dks, dvs = [], []
                t = raw - ck_ref[h] - lse_ref[h]
                if masked:
                    t = jnp.where(valid, t, NEG)
                p = jnp.exp(t)
                ds = p * (dp - (d0 if e == 0 else d1))
                dck_s[h] += jnp.sum(ds, axis=0, keepdims=True)
                row_sums = jnp.where(lane == h, jnp.sum(ds, axis=1, keepdims=True), row_sums)
                ds_b = ds.astype(BF16)
                dq = dq + jnp.dot(ds_b, ke, preferred_element_type=F32)
                dks.append(lax.dot_general(ds_b, qs, TN_DIMS, preferred_element_type=F32))
                dvs.append(lax.dot_general(p.astype(BF16), dop, TN_DIMS, preferred_element_type=F32))
                if e == 1:
                    dq_ref[pl.ds(q0, TM), cols] += dq
                    dk_s[pp] += jnp.where(half == 0, dks[0], dks[1])
                    dv_s[pp] += jnp.where(half == 0, dvs[0], dvs[1])
            dcq_ref[pl.ds(q0, TM), :] += row_sums

        @pl.when((qi > kj) & (kj > 0))
        def _():
            tile(False)

        @pl.when((qi == kj) | (kj == 0))
        def _():
            tile(True)

        @pl.when(qi == NT - 1)
        def _():
            for pp in range(4):
                cols = slice(pp * BLK, (pp + 1) * BLK)
                dk_ref[:, cols] = dk_s[pp].astype(BF16)
                dv_ref[:, cols] = dv_s[pp].astype(BF16)
            dck_ref[...] = dck_s[...]

    qrow = lambda blk, w=512: pl.BlockSpec((TM, w), lambda n, kj, qi: (qi[n], blk))
    krow = lambda blk: pl.BlockSpec((TM, W2), lambda n, kj, qi: (kj[n], blk))
    grid_spec = pltpu.PrefetchScalarGridSpec(
        num_scalar_prefetch=2, grid=(n_pairs,),
        in_specs=[qrow(QB, W2), qrow(QB + 1, W2), krow(KB), krow(KB + 1), krow(VB), krow(VB + 1), qrow(0), qrow(1),
                  pl.BlockSpec((N_HEADS, TM, 1), lambda n, kj, qi: (0, qi[n], 0)),
                  pl.BlockSpec((N_HEADS, 1, TM), lambda n, kj, qi: (0, 0, kj[n]))] + [HBM_SPEC] * len(rider.operands),
        out_specs=[pl.BlockSpec((LP, 512), lambda n, kj, qi: (0, 0)),
                   pl.BlockSpec((TM, 512), lambda n, kj, qi: (kj[n], 0)),
                   pl.BlockSpec((TM, 512), lambda n, kj, qi: (kj[n], 0)),
                   pl.BlockSpec((N_HEADS, 1, TM), lambda n, kj, qi: (0, 0, kj[n])),
                   pl.BlockSpec((LP, BLK), lambda n, kj, qi: (0, 0))] + [HBM_SPEC] * len(rider.out_shapes),
        scratch_shapes=[pltpu.VMEM((4, TM, BLK), F32), pltpu.VMEM((4, TM, BLK), F32),
                        pltpu.VMEM((N_HEADS, 1, TM), F32)] + rider.scratch())
    dq, dk, dv, dck, dcq, *carried = pl.pallas_call(
        body, grid_spec=grid_spec,
        out_shape=[SDS((LP, 512), F32), SDS((LP, 512), BF16), SDS((LP, 512), BF16), SDS((N_HEADS, 1, LP), F32),
                   SDS((LP, BLK), F32)] + rider.out_shapes,
        compiler_params=_cparams(("arbitrary",)), name="fox_bwd",
    )(jnp.asarray([p[0] for p in pairs], jnp.int32), jnp.asarray([p[1] for p in pairs], jnp.int32),
      proj, proj, proj, proj, proj, proj, o_b, dmix, lse, ck_t, *rider.operands)
    return dq, dk, dv, dck, dcq, carried


N_SEG = 3
N_KEY = N_SEG * BLK
GROUP = 4
QW = GROUP * BLK


def _bucket_tables_t():
    return np.ascontiguousarray(_bucket_tables().transpose(0, 2, 1))


def _stack_heads(ref, g, scale):
    half = _lane_half(BLK)
    out = []
    for pair in range(2):
        x = ref[:, (2 * g + pair) * BLK:(2 * g + pair + 1) * BLK].astype(F32) * scale
        swapped = pltpu.roll(x, HALF, 1)
        for e in range(2):
            out.append(jnp.where(half == g, x if e == g else swapped, 0.0).astype(BF16))
    return jnp.concatenate(out, axis=0)


def _unstack_heads(x_t, g, ref, scale):
    for pair in range(2):
        both = jnp.concatenate([x_t[:, (2 * pair) * BLK:(2 * pair + 1) * BLK],
                                x_t[:, (2 * pair + 1) * BLK:(2 * pair + 2) * BLK]], axis=0)
        ref[:, (2 * g + pair) * BLK:(2 * g + pair + 1) * BLK] = (both.T * scale).astype(ref.dtype)


def _swa_tables(tab_ref, sink_ref, bkt_ref, tbl, sink_row):
    kk = lax.broadcasted_iota(jnp.int32, (BLK, BLK), 0)
    qq = lax.broadcasted_iota(jnp.int32, (BLK, BLK), 1)
    neg = jnp.full((BLK, BLK), NEG, F32)
    lane = lax.broadcasted_iota(jnp.int32, (1, QW), 1) // BLK
    for g in range(2):
        row = jnp.zeros((1, QW), F32)
        for hh in range(GROUP):
            h = GROUP * g + hh
            cols = slice(hh * BLK, (hh + 1) * BLK)
            row = jnp.where(lane == hh, sink_ref[0, h], row)

            def step(b, carry, h=h):
                t = tab_ref[b, h]
                return jnp.where(bkt_ref[0] == b, t, carry[0]), jnp.where(bkt_ref[1] == b, t, carry[1])
            zero = jnp.zeros((BLK, BLK), F32)
            cur, prev = lax.fori_loop(0, N_BUCKETS, step, (zero, zero))
            far = jnp.full((BLK, BLK), tab_ref[N_BUCKETS - 1, h], F32)
            causal = jnp.where(kk <= qq, cur, neg)
            segments = [
                (neg, neg, jnp.where(kk >= PAD_ROWS, causal, neg)),
                (jnp.where(kk >= PAD_ROWS, prev, neg), neg, causal),
                (jnp.where(kk >= PAD_ROWS, far, neg), jnp.where(kk > qq, prev, neg), causal)]
            for case in range(3):
                for seg in range(N_SEG):
                    tbl[case, g, seg * BLK:(seg + 1) * BLK, cols] = segments[case][seg]
        sink_row[g] = row


def _swa_prep(proj):
    rows = LP // 3

    def body(k_ref, v_ref, kt_ref, vt_ref):
        kt_ref[...] = k_ref[...].astype(F32).T.astype(BF16)
        vt_ref[...] = v_ref[...].astype(F32).T.astype(BF16)

    col = pl.BlockSpec((BLK, rows), lambda i: (0, i))
    return pl.pallas_call(
        body, grid=(3,),
        in_specs=[pl.BlockSpec((rows, BLK), lambda i: (i, KA)), pl.BlockSpec((rows, BLK), lambda i: (i, VA))],
        out_specs=[col, col], out_shape=[SDS((BLK, LP), BF16)] * 2,
        compiler_params=_cparams(("parallel",)), name="swa_prep")(proj, proj)


def _seg_specs(rows_major, col):
    idx = [lambda i: 0, lambda i: jnp.maximum(i - 1, 0), lambda i: i]
    if rows_major:
        return [pl.BlockSpec((BLK, BLK), lambda i, f=f: (f(i), col)) for f in idx]
    return [pl.BlockSpec((BLK, BLK), lambda i, f=f: (0, f(i))) for f in idx]


def _swa_fwd(proj, vt_a, rel_bias, sinks, bkt_t, rider):
    def body(*refs):
        ((tab_ref, sink_ref, bkt_ref, q_ref, km_ref, kp_ref, kc_ref, vm_ref, vp_ref, vc_ref), (o_ref, lse_ref),
         (tbl, sink_row), mine) = rider.split(refs, 10, 2, 2)
        i = pl.program_id(0)
        rider.at_steps(mine, i == 0, i == NBLK // 2, i == NBLK - 1)

        @pl.when(i == 0)
        def _():
            _swa_tables(tab_ref, sink_ref, bkt_ref, tbl, sink_row)

        case = jnp.minimum(i, 2)
        k_cat = jnp.concatenate([km_ref[...], kp_ref[...], kc_ref[...]], axis=0)
        vt_cat = jnp.concatenate([vm_ref[...], vp_ref[...], vc_ref[...]], axis=1)
        raw = [lax.dot_general(k_cat, _stack_heads(q_ref, g, SCALE), NT_DIMS, preferred_element_type=F32)
               for g in range(2)]
        for g in range(2):
            s_t = raw[g] + tbl[case, g]
            sink = sink_row[g]
            m = jnp.maximum(_over_keys(jnp.max, s_t), sink)
            p_t = jnp.exp(s_t - m)
            l = _over_keys(jnp.sum, p_t) + jnp.exp(sink - m)
            o_t = jnp.dot(vt_cat[g * HALF:(g + 1) * HALF, :], p_t.astype(BF16), preferred_element_type=F32)
            _unstack_heads(o_t * (1.0 / l), g, o_ref, 1.0)
            lse = m + jnp.log(l)
            for hh in range(GROUP):
                lse_ref[GROUP * g + hh] = lse[:, hh * BLK:(hh + 1) * BLK]

    smem = pl.BlockSpec(memory_space=pltpu.SMEM)
    o_a, lse, *carried = pl.pallas_call(
        body, grid=(NBLK,),
        in_specs=[smem, smem, pl.BlockSpec((2, BLK, BLK), lambda i: (0, 0, 0)),
                  pl.BlockSpec((BLK, 512), lambda i: (i, QA))] + _seg_specs(True, KA) + _seg_specs(False, 0)
        + [HBM_SPEC] * len(rider.operands),
        out_specs=[pl.BlockSpec((BLK, 512), lambda i: (i, 0)),
                   pl.BlockSpec((N_HEADS, 1, BLK), lambda i: (0, 0, i))] + [HBM_SPEC] * len(rider.out_shapes),
        out_shape=[SDS((LP, 512), BF16), SDS((N_HEADS, 1, LP), F32)] + rider.out_shapes,
        scratch_shapes=[pltpu.VMEM((3, 2, N_KEY, QW), F32), pltpu.VMEM((2, 1, QW), F32)] + rider.scratch(),
        compiler_params=_cparams(("arbitrary",)), name="swa_fwd",
    )(rel_bias, sinks, bkt_t, proj, proj, proj, proj, vt_a, vt_a, vt_a, *rider.operands)
    return o_a, lse, carried


def _swa_bwd(proj, kt_a, o_a, dmix, lse, rel_bias, sinks, bkt_t):
    def body(tab_ref, sink_ref, bkt_ref, q_ref, km_ref, kp_ref, kc_ref, vm_ref, vp_ref, vc_ref,
             tm_ref, tp_ref, tc_ref, o_ref, do_ref, lse_ref,
             dq_ref, dk_ref, dv_ref, dbias_ref, dsink_ref, tbl, sink_row, acc, dsk):
        i = pl.program_id(0)

        @pl.when(i == 0)
        def _():
            _swa_tables(tab_ref, sink_ref, bkt_ref, tbl, sink_row)
            dk_ref[...] = jnp.zeros_like(dk_ref)
            dv_ref[...] = jnp.zeros_like(dv_ref)
            acc[...] = jnp.zeros_like(acc)
            dsk[...] = jnp.zeros_like(dsk)

        case = jnp.minimum(i, 2)
        first = jnp.full((BLK, QW), i, jnp.int32) == 1
        k_cat = jnp.concatenate([km_ref[...], kp_ref[...], kc_ref[...]], axis=0)
        v_cat = jnp.concatenate([vm_ref[...], vp_ref[...], vc_ref[...]], axis=0)
        kt_cat = jnp.concatenate([tm_ref[...], tp_ref[...], tc_ref[...]], axis=1)
        dk_cat = jnp.zeros((N_KEY, BLK), F32)
        dv_cat = jnp.zeros((N_KEY, BLK), F32)
        for g in range(2):
            d_parts = []
            for pair in range(2):
                cols = slice((2 * g + pair) * BLK, (2 * g + pair + 1) * BLK)
                prod_t = (do_ref[:, cols].astype(F32) * o_ref[:, cols].astype(F32)).T
                d_parts += [jnp.sum(prod_t[:HALF], axis=0, keepdims=True),
                            jnp.sum(prod_t[HALF:], axis=0, keepdims=True)]
            d_row = jnp.concatenate(d_parts, axis=1)
            lse_row = jnp.concatenate([lse_ref[GROUP * g + hh] for hh in range(GROUP)], axis=1)
            q_st = _stack_heads(q_ref, g, SCALE)
            do_st = _stack_heads(do_ref, g, 1.0)
            s_t = lax.dot_general(k_cat, q_st, NT_DIMS, preferred_element_type=F32) + tbl[case, g]
            p_t = jnp.exp(s_t - lse_row)
            dp_t = lax.dot_general(v_cat, do_st, NT_DIMS, preferred_element_type=F32)
            ds_t = p_t * (dp_t - d_row)
            dsk[g] += -jnp.exp(sink_row[g] - lse_row) * d_row
            acc[g, 0:BLK] += jnp.where(first, 0.0, ds_t[0:BLK])
            acc[g, BLK:2 * BLK] += jnp.where(first, ds_t[0:BLK], ds_t[BLK:2 * BLK])
            acc[g, 2 * BLK:N_KEY] += ds_t[2 * BLK:N_KEY]
            ds_b = ds_t.astype(BF16)
            dk_cat = dk_cat + jnp.dot(ds_b, q_st, preferred_element_type=F32)
            dv_cat = dv_cat + jnp.dot(p_t.astype(BF16), do_st, preferred_element_type=F32)
            dq_t = jnp.dot(kt_cat[g * HALF:(g + 1) * HALF, :], ds_b, preferred_element_type=F32)
            _unstack_heads(dq_t, g, dq_ref, SCALE)

        prev0 = pl.multiple_of(jnp.maximum(i - 1, 0) * BLK, BLK)
        cur0 = pl.multiple_of(i * BLK, BLK)
        for ref, cat in ((dk_ref, dk_cat), (dv_ref, dv_cat)):
            ref[0:BLK, :] += cat[0:BLK]
            ref[pl.ds(prev0, BLK), :] += cat[BLK:2 * BLK]
            ref[pl.ds(cur0, BLK), :] += cat[2 * BLK:N_KEY]

        @pl.when(i == NBLK - 1)
        def _():
            lane = lax.broadcasted_iota(jnp.int32, (1, BLK), 1)

            def per_bucket(b, carry):
                row = jnp.zeros((1, BLK), F32)
                for h in range(N_HEADS):
                    g, cols = h // GROUP, slice((h % GROUP) * BLK, (h % GROUP + 1) * BLK)
                    val = (jnp.sum(jnp.where(bkt_ref[0] == b, acc[g, 2 * BLK:N_KEY, cols], 0.0), keepdims=True)
                           + jnp.sum(jnp.where(bkt_ref[1] == b, acc[g, BLK:2 * BLK, cols], 0.0), keepdims=True))
                    row = jnp.where(lane == h, val, row)
                dbias_ref[pl.ds(b, 1), :] = row
                return carry

            lax.fori_loop(0, N_BUCKETS, per_bucket, 0)
            far = jnp.zeros((1, BLK), F32)
            dsr = jnp.zeros((1, BLK), F32)
            for h in range(N_HEADS):
                g, cols = h // GROUP, slice((h % GROUP) * BLK, (h % GROUP + 1) * BLK)
                far = jnp.where(lane == h, jnp.sum(acc[g, 0:BLK, cols], keepdims=True), far)
                dsr = jnp.where(lane == h, jnp.sum(dsk[g, :, cols], keepdims=True), dsr)
            dbias_ref[N_BUCKETS - 1:N_BUCKETS, :] += far
            dsink_ref[...] = dsr

    smem = pl.BlockSpec(memory_space=pltpu.SMEM)
    blk512 = lambda col: pl.BlockSpec((BLK, 512), lambda i: (i, col))
    full = lambda r, c: pl.BlockSpec((r, c), lambda i: (0, 0))
    return pl.pallas_call(
        body, grid=(NBLK,),
        in_specs=[smem, smem, pl.BlockSpec((2, BLK, BLK), lambda i: (0, 0, 0)), blk512(QA)]
        + _seg_specs(True, KA) + _seg_specs(True, VA) + _seg_specs(False, 0)
        + [blk512(0), blk512(0), pl.BlockSpec((N_HEADS, 1, BLK), lambda i: (0, 0, i))],
        out_specs=[blk512(0), full(LP, BLK), full(LP, BLK), full(N_BUCKETS, BLK), full(1, BLK)],
        out_shape=[SDS((LP, 512), BF16), SDS((LP, BLK), F32), SDS((LP, BLK), F32),
                   SDS((N_BUCKETS, BLK), F32), SDS((1, BLK), F32)],
        scratch_shapes=[pltpu.VMEM((3, 2, N_KEY, QW), F32), pltpu.VMEM((2, 1, QW), F32),
                        pltpu.VMEM((2, N_KEY, QW), F32), pltpu.VMEM((2, 1, QW), F32)],
        compiler_params=_cparams(("arbitrary",)), name="swa_bwd",
    )(rel_bias, sinks, bkt_t, proj, proj, proj, proj, proj, proj, proj, kt_a, kt_a, kt_a, o_a, dmix, lse)


def _local_step(x, tgt, meta, rel_bias, g_pre_mix, g_post_mix, g_pre_ffn, g_post_ffn, b_forget, sinks,
                w_in_b, out_rider, out_weight, ffn_rider, ffn_weights, early_grads):
    bkt_t = jnp.asarray(_bucket_tables_t())
    h0 = jnp.concatenate([jnp.zeros((PAD_ROWS, D_MODEL), F32), meta, x], axis=0)
    b_p = jnp.pad(b_forget, ((0, 0), (0, BLK - N_HEADS)))

    hn1, proj, f = _pre_mix(h0, g_pre_mix, w_in_b)
    kt_a, vt_a = _swa_prep(proj)
    o_a, lse_a, carried_out = _swa_fwd(proj, vt_a, rel_bias, sinks, bkt_t, out_rider)
    w_out_b = out_weight(carried_out)
    cum = _forget_cumsum(f, b_p)
    ck_t = cum[:, :N_HEADS].T.reshape(N_HEADS, 1, LP)
    q_aug, k_aug, v_t = _fox_prep(proj, cum)
    o_b, lse_row, carried = _fox_fwd(q_aug, k_aug, v_t, ffn_rider)
    lse_b = lse_row.reshape(N_HEADS, LP, 1)
    w_gu_b, w_dn_b = ffn_weights(carried)
    a, h1, hn2 = _attn_out(o_a, o_b, w_out_b, h0, g_post_mix, g_pre_ffn)
    g, u, act = _ffn_up(hn2, w_gu_b)
    dff, dy, loss_blk, dg_post_ffn = _ffn_down_loss(act, w_dn_b, h1, tgt, g_post_ffn)

    dw_dn = _mm_tn([act], dff, FF_T, "dw_down", BF16)
    dg, du = _ffn_down_bwd(dff, w_dn_b, g, u)
    dw_gu = _dw_gate_up(hn2, dg, du)
    dh1, da, dg_pre_ffn, dg_post_mix = _ffn_up_bwd(dg, du, w_gu_b, h1, a, dy, g_pre_ffn, g_post_mix)
    dw_out = _mm_tn([o_a, o_b], da, D_MODEL, "dw_out", BF16)
    dmix = _attn_out_bwd(da, w_out_b)
    dq_b, dk_b, dv_b, dck, dcq, landed = _fox_bwd(proj, o_b, dmix, lse_b, ck_t, early_grads(dw_gu, dw_dn, dw_out))
    dq_a, dk_a, dv_a, dbias, dsink = _swa_bwd(proj, kt_a, o_a, dmix, lse_a, rel_bias, sinks, bkt_t)
    dcum = dcq - jnp.pad(dck.reshape(N_HEADS, LP).T, ((0, 0), (0, BLK - N_HEADS)))
    df, db = _forget_cumsum_bwd(dcum, f, b_p)
    dproj, dh0, dg_pre_mix = _pre_mix_bwd(dq_a, dq_b, dk_b, dv_b, dk_a, dv_a, df, w_in_b, h0, dh1, g_pre_mix)
    dw_in = _mm_tn([hn1], dproj, D_MODEL, "dw_in", BF16)

    return dict(loss=loss_blk[0, 0], grad_x=dh0[ROW0:], meta=dh0[PAD_ROWS:ROW0],
                rel_bias=dbias[:, :N_HEADS], ln_pre_mix=dg_pre_mix, ln_post_mix=dg_post_mix,
                ln_pre_ffn=dg_pre_ffn, ln_post_ffn=dg_post_ffn, b_forget=db[:, :N_HEADS],
                sinks=dsink[:, :N_HEADS], w_in=dw_in, w_out=dw_out, w_gate_up=dw_gu, w_down=dw_dn,
                landed=landed)


N_SMALL = 24
LOSS_ROW = 6


def _place():
    x, y, c = lax.axis_index("x"), lax.axis_index("y"), lax.axis_index("c")
    return x, y, c, [(1 - x, y), (x, 1 - y), (1 - x, 1 - y)]


def _run_alone(rider, name):
    a, b = len(rider.operands), len(rider.out_shapes)

    def body(*refs):
        mine = (refs[:a], refs[a:a + b], refs[a + b:])
        rider.first(*mine)
        rider.middle(*mine)
        rider.last(*mine)

    return pl.pallas_call(body, in_specs=[HBM_SPEC] * a, out_specs=[HBM_SPEC] * b, out_shape=rider.out_shapes,
                          scratch_shapes=rider.scratch(), name=name)(*rider.operands)


def _gather_rider(shards, own_too, by_columns=()):
    n = len(shards)

    def slot(a, outs, chip, h):
        if a in by_columns:
            cols = shards[a].shape[2]
            return outs[a].at[h, :, pl.ds(pl.multiple_of(chip * cols, BLK), cols)]
        return outs[a].at[chip, h]

    def own_copies(ins, outs, sems):
        x, y, _, _ = _place()
        if not own_too:
            return []
        return [pltpu.make_async_copy(ins[a].at[h], slot(a, outs, 2 * x + y, h), sems[2].at[2 * a + h])
                for a in range(n) for h in range(2)]

    def copies(ins, outs, sems):
        send_sems, recv_sems = sems[:2]
        x, y, c, others = _place()
        chip = 2 * x + y
        sibling = (x, y, 1 - c)

        def rc(a, k, src, dst, to):
            return pltpu.make_async_remote_copy(src_ref=src, dst_ref=dst, send_sem=send_sems.at[6 * a + k],
                                                recv_sem=recv_sems.at[6 * a + k], device_id=to, device_id_type=MESH)

        pairs = [(a, j, ox, oy) for a in range(n) for j, (ox, oy) in enumerate(others)]
        there = lambda a, ox, oy, h: slot(a, outs, 2 * ox + oy, h)
        return dict(
            sent=lambda: [rc(a, j, ins[a].at[c], slot(a, outs, chip, c), (ox, oy, c)) for a, j, ox, oy in pairs],
            landed=lambda: [rc(a, j, there(a, ox, oy, c), there(a, ox, oy, c), sibling) for a, j, ox, oy in pairs],
            passed=lambda: [rc(a, 3 + j, there(a, ox, oy, c), there(a, ox, oy, c), sibling)
                            for a, j, ox, oy in pairs],
            arriving=lambda: [rc(a, 3 + j, there(a, ox, oy, 1 - c), there(a, ox, oy, 1 - c), sibling)
                              for a, j, ox, oy in pairs])

    def first(*mine):
        for cp in copies(*mine)["sent"]() + own_copies(*mine):
            cp.start()

    def middle(*mine):
        kinds = copies(*mine)
        for got, cp in zip(kinds["landed"](), kinds["passed"]()):
            got.wait_recv()
            cp.start()

    def last(*mine):
        kinds = copies(*mine)
        for cp in kinds["arriving"]():
            cp.wait_recv()
        for cp in kinds["sent"]() + kinds["passed"]():
            cp.wait_send()
        for cp in own_copies(*mine):
            cp.wait()

    shapes = [SDS((2, s.shape[1], 4 * s.shape[2]) if a in by_columns else (4,) + s.shape, s.dtype)
              for a, s in enumerate(shards)]
    return _Rider(shards, shapes, [6 * n, 6 * n] + [2 * n] * own_too, first, middle, last)


def _swap_rider(grads):
    n = len(grads)
    slabs = [(a, s) for a in range(n) for s in range(grads[a].shape[0])]

    def copies(ins, outs, sems):
        x, y, c, _ = _place()
        return [pltpu.make_async_remote_copy(
            src_ref=ins[a].at[s, 1 - c], dst_ref=outs[a].at[s], send_sem=sems[0].at[k], recv_sem=sems[1].at[k],
            device_id=(x, y, 1 - c), device_id_type=MESH) for k, (a, s) in enumerate(slabs)]

    def first(*mine):
        for cp in copies(*mine):
            cp.start()

    def middle(*mine):
        pass

    def last(*mine):
        for cp in copies(*mine):
            cp.wait()

    return _Rider(grads, [SDS(g.shape[:1] + g.shape[2:], g.dtype) for g in grads], [len(slabs), len(slabs)],
                  first, middle, last)


def _pair_sum(g, got, c_arr, name):
    n_s, rh, cc = got.shape

    def body(c_ref, g_ref, p_ref, o_ref):
        o_ref[0] = (g_ref[0, 0].astype(F32) + p_ref[0].astype(F32)).astype(BF16)

    grid_spec = pltpu.PrefetchScalarGridSpec(
        num_scalar_prefetch=1, grid=(n_s,),
        in_specs=[pl.BlockSpec((1, 1, rh, cc), lambda s, c_ref: (s, c_ref[0], 0, 0)),
                  pl.BlockSpec((1, rh, cc), lambda s, c_ref: (s, 0, 0))],
        out_specs=pl.BlockSpec((1, rh, cc), lambda s, c_ref: (s, 0, 0)))
    return pl.pallas_call(body, grid_spec=grid_spec, out_shape=SDS((n_s, rh, cc), BF16),
                          compiler_params=_cparams(("parallel",)), name=name)(c_arr, g, got)


def _direct_rider(grads):
    n = len(grads)

    def copies(ins, outs, sems):
        x, y, c, others = _place()
        peers = [(x, y, 1 - c)] + [(ox, oy, c) for ox, oy in others] + [(ox, oy, 1 - c) for ox, oy in others]
        return [pltpu.make_async_remote_copy(
            src_ref=ins[a].at[2 * px + py, pc], dst_ref=outs[a].at[k], send_sem=sems[0].at[7 * a + k],
            recv_sem=sems[1].at[7 * a + k], device_id=(px, py, pc), device_id_type=MESH)
            for a in range(n) for k, (px, py, pc) in enumerate(peers)]

    def first(*mine):
        for cp in copies(*mine):
            cp.start()

    def middle(*mine):
        pass

    def last(*mine):
        for cp in copies(*mine):
            cp.wait()

    return _Rider(grads, [SDS((7,) + g.shape[2:], g.dtype) for g in grads], [7 * n, 7 * n], first, middle, last)


def _owner_sum(grads, landed, own_arr, after, name):
    rh, cc = landed.shape[1:]
    tr = rh // 2

    def body(own_ref, g_ref, p_ref, after_ref, o_ref):
        total = g_ref[0, 0].astype(F32)
        for k in range(7):
            total = total + p_ref[k].astype(F32)
        o_ref[...] = total

    grid_spec = pltpu.PrefetchScalarGridSpec(
        num_scalar_prefetch=1, grid=(2,),
        in_specs=[pl.BlockSpec((1, 1, tr, cc), lambda i, own: (own[0], own[1], i, 0)),
                  pl.BlockSpec((7, tr, cc), lambda i, own: (0, i, 0)), pl.BlockSpec(memory_space=pl.ANY)],
        out_specs=pl.BlockSpec((tr, cc), lambda i, own: (i, 0)))
    return pl.pallas_call(body, grid_spec=grid_spec, out_shape=SDS((rh, cc), F32),
                          compiler_params=_cparams(("parallel",)), name=name)(own_arr, grads, landed, after)


SEM_SPEC = pl.BlockSpec(memory_space=pltpu.SEMAPHORE)
N_LATE = 10


def _late_copies(part_ref, landed_ref, small_ref, all_ref, send_sems, recv_sems):
    x, y, c, others = _place()
    me = 4 * x + 2 * y + c
    peers = [(x, y, 1 - c)] + [(ox, oy, c) for ox, oy in others] + [(ox, oy, 1 - c) for ox, oy in others]
    big = [pltpu.make_async_remote_copy(
        src_ref=part_ref.at[2 * ox + oy], dst_ref=landed_ref.at[j], send_sem=send_sems.at[j], recv_sem=recv_sems.at[j],
        device_id=(ox, oy, c), device_id_type=MESH) for j, (ox, oy) in enumerate(others)]
    small = [pltpu.make_async_remote_copy(
        src_ref=small_ref, dst_ref=all_ref.at[me], send_sem=send_sems.at[3 + k], recv_sem=recv_sems.at[3 + k],
        device_id=peer, device_id_type=MESH) for k, peer in enumerate(peers)]
    return big + small


def _late_exchange_start(part, small):
    def body(part_ref, landed_ref, small_ref, all_ref, send_sems, recv_sems, part_o, landed_o, small_o, all_o, token):
        for cp in _late_copies(part_ref, landed_ref, small_ref, all_ref, send_sems, recv_sems):
            cp.start()
        token[...] = jnp.zeros_like(token)

    hbm = lambda a: pltpu.HBM(a.shape, a.dtype)
    landed = lax.empty((3,) + part.shape[1:], part.dtype)
    everyone = lax.empty((8,) + small.shape, small.dtype)
    operands = [pltpu.with_memory_space_constraint(a, pltpu.HBM) for a in (part, landed, small, everyone)]
    return pl.pallas_call(
        body, name="late_exchange_start",
        out_shape=(pltpu.SemaphoreType.DMA((N_LATE,)), pltpu.SemaphoreType.DMA((N_LATE,)),
                   hbm(part), hbm(landed), hbm(small), hbm(everyone), SDS((8, BLK), F32)),
        in_specs=[HBM_SPEC] * 4,
        out_specs=(SEM_SPEC, SEM_SPEC, HBM_SPEC, HBM_SPEC, HBM_SPEC, HBM_SPEC, pl.BlockSpec(memory_space=pltpu.VMEM)),
        input_output_aliases={0: 2, 1: 3, 2: 4, 3: 5},
        compiler_params=pltpu.CompilerParams(has_side_effects=pltpu.SideEffectType.DATAFLOW_SIDE_EFFECTING),
    )(*operands)


def _late_exchange_wait(send_sems, recv_sems, part, landed, small, everyone, after):
    def body(part_ref, landed_ref, small_ref, all_ref, send_sems, recv_sems, after_ref, part_o, landed_o, small_o, all_o):
        for cp in _late_copies(part_ref, landed_ref, small_ref, all_ref, send_sems, recv_sems):
            cp.wait_send()
            cp.wait_recv()

    hbm = lambda a: pltpu.HBM(a.shape, a.dtype)
    out = pl.pallas_call(
        body, name="late_exchange_wait",
        out_shape=(hbm(part), hbm(landed), hbm(small), hbm(everyone)),
        in_specs=[HBM_SPEC] * 4 + [SEM_SPEC, SEM_SPEC, pl.BlockSpec(memory_space=pl.ANY)],
        out_specs=(HBM_SPEC,) * 4, input_output_aliases={0: 0, 1: 1, 2: 2, 3: 3},
        compiler_params=pltpu.CompilerParams(has_side_effects=pltpu.SideEffectType.DATAFLOW_SIDE_EFFECTING),
    )(part, landed, small, everyone, send_sems, recv_sems, after)
    return out[0], out[1], out[3]


def _chip_sum(parts, landed, chip_arr, name):
    rh, cc = landed.shape[1:]
    tr = rh // 2

    def body(chip_ref, own_ref, p_ref, o_ref):
        o_ref[...] = ((own_ref[0].astype(F32) + p_ref[0].astype(F32)) + p_ref[1].astype(F32)) + p_ref[2].astype(F32)

    grid_spec = pltpu.PrefetchScalarGridSpec(
        num_scalar_prefetch=1, grid=(2,),
        in_specs=[pl.BlockSpec((1, tr, cc), lambda i, chip_ref: (chip_ref[0], i, 0)),
                  pl.BlockSpec((3, tr, cc), lambda i, chip_ref: (0, i, 0))],
        out_specs=pl.BlockSpec((tr, cc), lambda i, chip_ref: (i, 0)))
    return pl.pallas_call(body, grid_spec=grid_spec, out_shape=SDS((rh, cc), F32),
                          compiler_params=_cparams(("parallel",)), name=name)(chip_arr, parts, landed)


def _device_sum(p):
    def body(p_ref, o_ref):
        acc = p_ref[0]
        for k in range(1, 8):
            acc = acc + p_ref[k]
        o_ref[...] = acc

    return pl.pallas_call(body, out_shape=SDS(p.shape[1:], F32), name="small_sum")(p)


def _join_halves(halves, name):
    n = len(halves)

    def body(*refs):
        ins, outs = refs[:n], refs[n:2 * n]
        send_sems, recv_sems = refs[2 * n:]
        x, y, c, _ = _place()
        copies = [pltpu.make_async_remote_copy(
            src_ref=ins[a], dst_ref=outs[a], send_sem=send_sems.at[a], recv_sem=recv_sems.at[a],
            device_id=(x, y, 1 - c), device_id_type=MESH) for a in range(n)]
        for cp in copies:
            cp.start()
        for cp in copies:
            cp.wait()

    return pl.pallas_call(
        body, in_specs=[HBM_SPEC] * n, out_specs=[HBM_SPEC] * n,
        out_shape=[SDS(h.shape, h.dtype) for h in halves],
        scratch_shapes=[pltpu.SemaphoreType.DMA((n,)), pltpu.SemaphoreType.DMA((n,))],
        name=name)(*halves)


def _adamw(w, g, m, v, name, tr=None):
    rows, cols = w.shape
    tr = tr or rows
    assert rows % tr == 0

    def body(w_ref, g_ref, m_ref, v_ref, d_ref, nm_ref, nv_ref):
        gg = g_ref[...]
        nm = ADAM_B1 * m_ref[...] + (1.0 - ADAM_B1) * gg
        nv = ADAM_B2 * v_ref[...] + (1.0 - ADAM_B2) * (gg * gg)
        nm_ref[...] = nm
        nv_ref[...] = nv
        m_hat = nm / (1.0 - ADAM_B1 ** ADAM_STEP)
        v_hat = nv / (1.0 - ADAM_B2 ** ADAM_STEP)
        d_ref[...] = -ADAM_LR * (m_hat / (jnp.sqrt(v_hat) + ADAM_EPS) + ADAM_WD * w_ref[...])

    blk = pl.BlockSpec((tr, cols), lambda i: (i, 0))
    return pl.pallas_call(
        body, grid=(rows // tr,), in_specs=[blk] * 4, out_specs=[blk] * 3,
        out_shape=[SDS((rows, cols), F32)] * 3,
        compiler_params=_cparams(("parallel",)), name=name)(w, g, m, v)


def _adamw_halves(w, mine, theirs, m, v, c_arr, name):
    rows, cols = w.shape
    rh = rows // 2
    tr = rh if rh <= 352 else 256
    nh = rh // tr

    def body(c_ref, w_ref, mine_ref, theirs_ref, m_ref, v_ref, g_ref, d_ref, nm_ref, nv_ref):
        own = jnp.full((tr, cols), pl.program_id(0), jnp.int32) == c_ref[0]
        gg = jnp.where(own, mine_ref[...], theirs_ref[...])
        g_ref[...] = gg
        nm = ADAM_B1 * m_ref[...] + (1.0 - ADAM_B1) * gg
        nv = ADAM_B2 * v_ref[...] + (1.0 - ADAM_B2) * (gg * gg)
        nm_ref[...] = nm
        nv_ref[...] = nv
        m_hat = nm / (1.0 - ADAM_B1 ** ADAM_STEP)
        v_hat = nv / (1.0 - ADAM_B2 ** ADAM_STEP)
        d_ref[...] = -ADAM_LR * (m_hat / (jnp.sqrt(v_hat) + ADAM_EPS) + ADAM_WD * w_ref[...])

    whole = pl.BlockSpec((tr, cols), lambda hh, i, c_ref: (hh * nh + i, 0))
    part = pl.BlockSpec((tr, cols), lambda hh, i, c_ref: (i, 0))
    grid_spec = pltpu.PrefetchScalarGridSpec(
        num_scalar_prefetch=1, grid=(2, nh), in_specs=[whole, part, part, whole, whole], out_specs=[whole] * 4)
    return pl.pallas_call(body, grid_spec=grid_spec, out_shape=[SDS((rows, cols), F32)] * 4,
                          compiler_params=_cparams(("parallel", "parallel")), name=name)(c_arr, w, mine, theirs, m, v)


def _pack_small(pre_mix, post_mix, pre_ffn, post_ffn, rel_bias, b_forget, sinks):
    def at(row, v):
        return jnp.pad(v, ((row, 7 - row), (0, D_MODEL - v.shape[1])))
    return (at(0, pre_mix) + at(1, post_mix) + at(2, pre_ffn) + at(3, post_ffn)
            + at(4, rel_bias.reshape(1, N_BUCKETS * N_HEADS)) + at(5, jnp.concatenate([b_forget, sinks], axis=1)))


def _unpack_small(p):
    return dict(ln_pre_mix=p[0:1], ln_post_mix=p[1:2], ln_pre_ffn=p[2:3], ln_post_ffn=p[3:4],
                rel_bias=p[4, :N_BUCKETS * N_HEADS].reshape(N_BUCKETS, N_HEADS),
                b_forget=p[5:6, 0:N_HEADS], sinks=p[5:6, N_HEADS:2 * N_HEADS])


WEIGHTS = ("meta_tokens", "rel_bias", "ln_pre_mix", "ln_post_mix", "ln_pre_ffn", "ln_post_ffn",
           "w_in", "b_forget", "sinks", "w_out", "w_gate_up", "w_down")


def kernel(x, meta_tokens, rel_bias, ln_pre_mix, ln_post_mix, ln_pre_ffn, ln_post_ffn, w_in, b_forget, sinks, w_out, w_gate_up, w_down, loss_target, m_meta_tokens, m_rel_bias, m_ln_pre_mix, m_ln_post_mix, m_ln_pre_ffn, m_ln_post_ffn, m_w_in, m_b_forget, m_sinks, m_w_out, m_w_gate_up, m_w_down, v_meta_tokens, v_rel_bias, v_ln_pre_mix, v_ln_post_mix, v_ln_pre_ffn, v_ln_post_ffn, v_w_in, v_b_forget, v_sinks, v_w_out, v_w_gate_up, v_w_down):
    xi, yi, ci = lax.axis_index("x"), lax.axis_index("y"), lax.axis_index("c")
    chip = 2 * xi + yi
    c_arr = jnp.reshape(ci, (1,)).astype(jnp.int32)

    def halves(w, dtype):
        return w.astype(dtype).reshape(2, w.shape[0] // 2, w.shape[1])

    def with_own(gathered, shards):
        return [lax.dynamic_update_slice(got, own[None], (chip, 0, 0, 0)) for got, own in zip(gathered, shards)]

    shards = [halves(w_in[0], BF16), halves(meta_tokens, F32)]
    gw_in, g_meta = with_own(_run_alone(_gather_rider(shards, False), "gather_mixer_weights"), shards)
    out_shards = [halves(w_out[0], BF16)]
    ffn_shards = [halves(w_gate_up[0], BF16), halves(w_down[0], BF16)]

    def ffn_weights(carried):
        gw_gu, gw_dn = carried
        return gw_gu.reshape(D_MODEL, 2 * D_FF), gw_dn.reshape(D_FF, D_MODEL)

    early = {}

    def early_grads(dw_gu, dw_dn, dw_out):
        early["grads"] = [dw_out.reshape(4, 2, 128, D_MODEL), dw_gu.reshape(4, 2, 512, FF_T),
                          dw_dn.reshape(4, 2, 352, D_MODEL)]
        return _direct_rider(early["grads"])
    w_in_all = gw_in.reshape(4, D_MODEL, D_PROJ // 4).transpose(1, 0, 2).reshape(D_MODEL, D_PROJ)
    w_in_b = jnp.pad(w_in_all, ((0, 0), (0, D_PROJ_P - D_PROJ)))
    meta_all = g_meta.reshape(4, N_META, D_MODEL // 4).transpose(1, 0, 2).reshape(N_META, D_MODEL)

    loc = _local_step(x[0], loss_target[0], meta_all, rel_bias, ln_pre_mix, ln_post_mix, ln_pre_ffn, ln_post_ffn,
                      b_forget, sinks, w_in_b, _gather_rider(out_shards, True),
                      lambda carried: carried[0].reshape(D_MODEL, D_MODEL),
                      _gather_rider(ffn_shards, True, by_columns=(0,)), ffn_weights, early_grads)

    small = jnp.concatenate(
        [_pack_small(loc["ln_pre_mix"], loc["ln_post_mix"], loc["ln_pre_ffn"], loc["ln_post_ffn"],
                     loc["rel_bias"], loc["b_forget"], loc["sinks"])
         + jnp.pad(loc["loss"].reshape(1, 1), ((LOSS_ROW, 7 - LOSS_ROW), (0, D_MODEL - 1))), loc["meta"]], axis=0)

    dw_in = loc["w_in"].reshape(1, 2, D_MODEL // 2, D_PROJ_P)
    (got_in,) = _run_alone(_swap_rider([dw_in]), "swap_halves_late")
    half_sum = _pair_sum(dw_in, got_in, c_arr, "pair_sum_late")
    part_in = half_sum[0, :, :D_PROJ].reshape(D_MODEL // 2, 4, D_PROJ // 4).transpose(1, 0, 2)
    send_sems, recv_sems, part_sent, landing, small_sent, everyone, token = _late_exchange_start(part_in, small)
    chip_arr = jnp.reshape(chip, (1,)).astype(jnp.int32)
    own_arr = jnp.stack([chip, ci]).astype(jnp.int32)
    grad, delta, new_m, new_v = {}, {}, {}, {}

    def update(names, mine):
        theirs = _join_halves(mine, "join_" + names[0])
        big = dict(w_in=(w_in, m_w_in, v_w_in), w_out=(w_out, m_w_out, v_w_out),
                   w_gate_up=(w_gate_up, m_w_gate_up, v_w_gate_up), w_down=(w_down, m_w_down, v_w_down))
        for name, g_mine, g_theirs in zip(names, mine, theirs):
            w, m, v = big[name]
            g, d, nm, nv = _adamw_halves(w[0], g_mine, g_theirs, m[0], v[0], c_arr, "adamw_" + name)
            grad[name], delta[name], new_m[name], new_v[name] = g[None], d[None], nm[None], nv[None]

    update(("w_out", "w_gate_up", "w_down"),
           [_owner_sum(g, l, own_arr, token, "owner_sum_%d" % a)
            for a, (g, l) in enumerate(zip(early["grads"], loc["landed"]))])
    part_back, landed_in, small_all = _late_exchange_wait(send_sems, recv_sems, part_sent, landing, small_sent,
                                                         everyone, new_v["w_down"])
    mine_in = _chip_sum(part_back, landed_in, chip_arr, "chip_sum_in")
    (theirs_in,) = _join_halves([mine_in], "join_w_in")
    g_w_in = jnp.where(ci == 0, jnp.concatenate([mine_in, theirs_in], axis=0),
                       jnp.concatenate([theirs_in, mine_in], axis=0))
    view = lambda a: jnp.transpose(a).reshape(D_PROJ // 4 * 8, BLK)
    back = lambda a: jnp.transpose(a.reshape(D_PROJ // 4, D_MODEL))[None]
    d, nm, nv = _adamw(view(w_in[0]), view(g_w_in), view(m_w_in[0]), view(v_w_in[0]), "adamw_w_in",
                       tr=D_PROJ // 4 * 4)
    grad["w_in"], delta["w_in"], new_m["w_in"], new_v["w_in"] = g_w_in[None], back(d), back(nm), back(nv)
    me = 4 * xi + 2 * yi + ci
    small_sum = _device_sum(lax.dynamic_update_slice(small_all, small[None], (me, 0, 0)))
    g_meta_tokens = lax.dynamic_slice(small_sum[8:N_SMALL], (0, chip * (D_MODEL // 4)), (N_META, D_MODEL // 4))
    g_small = small_sum[0:8]
    grad.update(_unpack_small(g_small))
    grad.update(meta_tokens=g_meta_tokens)
    delta["meta_tokens"], new_m["meta_tokens"], new_v["meta_tokens"] = _adamw(
        meta_tokens, g_meta_tokens, m_meta_tokens, v_meta_tokens, "adamw_meta")
    d, nm, nv = _adamw(
        _pack_small(ln_pre_mix, ln_post_mix, ln_pre_ffn, ln_post_ffn, rel_bias, b_forget, sinks), g_small,
        _pack_small(m_ln_pre_mix, m_ln_post_mix, m_ln_pre_ffn, m_ln_post_ffn, m_rel_bias, m_b_forget, m_sinks),
        _pack_small(v_ln_pre_mix, v_ln_post_mix, v_ln_pre_ffn, v_ln_post_ffn, v_rel_bias, v_b_forget, v_sinks),
        "adamw_small")
    delta.update(_unpack_small(d))
    new_m.update(_unpack_small(nm))
    new_v.update(_unpack_small(nv))

    loss = small_sum[LOSS_ROW, 0]
    return (loss,loc["grad_x"][None], *[grad[k] for k in WEIGHTS], *[delta[k] for k in WEIGHTS],
            *[new_m[k] for k in WEIGHTS], *[new_v[k] for k in WEIGHTS])
```

```python
import math

import numpy as np
import jax
import jax.numpy as jnp
from jax import lax
from jax.experimental import pallas as pl
from jax.experimental.pallas import tpu as pltpu

F32 = jnp.float32
BF16 = jnp.bfloat16
MESH = pl.DeviceIdType.MESH
SDS = jax.ShapeDtypeStruct

D_MODEL = 1024
SEQ = 4096
N_META = 16
N_HEADS = 8
HALF = 64
D_FF = 2816
N_BUCKETS = 32
EPS = 1e-6
NEG = -1e30
SCALE = 0.125
PAD_ROWS = 112
ROW0 = PAD_ROWS + N_META
LP = ROW0 + SEQ
BLK = 128
NBLK = LP // BLK
TM = 384
NT = LP // TM
TM_PURE = LP // 2
TM_MID = LP // 4
TM_EPI = LP // 6
TN = 256
D_PROJ = 2312
D_PROJ_P = 2432
D_QKV = 2304
FF_T = 1408
VMEM_LIMIT = 56 * 1024 * 1024

ADAM_LR = 0.001
ADAM_B1 = 0.9
ADAM_B2 = 0.999
ADAM_EPS = 1e-08
ADAM_WD = 0.01
ADAM_STEP = 10

QA = 0
KA, VA = 4, 5
QB, KB, VB = 3, 5, 7
W2 = 256

NT_DIMS = (((1,), (1,)), ((), ()))
TN_DIMS = (((0,), (0,)), ((), ()))


def _cparams(sem):
    return pltpu.CompilerParams(dimension_semantics=sem, vmem_limit_bytes=VMEM_LIMIT)


def _t5_bucket_np(d):
    n = np.maximum(d, 0).astype(np.int32)
    nf = np.maximum(n, 1).astype(np.float32)
    large = 16 + (np.log(nf / np.float32(16)) / np.float32(math.log(8.0)) * np.float32(16)).astype(np.int32)
    large = np.minimum(large, N_BUCKETS - 1)
    return np.where(n < 16, n, large).astype(np.int32)


def _bucket_tables():
    qi = np.arange(BLK)[:, None]
    ki = np.arange(BLK)[None, :]
    return np.stack([_t5_bucket_np(qi - ki), _t5_bucket_np(qi - ki + BLK)])


def _rms(x):
    return lax.rsqrt(jnp.mean(x * x, axis=-1, keepdims=True) + EPS)


def _rms_bwd(n, r, gdy):
    return r * (gdy - n * jnp.mean(n * gdy, axis=-1, keepdims=True))


def _pre_mix(h0, gain, w_in_b):
    half = D_QKV // 2

    def body(h_ref, g_ref, w_ref, hn_ref, proj_ref, f_ref):
        x = h_ref[...]
        hn = (x * _rms(x) * g_ref[...]).astype(BF16)
        hn_ref[...] = hn
        proj_ref[:, :half] = jnp.dot(hn, w_ref[:, :half], preferred_element_type=F32).astype(BF16)
        p = jnp.dot(hn, w_ref[:, half:], preferred_element_type=F32)
        proj_ref[:, half:] = p[:, :half].astype(BF16)
        f_ref[...] = p[:, half:]

    return pl.pallas_call(
        body, grid=(LP // TM_MID,),
        in_specs=[pl.BlockSpec((TM_MID, D_MODEL), lambda i: (i, 0)),
                  pl.BlockSpec((1, D_MODEL), lambda i: (0, 0)),
                  pl.BlockSpec((D_MODEL, D_PROJ_P), lambda i: (0, 0))],
        out_specs=[pl.BlockSpec((TM_MID, D_MODEL), lambda i: (i, 0)),
                   pl.BlockSpec((TM_MID, D_QKV), lambda i: (i, 0)),
                   pl.BlockSpec((TM_MID, BLK), lambda i: (i, 0))],
        out_shape=[SDS((LP, D_MODEL), BF16), SDS((LP, D_QKV), BF16), SDS((LP, BLK), F32)],
        compiler_params=_cparams(("parallel",)), name="pre_mix")(h0, gain, w_in_b)


def _attn_out(o_a, o_b, w_out_b, h0, g_post, g_pre_ffn):
    def body(oa_ref, ob_ref, w_ref, h0_ref, gp_ref, gf_ref, a_ref, h1_ref, hn2_ref):
        a = (jnp.dot(oa_ref[...], w_ref[0:512, :], preferred_element_type=F32)
             + jnp.dot(ob_ref[...], w_ref[512:1024, :], preferred_element_type=F32))
        a_ref[...] = a
        h1 = h0_ref[...] + a * _rms(a) * gp_ref[...]
        h1_ref[...] = h1
        hn2_ref[...] = (h1 * _rms(h1) * gf_ref[...]).astype(BF16)

    row = lambda w: pl.BlockSpec((TM_EPI, w), lambda i: (i, 0))
    vec = pl.BlockSpec((1, D_MODEL), lambda i: (0, 0))
    return pl.pallas_call(
        body, grid=(LP // TM_EPI,),
        in_specs=[row(512), row(512), pl.BlockSpec((D_MODEL, D_MODEL), lambda i: (0, 0)), row(D_MODEL), vec, vec],
        out_specs=[row(D_MODEL), row(D_MODEL), row(D_MODEL)],
        out_shape=[SDS((LP, D_MODEL), F32), SDS((LP, D_MODEL), F32), SDS((LP, D_MODEL), BF16)],
        compiler_params=_cparams(("parallel",)), name="attn_out")(o_a, o_b, w_out_b, h0, g_post, g_pre_ffn)


def _ffn_up(hn2, w_gu_b):
    def body(x_ref, wg_ref, wu_ref, g_ref, u_ref, act_ref):
        x = x_ref[...]
        g = jnp.dot(x, wg_ref[...], preferred_element_type=F32)
        u = jnp.dot(x, wu_ref[...], preferred_element_type=F32)
        g_ref[...] = g.astype(BF16)
        u_ref[...] = u.astype(BF16)
        act_ref[...] = (g * (1.0 / (1.0 + jnp.exp(-g))) * u).astype(BF16)

    out = pl.BlockSpec((TM_PURE, TN), lambda i, j: (i, j))
    return pl.pallas_call(
        body, grid=(LP // TM_PURE, D_FF // TN),
        in_specs=[pl.BlockSpec((TM_PURE, D_MODEL), lambda i, j: (i, 0)),
                  pl.BlockSpec((D_MODEL, TN), lambda i, j: (0, j)),
                  pl.BlockSpec((D_MODEL, TN), lambda i, j: (0, j + D_FF // TN))],
        out_specs=[out, out, out],
        out_shape=[SDS((LP, D_FF), BF16)] * 3,
        compiler_params=_cparams(("parallel", "parallel")), name="ffn_up")(hn2, w_gu_b, w_gu_b)


def _ffn_down_loss(act, w_dn_b, h1, tgt, g_post_ffn):
    def body(act_ref, w_ref, h1_ref, t0_ref, t1_ref, t2_ref, g_ref, dff_ref, dy_ref, loss_ref, dg_ref):
        i = pl.program_id(0)
        target = jnp.concatenate([t0_ref[...], t1_ref[...], t2_ref[...]], axis=0)

        @pl.when(i == 0)
        def _():
            loss_ref[...] = jnp.zeros_like(loss_ref)
            dg_ref[...] = jnp.zeros_like(dg_ref)

        ff = jnp.dot(act_ref[...], w_ref[...], preferred_element_type=F32)
        r = _rms(ff)
        n = ff * r
        g = g_ref[...]
        y = h1_ref[...] + n * g
        rows = i * TM + lax.broadcasted_iota(jnp.int32, (TM, D_MODEL), 0)
        diff = jnp.where(rows >= ROW0, y - target, 0.0)
        loss_ref[...] += 0.5 * jnp.sum(diff * diff) / D_MODEL
        dy = diff / D_MODEL
        dy_ref[...] = dy
        dg_ref[...] += jnp.sum(dy * n, axis=0, keepdims=True)
        dff_ref[...] = _rms_bwd(n, r, g * dy).astype(BF16)

    row = pl.BlockSpec((TM, D_MODEL), lambda i: (i, 0))
    tblk = lambda j: pl.BlockSpec((BLK, D_MODEL), lambda i: (jnp.maximum(3 * i - 1 + j, 0), 0))
    return pl.pallas_call(
        body, grid=(NT,),
        in_specs=[pl.BlockSpec((TM, D_FF), lambda i: (i, 0)), pl.BlockSpec((D_FF, D_MODEL), lambda i: (0, 0)),
                  row, tblk(0), tblk(1), tblk(2), pl.BlockSpec((1, D_MODEL), lambda i: (0, 0))],
        out_specs=[row, row, pl.BlockSpec((8, BLK), lambda i: (0, 0)), pl.BlockSpec((1, D_MODEL), lambda i: (0, 0))],
        out_shape=[SDS((LP, D_MODEL), BF16), SDS((LP, D_MODEL), F32), SDS((8, BLK), F32), SDS((1, D_MODEL), F32)],
        compiler_params=_cparams(("arbitrary",)), name="ffn_down_loss")(act, w_dn_b, h1, tgt, tgt, tgt, g_post_ffn)


def _ffn_down_bwd(dff, w_dn_b, g, u):
    def body(d_ref, w_ref, g_ref, u_ref, dg_ref, du_ref):
        dact = lax.dot_general(d_ref[...], w_ref[...], NT_DIMS, preferred_element_type=F32)
        gg = g_ref[...].astype(F32)
        sig = 1.0 / (1.0 + jnp.exp(-gg))
        dg_ref[...] = (dact * u_ref[...].astype(F32) * sig * (1.0 + gg * (1.0 - sig))).astype(BF16)
        du_ref[...] = (dact * gg * sig).astype(BF16)

    blk = pl.BlockSpec((TM_PURE, TN), lambda i, j: (i, j))
    return pl.pallas_call(
        body, grid=(LP // TM_PURE, D_FF // TN),
        in_specs=[pl.BlockSpec((TM_PURE, D_MODEL), lambda i, j: (i, 0)),
                  pl.BlockSpec((TN, D_MODEL), lambda i, j: (j, 0)), blk, blk],
        out_specs=[blk, blk],
        out_shape=[SDS((LP, D_FF), BF16)] * 2,
        compiler_params=_cparams(("parallel", "parallel")), name="ffn_down_bwd")(dff, w_dn_b, g, u)


def _ffn_up_bwd(dg, du, w_gu_b, h1, a, dy, g_pre_ffn, g_post_mix):
    def body(dg_ref, du_ref, w_ref, h1_ref, a_ref, dy_ref, gf_ref, gp_ref,
             dh1_ref, da_ref, dgf_ref, dgp_ref, acc):
        i = pl.program_id(0)
        s = pl.program_id(1)

        @pl.when((i == 0) & (s == 0))
        def _():
            dgf_ref[...] = jnp.zeros_like(dgf_ref)
            dgp_ref[...] = jnp.zeros_like(dgp_ref)

        @pl.when(s == 0)
        def _():
            acc[...] = jnp.zeros_like(acc)

        @pl.when(s < 2)
        def _():
            acc[...] += lax.dot_general(dg_ref[...], w_ref[...], NT_DIMS, preferred_element_type=F32)

        @pl.when(s >= 2)
        def _():
            acc[...] += lax.dot_general(du_ref[...], w_ref[...], NT_DIMS, preferred_element_type=F32)

        @pl.when(s == 3)
        def _():
            dhn2 = acc[...]
            h1 = h1_ref[...]
            r2 = _rms(h1)
            n2 = h1 * r2
            dgf_ref[...] += jnp.sum(dhn2 * n2, axis=0, keepdims=True)
            dh1 = dy_ref[...] + _rms_bwd(n2, r2, gf_ref[...] * dhn2)
            dh1_ref[...] = dh1
            av = a_ref[...]
            ra = _rms(av)
            na = av * ra
            dgp_ref[...] += jnp.sum(dh1 * na, axis=0, keepdims=True)
            da_ref[...] = _rms_bwd(na, ra, gp_ref[...] * dh1).astype(BF16)

    row = pl.BlockSpec((TM_EPI, D_MODEL), lambda i, s: (i, 0))
    vec = pl.BlockSpec((1, D_MODEL), lambda i, s: (0, 0))
    return pl.pallas_call(
        body, grid=(LP // TM_EPI, 4),
        in_specs=[pl.BlockSpec((TM_EPI, FF_T), lambda i, s: (i, jnp.minimum(s, 1))),
                  pl.BlockSpec((TM_EPI, FF_T), lambda i, s: (i, jnp.maximum(s - 2, 0))),
                  pl.BlockSpec((D_MODEL, FF_T), lambda i, s: (0, s)),
                  row, row, row, vec, vec],
        out_specs=[row, row, vec, vec],
        out_shape=[SDS((LP, D_MODEL), F32), SDS((LP, D_MODEL), BF16), SDS((1, D_MODEL), F32), SDS((1, D_MODEL), F32)],
        scratch_shapes=[pltpu.VMEM((TM_EPI, D_MODEL), F32)],
        compiler_params=_cparams(("arbitrary", "arbitrary")), name="ffn_up_bwd",
    )(dg, du, w_gu_b, h1, a, dy, g_pre_ffn, g_post_mix)


def _attn_out_bwd(da, w_out_b):
    def body(d_ref, w_ref, o_ref):
        o_ref[...] = lax.dot_general(d_ref[...], w_ref[...], NT_DIMS, preferred_element_type=F32).astype(BF16)

    row = pl.BlockSpec((TM_PURE, D_MODEL), lambda i: (i, 0))
    return pl.pallas_call(
        body, grid=(LP // TM_PURE,),
        in_specs=[row, pl.BlockSpec((D_MODEL, D_MODEL), lambda i: (0, 0))],
        out_specs=row, out_shape=SDS((LP, D_MODEL), BF16),
        compiler_params=_cparams(("parallel",)), name="attn_out_bwd")(da, w_out_b)


def _pre_mix_bwd(dq_a, dq_b, dk_b, dv_b, dk_a, dv_a, df, w_in_b, h0, dh1, g_pre_mix):
    def body(qa_ref, qb_ref, kb_ref, vb_ref, ka_ref, va_ref, f_ref, w_ref, h0_ref, dh1_ref, g_ref,
             dproj_ref, dh0_ref, dg_ref):
        i = pl.program_id(0)

        @pl.when(i == 0)
        def _():
            dg_ref[...] = jnp.zeros_like(dg_ref)

        dproj = jnp.concatenate(
            [qa_ref[...], ka_ref[...].astype(BF16), va_ref[...].astype(BF16), (qb_ref[...] * SCALE).astype(BF16),
             kb_ref[...], vb_ref[...], f_ref[...].astype(BF16)], axis=1)
        dproj_ref[...] = dproj
        dhn = lax.dot_general(dproj, w_ref[...], NT_DIMS, preferred_element_type=F32)
        x = h0_ref[...]
        r = _rms(x)
        n = x * r
        dg_ref[...] += jnp.sum(dhn * n, axis=0, keepdims=True)
        dh0_ref[...] = dh1_ref[...] + _rms_bwd(n, r, g_ref[...] * dhn)

    row = lambda w: pl.BlockSpec((TM_EPI, w), lambda i: (i, 0))
    vec = pl.BlockSpec((1, D_MODEL), lambda i: (0, 0))
    return pl.pallas_call(
        body, grid=(LP // TM_EPI,),
        in_specs=[row(512), row(512), row(512), row(512), row(BLK), row(BLK), row(BLK),
                  pl.BlockSpec((D_MODEL, D_PROJ_P), lambda i: (0, 0)), row(D_MODEL), row(D_MODEL), vec],
        out_specs=[row(D_PROJ_P), row(D_MODEL), vec],
        out_shape=[SDS((LP, D_PROJ_P), BF16), SDS((LP, D_MODEL), F32), SDS((1, D_MODEL), F32)],
        compiler_params=_cparams(("arbitrary",)), name="pre_mix_bwd",
    )(dq_a, dq_b, dk_b, dv_b, dk_a, dv_a, df, w_in_b, h0, dh1, g_pre_mix)


def _mm_tn(parts, b, tm, name, out_dtype=F32):
    widths = [p.shape[1] for p in parts]
    m_total = sum(widths)
    n = b.shape[1]
    whole = len(parts) > 1
    n_k = LP // TM_MID
    assert (tm == m_total) if whole else (m_total % tm == 0)

    def body(*refs):
        a_refs, b_ref, o_ref, acc = refs[:-3], refs[-3], refs[-2], refs[-1]
        k = pl.program_id(1)

        @pl.when(k == 0)
        def _():
            acc[...] = jnp.zeros_like(acc)
        a = a_refs[0][...] if not whole else jnp.concatenate([r[...] for r in a_refs], axis=1)
        acc[...] += lax.dot_general(a, b_ref[...], TN_DIMS, preferred_element_type=F32)

        @pl.when(k == n_k - 1)
        def _():
            o_ref[...] = acc[...].astype(out_dtype)

    a_specs = ([pl.BlockSpec((TM_MID, w), lambda mi, k: (k, 0)) for w in widths] if whole
               else [pl.BlockSpec((TM_MID, tm), lambda mi, k: (k, mi))])
    return pl.pallas_call(
        body, grid=(m_total // tm, n_k),
        in_specs=a_specs + [pl.BlockSpec((TM_MID, n), lambda mi, k: (k, 0))],
        out_specs=pl.BlockSpec((tm, n), lambda mi, k: (mi, 0)),
        out_shape=SDS((m_total, n), out_dtype),
        scratch_shapes=[pltpu.VMEM((tm, n), F32)],
        compiler_params=_cparams(("parallel", "arbitrary")), name=name)(*parts, b)


def _dw_gate_up(hn2, dg, du):
    n_k = LP // TM_MID

    def body(a_ref, dg_ref, du_ref, o_ref, acc):
        s = pl.program_id(0)
        k = pl.program_id(1)

        @pl.when(k == 0)
        def _():
            acc[...] = jnp.zeros_like(acc)

        @pl.when(s < 2)
        def _():
            acc[...] += lax.dot_general(a_ref[...], dg_ref[...], TN_DIMS, preferred_element_type=F32)

        @pl.when(s >= 2)
        def _():
            acc[...] += lax.dot_general(a_ref[...], du_ref[...], TN_DIMS, preferred_element_type=F32)

        @pl.when(k == n_k - 1)
        def _():
            o_ref[0] = acc[...].astype(BF16)

    return pl.pallas_call(
        body, grid=(4, n_k),
        in_specs=[pl.BlockSpec((TM_MID, D_MODEL), lambda s, k: (k, 0)),
                  pl.BlockSpec((TM_MID, FF_T), lambda s, k: (k, jnp.minimum(s, 1))),
                  pl.BlockSpec((TM_MID, FF_T), lambda s, k: (k, jnp.maximum(s - 2, 0)))],
        out_specs=pl.BlockSpec((1, D_MODEL, FF_T), lambda s, k: (s, 0, 0)),
        out_shape=SDS((4, D_MODEL, FF_T), BF16),
        scratch_shapes=[pltpu.VMEM((D_MODEL, FF_T), F32)],
        compiler_params=_cparams(("parallel", "arbitrary")), name="dw_gate_up")(hn2, dg, du)


def _split3(x):
    hi = x.astype(BF16)
    r1 = x - hi.astype(F32)
    mid = r1.astype(BF16)
    lo = (r1 - mid.astype(F32)).astype(BF16)
    return hi, mid, lo


def _tri_matmul(tri, x):
    hi, mid, lo = _split3(x)
    dot = lambda t: jnp.dot(tri, t, preferred_element_type=F32)
    return dot(hi) + dot(mid) + dot(lo)


def _forget_cumsum(f, b_forget_p):
    def body(f_ref, b_ref, cum_ref, carry):
        i = pl.program_id(0)

        @pl.when(i == 0)
        def _():
            carry[...] = jnp.zeros_like(carry)

        z = f_ref[...] + b_ref[...]
        ls = jnp.minimum(z, 0.0) - jnp.log(1.0 + jnp.exp(-jnp.abs(z)))
        rows = i * TM + lax.broadcasted_iota(jnp.int32, (TM, BLK), 0)
        ls = jnp.where(rows >= PAD_ROWS, ls, 0.0)
        r = lax.broadcasted_iota(jnp.int32, (TM, TM), 0)
        c = lax.broadcasted_iota(jnp.int32, (TM, TM), 1)
        tri = (c <= r).astype(BF16)
        cum = _tri_matmul(tri, ls) + carry[...]
        cum_ref[...] = cum
        carry[...] = cum[TM - 1:TM, :]

    return pl.pallas_call(
        body, grid=(NT,),
        in_specs=[pl.BlockSpec((TM, BLK), lambda i: (i, 0)), pl.BlockSpec((1, BLK), lambda i: (0, 0))],
        out_specs=pl.BlockSpec((TM, BLK), lambda i: (i, 0)),
        out_shape=SDS((LP, BLK), F32),
        scratch_shapes=[pltpu.VMEM((1, BLK), F32)],
        compiler_params=_cparams(("arbitrary",)), name="forget_cumsum")(f, b_forget_p)


def _forget_cumsum_bwd(dcum, f, b_forget_p):
    def body(d_ref, f_ref, b_ref, df_ref, db_ref, carry):
        i = pl.program_id(0)

        @pl.when(i == 0)
        def _():
            carry[...] = jnp.zeros_like(carry)
            db_ref[...] = jnp.zeros_like(db_ref)

        blk = NT - 1 - i
        r = lax.broadcasted_iota(jnp.int32, (TM, TM), 0)
        c = lax.broadcasted_iota(jnp.int32, (TM, TM), 1)
        tri = (c >= r).astype(BF16)
        d = d_ref[...]
        dls = _tri_matmul(tri, d) + carry[...]
        carry[...] = dls[0:1, :]
        z = f_ref[...] + b_ref[...]
        rows = blk * TM + lax.broadcasted_iota(jnp.int32, (TM, BLK), 0)
        df = jnp.where(rows >= PAD_ROWS, dls / (1.0 + jnp.exp(z)), 0.0)
        df_ref[...] = df
        db_ref[...] += jnp.sum(df, axis=0, keepdims=True)

    rev = pl.BlockSpec((TM, BLK), lambda i: (NT - 1 - i, 0))
    vec = pl.BlockSpec((1, BLK), lambda i: (0, 0))
    return pl.pallas_call(
        body, grid=(NT,),
        in_specs=[rev, rev, vec],
        out_specs=[rev, vec],
        out_shape=[SDS((LP, BLK), F32), SDS((1, BLK), F32)],
        scratch_shapes=[pltpu.VMEM((1, BLK), F32)],
        compiler_params=_cparams(("arbitrary",)), name="forget_cumsum_bwd")(dcum, f, b_forget_p)


def _lane_half(rows):
    return lax.broadcasted_iota(jnp.int32, (rows, BLK), 1) // HALF


def _fox_valid(qi, kj):
    qrow = qi * TM + lax.broadcasted_iota(jnp.int32, (TM, TM), 0)
    krow = kj * TM + lax.broadcasted_iota(jnp.int32, (TM, TM), 1)
    return (krow <= qrow) & ((krow >= PAD_ROWS) | (qrow < PAD_ROWS))


class _Rider:
    def __init__(self, operands, out_shapes, sem_counts, first, middle, last):
        self.operands, self.out_shapes, self.sem_counts = list(operands), list(out_shapes), list(sem_counts)
        self.first, self.middle, self.last = first, middle, last

    def scratch(self):
        return [pltpu.SemaphoreType.DMA((k,)) for k in self.sem_counts]

    def split(self, refs, n_in, n_out, n_scratch):
        a, b = len(self.operands), len(self.out_shapes)
        ins, mine_in = refs[:n_in], refs[n_in:n_in + a]
        outs, mine_out = refs[n_in + a:n_in + a + n_out], refs[n_in + a + n_out:n_in + a + n_out + b]
        rest = refs[n_in + a + n_out + b:]
        return ins, outs, rest[:n_scratch], (mine_in, mine_out, rest[n_scratch:])

    def at_steps(self, mine, is_first, is_middle, is_last):
        for cond, fn in ((is_first, self.first), (is_middle, self.middle), (is_last, self.last)):
            pl.when(cond)(lambda fn=fn: fn(*mine))


HBM_SPEC = pl.BlockSpec(memory_space=pltpu.HBM)


N_AUG = 4
QCHUNKS = ((0, 128), (128, 128), (256, 128))
KSUB = 384
AHEAD = 5
AHEAD_BWD = 1


def _fox_prep(proj, cum):
    def body(q0_ref, q1_ref, k0_ref, k1_ref, v0_ref, v1_ref, c_ref, qa_ref, ka_ref, vt_ref):
        half = _lane_half(TM)
        lane = lax.broadcasted_iota(jnp.int32, (TM, BLK), 1)
        for pp in range(4):
            cols = slice(pp * BLK, (pp + 1) * BLK)
            q_ref, k_ref, v_ref = ((q0_ref, k0_ref, v0_ref), (q1_ref, k1_ref, v1_ref))[pp // 2]
            part = slice((pp % 2) * BLK, (pp % 2 + 1) * BLK)
            qs = q_ref[:, part].astype(F32) * SCALE
            kp = k_ref[:, part].astype(F32)
            vp = v_ref[:, part]
            vt_ref[cols, :] = vp.astype(F32).T.astype(BF16)
            for e in range(2):
                h = 2 * pp + e
                a = (1 - e) * HALF
                blk = slice(h * BLK, (h + 1) * BLK)
                hi, mid, lo = _split3(-c_ref[:, h:h + 1])
                q_aug = jnp.where(half == e, qs, jnp.where((lane >= a) & (lane < a + 3), 1.0, 0.0))
                k_aug = jnp.where(half == e, kp, jnp.where(
                    lane == a, hi.astype(F32), jnp.where(lane == a + 1, mid.astype(F32), jnp.where(
                        lane == a + 2, lo.astype(F32), jnp.where(lane == a + 3, 1.0, 0.0)))))
                qa_ref[blk, :] = q_aug.T.astype(BF16)
                ka_ref[:, blk] = k_aug.astype(BF16)

    row = lambda blk: pl.BlockSpec((TM, W2), lambda i: (i, blk))
    wide = pl.BlockSpec((TM, 1024), lambda i: (i, 0))
    return pl.pallas_call(
        body, grid=(NT,),
        in_specs=[row(QB), row(QB + 1), row(KB), row(KB + 1), row(VB), row(VB + 1),
                  pl.BlockSpec((TM, BLK), lambda i: (i, 0))],
        out_specs=[pl.BlockSpec((1024, TM), lambda i: (0, i)), wide, pl.BlockSpec((512, TM), lambda i: (0, i))],
        out_shape=[SDS((1024, LP), BF16), SDS((LP, 1024), BF16), SDS((512, LP), BF16)],
        compiler_params=_cparams(("parallel",)), name="fox_prep")(proj, proj, proj, proj, proj, proj, cum)


def _over_keys(reduce, x):
    slabs = x.reshape(x.shape[0] // HALF, HALF, x.shape[1])
    return reduce(reduce(slabs, axis=0), axis=0, keepdims=True)


def _fox_valid_t(qi, kj, c, r):
    krow = kj * TM + r * KSUB + lax.broadcasted_iota(jnp.int32, (KSUB, c[1]), 0)
    qrow = qi * TM + c[0] + lax.broadcasted_iota(jnp.int32, (KSUB, c[1]), 1)
    return (krow <= qrow) & ((krow >= PAD_ROWS) | (qrow < PAD_ROWS))


def _fox_fwd(q_aug, k_aug, v_t, rider):
    pairs = [(qi, kj) for qi in range(NT) for kj in range(qi + 1)]
    n_pairs = len(pairs)

    def body(qi_ref, kj_ref, *refs):
        (q_ref, k_ref, vt_ref), (o_ref, lse_ref), (m_s, l_s, acc_s), mine = rider.split(refs, 3, 2, 3)
        n = pl.program_id(0)
        qi = qi_ref[n]
        kj = kj_ref[n]
        rider.at_steps(mine, n == 0, n == n_pairs // 2, n == n_pairs - 1)

        @pl.when(kj == 0)
        def _():
            m_s[...] = jnp.full_like(m_s, NEG)
            l_s[...] = jnp.zeros_like(l_s)
            acc_s[...] = jnp.zeros_like(acc_s)

        def tile(masked):
            steps = [(h, c, r) for h in range(N_HEADS) for c in QCHUNKS for r in range(TM // KSUB)]

            def scores(h, c, r):
                blk = slice(h * BLK, (h + 1) * BLK)
                return jnp.dot(k_ref[r * KSUB:(r + 1) * KSUB, blk], q_ref[blk, c[0]:c[0] + c[1]],
                               preferred_element_type=F32)

            ahead = [scores(*st) for st in steps[:AHEAD]]
            for n, (h, c, r) in enumerate(steps):
                s_t = ahead.pop(0)
                if n + AHEAD < len(steps):
                    ahead.append(scores(*steps[n + AHEAD]))
                cs = slice(c[0], c[0] + c[1])
                if masked:
                    s_t = jnp.where(_fox_valid_t(qi, kj, c, r), s_t, NEG)
                m_prev = m_s[h, :, cs]
                m_new = jnp.maximum(m_prev, _over_keys(jnp.max, s_t))
                p_t = jnp.exp(s_t - m_new)
                alpha = jnp.exp(m_prev - m_new)
                l_s[h, :, cs] = alpha * l_s[h, :, cs] + _over_keys(jnp.sum, p_t)
                m_s[h, :, cs] = m_new
                vt = vt_ref[h * HALF:(h + 1) * HALF, r * KSUB:(r + 1) * KSUB]
                acc_s[h, :, cs] = acc_s[h, :, cs] * alpha + jnp.dot(vt, p_t.astype(BF16),
                                                                    preferred_element_type=F32)

        @pl.when((kj < qi) & (kj > 0))
        def _():
            tile(False)

        @pl.when((kj == qi) | (kj == 0))
        def _():
            tile(True)

        @pl.when(kj == qi)
        def _():
            for pp in range(4):
                both = jnp.concatenate([acc_s[2 * pp] * (1.0 / l_s[2 * pp]),
                                        acc_s[2 * pp + 1] * (1.0 / l_s[2 * pp + 1])], axis=0)
                o_ref[:, pp * BLK:(pp + 1) * BLK] = both.T.astype(BF16)
            for h in range(N_HEADS):
                lse_ref[h] = m_s[h] + jnp.log(l_s[h])

    grid_spec = pltpu.PrefetchScalarGridSpec(
        num_scalar_prefetch=2, grid=(n_pairs,),
        in_specs=[pl.BlockSpec((1024, TM), lambda n, qi, kj: (0, qi[n])),
                  pl.BlockSpec((TM, 1024), lambda n, qi, kj: (kj[n], 0)),
                  pl.BlockSpec((512, TM), lambda n, qi, kj: (0, kj[n]))] + [HBM_SPEC] * len(rider.operands),
        out_specs=[pl.BlockSpec((TM, 512), lambda n, qi, kj: (qi[n], 0)),
                   pl.BlockSpec((N_HEADS, 1, TM), lambda n, qi, kj: (0, 0, qi[n]))]
        + [HBM_SPEC] * len(rider.out_shapes),
        scratch_shapes=[pltpu.VMEM((N_HEADS, 1, TM), F32), pltpu.VMEM((N_HEADS, 1, TM), F32),
                        pltpu.VMEM((N_HEADS, HALF, TM), F32)] + rider.scratch())
    o_b, lse, *carried = pl.pallas_call(
        body, grid_spec=grid_spec,
        out_shape=[SDS((LP, 512), BF16), SDS((N_HEADS, 1, LP), F32)] + rider.out_shapes,
        compiler_params=_cparams(("arbitrary",)), name="fox_fwd",
    )(jnp.asarray([p[0] for p in pairs], jnp.int32), jnp.asarray([p[1] for p in pairs], jnp.int32),
      q_aug, k_aug, v_t, *rider.operands)
    return o_b, lse, carried


def _fox_bwd(proj, o_b, dmix, lse, ck_t, rider):
    pairs = [(kj, qi) for kj in range(NT) for qi in range(kj, NT)]
    n_pairs = len(pairs)

    def body(kj_ref, qi_ref, *refs):
        ((q0_ref, q1_ref, k0_ref, k1_ref, v0_ref, v1_ref, o_ref, do_ref, lse_ref, ck_ref),
         (dq_ref, dk_ref, dv_ref, dck_ref, dcq_ref), (dk_s, dv_s, dck_s), mine) = rider.split(refs, 10, 5, 3)
        n = pl.program_id(0)
        kj = kj_ref[n]
        qi = qi_ref[n]
        rider.at_steps(mine, n == 0, n == n_pairs // 2, n == n_pairs - 1)

        @pl.when(n == 0)
        def _():
            dq_ref[...] = jnp.zeros_like(dq_ref)
            dcq_ref[...] = jnp.zeros_like(dcq_ref)

        @pl.when(qi == kj)
        def _():
            dk_s[...] = jnp.zeros_like(dk_s)
            dv_s[...] = jnp.zeros_like(dv_s)
            dck_s[...] = jnp.zeros_like(dck_s)

        def tile(masked):
            valid = _fox_valid(qi, kj) if masked else None
            half = _lane_half(TM)
            q0 = pl.multiple_of(qi * TM, TM)
            lane = lax.broadcasted_iota(jnp.int32, (TM, BLK), 1)
            row_sums = jnp.zeros((TM, BLK), F32)
            pair_ops = {}

            def operands(pp):
                if pp not in pair_ops:
                    cols = slice(pp * BLK, (pp + 1) * BLK)
                    q_ref, k_ref, v_ref = ((q0_ref, k0_ref, v0_ref), (q1_ref, k1_ref, v1_ref))[pp // 2]
                    part = slice((pp % 2) * BLK, (pp % 2 + 1) * BLK)
                    pair_ops[pp] = ((q_ref[:, part].astype(F32) * SCALE).astype(BF16), k_ref[:, part],
                                    v_ref[:, part], do_ref[:, cols])
                return pair_ops[pp]

            def scores(pp, e):
                qs, kp, vp, dop = operands(pp)
                ke = jnp.where(half == e, kp, jnp.zeros_like(kp))
                ve = jnp.where(half == e, vp, jnp.zeros_like(vp))
                return (lax.dot_general(qs, ke, NT_DIMS, preferred_element_type=F32),
                        lax.dot_general(dop, ve, NT_DIMS, preferred_element_type=F32), ke)

            steps = [(pp, e) for pp in range(4) for e in range(2)]
            ahead = [scores(*st) for st in steps[:AHEAD_BWD]]
            for n, (pp, e) in enumerate(steps):
                raw, dp, ke = ahead.pop(0)
                if n + AHEAD_BWD < len(steps):
                    ahead.append(scores(*steps[n + AHEAD_BWD]))
                h = 2 * pp + e
                cols = slice(pp * BLK, (pp + 1) * BLK)
                qs, kp, vp, dop = operands(pp)
                if e == 0:
                    prod = dop.astype(F32) * o_ref[:, cols].astype(F32)
                    d0 = jnp.sum(jnp.where(half == 0, prod, 0.0), axis=1, keepdims=True)
                    d1 = jnp.sum(prod, axis=1, keepdims=True) - d0
                    dq = jnp.zeros((TM, BLK), F32)
                    dks, dvs = [], []
                t = raw - ck_ref[h] - lse_ref[h]
                if masked:
                    t = jnp.where(valid, t, NEG)
                p = jnp.exp(t)
                ds = p * (dp - (d0 if e == 0 else d1))
                dck_s[h] += jnp.sum(ds, axis=0, keepdims=True)
                row_sums = jnp.where(lane == h, jnp.sum(ds, axis=1, keepdims=True), row_sums)
                ds_b = ds.astype(BF16)
                dq = dq + jnp.dot(ds_b, ke, preferred_element_type=F32)
                dks.append(lax.dot_general(ds_b, qs, TN_DIMS, preferred_element_type=F32))
                dvs.append(lax.dot_general(p.astype(BF16), dop, TN_DIMS, preferred_element_type=F32))
                if e == 1:
                    dq_ref[pl.ds(q0, TM), cols] += dq
                    dk_s[pp] += jnp.where(half == 0, dks[0], dks[1])
                    dv_s[pp] += jnp.where(half == 0, dvs[0], dvs[1])
            dcq_ref[pl.ds(q0, TM), :] += row_sums

        @pl.when((qi > kj) & (kj > 0))
        def _():
            tile(False)

        @pl.when((qi == kj) | (kj == 0))
        def _():
            tile(True)

        @pl.when(qi == NT - 1)
        def _():
            for pp in range(4):
                cols = slice(pp * BLK, (pp + 1) * BLK)
                dk_ref[:, cols] = dk_s[pp].astype(BF16)
                dv_ref[:, cols] = dv_s[pp].astype(BF16)
            dck_ref[...] = dck_s[...]

    qrow = lambda blk, w=512: pl.BlockSpec((TM, w), lambda n, kj, qi: (qi[n], blk))
    krow = lambda blk: pl.BlockSpec((TM, W2), lambda n, kj, qi: (kj[n], blk))
    grid_spec = pltpu.PrefetchScalarGridSpec(
        num_scalar_prefetch=2, grid=(n_pairs,),
        in_specs=[qrow(QB, W2), qrow(QB + 1, W2), krow(KB), krow(KB + 1), krow(VB), krow(VB + 1), qrow(0), qrow(1),
                  pl.BlockSpec((N_HEADS, TM, 1), lambda n, kj, qi: (0, qi[n], 0)),
                  pl.BlockSpec((N_HEADS, 1, TM), lambda n, kj, qi: (0, 0, kj[n]))] + [HBM_SPEC] * len(rider.operands),
        out_specs=[pl.BlockSpec((LP, 512), lambda n, kj, qi: (0, 0)),
                   pl.BlockSpec((TM, 512), lambda n, kj, qi: (kj[n], 0)),
                   pl.BlockSpec((TM, 512), lambda n, kj, qi: (kj[n], 0)),
                   pl.BlockSpec((N_HEADS, 1, TM), lambda n, kj, qi: (0, 0, kj[n])),
                   pl.BlockSpec((LP, BLK), lambda n, kj, qi: (0, 0))] + [HBM_SPEC] * len(rider.out_shapes),
        scratch_shapes=[pltpu.VMEM((4, TM, BLK), F32), pltpu.VMEM((4, TM, BLK), F32),
                        pltpu.VMEM((N_HEADS, 1, TM), F32)] + rider.scratch())
    dq, dk, dv, dck, dcq, *carried = pl.pallas_call(
        body, grid_spec=grid_spec,
        out_shape=[SDS((LP, 512), F32), SDS((LP, 512), BF16), SDS((LP, 512), BF16), SDS((N_HEADS, 1, LP), F32),
                   SDS((LP, BLK), F32)] + rider.out_shapes,
        compiler_params=_cparams(("arbitrary",)), name="fox_bwd",
    )(jnp.asarray([p[0] for p in pairs], jnp.int32), jnp.asarray([p[1] for p in pairs], jnp.int32),
      proj, proj, proj, proj, proj, proj, o_b, dmix, lse, ck_t, *rider.operands)
    return dq, dk, dv, dck, dcq, carried


N_SEG = 3
N_KEY = N_SEG * BLK
GROUP = 4
QW = GROUP * BLK


def _bucket_tables_t():
    return np.ascontiguousarray(_bucket_tables().transpose(0, 2, 1))


def _stack_heads(ref, g, scale):
    half = _lane_half(BLK)
    out = []
    for pair in range(2):
        x = ref[:, (2 * g + pair) * BLK:(2 * g + pair + 1) * BLK].astype(F32) * scale
        swapped = pltpu.roll(x, HALF, 1)
        for e in range(2):
            out.append(jnp.where(half == g, x if e == g else swapped, 0.0).astype(BF16))
    return jnp.concatenate(out, axis=0)


def _unstack_heads(x_t, g, ref, scale):
    for pair in range(2):
        both = jnp.concatenate([x_t[:, (2 * pair) * BLK:(2 * pair + 1) * BLK],
                                x_t[:, (2 * pair + 1) * BLK:(2 * pair + 2) * BLK]], axis=0)
        ref[:, (2 * g + pair) * BLK:(2 * g + pair + 1) * BLK] = (both.T * scale).astype(ref.dtype)


def _swa_tables(tab_ref, sink_ref, bkt_ref, tbl, sink_row):
    kk = lax.broadcasted_iota(jnp.int32, (BLK, BLK), 0)
    qq = lax.broadcasted_iota(jnp.int32, (BLK, BLK), 1)
    neg = jnp.full((BLK, BLK), NEG, F32)
    lane = lax.broadcasted_iota(jnp.int32, (1, QW), 1) // BLK
    for g in range(2):
        row = jnp.zeros((1, QW), F32)
        for hh in range(GROUP):
            h = GROUP * g + hh
            cols = slice(hh * BLK, (hh + 1) * BLK)
            row = jnp.where(lane == hh, sink_ref[0, h], row)

            def step(b, carry, h=h):
                t = tab_ref[b, h]
                return jnp.where(bkt_ref[0] == b, t, carry[0]), jnp.where(bkt_ref[1] == b, t, carry[1])
            zero = jnp.zeros((BLK, BLK), F32)
            cur, prev = lax.fori_loop(0, N_BUCKETS, step, (zero, zero))
            far = jnp.full((BLK, BLK), tab_ref[N_BUCKETS - 1, h], F32)
            causal = jnp.where(kk <= qq, cur, neg)
            segments = [
                (neg, neg, jnp.where(kk >= PAD_ROWS, causal, neg)),
                (jnp.where(kk >= PAD_ROWS, prev, neg), neg, causal),
                (jnp.where(kk >= PAD_ROWS, far, neg), jnp.where(kk > qq, prev, neg), causal)]
            for case in range(3):
                for seg in range(N_SEG):
                    tbl[case, g, seg * BLK:(seg + 1) * BLK, cols] = segments[case][seg]
        sink_row[g] = row


def _swa_prep(proj):
    rows = LP // 3

    def body(k_ref, v_ref, kt_ref, vt_ref):
        kt_ref[...] = k_ref[...].astype(F32).T.astype(BF16)
        vt_ref[...] = v_ref[...].astype(F32).T.astype(BF16)

    col = pl.BlockSpec((BLK, rows), lambda i: (0, i))
    return pl.pallas_call(
        body, grid=(3,),
        in_specs=[pl.BlockSpec((rows, BLK), lambda i: (i, KA)), pl.BlockSpec((rows, BLK), lambda i: (i, VA))],
        out_specs=[col, col], out_shape=[SDS((BLK, LP), BF16)] * 2,
        compiler_params=_cparams(("parallel",)), name="swa_prep")(proj, proj)


def _segments(ref, i, by_rows):
    starts = [0, pl.multiple_of(jnp.maximum(i - 1, 0) * BLK, BLK), pl.multiple_of(i * BLK, BLK)]
    if by_rows:
        return jnp.concatenate([ref[pl.ds(s, BLK), :] for s in starts], axis=0)
    return jnp.concatenate([ref[:, pl.ds(s, BLK)] for s in starts], axis=1)


def _swa_fwd(proj, vt_a, rel_bias, sinks, bkt_t, rider):
    def body(*refs):
        ((tab_ref, sink_ref, bkt_ref, q_ref, k_ref, vt_ref), (o_ref, lse_ref),
         (tbl, sink_row), mine) = rider.split(refs, 6, 2, 2)
        i = pl.program_id(0)
        rider.at_steps(mine, i == 0, i == NBLK // 2, i == NBLK - 1)

        @pl.when(i == 0)
        def _():
            _swa_tables(tab_ref, sink_ref, bkt_ref, tbl, sink_row)

        case = jnp.minimum(i, 2)
        k_cat = _segments(k_ref, i, True)
        vt_cat = _segments(vt_ref, i, False)
        raw = [lax.dot_general(k_cat, _stack_heads(q_ref, g, SCALE), NT_DIMS, preferred_element_type=F32)
               for g in range(2)]
        for g in range(2):
            s_t = raw[g] + tbl[case, g]
            sink = sink_row[g]
            m = jnp.maximum(_over_keys(jnp.max, s_t), sink)
            p_t = jnp.exp(s_t - m)
            l = _over_keys(jnp.sum, p_t) + jnp.exp(sink - m)
            o_t = jnp.dot(vt_cat[g * HALF:(g + 1) * HALF, :], p_t.astype(BF16), preferred_element_type=F32)
            _unstack_heads(o_t * (1.0 / l), g, o_ref, 1.0)
            lse = m + jnp.log(l)
            for hh in range(GROUP):
                lse_ref[GROUP * g + hh] = lse[:, hh * BLK:(hh + 1) * BLK]

    smem = pl.BlockSpec(memory_space=pltpu.SMEM)
    o_a, lse, *carried = pl.pallas_call(
        body, grid=(NBLK,),
        in_specs=[smem, smem, pl.BlockSpec((2, BLK, BLK), lambda i: (0, 0, 0)),
                  pl.BlockSpec((BLK, 512), lambda i: (i, QA)), pl.BlockSpec((LP, BLK), lambda i: (0, KA)),
                  pl.BlockSpec((BLK, LP), lambda i: (0, 0))] + [HBM_SPEC] * len(rider.operands),
        out_specs=[pl.BlockSpec((BLK, 512), lambda i: (i, 0)),
                   pl.BlockSpec((N_HEADS, 1, BLK), lambda i: (0, 0, i))] + [HBM_SPEC] * len(rider.out_shapes),
        out_shape=[SDS((LP, 512), BF16), SDS((N_HEADS, 1, LP), F32)] + rider.out_shapes,
        scratch_shapes=[pltpu.VMEM((3, 2, N_KEY, QW), F32), pltpu.VMEM((2, 1, QW), F32)] + rider.scratch(),
        compiler_params=_cparams(("arbitrary",)), name="swa_fwd",
    )(rel_bias, sinks, bkt_t, proj, proj, vt_a, *rider.operands)
    return o_a, lse, carried


def _swa_bwd(proj, kt_a, o_a, dmix, lse, rel_bias, sinks, bkt_t):
    def body(tab_ref, sink_ref, bkt_ref, q_ref, k_ref, v_ref, kt_ref, o_ref, do_ref, lse_ref,
             dq_ref, dk_ref, dv_ref, dbias_ref, dsink_ref, tbl, sink_row, acc, dsk):
        i = pl.program_id(0)

        @pl.when(i == 0)
        def _():
            _swa_tables(tab_ref, sink_ref, bkt_ref, tbl, sink_row)
            dk_ref[...] = jnp.zeros_like(dk_ref)
            dv_ref[...] = jnp.zeros_like(dv_ref)
            acc[...] = jnp.zeros_like(acc)
            dsk[...] = jnp.zeros_like(dsk)

        case = jnp.minimum(i, 2)
        first = jnp.full((BLK, QW), i, jnp.int32) == 1
        k_cat = _segments(k_ref, i, True)
        v_cat = _segments(v_ref, i, True)
        kt_cat = _segments(kt_ref, i, False)
        dk_cat = jnp.zeros((N_KEY, BLK), F32)
        dv_cat = jnp.zeros((N_KEY, BLK), F32)
        for g in range(2):
            d_parts = []
            for pair in range(2):
                cols = slice((2 * g + pair) * BLK, (2 * g + pair + 1) * BLK)
                prod_t = (do_ref[:, cols].astype(F32) * o_ref[:, cols].astype(F32)).T
                d_parts += [jnp.sum(prod_t[:HALF], axis=0, keepdims=True),
                            jnp.sum(prod_t[HALF:], axis=0, keepdims=True)]
            d_row = jnp.concatenate(d_parts, axis=1)
            lse_row = jnp.concatenate([lse_ref[GROUP * g + hh] for hh in range(GROUP)], axis=1)
            q_st = _stack_heads(q_ref, g, SCALE)
            do_st = _stack_heads(do_ref, g, 1.0)
            s_t = lax.dot_general(k_cat, q_st, NT_DIMS, preferred_element_type=F32) + tbl[case, g]
            p_t = jnp.exp(s_t - lse_row)
            dp_t = lax.dot_general(v_cat, do_st, NT_DIMS, preferred_element_type=F32)
            ds_t = p_t * (dp_t - d_row)
            dsk[g] += -jnp.exp(sink_row[g] - lse_row) * d_row
            acc[g, 0:BLK] += jnp.where(first, 0.0, ds_t[0:BLK])
            acc[g, BLK:2 * BLK] += jnp.where(first, ds_t[0:BLK], ds_t[BLK:2 * BLK])
            acc[g, 2 * BLK:N_KEY] += ds_t[2 * BLK:N_KEY]
            ds_b = ds_t.astype(BF16)
            dk_cat = dk_cat + jnp.dot(ds_b, q_st, preferred_element_type=F32)
            dv_cat = dv_cat + jnp.dot(p_t.astype(BF16), do_st, preferred_element_type=F32)
            dq_t = jnp.dot(kt_cat[g * HALF:(g + 1) * HALF, :], ds_b, preferred_element_type=F32)
            _unstack_heads(dq_t, g, dq_ref, SCALE)

        prev0 = pl.multiple_of(jnp.maximum(i - 1, 0) * BLK, BLK)
        cur0 = pl.multiple_of(i * BLK, BLK)
        for ref, cat in ((dk_ref, dk_cat), (dv_ref, dv_cat)):
            ref[0:BLK, :] += cat[0:BLK]
            ref[pl.ds(prev0, BLK), :] += cat[BLK:2 * BLK]
            ref[pl.ds(cur0, BLK), :] += cat[2 * BLK:N_KEY]

        @pl.when(i == NBLK - 1)
        def _():
            lane = lax.broadcasted_iota(jnp.int32, (1, BLK), 1)

            def per_bucket(b, carry):
                row = jnp.zeros((1, BLK), F32)
                for h in range(N_HEADS):
                    g, cols = h // GROUP, slice((h % GROUP) * BLK, (h % GROUP + 1) * BLK)
                    val = (jnp.sum(jnp.where(bkt_ref[0] == b, acc[g, 2 * BLK:N_KEY, cols], 0.0), keepdims=True)
                           + jnp.sum(jnp.where(bkt_ref[1] == b, acc[g, BLK:2 * BLK, cols], 0.0), keepdims=True))
                    row = jnp.where(lane == h, val, row)
                dbias_ref[pl.ds(b, 1), :] = row
                return carry

            lax.fori_loop(0, N_BUCKETS, per_bucket, 0)
            far = jnp.zeros((1, BLK), F32)
            dsr = jnp.zeros((1, BLK), F32)
            for h in range(N_HEADS):
                g, cols = h // GROUP, slice((h % GROUP) * BLK, (h % GROUP + 1) * BLK)
                far = jnp.where(lane == h, jnp.sum(acc[g, 0:BLK, cols], keepdims=True), far)
                dsr = jnp.where(lane == h, jnp.sum(dsk[g, :, cols], keepdims=True), dsr)
            dbias_ref[N_BUCKETS - 1:N_BUCKETS, :] += far
            dsink_ref[...] = dsr

    smem = pl.BlockSpec(memory_space=pltpu.SMEM)
    blk512 = lambda col: pl.BlockSpec((BLK, 512), lambda i: (i, col))
    full = lambda r, c: pl.BlockSpec((r, c), lambda i: (0, 0))
    return pl.pallas_call(
        body, grid=(NBLK,),
        in_specs=[smem, smem, pl.BlockSpec((2, BLK, BLK), lambda i: (0, 0, 0)), blk512(QA),
                  pl.BlockSpec((LP, BLK), lambda i: (0, KA)), pl.BlockSpec((LP, BLK), lambda i: (0, VA)),
                  full(BLK, LP), blk512(0), blk512(0), pl.BlockSpec((N_HEADS, 1, BLK), lambda i: (0, 0, i))],
        out_specs=[blk512(0), full(LP, BLK), full(LP, BLK), full(N_BUCKETS, BLK), full(1, BLK)],
        out_shape=[SDS((LP, 512), BF16), SDS((LP, BLK), F32), SDS((LP, BLK), F32),
                   SDS((N_BUCKETS, BLK), F32), SDS((1, BLK), F32)],
        scratch_shapes=[pltpu.VMEM((3, 2, N_KEY, QW), F32), pltpu.VMEM((2, 1, QW), F32),
                        pltpu.VMEM((2, N_KEY, QW), F32), pltpu.VMEM((2, 1, QW), F32)],
        compiler_params=_cparams(("arbitrary",)), name="swa_bwd",
    )(rel_bias, sinks, bkt_t, proj, proj, proj, kt_a, o_a, dmix, lse)


def _local_step(x, tgt, meta, rel_bias, g_pre_mix, g_post_mix, g_pre_ffn, g_post_ffn, b_forget, sinks,
                w_in_b, out_rider, out_weight, ffn_rider, ffn_weights, early_grads):
    bkt_t = jnp.asarray(_bucket_tables_t())
    h0 = jnp.concatenate([jnp.zeros((PAD_ROWS, D_MODEL), F32), meta, x], axis=0)
    b_p = jnp.pad(b_forget, ((0, 0), (0, BLK - N_HEADS)))

    hn1, proj, f = _pre_mix(h0, g_pre_mix, w_in_b)
    kt_a, vt_a = _swa_prep(proj)
    o_a, lse_a, carried_out = _swa_fwd(proj, vt_a, rel_bias, sinks, bkt_t, out_rider)
    w_out_b = out_weight(carried_out)
    cum = _forget_cumsum(f, b_p)
    ck_t = cum[:, :N_HEADS].T.reshape(N_HEADS, 1, LP)
    q_aug, k_aug, v_t = _fox_prep(proj, cum)
    o_b, lse_row, carried = _fox_fwd(q_aug, k_aug, v_t, ffn_rider)
    lse_b = lse_row.reshape(N_HEADS, LP, 1)
    w_gu_b, w_dn_b = ffn_weights(carried)
    a, h1, hn2 = _attn_out(o_a, o_b, w_out_b, h0, g_post_mix, g_pre_ffn)
    g, u, act = _ffn_up(hn2, w_gu_b)
    dff, dy, loss_blk, dg_post_ffn = _ffn_down_loss(act, w_dn_b, h1, tgt, g_post_ffn)

    dw_dn = _mm_tn([act], dff, FF_T, "dw_down", BF16)
    dg, du = _ffn_down_bwd(dff, w_dn_b, g, u)
    dw_gu = _dw_gate_up(hn2, dg, du)
    dh1, da, dg_pre_ffn, dg_post_mix = _ffn_up_bwd(dg, du, w_gu_b, h1, a, dy, g_pre_ffn, g_post_mix)
    dw_out = _mm_tn([o_a, o_b], da, D_MODEL, "dw_out", BF16)
    dmix = _attn_out_bwd(da, w_out_b)
    dq_b, dk_b, dv_b, dck, dcq, landed = _fox_bwd(proj, o_b, dmix, lse_b, ck_t, early_grads(dw_gu, dw_dn, dw_out))
    dq_a, dk_a, dv_a, dbias, dsink = _swa_bwd(proj, kt_a, o_a, dmix, lse_a, rel_bias, sinks, bkt_t)
    dcum = dcq - jnp.pad(dck.reshape(N_HEADS, LP).T, ((0, 0), (0, BLK - N_HEADS)))
    df, db = _forget_cumsum_bwd(dcum, f, b_p)
    dproj, dh0, dg_pre_mix = _pre_mix_bwd(dq_a, dq_b, dk_b, dv_b, dk_a, dv_a, df, w_in_b, h0, dh1, g_pre_mix)
    dw_in = _mm_tn([hn1], dproj, D_MODEL, "dw_in", BF16)

    return dict(loss=loss_blk[0, 0], grad_x=dh0[ROW0:], meta=dh0[PAD_ROWS:ROW0],
                rel_bias=dbias[:, :N_HEADS], ln_pre_mix=dg_pre_mix, ln_post_mix=dg_post_mix,
                ln_pre_ffn=dg_pre_ffn, ln_post_ffn=dg_post_ffn, b_forget=db[:, :N_HEADS],
                sinks=dsink[:, :N_HEADS], w_in=dw_in, w_out=dw_out, w_gate_up=dw_gu, w_down=dw_dn,
                landed=landed)


N_SMALL = 24
LOSS_ROW = 6


def _place():
    x, y, c = lax.axis_index("x"), lax.axis_index("y"), lax.axis_index("c")
    return x, y, c, [(1 - x, y), (x, 1 - y), (1 - x, 1 - y)]


def _run_alone(rider, name):
    a, b = len(rider.operands), len(rider.out_shapes)

    def body(*refs):
        mine = (refs[:a], refs[a:a + b], refs[a + b:])
        rider.first(*mine)
        rider.middle(*mine)
        rider.last(*mine)

    return pl.pallas_call(body, in_specs=[HBM_SPEC] * a, out_specs=[HBM_SPEC] * b, out_shape=rider.out_shapes,
                          scratch_shapes=rider.scratch(), name=name)(*rider.operands)


def _gather_rider(shards, own_too, by_columns=()):
    n = len(shards)

    def slot(a, outs, chip, h):
        if a in by_columns:
            cols = shards[a].shape[2]
            return outs[a].at[h, :, pl.ds(pl.multiple_of(chip * cols, BLK), cols)]
        return outs[a].at[chip, h]

    def own_copies(ins, outs, sems):
        x, y, _, _ = _place()
        if not own_too:
            return []
        return [pltpu.make_async_copy(ins[a].at[h], slot(a, outs, 2 * x + y, h), sems[2].at[2 * a + h])
                for a in range(n) for h in range(2)]

    def copies(ins, outs, sems):
        send_sems, recv_sems = sems[:2]
        x, y, c, others = _place()
        chip = 2 * x + y
        sibling = (x, y, 1 - c)

        def rc(a, k, src, dst, to):
            return pltpu.make_async_remote_copy(src_ref=src, dst_ref=dst, send_sem=send_sems.at[6 * a + k],
                                                recv_sem=recv_sems.at[6 * a + k], device_id=to, device_id_type=MESH)

        pairs = [(a, j, ox, oy) for a in range(n) for j, (ox, oy) in enumerate(others)]
        there = lambda a, ox, oy, h: slot(a, outs, 2 * ox + oy, h)
        return dict(
            sent=lambda: [rc(a, j, ins[a].at[c], slot(a, outs, chip, c), (ox, oy, c)) for a, j, ox, oy in pairs],
            landed=lambda: [rc(a, j, there(a, ox, oy, c), there(a, ox, oy, c), sibling) for a, j, ox, oy in pairs],
            passed=lambda: [rc(a, 3 + j, there(a, ox, oy, c), there(a, ox, oy, c), sibling)
                            for a, j, ox, oy in pairs],
            arriving=lambda: [rc(a, 3 + j, there(a, ox, oy, 1 - c), there(a, ox, oy, 1 - c), sibling)
                              for a, j, ox, oy in pairs])

    def first(*mine):
        for cp in copies(*mine)["sent"]() + own_copies(*mine):
            cp.start()

    def middle(*mine):
        kinds = copies(*mine)
        for got, cp in zip(kinds["landed"](), kinds["passed"]()):
            got.wait_recv()
            cp.start()

    def last(*mine):
        kinds = copies(*mine)
        for cp in kinds["arriving"]():
            cp.wait_recv()
        for cp in kinds["sent"]() + kinds["passed"]():
            cp.wait_send()
        for cp in own_copies(*mine):
            cp.wait()

    shapes = [SDS((2, s.shape[1], 4 * s.shape[2]) if a in by_columns else (4,) + s.shape, s.dtype)
              for a, s in enumerate(shards)]
    return _Rider(shards, shapes, [6 * n, 6 * n] + [2 * n] * own_too, first, middle, last)


def _swap_rider(grads):
    n = len(grads)
    slabs = [(a, s) for a in range(n) for s in range(grads[a].shape[0])]

    def copies(ins, outs, sems):
        x, y, c, _ = _place()
        return [pltpu.make_async_remote_copy(
            src_ref=ins[a].at[s, 1 - c], dst_ref=outs[a].at[s], send_sem=sems[0].at[k], recv_sem=sems[1].at[k],
            device_id=(x, y, 1 - c), device_id_type=MESH) for k, (a, s) in enumerate(slabs)]

    def first(*mine):
        for cp in copies(*mine):
            cp.start()

    def middle(*mine):
        pass

    def last(*mine):
        for cp in copies(*mine):
            cp.wait()

    return _Rider(grads, [SDS(g.shape[:1] + g.shape[2:], g.dtype) for g in grads], [len(slabs), len(slabs)],
                  first, middle, last)


def _pair_sum(g, got, c_arr, name):
    n_s, rh, cc = got.shape

    def body(c_ref, g_ref, p_ref, o_ref):
        o_ref[0] = (g_ref[0, 0].astype(F32) + p_ref[0].astype(F32)).astype(BF16)

    grid_spec = pltpu.PrefetchScalarGridSpec(
        num_scalar_prefetch=1, grid=(n_s,),
        in_specs=[pl.BlockSpec((1, 1, rh, cc), lambda s, c_ref: (s, c_ref[0], 0, 0)),
                  pl.BlockSpec((1, rh, cc), lambda s, c_ref: (s, 0, 0))],
        out_specs=pl.BlockSpec((1, rh, cc), lambda s, c_ref: (s, 0, 0)))
    return pl.pallas_call(body, grid_spec=grid_spec, out_shape=SDS((n_s, rh, cc), BF16),
                          compiler_params=_cparams(("parallel",)), name=name)(c_arr, g, got)


def _direct_rider(grads):
    n = len(grads)

    def copies(ins, outs, sems):
        x, y, c, others = _place()
        peers = [(x, y, 1 - c)] + [(ox, oy, c) for ox, oy in others] + [(ox, oy, 1 - c) for ox, oy in others]
        return [pltpu.make_async_remote_copy(
            src_ref=ins[a].at[2 * px + py, pc], dst_ref=outs[a].at[k], send_sem=sems[0].at[7 * a + k],
            recv_sem=sems[1].at[7 * a + k], device_id=(px, py, pc), device_id_type=MESH)
            for a in range(n) for k, (px, py, pc) in enumerate(peers)]

    def first(*mine):
        for cp in copies(*mine):
            cp.start()

    def middle(*mine):
        pass

    def last(*mine):
        for cp in copies(*mine):
            cp.wait()

    return _Rider(grads, [SDS((7,) + g.shape[2:], g.dtype) for g in grads], [7 * n, 7 * n], first, middle, last)


def _owner_sum(grads, landed, own_arr, after, name):
    rh, cc = landed.shape[1:]
    tr = rh // 2

    def body(own_ref, g_ref, p_ref, after_ref, o_ref):
        total = g_ref[0, 0].astype(F32)
        for k in range(7):
            total = total + p_ref[k].astype(F32)
        o_ref[...] = total

    grid_spec = pltpu.PrefetchScalarGridSpec(
        num_scalar_prefetch=1, grid=(2,),
        in_specs=[pl.BlockSpec((1, 1, tr, cc), lambda i, own: (own[0], own[1], i, 0)),
                  pl.BlockSpec((7, tr, cc), lambda i, own: (0, i, 0)), pl.BlockSpec(memory_space=pl.ANY)],
        out_specs=pl.BlockSpec((tr, cc), lambda i, own: (i, 0)))
    return pl.pallas_call(body, grid_spec=grid_spec, out_shape=SDS((rh, cc), F32),
                          compiler_params=_cparams(("parallel",)), name=name)(own_arr, grads, landed, after)


SEM_SPEC = pl.BlockSpec(memory_space=pltpu.SEMAPHORE)
N_LATE = 10


def _late_copies(part_ref, landed_ref, small_ref, all_ref, send_sems, recv_sems):
    x, y, c, others = _place()
    me = 4 * x + 2 * y + c
    peers = [(x, y, 1 - c)] + [(ox, oy, c) for ox, oy in others] + [(ox, oy, 1 - c) for ox, oy in others]
    big = [pltpu.make_async_remote_copy(
        src_ref=part_ref.at[2 * ox + oy], dst_ref=landed_ref.at[j], send_sem=send_sems.at[j], recv_sem=recv_sems.at[j],
        device_id=(ox, oy, c), device_id_type=MESH) for j, (ox, oy) in enumerate(others)]
    small = [pltpu.make_async_remote_copy(
        src_ref=small_ref, dst_ref=all_ref.at[me], send_sem=send_sems.at[3 + k], recv_sem=recv_sems.at[3 + k],
        device_id=peer, device_id_type=MESH) for k, peer in enumerate(peers)]
    return big + small


def _late_exchange_start(part, small):
    def body(part_ref, landed_ref, small_ref, all_ref, send_sems, recv_sems, part_o, landed_o, small_o, all_o, token):
        for cp in _late_copies(part_ref, landed_ref, small_ref, all_ref, send_sems, recv_sems):
            cp.start()
        token[...] = jnp.zeros_like(token)

    hbm = lambda a: pltpu.HBM(a.shape, a.dtype)
    landed = lax.empty((3,) + part.shape[1:], part.dtype)
    everyone = lax.empty((8,) + small.shape, small.dtype)
    operands = [pltpu.with_memory_space_constraint(a, pltpu.HBM) for a in (part, landed, small, everyone)]
    return pl.pallas_call(
        body, name="late_exchange_start",
        out_shape=(pltpu.SemaphoreType.DMA((N_LATE,)), pltpu.SemaphoreType.DMA((N_LATE,)),
                   hbm(part), hbm(landed), hbm(small), hbm(everyone), SDS((8, BLK), F32)),
        in_specs=[HBM_SPEC] * 4,
        out_specs=(SEM_SPEC, SEM_SPEC, HBM_SPEC, HBM_SPEC, HBM_SPEC, HBM_SPEC, pl.BlockSpec(memory_space=pltpu.VMEM)),
        input_output_aliases={0: 2, 1: 3, 2: 4, 3: 5},
        compiler_params=pltpu.CompilerParams(has_side_effects=pltpu.SideEffectType.DATAFLOW_SIDE_EFFECTING),
    )(*operands)


def _late_exchange_wait(send_sems, recv_sems, part, landed, small, everyone, after):
    def body(part_ref, landed_ref, small_ref, all_ref, send_sems, recv_sems, after_ref, part_o, landed_o, small_o, all_o):
        for cp in _late_copies(part_ref, landed_ref, small_ref, all_ref, send_sems, recv_sems):
            cp.wait_send()
            cp.wait_recv()

    hbm = lambda a: pltpu.HBM(a.shape, a.dtype)
    out = pl.pallas_call(
        body, name="late_exchange_wait",
        out_shape=(hbm(part), hbm(landed), hbm(small), hbm(everyone)),
        in_specs=[HBM_SPEC] * 4 + [SEM_SPEC, SEM_SPEC, pl.BlockSpec(memory_space=pl.ANY)],
        out_specs=(HBM_SPEC,) * 4, input_output_aliases={0: 0, 1: 1, 2: 2, 3: 3},
        compiler_params=pltpu.CompilerParams(has_side_effects=pltpu.SideEffectType.DATAFLOW_SIDE_EFFECTING),
    )(part, landed, small, everyone, send_sems, recv_sems, after)
    return out[0], out[1], out[3]


def _chip_sum(parts, landed, chip_arr, name):
    rh, cc = landed.shape[1:]
    tr = rh // 2

    def body(chip_ref, own_ref, p_ref, o_ref):
        o_ref[...] = ((own_ref[0].astype(F32) + p_ref[0].astype(F32)) + p_ref[1].astype(F32)) + p_ref[2].astype(F32)

    grid_spec = pltpu.PrefetchScalarGridSpec(
        num_scalar_prefetch=1, grid=(2,),
        in_specs=[pl.BlockSpec((1, tr, cc), lambda i, chip_ref: (chip_ref[0], i, 0)),
                  pl.BlockSpec((3, tr, cc), lambda i, chip_ref: (0, i, 0))],
        out_specs=pl.BlockSpec((tr, cc), lambda i, chip_ref: (i, 0)))
    return pl.pallas_call(body, grid_spec=grid_spec, out_shape=SDS((rh, cc), F32),
                          compiler_params=_cparams(("parallel",)), name=name)(chip_arr, parts, landed)


def _device_sum(p):
    def body(p_ref, o_ref):
        acc = p_ref[0]
        for k in range(1, 8):
            acc = acc + p_ref[k]
        o_ref[...] = acc

    return pl.pallas_call(body, out_shape=SDS(p.shape[1:], F32), name="small_sum")(p)


def _join_halves(halves, name):
    n = len(halves)

    def body(*refs):
        ins, outs = refs[:n], refs[n:2 * n]
        send_sems, recv_sems = refs[2 * n:]
        x, y, c, _ = _place()
        copies = [pltpu.make_async_remote_copy(
            src_ref=ins[a], dst_ref=outs[a], send_sem=send_sems.at[a], recv_sem=recv_sems.at[a],
            device_id=(x, y, 1 - c), device_id_type=MESH) for a in range(n)]
        for cp in copies:
            cp.start()
        for cp in copies:
            cp.wait()

    return pl.pallas_call(
        body, in_specs=[HBM_SPEC] * n, out_specs=[HBM_SPEC] * n,
        out_shape=[SDS(h.shape, h.dtype) for h in halves],
        scratch_shapes=[pltpu.SemaphoreType.DMA((n,)), pltpu.SemaphoreType.DMA((n,))],
        name=name)(*halves)


def _adamw(w, g, m, v, name, tr=None):
    rows, cols = w.shape
    tr = tr or rows
    assert rows % tr == 0

    def body(w_ref, g_ref, m_ref, v_ref, d_ref, nm_ref, nv_ref):
        gg = g_ref[...]
        nm = ADAM_B1 * m_ref[...] + (1.0 - ADAM_B1) * gg
        nv = ADAM_B2 * v_ref[...] + (1.0 - ADAM_B2) * (gg * gg)
        nm_ref[...] = nm
        nv_ref[...] = nv
        m_hat = nm / (1.0 - ADAM_B1 ** ADAM_STEP)
        v_hat = nv / (1.0 - ADAM_B2 ** ADAM_STEP)
        d_ref[...] = -ADAM_LR * (m_hat / (jnp.sqrt(v_hat) + ADAM_EPS) + ADAM_WD * w_ref[...])

    blk = pl.BlockSpec((tr, cols), lambda i: (i, 0))
    return pl.pallas_call(
        body, grid=(rows // tr,), in_specs=[blk] * 4, out_specs=[blk] * 3,
        out_shape=[SDS((rows, cols), F32)] * 3,
        compiler_params=_cparams(("parallel",)), name=name)(w, g, m, v)


def _adamw_halves(w, mine, theirs, m, v, c_arr, name):
    rows, cols = w.shape
    rh = rows // 2
    tr = rh if rh <= 352 else 256
    nh = rh // tr

    def body(c_ref, w_ref, mine_ref, theirs_ref, m_ref, v_ref, g_ref, d_ref, nm_ref, nv_ref):
        own = jnp.full((tr, cols), pl.program_id(0), jnp.int32) == c_ref[0]
        gg = jnp.where(own, mine_ref[...], theirs_ref[...])
        g_ref[...] = gg
        nm = ADAM_B1 * m_ref[...] + (1.0 - ADAM_B1) * gg
        nv = ADAM_B2 * v_ref[...] + (1.0 - ADAM_B2) * (gg * gg)
        nm_ref[...] = nm
        nv_ref[...] = nv
        m_hat = nm / (1.0 - ADAM_B1 ** ADAM_STEP)
        v_hat = nv / (1.0 - ADAM_B2 ** ADAM_STEP)
        d_ref[...] = -ADAM_LR * (m_hat / (jnp.sqrt(v_hat) + ADAM_EPS) + ADAM_WD * w_ref[...])

    whole = pl.BlockSpec((tr, cols), lambda hh, i, c_ref: (hh * nh + i, 0))
    part = pl.BlockSpec((tr, cols), lambda hh, i, c_ref: (i, 0))
    grid_spec = pltpu.PrefetchScalarGridSpec(
        num_scalar_prefetch=1, grid=(2, nh), in_specs=[whole, part, part, whole, whole], out_specs=[whole] * 4)
    return pl.pallas_call(body, grid_spec=grid_spec, out_shape=[SDS((rows, cols), F32)] * 4,
                          compiler_params=_cparams(("parallel", "parallel")), name=name)(c_arr, w, mine, theirs, m, v)


def _pack_small(pre_mix, post_mix, pre_ffn, post_ffn, rel_bias, b_forget, sinks):
    def at(row, v):
        return jnp.pad(v, ((row, 7 - row), (0, D_MODEL - v.shape[1])))
    return (at(0, pre_mix) + at(1, post_mix) + at(2, pre_ffn) + at(3, post_ffn)
            + at(4, rel_bias.reshape(1, N_BUCKETS * N_HEADS)) + at(5, jnp.concatenate([b_forget, sinks], axis=1)))


def _unpack_small(p):
    return dict(ln_pre_mix=p[0:1], ln_post_mix=p[1:2], ln_pre_ffn=p[2:3], ln_post_ffn=p[3:4],
                rel_bias=p[4, :N_BUCKETS * N_HEADS].reshape(N_BUCKETS, N_HEADS),
                b_forget=p[5:6, 0:N_HEADS], sinks=p[5:6, N_HEADS:2 * N_HEADS])


WEIGHTS = ("meta_tokens", "rel_bias", "ln_pre_mix", "ln_post_mix", "ln_pre_ffn", "ln_post_ffn",
           "w_in", "b_forget", "sinks", "w_out", "w_gate_up", "w_down")


def kernel(x, meta_tokens, rel_bias, ln_pre_mix, ln_post_mix, ln_pre_ffn, ln_post_ffn, w_in, b_forget, sinks, w_out, w_gate_up, w_down, loss_target, m_meta_tokens, m_rel_bias, m_ln_pre_mix, m_ln_post_mix, m_ln_pre_ffn, m_ln_post_ffn, m_w_in, m_b_forget, m_sinks, m_w_out, m_w_gate_up, m_w_down, v_meta_tokens, v_rel_bias, v_ln_pre_mix, v_ln_post_mix, v_ln_pre_ffn, v_ln_post_ffn, v_w_in, v_b_forget, v_sinks, v_w_out, v_w_gate_up, v_w_down):
    xi, yi, ci = lax.axis_index("x"), lax.axis_index("y"), lax.axis_index("c")
    chip = 2 * xi + yi
    c_arr = jnp.reshape(ci, (1,)).astype(jnp.int32)

    def halves(w, dtype):
        return w.astype(dtype).reshape(2, w.shape[0] // 2, w.shape[1])

    def with_own(gathered, shards):
        return [lax.dynamic_update_slice(got, own[None], (chip, 0, 0, 0)) for got, own in zip(gathered, shards)]

    shards = [halves(w_in[0], BF16), halves(meta_tokens, F32)]
    gw_in, g_meta = with_own(_run_alone(_gather_rider(shards, False), "gather_mixer_weights"), shards)
    out_shards = [halves(w_out[0], BF16)]
    ffn_shards = [halves(w_gate_up[0], BF16), halves(w_down[0], BF16)]

    def ffn_weights(carried):
        gw_gu, gw_dn = carried
        return gw_gu.reshape(D_MODEL, 2 * D_FF), gw_dn.reshape(D_FF, D_MODEL)

    early = {}

    def early_grads(dw_gu, dw_dn, dw_out):
        early["grads"] = [dw_out.reshape(4, 2, 128, D_MODEL), dw_gu.reshape(4, 2, 512, FF_T),
                          dw_dn.reshape(4, 2, 352, D_MODEL)]
        return _direct_rider(early["grads"])
    w_in_all = gw_in.reshape(4, D_MODEL, D_PROJ // 4).transpose(1, 0, 2).reshape(D_MODEL, D_PROJ)
    w_in_b = jnp.pad(w_in_all, ((0, 0), (0, D_PROJ_P - D_PROJ)))
    meta_all = g_meta.reshape(4, N_META, D_MODEL // 4).transpose(1, 0, 2).reshape(N_META, D_MODEL)

    loc = _local_step(x[0], loss_target[0], meta_all, rel_bias, ln_pre_mix, ln_post_mix, ln_pre_ffn, ln_post_ffn,
                      b_forget, sinks, w_in_b, _gather_rider(out_shards, True),
                      lambda carried: carried[0].reshape(D_MODEL, D_MODEL),
                      _gather_rider(ffn_shards, True, by_columns=(0,)), ffn_weights, early_grads)

    small = jnp.concatenate(
        [_pack_small(loc["ln_pre_mix"], loc["ln_post_mix"], loc["ln_pre_ffn"], loc["ln_post_ffn"],
                     loc["rel_bias"], loc["b_forget"], loc["sinks"])
         + jnp.pad(loc["loss"].reshape(1, 1), ((LOSS_ROW, 7 - LOSS_ROW), (0, D_MODEL - 1))), loc["meta"]], axis=0)

    dw_in = loc["w_in"].reshape(1, 2, D_MODEL // 2, D_PROJ_P)
    (got_in,) = _run_alone(_swap_rider([dw_in]), "swap_halves_late")
    half_sum = _pair_sum(dw_in, got_in, c_arr, "pair_sum_late")
    part_in = half_sum[0, :, :D_PROJ].reshape(D_MODEL // 2, 4, D_PROJ // 4).transpose(1, 0, 2)
    send_sems, recv_sems, part_sent, landing, small_sent, everyone, token = _late_exchange_start(part_in, small)
    chip_arr = jnp.reshape(chip, (1,)).astype(jnp.int32)
    own_arr = jnp.stack([chip, ci]).astype(jnp.int32)
    grad, delta, new_m, new_v = {}, {}, {}, {}

    def update(names, mine):
        theirs = _join_halves(mine, "join_" + names[0])
        big = dict(w_in=(w_in, m_w_in, v_w_in), w_out=(w_out, m_w_out, v_w_out),
                   w_gate_up=(w_gate_up, m_w_gate_up, v_w_gate_up), w_down=(w_down, m_w_down, v_w_down))
        for name, g_mine, g_theirs in zip(names, mine, theirs):
            w, m, v = big[name]
            g, d, nm, nv = _adamw_halves(w[0], g_mine, g_theirs, m[0], v[0], c_arr, "adamw_" + name)
            grad[name], delta[name], new_m[name], new_v[name] = g[None], d[None], nm[None], nv[None]

    update(("w_out", "w_gate_up", "w_down"),
           [_owner_sum(g, l, own_arr, token, "owner_sum_%d" % a)
            for a, (g, l) in enumerate(zip(early["grads"], loc["landed"]))])
    part_back, landed_in, small_all = _late_exchange_wait(send_sems, recv_sems, part_sent, landing, small_sent,
                                                         everyone, new_v["w_down"])
    mine_in = _chip_sum(part_back, landed_in, chip_arr, "chip_sum_in")
    (theirs_in,) = _join_halves([mine_in], "join_w_in")
    g_w_in = jnp.where(ci == 0, jnp.concatenate([mine_in, theirs_in], axis=0),
                       jnp.concatenate([theirs_in, mine_in], axis=0))
    view = lambda a: jnp.transpose(a).reshape(D_PROJ // 4 * 8, BLK)
    back = lambda a: jnp.transpose(a.reshape(D_PROJ // 4, D_MODEL))[None]
    d, nm, nv = _adamw(view(w_in[0]), view(g_w_in), view(m_w_in[0]), view(v_w_in[0]), "adamw_w_in",
                       tr=D_PROJ // 4 * 4)
    grad["w_in"], delta["w_in"], new_m["w_in"], new_v["w_in"] = g_w_in[None], back(d), back(nm), back(nv)
    me = 4 * xi + 2 * yi + ci
    small_sum = _device_sum(lax.dynamic_update_slice(small_all, small[None], (me, 0, 0)))
    g_meta_tokens = lax.dynamic_slice(small_sum[8:N_SMALL], (0, chip * (D_MODEL // 4)), (N_META, D_MODEL // 4))
    g_small = small_sum[0:8]
    grad.update(_unpack_small(g_small))
    grad.update(meta_tokens=g_meta_tokens)
    delta["meta_tokens"], new_m["meta_tokens"], new_v["meta_tokens"] = _adamw(
        meta_tokens, g_meta_tokens, m_meta_tokens, v_meta_tokens, "adamw_meta")
    d, nm, nv = _adamw(
        _pack_small(ln_pre_mix, ln_post_mix, ln_pre_ffn, ln_post_ffn, rel_bias, b_forget, sinks), g_small,
        _pack_small(m_ln_pre_mix, m_ln_post_mix, m_ln_pre_ffn, m_ln_post_ffn, m_rel_bias, m_b_forget, m_sinks),
        _pack_small(v_ln_pre_mix, v_ln_post_mix, v_ln_pre_ffn, v_ln_post_ffn, v_rel_bias, v_b_forget, v_sinks),
        "adamw_small")
    delta.update(_unpack_small(d))
    new_m.update(_unpack_small(nm))
    new_v.update(_unpack_small(nv))

    loss = small_sum[LOSS_ROW, 0]
    return (loss,loc["grad_x"][None], *[grad[k] for k in WEIGHTS], *[delta[k] for k in WEIGHTS],
            *[new_m[k] for k in WEIGHTS], *[new_v[k] for k in WEIGHTS])
```

```python
import math

import numpy as np
import jax
import jax.numpy as jnp
from jax import lax
from jax.experimental import pallas as pl
from jax.experimental.pallas import tpu as pltpu

F32 = jnp.float32
BF16 = jnp.bfloat16
MESH = pl.DeviceIdType.MESH
SDS = jax.ShapeDtypeStruct

D_MODEL = 1024
SEQ = 4096
N_META = 16
N_HEADS = 8
HALF = 64
D_FF = 2816
N_BUCKETS = 32
EPS = 1e-6
NEG = -1e30
SCALE = 0.125
LOG2E = 1.4426950408889634
LN2 = 0.6931471805599453
PAD_ROWS = 112
ROW0 = PAD_ROWS + N_META
LP = ROW0 + SEQ
BLK = 128
NBLK = LP // BLK
TM = 384
NT = LP // TM
TM_PURE = LP // 2
TM_MID = LP // 4
TM_EPI = LP // 6
TN = 256
D_PROJ = 2312
D_PROJ_P = 2432
D_QKV = 2304
FF_T = 1408
VMEM_LIMIT = 56 * 1024 * 1024

ADAM_LR = 0.001
ADAM_B1 = 0.9
ADAM_B2 = 0.999
ADAM_EPS = 1e-08
ADAM_WD = 0.01
ADAM_STEP = 10

QA = 0
KA, VA = 4, 5
QB, KB, VB = 3, 5, 7
W2 = 256

NT_DIMS = (((1,), (1,)), ((), ()))
TN_DIMS = (((0,), (0,)), ((), ()))


def _cparams(sem):
    return pltpu.CompilerParams(dimension_semantics=sem, vmem_limit_bytes=VMEM_LIMIT)


def _t5_bucket_np(d):
    n = np.maximum(d, 0).astype(np.int32)
    nf = np.maximum(n, 1).astype(np.float32)
    large = 16 + (np.log(nf / np.float32(16)) / np.float32(math.log(8.0)) * np.float32(16)).astype(np.int32)
    large = np.minimum(large, N_BUCKETS - 1)
    return np.where(n < 16, n, large).astype(np.int32)


def _bucket_tables():
    qi = np.arange(BLK)[:, None]
    ki = np.arange(BLK)[None, :]
    return np.stack([_t5_bucket_np(qi - ki), _t5_bucket_np(qi - ki + BLK)])


def _rms(x):
    return lax.rsqrt(jnp.mean(x * x, axis=-1, keepdims=True) + EPS)


def _rms_bwd(n, r, gdy):
    return r * (gdy - n * jnp.mean(n * gdy, axis=-1, keepdims=True))


def _pre_mix(h0, gain, w_in_b):
    half = D_QKV // 2

    def body(h_ref, g_ref, w_ref, hn_ref, proj_ref, f_ref):
        x = h_ref[...]
        hn = (x * _rms(x) * g_ref[...]).astype(BF16)
        hn_ref[...] = hn
        proj_ref[:, :half] = jnp.dot(hn, w_ref[:, :half], preferred_element_type=F32).astype(BF16)
        p = jnp.dot(hn, w_ref[:, half:], preferred_element_type=F32)
        proj_ref[:, half:] = p[:, :half].astype(BF16)
        f_ref[...] = p[:, half:]

    return pl.pallas_call(
        body, grid=(LP // TM_MID,),
        in_specs=[pl.BlockSpec((TM_MID, D_MODEL), lambda i: (i, 0)),
                  pl.BlockSpec((1, D_MODEL), lambda i: (0, 0)),
                  pl.BlockSpec((D_MODEL, D_PROJ_P), lambda i: (0, 0))],
        out_specs=[pl.BlockSpec((TM_MID, D_MODEL), lambda i: (i, 0)),
                   pl.BlockSpec((TM_MID, D_QKV), lambda i: (i, 0)),
                   pl.BlockSpec((TM_MID, BLK), lambda i: (i, 0))],
        out_shape=[SDS((LP, D_MODEL), BF16), SDS((LP, D_QKV), BF16), SDS((LP, BLK), F32)],
        compiler_params=_cparams(("parallel",)), name="pre_mix")(h0, gain, w_in_b)


def _attn_out(o_a, o_b, w_out_b, h0, g_post, g_pre_ffn):
    def body(oa_ref, ob_ref, w_ref, h0_ref, gp_ref, gf_ref, a_ref, h1_ref, hn2_ref):
        a = (jnp.dot(oa_ref[...], w_ref[0:512, :], preferred_element_type=F32)
             + jnp.dot(ob_ref[...], w_ref[512:1024, :], preferred_element_type=F32))
        a_ref[...] = a
        h1 = h0_ref[...] + a * _rms(a) * gp_ref[...]
        h1_ref[...] = h1
        hn2_ref[...] = (h1 * _rms(h1) * gf_ref[...]).astype(BF16)

    row = lambda w: pl.BlockSpec((TM_EPI, w), lambda i: (i, 0))
    vec = pl.BlockSpec((1, D_MODEL), lambda i: (0, 0))
    return pl.pallas_call(
        body, grid=(LP // TM_EPI,),
        in_specs=[row(512), row(512), pl.BlockSpec((D_MODEL, D_MODEL), lambda i: (0, 0)), row(D_MODEL), vec, vec],
        out_specs=[row(D_MODEL), row(D_MODEL), row(D_MODEL)],
        out_shape=[SDS((LP, D_MODEL), F32), SDS((LP, D_MODEL), F32), SDS((LP, D_MODEL), BF16)],
        compiler_params=_cparams(("parallel",)), name="attn_out")(o_a, o_b, w_out_b, h0, g_post, g_pre_ffn)


def _ffn_up(hn2, w_gu_b):
    def body(x_ref, wg_ref, wu_ref, g_ref, u_ref, act_ref):
        x = x_ref[...]
        g = jnp.dot(x, wg_ref[...], preferred_element_type=F32)
        u = jnp.dot(x, wu_ref[...], preferred_element_type=F32)
        g_ref[...] = g.astype(BF16)
        u_ref[...] = u.astype(BF16)
        act_ref[...] = (g * (1.0 / (1.0 + jnp.exp(-g))) * u).astype(BF16)

    out = pl.BlockSpec((TM_PURE, TN), lambda i, j: (i, j))
    return pl.pallas_call(
        body, grid=(LP // TM_PURE, D_FF // TN),
        in_specs=[pl.BlockSpec((TM_PURE, D_MODEL), lambda i, j: (i, 0)),
                  pl.BlockSpec((D_MODEL, TN), lambda i, j: (0, j)),
                  pl.BlockSpec((D_MODEL, TN), lambda i, j: (0, j + D_FF // TN))],
        out_specs=[out, out, out],
        out_shape=[SDS((LP, D_FF), BF16)] * 3,
        compiler_params=_cparams(("parallel", "parallel")), name="ffn_up")(hn2, w_gu_b, w_gu_b)


def _ffn_down_loss(act, w_dn_b, h1, tgt, g_post_ffn):
    def body(act_ref, w_ref, h1_ref, t0_ref, t1_ref, t2_ref, g_ref, dff_ref, dy_ref, loss_ref, dg_ref):
        i = pl.program_id(0)
        target = jnp.concatenate([t0_ref[...], t1_ref[...], t2_ref[...]], axis=0)

        @pl.when(i == 0)
        def _():
            loss_ref[...] = jnp.zeros_like(loss_ref)
            dg_ref[...] = jnp.zeros_like(dg_ref)

        ff = jnp.dot(act_ref[...], w_ref[...], preferred_element_type=F32)
        r = _rms(ff)
        n = ff * r
        g = g_ref[...]
        y = h1_ref[...] + n * g
        rows = i * TM + lax.broadcasted_iota(jnp.int32, (TM, D_MODEL), 0)
        diff = jnp.where(rows >= ROW0, y - target, 0.0)
        loss_ref[...] += 0.5 * jnp.sum(diff * diff) / D_MODEL
        dy = diff / D_MODEL
        dy_ref[...] = dy
        dg_ref[...] += jnp.sum(dy * n, axis=0, keepdims=True)
        dff_ref[...] = _rms_bwd(n, r, g * dy).astype(BF16)

    row = pl.BlockSpec((TM, D_MODEL), lambda i: (i, 0))
    tblk = lambda j: pl.BlockSpec((BLK, D_MODEL), lambda i: (jnp.maximum(3 * i - 1 + j, 0), 0))
    return pl.pallas_call(
        body, grid=(NT,),
        in_specs=[pl.BlockSpec((TM, D_FF), lambda i: (i, 0)), pl.BlockSpec((D_FF, D_MODEL), lambda i: (0, 0)),
                  row, tblk(0), tblk(1), tblk(2), pl.BlockSpec((1, D_MODEL), lambda i: (0, 0))],
        out_specs=[row, row, pl.BlockSpec((8, BLK), lambda i: (0, 0)), pl.BlockSpec((1, D_MODEL), lambda i: (0, 0))],
        out_shape=[SDS((LP, D_MODEL), BF16), SDS((LP, D_MODEL), F32), SDS((8, BLK), F32), SDS((1, D_MODEL), F32)],
        compiler_params=_cparams(("arbitrary",)), name="ffn_down_loss")(act, w_dn_b, h1, tgt, tgt, tgt, g_post_ffn)


def _ffn_down_bwd(dff, w_dn_b, g, u):
    def body(d_ref, w_ref, g_ref, u_ref, dg_ref, du_ref):
        dact = lax.dot_general(d_ref[...], w_ref[...], NT_DIMS, preferred_element_type=F32)
        gg = g_ref[...].astype(F32)
        sig = 1.0 / (1.0 + jnp.exp(-gg))
        dg_ref[...] = (dact * u_ref[...].astype(F32) * sig * (1.0 + gg * (1.0 - sig))).astype(BF16)
        du_ref[...] = (dact * gg * sig).astype(BF16)

    blk = pl.BlockSpec((TM_PURE, TN), lambda i, j: (i, j))
    return pl.pallas_call(
        body, grid=(LP // TM_PURE, D_FF // TN),
        in_specs=[pl.BlockSpec((TM_PURE, D_MODEL), lambda i, j: (i, 0)),
                  pl.BlockSpec((TN, D_MODEL), lambda i, j: (j, 0)), blk, blk],
        out_specs=[blk, blk],
        out_shape=[SDS((LP, D_FF), BF16)] * 2,
        compiler_params=_cparams(("parallel", "parallel")), name="ffn_down_bwd")(dff, w_dn_b, g, u)


def _ffn_up_bwd(dg, du, w_gu_b, h1, a, dy, g_pre_ffn, g_post_mix):
    def body(dg_ref, du_ref, w_ref, h1_ref, a_ref, dy_ref, gf_ref, gp_ref,
             dh1_ref, da_ref, dgf_ref, dgp_ref, acc):
        i = pl.program_id(0)
        s = pl.program_id(1)

        @pl.when((i == 0) & (s == 0))
        def _():
            dgf_ref[...] = jnp.zeros_like(dgf_ref)
            dgp_ref[...] = jnp.zeros_like(dgp_ref)

        @pl.when(s == 0)
        def _():
            acc[...] = jnp.zeros_like(acc)

        @pl.when(s < 2)
        def _():
            acc[...] += lax.dot_general(dg_ref[...], w_ref[...], NT_DIMS, preferred_element_type=F32)

        @pl.when(s >= 2)
        def _():
            acc[...] += lax.dot_general(du_ref[...], w_ref[...], NT_DIMS, preferred_element_type=F32)

        @pl.when(s == 3)
        def _():
            dhn2 = acc[...]
            h1 = h1_ref[...]
            r2 = _rms(h1)
            n2 = h1 * r2
            dgf_ref[...] += jnp.sum(dhn2 * n2, axis=0, keepdims=True)
            dh1 = dy_ref[...] + _rms_bwd(n2, r2, gf_ref[...] * dhn2)
            dh1_ref[...] = dh1
            av = a_ref[...]
            ra = _rms(av)
            na = av * ra
            dgp_ref[...] += jnp.sum(dh1 * na, axis=0, keepdims=True)
            da_ref[...] = _rms_bwd(na, ra, gp_ref[...] * dh1).astype(BF16)

    row = pl.BlockSpec((TM_EPI, D_MODEL), lambda i, s: (i, 0))
    vec = pl.BlockSpec((1, D_MODEL), lambda i, s: (0, 0))
    return pl.pallas_call(
        body, grid=(LP // TM_EPI, 4),
        in_specs=[pl.BlockSpec((TM_EPI, FF_T), lambda i, s: (i, jnp.minimum(s, 1))),
                  pl.BlockSpec((TM_EPI, FF_T), lambda i, s: (i, jnp.maximum(s - 2, 0))),
                  pl.BlockSpec((D_MODEL, FF_T), lambda i, s: (0, s)),
                  row, row, row, vec, vec],
        out_specs=[row, row, vec, vec],
        out_shape=[SDS((LP, D_MODEL), F32), SDS((LP, D_MODEL), BF16), SDS((1, D_MODEL), F32), SDS((1, D_MODEL), F32)],
        scratch_shapes=[pltpu.VMEM((TM_EPI, D_MODEL), F32)],
        compiler_params=_cparams(("arbitrary", "arbitrary")), name="ffn_up_bwd",
    )(dg, du, w_gu_b, h1, a, dy, g_pre_ffn, g_post_mix)


def _attn_out_bwd(da, w_out_b):
    def body(d_ref, w_ref, o_ref):
        o_ref[...] = lax.dot_general(d_ref[...], w_ref[...], NT_DIMS, preferred_element_type=F32).astype(BF16)

    row = pl.BlockSpec((TM_PURE, D_MODEL), lambda i: (i, 0))
    return pl.pallas_call(
        body, grid=(LP // TM_PURE,),
        in_specs=[row, pl.BlockSpec((D_MODEL, D_MODEL), lambda i: (0, 0))],
        out_specs=row, out_shape=SDS((LP, D_MODEL), BF16),
        compiler_params=_cparams(("parallel",)), name="attn_out_bwd")(da, w_out_b)


def _pre_mix_bwd(dq_a, dq_b, dk_b, dv_b, dk_a, dv_a, df, w_in_b, h0, dh1, g_pre_mix):
    def body(qa_ref, qb_ref, kb_ref, vb_ref, ka_ref, va_ref, f_ref, w_ref, h0_ref, dh1_ref, g_ref,
             dproj_ref, dh0_ref, dg_ref):
        i = pl.program_id(0)

        @pl.when(i == 0)
        def _():
            dg_ref[...] = jnp.zeros_like(dg_ref)

        dproj = jnp.concatenate(
            [qa_ref[...], ka_ref[...].astype(BF16), va_ref[...].astype(BF16), (qb_ref[...] * SCALE).astype(BF16),
             kb_ref[...], vb_ref[...], f_ref[...].astype(BF16)], axis=1)
        dproj_ref[...] = dproj
        dhn = lax.dot_general(dproj, w_ref[...], NT_DIMS, preferred_element_type=F32)
        x = h0_ref[...]
        r = _rms(x)
        n = x * r
        dg_ref[...] += jnp.sum(dhn * n, axis=0, keepdims=True)
        dh0_ref[...] = dh1_ref[...] + _rms_bwd(n, r, g_ref[...] * dhn)

    row = lambda w: pl.BlockSpec((TM_EPI, w), lambda i: (i, 0))
    vec = pl.BlockSpec((1, D_MODEL), lambda i: (0, 0))
    return pl.pallas_call(
        body, grid=(LP // TM_EPI,),
        in_specs=[row(512), row(512), row(512), row(512), row(BLK), row(BLK), row(BLK),
                  pl.BlockSpec((D_MODEL, D_PROJ_P), lambda i: (0, 0)), row(D_MODEL), row(D_MODEL), vec],
        out_specs=[row(D_PROJ_P), row(D_MODEL), vec],
        out_shape=[SDS((LP, D_PROJ_P), BF16), SDS((LP, D_MODEL), F32), SDS((1, D_MODEL), F32)],
        compiler_params=_cparams(("arbitrary",)), name="pre_mix_bwd",
    )(dq_a, dq_b, dk_b, dv_b, dk_a, dv_a, df, w_in_b, h0, dh1, g_pre_mix)


def _mm_tn(parts, b, tm, name, out_dtype=F32):
    widths = [p.shape[1] for p in parts]
    m_total = sum(widths)
    n = b.shape[1]
    whole = len(parts) > 1
    n_k = LP // TM_MID
    assert (tm == m_total) if whole else (m_total % tm == 0)

    def body(*refs):
        a_refs, b_ref, o_ref, acc = refs[:-3], refs[-3], refs[-2], refs[-1]
        k = pl.program_id(1)

        @pl.when(k == 0)
        def _():
            acc[...] = jnp.zeros_like(acc)
        a = a_refs[0][...] if not whole else jnp.concatenate([r[...] for r in a_refs], axis=1)
        acc[...] += lax.dot_general(a, b_ref[...], TN_DIMS, preferred_element_type=F32)

        @pl.when(k == n_k - 1)
        def _():
            o_ref[...] = acc[...].astype(out_dtype)

    a_specs = ([pl.BlockSpec((TM_MID, w), lambda mi, k: (k, 0)) for w in widths] if whole
               else [pl.BlockSpec((TM_MID, tm), lambda mi, k: (k, mi))])
    return pl.pallas_call(
        body, grid=(m_total // tm, n_k),
        in_specs=a_specs + [pl.BlockSpec((TM_MID, n), lambda mi, k: (k, 0))],
        out_specs=pl.BlockSpec((tm, n), lambda mi, k: (mi, 0)),
        out_shape=SDS((m_total, n), out_dtype),
        scratch_shapes=[pltpu.VMEM((tm, n), F32)],
        compiler_params=_cparams(("parallel", "arbitrary")), name=name)(*parts, b)


def _dw_gate_up(hn2, dg, du):
    n_k = LP // TM_MID

    def body(a_ref, dg_ref, du_ref, o_ref, acc):
        s = pl.program_id(0)
        k = pl.program_id(1)

        @pl.when(k == 0)
        def _():
            acc[...] = jnp.zeros_like(acc)

        @pl.when(s < 2)
        def _():
            acc[...] += lax.dot_general(a_ref[...], dg_ref[...], TN_DIMS, preferred_element_type=F32)

        @pl.when(s >= 2)
        def _():
            acc[...] += lax.dot_general(a_ref[...], du_ref[...], TN_DIMS, preferred_element_type=F32)

        @pl.when(k == n_k - 1)
        def _():
            o_ref[0] = acc[...].astype(BF16)

    return pl.pallas_call(
        body, grid=(4, n_k),
        in_specs=[pl.BlockSpec((TM_MID, D_MODEL), lambda s, k: (k, 0)),
                  pl.BlockSpec((TM_MID, FF_T), lambda s, k: (k, jnp.minimum(s, 1))),
                  pl.BlockSpec((TM_MID, FF_T), lambda s, k: (k, jnp.maximum(s - 2, 0)))],
        out_specs=pl.BlockSpec((1, D_MODEL, FF_T), lambda s, k: (s, 0, 0)),
        out_shape=SDS((4, D_MODEL, FF_T), BF16),
        scratch_shapes=[pltpu.VMEM((D_MODEL, FF_T), F32)],
        compiler_params=_cparams(("parallel", "arbitrary")), name="dw_gate_up")(hn2, dg, du)


def _split3(x):
    hi = x.astype(BF16)
    r1 = x - hi.astype(F32)
    mid = r1.astype(BF16)
    lo = (r1 - mid.astype(F32)).astype(BF16)
    return hi, mid, lo


def _tri_matmul(tri, x):
    hi, mid, lo = _split3(x)
    dot = lambda t: jnp.dot(tri, t, preferred_element_type=F32)
    return dot(hi) + dot(mid) + dot(lo)


def _forget_cumsum(f, b_forget_p):
    def body(f_ref, b_ref, cum_ref, carry):
        i = pl.program_id(0)

        @pl.when(i == 0)
        def _():
            carry[...] = jnp.zeros_like(carry)

        z = f_ref[...] + b_ref[...]
        ls = jnp.minimum(z, 0.0) - jnp.log(1.0 + jnp.exp(-jnp.abs(z)))
        rows = i * TM + lax.broadcasted_iota(jnp.int32, (TM, BLK), 0)
        ls = jnp.where(rows >= PAD_ROWS, ls, 0.0)
        r = lax.broadcasted_iota(jnp.int32, (TM, TM), 0)
        c = lax.broadcasted_iota(jnp.int32, (TM, TM), 1)
        tri = (c <= r).astype(BF16)
        cum = _tri_matmul(tri, ls) + carry[...]
        cum_ref[...] = cum
        carry[...] = cum[TM - 1:TM, :]

    return pl.pallas_call(
        body, grid=(NT,),
        in_specs=[pl.BlockSpec((TM, BLK), lambda i: (i, 0)), pl.BlockSpec((1, BLK), lambda i: (0, 0))],
        out_specs=pl.BlockSpec((TM, BLK), lambda i: (i, 0)),
        out_shape=SDS((LP, BLK), F32),
        scratch_shapes=[pltpu.VMEM((1, BLK), F32)],
        compiler_params=_cparams(("arbitrary",)), name="forget_cumsum")(f, b_forget_p)


def _forget_cumsum_bwd(dcum, f, b_forget_p):
    def body(d_ref, f_ref, b_ref, df_ref, db_ref, carry):
        i = pl.program_id(0)

        @pl.when(i == 0)
        def _():
            carry[...] = jnp.zeros_like(carry)
            db_ref[...] = jnp.zeros_like(db_ref)

        blk = NT - 1 - i
        r = lax.broadcasted_iota(jnp.int32, (TM, TM), 0)
        c = lax.broadcasted_iota(jnp.int32, (TM, TM), 1)
        tri = (c >= r).astype(BF16)
        d = d_ref[...]
        dls = _tri_matmul(tri, d) + carry[...]
        carry[...] = dls[0:1, :]
        z = f_ref[...] + b_ref[...]
        rows = blk * TM + lax.broadcasted_iota(jnp.int32, (TM, BLK), 0)
        df = jnp.where(rows >= PAD_ROWS, dls / (1.0 + jnp.exp(z)), 0.0)
        df_ref[...] = df
        db_ref[...] += jnp.sum(df, axis=0, keepdims=True)

    rev = pl.BlockSpec((TM, BLK), lambda i: (NT - 1 - i, 0))
    vec = pl.BlockSpec((1, BLK), lambda i: (0, 0))
    return pl.pallas_call(
        body, grid=(NT,),
        in_specs=[rev, rev, vec],
        out_specs=[rev, vec],
        out_shape=[SDS((LP, BLK), F32), SDS((1, BLK), F32)],
        scratch_shapes=[pltpu.VMEM((1, BLK), F32)],
        compiler_params=_cparams(("arbitrary",)), name="forget_cumsum_bwd")(dcum, f, b_forget_p)


def _lane_half(rows):
    return lax.broadcasted_iota(jnp.int32, (rows, BLK), 1) // HALF


def _fox_valid(qi, kj):
    qrow = qi * TM + lax.broadcasted_iota(jnp.int32, (TM, TM), 0)
    krow = kj * TM + lax.broadcasted_iota(jnp.int32, (TM, TM), 1)
    return (krow <= qrow) & ((krow >= PAD_ROWS) | (qrow < PAD_ROWS))


class _Rider:
    def __init__(self, operands, out_shapes, sem_counts, first, middle, last):
        self.operands, self.out_shapes, self.sem_counts = list(operands), list(out_shapes), list(sem_counts)
        self.first, self.middle, self.last = first, middle, last

    def scratch(self):
        return [pltpu.SemaphoreType.DMA((k,)) for k in self.sem_counts]

    def split(self, refs, n_in, n_out, n_scratch):
        a, b = len(self.operands), len(self.out_shapes)
        ins, mine_in = refs[:n_in], refs[n_in:n_in + a]
        outs, mine_out = refs[n_in + a:n_in + a + n_out], refs[n_in + a + n_out:n_in + a + n_out + b]
        rest = refs[n_in + a + n_out + b:]
        return ins, outs, rest[:n_scratch], (mine_in, mine_out, rest[n_scratch:])

    def at_steps(self, mine, is_first, is_middle, is_last):
        for cond, fn in ((is_first, self.first), (is_middle, self.middle), (is_last, self.last)):
            pl.when(cond)(lambda fn=fn: fn(*mine))


HBM_SPEC = pl.BlockSpec(memory_space=pltpu.HBM)


N_AUG = 4
QCHUNKS = ((0, 128), (128, 128), (256, 128))
KSUB = 384
AHEAD = 5
AHEAD_BWD = 1


def _fox_prep(proj, cum):
    def body(q0_ref, q1_ref, k0_ref, k1_ref, v0_ref, v1_ref, c_ref, qa_ref, ka_ref, vt_ref):
        half = _lane_half(TM)
        lane = lax.broadcasted_iota(jnp.int32, (TM, BLK), 1)
        for pp in range(4):
            cols = slice(pp * BLK, (pp + 1) * BLK)
            q_ref, k_ref, v_ref = ((q0_ref, k0_ref, v0_ref), (q1_ref, k1_ref, v1_ref))[pp // 2]
            part = slice((pp % 2) * BLK, (pp % 2 + 1) * BLK)
            qs = q_ref[:, part].astype(F32) * (SCALE * LOG2E)
            kp = k_ref[:, part].astype(F32)
            vp = v_ref[:, part]
            vt_ref[cols, :] = vp.astype(F32).T.astype(BF16)
            for e in range(2):
                h = 2 * pp + e
                a = (1 - e) * HALF
                blk = slice(h * BLK, (h + 1) * BLK)
                hi, mid, lo = _split3(-LOG2E * c_ref[:, h:h + 1])
                q_aug = jnp.where(half == e, qs, jnp.where((lane >= a) & (lane < a + 3), 1.0, 0.0))
                k_aug = jnp.where(half == e, kp, jnp.where(
                    lane == a, hi.astype(F32), jnp.where(lane == a + 1, mid.astype(F32), jnp.where(
                        lane == a + 2, lo.astype(F32), jnp.where(lane == a + 3, 1.0, 0.0)))))
                qa_ref[blk, :] = q_aug.T.astype(BF16)
                ka_ref[:, blk] = k_aug.astype(BF16)

    row = lambda blk: pl.BlockSpec((TM, W2), lambda i: (i, blk))
    wide = pl.BlockSpec((TM, 1024), lambda i: (i, 0))
    return pl.pallas_call(
        body, grid=(NT,),
        in_specs=[row(QB), row(QB + 1), row(KB), row(KB + 1), row(VB), row(VB + 1),
                  pl.BlockSpec((TM, BLK), lambda i: (i, 0))],
        out_specs=[pl.BlockSpec((1024, TM), lambda i: (0, i)), wide, pl.BlockSpec((512, TM), lambda i: (0, i))],
        out_shape=[SDS((1024, LP), BF16), SDS((LP, 1024), BF16), SDS((512, LP), BF16)],
        compiler_params=_cparams(("parallel",)), name="fox_prep")(proj, proj, proj, proj, proj, proj, cum)


def _over_keys(reduce, x):
    slabs = x.reshape(x.shape[0] // HALF, HALF, x.shape[1])
    return reduce(reduce(slabs, axis=0), axis=0, keepdims=True)


def _fox_valid_t(qi, kj, c, r):
    krow = kj * TM + r * KSUB + lax.broadcasted_iota(jnp.int32, (KSUB, c[1]), 0)
    qrow = qi * TM + c[0] + lax.broadcasted_iota(jnp.int32, (KSUB, c[1]), 1)
    return (krow <= qrow) & ((krow >= PAD_ROWS) | (qrow < PAD_ROWS))


def _fox_fwd(q_aug, k_aug, v_t, rider):
    pairs = [(qi, kj) for qi in range(NT) for kj in range(qi + 1)]
    n_pairs = len(pairs)

    def body(qi_ref, kj_ref, *refs):
        (q_ref, k_ref, vt_ref), (o_ref, lse_ref), (m_s, l_s, acc_s), mine = rider.split(refs, 3, 2, 3)
        n = pl.program_id(0)
        qi = qi_ref[n]
        kj = kj_ref[n]
        rider.at_steps(mine, n == 0, n == n_pairs // 2, n == n_pairs - 1)

        @pl.when(kj == 0)
        def _():
            m_s[...] = jnp.full_like(m_s, NEG)
            l_s[...] = jnp.zeros_like(l_s)
            acc_s[...] = jnp.zeros_like(acc_s)

        def tile(masked):
            steps = [(h, c, r) for h in range(N_HEADS) for c in QCHUNKS for r in range(TM // KSUB)]

            def scores(h, c, r):
                blk = slice(h * BLK, (h + 1) * BLK)
                return jnp.dot(k_ref[r * KSUB:(r + 1) * KSUB, blk], q_ref[blk, c[0]:c[0] + c[1]],
                               preferred_element_type=F32)

            ahead = [scores(*st) for st in steps[:AHEAD]]
            for n, (h, c, r) in enumerate(steps):
                s_t = ahead.pop(0)
                if n + AHEAD < len(steps):
                    ahead.append(scores(*steps[n + AHEAD]))
                cs = slice(c[0], c[0] + c[1])
                if masked:
                    s_t = jnp.where(_fox_valid_t(qi, kj, c, r), s_t, NEG)
                m_prev = m_s[h, :, cs]
                m_new = jnp.maximum(m_prev, _over_keys(jnp.max, s_t))
                p_t = jnp.exp2(s_t - m_new)
                alpha = jnp.exp2(m_prev - m_new)
                l_s[h, :, cs] = alpha * l_s[h, :, cs] + _over_keys(jnp.sum, p_t)
                m_s[h, :, cs] = m_new
                vt = vt_ref[h * HALF:(h + 1) * HALF, r * KSUB:(r + 1) * KSUB]
                acc_s[h, :, cs] = acc_s[h, :, cs] * alpha + jnp.dot(vt, p_t.astype(BF16),
                                                                    preferred_element_type=F32)

        @pl.when((kj < qi) & (kj > 0))
        def _():
            tile(False)

        @pl.when((kj == qi) | (kj == 0))
        def _():
            tile(True)

        @pl.when(kj == qi)
        def _():
            for pp in range(4):
                both = jnp.concatenate([acc_s[2 * pp] * (1.0 / l_s[2 * pp]),
                                        acc_s[2 * pp + 1] * (1.0 / l_s[2 * pp + 1])], axis=0)
                o_ref[:, pp * BLK:(pp + 1) * BLK] = both.T.astype(BF16)
            for h in range(N_HEADS):
                lse_ref[h] = m_s[h] * LN2 + jnp.log(l_s[h])

    grid_spec = pltpu.PrefetchScalarGridSpec(
        num_scalar_prefetch=2, grid=(n_pairs,),
        in_specs=[pl.BlockSpec((1024, TM), lambda n, qi, kj: (0, qi[n])),
                  pl.BlockSpec((TM, 1024), lambda n, qi, kj: (kj[n], 0)),
                  pl.BlockSpec((512, TM), lambda n, qi, kj: (0, kj[n]))] + [HBM_SPEC] * len(rider.operands),
        out_specs=[pl.BlockSpec((TM, 512), lambda n, qi, kj: (qi[n], 0)),
                   pl.BlockSpec((N_HEADS, 1, TM), lambda n, qi, kj: (0, 0, qi[n]))]
        + [HBM_SPEC] * len(rider.out_shapes),
        scratch_shapes=[pltpu.VMEM((N_HEADS, 1, TM), F32), pltpu.VMEM((N_HEADS, 1, TM), F32),
                        pltpu.VMEM((N_HEADS, HALF, TM), F32)] + rider.scratch())
    o_b, lse, *carried = pl.pallas_call(
        body, grid_spec=grid_spec,
        out_shape=[SDS((LP, 512), BF16), SDS((N_HEADS, 1, LP), F32)] + rider.out_shapes,
        compiler_params=_cparams(("arbitrary",)), name="fox_fwd",
    )(jnp.asarray([p[0] for p in pairs], jnp.int32), jnp.asarray([p[1] for p in pairs], jnp.int32),
      q_aug, k_aug, v_t, *rider.operands)
    return o_b, lse, carried


def _fox_bwd(proj, o_b, dmix, lse, ck_t, rider):
    pairs = [(kj, qi) for kj in range(NT) for qi in range(kj, NT)]
    n_pairs = len(pairs)

    def body(kj_ref, qi_ref, *refs):
        ((q0_ref, q1_ref, k0_ref, k1_ref, v0_ref, v1_ref, o_ref, do_ref, lse_ref, ck_ref),
         (dq_ref, dk_ref, dv_ref, dck_ref, dcq_ref), (dk_s, dv_s, dck_s), mine) = rider.split(refs, 10, 5, 3)
        n = pl.program_id(0)
        kj = kj_ref[n]
        qi = qi_ref[n]
        rider.at_steps(mine, n == 0, n == n_pairs // 2, n == n_pairs - 1)

        @pl.when(n == 0)
        def _():
            dq_ref[...] = jnp.zeros_like(dq_ref)
            dcq_ref[...] = jnp.zeros_like(dcq_ref)

        @pl.when(qi == kj)
        def _():
            dk_s[...] = jnp.zeros_like(dk_s)
            dv_s[...] = jnp.zeros_like(dv_s)
            dck_s[...] = jnp.zeros_like(dck_s)

        def tile(masked):
            valid = _fox_valid(qi, kj) if masked else None
            half = _lane_half(TM)
            q0 = pl.multiple_of(qi * TM, TM)
            lane = lax.broadcasted_iota(jnp.int32, (TM, BLK), 1)
            row_sums = jnp.zeros((TM, BLK), F32)
            pair_ops = {}

            def operands(pp):
                if pp not in pair_ops:
                    cols = slice(pp * BLK, (pp + 1) * BLK)
                    q_ref, k_ref, v_ref = ((q0_ref, k0_ref, v0_ref), (q1_ref, k1_ref, v1_ref))[pp // 2]
                    part = slice((pp % 2) * BLK, (pp % 2 + 1) * BLK)
                    pair_ops[pp] = ((q_ref[:, part].astype(F32) * SCALE).astype(BF16), k_ref[:, part],
                                    v_ref[:, part], do_ref[:, cols])
                return pair_ops[pp]

            def scores(pp, e):
                qs, kp, vp, dop = operands(pp)
                ke = jnp.where(half == e, kp, jnp.zeros_like(kp))
                ve = jnp.where(half == e, vp, jnp.zeros_like(vp))
                return (lax.dot_general(qs, ke, NT_DIMS, preferred_element_type=F32),
                        lax.dot_general(dop, ve, NT_DIMS, preferred_element_type=F32), ke)

            steps = [(pp, e) for pp in range(4) for e in range(2)]
            ahead = [scores(*st) for st in steps[:AHEAD_BWD]]
            for n, (pp, e) in enumerate(steps):
                raw, dp, ke = ahead.pop(0)
                if n + AHEAD_BWD < len(steps):
                    ahead.append(scores(*steps[n + AHEAD_BWD]))
                h = 2 * pp + e
                cols = slice(pp * BLK, (pp + 1) * BLK)
                qs, kp, vp, dop = operands(pp)
                if e == 0:
                    prod = dop.astype(F32) * o_ref[:, cols].astype(F32)
                    d0 = jnp.sum(jnp.where(half == 0, prod, 0.0), axis=1, keepdims=True)
                    d1 = jnp.sum(prod, axis=1, keepdims=True) - d0
                    dq = jnp.zeros((TM, BLK), F32)
                    dks, dvs = [], []
                t = raw - ck_ref[h] - lse_ref[h]
                if masked:
                    t = jnp.where(valid, t, NEG)
                p = jnp.exp(t)
                ds = p * (dp - (d0 if e == 0 else d1))
                dck_s[h] += jnp.sum(ds, axis=0, keepdims=True)
                row_sums = jnp.where(lane == h, jnp.sum(ds, axis=1, keepdims=True), row_sums)
                ds_b = ds.astype(BF16)
                dq = dq + jnp.dot(ds_b, ke, preferred_element_type=F32)
                dks.append(lax.dot_general(ds_b, qs, TN_DIMS, preferred_element_type=F32))
                dvs.append(lax.dot_general(p.astype(BF16), dop, TN_DIMS, preferred_element_type=F32))
                if e == 1:
                    dq_ref[pl.ds(q0, TM), cols] += dq
                    dk_s[pp] += jnp.where(half == 0, dks[0], dks[1])
                    dv_s[pp] += jnp.where(half == 0, dvs[0], dvs[1])
            dcq_ref[pl.ds(q0, TM), :] += row_sums

        @pl.when((qi > kj) & (kj > 0))
        def _():
            tile(False)

        @pl.when((qi == kj) | (kj == 0))
        def _():
            tile(True)

        @pl.when(qi == NT - 1)
        def _():
            for pp in range(4):
                cols = slice(pp * BLK, (pp + 1) * BLK)
                dk_ref[:, cols] = dk_s[pp].astype(BF16)
                dv_ref[:, cols] = dv_s[pp].astype(BF16)
            dck_ref[...] = dck_s[...]

    qrow = lambda blk, w=512: pl.BlockSpec((TM, w), lambda n, kj, qi: (qi[n], blk))
    krow = lambda blk: pl.BlockSpec((TM, W2), lambda n, kj, qi: (kj[n], blk))
    grid_spec = pltpu.PrefetchScalarGridSpec(
        num_scalar_prefetch=2, grid=(n_pairs,),
        in_specs=[qrow(QB, W2), qrow(QB + 1, W2), krow(KB), krow(KB + 1), krow(VB), krow(VB + 1), qrow(0), qrow(1),
                  pl.BlockSpec((N_HEADS, TM, 1), lambda n, kj, qi: (0, qi[n], 0)),
                  pl.BlockSpec((N_HEADS, 1, TM), lambda n, kj, qi: (0, 0, kj[n]))] + [HBM_SPEC] * len(rider.operands),
        out_specs=[pl.BlockSpec((LP, 512), lambda n, kj, qi: (0, 0)),
                   pl.BlockSpec((TM, 512), lambda n, kj, qi: (kj[n], 0)),
                   pl.BlockSpec((TM, 512), lambda n, kj, qi: (kj[n], 0)),
                   pl.BlockSpec((N_HEADS, 1, TM), lambda n, kj, qi: (0, 0, kj[n])),
                   pl.BlockSpec((LP, BLK), lambda n, kj, qi: (0, 0))] + [HBM_SPEC] * len(rider.out_shapes),
        scratch_shapes=[pltpu.VMEM((4, TM, BLK), F32), pltpu.VMEM((4, TM, BLK), F32),
                        pltpu.VMEM((N_HEADS, 1, TM), F32)] + rider.scratch())
    dq, dk, dv, dck, dcq, *carried = pl.pallas_call(
        body, grid_spec=grid_spec,
        out_shape=[SDS((LP, 512), F32), SDS((LP, 512), BF16), SDS((LP, 512), BF16), SDS((N_HEADS, 1, LP), F32),
                   SDS((LP, BLK), F32)] + rider.out_shapes,
        compiler_params=_cparams(("arbitrary",)), name="fox_bwd",
    )(jnp.asarray([p[0] for p in pairs], jnp.int32), jnp.asarray([p[1] for p in pairs], jnp.int32),
      proj, proj, proj, proj, proj, proj, o_b, dmix, lse, ck_t, *rider.operands)
    return dq, dk, dv, dck, dcq, carried


N_SEG = 3
N_KEY = N_SEG * BLK
GROUP = 4
QW = GROUP * BLK


def _bucket_tables_t():
    return np.ascontiguousarray(_bucket_tables().transpose(0, 2, 1))


def _stack_heads(ref, g, scale):
    half = _lane_half(BLK)
    out = []
    for pair in range(2):
        x = ref[:, (2 * g + pair) * BLK:(2 * g + pair + 1) * BLK].astype(F32) * scale
        swapped = pltpu.roll(x, HALF, 1)
        for e in range(2):
            out.append(jnp.where(half == g, x if e == g else swapped, 0.0).astype(BF16))
    return jnp.concatenate(out, axis=0)


def _unstack_heads(x_t, g, ref, scale):
    for pair in range(2):
        both = jnp.concatenate([x_t[:, (2 * pair) * BLK:(2 * pair + 1) * BLK],
                                x_t[:, (2 * pair + 1) * BLK:(2 * pair + 2) * BLK]], axis=0)
        ref[:, (2 * g + pair) * BLK:(2 * g + pair + 1) * BLK] = (both.T * scale).astype(ref.dtype)


def _swa_tables(tab_ref, sink_ref, bkt_ref, tbl, sink_row):
    kk = lax.broadcasted_iota(jnp.int32, (BLK, BLK), 0)
    qq = lax.broadcasted_iota(jnp.int32, (BLK, BLK), 1)
    neg = jnp.full((BLK, BLK), NEG, F32)
    lane = lax.broadcasted_iota(jnp.int32, (1, QW), 1) // BLK
    for g in range(2):
        row = jnp.zeros((1, QW), F32)
        for hh in range(GROUP):
            h = GROUP * g + hh
            cols = slice(hh * BLK, (hh + 1) * BLK)
            row = jnp.where(lane == hh, sink_ref[0, h], row)

            def step(b, carry, h=h):
                t = tab_ref[b, h]
                return jnp.where(bkt_ref[0] == b, t, carry[0]), jnp.where(bkt_ref[1] == b, t, carry[1])
            zero = jnp.zeros((BLK, BLK), F32)
            cur, prev = lax.fori_loop(0, N_BUCKETS, step, (zero, zero))
            far = jnp.full((BLK, BLK), tab_ref[N_BUCKETS - 1, h], F32)
            causal = jnp.where(kk <= qq, cur, neg)
            segments = [
                (neg, neg, jnp.where(kk >= PAD_ROWS, causal, neg)),
                (jnp.where(kk >= PAD_ROWS, prev, neg), neg, causal),
                (jnp.where(kk >= PAD_ROWS, far, neg), jnp.where(kk > qq, prev, neg), causal)]
            for case in range(3):
                for seg in range(N_SEG):
                    tbl[case, g, seg * BLK:(seg + 1) * BLK, cols] = segments[case][seg]
        sink_row[g] = row


def _swa_prep(proj):
    rows = LP // 3

    def body(k_ref, v_ref, kt_ref, vt_ref):
        kt_ref[...] = k_ref[...].astype(F32).T.astype(BF16)
        vt_ref[...] = v_ref[...].astype(F32).T.astype(BF16)

    col = pl.BlockSpec((BLK, rows), lambda i: (0, i))
    return pl.pallas_call(
        body, grid=(3,),
        in_specs=[pl.BlockSpec((rows, BLK), lambda i: (i, KA)), pl.BlockSpec((rows, BLK), lambda i: (i, VA))],
        out_specs=[col, col], out_shape=[SDS((BLK, LP), BF16)] * 2,
        compiler_params=_cparams(("parallel",)), name="swa_prep")(proj, proj)


def _segments(ref, i, by_rows):
    starts = [0, pl.multiple_of(jnp.maximum(i - 1, 0) * BLK, BLK), pl.multiple_of(i * BLK, BLK)]
    if by_rows:
        return jnp.concatenate([ref[pl.ds(s, BLK), :] for s in starts], axis=0)
    return jnp.concatenate([ref[:, pl.ds(s, BLK)] for s in starts], axis=1)


def _swa_fwd(proj, vt_a, rel_bias, sinks, bkt_t, rider):
    def body(*refs):
        ((tab_ref, sink_ref, bkt_ref, q_ref, k_ref, vt_ref), (o_ref, lse_ref),
         (tbl, sink_row), mine) = rider.split(refs, 6, 2, 2)
        i = pl.program_id(0)
        rider.at_steps(mine, i == 0, i == NBLK // 2, i == NBLK - 1)

        @pl.when(i == 0)
        def _():
            _swa_tables(tab_ref, sink_ref, bkt_ref, tbl, sink_row)

        case = jnp.minimum(i, 2)
        k_cat = _segments(k_ref, i, True)
        vt_cat = _segments(vt_ref, i, False)
        raw = [lax.dot_general(k_cat, _stack_heads(q_ref, g, SCALE), NT_DIMS, preferred_element_type=F32)
               for g in range(2)]
        for g in range(2):
            s_t = raw[g] + tbl[case, g]
            sink = sink_row[g]
            m = jnp.maximum(_over_keys(jnp.max, s_t), sink)
            p_t = jnp.exp(s_t - m)
            l = _over_keys(jnp.sum, p_t) + jnp.exp(sink - m)
            o_t = jnp.dot(vt_cat[g * HALF:(g + 1) * HALF, :], p_t.astype(BF16), preferred_element_type=F32)
            _unstack_heads(o_t * (1.0 / l), g, o_ref, 1.0)
            lse = m + jnp.log(l)
            for hh in range(GROUP):
                lse_ref[GROUP * g + hh] = lse[:, hh * BLK:(hh + 1) * BLK]

    smem = pl.BlockSpec(memory_space=pltpu.SMEM)
    o_a, lse, *carried = pl.pallas_call(
        body, grid=(NBLK,),
        in_specs=[smem, smem, pl.BlockSpec((2, BLK, BLK), lambda i: (0, 0, 0)),
                  pl.BlockSpec((BLK, 512), lambda i: (i, QA)), pl.BlockSpec((LP, BLK), lambda i: (0, KA)),
                  pl.BlockSpec((BLK, LP), lambda i: (0, 0))] + [HBM_SPEC] * len(rider.operands),
        out_specs=[pl.BlockSpec((BLK, 512), lambda i: (i, 0)),
                   pl.BlockSpec((N_HEADS, 1, BLK), lambda i: (0, 0, i))] + [HBM_SPEC] * len(rider.out_shapes),
        out_shape=[SDS((LP, 512), BF16), SDS((N_HEADS, 1, LP), F32)] + rider.out_shapes,
        scratch_shapes=[pltpu.VMEM((3, 2, N_KEY, QW), F32), pltpu.VMEM((2, 1, QW), F32)] + rider.scratch(),
        compiler_params=_cparams(("arbitrary",)), name="swa_fwd",
    )(rel_bias, sinks, bkt_t, proj, proj, vt_a, *rider.operands)
    return o_a, lse, carried


def _swa_bwd(proj, kt_a, o_a, dmix, lse, rel_bias, sinks, bkt_t):
    def body(tab_ref, sink_ref, bkt_ref, q_ref, k_ref, v_ref, kt_ref, o_ref, do_ref, lse_ref,
             dq_ref, dk_ref, dv_ref, dbias_ref, dsink_ref, tbl, sink_row, acc, dsk):
        i = pl.program_id(0)

        @pl.when(i == 0)
        def _():
            _swa_tables(tab_ref, sink_ref, bkt_ref, tbl, sink_row)
            dk_ref[...] = jnp.zeros_like(dk_ref)
            dv_ref[...] = jnp.zeros_like(dv_ref)
            acc[...] = jnp.zeros_like(acc)
            dsk[...] = jnp.zeros_like(dsk)

        case = jnp.minimum(i, 2)
        first = jnp.full((BLK, QW), i, jnp.int32) == 1
        k_cat = _segments(k_ref, i, True)
        v_cat = _segments(v_ref, i, True)
        kt_cat = _segments(kt_ref, i, False)
        dk_cat = jnp.zeros((N_KEY, BLK), F32)
        dv_cat = jnp.zeros((N_KEY, BLK), F32)
        for g in range(2):
            d_parts = []
            for pair in range(2):
                cols = slice((2 * g + pair) * BLK, (2 * g + pair + 1) * BLK)
                prod_t = (do_ref[:, cols].astype(F32) * o_ref[:, cols].astype(F32)).T
                d_parts += [jnp.sum(prod_t[:HALF], axis=0, keepdims=True),
                            jnp.sum(prod_t[HALF:], axis=0, keepdims=True)]
            d_row = jnp.concatenate(d_parts, axis=1)
            lse_row = jnp.concatenate([lse_ref[GROUP * g + hh] for hh in range(GROUP)], axis=1)
            q_st = _stack_heads(q_ref, g, SCALE)
            do_st = _stack_heads(do_ref, g, 1.0)
            s_t = lax.dot_general(k_cat, q_st, NT_DIMS, preferred_element_type=F32) + tbl[case, g]
            p_t = jnp.exp(s_t - lse_row)
            dp_t = lax.dot_general(v_cat, do_st, NT_DIMS, preferred_element_type=F32)
            ds_t = p_t * (dp_t - d_row)
            dsk[g] += -jnp.exp(sink_row[g] - lse_row) * d_row
            acc[g, 0:BLK] += jnp.where(first, 0.0, ds_t[0:BLK])
            acc[g, BLK:2 * BLK] += jnp.where(first, ds_t[0:BLK], ds_t[BLK:2 * BLK])
            acc[g, 2 * BLK:N_KEY] += ds_t[2 * BLK:N_KEY]
            ds_b = ds_t.astype(BF16)
            dk_cat = dk_cat + jnp.dot(ds_b, q_st, preferred_element_type=F32)
            dv_cat = dv_cat + jnp.dot(p_t.astype(BF16), do_st, preferred_element_type=F32)
            dq_t = jnp.dot(kt_cat[g * HALF:(g + 1) * HALF, :], ds_b, preferred_element_type=F32)
            _unstack_heads(dq_t, g, dq_ref, SCALE)

        prev0 = pl.multiple_of(jnp.maximum(i - 1, 0) * BLK, BLK)
        cur0 = pl.multiple_of(i * BLK, BLK)
        for ref, cat in ((dk_ref, dk_cat), (dv_ref, dv_cat)):
            ref[0:BLK, :] += cat[0:BLK]
            ref[pl.ds(prev0, BLK), :] += cat[BLK:2 * BLK]
            ref[pl.ds(cur0, BLK), :] += cat[2 * BLK:N_KEY]

        @pl.when(i == NBLK - 1)
        def _():
            lane = lax.broadcasted_iota(jnp.int32, (1, BLK), 1)

            def per_bucket(b, carry):
                row = jnp.zeros((1, BLK), F32)
                for h in range(N_HEADS):
                    g, cols = h // GROUP, slice((h % GROUP) * BLK, (h % GROUP + 1) * BLK)
                    val = (jnp.sum(jnp.where(bkt_ref[0] == b, acc[g, 2 * BLK:N_KEY, cols], 0.0), keepdims=True)
                           + jnp.sum(jnp.where(bkt_ref[1] == b, acc[g, BLK:2 * BLK, cols], 0.0), keepdims=True))
                    row = jnp.where(lane == h, val, row)
                dbias_ref[pl.ds(b, 1), :] = row
                return carry

            lax.fori_loop(0, N_BUCKETS, per_bucket, 0)
            far = jnp.zeros((1, BLK), F32)
            dsr = jnp.zeros((1, BLK), F32)
            for h in range(N_HEADS):
                g, cols = h // GROUP, slice((h % GROUP) * BLK, (h % GROUP + 1) * BLK)
                far = jnp.where(lane == h, jnp.sum(acc[g, 0:BLK, cols], keepdims=True), far)
                dsr = jnp.where(lane == h, jnp.sum(dsk[g, :, cols], keepdims=True), dsr)
            dbias_ref[N_BUCKETS - 1:N_BUCKETS, :] += far
            dsink_ref[...] = dsr

    smem = pl.BlockSpec(memory_space=pltpu.SMEM)
    blk512 = lambda col: pl.BlockSpec((BLK, 512), lambda i: (i, col))
    full = lambda r, c: pl.BlockSpec((r, c), lambda i: (0, 0))
    return pl.pallas_call(
        body, grid=(NBLK,),
        in_specs=[smem, smem, pl.BlockSpec((2, BLK, BLK), lambda i: (0, 0, 0)), blk512(QA),
                  pl.BlockSpec((LP, BLK), lambda i: (0, KA)), pl.BlockSpec((LP, BLK), lambda i: (0, VA)),
                  full(BLK, LP), blk512(0), blk512(0), pl.BlockSpec((N_HEADS, 1, BLK), lambda i: (0, 0, i))],
        out_specs=[blk512(0), full(LP, BLK), full(LP, BLK), full(N_BUCKETS, BLK), full(1, BLK)],
        out_shape=[SDS((LP, 512), BF16), SDS((LP, BLK), F32), SDS((LP, BLK), F32),
                   SDS((N_BUCKETS, BLK), F32), SDS((1, BLK), F32)],
        scratch_shapes=[pltpu.VMEM((3, 2, N_KEY, QW), F32), pltpu.VMEM((2, 1, QW), F32),
                        pltpu.VMEM((2, N_KEY, QW), F32), pltpu.VMEM((2, 1, QW), F32)],
        compiler_params=_cparams(("arbitrary",)), name="swa_bwd",
    )(rel_bias, sinks, bkt_t, proj, proj, proj, kt_a, o_a, dmix, lse)


def _local_step(x, tgt, meta, rel_bias, g_pre_mix, g_post_mix, g_pre_ffn, g_post_ffn, b_forget, sinks,
                w_in_b, out_rider, out_weight, ffn_rider, ffn_weights, early_grads):
    bkt_t = jnp.asarray(_bucket_tables_t())
    h0 = jnp.concatenate([jnp.zeros((PAD_ROWS, D_MODEL), F32), meta, x], axis=0)
    b_p = jnp.pad(b_forget, ((0, 0), (0, BLK - N_HEADS)))

    hn1, proj, f = _pre_mix(h0, g_pre_mix, w_in_b)
    kt_a, vt_a = _swa_prep(proj)
    o_a, lse_a, carried_out = _swa_fwd(proj, vt_a, rel_bias, sinks, bkt_t, out_rider)
    w_out_b = out_weight(carried_out)
    cum = _forget_cumsum(f, b_p)
    ck_t = cum[:, :N_HEADS].T.reshape(N_HEADS, 1, LP)
    q_aug, k_aug, v_t = _fox_prep(proj, cum)
    o_b, lse_row, carried = _fox_fwd(q_aug, k_aug, v_t, ffn_rider)
    lse_b = lse_row.reshape(N_HEADS, LP, 1)
    w_gu_b, w_dn_b = ffn_weights(carried)
    a, h1, hn2 = _attn_out(o_a, o_b, w_out_b, h0, g_post_mix, g_pre_ffn)
    g, u, act = _ffn_up(hn2, w_gu_b)
    dff, dy, loss_blk, dg_post_ffn = _ffn_down_loss(act, w_dn_b, h1, tgt, g_post_ffn)

    dw_dn = _mm_tn([act], dff, FF_T, "dw_down", BF16)
    dg, du = _ffn_down_bwd(dff, w_dn_b, g, u)
    dw_gu = _dw_gate_up(hn2, dg, du)
    dh1, da, dg_pre_ffn, dg_post_mix = _ffn_up_bwd(dg, du, w_gu_b, h1, a, dy, g_pre_ffn, g_post_mix)
    dw_out = _mm_tn([o_a, o_b], da, D_MODEL, "dw_out", BF16)
    dmix = _attn_out_bwd(da, w_out_b)
    dq_b, dk_b, dv_b, dck, dcq, landed = _fox_bwd(proj, o_b, dmix, lse_b, ck_t, early_grads(dw_gu, dw_dn, dw_out))
    dq_a, dk_a, dv_a, dbias, dsink = _swa_bwd(proj, kt_a, o_a, dmix, lse_a, rel_bias, sinks, bkt_t)
    dcum = dcq - jnp.pad(dck.reshape(N_HEADS, LP).T, ((0, 0), (0, BLK - N_HEADS)))
    df, db = _forget_cumsum_bwd(dcum, f, b_p)
    dproj, dh0, dg_pre_mix = _pre_mix_bwd(dq_a, dq_b, dk_b, dv_b, dk_a, dv_a, df, w_in_b, h0, dh1, g_pre_mix)
    dw_in = _mm_tn([hn1], dproj, D_MODEL, "dw_in", BF16)

    return dict(loss=loss_blk[0, 0], grad_x=dh0[ROW0:], meta=dh0[PAD_ROWS:ROW0],
                rel_bias=dbias[:, :N_HEADS], ln_pre_mix=dg_pre_mix, ln_post_mix=dg_post_mix,
                ln_pre_ffn=dg_pre_ffn, ln_post_ffn=dg_post_ffn, b_forget=db[:, :N_HEADS],
                sinks=dsink[:, :N_HEADS], w_in=dw_in, w_out=dw_out, w_gate_up=dw_gu, w_down=dw_dn,
                landed=landed)


N_SMALL = 24
LOSS_ROW = 6


def _place():
    x, y, c = lax.axis_index("x"), lax.axis_index("y"), lax.axis_index("c")
    return x, y, c, [(1 - x, y), (x, 1 - y), (1 - x, 1 - y)]


def _run_alone(rider, name):
    a, b = len(rider.operands), len(rider.out_shapes)

    def body(*refs):
        mine = (refs[:a], refs[a:a + b], refs[a + b:])
        rider.first(*mine)
        rider.middle(*mine)
        rider.last(*mine)

    return pl.pallas_call(body, in_specs=[HBM_SPEC] * a, out_specs=[HBM_SPEC] * b, out_shape=rider.out_shapes,
                          scratch_shapes=rider.scratch(), name=name)(*rider.operands)


def _gather_rider(shards, own_too, by_columns=()):
    n = len(shards)

    def slot(a, outs, chip, h):
        if a in by_columns:
            cols = shards[a].shape[2]
            return outs[a].at[h, :, pl.ds(pl.multiple_of(chip * cols, BLK), cols)]
        return outs[a].at[chip, h]

    def own_copies(ins, outs, sems):
        x, y, _, _ = _place()
        if not own_too:
            return []
        return [pltpu.make_async_copy(ins[a].at[h], slot(a, outs, 2 * x + y, h), sems[2].at[2 * a + h])
                for a in range(n) for h in range(2)]

    def copies(ins, outs, sems):
        send_sems, recv_sems = sems[:2]
        x, y, c, others = _place()
        chip = 2 * x + y
        sibling = (x, y, 1 - c)

        def rc(a, k, src, dst, to):
            return pltpu.make_async_remote_copy(src_ref=src, dst_ref=dst, send_sem=send_sems.at[6 * a + k],
                                                recv_sem=recv_sems.at[6 * a + k], device_id=to, device_id_type=MESH)

        pairs = [(a, j, ox, oy) for a in range(n) for j, (ox, oy) in enumerate(others)]
        there = lambda a, ox, oy, h: slot(a, outs, 2 * ox + oy, h)
        return dict(
            sent=lambda: [rc(a, j, ins[a].at[c], slot(a, outs, chip, c), (ox, oy, c)) for a, j, ox, oy in pairs],
            landed=lambda: [rc(a, j, there(a, ox, oy, c), there(a, ox, oy, c), sibling) for a, j, ox, oy in pairs],
            passed=lambda: [rc(a, 3 + j, there(a, ox, oy, c), there(a, ox, oy, c), sibling)
                            for a, j, ox, oy in pairs],
            arriving=lambda: [rc(a, 3 + j, there(a, ox, oy, 1 - c), there(a, ox, oy, 1 - c), sibling)
                              for a, j, ox, oy in pairs])

    def first(*mine):
        for cp in copies(*mine)["sent"]() + own_copies(*mine):
            cp.start()

    def middle(*mine):
        kinds = copies(*mine)
        for got, cp in zip(kinds["landed"](), kinds["passed"]()):
            got.wait_recv()
            cp.start()

    def last(*mine):
        kinds = copies(*mine)
        for cp in kinds["arriving"]():
            cp.wait_recv()
        for cp in kinds["sent"]() + kinds["passed"]():
            cp.wait_send()
        for cp in own_copies(*mine):
            cp.wait()

    shapes = [SDS((2, s.shape[1], 4 * s.shape[2]) if a in by_columns else (4,) + s.shape, s.dtype)
              for a, s in enumerate(shards)]
    return _Rider(shards, shapes, [6 * n, 6 * n] + [2 * n] * own_too, first, middle, last)


def _swap_rider(grads):
    n = len(grads)
    slabs = [(a, s) for a in range(n) for s in range(grads[a].shape[0])]

    def copies(ins, outs, sems):
        x, y, c, _ = _place()
        return [pltpu.make_async_remote_copy(
            src_ref=ins[a].at[s, 1 - c], dst_ref=outs[a].at[s], send_sem=sems[0].at[k], recv_sem=sems[1].at[k],
            device_id=(x, y, 1 - c), device_id_type=MESH) for k, (a, s) in enumerate(slabs)]

    def first(*mine):
        for cp in copies(*mine):
            cp.start()

    def middle(*mine):
        pass

    def last(*mine):
        for cp in copies(*mine):
            cp.wait()

    return _Rider(grads, [SDS(g.shape[:1] + g.shape[2:], g.dtype) for g in grads], [len(slabs), len(slabs)],
                  first, middle, last)


def _pair_sum(g, got, c_arr, name):
    n_s, rh, cc = got.shape

    def body(c_ref, g_ref, p_ref, o_ref):
        o_ref[0] = (g_ref[0, 0].astype(F32) + p_ref[0].astype(F32)).astype(BF16)

    grid_spec = pltpu.PrefetchScalarGridSpec(
        num_scalar_prefetch=1, grid=(n_s,),
        in_specs=[pl.BlockSpec((1, 1, rh, cc), lambda s, c_ref: (s, c_ref[0], 0, 0)),
                  pl.BlockSpec((1, rh, cc), lambda s, c_ref: (s, 0, 0))],
        out_specs=pl.BlockSpec((1, rh, cc), lambda s, c_ref: (s, 0, 0)))
    return pl.pallas_call(body, grid_spec=grid_spec, out_shape=SDS((n_s, rh, cc), BF16),
                          compiler_params=_cparams(("parallel",)), name=name)(c_arr, g, got)


def _direct_rider(grads):
    n = len(grads)

    def copies(ins, outs, sems):
        x, y, c, others = _place()
        peers = [(x, y, 1 - c)] + [(ox, oy, c) for ox, oy in others] + [(ox, oy, 1 - c) for ox, oy in others]
        return [pltpu.make_async_remote_copy(
            src_ref=ins[a].at[2 * px + py, pc], dst_ref=outs[a].at[k], send_sem=sems[0].at[7 * a + k],
            recv_sem=sems[1].at[7 * a + k], device_id=(px, py, pc), device_id_type=MESH)
            for a in range(n) for k, (px, py, pc) in enumerate(peers)]

    def first(*mine):
        for cp in copies(*mine):
            cp.start()

    def middle(*mine):
        pass

    def last(*mine):
        for cp in copies(*mine):
            cp.wait()

    return _Rider(grads, [SDS((7,) + g.shape[2:], g.dtype) for g in grads], [7 * n, 7 * n], first, middle, last)


def _owner_sum(grads, landed, own_arr, after, name):
    rh, cc = landed.shape[1:]
    tr = rh // 2

    def body(own_ref, g_ref, p_ref, after_ref, o_ref):
        total = g_ref[0, 0].astype(F32)
        for k in range(7):
            total = total + p_ref[k].astype(F32)
        o_ref[...] = total

    grid_spec = pltpu.PrefetchScalarGridSpec(
        num_scalar_prefetch=1, grid=(2,),
        in_specs=[pl.BlockSpec((1, 1, tr, cc), lambda i, own: (own[0], own[1], i, 0)),
                  pl.BlockSpec((7, tr, cc), lambda i, own: (0, i, 0)), pl.BlockSpec(memory_space=pl.ANY)],
        out_specs=pl.BlockSpec((tr, cc), lambda i, own: (i, 0)))
    return pl.pallas_call(body, grid_spec=grid_spec, out_shape=SDS((rh, cc), F32),
                          compiler_params=_cparams(("parallel",)), name=name)(own_arr, grads, landed, after)


SEM_SPEC = pl.BlockSpec(memory_space=pltpu.SEMAPHORE)
N_LATE = 10


def _late_copies(part_ref, landed_ref, small_ref, all_ref, send_sems, recv_sems):
    x, y, c, others = _place()
    me = 4 * x + 2 * y + c
    peers = [(x, y, 1 - c)] + [(ox, oy, c) for ox, oy in others] + [(ox, oy, 1 - c) for ox, oy in others]
    big = [pltpu.make_async_remote_copy(
        src_ref=part_ref.at[2 * ox + oy], dst_ref=landed_ref.at[j], send_sem=send_sems.at[j], recv_sem=recv_sems.at[j],
        device_id=(ox, oy, c), device_id_type=MESH) for j, (ox, oy) in enumerate(others)]
    small = [pltpu.make_async_remote_copy(
        src_ref=small_ref, dst_ref=all_ref.at[me], send_sem=send_sems.at[3 + k], recv_sem=recv_sems.at[3 + k],
        device_id=peer, device_id_type=MESH) for k, peer in enumerate(peers)]
    return big + small


def _late_exchange_start(part, small):
    def body(part_ref, landed_ref, small_ref, all_ref, send_sems, recv_sems, part_o, landed_o, small_o, all_o, token):
        for cp in _late_copies(part_ref, landed_ref, small_ref, all_ref, send_sems, recv_sems):
            cp.start()
        token[...] = jnp.zeros_like(token)

    hbm = lambda a: pltpu.HBM(a.shape, a.dtype)
    landed = lax.empty((3,) + part.shape[1:], part.dtype)
    everyone = lax.empty((8,) + small.shape, small.dtype)
    operands = [pltpu.with_memory_space_constraint(a, pltpu.HBM) for a in (part, landed, small, everyone)]
    return pl.pallas_call(
        body, name="late_exchange_start",
        out_shape=(pltpu.SemaphoreType.DMA((N_LATE,)), pltpu.SemaphoreType.DMA((N_LATE,)),
                   hbm(part), hbm(landed), hbm(small), hbm(everyone), SDS((8, BLK), F32)),
        in_specs=[HBM_SPEC] * 4,
        out_specs=(SEM_SPEC, SEM_SPEC, HBM_SPEC, HBM_SPEC, HBM_SPEC, HBM_SPEC, pl.BlockSpec(memory_space=pltpu.VMEM)),
        input_output_aliases={0: 2, 1: 3, 2: 4, 3: 5},
        compiler_params=pltpu.CompilerParams(has_side_effects=pltpu.SideEffectType.DATAFLOW_SIDE_EFFECTING),
    )(*operands)


def _late_exchange_wait(send_sems, recv_sems, part, landed, small, everyone, after):
    def body(part_ref, landed_ref, small_ref, all_ref, send_sems, recv_sems, after_ref, part_o, landed_o, small_o, all_o):
        for cp in _late_copies(part_ref, landed_ref, small_ref, all_ref, send_sems, recv_sems):
            cp.wait_send()
            cp.wait_recv()

    hbm = lambda a: pltpu.HBM(a.shape, a.dtype)
    out = pl.pallas_call(
        body, name="late_exchange_wait",
        out_shape=(hbm(part), hbm(landed), hbm(small), hbm(everyone)),
        in_specs=[HBM_SPEC] * 4 + [SEM_SPEC, SEM_SPEC, pl.BlockSpec(memory_space=pl.ANY)],
        out_specs=(HBM_SPEC,) * 4, input_output_aliases={0: 0, 1: 1, 2: 2, 3: 3},
        compiler_params=pltpu.CompilerParams(has_side_effects=pltpu.SideEffectType.DATAFLOW_SIDE_EFFECTING),
    )(part, landed, small, everyone, send_sems, recv_sems, after)
    return out[0], out[1], out[3]


def _chip_sum(parts, landed, chip_arr, name):
    rh, cc = landed.shape[1:]
    tr = rh // 2

    def body(chip_ref, own_ref, p_ref, o_ref):
        o_ref[...] = ((own_ref[0].astype(F32) + p_ref[0].astype(F32)) + p_ref[1].astype(F32)) + p_ref[2].astype(F32)

    grid_spec = pltpu.PrefetchScalarGridSpec(
        num_scalar_prefetch=1, grid=(2,),
        in_specs=[pl.BlockSpec((1, tr, cc), lambda i, chip_ref: (chip_ref[0], i, 0)),
                  pl.BlockSpec((3, tr, cc), lambda i, chip_ref: (0, i, 0))],
        out_specs=pl.BlockSpec((tr, cc), lambda i, chip_ref: (i, 0)))
    return pl.pallas_call(body, grid_spec=grid_spec, out_shape=SDS((rh, cc), F32),
                          compiler_params=_cparams(("parallel",)), name=name)(chip_arr, parts, landed)


def _device_sum(p):
    def body(p_ref, o_ref):
        acc = p_ref[0]
        for k in range(1, 8):
            acc = acc + p_ref[k]
        o_ref[...] = acc

    return pl.pallas_call(body, out_shape=SDS(p.shape[1:], F32), name="small_sum")(p)


def _join_halves(halves, name):
    n = len(halves)

    def body(*refs):
        ins, outs = refs[:n], refs[n:2 * n]
        send_sems, recv_sems = refs[2 * n:]
        x, y, c, _ = _place()
        copies = [pltpu.make_async_remote_copy(
            src_ref=ins[a], dst_ref=outs[a], send_sem=send_sems.at[a], recv_sem=recv_sems.at[a],
            device_id=(x, y, 1 - c), device_id_type=MESH) for a in range(n)]
        for cp in copies:
            cp.start()
        for cp in copies:
            cp.wait()

    return pl.pallas_call(
        body, in_specs=[HBM_SPEC] * n, out_specs=[HBM_SPEC] * n,
        out_shape=[SDS(h.shape, h.dtype) for h in halves],
        scratch_shapes=[pltpu.SemaphoreType.DMA((n,)), pltpu.SemaphoreType.DMA((n,))],
        name=name)(*halves)


def _adamw(w, g, m, v, name, tr=None):
    rows, cols = w.shape
    tr = tr or rows
    assert rows % tr == 0

    def body(w_ref, g_ref, m_ref, v_ref, d_ref, nm_ref, nv_ref):
        gg = g_ref[...]
        nm = ADAM_B1 * m_ref[...] + (1.0 - ADAM_B1) * gg
        nv = ADAM_B2 * v_ref[...] + (1.0 - ADAM_B2) * (gg * gg)
        nm_ref[...] = nm
        nv_ref[...] = nv
        m_hat = nm / (1.0 - ADAM_B1 ** ADAM_STEP)
        v_hat = nv / (1.0 - ADAM_B2 ** ADAM_STEP)
        d_ref[...] = -ADAM_LR * (m_hat / (jnp.sqrt(v_hat) + ADAM_EPS) + ADAM_WD * w_ref[...])

    blk = pl.BlockSpec((tr, cols), lambda i: (i, 0))
    return pl.pallas_call(
        body, grid=(rows // tr,), in_specs=[blk] * 4, out_specs=[blk] * 3,
        out_shape=[SDS((rows, cols), F32)] * 3,
        compiler_params=_cparams(("parallel",)), name=name)(w, g, m, v)


def _adamw_halves(w, mine, theirs, m, v, c_arr, name):
    rows, cols = w.shape
    rh = rows // 2
    tr = rh if rh <= 352 else 256
    nh = rh // tr

    def body(c_ref, w_ref, mine_ref, theirs_ref, m_ref, v_ref, g_ref, d_ref, nm_ref, nv_ref):
        own = jnp.full((tr, cols), pl.program_id(0), jnp.int32) == c_ref[0]
        gg = jnp.where(own, mine_ref[...], theirs_ref[...])
        g_ref[...] = gg
        nm = ADAM_B1 * m_ref[...] + (1.0 - ADAM_B1) * gg
        nv = ADAM_B2 * v_ref[...] + (1.0 - ADAM_B2) * (gg * gg)
        nm_ref[...] = nm
        nv_ref[...] = nv
        m_hat = nm / (1.0 - ADAM_B1 ** ADAM_STEP)
        v_hat = nv / (1.0 - ADAM_B2 ** ADAM_STEP)
        d_ref[...] = -ADAM_LR * (m_hat / (jnp.sqrt(v_hat) + ADAM_EPS) + ADAM_WD * w_ref[...])

    whole = pl.BlockSpec((tr, cols), lambda hh, i, c_ref: (hh * nh + i, 0))
    part = pl.BlockSpec((tr, cols), lambda hh, i, c_ref: (i, 0))
    grid_spec = pltpu.PrefetchScalarGridSpec(
        num_scalar_prefetch=1, grid=(2, nh), in_specs=[whole, part, part, whole, whole], out_specs=[whole] * 4)
    return pl.pallas_call(body, grid_spec=grid_spec, out_shape=[SDS((rows, cols), F32)] * 4,
                          compiler_params=_cparams(("parallel", "parallel")), name=name)(c_arr, w, mine, theirs, m, v)


def _pack_small(pre_mix, post_mix, pre_ffn, post_ffn, rel_bias, b_forget, sinks):
    def at(row, v):
        return jnp.pad(v, ((row, 7 - row), (0, D_MODEL - v.shape[1])))
    return (at(0, pre_mix) + at(1, post_mix) + at(2, pre_ffn) + at(3, post_ffn)
            + at(4, rel_bias.reshape(1, N_BUCKETS * N_HEADS)) + at(5, jnp.concatenate([b_forget, sinks], axis=1)))


def _unpack_small(p):
    return dict(ln_pre_mix=p[0:1], ln_post_mix=p[1:2], ln_pre_ffn=p[2:3], ln_post_ffn=p[3:4],
                rel_bias=p[4, :N_BUCKETS * N_HEADS].reshape(N_BUCKETS, N_HEADS),
                b_forget=p[5:6, 0:N_HEADS], sinks=p[5:6, N_HEADS:2 * N_HEADS])


WEIGHTS = ("meta_tokens", "rel_bias", "ln_pre_mix", "ln_post_mix", "ln_pre_ffn", "ln_post_ffn",
           "w_in", "b_forget", "sinks", "w_out", "w_gate_up", "w_down")


def kernel(x, meta_tokens, rel_bias, ln_pre_mix, ln_post_mix, ln_pre_ffn, ln_post_ffn, w_in, b_forget, sinks, w_out, w_gate_up, w_down, loss_target, m_meta_tokens, m_rel_bias, m_ln_pre_mix, m_ln_post_mix, m_ln_pre_ffn, m_ln_post_ffn, m_w_in, m_b_forget, m_sinks, m_w_out, m_w_gate_up, m_w_down, v_meta_tokens, v_rel_bias, v_ln_pre_mix, v_ln_post_mix, v_ln_pre_ffn, v_ln_post_ffn, v_w_in, v_b_forget, v_sinks, v_w_out, v_w_gate_up, v_w_down):
    xi, yi, ci = lax.axis_index("x"), lax.axis_index("y"), lax.axis_index("c")
    chip = 2 * xi + yi
    c_arr = jnp.reshape(ci, (1,)).astype(jnp.int32)

    def halves(w, dtype):
        return w.astype(dtype).reshape(2, w.shape[0] // 2, w.shape[1])

    def with_own(gathered, shards):
        return [lax.dynamic_update_slice(got, own[None], (chip, 0, 0, 0)) for got, own in zip(gathered, shards)]

    shards = [halves(w_in[0], BF16), halves(meta_tokens, F32)]
    gw_in, g_meta = with_own(_run_alone(_gather_rider(shards, False), "gather_mixer_weights"), shards)
    out_shards = [halves(w_out[0], BF16)]
    ffn_shards = [halves(w_gate_up[0], BF16), halves(w_down[0], BF16)]

    def ffn_weights(carried):
        gw_gu, gw_dn = carried
        return gw_gu.reshape(D_MODEL, 2 * D_FF), gw_dn.reshape(D_FF, D_MODEL)

    early = {}

    def early_grads(dw_gu, dw_dn, dw_out):
        early["grads"] = [dw_out.reshape(4, 2, 128, D_MODEL), dw_gu.reshape(4, 2, 512, FF_T),
                          dw_dn.reshape(4, 2, 352, D_MODEL)]
        return _direct_rider(early["grads"])
    w_in_all = gw_in.reshape(4, D_MODEL, D_PROJ // 4).transpose(1, 0, 2).reshape(D_MODEL, D_PROJ)
    w_in_b = jnp.pad(w_in_all, ((0, 0), (0, D_PROJ_P - D_PROJ)))
    meta_all = g_meta.reshape(4, N_META, D_MODEL // 4).transpose(1, 0, 2).reshape(N_META, D_MODEL)

    loc = _local_step(x[0], loss_target[0], meta_all, rel_bias, ln_pre_mix, ln_post_mix, ln_pre_ffn, ln_post_ffn,
                      b_forget, sinks, w_in_b, _gather_rider(out_shards, True),
                      lambda carried: carried[0].reshape(D_MODEL, D_MODEL),
                      _gather_rider(ffn_shards, True, by_columns=(0,)), ffn_weights, early_grads)

    small = jnp.concatenate(
        [_pack_small(loc["ln_pre_mix"], loc["ln_post_mix"], loc["ln_pre_ffn"], loc["ln_post_ffn"],
                     loc["rel_bias"], loc["b_forget"], loc["sinks"])
         + jnp.pad(loc["loss"].reshape(1, 1), ((LOSS_ROW, 7 - LOSS_ROW), (0, D_MODEL - 1))), loc["meta"]], axis=0)

    dw_in = loc["w_in"].reshape(1, 2, D_MODEL // 2, D_PROJ_P)
    (got_in,) = _run_alone(_swap_rider([dw_in]), "swap_halves_late")
    half_sum = _pair_sum(dw_in, got_in, c_arr, "pair_sum_late")
    part_in = half_sum[0, :, :D_PROJ].reshape(D_MODEL // 2, 4, D_PROJ // 4).transpose(1, 0, 2)
    send_sems, recv_sems, part_sent, landing, small_sent, everyone, token = _late_exchange_start(part_in, small)
    chip_arr = jnp.reshape(chip, (1,)).astype(jnp.int32)
    own_arr = jnp.stack([chip, ci]).astype(jnp.int32)
    grad, delta, new_m, new_v = {}, {}, {}, {}

    def update(names, mine):
        theirs = _join_halves(mine, "join_" + names[0])
        big = dict(w_in=(w_in, m_w_in, v_w_in), w_out=(w_out, m_w_out, v_w_out),
                   w_gate_up=(w_gate_up, m_w_gate_up, v_w_gate_up), w_down=(w_down, m_w_down, v_w_down))
        for name, g_mine, g_theirs in zip(names, mine, theirs):
            w, m, v = big[name]
            g, d, nm, nv = _adamw_halves(w[0], g_mine, g_theirs, m[0], v[0], c_arr, "adamw_" + name)
            grad[name], delta[name], new_m[name], new_v[name] = g[None], d[None], nm[None], nv[None]

    update(("w_out", "w_gate_up", "w_down"),
           [_owner_sum(g, l, own_arr, token, "owner_sum_%d" % a)
            for a, (g, l) in enumerate(zip(early["grads"], loc["landed"]))])
    part_back, landed_in, small_all = _late_exchange_wait(send_sems, recv_sems, part_sent, landing, small_sent,
                                                         everyone, new_v["w_down"])
    mine_in = _chip_sum(part_back, landed_in, chip_arr, "chip_sum_in")
    (theirs_in,) = _join_halves([mine_in], "join_w_in")
    g_w_in = jnp.where(ci == 0, jnp.concatenate([mine_in, theirs_in], axis=0),
                       jnp.concatenate([theirs_in, mine_in], axis=0))
    view = lambda a: jnp.transpose(a).reshape(D_PROJ // 4 * 8, BLK)
    back = lambda a: jnp.transpose(a.reshape(D_PROJ // 4, D_MODEL))[None]
    d, nm, nv = _adamw(view(w_in[0]), view(g_w_in), view(m_w_in[0]), view(v_w_in[0]), "adamw_w_in",
                       tr=D_PROJ // 4 * 4)
    grad["w_in"], delta["w_in"], new_m["w_in"], new_v["w_in"] = g_w_in[None], back(d), back(nm), back(nv)
    me = 4 * xi + 2 * yi + ci
    small_sum = _device_sum(lax.dynamic_update_slice(small_all, small[None], (me, 0, 0)))
    g_meta_tokens = lax.dynamic_slice(small_sum[8:N_SMALL], (0, chip * (D_MODEL // 4)), (N_META, D_MODEL // 4))
    g_small = small_sum[0:8]
    grad.update(_unpack_small(g_small))
    grad.update(meta_tokens=g_meta_tokens)
    delta["meta_tokens"], new_m["meta_tokens"], new_v["meta_tokens"] = _adamw(
        meta_tokens, g_meta_tokens, m_meta_tokens, v_meta_tokens, "adamw_meta")
    d, nm, nv = _adamw(
        _pack_small(ln_pre_mix, ln_post_mix, ln_pre_ffn, ln_post_ffn, rel_bias, b_forget, sinks), g_small,
        _pack_small(m_ln_pre_mix, m_ln_post_mix, m_ln_pre_ffn, m_ln_post_ffn, m_rel_bias, m_b_forget, m_sinks),
        _pack_small(v_ln_pre_mix, v_ln_post_mix, v_ln_pre_ffn, v_ln_post_ffn, v_rel_bias, v_b_forget, v_sinks),
        "adamw_small")
    delta.update(_unpack_small(d))
    new_m.update(_unpack_small(nm))
    new_v.update(_unpack_small(nv))

    loss = small_sum[LOSS_ROW, 0]
    return (loss,loc["grad_x"][None], *[grad[k] for k in WEIGHTS], *[delta[k] for k in WEIGHTS],
            *[new_m[k] for k in WEIGHTS], *[new_v[k] for k in WEIGHTS])
```

```python
import math

import numpy as np
import jax
import jax.numpy as jnp
from jax import lax
from jax.experimental import pallas as pl
from jax.experimental.pallas import tpu as pltpu

F32 = jnp.float32
BF16 = jnp.bfloat16
MESH = pl.DeviceIdType.MESH
SDS = jax.ShapeDtypeStruct

D_MODEL = 1024
SEQ = 4096
N_META = 16
N_HEADS = 8
HALF = 64
D_FF = 2816
N_BUCKETS = 32
EPS = 1e-6
NEG = -1e30
SCALE = 0.125
LOG2E = 1.4426950408889634
LN2 = 0.6931471805599453
PAD_ROWS = 112
ROW0 = PAD_ROWS + N_META
LP = ROW0 + SEQ
BLK = 128
NBLK = LP // BLK
TM = 384
NT = LP // TM
TM_PURE = LP // 2
TM_MID = LP // 4
TM_EPI = LP // 6
TN = 256
D_PROJ = 2312
D_PROJ_P = 2432
D_QKV = 2304
FF_T = 1408
VMEM_LIMIT = 56 * 1024 * 1024

ADAM_LR = 0.001
ADAM_B1 = 0.9
ADAM_B2 = 0.999
ADAM_EPS = 1e-08
ADAM_WD = 0.01
ADAM_STEP = 10

QA = 0
KA, VA = 4, 5
QB, KB, VB = 3, 5, 7
W2 = 256

NT_DIMS = (((1,), (1,)), ((), ()))
TN_DIMS = (((0,), (0,)), ((), ()))


def _cparams(sem):
    return pltpu.CompilerParams(dimension_semantics=sem, vmem_limit_bytes=VMEM_LIMIT)


def _t5_bucket_np(d):
    n = np.maximum(d, 0).astype(np.int32)
    nf = np.maximum(n, 1).astype(np.float32)
    large = 16 + (np.log(nf / np.float32(16)) / np.float32(math.log(8.0)) * np.float32(16)).astype(np.int32)
    large = np.minimum(large, N_BUCKETS - 1)
    return np.where(n < 16, n, large).astype(np.int32)


def _bucket_tables():
    qi = np.arange(BLK)[:, None]
    ki = np.arange(BLK)[None, :]
    return np.stack([_t5_bucket_np(qi - ki), _t5_bucket_np(qi - ki + BLK)])


def _rms(x):
    return lax.rsqrt(jnp.mean(x * x, axis=-1, keepdims=True) + EPS)


def _rms_bwd(n, r, gdy):
    return r * (gdy - n * jnp.mean(n * gdy, axis=-1, keepdims=True))


def _pre_mix(h0, gain, w_in_b):
    half = D_QKV // 2

    def body(h_ref, g_ref, w_ref, hn_ref, proj_ref, f_ref):
        x = h_ref[...]
        hn = (x * _rms(x) * g_ref[...]).astype(BF16)
        hn_ref[...] = hn
        proj_ref[:, :half] = jnp.dot(hn, w_ref[:, :half], preferred_element_type=F32).astype(BF16)
        p = jnp.dot(hn, w_ref[:, half:], preferred_element_type=F32)
        proj_ref[:, half:] = p[:, :half].astype(BF16)
        f_ref[...] = p[:, half:]

    return pl.pallas_call(
        body, grid=(LP // TM_MID,),
        in_specs=[pl.BlockSpec((TM_MID, D_MODEL), lambda i: (i, 0)),
                  pl.BlockSpec((1, D_MODEL), lambda i: (0, 0)),
                  pl.BlockSpec((D_MODEL, D_PROJ_P), lambda i: (0, 0))],
        out_specs=[pl.BlockSpec((TM_MID, D_MODEL), lambda i: (i, 0)),
                   pl.BlockSpec((TM_MID, D_QKV), lambda i: (i, 0)),
                   pl.BlockSpec((TM_MID, BLK), lambda i: (i, 0))],
        out_shape=[SDS((LP, D_MODEL), BF16), SDS((LP, D_QKV), BF16), SDS((LP, BLK), F32)],
        compiler_params=_cparams(("parallel",)), name="pre_mix")(h0, gain, w_in_b)


def _attn_out(o_a, o_b, w_out_b, h0, g_post, g_pre_ffn):
    def body(oa_ref, ob_ref, w_ref, h0_ref, gp_ref, gf_ref, a_ref, h1_ref, hn2_ref):
        a = (jnp.dot(oa_ref[...], w_ref[0:512, :], preferred_element_type=F32)
             + jnp.dot(ob_ref[...], w_ref[512:1024, :], preferred_element_type=F32))
        a_ref[...] = a
        h1 = h0_ref[...] + a * _rms(a) * gp_ref[...]
        h1_ref[...] = h1
        hn2_ref[...] = (h1 * _rms(h1) * gf_ref[...]).astype(BF16)

    row = lambda w: pl.BlockSpec((TM_EPI, w), lambda i: (i, 0))
    vec = pl.BlockSpec((1, D_MODEL), lambda i: (0, 0))
    return pl.pallas_call(
        body, grid=(LP // TM_EPI,),
        in_specs=[row(512), row(512), pl.BlockSpec((D_MODEL, D_MODEL), lambda i: (0, 0)), row(D_MODEL), vec, vec],
        out_specs=[row(D_MODEL), row(D_MODEL), row(D_MODEL)],
        out_shape=[SDS((LP, D_MODEL), F32), SDS((LP, D_MODEL), F32), SDS((LP, D_MODEL), BF16)],
        compiler_params=_cparams(("parallel",)), name="attn_out")(o_a, o_b, w_out_b, h0, g_post, g_pre_ffn)


def _ffn_up(hn2, w_gu_b):
    def body(x_ref, wg_ref, wu_ref, g_ref, u_ref, act_ref):
        x = x_ref[...]
        g = jnp.dot(x, wg_ref[...], preferred_element_type=F32)
        u = jnp.dot(x, wu_ref[...], preferred_element_type=F32)
        g_ref[...] = g.astype(BF16)
        u_ref[...] = u.astype(BF16)
        act_ref[...] = (g * (1.0 / (1.0 + jnp.exp(-g))) * u).astype(BF16)

    out = pl.BlockSpec((LP, TN), lambda j: (0, j))
    return pl.pallas_call(
        body, grid=(D_FF // TN,),
        in_specs=[pl.BlockSpec((LP, D_MODEL), lambda j: (0, 0)),
                  pl.BlockSpec((D_MODEL, TN), lambda j: (0, j)),
                  pl.BlockSpec((D_MODEL, TN), lambda j: (0, j + D_FF // TN))],
        out_specs=[out, out, out],
        out_shape=[SDS((LP, D_FF), BF16)] * 3,
        compiler_params=_cparams(("parallel",)), name="ffn_up")(hn2, w_gu_b, w_gu_b)


def _ffn_down_loss(act, w_dn_b, h1, tgt, g_post_ffn):
    def body(act_ref, w_ref, h1_ref, t0_ref, t1_ref, t2_ref, g_ref, dff_ref, dy_ref, loss_ref, dg_ref):
        i = pl.program_id(0)
        target = jnp.concatenate([t0_ref[...], t1_ref[...], t2_ref[...]], axis=0)

        @pl.when(i == 0)
        def _():
            loss_ref[...] = jnp.zeros_like(loss_ref)
            dg_ref[...] = jnp.zeros_like(dg_ref)

        ff = jnp.dot(act_ref[...], w_ref[...], preferred_element_type=F32)
        r = _rms(ff)
        n = ff * r
        g = g_ref[...]
        y = h1_ref[...] + n * g
        rows = i * TM + lax.broadcasted_iota(jnp.int32, (TM, D_MODEL), 0)
        diff = jnp.where(rows >= ROW0, y - target, 0.0)
        loss_ref[...] += 0.5 * jnp.sum(diff * diff) / D_MODEL
        dy = diff / D_MODEL
        dy_ref[...] = dy
        dg_ref[...] += jnp.sum(dy * n, axis=0, keepdims=True)
        dff_ref[...] = _rms_bwd(n, r, g * dy).astype(BF16)

    row = pl.BlockSpec((TM, D_MODEL), lambda i: (i, 0))
    tblk = lambda j: pl.BlockSpec((BLK, D_MODEL), lambda i: (jnp.maximum(3 * i - 1 + j, 0), 0))
    return pl.pallas_call(
        body, grid=(NT,),
        in_specs=[pl.BlockSpec((TM, D_FF), lambda i: (i, 0)), pl.BlockSpec((D_FF, D_MODEL), lambda i: (0, 0)),
                  row, tblk(0), tblk(1), tblk(2), pl.BlockSpec((1, D_MODEL), lambda i: (0, 0))],
        out_specs=[row, row, pl.BlockSpec((8, BLK), lambda i: (0, 0)), pl.BlockSpec((1, D_MODEL), lambda i: (0, 0))],
        out_shape=[SDS((LP, D_MODEL), BF16), SDS((LP, D_MODEL), F32), SDS((8, BLK), F32), SDS((1, D_MODEL), F32)],
        compiler_params=_cparams(("arbitrary",)), name="ffn_down_loss")(act, w_dn_b, h1, tgt, tgt, tgt, g_post_ffn)


def _ffn_down_bwd(dff, w_dn_b, g, u):
    def body(d_ref, w_ref, g_ref, u_ref, dg_ref, du_ref):
        dact = lax.dot_general(d_ref[...], w_ref[...], NT_DIMS, preferred_element_type=F32)
        gg = g_ref[...].astype(F32)
        sig = 1.0 / (1.0 + jnp.exp(-gg))
        dg_ref[...] = (dact * u_ref[...].astype(F32) * sig * (1.0 + gg * (1.0 - sig))).astype(BF16)
        du_ref[...] = (dact * gg * sig).astype(BF16)

    blk = pl.BlockSpec((LP, TN), lambda j: (0, j))
    return pl.pallas_call(
        body, grid=(D_FF // TN,),
        in_specs=[pl.BlockSpec((LP, D_MODEL), lambda j: (0, 0)),
                  pl.BlockSpec((TN, D_MODEL), lambda j: (j, 0)), blk, blk],
        out_specs=[blk, blk],
        out_shape=[SDS((LP, D_FF), BF16)] * 2,
        compiler_params=_cparams(("parallel",)), name="ffn_down_bwd")(dff, w_dn_b, g, u)


def _ffn_up_bwd(dg, du, w_gu_b, h1, a, dy, g_pre_ffn, g_post_mix):
    def body(dg_ref, du_ref, w_ref, h1_ref, a_ref, dy_ref, gf_ref, gp_ref,
             dh1_ref, da_ref, dgf_ref, dgp_ref, acc):
        i = pl.program_id(0)
        s = pl.program_id(1)

        @pl.when((i == 0) & (s == 0))
        def _():
            dgf_ref[...] = jnp.zeros_like(dgf_ref)
            dgp_ref[...] = jnp.zeros_like(dgp_ref)

        @pl.when(s == 0)
        def _():
            acc[...] = jnp.zeros_like(acc)

        @pl.when(s < 2)
        def _():
            acc[...] += lax.dot_general(dg_ref[...], w_ref[...], NT_DIMS, preferred_element_type=F32)

        @pl.when(s >= 2)
        def _():
            acc[...] += lax.dot_general(du_ref[...], w_ref[...], NT_DIMS, preferred_element_type=F32)

        @pl.when(s == 3)
        def _():
            dhn2 = acc[...]
            h1 = h1_ref[...]
            r2 = _rms(h1)
            n2 = h1 * r2
            dgf_ref[...] += jnp.sum(dhn2 * n2, axis=0, keepdims=True)
            dh1 = dy_ref[...] + _rms_bwd(n2, r2, gf_ref[...] * dhn2)
            dh1_ref[...] = dh1
            av = a_ref[...]
            ra = _rms(av)
            na = av * ra
            dgp_ref[...] += jnp.sum(dh1 * na, axis=0, keepdims=True)
            da_ref[...] = _rms_bwd(na, ra, gp_ref[...] * dh1).astype(BF16)

    row = pl.BlockSpec((TM_EPI, D_MODEL), lambda i, s: (i, 0))
    vec = pl.BlockSpec((1, D_MODEL), lambda i, s: (0, 0))
    return pl.pallas_call(
        body, grid=(LP // TM_EPI, 4),
        in_specs=[pl.BlockSpec((TM_EPI, FF_T), lambda i, s: (i, jnp.minimum(s, 1))),
                  pl.BlockSpec((TM_EPI, FF_T), lambda i, s: (i, jnp.maximum(s - 2, 0))),
                  pl.BlockSpec((D_MODEL, FF_T), lambda i, s: (0, s)),
                  row, row, row, vec, vec],
        out_specs=[row, row, vec, vec],
        out_shape=[SDS((LP, D_MODEL), F32), SDS((LP, D_MODEL), BF16), SDS((1, D_MODEL), F32), SDS((1, D_MODEL), F32)],
        scratch_shapes=[pltpu.VMEM((TM_EPI, D_MODEL), F32)],
        compiler_params=_cparams(("arbitrary", "arbitrary")), name="ffn_up_bwd",
    )(dg, du, w_gu_b, h1, a, dy, g_pre_ffn, g_post_mix)


def _attn_out_bwd(da, w_out_b):
    def body(d_ref, w_ref, o_ref):
        o_ref[...] = lax.dot_general(d_ref[...], w_ref[...], NT_DIMS, preferred_element_type=F32).astype(BF16)

    row = pl.BlockSpec((TM_PURE, D_MODEL), lambda i: (i, 0))
    return pl.pallas_call(
        body, grid=(LP // TM_PURE,),
        in_specs=[row, pl.BlockSpec((D_MODEL, D_MODEL), lambda i: (0, 0))],
        out_specs=row, out_shape=SDS((LP, D_MODEL), BF16),
        compiler_params=_cparams(("parallel",)), name="attn_out_bwd")(da, w_out_b)


def _pre_mix_bwd(dq_a, dq_b, dk_b, dv_b, dk_a, dv_a, df, w_in_b, h0, dh1, g_pre_mix):
    def body(qa_ref, qb_ref, kb_ref, vb_ref, ka_ref, va_ref, f_ref, w_ref, h0_ref, dh1_ref, g_ref,
             dproj_ref, dh0_ref, dg_ref):
        i = pl.program_id(0)

        @pl.when(i == 0)
        def _():
            dg_ref[...] = jnp.zeros_like(dg_ref)

        dproj = jnp.concatenate(
            [qa_ref[...], ka_ref[...].astype(BF16), va_ref[...].astype(BF16), (qb_ref[...] * SCALE).astype(BF16),
             kb_ref[...], vb_ref[...], f_ref[...].astype(BF16)], axis=1)
        dproj_ref[...] = dproj
        dhn = lax.dot_general(dproj, w_ref[...], NT_DIMS, preferred_element_type=F32)
        x = h0_ref[...]
        r = _rms(x)
        n = x * r
        dg_ref[...] += jnp.sum(dhn * n, axis=0, keepdims=True)
        dh0_ref[...] = dh1_ref[...] + _rms_bwd(n, r, g_ref[...] * dhn)

    row = lambda w: pl.BlockSpec((TM_EPI, w), lambda i: (i, 0))
    vec = pl.BlockSpec((1, D_MODEL), lambda i: (0, 0))
    return pl.pallas_call(
        body, grid=(LP // TM_EPI,),
        in_specs=[row(512), row(512), row(512), row(512), row(BLK), row(BLK), row(BLK),
                  pl.BlockSpec((D_MODEL, D_PROJ_P), lambda i: (0, 0)), row(D_MODEL), row(D_MODEL), vec],
        out_specs=[row(D_PROJ_P), row(D_MODEL), vec],
        out_shape=[SDS((LP, D_PROJ_P), BF16), SDS((LP, D_MODEL), F32), SDS((1, D_MODEL), F32)],
        compiler_params=_cparams(("arbitrary",)), name="pre_mix_bwd",
    )(dq_a, dq_b, dk_b, dv_b, dk_a, dv_a, df, w_in_b, h0, dh1, g_pre_mix)


def _mm_tn(parts, b, tm, name, out_dtype=F32):
    widths = [p.shape[1] for p in parts]
    m_total = sum(widths)
    n = b.shape[1]
    whole = len(parts) > 1
    n_k = LP // TM_MID
    assert (tm == m_total) if whole else (m_total % tm == 0)

    def body(*refs):
        a_refs, b_ref, o_ref, acc = refs[:-3], refs[-3], refs[-2], refs[-1]
        k = pl.program_id(1)

        @pl.when(k == 0)
        def _():
            acc[...] = jnp.zeros_like(acc)
        a = a_refs[0][...] if not whole else jnp.concatenate([r[...] for r in a_refs], axis=1)
        acc[...] += lax.dot_general(a, b_ref[...], TN_DIMS, preferred_element_type=F32)

        @pl.when(k == n_k - 1)
        def _():
            o_ref[...] = acc[...].astype(out_dtype)

    a_specs = ([pl.BlockSpec((TM_MID, w), lambda mi, k: (k, 0)) for w in widths] if whole
               else [pl.BlockSpec((TM_MID, tm), lambda mi, k: (k, mi))])
    return pl.pallas_call(
        body, grid=(m_total // tm, n_k),
        in_specs=a_specs + [pl.BlockSpec((TM_MID, n), lambda mi, k: (k, 0))],
        out_specs=pl.BlockSpec((tm, n), lambda mi, k: (mi, 0)),
        out_shape=SDS((m_total, n), out_dtype),
        scratch_shapes=[pltpu.VMEM((tm, n), F32)],
        compiler_params=_cparams(("parallel", "arbitrary")), name=name)(*parts, b)


def _dw_gate_up(hn2, dg, du):
    n_k = LP // TM_MID

    def body(a_ref, dg_ref, du_ref, o_ref, acc):
        s = pl.program_id(0)
        k = pl.program_id(1)

        @pl.when(k == 0)
        def _():
            acc[...] = jnp.zeros_like(acc)

        @pl.when(s < 2)
        def _():
            acc[...] += lax.dot_general(a_ref[...], dg_ref[...], TN_DIMS, preferred_element_type=F32)

        @pl.when(s >= 2)
        def _():
            acc[...] += lax.dot_general(a_ref[...], du_ref[...], TN_DIMS, preferred_element_type=F32)

        @pl.when(k == n_k - 1)
        def _():
            o_ref[0] = acc[...].astype(BF16)

    return pl.pallas_call(
        body, grid=(4, n_k),
        in_specs=[pl.BlockSpec((TM_MID, D_MODEL), lambda s, k: (k, 0)),
                  pl.BlockSpec((TM_MID, FF_T), lambda s, k: (k, jnp.minimum(s, 1))),
                  pl.BlockSpec((TM_MID, FF_T), lambda s, k: (k, jnp.maximum(s - 2, 0)))],
        out_specs=pl.BlockSpec((1, D_MODEL, FF_T), lambda s, k: (s, 0, 0)),
        out_shape=SDS((4, D_MODEL, FF_T), BF16),
        scratch_shapes=[pltpu.VMEM((D_MODEL, FF_T), F32)],
        compiler_params=_cparams(("parallel", "arbitrary")), name="dw_gate_up")(hn2, dg, du)


def _split3(x):
    hi = x.astype(BF16)
    r1 = x - hi.astype(F32)
    mid = r1.astype(BF16)
    lo = (r1 - mid.astype(F32)).astype(BF16)
    return hi, mid, lo


def _tri_matmul(tri, x):
    hi, mid, lo = _split3(x)
    dot = lambda t: jnp.dot(tri, t, preferred_element_type=F32)
    return dot(hi) + dot(mid) + dot(lo)


def _forget_cumsum(f, b_forget_p):
    def body(f_ref, b_ref, cum_ref, carry):
        i = pl.program_id(0)

        @pl.when(i == 0)
        def _():
            carry[...] = jnp.zeros_like(carry)

        z = f_ref[...] + b_ref[...]
        ls = jnp.minimum(z, 0.0) - jnp.log(1.0 + jnp.exp(-jnp.abs(z)))
        rows = i * TM + lax.broadcasted_iota(jnp.int32, (TM, BLK), 0)
        ls = jnp.where(rows >= PAD_ROWS, ls, 0.0)
        r = lax.broadcasted_iota(jnp.int32, (TM, TM), 0)
        c = lax.broadcasted_iota(jnp.int32, (TM, TM), 1)
        tri = (c <= r).astype(BF16)
        cum = _tri_matmul(tri, ls) + carry[...]
        cum_ref[...] = cum
        carry[...] = cum[TM - 1:TM, :]

    return pl.pallas_call(
        body, grid=(NT,),
        in_specs=[pl.BlockSpec((TM, BLK), lambda i: (i, 0)), pl.BlockSpec((1, BLK), lambda i: (0, 0))],
        out_specs=pl.BlockSpec((TM, BLK), lambda i: (i, 0)),
        out_shape=SDS((LP, BLK), F32),
        scratch_shapes=[pltpu.VMEM((1, BLK), F32)],
        compiler_params=_cparams(("arbitrary",)), name="forget_cumsum")(f, b_forget_p)


def _forget_cumsum_bwd(dcum, f, b_forget_p):
    def body(d_ref, f_ref, b_ref, df_ref, db_ref, carry):
        i = pl.program_id(0)

        @pl.when(i == 0)
        def _():
            carry[...] = jnp.zeros_like(carry)
            db_ref[...] = jnp.zeros_like(db_ref)

        blk = NT - 1 - i
        r = lax.broadcasted_iota(jnp.int32, (TM, TM), 0)
        c = lax.broadcasted_iota(jnp.int32, (TM, TM), 1)
        tri = (c >= r).astype(BF16)
        d = d_ref[...]
        dls = _tri_matmul(tri, d) + carry[...]
        carry[...] = dls[0:1, :]
        z = f_ref[...] + b_ref[...]
        rows = blk * TM + lax.broadcasted_iota(jnp.int32, (TM, BLK), 0)
        df = jnp.where(rows >= PAD_ROWS, dls / (1.0 + jnp.exp(z)), 0.0)
        df_ref[...] = df
        db_ref[...] += jnp.sum(df, axis=0, keepdims=True)

    rev = pl.BlockSpec((TM, BLK), lambda i: (NT - 1 - i, 0))
    vec = pl.BlockSpec((1, BLK), lambda i: (0, 0))
    return pl.pallas_call(
        body, grid=(NT,),
        in_specs=[rev, rev, vec],
        out_specs=[rev, vec],
        out_shape=[SDS((LP, BLK), F32), SDS((1, BLK), F32)],
        scratch_shapes=[pltpu.VMEM((1, BLK), F32)],
        compiler_params=_cparams(("arbitrary",)), name="forget_cumsum_bwd")(dcum, f, b_forget_p)


def _lane_half(rows):
    return lax.broadcasted_iota(jnp.int32, (rows, BLK), 1) // HALF


def _fox_valid(qi, kj):
    qrow = qi * TM + lax.broadcasted_iota(jnp.int32, (TM, TM), 0)
    krow = kj * TM + lax.broadcasted_iota(jnp.int32, (TM, TM), 1)
    return (krow <= qrow) & ((krow >= PAD_ROWS) | (qrow < PAD_ROWS))


class _Rider:
    def __init__(self, operands, out_shapes, sem_counts, first, middle, last):
        self.operands, self.out_shapes, self.sem_counts = list(operands), list(out_shapes), list(sem_counts)
        self.first, self.middle, self.last = first, middle, last

    def scratch(self):
        return [pltpu.SemaphoreType.DMA((k,)) for k in self.sem_counts]

    def split(self, refs, n_in, n_out, n_scratch):
        a, b = len(self.operands), len(self.out_shapes)
        ins, mine_in = refs[:n_in], refs[n_in:n_in + a]
        outs, mine_out = refs[n_in + a:n_in + a + n_out], refs[n_in + a + n_out:n_in + a + n_out + b]
        rest = refs[n_in + a + n_out + b:]
        return ins, outs, rest[:n_scratch], (mine_in, mine_out, rest[n_scratch:])

    def at_steps(self, mine, is_first, is_middle, is_last):
        for cond, fn in ((is_first, self.first), (is_middle, self.middle), (is_last, self.last)):
            pl.when(cond)(lambda fn=fn: fn(*mine))


HBM_SPEC = pl.BlockSpec(memory_space=pltpu.HBM)


N_AUG = 3
QCHUNKS = ((0, 128), (128, 128), (256, 128))
KSUB = 384
AHEAD = 5
AHEAD_BWD = 1


def _fox_prep(proj, cum):
    def body(q0_ref, q1_ref, k0_ref, k1_ref, v0_ref, v1_ref, c_ref, qa_ref, ka_ref, vt_ref):
        half = _lane_half(TM)
        lane = lax.broadcasted_iota(jnp.int32, (TM, BLK), 1)
        for pp in range(4):
            cols = slice(pp * BLK, (pp + 1) * BLK)
            q_ref, k_ref, v_ref = ((q0_ref, k0_ref, v0_ref), (q1_ref, k1_ref, v1_ref))[pp // 2]
            part = slice((pp % 2) * BLK, (pp % 2 + 1) * BLK)
            qs = q_ref[:, part].astype(F32) * (SCALE * LOG2E)
            kp = k_ref[:, part].astype(F32)
            vp = v_ref[:, part]
            vt_ref[cols, :] = vp.astype(F32).T.astype(BF16)
            for e in range(2):
                h = 2 * pp + e
                a = (1 - e) * HALF
                blk = slice(h * BLK, (h + 1) * BLK)
                hi, mid, lo = _split3(-LOG2E * c_ref[:, h:h + 1])
                q_aug = jnp.where(half == e, qs, jnp.where((lane >= a) & (lane < a + N_AUG), 1.0, 0.0))
                k_aug = jnp.where(half == e, kp, jnp.where(
                    lane == a, hi.astype(F32), jnp.where(lane == a + 1, mid.astype(F32), jnp.where(
                        lane == a + 2, lo.astype(F32), 0.0))))
                qa_ref[blk, :] = q_aug.T.astype(BF16)
                ka_ref[:, blk] = k_aug.astype(BF16)

    row = lambda blk: pl.BlockSpec((TM, W2), lambda i: (i, blk))
    wide = pl.BlockSpec((TM, 1024), lambda i: (i, 0))
    return pl.pallas_call(
        body, grid=(NT,),
        in_specs=[row(QB), row(QB + 1), row(KB), row(KB + 1), row(VB), row(VB + 1),
                  pl.BlockSpec((TM, BLK), lambda i: (i, 0))],
        out_specs=[pl.BlockSpec((1024, TM), lambda i: (0, i)), wide, pl.BlockSpec((512, TM), lambda i: (0, i))],
        out_shape=[SDS((1024, LP), BF16), SDS((LP, 1024), BF16), SDS((512, LP), BF16)],
        compiler_params=_cparams(("parallel",)), name="fox_prep")(proj, proj, proj, proj, proj, proj, cum)


def _over_keys(reduce, x):
    slabs = x.reshape(x.shape[0] // HALF, HALF, x.shape[1])
    return reduce(reduce(slabs, axis=0), axis=0, keepdims=True)


def _fox_valid_t(qi, kj, c, r):
    krow = kj * TM + r * KSUB + lax.broadcasted_iota(jnp.int32, (KSUB, c[1]), 0)
    qrow = qi * TM + c[0] + lax.broadcasted_iota(jnp.int32, (KSUB, c[1]), 1)
    return (krow <= qrow) & ((krow >= PAD_ROWS) | (qrow < PAD_ROWS))


def _fox_fwd(q_aug, k_aug, v_t, rider):
    pairs = [(qi, kj) for qi in range(NT) for kj in range(qi + 1)]
    n_pairs = len(pairs)

    def body(qi_ref, kj_ref, *refs):
        (q_ref, k_ref, vt_ref), (o_ref, lse_ref), (m_s, l_s, acc_s), mine = rider.split(refs, 3, 2, 3)
        n = pl.program_id(0)
        qi = qi_ref[n]
        kj = kj_ref[n]
        rider.at_steps(mine, n == 0, n == n_pairs // 2, n == n_pairs - 1)

        @pl.when(kj == 0)
        def _():
            m_s[...] = jnp.full_like(m_s, NEG)
            l_s[...] = jnp.zeros_like(l_s)
            acc_s[...] = jnp.zeros_like(acc_s)

        def tile(masked):
            steps = [(h, c, r) for h in range(N_HEADS) for c in QCHUNKS for r in range(TM // KSUB)]

            def scores(h, c, r):
                blk = slice(h * BLK, (h + 1) * BLK)
                return jnp.dot(k_ref[r * KSUB:(r + 1) * KSUB, blk], q_ref[blk, c[0]:c[0] + c[1]],
                               preferred_element_type=F32)

            ahead = [scores(*st) for st in steps[:AHEAD]]
            for n, (h, c, r) in enumerate(steps):
                s_t = ahead.pop(0)
                if n + AHEAD < len(steps):
                    ahead.append(scores(*steps[n + AHEAD]))
                cs = slice(c[0], c[0] + c[1])
                if masked:
                    s_t = jnp.where(_fox_valid_t(qi, kj, c, r), s_t, NEG)
                m_prev = m_s[h, :, cs]
                m_new = jnp.maximum(m_prev, _over_keys(jnp.max, s_t))
                p_t = jnp.exp2(s_t - m_new)
                alpha = jnp.exp2(m_prev - m_new)
                l_s[h, :, cs] = alpha * l_s[h, :, cs] + _over_keys(jnp.sum, p_t)
                m_s[h, :, cs] = m_new
                vt = vt_ref[h * HALF:(h + 1) * HALF, r * KSUB:(r + 1) * KSUB]
                acc_s[h, :, cs] = acc_s[h, :, cs] * alpha + jnp.dot(vt, p_t.astype(BF16),
                                                                    preferred_element_type=F32)

        @pl.when((kj < qi) & (kj > 0))
        def _():
            tile(False)

        @pl.when((kj == qi) | (kj == 0))
        def _():
            tile(True)

        @pl.when(kj == qi)
        def _():
            for pp in range(4):
                both = jnp.concatenate([acc_s[2 * pp] * (1.0 / l_s[2 * pp]),
                                        acc_s[2 * pp + 1] * (1.0 / l_s[2 * pp + 1])], axis=0)
                o_ref[:, pp * BLK:(pp + 1) * BLK] = both.T.astype(BF16)
            for h in range(N_HEADS):
                lse_ref[h] = m_s[h] * LN2 + jnp.log(l_s[h])

    grid_spec = pltpu.PrefetchScalarGridSpec(
        num_scalar_prefetch=2, grid=(n_pairs,),
        in_specs=[pl.BlockSpec((1024, TM), lambda n, qi, kj: (0, qi[n])),
                  pl.BlockSpec((TM, 1024), lambda n, qi, kj: (kj[n], 0)),
                  pl.BlockSpec((512, TM), lambda n, qi, kj: (0, kj[n]))] + [HBM_SPEC] * len(rider.operands),
        out_specs=[pl.BlockSpec((TM, 512), lambda n, qi, kj: (qi[n], 0)),
                   pl.BlockSpec((N_HEADS, 1, TM), lambda n, qi, kj: (0, 0, qi[n]))]
        + [HBM_SPEC] * len(rider.out_shapes),
        scratch_shapes=[pltpu.VMEM((N_HEADS, 1, TM), F32), pltpu.VMEM((N_HEADS, 1, TM), F32),
                        pltpu.VMEM((N_HEADS, HALF, TM), F32)] + rider.scratch())
    o_b, lse, *carried = pl.pallas_call(
        body, grid_spec=grid_spec,
        out_shape=[SDS((LP, 512), BF16), SDS((N_HEADS, 1, LP), F32)] + rider.out_shapes,
        compiler_params=_cparams(("arbitrary",)), name="fox_fwd",
    )(jnp.asarray([p[0] for p in pairs], jnp.int32), jnp.asarray([p[1] for p in pairs], jnp.int32),
      q_aug, k_aug, v_t, *rider.operands)
    return o_b, lse, carried


def _fox_bwd(proj, o_b, dmix, lse, ck_t, rider):
    pairs = [(kj, qi) for kj in range(NT) for qi in range(kj, NT)]
    n_pairs = len(pairs)

    def body(kj_ref, qi_ref, *refs):
        ((q0_ref, q1_ref, k0_ref, k1_ref, v0_ref, v1_ref, o_ref, do_ref, lse_ref, ck_ref),
         (dq_ref, dk_ref, dv_ref, dck_ref, dcq_ref), (dk_s, dv_s, dck_s), mine) = rider.split(refs, 10, 5, 3)
        n = pl.program_id(0)
        kj = kj_ref[n]
        qi = qi_ref[n]
        rider.at_steps(mine, n == 0, n == n_pairs // 2, n == n_pairs - 1)

        @pl.when(n == 0)
        def _():
            dq_ref[...] = jnp.zeros_like(dq_ref)
            dcq_ref[...] = jnp.zeros_like(dcq_ref)

        @pl.when(qi == kj)
        def _():
            dk_s[...] = jnp.zeros_like(dk_s)
            dv_s[...] = jnp.zeros_like(dv_s)
            dck_s[...] = jnp.zeros_like(dck_s)

        def tile(masked):
            valid = _fox_valid(qi, kj) if masked else None
            half = _lane_half(TM)
            q0 = pl.multiple_of(qi * TM, TM)
            lane = lax.broadcasted_iota(jnp.int32, (TM, BLK), 1)
            row_sums = jnp.zeros((TM, BLK), F32)
            pair_ops = {}

            def operands(pp):
                if pp not in pair_ops:
                    cols = slice(pp * BLK, (pp + 1) * BLK)
                    q_ref, k_ref, v_ref = ((q0_ref, k0_ref, v0_ref), (q1_ref, k1_ref, v1_ref))[pp // 2]
                    part = slice((pp % 2) * BLK, (pp % 2 + 1) * BLK)
                    pair_ops[pp] = ((q_ref[:, part].astype(F32) * SCALE).astype(BF16), k_ref[:, part],
                                    v_ref[:, part], do_ref[:, cols])
                return pair_ops[pp]

            def scores(pp, e):
                qs, kp, vp, dop = operands(pp)
                ke = jnp.where(half == e, kp, jnp.zeros_like(kp))
                ve = jnp.where(half == e, vp, jnp.zeros_like(vp))
                return (lax.dot_general(qs, ke, NT_DIMS, preferred_element_type=F32),
                        lax.dot_general(dop, ve, NT_DIMS, preferred_element_type=F32), ke)

            steps = [(pp, e) for pp in range(4) for e in range(2)]
            ahead = [scores(*st) for st in steps[:AHEAD_BWD]]
            for n, (pp, e) in enumerate(steps):
                raw, dp, ke = ahead.pop(0)
                if n + AHEAD_BWD < len(steps):
                    ahead.append(scores(*steps[n + AHEAD_BWD]))
                h = 2 * pp + e
                cols = slice(pp * BLK, (pp + 1) * BLK)
                qs, kp, vp, dop = operands(pp)
                if e == 0:
                    prod = dop.astype(F32) * o_ref[:, cols].astype(F32)
                    d0 = jnp.sum(jnp.where(half == 0, prod, 0.0), axis=1, keepdims=True)
                    d1 = jnp.sum(prod, axis=1, keepdims=True) - d0
                    dq = jnp.zeros((TM, BLK), F32)
                    dks, dvs = [], []
                t = raw - ck_ref[h] - lse_ref[h]
                if masked:
                    t = jnp.where(valid, t, NEG)
                p = jnp.exp(t)
                ds = p * (dp - (d0 if e == 0 else d1))
                dck_s[h] += jnp.sum(ds, axis=0, keepdims=True)
                row_sums = jnp.where(lane == h, jnp.sum(ds, axis=1, keepdims=True), row_sums)
                ds_b = ds.astype(BF16)
                dq = dq + jnp.dot(ds_b, ke, preferred_element_type=F32)
                dks.append(lax.dot_general(ds_b, qs, TN_DIMS, preferred_element_type=F32))
                dvs.append(lax.dot_general(p.astype(BF16), dop, TN_DIMS, preferred_element_type=F32))
                if e == 1:
                    dq_ref[pl.ds(q0, TM), cols] += dq
                    dk_s[pp] += jnp.where(half == 0, dks[0], dks[1])
                    dv_s[pp] += jnp.where(half == 0, dvs[0], dvs[1])
            dcq_ref[pl.ds(q0, TM), :] += row_sums

        @pl.when((qi > kj) & (kj > 0))
        def _():
            tile(False)

        @pl.when((qi == kj) | (kj == 0))
        def _():
            tile(True)

        @pl.when(qi == NT - 1)
        def _():
            for pp in range(4):
                cols = slice(pp * BLK, (pp + 1) * BLK)
                dk_ref[:, cols] = dk_s[pp].astype(BF16)
                dv_ref[:, cols] = dv_s[pp].astype(BF16)
            dck_ref[...] = dck_s[...]

    qrow = lambda blk, w=512: pl.BlockSpec((TM, w), lambda n, kj, qi: (qi[n], blk))
    krow = lambda blk: pl.BlockSpec((TM, W2), lambda n, kj, qi: (kj[n], blk))
    grid_spec = pltpu.PrefetchScalarGridSpec(
        num_scalar_prefetch=2, grid=(n_pairs,),
        in_specs=[qrow(QB, W2), qrow(QB + 1, W2), krow(KB), krow(KB + 1), krow(VB), krow(VB + 1), qrow(0), qrow(1),
                  pl.BlockSpec((N_HEADS, TM, 1), lambda n, kj, qi: (0, qi[n], 0)),
                  pl.BlockSpec((N_HEADS, 1, TM), lambda n, kj, qi: (0, 0, kj[n]))] + [HBM_SPEC] * len(rider.operands),
        out_specs=[pl.BlockSpec((LP, 512), lambda n, kj, qi: (0, 0)),
                   pl.BlockSpec((TM, 512), lambda n, kj, qi: (kj[n], 0)),
                   pl.BlockSpec((TM, 512), lambda n, kj, qi: (kj[n], 0)),
                   pl.BlockSpec((N_HEADS, 1, TM), lambda n, kj, qi: (0, 0, kj[n])),
                   pl.BlockSpec((LP, BLK), lambda n, kj, qi: (0, 0))] + [HBM_SPEC] * len(rider.out_shapes),
        scratch_shapes=[pltpu.VMEM((4, TM, BLK), F32), pltpu.VMEM((4, TM, BLK), F32),
                        pltpu.VMEM((N_HEADS, 1, TM), F32)] + rider.scratch())
    dq, dk, dv, dck, dcq, *carried = pl.pallas_call(
        body, grid_spec=grid_spec,
        out_shape=[SDS((LP, 512), F32), SDS((LP, 512), BF16), SDS((LP, 512), BF16), SDS((N_HEADS, 1, LP), F32),
                   SDS((LP, BLK), F32)] + rider.out_shapes,
        compiler_params=_cparams(("arbitrary",)), name="fox_bwd",
    )(jnp.asarray([p[0] for p in pairs], jnp.int32), jnp.asarray([p[1] for p in pairs], jnp.int32),
      proj, proj, proj, proj, proj, proj, o_b, dmix, lse, ck_t, *rider.operands)
    return dq, dk, dv, dck, dcq, carried


N_SEG = 3
N_KEY = N_SEG * BLK
GROUP = 4
QW = GROUP * BLK


def _bucket_tables_t():
    return np.ascontiguousarray(_bucket_tables().transpose(0, 2, 1))


def _stack_heads(ref, g, scale):
    half = _lane_half(BLK)
    out = []
    for pair in range(2):
        x = ref[:, (2 * g + pair) * BLK:(2 * g + pair + 1) * BLK].astype(F32) * scale
        swapped = pltpu.roll(x, HALF, 1)
        for e in range(2):
            out.append(jnp.where(half == g, x if e == g else swapped, 0.0).astype(BF16))
    return jnp.concatenate(out, axis=0)


def _unstack_heads(x_t, g, ref, scale):
    for pair in range(2):
        both = jnp.concatenate([x_t[:, (2 * pair) * BLK:(2 * pair + 1) * BLK],
                                x_t[:, (2 * pair + 1) * BLK:(2 * pair + 2) * BLK]], axis=0)
        ref[:, (2 * g + pair) * BLK:(2 * g + pair + 1) * BLK] = (both.T * scale).astype(ref.dtype)


def _swa_tables(tab_ref, sink_ref, bkt_ref, tbl, sink_row):
    kk = lax.broadcasted_iota(jnp.int32, (BLK, BLK), 0)
    qq = lax.broadcasted_iota(jnp.int32, (BLK, BLK), 1)
    neg = jnp.full((BLK, BLK), NEG, F32)
    lane = lax.broadcasted_iota(jnp.int32, (1, QW), 1) // BLK
    for g in range(2):
        row = jnp.zeros((1, QW), F32)
        for hh in range(GROUP):
            h = GROUP * g + hh
            cols = slice(hh * BLK, (hh + 1) * BLK)
            row = jnp.where(lane == hh, sink_ref[0, h], row)

            def step(b, carry, h=h):
                t = tab_ref[b, h]
                return jnp.where(bkt_ref[0] == b, t, carry[0]), jnp.where(bkt_ref[1] == b, t, carry[1])
            zero = jnp.zeros((BLK, BLK), F32)
            cur, prev = lax.fori_loop(0, N_BUCKETS, step, (zero, zero))
            far = jnp.full((BLK, BLK), tab_ref[N_BUCKETS - 1, h], F32)
            causal = jnp.where(kk <= qq, cur, neg)
            segments = [
                (neg, neg, jnp.where(kk >= PAD_ROWS, causal, neg)),
                (jnp.where(kk >= PAD_ROWS, prev, neg), neg, causal),
                (jnp.where(kk >= PAD_ROWS, far, neg), jnp.where(kk > qq, prev, neg), causal)]
            for case in range(3):
                for seg in range(N_SEG):
                    tbl[case, g, seg * BLK:(seg + 1) * BLK, cols] = segments[case][seg]
        sink_row[g] = row


def _swa_prep(proj):
    rows = LP // 3

    def body(k_ref, v_ref, kt_ref, vt_ref):
        kt_ref[...] = k_ref[...].astype(F32).T.astype(BF16)
        vt_ref[...] = v_ref[...].astype(F32).T.astype(BF16)

    col = pl.BlockSpec((BLK, rows), lambda i: (0, i))
    return pl.pallas_call(
        body, grid=(3,),
        in_specs=[pl.BlockSpec((rows, BLK), lambda i: (i, KA)), pl.BlockSpec((rows, BLK), lambda i: (i, VA))],
        out_specs=[col, col], out_shape=[SDS((BLK, LP), BF16)] * 2,
        compiler_params=_cparams(("parallel",)), name="swa_prep")(proj, proj)


def _segments(ref, i, by_rows):
    starts = [0, pl.multiple_of(jnp.maximum(i - 1, 0) * BLK, BLK), pl.multiple_of(i * BLK, BLK)]
    if by_rows:
        return jnp.concatenate([ref[pl.ds(s, BLK), :] for s in starts], axis=0)
    return jnp.concatenate([ref[:, pl.ds(s, BLK)] for s in starts], axis=1)


def _swa_fwd(proj, vt_a, rel_bias, sinks, bkt_t, rider):
    def body(*refs):
        ((tab_ref, sink_ref, bkt_ref, q_ref, k_ref, vt_ref), (o_ref, lse_ref),
         (tbl, sink_row), mine) = rider.split(refs, 6, 2, 2)
        i = pl.program_id(0)
        rider.at_steps(mine, i == 0, i == NBLK // 2, i == NBLK - 1)

        @pl.when(i == 0)
        def _():
            _swa_tables(tab_ref, sink_ref, bkt_ref, tbl, sink_row)

        case = jnp.minimum(i, 2)
        k_cat = _segments(k_ref, i, True)
        vt_cat = _segments(vt_ref, i, False)
        raw = [lax.dot_general(k_cat, _stack_heads(q_ref, g, SCALE), NT_DIMS, preferred_element_type=F32)
               for g in range(2)]
        for g in range(2):
            s_t = raw[g] + tbl[case, g]
            sink = sink_row[g]
            m = jnp.maximum(_over_keys(jnp.max, s_t), sink)
            p_t = jnp.exp(s_t - m)
            l = _over_keys(jnp.sum, p_t) + jnp.exp(sink - m)
            o_t = jnp.dot(vt_cat[g * HALF:(g + 1) * HALF, :], p_t.astype(BF16), preferred_element_type=F32)
            _unstack_heads(o_t * (1.0 / l), g, o_ref, 1.0)
            lse = m + jnp.log(l)
            for hh in range(GROUP):
                lse_ref[GROUP * g + hh] = lse[:, hh * BLK:(hh + 1) * BLK]

    smem = pl.BlockSpec(memory_space=pltpu.SMEM)
    o_a, lse, *carried = pl.pallas_call(
        body, grid=(NBLK,),
        in_specs=[smem, smem, pl.BlockSpec((2, BLK, BLK), lambda i: (0, 0, 0)),
                  pl.BlockSpec((BLK, 512), lambda i: (i, QA)), pl.BlockSpec((LP, BLK), lambda i: (0, KA)),
                  pl.BlockSpec((BLK, LP), lambda i: (0, 0))] + [HBM_SPEC] * len(rider.operands),
        out_specs=[pl.BlockSpec((BLK, 512), lambda i: (i, 0)),
                   pl.BlockSpec((N_HEADS, 1, BLK), lambda i: (0, 0, i))] + [HBM_SPEC] * len(rider.out_shapes),
        out_shape=[SDS((LP, 512), BF16), SDS((N_HEADS, 1, LP), F32)] + rider.out_shapes,
        scratch_shapes=[pltpu.VMEM((3, 2, N_KEY, QW), F32), pltpu.VMEM((2, 1, QW), F32)] + rider.scratch(),
        compiler_params=_cparams(("arbitrary",)), name="swa_fwd",
    )(rel_bias, sinks, bkt_t, proj, proj, vt_a, *rider.operands)
    return o_a, lse, carried


def _swa_bwd(proj, kt_a, o_a, dmix, lse, rel_bias, sinks, bkt_t):
    def body(tab_ref, sink_ref, bkt_ref, q_ref, k_ref, v_ref, kt_ref, o_ref, do_ref, lse_ref,
             dq_ref, dk_ref, dv_ref, dbias_ref, dsink_ref, tbl, sink_row, acc, dsk):
        i = pl.program_id(0)

        @pl.when(i == 0)
        def _():
            _swa_tables(tab_ref, sink_ref, bkt_ref, tbl, sink_row)
            dk_ref[...] = jnp.zeros_like(dk_ref)
            dv_ref[...] = jnp.zeros_like(dv_ref)
            acc[...] = jnp.zeros_like(acc)
            dsk[...] = jnp.zeros_like(dsk)

        case = jnp.minimum(i, 2)
        first = jnp.full((BLK, QW), i, jnp.int32) == 1
        k_cat = _segments(k_ref, i, True)
        v_cat = _segments(v_ref, i, True)
        kt_cat = _segments(kt_ref, i, False)
        dk_cat = jnp.zeros((N_KEY, BLK), F32)
        dv_cat = jnp.zeros((N_KEY, BLK), F32)
        for g in range(2):
            d_parts = []
            for pair in range(2):
                cols = slice((2 * g + pair) * BLK, (2 * g + pair + 1) * BLK)
                prod_t = (do_ref[:, cols].astype(F32) * o_ref[:, cols].astype(F32)).T
                d_parts += [jnp.sum(prod_t[:HALF], axis=0, keepdims=True),
                            jnp.sum(prod_t[HALF:], axis=0, keepdims=True)]
            d_row = jnp.concatenate(d_parts, axis=1)
            lse_row = jnp.concatenate([lse_ref[GROUP * g + hh] for hh in range(GROUP)], axis=1)
            q_st = _stack_heads(q_ref, g, SCALE)
            do_st = _stack_heads(do_ref, g, 1.0)
            s_t = lax.dot_general(k_cat, q_st, NT_DIMS, preferred_element_type=F32) + tbl[case, g]
            p_t = jnp.exp(s_t - lse_row)
            dp_t = lax.dot_general(v_cat, do_st, NT_DIMS, preferred_element_type=F32)
            ds_t = p_t * (dp_t - d_row)
            dsk[g] += -jnp.exp(sink_row[g] - lse_row) * d_row
            acc[g, 0:BLK] += jnp.where(first, 0.0, ds_t[0:BLK])
            acc[g, BLK:2 * BLK] += jnp.where(first, ds_t[0:BLK], ds_t[BLK:2 * BLK])
            acc[g, 2 * BLK:N_KEY] += ds_t[2 * BLK:N_KEY]
            ds_b = ds_t.astype(BF16)
            dk_cat = dk_cat + jnp.dot(ds_b, q_st, preferred_element_type=F32)
            dv_cat = dv_cat + jnp.dot(p_t.astype(BF16), do_st, preferred_element_type=F32)
            dq_t = jnp.dot(kt_cat[g * HALF:(g + 1) * HALF, :], ds_b, preferred_element_type=F32)
            _unstack_heads(dq_t, g, dq_ref, SCALE)

        prev0 = pl.multiple_of(jnp.maximum(i - 1, 0) * BLK, BLK)
        cur0 = pl.multiple_of(i * BLK, BLK)
        for ref, cat in ((dk_ref, dk_cat), (dv_ref, dv_cat)):
            ref[0:BLK, :] += cat[0:BLK]
            ref[pl.ds(prev0, BLK), :] += cat[BLK:2 * BLK]
            ref[pl.ds(cur0, BLK), :] += cat[2 * BLK:N_KEY]

        @pl.when(i == NBLK - 1)
        def _():
            lane = lax.broadcasted_iota(jnp.int32, (1, BLK), 1)

            def per_bucket(b, carry):
                row = jnp.zeros((1, BLK), F32)
                for h in range(N_HEADS):
                    g, cols = h // GROUP, slice((h % GROUP) * BLK, (h % GROUP + 1) * BLK)
                    val = (jnp.sum(jnp.where(bkt_ref[0] == b, acc[g, 2 * BLK:N_KEY, cols], 0.0), keepdims=True)
                           + jnp.sum(jnp.where(bkt_ref[1] == b, acc[g, BLK:2 * BLK, cols], 0.0), keepdims=True))
                    row = jnp.where(lane == h, val, row)
                dbias_ref[pl.ds(b, 1), :] = row
                return carry

            lax.fori_loop(0, N_BUCKETS, per_bucket, 0)
            far = jnp.zeros((1, BLK), F32)
            dsr = jnp.zeros((1, BLK), F32)
            for h in range(N_HEADS):
                g, cols = h // GROUP, slice((h % GROUP) * BLK, (h % GROUP + 1) * BLK)
                far = jnp.where(lane == h, jnp.sum(acc[g, 0:BLK, cols], keepdims=True), far)
                dsr = jnp.where(lane == h, jnp.sum(dsk[g, :, cols], keepdims=True), dsr)
            dbias_ref[N_BUCKETS - 1:N_BUCKETS, :] += far
            dsink_ref[...] = dsr

    smem = pl.BlockSpec(memory_space=pltpu.SMEM)
    blk512 = lambda col: pl.BlockSpec((BLK, 512), lambda i: (i, col))
    full = lambda r, c: pl.BlockSpec((r, c), lambda i: (0, 0))
    return pl.pallas_call(
        body, grid=(NBLK,),
        in_specs=[smem, smem, pl.BlockSpec((2, BLK, BLK), lambda i: (0, 0, 0)), blk512(QA),
                  pl.BlockSpec((LP, BLK), lambda i: (0, KA)), pl.BlockSpec((LP, BLK), lambda i: (0, VA)),
                  full(BLK, LP), blk512(0), blk512(0), pl.BlockSpec((N_HEADS, 1, BLK), lambda i: (0, 0, i))],
        out_specs=[blk512(0), full(LP, BLK), full(LP, BLK), full(N_BUCKETS, BLK), full(1, BLK)],
        out_shape=[SDS((LP, 512), BF16), SDS((LP, BLK), F32), SDS((LP, BLK), F32),
                   SDS((N_BUCKETS, BLK), F32), SDS((1, BLK), F32)],
        scratch_shapes=[pltpu.VMEM((3, 2, N_KEY, QW), F32), pltpu.VMEM((2, 1, QW), F32),
                        pltpu.VMEM((2, N_KEY, QW), F32), pltpu.VMEM((2, 1, QW), F32)],
        compiler_params=_cparams(("arbitrary",)), name="swa_bwd",
    )(rel_bias, sinks, bkt_t, proj, proj, proj, kt_a, o_a, dmix, lse)


def _local_step(x, tgt, meta, rel_bias, g_pre_mix, g_post_mix, g_pre_ffn, g_post_ffn, b_forget, sinks,
                w_in_b, out_rider, out_weight, ffn_rider, ffn_weights, early_grads):
    bkt_t = jnp.asarray(_bucket_tables_t())
    h0 = jnp.concatenate([jnp.zeros((PAD_ROWS, D_MODEL), F32), meta, x], axis=0)
    b_p = jnp.pad(b_forget, ((0, 0), (0, BLK - N_HEADS)))

    hn1, proj, f = _pre_mix(h0, g_pre_mix, w_in_b)
    kt_a, vt_a = _swa_prep(proj)
    o_a, lse_a, carried_out = _swa_fwd(proj, vt_a, rel_bias, sinks, bkt_t, out_rider)
    w_out_b = out_weight(carried_out)
    cum = _forget_cumsum(f, b_p)
    ck_t = cum[:, :N_HEADS].T.reshape(N_HEADS, 1, LP)
    q_aug, k_aug, v_t = _fox_prep(proj, cum)
    o_b, lse_row, carried = _fox_fwd(q_aug, k_aug, v_t, ffn_rider)
    lse_b = lse_row.reshape(N_HEADS, LP, 1)
    w_gu_b, w_dn_b = ffn_weights(carried)
    a, h1, hn2 = _attn_out(o_a, o_b, w_out_b, h0, g_post_mix, g_pre_ffn)
    g, u, act = _ffn_up(hn2, w_gu_b)
    dff, dy, loss_blk, dg_post_ffn = _ffn_down_loss(act, w_dn_b, h1, tgt, g_post_ffn)

    dw_dn = _mm_tn([act], dff, FF_T, "dw_down", BF16)
    dg, du = _ffn_down_bwd(dff, w_dn_b, g, u)
    dw_gu = _dw_gate_up(hn2, dg, du)
    dh1, da, dg_pre_ffn, dg_post_mix = _ffn_up_bwd(dg, du, w_gu_b, h1, a, dy, g_pre_ffn, g_post_mix)
    dw_out = _mm_tn([o_a, o_b], da, D_MODEL, "dw_out", BF16)
    dmix = _attn_out_bwd(da, w_out_b)
    dq_b, dk_b, dv_b, dck, dcq, landed = _fox_bwd(proj, o_b, dmix, lse_b, ck_t, early_grads(dw_gu, dw_dn, dw_out))
    dq_a, dk_a, dv_a, dbias, dsink = _swa_bwd(proj, kt_a, o_a, dmix, lse_a, rel_bias, sinks, bkt_t)
    dcum = dcq - jnp.pad(dck.reshape(N_HEADS, LP).T, ((0, 0), (0, BLK - N_HEADS)))
    df, db = _forget_cumsum_bwd(dcum, f, b_p)
    dproj, dh0, dg_pre_mix = _pre_mix_bwd(dq_a, dq_b, dk_b, dv_b, dk_a, dv_a, df, w_in_b, h0, dh1, g_pre_mix)
    dw_in = _mm_tn([hn1], dproj, D_MODEL, "dw_in", BF16)

    return dict(loss=loss_blk[0, 0], grad_x=dh0[ROW0:], meta=dh0[PAD_ROWS:ROW0],
                rel_bias=dbias[:, :N_HEADS], ln_pre_mix=dg_pre_mix, ln_post_mix=dg_post_mix,
                ln_pre_ffn=dg_pre_ffn, ln_post_ffn=dg_post_ffn, b_forget=db[:, :N_HEADS],
                sinks=dsink[:, :N_HEADS], w_in=dw_in, w_out=dw_out, w_gate_up=dw_gu, w_down=dw_dn,
                landed=landed)


N_SMALL = 24
LOSS_ROW = 6


def _place():
    x, y, c = lax.axis_index("x"), lax.axis_index("y"), lax.axis_index("c")
    return x, y, c, [(1 - x, y), (x, 1 - y), (1 - x, 1 - y)]


def _run_alone(rider, name):
    a, b = len(rider.operands), len(rider.out_shapes)

    def body(*refs):
        mine = (refs[:a], refs[a:a + b], refs[a + b:])
        rider.first(*mine)
        rider.middle(*mine)
        rider.last(*mine)

    return pl.pallas_call(body, in_specs=[HBM_SPEC] * a, out_specs=[HBM_SPEC] * b, out_shape=rider.out_shapes,
                          scratch_shapes=rider.scratch(), name=name)(*rider.operands)


def _gather_rider(shards, own_too, by_columns=()):
    n = len(shards)

    def slot(a, outs, chip, h):
        if a in by_columns:
            cols = shards[a].shape[2]
            return outs[a].at[h, :, pl.ds(pl.multiple_of(chip * cols, BLK), cols)]
        return outs[a].at[chip, h]

    def own_copies(ins, outs, sems):
        x, y, _, _ = _place()
        if not own_too:
            return []
        return [pltpu.make_async_copy(ins[a].at[h], slot(a, outs, 2 * x + y, h), sems[2].at[2 * a + h])
                for a in range(n) for h in range(2)]

    def copies(ins, outs, sems):
        send_sems, recv_sems = sems[:2]
        x, y, c, others = _place()
        chip = 2 * x + y
        sibling = (x, y, 1 - c)

        def rc(a, k, src, dst, to):
            return pltpu.make_async_remote_copy(src_ref=src, dst_ref=dst, send_sem=send_sems.at[6 * a + k],
                                                recv_sem=recv_sems.at[6 * a + k], device_id=to, device_id_type=MESH)

        pairs = [(a, j, ox, oy) for a in range(n) for j, (ox, oy) in enumerate(others)]
        there = lambda a, ox, oy, h: slot(a, outs, 2 * ox + oy, h)
        return dict(
            sent=lambda: [rc(a, j, ins[a].at[c], slot(a, outs, chip, c), (ox, oy, c)) for a, j, ox, oy in pairs],
            landed=lambda: [rc(a, j, there(a, ox, oy, c), there(a, ox, oy, c), sibling) for a, j, ox, oy in pairs],
            passed=lambda: [rc(a, 3 + j, there(a, ox, oy, c), there(a, ox, oy, c), sibling)
                            for a, j, ox, oy in pairs],
            arriving=lambda: [rc(a, 3 + j, there(a, ox, oy, 1 - c), there(a, ox, oy, 1 - c), sibling)
                              for a, j, ox, oy in pairs])

    def first(*mine):
        for cp in copies(*mine)["sent"]() + own_copies(*mine):
            cp.start()

    def middle(*mine):
        kinds = copies(*mine)
        for got, cp in zip(kinds["landed"](), kinds["passed"]()):
            got.wait_recv()
            cp.start()

    def last(*mine):
        kinds = copies(*mine)
        for cp in kinds["arriving"]():
            cp.wait_recv()
        for cp in kinds["sent"]() + kinds["passed"]():
            cp.wait_send()
        for cp in own_copies(*mine):
            cp.wait()

    shapes = [SDS((2, s.shape[1], 4 * s.shape[2]) if a in by_columns else (4,) + s.shape, s.dtype)
              for a, s in enumerate(shards)]
    return _Rider(shards, shapes, [6 * n, 6 * n] + [2 * n] * own_too, first, middle, last)


def _swap_rider(grads):
    n = len(grads)
    slabs = [(a, s) for a in range(n) for s in range(grads[a].shape[0])]

    def copies(ins, outs, sems):
        x, y, c, _ = _place()
        return [pltpu.make_async_remote_copy(
            src_ref=ins[a].at[s, 1 - c], dst_ref=outs[a].at[s], send_sem=sems[0].at[k], recv_sem=sems[1].at[k],
            device_id=(x, y, 1 - c), device_id_type=MESH) for k, (a, s) in enumerate(slabs)]

    def first(*mine):
        for cp in copies(*mine):
            cp.start()

    def middle(*mine):
        pass

    def last(*mine):
        for cp in copies(*mine):
            cp.wait()

    return _Rider(grads, [SDS(g.shape[:1] + g.shape[2:], g.dtype) for g in grads], [len(slabs), len(slabs)],
                  first, middle, last)


def _pair_sum(g, got, c_arr, name):
    n_s, rh, cc = got.shape

    def body(c_ref, g_ref, p_ref, o_ref):
        o_ref[0] = (g_ref[0, 0].astype(F32) + p_ref[0].astype(F32)).astype(BF16)

    grid_spec = pltpu.PrefetchScalarGridSpec(
        num_scalar_prefetch=1, grid=(n_s,),
        in_specs=[pl.BlockSpec((1, 1, rh, cc), lambda s, c_ref: (s, c_ref[0], 0, 0)),
                  pl.BlockSpec((1, rh, cc), lambda s, c_ref: (s, 0, 0))],
        out_specs=pl.BlockSpec((1, rh, cc), lambda s, c_ref: (s, 0, 0)))
    return pl.pallas_call(body, grid_spec=grid_spec, out_shape=SDS((n_s, rh, cc), BF16),
                          compiler_params=_cparams(("parallel",)), name=name)(c_arr, g, got)


def _direct_rider(grads):
    n = len(grads)

    def copies(ins, outs, sems):
        x, y, c, others = _place()
        peers = [(x, y, 1 - c)] + [(ox, oy, c) for ox, oy in others] + [(ox, oy, 1 - c) for ox, oy in others]
        return [pltpu.make_async_remote_copy(
            src_ref=ins[a].at[2 * px + py, pc], dst_ref=outs[a].at[k], send_sem=sems[0].at[7 * a + k],
            recv_sem=sems[1].at[7 * a + k], device_id=(px, py, pc), device_id_type=MESH)
            for a in range(n) for k, (px, py, pc) in enumerate(peers)]

    def first(*mine):
        for cp in copies(*mine):
            cp.start()

    def middle(*mine):
        pass

    def last(*mine):
        for cp in copies(*mine):
            cp.wait()

    return _Rider(grads, [SDS((7,) + g.shape[2:], g.dtype) for g in grads], [7 * n, 7 * n], first, middle, last)


def _owner_sum(grads, landed, own_arr, after, name):
    rh, cc = landed.shape[1:]
    tr = rh // 2

    def body(own_ref, g_ref, p_ref, after_ref, o_ref):
        total = g_ref[0, 0].astype(F32)
        for k in range(7):
            total = total + p_ref[k].astype(F32)
        o_ref[...] = total

    grid_spec = pltpu.PrefetchScalarGridSpec(
        num_scalar_prefetch=1, grid=(2,),
        in_specs=[pl.BlockSpec((1, 1, tr, cc), lambda i, own: (own[0], own[1], i, 0)),
                  pl.BlockSpec((7, tr, cc), lambda i, own: (0, i, 0)), pl.BlockSpec(memory_space=pl.ANY)],
        out_specs=pl.BlockSpec((tr, cc), lambda i, own: (i, 0)))
    return pl.pallas_call(body, grid_spec=grid_spec, out_shape=SDS((rh, cc), F32),
                          compiler_params=_cparams(("parallel",)), name=name)(own_arr, grads, landed, after)


SEM_SPEC = pl.BlockSpec(memory_space=pltpu.SEMAPHORE)
N_LATE = 10


def _late_copies(part_ref, landed_ref, small_ref, all_ref, send_sems, recv_sems):
    x, y, c, others = _place()
    me = 4 * x + 2 * y + c
    peers = [(x, y, 1 - c)] + [(ox, oy, c) for ox, oy in others] + [(ox, oy, 1 - c) for ox, oy in others]
    big = [pltpu.make_async_remote_copy(
        src_ref=part_ref.at[2 * ox + oy], dst_ref=landed_ref.at[j], send_sem=send_sems.at[j], recv_sem=recv_sems.at[j],
        device_id=(ox, oy, c), device_id_type=MESH) for j, (ox, oy) in enumerate(others)]
    small = [pltpu.make_async_remote_copy(
        src_ref=small_ref, dst_ref=all_ref.at[me], send_sem=send_sems.at[3 + k], recv_sem=recv_sems.at[3 + k],
        device_id=peer, device_id_type=MESH) for k, peer in enumerate(peers)]
    return big + small


def _late_exchange_start(part, small):
    def body(part_ref, landed_ref, small_ref, all_ref, send_sems, recv_sems, part_o, landed_o, small_o, all_o, token):
        for cp in _late_copies(part_ref, landed_ref, small_ref, all_ref, send_sems, recv_sems):
            cp.start()
        token[...] = jnp.zeros_like(token)

    hbm = lambda a: pltpu.HBM(a.shape, a.dtype)
    landed = lax.empty((3,) + part.shape[1:], part.dtype)
    everyone = lax.empty((8,) + small.shape, small.dtype)
    operands = [pltpu.with_memory_space_constraint(a, pltpu.HBM) for a in (part, landed, small, everyone)]
    return pl.pallas_call(
        body, name="late_exchange_start",
        out_shape=(pltpu.SemaphoreType.DMA((N_LATE,)), pltpu.SemaphoreType.DMA((N_LATE,)),
                   hbm(part), hbm(landed), hbm(small), hbm(everyone), SDS((8, BLK), F32)),
        in_specs=[HBM_SPEC] * 4,
        out_specs=(SEM_SPEC, SEM_SPEC, HBM_SPEC, HBM_SPEC, HBM_SPEC, HBM_SPEC, pl.BlockSpec(memory_space=pltpu.VMEM)),
        input_output_aliases={0: 2, 1: 3, 2: 4, 3: 5},
        compiler_params=pltpu.CompilerParams(has_side_effects=pltpu.SideEffectType.DATAFLOW_SIDE_EFFECTING),
    )(*operands)


def _late_exchange_wait(send_sems, recv_sems, part, landed, small, everyone, after):
    def body(part_ref, landed_ref, small_ref, all_ref, send_sems, recv_sems, after_ref, part_o, landed_o, small_o, all_o):
        for cp in _late_copies(part_ref, landed_ref, small_ref, all_ref, send_sems, recv_sems):
            cp.wait_send()
            cp.wait_recv()

    hbm = lambda a: pltpu.HBM(a.shape, a.dtype)
    out = pl.pallas_call(
        body, name="late_exchange_wait",
        out_shape=(hbm(part), hbm(landed), hbm(small), hbm(everyone)),
        in_specs=[HBM_SPEC] * 4 + [SEM_SPEC, SEM_SPEC, pl.BlockSpec(memory_space=pl.ANY)],
        out_specs=(HBM_SPEC,) * 4, input_output_aliases={0: 0, 1: 1, 2: 2, 3: 3},
        compiler_params=pltpu.CompilerParams(has_side_effects=pltpu.SideEffectType.DATAFLOW_SIDE_EFFECTING),
    )(part, landed, small, everyone, send_sems, recv_sems, after)
    return out[0], out[1], out[3]


def _chip_sum(parts, landed, chip_arr, name):
    rh, cc = landed.shape[1:]
    tr = rh // 2

    def body(chip_ref, own_ref, p_ref, o_ref):
        o_ref[...] = ((own_ref[0].astype(F32) + p_ref[0].astype(F32)) + p_ref[1].astype(F32)) + p_ref[2].astype(F32)

    grid_spec = pltpu.PrefetchScalarGridSpec(
        num_scalar_prefetch=1, grid=(2,),
        in_specs=[pl.BlockSpec((1, tr, cc), lambda i, chip_ref: (chip_ref[0], i, 0)),
                  pl.BlockSpec((3, tr, cc), lambda i, chip_ref: (0, i, 0))],
        out_specs=pl.BlockSpec((tr, cc), lambda i, chip_ref: (i, 0)))
    return pl.pallas_call(body, grid_spec=grid_spec, out_shape=SDS((rh, cc), F32),
                          compiler_params=_cparams(("parallel",)), name=name)(chip_arr, parts, landed)


def _device_sum(p):
    def body(p_ref, o_ref):
        acc = p_ref[0]
        for k in range(1, 8):
            acc = acc + p_ref[k]
        o_ref[...] = acc

    return pl.pallas_call(body, out_shape=SDS(p.shape[1:], F32), name="small_sum")(p)


def _join_halves(halves, name):
    n = len(halves)

    def body(*refs):
        ins, outs = refs[:n], refs[n:2 * n]
        send_sems, recv_sems = refs[2 * n:]
        x, y, c, _ = _place()
        copies = [pltpu.make_async_remote_copy(
            src_ref=ins[a], dst_ref=outs[a], send_sem=send_sems.at[a], recv_sem=recv_sems.at[a],
            device_id=(x, y, 1 - c), device_id_type=MESH) for a in range(n)]
        for cp in copies:
            cp.start()
        for cp in copies:
            cp.wait()

    return pl.pallas_call(
        body, in_specs=[HBM_SPEC] * n, out_specs=[HBM_SPEC] * n,
        out_shape=[SDS(h.shape, h.dtype) for h in halves],
        scratch_shapes=[pltpu.SemaphoreType.DMA((n,)), pltpu.SemaphoreType.DMA((n,))],
        name=name)(*halves)


def _adamw(w, g, m, v, name, tr=None):
    rows, cols = w.shape
    tr = tr or rows
    assert rows % tr == 0

    def body(w_ref, g_ref, m_ref, v_ref, d_ref, nm_ref, nv_ref):
        gg = g_ref[...]
        nm = ADAM_B1 * m_ref[...] + (1.0 - ADAM_B1) * gg
        nv = ADAM_B2 * v_ref[...] + (1.0 - ADAM_B2) * (gg * gg)
        nm_ref[...] = nm
        nv_ref[...] = nv
        m_hat = nm / (1.0 - ADAM_B1 ** ADAM_STEP)
        v_hat = nv / (1.0 - ADAM_B2 ** ADAM_STEP)
        d_ref[...] = -ADAM_LR * (m_hat / (jnp.sqrt(v_hat) + ADAM_EPS) + ADAM_WD * w_ref[...])

    blk = pl.BlockSpec((tr, cols), lambda i: (i, 0))
    return pl.pallas_call(
        body, grid=(rows // tr,), in_specs=[blk] * 4, out_specs=[blk] * 3,
        out_shape=[SDS((rows, cols), F32)] * 3,
        compiler_params=_cparams(("parallel",)), name=name)(w, g, m, v)


def _adamw_halves(w, mine, theirs, m, v, c_arr, name):
    rows, cols = w.shape
    rh = rows // 2
    tr = rh if rh <= 352 else 256
    nh = rh // tr

    def body(c_ref, w_ref, mine_ref, theirs_ref, m_ref, v_ref, g_ref, d_ref, nm_ref, nv_ref):
        own = jnp.full((tr, cols), pl.program_id(0), jnp.int32) == c_ref[0]
        gg = jnp.where(own, mine_ref[...], theirs_ref[...])
        g_ref[...] = gg
        nm = ADAM_B1 * m_ref[...] + (1.0 - ADAM_B1) * gg
        nv = ADAM_B2 * v_ref[...] + (1.0 - ADAM_B2) * (gg * gg)
        nm_ref[...] = nm
        nv_ref[...] = nv
        m_hat = nm / (1.0 - ADAM_B1 ** ADAM_STEP)
        v_hat = nv / (1.0 - ADAM_B2 ** ADAM_STEP)
        d_ref[...] = -ADAM_LR * (m_hat / (jnp.sqrt(v_hat) + ADAM_EPS) + ADAM_WD * w_ref[...])

    whole = pl.BlockSpec((tr, cols), lambda hh, i, c_ref: (hh * nh + i, 0))
    part = pl.BlockSpec((tr, cols), lambda hh, i, c_ref: (i, 0))
    grid_spec = pltpu.PrefetchScalarGridSpec(
        num_scalar_prefetch=1, grid=(2, nh), in_specs=[whole, part, part, whole, whole], out_specs=[whole] * 4)
    return pl.pallas_call(body, grid_spec=grid_spec, out_shape=[SDS((rows, cols), F32)] * 4,
                          compiler_params=_cparams(("parallel", "parallel")), name=name)(c_arr, w, mine, theirs, m, v)


def _pack_small(pre_mix, post_mix, pre_ffn, post_ffn, rel_bias, b_forget, sinks):
    def at(row, v):
        return jnp.pad(v, ((row, 7 - row), (0, D_MODEL - v.shape[1])))
    return (at(0, pre_mix) + at(1, post_mix) + at(2, pre_ffn) + at(3, post_ffn)
            + at(4, rel_bias.reshape(1, N_BUCKETS * N_HEADS)) + at(5, jnp.concatenate([b_forget, sinks], axis=1)))


def _unpack_small(p):
    return dict(ln_pre_mix=p[0:1], ln_post_mix=p[1:2], ln_pre_ffn=p[2:3], ln_post_ffn=p[3:4],
                rel_bias=p[4, :N_BUCKETS * N_HEADS].reshape(N_BUCKETS, N_HEADS),
                b_forget=p[5:6, 0:N_HEADS], sinks=p[5:6, N_HEADS:2 * N_HEADS])


WEIGHTS = ("meta_tokens", "rel_bias", "ln_pre_mix", "ln_post_mix", "ln_pre_ffn", "ln_post_ffn",
           "w_in", "b_forget", "sinks", "w_out", "w_gate_up", "w_down")


def kernel(x, meta_tokens, rel_bias, ln_pre_mix, ln_post_mix, ln_pre_ffn, ln_post_ffn, w_in, b_forget, sinks, w_out, w_gate_up, w_down, loss_target, m_meta_tokens, m_rel_bias, m_ln_pre_mix, m_ln_post_mix, m_ln_pre_ffn, m_ln_post_ffn, m_w_in, m_b_forget, m_sinks, m_w_out, m_w_gate_up, m_w_down, v_meta_tokens, v_rel_bias, v_ln_pre_mix, v_ln_post_mix, v_ln_pre_ffn, v_ln_post_ffn, v_w_in, v_b_forget, v_sinks, v_w_out, v_w_gate_up, v_w_down):
    xi, yi, ci = lax.axis_index("x"), lax.axis_index("y"), lax.axis_index("c")
    chip = 2 * xi + yi
    c_arr = jnp.reshape(ci, (1,)).astype(jnp.int32)

    def halves(w, dtype):
        return w.astype(dtype).reshape(2, w.shape[0] // 2, w.shape[1])

    def with_own(gathered, shards):
        return [lax.dynamic_update_slice(got, own[None], (chip, 0, 0, 0)) for got, own in zip(gathered, shards)]

    shards = [halves(w_in[0], BF16), halves(meta_tokens, F32)]
    gw_in, g_meta = with_own(_run_alone(_gather_rider(shards, False), "gather_mixer_weights"), shards)
    out_shards = [halves(w_out[0], BF16)]
    ffn_shards = [halves(w_gate_up[0], BF16), halves(w_down[0], BF16)]

    def ffn_weights(carried):
        gw_gu, gw_dn = carried
        return gw_gu.reshape(D_MODEL, 2 * D_FF), gw_dn.reshape(D_FF, D_MODEL)

    early = {}

    def early_grads(dw_gu, dw_dn, dw_out):
        early["grads"] = [dw_out.reshape(4, 2, 128, D_MODEL), dw_gu.reshape(4, 2, 512, FF_T),
                          dw_dn.reshape(4, 2, 352, D_MODEL)]
        return _direct_rider(early["grads"])
    w_in_all = gw_in.reshape(4, D_MODEL, D_PROJ // 4).transpose(1, 0, 2).reshape(D_MODEL, D_PROJ)
    w_in_b = jnp.pad(w_in_all, ((0, 0), (0, D_PROJ_P - D_PROJ)))
    meta_all = g_meta.reshape(4, N_META, D_MODEL // 4).transpose(1, 0, 2).reshape(N_META, D_MODEL)

    loc = _local_step(x[0], loss_target[0], meta_all, rel_bias, ln_pre_mix, ln_post_mix, ln_pre_ffn, ln_post_ffn,
                      b_forget, sinks, w_in_b, _gather_rider(out_shards, True),
                      lambda carried: carried[0].reshape(D_MODEL, D_MODEL),
                      _gather_rider(ffn_shards, True, by_columns=(0,)), ffn_weights, early_grads)

    small = jnp.concatenate(
        [_pack_small(loc["ln_pre_mix"], loc["ln_post_mix"], loc["ln_pre_ffn"], loc["ln_post_ffn"],
                     loc["rel_bias"], loc["b_forget"], loc["sinks"])
         + jnp.pad(loc["loss"].reshape(1, 1), ((LOSS_ROW, 7 - LOSS_ROW), (0, D_MODEL - 1))), loc["meta"]], axis=0)

    dw_in = loc["w_in"].reshape(1, 2, D_MODEL // 2, D_PROJ_P)
    (got_in,) = _run_alone(_swap_rider([dw_in]), "swap_halves_late")
    half_sum = _pair_sum(dw_in, got_in, c_arr, "pair_sum_late")
    part_in = half_sum[0, :, :D_PROJ].reshape(D_MODEL // 2, 4, D_PROJ // 4).transpose(1, 0, 2)
    send_sems, recv_sems, part_sent, landing, small_sent, everyone, token = _late_exchange_start(part_in, small)
    chip_arr = jnp.reshape(chip, (1,)).astype(jnp.int32)
    own_arr = jnp.stack([chip, ci]).astype(jnp.int32)
    grad, delta, new_m, new_v = {}, {}, {}, {}

    def update(names, mine):
        theirs = _join_halves(mine, "join_" + names[0])
        big = dict(w_in=(w_in, m_w_in, v_w_in), w_out=(w_out, m_w_out, v_w_out),
                   w_gate_up=(w_gate_up, m_w_gate_up, v_w_gate_up), w_down=(w_down, m_w_down, v_w_down))
        for name, g_mine, g_theirs in zip(names, mine, theirs):
            w, m, v = big[name]
            g, d, nm, nv = _adamw_halves(w[0], g_mine, g_theirs, m[0], v[0], c_arr, "adamw_" + name)
            grad[name], delta[name], new_m[name], new_v[name] = g[None], d[None], nm[None], nv[None]

    update(("w_out", "w_gate_up", "w_down"),
           [_owner_sum(g, l, own_arr, token, "owner_sum_%d" % a)
            for a, (g, l) in enumerate(zip(early["grads"], loc["landed"]))])
    part_back, landed_in, small_all = _late_exchange_wait(send_sems, recv_sems, part_sent, landing, small_sent,
                                                         everyone, new_v["w_down"])
    mine_in = _chip_sum(part_back, landed_in, chip_arr, "chip_sum_in")
    (theirs_in,) = _join_halves([mine_in], "join_w_in")
    g_w_in = jnp.where(ci == 0, jnp.concatenate([mine_in, theirs_in], axis=0),
                       jnp.concatenate([theirs_in, mine_in], axis=0))
    view = lambda a: jnp.transpose(a).reshape(D_PROJ // 4 * 8, BLK)
    back = lambda a: jnp.transpose(a.reshape(D_PROJ // 4, D_MODEL))[None]
    d, nm, nv = _adamw(view(w_in[0]), view(g_w_in), view(m_w_in[0]), view(v_w_in[0]), "adamw_w_in",
                       tr=D_PROJ // 4 * 4)
    grad["w_in"], delta["w_in"], new_m["w_in"], new_v["w_in"] = g_w_in[None], back(d), back(nm), back(nv)
    me = 4 * xi + 2 * yi + ci
    small_sum = _device_sum(lax.dynamic_update_slice(small_all, small[None], (me, 0, 0)))
    g_meta_tokens = lax.dynamic_slice(small_sum[8:N_SMALL], (0, chip * (D_MODEL // 4)), (N_META, D_MODEL // 4))
    g_small = small_sum[0:8]
    grad.update(_unpack_small(g_small))
    grad.update(meta_tokens=g_meta_tokens)
    delta["meta_tokens"], new_m["meta_tokens"], new_v["meta_tokens"] = _adamw(
        meta_tokens, g_meta_tokens, m_meta_tokens, v_meta_tokens, "adamw_meta")
    d, nm, nv = _adamw(
        _pack_small(ln_pre_mix, ln_post_mix, ln_pre_ffn, ln_post_ffn, rel_bias, b_forget, sinks), g_small,
        _pack_small(m_ln_pre_mix, m_ln_post_mix, m_ln_pre_ffn, m_ln_post_ffn, m_rel_bias, m_b_forget, m_sinks),
        _pack_small(v_ln_pre_mix, v_ln_post_mix, v_ln_pre_ffn, v_ln_post_ffn, v_rel_bias, v_b_forget, v_sinks),
        "adamw_small")
    delta.update(_unpack_small(d))
    new_m.update(_unpack_small(nm))
    new_v.update(_unpack_small(nv))

    loss = small_sum[LOSS_ROW, 0]
    return (loss,loc["grad_x"][None], *[grad[k] for k in WEIGHTS], *[delta[k] for k in WEIGHTS],
            *[new_m[k] for k in WEIGHTS], *[new_v[k] for k in WEIGHTS])
```

```python
import math

import numpy as np
import jax
import jax.numpy as jnp
from jax import lax
from jax.experimental import pallas as pl
from jax.experimental.pallas import tpu as pltpu

F32 = jnp.float32
BF16 = jnp.bfloat16
MESH = pl.DeviceIdType.MESH
SDS = jax.ShapeDtypeStruct

D_MODEL = 1024
SEQ = 4096
N_META = 16
N_HEADS = 8
HALF = 64
D_FF = 2816
N_BUCKETS = 32
EPS = 1e-6
NEG = -1e30
SCALE = 0.125
LOG2E = 1.4426950408889634
LN2 = 0.6931471805599453
PAD_ROWS = 112
ROW0 = PAD_ROWS + N_META
LP = ROW0 + SEQ
BLK = 128
NBLK = LP // BLK
TM = 384
NT = LP // TM
TM_PURE = LP // 2
TM_MID = LP // 4
TM_EPI = LP // 6
TN = 256
TK_W = LP // 2
D_PROJ = 2312
D_PROJ_P = 2432
D_QKV = 2304
FF_T = 1408
VMEM_LIMIT = 56 * 1024 * 1024

ADAM_LR = 0.001
ADAM_B1 = 0.9
ADAM_B2 = 0.999
ADAM_EPS = 1e-08
ADAM_WD = 0.01
ADAM_STEP = 10

QA = 0
KA, VA = 4, 5
QB, KB, VB = 3, 5, 7
W2 = 256

NT_DIMS = (((1,), (1,)), ((), ()))
TN_DIMS = (((0,), (0,)), ((), ()))


def _cparams(sem):
    return pltpu.CompilerParams(dimension_semantics=sem, vmem_limit_bytes=VMEM_LIMIT)


def _t5_bucket_np(d):
    n = np.maximum(d, 0).astype(np.int32)
    nf = np.maximum(n, 1).astype(np.float32)
    large = 16 + (np.log(nf / np.float32(16)) / np.float32(math.log(8.0)) * np.float32(16)).astype(np.int32)
    large = np.minimum(large, N_BUCKETS - 1)
    return np.where(n < 16, n, large).astype(np.int32)


def _bucket_tables():
    qi = np.arange(BLK)[:, None]
    ki = np.arange(BLK)[None, :]
    return np.stack([_t5_bucket_np(qi - ki), _t5_bucket_np(qi - ki + BLK)])


def _rms(x):
    return lax.rsqrt(jnp.mean(x * x, axis=-1, keepdims=True) + EPS)


def _rms_bwd(n, r, gdy):
    return r * (gdy - n * jnp.mean(n * gdy, axis=-1, keepdims=True))


def _pre_mix(h0, gain, w_in_b):
    half = D_QKV // 2

    def body(h_ref, g_ref, w_ref, hn_ref, proj_ref, f_ref):
        x = h_ref[...]
        hn = (x * _rms(x) * g_ref[...]).astype(BF16)
        hn_ref[...] = hn
        proj_ref[:, :half] = jnp.dot(hn, w_ref[:, :half], preferred_element_type=F32).astype(BF16)
        p = jnp.dot(hn, w_ref[:, half:], preferred_element_type=F32)
        proj_ref[:, half:] = p[:, :half].astype(BF16)
        f_ref[...] = p[:, half:]

    return pl.pallas_call(
        body, grid=(LP // TM_MID,),
        in_specs=[pl.BlockSpec((TM_MID, D_MODEL), lambda i: (i, 0)),
                  pl.BlockSpec((1, D_MODEL), lambda i: (0, 0)),
                  pl.BlockSpec((D_MODEL, D_PROJ_P), lambda i: (0, 0))],
        out_specs=[pl.BlockSpec((TM_MID, D_MODEL), lambda i: (i, 0)),
                   pl.BlockSpec((TM_MID, D_QKV), lambda i: (i, 0)),
                   pl.BlockSpec((TM_MID, BLK), lambda i: (i, 0))],
        out_shape=[SDS((LP, D_MODEL), BF16), SDS((LP, D_QKV), BF16), SDS((LP, BLK), F32)],
        compiler_params=_cparams(("parallel",)), name="pre_mix")(h0, gain, w_in_b)


def _attn_out(o_a, o_b, w_out_b, h0, g_post, g_pre_ffn):
    def body(oa_ref, ob_ref, w_ref, h0_ref, gp_ref, gf_ref, a_ref, h1_ref, hn2_ref):
        a = (jnp.dot(oa_ref[...], w_ref[0:512, :], preferred_element_type=F32)
             + jnp.dot(ob_ref[...], w_ref[512:1024, :], preferred_element_type=F32))
        a_ref[...] = a
        h1 = h0_ref[...] + a * _rms(a) * gp_ref[...]
        h1_ref[...] = h1
        hn2_ref[...] = (h1 * _rms(h1) * gf_ref[...]).astype(BF16)

    row = lambda w: pl.BlockSpec((TM_EPI, w), lambda i: (i, 0))
    vec = pl.BlockSpec((1, D_MODEL), lambda i: (0, 0))
    return pl.pallas_call(
        body, grid=(LP // TM_EPI,),
        in_specs=[row(512), row(512), pl.BlockSpec((D_MODEL, D_MODEL), lambda i: (0, 0)), row(D_MODEL), vec, vec],
        out_specs=[row(D_MODEL), row(D_MODEL), row(D_MODEL)],
        out_shape=[SDS((LP, D_MODEL), F32), SDS((LP, D_MODEL), F32), SDS((LP, D_MODEL), BF16)],
        compiler_params=_cparams(("parallel",)), name="attn_out")(o_a, o_b, w_out_b, h0, g_post, g_pre_ffn)


def _ffn_up(hn2, w_gu_b):
    def body(x_ref, wg_ref, wu_ref, g_ref, u_ref, act_ref):
        x = x_ref[...]
        g = jnp.dot(x, wg_ref[...], preferred_element_type=F32)
        u = jnp.dot(x, wu_ref[...], preferred_element_type=F32)
        g_ref[...] = g.astype(BF16)
        u_ref[...] = u.astype(BF16)
        act_ref[...] = (g * (1.0 / (1.0 + jnp.exp(-g))) * u).astype(BF16)

    out = pl.BlockSpec((LP, TN), lambda j: (0, j))
    return pl.pallas_call(
        body, grid=(D_FF // TN,),
        in_specs=[pl.BlockSpec((LP, D_MODEL), lambda j: (0, 0)),
                  pl.BlockSpec((D_MODEL, TN), lambda j: (0, j)),
                  pl.BlockSpec((D_MODEL, TN), lambda j: (0, j + D_FF // TN))],
        out_specs=[out, out, out],
        out_shape=[SDS((LP, D_FF), BF16)] * 3,
        compiler_params=_cparams(("parallel",)), name="ffn_up")(hn2, w_gu_b, w_gu_b)


def _ffn_down_loss(act, w_dn_b, h1, tgt, g_post_ffn):
    def body(act_ref, w_ref, h1_ref, t0_ref, t1_ref, t2_ref, g_ref, dff_ref, dy_ref, loss_ref, dg_ref):
        i = pl.program_id(0)
        target = jnp.concatenate([t0_ref[...], t1_ref[...], t2_ref[...]], axis=0)

        @pl.when(i == 0)
        def _():
            loss_ref[...] = jnp.zeros_like(loss_ref)
            dg_ref[...] = jnp.zeros_like(dg_ref)

        ff = jnp.dot(act_ref[...], w_ref[...], preferred_element_type=F32)
        r = _rms(ff)
        n = ff * r
        g = g_ref[...]
        y = h1_ref[...] + n * g
        rows = i * TM + lax.broadcasted_iota(jnp.int32, (TM, D_MODEL), 0)
        diff = jnp.where(rows >= ROW0, y - target, 0.0)
        loss_ref[...] += 0.5 * jnp.sum(diff * diff) / D_MODEL
        dy = diff / D_MODEL
        dy_ref[...] = dy
        dg_ref[...] += jnp.sum(dy * n, axis=0, keepdims=True)
        dff_ref[...] = _rms_bwd(n, r, g * dy).astype(BF16)

    row = pl.BlockSpec((TM, D_MODEL), lambda i: (i, 0))
    tblk = lambda j: pl.BlockSpec((BLK, D_MODEL), lambda i: (jnp.maximum(3 * i - 1 + j, 0), 0))
    return pl.pallas_call(
        body, grid=(NT,),
        in_specs=[pl.BlockSpec((TM, D_FF), lambda i: (i, 0)), pl.BlockSpec((D_FF, D_MODEL), lambda i: (0, 0)),
                  row, tblk(0), tblk(1), tblk(2), pl.BlockSpec((1, D_MODEL), lambda i: (0, 0))],
        out_specs=[row, row, pl.BlockSpec((8, BLK), lambda i: (0, 0)), pl.BlockSpec((1, D_MODEL), lambda i: (0, 0))],
        out_shape=[SDS((LP, D_MODEL), BF16), SDS((LP, D_MODEL), F32), SDS((8, BLK), F32), SDS((1, D_MODEL), F32)],
        compiler_params=_cparams(("arbitrary",)), name="ffn_down_loss")(act, w_dn_b, h1, tgt, tgt, tgt, g_post_ffn)


def _ffn_down_bwd(dff, w_dn_b, g, u):
    def body(d_ref, w_ref, g_ref, u_ref, dg_ref, du_ref):
        dact = lax.dot_general(d_ref[...], w_ref[...], NT_DIMS, preferred_element_type=F32)
        gg = g_ref[...].astype(F32)
        sig = 1.0 / (1.0 + jnp.exp(-gg))
        dg_ref[...] = (dact * u_ref[...].astype(F32) * sig * (1.0 + gg * (1.0 - sig))).astype(BF16)
        du_ref[...] = (dact * gg * sig).astype(BF16)

    blk = pl.BlockSpec((LP, TN), lambda j: (0, j))
    return pl.pallas_call(
        body, grid=(D_FF // TN,),
        in_specs=[pl.BlockSpec((LP, D_MODEL), lambda j: (0, 0)),
                  pl.BlockSpec((TN, D_MODEL), lambda j: (j, 0)), blk, blk],
        out_specs=[blk, blk],
        out_shape=[SDS((LP, D_FF), BF16)] * 2,
        compiler_params=_cparams(("parallel",)), name="ffn_down_bwd")(dff, w_dn_b, g, u)


def _ffn_up_bwd(dg, du, w_gu_b, h1, a, dy, g_pre_ffn, g_post_mix):
    def body(dg_ref, du_ref, w_ref, h1_ref, a_ref, dy_ref, gf_ref, gp_ref,
             dh1_ref, da_ref, dgf_ref, dgp_ref, acc):
        i = pl.program_id(0)
        s = pl.program_id(1)

        @pl.when((i == 0) & (s == 0))
        def _():
            dgf_ref[...] = jnp.zeros_like(dgf_ref)
            dgp_ref[...] = jnp.zeros_like(dgp_ref)

        @pl.when(s == 0)
        def _():
            acc[...] = jnp.zeros_like(acc)

        @pl.when(s < 2)
        def _():
            acc[...] += lax.dot_general(dg_ref[...], w_ref[...], NT_DIMS, preferred_element_type=F32)

        @pl.when(s >= 2)
        def _():
            acc[...] += lax.dot_general(du_ref[...], w_ref[...], NT_DIMS, preferred_element_type=F32)

        @pl.when(s == 3)
        def _():
            dhn2 = acc[...]
            h1 = h1_ref[...]
            r2 = _rms(h1)
            n2 = h1 * r2
            dgf_ref[...] += jnp.sum(dhn2 * n2, axis=0, keepdims=True)
            dh1 = dy_ref[...] + _rms_bwd(n2, r2, gf_ref[...] * dhn2)
            dh1_ref[...] = dh1
            av = a_ref[...]
            ra = _rms(av)
            na = av * ra
            dgp_ref[...] += jnp.sum(dh1 * na, axis=0, keepdims=True)
            da_ref[...] = _rms_bwd(na, ra, gp_ref[...] * dh1).astype(BF16)

    row = pl.BlockSpec((TM_EPI, D_MODEL), lambda i, s: (i, 0))
    vec = pl.BlockSpec((1, D_MODEL), lambda i, s: (0, 0))
    return pl.pallas_call(
        body, grid=(LP // TM_EPI, 4),
        in_specs=[pl.BlockSpec((TM_EPI, FF_T), lambda i, s: (i, jnp.minimum(s, 1))),
                  pl.BlockSpec((TM_EPI, FF_T), lambda i, s: (i, jnp.maximum(s - 2, 0))),
                  pl.BlockSpec((D_MODEL, FF_T), lambda i, s: (0, s)),
                  row, row, row, vec, vec],
        out_specs=[row, row, vec, vec],
        out_shape=[SDS((LP, D_MODEL), F32), SDS((LP, D_MODEL), BF16), SDS((1, D_MODEL), F32), SDS((1, D_MODEL), F32)],
        scratch_shapes=[pltpu.VMEM((TM_EPI, D_MODEL), F32)],
        compiler_params=_cparams(("arbitrary", "arbitrary")), name="ffn_up_bwd",
    )(dg, du, w_gu_b, h1, a, dy, g_pre_ffn, g_post_mix)


def _attn_out_bwd(da, w_out_b):
    def body(d_ref, w_ref, o_ref):
        o_ref[...] = lax.dot_general(d_ref[...], w_ref[...], NT_DIMS, preferred_element_type=F32).astype(BF16)

    row = pl.BlockSpec((TM_PURE, D_MODEL), lambda i: (i, 0))
    return pl.pallas_call(
        body, grid=(LP // TM_PURE,),
        in_specs=[row, pl.BlockSpec((D_MODEL, D_MODEL), lambda i: (0, 0))],
        out_specs=row, out_shape=SDS((LP, D_MODEL), BF16),
        compiler_params=_cparams(("parallel",)), name="attn_out_bwd")(da, w_out_b)


def _pre_mix_bwd(dq_a, dq_b, dk_b, dv_b, dk_a, dv_a, df, w_in_b, h0, dh1, g_pre_mix):
    def body(qa_ref, qb_ref, kb_ref, vb_ref, ka_ref, va_ref, f_ref, w_ref, h0_ref, dh1_ref, g_ref,
             dproj_ref, dh0_ref, dg_ref):
        i = pl.program_id(0)

        @pl.when(i == 0)
        def _():
            dg_ref[...] = jnp.zeros_like(dg_ref)

        dproj = jnp.concatenate(
            [qa_ref[...], ka_ref[...].astype(BF16), va_ref[...].astype(BF16), (qb_ref[...] * SCALE).astype(BF16),
             kb_ref[...], vb_ref[...], f_ref[...].astype(BF16)], axis=1)
        dproj_ref[...] = dproj
        dhn = lax.dot_general(dproj, w_ref[...], NT_DIMS, preferred_element_type=F32)
        x = h0_ref[...]
        r = _rms(x)
        n = x * r
        dg_ref[...] += jnp.sum(dhn * n, axis=0, keepdims=True)
        dh0_ref[...] = dh1_ref[...] + _rms_bwd(n, r, g_ref[...] * dhn)

    row = lambda w: pl.BlockSpec((TM_EPI, w), lambda i: (i, 0))
    vec = pl.BlockSpec((1, D_MODEL), lambda i: (0, 0))
    return pl.pallas_call(
        body, grid=(LP // TM_EPI,),
        in_specs=[row(512), row(512), row(512), row(512), row(BLK), row(BLK), row(BLK),
                  pl.BlockSpec((D_MODEL, D_PROJ_P), lambda i: (0, 0)), row(D_MODEL), row(D_MODEL), vec],
        out_specs=[row(D_PROJ_P), row(D_MODEL), vec],
        out_shape=[SDS((LP, D_PROJ_P), BF16), SDS((LP, D_MODEL), F32), SDS((1, D_MODEL), F32)],
        compiler_params=_cparams(("arbitrary",)), name="pre_mix_bwd",
    )(dq_a, dq_b, dk_b, dv_b, dk_a, dv_a, df, w_in_b, h0, dh1, g_pre_mix)


def _mm_tn(parts, b, tm, name, out_dtype=F32):
    widths = [p.shape[1] for p in parts]
    m_total = sum(widths)
    n = b.shape[1]
    whole = len(parts) > 1
    n_k = LP // TK_W
    assert (tm == m_total) if whole else (m_total % tm == 0)

    def body(*refs):
        a_refs, b_ref, o_ref, acc = refs[:-3], refs[-3], refs[-2], refs[-1]
        k = pl.program_id(1)

        @pl.when(k == 0)
        def _():
            acc[...] = jnp.zeros_like(acc)
        a = a_refs[0][...] if not whole else jnp.concatenate([r[...] for r in a_refs], axis=1)
        acc[...] += lax.dot_general(a, b_ref[...], TN_DIMS, preferred_element_type=F32)

        @pl.when(k == n_k - 1)
        def _():
            o_ref[...] = acc[...].astype(out_dtype)

    a_specs = ([pl.BlockSpec((TK_W, w), lambda mi, k: (k, 0)) for w in widths] if whole
               else [pl.BlockSpec((TK_W, tm), lambda mi, k: (k, mi))])
    return pl.pallas_call(
        body, grid=(m_total // tm, n_k),
        in_specs=a_specs + [pl.BlockSpec((TK_W, n), lambda mi, k: (k, 0))],
        out_specs=pl.BlockSpec((tm, n), lambda mi, k: (mi, 0)),
        out_shape=SDS((m_total, n), out_dtype),
        scratch_shapes=[pltpu.VMEM((tm, n), F32)],
        compiler_params=_cparams(("parallel", "arbitrary")), name=name)(*parts, b)


def _dw_gate_up(hn2, dg, du):
    n_k = LP // TK_W

    def body(a_ref, dg_ref, du_ref, o_ref, acc):
        s = pl.program_id(0)
        k = pl.program_id(1)

        @pl.when(k == 0)
        def _():
            acc[...] = jnp.zeros_like(acc)

        @pl.when(s < 2)
        def _():
            acc[...] += lax.dot_general(a_ref[...], dg_ref[...], TN_DIMS, preferred_element_type=F32)

        @pl.when(s >= 2)
        def _():
            acc[...] += lax.dot_general(a_ref[...], du_ref[...], TN_DIMS, preferred_element_type=F32)

        @pl.when(k == n_k - 1)
        def _():
            o_ref[0] = acc[...].astype(BF16)

    return pl.pallas_call(
        body, grid=(4, n_k),
        in_specs=[pl.BlockSpec((TK_W, D_MODEL), lambda s, k: (k, 0)),
                  pl.BlockSpec((TK_W, FF_T), lambda s, k: (k, jnp.minimum(s, 1))),
                  pl.BlockSpec((TK_W, FF_T), lambda s, k: (k, jnp.maximum(s - 2, 0)))],
        out_specs=pl.BlockSpec((1, D_MODEL, FF_T), lambda s, k: (s, 0, 0)),
        out_shape=SDS((4, D_MODEL, FF_T), BF16),
        scratch_shapes=[pltpu.VMEM((D_MODEL, FF_T), F32)],
        compiler_params=_cparams(("parallel", "arbitrary")), name="dw_gate_up")(hn2, dg, du)


def _split3(x):
    hi = x.astype(BF16)
    r1 = x - hi.astype(F32)
    mid = r1.astype(BF16)
    lo = (r1 - mid.astype(F32)).astype(BF16)
    return hi, mid, lo


def _tri_matmul(tri, x):
    hi, mid, lo = _split3(x)
    dot = lambda t: jnp.dot(tri, t, preferred_element_type=F32)
    return dot(hi) + dot(mid) + dot(lo)


def _forget_cumsum(f, b_forget_p):
    def body(f_ref, b_ref, cum_ref, carry):
        i = pl.program_id(0)

        @pl.when(i == 0)
        def _():
            carry[...] = jnp.zeros_like(carry)

        z = f_ref[...] + b_ref[...]
        ls = jnp.minimum(z, 0.0) - jnp.log(1.0 + jnp.exp(-jnp.abs(z)))
        rows = i * TM + lax.broadcasted_iota(jnp.int32, (TM, BLK), 0)
        ls = jnp.where(rows >= PAD_ROWS, ls, 0.0)
        r = lax.broadcasted_iota(jnp.int32, (TM, TM), 0)
        c = lax.broadcasted_iota(jnp.int32, (TM, TM), 1)
        tri = (c <= r).astype(BF16)
        cum = _tri_matmul(tri, ls) + carry[...]
        cum_ref[...] = cum
        carry[...] = cum[TM - 1:TM, :]

    return pl.pallas_call(
        body, grid=(NT,),
        in_specs=[pl.BlockSpec((TM, BLK), lambda i: (i, 0)), pl.BlockSpec((1, BLK), lambda i: (0, 0))],
        out_specs=pl.BlockSpec((TM, BLK), lambda i: (i, 0)),
        out_shape=SDS((LP, BLK), F32),
        scratch_shapes=[pltpu.VMEM((1, BLK), F32)],
        compiler_params=_cparams(("arbitrary",)), name="forget_cumsum")(f, b_forget_p)


def _forget_cumsum_bwd(dcum, f, b_forget_p):
    def body(d_ref, f_ref, b_ref, df_ref, db_ref, carry):
        i = pl.program_id(0)

        @pl.when(i == 0)
        def _():
            carry[...] = jnp.zeros_like(carry)
            db_ref[...] = jnp.zeros_like(db_ref)

        blk = NT - 1 - i
        r = lax.broadcasted_iota(jnp.int32, (TM, TM), 0)
        c = lax.broadcasted_iota(jnp.int32, (TM, TM), 1)
        tri = (c >= r).astype(BF16)
        d = d_ref[...]
        dls = _tri_matmul(tri, d) + carry[...]
        carry[...] = dls[0:1, :]
        z = f_ref[...] + b_ref[...]
        rows = blk * TM + lax.broadcasted_iota(jnp.int32, (TM, BLK), 0)
        df = jnp.where(rows >= PAD_ROWS, dls / (1.0 + jnp.exp(z)), 0.0)
        df_ref[...] = df
        db_ref[...] += jnp.sum(df, axis=0, keepdims=True)

    rev = pl.BlockSpec((TM, BLK), lambda i: (NT - 1 - i, 0))
    vec = pl.BlockSpec((1, BLK), lambda i: (0, 0))
    return pl.pallas_call(
        body, grid=(NT,),
        in_specs=[rev, rev, vec],
        out_specs=[rev, vec],
        out_shape=[SDS((LP, BLK), F32), SDS((1, BLK), F32)],
        scratch_shapes=[pltpu.VMEM((1, BLK), F32)],
        compiler_params=_cparams(("arbitrary",)), name="forget_cumsum_bwd")(dcum, f, b_forget_p)


def _lane_half(rows):
    return lax.broadcasted_iota(jnp.int32, (rows, BLK), 1) // HALF


def _fox_valid(qi, kj):
    qrow = qi * TM + lax.broadcasted_iota(jnp.int32, (TM, TM), 0)
    krow = kj * TM + lax.broadcasted_iota(jnp.int32, (TM, TM), 1)
    return (krow <= qrow) & ((krow >= PAD_ROWS) | (qrow < PAD_ROWS))


class _Rider:
    def __init__(self, operands, out_shapes, sem_counts, first, middle, last):
        self.operands, self.out_shapes, self.sem_counts = list(operands), list(out_shapes), list(sem_counts)
        self.first, self.middle, self.last = first, middle, last

    def scratch(self):
        return [pltpu.SemaphoreType.DMA((k,)) for k in self.sem_counts]

    def split(self, refs, n_in, n_out, n_scratch):
        a, b = len(self.operands), len(self.out_shapes)
        ins, mine_in = refs[:n_in], refs[n_in:n_in + a]
        outs, mine_out = refs[n_in + a:n_in + a + n_out], refs[n_in + a + n_out:n_in + a + n_out + b]
        rest = refs[n_in + a + n_out + b:]
        return ins, outs, rest[:n_scratch], (mine_in, mine_out, rest[n_scratch:])

    def at_steps(self, mine, is_first, is_middle, is_last):
        for cond, fn in ((is_first, self.first), (is_middle, self.middle), (is_last, self.last)):
            pl.when(cond)(lambda fn=fn: fn(*mine))


HBM_SPEC = pl.BlockSpec(memory_space=pltpu.HBM)


N_AUG = 3
QCHUNKS = ((0, 128), (128, 128), (256, 128))
KSUB = 384
AHEAD = 5
AHEAD_BWD = 1


def _fox_prep(proj, cum):
    def body(q0_ref, q1_ref, k0_ref, k1_ref, v0_ref, v1_ref, c_ref, qa_ref, ka_ref, vt_ref):
        half = _lane_half(TM)
        lane = lax.broadcasted_iota(jnp.int32, (TM, BLK), 1)
        for pp in range(4):
            cols = slice(pp * BLK, (pp + 1) * BLK)
            q_ref, k_ref, v_ref = ((q0_ref, k0_ref, v0_ref), (q1_ref, k1_ref, v1_ref))[pp // 2]
            part = slice((pp % 2) * BLK, (pp % 2 + 1) * BLK)
            qs = q_ref[:, part].astype(F32) * (SCALE * LOG2E)
            kp = k_ref[:, part].astype(F32)
            vp = v_ref[:, part]
            vt_ref[cols, :] = vp.astype(F32).T.astype(BF16)
            for e in range(2):
                h = 2 * pp + e
                a = (1 - e) * HALF
                blk = slice(h * BLK, (h + 1) * BLK)
                hi, mid, lo = _split3(-LOG2E * c_ref[:, h:h + 1])
                q_aug = jnp.where(half == e, qs, jnp.where((lane >= a) & (lane < a + N_AUG), 1.0, 0.0))
                k_aug = jnp.where(half == e, kp, jnp.where(
                    lane == a, hi.astype(F32), jnp.where(lane == a + 1, mid.astype(F32), jnp.where(
                        lane == a + 2, lo.astype(F32), 0.0))))
                qa_ref[blk, :] = q_aug.T.astype(BF16)
                ka_ref[:, blk] = k_aug.astype(BF16)

    row = lambda blk: pl.BlockSpec((TM, W2), lambda i: (i, blk))
    wide = pl.BlockSpec((TM, 1024), lambda i: (i, 0))
    return pl.pallas_call(
        body, grid=(NT,),
        in_specs=[row(QB), row(QB + 1), row(KB), row(KB + 1), row(VB), row(VB + 1),
                  pl.BlockSpec((TM, BLK), lambda i: (i, 0))],
        out_specs=[pl.BlockSpec((1024, TM), lambda i: (0, i)), wide, pl.BlockSpec((512, TM), lambda i: (0, i))],
        out_shape=[SDS((1024, LP), BF16), SDS((LP, 1024), BF16), SDS((512, LP), BF16)],
        compiler_params=_cparams(("parallel",)), name="fox_prep")(proj, proj, proj, proj, proj, proj, cum)


def _over_keys(reduce, x):
    slabs = x.reshape(x.shape[0] // HALF, HALF, x.shape[1])
    return reduce(reduce(slabs, axis=0), axis=0, keepdims=True)


def _fox_valid_t(qi, kj, c, r):
    krow = kj * TM + r * KSUB + lax.broadcasted_iota(jnp.int32, (KSUB, c[1]), 0)
    qrow = qi * TM + c[0] + lax.broadcasted_iota(jnp.int32, (KSUB, c[1]), 1)
    return (krow <= qrow) & ((krow >= PAD_ROWS) | (qrow < PAD_ROWS))


def _fox_fwd(q_aug, k_aug, v_t, rider):
    pairs = [(qi, kj) for qi in range(NT) for kj in range(qi + 1)]
    n_pairs = len(pairs)

    def body(qi_ref, kj_ref, *refs):
        (q_ref, k_ref, vt_ref), (o_ref, lse_ref), (m_s, l_s, acc_s), mine = rider.split(refs, 3, 2, 3)
        n = pl.program_id(0)
        qi = qi_ref[n]
        kj = kj_ref[n]
        rider.at_steps(mine, n == 0, n == n_pairs // 2, n == n_pairs - 1)

        @pl.when(kj == 0)
        def _():
            m_s[...] = jnp.full_like(m_s, NEG)
            l_s[...] = jnp.zeros_like(l_s)
            acc_s[...] = jnp.zeros_like(acc_s)

        def tile(masked):
            steps = [(h, c, r) for h in range(N_HEADS) for c in QCHUNKS for r in range(TM // KSUB)]

            def scores(h, c, r):
                blk = slice(h * BLK, (h + 1) * BLK)
                return jnp.dot(k_ref[r * KSUB:(r + 1) * KSUB, blk], q_ref[blk, c[0]:c[0] + c[1]],
                               preferred_element_type=F32)

            ahead = [scores(*st) for st in steps[:AHEAD]]
            for n, (h, c, r) in enumerate(steps):
                s_t = ahead.pop(0)
                if n + AHEAD < len(steps):
                    ahead.append(scores(*steps[n + AHEAD]))
                cs = slice(c[0], c[0] + c[1])
                if masked:
                    s_t = jnp.where(_fox_valid_t(qi, kj, c, r), s_t, NEG)
                m_prev = m_s[h, :, cs]
                m_new = jnp.maximum(m_prev, _over_keys(jnp.max, s_t))
                p_t = jnp.exp2(s_t - m_new)
                alpha = jnp.exp2(m_prev - m_new)
                l_s[h, :, cs] = alpha * l_s[h, :, cs] + _over_keys(jnp.sum, p_t)
                m_s[h, :, cs] = m_new
                vt = vt_ref[h * HALF:(h + 1) * HALF, r * KSUB:(r + 1) * KSUB]
                acc_s[h, :, cs] = acc_s[h, :, cs] * alpha + jnp.dot(vt, p_t.astype(BF16),
                                                                    preferred_element_type=F32)

        @pl.when((kj < qi) & (kj > 0))
        def _():
            tile(False)

        @pl.when((kj == qi) | (kj == 0))
        def _():
            tile(True)

        @pl.when(kj == qi)
        def _():
            for pp in range(4):
                both = jnp.concatenate([acc_s[2 * pp] * (1.0 / l_s[2 * pp]),
                                        acc_s[2 * pp + 1] * (1.0 / l_s[2 * pp + 1])], axis=0)
                o_ref[:, pp * BLK:(pp + 1) * BLK] = both.T.astype(BF16)
            for h in range(N_HEADS):
                lse_ref[h] = m_s[h] * LN2 + jnp.log(l_s[h])

    grid_spec = pltpu.PrefetchScalarGridSpec(
        num_scalar_prefetch=2, grid=(n_pairs,),
        in_specs=[pl.BlockSpec((1024, TM), lambda n, qi, kj: (0, qi[n])),
                  pl.BlockSpec((TM, 1024), lambda n, qi, kj: (kj[n], 0)),
                  pl.BlockSpec((512, TM), lambda n, qi, kj: (0, kj[n]))] + [HBM_SPEC] * len(rider.operands),
        out_specs=[pl.BlockSpec((TM, 512), lambda n, qi, kj: (qi[n], 0)),
                   pl.BlockSpec((N_HEADS, 1, TM), lambda n, qi, kj: (0, 0, qi[n]))]
        + [HBM_SPEC] * len(rider.out_shapes),
        scratch_shapes=[pltpu.VMEM((N_HEADS, 1, TM), F32), pltpu.VMEM((N_HEADS, 1, TM), F32),
                        pltpu.VMEM((N_HEADS, HALF, TM), F32)] + rider.scratch())
    o_b, lse, *carried = pl.pallas_call(
        body, grid_spec=grid_spec,
        out_shape=[SDS((LP, 512), BF16), SDS((N_HEADS, 1, LP), F32)] + rider.out_shapes,
        compiler_params=_cparams(("arbitrary",)), name="fox_fwd",
    )(jnp.asarray([p[0] for p in pairs], jnp.int32), jnp.asarray([p[1] for p in pairs], jnp.int32),
      q_aug, k_aug, v_t, *rider.operands)
    return o_b, lse, carried


def _fox_bwd(proj, o_b, dmix, lse, ck_t, rider):
    pairs = [(kj, qi) for kj in range(NT) for qi in range(kj, NT)]
    n_pairs = len(pairs)

    def body(kj_ref, qi_ref, *refs):
        ((q0_ref, q1_ref, k0_ref, k1_ref, v0_ref, v1_ref, o_ref, do_ref, lse_ref, ck_ref),
         (dq_ref, dk_ref, dv_ref, dck_ref, dcq_ref), (dk_s, dv_s, dck_s), mine) = rider.split(refs, 10, 5, 3)
        n = pl.program_id(0)
        kj = kj_ref[n]
        qi = qi_ref[n]
        rider.at_steps(mine, n == 0, n == n_pairs // 2, n == n_pairs - 1)

        @pl.when(n == 0)
        def _():
            dq_ref[...] = jnp.zeros_like(dq_ref)
            dcq_ref[...] = jnp.zeros_like(dcq_ref)

        @pl.when(qi == kj)
        def _():
            dk_s[...] = jnp.zeros_like(dk_s)
            dv_s[...] = jnp.zeros_like(dv_s)
            dck_s[...] = jnp.zeros_like(dck_s)

        def tile(masked):
            valid = _fox_valid(qi, kj) if masked else None
            half = _lane_half(TM)
            q0 = pl.multiple_of(qi * TM, TM)
            lane = lax.broadcasted_iota(jnp.int32, (TM, BLK), 1)
            row_sums = jnp.zeros((TM, BLK), F32)
            pair_ops = {}

            def operands(pp):
                if pp not in pair_ops:
                    cols = slice(pp * BLK, (pp + 1) * BLK)
                    q_ref, k_ref, v_ref = ((q0_ref, k0_ref, v0_ref), (q1_ref, k1_ref, v1_ref))[pp // 2]
                    part = slice((pp % 2) * BLK, (pp % 2 + 1) * BLK)
                    pair_ops[pp] = ((q_ref[:, part].astype(F32) * SCALE).astype(BF16), k_ref[:, part],
                                    v_ref[:, part], do_ref[:, cols])
                return pair_ops[pp]

            def scores(pp, e):
                qs, kp, vp, dop = operands(pp)
                ke = jnp.where(half == e, kp, jnp.zeros_like(kp))
                ve = jnp.where(half == e, vp, jnp.zeros_like(vp))
                return (lax.dot_general(qs, ke, NT_DIMS, preferred_element_type=F32),
                        lax.dot_general(dop, ve, NT_DIMS, preferred_element_type=F32), ke)

            steps = [(pp, e) for pp in range(4) for e in range(2)]
            ahead = [scores(*st) for st in steps[:AHEAD_BWD]]
            for n, (pp, e) in enumerate(steps):
                raw, dp, ke = ahead.pop(0)
                if n + AHEAD_BWD < len(steps):
                    ahead.append(scores(*steps[n + AHEAD_BWD]))
                h = 2 * pp + e
                cols = slice(pp * BLK, (pp + 1) * BLK)
                qs, kp, vp, dop = operands(pp)
                if e == 0:
                    prod = dop.astype(F32) * o_ref[:, cols].astype(F32)
                    d0 = jnp.sum(jnp.where(half == 0, prod, 0.0), axis=1, keepdims=True)
                    d1 = jnp.sum(prod, axis=1, keepdims=True) - d0
                    dq = jnp.zeros((TM, BLK), F32)
                    dks, dvs = [], []
                t = raw - ck_ref[h] - lse_ref[h]
                if masked:
                    t = jnp.where(valid, t, NEG)
                p = jnp.exp(t)
                ds = p * (dp - (d0 if e == 0 else d1))
                dck_s[h] += jnp.sum(ds, axis=0, keepdims=True)
                row_sums = jnp.where(lane == h, jnp.sum(ds, axis=1, keepdims=True), row_sums)
                ds_b = ds.astype(BF16)
                dq = dq + jnp.dot(ds_b, ke, preferred_element_type=F32)
                dks.append(lax.dot_general(ds_b, qs, TN_DIMS, preferred_element_type=F32))
                dvs.append(lax.dot_general(p.astype(BF16), dop, TN_DIMS, preferred_element_type=F32))
                if e == 1:
                    dq_ref[pl.ds(q0, TM), cols] += dq
                    dk_s[pp] += jnp.where(half == 0, dks[0], dks[1])
                    dv_s[pp] += jnp.where(half == 0, dvs[0], dvs[1])
            dcq_ref[pl.ds(q0, TM), :] += row_sums

        @pl.when((qi > kj) & (kj > 0))
        def _():
            tile(False)

        @pl.when((qi == kj) | (kj == 0))
        def _():
            tile(True)

        @pl.when(qi == NT - 1)
        def _():
            for pp in range(4):
                cols = slice(pp * BLK, (pp + 1) * BLK)
                dk_ref[:, cols] = dk_s[pp].astype(BF16)
                dv_ref[:, cols] = dv_s[pp].astype(BF16)
            dck_ref[...] = dck_s[...]

    qrow = lambda blk, w=512: pl.BlockSpec((TM, w), lambda n, kj, qi: (qi[n], blk))
    krow = lambda blk: pl.BlockSpec((TM, W2), lambda n, kj, qi: (kj[n], blk))
    grid_spec = pltpu.PrefetchScalarGridSpec(
        num_scalar_prefetch=2, grid=(n_pairs,),
        in_specs=[qrow(QB, W2), qrow(QB + 1, W2), krow(KB), krow(KB + 1), krow(VB), krow(VB + 1), qrow(0), qrow(1),
                  pl.BlockSpec((N_HEADS, TM, 1), lambda n, kj, qi: (0, qi[n], 0)),
                  pl.BlockSpec((N_HEADS, 1, TM), lambda n, kj, qi: (0, 0, kj[n]))] + [HBM_SPEC] * len(rider.operands),
        out_specs=[pl.BlockSpec((LP, 512), lambda n, kj, qi: (0, 0)),
                   pl.BlockSpec((TM, 512), lambda n, kj, qi: (kj[n], 0)),
                   pl.BlockSpec((TM, 512), lambda n, kj, qi: (kj[n], 0)),
                   pl.BlockSpec((N_HEADS, 1, TM), lambda n, kj, qi: (0, 0, kj[n])),
                   pl.BlockSpec((LP, BLK), lambda n, kj, qi: (0, 0))] + [HBM_SPEC] * len(rider.out_shapes),
        scratch_shapes=[pltpu.VMEM((4, TM, BLK), F32), pltpu.VMEM((4, TM, BLK), F32),
                        pltpu.VMEM((N_HEADS, 1, TM), F32)] + rider.scratch())
    dq, dk, dv, dck, dcq, *carried = pl.pallas_call(
        body, grid_spec=grid_spec,
        out_shape=[SDS((LP, 512), F32), SDS((LP, 512), BF16), SDS((LP, 512), BF16), SDS((N_HEADS, 1, LP), F32),
                   SDS((LP, BLK), F32)] + rider.out_shapes,
        compiler_params=_cparams(("arbitrary",)), name="fox_bwd",
    )(jnp.asarray([p[0] for p in pairs], jnp.int32), jnp.asarray([p[1] for p in pairs], jnp.int32),
      proj, proj, proj, proj, proj, proj, o_b, dmix, lse, ck_t, *rider.operands)
    return dq, dk, dv, dck, dcq, carried


N_SEG = 3
N_KEY = N_SEG * BLK
GROUP = 4
QW = GROUP * BLK


def _bucket_tables_t():
    return np.ascontiguousarray(_bucket_tables().transpose(0, 2, 1))


def _stack_heads(ref, g, scale):
    half = _lane_half(BLK)
    out = []
    for pair in range(2):
        x = ref[:, (2 * g + pair) * BLK:(2 * g + pair + 1) * BLK].astype(F32) * scale
        swapped = pltpu.roll(x, HALF, 1)
        for e in range(2):
            out.append(jnp.where(half == g, x if e == g else swapped, 0.0).astype(BF16))
    return jnp.concatenate(out, axis=0)


def _unstack_heads(x_t, g, ref, scale):
    for pair in range(2):
        both = jnp.concatenate([x_t[:, (2 * pair) * BLK:(2 * pair + 1) * BLK],
                                x_t[:, (2 * pair + 1) * BLK:(2 * pair + 2) * BLK]], axis=0)
        ref[:, (2 * g + pair) * BLK:(2 * g + pair + 1) * BLK] = (both.T * scale).astype(ref.dtype)


def _swa_tables(tab_ref, sink_ref, bkt_ref, tbl, sink_row):
    kk = lax.broadcasted_iota(jnp.int32, (BLK, BLK), 0)
    qq = lax.broadcasted_iota(jnp.int32, (BLK, BLK), 1)
    neg = jnp.full((BLK, BLK), NEG, F32)
    lane = lax.broadcasted_iota(jnp.int32, (1, QW), 1) // BLK
    for g in range(2):
        row = jnp.zeros((1, QW), F32)
        for hh in range(GROUP):
            h = GROUP * g + hh
            cols = slice(hh * BLK, (hh + 1) * BLK)
            row = jnp.where(lane == hh, sink_ref[0, h], row)

            def step(b, carry, h=h):
                t = tab_ref[b, h]
                return jnp.where(bkt_ref[0] == b, t, carry[0]), jnp.where(bkt_ref[1] == b, t, carry[1])
            zero = jnp.zeros((BLK, BLK), F32)
            cur, prev = lax.fori_loop(0, N_BUCKETS, step, (zero, zero))
            far = jnp.full((BLK, BLK), tab_ref[N_BUCKETS - 1, h], F32)
            causal = jnp.where(kk <= qq, cur, neg)
            segments = [
                (neg, neg, jnp.where(kk >= PAD_ROWS, causal, neg)),
                (jnp.where(kk >= PAD_ROWS, prev, neg), neg, causal),
                (jnp.where(kk >= PAD_ROWS, far, neg), jnp.where(kk > qq, prev, neg), causal)]
            for case in range(3):
                for seg in range(N_SEG):
                    tbl[case, g, seg * BLK:(seg + 1) * BLK, cols] = segments[case][seg]
        sink_row[g] = row


def _swa_prep(proj):
    rows = LP // 3

    def body(k_ref, v_ref, kt_ref, vt_ref):
        kt_ref[...] = k_ref[...].astype(F32).T.astype(BF16)
        vt_ref[...] = v_ref[...].astype(F32).T.astype(BF16)

    col = pl.BlockSpec((BLK, rows), lambda i: (0, i))
    return pl.pallas_call(
        body, grid=(3,),
        in_specs=[pl.BlockSpec((rows, BLK), lambda i: (i, KA)), pl.BlockSpec((rows, BLK), lambda i: (i, VA))],
        out_specs=[col, col], out_shape=[SDS((BLK, LP), BF16)] * 2,
        compiler_params=_cparams(("parallel",)), name="swa_prep")(proj, proj)


def _segments(ref, i, by_rows):
    starts = [0, pl.multiple_of(jnp.maximum(i - 1, 0) * BLK, BLK), pl.multiple_of(i * BLK, BLK)]
    if by_rows:
        return jnp.concatenate([ref[pl.ds(s, BLK), :] for s in starts], axis=0)
    return jnp.concatenate([ref[:, pl.ds(s, BLK)] for s in starts], axis=1)


def _swa_fwd(proj, vt_a, rel_bias, sinks, bkt_t, rider):
    def body(*refs):
        ((tab_ref, sink_ref, bkt_ref, q_ref, k_ref, vt_ref), (o_ref, lse_ref),
         (tbl, sink_row), mine) = rider.split(refs, 6, 2, 2)
        i = pl.program_id(0)
        rider.at_steps(mine, i == 0, i == NBLK // 2, i == NBLK - 1)

        @pl.when(i == 0)
        def _():
            _swa_tables(tab_ref, sink_ref, bkt_ref, tbl, sink_row)

        case = jnp.minimum(i, 2)
        k_cat = _segments(k_ref, i, True)
        vt_cat = _segments(vt_ref, i, False)
        raw = [lax.dot_general(k_cat, _stack_heads(q_ref, g, SCALE), NT_DIMS, preferred_element_type=F32)
               for g in range(2)]
        for g in range(2):
            s_t = raw[g] + tbl[case, g]
            sink = sink_row[g]
            m = jnp.maximum(_over_keys(jnp.max, s_t), sink)
            p_t = jnp.exp(s_t - m)
            l = _over_keys(jnp.sum, p_t) + jnp.exp(sink - m)
            o_t = jnp.dot(vt_cat[g * HALF:(g + 1) * HALF, :], p_t.astype(BF16), preferred_element_type=F32)
            _unstack_heads(o_t * (1.0 / l), g, o_ref, 1.0)
            lse = m + jnp.log(l)
            for hh in range(GROUP):
                lse_ref[GROUP * g + hh] = lse[:, hh * BLK:(hh + 1) * BLK]

    smem = pl.BlockSpec(memory_space=pltpu.SMEM)
    o_a, lse, *carried = pl.pallas_call(
        body, grid=(NBLK,),
        in_specs=[smem, smem, pl.BlockSpec((2, BLK, BLK), lambda i: (0, 0, 0)),
                  pl.BlockSpec((BLK, 512), lambda i: (i, QA)), pl.BlockSpec((LP, BLK), lambda i: (0, KA)),
                  pl.BlockSpec((BLK, LP), lambda i: (0, 0))] + [HBM_SPEC] * len(rider.operands),
        out_specs=[pl.BlockSpec((BLK, 512), lambda i: (i, 0)),
                   pl.BlockSpec((N_HEADS, 1, BLK), lambda i: (0, 0, i))] + [HBM_SPEC] * len(rider.out_shapes),
        out_shape=[SDS((LP, 512), BF16), SDS((N_HEADS, 1, LP), F32)] + rider.out_shapes,
        scratch_shapes=[pltpu.VMEM((3, 2, N_KEY, QW), F32), pltpu.VMEM((2, 1, QW), F32)] + rider.scratch(),
        compiler_params=_cparams(("arbitrary",)), name="swa_fwd",
    )(rel_bias, sinks, bkt_t, proj, proj, vt_a, *rider.operands)
    return o_a, lse, carried


def _swa_bwd(proj, kt_a, o_a, dmix, lse, rel_bias, sinks, bkt_t):
    def body(tab_ref, sink_ref, bkt_ref, q_ref, k_ref, v_ref, kt_ref, o_ref, do_ref, lse_ref,
             dq_ref, dk_ref, dv_ref, dbias_ref, dsink_ref, tbl, sink_row, acc, dsk):
        i = pl.program_id(0)

        @pl.when(i == 0)
        def _():
            _swa_tables(tab_ref, sink_ref, bkt_ref, tbl, sink_row)
            dk_ref[...] = jnp.zeros_like(dk_ref)
            dv_ref[...] = jnp.zeros_like(dv_ref)
            acc[...] = jnp.zeros_like(acc)
            dsk[...] = jnp.zeros_like(dsk)

        case = jnp.minimum(i, 2)
        first = jnp.full((BLK, QW), i, jnp.int32) == 1
        k_cat = _segments(k_ref, i, True)
        v_cat = _segments(v_ref, i, True)
        kt_cat = _segments(kt_ref, i, False)
        dk_cat = jnp.zeros((N_KEY, BLK), F32)
        dv_cat = jnp.zeros((N_KEY, BLK), F32)
        for g in range(2):
            d_parts = []
            for pair in range(2):
                cols = slice((2 * g + pair) * BLK, (2 * g + pair + 1) * BLK)
                prod_t = (do_ref[:, cols].astype(F32) * o_ref[:, cols].astype(F32)).T
                d_parts += [jnp.sum(prod_t[:HALF], axis=0, keepdims=True),
                            jnp.sum(prod_t[HALF:], axis=0, keepdims=True)]
            d_row = jnp.concatenate(d_parts, axis=1)
            lse_row = jnp.concatenate([lse_ref[GROUP * g + hh] for hh in range(GROUP)], axis=1)
            q_st = _stack_heads(q_ref, g, SCALE)
            do_st = _stack_heads(do_ref, g, 1.0)
            s_t = lax.dot_general(k_cat, q_st, NT_DIMS, preferred_element_type=F32) + tbl[case, g]
            p_t = jnp.exp(s_t - lse_row)
            dp_t = lax.dot_general(v_cat, do_st, NT_DIMS, preferred_element_type=F32)
            ds_t = p_t * (dp_t - d_row)
            dsk[g] += -jnp.exp(sink_row[g] - lse_row) * d_row
            acc[g, 0:BLK] += jnp.where(first, 0.0, ds_t[0:BLK])
            acc[g, BLK:2 * BLK] += jnp.where(first, ds_t[0:BLK], ds_t[BLK:2 * BLK])
            acc[g, 2 * BLK:N_KEY] += ds_t[2 * BLK:N_KEY]
            ds_b = ds_t.astype(BF16)
            dk_cat = dk_cat + jnp.dot(ds_b, q_st, preferred_element_type=F32)
            dv_cat = dv_cat + jnp.dot(p_t.astype(BF16), do_st, preferred_element_type=F32)
            dq_t = jnp.dot(kt_cat[g * HALF:(g + 1) * HALF, :], ds_b, preferred_element_type=F32)
            _unstack_heads(dq_t, g, dq_ref, SCALE)

        prev0 = pl.multiple_of(jnp.maximum(i - 1, 0) * BLK, BLK)
        cur0 = pl.multiple_of(i * BLK, BLK)
        for ref, cat in ((dk_ref, dk_cat), (dv_ref, dv_cat)):
            ref[0:BLK, :] += cat[0:BLK]
            ref[pl.ds(prev0, BLK), :] += cat[BLK:2 * BLK]
            ref[pl.ds(cur0, BLK), :] += cat[2 * BLK:N_KEY]

        @pl.when(i == NBLK - 1)
        def _():
            lane = lax.broadcasted_iota(jnp.int32, (1, BLK), 1)

            def per_bucket(b, carry):
                row = jnp.zeros((1, BLK), F32)
                for h in range(N_HEADS):
                    g, cols = h // GROUP, slice((h % GROUP) * BLK, (h % GROUP + 1) * BLK)
                    val = (jnp.sum(jnp.where(bkt_ref[0] == b, acc[g, 2 * BLK:N_KEY, cols], 0.0), keepdims=True)
                           + jnp.sum(jnp.where(bkt_ref[1] == b, acc[g, BLK:2 * BLK, cols], 0.0), keepdims=True))
                    row = jnp.where(lane == h, val, row)
                dbias_ref[pl.ds(b, 1), :] = row
                return carry

            lax.fori_loop(0, N_BUCKETS, per_bucket, 0)
            far = jnp.zeros((1, BLK), F32)
            dsr = jnp.zeros((1, BLK), F32)
            for h in range(N_HEADS):
                g, cols = h // GROUP, slice((h % GROUP) * BLK, (h % GROUP + 1) * BLK)
                far = jnp.where(lane == h, jnp.sum(acc[g, 0:BLK, cols], keepdims=True), far)
                dsr = jnp.where(lane == h, jnp.sum(dsk[g, :, cols], keepdims=True), dsr)
            dbias_ref[N_BUCKETS - 1:N_BUCKETS, :] += far
            dsink_ref[...] = dsr

    smem = pl.BlockSpec(memory_space=pltpu.SMEM)
    blk512 = lambda col: pl.BlockSpec((BLK, 512), lambda i: (i, col))
    full = lambda r, c: pl.BlockSpec((r, c), lambda i: (0, 0))
    return pl.pallas_call(
        body, grid=(NBLK,),
        in_specs=[smem, smem, pl.BlockSpec((2, BLK, BLK), lambda i: (0, 0, 0)), blk512(QA),
                  pl.BlockSpec((LP, BLK), lambda i: (0, KA)), pl.BlockSpec((LP, BLK), lambda i: (0, VA)),
                  full(BLK, LP), blk512(0), blk512(0), pl.BlockSpec((N_HEADS, 1, BLK), lambda i: (0, 0, i))],
        out_specs=[blk512(0), full(LP, BLK), full(LP, BLK), full(N_BUCKETS, BLK), full(1, BLK)],
        out_shape=[SDS((LP, 512), BF16), SDS((LP, BLK), F32), SDS((LP, BLK), F32),
                   SDS((N_BUCKETS, BLK), F32), SDS((1, BLK), F32)],
        scratch_shapes=[pltpu.VMEM((3, 2, N_KEY, QW), F32), pltpu.VMEM((2, 1, QW), F32),
                        pltpu.VMEM((2, N_KEY, QW), F32), pltpu.VMEM((2, 1, QW), F32)],
        compiler_params=_cparams(("arbitrary",)), name="swa_bwd",
    )(rel_bias, sinks, bkt_t, proj, proj, proj, kt_a, o_a, dmix, lse)


def _local_step(x, tgt, meta, rel_bias, g_pre_mix, g_post_mix, g_pre_ffn, g_post_ffn, b_forget, sinks,
                w_in_b, out_rider, out_weight, ffn_rider, ffn_weights, early_grads):
    bkt_t = jnp.asarray(_bucket_tables_t())
    h0 = jnp.concatenate([jnp.zeros((PAD_ROWS, D_MODEL), F32), meta, x], axis=0)
    b_p = jnp.pad(b_forget, ((0, 0), (0, BLK - N_HEADS)))

    hn1, proj, f = _pre_mix(h0, g_pre_mix, w_in_b)
    kt_a, vt_a = _swa_prep(proj)
    o_a, lse_a, carried_out = _swa_fwd(proj, vt_a, rel_bias, sinks, bkt_t, out_rider)
    w_out_b = out_weight(carried_out)
    cum = _forget_cumsum(f, b_p)
    ck_t = cum[:, :N_HEADS].T.reshape(N_HEADS, 1, LP)
    q_aug, k_aug, v_t = _fox_prep(proj, cum)
    o_b, lse_row, carried = _fox_fwd(q_aug, k_aug, v_t, ffn_rider)
    lse_b = lse_row.reshape(N_HEADS, LP, 1)
    w_gu_b, w_dn_b = ffn_weights(carried)
    a, h1, hn2 = _attn_out(o_a, o_b, w_out_b, h0, g_post_mix, g_pre_ffn)
    g, u, act = _ffn_up(hn2, w_gu_b)
    dff, dy, loss_blk, dg_post_ffn = _ffn_down_loss(act, w_dn_b, h1, tgt, g_post_ffn)

    dw_dn = _mm_tn([act], dff, FF_T, "dw_down", BF16)
    dg, du = _ffn_down_bwd(dff, w_dn_b, g, u)
    dw_gu = _dw_gate_up(hn2, dg, du)
    dh1, da, dg_pre_ffn, dg_post_mix = _ffn_up_bwd(dg, du, w_gu_b, h1, a, dy, g_pre_ffn, g_post_mix)
    dw_out = _mm_tn([o_a, o_b], da, D_MODEL, "dw_out", BF16)
    dmix = _attn_out_bwd(da, w_out_b)
    dq_b, dk_b, dv_b, dck, dcq, landed = _fox_bwd(proj, o_b, dmix, lse_b, ck_t, early_grads(dw_gu, dw_dn, dw_out))
    dq_a, dk_a, dv_a, dbias, dsink = _swa_bwd(proj, kt_a, o_a, dmix, lse_a, rel_bias, sinks, bkt_t)
    dcum = dcq - jnp.pad(dck.reshape(N_HEADS, LP).T, ((0, 0), (0, BLK - N_HEADS)))
    df, db = _forget_cumsum_bwd(dcum, f, b_p)
    dproj, dh0, dg_pre_mix = _pre_mix_bwd(dq_a, dq_b, dk_b, dv_b, dk_a, dv_a, df, w_in_b, h0, dh1, g_pre_mix)
    dw_in = _mm_tn([hn1], dproj, D_MODEL, "dw_in", BF16)

    return dict(loss=loss_blk[0, 0], grad_x=dh0[ROW0:], meta=dh0[PAD_ROWS:ROW0],
                rel_bias=dbias[:, :N_HEADS], ln_pre_mix=dg_pre_mix, ln_post_mix=dg_post_mix,
                ln_pre_ffn=dg_pre_ffn, ln_post_ffn=dg_post_ffn, b_forget=db[:, :N_HEADS],
                sinks=dsink[:, :N_HEADS], w_in=dw_in, w_out=dw_out, w_gate_up=dw_gu, w_down=dw_dn,
                landed=landed)


N_SMALL = 24
LOSS_ROW = 6


def _place():
    x, y, c = lax.axis_index("x"), lax.axis_index("y"), lax.axis_index("c")
    return x, y, c, [(1 - x, y), (x, 1 - y), (1 - x, 1 - y)]


def _run_alone(rider, name):
    a, b = len(rider.operands), len(rider.out_shapes)

    def body(*refs):
        mine = (refs[:a], refs[a:a + b], refs[a + b:])
        rider.first(*mine)
        rider.middle(*mine)
        rider.last(*mine)

    return pl.pallas_call(body, in_specs=[HBM_SPEC] * a, out_specs=[HBM_SPEC] * b, out_shape=rider.out_shapes,
                          scratch_shapes=rider.scratch(), name=name)(*rider.operands)


def _gather_rider(shards, own_too, by_columns=()):
    n = len(shards)

    def slot(a, outs, chip, h):
        if a in by_columns:
            cols = shards[a].shape[2]
            return outs[a].at[h, :, pl.ds(pl.multiple_of(chip * cols, BLK), cols)]
        return outs[a].at[chip, h]

    def own_copies(ins, outs, sems):
        x, y, _, _ = _place()
        if not own_too:
            return []
        return [pltpu.make_async_copy(ins[a].at[h], slot(a, outs, 2 * x + y, h), sems[2].at[2 * a + h])
                for a in range(n) for h in range(2)]

    def copies(ins, outs, sems):
        send_sems, recv_sems = sems[:2]
        x, y, c, others = _place()
        chip = 2 * x + y
        sibling = (x, y, 1 - c)

        def rc(a, k, src, dst, to):
            return pltpu.make_async_remote_copy(src_ref=src, dst_ref=dst, send_sem=send_sems.at[6 * a + k],
                                                recv_sem=recv_sems.at[6 * a + k], device_id=to, device_id_type=MESH)

        pairs = [(a, j, ox, oy) for a in range(n) for j, (ox, oy) in enumerate(others)]
        there = lambda a, ox, oy, h: slot(a, outs, 2 * ox + oy, h)
        return dict(
            sent=lambda: [rc(a, j, ins[a].at[c], slot(a, outs, chip, c), (ox, oy, c)) for a, j, ox, oy in pairs],
            landed=lambda: [rc(a, j, there(a, ox, oy, c), there(a, ox, oy, c), sibling) for a, j, ox, oy in pairs],
            passed=lambda: [rc(a, 3 + j, there(a, ox, oy, c), there(a, ox, oy, c), sibling)
                            for a, j, ox, oy in pairs],
            arriving=lambda: [rc(a, 3 + j, there(a, ox, oy, 1 - c), there(a, ox, oy, 1 - c), sibling)
                              for a, j, ox, oy in pairs])

    def first(*mine):
        for cp in copies(*mine)["sent"]() + own_copies(*mine):
            cp.start()

    def middle(*mine):
        kinds = copies(*mine)
        for got, cp in zip(kinds["landed"](), kinds["passed"]()):
            got.wait_recv()
            cp.start()

    def last(*mine):
        kinds = copies(*mine)
        for cp in kinds["arriving"]():
            cp.wait_recv()
        for cp in kinds["sent"]() + kinds["passed"]():
            cp.wait_send()
        for cp in own_copies(*mine):
            cp.wait()

    shapes = [SDS((2, s.shape[1], 4 * s.shape[2]) if a in by_columns else (4,) + s.shape, s.dtype)
              for a, s in enumerate(shards)]
    return _Rider(shards, shapes, [6 * n, 6 * n] + [2 * n] * own_too, first, middle, last)


def _swap_rider(grads):
    n = len(grads)
    slabs = [(a, s) for a in range(n) for s in range(grads[a].shape[0])]

    def copies(ins, outs, sems):
        x, y, c, _ = _place()
        return [pltpu.make_async_remote_copy(
            src_ref=ins[a].at[s, 1 - c], dst_ref=outs[a].at[s], send_sem=sems[0].at[k], recv_sem=sems[1].at[k],
            device_id=(x, y, 1 - c), device_id_type=MESH) for k, (a, s) in enumerate(slabs)]

    def first(*mine):
        for cp in copies(*mine):
            cp.start()

    def middle(*mine):
        pass

    def last(*mine):
        for cp in copies(*mine):
            cp.wait()

    return _Rider(grads, [SDS(g.shape[:1] + g.shape[2:], g.dtype) for g in grads], [len(slabs), len(slabs)],
                  first, middle, last)


def _pair_sum(g, got, c_arr, name):
    n_s, rh, cc = got.shape

    def body(c_ref, g_ref, p_ref, o_ref):
        o_ref[0] = (g_ref[0, 0].astype(F32) + p_ref[0].astype(F32)).astype(BF16)

    grid_spec = pltpu.PrefetchScalarGridSpec(
        num_scalar_prefetch=1, grid=(n_s,),
        in_specs=[pl.BlockSpec((1, 1, rh, cc), lambda s, c_ref: (s, c_ref[0], 0, 0)),
                  pl.BlockSpec((1, rh, cc), lambda s, c_ref: (s, 0, 0))],
        out_specs=pl.BlockSpec((1, rh, cc), lambda s, c_ref: (s, 0, 0)))
    return pl.pallas_call(body, grid_spec=grid_spec, out_shape=SDS((n_s, rh, cc), BF16),
                          compiler_params=_cparams(("parallel",)), name=name)(c_arr, g, got)


def _direct_rider(grads):
    n = len(grads)

    def copies(ins, outs, sems):
        x, y, c, others = _place()
        peers = [(x, y, 1 - c)] + [(ox, oy, c) for ox, oy in others] + [(ox, oy, 1 - c) for ox, oy in others]
        return [pltpu.make_async_remote_copy(
            src_ref=ins[a].at[2 * px + py, pc], dst_ref=outs[a].at[k], send_sem=sems[0].at[7 * a + k],
            recv_sem=sems[1].at[7 * a + k], device_id=(px, py, pc), device_id_type=MESH)
            for a in range(n) for k, (px, py, pc) in enumerate(peers)]

    def first(*mine):
        for cp in copies(*mine):
            cp.start()

    def middle(*mine):
        pass

    def last(*mine):
        for cp in copies(*mine):
            cp.wait()

    return _Rider(grads, [SDS((7,) + g.shape[2:], g.dtype) for g in grads], [7 * n, 7 * n], first, middle, last)


def _owner_sum(grads, landed, own_arr, after, name):
    rh, cc = landed.shape[1:]
    tr = rh // 2

    def body(own_ref, g_ref, p_ref, after_ref, o_ref):
        total = g_ref[0, 0].astype(F32)
        for k in range(7):
            total = total + p_ref[k].astype(F32)
        o_ref[...] = total

    grid_spec = pltpu.PrefetchScalarGridSpec(
        num_scalar_prefetch=1, grid=(2,),
        in_specs=[pl.BlockSpec((1, 1, tr, cc), lambda i, own: (own[0], own[1], i, 0)),
                  pl.BlockSpec((7, tr, cc), lambda i, own: (0, i, 0)), pl.BlockSpec(memory_space=pl.ANY)],
        out_specs=pl.BlockSpec((tr, cc), lambda i, own: (i, 0)))
    return pl.pallas_call(body, grid_spec=grid_spec, out_shape=SDS((rh, cc), F32),
                          compiler_params=_cparams(("parallel",)), name=name)(own_arr, grads, landed, after)


SEM_SPEC = pl.BlockSpec(memory_space=pltpu.SEMAPHORE)
N_LATE = 10


def _late_copies(part_ref, landed_ref, small_ref, all_ref, send_sems, recv_sems):
    x, y, c, others = _place()
    me = 4 * x + 2 * y + c
    peers = [(x, y, 1 - c)] + [(ox, oy, c) for ox, oy in others] + [(ox, oy, 1 - c) for ox, oy in others]
    big = [pltpu.make_async_remote_copy(
        src_ref=part_ref.at[2 * ox + oy], dst_ref=landed_ref.at[j], send_sem=send_sems.at[j], recv_sem=recv_sems.at[j],
        device_id=(ox, oy, c), device_id_type=MESH) for j, (ox, oy) in enumerate(others)]
    small = [pltpu.make_async_remote_copy(
        src_ref=small_ref, dst_ref=all_ref.at[me], send_sem=send_sems.at[3 + k], recv_sem=recv_sems.at[3 + k],
        device_id=peer, device_id_type=MESH) for k, peer in enumerate(peers)]
    return big + small


def _late_exchange_start(part, small):
    def body(part_ref, landed_ref, small_ref, all_ref, send_sems, recv_sems, part_o, landed_o, small_o, all_o, token):
        for cp in _late_copies(part_ref, landed_ref, small_ref, all_ref, send_sems, recv_sems):
            cp.start()
        token[...] = jnp.zeros_like(token)

    hbm = lambda a: pltpu.HBM(a.shape, a.dtype)
    landed = lax.empty((3,) + part.shape[1:], part.dtype)
    everyone = lax.empty((8,) + small.shape, small.dtype)
    operands = [pltpu.with_memory_space_constraint(a, pltpu.HBM) for a in (part, landed, small, everyone)]
    return pl.pallas_call(
        body, name="late_exchange_start",
        out_shape=(pltpu.SemaphoreType.DMA((N_LATE,)), pltpu.SemaphoreType.DMA((N_LATE,)),
                   hbm(part), hbm(landed), hbm(small), hbm(everyone), SDS((8, BLK), F32)),
        in_specs=[HBM_SPEC] * 4,
        out_specs=(SEM_SPEC, SEM_SPEC, HBM_SPEC, HBM_SPEC, HBM_SPEC, HBM_SPEC, pl.BlockSpec(memory_space=pltpu.VMEM)),
        input_output_aliases={0: 2, 1: 3, 2: 4, 3: 5},
        compiler_params=pltpu.CompilerParams(has_side_effects=pltpu.SideEffectType.DATAFLOW_SIDE_EFFECTING),
    )(*operands)


def _late_exchange_wait(send_sems, recv_sems, part, landed, small, everyone, after):
    def body(part_ref, landed_ref, small_ref, all_ref, send_sems, recv_sems, after_ref, part_o, landed_o, small_o, all_o):
        for cp in _late_copies(part_ref, landed_ref, small_ref, all_ref, send_sems, recv_sems):
            cp.wait_send()
            cp.wait_recv()

    hbm = lambda a: pltpu.HBM(a.shape, a.dtype)
    out = pl.pallas_call(
        body, name="late_exchange_wait",
        out_shape=(hbm(part), hbm(landed), hbm(small), hbm(everyone)),
        in_specs=[HBM_SPEC] * 4 + [SEM_SPEC, SEM_SPEC, pl.BlockSpec(memory_space=pl.ANY)],
        out_specs=(HBM_SPEC,) * 4, input_output_aliases={0: 0, 1: 1, 2: 2, 3: 3},
        compiler_params=pltpu.CompilerParams(has_side_effects=pltpu.SideEffectType.DATAFLOW_SIDE_EFFECTING),
    )(part, landed, small, everyone, send_sems, recv_sems, after)
    return out[0], out[1], out[3]


def _chip_sum(parts, landed, chip_arr, name):
    rh, cc = landed.shape[1:]
    tr = rh // 2

    def body(chip_ref, own_ref, p_ref, o_ref):
        o_ref[...] = ((own_ref[0].astype(F32) + p_ref[0].astype(F32)) + p_ref[1].astype(F32)) + p_ref[2].astype(F32)

    grid_spec = pltpu.PrefetchScalarGridSpec(
        num_scalar_prefetch=1, grid=(2,),
        in_specs=[pl.BlockSpec((1, tr, cc), lambda i, chip_ref: (chip_ref[0], i, 0)),
                  pl.BlockSpec((3, tr, cc), lambda i, chip_ref: (0, i, 0))],
        out_specs=pl.BlockSpec((tr, cc), lambda i, chip_ref: (i, 0)))
    return pl.pallas_call(body, grid_spec=grid_spec, out_shape=SDS((rh, cc), F32),
                          compiler_params=_cparams(("parallel",)), name=name)(chip_arr, parts, landed)


def _device_sum(p):
    def body(p_ref, o_ref):
        acc = p_ref[0]
        for k in range(1, 8):
            acc = acc + p_ref[k]
        o_ref[...] = acc

    return pl.pallas_call(body, out_shape=SDS(p.shape[1:], F32), name="small_sum")(p)


def _join_halves(halves, name):
    n = len(halves)

    def body(*refs):
        ins, outs = refs[:n], refs[n:2 * n]
        send_sems, recv_sems = refs[2 * n:]
        x, y, c, _ = _place()
        copies = [pltpu.make_async_remote_copy(
            src_ref=ins[a], dst_ref=outs[a], send_sem=send_sems.at[a], recv_sem=recv_sems.at[a],
            device_id=(x, y, 1 - c), device_id_type=MESH) for a in range(n)]
        for cp in copies:
            cp.start()
        for cp in copies:
            cp.wait()

    return pl.pallas_call(
        body, in_specs=[HBM_SPEC] * n, out_specs=[HBM_SPEC] * n,
        out_shape=[SDS(h.shape, h.dtype) for h in halves],
        scratch_shapes=[pltpu.SemaphoreType.DMA((n,)), pltpu.SemaphoreType.DMA((n,))],
        name=name)(*halves)


def _adamw(w, g, m, v, name, tr=None):
    rows, cols = w.shape
    tr = tr or rows
    assert rows % tr == 0

    def body(w_ref, g_ref, m_ref, v_ref, d_ref, nm_ref, nv_ref):
        gg = g_ref[...]
        nm = ADAM_B1 * m_ref[...] + (1.0 - ADAM_B1) * gg
        nv = ADAM_B2 * v_ref[...] + (1.0 - ADAM_B2) * (gg * gg)
        nm_ref[...] = nm
        nv_ref[...] = nv
        m_hat = nm / (1.0 - ADAM_B1 ** ADAM_STEP)
        v_hat = nv / (1.0 - ADAM_B2 ** ADAM_STEP)
        d_ref[...] = -ADAM_LR * (m_hat / (jnp.sqrt(v_hat) + ADAM_EPS) + ADAM_WD * w_ref[...])

    blk = pl.BlockSpec((tr, cols), lambda i: (i, 0))
    return pl.pallas_call(
        body, grid=(rows // tr,), in_specs=[blk] * 4, out_specs=[blk] * 3,
        out_shape=[SDS((rows, cols), F32)] * 3,
        compiler_params=_cparams(("parallel",)), name=name)(w, g, m, v)


def _adamw_halves(w, mine, theirs, m, v, c_arr, name):
    rows, cols = w.shape
    rh = rows // 2
    tr = rh if rh <= 352 else 256
    nh = rh // tr

    def body(c_ref, w_ref, mine_ref, theirs_ref, m_ref, v_ref, g_ref, d_ref, nm_ref, nv_ref):
        own = jnp.full((tr, cols), pl.program_id(0), jnp.int32) == c_ref[0]
        gg = jnp.where(own, mine_ref[...], theirs_ref[...])
        g_ref[...] = gg
        nm = ADAM_B1 * m_ref[...] + (1.0 - ADAM_B1) * gg
        nv = ADAM_B2 * v_ref[...] + (1.0 - ADAM_B2) * (gg * gg)
        nm_ref[...] = nm
        nv_ref[...] = nv
        m_hat = nm / (1.0 - ADAM_B1 ** ADAM_STEP)
        v_hat = nv / (1.0 - ADAM_B2 ** ADAM_STEP)
        d_ref[...] = -ADAM_LR * (m_hat / (jnp.sqrt(v_hat) + ADAM_EPS) + ADAM_WD * w_ref[...])

    whole = pl.BlockSpec((tr, cols), lambda hh, i, c_ref: (hh * nh + i, 0))
    part = pl.BlockSpec((tr, cols), lambda hh, i, c_ref: (i, 0))
    grid_spec = pltpu.PrefetchScalarGridSpec(
        num_scalar_prefetch=1, grid=(2, nh), in_specs=[whole, part, part, whole, whole], out_specs=[whole] * 4)
    return pl.pallas_call(body, grid_spec=grid_spec, out_shape=[SDS((rows, cols), F32)] * 4,
                          compiler_params=_cparams(("parallel", "parallel")), name=name)(c_arr, w, mine, theirs, m, v)


def _pack_small(pre_mix, post_mix, pre_ffn, post_ffn, rel_bias, b_forget, sinks):
    def at(row, v):
        return jnp.pad(v, ((row, 7 - row), (0, D_MODEL - v.shape[1])))
    return (at(0, pre_mix) + at(1, post_mix) + at(2, pre_ffn) + at(3, post_ffn)
            + at(4, rel_bias.reshape(1, N_BUCKETS * N_HEADS)) + at(5, jnp.concatenate([b_forget, sinks], axis=1)))


def _unpack_small(p):
    return dict(ln_pre_mix=p[0:1], ln_post_mix=p[1:2], ln_pre_ffn=p[2:3], ln_post_ffn=p[3:4],
                rel_bias=p[4, :N_BUCKETS * N_HEADS].reshape(N_BUCKETS, N_HEADS),
                b_forget=p[5:6, 0:N_HEADS], sinks=p[5:6, N_HEADS:2 * N_HEADS])


WEIGHTS = ("meta_tokens", "rel_bias", "ln_pre_mix", "ln_post_mix", "ln_pre_ffn", "ln_post_ffn",
           "w_in", "b_forget", "sinks", "w_out", "w_gate_up", "w_down")


def kernel(x, meta_tokens, rel_bias, ln_pre_mix, ln_post_mix, ln_pre_ffn, ln_post_ffn, w_in, b_forget, sinks, w_out, w_gate_up, w_down, loss_target, m_meta_tokens, m_rel_bias, m_ln_pre_mix, m_ln_post_mix, m_ln_pre_ffn, m_ln_post_ffn, m_w_in, m_b_forget, m_sinks, m_w_out, m_w_gate_up, m_w_down, v_meta_tokens, v_rel_bias, v_ln_pre_mix, v_ln_post_mix, v_ln_pre_ffn, v_ln_post_ffn, v_w_in, v_b_forget, v_sinks, v_w_out, v_w_gate_up, v_w_down):
    xi, yi, ci = lax.axis_index("x"), lax.axis_index("y"), lax.axis_index("c")
    chip = 2 * xi + yi
    c_arr = jnp.reshape(ci, (1,)).astype(jnp.int32)

    def halves(w, dtype):
        return w.astype(dtype).reshape(2, w.shape[0] // 2, w.shape[1])

    def with_own(gathered, shards):
        return [lax.dynamic_update_slice(got, own[None], (chip, 0, 0, 0)) for got, own in zip(gathered, shards)]

    shards = [halves(w_in[0], BF16), halves(meta_tokens, F32)]
    gw_in, g_meta = with_own(_run_alone(_gather_rider(shards, False), "gather_mixer_weights"), shards)
    out_shards = [halves(w_out[0], BF16)]
    ffn_shards = [halves(w_gate_up[0], BF16), halves(w_down[0], BF16)]

    def ffn_weights(carried):
        gw_gu, gw_dn = carried
        return gw_gu.reshape(D_MODEL, 2 * D_FF), gw_dn.reshape(D_FF, D_MODEL)

    early = {}

    def early_grads(dw_gu, dw_dn, dw_out):
        early["grads"] = [dw_out.reshape(4, 2, 128, D_MODEL), dw_gu.reshape(4, 2, 512, FF_T),
                          dw_dn.reshape(4, 2, 352, D_MODEL)]
        return _direct_rider(early["grads"])
    w_in_all = gw_in.reshape(4, D_MODEL, D_PROJ // 4).transpose(1, 0, 2).reshape(D_MODEL, D_PROJ)
    w_in_b = jnp.pad(w_in_all, ((0, 0), (0, D_PROJ_P - D_PROJ)))
    meta_all = g_meta.reshape(4, N_META, D_MODEL // 4).transpose(1, 0, 2).reshape(N_META, D_MODEL)

    loc = _local_step(x[0], loss_target[0], meta_all, rel_bias, ln_pre_mix, ln_post_mix, ln_pre_ffn, ln_post_ffn,
                      b_forget, sinks, w_in_b, _gather_rider(out_shards, True),
                      lambda carried: carried[0].reshape(D_MODEL, D_MODEL),
                      _gather_rider(ffn_shards, True, by_columns=(0,)), ffn_weights, early_grads)

    small = jnp.concatenate(
        [_pack_small(loc["ln_pre_mix"], loc["ln_post_mix"], loc["ln_pre_ffn"], loc["ln_post_ffn"],
                     loc["rel_bias"], loc["b_forget"], loc["sinks"])
         + jnp.pad(loc["loss"].reshape(1, 1), ((LOSS_ROW, 7 - LOSS_ROW), (0, D_MODEL - 1))), loc["meta"]], axis=0)

    dw_in = loc["w_in"].reshape(1, 2, D_MODEL // 2, D_PROJ_P)
    (got_in,) = _run_alone(_swap_rider([dw_in]), "swap_halves_late")
    half_sum = _pair_sum(dw_in, got_in, c_arr, "pair_sum_late")
    part_in = half_sum[0, :, :D_PROJ].reshape(D_MODEL // 2, 4, D_PROJ // 4).transpose(1, 0, 2)
    send_sems, recv_sems, part_sent, landing, small_sent, everyone, token = _late_exchange_start(part_in, small)
    chip_arr = jnp.reshape(chip, (1,)).astype(jnp.int32)
    own_arr = jnp.stack([chip, ci]).astype(jnp.int32)
    grad, delta, new_m, new_v = {}, {}, {}, {}

    def update(names, mine):
        theirs = _join_halves(mine, "join_" + names[0])
        big = dict(w_in=(w_in, m_w_in, v_w_in), w_out=(w_out, m_w_out, v_w_out),
                   w_gate_up=(w_gate_up, m_w_gate_up, v_w_gate_up), w_down=(w_down, m_w_down, v_w_down))
        for name, g_mine, g_theirs in zip(names, mine, theirs):
            w, m, v = big[name]
            g, d, nm, nv = _adamw_halves(w[0], g_mine, g_theirs, m[0], v[0], c_arr, "adamw_" + name)
            grad[name], delta[name], new_m[name], new_v[name] = g[None], d[None], nm[None], nv[None]

    update(("w_out", "w_gate_up", "w_down"),
           [_owner_sum(g, l, own_arr, token, "owner_sum_%d" % a)
            for a, (g, l) in enumerate(zip(early["grads"], loc["landed"]))])
    part_back, landed_in, small_all = _late_exchange_wait(send_sems, recv_sems, part_sent, landing, small_sent,
                                                         everyone, new_v["w_down"])
    mine_in = _chip_sum(part_back, landed_in, chip_arr, "chip_sum_in")
    (theirs_in,) = _join_halves([mine_in], "join_w_in")
    g_w_in = jnp.where(ci == 0, jnp.concatenate([mine_in, theirs_in], axis=0),
                       jnp.concatenate([theirs_in, mine_in], axis=0))
    view = lambda a: jnp.transpose(a).reshape(D_PROJ // 4 * 8, BLK)
    back = lambda a: jnp.transpose(a.reshape(D_PROJ // 4, D_MODEL))[None]
    d, nm, nv = _adamw(view(w_in[0]), view(g_w_in), view(m_w_in[0]), view(v_w_in[0]), "adamw_w_in",
                       tr=D_PROJ // 4 * 4)
    grad["w_in"], delta["w_in"], new_m["w_in"], new_v["w_in"] = g_w_in[None], back(d), back(nm), back(nv)
    me = 4 * xi + 2 * yi + ci
    small_sum = _device_sum(lax.dynamic_update_slice(small_all, small[None], (me, 0, 0)))
    g_meta_tokens = lax.dynamic_slice(small_sum[8:N_SMALL], (0, chip * (D_MODEL // 4)), (N_META, D_MODEL // 4))
    g_small = small_sum[0:8]
    grad.update(_unpack_small(g_small))
    grad.update(meta_tokens=g_meta_tokens)
    delta["meta_tokens"], new_m["meta_tokens"], new_v["meta_tokens"] = _adamw(
        meta_tokens, g_meta_tokens, m_meta_tokens, v_meta_tokens, "adamw_meta")
    d, nm, nv = _adamw(
        _pack_small(ln_pre_mix, ln_post_mix, ln_pre_ffn, ln_post_ffn, rel_bias, b_forget, sinks), g_small,
        _pack_small(m_ln_pre_mix, m_ln_post_mix, m_ln_pre_ffn, m_ln_post_ffn, m_rel_bias, m_b_forget, m_sinks),
        _pack_small(v_ln_pre_mix, v_ln_post_mix, v_ln_pre_ffn, v_ln_post_ffn, v_rel_bias, v_b_forget, v_sinks),
        "adamw_small")
    delta.update(_unpack_small(d))
    new_m.update(_unpack_small(nm))
    new_v.update(_unpack_small(nv))

    loss = small_sum[LOSS_ROW, 0]
    return (loss,loc["grad_x"][None], *[grad[k] for k in WEIGHTS], *[delta[k] for k in WEIGHTS],
            *[new_m[k] for k in WEIGHTS], *[new_v[k] for k in WEIGHTS])
```

```python
import math

import numpy as np
import jax
import jax.numpy as jnp
from jax import lax
from jax.experimental import pallas as pl
from jax.experimental.pallas import tpu as pltpu

F32 = jnp.float32
BF16 = jnp.bfloat16
MESH = pl.DeviceIdType.MESH
SDS = jax.ShapeDtypeStruct

D_MODEL = 1024
SEQ = 4096
N_META = 16
N_HEADS = 8
HALF = 64
D_FF = 2816
N_BUCKETS = 32
EPS = 1e-6
NEG = -1e30
SCALE = 0.125
LOG2E = 1.4426950408889634
LN2 = 0.6931471805599453
PAD_ROWS = 112
ROW0 = PAD_ROWS + N_META
LP = ROW0 + SEQ
BLK = 128
NBLK = LP // BLK
TM = 384
NT = LP // TM
TM_PURE = LP // 2
TM_MID = LP // 4
TM_EPI = LP // 6
TN = 256
TK_W = LP // 2
D_PROJ = 2312
D_PROJ_P = 2432
D_QKV = 2304
FF_T = 1408
VMEM_LIMIT = 56 * 1024 * 1024

ADAM_LR = 0.001
ADAM_B1 = 0.9
ADAM_B2 = 0.999
ADAM_EPS = 1e-08
ADAM_WD = 0.01
ADAM_STEP = 10

QA = 0
KA, VA = 4, 5
QB, KB, VB = 3, 5, 7
W2 = 256

NT_DIMS = (((1,), (1,)), ((), ()))
TN_DIMS = (((0,), (0,)), ((), ()))


def _cparams(sem):
    return pltpu.CompilerParams(dimension_semantics=sem, vmem_limit_bytes=VMEM_LIMIT)


def _t5_bucket_np(d):
    n = np.maximum(d, 0).astype(np.int32)
    nf = np.maximum(n, 1).astype(np.float32)
    large = 16 + (np.log(nf / np.float32(16)) / np.float32(math.log(8.0)) * np.float32(16)).astype(np.int32)
    large = np.minimum(large, N_BUCKETS - 1)
    return np.where(n < 16, n, large).astype(np.int32)


def _bucket_tables():
    qi = np.arange(BLK)[:, None]
    ki = np.arange(BLK)[None, :]
    return np.stack([_t5_bucket_np(qi - ki), _t5_bucket_np(qi - ki + BLK)])


def _rms(x):
    return lax.rsqrt(jnp.mean(x * x, axis=-1, keepdims=True) + EPS)


def _rms_bwd(n, r, gdy):
    return r * (gdy - n * jnp.mean(n * gdy, axis=-1, keepdims=True))


def _pre_mix(h0, gain, w_in_b):
    half = D_QKV // 2

    def body(h_ref, g_ref, w_ref, hn_ref, proj_ref, f_ref):
        x = h_ref[...]
        hn = (x * _rms(x) * g_ref[...]).astype(BF16)
        hn_ref[...] = hn
        proj_ref[:, :half] = jnp.dot(hn, w_ref[:, :half], preferred_element_type=F32).astype(BF16)
        p = jnp.dot(hn, w_ref[:, half:], preferred_element_type=F32)
        proj_ref[:, half:] = p[:, :half].astype(BF16)
        f_ref[...] = p[:, half:]

    return pl.pallas_call(
        body, grid=(LP // TM_MID,),
        in_specs=[pl.BlockSpec((TM_MID, D_MODEL), lambda i: (i, 0)),
                  pl.BlockSpec((1, D_MODEL), lambda i: (0, 0)),
                  pl.BlockSpec((D_MODEL, D_PROJ_P), lambda i: (0, 0))],
        out_specs=[pl.BlockSpec((TM_MID, D_MODEL), lambda i: (i, 0)),
                   pl.BlockSpec((TM_MID, D_QKV), lambda i: (i, 0)),
                   pl.BlockSpec((TM_MID, BLK), lambda i: (i, 0))],
        out_shape=[SDS((LP, D_MODEL), BF16), SDS((LP, D_QKV), BF16), SDS((LP, BLK), F32)],
        compiler_params=_cparams(("parallel",)), name="pre_mix")(h0, gain, w_in_b)


def _attn_out(o_a, o_b, w_out_b, h0, g_post, g_pre_ffn):
    def body(oa_ref, ob_ref, w_ref, h0_ref, gp_ref, gf_ref, a_ref, h1_ref, hn2_ref):
        a = (jnp.dot(oa_ref[...], w_ref[0:512, :], preferred_element_type=F32)
             + jnp.dot(ob_ref[...], w_ref[512:1024, :], preferred_element_type=F32))
        a_ref[...] = a
        h1 = h0_ref[...] + a * _rms(a) * gp_ref[...]
        h1_ref[...] = h1
        hn2_ref[...] = (h1 * _rms(h1) * gf_ref[...]).astype(BF16)

    row = lambda w: pl.BlockSpec((TM_EPI, w), lambda i: (i, 0))
    vec = pl.BlockSpec((1, D_MODEL), lambda i: (0, 0))
    return pl.pallas_call(
        body, grid=(LP // TM_EPI,),
        in_specs=[row(512), row(512), pl.BlockSpec((D_MODEL, D_MODEL), lambda i: (0, 0)), row(D_MODEL), vec, vec],
        out_specs=[row(D_MODEL), row(D_MODEL), row(D_MODEL)],
        out_shape=[SDS((LP, D_MODEL), F32), SDS((LP, D_MODEL), F32), SDS((LP, D_MODEL), BF16)],
        compiler_params=_cparams(("parallel",)), name="attn_out")(o_a, o_b, w_out_b, h0, g_post, g_pre_ffn)


def _ffn_up(hn2, w_gu_b):
    def body(x_ref, wg_ref, wu_ref, g_ref, u_ref, act_ref):
        x = x_ref[...]
        g = jnp.dot(x, wg_ref[...], preferred_element_type=F32)
        u = jnp.dot(x, wu_ref[...], preferred_element_type=F32)
        g_ref[...] = g.astype(BF16)
        u_ref[...] = u.astype(BF16)
        act_ref[...] = (g * (1.0 / (1.0 + jnp.exp(-g))) * u).astype(BF16)

    out = pl.BlockSpec((LP, TN), lambda j: (0, j))
    return pl.pallas_call(
        body, grid=(D_FF // TN,),
        in_specs=[pl.BlockSpec((LP, D_MODEL), lambda j: (0, 0)),
                  pl.BlockSpec((D_MODEL, TN), lambda j: (0, j)),
                  pl.BlockSpec((D_MODEL, TN), lambda j: (0, j + D_FF // TN))],
        out_specs=[out, out, out],
        out_shape=[SDS((LP, D_FF), BF16)] * 3,
        compiler_params=_cparams(("parallel",)), name="ffn_up")(hn2, w_gu_b, w_gu_b)


def _ffn_down_loss(act, w_dn_b, h1, tgt, g_post_ffn):
    def body(act_ref, w_ref, h1_ref, t0_ref, t1_ref, t2_ref, g_ref, dff_ref, dy_ref, loss_ref, dg_ref):
        i = pl.program_id(0)
        target = jnp.concatenate([t0_ref[...], t1_ref[...], t2_ref[...]], axis=0)

        @pl.when(i == 0)
        def _():
            loss_ref[...] = jnp.zeros_like(loss_ref)
            dg_ref[...] = jnp.zeros_like(dg_ref)

        ff = jnp.dot(act_ref[...], w_ref[...], preferred_element_type=F32)
        r = _rms(ff)
        n = ff * r
        g = g_ref[...]
        y = h1_ref[...] + n * g
        rows = i * TM + lax.broadcasted_iota(jnp.int32, (TM, D_MODEL), 0)
        diff = jnp.where(rows >= ROW0, y - target, 0.0)
        loss_ref[...] += 0.5 * jnp.sum(diff * diff) / D_MODEL
        dy = diff / D_MODEL
        dy_ref[...] = dy
        dg_ref[...] += jnp.sum(dy * n, axis=0, keepdims=True)
        dff_ref[...] = _rms_bwd(n, r, g * dy).astype(BF16)

    row = pl.BlockSpec((TM, D_MODEL), lambda i: (i, 0))
    tblk = lambda j: pl.BlockSpec((BLK, D_MODEL), lambda i: (jnp.maximum(3 * i - 1 + j, 0), 0))
    return pl.pallas_call(
        body, grid=(NT,),
        in_specs=[pl.BlockSpec((TM, D_FF), lambda i: (i, 0)), pl.BlockSpec((D_FF, D_MODEL), lambda i: (0, 0)),
                  row, tblk(0), tblk(1), tblk(2), pl.BlockSpec((1, D_MODEL), lambda i: (0, 0))],
        out_specs=[row, row, pl.BlockSpec((8, BLK), lambda i: (0, 0)), pl.BlockSpec((1, D_MODEL), lambda i: (0, 0))],
        out_shape=[SDS((LP, D_MODEL), BF16), SDS((LP, D_MODEL), F32), SDS((8, BLK), F32), SDS((1, D_MODEL), F32)],
        compiler_params=_cparams(("arbitrary",)), name="ffn_down_loss")(act, w_dn_b, h1, tgt, tgt, tgt, g_post_ffn)


def _ffn_down_bwd(dff, w_dn_b, g, u):
    def body(d_ref, w_ref, g_ref, u_ref, dg_ref, du_ref):
        dact = lax.dot_general(d_ref[...], w_ref[...], NT_DIMS, preferred_element_type=F32)
        gg = g_ref[...].astype(F32)
        sig = 1.0 / (1.0 + jnp.exp(-gg))
        dg_ref[...] = (dact * u_ref[...].astype(F32) * sig * (1.0 + gg * (1.0 - sig))).astype(BF16)
        du_ref[...] = (dact * gg * sig).astype(BF16)

    blk = pl.BlockSpec((LP, TN), lambda j: (0, j))
    return pl.pallas_call(
        body, grid=(D_FF // TN,),
        in_specs=[pl.BlockSpec((LP, D_MODEL), lambda j: (0, 0)),
                  pl.BlockSpec((TN, D_MODEL), lambda j: (j, 0)), blk, blk],
        out_specs=[blk, blk],
        out_shape=[SDS((LP, D_FF), BF16)] * 2,
        compiler_params=_cparams(("parallel",)), name="ffn_down_bwd")(dff, w_dn_b, g, u)


def _ffn_up_bwd(dg, du, w_gu_b, h1, a, dy, g_pre_ffn, g_post_mix):
    def body(dg_ref, du_ref, w_ref, h1_ref, a_ref, dy_ref, gf_ref, gp_ref,
             dh1_ref, da_ref, dgf_ref, dgp_ref, acc):
        i = pl.program_id(0)
        s = pl.program_id(1)

        @pl.when((i == 0) & (s == 0))
        def _():
            dgf_ref[...] = jnp.zeros_like(dgf_ref)
            dgp_ref[...] = jnp.zeros_like(dgp_ref)

        @pl.when(s == 0)
        def _():
            acc[...] = jnp.zeros_like(acc)

        @pl.when(s < 2)
        def _():
            acc[...] += lax.dot_general(dg_ref[...], w_ref[...], NT_DIMS, preferred_element_type=F32)

        @pl.when(s >= 2)
        def _():
            acc[...] += lax.dot_general(du_ref[...], w_ref[...], NT_DIMS, preferred_element_type=F32)

        @pl.when(s == 3)
        def _():
            dhn2 = acc[...]
            h1 = h1_ref[...]
            r2 = _rms(h1)
            n2 = h1 * r2
            dgf_ref[...] += jnp.sum(dhn2 * n2, axis=0, keepdims=True)
            dh1 = dy_ref[...] + _rms_bwd(n2, r2, gf_ref[...] * dhn2)
            dh1_ref[...] = dh1
            av = a_ref[...]
            ra = _rms(av)
            na = av * ra
            dgp_ref[...] += jnp.sum(dh1 * na, axis=0, keepdims=True)
            da_ref[...] = _rms_bwd(na, ra, gp_ref[...] * dh1).astype(BF16)

    row = pl.BlockSpec((TM_EPI, D_MODEL), lambda i, s: (i, 0))
    vec = pl.BlockSpec((1, D_MODEL), lambda i, s: (0, 0))
    return pl.pallas_call(
        body, grid=(LP // TM_EPI, 4),
        in_specs=[pl.BlockSpec((TM_EPI, FF_T), lambda i, s: (i, jnp.minimum(s, 1))),
                  pl.BlockSpec((TM_EPI, FF_T), lambda i, s: (i, jnp.maximum(s - 2, 0))),
                  pl.BlockSpec((D_MODEL, FF_T), lambda i, s: (0, s)),
                  row, row, row, vec, vec],
        out_specs=[row, row, vec, vec],
        out_shape=[SDS((LP, D_MODEL), F32), SDS((LP, D_MODEL), BF16), SDS((1, D_MODEL), F32), SDS((1, D_MODEL), F32)],
        scratch_shapes=[pltpu.VMEM((TM_EPI, D_MODEL), F32)],
        compiler_params=_cparams(("arbitrary", "arbitrary")), name="ffn_up_bwd",
    )(dg, du, w_gu_b, h1, a, dy, g_pre_ffn, g_post_mix)


def _attn_out_bwd(da, w_out_b):
    def body(d_ref, w_ref, o_ref):
        o_ref[...] = lax.dot_general(d_ref[...], w_ref[...], NT_DIMS, preferred_element_type=F32).astype(BF16)

    row = pl.BlockSpec((TM_PURE, D_MODEL), lambda i: (i, 0))
    return pl.pallas_call(
        body, grid=(LP // TM_PURE,),
        in_specs=[row, pl.BlockSpec((D_MODEL, D_MODEL), lambda i: (0, 0))],
        out_specs=row, out_shape=SDS((LP, D_MODEL), BF16),
        compiler_params=_cparams(("parallel",)), name="attn_out_bwd")(da, w_out_b)


def _pre_mix_bwd(dq_a, dq_b, dk_b, dv_b, dk_a, dv_a, df, w_in_b, h0, dh1, g_pre_mix):
    n_steps = LP // TM_EPI

    def body(qa_ref, qb_ref, kb_ref, vb_ref, ka_ref, va_ref, f_ref, w_ref, h0_ref, dh1_ref, g_ref,
             dproj_ref, dx_ref, dmeta_ref, dg_ref, tile, sems):
        i = pl.program_id(0)

        def tile_copy(s):
            skip = ROW0 if s == 0 else 0
            return pltpu.make_async_copy(tile.at[s % 2, skip:TM_EPI],
                                         dx_ref.at[s * TM_EPI + skip - ROW0:(s + 1) * TM_EPI - ROW0],
                                         sems.at[s % 2])

        @pl.when(i == 0)
        def _():
            dg_ref[...] = jnp.zeros_like(dg_ref)

        for s in range(2, n_steps):
            @pl.when(i == s)
            def _():
                tile_copy(s - 2).wait()

        dproj = jnp.concatenate(
            [qa_ref[...], ka_ref[...].astype(BF16), va_ref[...].astype(BF16), (qb_ref[...] * SCALE).astype(BF16),
             kb_ref[...], vb_ref[...], f_ref[...].astype(BF16)], axis=1)
        dproj_ref[...] = dproj
        dhn = lax.dot_general(dproj, w_ref[...], NT_DIMS, preferred_element_type=F32)
        x = h0_ref[...]
        r = _rms(x)
        n = x * r
        dg_ref[...] += jnp.sum(dhn * n, axis=0, keepdims=True)
        tile[i % 2] = dh1_ref[...] + _rms_bwd(n, r, g_ref[...] * dhn)

        for s in range(n_steps):
            @pl.when(i == s)
            def _():
                tile_copy(s).start()
                if s == 0:
                    dmeta_ref[...] = tile[0, PAD_ROWS:ROW0]
                if s == n_steps - 1:
                    tile_copy(s - 1).wait()
                    tile_copy(s).wait()

    row = lambda w: pl.BlockSpec((TM_EPI, w), lambda i: (i, 0))
    vec = pl.BlockSpec((1, D_MODEL), lambda i: (0, 0))
    return pl.pallas_call(
        body, grid=(n_steps,),
        in_specs=[row(512), row(512), row(512), row(512), row(BLK), row(BLK), row(BLK),
                  pl.BlockSpec((D_MODEL, D_PROJ_P), lambda i: (0, 0)), row(D_MODEL), row(D_MODEL), vec],
        out_specs=[row(D_PROJ_P), HBM_SPEC, pl.BlockSpec((N_META, D_MODEL), lambda i: (0, 0)), vec],
        out_shape=[SDS((LP, D_PROJ_P), BF16), SDS((SEQ, D_MODEL), F32), SDS((N_META, D_MODEL), F32),
                   SDS((1, D_MODEL), F32)],
        scratch_shapes=[pltpu.VMEM((2, TM_EPI, D_MODEL), F32), pltpu.SemaphoreType.DMA((2,))],
        compiler_params=_cparams(("arbitrary",)), name="pre_mix_bwd",
    )(dq_a, dq_b, dk_b, dv_b, dk_a, dv_a, df, w_in_b, h0, dh1, g_pre_mix)


def _mm_tn(parts, b, tm, name, out_dtype=F32):
    widths = [p.shape[1] for p in parts]
    m_total = sum(widths)
    n = b.shape[1]
    whole = len(parts) > 1
    n_k = LP // TK_W
    assert (tm == m_total) if whole else (m_total % tm == 0)

    def body(*refs):
        a_refs, b_ref, o_ref, acc = refs[:-3], refs[-3], refs[-2], refs[-1]
        k = pl.program_id(1)

        @pl.when(k == 0)
        def _():
            acc[...] = jnp.zeros_like(acc)
        a = a_refs[0][...] if not whole else jnp.concatenate([r[...] for r in a_refs], axis=1)
        acc[...] += lax.dot_general(a, b_ref[...], TN_DIMS, preferred_element_type=F32)

        @pl.when(k == n_k - 1)
        def _():
            o_ref[...] = acc[...].astype(out_dtype)

    a_specs = ([pl.BlockSpec((TK_W, w), lambda mi, k: (k, 0)) for w in widths] if whole
               else [pl.BlockSpec((TK_W, tm), lambda mi, k: (k, mi))])
    return pl.pallas_call(
        body, grid=(m_total // tm, n_k),
        in_specs=a_specs + [pl.BlockSpec((TK_W, n), lambda mi, k: (k, 0))],
        out_specs=pl.BlockSpec((tm, n), lambda mi, k: (mi, 0)),
        out_shape=SDS((m_total, n), out_dtype),
        scratch_shapes=[pltpu.VMEM((tm, n), F32)],
        compiler_params=_cparams(("parallel", "arbitrary")), name=name)(*parts, b)


def _dw_gate_up(hn2, dg, du):
    n_k = LP // TK_W

    def body(a_ref, dg_ref, du_ref, o_ref, acc):
        s = pl.program_id(0)
        k = pl.program_id(1)

        @pl.when(k == 0)
        def _():
            acc[...] = jnp.zeros_like(acc)

        @pl.when(s < 2)
        def _():
            acc[...] += lax.dot_general(a_ref[...], dg_ref[...], TN_DIMS, preferred_element_type=F32)

        @pl.when(s >= 2)
        def _():
            acc[...] += lax.dot_general(a_ref[...], du_ref[...], TN_DIMS, preferred_element_type=F32)

        @pl.when(k == n_k - 1)
        def _():
            o_ref[0] = acc[...].astype(BF16)

    return pl.pallas_call(
        body, grid=(4, n_k),
        in_specs=[pl.BlockSpec((TK_W, D_MODEL), lambda s, k: (k, 0)),
                  pl.BlockSpec((TK_W, FF_T), lambda s, k: (k, jnp.minimum(s, 1))),
                  pl.BlockSpec((TK_W, FF_T), lambda s, k: (k, jnp.maximum(s - 2, 0)))],
        out_specs=pl.BlockSpec((1, D_MODEL, FF_T), lambda s, k: (s, 0, 0)),
        out_shape=SDS((4, D_MODEL, FF_T), BF16),
        scratch_shapes=[pltpu.VMEM((D_MODEL, FF_T), F32)],
        compiler_params=_cparams(("parallel", "arbitrary")), name="dw_gate_up")(hn2, dg, du)


def _split3(x):
    hi = x.astype(BF16)
    r1 = x - hi.astype(F32)
    mid = r1.astype(BF16)
    lo = (r1 - mid.astype(F32)).astype(BF16)
    return hi, mid, lo


def _tri_matmul(tri, x):
    hi, mid, lo = _split3(x)
    dot = lambda t: jnp.dot(tri, t, preferred_element_type=F32)
    return dot(hi) + dot(mid) + dot(lo)


def _forget_cumsum(f, b_forget_p):
    def body(f_ref, b_ref, cum_ref, carry):
        i = pl.program_id(0)

        @pl.when(i == 0)
        def _():
            carry[...] = jnp.zeros_like(carry)

        z = f_ref[...] + b_ref[...]
        ls = jnp.minimum(z, 0.0) - jnp.log(1.0 + jnp.exp(-jnp.abs(z)))
        rows = i * TM + lax.broadcasted_iota(jnp.int32, (TM, BLK), 0)
        ls = jnp.where(rows >= PAD_ROWS, ls, 0.0)
        r = lax.broadcasted_iota(jnp.int32, (TM, TM), 0)
        c = lax.broadcasted_iota(jnp.int32, (TM, TM), 1)
        tri = (c <= r).astype(BF16)
        cum = _tri_matmul(tri, ls) + carry[...]
        cum_ref[...] = cum
        carry[...] = cum[TM - 1:TM, :]

    return pl.pallas_call(
        body, grid=(NT,),
        in_specs=[pl.BlockSpec((TM, BLK), lambda i: (i, 0)), pl.BlockSpec((1, BLK), lambda i: (0, 0))],
        out_specs=pl.BlockSpec((TM, BLK), lambda i: (i, 0)),
        out_shape=SDS((LP, BLK), F32),
        scratch_shapes=[pltpu.VMEM((1, BLK), F32)],
        compiler_params=_cparams(("arbitrary",)), name="forget_cumsum")(f, b_forget_p)


def _forget_cumsum_bwd(dcum, f, b_forget_p):
    def body(d_ref, f_ref, b_ref, df_ref, db_ref, carry):
        i = pl.program_id(0)

        @pl.when(i == 0)
        def _():
            carry[...] = jnp.zeros_like(carry)
            db_ref[...] = jnp.zeros_like(db_ref)

        blk = NT - 1 - i
        r = lax.broadcasted_iota(jnp.int32, (TM, TM), 0)
        c = lax.broadcasted_iota(jnp.int32, (TM, TM), 1)
        tri = (c >= r).astype(BF16)
        d = d_ref[...]
        dls = _tri_matmul(tri, d) + carry[...]
        carry[...] = dls[0:1, :]
        z = f_ref[...] + b_ref[...]
        rows = blk * TM + lax.broadcasted_iota(jnp.int32, (TM, BLK), 0)
        df = jnp.where(rows >= PAD_ROWS, dls / (1.0 + jnp.exp(z)), 0.0)
        df_ref[...] = df
        db_ref[...] += jnp.sum(df, axis=0, keepdims=True)

    rev = pl.BlockSpec((TM, BLK), lambda i: (NT - 1 - i, 0))
    vec = pl.BlockSpec((1, BLK), lambda i: (0, 0))
    return pl.pallas_call(
        body, grid=(NT,),
        in_specs=[rev, rev, vec],
        out_specs=[rev, vec],
        out_shape=[SDS((LP, BLK), F32), SDS((1, BLK), F32)],
        scratch_shapes=[pltpu.VMEM((1, BLK), F32)],
        compiler_params=_cparams(("arbitrary",)), name="forget_cumsum_bwd")(dcum, f, b_forget_p)


def _lane_half(rows):
    return lax.broadcasted_iota(jnp.int32, (rows, BLK), 1) // HALF


def _fox_valid(qi, kj):
    qrow = qi * TM + lax.broadcasted_iota(jnp.int32, (TM, TM), 0)
    krow = kj * TM + lax.broadcasted_iota(jnp.int32, (TM, TM), 1)
    return (krow <= qrow) & ((krow >= PAD_ROWS) | (qrow < PAD_ROWS))


class _Rider:
    def __init__(self, operands, out_shapes, sem_counts, first, middle, last):
        self.operands, self.out_shapes, self.sem_counts = list(operands), list(out_shapes), list(sem_counts)
        self.first, self.middle, self.last = first, middle, last

    def scratch(self):
        return [pltpu.SemaphoreType.DMA((k,)) for k in self.sem_counts]

    def split(self, refs, n_in, n_out, n_scratch):
        a, b = len(self.operands), len(self.out_shapes)
        ins, mine_in = refs[:n_in], refs[n_in:n_in + a]
        outs, mine_out = refs[n_in + a:n_in + a + n_out], refs[n_in + a + n_out:n_in + a + n_out + b]
        rest = refs[n_in + a + n_out + b:]
        return ins, outs, rest[:n_scratch], (mine_in, mine_out, rest[n_scratch:])

    def at_steps(self, mine, is_first, is_middle, is_last):
        for cond, fn in ((is_first, self.first), (is_middle, self.middle), (is_last, self.last)):
            pl.when(cond)(lambda fn=fn: fn(*mine))


HBM_SPEC = pl.BlockSpec(memory_space=pltpu.HBM)


N_AUG = 3
QCHUNKS = ((0, 128), (128, 128), (256, 128))
KSUB = 384
AHEAD = 5
AHEAD_BWD = 1


def _fox_prep(proj, cum):
    def body(q0_ref, q1_ref, k0_ref, k1_ref, v0_ref, v1_ref, c_ref, qa_ref, ka_ref, vt_ref):
        half = _lane_half(TM)
        lane = lax.broadcasted_iota(jnp.int32, (TM, BLK), 1)
        for pp in range(4):
            cols = slice(pp * BLK, (pp + 1) * BLK)
            q_ref, k_ref, v_ref = ((q0_ref, k0_ref, v0_ref), (q1_ref, k1_ref, v1_ref))[pp // 2]
            part = slice((pp % 2) * BLK, (pp % 2 + 1) * BLK)
            qs = q_ref[:, part].astype(F32) * (SCALE * LOG2E)
            kp = k_ref[:, part].astype(F32)
            vp = v_ref[:, part]
            vt_ref[cols, :] = vp.astype(F32).T.astype(BF16)
            for e in range(2):
                h = 2 * pp + e
                a = (1 - e) * HALF
                blk = slice(h * BLK, (h + 1) * BLK)
                hi, mid, lo = _split3(-LOG2E * c_ref[:, h:h + 1])
                q_aug = jnp.where(half == e, qs, jnp.where((lane >= a) & (lane < a + N_AUG), 1.0, 0.0))
                k_aug = jnp.where(half == e, kp, jnp.where(
                    lane == a, hi.astype(F32), jnp.where(lane == a + 1, mid.astype(F32), jnp.where(
                        lane == a + 2, lo.astype(F32), 0.0))))
                qa_ref[blk, :] = q_aug.T.astype(BF16)
                ka_ref[:, blk] = k_aug.astype(BF16)

    row = lambda blk: pl.BlockSpec((TM, W2), lambda i: (i, blk))
    wide = pl.BlockSpec((TM, 1024), lambda i: (i, 0))
    return pl.pallas_call(
        body, grid=(NT,),
        in_specs=[row(QB), row(QB + 1), row(KB), row(KB + 1), row(VB), row(VB + 1),
                  pl.BlockSpec((TM, BLK), lambda i: (i, 0))],
        out_specs=[pl.BlockSpec((1024, TM), lambda i: (0, i)), wide, pl.BlockSpec((512, TM), lambda i: (0, i))],
        out_shape=[SDS((1024, LP), BF16), SDS((LP, 1024), BF16), SDS((512, LP), BF16)],
        compiler_params=_cparams(("parallel",)), name="fox_prep")(proj, proj, proj, proj, proj, proj, cum)


def _over_keys(reduce, x):
    slabs = x.reshape(x.shape[0] // HALF, HALF, x.shape[1])
    return reduce(reduce(slabs, axis=0), axis=0, keepdims=True)


def _fox_valid_t(qi, kj, c, r):
    krow = kj * TM + r * KSUB + lax.broadcasted_iota(jnp.int32, (KSUB, c[1]), 0)
    qrow = qi * TM + c[0] + lax.broadcasted_iota(jnp.int32, (KSUB, c[1]), 1)
    return (krow <= qrow) & ((krow >= PAD_ROWS) | (qrow < PAD_ROWS))


def _fox_fwd(q_aug, k_aug, v_t, rider):
    pairs = [(qi, kj) for qi in range(NT) for kj in range(qi + 1)]
    n_pairs = len(pairs)

    def body(qi_ref, kj_ref, *refs):
        (q_ref, k_ref, vt_ref), (o_ref, lse_ref), (m_s, l_s, acc_s), mine = rider.split(refs, 3, 2, 3)
        n = pl.program_id(0)
        qi = qi_ref[n]
        kj = kj_ref[n]
        rider.at_steps(mine, n == 0, n == n_pairs // 2, n == n_pairs - 1)

        @pl.when(kj == 0)
        def _():
            m_s[...] = jnp.full_like(m_s, NEG)
            l_s[...] = jnp.zeros_like(l_s)
            acc_s[...] = jnp.zeros_like(acc_s)

        def tile(masked):
            steps = [(h, c, r) for h in range(N_HEADS) for c in QCHUNKS for r in range(TM // KSUB)]

            def scores(h, c, r):
                blk = slice(h * BLK, (h + 1) * BLK)
                return jnp.dot(k_ref[r * KSUB:(r + 1) * KSUB, blk], q_ref[blk, c[0]:c[0] + c[1]],
                               preferred_element_type=F32)

            ahead = [scores(*st) for st in steps[:AHEAD]]
            for n, (h, c, r) in enumerate(steps):
                s_t = ahead.pop(0)
                if n + AHEAD < len(steps):
                    ahead.append(scores(*steps[n + AHEAD]))
                cs = slice(c[0], c[0] + c[1])
                if masked:
                    s_t = jnp.where(_fox_valid_t(qi, kj, c, r), s_t, NEG)
                m_prev = m_s[h, :, cs]
                m_new = jnp.maximum(m_prev, _over_keys(jnp.max, s_t))
                p_t = jnp.exp2(s_t - m_new)
                alpha = jnp.exp2(m_prev - m_new)
                l_s[h, :, cs] = alpha * l_s[h, :, cs] + _over_keys(jnp.sum, p_t)
                m_s[h, :, cs] = m_new
                vt = vt_ref[h * HALF:(h + 1) * HALF, r * KSUB:(r + 1) * KSUB]
                acc_s[h, :, cs] = acc_s[h, :, cs] * alpha + jnp.dot(vt, p_t.astype(BF16),
                                                                    preferred_element_type=F32)

        @pl.when((kj < qi) & (kj > 0))
        def _():
            tile(False)

        @pl.when((kj == qi) | (kj == 0))
        def _():
            tile(True)

        @pl.when(kj == qi)
        def _():
            for pp in range(4):
                both = jnp.concatenate([acc_s[2 * pp] * (1.0 / l_s[2 * pp]),
                                        acc_s[2 * pp + 1] * (1.0 / l_s[2 * pp + 1])], axis=0)
                o_ref[:, pp * BLK:(pp + 1) * BLK] = both.T.astype(BF16)
            for h in range(N_HEADS):
                lse_ref[h] = m_s[h] * LN2 + jnp.log(l_s[h])

    grid_spec = pltpu.PrefetchScalarGridSpec(
        num_scalar_prefetch=2, grid=(n_pairs,),
        in_specs=[pl.BlockSpec((1024, TM), lambda n, qi, kj: (0, qi[n])),
                  pl.BlockSpec((TM, 1024), lambda n, qi, kj: (kj[n], 0)),
                  pl.BlockSpec((512, TM), lambda n, qi, kj: (0, kj[n]))] + [HBM_SPEC] * len(rider.operands),
        out_specs=[pl.BlockSpec((TM, 512), lambda n, qi, kj: (qi[n], 0)),
                   pl.BlockSpec((N_HEADS, 1, TM), lambda n, qi, kj: (0, 0, qi[n]))]
        + [HBM_SPEC] * len(rider.out_shapes),
        scratch_shapes=[pltpu.VMEM((N_HEADS, 1, TM), F32), pltpu.VMEM((N_HEADS, 1, TM), F32),
                        pltpu.VMEM((N_HEADS, HALF, TM), F32)] + rider.scratch())
    o_b, lse, *carried = pl.pallas_call(
        body, grid_spec=grid_spec,
        out_shape=[SDS((LP, 512), BF16), SDS((N_HEADS, 1, LP), F32)] + rider.out_shapes,
        compiler_params=_cparams(("arbitrary",)), name="fox_fwd",
    )(jnp.asarray([p[0] for p in pairs], jnp.int32), jnp.asarray([p[1] for p in pairs], jnp.int32),
      q_aug, k_aug, v_t, *rider.operands)
    return o_b, lse, carried


def _fox_bwd(proj, o_b, dmix, lse, ck_t, rider):
    pairs = [(kj, qi) for kj in range(NT) for qi in range(kj, NT)]
    n_pairs = len(pairs)

    def body(kj_ref, qi_ref, *refs):
        ((q0_ref, q1_ref, k0_ref, k1_ref, v0_ref, v1_ref, o_ref, do_ref, lse_ref, ck_ref),
         (dq_ref, dk_ref, dv_ref, dck_ref, dcq_ref), (dk_s, dv_s, dck_s), mine) = rider.split(refs, 10, 5, 3)
        n = pl.program_id(0)
        kj = kj_ref[n]
        qi = qi_ref[n]
        rider.at_steps(mine, n == 0, n == n_pairs // 2, n == n_pairs - 1)

        @pl.when(n == 0)
        def _():
            dq_ref[...] = jnp.zeros_like(dq_ref)
            dcq_ref[...] = jnp.zeros_like(dcq_ref)

        @pl.when(qi == kj)
        def _():
            dk_s[...] = jnp.zeros_like(dk_s)
            dv_s[...] = jnp.zeros_like(dv_s)
            dck_s[...] = jnp.zeros_like(dck_s)

        def tile(masked):
            valid = _fox_valid(qi, kj) if masked else None
            half = _lane_half(TM)
            q0 = pl.multiple_of(qi * TM, TM)
            lane = lax.broadcasted_iota(jnp.int32, (TM, BLK), 1)
            row_sums = jnp.zeros((TM, BLK), F32)
            pair_ops = {}

            def operands(pp):
                if pp not in pair_ops:
                    cols = slice(pp * BLK, (pp + 1) * BLK)
                    q_ref, k_ref, v_ref = ((q0_ref, k0_ref, v0_ref), (q1_ref, k1_ref, v1_ref))[pp // 2]
                    part = slice((pp % 2) * BLK, (pp % 2 + 1) * BLK)
                    pair_ops[pp] = ((q_ref[:, part].astype(F32) * SCALE).astype(BF16), k_ref[:, part],
                                    v_ref[:, part], do_ref[:, cols])
                return pair_ops[pp]

            def scores(pp, e):
                qs, kp, vp, dop = operands(pp)
                ke = jnp.where(half == e, kp, jnp.zeros_like(kp))
                ve = jnp.where(half == e, vp, jnp.zeros_like(vp))
                return (lax.dot_general(qs, ke, NT_DIMS, preferred_element_type=F32),
                        lax.dot_general(dop, ve, NT_DIMS, preferred_element_type=F32), ke)

            steps = [(pp, e) for pp in range(4) for e in range(2)]
            ahead = [scores(*st) for st in steps[:AHEAD_BWD]]
            for n, (pp, e) in enumerate(steps):
                raw, dp, ke = ahead.pop(0)
                if n + AHEAD_BWD < len(steps):
                    ahead.append(scores(*steps[n + AHEAD_BWD]))
                h = 2 * pp + e
                cols = slice(pp * BLK, (pp + 1) * BLK)
                qs, kp, vp, dop = operands(pp)
                if e == 0:
                    prod = dop.astype(F32) * o_ref[:, cols].astype(F32)
                    d0 = jnp.sum(jnp.where(half == 0, prod, 0.0), axis=1, keepdims=True)
                    d1 = jnp.sum(prod, axis=1, keepdims=True) - d0
                    dq = jnp.zeros((TM, BLK), F32)
                    dks, dvs = [], []
                t = raw - ck_ref[h] - lse_ref[h]
                if masked:
                    t = jnp.where(valid, t, NEG)
                p = jnp.exp(t)
                ds = p * (dp - (d0 if e == 0 else d1))
                dck_s[h] += jnp.sum(ds, axis=0, keepdims=True)
                row_sums = jnp.where(lane == h, jnp.sum(ds, axis=1, keepdims=True), row_sums)
                ds_b = ds.astype(BF16)
                dq = dq + jnp.dot(ds_b, ke, preferred_element_type=F32)
                dks.append(lax.dot_general(ds_b, qs, TN_DIMS, preferred_element_type=F32))
                dvs.append(lax.dot_general(p.astype(BF16), dop, TN_DIMS, preferred_element_type=F32))
                if e == 1:
                    dq_ref[pl.ds(q0, TM), cols] += dq
                    dk_s[pp] += jnp.where(half == 0, dks[0], dks[1])
                    dv_s[pp] += jnp.where(half == 0, dvs[0], dvs[1])
            dcq_ref[pl.ds(q0, TM), :] += row_sums

        @pl.when((qi > kj) & (kj > 0))
        def _():
            tile(False)

        @pl.when((qi == kj) | (kj == 0))
        def _():
            tile(True)

        @pl.when(qi == NT - 1)
        def _():
            for pp in range(4):
                cols = slice(pp * BLK, (pp + 1) * BLK)
                dk_ref[:, cols] = dk_s[pp].astype(BF16)
                dv_ref[:, cols] = dv_s[pp].astype(BF16)
            dck_ref[...] = dck_s[...]

    qrow = lambda blk, w=512: pl.BlockSpec((TM, w), lambda n, kj, qi: (qi[n], blk))
    krow = lambda blk: pl.BlockSpec((TM, W2), lambda n, kj, qi: (kj[n], blk))
    grid_spec = pltpu.PrefetchScalarGridSpec(
        num_scalar_prefetch=2, grid=(n_pairs,),
        in_specs=[qrow(QB, W2), qrow(QB + 1, W2), krow(KB), krow(KB + 1), krow(VB), krow(VB + 1), qrow(0), qrow(1),
                  pl.BlockSpec((N_HEADS, TM, 1), lambda n, kj, qi: (0, qi[n], 0)),
                  pl.BlockSpec((N_HEADS, 1, TM), lambda n, kj, qi: (0, 0, kj[n]))] + [HBM_SPEC] * len(rider.operands),
        out_specs=[pl.BlockSpec((LP, 512), lambda n, kj, qi: (0, 0)),
                   pl.BlockSpec((TM, 512), lambda n, kj, qi: (kj[n], 0)),
                   pl.BlockSpec((TM, 512), lambda n, kj, qi: (kj[n], 0)),
                   pl.BlockSpec((N_HEADS, 1, TM), lambda n, kj, qi: (0, 0, kj[n])),
                   pl.BlockSpec((LP, BLK), lambda n, kj, qi: (0, 0))] + [HBM_SPEC] * len(rider.out_shapes),
        scratch_shapes=[pltpu.VMEM((4, TM, BLK), F32), pltpu.VMEM((4, TM, BLK), F32),
                        pltpu.VMEM((N_HEADS, 1, TM), F32)] + rider.scratch())
    dq, dk, dv, dck, dcq, *carried = pl.pallas_call(
        body, grid_spec=grid_spec,
        out_shape=[SDS((LP, 512), F32), SDS((LP, 512), BF16), SDS((LP, 512), BF16), SDS((N_HEADS, 1, LP), F32),
                   SDS((LP, BLK), F32)] + rider.out_shapes,
        compiler_params=_cparams(("arbitrary",)), name="fox_bwd",
    )(jnp.asarray([p[0] for p in pairs], jnp.int32), jnp.asarray([p[1] for p in pairs], jnp.int32),
      proj, proj, proj, proj, proj, proj, o_b, dmix, lse, ck_t, *rider.operands)
    return dq, dk, dv, dck, dcq, carried


N_SEG = 3
N_KEY = N_SEG * BLK
GROUP = 4
QW = GROUP * BLK


def _bucket_tables_t():
    return np.ascontiguousarray(_bucket_tables().transpose(0, 2, 1))


def _stack_heads(ref, g, scale):
    half = _lane_half(BLK)
    out = []
    for pair in range(2):
        x = ref[:, (2 * g + pair) * BLK:(2 * g + pair + 1) * BLK].astype(F32) * scale
        swapped = pltpu.roll(x, HALF, 1)
        for e in range(2):
            out.append(jnp.where(half == g, x if e == g else swapped, 0.0).astype(BF16))
    return jnp.concatenate(out, axis=0)


def _unstack_heads(x_t, g, ref, scale):
    for pair in range(2):
        both = jnp.concatenate([x_t[:, (2 * pair) * BLK:(2 * pair + 1) * BLK],
                                x_t[:, (2 * pair + 1) * BLK:(2 * pair + 2) * BLK]], axis=0)
        ref[:, (2 * g + pair) * BLK:(2 * g + pair + 1) * BLK] = (both.T * scale).astype(ref.dtype)


def _swa_tables(tab_ref, sink_ref, bkt_ref, tbl, sink_row):
    kk = lax.broadcasted_iota(jnp.int32, (BLK, BLK), 0)
    qq = lax.broadcasted_iota(jnp.int32, (BLK, BLK), 1)
    neg = jnp.full((BLK, BLK), NEG, F32)
    lane = lax.broadcasted_iota(jnp.int32, (1, QW), 1) // BLK
    for g in range(2):
        row = jnp.zeros((1, QW), F32)
        for hh in range(GROUP):
            h = GROUP * g + hh
            cols = slice(hh * BLK, (hh + 1) * BLK)
            row = jnp.where(lane == hh, sink_ref[0, h], row)

            def step(b, carry, h=h):
                t = tab_ref[b, h]
                return jnp.where(bkt_ref[0] == b, t, carry[0]), jnp.where(bkt_ref[1] == b, t, carry[1])
            zero = jnp.zeros((BLK, BLK), F32)
            cur, prev = lax.fori_loop(0, N_BUCKETS, step, (zero, zero))
            far = jnp.full((BLK, BLK), tab_ref[N_BUCKETS - 1, h], F32)
            causal = jnp.where(kk <= qq, cur, neg)
            segments = [
                (neg, neg, jnp.where(kk >= PAD_ROWS, causal, neg)),
                (jnp.where(kk >= PAD_ROWS, prev, neg), neg, causal),
                (jnp.where(kk >= PAD_ROWS, far, neg), jnp.where(kk > qq, prev, neg), causal)]
            for case in range(3):
                for seg in range(N_SEG):
                    tbl[case, g, seg * BLK:(seg + 1) * BLK, cols] = segments[case][seg]
        sink_row[g] = row


def _swa_prep(proj):
    rows = LP // 3

    def body(k_ref, v_ref, kt_ref, vt_ref):
        kt_ref[...] = k_ref[...].astype(F32).T.astype(BF16)
        vt_ref[...] = v_ref[...].astype(F32).T.astype(BF16)

    col = pl.BlockSpec((BLK, rows), lambda i: (0, i))
    return pl.pallas_call(
        body, grid=(3,),
        in_specs=[pl.BlockSpec((rows, BLK), lambda i: (i, KA)), pl.BlockSpec((rows, BLK), lambda i: (i, VA))],
        out_specs=[col, col], out_shape=[SDS((BLK, LP), BF16)] * 2,
        compiler_params=_cparams(("parallel",)), name="swa_prep")(proj, proj)


def _segments(ref, i, by_rows):
    starts = [0, pl.multiple_of(jnp.maximum(i - 1, 0) * BLK, BLK), pl.multiple_of(i * BLK, BLK)]
    if by_rows:
        return jnp.concatenate([ref[pl.ds(s, BLK), :] for s in starts], axis=0)
    return jnp.concatenate([ref[:, pl.ds(s, BLK)] for s in starts], axis=1)


def _swa_fwd(proj, vt_a, rel_bias, sinks, bkt_t, rider):
    def body(*refs):
        ((tab_ref, sink_ref, bkt_ref, q_ref, k_ref, vt_ref), (o_ref, lse_ref),
         (tbl, sink_row), mine) = rider.split(refs, 6, 2, 2)
        i = pl.program_id(0)
        rider.at_steps(mine, i == 0, i == NBLK // 2, i == NBLK - 1)

        @pl.when(i == 0)
        def _():
            _swa_tables(tab_ref, sink_ref, bkt_ref, tbl, sink_row)

        case = jnp.minimum(i, 2)
        k_cat = _segments(k_ref, i, True)
        vt_cat = _segments(vt_ref, i, False)
        raw = [lax.dot_general(k_cat, _stack_heads(q_ref, g, SCALE), NT_DIMS, preferred_element_type=F32)
               for g in range(2)]
        for g in range(2):
            s_t = raw[g] + tbl[case, g]
            sink = sink_row[g]
            m = jnp.maximum(_over_keys(jnp.max, s_t), sink)
            p_t = jnp.exp(s_t - m)
            l = _over_keys(jnp.sum, p_t) + jnp.exp(sink - m)
            o_t = jnp.dot(vt_cat[g * HALF:(g + 1) * HALF, :], p_t.astype(BF16), preferred_element_type=F32)
            _unstack_heads(o_t * (1.0 / l), g, o_ref, 1.0)
            lse = m + jnp.log(l)
            for hh in range(GROUP):
                lse_ref[GROUP * g + hh] = lse[:, hh * BLK:(hh + 1) * BLK]

    smem = pl.BlockSpec(memory_space=pltpu.SMEM)
    o_a, lse, *carried = pl.pallas_call(
        body, grid=(NBLK,),
        in_specs=[smem, smem, pl.BlockSpec((2, BLK, BLK), lambda i: (0, 0, 0)),
                  pl.BlockSpec((BLK, 512), lambda i: (i, QA)), pl.BlockSpec((LP, BLK), lambda i: (0, KA)),
                  pl.BlockSpec((BLK, LP), lambda i: (0, 0))] + [HBM_SPEC] * len(rider.operands),
        out_specs=[pl.BlockSpec((BLK, 512), lambda i: (i, 0)),
                   pl.BlockSpec((N_HEADS, 1, BLK), lambda i: (0, 0, i))] + [HBM_SPEC] * len(rider.out_shapes),
        out_shape=[SDS((LP, 512), BF16), SDS((N_HEADS, 1, LP), F32)] + rider.out_shapes,
        scratch_shapes=[pltpu.VMEM((3, 2, N_KEY, QW), F32), pltpu.VMEM((2, 1, QW), F32)] + rider.scratch(),
        compiler_params=_cparams(("arbitrary",)), name="swa_fwd",
    )(rel_bias, sinks, bkt_t, proj, proj, vt_a, *rider.operands)
    return o_a, lse, carried


def _swa_bwd(proj, kt_a, o_a, dmix, lse, rel_bias, sinks, bkt_t):
    def body(tab_ref, sink_ref, bkt_ref, q_ref, k_ref, v_ref, kt_ref, o_ref, do_ref, lse_ref,
             dq_ref, dk_ref, dv_ref, dbias_ref, dsink_ref, tbl, sink_row, acc, dsk):
        i = pl.program_id(0)

        @pl.when(i == 0)
        def _():
            _swa_tables(tab_ref, sink_ref, bkt_ref, tbl, sink_row)
            dk_ref[...] = jnp.zeros_like(dk_ref)
            dv_ref[...] = jnp.zeros_like(dv_ref)
            acc[...] = jnp.zeros_like(acc)
            dsk[...] = jnp.zeros_like(dsk)

        case = jnp.minimum(i, 2)
        first = jnp.full((BLK, QW), i, jnp.int32) == 1
        k_cat = _segments(k_ref, i, True)
        v_cat = _segments(v_ref, i, True)
        kt_cat = _segments(kt_ref, i, False)
        dk_cat = jnp.zeros((N_KEY, BLK), F32)
        dv_cat = jnp.zeros((N_KEY, BLK), F32)
        for g in range(2):
            d_parts = []
            for pair in range(2):
                cols = slice((2 * g + pair) * BLK, (2 * g + pair + 1) * BLK)
                prod_t = (do_ref[:, cols].astype(F32) * o_ref[:, cols].astype(F32)).T
                d_parts += [jnp.sum(prod_t[:HALF], axis=0, keepdims=True),
                            jnp.sum(prod_t[HALF:], axis=0, keepdims=True)]
            d_row = jnp.concatenate(d_parts, axis=1)
            lse_row = jnp.concatenate([lse_ref[GROUP * g + hh] for hh in range(GROUP)], axis=1)
            q_st = _stack_heads(q_ref, g, SCALE)
            do_st = _stack_heads(do_ref, g, 1.0)
            s_t = lax.dot_general(k_cat, q_st, NT_DIMS, preferred_element_type=F32) + tbl[case, g]
            p_t = jnp.exp(s_t - lse_row)
            dp_t = lax.dot_general(v_cat, do_st, NT_DIMS, preferred_element_type=F32)
            ds_t = p_t * (dp_t - d_row)
            dsk[g] += -jnp.exp(sink_row[g] - lse_row) * d_row
            acc[g, 0:BLK] += jnp.where(first, 0.0, ds_t[0:BLK])
            acc[g, BLK:2 * BLK] += jnp.where(first, ds_t[0:BLK], ds_t[BLK:2 * BLK])
            acc[g, 2 * BLK:N_KEY] += ds_t[2 * BLK:N_KEY]
            ds_b = ds_t.astype(BF16)
            dk_cat = dk_cat + jnp.dot(ds_b, q_st, preferred_element_type=F32)
            dv_cat = dv_cat + jnp.dot(p_t.astype(BF16), do_st, preferred_element_type=F32)
            dq_t = jnp.dot(kt_cat[g * HALF:(g + 1) * HALF, :], ds_b, preferred_element_type=F32)
            _unstack_heads(dq_t, g, dq_ref, SCALE)

        prev0 = pl.multiple_of(jnp.maximum(i - 1, 0) * BLK, BLK)
        cur0 = pl.multiple_of(i * BLK, BLK)
        for ref, cat in ((dk_ref, dk_cat), (dv_ref, dv_cat)):
            ref[0:BLK, :] += cat[0:BLK]
            ref[pl.ds(prev0, BLK), :] += cat[BLK:2 * BLK]
            ref[pl.ds(cur0, BLK), :] += cat[2 * BLK:N_KEY]

        @pl.when(i == NBLK - 1)
        def _():
            lane = lax.broadcasted_iota(jnp.int32, (1, BLK), 1)

            def per_bucket(b, carry):
                row = jnp.zeros((1, BLK), F32)
                for h in range(N_HEADS):
                    g, cols = h // GROUP, slice((h % GROUP) * BLK, (h % GROUP + 1) * BLK)
                    val = (jnp.sum(jnp.where(bkt_ref[0] == b, acc[g, 2 * BLK:N_KEY, cols], 0.0), keepdims=True)
                           + jnp.sum(jnp.where(bkt_ref[1] == b, acc[g, BLK:2 * BLK, cols], 0.0), keepdims=True))
                    row = jnp.where(lane == h, val, row)
                dbias_ref[pl.ds(b, 1), :] = row
                return carry

            lax.fori_loop(0, N_BUCKETS, per_bucket, 0)
            far = jnp.zeros((1, BLK), F32)
            dsr = jnp.zeros((1, BLK), F32)
            for h in range(N_HEADS):
                g, cols = h // GROUP, slice((h % GROUP) * BLK, (h % GROUP + 1) * BLK)
                far = jnp.where(lane == h, jnp.sum(acc[g, 0:BLK, cols], keepdims=True), far)
                dsr = jnp.where(lane == h, jnp.sum(dsk[g, :, cols], keepdims=True), dsr)
            dbias_ref[N_BUCKETS - 1:N_BUCKETS, :] += far
            dsink_ref[...] = dsr

    smem = pl.BlockSpec(memory_space=pltpu.SMEM)
    blk512 = lambda col: pl.BlockSpec((BLK, 512), lambda i: (i, col))
    full = lambda r, c: pl.BlockSpec((r, c), lambda i: (0, 0))
    return pl.pallas_call(
        body, grid=(NBLK,),
        in_specs=[smem, smem, pl.BlockSpec((2, BLK, BLK), lambda i: (0, 0, 0)), blk512(QA),
                  pl.BlockSpec((LP, BLK), lambda i: (0, KA)), pl.BlockSpec((LP, BLK), lambda i: (0, VA)),
                  full(BLK, LP), blk512(0), blk512(0), pl.BlockSpec((N_HEADS, 1, BLK), lambda i: (0, 0, i))],
        out_specs=[blk512(0), full(LP, BLK), full(LP, BLK), full(N_BUCKETS, BLK), full(1, BLK)],
        out_shape=[SDS((LP, 512), BF16), SDS((LP, BLK), F32), SDS((LP, BLK), F32),
                   SDS((N_BUCKETS, BLK), F32), SDS((1, BLK), F32)],
        scratch_shapes=[pltpu.VMEM((3, 2, N_KEY, QW), F32), pltpu.VMEM((2, 1, QW), F32),
                        pltpu.VMEM((2, N_KEY, QW), F32), pltpu.VMEM((2, 1, QW), F32)],
        compiler_params=_cparams(("arbitrary",)), name="swa_bwd",
    )(rel_bias, sinks, bkt_t, proj, proj, proj, kt_a, o_a, dmix, lse)


def _local_step(x, tgt, meta, rel_bias, g_pre_mix, g_post_mix, g_pre_ffn, g_post_ffn, b_forget, sinks,
                w_in_b, out_rider, out_weight, ffn_rider, ffn_weights, early_grads):
    bkt_t = jnp.asarray(_bucket_tables_t())
    h0 = jnp.concatenate([jnp.zeros((PAD_ROWS, D_MODEL), F32), meta, x], axis=0)
    b_p = jnp.pad(b_forget, ((0, 0), (0, BLK - N_HEADS)))

    hn1, proj, f = _pre_mix(h0, g_pre_mix, w_in_b)
    kt_a, vt_a = _swa_prep(proj)
    o_a, lse_a, carried_out = _swa_fwd(proj, vt_a, rel_bias, sinks, bkt_t, out_rider)
    w_out_b = out_weight(carried_out)
    cum = _forget_cumsum(f, b_p)
    ck_t = cum[:, :N_HEADS].T.reshape(N_HEADS, 1, LP)
    q_aug, k_aug, v_t = _fox_prep(proj, cum)
    o_b, lse_row, carried = _fox_fwd(q_aug, k_aug, v_t, ffn_rider)
    lse_b = lse_row.reshape(N_HEADS, LP, 1)
    w_gu_b, w_dn_b = ffn_weights(carried)
    a, h1, hn2 = _attn_out(o_a, o_b, w_out_b, h0, g_post_mix, g_pre_ffn)
    g, u, act = _ffn_up(hn2, w_gu_b)
    dff, dy, loss_blk, dg_post_ffn = _ffn_down_loss(act, w_dn_b, h1, tgt, g_post_ffn)

    dw_dn = _mm_tn([act], dff, FF_T, "dw_down", BF16)
    dg, du = _ffn_down_bwd(dff, w_dn_b, g, u)
    dw_gu = _dw_gate_up(hn2, dg, du)
    dh1, da, dg_pre_ffn, dg_post_mix = _ffn_up_bwd(dg, du, w_gu_b, h1, a, dy, g_pre_ffn, g_post_mix)
    dw_out = _mm_tn([o_a, o_b], da, D_MODEL, "dw_out", BF16)
    dmix = _attn_out_bwd(da, w_out_b)
    dq_b, dk_b, dv_b, dck, dcq, landed = _fox_bwd(proj, o_b, dmix, lse_b, ck_t, early_grads(dw_gu, dw_dn, dw_out))
    dq_a, dk_a, dv_a, dbias, dsink = _swa_bwd(proj, kt_a, o_a, dmix, lse_a, rel_bias, sinks, bkt_t)
    dcum = dcq - jnp.pad(dck.reshape(N_HEADS, LP).T, ((0, 0), (0, BLK - N_HEADS)))
    df, db = _forget_cumsum_bwd(dcum, f, b_p)
    dproj, dx, dmeta, dg_pre_mix = _pre_mix_bwd(dq_a, dq_b, dk_b, dv_b, dk_a, dv_a, df, w_in_b, h0, dh1, g_pre_mix)
    dw_in = _mm_tn([hn1], dproj, D_MODEL, "dw_in", BF16)

    return dict(loss=loss_blk[0, 0], grad_x=dx, meta=dmeta,
                rel_bias=dbias[:, :N_HEADS], ln_pre_mix=dg_pre_mix, ln_post_mix=dg_post_mix,
                ln_pre_ffn=dg_pre_ffn, ln_post_ffn=dg_post_ffn, b_forget=db[:, :N_HEADS],
                sinks=dsink[:, :N_HEADS], w_in=dw_in, w_out=dw_out, w_gate_up=dw_gu, w_down=dw_dn,
                landed=landed)


N_SMALL = 24
LOSS_ROW = 6


def _place():
    x, y, c = lax.axis_index("x"), lax.axis_index("y"), lax.axis_index("c")
    return x, y, c, [(1 - x, y), (x, 1 - y), (1 - x, 1 - y)]


def _run_alone(rider, name):
    a, b = len(rider.operands), len(rider.out_shapes)

    def body(*refs):
        mine = (refs[:a], refs[a:a + b], refs[a + b:])
        rider.first(*mine)
        rider.middle(*mine)
        rider.last(*mine)

    return pl.pallas_call(body, in_specs=[HBM_SPEC] * a, out_specs=[HBM_SPEC] * b, out_shape=rider.out_shapes,
                          scratch_shapes=rider.scratch(), name=name)(*rider.operands)


def _gather_rider(shards, own_too, by_columns=()):
    n = len(shards)

    def slot(a, outs, chip, h):
        if a in by_columns:
            cols = shards[a].shape[2]
            return outs[a].at[h, :, pl.ds(pl.multiple_of(chip * cols, BLK), cols)]
        return outs[a].at[chip, h]

    def own_copies(ins, outs, sems):
        x, y, _, _ = _place()
        if not own_too:
            return []
        return [pltpu.make_async_copy(ins[a].at[h], slot(a, outs, 2 * x + y, h), sems[2].at[2 * a + h])
                for a in range(n) for h in range(2)]

    def copies(ins, outs, sems):
        send_sems, recv_sems = sems[:2]
        x, y, c, others = _place()
        chip = 2 * x + y
        sibling = (x, y, 1 - c)

        def rc(a, k, src, dst, to):
            return pltpu.make_async_remote_copy(src_ref=src, dst_ref=dst, send_sem=send_sems.at[6 * a + k],
                                                recv_sem=recv_sems.at[6 * a + k], device_id=to, device_id_type=MESH)

        pairs = [(a, j, ox, oy) for a in range(n) for j, (ox, oy) in enumerate(others)]
        there = lambda a, ox, oy, h: slot(a, outs, 2 * ox + oy, h)
        return dict(
            sent=lambda: [rc(a, j, ins[a].at[c], slot(a, outs, chip, c), (ox, oy, c)) for a, j, ox, oy in pairs],
            landed=lambda: [rc(a, j, there(a, ox, oy, c), there(a, ox, oy, c), sibling) for a, j, ox, oy in pairs],
            passed=lambda: [rc(a, 3 + j, there(a, ox, oy, c), there(a, ox, oy, c), sibling)
                            for a, j, ox, oy in pairs],
            arriving=lambda: [rc(a, 3 + j, there(a, ox, oy, 1 - c), there(a, ox, oy, 1 - c), sibling)
                              for a, j, ox, oy in pairs])

    def first(*mine):
        for cp in copies(*mine)["sent"]() + own_copies(*mine):
            cp.start()

    def middle(*mine):
        kinds = copies(*mine)
        for got, cp in zip(kinds["landed"](), kinds["passed"]()):
            got.wait_recv()
            cp.start()

    def last(*mine):
        kinds = copies(*mine)
        for cp in kinds["arriving"]():
            cp.wait_recv()
        for cp in kinds["sent"]() + kinds["passed"]():
            cp.wait_send()
        for cp in own_copies(*mine):
            cp.wait()

    shapes = [SDS((2, s.shape[1], 4 * s.shape[2]) if a in by_columns else (4,) + s.shape, s.dtype)
              for a, s in enumerate(shards)]
    return _Rider(shards, shapes, [6 * n, 6 * n] + [2 * n] * own_too, first, middle, last)


def _swap_rider(grads):
    n = len(grads)
    slabs = [(a, s) for a in range(n) for s in range(grads[a].shape[0])]

    def copies(ins, outs, sems):
        x, y, c, _ = _place()
        return [pltpu.make_async_remote_copy(
            src_ref=ins[a].at[s, 1 - c], dst_ref=outs[a].at[s], send_sem=sems[0].at[k], recv_sem=sems[1].at[k],
            device_id=(x, y, 1 - c), device_id_type=MESH) for k, (a, s) in enumerate(slabs)]

    def first(*mine):
        for cp in copies(*mine):
            cp.start()

    def middle(*mine):
        pass

    def last(*mine):
        for cp in copies(*mine):
            cp.wait()

    return _Rider(grads, [SDS(g.shape[:1] + g.shape[2:], g.dtype) for g in grads], [len(slabs), len(slabs)],
                  first, middle, last)


def _pair_sum(g, got, c_arr, name):
    n_s, rh, cc = got.shape

    def body(c_ref, g_ref, p_ref, o_ref):
        o_ref[0] = (g_ref[0, 0].astype(F32) + p_ref[0].astype(F32)).astype(BF16)

    grid_spec = pltpu.PrefetchScalarGridSpec(
        num_scalar_prefetch=1, grid=(n_s,),
        in_specs=[pl.BlockSpec((1, 1, rh, cc), lambda s, c_ref: (s, c_ref[0], 0, 0)),
                  pl.BlockSpec((1, rh, cc), lambda s, c_ref: (s, 0, 0))],
        out_specs=pl.BlockSpec((1, rh, cc), lambda s, c_ref: (s, 0, 0)))
    return pl.pallas_call(body, grid_spec=grid_spec, out_shape=SDS((n_s, rh, cc), BF16),
                          compiler_params=_cparams(("parallel",)), name=name)(c_arr, g, got)


def _direct_rider(grads):
    n = len(grads)

    def copies(ins, outs, sems):
        x, y, c, others = _place()
        peers = [(x, y, 1 - c)] + [(ox, oy, c) for ox, oy in others] + [(ox, oy, 1 - c) for ox, oy in others]
        return [pltpu.make_async_remote_copy(
            src_ref=ins[a].at[2 * px + py, pc], dst_ref=outs[a].at[k], send_sem=sems[0].at[7 * a + k],
            recv_sem=sems[1].at[7 * a + k], device_id=(px, py, pc), device_id_type=MESH)
            for a in range(n) for k, (px, py, pc) in enumerate(peers)]

    def first(*mine):
        for cp in copies(*mine):
            cp.start()

    def middle(*mine):
        pass

    def last(*mine):
        for cp in copies(*mine):
            cp.wait()

    return _Rider(grads, [SDS((7,) + g.shape[2:], g.dtype) for g in grads], [7 * n, 7 * n], first, middle, last)


def _owner_sum(grads, landed, own_arr, after, name):
    rh, cc = landed.shape[1:]
    tr = rh // 2

    def body(own_ref, g_ref, p_ref, after_ref, o_ref):
        total = g_ref[0, 0].astype(F32)
        for k in range(7):
            total = total + p_ref[k].astype(F32)
        o_ref[...] = total

    grid_spec = pltpu.PrefetchScalarGridSpec(
        num_scalar_prefetch=1, grid=(2,),
        in_specs=[pl.BlockSpec((1, 1, tr, cc), lambda i, own: (own[0], own[1], i, 0)),
                  pl.BlockSpec((7, tr, cc), lambda i, own: (0, i, 0)), pl.BlockSpec(memory_space=pl.ANY)],
        out_specs=pl.BlockSpec((tr, cc), lambda i, own: (i, 0)))
    return pl.pallas_call(body, grid_spec=grid_spec, out_shape=SDS((rh, cc), F32),
                          compiler_params=_cparams(("parallel",)), name=name)(own_arr, grads, landed, after)


SEM_SPEC = pl.BlockSpec(memory_space=pltpu.SEMAPHORE)
N_LATE = 10


def _late_copies(part_ref, landed_ref, small_ref, all_ref, send_sems, recv_sems):
    x, y, c, others = _place()
    me = 4 * x + 2 * y + c
    peers = [(x, y, 1 - c)] + [(ox, oy, c) for ox, oy in others] + [(ox, oy, 1 - c) for ox, oy in others]
    big = [pltpu.make_async_remote_copy(
        src_ref=part_ref.at[2 * ox + oy], dst_ref=landed_ref.at[j], send_sem=send_sems.at[j], recv_sem=recv_sems.at[j],
        device_id=(ox, oy, c), device_id_type=MESH) for j, (ox, oy) in enumerate(others)]
    small = [pltpu.make_async_remote_copy(
        src_ref=small_ref, dst_ref=all_ref.at[me], send_sem=send_sems.at[3 + k], recv_sem=recv_sems.at[3 + k],
        device_id=peer, device_id_type=MESH) for k, peer in enumerate(peers)]
    return big + small


def _late_exchange_start(part, small):
    def body(part_ref, landed_ref, small_ref, all_ref, send_sems, recv_sems, part_o, landed_o, small_o, all_o, token):
        for cp in _late_copies(part_ref, landed_ref, small_ref, all_ref, send_sems, recv_sems):
            cp.start()
        token[...] = jnp.zeros_like(token)

    hbm = lambda a: pltpu.HBM(a.shape, a.dtype)
    landed = lax.empty((3,) + part.shape[1:], part.dtype)
    everyone = lax.empty((8,) + small.shape, small.dtype)
    operands = [pltpu.with_memory_space_constraint(a, pltpu.HBM) for a in (part, landed, small, everyone)]
    return pl.pallas_call(
        body, name="late_exchange_start",
        out_shape=(pltpu.SemaphoreType.DMA((N_LATE,)), pltpu.SemaphoreType.DMA((N_LATE,)),
                   hbm(part), hbm(landed), hbm(small), hbm(everyone), SDS((8, BLK), F32)),
        in_specs=[HBM_SPEC] * 4,
        out_specs=(SEM_SPEC, SEM_SPEC, HBM_SPEC, HBM_SPEC, HBM_SPEC, HBM_SPEC, pl.BlockSpec(memory_space=pltpu.VMEM)),
        input_output_aliases={0: 2, 1: 3, 2: 4, 3: 5},
        compiler_params=pltpu.CompilerParams(has_side_effects=pltpu.SideEffectType.DATAFLOW_SIDE_EFFECTING),
    )(*operands)


def _late_exchange_wait(send_sems, recv_sems, part, landed, small, everyone, after):
    def body(part_ref, landed_ref, small_ref, all_ref, send_sems, recv_sems, after_ref, part_o, landed_o, small_o, all_o):
        for cp in _late_copies(part_ref, landed_ref, small_ref, all_ref, send_sems, recv_sems):
            cp.wait_send()
            cp.wait_recv()

    hbm = lambda a: pltpu.HBM(a.shape, a.dtype)
    out = pl.pallas_call(
        body, name="late_exchange_wait",
        out_shape=(hbm(part), hbm(landed), hbm(small), hbm(everyone)),
        in_specs=[HBM_SPEC] * 4 + [SEM_SPEC, SEM_SPEC, pl.BlockSpec(memory_space=pl.ANY)],
        out_specs=(HBM_SPEC,) * 4, input_output_aliases={0: 0, 1: 1, 2: 2, 3: 3},
        compiler_params=pltpu.CompilerParams(has_side_effects=pltpu.SideEffectType.DATAFLOW_SIDE_EFFECTING),
    )(part, landed, small, everyone, send_sems, recv_sems, after)
    return out[0], out[1], out[3]


def _chip_sum(parts, landed, chip_arr, name):
    rh, cc = landed.shape[1:]
    tr = rh // 2

    def body(chip_ref, own_ref, p_ref, o_ref):
        o_ref[...] = ((own_ref[0].astype(F32) + p_ref[0].astype(F32)) + p_ref[1].astype(F32)) + p_ref[2].astype(F32)

    grid_spec = pltpu.PrefetchScalarGridSpec(
        num_scalar_prefetch=1, grid=(2,),
        in_specs=[pl.BlockSpec((1, tr, cc), lambda i, chip_ref: (chip_ref[0], i, 0)),
                  pl.BlockSpec((3, tr, cc), lambda i, chip_ref: (0, i, 0))],
        out_specs=pl.BlockSpec((tr, cc), lambda i, chip_ref: (i, 0)))
    return pl.pallas_call(body, grid_spec=grid_spec, out_shape=SDS((rh, cc), F32),
                          compiler_params=_cparams(("parallel",)), name=name)(chip_arr, parts, landed)


def _device_sum(p):
    def body(p_ref, o_ref):
        acc = p_ref[0]
        for k in range(1, 8):
            acc = acc + p_ref[k]
        o_ref[...] = acc

    return pl.pallas_call(body, out_shape=SDS(p.shape[1:], F32), name="small_sum")(p)


def _join_halves(halves, name):
    n = len(halves)

    def body(*refs):
        ins, outs = refs[:n], refs[n:2 * n]
        send_sems, recv_sems = refs[2 * n:]
        x, y, c, _ = _place()
        copies = [pltpu.make_async_remote_copy(
            src_ref=ins[a], dst_ref=outs[a], send_sem=send_sems.at[a], recv_sem=recv_sems.at[a],
            device_id=(x, y, 1 - c), device_id_type=MESH) for a in range(n)]
        for cp in copies:
            cp.start()
        for cp in copies:
            cp.wait()

    return pl.pallas_call(
        body, in_specs=[HBM_SPEC] * n, out_specs=[HBM_SPEC] * n,
        out_shape=[SDS(h.shape, h.dtype) for h in halves],
        scratch_shapes=[pltpu.SemaphoreType.DMA((n,)), pltpu.SemaphoreType.DMA((n,))],
        name=name)(*halves)


def _adamw(w, g, m, v, name, tr=None):
    rows, cols = w.shape
    tr = tr or rows
    assert rows % tr == 0

    def body(w_ref, g_ref, m_ref, v_ref, d_ref, nm_ref, nv_ref):
        gg = g_ref[...]
        nm = ADAM_B1 * m_ref[...] + (1.0 - ADAM_B1) * gg
        nv = ADAM_B2 * v_ref[...] + (1.0 - ADAM_B2) * (gg * gg)
        nm_ref[...] = nm
        nv_ref[...] = nv
        m_hat = nm / (1.0 - ADAM_B1 ** ADAM_STEP)
        v_hat = nv / (1.0 - ADAM_B2 ** ADAM_STEP)
        d_ref[...] = -ADAM_LR * (m_hat / (jnp.sqrt(v_hat) + ADAM_EPS) + ADAM_WD * w_ref[...])

    blk = pl.BlockSpec((tr, cols), lambda i: (i, 0))
    return pl.pallas_call(
        body, grid=(rows // tr,), in_specs=[blk] * 4, out_specs=[blk] * 3,
        out_shape=[SDS((rows, cols), F32)] * 3,
        compiler_params=_cparams(("parallel",)), name=name)(w, g, m, v)


def _adamw_halves(w, mine, theirs, m, v, c_arr, name):
    rows, cols = w.shape
    rh = rows // 2
    tr = rh if rh <= 352 else 256
    nh = rh // tr

    def body(c_ref, w_ref, mine_ref, theirs_ref, m_ref, v_ref, g_ref, d_ref, nm_ref, nv_ref):
        own = jnp.full((tr, cols), pl.program_id(0), jnp.int32) == c_ref[0]
        gg = jnp.where(own, mine_ref[...], theirs_ref[...])
        g_ref[...] = gg
        nm = ADAM_B1 * m_ref[...] + (1.0 - ADAM_B1) * gg
        nv = ADAM_B2 * v_ref[...] + (1.0 - ADAM_B2) * (gg * gg)
        nm_ref[...] = nm
        nv_ref[...] = nv
        m_hat = nm / (1.0 - ADAM_B1 ** ADAM_STEP)
        v_hat = nv / (1.0 - ADAM_B2 ** ADAM_STEP)
        d_ref[...] = -ADAM_LR * (m_hat / (jnp.sqrt(v_hat) + ADAM_EPS) + ADAM_WD * w_ref[...])

    whole = pl.BlockSpec((tr, cols), lambda hh, i, c_ref: (hh * nh + i, 0))
    part = pl.BlockSpec((tr, cols), lambda hh, i, c_ref: (i, 0))
    grid_spec = pltpu.PrefetchScalarGridSpec(
        num_scalar_prefetch=1, grid=(2, nh), in_specs=[whole, part, part, whole, whole], out_specs=[whole] * 4)
    return pl.pallas_call(body, grid_spec=grid_spec, out_shape=[SDS((rows, cols), F32)] * 4,
                          compiler_params=_cparams(("parallel", "parallel")), name=name)(c_arr, w, mine, theirs, m, v)


def _pack_small(pre_mix, post_mix, pre_ffn, post_ffn, rel_bias, b_forget, sinks):
    def at(row, v):
        return jnp.pad(v, ((row, 7 - row), (0, D_MODEL - v.shape[1])))
    return (at(0, pre_mix) + at(1, post_mix) + at(2, pre_ffn) + at(3, post_ffn)
            + at(4, rel_bias.reshape(1, N_BUCKETS * N_HEADS)) + at(5, jnp.concatenate([b_forget, sinks], axis=1)))


def _unpack_small(p):
    return dict(ln_pre_mix=p[0:1], ln_post_mix=p[1:2], ln_pre_ffn=p[2:3], ln_post_ffn=p[3:4],
                rel_bias=p[4, :N_BUCKETS * N_HEADS].reshape(N_BUCKETS, N_HEADS),
                b_forget=p[5:6, 0:N_HEADS], sinks=p[5:6, N_HEADS:2 * N_HEADS])


WEIGHTS = ("meta_tokens", "rel_bias", "ln_pre_mix", "ln_post_mix", "ln_pre_ffn", "ln_post_ffn",
           "w_in", "b_forget", "sinks", "w_out", "w_gate_up", "w_down")


def kernel(x, meta_tokens, rel_bias, ln_pre_mix, ln_post_mix, ln_pre_ffn, ln_post_ffn, w_in, b_forget, sinks, w_out, w_gate_up, w_down, loss_target, m_meta_tokens, m_rel_bias, m_ln_pre_mix, m_ln_post_mix, m_ln_pre_ffn, m_ln_post_ffn, m_w_in, m_b_forget, m_sinks, m_w_out, m_w_gate_up, m_w_down, v_meta_tokens, v_rel_bias, v_ln_pre_mix, v_ln_post_mix, v_ln_pre_ffn, v_ln_post_ffn, v_w_in, v_b_forget, v_sinks, v_w_out, v_w_gate_up, v_w_down):
    xi, yi, ci = lax.axis_index("x"), lax.axis_index("y"), lax.axis_index("c")
    chip = 2 * xi + yi
    c_arr = jnp.reshape(ci, (1,)).astype(jnp.int32)

    def halves(w, dtype):
        return w.astype(dtype).reshape(2, w.shape[0] // 2, w.shape[1])

    def with_own(gathered, shards):
        return [lax.dynamic_update_slice(got, own[None], (chip, 0, 0, 0)) for got, own in zip(gathered, shards)]

    shards = [halves(w_in[0], BF16), halves(meta_tokens, F32)]
    gw_in, g_meta = with_own(_run_alone(_gather_rider(shards, False), "gather_mixer_weights"), shards)
    out_shards = [halves(w_out[0], BF16)]
    ffn_shards = [halves(w_gate_up[0], BF16), halves(w_down[0], BF16)]

    def ffn_weights(carried):
        gw_gu, gw_dn = carried
        return gw_gu.reshape(D_MODEL, 2 * D_FF), gw_dn.reshape(D_FF, D_MODEL)

    early = {}

    def early_grads(dw_gu, dw_dn, dw_out):
        early["grads"] = [dw_out.reshape(4, 2, 128, D_MODEL), dw_gu.reshape(4, 2, 512, FF_T),
                          dw_dn.reshape(4, 2, 352, D_MODEL)]
        return _direct_rider(early["grads"])
    w_in_all = gw_in.reshape(4, D_MODEL, D_PROJ // 4).transpose(1, 0, 2).reshape(D_MODEL, D_PROJ)
    w_in_b = jnp.pad(w_in_all, ((0, 0), (0, D_PROJ_P - D_PROJ)))
    meta_all = g_meta.reshape(4, N_META, D_MODEL // 4).transpose(1, 0, 2).reshape(N_META, D_MODEL)

    loc = _local_step(x[0], loss_target[0], meta_all, rel_bias, ln_pre_mix, ln_post_mix, ln_pre_ffn, ln_post_ffn,
                      b_forget, sinks, w_in_b, _gather_rider(out_shards, True),
                      lambda carried: carried[0].reshape(D_MODEL, D_MODEL),
                      _gather_rider(ffn_shards, True, by_columns=(0,)), ffn_weights, early_grads)

    small = jnp.concatenate(
        [_pack_small(loc["ln_pre_mix"], loc["ln_post_mix"], loc["ln_pre_ffn"], loc["ln_post_ffn"],
                     loc["rel_bias"], loc["b_forget"], loc["sinks"])
         + jnp.pad(loc["loss"].reshape(1, 1), ((LOSS_ROW, 7 - LOSS_ROW), (0, D_MODEL - 1))), loc["meta"]], axis=0)

    dw_in = loc["w_in"].reshape(1, 2, D_MODEL // 2, D_PROJ_P)
    (got_in,) = _run_alone(_swap_rider([dw_in]), "swap_halves_late")
    half_sum = _pair_sum(dw_in, got_in, c_arr, "pair_sum_late")
    part_in = half_sum[0, :, :D_PROJ].reshape(D_MODEL // 2, 4, D_PROJ // 4).transpose(1, 0, 2)
    send_sems, recv_sems, part_sent, landing, small_sent, everyone, token = _late_exchange_start(part_in, small)
    chip_arr = jnp.reshape(chip, (1,)).astype(jnp.int32)
    own_arr = jnp.stack([chip, ci]).astype(jnp.int32)
    grad, delta, new_m, new_v = {}, {}, {}, {}

    def update(names, mine):
        theirs = _join_halves(mine, "join_" + names[0])
        big = dict(w_in=(w_in, m_w_in, v_w_in), w_out=(w_out, m_w_out, v_w_out),
                   w_gate_up=(w_gate_up, m_w_gate_up, v_w_gate_up), w_down=(w_down, m_w_down, v_w_down))
        for name, g_mine, g_theirs in zip(names, mine, theirs):
            w, m, v = big[name]
            g, d, nm, nv = _adamw_halves(w[0], g_mine, g_theirs, m[0], v[0], c_arr, "adamw_" + name)
            grad[name], delta[name], new_m[name], new_v[name] = g[None], d[None], nm[None], nv[None]

    update(("w_out", "w_gate_up", "w_down"),
           [_owner_sum(g, l, own_arr, token, "owner_sum_%d" % a)
            for a, (g, l) in enumerate(zip(early["grads"], loc["landed"]))])
    part_back, landed_in, small_all = _late_exchange_wait(send_sems, recv_sems, part_sent, landing, small_sent,
                                                         everyone, new_v["w_down"])
    mine_in = _chip_sum(part_back, landed_in, chip_arr, "chip_sum_in")
    (theirs_in,) = _join_halves([mine_in], "join_w_in")
    g_w_in = jnp.where(ci == 0, jnp.concatenate([mine_in, theirs_in], axis=0),
                       jnp.concatenate([theirs_in, mine_in], axis=0))
    view = lambda a: jnp.transpose(a).reshape(D_PROJ // 4 * 8, BLK)
    back = lambda a: jnp.transpose(a.reshape(D_PROJ // 4, D_MODEL))[None]
    d, nm, nv = _adamw(view(w_in[0]), view(g_w_in), view(m_w_in[0]), view(v_w_in[0]), "adamw_w_in",
                       tr=D_PROJ // 4 * 4)
    grad["w_in"], delta["w_in"], new_m["w_in"], new_v["w_in"] = g_w_in[None], back(d), back(nm), back(nv)
    me = 4 * xi + 2 * yi + ci
    small_sum = _device_sum(lax.dynamic_update_slice(small_all, small[None], (me, 0, 0)))
    g_meta_tokens = lax.dynamic_slice(small_sum[8:N_SMALL], (0, chip * (D_MODEL // 4)), (N_META, D_MODEL // 4))
    g_small = small_sum[0:8]
    grad.update(_unpack_small(g_small))
    grad.update(meta_tokens=g_meta_tokens)
    delta["meta_tokens"], new_m["meta_tokens"], new_v["meta_tokens"] = _adamw(
        meta_tokens, g_meta_tokens, m_meta_tokens, v_meta_tokens, "adamw_meta")
    d, nm, nv = _adamw(
        _pack_small(ln_pre_mix, ln_post_mix, ln_pre_ffn, ln_post_ffn, rel_bias, b_forget, sinks), g_small,
        _pack_small(m_ln_pre_mix, m_ln_post_mix, m_ln_pre_ffn, m_ln_post_ffn, m_rel_bias, m_b_forget, m_sinks),
        _pack_small(v_ln_pre_mix, v_ln_post_mix, v_ln_pre_ffn, v_ln_post_ffn, v_rel_bias, v_b_forget, v_sinks),
        "adamw_small")
    delta.update(_unpack_small(d))
    new_m.update(_unpack_small(nm))
    new_v.update(_unpack_small(nv))

    loss = small_sum[LOSS_ROW, 0]
    return (loss,loc["grad_x"][None], *[grad[k] for k in WEIGHTS], *[delta[k] for k in WEIGHTS],
            *[new_m[k] for k in WEIGHTS], *[new_v[k] for k in WEIGHTS])
```

```python
import math

import numpy as np
import jax
import jax.numpy as jnp
from jax import lax
from jax.experimental import pallas as pl
from jax.experimental.pallas import tpu as pltpu

F32 = jnp.float32
BF16 = jnp.bfloat16
MESH = pl.DeviceIdType.MESH
SDS = jax.ShapeDtypeStruct

D_MODEL = 1024
SEQ = 4096
N_META = 16
N_HEADS = 8
HALF = 64
D_FF = 2816
N_BUCKETS = 32
EPS = 1e-6
NEG = -1e30
SCALE = 0.125
LOG2E = 1.4426950408889634
LN2 = 0.6931471805599453
PAD_ROWS = 112
ROW0 = PAD_ROWS + N_META
LP = ROW0 + SEQ
BLK = 128
NBLK = LP // BLK
TM = 384
NT = LP // TM
TM_PURE = LP // 2
TM_MID = LP // 4
TM_EPI = LP // 6
TN = 256
TK_W = LP // 2
D_PROJ = 2312
D_PROJ_P = 2432
D_QKV = 2304
FF_T = 1408
VMEM_LIMIT = 56 * 1024 * 1024

ADAM_LR = 0.001
ADAM_B1 = 0.9
ADAM_B2 = 0.999
ADAM_EPS = 1e-08
ADAM_WD = 0.01
ADAM_STEP = 10

QA = 0
KA, VA = 4, 5
QB, KB, VB = 3, 5, 7
W2 = 256

NT_DIMS = (((1,), (1,)), ((), ()))
TN_DIMS = (((0,), (0,)), ((), ()))


def _cparams(sem):
    return pltpu.CompilerParams(dimension_semantics=sem, vmem_limit_bytes=VMEM_LIMIT)


def _t5_bucket_np(d):
    n = np.maximum(d, 0).astype(np.int32)
    nf = np.maximum(n, 1).astype(np.float32)
    large = 16 + (np.log(nf / np.float32(16)) / np.float32(math.log(8.0)) * np.float32(16)).astype(np.int32)
    large = np.minimum(large, N_BUCKETS - 1)
    return np.where(n < 16, n, large).astype(np.int32)


def _bucket_tables():
    qi = np.arange(BLK)[:, None]
    ki = np.arange(BLK)[None, :]
    return np.stack([_t5_bucket_np(qi - ki), _t5_bucket_np(qi - ki + BLK)])


def _rms(x):
    return lax.rsqrt(jnp.mean(x * x, axis=-1, keepdims=True) + EPS)


def _rms_bwd(n, r, gdy):
    return r * (gdy - n * jnp.mean(n * gdy, axis=-1, keepdims=True))


def _pre_mix(x, meta, gain, w_in_b):
    half = D_QKV // 2
    n_steps = LP // TM_MID

    def body(x_ref, meta_ref, g_ref, w_ref, h0_ref, hn_ref, proj_ref, f_ref, tile, sems):
        i = pl.program_id(0)

        def tile_copy(s):
            skip = ROW0 if s == 0 else 0
            return pltpu.make_async_copy(x_ref.at[s * TM_MID + skip - ROW0:(s + 1) * TM_MID - ROW0],
                                         tile.at[s % 2, skip:TM_MID], sems.at[s % 2])

        @pl.when(i == 0)
        def _():
            tile_copy(0).start()
            tile[0, :PAD_ROWS] = jnp.zeros((PAD_ROWS, D_MODEL), F32)
            tile[0, PAD_ROWS:ROW0] = meta_ref[...]

        for s in range(n_steps):
            @pl.when(i == s)
            def _():
                if s + 1 < n_steps:
                    tile_copy(s + 1).start()
                tile_copy(s).wait()

        x = tile[i % 2]
        h0_ref[...] = x
        hn = (x * _rms(x) * g_ref[...]).astype(BF16)
        hn_ref[...] = hn
        proj_ref[:, :half] = jnp.dot(hn, w_ref[:, :half], preferred_element_type=F32).astype(BF16)
        p = jnp.dot(hn, w_ref[:, half:], preferred_element_type=F32)
        proj_ref[:, half:] = p[:, :half].astype(BF16)
        f_ref[...] = p[:, half:]

    return pl.pallas_call(
        body, grid=(n_steps,),
        in_specs=[HBM_SPEC,
                  pl.BlockSpec((N_META, D_MODEL), lambda i: (0, 0)),
                  pl.BlockSpec((1, D_MODEL), lambda i: (0, 0)),
                  pl.BlockSpec((D_MODEL, D_PROJ_P), lambda i: (0, 0))],
        out_specs=[pl.BlockSpec((TM_MID, D_MODEL), lambda i: (i, 0)),
                   pl.BlockSpec((TM_MID, D_MODEL), lambda i: (i, 0)),
                   pl.BlockSpec((TM_MID, D_QKV), lambda i: (i, 0)),
                   pl.BlockSpec((TM_MID, BLK), lambda i: (i, 0))],
        out_shape=[SDS((LP, D_MODEL), F32), SDS((LP, D_MODEL), BF16), SDS((LP, D_QKV), BF16), SDS((LP, BLK), F32)],
        scratch_shapes=[pltpu.VMEM((2, TM_MID, D_MODEL), F32), pltpu.SemaphoreType.DMA((2,))],
        compiler_params=_cparams(("arbitrary",)), name="pre_mix")(x, meta, gain, w_in_b)


def _attn_out(o_a, o_b, w_out_b, h0, g_post, g_pre_ffn):
    def body(oa_ref, ob_ref, w_ref, h0_ref, gp_ref, gf_ref, a_ref, h1_ref, hn2_ref):
        a = (jnp.dot(oa_ref[...], w_ref[0:512, :], preferred_element_type=F32)
             + jnp.dot(ob_ref[...], w_ref[512:1024, :], preferred_element_type=F32))
        a_ref[...] = a
        h1 = h0_ref[...] + a * _rms(a) * gp_ref[...]
        h1_ref[...] = h1
        hn2_ref[...] = (h1 * _rms(h1) * gf_ref[...]).astype(BF16)

    row = lambda w: pl.BlockSpec((TM_EPI, w), lambda i: (i, 0))
    vec = pl.BlockSpec((1, D_MODEL), lambda i: (0, 0))
    return pl.pallas_call(
        body, grid=(LP // TM_EPI,),
        in_specs=[row(512), row(512), pl.BlockSpec((D_MODEL, D_MODEL), lambda i: (0, 0)), row(D_MODEL), vec, vec],
        out_specs=[row(D_MODEL), row(D_MODEL), row(D_MODEL)],
        out_shape=[SDS((LP, D_MODEL), F32), SDS((LP, D_MODEL), F32), SDS((LP, D_MODEL), BF16)],
        compiler_params=_cparams(("parallel",)), name="attn_out")(o_a, o_b, w_out_b, h0, g_post, g_pre_ffn)


def _ffn_up(hn2, w_gu_b):
    def body(x_ref, wg_ref, wu_ref, g_ref, u_ref, act_ref):
        x = x_ref[...]
        g = jnp.dot(x, wg_ref[...], preferred_element_type=F32)
        u = jnp.dot(x, wu_ref[...], preferred_element_type=F32)
        g_ref[...] = g.astype(BF16)
        u_ref[...] = u.astype(BF16)
        act_ref[...] = (g * (1.0 / (1.0 + jnp.exp(-g))) * u).astype(BF16)

    out = pl.BlockSpec((LP, TN), lambda j: (0, j))
    return pl.pallas_call(
        body, grid=(D_FF // TN,),
        in_specs=[pl.BlockSpec((LP, D_MODEL), lambda j: (0, 0)),
                  pl.BlockSpec((D_MODEL, TN), lambda j: (0, j)),
                  pl.BlockSpec((D_MODEL, TN), lambda j: (0, j + D_FF // TN))],
        out_specs=[out, out, out],
        out_shape=[SDS((LP, D_FF), BF16)] * 3,
        compiler_params=_cparams(("parallel",)), name="ffn_up")(hn2, w_gu_b, w_gu_b)


def _ffn_down_loss(act, w_dn_b, h1, tgt, g_post_ffn):
    def body(act_ref, w_ref, h1_ref, t0_ref, t1_ref, t2_ref, g_ref, dff_ref, dy_ref, loss_ref, dg_ref):
        i = pl.program_id(0)
        target = jnp.concatenate([t0_ref[...], t1_ref[...], t2_ref[...]], axis=0)

        @pl.when(i == 0)
        def _():
            loss_ref[...] = jnp.zeros_like(loss_ref)
            dg_ref[...] = jnp.zeros_like(dg_ref)

        ff = jnp.dot(act_ref[...], w_ref[...], preferred_element_type=F32)
        r = _rms(ff)
        n = ff * r
        g = g_ref[...]
        y = h1_ref[...] + n * g
        rows = i * TM + lax.broadcasted_iota(jnp.int32, (TM, D_MODEL), 0)
        diff = jnp.where(rows >= ROW0, y - target, 0.0)
        loss_ref[...] += 0.5 * jnp.sum(diff * diff) / D_MODEL
        dy = diff / D_MODEL
        dy_ref[...] = dy
        dg_ref[...] += jnp.sum(dy * n, axis=0, keepdims=True)
        dff_ref[...] = _rms_bwd(n, r, g * dy).astype(BF16)

    row = pl.BlockSpec((TM, D_MODEL), lambda i: (i, 0))
    tblk = lambda j: pl.BlockSpec((BLK, D_MODEL), lambda i: (jnp.maximum(3 * i - 1 + j, 0), 0))
    return pl.pallas_call(
        body, grid=(NT,),
        in_specs=[pl.BlockSpec((TM, D_FF), lambda i: (i, 0)), pl.BlockSpec((D_FF, D_MODEL), lambda i: (0, 0)),
                  row, tblk(0), tblk(1), tblk(2), pl.BlockSpec((1, D_MODEL), lambda i: (0, 0))],
        out_specs=[row, row, pl.BlockSpec((8, BLK), lambda i: (0, 0)), pl.BlockSpec((1, D_MODEL), lambda i: (0, 0))],
        out_shape=[SDS((LP, D_MODEL), BF16), SDS((LP, D_MODEL), F32), SDS((8, BLK), F32), SDS((1, D_MODEL), F32)],
        compiler_params=_cparams(("arbitrary",)), name="ffn_down_loss")(act, w_dn_b, h1, tgt, tgt, tgt, g_post_ffn)


def _ffn_down_bwd(dff, w_dn_b, g, u):
    def body(d_ref, w_ref, g_ref, u_ref, dg_ref, du_ref):
        dact = lax.dot_general(d_ref[...], w_ref[...], NT_DIMS, preferred_element_type=F32)
        gg = g_ref[...].astype(F32)
        sig = 1.0 / (1.0 + jnp.exp(-gg))
        dg_ref[...] = (dact * u_ref[...].astype(F32) * sig * (1.0 + gg * (1.0 - sig))).astype(BF16)
        du_ref[...] = (dact * gg * sig).astype(BF16)

    blk = pl.BlockSpec((LP, TN), lambda j: (0, j))
    return pl.pallas_call(
        body, grid=(D_FF // TN,),
        in_specs=[pl.BlockSpec((LP, D_MODEL), lambda j: (0, 0)),
                  pl.BlockSpec((TN, D_MODEL), lambda j: (j, 0)), blk, blk],
        out_specs=[blk, blk],
        out_shape=[SDS((LP, D_FF), BF16)] * 2,
        compiler_params=_cparams(("parallel",)), name="ffn_down_bwd")(dff, w_dn_b, g, u)


def _ffn_up_bwd(dg, du, w_gu_b, h1, a, dy, g_pre_ffn, g_post_mix):
    def body(dg_ref, du_ref, w_ref, h1_ref, a_ref, dy_ref, gf_ref, gp_ref,
             dh1_ref, da_ref, dgf_ref, dgp_ref, acc):
        i = pl.program_id(0)
        s = pl.program_id(1)

        @pl.when((i == 0) & (s == 0))
        def _():
            dgf_ref[...] = jnp.zeros_like(dgf_ref)
            dgp_ref[...] = jnp.zeros_like(dgp_ref)

        @pl.when(s == 0)
        def _():
            acc[...] = jnp.zeros_like(acc)

        @pl.when(s < 2)
        def _():
            acc[...] += lax.dot_general(dg_ref[...], w_ref[...], NT_DIMS, preferred_element_type=F32)

        @pl.when(s >= 2)
        def _():
            acc[...] += lax.dot_general(du_ref[...], w_ref[...], NT_DIMS, preferred_element_type=F32)

        @pl.when(s == 3)
        def _():
            dhn2 = acc[...]
            h1 = h1_ref[...]
            r2 = _rms(h1)
            n2 = h1 * r2
            dgf_ref[...] += jnp.sum(dhn2 * n2, axis=0, keepdims=True)
            dh1 = dy_ref[...] + _rms_bwd(n2, r2, gf_ref[...] * dhn2)
            dh1_ref[...] = dh1
            av = a_ref[...]
            ra = _rms(av)
            na = av * ra
            dgp_ref[...] += jnp.sum(dh1 * na, axis=0, keepdims=True)
            da_ref[...] = _rms_bwd(na, ra, gp_ref[...] * dh1).astype(BF16)

    row = pl.BlockSpec((TM_EPI, D_MODEL), lambda i, s: (i, 0))
    vec = pl.BlockSpec((1, D_MODEL), lambda i, s: (0, 0))
    return pl.pallas_call(
        body, grid=(LP // TM_EPI, 4),
        in_specs=[pl.BlockSpec((TM_EPI, FF_T), lambda i, s: (i, jnp.minimum(s, 1))),
                  pl.BlockSpec((TM_EPI, FF_T), lambda i, s: (i, jnp.maximum(s - 2, 0))),
                  pl.BlockSpec((D_MODEL, FF_T), lambda i, s: (0, s)),
                  row, row, row, vec, vec],
        out_specs=[row, row, vec, vec],
        out_shape=[SDS((LP, D_MODEL), F32), SDS((LP, D_MODEL), BF16), SDS((1, D_MODEL), F32), SDS((1, D_MODEL), F32)],
        scratch_shapes=[pltpu.VMEM((TM_EPI, D_MODEL), F32)],
        compiler_params=_cparams(("arbitrary", "arbitrary")), name="ffn_up_bwd",
    )(dg, du, w_gu_b, h1, a, dy, g_pre_ffn, g_post_mix)


def _attn_out_bwd(da, w_out_b):
    def body(d_ref, w_ref, o_ref):
        o_ref[...] = lax.dot_general(d_ref[...], w_ref[...], NT_DIMS, preferred_element_type=F32).astype(BF16)

    row = pl.BlockSpec((TM_PURE, D_MODEL), lambda i: (i, 0))
    return pl.pallas_call(
        body, grid=(LP // TM_PURE,),
        in_specs=[row, pl.BlockSpec((D_MODEL, D_MODEL), lambda i: (0, 0))],
        out_specs=row, out_shape=SDS((LP, D_MODEL), BF16),
        compiler_params=_cparams(("parallel",)), name="attn_out_bwd")(da, w_out_b)


def _pre_mix_bwd(dq_a, dq_b, dk_b, dv_b, dk_a, dv_a, df, w_in_b, h0, dh1, g_pre_mix):
    n_steps = LP // TM_EPI

    def body(qa_ref, qb_ref, kb_ref, vb_ref, ka_ref, va_ref, f_ref, w_ref, h0_ref, dh1_ref, g_ref,
             dproj_ref, dx_ref, dmeta_ref, dg_ref, tile, sems):
        i = pl.program_id(0)

        def tile_copy(s):
            skip = ROW0 if s == 0 else 0
            return pltpu.make_async_copy(tile.at[s % 2, skip:TM_EPI],
                                         dx_ref.at[s * TM_EPI + skip - ROW0:(s + 1) * TM_EPI - ROW0],
                                         sems.at[s % 2])

        @pl.when(i == 0)
        def _():
            dg_ref[...] = jnp.zeros_like(dg_ref)

        for s in range(2, n_steps):
            @pl.when(i == s)
            def _():
                tile_copy(s - 2).wait()

        dproj = jnp.concatenate(
            [qa_ref[...], ka_ref[...].astype(BF16), va_ref[...].astype(BF16), (qb_ref[...] * SCALE).astype(BF16),
             kb_ref[...], vb_ref[...], f_ref[...].astype(BF16)], axis=1)
        dproj_ref[...] = dproj
        dhn = lax.dot_general(dproj, w_ref[...], NT_DIMS, preferred_element_type=F32)
        x = h0_ref[...]
        r = _rms(x)
        n = x * r
        dg_ref[...] += jnp.sum(dhn * n, axis=0, keepdims=True)
        tile[i % 2] = dh1_ref[...] + _rms_bwd(n, r, g_ref[...] * dhn)

        for s in range(n_steps):
            @pl.when(i == s)
            def _():
                tile_copy(s).start()
                if s == 0:
                    dmeta_ref[...] = tile[0, PAD_ROWS:ROW0]
                if s == n_steps - 1:
                    tile_copy(s - 1).wait()
                    tile_copy(s).wait()

    row = lambda w: pl.BlockSpec((TM_EPI, w), lambda i: (i, 0))
    vec = pl.BlockSpec((1, D_MODEL), lambda i: (0, 0))
    return pl.pallas_call(
        body, grid=(n_steps,),
        in_specs=[row(512), row(512), row(512), row(512), row(BLK), row(BLK), row(BLK),
                  pl.BlockSpec((D_MODEL, D_PROJ_P), lambda i: (0, 0)), row(D_MODEL), row(D_MODEL), vec],
        out_specs=[row(D_PROJ_P), HBM_SPEC, pl.BlockSpec((N_META, D_MODEL), lambda i: (0, 0)), vec],
        out_shape=[SDS((LP, D_PROJ_P), BF16), SDS((SEQ, D_MODEL), F32), SDS((N_META, D_MODEL), F32),
                   SDS((1, D_MODEL), F32)],
        scratch_shapes=[pltpu.VMEM((2, TM_EPI, D_MODEL), F32), pltpu.SemaphoreType.DMA((2,))],
        compiler_params=_cparams(("arbitrary",)), name="pre_mix_bwd",
    )(dq_a, dq_b, dk_b, dv_b, dk_a, dv_a, df, w_in_b, h0, dh1, g_pre_mix)


def _mm_tn(parts, b, tm, name, out_dtype=F32):
    widths = [p.shape[1] for p in parts]
    m_total = sum(widths)
    n = b.shape[1]
    whole = len(parts) > 1
    n_k = LP // TK_W
    assert (tm == m_total) if whole else (m_total % tm == 0)

    def body(*refs):
        a_refs, b_ref, o_ref, acc = refs[:-3], refs[-3], refs[-2], refs[-1]
        k = pl.program_id(1)

        @pl.when(k == 0)
        def _():
            acc[...] = jnp.zeros_like(acc)
        a = a_refs[0][...] if not whole else jnp.concatenate([r[...] for r in a_refs], axis=1)
        acc[...] += lax.dot_general(a, b_ref[...], TN_DIMS, preferred_element_type=F32)

        @pl.when(k == n_k - 1)
        def _():
            o_ref[...] = acc[...].astype(out_dtype)

    a_specs = ([pl.BlockSpec((TK_W, w), lambda mi, k: (k, 0)) for w in widths] if whole
               else [pl.BlockSpec((TK_W, tm), lambda mi, k: (k, mi))])
    return pl.pallas_call(
        body, grid=(m_total // tm, n_k),
        in_specs=a_specs + [pl.BlockSpec((TK_W, n), lambda mi, k: (k, 0))],
        out_specs=pl.BlockSpec((tm, n), lambda mi, k: (mi, 0)),
        out_shape=SDS((m_total, n), out_dtype),
        scratch_shapes=[pltpu.VMEM((tm, n), F32)],
        compiler_params=_cparams(("parallel", "arbitrary")), name=name)(*parts, b)


def _dw_gate_up(hn2, dg, du):
    n_k = LP // TK_W

    def body(a_ref, dg_ref, du_ref, o_ref, acc):
        s = pl.program_id(0)
        k = pl.program_id(1)

        @pl.when(k == 0)
        def _():
            acc[...] = jnp.zeros_like(acc)

        @pl.when(s < 2)
        def _():
            acc[...] += lax.dot_general(a_ref[...], dg_ref[...], TN_DIMS, preferred_element_type=F32)

        @pl.when(s >= 2)
        def _():
            acc[...] += lax.dot_general(a_ref[...], du_ref[...], TN_DIMS, preferred_element_type=F32)

        @pl.when(k == n_k - 1)
        def _():
            o_ref[0] = acc[...].astype(BF16)

    return pl.pallas_call(
        body, grid=(4, n_k),
        in_specs=[pl.BlockSpec((TK_W, D_MODEL), lambda s, k: (k, 0)),
                  pl.BlockSpec((TK_W, FF_T), lambda s, k: (k, jnp.minimum(s, 1))),
                  pl.BlockSpec((TK_W, FF_T), lambda s, k: (k, jnp.maximum(s - 2, 0)))],
        out_specs=pl.BlockSpec((1, D_MODEL, FF_T), lambda s, k: (s, 0, 0)),
        out_shape=SDS((4, D_MODEL, FF_T), BF16),
        scratch_shapes=[pltpu.VMEM((D_MODEL, FF_T), F32)],
        compiler_params=_cparams(("parallel", "arbitrary")), name="dw_gate_up")(hn2, dg, du)


def _split3(x):
    hi = x.astype(BF16)
    r1 = x - hi.astype(F32)
    mid = r1.astype(BF16)
    lo = (r1 - mid.astype(F32)).astype(BF16)
    return hi, mid, lo


def _tri_matmul(tri, x):
    hi, mid, lo = _split3(x)
    dot = lambda t: jnp.dot(tri, t, preferred_element_type=F32)
    return dot(hi) + dot(mid) + dot(lo)


def _forget_cumsum(f, b_forget_p):
    def body(f_ref, b_ref, cum_ref, carry):
        i = pl.program_id(0)

        @pl.when(i == 0)
        def _():
            carry[...] = jnp.zeros_like(carry)

        z = f_ref[...] + b_ref[...]
        ls = jnp.minimum(z, 0.0) - jnp.log(1.0 + jnp.exp(-jnp.abs(z)))
        rows = i * TM + lax.broadcasted_iota(jnp.int32, (TM, BLK), 0)
        ls = jnp.where(rows >= PAD_ROWS, ls, 0.0)
        r = lax.broadcasted_iota(jnp.int32, (TM, TM), 0)
        c = lax.broadcasted_iota(jnp.int32, (TM, TM), 1)
        tri = (c <= r).astype(BF16)
        cum = _tri_matmul(tri, ls) + carry[...]
        cum_ref[...] = cum
        carry[...] = cum[TM - 1:TM, :]

    return pl.pallas_call(
        body, grid=(NT,),
        in_specs=[pl.BlockSpec((TM, BLK), lambda i: (i, 0)), pl.BlockSpec((1, BLK), lambda i: (0, 0))],
        out_specs=pl.BlockSpec((TM, BLK), lambda i: (i, 0)),
        out_shape=SDS((LP, BLK), F32),
        scratch_shapes=[pltpu.VMEM((1, BLK), F32)],
        compiler_params=_cparams(("arbitrary",)), name="forget_cumsum")(f, b_forget_p)


def _forget_cumsum_bwd(dcum, f, b_forget_p):
    def body(d_ref, f_ref, b_ref, df_ref, db_ref, carry):
        i = pl.program_id(0)

        @pl.when(i == 0)
        def _():
            carry[...] = jnp.zeros_like(carry)
            db_ref[...] = jnp.zeros_like(db_ref)

        blk = NT - 1 - i
        r = lax.broadcasted_iota(jnp.int32, (TM, TM), 0)
        c = lax.broadcasted_iota(jnp.int32, (TM, TM), 1)
        tri = (c >= r).astype(BF16)
        d = d_ref[...]
        dls = _tri_matmul(tri, d) + carry[...]
        carry[...] = dls[0:1, :]
        z = f_ref[...] + b_ref[...]
        rows = blk * TM + lax.broadcasted_iota(jnp.int32, (TM, BLK), 0)
        df = jnp.where(rows >= PAD_ROWS, dls / (1.0 + jnp.exp(z)), 0.0)
        df_ref[...] = df
        db_ref[...] += jnp.sum(df, axis=0, keepdims=True)

    rev = pl.BlockSpec((TM, BLK), lambda i: (NT - 1 - i, 0))
    vec = pl.BlockSpec((1, BLK), lambda i: (0, 0))
    return pl.pallas_call(
        body, grid=(NT,),
        in_specs=[rev, rev, vec],
        out_specs=[rev, vec],
        out_shape=[SDS((LP, BLK), F32), SDS((1, BLK), F32)],
        scratch_shapes=[pltpu.VMEM((1, BLK), F32)],
        compiler_params=_cparams(("arbitrary",)), name="forget_cumsum_bwd")(dcum, f, b_forget_p)


def _lane_half(rows):
    return lax.broadcasted_iota(jnp.int32, (rows, BLK), 1) // HALF


def _fox_valid(qi, kj):
    qrow = qi * TM + lax.broadcasted_iota(jnp.int32, (TM, TM), 0)
    krow = kj * TM + lax.broadcasted_iota(jnp.int32, (TM, TM), 1)
    return (krow <= qrow) & ((krow >= PAD_ROWS) | (qrow < PAD_ROWS))


class _Rider:
    def __init__(self, operands, out_shapes, sem_counts, first, middle, last):
        self.operands, self.out_shapes, self.sem_counts = list(operands), list(out_shapes), list(sem_counts)
        self.first, self.middle, self.last = first, middle, last

    def scratch(self):
        return [pltpu.SemaphoreType.DMA((k,)) for k in self.sem_counts]

    def split(self, refs, n_in, n_out, n_scratch):
        a, b = len(self.operands), len(self.out_shapes)
        ins, mine_in = refs[:n_in], refs[n_in:n_in + a]
        outs, mine_out = refs[n_in + a:n_in + a + n_out], refs[n_in + a + n_out:n_in + a + n_out + b]
        rest = refs[n_in + a + n_out + b:]
        return ins, outs, rest[:n_scratch], (mine_in, mine_out, rest[n_scratch:])

    def at_steps(self, mine, is_first, is_middle, is_last):
        for cond, fn in ((is_first, self.first), (is_middle, self.middle), (is_last, self.last)):
            pl.when(cond)(lambda fn=fn: fn(*mine))


HBM_SPEC = pl.BlockSpec(memory_space=pltpu.HBM)


N_AUG = 3
QCHUNKS = ((0, 128), (128, 128), (256, 128))
KSUB = 384
AHEAD = 5
AHEAD_BWD = 1


def _fox_prep(proj, cum):
    def body(q0_ref, q1_ref, k0_ref, k1_ref, v0_ref, v1_ref, c_ref, qa_ref, ka_ref, vt_ref):
        half = _lane_half(TM)
        lane = lax.broadcasted_iota(jnp.int32, (TM, BLK), 1)
        for pp in range(4):
            cols = slice(pp * BLK, (pp + 1) * BLK)
            q_ref, k_ref, v_ref = ((q0_ref, k0_ref, v0_ref), (q1_ref, k1_ref, v1_ref))[pp // 2]
            part = slice((pp % 2) * BLK, (pp % 2 + 1) * BLK)
            qs = q_ref[:, part].astype(F32) * (SCALE * LOG2E)
            kp = k_ref[:, part].astype(F32)
            vp = v_ref[:, part]
            vt_ref[cols, :] = vp.astype(F32).T.astype(BF16)
            for e in range(2):
                h = 2 * pp + e
                a = (1 - e) * HALF
                blk = slice(h * BLK, (h + 1) * BLK)
                hi, mid, lo = _split3(-LOG2E * c_ref[:, h:h + 1])
                q_aug = jnp.where(half == e, qs, jnp.where((lane >= a) & (lane < a + N_AUG), 1.0, 0.0))
                k_aug = jnp.where(half == e, kp, jnp.where(
                    lane == a, hi.astype(F32), jnp.where(lane == a + 1, mid.astype(F32), jnp.where(
                        lane == a + 2, lo.astype(F32), 0.0))))
                qa_ref[blk, :] = q_aug.T.astype(BF16)
                ka_ref[:, blk] = k_aug.astype(BF16)

    row = lambda blk: pl.BlockSpec((TM, W2), lambda i: (i, blk))
    wide = pl.BlockSpec((TM, 1024), lambda i: (i, 0))
    return pl.pallas_call(
        body, grid=(NT,),
        in_specs=[row(QB), row(QB + 1), row(KB), row(KB + 1), row(VB), row(VB + 1),
                  pl.BlockSpec((TM, BLK), lambda i: (i, 0))],
        out_specs=[pl.BlockSpec((1024, TM), lambda i: (0, i)), wide, pl.BlockSpec((512, TM), lambda i: (0, i))],
        out_shape=[SDS((1024, LP), BF16), SDS((LP, 1024), BF16), SDS((512, LP), BF16)],
        compiler_params=_cparams(("parallel",)), name="fox_prep")(proj, proj, proj, proj, proj, proj, cum)


def _over_keys(reduce, x):
    slabs = x.reshape(x.shape[0] // HALF, HALF, x.shape[1])
    return reduce(reduce(slabs, axis=0), axis=0, keepdims=True)


def _fox_valid_t(qi, kj, c, r):
    krow = kj * TM + r * KSUB + lax.broadcasted_iota(jnp.int32, (KSUB, c[1]), 0)
    qrow = qi * TM + c[0] + lax.broadcasted_iota(jnp.int32, (KSUB, c[1]), 1)
    return (krow <= qrow) & ((krow >= PAD_ROWS) | (qrow < PAD_ROWS))


def _fox_fwd(q_aug, k_aug, v_t, rider):
    pairs = [(qi, kj) for qi in range(NT) for kj in range(qi + 1)]
    n_pairs = len(pairs)

    def body(qi_ref, kj_ref, *refs):
        (q_ref, k_ref, vt_ref), (o_ref, lse_ref), (m_s, l_s, acc_s), mine = rider.split(refs, 3, 2, 3)
        n = pl.program_id(0)
        qi = qi_ref[n]
        kj = kj_ref[n]
        rider.at_steps(mine, n == 0, n == n_pairs // 2, n == n_pairs - 1)

        @pl.when(kj == 0)
        def _():
            m_s[...] = jnp.full_like(m_s, NEG)
            l_s[...] = jnp.zeros_like(l_s)
            acc_s[...] = jnp.zeros_like(acc_s)

        def tile(masked):
            steps = [(h, c, r) for h in range(N_HEADS) for c in QCHUNKS for r in range(TM // KSUB)]

            def scores(h, c, r):
                blk = slice(h * BLK, (h + 1) * BLK)
                return jnp.dot(k_ref[r * KSUB:(r + 1) * KSUB, blk], q_ref[blk, c[0]:c[0] + c[1]],
                               preferred_element_type=F32)

            ahead = [scores(*st) for st in steps[:AHEAD]]
            for n, (h, c, r) in enumerate(steps):
                s_t = ahead.pop(0)
                if n + AHEAD < len(steps):
                    ahead.append(scores(*steps[n + AHEAD]))
                cs = slice(c[0], c[0] + c[1])
                if masked:
                    s_t = jnp.where(_fox_valid_t(qi, kj, c, r), s_t, NEG)
                m_prev = m_s[h, :, cs]
                m_new = jnp.maximum(m_prev, _over_keys(jnp.max, s_t))
                p_t = jnp.exp2(s_t - m_new)
                alpha = jnp.exp2(m_prev - m_new)
                l_s[h, :, cs] = alpha * l_s[h, :, cs] + _over_keys(jnp.sum, p_t)
                m_s[h, :, cs] = m_new
                vt = vt_ref[h * HALF:(h + 1) * HALF, r * KSUB:(r + 1) * KSUB]
                acc_s[h, :, cs] = acc_s[h, :, cs] * alpha + jnp.dot(vt, p_t.astype(BF16),
                                                                    preferred_element_type=F32)

        @pl.when((kj < qi) & (kj > 0))
        def _():
            tile(False)

        @pl.when((kj == qi) | (kj == 0))
        def _():
            tile(True)

        @pl.when(kj == qi)
        def _():
            for pp in range(4):
                both = jnp.concatenate([acc_s[2 * pp] * (1.0 / l_s[2 * pp]),
                                        acc_s[2 * pp + 1] * (1.0 / l_s[2 * pp + 1])], axis=0)
                o_ref[:, pp * BLK:(pp + 1) * BLK] = both.T.astype(BF16)
            for h in range(N_HEADS):
                lse_ref[h] = m_s[h] * LN2 + jnp.log(l_s[h])

    grid_spec = pltpu.PrefetchScalarGridSpec(
        num_scalar_prefetch=2, grid=(n_pairs,),
        in_specs=[pl.BlockSpec((1024, TM), lambda n, qi, kj: (0, qi[n])),
                  pl.BlockSpec((TM, 1024), lambda n, qi, kj: (kj[n], 0)),
                  pl.BlockSpec((512, TM), lambda n, qi, kj: (0, kj[n]))] + [HBM_SPEC] * len(rider.operands),
        out_specs=[pl.BlockSpec((TM, 512), lambda n, qi, kj: (qi[n], 0)),
                   pl.BlockSpec((N_HEADS, 1, TM), lambda n, qi, kj: (0, 0, qi[n]))]
        + [HBM_SPEC] * len(rider.out_shapes),
        scratch_shapes=[pltpu.VMEM((N_HEADS, 1, TM), F32), pltpu.VMEM((N_HEADS, 1, TM), F32),
                        pltpu.VMEM((N_HEADS, HALF, TM), F32)] + rider.scratch())
    o_b, lse, *carried = pl.pallas_call(
        body, grid_spec=grid_spec,
        out_shape=[SDS((LP, 512), BF16), SDS((N_HEADS, 1, LP), F32)] + rider.out_shapes,
        compiler_params=_cparams(("arbitrary",)), name="fox_fwd",
    )(jnp.asarray([p[0] for p in pairs], jnp.int32), jnp.asarray([p[1] for p in pairs], jnp.int32),
      q_aug, k_aug, v_t, *rider.operands)
    return o_b, lse, carried


def _fox_bwd(proj, o_b, dmix, lse, ck_t, rider):
    pairs = [(kj, qi) for kj in range(NT) for qi in range(kj, NT)]
    n_pairs = len(pairs)

    def body(kj_ref, qi_ref, *refs):
        ((q0_ref, q1_ref, k0_ref, k1_ref, v0_ref, v1_ref, o_ref, do_ref, lse_ref, ck_ref),
         (dq_ref, dk_ref, dv_ref, dck_ref, dcq_ref), (dk_s, dv_s, dck_s), mine) = rider.split(refs, 10, 5, 3)
        n = pl.program_id(0)
        kj = kj_ref[n]
        qi = qi_ref[n]
        rider.at_steps(mine, n == 0, n == n_pairs // 2, n == n_pairs - 1)

        @pl.when(n == 0)
        def _():
            dq_ref[...] = jnp.zeros_like(dq_ref)
            dcq_ref[...] = jnp.zeros_like(dcq_ref)

        @pl.when(qi == kj)
        def _():
            dk_s[...] = jnp.zeros_like(dk_s)
            dv_s[...] = jnp.zeros_like(dv_s)
            dck_s[...] = jnp.zeros_like(dck_s)

        def tile(masked):
            valid = _fox_valid(qi, kj) if masked else None
            half = _lane_half(TM)
            q0 = pl.multiple_of(qi * TM, TM)
            lane = lax.broadcasted_iota(jnp.int32, (TM, BLK), 1)
            row_sums = jnp.zeros((TM, BLK), F32)
            pair_ops = {}

            def operands(pp):
                if pp not in pair_ops:
                    cols = slice(pp * BLK, (pp + 1) * BLK)
                    q_ref, k_ref, v_ref = ((q0_ref, k0_ref, v0_ref), (q1_ref, k1_ref, v1_ref))[pp // 2]
                    part = slice((pp % 2) * BLK, (pp % 2 + 1) * BLK)
                    pair_ops[pp] = ((q_ref[:, part].astype(F32) * SCALE).astype(BF16), k_ref[:, part],
                                    v_ref[:, part], do_ref[:, cols])
                return pair_ops[pp]

            def scores(pp, e):
                qs, kp, vp, dop = operands(pp)
                ke = jnp.where(half == e, kp, jnp.zeros_like(kp))
                ve = jnp.where(half == e, vp, jnp.zeros_like(vp))
                return (lax.dot_general(qs, ke, NT_DIMS, preferred_element_type=F32),
                        lax.dot_general(dop, ve, NT_DIMS, preferred_element_type=F32), ke)

            steps = [(pp, e) for pp in range(4) for e in range(2)]
            ahead = [scores(*st) for st in steps[:AHEAD_BWD]]
            for n, (pp, e) in enumerate(steps):
                raw, dp, ke = ahead.pop(0)
                if n + AHEAD_BWD < len(steps):
                    ahead.append(scores(*steps[n + AHEAD_BWD]))
                h = 2 * pp + e
                cols = slice(pp * BLK, (pp + 1) * BLK)
                qs, kp, vp, dop = operands(pp)
                if e == 0:
                    prod = dop.astype(F32) * o_ref[:, cols].astype(F32)
                    d0 = jnp.sum(jnp.where(half == 0, prod, 0.0), axis=1, keepdims=True)
                    d1 = jnp.sum(prod, axis=1, keepdims=True) - d0
                    dq = jnp.zeros((TM, BLK), F32)
                    dks, dvs = [], []
                t = raw - ck_ref[h] - lse_ref[h]
                if masked:
                    t = jnp.where(valid, t, NEG)
                p = jnp.exp(t)
                ds = p * (dp - (d0 if e == 0 else d1))
                dck_s[h] += jnp.sum(ds, axis=0, keepdims=True)
                row_sums = jnp.where(lane == h, jnp.sum(ds, axis=1, keepdims=True), row_sums)
                ds_b = ds.astype(BF16)
                dq = dq + jnp.dot(ds_b, ke, preferred_element_type=F32)
                dks.append(lax.dot_general(ds_b, qs, TN_DIMS, preferred_element_type=F32))
                dvs.append(lax.dot_general(p.astype(BF16), dop, TN_DIMS, preferred_element_type=F32))
                if e == 1:
                    dq_ref[pl.ds(q0, TM), cols] += dq
                    dk_s[pp] += jnp.where(half == 0, dks[0], dks[1])
                    dv_s[pp] += jnp.where(half == 0, dvs[0], dvs[1])
            dcq_ref[pl.ds(q0, TM), :] += row_sums

        @pl.when((qi > kj) & (kj > 0))
        def _():
            tile(False)

        @pl.when((qi == kj) | (kj == 0))
        def _():
            tile(True)

        @pl.when(qi == NT - 1)
        def _():
            for pp in range(4):
                cols = slice(pp * BLK, (pp + 1) * BLK)
                dk_ref[:, cols] = dk_s[pp].astype(BF16)
                dv_ref[:, cols] = dv_s[pp].astype(BF16)
            dck_ref[...] = dck_s[...]

    qrow = lambda blk, w=512: pl.BlockSpec((TM, w), lambda n, kj, qi: (qi[n], blk))
    krow = lambda blk: pl.BlockSpec((TM, W2), lambda n, kj, qi: (kj[n], blk))
    grid_spec = pltpu.PrefetchScalarGridSpec(
        num_scalar_prefetch=2, grid=(n_pairs,),
        in_specs=[qrow(QB, W2), qrow(QB + 1, W2), krow(KB), krow(KB + 1), krow(VB), krow(VB + 1), qrow(0), qrow(1),
                  pl.BlockSpec((N_HEADS, TM, 1), lambda n, kj, qi: (0, qi[n], 0)),
                  pl.BlockSpec((N_HEADS, 1, TM), lambda n, kj, qi: (0, 0, kj[n]))] + [HBM_SPEC] * len(rider.operands),
        out_specs=[pl.BlockSpec((LP, 512), lambda n, kj, qi: (0, 0)),
                   pl.BlockSpec((TM, 512), lambda n, kj, qi: (kj[n], 0)),
                   pl.BlockSpec((TM, 512), lambda n, kj, qi: (kj[n], 0)),
                   pl.BlockSpec((N_HEADS, 1, TM), lambda n, kj, qi: (0, 0, kj[n])),
                   pl.BlockSpec((LP, BLK), lambda n, kj, qi: (0, 0))] + [HBM_SPEC] * len(rider.out_shapes),
        scratch_shapes=[pltpu.VMEM((4, TM, BLK), F32), pltpu.VMEM((4, TM, BLK), F32),
                        pltpu.VMEM((N_HEADS, 1, TM), F32)] + rider.scratch())
    dq, dk, dv, dck, dcq, *carried = pl.pallas_call(
        body, grid_spec=grid_spec,
        out_shape=[SDS((LP, 512), F32), SDS((LP, 512), BF16), SDS((LP, 512), BF16), SDS((N_HEADS, 1, LP), F32),
                   SDS((LP, BLK), F32)] + rider.out_shapes,
        compiler_params=_cparams(("arbitrary",)), name="fox_bwd",
    )(jnp.asarray([p[0] for p in pairs], jnp.int32), jnp.asarray([p[1] for p in pairs], jnp.int32),
      proj, proj, proj, proj, proj, proj, o_b, dmix, lse, ck_t, *rider.operands)
    return dq, dk, dv, dck, dcq, carried


N_SEG = 3
N_KEY = N_SEG * BLK
GROUP = 4
QW = GROUP * BLK


def _bucket_tables_t():
    return np.ascontiguousarray(_bucket_tables().transpose(0, 2, 1))


def _stack_heads(ref, g, scale):
    half = _lane_half(BLK)
    out = []
    for pair in range(2):
        x = ref[:, (2 * g + pair) * BLK:(2 * g + pair + 1) * BLK].astype(F32) * scale
        swapped = pltpu.roll(x, HALF, 1)
        for e in range(2):
            out.append(jnp.where(half == g, x if e == g else swapped, 0.0).astype(BF16))
    return jnp.concatenate(out, axis=0)


def _unstack_heads(x_t, g, ref, scale):
    for pair in range(2):
        both = jnp.concatenate([x_t[:, (2 * pair) * BLK:(2 * pair + 1) * BLK],
                                x_t[:, (2 * pair + 1) * BLK:(2 * pair + 2) * BLK]], axis=0)
        ref[:, (2 * g + pair) * BLK:(2 * g + pair + 1) * BLK] = (both.T * scale).astype(ref.dtype)


def _swa_tables(tab_ref, sink_ref, bkt_ref, tbl, sink_row):
    kk = lax.broadcasted_iota(jnp.int32, (BLK, BLK), 0)
    qq = lax.broadcasted_iota(jnp.int32, (BLK, BLK), 1)
    neg = jnp.full((BLK, BLK), NEG, F32)
    lane = lax.broadcasted_iota(jnp.int32, (1, QW), 1) // BLK
    for g in range(2):
        row = jnp.zeros((1, QW), F32)
        for hh in range(GROUP):
            h = GROUP * g + hh
            cols = slice(hh * BLK, (hh + 1) * BLK)
            row = jnp.where(lane == hh, sink_ref[0, h], row)

            def step(b, carry, h=h):
                t = tab_ref[b, h]
                return jnp.where(bkt_ref[0] == b, t, carry[0]), jnp.where(bkt_ref[1] == b, t, carry[1])
            zero = jnp.zeros((BLK, BLK), F32)
            cur, prev = lax.fori_loop(0, N_BUCKETS, step, (zero, zero))
            far = jnp.full((BLK, BLK), tab_ref[N_BUCKETS - 1, h], F32)
            causal = jnp.where(kk <= qq, cur, neg)
            segments = [
                (neg, neg, jnp.where(kk >= PAD_ROWS, causal, neg)),
                (jnp.where(kk >= PAD_ROWS, prev, neg), neg, causal),
                (jnp.where(kk >= PAD_ROWS, far, neg), jnp.where(kk > qq, prev, neg), causal)]
            for case in range(3):
                for seg in range(N_SEG):
                    tbl[case, g, seg * BLK:(seg + 1) * BLK, cols] = segments[case][seg]
        sink_row[g] = row


def _swa_prep(proj):
    rows = LP // 3

    def body(k_ref, v_ref, kt_ref, vt_ref):
        kt_ref[...] = k_ref[...].astype(F32).T.astype(BF16)
        vt_ref[...] = v_ref[...].astype(F32).T.astype(BF16)

    col = pl.BlockSpec((BLK, rows), lambda i: (0, i))
    return pl.pallas_call(
        body, grid=(3,),
        in_specs=[pl.BlockSpec((rows, BLK), lambda i: (i, KA)), pl.BlockSpec((rows, BLK), lambda i: (i, VA))],
        out_specs=[col, col], out_shape=[SDS((BLK, LP), BF16)] * 2,
        compiler_params=_cparams(("parallel",)), name="swa_prep")(proj, proj)


def _segments(ref, i, by_rows):
    starts = [0, pl.multiple_of(jnp.maximum(i - 1, 0) * BLK, BLK), pl.multiple_of(i * BLK, BLK)]
    if by_rows:
        return jnp.concatenate([ref[pl.ds(s, BLK), :] for s in starts], axis=0)
    return jnp.concatenate([ref[:, pl.ds(s, BLK)] for s in starts], axis=1)


def _swa_fwd(proj, vt_a, rel_bias, sinks, bkt_t, rider):
    def body(*refs):
        ((tab_ref, sink_ref, bkt_ref, q_ref, k_ref, vt_ref), (o_ref, lse_ref),
         (tbl, sink_row), mine) = rider.split(refs, 6, 2, 2)
        i = pl.program_id(0)
        rider.at_steps(mine, i == 0, i == NBLK // 2, i == NBLK - 1)

        @pl.when(i == 0)
        def _():
            _swa_tables(tab_ref, sink_ref, bkt_ref, tbl, sink_row)

        case = jnp.minimum(i, 2)
        k_cat = _segments(k_ref, i, True)
        vt_cat = _segments(vt_ref, i, False)
        raw = [lax.dot_general(k_cat, _stack_heads(q_ref, g, SCALE), NT_DIMS, preferred_element_type=F32)
               for g in range(2)]
        for g in range(2):
            s_t = raw[g] + tbl[case, g]
            sink = sink_row[g]
            m = jnp.maximum(_over_keys(jnp.max, s_t), sink)
            p_t = jnp.exp(s_t - m)
            l = _over_keys(jnp.sum, p_t) + jnp.exp(sink - m)
            o_t = jnp.dot(vt_cat[g * HALF:(g + 1) * HALF, :], p_t.astype(BF16), preferred_element_type=F32)
            _unstack_heads(o_t * (1.0 / l), g, o_ref, 1.0)
            lse = m + jnp.log(l)
            for hh in range(GROUP):
                lse_ref[GROUP * g + hh] = lse[:, hh * BLK:(hh + 1) * BLK]

    smem = pl.BlockSpec(memory_space=pltpu.SMEM)
    o_a, lse, *carried = pl.pallas_call(
        body, grid=(NBLK,),
        in_specs=[smem, smem, pl.BlockSpec((2, BLK, BLK), lambda i: (0, 0, 0)),
                  pl.BlockSpec((BLK, 512), lambda i: (i, QA)), pl.BlockSpec((LP, BLK), lambda i: (0, KA)),
                  pl.BlockSpec((BLK, LP), lambda i: (0, 0))] + [HBM_SPEC] * len(rider.operands),
        out_specs=[pl.BlockSpec((BLK, 512), lambda i: (i, 0)),
                   pl.BlockSpec((N_HEADS, 1, BLK), lambda i: (0, 0, i))] + [HBM_SPEC] * len(rider.out_shapes),
        out_shape=[SDS((LP, 512), BF16), SDS((N_HEADS, 1, LP), F32)] + rider.out_shapes,
        scratch_shapes=[pltpu.VMEM((3, 2, N_KEY, QW), F32), pltpu.VMEM((2, 1, QW), F32)] + rider.scratch(),
        compiler_params=_cparams(("arbitrary",)), name="swa_fwd",
    )(rel_bias, sinks, bkt_t, proj, proj, vt_a, *rider.operands)
    return o_a, lse, carried


def _swa_bwd(proj, kt_a, o_a, dmix, lse, rel_bias, sinks, bkt_t):
    def body(tab_ref, sink_ref, bkt_ref, q_ref, k_ref, v_ref, kt_ref, o_ref, do_ref, lse_ref,
             dq_ref, dk_ref, dv_ref, dbias_ref, dsink_ref, tbl, sink_row, acc, dsk):
        i = pl.program_id(0)

        @pl.when(i == 0)
        def _():
            _swa_tables(tab_ref, sink_ref, bkt_ref, tbl, sink_row)
            dk_ref[...] = jnp.zeros_like(dk_ref)
            dv_ref[...] = jnp.zeros_like(dv_ref)
            acc[...] = jnp.zeros_like(acc)
            dsk[...] = jnp.zeros_like(dsk)

        case = jnp.minimum(i, 2)
        first = jnp.full((BLK, QW), i, jnp.int32) == 1
        k_cat = _segments(k_ref, i, True)
        v_cat = _segments(v_ref, i, True)
        kt_cat = _segments(kt_ref, i, False)
        dk_cat = jnp.zeros((N_KEY, BLK), F32)
        dv_cat = jnp.zeros((N_KEY, BLK), F32)
        for g in range(2):
            d_parts = []
            for pair in range(2):
                cols = slice((2 * g + pair) * BLK, (2 * g + pair + 1) * BLK)
                prod_t = (do_ref[:, cols].astype(F32) * o_ref[:, cols].astype(F32)).T
                d_parts += [jnp.sum(prod_t[:HALF], axis=0, keepdims=True),
                            jnp.sum(prod_t[HALF:], axis=0, keepdims=True)]
            d_row = jnp.concatenate(d_parts, axis=1)
            lse_row = jnp.concatenate([lse_ref[GROUP * g + hh] for hh in range(GROUP)], axis=1)
            q_st = _stack_heads(q_ref, g, SCALE)
            do_st = _stack_heads(do_ref, g, 1.0)
            s_t = lax.dot_general(k_cat, q_st, NT_DIMS, preferred_element_type=F32) + tbl[case, g]
            p_t = jnp.exp(s_t - lse_row)
            dp_t = lax.dot_general(v_cat, do_st, NT_DIMS, preferred_element_type=F32)
            ds_t = p_t * (dp_t - d_row)
            dsk[g] += -jnp.exp(sink_row[g] - lse_row) * d_row
            acc[g, 0:BLK] += jnp.where(first, 0.0, ds_t[0:BLK])
            acc[g, BLK:2 * BLK] += jnp.where(first, ds_t[0:BLK], ds_t[BLK:2 * BLK])
            acc[g, 2 * BLK:N_KEY] += ds_t[2 * BLK:N_KEY]
            ds_b = ds_t.astype(BF16)
            dk_cat = dk_cat + jnp.dot(ds_b, q_st, preferred_element_type=F32)
            dv_cat = dv_cat + jnp.dot(p_t.astype(BF16), do_st, preferred_element_type=F32)
            dq_t = jnp.dot(kt_cat[g * HALF:(g + 1) * HALF, :], ds_b, preferred_element_type=F32)
            _unstack_heads(dq_t, g, dq_ref, SCALE)

        prev0 = pl.multiple_of(jnp.maximum(i - 1, 0) * BLK, BLK)
        cur0 = pl.multiple_of(i * BLK, BLK)
        for ref, cat in ((dk_ref, dk_cat), (dv_ref, dv_cat)):
            ref[0:BLK, :] += cat[0:BLK]
            ref[pl.ds(prev0, BLK), :] += cat[BLK:2 * BLK]
            ref[pl.ds(cur0, BLK), :] += cat[2 * BLK:N_KEY]

        @pl.when(i == NBLK - 1)
        def _():
            lane = lax.broadcasted_iota(jnp.int32, (1, BLK), 1)

            def per_bucket(b, carry):
                row = jnp.zeros((1, BLK), F32)
                for h in range(N_HEADS):
                    g, cols = h // GROUP, slice((h % GROUP) * BLK, (h % GROUP + 1) * BLK)
                    val = (jnp.sum(jnp.where(bkt_ref[0] == b, acc[g, 2 * BLK:N_KEY, cols], 0.0), keepdims=True)
                           + jnp.sum(jnp.where(bkt_ref[1] == b, acc[g, BLK:2 * BLK, cols], 0.0), keepdims=True))
                    row = jnp.where(lane == h, val, row)
                dbias_ref[pl.ds(b, 1), :] = row
                return carry

            lax.fori_loop(0, N_BUCKETS, per_bucket, 0)
            far = jnp.zeros((1, BLK), F32)
            dsr = jnp.zeros((1, BLK), F32)
            for h in range(N_HEADS):
                g, cols = h // GROUP, slice((h % GROUP) * BLK, (h % GROUP + 1) * BLK)
                far = jnp.where(lane == h, jnp.sum(acc[g, 0:BLK, cols], keepdims=True), far)
                dsr = jnp.where(lane == h, jnp.sum(dsk[g, :, cols], keepdims=True), dsr)
            dbias_ref[N_BUCKETS - 1:N_BUCKETS, :] += far
            dsink_ref[...] = dsr

    smem = pl.BlockSpec(memory_space=pltpu.SMEM)
    blk512 = lambda col: pl.BlockSpec((BLK, 512), lambda i: (i, col))
    full = lambda r, c: pl.BlockSpec((r, c), lambda i: (0, 0))
    return pl.pallas_call(
        body, grid=(NBLK,),
        in_specs=[smem, smem, pl.BlockSpec((2, BLK, BLK), lambda i: (0, 0, 0)), blk512(QA),
                  pl.BlockSpec((LP, BLK), lambda i: (0, KA)), pl.BlockSpec((LP, BLK), lambda i: (0, VA)),
                  full(BLK, LP), blk512(0), blk512(0), pl.BlockSpec((N_HEADS, 1, BLK), lambda i: (0, 0, i))],
        out_specs=[blk512(0), full(LP, BLK), full(LP, BLK), full(N_BUCKETS, BLK), full(1, BLK)],
        out_shape=[SDS((LP, 512), BF16), SDS((LP, BLK), F32), SDS((LP, BLK), F32),
                   SDS((N_BUCKETS, BLK), F32), SDS((1, BLK), F32)],
        scratch_shapes=[pltpu.VMEM((3, 2, N_KEY, QW), F32), pltpu.VMEM((2, 1, QW), F32),
                        pltpu.VMEM((2, N_KEY, QW), F32), pltpu.VMEM((2, 1, QW), F32)],
        compiler_params=_cparams(("arbitrary",)), name="swa_bwd",
    )(rel_bias, sinks, bkt_t, proj, proj, proj, kt_a, o_a, dmix, lse)


def _local_step(x, tgt, meta, rel_bias, g_pre_mix, g_post_mix, g_pre_ffn, g_post_ffn, b_forget, sinks,
                w_in_b, out_rider, out_weight, ffn_rider, ffn_weights, early_grads):
    bkt_t = jnp.asarray(_bucket_tables_t())
    b_p = jnp.pad(b_forget, ((0, 0), (0, BLK - N_HEADS)))

    h0, hn1, proj, f = _pre_mix(x, meta, g_pre_mix, w_in_b)
    kt_a, vt_a = _swa_prep(proj)
    o_a, lse_a, carried_out = _swa_fwd(proj, vt_a, rel_bias, sinks, bkt_t, out_rider)
    w_out_b = out_weight(carried_out)
    cum = _forget_cumsum(f, b_p)
    ck_t = cum[:, :N_HEADS].T.reshape(N_HEADS, 1, LP)
    q_aug, k_aug, v_t = _fox_prep(proj, cum)
    o_b, lse_row, carried = _fox_fwd(q_aug, k_aug, v_t, ffn_rider)
    lse_b = lse_row.reshape(N_HEADS, LP, 1)
    w_gu_b, w_dn_b = ffn_weights(carried)
    a, h1, hn2 = _attn_out(o_a, o_b, w_out_b, h0, g_post_mix, g_pre_ffn)
    g, u, act = _ffn_up(hn2, w_gu_b)
    dff, dy, loss_blk, dg_post_ffn = _ffn_down_loss(act, w_dn_b, h1, tgt, g_post_ffn)

    dw_dn = _mm_tn([act], dff, FF_T, "dw_down", BF16)
    dg, du = _ffn_down_bwd(dff, w_dn_b, g, u)
    dw_gu = _dw_gate_up(hn2, dg, du)
    dh1, da, dg_pre_ffn, dg_post_mix = _ffn_up_bwd(dg, du, w_gu_b, h1, a, dy, g_pre_ffn, g_post_mix)
    dw_out = _mm_tn([o_a, o_b], da, D_MODEL, "dw_out", BF16)
    dmix = _attn_out_bwd(da, w_out_b)
    dq_b, dk_b, dv_b, dck, dcq, landed = _fox_bwd(proj, o_b, dmix, lse_b, ck_t, early_grads(dw_gu, dw_dn, dw_out))
    dq_a, dk_a, dv_a, dbias, dsink = _swa_bwd(proj, kt_a, o_a, dmix, lse_a, rel_bias, sinks, bkt_t)
    dcum = dcq - jnp.pad(dck.reshape(N_HEADS, LP).T, ((0, 0), (0, BLK - N_HEADS)))
    df, db = _forget_cumsum_bwd(dcum, f, b_p)
    dproj, dx, dmeta, dg_pre_mix = _pre_mix_bwd(dq_a, dq_b, dk_b, dv_b, dk_a, dv_a, df, w_in_b, h0, dh1, g_pre_mix)
    dw_in = _mm_tn([hn1], dproj, D_MODEL, "dw_in", BF16)

    return dict(loss=loss_blk[0, 0], grad_x=dx, meta=dmeta,
                rel_bias=dbias[:, :N_HEADS], ln_pre_mix=dg_pre_mix, ln_post_mix=dg_post_mix,
                ln_pre_ffn=dg_pre_ffn, ln_post_ffn=dg_post_ffn, b_forget=db[:, :N_HEADS],
                sinks=dsink[:, :N_HEADS], w_in=dw_in, w_out=dw_out, w_gate_up=dw_gu, w_down=dw_dn,
                landed=landed)


N_SMALL = 24
LOSS_ROW = 6


def _place():
    x, y, c = lax.axis_index("x"), lax.axis_index("y"), lax.axis_index("c")
    return x, y, c, [(1 - x, y), (x, 1 - y), (1 - x, 1 - y)]


def _run_alone(rider, name):
    a, b = len(rider.operands), len(rider.out_shapes)

    def body(*refs):
        mine = (refs[:a], refs[a:a + b], refs[a + b:])
        rider.first(*mine)
        rider.middle(*mine)
        rider.last(*mine)

    return pl.pallas_call(body, in_specs=[HBM_SPEC] * a, out_specs=[HBM_SPEC] * b, out_shape=rider.out_shapes,
                          scratch_shapes=rider.scratch(), name=name)(*rider.operands)


def _gather_rider(shards, own_too, by_columns=()):
    n = len(shards)

    def slot(a, outs, chip, h):
        if a in by_columns:
            cols = shards[a].shape[2]
            return outs[a].at[h, :, pl.ds(pl.multiple_of(chip * cols, BLK), cols)]
        return outs[a].at[chip, h]

    def own_copies(ins, outs, sems):
        x, y, _, _ = _place()
        if not own_too:
            return []
        return [pltpu.make_async_copy(ins[a].at[h], slot(a, outs, 2 * x + y, h), sems[2].at[2 * a + h])
                for a in range(n) for h in range(2)]

    def copies(ins, outs, sems):
        send_sems, recv_sems = sems[:2]
        x, y, c, others = _place()
        chip = 2 * x + y
        sibling = (x, y, 1 - c)

        def rc(a, k, src, dst, to):
            return pltpu.make_async_remote_copy(src_ref=src, dst_ref=dst, send_sem=send_sems.at[6 * a + k],
                                                recv_sem=recv_sems.at[6 * a + k], device_id=to, device_id_type=MESH)

        pairs = [(a, j, ox, oy) for a in range(n) for j, (ox, oy) in enumerate(others)]
        there = lambda a, ox, oy, h: slot(a, outs, 2 * ox + oy, h)
        return dict(
            sent=lambda: [rc(a, j, ins[a].at[c], slot(a, outs, chip, c), (ox, oy, c)) for a, j, ox, oy in pairs],
            landed=lambda: [rc(a, j, there(a, ox, oy, c), there(a, ox, oy, c), sibling) for a, j, ox, oy in pairs],
            passed=lambda: [rc(a, 3 + j, there(a, ox, oy, c), there(a, ox, oy, c), sibling)
                            for a, j, ox, oy in pairs],
            arriving=lambda: [rc(a, 3 + j, there(a, ox, oy, 1 - c), there(a, ox, oy, 1 - c), sibling)
                              for a, j, ox, oy in pairs])

    def first(*mine):
        for cp in copies(*mine)["sent"]() + own_copies(*mine):
            cp.start()

    def middle(*mine):
        kinds = copies(*mine)
        for got, cp in zip(kinds["landed"](), kinds["passed"]()):
            got.wait_recv()
            cp.start()

    def last(*mine):
        kinds = copies(*mine)
        for cp in kinds["arriving"]():
            cp.wait_recv()
        for cp in kinds["sent"]() + kinds["passed"]():
            cp.wait_send()
        for cp in own_copies(*mine):
            cp.wait()

    shapes = [SDS((2, s.shape[1], 4 * s.shape[2]) if a in by_columns else (4,) + s.shape, s.dtype)
              for a, s in enumerate(shards)]
    return _Rider(shards, shapes, [6 * n, 6 * n] + [2 * n] * own_too, first, middle, last)


def _swap_rider(grads):
    n = len(grads)
    slabs = [(a, s) for a in range(n) for s in range(grads[a].shape[0])]

    def copies(ins, outs, sems):
        x, y, c, _ = _place()
        return [pltpu.make_async_remote_copy(
            src_ref=ins[a].at[s, 1 - c], dst_ref=outs[a].at[s], send_sem=sems[0].at[k], recv_sem=sems[1].at[k],
            device_id=(x, y, 1 - c), device_id_type=MESH) for k, (a, s) in enumerate(slabs)]

    def first(*mine):
        for cp in copies(*mine):
            cp.start()

    def middle(*mine):
        pass

    def last(*mine):
        for cp in copies(*mine):
            cp.wait()

    return _Rider(grads, [SDS(g.shape[:1] + g.shape[2:], g.dtype) for g in grads], [len(slabs), len(slabs)],
                  first, middle, last)


def _pair_sum(g, got, c_arr, name):
    n_s, rh, cc = got.shape

    def body(c_ref, g_ref, p_ref, o_ref):
        o_ref[0] = (g_ref[0, 0].astype(F32) + p_ref[0].astype(F32)).astype(BF16)

    grid_spec = pltpu.PrefetchScalarGridSpec(
        num_scalar_prefetch=1, grid=(n_s,),
        in_specs=[pl.BlockSpec((1, 1, rh, cc), lambda s, c_ref: (s, c_ref[0], 0, 0)),
                  pl.BlockSpec((1, rh, cc), lambda s, c_ref: (s, 0, 0))],
        out_specs=pl.BlockSpec((1, rh, cc), lambda s, c_ref: (s, 0, 0)))
    return pl.pallas_call(body, grid_spec=grid_spec, out_shape=SDS((n_s, rh, cc), BF16),
                          compiler_params=_cparams(("parallel",)), name=name)(c_arr, g, got)


def _direct_rider(grads):
    n = len(grads)

    def copies(ins, outs, sems):
        x, y, c, others = _place()
        peers = [(x, y, 1 - c)] + [(ox, oy, c) for ox, oy in others] + [(ox, oy, 1 - c) for ox, oy in others]
        return [pltpu.make_async_remote_copy(
            src_ref=ins[a].at[2 * px + py, pc], dst_ref=outs[a].at[k], send_sem=sems[0].at[7 * a + k],
            recv_sem=sems[1].at[7 * a + k], device_id=(px, py, pc), device_id_type=MESH)
            for a in range(n) for k, (px, py, pc) in enumerate(peers)]

    def first(*mine):
        for cp in copies(*mine):
            cp.start()

    def middle(*mine):
        pass

    def last(*mine):
        for cp in copies(*mine):
            cp.wait()

    return _Rider(grads, [SDS((7,) + g.shape[2:], g.dtype) for g in grads], [7 * n, 7 * n], first, middle, last)


def _owner_sum(grads, landed, own_arr, after, name):
    rh, cc = landed.shape[1:]
    tr = rh // 2

    def body(own_ref, g_ref, p_ref, after_ref, o_ref):
        total = g_ref[0, 0].astype(F32)
        for k in range(7):
            total = total + p_ref[k].astype(F32)
        o_ref[...] = total

    grid_spec = pltpu.PrefetchScalarGridSpec(
        num_scalar_prefetch=1, grid=(2,),
        in_specs=[pl.BlockSpec((1, 1, tr, cc), lambda i, own: (own[0], own[1], i, 0)),
                  pl.BlockSpec((7, tr, cc), lambda i, own: (0, i, 0)), pl.BlockSpec(memory_space=pl.ANY)],
        out_specs=pl.BlockSpec((tr, cc), lambda i, own: (i, 0)))
    return pl.pallas_call(body, grid_spec=grid_spec, out_shape=SDS((rh, cc), F32),
                          compiler_params=_cparams(("parallel",)), name=name)(own_arr, grads, landed, after)


SEM_SPEC = pl.BlockSpec(memory_space=pltpu.SEMAPHORE)
N_LATE = 10


def _late_copies(part_ref, landed_ref, small_ref, all_ref, send_sems, recv_sems):
    x, y, c, others = _place()
    me = 4 * x + 2 * y + c
    peers = [(x, y, 1 - c)] + [(ox, oy, c) for ox, oy in others] + [(ox, oy, 1 - c) for ox, oy in others]
    big = [pltpu.make_async_remote_copy(
        src_ref=part_ref.at[2 * ox + oy], dst_ref=landed_ref.at[j], send_sem=send_sems.at[j], recv_sem=recv_sems.at[j],
        device_id=(ox, oy, c), device_id_type=MESH) for j, (ox, oy) in enumerate(others)]
    small = [pltpu.make_async_remote_copy(
        src_ref=small_ref, dst_ref=all_ref.at[me], send_sem=send_sems.at[3 + k], recv_sem=recv_sems.at[3 + k],
        device_id=peer, device_id_type=MESH) for k, peer in enumerate(peers)]
    return big + small


def _late_exchange_start(part, small):
    def body(part_ref, landed_ref, small_ref, all_ref, send_sems, recv_sems, part_o, landed_o, small_o, all_o, token):
        for cp in _late_copies(part_ref, landed_ref, small_ref, all_ref, send_sems, recv_sems):
            cp.start()
        token[...] = jnp.zeros_like(token)

    hbm = lambda a: pltpu.HBM(a.shape, a.dtype)
    landed = lax.empty((3,) + part.shape[1:], part.dtype)
    everyone = lax.empty((8,) + small.shape, small.dtype)
    operands = [pltpu.with_memory_space_constraint(a, pltpu.HBM) for a in (part, landed, small, everyone)]
    return pl.pallas_call(
        body, name="late_exchange_start",
        out_shape=(pltpu.SemaphoreType.DMA((N_LATE,)), pltpu.SemaphoreType.DMA((N_LATE,)),
                   hbm(part), hbm(landed), hbm(small), hbm(everyone), SDS((8, BLK), F32)),
        in_specs=[HBM_SPEC] * 4,
        out_specs=(SEM_SPEC, SEM_SPEC, HBM_SPEC, HBM_SPEC, HBM_SPEC, HBM_SPEC, pl.BlockSpec(memory_space=pltpu.VMEM)),
        input_output_aliases={0: 2, 1: 3, 2: 4, 3: 5},
        compiler_params=pltpu.CompilerParams(has_side_effects=pltpu.SideEffectType.DATAFLOW_SIDE_EFFECTING),
    )(*operands)


def _late_exchange_wait(send_sems, recv_sems, part, landed, small, everyone, after):
    def body(part_ref, landed_ref, small_ref, all_ref, send_sems, recv_sems, after_ref, part_o, landed_o, small_o, all_o):
        for cp in _late_copies(part_ref, landed_ref, small_ref, all_ref, send_sems, recv_sems):
            cp.wait_send()
            cp.wait_recv()

    hbm = lambda a: pltpu.HBM(a.shape, a.dtype)
    out = pl.pallas_call(
        body, name="late_exchange_wait",
        out_shape=(hbm(part), hbm(landed), hbm(small), hbm(everyone)),
        in_specs=[HBM_SPEC] * 4 + [SEM_SPEC, SEM_SPEC, pl.BlockSpec(memory_space=pl.ANY)],
        out_specs=(HBM_SPEC,) * 4, input_output_aliases={0: 0, 1: 1, 2: 2, 3: 3},
        compiler_params=pltpu.CompilerParams(has_side_effects=pltpu.SideEffectType.DATAFLOW_SIDE_EFFECTING),
    )(part, landed, small, everyone, send_sems, recv_sems, after)
    return out[0], out[1], out[3]


def _chip_sum(parts, landed, chip_arr, name):
    rh, cc = landed.shape[1:]
    tr = rh // 2

    def body(chip_ref, own_ref, p_ref, o_ref):
        o_ref[...] = ((own_ref[0].astype(F32) + p_ref[0].astype(F32)) + p_ref[1].astype(F32)) + p_ref[2].astype(F32)

    grid_spec = pltpu.PrefetchScalarGridSpec(
        num_scalar_prefetch=1, grid=(2,),
        in_specs=[pl.BlockSpec((1, tr, cc), lambda i, chip_ref: (chip_ref[0], i, 0)),
                  pl.BlockSpec((3, tr, cc), lambda i, chip_ref: (0, i, 0))],
        out_specs=pl.BlockSpec((tr, cc), lambda i, chip_ref: (i, 0)))
    return pl.pallas_call(body, grid_spec=grid_spec, out_shape=SDS((rh, cc), F32),
                          compiler_params=_cparams(("parallel",)), name=name)(chip_arr, parts, landed)


def _device_sum(p):
    def body(p_ref, o_ref):
        acc = p_ref[0]
        for k in range(1, 8):
            acc = acc + p_ref[k]
        o_ref[...] = acc

    return pl.pallas_call(body, out_shape=SDS(p.shape[1:], F32), name="small_sum")(p)


def _join_halves(halves, name):
    n = len(halves)

    def body(*refs):
        ins, outs = refs[:n], refs[n:2 * n]
        send_sems, recv_sems = refs[2 * n:]
        x, y, c, _ = _place()
        copies = [pltpu.make_async_remote_copy(
            src_ref=ins[a], dst_ref=outs[a], send_sem=send_sems.at[a], recv_sem=recv_sems.at[a],
            device_id=(x, y, 1 - c), device_id_type=MESH) for a in range(n)]
        for cp in copies:
            cp.start()
        for cp in copies:
            cp.wait()

    return pl.pallas_call(
        body, in_specs=[HBM_SPEC] * n, out_specs=[HBM_SPEC] * n,
        out_shape=[SDS(h.shape, h.dtype) for h in halves],
        scratch_shapes=[pltpu.SemaphoreType.DMA((n,)), pltpu.SemaphoreType.DMA((n,))],
        name=name)(*halves)


def _adamw(w, g, m, v, name, tr=None):
    rows, cols = w.shape
    tr = tr or rows
    assert rows % tr == 0

    def body(w_ref, g_ref, m_ref, v_ref, d_ref, nm_ref, nv_ref):
        gg = g_ref[...]
        nm = ADAM_B1 * m_ref[...] + (1.0 - ADAM_B1) * gg
        nv = ADAM_B2 * v_ref[...] + (1.0 - ADAM_B2) * (gg * gg)
        nm_ref[...] = nm
        nv_ref[...] = nv
        m_hat = nm / (1.0 - ADAM_B1 ** ADAM_STEP)
        v_hat = nv / (1.0 - ADAM_B2 ** ADAM_STEP)
        d_ref[...] = -ADAM_LR * (m_hat / (jnp.sqrt(v_hat) + ADAM_EPS) + ADAM_WD * w_ref[...])

    blk = pl.BlockSpec((tr, cols), lambda i: (i, 0))
    return pl.pallas_call(
        body, grid=(rows // tr,), in_specs=[blk] * 4, out_specs=[blk] * 3,
        out_shape=[SDS((rows, cols), F32)] * 3,
        compiler_params=_cparams(("parallel",)), name=name)(w, g, m, v)


def _adamw_halves(w, mine, theirs, m, v, c_arr, name):
    rows, cols = w.shape
    rh = rows // 2
    tr = rh if rh <= 352 else 256
    nh = rh // tr

    def body(c_ref, w_ref, mine_ref, theirs_ref, m_ref, v_ref, g_ref, d_ref, nm_ref, nv_ref):
        own = jnp.full((tr, cols), pl.program_id(0), jnp.int32) == c_ref[0]
        gg = jnp.where(own, mine_ref[...], theirs_ref[...])
        g_ref[...] = gg
        nm = ADAM_B1 * m_ref[...] + (1.0 - ADAM_B1) * gg
        nv = ADAM_B2 * v_ref[...] + (1.0 - ADAM_B2) * (gg * gg)
        nm_ref[...] = nm
        nv_ref[...] = nv
        m_hat = nm / (1.0 - ADAM_B1 ** ADAM_STEP)
        v_hat = nv / (1.0 - ADAM_B2 ** ADAM_STEP)
        d_ref[...] = -ADAM_LR * (m_hat / (jnp.sqrt(v_hat) + ADAM_EPS) + ADAM_WD * w_ref[...])

    whole = pl.BlockSpec((tr, cols), lambda hh, i, c_ref: (hh * nh + i, 0))
    part = pl.BlockSpec((tr, cols), lambda hh, i, c_ref: (i, 0))
    grid_spec = pltpu.PrefetchScalarGridSpec(
        num_scalar_prefetch=1, grid=(2, nh), in_specs=[whole, part, part, whole, whole], out_specs=[whole] * 4)
    return pl.pallas_call(body, grid_spec=grid_spec, out_shape=[SDS((rows, cols), F32)] * 4,
                          compiler_params=_cparams(("parallel", "parallel")), name=name)(c_arr, w, mine, theirs, m, v)


def _pack_small(pre_mix, post_mix, pre_ffn, post_ffn, rel_bias, b_forget, sinks):
    def at(row, v):
        return jnp.pad(v, ((row, 7 - row), (0, D_MODEL - v.shape[1])))
    return (at(0, pre_mix) + at(1, post_mix) + at(2, pre_ffn) + at(3, post_ffn)
            + at(4, rel_bias.reshape(1, N_BUCKETS * N_HEADS)) + at(5, jnp.concatenate([b_forget, sinks], axis=1)))


def _unpack_small(p):
    return dict(ln_pre_mix=p[0:1], ln_post_mix=p[1:2], ln_pre_ffn=p[2:3], ln_post_ffn=p[3:4],
                rel_bias=p[4, :N_BUCKETS * N_HEADS].reshape(N_BUCKETS, N_HEADS),
                b_forget=p[5:6, 0:N_HEADS], sinks=p[5:6, N_HEADS:2 * N_HEADS])


WEIGHTS = ("meta_tokens", "rel_bias", "ln_pre_mix", "ln_post_mix", "ln_pre_ffn", "ln_post_ffn",
           "w_in", "b_forget", "sinks", "w_out", "w_gate_up", "w_down")


def kernel(x, meta_tokens, rel_bias, ln_pre_mix, ln_post_mix, ln_pre_ffn, ln_post_ffn, w_in, b_forget, sinks, w_out, w_gate_up, w_down, loss_target, m_meta_tokens, m_rel_bias, m_ln_pre_mix, m_ln_post_mix, m_ln_pre_ffn, m_ln_post_ffn, m_w_in, m_b_forget, m_sinks, m_w_out, m_w_gate_up, m_w_down, v_meta_tokens, v_rel_bias, v_ln_pre_mix, v_ln_post_mix, v_ln_pre_ffn, v_ln_post_ffn, v_w_in, v_b_forget, v_sinks, v_w_out, v_w_gate_up, v_w_down):
    xi, yi, ci = lax.axis_index("x"), lax.axis_index("y"), lax.axis_index("c")
    chip = 2 * xi + yi
    c_arr = jnp.reshape(ci, (1,)).astype(jnp.int32)

    def halves(w, dtype):
        return w.astype(dtype).reshape(2, w.shape[0] // 2, w.shape[1])

    def with_own(gathered, shards):
        return [lax.dynamic_update_slice(got, own[None], (chip, 0, 0, 0)) for got, own in zip(gathered, shards)]

    shards = [halves(w_in[0], BF16), halves(meta_tokens, F32)]
    gw_in, g_meta = with_own(_run_alone(_gather_rider(shards, False), "gather_mixer_weights"), shards)
    out_shards = [halves(w_out[0], BF16)]
    ffn_shards = [halves(w_gate_up[0], BF16), halves(w_down[0], BF16)]

    def ffn_weights(carried):
        gw_gu, gw_dn = carried
        return gw_gu.reshape(D_MODEL, 2 * D_FF), gw_dn.reshape(D_FF, D_MODEL)

    early = {}

    def early_grads(dw_gu, dw_dn, dw_out):
        early["grads"] = [dw_out.reshape(4, 2, 128, D_MODEL), dw_gu.reshape(4, 2, 512, FF_T),
                          dw_dn.reshape(4, 2, 352, D_MODEL)]
        return _direct_rider(early["grads"])
    w_in_all = gw_in.reshape(4, D_MODEL, D_PROJ // 4).transpose(1, 0, 2).reshape(D_MODEL, D_PROJ)
    w_in_b = jnp.pad(w_in_all, ((0, 0), (0, D_PROJ_P - D_PROJ)))
    meta_all = g_meta.reshape(4, N_META, D_MODEL // 4).transpose(1, 0, 2).reshape(N_META, D_MODEL)

    loc = _local_step(x[0], loss_target[0], meta_all, rel_bias, ln_pre_mix, ln_post_mix, ln_pre_ffn, ln_post_ffn,
                      b_forget, sinks, w_in_b, _gather_rider(out_shards, True),
                      lambda carried: carried[0].reshape(D_MODEL, D_MODEL),
                      _gather_rider(ffn_shards, True, by_columns=(0,)), ffn_weights, early_grads)

    small = jnp.concatenate(
        [_pack_small(loc["ln_pre_mix"], loc["ln_post_mix"], loc["ln_pre_ffn"], loc["ln_post_ffn"],
                     loc["rel_bias"], loc["b_forget"], loc["sinks"])
         + jnp.pad(loc["loss"].reshape(1, 1), ((LOSS_ROW, 7 - LOSS_ROW), (0, D_MODEL - 1))), loc["meta"]], axis=0)

    dw_in = loc["w_in"].reshape(1, 2, D_MODEL // 2, D_PROJ_P)
    (got_in,) = _run_alone(_swap_rider([dw_in]), "swap_halves_late")
    half_sum = _pair_sum(dw_in, got_in, c_arr, "pair_sum_late")
    part_in = half_sum[0, :, :D_PROJ].reshape(D_MODEL // 2, 4, D_PROJ // 4).transpose(1, 0, 2)
    send_sems, recv_sems, part_sent, landing, small_sent, everyone, token = _late_exchange_start(part_in, small)
    chip_arr = jnp.reshape(chip, (1,)).astype(jnp.int32)
    own_arr = jnp.stack([chip, ci]).astype(jnp.int32)
    grad, delta, new_m, new_v = {}, {}, {}, {}

    def update(names, mine):
        theirs = _join_halves(mine, "join_" + names[0])
        big = dict(w_in=(w_in, m_w_in, v_w_in), w_out=(w_out, m_w_out, v_w_out),
                   w_gate_up=(w_gate_up, m_w_gate_up, v_w_gate_up), w_down=(w_down, m_w_down, v_w_down))
        for name, g_mine, g_theirs in zip(names, mine, theirs):
            w, m, v = big[name]
            g, d, nm, nv = _adamw_halves(w[0], g_mine, g_theirs, m[0], v[0], c_arr, "adamw_" + name)
            grad[name], delta[name], new_m[name], new_v[name] = g[None], d[None], nm[None], nv[None]

    update(("w_out", "w_gate_up", "w_down"),
           [_owner_sum(g, l, own_arr, token, "owner_sum_%d" % a)
            for a, (g, l) in enumerate(zip(early["grads"], loc["landed"]))])
    part_back, landed_in, small_all = _late_exchange_wait(send_sems, recv_sems, part_sent, landing, small_sent,
                                                         everyone, new_v["w_down"])
    mine_in = _chip_sum(part_back, landed_in, chip_arr, "chip_sum_in")
    (theirs_in,) = _join_halves([mine_in], "join_w_in")
    g_w_in = jnp.where(ci == 0, jnp.concatenate([mine_in, theirs_in], axis=0),
                       jnp.concatenate([theirs_in, mine_in], axis=0))
    view = lambda a: jnp.transpose(a).reshape(D_PROJ // 4 * 8, BLK)
    back = lambda a: jnp.transpose(a.reshape(D_PROJ // 4, D_MODEL))[None]
    d, nm, nv = _adamw(view(w_in[0]), view(g_w_in), view(m_w_in[0]), view(v_w_in[0]), "adamw_w_in",
                       tr=D_PROJ // 4 * 4)
    grad["w_in"], delta["w_in"], new_m["w_in"], new_v["w_in"] = g_w_in[None], back(d), back(nm), back(nv)
    me = 4 * xi + 2 * yi + ci
    small_sum = _device_sum(lax.dynamic_update_slice(small_all, small[None], (me, 0, 0)))
    g_meta_tokens = lax.dynamic_slice(small_sum[8:N_SMALL], (0, chip * (D_MODEL // 4)), (N_META, D_MODEL // 4))
    g_small = small_sum[0:8]
    grad.update(_unpack_small(g_small))
    grad.update(meta_tokens=g_meta_tokens)
    delta["meta_tokens"], new_m["meta_tokens"], new_v["meta_tokens"] = _adamw(
        meta_tokens, g_meta_tokens, m_meta_tokens, v_meta_tokens, "adamw_meta")
    d, nm, nv = _adamw(
        _pack_small(ln_pre_mix, ln_post_mix, ln_pre_ffn, ln_post_ffn, rel_bias, b_forget, sinks), g_small,
        _pack_small(m_ln_pre_mix, m_ln_post_mix, m_ln_pre_ffn, m_ln_post_ffn, m_rel_bias, m_b_forget, m_sinks),
        _pack_small(v_ln_pre_mix, v_ln_post_mix, v_ln_pre_ffn, v_ln_post_ffn, v_rel_bias, v_b_forget, v_sinks),
        "adamw_small")
    delta.update(_unpack_small(d))
    new_m.update(_unpack_small(nm))
    new_v.update(_unpack_small(nv))

    loss = small_sum[LOSS_ROW, 0]
    return (loss,loc["grad_x"][None], *[grad[k] for k in WEIGHTS], *[delta[k] for k in WEIGHTS],
            *[new_m[k] for k in WEIGHTS], *[new_v[k] for k in WEIGHTS])
```

```python
import math

import numpy as np
import jax
import jax.numpy as jnp
from jax import lax
from jax.experimental import pallas as pl
from jax.experimental.pallas import tpu as pltpu

F32 = jnp.float32
BF16 = jnp.bfloat16
MESH = pl.DeviceIdType.MESH
SDS = jax.ShapeDtypeStruct

D_MODEL = 1024
SEQ = 4096
N_META = 16
N_HEADS = 8
HALF = 64
D_FF = 2816
N_BUCKETS = 32
EPS = 1e-6
NEG = -1e30
SCALE = 0.125
LOG2E = 1.4426950408889634
LN2 = 0.6931471805599453
PAD_ROWS = 112
ROW0 = PAD_ROWS + N_META
LP = ROW0 + SEQ
BLK = 128
NBLK = LP // BLK
TM = 384
NT = LP // TM
TM_PURE = LP // 2
TM_MID = LP // 4
TM_EPI = LP // 6
TN = 256
TK_W = LP // 2
D_PROJ = 2312
D_PROJ_P = 2432
D_QKV = 2304
FF_T = 1408
VMEM_LIMIT = 56 * 1024 * 1024

ADAM_LR = 0.001
ADAM_B1 = 0.9
ADAM_B2 = 0.999
ADAM_EPS = 1e-08
ADAM_WD = 0.01
ADAM_STEP = 10

QA = 0
KA, VA = 4, 5
QB, KB, VB = 3, 5, 7
W2 = 256

NT_DIMS = (((1,), (1,)), ((), ()))
TN_DIMS = (((0,), (0,)), ((), ()))


def _cparams(sem):
    return pltpu.CompilerParams(dimension_semantics=sem, vmem_limit_bytes=VMEM_LIMIT)


def _t5_bucket_np(d):
    n = np.maximum(d, 0).astype(np.int32)
    nf = np.maximum(n, 1).astype(np.float32)
    large = 16 + (np.log(nf / np.float32(16)) / np.float32(math.log(8.0)) * np.float32(16)).astype(np.int32)
    large = np.minimum(large, N_BUCKETS - 1)
    return np.where(n < 16, n, large).astype(np.int32)


def _bucket_tables():
    qi = np.arange(BLK)[:, None]
    ki = np.arange(BLK)[None, :]
    return np.stack([_t5_bucket_np(qi - ki), _t5_bucket_np(qi - ki + BLK)])


def _rms(x):
    return lax.rsqrt(jnp.mean(x * x, axis=-1, keepdims=True) + EPS)


def _rms_bwd(n, r, gdy):
    return r * (gdy - n * jnp.mean(n * gdy, axis=-1, keepdims=True))


def _pre_mix(x, meta, gain, w_in_b):
    half = D_QKV // 2
    n_steps = LP // TM_MID

    def body(x_ref, meta_ref, g_ref, w_ref, h0_ref, hn_ref, proj_ref, f_ref, tile, sems):
        i = pl.program_id(0)

        def tile_copy(s):
            skip = ROW0 if s == 0 else 0
            return pltpu.make_async_copy(x_ref.at[s * TM_MID + skip - ROW0:(s + 1) * TM_MID - ROW0],
                                         tile.at[s % 2, skip:TM_MID], sems.at[s % 2])

        @pl.when(i == 0)
        def _():
            tile_copy(0).start()
            tile[0, :PAD_ROWS] = jnp.zeros((PAD_ROWS, D_MODEL), F32)
            tile[0, PAD_ROWS:ROW0] = meta_ref[...]

        for s in range(n_steps):
            @pl.when(i == s)
            def _():
                if s + 1 < n_steps:
                    tile_copy(s + 1).start()
                tile_copy(s).wait()

        x = tile[i % 2]
        h0_ref[...] = x
        hn = (x * _rms(x) * g_ref[...]).astype(BF16)
        hn_ref[...] = hn
        proj_ref[:, :half] = lax.dot_general(hn, w_ref[:half, :], NT_DIMS, preferred_element_type=F32).astype(BF16)
        p = lax.dot_general(hn, w_ref[half:, :], NT_DIMS, preferred_element_type=F32)
        proj_ref[:, half:] = p[:, :half].astype(BF16)
        f_ref[...] = p[:, half:]

    return pl.pallas_call(
        body, grid=(n_steps,),
        in_specs=[HBM_SPEC,
                  pl.BlockSpec((N_META, D_MODEL), lambda i: (0, 0)),
                  pl.BlockSpec((1, D_MODEL), lambda i: (0, 0)),
                  pl.BlockSpec((D_PROJ_P, D_MODEL), lambda i: (0, 0))],
        out_specs=[pl.BlockSpec((TM_MID, D_MODEL), lambda i: (i, 0)),
                   pl.BlockSpec((TM_MID, D_MODEL), lambda i: (i, 0)),
                   pl.BlockSpec((TM_MID, D_QKV), lambda i: (i, 0)),
                   pl.BlockSpec((TM_MID, BLK), lambda i: (i, 0))],
        out_shape=[SDS((LP, D_MODEL), F32), SDS((LP, D_MODEL), BF16), SDS((LP, D_QKV), BF16), SDS((LP, BLK), F32)],
        scratch_shapes=[pltpu.VMEM((2, TM_MID, D_MODEL), F32), pltpu.SemaphoreType.DMA((2,))],
        compiler_params=_cparams(("arbitrary",)), name="pre_mix")(x, meta, gain, w_in_b)


def _attn_out(o_a, o_b, w_out_b, h0, g_post, g_pre_ffn):
    def body(oa_ref, ob_ref, w_ref, h0_ref, gp_ref, gf_ref, a_ref, h1_ref, hn2_ref):
        a = (jnp.dot(oa_ref[...], w_ref[0:512, :], preferred_element_type=F32)
             + jnp.dot(ob_ref[...], w_ref[512:1024, :], preferred_element_type=F32))
        a_ref[...] = a
        h1 = h0_ref[...] + a * _rms(a) * gp_ref[...]
        h1_ref[...] = h1
        hn2_ref[...] = (h1 * _rms(h1) * gf_ref[...]).astype(BF16)

    row = lambda w: pl.BlockSpec((TM_EPI, w), lambda i: (i, 0))
    vec = pl.BlockSpec((1, D_MODEL), lambda i: (0, 0))
    return pl.pallas_call(
        body, grid=(LP // TM_EPI,),
        in_specs=[row(512), row(512), pl.BlockSpec((D_MODEL, D_MODEL), lambda i: (0, 0)), row(D_MODEL), vec, vec],
        out_specs=[row(D_MODEL), row(D_MODEL), row(D_MODEL)],
        out_shape=[SDS((LP, D_MODEL), F32), SDS((LP, D_MODEL), F32), SDS((LP, D_MODEL), BF16)],
        compiler_params=_cparams(("parallel",)), name="attn_out")(o_a, o_b, w_out_b, h0, g_post, g_pre_ffn)


def _ffn_up(hn2, w_gu_b):
    def body(x_ref, wg_ref, wu_ref, g_ref, u_ref, act_ref):
        x = x_ref[...]
        g = jnp.dot(x, wg_ref[...], preferred_element_type=F32)
        u = jnp.dot(x, wu_ref[...], preferred_element_type=F32)
        g_ref[...] = g.astype(BF16)
        u_ref[...] = u.astype(BF16)
        act_ref[...] = (g * (1.0 / (1.0 + jnp.exp(-g))) * u).astype(BF16)

    out = pl.BlockSpec((LP, TN), lambda j: (0, j))
    return pl.pallas_call(
        body, grid=(D_FF // TN,),
        in_specs=[pl.BlockSpec((LP, D_MODEL), lambda j: (0, 0)),
                  pl.BlockSpec((D_MODEL, TN), lambda j: (0, j)),
                  pl.BlockSpec((D_MODEL, TN), lambda j: (0, j + D_FF // TN))],
        out_specs=[out, out, out],
        out_shape=[SDS((LP, D_FF), BF16)] * 3,
        compiler_params=_cparams(("parallel",)), name="ffn_up")(hn2, w_gu_b, w_gu_b)


def _ffn_down_loss(act, w_dn_b, h1, tgt, g_post_ffn):
    def body(act_ref, w_ref, h1_ref, t0_ref, t1_ref, t2_ref, g_ref, dff_ref, dy_ref, loss_ref, dg_ref):
        i = pl.program_id(0)
        target = jnp.concatenate([t0_ref[...], t1_ref[...], t2_ref[...]], axis=0)

        @pl.when(i == 0)
        def _():
            loss_ref[...] = jnp.zeros_like(loss_ref)
            dg_ref[...] = jnp.zeros_like(dg_ref)

        ff = jnp.dot(act_ref[...], w_ref[...], preferred_element_type=F32)
        r = _rms(ff)
        n = ff * r
        g = g_ref[...]
        y = h1_ref[...] + n * g
        rows = i * TM + lax.broadcasted_iota(jnp.int32, (TM, D_MODEL), 0)
        diff = jnp.where(rows >= ROW0, y - target, 0.0)
        loss_ref[...] += 0.5 * jnp.sum(diff * diff) / D_MODEL
        dy = diff / D_MODEL
        dy_ref[...] = dy
        dg_ref[...] += jnp.sum(dy * n, axis=0, keepdims=True)
        dff_ref[...] = _rms_bwd(n, r, g * dy).astype(BF16)

    row = pl.BlockSpec((TM, D_MODEL), lambda i: (i, 0))
    tblk = lambda j: pl.BlockSpec((BLK, D_MODEL), lambda i: (jnp.maximum(3 * i - 1 + j, 0), 0))
    return pl.pallas_call(
        body, grid=(NT,),
        in_specs=[pl.BlockSpec((TM, D_FF), lambda i: (i, 0)), pl.BlockSpec((D_FF, D_MODEL), lambda i: (0, 0)),
                  row, tblk(0), tblk(1), tblk(2), pl.BlockSpec((1, D_MODEL), lambda i: (0, 0))],
        out_specs=[row, row, pl.BlockSpec((8, BLK), lambda i: (0, 0)), pl.BlockSpec((1, D_MODEL), lambda i: (0, 0))],
        out_shape=[SDS((LP, D_MODEL), BF16), SDS((LP, D_MODEL), F32), SDS((8, BLK), F32), SDS((1, D_MODEL), F32)],
        compiler_params=_cparams(("arbitrary",)), name="ffn_down_loss")(act, w_dn_b, h1, tgt, tgt, tgt, g_post_ffn)


def _ffn_down_bwd(dff, w_dn_b, g, u):
    def body(d_ref, w_ref, g_ref, u_ref, dg_ref, du_ref):
        dact = lax.dot_general(d_ref[...], w_ref[...], NT_DIMS, preferred_element_type=F32)
        gg = g_ref[...].astype(F32)
        sig = 1.0 / (1.0 + jnp.exp(-gg))
        dg_ref[...] = (dact * u_ref[...].astype(F32) * sig * (1.0 + gg * (1.0 - sig))).astype(BF16)
        du_ref[...] = (dact * gg * sig).astype(BF16)

    blk = pl.BlockSpec((LP, TN), lambda j: (0, j))
    return pl.pallas_call(
        body, grid=(D_FF // TN,),
        in_specs=[pl.BlockSpec((LP, D_MODEL), lambda j: (0, 0)),
                  pl.BlockSpec((TN, D_MODEL), lambda j: (j, 0)), blk, blk],
        out_specs=[blk, blk],
        out_shape=[SDS((LP, D_FF), BF16)] * 2,
        compiler_params=_cparams(("parallel",)), name="ffn_down_bwd")(dff, w_dn_b, g, u)


def _ffn_up_bwd(dg, du, w_gu_b, h1, a, dy, g_pre_ffn, g_post_mix):
    def body(dg_ref, du_ref, w_ref, h1_ref, a_ref, dy_ref, gf_ref, gp_ref,
             dh1_ref, da_ref, dgf_ref, dgp_ref, acc):
        i = pl.program_id(0)
        s = pl.program_id(1)

        @pl.when((i == 0) & (s == 0))
        def _():
            dgf_ref[...] = jnp.zeros_like(dgf_ref)
            dgp_ref[...] = jnp.zeros_like(dgp_ref)

        @pl.when(s == 0)
        def _():
            acc[...] = jnp.zeros_like(acc)

        @pl.when(s < 2)
        def _():
            acc[...] += lax.dot_general(dg_ref[...], w_ref[...], NT_DIMS, preferred_element_type=F32)

        @pl.when(s >= 2)
        def _():
            acc[...] += lax.dot_general(du_ref[...], w_ref[...], NT_DIMS, preferred_element_type=F32)

        @pl.when(s == 3)
        def _():
            dhn2 = acc[...]
            h1 = h1_ref[...]
            r2 = _rms(h1)
            n2 = h1 * r2
            dgf_ref[...] += jnp.sum(dhn2 * n2, axis=0, keepdims=True)
            dh1 = dy_ref[...] + _rms_bwd(n2, r2, gf_ref[...] * dhn2)
            dh1_ref[...] = dh1
            av = a_ref[...]
            ra = _rms(av)
            na = av * ra
            dgp_ref[...] += jnp.sum(dh1 * na, axis=0, keepdims=True)
            da_ref[...] = _rms_bwd(na, ra, gp_ref[...] * dh1).astype(BF16)

    row = pl.BlockSpec((TM_EPI, D_MODEL), lambda i, s: (i, 0))
    vec = pl.BlockSpec((1, D_MODEL), lambda i, s: (0, 0))
    return pl.pallas_call(
        body, grid=(LP // TM_EPI, 4),
        in_specs=[pl.BlockSpec((TM_EPI, FF_T), lambda i, s: (i, jnp.minimum(s, 1))),
                  pl.BlockSpec((TM_EPI, FF_T), lambda i, s: (i, jnp.maximum(s - 2, 0))),
                  pl.BlockSpec((D_MODEL, FF_T), lambda i, s: (0, s)),
                  row, row, row, vec, vec],
        out_specs=[row, row, vec, vec],
        out_shape=[SDS((LP, D_MODEL), F32), SDS((LP, D_MODEL), BF16), SDS((1, D_MODEL), F32), SDS((1, D_MODEL), F32)],
        scratch_shapes=[pltpu.VMEM((TM_EPI, D_MODEL), F32)],
        compiler_params=_cparams(("arbitrary", "arbitrary")), name="ffn_up_bwd",
    )(dg, du, w_gu_b, h1, a, dy, g_pre_ffn, g_post_mix)


def _attn_out_bwd(da, w_out_b):
    def body(d_ref, w_ref, o_ref):
        o_ref[...] = lax.dot_general(d_ref[...], w_ref[...], NT_DIMS, preferred_element_type=F32).astype(BF16)

    row = pl.BlockSpec((TM_PURE, D_MODEL), lambda i: (i, 0))
    return pl.pallas_call(
        body, grid=(LP // TM_PURE,),
        in_specs=[row, pl.BlockSpec((D_MODEL, D_MODEL), lambda i: (0, 0))],
        out_specs=row, out_shape=SDS((LP, D_MODEL), BF16),
        compiler_params=_cparams(("parallel",)), name="attn_out_bwd")(da, w_out_b)


def _pre_mix_bwd(dq_a, dq_b, dk_b, dv_b, dk_a, dv_a, df, w_in_b, h0, dh1, g_pre_mix):
    n_steps = LP // TM_EPI

    def body(qa_ref, qb_ref, kb_ref, vb_ref, ka_ref, va_ref, f_ref, w_ref, h0_ref, dh1_ref, g_ref,
             dproj_ref, dx_ref, dmeta_ref, dg_ref, tile, sems):
        i = pl.program_id(0)

        def tile_copy(s):
            skip = ROW0 if s == 0 else 0
            return pltpu.make_async_copy(tile.at[s % 2, skip:TM_EPI],
                                         dx_ref.at[s * TM_EPI + skip - ROW0:(s + 1) * TM_EPI - ROW0],
                                         sems.at[s % 2])

        @pl.when(i == 0)
        def _():
            dg_ref[...] = jnp.zeros_like(dg_ref)

        for s in range(2, n_steps):
            @pl.when(i == s)
            def _():
                tile_copy(s - 2).wait()

        dproj = jnp.concatenate(
            [qa_ref[...], ka_ref[...].astype(BF16), va_ref[...].astype(BF16), (qb_ref[...] * SCALE).astype(BF16),
             kb_ref[...], vb_ref[...], f_ref[...].astype(BF16)], axis=1)
        dproj_ref[...] = dproj
        dhn = jnp.dot(dproj, w_ref[...], preferred_element_type=F32)
        x = h0_ref[...]
        r = _rms(x)
        n = x * r
        dg_ref[...] += jnp.sum(dhn * n, axis=0, keepdims=True)
        tile[i % 2] = dh1_ref[...] + _rms_bwd(n, r, g_ref[...] * dhn)

        for s in range(n_steps):
            @pl.when(i == s)
            def _():
                tile_copy(s).start()
                if s == 0:
                    dmeta_ref[...] = tile[0, PAD_ROWS:ROW0]
                if s == n_steps - 1:
                    tile_copy(s - 1).wait()
                    tile_copy(s).wait()

    row = lambda w: pl.BlockSpec((TM_EPI, w), lambda i: (i, 0))
    vec = pl.BlockSpec((1, D_MODEL), lambda i: (0, 0))
    return pl.pallas_call(
        body, grid=(n_steps,),
        in_specs=[row(512), row(512), row(512), row(512), row(BLK), row(BLK), row(BLK),
                  pl.BlockSpec((D_PROJ_P, D_MODEL), lambda i: (0, 0)), row(D_MODEL), row(D_MODEL), vec],
        out_specs=[row(D_PROJ_P), HBM_SPEC, pl.BlockSpec((N_META, D_MODEL), lambda i: (0, 0)), vec],
        out_shape=[SDS((LP, D_PROJ_P), BF16), SDS((SEQ, D_MODEL), F32), SDS((N_META, D_MODEL), F32),
                   SDS((1, D_MODEL), F32)],
        scratch_shapes=[pltpu.VMEM((2, TM_EPI, D_MODEL), F32), pltpu.SemaphoreType.DMA((2,))],
        compiler_params=_cparams(("arbitrary",)), name="pre_mix_bwd",
    )(dq_a, dq_b, dk_b, dv_b, dk_a, dv_a, df, w_in_b, h0, dh1, g_pre_mix)


def _mm_tn(parts, b, tm, name, out_dtype=F32):
    widths = [p.shape[1] for p in parts]
    m_total = sum(widths)
    n = b.shape[1]
    whole = len(parts) > 1
    n_k = LP // TK_W
    assert (tm == m_total) if whole else (m_total % tm == 0)

    def body(*refs):
        a_refs, b_ref, o_ref, acc = refs[:-3], refs[-3], refs[-2], refs[-1]
        k = pl.program_id(1)

        @pl.when(k == 0)
        def _():
            acc[...] = jnp.zeros_like(acc)
        a = a_refs[0][...] if not whole else jnp.concatenate([r[...] for r in a_refs], axis=1)
        acc[...] += lax.dot_general(a, b_ref[...], TN_DIMS, preferred_element_type=F32)

        @pl.when(k == n_k - 1)
        def _():
            o_ref[...] = acc[...].astype(out_dtype)

    a_specs = ([pl.BlockSpec((TK_W, w), lambda mi, k: (k, 0)) for w in widths] if whole
               else [pl.BlockSpec((TK_W, tm), lambda mi, k: (k, mi))])
    return pl.pallas_call(
        body, grid=(m_total // tm, n_k),
        in_specs=a_specs + [pl.BlockSpec((TK_W, n), lambda mi, k: (k, 0))],
        out_specs=pl.BlockSpec((tm, n), lambda mi, k: (mi, 0)),
        out_shape=SDS((m_total, n), out_dtype),
        scratch_shapes=[pltpu.VMEM((tm, n), F32)],
        compiler_params=_cparams(("parallel", "arbitrary")), name=name)(*parts, b)


def _dw_gate_up(hn2, dg, du):
    n_k = LP // TK_W

    def body(a_ref, dg_ref, du_ref, o_ref, acc):
        s = pl.program_id(0)
        k = pl.program_id(1)

        @pl.when(k == 0)
        def _():
            acc[...] = jnp.zeros_like(acc)

        @pl.when(s < 2)
        def _():
            acc[...] += lax.dot_general(a_ref[...], dg_ref[...], TN_DIMS, preferred_element_type=F32)

        @pl.when(s >= 2)
        def _():
            acc[...] += lax.dot_general(a_ref[...], du_ref[...], TN_DIMS, preferred_element_type=F32)

        @pl.when(k == n_k - 1)
        def _():
            o_ref[0] = acc[...].astype(BF16)

    return pl.pallas_call(
        body, grid=(4, n_k),
        in_specs=[pl.BlockSpec((TK_W, D_MODEL), lambda s, k: (k, 0)),
                  pl.BlockSpec((TK_W, FF_T), lambda s, k: (k, jnp.minimum(s, 1))),
                  pl.BlockSpec((TK_W, FF_T), lambda s, k: (k, jnp.maximum(s - 2, 0)))],
        out_specs=pl.BlockSpec((1, D_MODEL, FF_T), lambda s, k: (s, 0, 0)),
        out_shape=SDS((4, D_MODEL, FF_T), BF16),
        scratch_shapes=[pltpu.VMEM((D_MODEL, FF_T), F32)],
        compiler_params=_cparams(("parallel", "arbitrary")), name="dw_gate_up")(hn2, dg, du)


def _split3(x):
    hi = x.astype(BF16)
    r1 = x - hi.astype(F32)
    mid = r1.astype(BF16)
    lo = (r1 - mid.astype(F32)).astype(BF16)
    return hi, mid, lo


def _tri_matmul(tri, x):
    hi, mid, lo = _split3(x)
    dot = lambda t: jnp.dot(tri, t, preferred_element_type=F32)
    return dot(hi) + dot(mid) + dot(lo)


def _forget_cumsum(f, b_forget_p):
    def body(f_ref, b_ref, cum_ref, carry):
        i = pl.program_id(0)

        @pl.when(i == 0)
        def _():
            carry[...] = jnp.zeros_like(carry)

        z = f_ref[...] + b_ref[...]
        ls = jnp.minimum(z, 0.0) - jnp.log(1.0 + jnp.exp(-jnp.abs(z)))
        rows = i * TM + lax.broadcasted_iota(jnp.int32, (TM, BLK), 0)
        ls = jnp.where(rows >= PAD_ROWS, ls, 0.0)
        r = lax.broadcasted_iota(jnp.int32, (TM, TM), 0)
        c = lax.broadcasted_iota(jnp.int32, (TM, TM), 1)
        tri = (c <= r).astype(BF16)
        cum = _tri_matmul(tri, ls) + carry[...]
        cum_ref[...] = cum
        carry[...] = cum[TM - 1:TM, :]

    return pl.pallas_call(
        body, grid=(NT,),
        in_specs=[pl.BlockSpec((TM, BLK), lambda i: (i, 0)), pl.BlockSpec((1, BLK), lambda i: (0, 0))],
        out_specs=pl.BlockSpec((TM, BLK), lambda i: (i, 0)),
        out_shape=SDS((LP, BLK), F32),
        scratch_shapes=[pltpu.VMEM((1, BLK), F32)],
        compiler_params=_cparams(("arbitrary",)), name="forget_cumsum")(f, b_forget_p)


def _forget_cumsum_bwd(dcum, f, b_forget_p):
    def body(d_ref, f_ref, b_ref, df_ref, db_ref, carry):
        i = pl.program_id(0)

        @pl.when(i == 0)
        def _():
            carry[...] = jnp.zeros_like(carry)
            db_ref[...] = jnp.zeros_like(db_ref)

        blk = NT - 1 - i
        r = lax.broadcasted_iota(jnp.int32, (TM, TM), 0)
        c = lax.broadcasted_iota(jnp.int32, (TM, TM), 1)
        tri = (c >= r).astype(BF16)
        d = d_ref[...]
        dls = _tri_matmul(tri, d) + carry[...]
        carry[...] = dls[0:1, :]
        z = f_ref[...] + b_ref[...]
        rows = blk * TM + lax.broadcasted_iota(jnp.int32, (TM, BLK), 0)
        df = jnp.where(rows >= PAD_ROWS, dls / (1.0 + jnp.exp(z)), 0.0)
        df_ref[...] = df
        db_ref[...] += jnp.sum(df, axis=0, keepdims=True)

    rev = pl.BlockSpec((TM, BLK), lambda i: (NT - 1 - i, 0))
    vec = pl.BlockSpec((1, BLK), lambda i: (0, 0))
    return pl.pallas_call(
        body, grid=(NT,),
        in_specs=[rev, rev, vec],
        out_specs=[rev, vec],
        out_shape=[SDS((LP, BLK), F32), SDS((1, BLK), F32)],
        scratch_shapes=[pltpu.VMEM((1, BLK), F32)],
        compiler_params=_cparams(("arbitrary",)), name="forget_cumsum_bwd")(dcum, f, b_forget_p)


def _lane_half(rows):
    return lax.broadcasted_iota(jnp.int32, (rows, BLK), 1) // HALF


def _fox_valid(qi, kj):
    qrow = qi * TM + lax.broadcasted_iota(jnp.int32, (TM, TM), 0)
    krow = kj * TM + lax.broadcasted_iota(jnp.int32, (TM, TM), 1)
    return (krow <= qrow) & ((krow >= PAD_ROWS) | (qrow < PAD_ROWS))


class _Rider:
    def __init__(self, operands, out_shapes, sem_counts, first, middle, last):
        self.operands, self.out_shapes, self.sem_counts = list(operands), list(out_shapes), list(sem_counts)
        self.first, self.middle, self.last = first, middle, last

    def scratch(self):
        return [pltpu.SemaphoreType.DMA((k,)) for k in self.sem_counts]

    def split(self, refs, n_in, n_out, n_scratch):
        a, b = len(self.operands), len(self.out_shapes)
        ins, mine_in = refs[:n_in], refs[n_in:n_in + a]
        outs, mine_out = refs[n_in + a:n_in + a + n_out], refs[n_in + a + n_out:n_in + a + n_out + b]
        rest = refs[n_in + a + n_out + b:]
        return ins, outs, rest[:n_scratch], (mine_in, mine_out, rest[n_scratch:])

    def at_steps(self, mine, is_first, is_middle, is_last):
        for cond, fn in ((is_first, self.first), (is_middle, self.middle), (is_last, self.last)):
            pl.when(cond)(lambda fn=fn: fn(*mine))


HBM_SPEC = pl.BlockSpec(memory_space=pltpu.HBM)


N_AUG = 3
QCHUNKS = ((0, 128), (128, 128), (256, 128))
KSUB = 384
AHEAD = 5
AHEAD_BWD = 1


def _fox_prep(proj, cum):
    def body(q0_ref, q1_ref, k0_ref, k1_ref, v0_ref, v1_ref, c_ref, qa_ref, ka_ref, vt_ref):
        half = _lane_half(TM)
        lane = lax.broadcasted_iota(jnp.int32, (TM, BLK), 1)
        for pp in range(4):
            cols = slice(pp * BLK, (pp + 1) * BLK)
            q_ref, k_ref, v_ref = ((q0_ref, k0_ref, v0_ref), (q1_ref, k1_ref, v1_ref))[pp // 2]
            part = slice((pp % 2) * BLK, (pp % 2 + 1) * BLK)
            qs = q_ref[:, part].astype(F32) * (SCALE * LOG2E)
            kp = k_ref[:, part].astype(F32)
            vp = v_ref[:, part]
            vt_ref[cols, :] = vp.astype(F32).T.astype(BF16)
            for e in range(2):
                h = 2 * pp + e
                a = (1 - e) * HALF
                blk = slice(h * BLK, (h + 1) * BLK)
                hi, mid, lo = _split3(-LOG2E * c_ref[:, h:h + 1])
                q_aug = jnp.where(half == e, qs, jnp.where((lane >= a) & (lane < a + N_AUG), 1.0, 0.0))
                k_aug = jnp.where(half == e, kp, jnp.where(
                    lane == a, hi.astype(F32), jnp.where(lane == a + 1, mid.astype(F32), jnp.where(
                        lane == a + 2, lo.astype(F32), 0.0))))
                qa_ref[blk, :] = q_aug.T.astype(BF16)
                ka_ref[:, blk] = k_aug.astype(BF16)

    row = lambda blk: pl.BlockSpec((TM, W2), lambda i: (i, blk))
    wide = pl.BlockSpec((TM, 1024), lambda i: (i, 0))
    return pl.pallas_call(
        body, grid=(NT,),
        in_specs=[row(QB), row(QB + 1), row(KB), row(KB + 1), row(VB), row(VB + 1),
                  pl.BlockSpec((TM, BLK), lambda i: (i, 0))],
        out_specs=[pl.BlockSpec((1024, TM), lambda i: (0, i)), wide, pl.BlockSpec((512, TM), lambda i: (0, i))],
        out_shape=[SDS((1024, LP), BF16), SDS((LP, 1024), BF16), SDS((512, LP), BF16)],
        compiler_params=_cparams(("parallel",)), name="fox_prep")(proj, proj, proj, proj, proj, proj, cum)


def _over_keys(reduce, x):
    slabs = x.reshape(x.shape[0] // HALF, HALF, x.shape[1])
    return reduce(reduce(slabs, axis=0), axis=0, keepdims=True)


def _fox_valid_t(qi, kj, c, r):
    krow = kj * TM + r * KSUB + lax.broadcasted_iota(jnp.int32, (KSUB, c[1]), 0)
    qrow = qi * TM + c[0] + lax.broadcasted_iota(jnp.int32, (KSUB, c[1]), 1)
    return (krow <= qrow) & ((krow >= PAD_ROWS) | (qrow < PAD_ROWS))


def _fox_fwd(q_aug, k_aug, v_t, rider):
    pairs = [(qi, kj) for qi in range(NT) for kj in range(qi + 1)]
    n_pairs = len(pairs)

    def body(qi_ref, kj_ref, *refs):
        (q_ref, k_ref, vt_ref), (o_ref, lse_ref), (m_s, l_s, acc_s), mine = rider.split(refs, 3, 2, 3)
        n = pl.program_id(0)
        qi = qi_ref[n]
        kj = kj_ref[n]
        rider.at_steps(mine, n == 0, n == n_pairs // 2, n == n_pairs - 1)

        @pl.when(kj == 0)
        def _():
            m_s[...] = jnp.full_like(m_s, NEG)
            l_s[...] = jnp.zeros_like(l_s)
            acc_s[...] = jnp.zeros_like(acc_s)

        def tile(masked):
            steps = [(h, c, r) for h in range(N_HEADS) for c in QCHUNKS for r in range(TM // KSUB)]

            def scores(h, c, r):
                blk = slice(h * BLK, (h + 1) * BLK)
                return jnp.dot(k_ref[r * KSUB:(r + 1) * KSUB, blk], q_ref[blk, c[0]:c[0] + c[1]],
                               preferred_element_type=F32)

            ahead = [scores(*st) for st in steps[:AHEAD]]
            for n, (h, c, r) in enumerate(steps):
                s_t = ahead.pop(0)
                if n + AHEAD < len(steps):
                    ahead.append(scores(*steps[n + AHEAD]))
                cs = slice(c[0], c[0] + c[1])
                if masked:
                    s_t = jnp.where(_fox_valid_t(qi, kj, c, r), s_t, NEG)
                m_prev = m_s[h, :, cs]
                m_new = jnp.maximum(m_prev, _over_keys(jnp.max, s_t))
                p_t = jnp.exp2(s_t - m_new)
                alpha = jnp.exp2(m_prev - m_new)
                l_s[h, :, cs] = alpha * l_s[h, :, cs] + _over_keys(jnp.sum, p_t)
                m_s[h, :, cs] = m_new
                vt = vt_ref[h * HALF:(h + 1) * HALF, r * KSUB:(r + 1) * KSUB]
                acc_s[h, :, cs] = acc_s[h, :, cs] * alpha + jnp.dot(vt, p_t.astype(BF16),
                                                                    preferred_element_type=F32)

        @pl.when((kj < qi) & (kj > 0))
        def _():
            tile(False)

        @pl.when((kj == qi) | (kj == 0))
        def _():
            tile(True)

        @pl.when(kj == qi)
        def _():
            for pp in range(4):
                both = jnp.concatenate([acc_s[2 * pp] * (1.0 / l_s[2 * pp]),
                                        acc_s[2 * pp + 1] * (1.0 / l_s[2 * pp + 1])], axis=0)
                o_ref[:, pp * BLK:(pp + 1) * BLK] = both.T.astype(BF16)
            for h in range(N_HEADS):
                lse_ref[h] = m_s[h] * LN2 + jnp.log(l_s[h])

    grid_spec = pltpu.PrefetchScalarGridSpec(
        num_scalar_prefetch=2, grid=(n_pairs,),
        in_specs=[pl.BlockSpec((1024, TM), lambda n, qi, kj: (0, qi[n])),
                  pl.BlockSpec((TM, 1024), lambda n, qi, kj: (kj[n], 0)),
                  pl.BlockSpec((512, TM), lambda n, qi, kj: (0, kj[n]))] + [HBM_SPEC] * len(rider.operands),
        out_specs=[pl.BlockSpec((TM, 512), lambda n, qi, kj: (qi[n], 0)),
                   pl.BlockSpec((N_HEADS, 1, TM), lambda n, qi, kj: (0, 0, qi[n]))]
        + [HBM_SPEC] * len(rider.out_shapes),
        scratch_shapes=[pltpu.VMEM((N_HEADS, 1, TM), F32), pltpu.VMEM((N_HEADS, 1, TM), F32),
                        pltpu.VMEM((N_HEADS, HALF, TM), F32)] + rider.scratch())
    o_b, lse, *carried = pl.pallas_call(
        body, grid_spec=grid_spec,
        out_shape=[SDS((LP, 512), BF16), SDS((N_HEADS, 1, LP), F32)] + rider.out_shapes,
        compiler_params=_cparams(("arbitrary",)), name="fox_fwd",
    )(jnp.asarray([p[0] for p in pairs], jnp.int32), jnp.asarray([p[1] for p in pairs], jnp.int32),
      q_aug, k_aug, v_t, *rider.operands)
    return o_b, lse, carried


def _fox_bwd(proj, o_b, dmix, lse, ck_t, rider):
    pairs = [(kj, qi) for kj in range(NT) for qi in range(kj, NT)]
    n_pairs = len(pairs)

    def body(kj_ref, qi_ref, *refs):
        ((q0_ref, q1_ref, k0_ref, k1_ref, v0_ref, v1_ref, o_ref, do_ref, lse_ref, ck_ref),
         (dq_ref, dk_ref, dv_ref, dck_ref, dcq_ref), (dk_s, dv_s, dck_s), mine) = rider.split(refs, 10, 5, 3)
        n = pl.program_id(0)
        kj = kj_ref[n]
        qi = qi_ref[n]
        rider.at_steps(mine, n == 0, n == n_pairs // 2, n == n_pairs - 1)

        @pl.when(n == 0)
        def _():
            dq_ref[...] = jnp.zeros_like(dq_ref)
            dcq_ref[...] = jnp.zeros_like(dcq_ref)

        @pl.when(qi == kj)
        def _():
            dk_s[...] = jnp.zeros_like(dk_s)
            dv_s[...] = jnp.zeros_like(dv_s)
            dck_s[...] = jnp.zeros_like(dck_s)

        def tile(masked):
            valid = _fox_valid(qi, kj) if masked else None
            half = _lane_half(TM)
            q0 = pl.multiple_of(qi * TM, TM)
            lane = lax.broadcasted_iota(jnp.int32, (TM, BLK), 1)
            row_sums = jnp.zeros((TM, BLK), F32)
            pair_ops = {}

            def operands(pp):
                if pp not in pair_ops:
                    cols = slice(pp * BLK, (pp + 1) * BLK)
                    q_ref, k_ref, v_ref = ((q0_ref, k0_ref, v0_ref), (q1_ref, k1_ref, v1_ref))[pp // 2]
                    part = slice((pp % 2) * BLK, (pp % 2 + 1) * BLK)
                    pair_ops[pp] = ((q_ref[:, part].astype(F32) * SCALE).astype(BF16), k_ref[:, part],
                                    v_ref[:, part], do_ref[:, cols])
                return pair_ops[pp]

            def scores(pp, e):
                qs, kp, vp, dop = operands(pp)
                ke = jnp.where(half == e, kp, jnp.zeros_like(kp))
                ve = jnp.where(half == e, vp, jnp.zeros_like(vp))
                return (lax.dot_general(qs, ke, NT_DIMS, preferred_element_type=F32),
                        lax.dot_general(dop, ve, NT_DIMS, preferred_element_type=F32), ke)

            steps = [(pp, e) for pp in range(4) for e in range(2)]
            ahead = [scores(*st) for st in steps[:AHEAD_BWD]]
            for n, (pp, e) in enumerate(steps):
                raw, dp, ke = ahead.pop(0)
                if n + AHEAD_BWD < len(steps):
                    ahead.append(scores(*steps[n + AHEAD_BWD]))
                h = 2 * pp + e
                cols = slice(pp * BLK, (pp + 1) * BLK)
                qs, kp, vp, dop = operands(pp)
                if e == 0:
                    prod = dop.astype(F32) * o_ref[:, cols].astype(F32)
                    d0 = jnp.sum(jnp.where(half == 0, prod, 0.0), axis=1, keepdims=True)
                    d1 = jnp.sum(prod, axis=1, keepdims=True) - d0
                    dq = jnp.zeros((TM, BLK), F32)
                    dks, dvs = [], []
                t = raw - ck_ref[h] - lse_ref[h]
                if masked:
                    t = jnp.where(valid, t, NEG)
                p = jnp.exp(t)
                ds = p * (dp - (d0 if e == 0 else d1))
                dck_s[h] += jnp.sum(ds, axis=0, keepdims=True)
                row_sums = jnp.where(lane == h, jnp.sum(ds, axis=1, keepdims=True), row_sums)
                ds_b = ds.astype(BF16)
                dq = dq + jnp.dot(ds_b, ke, preferred_element_type=F32)
                dks.append(lax.dot_general(ds_b, qs, TN_DIMS, preferred_element_type=F32))
                dvs.append(lax.dot_general(p.astype(BF16), dop, TN_DIMS, preferred_element_type=F32))
                if e == 1:
                    dq_ref[pl.ds(q0, TM), cols] += dq
                    dk_s[pp] += jnp.where(half == 0, dks[0], dks[1])
                    dv_s[pp] += jnp.where(half == 0, dvs[0], dvs[1])
            dcq_ref[pl.ds(q0, TM), :] += row_sums

        @pl.when((qi > kj) & (kj > 0))
        def _():
            tile(False)

        @pl.when((qi == kj) | (kj == 0))
        def _():
            tile(True)

        @pl.when(qi == NT - 1)
        def _():
            for pp in range(4):
                cols = slice(pp * BLK, (pp + 1) * BLK)
                dk_ref[:, cols] = dk_s[pp].astype(BF16)
                dv_ref[:, cols] = dv_s[pp].astype(BF16)
            dck_ref[...] = dck_s[...]

    qrow = lambda blk, w=512: pl.BlockSpec((TM, w), lambda n, kj, qi: (qi[n], blk))
    krow = lambda blk: pl.BlockSpec((TM, W2), lambda n, kj, qi: (kj[n], blk))
    grid_spec = pltpu.PrefetchScalarGridSpec(
        num_scalar_prefetch=2, grid=(n_pairs,),
        in_specs=[qrow(QB, W2), qrow(QB + 1, W2), krow(KB), krow(KB + 1), krow(VB), krow(VB + 1), qrow(0), qrow(1),
                  pl.BlockSpec((N_HEADS, TM, 1), lambda n, kj, qi: (0, qi[n], 0)),
                  pl.BlockSpec((N_HEADS, 1, TM), lambda n, kj, qi: (0, 0, kj[n]))] + [HBM_SPEC] * len(rider.operands),
        out_specs=[pl.BlockSpec((LP, 512), lambda n, kj, qi: (0, 0)),
                   pl.BlockSpec((TM, 512), lambda n, kj, qi: (kj[n], 0)),
                   pl.BlockSpec((TM, 512), lambda n, kj, qi: (kj[n], 0)),
                   pl.BlockSpec((N_HEADS, 1, TM), lambda n, kj, qi: (0, 0, kj[n])),
                   pl.BlockSpec((LP, BLK), lambda n, kj, qi: (0, 0))] + [HBM_SPEC] * len(rider.out_shapes),
        scratch_shapes=[pltpu.VMEM((4, TM, BLK), F32), pltpu.VMEM((4, TM, BLK), F32),
                        pltpu.VMEM((N_HEADS, 1, TM), F32)] + rider.scratch())
    dq, dk, dv, dck, dcq, *carried = pl.pallas_call(
        body, grid_spec=grid_spec,
        out_shape=[SDS((LP, 512), F32), SDS((LP, 512), BF16), SDS((LP, 512), BF16), SDS((N_HEADS, 1, LP), F32),
                   SDS((LP, BLK), F32)] + rider.out_shapes,
        compiler_params=_cparams(("arbitrary",)), name="fox_bwd",
    )(jnp.asarray([p[0] for p in pairs], jnp.int32), jnp.asarray([p[1] for p in pairs], jnp.int32),
      proj, proj, proj, proj, proj, proj, o_b, dmix, lse, ck_t, *rider.operands)
    return dq, dk, dv, dck, dcq, carried


N_SEG = 3
N_KEY = N_SEG * BLK
GROUP = 4
QW = GROUP * BLK


def _bucket_tables_t():
    return np.ascontiguousarray(_bucket_tables().transpose(0, 2, 1))


def _stack_heads(ref, g, scale):
    half = _lane_half(BLK)
    out = []
    for pair in range(2):
        x = ref[:, (2 * g + pair) * BLK:(2 * g + pair + 1) * BLK].astype(F32) * scale
        swapped = pltpu.roll(x, HALF, 1)
        for e in range(2):
            out.append(jnp.where(half == g, x if e == g else swapped, 0.0).astype(BF16))
    return jnp.concatenate(out, axis=0)


def _unstack_heads(x_t, g, ref, scale):
    for pair in range(2):
        both = jnp.concatenate([x_t[:, (2 * pair) * BLK:(2 * pair + 1) * BLK],
                                x_t[:, (2 * pair + 1) * BLK:(2 * pair + 2) * BLK]], axis=0)
        ref[:, (2 * g + pair) * BLK:(2 * g + pair + 1) * BLK] = (both.T * scale).astype(ref.dtype)


def _swa_tables(tab_ref, sink_ref, bkt_ref, tbl, sink_row):
    kk = lax.broadcasted_iota(jnp.int32, (BLK, BLK), 0)
    qq = lax.broadcasted_iota(jnp.int32, (BLK, BLK), 1)
    neg = jnp.full((BLK, BLK), NEG, F32)
    lane = lax.broadcasted_iota(jnp.int32, (1, QW), 1) // BLK
    for g in range(2):
        row = jnp.zeros((1, QW), F32)
        for hh in range(GROUP):
            h = GROUP * g + hh
            cols = slice(hh * BLK, (hh + 1) * BLK)
            row = jnp.where(lane == hh, sink_ref[0, h], row)

            def step(b, carry, h=h):
                t = tab_ref[b, h]
                return jnp.where(bkt_ref[0] == b, t, carry[0]), jnp.where(bkt_ref[1] == b, t, carry[1])
            zero = jnp.zeros((BLK, BLK), F32)
            cur, prev = lax.fori_loop(0, N_BUCKETS, step, (zero, zero))
            far = jnp.full((BLK, BLK), tab_ref[N_BUCKETS - 1, h], F32)
            causal = jnp.where(kk <= qq, cur, neg)
            segments = [
                (neg, neg, jnp.where(kk >= PAD_ROWS, causal, neg)),
                (jnp.where(kk >= PAD_ROWS, prev, neg), neg, causal),
                (jnp.where(kk >= PAD_ROWS, far, neg), jnp.where(kk > qq, prev, neg), causal)]
            for case in range(3):
                for seg in range(N_SEG):
                    tbl[case, g, seg * BLK:(seg + 1) * BLK, cols] = segments[case][seg]
        sink_row[g] = row


def _swa_prep(proj):
    rows = LP // 3

    def body(k_ref, v_ref, kt_ref, vt_ref):
        kt_ref[...] = k_ref[...].astype(F32).T.astype(BF16)
        vt_ref[...] = v_ref[...].astype(F32).T.astype(BF16)

    col = pl.BlockSpec((BLK, rows), lambda i: (0, i))
    return pl.pallas_call(
        body, grid=(3,),
        in_specs=[pl.BlockSpec((rows, BLK), lambda i: (i, KA)), pl.BlockSpec((rows, BLK), lambda i: (i, VA))],
        out_specs=[col, col], out_shape=[SDS((BLK, LP), BF16)] * 2,
        compiler_params=_cparams(("parallel",)), name="swa_prep")(proj, proj)


def _segments(ref, i, by_rows):
    starts = [0, pl.multiple_of(jnp.maximum(i - 1, 0) * BLK, BLK), pl.multiple_of(i * BLK, BLK)]
    if by_rows:
        return jnp.concatenate([ref[pl.ds(s, BLK), :] for s in starts], axis=0)
    return jnp.concatenate([ref[:, pl.ds(s, BLK)] for s in starts], axis=1)


def _swa_fwd(proj, vt_a, rel_bias, sinks, bkt_t, rider):
    def body(*refs):
        ((tab_ref, sink_ref, bkt_ref, q_ref, k_ref, vt_ref), (o_ref, lse_ref),
         (tbl, sink_row), mine) = rider.split(refs, 6, 2, 2)
        i = pl.program_id(0)
        rider.at_steps(mine, i == 0, i == NBLK // 2, i == NBLK - 1)

        @pl.when(i == 0)
        def _():
            _swa_tables(tab_ref, sink_ref, bkt_ref, tbl, sink_row)

        case = jnp.minimum(i, 2)
        k_cat = _segments(k_ref, i, True)
        vt_cat = _segments(vt_ref, i, False)
        raw = [lax.dot_general(k_cat, _stack_heads(q_ref, g, SCALE), NT_DIMS, preferred_element_type=F32)
               for g in range(2)]
        for g in range(2):
            s_t = raw[g] + tbl[case, g]
            sink = sink_row[g]
            m = jnp.maximum(_over_keys(jnp.max, s_t), sink)
            p_t = jnp.exp(s_t - m)
            l = _over_keys(jnp.sum, p_t) + jnp.exp(sink - m)
            o_t = jnp.dot(vt_cat[g * HALF:(g + 1) * HALF, :], p_t.astype(BF16), preferred_element_type=F32)
            _unstack_heads(o_t * (1.0 / l), g, o_ref, 1.0)
            lse = m + jnp.log(l)
            for hh in range(GROUP):
                lse_ref[GROUP * g + hh] = lse[:, hh * BLK:(hh + 1) * BLK]

    smem = pl.BlockSpec(memory_space=pltpu.SMEM)
    o_a, lse, *carried = pl.pallas_call(
        body, grid=(NBLK,),
        in_specs=[smem, smem, pl.BlockSpec((2, BLK, BLK), lambda i: (0, 0, 0)),
                  pl.BlockSpec((BLK, 512), lambda i: (i, QA)), pl.BlockSpec((LP, BLK), lambda i: (0, KA)),
                  pl.BlockSpec((BLK, LP), lambda i: (0, 0))] + [HBM_SPEC] * len(rider.operands),
        out_specs=[pl.BlockSpec((BLK, 512), lambda i: (i, 0)),
                   pl.BlockSpec((N_HEADS, 1, BLK), lambda i: (0, 0, i))] + [HBM_SPEC] * len(rider.out_shapes),
        out_shape=[SDS((LP, 512), BF16), SDS((N_HEADS, 1, LP), F32)] + rider.out_shapes,
        scratch_shapes=[pltpu.VMEM((3, 2, N_KEY, QW), F32), pltpu.VMEM((2, 1, QW), F32)] + rider.scratch(),
        compiler_params=_cparams(("arbitrary",)), name="swa_fwd",
    )(rel_bias, sinks, bkt_t, proj, proj, vt_a, *rider.operands)
    return o_a, lse, carried


def _swa_bwd(proj, kt_a, o_a, dmix, lse, rel_bias, sinks, bkt_t):
    def body(tab_ref, sink_ref, bkt_ref, q_ref, k_ref, v_ref, kt_ref, o_ref, do_ref, lse_ref,
             dq_ref, dk_ref, dv_ref, dbias_ref, dsink_ref, tbl, sink_row, acc, dsk):
        i = pl.program_id(0)

        @pl.when(i == 0)
        def _():
            _swa_tables(tab_ref, sink_ref, bkt_ref, tbl, sink_row)
            dk_ref[...] = jnp.zeros_like(dk_ref)
            dv_ref[...] = jnp.zeros_like(dv_ref)
            acc[...] = jnp.zeros_like(acc)
            dsk[...] = jnp.zeros_like(dsk)

        case = jnp.minimum(i, 2)
        first = jnp.full((BLK, QW), i, jnp.int32) == 1
        k_cat = _segments(k_ref, i, True)
        v_cat = _segments(v_ref, i, True)
        kt_cat = _segments(kt_ref, i, False)
        dk_cat = jnp.zeros((N_KEY, BLK), F32)
        dv_cat = jnp.zeros((N_KEY, BLK), F32)
        for g in range(2):
            d_parts = []
            for pair in range(2):
                cols = slice((2 * g + pair) * BLK, (2 * g + pair + 1) * BLK)
                prod_t = (do_ref[:, cols].astype(F32) * o_ref[:, cols].astype(F32)).T
                d_parts += [jnp.sum(prod_t[:HALF], axis=0, keepdims=True),
                            jnp.sum(prod_t[HALF:], axis=0, keepdims=True)]
            d_row = jnp.concatenate(d_parts, axis=1)
            lse_row = jnp.concatenate([lse_ref[GROUP * g + hh] for hh in range(GROUP)], axis=1)
            q_st = _stack_heads(q_ref, g, SCALE)
            do_st = _stack_heads(do_ref, g, 1.0)
            s_t = lax.dot_general(k_cat, q_st, NT_DIMS, preferred_element_type=F32) + tbl[case, g]
            p_t = jnp.exp(s_t - lse_row)
            dp_t = lax.dot_general(v_cat, do_st, NT_DIMS, preferred_element_type=F32)
            ds_t = p_t * (dp_t - d_row)
            dsk[g] += -jnp.exp(sink_row[g] - lse_row) * d_row
            acc[g, 0:BLK] += jnp.where(first, 0.0, ds_t[0:BLK])
            acc[g, BLK:2 * BLK] += jnp.where(first, ds_t[0:BLK], ds_t[BLK:2 * BLK])
            acc[g, 2 * BLK:N_KEY] += ds_t[2 * BLK:N_KEY]
            ds_b = ds_t.astype(BF16)
            dk_cat = dk_cat + jnp.dot(ds_b, q_st, preferred_element_type=F32)
            dv_cat = dv_cat + jnp.dot(p_t.astype(BF16), do_st, preferred_element_type=F32)
            dq_t = jnp.dot(kt_cat[g * HALF:(g + 1) * HALF, :], ds_b, preferred_element_type=F32)
            _unstack_heads(dq_t, g, dq_ref, SCALE)

        prev0 = pl.multiple_of(jnp.maximum(i - 1, 0) * BLK, BLK)
        cur0 = pl.multiple_of(i * BLK, BLK)
        for ref, cat in ((dk_ref, dk_cat), (dv_ref, dv_cat)):
            ref[0:BLK, :] += cat[0:BLK]
            ref[pl.ds(prev0, BLK), :] += cat[BLK:2 * BLK]
            ref[pl.ds(cur0, BLK), :] += cat[2 * BLK:N_KEY]

        @pl.when(i == NBLK - 1)
        def _():
            lane = lax.broadcasted_iota(jnp.int32, (1, BLK), 1)

            def per_bucket(b, carry):
                row = jnp.zeros((1, BLK), F32)
                for h in range(N_HEADS):
                    g, cols = h // GROUP, slice((h % GROUP) * BLK, (h % GROUP + 1) * BLK)
                    val = (jnp.sum(jnp.where(bkt_ref[0] == b, acc[g, 2 * BLK:N_KEY, cols], 0.0), keepdims=True)
                           + jnp.sum(jnp.where(bkt_ref[1] == b, acc[g, BLK:2 * BLK, cols], 0.0), keepdims=True))
                    row = jnp.where(lane == h, val, row)
                dbias_ref[pl.ds(b, 1), :] = row
                return carry

            lax.fori_loop(0, N_BUCKETS, per_bucket, 0)
            far = jnp.zeros((1, BLK), F32)
            dsr = jnp.zeros((1, BLK), F32)
            for h in range(N_HEADS):
                g, cols = h // GROUP, slice((h % GROUP) * BLK, (h % GROUP + 1) * BLK)
                far = jnp.where(lane == h, jnp.sum(acc[g, 0:BLK, cols], keepdims=True), far)
                dsr = jnp.where(lane == h, jnp.sum(dsk[g, :, cols], keepdims=True), dsr)
            dbias_ref[N_BUCKETS - 1:N_BUCKETS, :] += far
            dsink_ref[...] = dsr

    smem = pl.BlockSpec(memory_space=pltpu.SMEM)
    blk512 = lambda col: pl.BlockSpec((BLK, 512), lambda i: (i, col))
    full = lambda r, c: pl.BlockSpec((r, c), lambda i: (0, 0))
    return pl.pallas_call(
        body, grid=(NBLK,),
        in_specs=[smem, smem, pl.BlockSpec((2, BLK, BLK), lambda i: (0, 0, 0)), blk512(QA),
                  pl.BlockSpec((LP, BLK), lambda i: (0, KA)), pl.BlockSpec((LP, BLK), lambda i: (0, VA)),
                  full(BLK, LP), blk512(0), blk512(0), pl.BlockSpec((N_HEADS, 1, BLK), lambda i: (0, 0, i))],
        out_specs=[blk512(0), full(LP, BLK), full(LP, BLK), full(N_BUCKETS, BLK), full(1, BLK)],
        out_shape=[SDS((LP, 512), BF16), SDS((LP, BLK), F32), SDS((LP, BLK), F32),
                   SDS((N_BUCKETS, BLK), F32), SDS((1, BLK), F32)],
        scratch_shapes=[pltpu.VMEM((3, 2, N_KEY, QW), F32), pltpu.VMEM((2, 1, QW), F32),
                        pltpu.VMEM((2, N_KEY, QW), F32), pltpu.VMEM((2, 1, QW), F32)],
        compiler_params=_cparams(("arbitrary",)), name="swa_bwd",
    )(rel_bias, sinks, bkt_t, proj, proj, proj, kt_a, o_a, dmix, lse)


def _local_step(x, tgt, meta, rel_bias, g_pre_mix, g_post_mix, g_pre_ffn, g_post_ffn, b_forget, sinks,
                w_in_b, out_rider, out_weight, ffn_rider, ffn_weights, early_grads):
    bkt_t = jnp.asarray(_bucket_tables_t())
    b_p = jnp.pad(b_forget, ((0, 0), (0, BLK - N_HEADS)))

    h0, hn1, proj, f = _pre_mix(x, meta, g_pre_mix, w_in_b)
    kt_a, vt_a = _swa_prep(proj)
    o_a, lse_a, carried_out = _swa_fwd(proj, vt_a, rel_bias, sinks, bkt_t, out_rider)
    w_out_b = out_weight(carried_out)
    cum = _forget_cumsum(f, b_p)
    ck_t = cum[:, :N_HEADS].T.reshape(N_HEADS, 1, LP)
    q_aug, k_aug, v_t = _fox_prep(proj, cum)
    o_b, lse_row, carried = _fox_fwd(q_aug, k_aug, v_t, ffn_rider)
    lse_b = lse_row.reshape(N_HEADS, LP, 1)
    w_gu_b, w_dn_b = ffn_weights(carried)
    a, h1, hn2 = _attn_out(o_a, o_b, w_out_b, h0, g_post_mix, g_pre_ffn)
    g, u, act = _ffn_up(hn2, w_gu_b)
    dff, dy, loss_blk, dg_post_ffn = _ffn_down_loss(act, w_dn_b, h1, tgt, g_post_ffn)

    dw_dn = _mm_tn([act], dff, FF_T, "dw_down", BF16)
    dg, du = _ffn_down_bwd(dff, w_dn_b, g, u)
    dw_gu = _dw_gate_up(hn2, dg, du)
    dh1, da, dg_pre_ffn, dg_post_mix = _ffn_up_bwd(dg, du, w_gu_b, h1, a, dy, g_pre_ffn, g_post_mix)
    dw_out = _mm_tn([o_a, o_b], da, D_MODEL, "dw_out", BF16)
    dmix = _attn_out_bwd(da, w_out_b)
    dq_b, dk_b, dv_b, dck, dcq, landed = _fox_bwd(proj, o_b, dmix, lse_b, ck_t, early_grads(dw_gu, dw_dn, dw_out))
    dq_a, dk_a, dv_a, dbias, dsink = _swa_bwd(proj, kt_a, o_a, dmix, lse_a, rel_bias, sinks, bkt_t)
    dcum = dcq - jnp.pad(dck.reshape(N_HEADS, LP).T, ((0, 0), (0, BLK - N_HEADS)))
    df, db = _forget_cumsum_bwd(dcum, f, b_p)
    dproj, dx, dmeta, dg_pre_mix = _pre_mix_bwd(dq_a, dq_b, dk_b, dv_b, dk_a, dv_a, df, w_in_b, h0, dh1, g_pre_mix)
    dw_in = _mm_tn([hn1], dproj, D_MODEL, "dw_in", BF16)

    return dict(loss=loss_blk[0, 0], grad_x=dx, meta=dmeta,
                rel_bias=dbias[:, :N_HEADS], ln_pre_mix=dg_pre_mix, ln_post_mix=dg_post_mix,
                ln_pre_ffn=dg_pre_ffn, ln_post_ffn=dg_post_ffn, b_forget=db[:, :N_HEADS],
                sinks=dsink[:, :N_HEADS], w_in=dw_in, w_out=dw_out, w_gate_up=dw_gu, w_down=dw_dn,
                landed=landed)


N_SMALL = 24
LOSS_ROW = 6


def _place():
    x, y, c = lax.axis_index("x"), lax.axis_index("y"), lax.axis_index("c")
    return x, y, c, [(1 - x, y), (x, 1 - y), (1 - x, 1 - y)]


def _run_alone(rider, name):
    a, b = len(rider.operands), len(rider.out_shapes)

    def body(*refs):
        mine = (refs[:a], refs[a:a + b], refs[a + b:])
        rider.first(*mine)
        rider.middle(*mine)
        rider.last(*mine)

    return pl.pallas_call(body, in_specs=[HBM_SPEC] * a, out_specs=[HBM_SPEC] * b, out_shape=rider.out_shapes,
                          scratch_shapes=rider.scratch(), name=name)(*rider.operands)


def _gather_rider(shards, own_too, by_columns=()):
    n = len(shards)

    def slot(a, outs, chip, h):
        if a in by_columns:
            cols = shards[a].shape[2]
            return outs[a].at[h, :, pl.ds(pl.multiple_of(chip * cols, BLK), cols)]
        return outs[a].at[chip, h]

    def own_copies(ins, outs, sems):
        x, y, _, _ = _place()
        if not own_too:
            return []
        return [pltpu.make_async_copy(ins[a].at[h], slot(a, outs, 2 * x + y, h), sems[2].at[2 * a + h])
                for a in range(n) for h in range(2)]

    def copies(ins, outs, sems):
        send_sems, recv_sems = sems[:2]
        x, y, c, others = _place()
        chip = 2 * x + y
        sibling = (x, y, 1 - c)

        def rc(a, k, src, dst, to):
            return pltpu.make_async_remote_copy(src_ref=src, dst_ref=dst, send_sem=send_sems.at[6 * a + k],
                                                recv_sem=recv_sems.at[6 * a + k], device_id=to, device_id_type=MESH)

        pairs = [(a, j, ox, oy) for a in range(n) for j, (ox, oy) in enumerate(others)]
        there = lambda a, ox, oy, h: slot(a, outs, 2 * ox + oy, h)
        return dict(
            sent=lambda: [rc(a, j, ins[a].at[c], slot(a, outs, chip, c), (ox, oy, c)) for a, j, ox, oy in pairs],
            landed=lambda: [rc(a, j, there(a, ox, oy, c), there(a, ox, oy, c), sibling) for a, j, ox, oy in pairs],
            passed=lambda: [rc(a, 3 + j, there(a, ox, oy, c), there(a, ox, oy, c), sibling)
                            for a, j, ox, oy in pairs],
            arriving=lambda: [rc(a, 3 + j, there(a, ox, oy, 1 - c), there(a, ox, oy, 1 - c), sibling)
                              for a, j, ox, oy in pairs])

    def first(*mine):
        for cp in copies(*mine)["sent"]() + own_copies(*mine):
            cp.start()

    def middle(*mine):
        kinds = copies(*mine)
        for got, cp in zip(kinds["landed"](), kinds["passed"]()):
            got.wait_recv()
            cp.start()

    def last(*mine):
        kinds = copies(*mine)
        for cp in kinds["arriving"]():
            cp.wait_recv()
        for cp in kinds["sent"]() + kinds["passed"]():
            cp.wait_send()
        for cp in own_copies(*mine):
            cp.wait()

    shapes = [SDS((2, s.shape[1], 4 * s.shape[2]) if a in by_columns else (4,) + s.shape, s.dtype)
              for a, s in enumerate(shards)]
    return _Rider(shards, shapes, [6 * n, 6 * n] + [2 * n] * own_too, first, middle, last)


def _swap_rider(grads):
    n = len(grads)
    slabs = [(a, s) for a in range(n) for s in range(grads[a].shape[0])]

    def copies(ins, outs, sems):
        x, y, c, _ = _place()
        return [pltpu.make_async_remote_copy(
            src_ref=ins[a].at[s, 1 - c], dst_ref=outs[a].at[s], send_sem=sems[0].at[k], recv_sem=sems[1].at[k],
            device_id=(x, y, 1 - c), device_id_type=MESH) for k, (a, s) in enumerate(slabs)]

    def first(*mine):
        for cp in copies(*mine):
            cp.start()

    def middle(*mine):
        pass

    def last(*mine):
        for cp in copies(*mine):
            cp.wait()

    return _Rider(grads, [SDS(g.shape[:1] + g.shape[2:], g.dtype) for g in grads], [len(slabs), len(slabs)],
                  first, middle, last)


def _pair_sum(g, got, c_arr, name):
    n_s, rh, cc = got.shape

    def body(c_ref, g_ref, p_ref, o_ref):
        o_ref[0] = (g_ref[0, 0].astype(F32) + p_ref[0].astype(F32)).astype(BF16)

    grid_spec = pltpu.PrefetchScalarGridSpec(
        num_scalar_prefetch=1, grid=(n_s,),
        in_specs=[pl.BlockSpec((1, 1, rh, cc), lambda s, c_ref: (s, c_ref[0], 0, 0)),
                  pl.BlockSpec((1, rh, cc), lambda s, c_ref: (s, 0, 0))],
        out_specs=pl.BlockSpec((1, rh, cc), lambda s, c_ref: (s, 0, 0)))
    return pl.pallas_call(body, grid_spec=grid_spec, out_shape=SDS((n_s, rh, cc), BF16),
                          compiler_params=_cparams(("parallel",)), name=name)(c_arr, g, got)


def _direct_rider(grads):
    n = len(grads)

    def copies(ins, outs, sems):
        x, y, c, others = _place()
        peers = [(x, y, 1 - c)] + [(ox, oy, c) for ox, oy in others] + [(ox, oy, 1 - c) for ox, oy in others]
        return [pltpu.make_async_remote_copy(
            src_ref=ins[a].at[2 * px + py, pc], dst_ref=outs[a].at[k], send_sem=sems[0].at[7 * a + k],
            recv_sem=sems[1].at[7 * a + k], device_id=(px, py, pc), device_id_type=MESH)
            for a in range(n) for k, (px, py, pc) in enumerate(peers)]

    def first(*mine):
        for cp in copies(*mine):
            cp.start()

    def middle(*mine):
        pass

    def last(*mine):
        for cp in copies(*mine):
            cp.wait()

    return _Rider(grads, [SDS((7,) + g.shape[2:], g.dtype) for g in grads], [7 * n, 7 * n], first, middle, last)


def _owner_sum(grads, landed, own_arr, after, name):
    rh, cc = landed.shape[1:]
    tr = rh // 2

    def body(own_ref, g_ref, p_ref, after_ref, o_ref):
        total = g_ref[0, 0].astype(F32)
        for k in range(7):
            total = total + p_ref[k].astype(F32)
        o_ref[...] = total

    grid_spec = pltpu.PrefetchScalarGridSpec(
        num_scalar_prefetch=1, grid=(2,),
        in_specs=[pl.BlockSpec((1, 1, tr, cc), lambda i, own: (own[0], own[1], i, 0)),
                  pl.BlockSpec((7, tr, cc), lambda i, own: (0, i, 0)), pl.BlockSpec(memory_space=pl.ANY)],
        out_specs=pl.BlockSpec((tr, cc), lambda i, own: (i, 0)))
    return pl.pallas_call(body, grid_spec=grid_spec, out_shape=SDS((rh, cc), F32),
                          compiler_params=_cparams(("parallel",)), name=name)(own_arr, grads, landed, after)


SEM_SPEC = pl.BlockSpec(memory_space=pltpu.SEMAPHORE)
N_LATE = 10


def _late_copies(part_ref, landed_ref, small_ref, all_ref, send_sems, recv_sems):
    x, y, c, others = _place()
    me = 4 * x + 2 * y + c
    peers = [(x, y, 1 - c)] + [(ox, oy, c) for ox, oy in others] + [(ox, oy, 1 - c) for ox, oy in others]
    big = [pltpu.make_async_remote_copy(
        src_ref=part_ref.at[2 * ox + oy], dst_ref=landed_ref.at[j], send_sem=send_sems.at[j], recv_sem=recv_sems.at[j],
        device_id=(ox, oy, c), device_id_type=MESH) for j, (ox, oy) in enumerate(others)]
    small = [pltpu.make_async_remote_copy(
        src_ref=small_ref, dst_ref=all_ref.at[me], send_sem=send_sems.at[3 + k], recv_sem=recv_sems.at[3 + k],
        device_id=peer, device_id_type=MESH) for k, peer in enumerate(peers)]
    return big + small


def _late_exchange_start(part, small):
    def body(part_ref, landed_ref, small_ref, all_ref, send_sems, recv_sems, part_o, landed_o, small_o, all_o, token):
        for cp in _late_copies(part_ref, landed_ref, small_ref, all_ref, send_sems, recv_sems):
            cp.start()
        token[...] = jnp.zeros_like(token)

    hbm = lambda a: pltpu.HBM(a.shape, a.dtype)
    landed = lax.empty((3,) + part.shape[1:], part.dtype)
    everyone = lax.empty((8,) + small.shape, small.dtype)
    operands = [pltpu.with_memory_space_constraint(a, pltpu.HBM) for a in (part, landed, small, everyone)]
    return pl.pallas_call(
        body, name="late_exchange_start",
        out_shape=(pltpu.SemaphoreType.DMA((N_LATE,)), pltpu.SemaphoreType.DMA((N_LATE,)),
                   hbm(part), hbm(landed), hbm(small), hbm(everyone), SDS((8, BLK), F32)),
        in_specs=[HBM_SPEC] * 4,
        out_specs=(SEM_SPEC, SEM_SPEC, HBM_SPEC, HBM_SPEC, HBM_SPEC, HBM_SPEC, pl.BlockSpec(memory_space=pltpu.VMEM)),
        input_output_aliases={0: 2, 1: 3, 2: 4, 3: 5},
        compiler_params=pltpu.CompilerParams(has_side_effects=pltpu.SideEffectType.DATAFLOW_SIDE_EFFECTING),
    )(*operands)


def _late_exchange_wait(send_sems, recv_sems, part, landed, small, everyone, after):
    def body(part_ref, landed_ref, small_ref, all_ref, send_sems, recv_sems, after_ref, part_o, landed_o, small_o, all_o):
        for cp in _late_copies(part_ref, landed_ref, small_ref, all_ref, send_sems, recv_sems):
            cp.wait_send()
            cp.wait_recv()

    hbm = lambda a: pltpu.HBM(a.shape, a.dtype)
    out = pl.pallas_call(
        body, name="late_exchange_wait",
        out_shape=(hbm(part), hbm(landed), hbm(small), hbm(everyone)),
        in_specs=[HBM_SPEC] * 4 + [SEM_SPEC, SEM_SPEC, pl.BlockSpec(memory_space=pl.ANY)],
        out_specs=(HBM_SPEC,) * 4, input_output_aliases={0: 0, 1: 1, 2: 2, 3: 3},
        compiler_params=pltpu.CompilerParams(has_side_effects=pltpu.SideEffectType.DATAFLOW_SIDE_EFFECTING),
    )(part, landed, small, everyone, send_sems, recv_sems, after)
    return out[0], out[1], out[3]


def _chip_sum(parts, landed, chip_arr, name):
    rh, cc = landed.shape[1:]
    tr = rh // 2

    def body(chip_ref, own_ref, p_ref, o_ref):
        o_ref[...] = ((own_ref[0].astype(F32) + p_ref[0].astype(F32)) + p_ref[1].astype(F32)) + p_ref[2].astype(F32)

    grid_spec = pltpu.PrefetchScalarGridSpec(
        num_scalar_prefetch=1, grid=(2,),
        in_specs=[pl.BlockSpec((1, tr, cc), lambda i, chip_ref: (chip_ref[0], i, 0)),
                  pl.BlockSpec((3, tr, cc), lambda i, chip_ref: (0, i, 0))],
        out_specs=pl.BlockSpec((tr, cc), lambda i, chip_ref: (i, 0)))
    return pl.pallas_call(body, grid_spec=grid_spec, out_shape=SDS((rh, cc), F32),
                          compiler_params=_cparams(("parallel",)), name=name)(chip_arr, parts, landed)


def _device_sum(p):
    def body(p_ref, o_ref):
        acc = p_ref[0]
        for k in range(1, 8):
            acc = acc + p_ref[k]
        o_ref[...] = acc

    return pl.pallas_call(body, out_shape=SDS(p.shape[1:], F32), name="small_sum")(p)


def _join_halves(halves, name):
    n = len(halves)

    def body(*refs):
        ins, outs = refs[:n], refs[n:2 * n]
        send_sems, recv_sems = refs[2 * n:]
        x, y, c, _ = _place()
        copies = [pltpu.make_async_remote_copy(
            src_ref=ins[a], dst_ref=outs[a], send_sem=send_sems.at[a], recv_sem=recv_sems.at[a],
            device_id=(x, y, 1 - c), device_id_type=MESH) for a in range(n)]
        for cp in copies:
            cp.start()
        for cp in copies:
            cp.wait()

    return pl.pallas_call(
        body, in_specs=[HBM_SPEC] * n, out_specs=[HBM_SPEC] * n,
        out_shape=[SDS(h.shape, h.dtype) for h in halves],
        scratch_shapes=[pltpu.SemaphoreType.DMA((n,)), pltpu.SemaphoreType.DMA((n,))],
        name=name)(*halves)


def _adamw(w, g, m, v, name, tr=None):
    rows, cols = w.shape
    tr = tr or rows
    assert rows % tr == 0

    def body(w_ref, g_ref, m_ref, v_ref, d_ref, nm_ref, nv_ref):
        gg = g_ref[...]
        nm = ADAM_B1 * m_ref[...] + (1.0 - ADAM_B1) * gg
        nv = ADAM_B2 * v_ref[...] + (1.0 - ADAM_B2) * (gg * gg)
        nm_ref[...] = nm
        nv_ref[...] = nv
        m_hat = nm / (1.0 - ADAM_B1 ** ADAM_STEP)
        v_hat = nv / (1.0 - ADAM_B2 ** ADAM_STEP)
        d_ref[...] = -ADAM_LR * (m_hat / (jnp.sqrt(v_hat) + ADAM_EPS) + ADAM_WD * w_ref[...])

    blk = pl.BlockSpec((tr, cols), lambda i: (i, 0))
    return pl.pallas_call(
        body, grid=(rows // tr,), in_specs=[blk] * 4, out_specs=[blk] * 3,
        out_shape=[SDS((rows, cols), F32)] * 3,
        compiler_params=_cparams(("parallel",)), name=name)(w, g, m, v)


def _adamw_halves(w, mine, theirs, m, v, c_arr, name):
    rows, cols = w.shape
    rh = rows // 2
    tr = rh if rh <= 352 else 256
    nh = rh // tr

    def body(c_ref, w_ref, mine_ref, theirs_ref, m_ref, v_ref, g_ref, d_ref, nm_ref, nv_ref):
        own = jnp.full((tr, cols), pl.program_id(0), jnp.int32) == c_ref[0]
        gg = jnp.where(own, mine_ref[...], theirs_ref[...])
        g_ref[...] = gg
        nm = ADAM_B1 * m_ref[...] + (1.0 - ADAM_B1) * gg
        nv = ADAM_B2 * v_ref[...] + (1.0 - ADAM_B2) * (gg * gg)
        nm_ref[...] = nm
        nv_ref[...] = nv
        m_hat = nm / (1.0 - ADAM_B1 ** ADAM_STEP)
        v_hat = nv / (1.0 - ADAM_B2 ** ADAM_STEP)
        d_ref[...] = -ADAM_LR * (m_hat / (jnp.sqrt(v_hat) + ADAM_EPS) + ADAM_WD * w_ref[...])

    whole = pl.BlockSpec((tr, cols), lambda hh, i, c_ref: (hh * nh + i, 0))
    part = pl.BlockSpec((tr, cols), lambda hh, i, c_ref: (i, 0))
    grid_spec = pltpu.PrefetchScalarGridSpec(
        num_scalar_prefetch=1, grid=(2, nh), in_specs=[whole, part, part, whole, whole], out_specs=[whole] * 4)
    return pl.pallas_call(body, grid_spec=grid_spec, out_shape=[SDS((rows, cols), F32)] * 4,
                          compiler_params=_cparams(("parallel", "parallel")), name=name)(c_arr, w, mine, theirs, m, v)


def _pack_small(pre_mix, post_mix, pre_ffn, post_ffn, rel_bias, b_forget, sinks):
    def at(row, v):
        return jnp.pad(v, ((row, 7 - row), (0, D_MODEL - v.shape[1])))
    return (at(0, pre_mix) + at(1, post_mix) + at(2, pre_ffn) + at(3, post_ffn)
            + at(4, rel_bias.reshape(1, N_BUCKETS * N_HEADS)) + at(5, jnp.concatenate([b_forget, sinks], axis=1)))


def _unpack_small(p):
    return dict(ln_pre_mix=p[0:1], ln_post_mix=p[1:2], ln_pre_ffn=p[2:3], ln_post_ffn=p[3:4],
                rel_bias=p[4, :N_BUCKETS * N_HEADS].reshape(N_BUCKETS, N_HEADS),
                b_forget=p[5:6, 0:N_HEADS], sinks=p[5:6, N_HEADS:2 * N_HEADS])


WEIGHTS = ("meta_tokens", "rel_bias", "ln_pre_mix", "ln_post_mix", "ln_pre_ffn", "ln_post_ffn",
           "w_in", "b_forget", "sinks", "w_out", "w_gate_up", "w_down")


def kernel(x, meta_tokens, rel_bias, ln_pre_mix, ln_post_mix, ln_pre_ffn, ln_post_ffn, w_in, b_forget, sinks, w_out, w_gate_up, w_down, loss_target, m_meta_tokens, m_rel_bias, m_ln_pre_mix, m_ln_post_mix, m_ln_pre_ffn, m_ln_post_ffn, m_w_in, m_b_forget, m_sinks, m_w_out, m_w_gate_up, m_w_down, v_meta_tokens, v_rel_bias, v_ln_pre_mix, v_ln_post_mix, v_ln_pre_ffn, v_ln_post_ffn, v_w_in, v_b_forget, v_sinks, v_w_out, v_w_gate_up, v_w_down):
    xi, yi, ci = lax.axis_index("x"), lax.axis_index("y"), lax.axis_index("c")
    chip = 2 * xi + yi
    c_arr = jnp.reshape(ci, (1,)).astype(jnp.int32)

    def halves(w, dtype):
        return w.astype(dtype).reshape(2, w.shape[0] // 2, w.shape[1])

    def with_own(gathered, shards):
        return [lax.dynamic_update_slice(got, own[None], (chip, 0, 0, 0)) for got, own in zip(gathered, shards)]

    shards = [halves(jnp.transpose(w_in[0]), BF16), halves(meta_tokens, F32)]
    gw_in, g_meta = with_own(_run_alone(_gather_rider(shards, False), "gather_mixer_weights"), shards)
    out_shards = [halves(w_out[0], BF16)]
    ffn_shards = [halves(w_gate_up[0], BF16), halves(w_down[0], BF16)]

    def ffn_weights(carried):
        gw_gu, gw_dn = carried
        return gw_gu.reshape(D_MODEL, 2 * D_FF), gw_dn.reshape(D_FF, D_MODEL)

    early = {}

    def early_grads(dw_gu, dw_dn, dw_out):
        early["grads"] = [dw_out.reshape(4, 2, 128, D_MODEL), dw_gu.reshape(4, 2, 512, FF_T),
                          dw_dn.reshape(4, 2, 352, D_MODEL)]
        return _direct_rider(early["grads"])
    w_in_b = jnp.pad(gw_in.reshape(D_PROJ, D_MODEL), ((0, D_PROJ_P - D_PROJ), (0, 0)))
    meta_all = g_meta.reshape(4, N_META, D_MODEL // 4).transpose(1, 0, 2).reshape(N_META, D_MODEL)

    loc = _local_step(x[0], loss_target[0], meta_all, rel_bias, ln_pre_mix, ln_post_mix, ln_pre_ffn, ln_post_ffn,
                      b_forget, sinks, w_in_b, _gather_rider(out_shards, True),
                      lambda carried: carried[0].reshape(D_MODEL, D_MODEL),
                      _gather_rider(ffn_shards, True, by_columns=(0,)), ffn_weights, early_grads)

    small = jnp.concatenate(
        [_pack_small(loc["ln_pre_mix"], loc["ln_post_mix"], loc["ln_pre_ffn"], loc["ln_post_ffn"],
                     loc["rel_bias"], loc["b_forget"], loc["sinks"])
         + jnp.pad(loc["loss"].reshape(1, 1), ((LOSS_ROW, 7 - LOSS_ROW), (0, D_MODEL - 1))), loc["meta"]], axis=0)

    dw_in = loc["w_in"].reshape(1, 2, D_MODEL // 2, D_PROJ_P)
    (got_in,) = _run_alone(_swap_rider([dw_in]), "swap_halves_late")
    half_sum = _pair_sum(dw_in, got_in, c_arr, "pair_sum_late")
    part_in = half_sum[0, :, :D_PROJ].reshape(D_MODEL // 2, 4, D_PROJ // 4).transpose(1, 0, 2)
    send_sems, recv_sems, part_sent, landing, small_sent, everyone, token = _late_exchange_start(part_in, small)
    chip_arr = jnp.reshape(chip, (1,)).astype(jnp.int32)
    own_arr = jnp.stack([chip, ci]).astype(jnp.int32)
    grad, delta, new_m, new_v = {}, {}, {}, {}

    def update(names, mine):
        theirs = _join_halves(mine, "join_" + names[0])
        big = dict(w_in=(w_in, m_w_in, v_w_in), w_out=(w_out, m_w_out, v_w_out),
                   w_gate_up=(w_gate_up, m_w_gate_up, v_w_gate_up), w_down=(w_down, m_w_down, v_w_down))
        for name, g_mine, g_theirs in zip(names, mine, theirs):
            w, m, v = big[name]
            g, d, nm, nv = _adamw_halves(w[0], g_mine, g_theirs, m[0], v[0], c_arr, "adamw_" + name)
            grad[name], delta[name], new_m[name], new_v[name] = g[None], d[None], nm[None], nv[None]

    update(("w_out", "w_gate_up", "w_down"),
           [_owner_sum(g, l, own_arr, token, "owner_sum_%d" % a)
            for a, (g, l) in enumerate(zip(early["grads"], loc["landed"]))])
    part_back, landed_in, small_all = _late_exchange_wait(send_sems, recv_sems, part_sent, landing, small_sent,
                                                         everyone, new_v["w_down"])
    mine_in = _chip_sum(part_back, landed_in, chip_arr, "chip_sum_in")
    (theirs_in,) = _join_halves([mine_in], "join_w_in")
    g_w_in = jnp.where(ci == 0, jnp.concatenate([mine_in, theirs_in], axis=0),
                       jnp.concatenate([theirs_in, mine_in], axis=0))
    view = lambda a: jnp.transpose(a).reshape(D_PROJ // 4 * 8, BLK)
    back = lambda a: jnp.transpose(a.reshape(D_PROJ // 4, D_MODEL))[None]
    d, nm, nv = _adamw(view(w_in[0]), view(g_w_in), view(m_w_in[0]), view(v_w_in[0]), "adamw_w_in",
                       tr=D_PROJ // 4 * 4)
    grad["w_in"], delta["w_in"], new_m["w_in"], new_v["w_in"] = g_w_in[None], back(d), back(nm), back(nv)
    me = 4 * xi + 2 * yi + ci
    small_sum = _device_sum(lax.dynamic_update_slice(small_all, small[None], (me, 0, 0)))
    g_meta_tokens = lax.dynamic_slice(small_sum[8:N_SMALL], (0, chip * (D_MODEL // 4)), (N_META, D_MODEL // 4))
    g_small = small_sum[0:8]
    grad.update(_unpack_small(g_small))
    grad.update(meta_tokens=g_meta_tokens)
    delta["meta_tokens"], new_m["meta_tokens"], new_v["meta_tokens"] = _adamw(
        meta_tokens, g_meta_tokens, m_meta_tokens, v_meta_tokens, "adamw_meta")
    d, nm, nv = _adamw(
        _pack_small(ln_pre_mix, ln_post_mix, ln_pre_ffn, ln_post_ffn, rel_bias, b_forget, sinks), g_small,
        _pack_small(m_ln_pre_mix, m_ln_post_mix, m_ln_pre_ffn, m_ln_post_ffn, m_rel_bias, m_b_forget, m_sinks),
        _pack_small(v_ln_pre_mix, v_ln_post_mix, v_ln_pre_ffn, v_ln_post_ffn, v_rel_bias, v_b_forget, v_sinks),
        "adamw_small")
    delta.update(_unpack_small(d))
    new_m.update(_unpack_small(nm))
    new_v.update(_unpack_small(nv))

    loss = small_sum[LOSS_ROW, 0]
    return (loss,loc["grad_x"][None], *[grad[k] for k in WEIGHTS], *[delta[k] for k in WEIGHTS],
            *[new_m[k] for k in WEIGHTS], *[new_v[k] for k in WEIGHTS])
```

```python
import math

import numpy as np
import jax
import jax.numpy as jnp
from jax import lax
from jax.experimental import pallas as pl
from jax.experimental.pallas import tpu as pltpu

F32 = jnp.float32
BF16 = jnp.bfloat16
MESH = pl.DeviceIdType.MESH
SDS = jax.ShapeDtypeStruct

D_MODEL = 1024
SEQ = 4096
N_META = 16
N_HEADS = 8
HALF = 64
D_FF = 2816
N_BUCKETS = 32
EPS = 1e-6
NEG = -1e30
SCALE = 0.125
LOG2E = 1.4426950408889634
LN2 = 0.6931471805599453
PAD_ROWS = 112
ROW0 = PAD_ROWS + N_META
LP = ROW0 + SEQ
BLK = 128
NBLK = LP // BLK
TM = 384
NT = LP // TM
TM_PURE = LP // 2
TM_MID = LP // 4
TM_EPI = LP // 6
TN = 256
TK_W = LP // 2
D_PROJ = 2312
D_PROJ_P = 2432
D_QKV = 2304
FF_T = 1408
VMEM_LIMIT = 56 * 1024 * 1024

ADAM_LR = 0.001
ADAM_B1 = 0.9
ADAM_B2 = 0.999
ADAM_EPS = 1e-08
ADAM_WD = 0.01
ADAM_STEP = 10

QA = 0
KA, VA = 4, 5
QB, KB, VB = 3, 5, 7
W2 = 256

NT_DIMS = (((1,), (1,)), ((), ()))
TN_DIMS = (((0,), (0,)), ((), ()))


def _cparams(sem):
    return pltpu.CompilerParams(dimension_semantics=sem, vmem_limit_bytes=VMEM_LIMIT)


def _t5_bucket_np(d):
    n = np.maximum(d, 0).astype(np.int32)
    nf = np.maximum(n, 1).astype(np.float32)
    large = 16 + (np.log(nf / np.float32(16)) / np.float32(math.log(8.0)) * np.float32(16)).astype(np.int32)
    large = np.minimum(large, N_BUCKETS - 1)
    return np.where(n < 16, n, large).astype(np.int32)


def _bucket_tables():
    qi = np.arange(BLK)[:, None]
    ki = np.arange(BLK)[None, :]
    return np.stack([_t5_bucket_np(qi - ki), _t5_bucket_np(qi - ki + BLK)])


def _rms(x):
    return lax.rsqrt(jnp.mean(x * x, axis=-1, keepdims=True) + EPS)


def _rms_bwd(n, r, gdy):
    return r * (gdy - n * jnp.mean(n * gdy, axis=-1, keepdims=True))


def _pre_mix(x, meta, gain, w_in_b):
    half = D_QKV // 2
    n_steps = LP // TM_MID

    def body(x_ref, meta_ref, g_ref, w_ref, h0_ref, hn_ref, proj_ref, f_ref, tile, sems):
        i = pl.program_id(0)

        def tile_copy(s):
            skip = ROW0 if s == 0 else 0
            return pltpu.make_async_copy(x_ref.at[s * TM_MID + skip - ROW0:(s + 1) * TM_MID - ROW0],
                                         tile.at[s % 2, skip:TM_MID], sems.at[s % 2])

        @pl.when(i == 0)
        def _():
            tile_copy(0).start()
            tile[0, :PAD_ROWS] = jnp.zeros((PAD_ROWS, D_MODEL), F32)
            tile[0, PAD_ROWS:ROW0] = meta_ref[...]

        for s in range(n_steps):
            @pl.when(i == s)
            def _():
                if s + 1 < n_steps:
                    tile_copy(s + 1).start()
                tile_copy(s).wait()

        x = tile[i % 2]
        h0_ref[...] = x
        hn = (x * _rms(x) * g_ref[...]).astype(BF16)
        hn_ref[...] = hn
        proj_ref[:, :half] = lax.dot_general(hn, w_ref[:half, :], NT_DIMS, preferred_element_type=F32).astype(BF16)
        p = lax.dot_general(hn, w_ref[half:, :], NT_DIMS, preferred_element_type=F32)
        proj_ref[:, half:] = p[:, :half].astype(BF16)
        f_ref[...] = p[:, half:]

    return pl.pallas_call(
        body, grid=(n_steps,),
        in_specs=[HBM_SPEC,
                  pl.BlockSpec((N_META, D_MODEL), lambda i: (0, 0)),
                  pl.BlockSpec((1, D_MODEL), lambda i: (0, 0)),
                  pl.BlockSpec((D_PROJ_P, D_MODEL), lambda i: (0, 0))],
        out_specs=[pl.BlockSpec((TM_MID, D_MODEL), lambda i: (i, 0)),
                   pl.BlockSpec((TM_MID, D_MODEL), lambda i: (i, 0)),
                   pl.BlockSpec((TM_MID, D_QKV), lambda i: (i, 0)),
                   pl.BlockSpec((TM_MID, BLK), lambda i: (i, 0))],
        out_shape=[SDS((LP, D_MODEL), F32), SDS((LP, D_MODEL), BF16), SDS((LP, D_QKV), BF16), SDS((LP, BLK), F32)],
        scratch_shapes=[pltpu.VMEM((2, TM_MID, D_MODEL), F32), pltpu.SemaphoreType.DMA((2,))],
        compiler_params=_cparams(("arbitrary",)), name="pre_mix")(x, meta, gain, w_in_b)


def _attn_out(o_a, o_b, w_out_b, h0, g_post, g_pre_ffn):
    def body(oa_ref, ob_ref, w_ref, h0_ref, gp_ref, gf_ref, a_ref, h1_ref, hn2_ref):
        a = (jnp.dot(oa_ref[...], w_ref[0:512, :], preferred_element_type=F32)
             + jnp.dot(ob_ref[...], w_ref[512:1024, :], preferred_element_type=F32))
        a_ref[...] = a
        h1 = h0_ref[...] + a * _rms(a) * gp_ref[...]
        h1_ref[...] = h1
        hn2_ref[...] = (h1 * _rms(h1) * gf_ref[...]).astype(BF16)

    row = lambda w: pl.BlockSpec((TM_EPI, w), lambda i: (i, 0))
    vec = pl.BlockSpec((1, D_MODEL), lambda i: (0, 0))
    return pl.pallas_call(
        body, grid=(LP // TM_EPI,),
        in_specs=[row(512), row(512), pl.BlockSpec((D_MODEL, D_MODEL), lambda i: (0, 0)), row(D_MODEL), vec, vec],
        out_specs=[row(D_MODEL), row(D_MODEL), row(D_MODEL)],
        out_shape=[SDS((LP, D_MODEL), F32), SDS((LP, D_MODEL), F32), SDS((LP, D_MODEL), BF16)],
        compiler_params=_cparams(("parallel",)), name="attn_out")(o_a, o_b, w_out_b, h0, g_post, g_pre_ffn)


def _ffn_up(hn2, w_gu_b):
    def body(x_ref, wg_ref, wu_ref, g_ref, u_ref, act_ref):
        x = x_ref[...]
        g = jnp.dot(x, wg_ref[...], preferred_element_type=F32)
        u = jnp.dot(x, wu_ref[...], preferred_element_type=F32)
        g_ref[...] = g.astype(BF16)
        u_ref[...] = u.astype(BF16)
        act_ref[...] = (g * (1.0 / (1.0 + jnp.exp(-g))) * u).astype(BF16)

    out = pl.BlockSpec((LP, TN), lambda j: (0, j))
    return pl.pallas_call(
        body, grid=(D_FF // TN,),
        in_specs=[pl.BlockSpec((LP, D_MODEL), lambda j: (0, 0)),
                  pl.BlockSpec((D_MODEL, TN), lambda j: (0, j)),
                  pl.BlockSpec((D_MODEL, TN), lambda j: (0, j + D_FF // TN))],
        out_specs=[out, out, out],
        out_shape=[SDS((LP, D_FF), BF16)] * 3,
        compiler_params=_cparams(("parallel",)), name="ffn_up")(hn2, w_gu_b, w_gu_b)


def _ffn_down_loss(act, w_dn_b, h1, tgt, g_post_ffn):
    def body(act_ref, w_ref, h1_ref, t0_ref, t1_ref, t2_ref, g_ref, dff_ref, dy_ref, loss_ref, dg_ref):
        i = pl.program_id(0)
        target = jnp.concatenate([t0_ref[...], t1_ref[...], t2_ref[...]], axis=0)

        @pl.when(i == 0)
        def _():
            loss_ref[...] = jnp.zeros_like(loss_ref)
            dg_ref[...] = jnp.zeros_like(dg_ref)

        ff = jnp.dot(act_ref[...], w_ref[...], preferred_element_type=F32)
        r = _rms(ff)
        n = ff * r
        g = g_ref[...]
        y = h1_ref[...] + n * g
        rows = i * TM + lax.broadcasted_iota(jnp.int32, (TM, D_MODEL), 0)
        diff = jnp.where(rows >= ROW0, y - target, 0.0)
        loss_ref[...] += 0.5 * jnp.sum(diff * diff) / D_MODEL
        dy = diff / D_MODEL
        dy_ref[...] = dy
        dg_ref[...] += jnp.sum(dy * n, axis=0, keepdims=True)
        dff_ref[...] = _rms_bwd(n, r, g * dy).astype(BF16)

    row = pl.BlockSpec((TM, D_MODEL), lambda i: (i, 0))
    tblk = lambda j: pl.BlockSpec((BLK, D_MODEL), lambda i: (jnp.maximum(3 * i - 1 + j, 0), 0))
    return pl.pallas_call(
        body, grid=(NT,),
        in_specs=[pl.BlockSpec((TM, D_FF), lambda i: (i, 0)), pl.BlockSpec((D_FF, D_MODEL), lambda i: (0, 0)),
                  row, tblk(0), tblk(1), tblk(2), pl.BlockSpec((1, D_MODEL), lambda i: (0, 0))],
        out_specs=[row, row, pl.BlockSpec((8, BLK), lambda i: (0, 0)), pl.BlockSpec((1, D_MODEL), lambda i: (0, 0))],
        out_shape=[SDS((LP, D_MODEL), BF16), SDS((LP, D_MODEL), F32), SDS((8, BLK), F32), SDS((1, D_MODEL), F32)],
        compiler_params=_cparams(("arbitrary",)), name="ffn_down_loss")(act, w_dn_b, h1, tgt, tgt, tgt, g_post_ffn)


def _ffn_down_bwd(dff, w_dn_b, g, u):
    def body(d_ref, w_ref, g_ref, u_ref, dg_ref, du_ref):
        dact = lax.dot_general(d_ref[...], w_ref[...], NT_DIMS, preferred_element_type=F32)
        gg = g_ref[...].astype(F32)
        sig = 1.0 / (1.0 + jnp.exp(-gg))
        dg_ref[...] = (dact * u_ref[...].astype(F32) * sig * (1.0 + gg * (1.0 - sig))).astype(BF16)
        du_ref[...] = (dact * gg * sig).astype(BF16)

    blk = pl.BlockSpec((LP, TN), lambda j: (0, j))
    return pl.pallas_call(
        body, grid=(D_FF // TN,),
        in_specs=[pl.BlockSpec((LP, D_MODEL), lambda j: (0, 0)),
                  pl.BlockSpec((TN, D_MODEL), lambda j: (j, 0)), blk, blk],
        out_specs=[blk, blk],
        out_shape=[SDS((LP, D_FF), BF16)] * 2,
        compiler_params=_cparams(("parallel",)), name="ffn_down_bwd")(dff, w_dn_b, g, u)


def _ffn_up_bwd(dg, du, w_gu_b, h1, a, dy, g_pre_ffn, g_post_mix):
    def body(dg_ref, du_ref, w_ref, h1_ref, a_ref, dy_ref, gf_ref, gp_ref,
             dh1_ref, da_ref, dgf_ref, dgp_ref, acc):
        i = pl.program_id(0)
        s = pl.program_id(1)

        @pl.when((i == 0) & (s == 0))
        def _():
            dgf_ref[...] = jnp.zeros_like(dgf_ref)
            dgp_ref[...] = jnp.zeros_like(dgp_ref)

        @pl.when(s == 0)
        def _():
            acc[...] = jnp.zeros_like(acc)

        @pl.when(s < 2)
        def _():
            acc[...] += lax.dot_general(dg_ref[...], w_ref[...], NT_DIMS, preferred_element_type=F32)

        @pl.when(s >= 2)
        def _():
            acc[...] += lax.dot_general(du_ref[...], w_ref[...], NT_DIMS, preferred_element_type=F32)

        @pl.when(s == 3)
        def _():
            dhn2 = acc[...]
            h1 = h1_ref[...]
            r2 = _rms(h1)
            n2 = h1 * r2
            dgf_ref[...] += jnp.sum(dhn2 * n2, axis=0, keepdims=True)
            dh1 = dy_ref[...] + _rms_bwd(n2, r2, gf_ref[...] * dhn2)
            dh1_ref[...] = dh1
            av = a_ref[...]
            ra = _rms(av)
            na = av * ra
            dgp_ref[...] += jnp.sum(dh1 * na, axis=0, keepdims=True)
            da_ref[...] = _rms_bwd(na, ra, gp_ref[...] * dh1).astype(BF16)

    row = pl.BlockSpec((TM_EPI, D_MODEL), lambda i, s: (i, 0))
    vec = pl.BlockSpec((1, D_MODEL), lambda i, s: (0, 0))
    return pl.pallas_call(
        body, grid=(LP // TM_EPI, 4),
        in_specs=[pl.BlockSpec((TM_EPI, FF_T), lambda i, s: (i, jnp.minimum(s, 1))),
                  pl.BlockSpec((TM_EPI, FF_T), lambda i, s: (i, jnp.maximum(s - 2, 0))),
                  pl.BlockSpec((D_MODEL, FF_T), lambda i, s: (0, s)),
                  row, row, row, vec, vec],
        out_specs=[row, row, vec, vec],
        out_shape=[SDS((LP, D_MODEL), F32), SDS((LP, D_MODEL), BF16), SDS((1, D_MODEL), F32), SDS((1, D_MODEL), F32)],
        scratch_shapes=[pltpu.VMEM((TM_EPI, D_MODEL), F32)],
        compiler_params=_cparams(("arbitrary", "arbitrary")), name="ffn_up_bwd",
    )(dg, du, w_gu_b, h1, a, dy, g_pre_ffn, g_post_mix)


def _attn_out_bwd(da, w_out_b):
    def body(d_ref, w_ref, o_ref):
        o_ref[...] = lax.dot_general(d_ref[...], w_ref[...], NT_DIMS, preferred_element_type=F32).astype(BF16)

    row = pl.BlockSpec((TM_PURE, D_MODEL), lambda i: (i, 0))
    return pl.pallas_call(
        body, grid=(LP // TM_PURE,),
        in_specs=[row, pl.BlockSpec((D_MODEL, D_MODEL), lambda i: (0, 0))],
        out_specs=row, out_shape=SDS((LP, D_MODEL), BF16),
        compiler_params=_cparams(("parallel",)), name="attn_out_bwd")(da, w_out_b)


def _pre_mix_bwd(dq_a, dq_b, dk_b, dv_b, dk_a, dv_a, df, w_in_b, h0, dh1, g_pre_mix):
    n_steps = LP // TM_EPI

    def body(qa_ref, qb_ref, kb_ref, vb_ref, ka_ref, va_ref, f_ref, w_ref, h0_ref, dh1_ref, g_ref,
             dproj_ref, dx_ref, dmeta_ref, dg_ref, tile, sems):
        i = pl.program_id(0)

        def tile_copy(s):
            skip = ROW0 if s == 0 else 0
            return pltpu.make_async_copy(tile.at[s % 2, skip:TM_EPI],
                                         dx_ref.at[s * TM_EPI + skip - ROW0:(s + 1) * TM_EPI - ROW0],
                                         sems.at[s % 2])

        @pl.when(i == 0)
        def _():
            dg_ref[...] = jnp.zeros_like(dg_ref)

        for s in range(2, n_steps):
            @pl.when(i == s)
            def _():
                tile_copy(s - 2).wait()

        dproj = jnp.concatenate(
            [qa_ref[...], ka_ref[...].astype(BF16), va_ref[...].astype(BF16), (qb_ref[...] * SCALE).astype(BF16),
             kb_ref[...], vb_ref[...], f_ref[...].astype(BF16)], axis=1)
        dproj_ref[...] = dproj
        dhn = jnp.dot(dproj, w_ref[...], preferred_element_type=F32)
        x = h0_ref[...]
        r = _rms(x)
        n = x * r
        dg_ref[...] += jnp.sum(dhn * n, axis=0, keepdims=True)
        tile[i % 2] = dh1_ref[...] + _rms_bwd(n, r, g_ref[...] * dhn)

        for s in range(n_steps):
            @pl.when(i == s)
            def _():
                tile_copy(s).start()
                if s == 0:
                    dmeta_ref[...] = tile[0, PAD_ROWS:ROW0]
                if s == n_steps - 1:
                    tile_copy(s - 1).wait()
                    tile_copy(s).wait()

    row = lambda w: pl.BlockSpec((TM_EPI, w), lambda i: (i, 0))
    vec = pl.BlockSpec((1, D_MODEL), lambda i: (0, 0))
    return pl.pallas_call(
        body, grid=(n_steps,),
        in_specs=[row(512), row(512), row(512), row(512), row(BLK), row(BLK), row(BLK),
                  pl.BlockSpec((D_PROJ_P, D_MODEL), lambda i: (0, 0)), row(D_MODEL), row(D_MODEL), vec],
        out_specs=[row(D_PROJ_P), HBM_SPEC, pl.BlockSpec((N_META, D_MODEL), lambda i: (0, 0)), vec],
        out_shape=[SDS((LP, D_PROJ_P), BF16), SDS((SEQ, D_MODEL), F32), SDS((N_META, D_MODEL), F32),
                   SDS((1, D_MODEL), F32)],
        scratch_shapes=[pltpu.VMEM((2, TM_EPI, D_MODEL), F32), pltpu.SemaphoreType.DMA((2,))],
        compiler_params=_cparams(("arbitrary",)), name="pre_mix_bwd",
    )(dq_a, dq_b, dk_b, dv_b, dk_a, dv_a, df, w_in_b, h0, dh1, g_pre_mix)


def _mm_tn(parts, b, tm, name, out_dtype=F32):
    widths = [p.shape[1] for p in parts]
    m_total = sum(widths)
    n = b.shape[1]
    whole = len(parts) > 1
    n_k = LP // TK_W
    assert (tm == m_total) if whole else (m_total % tm == 0)

    def body(*refs):
        a_refs, b_ref, o_ref, acc = refs[:-3], refs[-3], refs[-2], refs[-1]
        k = pl.program_id(1)

        @pl.when(k == 0)
        def _():
            acc[...] = jnp.zeros_like(acc)
        a = a_refs[0][...] if not whole else jnp.concatenate([r[...] for r in a_refs], axis=1)
        acc[...] += lax.dot_general(a, b_ref[...], TN_DIMS, preferred_element_type=F32)

        @pl.when(k == n_k - 1)
        def _():
            o_ref[...] = acc[...].astype(out_dtype)

    a_specs = ([pl.BlockSpec((TK_W, w), lambda mi, k: (k, 0)) for w in widths] if whole
               else [pl.BlockSpec((TK_W, tm), lambda mi, k: (k, mi))])
    return pl.pallas_call(
        body, grid=(m_total // tm, n_k),
        in_specs=a_specs + [pl.BlockSpec((TK_W, n), lambda mi, k: (k, 0))],
        out_specs=pl.BlockSpec((tm, n), lambda mi, k: (mi, 0)),
        out_shape=SDS((m_total, n), out_dtype),
        scratch_shapes=[pltpu.VMEM((tm, n), F32)],
        compiler_params=_cparams(("parallel", "arbitrary")), name=name)(*parts, b)


def _dw_gate_up(hn2, dg, du):
    n_k = LP // TK_W

    def body(a_ref, dg_ref, du_ref, o_ref, acc):
        s = pl.program_id(0)
        k = pl.program_id(1)

        @pl.when(k == 0)
        def _():
            acc[...] = jnp.zeros_like(acc)

        @pl.when(s < 2)
        def _():
            acc[...] += lax.dot_general(a_ref[...], dg_ref[...], TN_DIMS, preferred_element_type=F32)

        @pl.when(s >= 2)
        def _():
            acc[...] += lax.dot_general(a_ref[...], du_ref[...], TN_DIMS, preferred_element_type=F32)

        @pl.when(k == n_k - 1)
        def _():
            o_ref[0] = acc[...].astype(BF16)

    return pl.pallas_call(
        body, grid=(4, n_k),
        in_specs=[pl.BlockSpec((TK_W, D_MODEL), lambda s, k: (k, 0)),
                  pl.BlockSpec((TK_W, FF_T), lambda s, k: (k, jnp.minimum(s, 1))),
                  pl.BlockSpec((TK_W, FF_T), lambda s, k: (k, jnp.maximum(s - 2, 0)))],
        out_specs=pl.BlockSpec((1, D_MODEL, FF_T), lambda s, k: (s, 0, 0)),
        out_shape=SDS((4, D_MODEL, FF_T), BF16),
        scratch_shapes=[pltpu.VMEM((D_MODEL, FF_T), F32)],
        compiler_params=_cparams(("parallel", "arbitrary")), name="dw_gate_up")(hn2, dg, du)


def _split3(x):
    hi = x.astype(BF16)
    r1 = x - hi.astype(F32)
    mid = r1.astype(BF16)
    lo = (r1 - mid.astype(F32)).astype(BF16)
    return hi, mid, lo


def _tri_matmul(tri, x):
    hi, mid, lo = _split3(x)
    dot = lambda t: jnp.dot(tri, t, preferred_element_type=F32)
    return dot(hi) + dot(mid) + dot(lo)


def _forget_cumsum(f, b_forget_p):
    def body(f_ref, b_ref, cum_ref, carry):
        i = pl.program_id(0)

        @pl.when(i == 0)
        def _():
            carry[...] = jnp.zeros_like(carry)

        z = f_ref[...] + b_ref[...]
        ls = jnp.minimum(z, 0.0) - jnp.log(1.0 + jnp.exp(-jnp.abs(z)))
        rows = i * TM + lax.broadcasted_iota(jnp.int32, (TM, BLK), 0)
        ls = jnp.where(rows >= PAD_ROWS, ls, 0.0)
        r = lax.broadcasted_iota(jnp.int32, (TM, TM), 0)
        c = lax.broadcasted_iota(jnp.int32, (TM, TM), 1)
        tri = (c <= r).astype(BF16)
        cum = _tri_matmul(tri, ls) + carry[...]
        cum_ref[...] = cum
        carry[...] = cum[TM - 1:TM, :]

    return pl.pallas_call(
        body, grid=(NT,),
        in_specs=[pl.BlockSpec((TM, BLK), lambda i: (i, 0)), pl.BlockSpec((1, BLK), lambda i: (0, 0))],
        out_specs=pl.BlockSpec((TM, BLK), lambda i: (i, 0)),
        out_shape=SDS((LP, BLK), F32),
        scratch_shapes=[pltpu.VMEM((1, BLK), F32)],
        compiler_params=_cparams(("arbitrary",)), name="forget_cumsum")(f, b_forget_p)


def _forget_cumsum_bwd(dcum, f, b_forget_p):
    def body(d_ref, f_ref, b_ref, df_ref, db_ref, carry):
        i = pl.program_id(0)

        @pl.when(i == 0)
        def _():
            carry[...] = jnp.zeros_like(carry)
            db_ref[...] = jnp.zeros_like(db_ref)

        blk = NT - 1 - i
        r = lax.broadcasted_iota(jnp.int32, (TM, TM), 0)
        c = lax.broadcasted_iota(jnp.int32, (TM, TM), 1)
        tri = (c >= r).astype(BF16)
        d = d_ref[...]
        dls = _tri_matmul(tri, d) + carry[...]
        carry[...] = dls[0:1, :]
        z = f_ref[...] + b_ref[...]
        rows = blk * TM + lax.broadcasted_iota(jnp.int32, (TM, BLK), 0)
        df = jnp.where(rows >= PAD_ROWS, dls / (1.0 + jnp.exp(z)), 0.0)
        df_ref[...] = df
        db_ref[...] += jnp.sum(df, axis=0, keepdims=True)

    rev = pl.BlockSpec((TM, BLK), lambda i: (NT - 1 - i, 0))
    vec = pl.BlockSpec((1, BLK), lambda i: (0, 0))
    return pl.pallas_call(
        body, grid=(NT,),
        in_specs=[rev, rev, vec],
        out_specs=[rev, vec],
        out_shape=[SDS((LP, BLK), F32), SDS((1, BLK), F32)],
        scratch_shapes=[pltpu.VMEM((1, BLK), F32)],
        compiler_params=_cparams(("arbitrary",)), name="forget_cumsum_bwd")(dcum, f, b_forget_p)


def _lane_half(rows):
    return lax.broadcasted_iota(jnp.int32, (rows, BLK), 1) // HALF


def _fox_valid(qi, kj):
    qrow = qi * TM + lax.broadcasted_iota(jnp.int32, (TM, TM), 0)
    krow = kj * TM + lax.broadcasted_iota(jnp.int32, (TM, TM), 1)
    return (krow <= qrow) & ((krow >= PAD_ROWS) | (qrow < PAD_ROWS))


class _Rider:
    def __init__(self, operands, out_shapes, sem_counts, first, middle, last):
        self.operands, self.out_shapes, self.sem_counts = list(operands), list(out_shapes), list(sem_counts)
        self.first, self.middle, self.last = first, middle, last

    def scratch(self):
        return [pltpu.SemaphoreType.DMA((k,)) for k in self.sem_counts]

    def split(self, refs, n_in, n_out, n_scratch):
        a, b = len(self.operands), len(self.out_shapes)
        ins, mine_in = refs[:n_in], refs[n_in:n_in + a]
        outs, mine_out = refs[n_in + a:n_in + a + n_out], refs[n_in + a + n_out:n_in + a + n_out + b]
        rest = refs[n_in + a + n_out + b:]
        return ins, outs, rest[:n_scratch], (mine_in, mine_out, rest[n_scratch:])

    def at_steps(self, mine, is_first, is_middle, is_last):
        for cond, fn in ((is_first, self.first), (is_middle, self.middle), (is_last, self.last)):
            pl.when(cond)(lambda fn=fn: fn(*mine))


HBM_SPEC = pl.BlockSpec(memory_space=pltpu.HBM)


N_AUG = 3
QCHUNKS = ((0, 128), (128, 128), (256, 128))
KSUB = 384
AHEAD = 5
AHEAD_BWD = 1


def _fox_prep(proj, cum):
    def body(q0_ref, q1_ref, k0_ref, k1_ref, v0_ref, v1_ref, c_ref, qa_ref, ka_ref, vt_ref):
        half = _lane_half(TM)
        lane = lax.broadcasted_iota(jnp.int32, (TM, BLK), 1)
        for pp in range(4):
            cols = slice(pp * BLK, (pp + 1) * BLK)
            q_ref, k_ref, v_ref = ((q0_ref, k0_ref, v0_ref), (q1_ref, k1_ref, v1_ref))[pp // 2]
            part = slice((pp % 2) * BLK, (pp % 2 + 1) * BLK)
            qs = q_ref[:, part].astype(F32) * (SCALE * LOG2E)
            kp = k_ref[:, part].astype(F32)
            vp = v_ref[:, part]
            vt_ref[cols, :] = vp.astype(F32).T.astype(BF16)
            for e in range(2):
                h = 2 * pp + e
                a = (1 - e) * HALF
                blk = slice(h * BLK, (h + 1) * BLK)
                hi, mid, lo = _split3(-LOG2E * c_ref[:, h:h + 1])
                q_aug = jnp.where(half == e, qs, jnp.where((lane >= a) & (lane < a + N_AUG), 1.0, 0.0))
                k_aug = jnp.where(half == e, kp, jnp.where(
                    lane == a, hi.astype(F32), jnp.where(lane == a + 1, mid.astype(F32), jnp.where(
                        lane == a + 2, lo.astype(F32), 0.0))))
                qa_ref[blk, :] = q_aug.T.astype(BF16)
                ka_ref[:, blk] = k_aug.astype(BF16)

    row = lambda blk: pl.BlockSpec((TM, W2), lambda i: (i, blk))
    wide = pl.BlockSpec((TM, 1024), lambda i: (i, 0))
    return pl.pallas_call(
        body, grid=(NT,),
        in_specs=[row(QB), row(QB + 1), row(KB), row(KB + 1), row(VB), row(VB + 1),
                  pl.BlockSpec((TM, BLK), lambda i: (i, 0))],
        out_specs=[pl.BlockSpec((1024, TM), lambda i: (0, i)), wide, pl.BlockSpec((512, TM), lambda i: (0, i))],
        out_shape=[SDS((1024, LP), BF16), SDS((LP, 1024), BF16), SDS((512, LP), BF16)],
        compiler_params=_cparams(("parallel",)), name="fox_prep")(proj, proj, proj, proj, proj, proj, cum)


def _over_keys(reduce, x):
    slabs = x.reshape(x.shape[0] // HALF, HALF, x.shape[1])
    return reduce(reduce(slabs, axis=0), axis=0, keepdims=True)


def _fox_valid_t(qi, kj, c, r):
    krow = kj * TM + r * KSUB + lax.broadcasted_iota(jnp.int32, (KSUB, c[1]), 0)
    qrow = qi * TM + c[0] + lax.broadcasted_iota(jnp.int32, (KSUB, c[1]), 1)
    return (krow <= qrow) & ((krow >= PAD_ROWS) | (qrow < PAD_ROWS))


def _fox_fwd(q_aug, k_aug, v_t, rider):
    pairs = [(qi, kj) for qi in range(NT) for kj in range(qi + 1)]
    n_pairs = len(pairs)

    def body(qi_ref, kj_ref, *refs):
        (q_ref, k_ref, vt_ref), (o_ref, lse_ref), (m_s, l_s, acc_s), mine = rider.split(refs, 3, 2, 3)
        n = pl.program_id(0)
        qi = qi_ref[n]
        kj = kj_ref[n]
        rider.at_steps(mine, n == 0, n == n_pairs // 2, n == n_pairs - 1)

        @pl.when(kj == 0)
        def _():
            m_s[...] = jnp.full_like(m_s, NEG)
            l_s[...] = jnp.zeros_like(l_s)
            acc_s[...] = jnp.zeros_like(acc_s)

        def tile(masked):
            steps = [(h, c, r) for h in range(N_HEADS) for c in QCHUNKS for r in range(TM // KSUB)]

            def scores(h, c, r):
                blk = slice(h * BLK, (h + 1) * BLK)
                return jnp.dot(k_ref[r * KSUB:(r + 1) * KSUB, blk], q_ref[blk, c[0]:c[0] + c[1]],
                               preferred_element_type=F32)

            ahead = [scores(*st) for st in steps[:AHEAD]]
            for n, (h, c, r) in enumerate(steps):
                s_t = ahead.pop(0)
                if n + AHEAD < len(steps):
                    ahead.append(scores(*steps[n + AHEAD]))
                cs = slice(c[0], c[0] + c[1])
                if masked:
                    s_t = jnp.where(_fox_valid_t(qi, kj, c, r), s_t, NEG)
                m_prev = m_s[h, :, cs]
                m_new = jnp.maximum(m_prev, _over_keys(jnp.max, s_t))
                p_t = jnp.exp2(s_t - m_new)
                alpha = jnp.exp2(m_prev - m_new)
                l_s[h, :, cs] = alpha * l_s[h, :, cs] + _over_keys(jnp.sum, p_t)
                m_s[h, :, cs] = m_new
                vt = vt_ref[h * HALF:(h + 1) * HALF, r * KSUB:(r + 1) * KSUB]
                acc_s[h, :, cs] = acc_s[h, :, cs] * alpha + jnp.dot(vt, p_t.astype(BF16),
                                                                    preferred_element_type=F32)

        @pl.when((kj < qi) & (kj > 0))
        def _():
            tile(False)

        @pl.when((kj == qi) | (kj == 0))
        def _():
            tile(True)

        @pl.when(kj == qi)
        def _():
            for pp in range(4):
                both = jnp.concatenate([acc_s[2 * pp] * (1.0 / l_s[2 * pp]),
                                        acc_s[2 * pp + 1] * (1.0 / l_s[2 * pp + 1])], axis=0)
                o_ref[:, pp * BLK:(pp + 1) * BLK] = both.T.astype(BF16)
            rows = [m_s[h] * LN2 + jnp.log(l_s[h]) for h in range(N_HEADS)]
            lse_ref[...] = jnp.concatenate(rows + [jnp.zeros((BLK - N_HEADS, TM), F32)], axis=0).T

    grid_spec = pltpu.PrefetchScalarGridSpec(
        num_scalar_prefetch=2, grid=(n_pairs,),
        in_specs=[pl.BlockSpec((1024, TM), lambda n, qi, kj: (0, qi[n])),
                  pl.BlockSpec((TM, 1024), lambda n, qi, kj: (kj[n], 0)),
                  pl.BlockSpec((512, TM), lambda n, qi, kj: (0, kj[n]))] + [HBM_SPEC] * len(rider.operands),
        out_specs=[pl.BlockSpec((TM, 512), lambda n, qi, kj: (qi[n], 0)),
                   pl.BlockSpec((TM, BLK), lambda n, qi, kj: (qi[n], 0))]
        + [HBM_SPEC] * len(rider.out_shapes),
        scratch_shapes=[pltpu.VMEM((N_HEADS, 1, TM), F32), pltpu.VMEM((N_HEADS, 1, TM), F32),
                        pltpu.VMEM((N_HEADS, HALF, TM), F32)] + rider.scratch())
    o_b, lse, *carried = pl.pallas_call(
        body, grid_spec=grid_spec,
        out_shape=[SDS((LP, 512), BF16), SDS((LP, BLK), F32)] + rider.out_shapes,
        compiler_params=_cparams(("arbitrary",)), name="fox_fwd",
    )(jnp.asarray([p[0] for p in pairs], jnp.int32), jnp.asarray([p[1] for p in pairs], jnp.int32),
      q_aug, k_aug, v_t, *rider.operands)
    return o_b, lse, carried


def _fox_bwd(proj, o_b, dmix, lse, ck_t, rider):
    pairs = [(kj, qi) for kj in range(NT) for qi in range(kj, NT)]
    n_pairs = len(pairs)

    def body(kj_ref, qi_ref, *refs):
        ((q0_ref, q1_ref, k0_ref, k1_ref, v0_ref, v1_ref, o_ref, do_ref, lse_ref, ck_ref),
         (dq_ref, dk_ref, dv_ref, dck_ref, dcq_ref), (dk_s, dv_s, dck_s), mine) = rider.split(refs, 10, 5, 3)
        n = pl.program_id(0)
        kj = kj_ref[n]
        qi = qi_ref[n]
        rider.at_steps(mine, n == 0, n == n_pairs // 2, n == n_pairs - 1)

        @pl.when(n == 0)
        def _():
            dq_ref[...] = jnp.zeros_like(dq_ref)
            dcq_ref[...] = jnp.zeros_like(dcq_ref)

        @pl.when(qi == kj)
        def _():
            dk_s[...] = jnp.zeros_like(dk_s)
            dv_s[...] = jnp.zeros_like(dv_s)
            dck_s[...] = jnp.zeros_like(dck_s)

        def tile(masked):
            valid = _fox_valid(qi, kj) if masked else None
            half = _lane_half(TM)
            q0 = pl.multiple_of(qi * TM, TM)
            lane = lax.broadcasted_iota(jnp.int32, (TM, BLK), 1)
            row_sums = jnp.zeros((TM, BLK), F32)
            pair_ops = {}

            def operands(pp):
                if pp not in pair_ops:
                    cols = slice(pp * BLK, (pp + 1) * BLK)
                    q_ref, k_ref, v_ref = ((q0_ref, k0_ref, v0_ref), (q1_ref, k1_ref, v1_ref))[pp // 2]
                    part = slice((pp % 2) * BLK, (pp % 2 + 1) * BLK)
                    pair_ops[pp] = ((q_ref[:, part].astype(F32) * SCALE).astype(BF16), k_ref[:, part],
                                    v_ref[:, part], do_ref[:, cols])
                return pair_ops[pp]

            def scores(pp, e):
                qs, kp, vp, dop = operands(pp)
                ke = jnp.where(half == e, kp, jnp.zeros_like(kp))
                ve = jnp.where(half == e, vp, jnp.zeros_like(vp))
                return (lax.dot_general(qs, ke, NT_DIMS, preferred_element_type=F32),
                        lax.dot_general(dop, ve, NT_DIMS, preferred_element_type=F32), ke)

            steps = [(pp, e) for pp in range(4) for e in range(2)]
            ahead = [scores(*st) for st in steps[:AHEAD_BWD]]
            for n, (pp, e) in enumerate(steps):
                raw, dp, ke = ahead.pop(0)
                if n + AHEAD_BWD < len(steps):
                    ahead.append(scores(*steps[n + AHEAD_BWD]))
                h = 2 * pp + e
                cols = slice(pp * BLK, (pp + 1) * BLK)
                qs, kp, vp, dop = operands(pp)
                if e == 0:
                    prod = dop.astype(F32) * o_ref[:, cols].astype(F32)
                    d0 = jnp.sum(jnp.where(half == 0, prod, 0.0), axis=1, keepdims=True)
                    d1 = jnp.sum(prod, axis=1, keepdims=True) - d0
                    dq = jnp.zeros((TM, BLK), F32)
                    dks, dvs = [], []
                t = raw - ck_ref[h] - lse_ref[:, h:h + 1]
                if masked:
                    t = jnp.where(valid, t, NEG)
                p = jnp.exp(t)
                ds = p * (dp - (d0 if e == 0 else d1))
                dck_s[h] += jnp.sum(ds, axis=0, keepdims=True)
                row_sums = jnp.where(lane == h, jnp.sum(ds, axis=1, keepdims=True), row_sums)
                ds_b = ds.astype(BF16)
                dq = dq + jnp.dot(ds_b, ke, preferred_element_type=F32)
                dks.append(lax.dot_general(ds_b, qs, TN_DIMS, preferred_element_type=F32))
                dvs.append(lax.dot_general(p.astype(BF16), dop, TN_DIMS, preferred_element_type=F32))
                if e == 1:
                    dq_ref[pl.ds(q0, TM), cols] += dq
                    dk_s[pp] += jnp.where(half == 0, dks[0], dks[1])
                    dv_s[pp] += jnp.where(half == 0, dvs[0], dvs[1])
            dcq_ref[pl.ds(q0, TM), :] += row_sums

        @pl.when((qi > kj) & (kj > 0))
        def _():
            tile(False)

        @pl.when((qi == kj) | (kj == 0))
        def _():
            tile(True)

        @pl.when(qi == NT - 1)
        def _():
            for pp in range(4):
                cols = slice(pp * BLK, (pp + 1) * BLK)
                dk_ref[:, cols] = dk_s[pp].astype(BF16)
                dv_ref[:, cols] = dv_s[pp].astype(BF16)
            dck_ref[...] = dck_s[...]

    qrow = lambda blk, w=512: pl.BlockSpec((TM, w), lambda n, kj, qi: (qi[n], blk))
    krow = lambda blk: pl.BlockSpec((TM, W2), lambda n, kj, qi: (kj[n], blk))
    grid_spec = pltpu.PrefetchScalarGridSpec(
        num_scalar_prefetch=2, grid=(n_pairs,),
        in_specs=[qrow(QB, W2), qrow(QB + 1, W2), krow(KB), krow(KB + 1), krow(VB), krow(VB + 1), qrow(0), qrow(1),
                  pl.BlockSpec((TM, BLK), lambda n, kj, qi: (qi[n], 0)),
                  pl.BlockSpec((N_HEADS, 1, TM), lambda n, kj, qi: (0, 0, kj[n]))] + [HBM_SPEC] * len(rider.operands),
        out_specs=[pl.BlockSpec((LP, 512), lambda n, kj, qi: (0, 0)),
                   pl.BlockSpec((TM, 512), lambda n, kj, qi: (kj[n], 0)),
                   pl.BlockSpec((TM, 512), lambda n, kj, qi: (kj[n], 0)),
                   pl.BlockSpec((N_HEADS, 1, TM), lambda n, kj, qi: (0, 0, kj[n])),
                   pl.BlockSpec((LP, BLK), lambda n, kj, qi: (0, 0))] + [HBM_SPEC] * len(rider.out_shapes),
        scratch_shapes=[pltpu.VMEM((4, TM, BLK), F32), pltpu.VMEM((4, TM, BLK), F32),
                        pltpu.VMEM((N_HEADS, 1, TM), F32)] + rider.scratch())
    dq, dk, dv, dck, dcq, *carried = pl.pallas_call(
        body, grid_spec=grid_spec,
        out_shape=[SDS((LP, 512), F32), SDS((LP, 512), BF16), SDS((LP, 512), BF16), SDS((N_HEADS, 1, LP), F32),
                   SDS((LP, BLK), F32)] + rider.out_shapes,
        compiler_params=_cparams(("arbitrary",)), name="fox_bwd",
    )(jnp.asarray([p[0] for p in pairs], jnp.int32), jnp.asarray([p[1] for p in pairs], jnp.int32),
      proj, proj, proj, proj, proj, proj, o_b, dmix, lse, ck_t, *rider.operands)
    return dq, dk, dv, dck, dcq, carried


N_SEG = 3
N_KEY = N_SEG * BLK
GROUP = 4
QW = GROUP * BLK


def _bucket_tables_t():
    return np.ascontiguousarray(_bucket_tables().transpose(0, 2, 1))


def _stack_heads(ref, g, scale):
    half = _lane_half(BLK)
    out = []
    for pair in range(2):
        x = ref[:, (2 * g + pair) * BLK:(2 * g + pair + 1) * BLK].astype(F32) * scale
        swapped = pltpu.roll(x, HALF, 1)
        for e in range(2):
            out.append(jnp.where(half == g, x if e == g else swapped, 0.0).astype(BF16))
    return jnp.concatenate(out, axis=0)


def _unstack_heads(x_t, g, ref, scale):
    for pair in range(2):
        both = jnp.concatenate([x_t[:, (2 * pair) * BLK:(2 * pair + 1) * BLK],
                                x_t[:, (2 * pair + 1) * BLK:(2 * pair + 2) * BLK]], axis=0)
        ref[:, (2 * g + pair) * BLK:(2 * g + pair + 1) * BLK] = (both.T * scale).astype(ref.dtype)


def _swa_tables(tab_ref, sink_ref, bkt_ref, tbl, sink_row):
    kk = lax.broadcasted_iota(jnp.int32, (BLK, BLK), 0)
    qq = lax.broadcasted_iota(jnp.int32, (BLK, BLK), 1)
    neg = jnp.full((BLK, BLK), NEG, F32)
    lane = lax.broadcasted_iota(jnp.int32, (1, QW), 1) // BLK
    for g in range(2):
        row = jnp.zeros((1, QW), F32)
        for hh in range(GROUP):
            h = GROUP * g + hh
            cols = slice(hh * BLK, (hh + 1) * BLK)
            row = jnp.where(lane == hh, sink_ref[0, h], row)

            def step(b, carry, h=h):
                t = tab_ref[b, h]
                return jnp.where(bkt_ref[0] == b, t, carry[0]), jnp.where(bkt_ref[1] == b, t, carry[1])
            zero = jnp.zeros((BLK, BLK), F32)
            cur, prev = lax.fori_loop(0, N_BUCKETS, step, (zero, zero))
            far = jnp.full((BLK, BLK), tab_ref[N_BUCKETS - 1, h], F32)
            causal = jnp.where(kk <= qq, cur, neg)
            segments = [
                (neg, neg, jnp.where(kk >= PAD_ROWS, causal, neg)),
                (jnp.where(kk >= PAD_ROWS, prev, neg), neg, causal),
                (jnp.where(kk >= PAD_ROWS, far, neg), jnp.where(kk > qq, prev, neg), causal)]
            for case in range(3):
                for seg in range(N_SEG):
                    tbl[case, g, seg * BLK:(seg + 1) * BLK, cols] = segments[case][seg]
        sink_row[g] = row


def _swa_prep(proj):
    rows = LP // 3

    def body(k_ref, v_ref, kt_ref, vt_ref):
        kt_ref[...] = k_ref[...].astype(F32).T.astype(BF16)
        vt_ref[...] = v_ref[...].astype(F32).T.astype(BF16)

    col = pl.BlockSpec((BLK, rows), lambda i: (0, i))
    return pl.pallas_call(
        body, grid=(3,),
        in_specs=[pl.BlockSpec((rows, BLK), lambda i: (i, KA)), pl.BlockSpec((rows, BLK), lambda i: (i, VA))],
        out_specs=[col, col], out_shape=[SDS((BLK, LP), BF16)] * 2,
        compiler_params=_cparams(("parallel",)), name="swa_prep")(proj, proj)


def _segments(ref, i, by_rows):
    starts = [0, pl.multiple_of(jnp.maximum(i - 1, 0) * BLK, BLK), pl.multiple_of(i * BLK, BLK)]
    if by_rows:
        return jnp.concatenate([ref[pl.ds(s, BLK), :] for s in starts], axis=0)
    return jnp.concatenate([ref[:, pl.ds(s, BLK)] for s in starts], axis=1)


def _swa_fwd(proj, vt_a, rel_bias, sinks, bkt_t, rider):
    def body(*refs):
        ((tab_ref, sink_ref, bkt_ref, q_ref, k_ref, vt_ref), (o_ref, lse_ref),
         (tbl, sink_row), mine) = rider.split(refs, 6, 2, 2)
        i = pl.program_id(0)
        rider.at_steps(mine, i == 0, i == NBLK // 2, i == NBLK - 1)

        @pl.when(i == 0)
        def _():
            _swa_tables(tab_ref, sink_ref, bkt_ref, tbl, sink_row)

        case = jnp.minimum(i, 2)
        k_cat = _segments(k_ref, i, True)
        vt_cat = _segments(vt_ref, i, False)
        raw = [lax.dot_general(k_cat, _stack_heads(q_ref, g, SCALE), NT_DIMS, preferred_element_type=F32)
               for g in range(2)]
        for g in range(2):
            s_t = raw[g] + tbl[case, g]
            sink = sink_row[g]
            m = jnp.maximum(_over_keys(jnp.max, s_t), sink)
            p_t = jnp.exp(s_t - m)
            l = _over_keys(jnp.sum, p_t) + jnp.exp(sink - m)
            o_t = jnp.dot(vt_cat[g * HALF:(g + 1) * HALF, :], p_t.astype(BF16), preferred_element_type=F32)
            _unstack_heads(o_t * (1.0 / l), g, o_ref, 1.0)
            lse = m + jnp.log(l)
            for hh in range(GROUP):
                lse_ref[GROUP * g + hh] = lse[:, hh * BLK:(hh + 1) * BLK]

    smem = pl.BlockSpec(memory_space=pltpu.SMEM)
    o_a, lse, *carried = pl.pallas_call(
        body, grid=(NBLK,),
        in_specs=[smem, smem, pl.BlockSpec((2, BLK, BLK), lambda i: (0, 0, 0)),
                  pl.BlockSpec((BLK, 512), lambda i: (i, QA)), pl.BlockSpec((LP, BLK), lambda i: (0, KA)),
                  pl.BlockSpec((BLK, LP), lambda i: (0, 0))] + [HBM_SPEC] * len(rider.operands),
        out_specs=[pl.BlockSpec((BLK, 512), lambda i: (i, 0)),
                   pl.BlockSpec((N_HEADS, 1, BLK), lambda i: (0, 0, i))] + [HBM_SPEC] * len(rider.out_shapes),
        out_shape=[SDS((LP, 512), BF16), SDS((N_HEADS, 1, LP), F32)] + rider.out_shapes,
        scratch_shapes=[pltpu.VMEM((3, 2, N_KEY, QW), F32), pltpu.VMEM((2, 1, QW), F32)] + rider.scratch(),
        compiler_params=_cparams(("arbitrary",)), name="swa_fwd",
    )(rel_bias, sinks, bkt_t, proj, proj, vt_a, *rider.operands)
    return o_a, lse, carried


def _swa_bwd(proj, kt_a, o_a, dmix, lse, rel_bias, sinks, bkt_t):
    def body(tab_ref, sink_ref, bkt_ref, q_ref, k_ref, v_ref, kt_ref, o_ref, do_ref, lse_ref,
             dq_ref, dk_ref, dv_ref, dbias_ref, dsink_ref, tbl, sink_row, acc, dsk):
        i = pl.program_id(0)

        @pl.when(i == 0)
        def _():
            _swa_tables(tab_ref, sink_ref, bkt_ref, tbl, sink_row)
            dk_ref[...] = jnp.zeros_like(dk_ref)
            dv_ref[...] = jnp.zeros_like(dv_ref)
            acc[...] = jnp.zeros_like(acc)
            dsk[...] = jnp.zeros_like(dsk)

        case = jnp.minimum(i, 2)
        first = jnp.full((BLK, QW), i, jnp.int32) == 1
        k_cat = _segments(k_ref, i, True)
        v_cat = _segments(v_ref, i, True)
        kt_cat = _segments(kt_ref, i, False)
        dk_cat = jnp.zeros((N_KEY, BLK), F32)
        dv_cat = jnp.zeros((N_KEY, BLK), F32)
        for g in range(2):
            d_parts = []
            for pair in range(2):
                cols = slice((2 * g + pair) * BLK, (2 * g + pair + 1) * BLK)
                prod_t = (do_ref[:, cols].astype(F32) * o_ref[:, cols].astype(F32)).T
                d_parts += [jnp.sum(prod_t[:HALF], axis=0, keepdims=True),
                            jnp.sum(prod_t[HALF:], axis=0, keepdims=True)]
            d_row = jnp.concatenate(d_parts, axis=1)
            lse_row = jnp.concatenate([lse_ref[GROUP * g + hh] for hh in range(GROUP)], axis=1)
            q_st = _stack_heads(q_ref, g, SCALE)
            do_st = _stack_heads(do_ref, g, 1.0)
            s_t = lax.dot_general(k_cat, q_st, NT_DIMS, preferred_element_type=F32) + tbl[case, g]
            p_t = jnp.exp(s_t - lse_row)
            dp_t = lax.dot_general(v_cat, do_st, NT_DIMS, preferred_element_type=F32)
            ds_t = p_t * (dp_t - d_row)
            dsk[g] += -jnp.exp(sink_row[g] - lse_row) * d_row
            acc[g, 0:BLK] += jnp.where(first, 0.0, ds_t[0:BLK])
            acc[g, BLK:2 * BLK] += jnp.where(first, ds_t[0:BLK], ds_t[BLK:2 * BLK])
            acc[g, 2 * BLK:N_KEY] += ds_t[2 * BLK:N_KEY]
            ds_b = ds_t.astype(BF16)
            dk_cat = dk_cat + jnp.dot(ds_b, q_st, preferred_element_type=F32)
            dv_cat = dv_cat + jnp.dot(p_t.astype(BF16), do_st, preferred_element_type=F32)
            dq_t = jnp.dot(kt_cat[g * HALF:(g + 1) * HALF, :], ds_b, preferred_element_type=F32)
            _unstack_heads(dq_t, g, dq_ref, SCALE)

        prev0 = pl.multiple_of(jnp.maximum(i - 1, 0) * BLK, BLK)
        cur0 = pl.multiple_of(i * BLK, BLK)
        for ref, cat in ((dk_ref, dk_cat), (dv_ref, dv_cat)):
            ref[0:BLK, :] += cat[0:BLK]
            ref[pl.ds(prev0, BLK), :] += cat[BLK:2 * BLK]
            ref[pl.ds(cur0, BLK), :] += cat[2 * BLK:N_KEY]

        @pl.when(i == NBLK - 1)
        def _():
            lane = lax.broadcasted_iota(jnp.int32, (1, BLK), 1)

            def per_bucket(b, carry):
                row = jnp.zeros((1, BLK), F32)
                for h in range(N_HEADS):
                    g, cols = h // GROUP, slice((h % GROUP) * BLK, (h % GROUP + 1) * BLK)
                    val = (jnp.sum(jnp.where(bkt_ref[0] == b, acc[g, 2 * BLK:N_KEY, cols], 0.0), keepdims=True)
                           + jnp.sum(jnp.where(bkt_ref[1] == b, acc[g, BLK:2 * BLK, cols], 0.0), keepdims=True))
                    row = jnp.where(lane == h, val, row)
                dbias_ref[pl.ds(b, 1), :] = row
                return carry

            lax.fori_loop(0, N_BUCKETS, per_bucket, 0)
            far = jnp.zeros((1, BLK), F32)
            dsr = jnp.zeros((1, BLK), F32)
            for h in range(N_HEADS):
                g, cols = h // GROUP, slice((h % GROUP) * BLK, (h % GROUP + 1) * BLK)
                far = jnp.where(lane == h, jnp.sum(acc[g, 0:BLK, cols], keepdims=True), far)
                dsr = jnp.where(lane == h, jnp.sum(dsk[g, :, cols], keepdims=True), dsr)
            dbias_ref[N_BUCKETS - 1:N_BUCKETS, :] += far
            dsink_ref[...] = dsr

    smem = pl.BlockSpec(memory_space=pltpu.SMEM)
    blk512 = lambda col: pl.BlockSpec((BLK, 512), lambda i: (i, col))
    full = lambda r, c: pl.BlockSpec((r, c), lambda i: (0, 0))
    return pl.pallas_call(
        body, grid=(NBLK,),
        in_specs=[smem, smem, pl.BlockSpec((2, BLK, BLK), lambda i: (0, 0, 0)), blk512(QA),
                  pl.BlockSpec((LP, BLK), lambda i: (0, KA)), pl.BlockSpec((LP, BLK), lambda i: (0, VA)),
                  full(BLK, LP), blk512(0), blk512(0), pl.BlockSpec((N_HEADS, 1, BLK), lambda i: (0, 0, i))],
        out_specs=[blk512(0), full(LP, BLK), full(LP, BLK), full(N_BUCKETS, BLK), full(1, BLK)],
        out_shape=[SDS((LP, 512), BF16), SDS((LP, BLK), F32), SDS((LP, BLK), F32),
                   SDS((N_BUCKETS, BLK), F32), SDS((1, BLK), F32)],
        scratch_shapes=[pltpu.VMEM((3, 2, N_KEY, QW), F32), pltpu.VMEM((2, 1, QW), F32),
                        pltpu.VMEM((2, N_KEY, QW), F32), pltpu.VMEM((2, 1, QW), F32)],
        compiler_params=_cparams(("arbitrary",)), name="swa_bwd",
    )(rel_bias, sinks, bkt_t, proj, proj, proj, kt_a, o_a, dmix, lse)


def _local_step(x, tgt, meta, rel_bias, g_pre_mix, g_post_mix, g_pre_ffn, g_post_ffn, b_forget, sinks,
                w_in_b, out_rider, out_weight, ffn_rider, ffn_weights, early_grads):
    bkt_t = jnp.asarray(_bucket_tables_t())
    b_p = jnp.pad(b_forget, ((0, 0), (0, BLK - N_HEADS)))

    h0, hn1, proj, f = _pre_mix(x, meta, g_pre_mix, w_in_b)
    kt_a, vt_a = _swa_prep(proj)
    o_a, lse_a, carried_out = _swa_fwd(proj, vt_a, rel_bias, sinks, bkt_t, out_rider)
    w_out_b = out_weight(carried_out)
    cum = _forget_cumsum(f, b_p)
    ck_t = cum[:, :N_HEADS].T.reshape(N_HEADS, 1, LP)
    q_aug, k_aug, v_t = _fox_prep(proj, cum)
    o_b, lse_b, carried = _fox_fwd(q_aug, k_aug, v_t, ffn_rider)
    w_gu_b, w_dn_b = ffn_weights(carried)
    a, h1, hn2 = _attn_out(o_a, o_b, w_out_b, h0, g_post_mix, g_pre_ffn)
    g, u, act = _ffn_up(hn2, w_gu_b)
    dff, dy, loss_blk, dg_post_ffn = _ffn_down_loss(act, w_dn_b, h1, tgt, g_post_ffn)

    dw_dn = _mm_tn([act], dff, FF_T, "dw_down", BF16)
    dg, du = _ffn_down_bwd(dff, w_dn_b, g, u)
    dw_gu = _dw_gate_up(hn2, dg, du)
    dh1, da, dg_pre_ffn, dg_post_mix = _ffn_up_bwd(dg, du, w_gu_b, h1, a, dy, g_pre_ffn, g_post_mix)
    dw_out = _mm_tn([o_a, o_b], da, D_MODEL, "dw_out", BF16)
    dmix = _attn_out_bwd(da, w_out_b)
    dq_b, dk_b, dv_b, dck, dcq, landed = _fox_bwd(proj, o_b, dmix, lse_b, ck_t, early_grads(dw_gu, dw_dn, dw_out))
    dq_a, dk_a, dv_a, dbias, dsink = _swa_bwd(proj, kt_a, o_a, dmix, lse_a, rel_bias, sinks, bkt_t)
    dcum = dcq - jnp.pad(dck.reshape(N_HEADS, LP).T, ((0, 0), (0, BLK - N_HEADS)))
    df, db = _forget_cumsum_bwd(dcum, f, b_p)
    dproj, dx, dmeta, dg_pre_mix = _pre_mix_bwd(dq_a, dq_b, dk_b, dv_b, dk_a, dv_a, df, w_in_b, h0, dh1, g_pre_mix)
    dw_in = _mm_tn([hn1], dproj, D_MODEL, "dw_in", BF16)

    return dict(loss=loss_blk[0, 0], grad_x=dx, meta=dmeta,
                rel_bias=dbias[:, :N_HEADS], ln_pre_mix=dg_pre_mix, ln_post_mix=dg_post_mix,
                ln_pre_ffn=dg_pre_ffn, ln_post_ffn=dg_post_ffn, b_forget=db[:, :N_HEADS],
                sinks=dsink[:, :N_HEADS], w_in=dw_in, w_out=dw_out, w_gate_up=dw_gu, w_down=dw_dn,
                landed=landed)


N_SMALL = 24
LOSS_ROW = 6


def _place():
    x, y, c = lax.axis_index("x"), lax.axis_index("y"), lax.axis_index("c")
    return x, y, c, [(1 - x, y), (x, 1 - y), (1 - x, 1 - y)]


def _run_alone(rider, name):
    a, b = len(rider.operands), len(rider.out_shapes)

    def body(*refs):
        mine = (refs[:a], refs[a:a + b], refs[a + b:])
        rider.first(*mine)
        rider.middle(*mine)
        rider.last(*mine)

    return pl.pallas_call(body, in_specs=[HBM_SPEC] * a, out_specs=[HBM_SPEC] * b, out_shape=rider.out_shapes,
                          scratch_shapes=rider.scratch(), name=name)(*rider.operands)


def _gather_rider(shards, own_too, by_columns=()):
    n = len(shards)

    def slot(a, outs, chip, h):
        if a in by_columns:
            cols = shards[a].shape[2]
            return outs[a].at[h, :, pl.ds(pl.multiple_of(chip * cols, BLK), cols)]
        return outs[a].at[chip, h]

    def own_copies(ins, outs, sems):
        x, y, _, _ = _place()
        if not own_too:
            return []
        return [pltpu.make_async_copy(ins[a].at[h], slot(a, outs, 2 * x + y, h), sems[2].at[2 * a + h])
                for a in range(n) for h in range(2)]

    def copies(ins, outs, sems):
        send_sems, recv_sems = sems[:2]
        x, y, c, others = _place()
        chip = 2 * x + y
        sibling = (x, y, 1 - c)

        def rc(a, k, src, dst, to):
            return pltpu.make_async_remote_copy(src_ref=src, dst_ref=dst, send_sem=send_sems.at[6 * a + k],
                                                recv_sem=recv_sems.at[6 * a + k], device_id=to, device_id_type=MESH)

        pairs = [(a, j, ox, oy) for a in range(n) for j, (ox, oy) in enumerate(others)]
        there = lambda a, ox, oy, h: slot(a, outs, 2 * ox + oy, h)
        return dict(
            sent=lambda: [rc(a, j, ins[a].at[c], slot(a, outs, chip, c), (ox, oy, c)) for a, j, ox, oy in pairs],
            landed=lambda: [rc(a, j, there(a, ox, oy, c), there(a, ox, oy, c), sibling) for a, j, ox, oy in pairs],
            passed=lambda: [rc(a, 3 + j, there(a, ox, oy, c), there(a, ox, oy, c), sibling)
                            for a, j, ox, oy in pairs],
            arriving=lambda: [rc(a, 3 + j, there(a, ox, oy, 1 - c), there(a, ox, oy, 1 - c), sibling)
                              for a, j, ox, oy in pairs])

    def first(*mine):
        for cp in copies(*mine)["sent"]() + own_copies(*mine):
            cp.start()

    def middle(*mine):
        kinds = copies(*mine)
        for got, cp in zip(kinds["landed"](), kinds["passed"]()):
            got.wait_recv()
            cp.start()

    def last(*mine):
        kinds = copies(*mine)
        for cp in kinds["arriving"]():
            cp.wait_recv()
        for cp in kinds["sent"]() + kinds["passed"]():
            cp.wait_send()
        for cp in own_copies(*mine):
            cp.wait()

    shapes = [SDS((2, s.shape[1], 4 * s.shape[2]) if a in by_columns else (4,) + s.shape, s.dtype)
              for a, s in enumerate(shards)]
    return _Rider(shards, shapes, [6 * n, 6 * n] + [2 * n] * own_too, first, middle, last)


def _swap_rider(grads):
    n = len(grads)
    slabs = [(a, s) for a in range(n) for s in range(grads[a].shape[0])]

    def copies(ins, outs, sems):
        x, y, c, _ = _place()
        return [pltpu.make_async_remote_copy(
            src_ref=ins[a].at[s, 1 - c], dst_ref=outs[a].at[s], send_sem=sems[0].at[k], recv_sem=sems[1].at[k],
            device_id=(x, y, 1 - c), device_id_type=MESH) for k, (a, s) in enumerate(slabs)]

    def first(*mine):
        for cp in copies(*mine):
            cp.start()

    def middle(*mine):
        pass

    def last(*mine):
        for cp in copies(*mine):
            cp.wait()

    return _Rider(grads, [SDS(g.shape[:1] + g.shape[2:], g.dtype) for g in grads], [len(slabs), len(slabs)],
                  first, middle, last)


def _pair_sum(g, got, c_arr, name):
    n_s, rh, cc = got.shape

    def body(c_ref, g_ref, p_ref, o_ref):
        o_ref[0] = (g_ref[0, 0].astype(F32) + p_ref[0].astype(F32)).astype(BF16)

    grid_spec = pltpu.PrefetchScalarGridSpec(
        num_scalar_prefetch=1, grid=(n_s,),
        in_specs=[pl.BlockSpec((1, 1, rh, cc), lambda s, c_ref: (s, c_ref[0], 0, 0)),
                  pl.BlockSpec((1, rh, cc), lambda s, c_ref: (s, 0, 0))],
        out_specs=pl.BlockSpec((1, rh, cc), lambda s, c_ref: (s, 0, 0)))
    return pl.pallas_call(body, grid_spec=grid_spec, out_shape=SDS((n_s, rh, cc), BF16),
                          compiler_params=_cparams(("parallel",)), name=name)(c_arr, g, got)


def _direct_rider(grads):
    n = len(grads)

    def copies(ins, outs, sems):
        x, y, c, others = _place()
        peers = [(x, y, 1 - c)] + [(ox, oy, c) for ox, oy in others] + [(ox, oy, 1 - c) for ox, oy in others]
        return [pltpu.make_async_remote_copy(
            src_ref=ins[a].at[2 * px + py, pc], dst_ref=outs[a].at[k], send_sem=sems[0].at[7 * a + k],
            recv_sem=sems[1].at[7 * a + k], device_id=(px, py, pc), device_id_type=MESH)
            for a in range(n) for k, (px, py, pc) in enumerate(peers)]

    def first(*mine):
        for cp in copies(*mine):
            cp.start()

    def middle(*mine):
        pass

    def last(*mine):
        for cp in copies(*mine):
            cp.wait()

    return _Rider(grads, [SDS((7,) + g.shape[2:], g.dtype) for g in grads], [7 * n, 7 * n], first, middle, last)


def _owner_sum(grads, landed, own_arr, after, name):
    rh, cc = landed.shape[1:]
    tr = rh // 2

    def body(own_ref, g_ref, p_ref, after_ref, o_ref):
        total = g_ref[0, 0].astype(F32)
        for k in range(7):
            total = total + p_ref[k].astype(F32)
        o_ref[...] = total

    grid_spec = pltpu.PrefetchScalarGridSpec(
        num_scalar_prefetch=1, grid=(2,),
        in_specs=[pl.BlockSpec((1, 1, tr, cc), lambda i, own: (own[0], own[1], i, 0)),
                  pl.BlockSpec((7, tr, cc), lambda i, own: (0, i, 0)), pl.BlockSpec(memory_space=pl.ANY)],
        out_specs=pl.BlockSpec((tr, cc), lambda i, own: (i, 0)))
    return pl.pallas_call(body, grid_spec=grid_spec, out_shape=SDS((rh, cc), F32),
                          compiler_params=_cparams(("parallel",)), name=name)(own_arr, grads, landed, after)


SEM_SPEC = pl.BlockSpec(memory_space=pltpu.SEMAPHORE)
N_LATE = 10


def _late_copies(part_ref, landed_ref, small_ref, all_ref, send_sems, recv_sems):
    x, y, c, others = _place()
    me = 4 * x + 2 * y + c
    peers = [(x, y, 1 - c)] + [(ox, oy, c) for ox, oy in others] + [(ox, oy, 1 - c) for ox, oy in others]
    big = [pltpu.make_async_remote_copy(
        src_ref=part_ref.at[2 * ox + oy], dst_ref=landed_ref.at[j], send_sem=send_sems.at[j], recv_sem=recv_sems.at[j],
        device_id=(ox, oy, c), device_id_type=MESH) for j, (ox, oy) in enumerate(others)]
    small = [pltpu.make_async_remote_copy(
        src_ref=small_ref, dst_ref=all_ref.at[me], send_sem=send_sems.at[3 + k], recv_sem=recv_sems.at[3 + k],
        device_id=peer, device_id_type=MESH) for k, peer in enumerate(peers)]
    return big + small


def _late_exchange_start(part, small):
    def body(part_ref, landed_ref, small_ref, all_ref, send_sems, recv_sems, part_o, landed_o, small_o, all_o, token):
        for cp in _late_copies(part_ref, landed_ref, small_ref, all_ref, send_sems, recv_sems):
            cp.start()
        token[...] = jnp.zeros_like(token)

    hbm = lambda a: pltpu.HBM(a.shape, a.dtype)
    landed = lax.empty((3,) + part.shape[1:], part.dtype)
    everyone = lax.empty((8,) + small.shape, small.dtype)
    operands = [pltpu.with_memory_space_constraint(a, pltpu.HBM) for a in (part, landed, small, everyone)]
    return pl.pallas_call(
        body, name="late_exchange_start",
        out_shape=(pltpu.SemaphoreType.DMA((N_LATE,)), pltpu.SemaphoreType.DMA((N_LATE,)),
                   hbm(part), hbm(landed), hbm(small), hbm(everyone), SDS((8, BLK), F32)),
        in_specs=[HBM_SPEC] * 4,
        out_specs=(SEM_SPEC, SEM_SPEC, HBM_SPEC, HBM_SPEC, HBM_SPEC, HBM_SPEC, pl.BlockSpec(memory_space=pltpu.VMEM)),
        input_output_aliases={0: 2, 1: 3, 2: 4, 3: 5},
        compiler_params=pltpu.CompilerParams(has_side_effects=pltpu.SideEffectType.DATAFLOW_SIDE_EFFECTING),
    )(*operands)


def _late_exchange_wait(send_sems, recv_sems, part, landed, small, everyone, after):
    def body(part_ref, landed_ref, small_ref, all_ref, send_sems, recv_sems, after_ref, part_o, landed_o, small_o, all_o):
        for cp in _late_copies(part_ref, landed_ref, small_ref, all_ref, send_sems, recv_sems):
            cp.wait_send()
            cp.wait_recv()

    hbm = lambda a: pltpu.HBM(a.shape, a.dtype)
    out = pl.pallas_call(
        body, name="late_exchange_wait",
        out_shape=(hbm(part), hbm(landed), hbm(small), hbm(everyone)),
        in_specs=[HBM_SPEC] * 4 + [SEM_SPEC, SEM_SPEC, pl.BlockSpec(memory_space=pl.ANY)],
        out_specs=(HBM_SPEC,) * 4, input_output_aliases={0: 0, 1: 1, 2: 2, 3: 3},
        compiler_params=pltpu.CompilerParams(has_side_effects=pltpu.SideEffectType.DATAFLOW_SIDE_EFFECTING),
    )(part, landed, small, everyone, send_sems, recv_sems, after)
    return out[0], out[1], out[3]


def _chip_sum(parts, landed, chip_arr, name):
    rh, cc = landed.shape[1:]
    tr = rh // 2

    def body(chip_ref, own_ref, p_ref, o_ref):
        o_ref[...] = ((own_ref[0].astype(F32) + p_ref[0].astype(F32)) + p_ref[1].astype(F32)) + p_ref[2].astype(F32)

    grid_spec = pltpu.PrefetchScalarGridSpec(
        num_scalar_prefetch=1, grid=(2,),
        in_specs=[pl.BlockSpec((1, tr, cc), lambda i, chip_ref: (chip_ref[0], i, 0)),
                  pl.BlockSpec((3, tr, cc), lambda i, chip_ref: (0, i, 0))],
        out_specs=pl.BlockSpec((tr, cc), lambda i, chip_ref: (i, 0)))
    return pl.pallas_call(body, grid_spec=grid_spec, out_shape=SDS((rh, cc), F32),
                          compiler_params=_cparams(("parallel",)), name=name)(chip_arr, parts, landed)


def _device_sum(p):
    def body(p_ref, o_ref):
        acc = p_ref[0]
        for k in range(1, 8):
            acc = acc + p_ref[k]
        o_ref[...] = acc

    return pl.pallas_call(body, out_shape=SDS(p.shape[1:], F32), name="small_sum")(p)


def _join_halves(halves, name):
    n = len(halves)

    def body(*refs):
        ins, outs = refs[:n], refs[n:2 * n]
        send_sems, recv_sems = refs[2 * n:]
        x, y, c, _ = _place()
        copies = [pltpu.make_async_remote_copy(
            src_ref=ins[a], dst_ref=outs[a], send_sem=send_sems.at[a], recv_sem=recv_sems.at[a],
            device_id=(x, y, 1 - c), device_id_type=MESH) for a in range(n)]
        for cp in copies:
            cp.start()
        for cp in copies:
            cp.wait()

    return pl.pallas_call(
        body, in_specs=[HBM_SPEC] * n, out_specs=[HBM_SPEC] * n,
        out_shape=[SDS(h.shape, h.dtype) for h in halves],
        scratch_shapes=[pltpu.SemaphoreType.DMA((n,)), pltpu.SemaphoreType.DMA((n,))],
        name=name)(*halves)


def _adamw(w, g, m, v, name, tr=None):
    rows, cols = w.shape
    tr = tr or rows
    assert rows % tr == 0

    def body(w_ref, g_ref, m_ref, v_ref, d_ref, nm_ref, nv_ref):
        gg = g_ref[...]
        nm = ADAM_B1 * m_ref[...] + (1.0 - ADAM_B1) * gg
        nv = ADAM_B2 * v_ref[...] + (1.0 - ADAM_B2) * (gg * gg)
        nm_ref[...] = nm
        nv_ref[...] = nv
        m_hat = nm / (1.0 - ADAM_B1 ** ADAM_STEP)
        v_hat = nv / (1.0 - ADAM_B2 ** ADAM_STEP)
        d_ref[...] = -ADAM_LR * (m_hat / (jnp.sqrt(v_hat) + ADAM_EPS) + ADAM_WD * w_ref[...])

    blk = pl.BlockSpec((tr, cols), lambda i: (i, 0))
    return pl.pallas_call(
        body, grid=(rows // tr,), in_specs=[blk] * 4, out_specs=[blk] * 3,
        out_shape=[SDS((rows, cols), F32)] * 3,
        compiler_params=_cparams(("parallel",)), name=name)(w, g, m, v)


def _adamw_halves(w, mine, theirs, m, v, c_arr, name):
    rows, cols = w.shape
    rh = rows // 2
    tr = rh if rh <= 352 else 256
    nh = rh // tr

    def body(c_ref, w_ref, mine_ref, theirs_ref, m_ref, v_ref, g_ref, d_ref, nm_ref, nv_ref):
        own = jnp.full((tr, cols), pl.program_id(0), jnp.int32) == c_ref[0]
        gg = jnp.where(own, mine_ref[...], theirs_ref[...])
        g_ref[...] = gg
        nm = ADAM_B1 * m_ref[...] + (1.0 - ADAM_B1) * gg
        nv = ADAM_B2 * v_ref[...] + (1.0 - ADAM_B2) * (gg * gg)
        nm_ref[...] = nm
        nv_ref[...] = nv
        m_hat = nm / (1.0 - ADAM_B1 ** ADAM_STEP)
        v_hat = nv / (1.0 - ADAM_B2 ** ADAM_STEP)
        d_ref[...] = -ADAM_LR * (m_hat / (jnp.sqrt(v_hat) + ADAM_EPS) + ADAM_WD * w_ref[...])

    whole = pl.BlockSpec((tr, cols), lambda hh, i, c_ref: (hh * nh + i, 0))
    part = pl.BlockSpec((tr, cols), lambda hh, i, c_ref: (i, 0))
    grid_spec = pltpu.PrefetchScalarGridSpec(
        num_scalar_prefetch=1, grid=(2, nh), in_specs=[whole, part, part, whole, whole], out_specs=[whole] * 4)
    return pl.pallas_call(body, grid_spec=grid_spec, out_shape=[SDS((rows, cols), F32)] * 4,
                          compiler_params=_cparams(("parallel", "parallel")), name=name)(c_arr, w, mine, theirs, m, v)


def _pack_small(pre_mix, post_mix, pre_ffn, post_ffn, rel_bias, b_forget, sinks):
    def at(row, v):
        return jnp.pad(v, ((row, 7 - row), (0, D_MODEL - v.shape[1])))
    return (at(0, pre_mix) + at(1, post_mix) + at(2, pre_ffn) + at(3, post_ffn)
            + at(4, rel_bias.reshape(1, N_BUCKETS * N_HEADS)) + at(5, jnp.concatenate([b_forget, sinks], axis=1)))


def _unpack_small(p):
    return dict(ln_pre_mix=p[0:1], ln_post_mix=p[1:2], ln_pre_ffn=p[2:3], ln_post_ffn=p[3:4],
                rel_bias=p[4, :N_BUCKETS * N_HEADS].reshape(N_BUCKETS, N_HEADS),
                b_forget=p[5:6, 0:N_HEADS], sinks=p[5:6, N_HEADS:2 * N_HEADS])


WEIGHTS = ("meta_tokens", "rel_bias", "ln_pre_mix", "ln_post_mix", "ln_pre_ffn", "ln_post_ffn",
           "w_in", "b_forget", "sinks", "w_out", "w_gate_up", "w_down")


def kernel(x, meta_tokens, rel_bias, ln_pre_mix, ln_post_mix, ln_pre_ffn, ln_post_ffn, w_in, b_forget, sinks, w_out, w_gate_up, w_down, loss_target, m_meta_tokens, m_rel_bias, m_ln_pre_mix, m_ln_post_mix, m_ln_pre_ffn, m_ln_post_ffn, m_w_in, m_b_forget, m_sinks, m_w_out, m_w_gate_up, m_w_down, v_meta_tokens, v_rel_bias, v_ln_pre_mix, v_ln_post_mix, v_ln_pre_ffn, v_ln_post_ffn, v_w_in, v_b_forget, v_sinks, v_w_out, v_w_gate_up, v_w_down):
    xi, yi, ci = lax.axis_index("x"), lax.axis_index("y"), lax.axis_index("c")
    chip = 2 * xi + yi
    c_arr = jnp.reshape(ci, (1,)).astype(jnp.int32)

    def halves(w, dtype):
        return w.astype(dtype).reshape(2, w.shape[0] // 2, w.shape[1])

    def with_own(gathered, shards):
        return [lax.dynamic_update_slice(got, own[None], (chip, 0, 0, 0)) for got, own in zip(gathered, shards)]

    shards = [halves(jnp.transpose(w_in[0]), BF16), halves(meta_tokens, F32)]
    gw_in, g_meta = with_own(_run_alone(_gather_rider(shards, False), "gather_mixer_weights"), shards)
    out_shards = [halves(w_out[0], BF16)]
    ffn_shards = [halves(w_gate_up[0], BF16), halves(w_down[0], BF16)]

    def ffn_weights(carried):
        gw_gu, gw_dn = carried
        return gw_gu.reshape(D_MODEL, 2 * D_FF), gw_dn.reshape(D_FF, D_MODEL)

    early = {}

    def early_grads(dw_gu, dw_dn, dw_out):
        early["grads"] = [dw_out.reshape(4, 2, 128, D_MODEL), dw_gu.reshape(4, 2, 512, FF_T),
                          dw_dn.reshape(4, 2, 352, D_MODEL)]
        return _direct_rider(early["grads"])
    w_in_b = jnp.pad(gw_in.reshape(D_PROJ, D_MODEL), ((0, D_PROJ_P - D_PROJ), (0, 0)))
    meta_all = g_meta.reshape(4, N_META, D_MODEL // 4).transpose(1, 0, 2).reshape(N_META, D_MODEL)

    loc = _local_step(x[0], loss_target[0], meta_all, rel_bias, ln_pre_mix, ln_post_mix, ln_pre_ffn, ln_post_ffn,
                      b_forget, sinks, w_in_b, _gather_rider(out_shards, True),
                      lambda carried: carried[0].reshape(D_MODEL, D_MODEL),
                      _gather_rider(ffn_shards, True, by_columns=(0,)), ffn_weights, early_grads)

    small = jnp.concatenate(
        [_pack_small(loc["ln_pre_mix"], loc["ln_post_mix"], loc["ln_pre_ffn"], loc["ln_post_ffn"],
                     loc["rel_bias"], loc["b_forget"], loc["sinks"])
         + jnp.pad(loc["loss"].reshape(1, 1), ((LOSS_ROW, 7 - LOSS_ROW), (0, D_MODEL - 1))), loc["meta"]], axis=0)

    dw_in = loc["w_in"].reshape(1, 2, D_MODEL // 2, D_PROJ_P)
    (got_in,) = _run_alone(_swap_rider([dw_in]), "swap_halves_late")
    half_sum = _pair_sum(dw_in, got_in, c_arr, "pair_sum_late")
    part_in = half_sum[0, :, :D_PROJ].reshape(D_MODEL // 2, 4, D_PROJ // 4).transpose(1, 0, 2)
    send_sems, recv_sems, part_sent, landing, small_sent, everyone, token = _late_exchange_start(part_in, small)
    chip_arr = jnp.reshape(chip, (1,)).astype(jnp.int32)
    own_arr = jnp.stack([chip, ci]).astype(jnp.int32)
    grad, delta, new_m, new_v = {}, {}, {}, {}

    def update(names, mine):
        theirs = _join_halves(mine, "join_" + names[0])
        big = dict(w_in=(w_in, m_w_in, v_w_in), w_out=(w_out, m_w_out, v_w_out),
                   w_gate_up=(w_gate_up, m_w_gate_up, v_w_gate_up), w_down=(w_down, m_w_down, v_w_down))
        for name, g_mine, g_theirs in zip(names, mine, theirs):
            w, m, v = big[name]
            g, d, nm, nv = _adamw_halves(w[0], g_mine, g_theirs, m[0], v[0], c_arr, "adamw_" + name)
            grad[name], delta[name], new_m[name], new_v[name] = g[None], d[None], nm[None], nv[None]

    update(("w_out", "w_gate_up", "w_down"),
           [_owner_sum(g, l, own_arr, token, "owner_sum_%d" % a)
            for a, (g, l) in enumerate(zip(early["grads"], loc["landed"]))])
    part_back, landed_in, small_all = _late_exchange_wait(send_sems, recv_sems, part_sent, landing, small_sent,
                                                         everyone, new_v["w_down"])
    mine_in = _chip_sum(part_back, landed_in, chip_arr, "chip_sum_in")
    (theirs_in,) = _join_halves([mine_in], "join_w_in")
    g_w_in = jnp.where(ci == 0, jnp.concatenate([mine_in, theirs_in], axis=0),
                       jnp.concatenate([theirs_in, mine_in], axis=0))
    view = lambda a: jnp.transpose(a).reshape(D_PROJ // 4 * 8, BLK)
    back = lambda a: jnp.transpose(a.reshape(D_PROJ // 4, D_MODEL))[None]
    d, nm, nv = _adamw(view(w_in[0]), view(g_w_in), view(m_w_in[0]), view(v_w_in[0]), "adamw_w_in",
                       tr=D_PROJ // 4 * 4)
    grad["w_in"], delta["w_in"], new_m["w_in"], new_v["w_in"] = g_w_in[None], back(d), back(nm), back(nv)
    me = 4 * xi + 2 * yi + ci
    small_sum = _device_sum(lax.dynamic_update_slice(small_all, small[None], (me, 0, 0)))
    g_meta_tokens = lax.dynamic_slice(small_sum[8:N_SMALL], (0, chip * (D_MODEL // 4)), (N_META, D_MODEL // 4))
    g_small = small_sum[0:8]
    grad.update(_unpack_small(g_small))
    grad.update(meta_tokens=g_meta_tokens)
    delta["meta_tokens"], new_m["meta_tokens"], new_v["meta_tokens"] = _adamw(
        meta_tokens, g_meta_tokens, m_meta_tokens, v_meta_tokens, "adamw_meta")
    d, nm, nv = _adamw(
        _pack_small(ln_pre_mix, ln_post_mix, ln_pre_ffn, ln_post_ffn, rel_bias, b_forget, sinks), g_small,
        _pack_small(m_ln_pre_mix, m_ln_post_mix, m_ln_pre_ffn, m_ln_post_ffn, m_rel_bias, m_b_forget, m_sinks),
        _pack_small(v_ln_pre_mix, v_ln_post_mix, v_ln_pre_ffn, v_ln_post_ffn, v_rel_bias, v_b_forget, v_sinks),
        "adamw_small")
    delta.update(_unpack_small(d))
    new_m.update(_unpack_small(nm))
    new_v.update(_unpack_small(nv))

    loss = small_sum[LOSS_ROW, 0]
    return (loss,loc["grad_x"][None], *[grad[k] for k in WEIGHTS], *[delta[k] for k in WEIGHTS],
            *[new_m[k] for k in WEIGHTS], *[new_v[k] for k in WEIGHTS])
```

```python
import math

import numpy as np
import jax
import jax.numpy as jnp
from jax import lax
from jax.experimental import pallas as pl
from jax.experimental.pallas import tpu as pltpu

F32 = jnp.float32
BF16 = jnp.bfloat16
MESH = pl.DeviceIdType.MESH
SDS = jax.ShapeDtypeStruct

D_MODEL = 1024
SEQ = 4096
N_META = 16
N_HEADS = 8
HALF = 64
D_FF = 2816
N_BUCKETS = 32
EPS = 1e-6
NEG = -1e30
SCALE = 0.125
LOG2E = 1.4426950408889634
LN2 = 0.6931471805599453
PAD_ROWS = 112
ROW0 = PAD_ROWS + N_META
LP = ROW0 + SEQ
BLK = 128
NBLK = LP // BLK
TM = 384
NT = LP // TM
TM_PURE = LP // 2
TM_MID = LP // 4
TM_EPI = LP // 6
TN = 256
TK_W = LP // 2
D_PROJ = 2312
D_PROJ_P = 2432
D_QKV = 2304
FF_T = 1408
VMEM_LIMIT = 56 * 1024 * 1024

ADAM_LR = 0.001
ADAM_B1 = 0.9
ADAM_B2 = 0.999
ADAM_EPS = 1e-08
ADAM_WD = 0.01
ADAM_STEP = 10

QA = 0
KA, VA = 4, 5
QB, KB, VB = 3, 5, 7
W2 = 256

NT_DIMS = (((1,), (1,)), ((), ()))
TN_DIMS = (((0,), (0,)), ((), ()))


def _cparams(sem):
    return pltpu.CompilerParams(dimension_semantics=sem, vmem_limit_bytes=VMEM_LIMIT)


def _t5_bucket_np(d):
    n = np.maximum(d, 0).astype(np.int32)
    nf = np.maximum(n, 1).astype(np.float32)
    large = 16 + (np.log(nf / np.float32(16)) / np.float32(math.log(8.0)) * np.float32(16)).astype(np.int32)
    large = np.minimum(large, N_BUCKETS - 1)
    return np.where(n < 16, n, large).astype(np.int32)


def _bucket_tables():
    qi = np.arange(BLK)[:, None]
    ki = np.arange(BLK)[None, :]
    return np.stack([_t5_bucket_np(qi - ki), _t5_bucket_np(qi - ki + BLK)])


def _rms(x):
    return lax.rsqrt(jnp.mean(x * x, axis=-1, keepdims=True) + EPS)


def _rms_bwd(n, r, gdy):
    return r * (gdy - n * jnp.mean(n * gdy, axis=-1, keepdims=True))


def _pre_mix(x, meta, gain, w_in_b):
    half = D_QKV // 2
    n_steps = LP // TM_MID

    def body(x_ref, meta_ref, g_ref, w_ref, h0_ref, hn_ref, proj_ref, f_ref, tile, sems):
        i = pl.program_id(0)

        def tile_copy(s):
            skip = ROW0 if s == 0 else 0
            return pltpu.make_async_copy(x_ref.at[s * TM_MID + skip - ROW0:(s + 1) * TM_MID - ROW0],
                                         tile.at[s % 2, skip:TM_MID], sems.at[s % 2])

        @pl.when(i == 0)
        def _():
            tile_copy(0).start()
            tile[0, :PAD_ROWS] = jnp.zeros((PAD_ROWS, D_MODEL), F32)
            tile[0, PAD_ROWS:ROW0] = meta_ref[...]

        for s in range(n_steps):
            @pl.when(i == s)
            def _():
                if s + 1 < n_steps:
                    tile_copy(s + 1).start()
                tile_copy(s).wait()

        x = tile[i % 2]
        h0_ref[...] = x
        hn = (x * _rms(x) * g_ref[...]).astype(BF16)
        hn_ref[...] = hn
        proj_ref[:, :half] = lax.dot_general(hn, w_ref[:half, :], NT_DIMS, preferred_element_type=F32).astype(BF16)
        p = lax.dot_general(hn, w_ref[half:, :], NT_DIMS, preferred_element_type=F32)
        proj_ref[:, half:] = p[:, :half].astype(BF16)
        f_ref[...] = p[:, half:]

    return pl.pallas_call(
        body, grid=(n_steps,),
        in_specs=[HBM_SPEC,
                  pl.BlockSpec((N_META, D_MODEL), lambda i: (0, 0)),
                  pl.BlockSpec((1, D_MODEL), lambda i: (0, 0)),
                  pl.BlockSpec((D_PROJ_P, D_MODEL), lambda i: (0, 0))],
        out_specs=[pl.BlockSpec((TM_MID, D_MODEL), lambda i: (i, 0)),
                   pl.BlockSpec((TM_MID, D_MODEL), lambda i: (i, 0)),
                   pl.BlockSpec((TM_MID, D_QKV), lambda i: (i, 0)),
                   pl.BlockSpec((TM_MID, BLK), lambda i: (i, 0))],
        out_shape=[SDS((LP, D_MODEL), F32), SDS((LP, D_MODEL), BF16), SDS((LP, D_QKV), BF16), SDS((LP, BLK), F32)],
        scratch_shapes=[pltpu.VMEM((2, TM_MID, D_MODEL), F32), pltpu.SemaphoreType.DMA((2,))],
        compiler_params=_cparams(("arbitrary",)), name="pre_mix")(x, meta, gain, w_in_b)


def _attn_out(o_a, o_b, w_out_b, h0, g_post, g_pre_ffn):
    def body(oa_ref, ob_ref, w_ref, h0_ref, gp_ref, gf_ref, a_ref, h1_ref, hn2_ref):
        a = (jnp.dot(oa_ref[...], w_ref[0:512, :], preferred_element_type=F32)
             + jnp.dot(ob_ref[...], w_ref[512:1024, :], preferred_element_type=F32))
        a_ref[...] = a
        h1 = h0_ref[...] + a * _rms(a) * gp_ref[...]
        h1_ref[...] = h1
        hn2_ref[...] = (h1 * _rms(h1) * gf_ref[...]).astype(BF16)

    row = lambda w: pl.BlockSpec((TM_EPI, w), lambda i: (i, 0))
    vec = pl.BlockSpec((1, D_MODEL), lambda i: (0, 0))
    return pl.pallas_call(
        body, grid=(LP // TM_EPI,),
        in_specs=[row(512), row(512), pl.BlockSpec((D_MODEL, D_MODEL), lambda i: (0, 0)), row(D_MODEL), vec, vec],
        out_specs=[row(D_MODEL), row(D_MODEL), row(D_MODEL)],
        out_shape=[SDS((LP, D_MODEL), F32), SDS((LP, D_MODEL), F32), SDS((LP, D_MODEL), BF16)],
        compiler_params=_cparams(("parallel",)), name="attn_out")(o_a, o_b, w_out_b, h0, g_post, g_pre_ffn)


def _ffn_up(hn2, w_gu_b):
    def body(x_ref, wg_ref, wu_ref, g_ref, u_ref, act_ref):
        x = x_ref[...]
        g = jnp.dot(x, wg_ref[...], preferred_element_type=F32)
        u = jnp.dot(x, wu_ref[...], preferred_element_type=F32)
        g_ref[...] = g.astype(BF16)
        u_ref[...] = u.astype(BF16)
        act_ref[...] = (g * (1.0 / (1.0 + jnp.exp(-g))) * u).astype(BF16)

    out = pl.BlockSpec((LP, TN), lambda j: (0, j))
    return pl.pallas_call(
        body, grid=(D_FF // TN,),
        in_specs=[pl.BlockSpec((LP, D_MODEL), lambda j: (0, 0)),
                  pl.BlockSpec((D_MODEL, TN), lambda j: (0, j)),
                  pl.BlockSpec((D_MODEL, TN), lambda j: (0, j + D_FF // TN))],
        out_specs=[out, out, out],
        out_shape=[SDS((LP, D_FF), BF16)] * 3,
        compiler_params=_cparams(("parallel",)), name="ffn_up")(hn2, w_gu_b, w_gu_b)


def _ffn_down_loss(act, w_dn_b, h1, tgt, g_post_ffn):
    def body(act_ref, w_ref, h1_ref, t0_ref, t1_ref, t2_ref, g_ref, dff_ref, dy_ref, loss_ref, dg_ref):
        i = pl.program_id(0)
        target = jnp.concatenate([t0_ref[...], t1_ref[...], t2_ref[...]], axis=0)

        @pl.when(i == 0)
        def _():
            loss_ref[...] = jnp.zeros_like(loss_ref)
            dg_ref[...] = jnp.zeros_like(dg_ref)

        ff = jnp.dot(act_ref[...], w_ref[...], preferred_element_type=F32)
        r = _rms(ff)
        n = ff * r
        g = g_ref[...]
        y = h1_ref[...] + n * g
        rows = i * TM + lax.broadcasted_iota(jnp.int32, (TM, D_MODEL), 0)
        diff = jnp.where(rows >= ROW0, y - target, 0.0)
        loss_ref[...] += 0.5 * jnp.sum(diff * diff) / D_MODEL
        dy = diff / D_MODEL
        dy_ref[...] = dy
        dg_ref[...] += jnp.sum(dy * n, axis=0, keepdims=True)
        dff_ref[...] = _rms_bwd(n, r, g * dy).astype(BF16)

    row = pl.BlockSpec((TM, D_MODEL), lambda i: (i, 0))
    tblk = lambda j: pl.BlockSpec((BLK, D_MODEL), lambda i: (jnp.maximum(3 * i - 1 + j, 0), 0))
    return pl.pallas_call(
        body, grid=(NT,),
        in_specs=[pl.BlockSpec((TM, D_FF), lambda i: (i, 0)), pl.BlockSpec((D_FF, D_MODEL), lambda i: (0, 0)),
                  row, tblk(0), tblk(1), tblk(2), pl.BlockSpec((1, D_MODEL), lambda i: (0, 0))],
        out_specs=[row, row, pl.BlockSpec((8, BLK), lambda i: (0, 0)), pl.BlockSpec((1, D_MODEL), lambda i: (0, 0))],
        out_shape=[SDS((LP, D_MODEL), BF16), SDS((LP, D_MODEL), F32), SDS((8, BLK), F32), SDS((1, D_MODEL), F32)],
        compiler_params=_cparams(("arbitrary",)), name="ffn_down_loss")(act, w_dn_b, h1, tgt, tgt, tgt, g_post_ffn)


def _ffn_down_bwd(dff, w_dn_b, g, u):
    def body(d_ref, w_ref, g_ref, u_ref, dg_ref, du_ref):
        dact = lax.dot_general(d_ref[...], w_ref[...], NT_DIMS, preferred_element_type=F32)
        gg = g_ref[...].astype(F32)
        sig = 1.0 / (1.0 + jnp.exp(-gg))
        dg_ref[...] = (dact * u_ref[...].astype(F32) * sig * (1.0 + gg * (1.0 - sig))).astype(BF16)
        du_ref[...] = (dact * gg * sig).astype(BF16)

    blk = pl.BlockSpec((LP, TN), lambda j: (0, j))
    return pl.pallas_call(
        body, grid=(D_FF // TN,),
        in_specs=[pl.BlockSpec((LP, D_MODEL), lambda j: (0, 0)),
                  pl.BlockSpec((TN, D_MODEL), lambda j: (j, 0)), blk, blk],
        out_specs=[blk, blk],
        out_shape=[SDS((LP, D_FF), BF16)] * 2,
        compiler_params=_cparams(("parallel",)), name="ffn_down_bwd")(dff, w_dn_b, g, u)


def _ffn_up_bwd(dg, du, w_gu_b, h1, a, dy, g_pre_ffn, g_post_mix):
    def body(dg_ref, du_ref, w_ref, h1_ref, a_ref, dy_ref, gf_ref, gp_ref,
             dh1_ref, da_ref, dgf_ref, dgp_ref, acc):
        i = pl.program_id(0)
        s = pl.program_id(1)

        @pl.when((i == 0) & (s == 0))
        def _():
            dgf_ref[...] = jnp.zeros_like(dgf_ref)
            dgp_ref[...] = jnp.zeros_like(dgp_ref)

        @pl.when(s == 0)
        def _():
            acc[...] = jnp.zeros_like(acc)

        @pl.when(s < 2)
        def _():
            acc[...] += lax.dot_general(dg_ref[...], w_ref[...], NT_DIMS, preferred_element_type=F32)

        @pl.when(s >= 2)
        def _():
            acc[...] += lax.dot_general(du_ref[...], w_ref[...], NT_DIMS, preferred_element_type=F32)

        @pl.when(s == 3)
        def _():
            dhn2 = acc[...]
            h1 = h1_ref[...]
            r2 = _rms(h1)
            n2 = h1 * r2
            dgf_ref[...] += jnp.sum(dhn2 * n2, axis=0, keepdims=True)
            dh1 = dy_ref[...] + _rms_bwd(n2, r2, gf_ref[...] * dhn2)
            dh1_ref[...] = dh1
            av = a_ref[...]
            ra = _rms(av)
            na = av * ra
            dgp_ref[...] += jnp.sum(dh1 * na, axis=0, keepdims=True)
            da_ref[...] = _rms_bwd(na, ra, gp_ref[...] * dh1).astype(BF16)

    row = pl.BlockSpec((TM_EPI, D_MODEL), lambda i, s: (i, 0))
    vec = pl.BlockSpec((1, D_MODEL), lambda i, s: (0, 0))
    return pl.pallas_call(
        body, grid=(LP // TM_EPI, 4),
        in_specs=[pl.BlockSpec((TM_EPI, FF_T), lambda i, s: (i, jnp.minimum(s, 1))),
                  pl.BlockSpec((TM_EPI, FF_T), lambda i, s: (i, jnp.maximum(s - 2, 0))),
                  pl.BlockSpec((D_MODEL, FF_T), lambda i, s: (0, s)),
                  row, row, row, vec, vec],
        out_specs=[row, row, vec, vec],
        out_shape=[SDS((LP, D_MODEL), F32), SDS((LP, D_MODEL), BF16), SDS((1, D_MODEL), F32), SDS((1, D_MODEL), F32)],
        scratch_shapes=[pltpu.VMEM((TM_EPI, D_MODEL), F32)],
        compiler_params=_cparams(("arbitrary", "arbitrary")), name="ffn_up_bwd",
    )(dg, du, w_gu_b, h1, a, dy, g_pre_ffn, g_post_mix)


def _attn_out_bwd(da, w_out_b):
    def body(d_ref, w_ref, o_ref):
        o_ref[...] = lax.dot_general(d_ref[...], w_ref[...], NT_DIMS, preferred_element_type=F32).astype(BF16)

    row = pl.BlockSpec((TM_PURE, D_MODEL), lambda i: (i, 0))
    return pl.pallas_call(
        body, grid=(LP // TM_PURE,),
        in_specs=[row, pl.BlockSpec((D_MODEL, D_MODEL), lambda i: (0, 0))],
        out_specs=row, out_shape=SDS((LP, D_MODEL), BF16),
        compiler_params=_cparams(("parallel",)), name="attn_out_bwd")(da, w_out_b)


def _pre_mix_bwd(dq_a, dq_b, dk_b, dv_b, dk_a, dv_a, df, w_in_b, h0, dh1, g_pre_mix):
    n_steps = LP // TM_EPI

    def body(qa_ref, qb_ref, kb_ref, vb_ref, ka_ref, va_ref, f_ref, w_ref, h0_ref, dh1_ref, g_ref,
             dproj_ref, dx_ref, dmeta_ref, dg_ref, tile, sems):
        i = pl.program_id(0)

        def tile_copy(s):
            skip = ROW0 if s == 0 else 0
            return pltpu.make_async_copy(tile.at[s % 2, skip:TM_EPI],
                                         dx_ref.at[s * TM_EPI + skip - ROW0:(s + 1) * TM_EPI - ROW0],
                                         sems.at[s % 2])

        @pl.when(i == 0)
        def _():
            dg_ref[...] = jnp.zeros_like(dg_ref)

        for s in range(2, n_steps):
            @pl.when(i == s)
            def _():
                tile_copy(s - 2).wait()

        dproj = jnp.concatenate(
            [qa_ref[...], ka_ref[...].astype(BF16), va_ref[...].astype(BF16), (qb_ref[...] * SCALE).astype(BF16),
             kb_ref[...], vb_ref[...], f_ref[...].astype(BF16)], axis=1)
        dproj_ref[...] = dproj
        dhn = jnp.dot(dproj, w_ref[...], preferred_element_type=F32)
        x = h0_ref[...]
        r = _rms(x)
        n = x * r
        dg_ref[...] += jnp.sum(dhn * n, axis=0, keepdims=True)
        tile[i % 2] = dh1_ref[...] + _rms_bwd(n, r, g_ref[...] * dhn)

        for s in range(n_steps):
            @pl.when(i == s)
            def _():
                tile_copy(s).start()
                if s == 0:
                    dmeta_ref[...] = tile[0, PAD_ROWS:ROW0]
                if s == n_steps - 1:
                    tile_copy(s - 1).wait()
                    tile_copy(s).wait()

    row = lambda w: pl.BlockSpec((TM_EPI, w), lambda i: (i, 0))
    vec = pl.BlockSpec((1, D_MODEL), lambda i: (0, 0))
    return pl.pallas_call(
        body, grid=(n_steps,),
        in_specs=[row(512), row(512), row(512), row(512), row(BLK), row(BLK), row(BLK),
                  pl.BlockSpec((D_PROJ_P, D_MODEL), lambda i: (0, 0)), row(D_MODEL), row(D_MODEL), vec],
        out_specs=[row(D_PROJ_P), HBM_SPEC, pl.BlockSpec((N_META, D_MODEL), lambda i: (0, 0)), vec],
        out_shape=[SDS((LP, D_PROJ_P), BF16), SDS((SEQ, D_MODEL), F32), SDS((N_META, D_MODEL), F32),
                   SDS((1, D_MODEL), F32)],
        scratch_shapes=[pltpu.VMEM((2, TM_EPI, D_MODEL), F32), pltpu.SemaphoreType.DMA((2,))],
        compiler_params=_cparams(("arbitrary",)), name="pre_mix_bwd",
    )(dq_a, dq_b, dk_b, dv_b, dk_a, dv_a, df, w_in_b, h0, dh1, g_pre_mix)


def _mm_tn(parts, b, tm, name, out_dtype=F32):
    widths = [p.shape[1] for p in parts]
    m_total = sum(widths)
    n = b.shape[1]
    whole = len(parts) > 1
    n_k = LP // TK_W
    assert (tm == m_total) if whole else (m_total % tm == 0)

    def body(*refs):
        a_refs, b_ref, o_ref, acc = refs[:-3], refs[-3], refs[-2], refs[-1]
        k = pl.program_id(1)

        @pl.when(k == 0)
        def _():
            acc[...] = jnp.zeros_like(acc)
        a = a_refs[0][...] if not whole else jnp.concatenate([r[...] for r in a_refs], axis=1)
        acc[...] += lax.dot_general(a, b_ref[...], TN_DIMS, preferred_element_type=F32)

        @pl.when(k == n_k - 1)
        def _():
            o_ref[...] = acc[...].astype(out_dtype)

    a_specs = ([pl.BlockSpec((TK_W, w), lambda mi, k: (k, 0)) for w in widths] if whole
               else [pl.BlockSpec((TK_W, tm), lambda mi, k: (k, mi))])
    return pl.pallas_call(
        body, grid=(m_total // tm, n_k),
        in_specs=a_specs + [pl.BlockSpec((TK_W, n), lambda mi, k: (k, 0))],
        out_specs=pl.BlockSpec((tm, n), lambda mi, k: (mi, 0)),
        out_shape=SDS((m_total, n), out_dtype),
        scratch_shapes=[pltpu.VMEM((tm, n), F32)],
        compiler_params=_cparams(("parallel", "arbitrary")), name=name)(*parts, b)


def _dw_gate_up(hn2, dg, du):
    n_k = LP // TK_W

    def body(a_ref, dg_ref, du_ref, o_ref, acc):
        s = pl.program_id(0)
        k = pl.program_id(1)

        @pl.when(k == 0)
        def _():
            acc[...] = jnp.zeros_like(acc)

        @pl.when(s < 2)
        def _():
            acc[...] += lax.dot_general(a_ref[...], dg_ref[...], TN_DIMS, preferred_element_type=F32)

        @pl.when(s >= 2)
        def _():
            acc[...] += lax.dot_general(a_ref[...], du_ref[...], TN_DIMS, preferred_element_type=F32)

        @pl.when(k == n_k - 1)
        def _():
            o_ref[0] = acc[...].astype(BF16)

    return pl.pallas_call(
        body, grid=(4, n_k),
        in_specs=[pl.BlockSpec((TK_W, D_MODEL), lambda s, k: (k, 0)),
                  pl.BlockSpec((TK_W, FF_T), lambda s, k: (k, jnp.minimum(s, 1))),
                  pl.BlockSpec((TK_W, FF_T), lambda s, k: (k, jnp.maximum(s - 2, 0)))],
        out_specs=pl.BlockSpec((1, D_MODEL, FF_T), lambda s, k: (s, 0, 0)),
        out_shape=SDS((4, D_MODEL, FF_T), BF16),
        scratch_shapes=[pltpu.VMEM((D_MODEL, FF_T), F32)],
        compiler_params=_cparams(("parallel", "arbitrary")), name="dw_gate_up")(hn2, dg, du)


def _split3(x):
    hi = x.astype(BF16)
    r1 = x - hi.astype(F32)
    mid = r1.astype(BF16)
    lo = (r1 - mid.astype(F32)).astype(BF16)
    return hi, mid, lo


def _tri_matmul(tri, x):
    hi, mid, lo = _split3(x)
    dot = lambda t: jnp.dot(tri, t, preferred_element_type=F32)
    return dot(hi) + dot(mid) + dot(lo)


def _forget_cumsum(f, b_forget_p):
    def body(f_ref, b_ref, cum_ref, carry):
        i = pl.program_id(0)

        @pl.when(i == 0)
        def _():
            carry[...] = jnp.zeros_like(carry)

        z = f_ref[...] + b_ref[...]
        ls = jnp.minimum(z, 0.0) - jnp.log(1.0 + jnp.exp(-jnp.abs(z)))
        rows = i * TM + lax.broadcasted_iota(jnp.int32, (TM, BLK), 0)
        ls = jnp.where(rows >= PAD_ROWS, ls, 0.0)
        r = lax.broadcasted_iota(jnp.int32, (TM, TM), 0)
        c = lax.broadcasted_iota(jnp.int32, (TM, TM), 1)
        tri = (c <= r).astype(BF16)
        cum = _tri_matmul(tri, ls) + carry[...]
        cum_ref[...] = cum
        carry[...] = cum[TM - 1:TM, :]

    return pl.pallas_call(
        body, grid=(NT,),
        in_specs=[pl.BlockSpec((TM, BLK), lambda i: (i, 0)), pl.BlockSpec((1, BLK), lambda i: (0, 0))],
        out_specs=pl.BlockSpec((TM, BLK), lambda i: (i, 0)),
        out_shape=SDS((LP, BLK), F32),
        scratch_shapes=[pltpu.VMEM((1, BLK), F32)],
        compiler_params=_cparams(("arbitrary",)), name="forget_cumsum")(f, b_forget_p)


def _forget_cumsum_bwd(dcum, f, b_forget_p):
    def body(d_ref, f_ref, b_ref, df_ref, db_ref, carry):
        i = pl.program_id(0)

        @pl.when(i == 0)
        def _():
            carry[...] = jnp.zeros_like(carry)
            db_ref[...] = jnp.zeros_like(db_ref)

        blk = NT - 1 - i
        r = lax.broadcasted_iota(jnp.int32, (TM, TM), 0)
        c = lax.broadcasted_iota(jnp.int32, (TM, TM), 1)
        tri = (c >= r).astype(BF16)
        d = d_ref[...]
        dls = _tri_matmul(tri, d) + carry[...]
        carry[...] = dls[0:1, :]
        z = f_ref[...] + b_ref[...]
        rows = blk * TM + lax.broadcasted_iota(jnp.int32, (TM, BLK), 0)
        df = jnp.where(rows >= PAD_ROWS, dls / (1.0 + jnp.exp(z)), 0.0)
        df_ref[...] = df
        db_ref[...] += jnp.sum(df, axis=0, keepdims=True)

    rev = pl.BlockSpec((TM, BLK), lambda i: (NT - 1 - i, 0))
    vec = pl.BlockSpec((1, BLK), lambda i: (0, 0))
    return pl.pallas_call(
        body, grid=(NT,),
        in_specs=[rev, rev, vec],
        out_specs=[rev, vec],
        out_shape=[SDS((LP, BLK), F32), SDS((1, BLK), F32)],
        scratch_shapes=[pltpu.VMEM((1, BLK), F32)],
        compiler_params=_cparams(("arbitrary",)), name="forget_cumsum_bwd")(dcum, f, b_forget_p)


def _lane_half(rows):
    return lax.broadcasted_iota(jnp.int32, (rows, BLK), 1) // HALF


def _fox_valid(qi, kj):
    qrow = qi * TM + lax.broadcasted_iota(jnp.int32, (TM, TM), 0)
    krow = kj * TM + lax.broadcasted_iota(jnp.int32, (TM, TM), 1)
    return (krow <= qrow) & ((krow >= PAD_ROWS) | (qrow < PAD_ROWS))


class _Rider:
    def __init__(self, operands, out_shapes, sem_counts, first, middle, last):
        self.operands, self.out_shapes, self.sem_counts = list(operands), list(out_shapes), list(sem_counts)
        self.first, self.middle, self.last = first, middle, last

    def scratch(self):
        return [pltpu.SemaphoreType.DMA((k,)) for k in self.sem_counts]

    def split(self, refs, n_in, n_out, n_scratch):
        a, b = len(self.operands), len(self.out_shapes)
        ins, mine_in = refs[:n_in], refs[n_in:n_in + a]
        outs, mine_out = refs[n_in + a:n_in + a + n_out], refs[n_in + a + n_out:n_in + a + n_out + b]
        rest = refs[n_in + a + n_out + b:]
        return ins, outs, rest[:n_scratch], (mine_in, mine_out, rest[n_scratch:])

    def at_steps(self, mine, is_first, is_middle, is_last):
        for cond, fn in ((is_first, self.first), (is_middle, self.middle), (is_last, self.last)):
            pl.when(cond)(lambda fn=fn: fn(*mine))


HBM_SPEC = pl.BlockSpec(memory_space=pltpu.HBM)


N_AUG = 3
QCHUNKS = ((0, 128), (128, 128), (256, 128))
KSUB = 384
AHEAD = 5
AHEAD_BWD = 1


def _fox_prep(proj, cum):
    def body(q0_ref, q1_ref, k0_ref, k1_ref, v0_ref, v1_ref, c_ref, qa_ref, ka_ref, vt_ref):
        half = _lane_half(TM)
        lane = lax.broadcasted_iota(jnp.int32, (TM, BLK), 1)
        for pp in range(4):
            cols = slice(pp * BLK, (pp + 1) * BLK)
            q_ref, k_ref, v_ref = ((q0_ref, k0_ref, v0_ref), (q1_ref, k1_ref, v1_ref))[pp // 2]
            part = slice((pp % 2) * BLK, (pp % 2 + 1) * BLK)
            qs = q_ref[:, part].astype(F32) * (SCALE * LOG2E)
            kp = k_ref[:, part].astype(F32)
            vp = v_ref[:, part]
            vt_ref[cols, :] = vp.astype(F32).T.astype(BF16)
            for e in range(2):
                h = 2 * pp + e
                a = (1 - e) * HALF
                blk = slice(h * BLK, (h + 1) * BLK)
                hi, mid, lo = _split3(-LOG2E * c_ref[:, h:h + 1])
                q_aug = jnp.where(half == e, qs, jnp.where((lane >= a) & (lane < a + N_AUG), 1.0, 0.0))
                k_aug = jnp.where(half == e, kp, jnp.where(
                    lane == a, hi.astype(F32), jnp.where(lane == a + 1, mid.astype(F32), jnp.where(
                        lane == a + 2, lo.astype(F32), 0.0))))
                qa_ref[blk, :] = q_aug.T.astype(BF16)
                ka_ref[:, blk] = k_aug.astype(BF16)

    row = lambda blk: pl.BlockSpec((TM, W2), lambda i: (i, blk))
    wide = pl.BlockSpec((TM, 1024), lambda i: (i, 0))
    return pl.pallas_call(
        body, grid=(NT,),
        in_specs=[row(QB), row(QB + 1), row(KB), row(KB + 1), row(VB), row(VB + 1),
                  pl.BlockSpec((TM, BLK), lambda i: (i, 0))],
        out_specs=[pl.BlockSpec((1024, TM), lambda i: (0, i)), wide, pl.BlockSpec((512, TM), lambda i: (0, i))],
        out_shape=[SDS((1024, LP), BF16), SDS((LP, 1024), BF16), SDS((512, LP), BF16)],
        compiler_params=_cparams(("parallel",)), name="fox_prep")(proj, proj, proj, proj, proj, proj, cum)


def _over_keys(reduce, x):
    slabs = x.reshape(x.shape[0] // HALF, HALF, x.shape[1])
    return reduce(reduce(slabs, axis=0), axis=0, keepdims=True)


def _fox_valid_t(qi, kj, c, r):
    krow = kj * TM + r * KSUB + lax.broadcasted_iota(jnp.int32, (KSUB, c[1]), 0)
    qrow = qi * TM + c[0] + lax.broadcasted_iota(jnp.int32, (KSUB, c[1]), 1)
    return (krow <= qrow) & ((krow >= PAD_ROWS) | (qrow < PAD_ROWS))


def _fox_fwd(q_aug, k_aug, v_t, rider):
    pairs = [(qi, kj) for qi in range(NT) for kj in range(qi + 1)]
    n_pairs = len(pairs)

    def body(qi_ref, kj_ref, *refs):
        (q_ref, k_ref, vt_ref), (o_ref, lse_ref), (m_s, l_s, acc_s), mine = rider.split(refs, 3, 2, 3)
        n = pl.program_id(0)
        qi = qi_ref[n]
        kj = kj_ref[n]
        rider.at_steps(mine, n == 0, n == n_pairs // 2, n == n_pairs - 1)

        @pl.when(kj == 0)
        def _():
            m_s[...] = jnp.full_like(m_s, NEG)
            l_s[...] = jnp.zeros_like(l_s)
            acc_s[...] = jnp.zeros_like(acc_s)

        def tile(masked):
            steps = [(h, c, r) for h in range(N_HEADS) for c in QCHUNKS for r in range(TM // KSUB)]

            def scores(h, c, r):
                blk = slice(h * BLK, (h + 1) * BLK)
                return jnp.dot(k_ref[r * KSUB:(r + 1) * KSUB, blk], q_ref[blk, c[0]:c[0] + c[1]],
                               preferred_element_type=F32)

            ahead = [scores(*st) for st in steps[:AHEAD]]
            for n, (h, c, r) in enumerate(steps):
                s_t = ahead.pop(0)
                if n + AHEAD < len(steps):
                    ahead.append(scores(*steps[n + AHEAD]))
                cs = slice(c[0], c[0] + c[1])
                if masked:
                    s_t = jnp.where(_fox_valid_t(qi, kj, c, r), s_t, NEG)
                m_prev = m_s[h, :, cs]
                m_new = jnp.maximum(m_prev, _over_keys(jnp.max, s_t))
                p_t = jnp.exp2(s_t - m_new)
                alpha = jnp.exp2(m_prev - m_new)
                l_s[h, :, cs] = alpha * l_s[h, :, cs] + _over_keys(jnp.sum, p_t)
                m_s[h, :, cs] = m_new
                vt = vt_ref[h * HALF:(h + 1) * HALF, r * KSUB:(r + 1) * KSUB]
                acc_s[h, :, cs] = acc_s[h, :, cs] * alpha + jnp.dot(vt, p_t.astype(BF16),
                                                                    preferred_element_type=F32)

        @pl.when((kj < qi) & (kj > 0))
        def _():
            tile(False)

        @pl.when((kj == qi) | (kj == 0))
        def _():
            tile(True)

        @pl.when(kj == qi)
        def _():
            for pp in range(4):
                both = jnp.concatenate([acc_s[2 * pp] * (1.0 / l_s[2 * pp]),
                                        acc_s[2 * pp + 1] * (1.0 / l_s[2 * pp + 1])], axis=0)
                o_ref[:, pp * BLK:(pp + 1) * BLK] = both.T.astype(BF16)
            for h in range(N_HEADS):
                lse_ref[h] = m_s[h] * LN2 + jnp.log(l_s[h])

    grid_spec = pltpu.PrefetchScalarGridSpec(
        num_scalar_prefetch=2, grid=(n_pairs,),
        in_specs=[pl.BlockSpec((1024, TM), lambda n, qi, kj: (0, qi[n])),
                  pl.BlockSpec((TM, 1024), lambda n, qi, kj: (kj[n], 0)),
                  pl.BlockSpec((512, TM), lambda n, qi, kj: (0, kj[n]))] + [HBM_SPEC] * len(rider.operands),
        out_specs=[pl.BlockSpec((TM, 512), lambda n, qi, kj: (qi[n], 0)),
                   pl.BlockSpec((N_HEADS, 1, TM), lambda n, qi, kj: (0, 0, qi[n]))]
        + [HBM_SPEC] * len(rider.out_shapes),
        scratch_shapes=[pltpu.VMEM((N_HEADS, 1, TM), F32), pltpu.VMEM((N_HEADS, 1, TM), F32),
                        pltpu.VMEM((N_HEADS, HALF, TM), F32)] + rider.scratch())
    o_b, lse, *carried = pl.pallas_call(
        body, grid_spec=grid_spec,
        out_shape=[SDS((LP, 512), BF16), SDS((N_HEADS, 1, LP), F32)] + rider.out_shapes,
        compiler_params=_cparams(("arbitrary",)), name="fox_fwd",
    )(jnp.asarray([p[0] for p in pairs], jnp.int32), jnp.asarray([p[1] for p in pairs], jnp.int32),
      q_aug, k_aug, v_t, *rider.operands)
    return o_b, lse, carried


def _fox_bwd(proj, o_b, dmix, lse, ck_t, rider):
    pairs = [(kj, qi) for kj in range(NT) for qi in range(kj, NT)]
    n_pairs = len(pairs)

    def body(kj_ref, qi_ref, *refs):
        ((q0_ref, q1_ref, k0_ref, k1_ref, v0_ref, v1_ref, o_ref, do_ref, lse_ref, ck_ref),
         (dq_ref, dk_ref, dv_ref, dck_ref, dcq_ref), (dk_s, dv_s, dck_s), mine) = rider.split(refs, 10, 5, 3)
        n = pl.program_id(0)
        kj = kj_ref[n]
        qi = qi_ref[n]
        rider.at_steps(mine, n == 0, n == n_pairs // 2, n == n_pairs - 1)

        @pl.when(n == 0)
        def _():
            dq_ref[...] = jnp.zeros_like(dq_ref)
            dcq_ref[...] = jnp.zeros_like(dcq_ref)

        @pl.when(qi == kj)
        def _():
            dk_s[...] = jnp.zeros_like(dk_s)
            dv_s[...] = jnp.zeros_like(dv_s)
            dck_s[...] = jnp.zeros_like(dck_s)

        def tile(masked):
            valid = _fox_valid(qi, kj) if masked else None
            half = _lane_half(TM)
            q0 = pl.multiple_of(qi * TM, TM)
            lane = lax.broadcasted_iota(jnp.int32, (TM, BLK), 1)
            row_sums = jnp.zeros((TM, BLK), F32)
            pair_ops = {}

            def operands(pp):
                if pp not in pair_ops:
                    cols = slice(pp * BLK, (pp + 1) * BLK)
                    q_ref, k_ref, v_ref = ((q0_ref, k0_ref, v0_ref), (q1_ref, k1_ref, v1_ref))[pp // 2]
                    part = slice((pp % 2) * BLK, (pp % 2 + 1) * BLK)
                    pair_ops[pp] = ((q_ref[:, part].astype(F32) * SCALE).astype(BF16), k_ref[:, part],
                                    v_ref[:, part], do_ref[:, cols])
                return pair_ops[pp]

            def scores(pp, e):
                qs, kp, vp, dop = operands(pp)
                ke = jnp.where(half == e, kp, jnp.zeros_like(kp))
                ve = jnp.where(half == e, vp, jnp.zeros_like(vp))
                return (lax.dot_general(qs, ke, NT_DIMS, preferred_element_type=F32),
                        lax.dot_general(dop, ve, NT_DIMS, preferred_element_type=F32), ke)

            steps = [(pp, e) for pp in range(4) for e in range(2)]
            ahead = [scores(*st) for st in steps[:AHEAD_BWD]]
            for n, (pp, e) in enumerate(steps):
                raw, dp, ke = ahead.pop(0)
                if n + AHEAD_BWD < len(steps):
                    ahead.append(scores(*steps[n + AHEAD_BWD]))
                h = 2 * pp + e
                cols = slice(pp * BLK, (pp + 1) * BLK)
                qs, kp, vp, dop = operands(pp)
                if e == 0:
                    prod = dop.astype(F32) * o_ref[:, cols].astype(F32)
                    d0 = jnp.sum(jnp.where(half == 0, prod, 0.0), axis=1, keepdims=True)
                    d1 = jnp.sum(prod, axis=1, keepdims=True) - d0
                    dq = jnp.zeros((TM, BLK), F32)
                    dks, dvs = [], []
                t = raw - ck_ref[h] - lse_ref[h]
                if masked:
                    t = jnp.where(valid, t, NEG)
                p = jnp.exp(t)
                ds = p * (dp - (d0 if e == 0 else d1))
                dck_s[h] += jnp.sum(ds, axis=0, keepdims=True)
                row_sums = jnp.where(lane == h, jnp.sum(ds, axis=1, keepdims=True), row_sums)
                ds_b = ds.astype(BF16)
                dq = dq + jnp.dot(ds_b, ke, preferred_element_type=F32)
                dks.append(lax.dot_general(ds_b, qs, TN_DIMS, preferred_element_type=F32))
                dvs.append(lax.dot_general(p.astype(BF16), dop, TN_DIMS, preferred_element_type=F32))
                if e == 1:
                    dq_ref[pl.ds(q0, TM), cols] += dq
                    dk_s[pp] += jnp.where(half == 0, dks[0], dks[1])
                    dv_s[pp] += jnp.where(half == 0, dvs[0], dvs[1])
            dcq_ref[pl.ds(q0, TM), :] += row_sums

        @pl.when((qi > kj) & (kj > 0))
        def _():
            tile(False)

        @pl.when((qi == kj) | (kj == 0))
        def _():
            tile(True)

        @pl.when(qi == NT - 1)
        def _():
            for pp in range(4):
                cols = slice(pp * BLK, (pp + 1) * BLK)
                dk_ref[:, cols] = dk_s[pp].astype(BF16)
                dv_ref[:, cols] = dv_s[pp].astype(BF16)
            dck_ref[...] = dck_s[...]

    qrow = lambda blk, w=512: pl.BlockSpec((TM, w), lambda n, kj, qi: (qi[n], blk))
    krow = lambda blk: pl.BlockSpec((TM, W2), lambda n, kj, qi: (kj[n], blk))
    grid_spec = pltpu.PrefetchScalarGridSpec(
        num_scalar_prefetch=2, grid=(n_pairs,),
        in_specs=[qrow(QB, W2), qrow(QB + 1, W2), krow(KB), krow(KB + 1), krow(VB), krow(VB + 1), qrow(0), qrow(1),
                  pl.BlockSpec((N_HEADS, TM, 1), lambda n, kj, qi: (0, qi[n], 0)),
                  pl.BlockSpec((N_HEADS, 1, TM), lambda n, kj, qi: (0, 0, kj[n]))] + [HBM_SPEC] * len(rider.operands),
        out_specs=[pl.BlockSpec((LP, 512), lambda n, kj, qi: (0, 0)),
                   pl.BlockSpec((TM, 512), lambda n, kj, qi: (kj[n], 0)),
                   pl.BlockSpec((TM, 512), lambda n, kj, qi: (kj[n], 0)),
                   pl.BlockSpec((N_HEADS, 1, TM), lambda n, kj, qi: (0, 0, kj[n])),
                   pl.BlockSpec((LP, BLK), lambda n, kj, qi: (0, 0))] + [HBM_SPEC] * len(rider.out_shapes),
        scratch_shapes=[pltpu.VMEM((4, TM, BLK), F32), pltpu.VMEM((4, TM, BLK), F32),
                        pltpu.VMEM((N_HEADS, 1, TM), F32)] + rider.scratch())
    dq, dk, dv, dck, dcq, *carried = pl.pallas_call(
        body, grid_spec=grid_spec,
        out_shape=[SDS((LP, 512), F32), SDS((LP, 512), BF16), SDS((LP, 512), BF16), SDS((N_HEADS, 1, LP), F32),
                   SDS((LP, BLK), F32)] + rider.out_shapes,
        compiler_params=_cparams(("arbitrary",)), name="fox_bwd",
    )(jnp.asarray([p[0] for p in pairs], jnp.int32), jnp.asarray([p[1] for p in pairs], jnp.int32),
      proj, proj, proj, proj, proj, proj, o_b, dmix, lse, ck_t, *rider.operands)
    return dq, dk, dv, dck, dcq, carried


N_SEG = 3
N_KEY = N_SEG * BLK
GROUP = 4
QW = GROUP * BLK


def _bucket_tables_t():
    return np.ascontiguousarray(_bucket_tables().transpose(0, 2, 1))


def _stack_heads(ref, g, scale):
    half = _lane_half(BLK)
    out = []
    for pair in range(2):
        x = ref[:, (2 * g + pair) * BLK:(2 * g + pair + 1) * BLK].astype(F32) * scale
        swapped = pltpu.roll(x, HALF, 1)
        for e in range(2):
            out.append(jnp.where(half == g, x if e == g else swapped, 0.0).astype(BF16))
    return jnp.concatenate(out, axis=0)


def _unstack_heads(x_t, g, ref, scale):
    for pair in range(2):
        both = jnp.concatenate([x_t[:, (2 * pair) * BLK:(2 * pair + 1) * BLK],
                                x_t[:, (2 * pair + 1) * BLK:(2 * pair + 2) * BLK]], axis=0)
        ref[:, (2 * g + pair) * BLK:(2 * g + pair + 1) * BLK] = (both.T * scale).astype(ref.dtype)


def _swa_tables(tab_ref, sink_ref, bkt_ref, tbl, sink_row):
    kk = lax.broadcasted_iota(jnp.int32, (BLK, BLK), 0)
    qq = lax.broadcasted_iota(jnp.int32, (BLK, BLK), 1)
    neg = jnp.full((BLK, BLK), NEG, F32)
    lane = lax.broadcasted_iota(jnp.int32, (1, QW), 1) // BLK
    for g in range(2):
        row = jnp.zeros((1, QW), F32)
        for hh in range(GROUP):
            h = GROUP * g + hh
            cols = slice(hh * BLK, (hh + 1) * BLK)
            row = jnp.where(lane == hh, sink_ref[0, h], row)

            def step(b, carry, h=h):
                t = tab_ref[b, h]
                return jnp.where(bkt_ref[0] == b, t, carry[0]), jnp.where(bkt_ref[1] == b, t, carry[1])
            zero = jnp.zeros((BLK, BLK), F32)
            cur, prev = lax.fori_loop(0, N_BUCKETS, step, (zero, zero))
            far = jnp.full((BLK, BLK), tab_ref[N_BUCKETS - 1, h], F32)
            causal = jnp.where(kk <= qq, cur, neg)
            segments = [
                (neg, neg, jnp.where(kk >= PAD_ROWS, causal, neg)),
                (jnp.where(kk >= PAD_ROWS, prev, neg), neg, causal),
                (jnp.where(kk >= PAD_ROWS, far, neg), jnp.where(kk > qq, prev, neg), causal)]
            for case in range(3):
                for seg in range(N_SEG):
                    tbl[case, g, seg * BLK:(seg + 1) * BLK, cols] = segments[case][seg]
        sink_row[g] = row


def _swa_prep(proj):
    rows = LP // 3

    def body(k_ref, v_ref, kt_ref, vt_ref):
        kt_ref[...] = k_ref[...].astype(F32).T.astype(BF16)
        vt_ref[...] = v_ref[...].astype(F32).T.astype(BF16)

    col = pl.BlockSpec((BLK, rows), lambda i: (0, i))
    return pl.pallas_call(
        body, grid=(3,),
        in_specs=[pl.BlockSpec((rows, BLK), lambda i: (i, KA)), pl.BlockSpec((rows, BLK), lambda i: (i, VA))],
        out_specs=[col, col], out_shape=[SDS((BLK, LP), BF16)] * 2,
        compiler_params=_cparams(("parallel",)), name="swa_prep")(proj, proj)


def _segments(ref, i, by_rows):
    starts = [0, pl.multiple_of(jnp.maximum(i - 1, 0) * BLK, BLK), pl.multiple_of(i * BLK, BLK)]
    if by_rows:
        return jnp.concatenate([ref[pl.ds(s, BLK), :] for s in starts], axis=0)
    return jnp.concatenate([ref[:, pl.ds(s, BLK)] for s in starts], axis=1)


def _swa_fwd(proj, vt_a, rel_bias, sinks, bkt_t, rider):
    def body(*refs):
        ((tab_ref, sink_ref, bkt_ref, q_ref, k_ref, vt_ref), (o_ref, lse_ref),
         (tbl, sink_row), mine) = rider.split(refs, 6, 2, 2)
        i = pl.program_id(0)
        rider.at_steps(mine, i == 0, i == NBLK // 2, i == NBLK - 1)

        @pl.when(i == 0)
        def _():
            _swa_tables(tab_ref, sink_ref, bkt_ref, tbl, sink_row)

        case = jnp.minimum(i, 2)
        k_cat = _segments(k_ref, i, True)
        vt_cat = _segments(vt_ref, i, False)
        raw = [lax.dot_general(k_cat, _stack_heads(q_ref, g, SCALE), NT_DIMS, preferred_element_type=F32)
               for g in range(2)]
        for g in range(2):
            s_t = raw[g] + tbl[case, g]
            sink = sink_row[g]
            m = jnp.maximum(_over_keys(jnp.max, s_t), sink)
            p_t = jnp.exp(s_t - m)
            l = _over_keys(jnp.sum, p_t) + jnp.exp(sink - m)
            o_t = jnp.dot(vt_cat[g * HALF:(g + 1) * HALF, :], p_t.astype(BF16), preferred_element_type=F32)
            _unstack_heads(o_t * (1.0 / l), g, o_ref, 1.0)
            lse = m + jnp.log(l)
            for hh in range(GROUP):
                lse_ref[GROUP * g + hh] = lse[:, hh * BLK:(hh + 1) * BLK]

    smem = pl.BlockSpec(memory_space=pltpu.SMEM)
    o_a, lse, *carried = pl.pallas_call(
        body, grid=(NBLK,),
        in_specs=[smem, smem, pl.BlockSpec((2, BLK, BLK), lambda i: (0, 0, 0)),
                  pl.BlockSpec((BLK, 512), lambda i: (i, QA)), pl.BlockSpec((LP, BLK), lambda i: (0, KA)),
                  pl.BlockSpec((BLK, LP), lambda i: (0, 0))] + [HBM_SPEC] * len(rider.operands),
        out_specs=[pl.BlockSpec((BLK, 512), lambda i: (i, 0)),
                   pl.BlockSpec((N_HEADS, 1, BLK), lambda i: (0, 0, i))] + [HBM_SPEC] * len(rider.out_shapes),
        out_shape=[SDS((LP, 512), BF16), SDS((N_HEADS, 1, LP), F32)] + rider.out_shapes,
        scratch_shapes=[pltpu.VMEM((3, 2, N_KEY, QW), F32), pltpu.VMEM((2, 1, QW), F32)] + rider.scratch(),
        compiler_params=_cparams(("arbitrary",)), name="swa_fwd",
    )(rel_bias, sinks, bkt_t, proj, proj, vt_a, *rider.operands)
    return o_a, lse, carried


def _swa_bwd(proj, kt_a, o_a, dmix, lse, rel_bias, sinks, bkt_t, rider):
    def body(*refs):
        ((tab_ref, sink_ref, bkt_ref, q_ref, k_ref, v_ref, kt_ref, o_ref, do_ref, lse_ref),
         (dq_ref, dk_ref, dv_ref, dbias_ref, dsink_ref), (tbl, sink_row, acc, dsk), mine) = rider.split(refs, 10, 5, 4)
        i = pl.program_id(0)
        rider.at_steps(mine, i == 0, i == NBLK // 2, i == NBLK - 1)

        @pl.when(i == 0)
        def _():
            _swa_tables(tab_ref, sink_ref, bkt_ref, tbl, sink_row)
            dk_ref[...] = jnp.zeros_like(dk_ref)
            dv_ref[...] = jnp.zeros_like(dv_ref)
            acc[...] = jnp.zeros_like(acc)
            dsk[...] = jnp.zeros_like(dsk)

        case = jnp.minimum(i, 2)
        first = jnp.full((BLK, QW), i, jnp.int32) == 1
        k_cat = _segments(k_ref, i, True)
        v_cat = _segments(v_ref, i, True)
        kt_cat = _segments(kt_ref, i, False)
        dk_cat = jnp.zeros((N_KEY, BLK), F32)
        dv_cat = jnp.zeros((N_KEY, BLK), F32)
        for g in range(2):
            d_parts = []
            for pair in range(2):
                cols = slice((2 * g + pair) * BLK, (2 * g + pair + 1) * BLK)
                prod_t = (do_ref[:, cols].astype(F32) * o_ref[:, cols].astype(F32)).T
                d_parts += [jnp.sum(prod_t[:HALF], axis=0, keepdims=True),
                            jnp.sum(prod_t[HALF:], axis=0, keepdims=True)]
            d_row = jnp.concatenate(d_parts, axis=1)
            lse_row = jnp.concatenate([lse_ref[GROUP * g + hh] for hh in range(GROUP)], axis=1)
            q_st = _stack_heads(q_ref, g, SCALE)
            do_st = _stack_heads(do_ref, g, 1.0)
            s_t = lax.dot_general(k_cat, q_st, NT_DIMS, preferred_element_type=F32) + tbl[case, g]
            p_t = jnp.exp(s_t - lse_row)
            dp_t = lax.dot_general(v_cat, do_st, NT_DIMS, preferred_element_type=F32)
            ds_t = p_t * (dp_t - d_row)
            dsk[g] += -jnp.exp(sink_row[g] - lse_row) * d_row
            acc[g, 0:BLK] += jnp.where(first, 0.0, ds_t[0:BLK])
            acc[g, BLK:2 * BLK] += jnp.where(first, ds_t[0:BLK], ds_t[BLK:2 * BLK])
            acc[g, 2 * BLK:N_KEY] += ds_t[2 * BLK:N_KEY]
            ds_b = ds_t.astype(BF16)
            dk_cat = dk_cat + jnp.dot(ds_b, q_st, preferred_element_type=F32)
            dv_cat = dv_cat + jnp.dot(p_t.astype(BF16), do_st, preferred_element_type=F32)
            dq_t = jnp.dot(kt_cat[g * HALF:(g + 1) * HALF, :], ds_b, preferred_element_type=F32)
            _unstack_heads(dq_t, g, dq_ref, SCALE)

        prev0 = pl.multiple_of(jnp.maximum(i - 1, 0) * BLK, BLK)
        cur0 = pl.multiple_of(i * BLK, BLK)
        for ref, cat in ((dk_ref, dk_cat), (dv_ref, dv_cat)):
            ref[0:BLK, :] += cat[0:BLK]
            ref[pl.ds(prev0, BLK), :] += cat[BLK:2 * BLK]
            ref[pl.ds(cur0, BLK), :] += cat[2 * BLK:N_KEY]

        @pl.when(i == NBLK - 1)
        def _():
            lane = lax.broadcasted_iota(jnp.int32, (1, BLK), 1)

            def per_bucket(b, carry):
                row = jnp.zeros((1, BLK), F32)
                for h in range(N_HEADS):
                    g, cols = h // GROUP, slice((h % GROUP) * BLK, (h % GROUP + 1) * BLK)
                    val = (jnp.sum(jnp.where(bkt_ref[0] == b, acc[g, 2 * BLK:N_KEY, cols], 0.0), keepdims=True)
                           + jnp.sum(jnp.where(bkt_ref[1] == b, acc[g, BLK:2 * BLK, cols], 0.0), keepdims=True))
                    row = jnp.where(lane == h, val, row)
                dbias_ref[pl.ds(b, 1), :] = row
                return carry

            lax.fori_loop(0, N_BUCKETS, per_bucket, 0)
            far = jnp.zeros((1, BLK), F32)
            dsr = jnp.zeros((1, BLK), F32)
            for h in range(N_HEADS):
                g, cols = h // GROUP, slice((h % GROUP) * BLK, (h % GROUP + 1) * BLK)
                far = jnp.where(lane == h, jnp.sum(acc[g, 0:BLK, cols], keepdims=True), far)
                dsr = jnp.where(lane == h, jnp.sum(dsk[g, :, cols], keepdims=True), dsr)
            dbias_ref[N_BUCKETS - 1:N_BUCKETS, :] += far
            dsink_ref[...] = dsr

    smem = pl.BlockSpec(memory_space=pltpu.SMEM)
    blk512 = lambda col: pl.BlockSpec((BLK, 512), lambda i: (i, col))
    full = lambda r, c: pl.BlockSpec((r, c), lambda i: (0, 0))
    dq, dk, dv, dbias, dsink, *carried = pl.pallas_call(
        body, grid=(NBLK,),
        in_specs=[smem, smem, pl.BlockSpec((2, BLK, BLK), lambda i: (0, 0, 0)), blk512(QA),
                  pl.BlockSpec((LP, BLK), lambda i: (0, KA)), pl.BlockSpec((LP, BLK), lambda i: (0, VA)),
                  full(BLK, LP), blk512(0), blk512(0), pl.BlockSpec((N_HEADS, 1, BLK), lambda i: (0, 0, i))]
        + [HBM_SPEC] * len(rider.operands),
        out_specs=[blk512(0), full(LP, BLK), full(LP, BLK), full(N_BUCKETS, BLK), full(1, BLK)]
        + [HBM_SPEC] * len(rider.out_shapes),
        out_shape=[SDS((LP, 512), BF16), SDS((LP, BLK), F32), SDS((LP, BLK), F32),
                   SDS((N_BUCKETS, BLK), F32), SDS((1, BLK), F32)] + rider.out_shapes,
        scratch_shapes=[pltpu.VMEM((3, 2, N_KEY, QW), F32), pltpu.VMEM((2, 1, QW), F32),
                        pltpu.VMEM((2, N_KEY, QW), F32), pltpu.VMEM((2, 1, QW), F32)] + rider.scratch(),
        compiler_params=_cparams(("arbitrary",)), name="swa_bwd",
    )(rel_bias, sinks, bkt_t, proj, proj, proj, kt_a, o_a, dmix, lse, *rider.operands)
    return dq, dk, dv, dbias, dsink, carried


def _local_step(x, tgt, meta, rel_bias, g_pre_mix, g_post_mix, g_pre_ffn, g_post_ffn, b_forget, sinks,
                w_in_b, out_rider, out_weight, ffn_rider, ffn_weights, early_grads, early_sums):
    bkt_t = jnp.asarray(_bucket_tables_t())
    b_p = jnp.pad(b_forget, ((0, 0), (0, BLK - N_HEADS)))

    h0, hn1, proj, f = _pre_mix(x, meta, g_pre_mix, w_in_b)
    kt_a, vt_a = _swa_prep(proj)
    o_a, lse_a, carried_out = _swa_fwd(proj, vt_a, rel_bias, sinks, bkt_t, out_rider)
    w_out_b = out_weight(carried_out)
    cum = _forget_cumsum(f, b_p)
    ck_t = cum[:, :N_HEADS].T.reshape(N_HEADS, 1, LP)
    q_aug, k_aug, v_t = _fox_prep(proj, cum)
    o_b, lse_row, carried = _fox_fwd(q_aug, k_aug, v_t, ffn_rider)
    lse_b = lse_row.reshape(N_HEADS, LP, 1)
    w_gu_b, w_dn_b = ffn_weights(carried)
    a, h1, hn2 = _attn_out(o_a, o_b, w_out_b, h0, g_post_mix, g_pre_ffn)
    g, u, act = _ffn_up(hn2, w_gu_b)
    dff, dy, loss_blk, dg_post_ffn = _ffn_down_loss(act, w_dn_b, h1, tgt, g_post_ffn)

    dw_dn = _mm_tn([act], dff, FF_T, "dw_down", BF16)
    dg, du = _ffn_down_bwd(dff, w_dn_b, g, u)
    dw_gu = _dw_gate_up(hn2, dg, du)
    dh1, da, dg_pre_ffn, dg_post_mix = _ffn_up_bwd(dg, du, w_gu_b, h1, a, dy, g_pre_ffn, g_post_mix)
    dw_out = _mm_tn([o_a, o_b], da, D_MODEL, "dw_out", BF16)
    dmix = _attn_out_bwd(da, w_out_b)
    dq_b, dk_b, dv_b, dck, dcq, landed = _fox_bwd(proj, o_b, dmix, lse_b, ck_t, early_grads(dw_gu, dw_dn, dw_out))
    dq_a, dk_a, dv_a, dbias, dsink, joined = _swa_bwd(proj, kt_a, o_a, dmix, lse_a, rel_bias, sinks, bkt_t,
                                                      early_sums(landed))
    dcum = dcq - jnp.pad(dck.reshape(N_HEADS, LP).T, ((0, 0), (0, BLK - N_HEADS)))
    df, db = _forget_cumsum_bwd(dcum, f, b_p)
    dproj, dx, dmeta, dg_pre_mix = _pre_mix_bwd(dq_a, dq_b, dk_b, dv_b, dk_a, dv_a, df, w_in_b, h0, dh1, g_pre_mix)
    dw_in = _mm_tn([hn1], dproj, D_MODEL, "dw_in", BF16)

    return dict(loss=loss_blk[0, 0], grad_x=dx, meta=dmeta,
                rel_bias=dbias[:, :N_HEADS], ln_pre_mix=dg_pre_mix, ln_post_mix=dg_post_mix,
                ln_pre_ffn=dg_pre_ffn, ln_post_ffn=dg_post_ffn, b_forget=db[:, :N_HEADS],
                sinks=dsink[:, :N_HEADS], w_in=dw_in, w_out=dw_out, w_gate_up=dw_gu, w_down=dw_dn,
                joined=joined)


N_SMALL = 24
LOSS_ROW = 6


def _place():
    x, y, c = lax.axis_index("x"), lax.axis_index("y"), lax.axis_index("c")
    return x, y, c, [(1 - x, y), (x, 1 - y), (1 - x, 1 - y)]


def _run_alone(rider, name):
    a, b = len(rider.operands), len(rider.out_shapes)

    def body(*refs):
        mine = (refs[:a], refs[a:a + b], refs[a + b:])
        rider.first(*mine)
        rider.middle(*mine)
        rider.last(*mine)

    return pl.pallas_call(body, in_specs=[HBM_SPEC] * a, out_specs=[HBM_SPEC] * b, out_shape=rider.out_shapes,
                          scratch_shapes=rider.scratch(), name=name)(*rider.operands)


def _gather_rider(shards, own_too, by_columns=()):
    n = len(shards)

    def slot(a, outs, chip, h):
        if a in by_columns:
            cols = shards[a].shape[2]
            return outs[a].at[h, :, pl.ds(pl.multiple_of(chip * cols, BLK), cols)]
        return outs[a].at[chip, h]

    def own_copies(ins, outs, sems):
        x, y, _, _ = _place()
        if not own_too:
            return []
        return [pltpu.make_async_copy(ins[a].at[h], slot(a, outs, 2 * x + y, h), sems[2].at[2 * a + h])
                for a in range(n) for h in range(2)]

    def copies(ins, outs, sems):
        send_sems, recv_sems = sems[:2]
        x, y, c, others = _place()
        chip = 2 * x + y
        sibling = (x, y, 1 - c)

        def rc(a, k, src, dst, to):
            return pltpu.make_async_remote_copy(src_ref=src, dst_ref=dst, send_sem=send_sems.at[6 * a + k],
                                                recv_sem=recv_sems.at[6 * a + k], device_id=to, device_id_type=MESH)

        pairs = [(a, j, ox, oy) for a in range(n) for j, (ox, oy) in enumerate(others)]
        there = lambda a, ox, oy, h: slot(a, outs, 2 * ox + oy, h)
        return dict(
            sent=lambda: [rc(a, j, ins[a].at[c], slot(a, outs, chip, c), (ox, oy, c)) for a, j, ox, oy in pairs],
            landed=lambda: [rc(a, j, there(a, ox, oy, c), there(a, ox, oy, c), sibling) for a, j, ox, oy in pairs],
            passed=lambda: [rc(a, 3 + j, there(a, ox, oy, c), there(a, ox, oy, c), sibling)
                            for a, j, ox, oy in pairs],
            arriving=lambda: [rc(a, 3 + j, there(a, ox, oy, 1 - c), there(a, ox, oy, 1 - c), sibling)
                              for a, j, ox, oy in pairs])

    def first(*mine):
        for cp in copies(*mine)["sent"]() + own_copies(*mine):
            cp.start()

    def middle(*mine):
        kinds = copies(*mine)
        for got, cp in zip(kinds["landed"](), kinds["passed"]()):
            got.wait_recv()
            cp.start()

    def last(*mine):
        kinds = copies(*mine)
        for cp in kinds["arriving"]():
            cp.wait_recv()
        for cp in kinds["sent"]() + kinds["passed"]():
            cp.wait_send()
        for cp in own_copies(*mine):
            cp.wait()

    shapes = [SDS((2, s.shape[1], 4 * s.shape[2]) if a in by_columns else (4,) + s.shape, s.dtype)
              for a, s in enumerate(shards)]
    return _Rider(shards, shapes, [6 * n, 6 * n] + [2 * n] * own_too, first, middle, last)


def _swap_rider(grads):
    n = len(grads)
    slabs = [(a, s) for a in range(n) for s in range(grads[a].shape[0])]

    def copies(ins, outs, sems):
        x, y, c, _ = _place()
        return [pltpu.make_async_remote_copy(
            src_ref=ins[a].at[s, 1 - c], dst_ref=outs[a].at[s], send_sem=sems[0].at[k], recv_sem=sems[1].at[k],
            device_id=(x, y, 1 - c), device_id_type=MESH) for k, (a, s) in enumerate(slabs)]

    def first(*mine):
        for cp in copies(*mine):
            cp.start()

    def middle(*mine):
        pass

    def last(*mine):
        for cp in copies(*mine):
            cp.wait()

    return _Rider(grads, [SDS(g.shape[:1] + g.shape[2:], g.dtype) for g in grads], [len(slabs), len(slabs)],
                  first, middle, last)


def _pair_sum(g, got, c_arr, name):
    n_s, rh, cc = got.shape

    def body(c_ref, g_ref, p_ref, o_ref):
        o_ref[0] = (g_ref[0, 0].astype(F32) + p_ref[0].astype(F32)).astype(BF16)

    grid_spec = pltpu.PrefetchScalarGridSpec(
        num_scalar_prefetch=1, grid=(n_s,),
        in_specs=[pl.BlockSpec((1, 1, rh, cc), lambda s, c_ref: (s, c_ref[0], 0, 0)),
                  pl.BlockSpec((1, rh, cc), lambda s, c_ref: (s, 0, 0))],
        out_specs=pl.BlockSpec((1, rh, cc), lambda s, c_ref: (s, 0, 0)))
    return pl.pallas_call(body, grid_spec=grid_spec, out_shape=SDS((n_s, rh, cc), BF16),
                          compiler_params=_cparams(("parallel",)), name=name)(c_arr, g, got)


def _direct_rider(grads):
    n = len(grads)

    def copies(ins, outs, sems):
        x, y, c, others = _place()
        peers = [(x, y, 1 - c)] + [(ox, oy, c) for ox, oy in others] + [(ox, oy, 1 - c) for ox, oy in others]
        return [pltpu.make_async_remote_copy(
            src_ref=ins[a].at[2 * px + py, pc], dst_ref=outs[a].at[k], send_sem=sems[0].at[7 * a + k],
            recv_sem=sems[1].at[7 * a + k], device_id=(px, py, pc), device_id_type=MESH)
            for a in range(n) for k, (px, py, pc) in enumerate(peers)]

    def first(*mine):
        for cp in copies(*mine):
            cp.start()

    def middle(*mine):
        pass

    def last(*mine):
        for cp in copies(*mine):
            cp.wait()

    return _Rider(grads, [SDS((7,) + g.shape[2:], g.dtype) for g in grads], [7 * n, 7 * n], first, middle, last)


def _owner_sum(grads, landed, own_arr, after, name):
    rh, cc = landed.shape[1:]
    tr = rh // 2

    def body(own_ref, g_ref, p_ref, after_ref, o_ref):
        total = g_ref[0, 0].astype(F32)
        for k in range(7):
            total = total + p_ref[k].astype(F32)
        o_ref[...] = total

    grid_spec = pltpu.PrefetchScalarGridSpec(
        num_scalar_prefetch=1, grid=(2,),
        in_specs=[pl.BlockSpec((1, 1, tr, cc), lambda i, own: (own[0], own[1], i, 0)),
                  pl.BlockSpec((7, tr, cc), lambda i, own: (0, i, 0)), pl.BlockSpec(memory_space=pl.ANY)],
        out_specs=pl.BlockSpec((tr, cc), lambda i, own: (i, 0)))
    return pl.pallas_call(body, grid_spec=grid_spec, out_shape=SDS((rh, cc), F32),
                          compiler_params=_cparams(("parallel",)), name=name)(own_arr, grads, landed, after)


SEM_SPEC = pl.BlockSpec(memory_space=pltpu.SEMAPHORE)
N_LATE = 10


def _late_copies(part_ref, landed_ref, small_ref, all_ref, send_sems, recv_sems):
    x, y, c, others = _place()
    me = 4 * x + 2 * y + c
    peers = [(x, y, 1 - c)] + [(ox, oy, c) for ox, oy in others] + [(ox, oy, 1 - c) for ox, oy in others]
    big = [pltpu.make_async_remote_copy(
        src_ref=part_ref.at[2 * ox + oy], dst_ref=landed_ref.at[j], send_sem=send_sems.at[j], recv_sem=recv_sems.at[j],
        device_id=(ox, oy, c), device_id_type=MESH) for j, (ox, oy) in enumerate(others)]
    small = [pltpu.make_async_remote_copy(
        src_ref=small_ref, dst_ref=all_ref.at[me], send_sem=send_sems.at[3 + k], recv_sem=recv_sems.at[3 + k],
        device_id=peer, device_id_type=MESH) for k, peer in enumerate(peers)]
    return big + small


def _late_exchange_start(part, small):
    def body(part_ref, landed_ref, small_ref, all_ref, send_sems, recv_sems, part_o, landed_o, small_o, all_o, token):
        for cp in _late_copies(part_ref, landed_ref, small_ref, all_ref, send_sems, recv_sems):
            cp.start()
        token[...] = jnp.zeros_like(token)

    hbm = lambda a: pltpu.HBM(a.shape, a.dtype)
    landed = lax.empty((3,) + part.shape[1:], part.dtype)
    everyone = lax.empty((8,) + small.shape, small.dtype)
    operands = [pltpu.with_memory_space_constraint(a, pltpu.HBM) for a in (part, landed, small, everyone)]
    return pl.pallas_call(
        body, name="late_exchange_start",
        out_shape=(pltpu.SemaphoreType.DMA((N_LATE,)), pltpu.SemaphoreType.DMA((N_LATE,)),
                   hbm(part), hbm(landed), hbm(small), hbm(everyone), SDS((8, BLK), F32)),
        in_specs=[HBM_SPEC] * 4,
        out_specs=(SEM_SPEC, SEM_SPEC, HBM_SPEC, HBM_SPEC, HBM_SPEC, HBM_SPEC, pl.BlockSpec(memory_space=pltpu.VMEM)),
        input_output_aliases={0: 2, 1: 3, 2: 4, 3: 5},
        compiler_params=pltpu.CompilerParams(has_side_effects=pltpu.SideEffectType.DATAFLOW_SIDE_EFFECTING),
    )(*operands)


def _late_exchange_wait(send_sems, recv_sems, part, landed, small, everyone, after):
    def body(part_ref, landed_ref, small_ref, all_ref, send_sems, recv_sems, after_ref, part_o, landed_o, small_o, all_o):
        for cp in _late_copies(part_ref, landed_ref, small_ref, all_ref, send_sems, recv_sems):
            cp.wait_send()
            cp.wait_recv()

    hbm = lambda a: pltpu.HBM(a.shape, a.dtype)
    out = pl.pallas_call(
        body, name="late_exchange_wait",
        out_shape=(hbm(part), hbm(landed), hbm(small), hbm(everyone)),
        in_specs=[HBM_SPEC] * 4 + [SEM_SPEC, SEM_SPEC, pl.BlockSpec(memory_space=pl.ANY)],
        out_specs=(HBM_SPEC,) * 4, input_output_aliases={0: 0, 1: 1, 2: 2, 3: 3},
        compiler_params=pltpu.CompilerParams(has_side_effects=pltpu.SideEffectType.DATAFLOW_SIDE_EFFECTING),
    )(part, landed, small, everyone, send_sems, recv_sems, after)
    return out[0], out[1], out[3]


def _chip_sum(parts, landed, chip_arr, name):
    rh, cc = landed.shape[1:]
    tr = rh // 2

    def body(chip_ref, own_ref, p_ref, o_ref):
        o_ref[...] = ((own_ref[0].astype(F32) + p_ref[0].astype(F32)) + p_ref[1].astype(F32)) + p_ref[2].astype(F32)

    grid_spec = pltpu.PrefetchScalarGridSpec(
        num_scalar_prefetch=1, grid=(2,),
        in_specs=[pl.BlockSpec((1, tr, cc), lambda i, chip_ref: (chip_ref[0], i, 0)),
                  pl.BlockSpec((3, tr, cc), lambda i, chip_ref: (0, i, 0))],
        out_specs=pl.BlockSpec((tr, cc), lambda i, chip_ref: (i, 0)))
    return pl.pallas_call(body, grid_spec=grid_spec, out_shape=SDS((rh, cc), F32),
                          compiler_params=_cparams(("parallel",)), name=name)(chip_arr, parts, landed)


def _device_sum(p):
    def body(p_ref, o_ref):
        acc = p_ref[0]
        for k in range(1, 8):
            acc = acc + p_ref[k]
        o_ref[...] = acc

    return pl.pallas_call(body, out_shape=SDS(p.shape[1:], F32), name="small_sum")(p)


def _join_halves(halves, name):
    n = len(halves)

    def body(*refs):
        ins, outs = refs[:n], refs[n:2 * n]
        send_sems, recv_sems = refs[2 * n:]
        x, y, c, _ = _place()
        copies = [pltpu.make_async_remote_copy(
            src_ref=ins[a], dst_ref=outs[a], send_sem=send_sems.at[a], recv_sem=recv_sems.at[a],
            device_id=(x, y, 1 - c), device_id_type=MESH) for a in range(n)]
        for cp in copies:
            cp.start()
        for cp in copies:
            cp.wait()

    return pl.pallas_call(
        body, in_specs=[HBM_SPEC] * n, out_specs=[HBM_SPEC] * n,
        out_shape=[SDS(h.shape, h.dtype) for h in halves],
        scratch_shapes=[pltpu.SemaphoreType.DMA((n,)), pltpu.SemaphoreType.DMA((n,))],
        name=name)(*halves)


def _join_rider(halves):
    n = len(halves)

    def copies(ins, outs, sems):
        x, y, c, _ = _place()
        return [pltpu.make_async_remote_copy(
            src_ref=ins[a], dst_ref=outs[a], send_sem=sems[0].at[a], recv_sem=sems[1].at[a],
            device_id=(x, y, 1 - c), device_id_type=MESH) for a in range(n)]

    def first(*mine):
        for cp in copies(*mine):
            cp.start()

    def middle(*mine):
        pass

    def last(*mine):
        for cp in copies(*mine):
            cp.wait()

    return _Rider(halves, [SDS(h.shape, h.dtype) for h in halves], [n, n], first, middle, last)


def _adamw(w, g, m, v, name, tr=None):
    rows, cols = w.shape
    tr = tr or rows
    assert rows % tr == 0

    def body(w_ref, g_ref, m_ref, v_ref, d_ref, nm_ref, nv_ref):
        gg = g_ref[...]
        nm = ADAM_B1 * m_ref[...] + (1.0 - ADAM_B1) * gg
        nv = ADAM_B2 * v_ref[...] + (1.0 - ADAM_B2) * (gg * gg)
        nm_ref[...] = nm
        nv_ref[...] = nv
        m_hat = nm / (1.0 - ADAM_B1 ** ADAM_STEP)
        v_hat = nv / (1.0 - ADAM_B2 ** ADAM_STEP)
        d_ref[...] = -ADAM_LR * (m_hat / (jnp.sqrt(v_hat) + ADAM_EPS) + ADAM_WD * w_ref[...])

    blk = pl.BlockSpec((tr, cols), lambda i: (i, 0))
    return pl.pallas_call(
        body, grid=(rows // tr,), in_specs=[blk] * 4, out_specs=[blk] * 3,
        out_shape=[SDS((rows, cols), F32)] * 3,
        compiler_params=_cparams(("parallel",)), name=name)(w, g, m, v)


def _adamw_halves(w, mine, theirs, m, v, c_arr, after, name):
    rows, cols = w.shape
    rh = rows // 2
    tr = rh if rh <= 352 else 256
    nh = rh // tr

    def body(c_ref, w_ref, mine_ref, theirs_ref, m_ref, v_ref, after_ref, g_ref, d_ref, nm_ref, nv_ref):
        own = jnp.full((tr, cols), pl.program_id(0), jnp.int32) == c_ref[0]
        gg = jnp.where(own, mine_ref[...], theirs_ref[...])
        g_ref[...] = gg
        nm = ADAM_B1 * m_ref[...] + (1.0 - ADAM_B1) * gg
        nv = ADAM_B2 * v_ref[...] + (1.0 - ADAM_B2) * (gg * gg)
        nm_ref[...] = nm
        nv_ref[...] = nv
        m_hat = nm / (1.0 - ADAM_B1 ** ADAM_STEP)
        v_hat = nv / (1.0 - ADAM_B2 ** ADAM_STEP)
        d_ref[...] = -ADAM_LR * (m_hat / (jnp.sqrt(v_hat) + ADAM_EPS) + ADAM_WD * w_ref[...])

    whole = pl.BlockSpec((tr, cols), lambda hh, i, c_ref: (hh * nh + i, 0))
    part = pl.BlockSpec((tr, cols), lambda hh, i, c_ref: (i, 0))
    grid_spec = pltpu.PrefetchScalarGridSpec(
        num_scalar_prefetch=1, grid=(2, nh),
        in_specs=[whole, part, part, whole, whole, pl.BlockSpec(memory_space=pl.ANY)], out_specs=[whole] * 4)
    return pl.pallas_call(body, grid_spec=grid_spec, out_shape=[SDS((rows, cols), F32)] * 4,
                          compiler_params=_cparams(("parallel", "parallel")),
                          name=name)(c_arr, w, mine, theirs, m, v, after)


def _pack_small(pre_mix, post_mix, pre_ffn, post_ffn, rel_bias, b_forget, sinks):
    def at(row, v):
        return jnp.pad(v, ((row, 7 - row), (0, D_MODEL - v.shape[1])))
    return (at(0, pre_mix) + at(1, post_mix) + at(2, pre_ffn) + at(3, post_ffn)
            + at(4, rel_bias.reshape(1, N_BUCKETS * N_HEADS)) + at(5, jnp.concatenate([b_forget, sinks], axis=1)))


def _unpack_small(p):
    return dict(ln_pre_mix=p[0:1], ln_post_mix=p[1:2], ln_pre_ffn=p[2:3], ln_post_ffn=p[3:4],
                rel_bias=p[4, :N_BUCKETS * N_HEADS].reshape(N_BUCKETS, N_HEADS),
                b_forget=p[5:6, 0:N_HEADS], sinks=p[5:6, N_HEADS:2 * N_HEADS])


WEIGHTS = ("meta_tokens", "rel_bias", "ln_pre_mix", "ln_post_mix", "ln_pre_ffn", "ln_post_ffn",
           "w_in", "b_forget", "sinks", "w_out", "w_gate_up", "w_down")


def kernel(x, meta_tokens, rel_bias, ln_pre_mix, ln_post_mix, ln_pre_ffn, ln_post_ffn, w_in, b_forget, sinks, w_out, w_gate_up, w_down, loss_target, m_meta_tokens, m_rel_bias, m_ln_pre_mix, m_ln_post_mix, m_ln_pre_ffn, m_ln_post_ffn, m_w_in, m_b_forget, m_sinks, m_w_out, m_w_gate_up, m_w_down, v_meta_tokens, v_rel_bias, v_ln_pre_mix, v_ln_post_mix, v_ln_pre_ffn, v_ln_post_ffn, v_w_in, v_b_forget, v_sinks, v_w_out, v_w_gate_up, v_w_down):
    xi, yi, ci = lax.axis_index("x"), lax.axis_index("y"), lax.axis_index("c")
    chip = 2 * xi + yi
    c_arr = jnp.reshape(ci, (1,)).astype(jnp.int32)

    def halves(w, dtype):
        return w.astype(dtype).reshape(2, w.shape[0] // 2, w.shape[1])

    def with_own(gathered, shards):
        return [lax.dynamic_update_slice(got, own[None], (chip, 0, 0, 0)) for got, own in zip(gathered, shards)]

    shards = [halves(jnp.transpose(w_in[0]), BF16), halves(meta_tokens, F32)]
    gw_in, g_meta = with_own(_run_alone(_gather_rider(shards, False), "gather_mixer_weights"), shards)
    out_shards = [halves(w_out[0], BF16)]
    ffn_shards = [halves(w_gate_up[0], BF16), halves(w_down[0], BF16)]

    def ffn_weights(carried):
        gw_gu, gw_dn = carried
        return gw_gu.reshape(D_MODEL, 2 * D_FF), gw_dn.reshape(D_FF, D_MODEL)

    early = {}

    def early_grads(dw_gu, dw_dn, dw_out):
        early["grads"] = [dw_out.reshape(4, 2, 128, D_MODEL), dw_gu.reshape(4, 2, 512, FF_T),
                          dw_dn.reshape(4, 2, 352, D_MODEL)]
        return _direct_rider(early["grads"])

    own_arr = jnp.stack([chip, ci]).astype(jnp.int32)

    def early_sums(landed):
        early["mine"] = [_owner_sum(g, l, own_arr, l, "owner_sum_%d" % a)
                         for a, (g, l) in enumerate(zip(early["grads"], landed))]
        return _join_rider(early["mine"])
    w_in_b = jnp.pad(gw_in.reshape(D_PROJ, D_MODEL), ((0, D_PROJ_P - D_PROJ), (0, 0)))
    meta_all = g_meta.reshape(4, N_META, D_MODEL // 4).transpose(1, 0, 2).reshape(N_META, D_MODEL)

    loc = _local_step(x[0], loss_target[0], meta_all, rel_bias, ln_pre_mix, ln_post_mix, ln_pre_ffn, ln_post_ffn,
                      b_forget, sinks, w_in_b, _gather_rider(out_shards, True),
                      lambda carried: carried[0].reshape(D_MODEL, D_MODEL),
                      _gather_rider(ffn_shards, True, by_columns=(0,)), ffn_weights, early_grads, early_sums)

    small = jnp.concatenate(
        [_pack_small(loc["ln_pre_mix"], loc["ln_post_mix"], loc["ln_pre_ffn"], loc["ln_post_ffn"],
                     loc["rel_bias"], loc["b_forget"], loc["sinks"])
         + jnp.pad(loc["loss"].reshape(1, 1), ((LOSS_ROW, 7 - LOSS_ROW), (0, D_MODEL - 1))), loc["meta"]], axis=0)

    dw_in = loc["w_in"].reshape(1, 2, D_MODEL // 2, D_PROJ_P)
    (got_in,) = _run_alone(_swap_rider([dw_in]), "swap_halves_late")
    half_sum = _pair_sum(dw_in, got_in, c_arr, "pair_sum_late")
    part_in = half_sum[0, :, :D_PROJ].reshape(D_MODEL // 2, 4, D_PROJ // 4).transpose(1, 0, 2)
    send_sems, recv_sems, part_sent, landing, small_sent, everyone, token = _late_exchange_start(part_in, small)
    chip_arr = jnp.reshape(chip, (1,)).astype(jnp.int32)
    grad, delta, new_m, new_v = {}, {}, {}, {}
    big = dict(w_out=(w_out, m_w_out, v_w_out), w_gate_up=(w_gate_up, m_w_gate_up, v_w_gate_up),
               w_down=(w_down, m_w_down, v_w_down))
    for name, g_mine, g_theirs in zip(("w_out", "w_gate_up", "w_down"), early["mine"], loc["joined"]):
        w, m, v = big[name]
        g, d, nm, nv = _adamw_halves(w[0], g_mine, g_theirs, m[0], v[0], c_arr, token, "adamw_" + name)
        grad[name], delta[name], new_m[name], new_v[name] = g[None], d[None], nm[None], nv[None]
    part_back, landed_in, small_all = _late_exchange_wait(send_sems, recv_sems, part_sent, landing, small_sent,
                                                         everyone, new_v["w_down"])
    mine_in = _chip_sum(part_back, landed_in, chip_arr, "chip_sum_in")
    (theirs_in,) = _join_halves([mine_in], "join_w_in")
    g_w_in = jnp.where(ci == 0, jnp.concatenate([mine_in, theirs_in], axis=0),
                       jnp.concatenate([theirs_in, mine_in], axis=0))
    view = lambda a: jnp.transpose(a).reshape(D_PROJ // 4 * 8, BLK)
    back = lambda a: jnp.transpose(a.reshape(D_PROJ // 4, D_MODEL))[None]
    d, nm, nv = _adamw(view(w_in[0]), view(g_w_in), view(m_w_in[0]), view(v_w_in[0]), "adamw_w_in",
                       tr=D_PROJ // 4 * 4)
    grad["w_in"], delta["w_in"], new_m["w_in"], new_v["w_in"] = g_w_in[None], back(d), back(nm), back(nv)
    me = 4 * xi + 2 * yi + ci
    small_sum = _device_sum(lax.dynamic_update_slice(small_all, small[None], (me, 0, 0)))
    g_meta_tokens = lax.dynamic_slice(small_sum[8:N_SMALL], (0, chip * (D_MODEL // 4)), (N_META, D_MODEL // 4))
    g_small = small_sum[0:8]
    grad.update(_unpack_small(g_small))
    grad.update(meta_tokens=g_meta_tokens)
    delta["meta_tokens"], new_m["meta_tokens"], new_v["meta_tokens"] = _adamw(
        meta_tokens, g_meta_tokens, m_meta_tokens, v_meta_tokens, "adamw_meta")
    d, nm, nv = _adamw(
        _pack_small(ln_pre_mix, ln_post_mix, ln_pre_ffn, ln_post_ffn, rel_bias, b_forget, sinks), g_small,
        _pack_small(m_ln_pre_mix, m_ln_post_mix, m_ln_pre_ffn, m_ln_post_ffn, m_rel_bias, m_b_forget, m_sinks),
        _pack_small(v_ln_pre_mix, v_ln_post_mix, v_ln_pre_ffn, v_ln_post_ffn, v_rel_bias, v_b_forget, v_sinks),
        "adamw_small")
    delta.update(_unpack_small(d))
    new_m.update(_unpack_small(nm))
    new_v.update(_unpack_small(nv))

    loss = small_sum[LOSS_ROW, 0]
    return (loss,loc["grad_x"][None], *[grad[k] for k in WEIGHTS], *[delta[k] for k in WEIGHTS],
            *[new_m[k] for k in WEIGHTS], *[new_v[k] for k in WEIGHTS])
```

```python
import math

import numpy as np
import jax
import jax.numpy as jnp
from jax import lax
from jax.experimental import pallas as pl
from jax.experimental.pallas import tpu as pltpu

F32 = jnp.float32
BF16 = jnp.bfloat16
MESH = pl.DeviceIdType.MESH
SDS = jax.ShapeDtypeStruct

D_MODEL = 1024
SEQ = 4096
N_META = 16
N_HEADS = 8
HALF = 64
D_FF = 2816
N_BUCKETS = 32
EPS = 1e-6
NEG = -1e30
SCALE = 0.125
LOG2E = 1.4426950408889634
LN2 = 0.6931471805599453
PAD_ROWS = 112
ROW0 = PAD_ROWS + N_META
LP = ROW0 + SEQ
BLK = 128
NBLK = LP // BLK
TM = 384
NT = LP // TM
TM_PURE = LP // 2
TM_MID = LP // 4
TM_EPI = LP // 6
TN = 256
TK_W = LP // 2
D_PROJ = 2312
D_PROJ_P = 2432
D_QKV = 2304
FF_T = 1408
VMEM_LIMIT = 56 * 1024 * 1024

ADAM_LR = 0.001
ADAM_B1 = 0.9
ADAM_B2 = 0.999
ADAM_EPS = 1e-08
ADAM_WD = 0.01
ADAM_STEP = 10

QA = 0
KA, VA = 4, 5
QB, KB, VB = 3, 5, 7
W2 = 256

NT_DIMS = (((1,), (1,)), ((), ()))
TN_DIMS = (((0,), (0,)), ((), ()))


def _cparams(sem):
    return pltpu.CompilerParams(dimension_semantics=sem, vmem_limit_bytes=VMEM_LIMIT)


def _t5_bucket_np(d):
    n = np.maximum(d, 0).astype(np.int32)
    nf = np.maximum(n, 1).astype(np.float32)
    large = 16 + (np.log(nf / np.float32(16)) / np.float32(math.log(8.0)) * np.float32(16)).astype(np.int32)
    large = np.minimum(large, N_BUCKETS - 1)
    return np.where(n < 16, n, large).astype(np.int32)


def _bucket_tables():
    qi = np.arange(BLK)[:, None]
    ki = np.arange(BLK)[None, :]
    return np.stack([_t5_bucket_np(qi - ki), _t5_bucket_np(qi - ki + BLK)])


def _rms(x):
    return lax.rsqrt(jnp.mean(x * x, axis=-1, keepdims=True) + EPS)


def _rms_bwd(n, r, gdy):
    return r * (gdy - n * jnp.mean(n * gdy, axis=-1, keepdims=True))


def _pre_mix(x, meta, gain, w_in_b):
    half = D_QKV // 2
    n_steps = LP // TM_MID

    def body(x_ref, meta_ref, g_ref, w_ref, h0_ref, hn_ref, proj_ref, f_ref, tile, sems):
        i = pl.program_id(0)

        def tile_copy(s):
            skip = ROW0 if s == 0 else 0
            return pltpu.make_async_copy(x_ref.at[s * TM_MID + skip - ROW0:(s + 1) * TM_MID - ROW0],
                                         tile.at[s % 2, skip:TM_MID], sems.at[s % 2])

        @pl.when(i == 0)
        def _():
            tile_copy(0).start()
            tile[0, :PAD_ROWS] = jnp.zeros((PAD_ROWS, D_MODEL), F32)
            tile[0, PAD_ROWS:ROW0] = meta_ref[...]

        for s in range(n_steps):
            @pl.when(i == s)
            def _():
                if s + 1 < n_steps:
                    tile_copy(s + 1).start()
                tile_copy(s).wait()

        x = tile[i % 2]
        h0_ref[...] = x
        hn = (x * _rms(x) * g_ref[...]).astype(BF16)
        hn_ref[...] = hn
        proj_ref[:, :half] = lax.dot_general(hn, w_ref[:half, :], NT_DIMS, preferred_element_type=F32).astype(BF16)
        p = lax.dot_general(hn, w_ref[half:, :], NT_DIMS, preferred_element_type=F32)
        proj_ref[:, half:] = p[:, :half].astype(BF16)
        f_ref[...] = p[:, half:]

    return pl.pallas_call(
        body, grid=(n_steps,),
        in_specs=[HBM_SPEC,
                  pl.BlockSpec((N_META, D_MODEL), lambda i: (0, 0)),
                  pl.BlockSpec((1, D_MODEL), lambda i: (0, 0)),
                  pl.BlockSpec((D_PROJ_P, D_MODEL), lambda i: (0, 0))],
        out_specs=[pl.BlockSpec((TM_MID, D_MODEL), lambda i: (i, 0)),
                   pl.BlockSpec((TM_MID, D_MODEL), lambda i: (i, 0)),
                   pl.BlockSpec((TM_MID, D_QKV), lambda i: (i, 0)),
                   pl.BlockSpec((TM_MID, BLK), lambda i: (i, 0))],
        out_shape=[SDS((LP, D_MODEL), F32), SDS((LP, D_MODEL), BF16), SDS((LP, D_QKV), BF16), SDS((LP, BLK), F32)],
        scratch_shapes=[pltpu.VMEM((2, TM_MID, D_MODEL), F32), pltpu.SemaphoreType.DMA((2,))],
        compiler_params=_cparams(("arbitrary",)), name="pre_mix")(x, meta, gain, w_in_b)


def _attn_out(o_a, o_b, w_out_b, h0, g_post, g_pre_ffn):
    def body(oa_ref, ob_ref, w_ref, h0_ref, gp_ref, gf_ref, a_ref, h1_ref, hn2_ref):
        a = (jnp.dot(oa_ref[...], w_ref[0:512, :], preferred_element_type=F32)
             + jnp.dot(ob_ref[...], w_ref[512:1024, :], preferred_element_type=F32))
        a_ref[...] = a
        h1 = h0_ref[...] + a * _rms(a) * gp_ref[...]
        h1_ref[...] = h1
        hn2_ref[...] = (h1 * _rms(h1) * gf_ref[...]).astype(BF16)

    row = lambda w: pl.BlockSpec((TM_EPI, w), lambda i: (i, 0))
    vec = pl.BlockSpec((1, D_MODEL), lambda i: (0, 0))
    return pl.pallas_call(
        body, grid=(LP // TM_EPI,),
        in_specs=[row(512), row(512), pl.BlockSpec((D_MODEL, D_MODEL), lambda i: (0, 0)), row(D_MODEL), vec, vec],
        out_specs=[row(D_MODEL), row(D_MODEL), row(D_MODEL)],
        out_shape=[SDS((LP, D_MODEL), F32), SDS((LP, D_MODEL), F32), SDS((LP, D_MODEL), BF16)],
        compiler_params=_cparams(("parallel",)), name="attn_out")(o_a, o_b, w_out_b, h0, g_post, g_pre_ffn)


def _ffn_up(hn2, w_gu_b):
    def body(x_ref, wg_ref, wu_ref, g_ref, u_ref, act_ref):
        x = x_ref[...]
        g = jnp.dot(x, wg_ref[...], preferred_element_type=F32)
        u = jnp.dot(x, wu_ref[...], preferred_element_type=F32)
        g_ref[...] = g.astype(BF16)
        u_ref[...] = u.astype(BF16)
        act_ref[...] = (g * (1.0 / (1.0 + jnp.exp(-g))) * u).astype(BF16)

    out = pl.BlockSpec((LP, TN), lambda j: (0, j))
    return pl.pallas_call(
        body, grid=(D_FF // TN,),
        in_specs=[pl.BlockSpec((LP, D_MODEL), lambda j: (0, 0)),
                  pl.BlockSpec((D_MODEL, TN), lambda j: (0, j)),
                  pl.BlockSpec((D_MODEL, TN), lambda j: (0, j + D_FF // TN))],
        out_specs=[out, out, out],
        out_shape=[SDS((LP, D_FF), BF16)] * 3,
        compiler_params=_cparams(("parallel",)), name="ffn_up")(hn2, w_gu_b, w_gu_b)


def _ffn_down_loss(act, w_dn_b, h1, tgt, g_post_ffn):
    def body(act_ref, w_ref, h1_ref, t0_ref, t1_ref, t2_ref, g_ref, dff_ref, dy_ref, loss_ref, dg_ref, tiles, sems):
        i = pl.program_id(0)
        target = jnp.concatenate([t0_ref[...], t1_ref[...], t2_ref[...]], axis=0)

        def act_copy(s):
            return pltpu.make_async_copy(act_ref.at[pl.ds(s * TM, TM)], tiles.at[s % 3], sems.at[s % 3])

        @pl.when(i == 0)
        def _():
            loss_ref[...] = jnp.zeros_like(loss_ref)
            dg_ref[...] = jnp.zeros_like(dg_ref)
            act_copy(0).start()
            act_copy(1).start()

        @pl.when(i + 2 < NT)
        def _():
            act_copy(i + 2).start()
        act_copy(i).wait()
        ff = jnp.dot(tiles[i % 3], w_ref[...], preferred_element_type=F32)
        r = _rms(ff)
        n = ff * r
        g = g_ref[...]
        y = h1_ref[...] + n * g
        rows = i * TM + lax.broadcasted_iota(jnp.int32, (TM, D_MODEL), 0)
        diff = jnp.where(rows >= ROW0, y - target, 0.0)
        loss_ref[...] += 0.5 * jnp.sum(diff * diff) / D_MODEL
        dy = diff / D_MODEL
        dy_ref[...] = dy
        dg_ref[...] += jnp.sum(dy * n, axis=0, keepdims=True)
        dff_ref[...] = _rms_bwd(n, r, g * dy).astype(BF16)

    row = pl.BlockSpec((TM, D_MODEL), lambda i: (i, 0))
    tblk = lambda j: pl.BlockSpec((BLK, D_MODEL), lambda i: (jnp.maximum(3 * i - 1 + j, 0), 0))
    return pl.pallas_call(
        body, grid=(NT,),
        in_specs=[HBM_SPEC, pl.BlockSpec((D_FF, D_MODEL), lambda i: (0, 0)),
                  row, tblk(0), tblk(1), tblk(2), pl.BlockSpec((1, D_MODEL), lambda i: (0, 0))],
        out_specs=[row, row, pl.BlockSpec((8, BLK), lambda i: (0, 0)), pl.BlockSpec((1, D_MODEL), lambda i: (0, 0))],
        out_shape=[SDS((LP, D_MODEL), BF16), SDS((LP, D_MODEL), F32), SDS((8, BLK), F32), SDS((1, D_MODEL), F32)],
        scratch_shapes=[pltpu.VMEM((3, TM, D_FF), BF16), pltpu.SemaphoreType.DMA((3,))],
        compiler_params=_cparams(("arbitrary",)), name="ffn_down_loss")(act, w_dn_b, h1, tgt, tgt, tgt, g_post_ffn)


def _ffn_down_bwd(dff, w_dn_b, g, u):
    def body(d_ref, w_ref, g_ref, u_ref, dg_ref, du_ref):
        dact = lax.dot_general(d_ref[...], w_ref[...], NT_DIMS, preferred_element_type=F32)
        gg = g_ref[...].astype(F32)
        sig = 1.0 / (1.0 + jnp.exp(-gg))
        dg_ref[...] = (dact * u_ref[...].astype(F32) * sig * (1.0 + gg * (1.0 - sig))).astype(BF16)
        du_ref[...] = (dact * gg * sig).astype(BF16)

    blk = pl.BlockSpec((LP, TN), lambda j: (0, j))
    return pl.pallas_call(
        body, grid=(D_FF // TN,),
        in_specs=[pl.BlockSpec((LP, D_MODEL), lambda j: (0, 0)),
                  pl.BlockSpec((TN, D_MODEL), lambda j: (j, 0)), blk, blk],
        out_specs=[blk, blk],
        out_shape=[SDS((LP, D_FF), BF16)] * 2,
        compiler_params=_cparams(("parallel",)), name="ffn_down_bwd")(dff, w_dn_b, g, u)


def _ffn_up_bwd(dg, du, w_gu_b, h1, a, dy, g_pre_ffn, g_post_mix):
    def body(dg_ref, du_ref, w_ref, h1_ref, a_ref, dy_ref, gf_ref, gp_ref,
             dh1_ref, da_ref, dgf_ref, dgp_ref, acc):
        i = pl.program_id(0)
        s = pl.program_id(1)

        @pl.when((i == 0) & (s == 0))
        def _():
            dgf_ref[...] = jnp.zeros_like(dgf_ref)
            dgp_ref[...] = jnp.zeros_like(dgp_ref)

        @pl.when(s == 0)
        def _():
            acc[...] = jnp.zeros_like(acc)

        @pl.when(s < 2)
        def _():
            acc[...] += lax.dot_general(dg_ref[...], w_ref[...], NT_DIMS, preferred_element_type=F32)

        @pl.when(s >= 2)
        def _():
            acc[...] += lax.dot_general(du_ref[...], w_ref[...], NT_DIMS, preferred_element_type=F32)

        @pl.when(s == 3)
        def _():
            dhn2 = acc[...]
            h1 = h1_ref[...]
            r2 = _rms(h1)
            n2 = h1 * r2
            dgf_ref[...] += jnp.sum(dhn2 * n2, axis=0, keepdims=True)
            dh1 = dy_ref[...] + _rms_bwd(n2, r2, gf_ref[...] * dhn2)
            dh1_ref[...] = dh1
            av = a_ref[...]
            ra = _rms(av)
            na = av * ra
            dgp_ref[...] += jnp.sum(dh1 * na, axis=0, keepdims=True)
            da_ref[...] = _rms_bwd(na, ra, gp_ref[...] * dh1).astype(BF16)

    row = pl.BlockSpec((TM_EPI, D_MODEL), lambda i, s: (i, 0))
    vec = pl.BlockSpec((1, D_MODEL), lambda i, s: (0, 0))
    return pl.pallas_call(
        body, grid=(LP // TM_EPI, 4),
        in_specs=[pl.BlockSpec((TM_EPI, FF_T), lambda i, s: (i, jnp.minimum(s, 1))),
                  pl.BlockSpec((TM_EPI, FF_T), lambda i, s: (i, jnp.maximum(s - 2, 0))),
                  pl.BlockSpec((D_MODEL, FF_T), lambda i, s: (0, s)),
                  row, row, row, vec, vec],
        out_specs=[row, row, vec, vec],
        out_shape=[SDS((LP, D_MODEL), F32), SDS((LP, D_MODEL), BF16), SDS((1, D_MODEL), F32), SDS((1, D_MODEL), F32)],
        scratch_shapes=[pltpu.VMEM((TM_EPI, D_MODEL), F32)],
        compiler_params=_cparams(("arbitrary", "arbitrary")), name="ffn_up_bwd",
    )(dg, du, w_gu_b, h1, a, dy, g_pre_ffn, g_post_mix)


def _attn_out_bwd(da, w_out_b):
    def body(d_ref, w_ref, o_ref):
        o_ref[...] = lax.dot_general(d_ref[...], w_ref[...], NT_DIMS, preferred_element_type=F32).astype(BF16)

    row = pl.BlockSpec((TM_PURE, D_MODEL), lambda i: (i, 0))
    return pl.pallas_call(
        body, grid=(LP // TM_PURE,),
        in_specs=[row, pl.BlockSpec((D_MODEL, D_MODEL), lambda i: (0, 0))],
        out_specs=row, out_shape=SDS((LP, D_MODEL), BF16),
        compiler_params=_cparams(("parallel",)), name="attn_out_bwd")(da, w_out_b)


def _pre_mix_bwd(dq_a, dq_b, dk_b, dv_b, dk_a, dv_a, df, w_in_b, h0, dh1, g_pre_mix):
    n_steps = LP // TM_EPI

    def body(qa_ref, qb_ref, kb_ref, vb_ref, ka_ref, va_ref, f_ref, w_ref, h0_ref, dh1_ref, g_ref,
             dproj_ref, dx_ref, dmeta_ref, dg_ref, tile, sems):
        i = pl.program_id(0)

        def tile_copy(s):
            skip = ROW0 if s == 0 else 0
            return pltpu.make_async_copy(tile.at[s % 2, skip:TM_EPI],
                                         dx_ref.at[s * TM_EPI + skip - ROW0:(s + 1) * TM_EPI - ROW0],
                                         sems.at[s % 2])

        @pl.when(i == 0)
        def _():
            dg_ref[...] = jnp.zeros_like(dg_ref)

        for s in range(2, n_steps):
            @pl.when(i == s)
            def _():
                tile_copy(s - 2).wait()

        dproj = jnp.concatenate(
            [qa_ref[...], ka_ref[...].astype(BF16), va_ref[...].astype(BF16), (qb_ref[...] * SCALE).astype(BF16),
             kb_ref[...], vb_ref[...], f_ref[...].astype(BF16)], axis=1)
        dproj_ref[...] = dproj
        dhn = jnp.dot(dproj, w_ref[...], preferred_element_type=F32)
        x = h0_ref[...]
        r = _rms(x)
        n = x * r
        dg_ref[...] += jnp.sum(dhn * n, axis=0, keepdims=True)
        tile[i % 2] = dh1_ref[...] + _rms_bwd(n, r, g_ref[...] * dhn)

        for s in range(n_steps):
            @pl.when(i == s)
            def _():
                tile_copy(s).start()
                if s == 0:
                    dmeta_ref[...] = tile[0, PAD_ROWS:ROW0]
                if s == n_steps - 1:
                    tile_copy(s - 1).wait()
                    tile_copy(s).wait()

    row = lambda w: pl.BlockSpec((TM_EPI, w), lambda i: (i, 0))
    vec = pl.BlockSpec((1, D_MODEL), lambda i: (0, 0))
    return pl.pallas_call(
        body, grid=(n_steps,),
        in_specs=[row(512), row(512), row(512), row(512), row(BLK), row(BLK), row(BLK),
                  pl.BlockSpec((D_PROJ_P, D_MODEL), lambda i: (0, 0)), row(D_MODEL), row(D_MODEL), vec],
        out_specs=[row(D_PROJ_P), HBM_SPEC, pl.BlockSpec((N_META, D_MODEL), lambda i: (0, 0)), vec],
        out_shape=[SDS((LP, D_PROJ_P), BF16), SDS((SEQ, D_MODEL), F32), SDS((N_META, D_MODEL), F32),
                   SDS((1, D_MODEL), F32)],
        scratch_shapes=[pltpu.VMEM((2, TM_EPI, D_MODEL), F32), pltpu.SemaphoreType.DMA((2,))],
        compiler_params=_cparams(("arbitrary",)), name="pre_mix_bwd",
    )(dq_a, dq_b, dk_b, dv_b, dk_a, dv_a, df, w_in_b, h0, dh1, g_pre_mix)


def _mm_tn(parts, b, tm, name, out_dtype=F32):
    widths = [p.shape[1] for p in parts]
    m_total = sum(widths)
    n = b.shape[1]
    whole = len(parts) > 1
    n_k = LP // TK_W
    assert (tm == m_total) if whole else (m_total % tm == 0)

    def body(*refs):
        a_refs, b_ref, o_ref, acc = refs[:-3], refs[-3], refs[-2], refs[-1]
        k = pl.program_id(1)

        @pl.when(k == 0)
        def _():
            acc[...] = jnp.zeros_like(acc)
        a = a_refs[0][...] if not whole else jnp.concatenate([r[...] for r in a_refs], axis=1)
        acc[...] += lax.dot_general(a, b_ref[...], TN_DIMS, preferred_element_type=F32)

        @pl.when(k == n_k - 1)
        def _():
            o_ref[...] = acc[...].astype(out_dtype)

    a_specs = ([pl.BlockSpec((TK_W, w), lambda mi, k: (k, 0)) for w in widths] if whole
               else [pl.BlockSpec((TK_W, tm), lambda mi, k: (k, mi))])
    return pl.pallas_call(
        body, grid=(m_total // tm, n_k),
        in_specs=a_specs + [pl.BlockSpec((TK_W, n), lambda mi, k: (k, 0))],
        out_specs=pl.BlockSpec((tm, n), lambda mi, k: (mi, 0)),
        out_shape=SDS((m_total, n), out_dtype),
        scratch_shapes=[pltpu.VMEM((tm, n), F32)],
        compiler_params=_cparams(("parallel", "arbitrary")), name=name)(*parts, b)


def _dw_gate_up(hn2, dg, du):
    n_k = LP // TK_W

    def body(a_ref, dg_ref, du_ref, o_ref, acc):
        s = pl.program_id(0)
        k = pl.program_id(1)

        @pl.when(k == 0)
        def _():
            acc[...] = jnp.zeros_like(acc)

        @pl.when(s < 2)
        def _():
            acc[...] += lax.dot_general(a_ref[...], dg_ref[...], TN_DIMS, preferred_element_type=F32)

        @pl.when(s >= 2)
        def _():
            acc[...] += lax.dot_general(a_ref[...], du_ref[...], TN_DIMS, preferred_element_type=F32)

        @pl.when(k == n_k - 1)
        def _():
            o_ref[0] = acc[...].astype(BF16)

    return pl.pallas_call(
        body, grid=(4, n_k),
        in_specs=[pl.BlockSpec((TK_W, D_MODEL), lambda s, k: (k, 0)),
                  pl.BlockSpec((TK_W, FF_T), lambda s, k: (k, jnp.minimum(s, 1))),
                  pl.BlockSpec((TK_W, FF_T), lambda s, k: (k, jnp.maximum(s - 2, 0)))],
        out_specs=pl.BlockSpec((1, D_MODEL, FF_T), lambda s, k: (s, 0, 0)),
        out_shape=SDS((4, D_MODEL, FF_T), BF16),
        scratch_shapes=[pltpu.VMEM((D_MODEL, FF_T), F32)],
        compiler_params=_cparams(("parallel", "arbitrary")), name="dw_gate_up")(hn2, dg, du)


def _split3(x):
    hi = x.astype(BF16)
    r1 = x - hi.astype(F32)
    mid = r1.astype(BF16)
    lo = (r1 - mid.astype(F32)).astype(BF16)
    return hi, mid, lo


def _tri_matmul(tri, x):
    hi, mid, lo = _split3(x)
    dot = lambda t: jnp.dot(tri, t, preferred_element_type=F32)
    return dot(hi) + dot(mid) + dot(lo)


def _forget_cumsum(f, b_forget_p):
    def body(f_ref, b_ref, cum_ref, carry):
        i = pl.program_id(0)

        @pl.when(i == 0)
        def _():
            carry[...] = jnp.zeros_like(carry)

        z = f_ref[...] + b_ref[...]
        ls = jnp.minimum(z, 0.0) - jnp.log(1.0 + jnp.exp(-jnp.abs(z)))
        rows = i * TM + lax.broadcasted_iota(jnp.int32, (TM, BLK), 0)
        ls = jnp.where(rows >= PAD_ROWS, ls, 0.0)
        r = lax.broadcasted_iota(jnp.int32, (TM, TM), 0)
        c = lax.broadcasted_iota(jnp.int32, (TM, TM), 1)
        tri = (c <= r).astype(BF16)
        cum = _tri_matmul(tri, ls) + carry[...]
        cum_ref[...] = cum
        carry[...] = cum[TM - 1:TM, :]

    return pl.pallas_call(
        body, grid=(NT,),
        in_specs=[pl.BlockSpec((TM, BLK), lambda i: (i, 0)), pl.BlockSpec((1, BLK), lambda i: (0, 0))],
        out_specs=pl.BlockSpec((TM, BLK), lambda i: (i, 0)),
        out_shape=SDS((LP, BLK), F32),
        scratch_shapes=[pltpu.VMEM((1, BLK), F32)],
        compiler_params=_cparams(("arbitrary",)), name="forget_cumsum")(f, b_forget_p)


def _forget_cumsum_bwd(dcum, f, b_forget_p):
    def body(d_ref, f_ref, b_ref, df_ref, db_ref, carry):
        i = pl.program_id(0)

        @pl.when(i == 0)
        def _():
            carry[...] = jnp.zeros_like(carry)
            db_ref[...] = jnp.zeros_like(db_ref)

        blk = NT - 1 - i
        r = lax.broadcasted_iota(jnp.int32, (TM, TM), 0)
        c = lax.broadcasted_iota(jnp.int32, (TM, TM), 1)
        tri = (c >= r).astype(BF16)
        d = d_ref[...]
        dls = _tri_matmul(tri, d) + carry[...]
        carry[...] = dls[0:1, :]
        z = f_ref[...] + b_ref[...]
        rows = blk * TM + lax.broadcasted_iota(jnp.int32, (TM, BLK), 0)
        df = jnp.where(rows >= PAD_ROWS, dls / (1.0 + jnp.exp(z)), 0.0)
        df_ref[...] = df
        db_ref[...] += jnp.sum(df, axis=0, keepdims=True)

    rev = pl.BlockSpec((TM, BLK), lambda i: (NT - 1 - i, 0))
    vec = pl.BlockSpec((1, BLK), lambda i: (0, 0))
    return pl.pallas_call(
        body, grid=(NT,),
        in_specs=[rev, rev, vec],
        out_specs=[rev, vec],
        out_shape=[SDS((LP, BLK), F32), SDS((1, BLK), F32)],
        scratch_shapes=[pltpu.VMEM((1, BLK), F32)],
        compiler_params=_cparams(("arbitrary",)), name="forget_cumsum_bwd")(dcum, f, b_forget_p)


def _lane_half(rows):
    return lax.broadcasted_iota(jnp.int32, (rows, BLK), 1) // HALF


def _fox_valid(qi, kj):
    qrow = qi * TM + lax.broadcasted_iota(jnp.int32, (TM, TM), 0)
    krow = kj * TM + lax.broadcasted_iota(jnp.int32, (TM, TM), 1)
    return (krow <= qrow) & ((krow >= PAD_ROWS) | (qrow < PAD_ROWS))


class _Rider:
    def __init__(self, operands, out_shapes, sem_counts, first, middle, last):
        self.operands, self.out_shapes, self.sem_counts = list(operands), list(out_shapes), list(sem_counts)
        self.first, self.middle, self.last = first, middle, last

    def scratch(self):
        return [pltpu.SemaphoreType.DMA((k,)) for k in self.sem_counts]

    def split(self, refs, n_in, n_out, n_scratch):
        a, b = len(self.operands), len(self.out_shapes)
        ins, mine_in = refs[:n_in], refs[n_in:n_in + a]
        outs, mine_out = refs[n_in + a:n_in + a + n_out], refs[n_in + a + n_out:n_in + a + n_out + b]
        rest = refs[n_in + a + n_out + b:]
        return ins, outs, rest[:n_scratch], (mine_in, mine_out, rest[n_scratch:])

    def at_steps(self, mine, is_first, is_middle, is_last):
        for cond, fn in ((is_first, self.first), (is_middle, self.middle), (is_last, self.last)):
            pl.when(cond)(lambda fn=fn: fn(*mine))


HBM_SPEC = pl.BlockSpec(memory_space=pltpu.HBM)


N_AUG = 3
QCHUNKS = ((0, 128), (128, 128), (256, 128))
KSUB = 384
AHEAD = 5
AHEAD_BWD = 1


def _fox_prep(proj, cum):
    def body(q0_ref, q1_ref, k0_ref, k1_ref, v0_ref, v1_ref, c_ref, qa_ref, ka_ref, vt_ref):
        half = _lane_half(TM)
        lane = lax.broadcasted_iota(jnp.int32, (TM, BLK), 1)
        for pp in range(4):
            cols = slice(pp * BLK, (pp + 1) * BLK)
            q_ref, k_ref, v_ref = ((q0_ref, k0_ref, v0_ref), (q1_ref, k1_ref, v1_ref))[pp // 2]
            part = slice((pp % 2) * BLK, (pp % 2 + 1) * BLK)
            qs = q_ref[:, part].astype(F32) * (SCALE * LOG2E)
            kp = k_ref[:, part].astype(F32)
            vp = v_ref[:, part]
            vt_ref[cols, :] = vp.astype(F32).T.astype(BF16)
            for e in range(2):
                h = 2 * pp + e
                a = (1 - e) * HALF
                blk = slice(h * BLK, (h + 1) * BLK)
                hi, mid, lo = _split3(-LOG2E * c_ref[:, h:h + 1])
                q_aug = jnp.where(half == e, qs, jnp.where((lane >= a) & (lane < a + N_AUG), 1.0, 0.0))
                k_aug = jnp.where(half == e, kp, jnp.where(
                    lane == a, hi.astype(F32), jnp.where(lane == a + 1, mid.astype(F32), jnp.where(
                        lane == a + 2, lo.astype(F32), 0.0))))
                qa_ref[blk, :] = q_aug.T.astype(BF16)
                ka_ref[:, blk] = k_aug.astype(BF16)

    row = lambda blk: pl.BlockSpec((TM, W2), lambda i: (i, blk))
    wide = pl.BlockSpec((TM, 1024), lambda i: (i, 0))
    return pl.pallas_call(
        body, grid=(NT,),
        in_specs=[row(QB), row(QB + 1), row(KB), row(KB + 1), row(VB), row(VB + 1),
                  pl.BlockSpec((TM, BLK), lambda i: (i, 0))],
        out_specs=[pl.BlockSpec((1024, TM), lambda i: (0, i)), wide, pl.BlockSpec((512, TM), lambda i: (0, i))],
        out_shape=[SDS((1024, LP), BF16), SDS((LP, 1024), BF16), SDS((512, LP), BF16)],
        compiler_params=_cparams(("parallel",)), name="fox_prep")(proj, proj, proj, proj, proj, proj, cum)


def _over_keys(reduce, x):
    slabs = x.reshape(x.shape[0] // HALF, HALF, x.shape[1])
    return reduce(reduce(slabs, axis=0), axis=0, keepdims=True)


def _fox_valid_t(qi, kj, c, r):
    krow = kj * TM + r * KSUB + lax.broadcasted_iota(jnp.int32, (KSUB, c[1]), 0)
    qrow = qi * TM + c[0] + lax.broadcasted_iota(jnp.int32, (KSUB, c[1]), 1)
    return (krow <= qrow) & ((krow >= PAD_ROWS) | (qrow < PAD_ROWS))


def _fox_fwd(q_aug, k_aug, v_t, rider):
    pairs = [(qi, kj) for qi in range(NT) for kj in range(qi + 1)]
    n_pairs = len(pairs)

    def body(qi_ref, kj_ref, *refs):
        (q_ref, k_ref, vt_ref), (o_ref, lse_ref), (m_s, l_s, acc_s), mine = rider.split(refs, 3, 2, 3)
        n = pl.program_id(0)
        qi = qi_ref[n]
        kj = kj_ref[n]
        rider.at_steps(mine, n == 0, n == n_pairs // 2, n == n_pairs - 1)

        @pl.when(kj == 0)
        def _():
            m_s[...] = jnp.full_like(m_s, NEG)
            l_s[...] = jnp.zeros_like(l_s)
            acc_s[...] = jnp.zeros_like(acc_s)

        def tile(masked):
            steps = [(h, c, r) for h in range(N_HEADS) for c in QCHUNKS for r in range(TM // KSUB)]

            def scores(h, c, r):
                blk = slice(h * BLK, (h + 1) * BLK)
                return jnp.dot(k_ref[r * KSUB:(r + 1) * KSUB, blk], q_ref[blk, c[0]:c[0] + c[1]],
                               preferred_element_type=F32)

            ahead = [scores(*st) for st in steps[:AHEAD]]
            for n, (h, c, r) in enumerate(steps):
                s_t = ahead.pop(0)
                if n + AHEAD < len(steps):
                    ahead.append(scores(*steps[n + AHEAD]))
                cs = slice(c[0], c[0] + c[1])
                if masked:
                    s_t = jnp.where(_fox_valid_t(qi, kj, c, r), s_t, NEG)
                m_prev = m_s[h, :, cs]
                m_new = jnp.maximum(m_prev, _over_keys(jnp.max, s_t))
                p_t = jnp.exp2(s_t - m_new)
                alpha = jnp.exp2(m_prev - m_new)
                l_s[h, :, cs] = alpha * l_s[h, :, cs] + _over_keys(jnp.sum, p_t)
                m_s[h, :, cs] = m_new
                vt = vt_ref[h * HALF:(h + 1) * HALF, r * KSUB:(r + 1) * KSUB]
                acc_s[h, :, cs] = acc_s[h, :, cs] * alpha + jnp.dot(vt, p_t.astype(BF16),
                                                                    preferred_element_type=F32)

        @pl.when((kj < qi) & (kj > 0))
        def _():
            tile(False)

        @pl.when((kj == qi) | (kj == 0))
        def _():
            tile(True)

        @pl.when(kj == qi)
        def _():
            for pp in range(4):
                both = jnp.concatenate([acc_s[2 * pp] * (1.0 / l_s[2 * pp]),
                                        acc_s[2 * pp + 1] * (1.0 / l_s[2 * pp + 1])], axis=0)
                o_ref[:, pp * BLK:(pp + 1) * BLK] = both.T.astype(BF16)
            for h in range(N_HEADS):
                lse_ref[h] = m_s[h] * LN2 + jnp.log(l_s[h])

    grid_spec = pltpu.PrefetchScalarGridSpec(
        num_scalar_prefetch=2, grid=(n_pairs,),
        in_specs=[pl.BlockSpec((1024, TM), lambda n, qi, kj: (0, qi[n])),
                  pl.BlockSpec((TM, 1024), lambda n, qi, kj: (kj[n], 0)),
                  pl.BlockSpec((512, TM), lambda n, qi, kj: (0, kj[n]))] + [HBM_SPEC] * len(rider.operands),
        out_specs=[pl.BlockSpec((TM, 512), lambda n, qi, kj: (qi[n], 0)),
                   pl.BlockSpec((N_HEADS, 1, TM), lambda n, qi, kj: (0, 0, qi[n]))]
        + [HBM_SPEC] * len(rider.out_shapes),
        scratch_shapes=[pltpu.VMEM((N_HEADS, 1, TM), F32), pltpu.VMEM((N_HEADS, 1, TM), F32),
                        pltpu.VMEM((N_HEADS, HALF, TM), F32)] + rider.scratch())
    o_b, lse, *carried = pl.pallas_call(
        body, grid_spec=grid_spec,
        out_shape=[SDS((LP, 512), BF16), SDS((N_HEADS, 1, LP), F32)] + rider.out_shapes,
        compiler_params=_cparams(("arbitrary",)), name="fox_fwd",
    )(jnp.asarray([p[0] for p in pairs], jnp.int32), jnp.asarray([p[1] for p in pairs], jnp.int32),
      q_aug, k_aug, v_t, *rider.operands)
    return o_b, lse, carried


def _fox_bwd(proj, o_b, dmix, lse, ck_t, rider):
    pairs = [(kj, qi) for kj in range(NT) for qi in range(kj, NT)]
    n_pairs = len(pairs)

    def body(kj_ref, qi_ref, *refs):
        ((q0_ref, q1_ref, k0_ref, k1_ref, v0_ref, v1_ref, o_ref, do_ref, lse_ref, ck_ref),
         (dq_ref, dk_ref, dv_ref, dck_ref, dcq_ref), (dk_s, dv_s, dck_s), mine) = rider.split(refs, 10, 5, 3)
        n = pl.program_id(0)
        kj = kj_ref[n]
        qi = qi_ref[n]
        rider.at_steps(mine, n == 0, n == n_pairs // 2, n == n_pairs - 1)

        @pl.when(n == 0)
        def _():
            dq_ref[...] = jnp.zeros_like(dq_ref)
            dcq_ref[...] = jnp.zeros_like(dcq_ref)

        @pl.when(qi == kj)
        def _():
            dk_s[...] = jnp.zeros_like(dk_s)
            dv_s[...] = jnp.zeros_like(dv_s)
            dck_s[...] = jnp.zeros_like(dck_s)

        def tile(masked):
            valid = _fox_valid(qi, kj) if masked else None
            half = _lane_half(TM)
            q0 = pl.multiple_of(qi * TM, TM)
            lane = lax.broadcasted_iota(jnp.int32, (TM, BLK), 1)
            row_sums = jnp.zeros((TM, BLK), F32)
            pair_ops = {}

            def operands(pp):
                if pp not in pair_ops:
                    cols = slice(pp * BLK, (pp + 1) * BLK)
                    q_ref, k_ref, v_ref = ((q0_ref, k0_ref, v0_ref), (q1_ref, k1_ref, v1_ref))[pp // 2]
                    part = slice((pp % 2) * BLK, (pp % 2 + 1) * BLK)
                    pair_ops[pp] = ((q_ref[:, part].astype(F32) * SCALE).astype(BF16), k_ref[:, part],
                                    v_ref[:, part], do_ref[:, cols])
                return pair_ops[pp]

            def scores(pp, e):
                qs, kp, vp, dop = operands(pp)
                ke = jnp.where(half == e, kp, jnp.zeros_like(kp))
                ve = jnp.where(half == e, vp, jnp.zeros_like(vp))
                return (lax.dot_general(qs, ke, NT_DIMS, preferred_element_type=F32),
                        lax.dot_general(dop, ve, NT_DIMS, preferred_element_type=F32), ke)

            steps = [(pp, e) for pp in range(4) for e in range(2)]
            ahead = [scores(*st) for st in steps[:AHEAD_BWD]]
            for n, (pp, e) in enumerate(steps):
                raw, dp, ke = ahead.pop(0)
                if n + AHEAD_BWD < len(steps):
                    ahead.append(scores(*steps[n + AHEAD_BWD]))
                h = 2 * pp + e
                cols = slice(pp * BLK, (pp + 1) * BLK)
                qs, kp, vp, dop = operands(pp)
                if e == 0:
                    prod = dop.astype(F32) * o_ref[:, cols].astype(F32)
                    d0 = jnp.sum(jnp.where(half == 0, prod, 0.0), axis=1, keepdims=True)
                    d1 = jnp.sum(prod, axis=1, keepdims=True) - d0
                    dq = jnp.zeros((TM, BLK), F32)
                    dks, dvs = [], []
                t = raw - ck_ref[h] - lse_ref[h]
                if masked:
                    t = jnp.where(valid, t, NEG)
                p = jnp.exp(t)
                ds = p * (dp - (d0 if e == 0 else d1))
                dck_s[h] += jnp.sum(ds, axis=0, keepdims=True)
                row_sums = jnp.where(lane == h, jnp.sum(ds, axis=1, keepdims=True), row_sums)
                ds_b = ds.astype(BF16)
                dq = dq + jnp.dot(ds_b, ke, preferred_element_type=F32)
                dks.append(lax.dot_general(ds_b, qs, TN_DIMS, preferred_element_type=F32))
                dvs.append(lax.dot_general(p.astype(BF16), dop, TN_DIMS, preferred_element_type=F32))
                if e == 1:
                    dq_ref[pl.ds(q0, TM), cols] += dq
                    dk_s[pp] += jnp.where(half == 0, dks[0], dks[1])
                    dv_s[pp] += jnp.where(half == 0, dvs[0], dvs[1])
            dcq_ref[pl.ds(q0, TM), :] += row_sums

        @pl.when((qi > kj) & (kj > 0))
        def _():
            tile(False)

        @pl.when((qi == kj) | (kj == 0))
        def _():
            tile(True)

        @pl.when(qi == NT - 1)
        def _():
            for pp in range(4):
                cols = slice(pp * BLK, (pp + 1) * BLK)
                dk_ref[:, cols] = dk_s[pp].astype(BF16)
                dv_ref[:, cols] = dv_s[pp].astype(BF16)
            dck_ref[...] = dck_s[...]

    qrow = lambda blk, w=512: pl.BlockSpec((TM, w), lambda n, kj, qi: (qi[n], blk))
    krow = lambda blk: pl.BlockSpec((TM, W2), lambda n, kj, qi: (kj[n], blk))
    grid_spec = pltpu.PrefetchScalarGridSpec(
        num_scalar_prefetch=2, grid=(n_pairs,),
        in_specs=[qrow(QB, W2), qrow(QB + 1, W2), krow(KB), krow(KB + 1), krow(VB), krow(VB + 1), qrow(0), qrow(1),
                  pl.BlockSpec((N_HEADS, TM, 1), lambda n, kj, qi: (0, qi[n], 0)),
                  pl.BlockSpec((N_HEADS, 1, TM), lambda n, kj, qi: (0, 0, kj[n]))] + [HBM_SPEC] * len(rider.operands),
        out_specs=[pl.BlockSpec((LP, 512), lambda n, kj, qi: (0, 0)),
                   pl.BlockSpec((TM, 512), lambda n, kj, qi: (kj[n], 0)),
                   pl.BlockSpec((TM, 512), lambda n, kj, qi: (kj[n], 0)),
                   pl.BlockSpec((N_HEADS, 1, TM), lambda n, kj, qi: (0, 0, kj[n])),
                   pl.BlockSpec((LP, BLK), lambda n, kj, qi: (0, 0))] + [HBM_SPEC] * len(rider.out_shapes),
        scratch_shapes=[pltpu.VMEM((4, TM, BLK), F32), pltpu.VMEM((4, TM, BLK), F32),
                        pltpu.VMEM((N_HEADS, 1, TM), F32)] + rider.scratch())
    dq, dk, dv, dck, dcq, *carried = pl.pallas_call(
        body, grid_spec=grid_spec,
        out_shape=[SDS((LP, 512), F32), SDS((LP, 512), BF16), SDS((LP, 512), BF16), SDS((N_HEADS, 1, LP), F32),
                   SDS((LP, BLK), F32)] + rider.out_shapes,
        compiler_params=_cparams(("arbitrary",)), name="fox_bwd",
    )(jnp.asarray([p[0] for p in pairs], jnp.int32), jnp.asarray([p[1] for p in pairs], jnp.int32),
      proj, proj, proj, proj, proj, proj, o_b, dmix, lse, ck_t, *rider.operands)
    return dq, dk, dv, dck, dcq, carried


N_SEG = 3
N_KEY = N_SEG * BLK
GROUP = 4
QW = GROUP * BLK


def _bucket_tables_t():
    return np.ascontiguousarray(_bucket_tables().transpose(0, 2, 1))


def _stack_heads(ref, g, scale):
    half = _lane_half(BLK)
    out = []
    for pair in range(2):
        x = ref[:, (2 * g + pair) * BLK:(2 * g + pair + 1) * BLK].astype(F32) * scale
        swapped = pltpu.roll(x, HALF, 1)
        for e in range(2):
            out.append(jnp.where(half == g, x if e == g else swapped, 0.0).astype(BF16))
    return jnp.concatenate(out, axis=0)


def _unstack_heads(x_t, g, ref, scale):
    for pair in range(2):
        both = jnp.concatenate([x_t[:, (2 * pair) * BLK:(2 * pair + 1) * BLK],
                                x_t[:, (2 * pair + 1) * BLK:(2 * pair + 2) * BLK]], axis=0)
        ref[:, (2 * g + pair) * BLK:(2 * g + pair + 1) * BLK] = (both.T * scale).astype(ref.dtype)


def _swa_tables(tab_ref, sink_ref, bkt_ref, tbl, sink_row):
    kk = lax.broadcasted_iota(jnp.int32, (BLK, BLK), 0)
    qq = lax.broadcasted_iota(jnp.int32, (BLK, BLK), 1)
    neg = jnp.full((BLK, BLK), NEG, F32)
    lane = lax.broadcasted_iota(jnp.int32, (1, QW), 1) // BLK
    for g in range(2):
        row = jnp.zeros((1, QW), F32)
        for hh in range(GROUP):
            h = GROUP * g + hh
            cols = slice(hh * BLK, (hh + 1) * BLK)
            row = jnp.where(lane == hh, sink_ref[0, h], row)

            def step(b, carry, h=h):
                t = tab_ref[b, h]
                return jnp.where(bkt_ref[0] == b, t, carry[0]), jnp.where(bkt_ref[1] == b, t, carry[1])
            zero = jnp.zeros((BLK, BLK), F32)
            cur, prev = lax.fori_loop(0, N_BUCKETS, step, (zero, zero))
            far = jnp.full((BLK, BLK), tab_ref[N_BUCKETS - 1, h], F32)
            causal = jnp.where(kk <= qq, cur, neg)
            segments = [
                (neg, neg, jnp.where(kk >= PAD_ROWS, causal, neg)),
                (jnp.where(kk >= PAD_ROWS, prev, neg), neg, causal),
                (jnp.where(kk >= PAD_ROWS, far, neg), jnp.where(kk > qq, prev, neg), causal)]
            for case in range(3):
                for seg in range(N_SEG):
                    tbl[case, g, seg * BLK:(seg + 1) * BLK, cols] = segments[case][seg]
        sink_row[g] = row


def _swa_prep(proj):
    rows = LP // 3

    def body(k_ref, v_ref, kt_ref, vt_ref):
        kt_ref[...] = k_ref[...].astype(F32).T.astype(BF16)
        vt_ref[...] = v_ref[...].astype(F32).T.astype(BF16)

    col = pl.BlockSpec((BLK, rows), lambda i: (0, i))
    return pl.pallas_call(
        body, grid=(3,),
        in_specs=[pl.BlockSpec((rows, BLK), lambda i: (i, KA)), pl.BlockSpec((rows, BLK), lambda i: (i, VA))],
        out_specs=[col, col], out_shape=[SDS((BLK, LP), BF16)] * 2,
        compiler_params=_cparams(("parallel",)), name="swa_prep")(proj, proj)


def _segments(ref, i, by_rows):
    starts = [0, pl.multiple_of(jnp.maximum(i - 1, 0) * BLK, BLK), pl.multiple_of(i * BLK, BLK)]
    if by_rows:
        return jnp.concatenate([ref[pl.ds(s, BLK), :] for s in starts], axis=0)
    return jnp.concatenate([ref[:, pl.ds(s, BLK)] for s in starts], axis=1)


def _swa_fwd(proj, vt_a, rel_bias, sinks, bkt_t, rider):
    def body(*refs):
        ((tab_ref, sink_ref, bkt_ref, q_ref, k_ref, vt_ref), (o_ref, lse_ref),
         (tbl, sink_row), mine) = rider.split(refs, 6, 2, 2)
        i = pl.program_id(0)
        rider.at_steps(mine, i == 0, i == NBLK // 2, i == NBLK - 1)

        @pl.when(i == 0)
        def _():
            _swa_tables(tab_ref, sink_ref, bkt_ref, tbl, sink_row)

        case = jnp.minimum(i, 2)
        k_cat = _segments(k_ref, i, True)
        vt_cat = _segments(vt_ref, i, False)
        raw = [lax.dot_general(k_cat, _stack_heads(q_ref, g, SCALE), NT_DIMS, preferred_element_type=F32)
               for g in range(2)]
        for g in range(2):
            s_t = raw[g] + tbl[case, g]
            sink = sink_row[g]
            m = jnp.maximum(_over_keys(jnp.max, s_t), sink)
            p_t = jnp.exp(s_t - m)
            l = _over_keys(jnp.sum, p_t) + jnp.exp(sink - m)
            o_t = jnp.dot(vt_cat[g * HALF:(g + 1) * HALF, :], p_t.astype(BF16), preferred_element_type=F32)
            _unstack_heads(o_t * (1.0 / l), g, o_ref, 1.0)
            lse = m + jnp.log(l)
            for hh in range(GROUP):
                lse_ref[GROUP * g + hh] = lse[:, hh * BLK:(hh + 1) * BLK]

    smem = pl.BlockSpec(memory_space=pltpu.SMEM)
    o_a, lse, *carried = pl.pallas_call(
        body, grid=(NBLK,),
        in_specs=[smem, smem, pl.BlockSpec((2, BLK, BLK), lambda i: (0, 0, 0)),
                  pl.BlockSpec((BLK, 512), lambda i: (i, QA)), pl.BlockSpec((LP, BLK), lambda i: (0, KA)),
                  pl.BlockSpec((BLK, LP), lambda i: (0, 0))] + [HBM_SPEC] * len(rider.operands),
        out_specs=[pl.BlockSpec((BLK, 512), lambda i: (i, 0)),
                   pl.BlockSpec((N_HEADS, 1, BLK), lambda i: (0, 0, i))] + [HBM_SPEC] * len(rider.out_shapes),
        out_shape=[SDS((LP, 512), BF16), SDS((N_HEADS, 1, LP), F32)] + rider.out_shapes,
        scratch_shapes=[pltpu.VMEM((3, 2, N_KEY, QW), F32), pltpu.VMEM((2, 1, QW), F32)] + rider.scratch(),
        compiler_params=_cparams(("arbitrary",)), name="swa_fwd",
    )(rel_bias, sinks, bkt_t, proj, proj, vt_a, *rider.operands)
    return o_a, lse, carried


def _swa_bwd(proj, kt_a, o_a, dmix, lse, rel_bias, sinks, bkt_t):
    def body(tab_ref, sink_ref, bkt_ref, q_ref, k_ref, v_ref, kt_ref, o_ref, do_ref, lse_ref,
             dq_ref, dk_ref, dv_ref, dbias_ref, dsink_ref, tbl, sink_row, acc, dsk):
        i = pl.program_id(0)

        @pl.when(i == 0)
        def _():
            _swa_tables(tab_ref, sink_ref, bkt_ref, tbl, sink_row)
            dk_ref[...] = jnp.zeros_like(dk_ref)
            dv_ref[...] = jnp.zeros_like(dv_ref)
            acc[...] = jnp.zeros_like(acc)
            dsk[...] = jnp.zeros_like(dsk)

        case = jnp.minimum(i, 2)
        first = jnp.full((BLK, QW), i, jnp.int32) == 1
        k_cat = _segments(k_ref, i, True)
        v_cat = _segments(v_ref, i, True)
        kt_cat = _segments(kt_ref, i, False)
        dk_cat = jnp.zeros((N_KEY, BLK), F32)
        dv_cat = jnp.zeros((N_KEY, BLK), F32)
        for g in range(2):
            d_parts = []
            for pair in range(2):
                cols = slice((2 * g + pair) * BLK, (2 * g + pair + 1) * BLK)
                prod_t = (do_ref[:, cols].astype(F32) * o_ref[:, cols].astype(F32)).T
                d_parts += [jnp.sum(prod_t[:HALF], axis=0, keepdims=True),
                            jnp.sum(prod_t[HALF:], axis=0, keepdims=True)]
            d_row = jnp.concatenate(d_parts, axis=1)
            lse_row = jnp.concatenate([lse_ref[GROUP * g + hh] for hh in range(GROUP)], axis=1)
            q_st = _stack_heads(q_ref, g, SCALE)
            do_st = _stack_heads(do_ref, g, 1.0)
            s_t = lax.dot_general(k_cat, q_st, NT_DIMS, preferred_element_type=F32) + tbl[case, g]
            p_t = jnp.exp(s_t - lse_row)
            dp_t = lax.dot_general(v_cat, do_st, NT_DIMS, preferred_element_type=F32)
            ds_t = p_t * (dp_t - d_row)
            dsk[g] += -jnp.exp(sink_row[g] - lse_row) * d_row
            acc[g, 0:BLK] += jnp.where(first, 0.0, ds_t[0:BLK])
            acc[g, BLK:2 * BLK] += jnp.where(first, ds_t[0:BLK], ds_t[BLK:2 * BLK])
            acc[g, 2 * BLK:N_KEY] += ds_t[2 * BLK:N_KEY]
            ds_b = ds_t.astype(BF16)
            dk_cat = dk_cat + jnp.dot(ds_b, q_st, preferred_element_type=F32)
            dv_cat = dv_cat + jnp.dot(p_t.astype(BF16), do_st, preferred_element_type=F32)
            dq_t = jnp.dot(kt_cat[g * HALF:(g + 1) * HALF, :], ds_b, preferred_element_type=F32)
            _unstack_heads(dq_t, g, dq_ref, SCALE)

        prev0 = pl.multiple_of(jnp.maximum(i - 1, 0) * BLK, BLK)
        cur0 = pl.multiple_of(i * BLK, BLK)
        for ref, cat in ((dk_ref, dk_cat), (dv_ref, dv_cat)):
            ref[0:BLK, :] += cat[0:BLK]
            ref[pl.ds(prev0, BLK), :] += cat[BLK:2 * BLK]
            ref[pl.ds(cur0, BLK), :] += cat[2 * BLK:N_KEY]

        @pl.when(i == NBLK - 1)
        def _():
            lane = lax.broadcasted_iota(jnp.int32, (1, BLK), 1)

            def per_bucket(b, carry):
                row = jnp.zeros((1, BLK), F32)
                for h in range(N_HEADS):
                    g, cols = h // GROUP, slice((h % GROUP) * BLK, (h % GROUP + 1) * BLK)
                    val = (jnp.sum(jnp.where(bkt_ref[0] == b, acc[g, 2 * BLK:N_KEY, cols], 0.0), keepdims=True)
                           + jnp.sum(jnp.where(bkt_ref[1] == b, acc[g, BLK:2 * BLK, cols], 0.0), keepdims=True))
                    row = jnp.where(lane == h, val, row)
                dbias_ref[pl.ds(b, 1), :] = row
                return carry

            lax.fori_loop(0, N_BUCKETS, per_bucket, 0)
            far = jnp.zeros((1, BLK), F32)
            dsr = jnp.zeros((1, BLK), F32)
            for h in range(N_HEADS):
                g, cols = h // GROUP, slice((h % GROUP) * BLK, (h % GROUP + 1) * BLK)
                far = jnp.where(lane == h, jnp.sum(acc[g, 0:BLK, cols], keepdims=True), far)
                dsr = jnp.where(lane == h, jnp.sum(dsk[g, :, cols], keepdims=True), dsr)
            dbias_ref[N_BUCKETS - 1:N_BUCKETS, :] += far
            dsink_ref[...] = dsr

    smem = pl.BlockSpec(memory_space=pltpu.SMEM)
    blk512 = lambda col: pl.BlockSpec((BLK, 512), lambda i: (i, col))
    full = lambda r, c: pl.BlockSpec((r, c), lambda i: (0, 0))
    return pl.pallas_call(
        body, grid=(NBLK,),
        in_specs=[smem, smem, pl.BlockSpec((2, BLK, BLK), lambda i: (0, 0, 0)), blk512(QA),
                  pl.BlockSpec((LP, BLK), lambda i: (0, KA)), pl.BlockSpec((LP, BLK), lambda i: (0, VA)),
                  full(BLK, LP), blk512(0), blk512(0), pl.BlockSpec((N_HEADS, 1, BLK), lambda i: (0, 0, i))],
        out_specs=[blk512(0), full(LP, BLK), full(LP, BLK), full(N_BUCKETS, BLK), full(1, BLK)],
        out_shape=[SDS((LP, 512), BF16), SDS((LP, BLK), F32), SDS((LP, BLK), F32),
                   SDS((N_BUCKETS, BLK), F32), SDS((1, BLK), F32)],
        scratch_shapes=[pltpu.VMEM((3, 2, N_KEY, QW), F32), pltpu.VMEM((2, 1, QW), F32),
                        pltpu.VMEM((2, N_KEY, QW), F32), pltpu.VMEM((2, 1, QW), F32)],
        compiler_params=_cparams(("arbitrary",)), name="swa_bwd",
    )(rel_bias, sinks, bkt_t, proj, proj, proj, kt_a, o_a, dmix, lse)


def _local_step(x, tgt, meta, rel_bias, g_pre_mix, g_post_mix, g_pre_ffn, g_post_ffn, b_forget, sinks,
                w_in_b, out_rider, out_weight, ffn_rider, ffn_weights, early_grads):
    bkt_t = jnp.asarray(_bucket_tables_t())
    b_p = jnp.pad(b_forget, ((0, 0), (0, BLK - N_HEADS)))

    h0, hn1, proj, f = _pre_mix(x, meta, g_pre_mix, w_in_b)
    kt_a, vt_a = _swa_prep(proj)
    o_a, lse_a, carried_out = _swa_fwd(proj, vt_a, rel_bias, sinks, bkt_t, out_rider)
    w_out_b = out_weight(carried_out)
    cum = _forget_cumsum(f, b_p)
    ck_t = cum[:, :N_HEADS].T.reshape(N_HEADS, 1, LP)
    q_aug, k_aug, v_t = _fox_prep(proj, cum)
    o_b, lse_row, carried = _fox_fwd(q_aug, k_aug, v_t, ffn_rider)
    lse_b = lse_row.reshape(N_HEADS, LP, 1)
    w_gu_b, w_dn_b = ffn_weights(carried)
    a, h1, hn2 = _attn_out(o_a, o_b, w_out_b, h0, g_post_mix, g_pre_ffn)
    g, u, act = _ffn_up(hn2, w_gu_b)
    dff, dy, loss_blk, dg_post_ffn = _ffn_down_loss(act, w_dn_b, h1, tgt, g_post_ffn)

    dw_dn = _mm_tn([act], dff, FF_T, "dw_down", BF16)
    dg, du = _ffn_down_bwd(dff, w_dn_b, g, u)
    dw_gu = _dw_gate_up(hn2, dg, du)
    dh1, da, dg_pre_ffn, dg_post_mix = _ffn_up_bwd(dg, du, w_gu_b, h1, a, dy, g_pre_ffn, g_post_mix)
    dw_out = _mm_tn([o_a, o_b], da, D_MODEL, "dw_out", BF16)
    dmix = _attn_out_bwd(da, w_out_b)
    dq_b, dk_b, dv_b, dck, dcq, landed = _fox_bwd(proj, o_b, dmix, lse_b, ck_t, early_grads(dw_gu, dw_dn, dw_out))
    dq_a, dk_a, dv_a, dbias, dsink = _swa_bwd(proj, kt_a, o_a, dmix, lse_a, rel_bias, sinks, bkt_t)
    dcum = dcq - jnp.pad(dck.reshape(N_HEADS, LP).T, ((0, 0), (0, BLK - N_HEADS)))
    df, db = _forget_cumsum_bwd(dcum, f, b_p)
    dproj, dx, dmeta, dg_pre_mix = _pre_mix_bwd(dq_a, dq_b, dk_b, dv_b, dk_a, dv_a, df, w_in_b, h0, dh1, g_pre_mix)
    dw_in = _mm_tn([hn1], dproj, D_MODEL, "dw_in", BF16)

    return dict(loss=loss_blk[0, 0], grad_x=dx, meta=dmeta,
                rel_bias=dbias[:, :N_HEADS], ln_pre_mix=dg_pre_mix, ln_post_mix=dg_post_mix,
                ln_pre_ffn=dg_pre_ffn, ln_post_ffn=dg_post_ffn, b_forget=db[:, :N_HEADS],
                sinks=dsink[:, :N_HEADS], w_in=dw_in, w_out=dw_out, w_gate_up=dw_gu, w_down=dw_dn,
                landed=landed)


N_SMALL = 24
LOSS_ROW = 6


def _place():
    x, y, c = lax.axis_index("x"), lax.axis_index("y"), lax.axis_index("c")
    return x, y, c, [(1 - x, y), (x, 1 - y), (1 - x, 1 - y)]


def _run_alone(rider, name):
    a, b = len(rider.operands), len(rider.out_shapes)

    def body(*refs):
        mine = (refs[:a], refs[a:a + b], refs[a + b:])
        rider.first(*mine)
        rider.middle(*mine)
        rider.last(*mine)

    return pl.pallas_call(body, in_specs=[HBM_SPEC] * a, out_specs=[HBM_SPEC] * b, out_shape=rider.out_shapes,
                          scratch_shapes=rider.scratch(), name=name)(*rider.operands)


def _gather_rider(shards, own_too, by_columns=()):
    n = len(shards)

    def slot(a, outs, chip, h):
        if a in by_columns:
            cols = shards[a].shape[2]
            return outs[a].at[h, :, pl.ds(pl.multiple_of(chip * cols, BLK), cols)]
        return outs[a].at[chip, h]

    def own_copies(ins, outs, sems):
        x, y, _, _ = _place()
        if not own_too:
            return []
        return [pltpu.make_async_copy(ins[a].at[h], slot(a, outs, 2 * x + y, h), sems[2].at[2 * a + h])
                for a in range(n) for h in range(2)]

    def copies(ins, outs, sems):
        send_sems, recv_sems = sems[:2]
        x, y, c, others = _place()
        chip = 2 * x + y
        sibling = (x, y, 1 - c)

        def rc(a, k, src, dst, to):
            return pltpu.make_async_remote_copy(src_ref=src, dst_ref=dst, send_sem=send_sems.at[6 * a + k],
                                                recv_sem=recv_sems.at[6 * a + k], device_id=to, device_id_type=MESH)

        pairs = [(a, j, ox, oy) for a in range(n) for j, (ox, oy) in enumerate(others)]
        there = lambda a, ox, oy, h: slot(a, outs, 2 * ox + oy, h)
        return dict(
            sent=lambda: [rc(a, j, ins[a].at[c], slot(a, outs, chip, c), (ox, oy, c)) for a, j, ox, oy in pairs],
            landed=lambda: [rc(a, j, there(a, ox, oy, c), there(a, ox, oy, c), sibling) for a, j, ox, oy in pairs],
            passed=lambda: [rc(a, 3 + j, there(a, ox, oy, c), there(a, ox, oy, c), sibling)
                            for a, j, ox, oy in pairs],
            arriving=lambda: [rc(a, 3 + j, there(a, ox, oy, 1 - c), there(a, ox, oy, 1 - c), sibling)
                              for a, j, ox, oy in pairs])

    def first(*mine):
        for cp in copies(*mine)["sent"]() + own_copies(*mine):
            cp.start()

    def middle(*mine):
        kinds = copies(*mine)
        for got, cp in zip(kinds["landed"](), kinds["passed"]()):
            got.wait_recv()
            cp.start()

    def last(*mine):
        kinds = copies(*mine)
        for cp in kinds["arriving"]():
            cp.wait_recv()
        for cp in kinds["sent"]() + kinds["passed"]():
            cp.wait_send()
        for cp in own_copies(*mine):
            cp.wait()

    shapes = [SDS((2, s.shape[1], 4 * s.shape[2]) if a in by_columns else (4,) + s.shape, s.dtype)
              for a, s in enumerate(shards)]
    return _Rider(shards, shapes, [6 * n, 6 * n] + [2 * n] * own_too, first, middle, last)


def _swap_rider(grads):
    n = len(grads)
    slabs = [(a, s) for a in range(n) for s in range(grads[a].shape[0])]

    def copies(ins, outs, sems):
        x, y, c, _ = _place()
        return [pltpu.make_async_remote_copy(
            src_ref=ins[a].at[s, 1 - c], dst_ref=outs[a].at[s], send_sem=sems[0].at[k], recv_sem=sems[1].at[k],
            device_id=(x, y, 1 - c), device_id_type=MESH) for k, (a, s) in enumerate(slabs)]

    def first(*mine):
        for cp in copies(*mine):
            cp.start()

    def middle(*mine):
        pass

    def last(*mine):
        for cp in copies(*mine):
            cp.wait()

    return _Rider(grads, [SDS(g.shape[:1] + g.shape[2:], g.dtype) for g in grads], [len(slabs), len(slabs)],
                  first, middle, last)


def _pair_sum(g, got, c_arr, name):
    n_s, rh, cc = got.shape

    def body(c_ref, g_ref, p_ref, o_ref):
        o_ref[0] = (g_ref[0, 0].astype(F32) + p_ref[0].astype(F32)).astype(BF16)

    grid_spec = pltpu.PrefetchScalarGridSpec(
        num_scalar_prefetch=1, grid=(n_s,),
        in_specs=[pl.BlockSpec((1, 1, rh, cc), lambda s, c_ref: (s, c_ref[0], 0, 0)),
                  pl.BlockSpec((1, rh, cc), lambda s, c_ref: (s, 0, 0))],
        out_specs=pl.BlockSpec((1, rh, cc), lambda s, c_ref: (s, 0, 0)))
    return pl.pallas_call(body, grid_spec=grid_spec, out_shape=SDS((n_s, rh, cc), BF16),
                          compiler_params=_cparams(("parallel",)), name=name)(c_arr, g, got)


def _direct_rider(grads):
    n = len(grads)

    def copies(ins, outs, sems):
        x, y, c, others = _place()
        peers = [(x, y, 1 - c)] + [(ox, oy, c) for ox, oy in others] + [(ox, oy, 1 - c) for ox, oy in others]
        return [pltpu.make_async_remote_copy(
            src_ref=ins[a].at[2 * px + py, pc], dst_ref=outs[a].at[k], send_sem=sems[0].at[7 * a + k],
            recv_sem=sems[1].at[7 * a + k], device_id=(px, py, pc), device_id_type=MESH)
            for a in range(n) for k, (px, py, pc) in enumerate(peers)]

    def first(*mine):
        for cp in copies(*mine):
            cp.start()

    def middle(*mine):
        pass

    def last(*mine):
        for cp in copies(*mine):
            cp.wait()

    return _Rider(grads, [SDS((7,) + g.shape[2:], g.dtype) for g in grads], [7 * n, 7 * n], first, middle, last)


def _owner_sum(grads, landed, own_arr, after, name):
    rh, cc = landed.shape[1:]
    tr = rh // 2

    def body(own_ref, g_ref, p_ref, after_ref, o_ref):
        total = g_ref[0, 0].astype(F32)
        for k in range(7):
            total = total + p_ref[k].astype(F32)
        o_ref[...] = total

    grid_spec = pltpu.PrefetchScalarGridSpec(
        num_scalar_prefetch=1, grid=(2,),
        in_specs=[pl.BlockSpec((1, 1, tr, cc), lambda i, own: (own[0], own[1], i, 0)),
                  pl.BlockSpec((7, tr, cc), lambda i, own: (0, i, 0)), pl.BlockSpec(memory_space=pl.ANY)],
        out_specs=pl.BlockSpec((tr, cc), lambda i, own: (i, 0)))
    return pl.pallas_call(body, grid_spec=grid_spec, out_shape=SDS((rh, cc), F32),
                          compiler_params=_cparams(("parallel",)), name=name)(own_arr, grads, landed, after)


SEM_SPEC = pl.BlockSpec(memory_space=pltpu.SEMAPHORE)
N_LATE = 10


def _late_copies(part_ref, landed_ref, small_ref, all_ref, send_sems, recv_sems):
    x, y, c, others = _place()
    me = 4 * x + 2 * y + c
    peers = [(x, y, 1 - c)] + [(ox, oy, c) for ox, oy in others] + [(ox, oy, 1 - c) for ox, oy in others]
    big = [pltpu.make_async_remote_copy(
        src_ref=part_ref.at[2 * ox + oy], dst_ref=landed_ref.at[j], send_sem=send_sems.at[j], recv_sem=recv_sems.at[j],
        device_id=(ox, oy, c), device_id_type=MESH) for j, (ox, oy) in enumerate(others)]
    small = [pltpu.make_async_remote_copy(
        src_ref=small_ref, dst_ref=all_ref.at[me], send_sem=send_sems.at[3 + k], recv_sem=recv_sems.at[3 + k],
        device_id=peer, device_id_type=MESH) for k, peer in enumerate(peers)]
    return big + small


def _late_exchange_start(part, small):
    def body(part_ref, landed_ref, small_ref, all_ref, send_sems, recv_sems, part_o, landed_o, small_o, all_o, token):
        for cp in _late_copies(part_ref, landed_ref, small_ref, all_ref, send_sems, recv_sems):
            cp.start()
        token[...] = jnp.zeros_like(token)

    hbm = lambda a: pltpu.HBM(a.shape, a.dtype)
    landed = lax.empty((3,) + part.shape[1:], part.dtype)
    everyone = lax.empty((8,) + small.shape, small.dtype)
    operands = [pltpu.with_memory_space_constraint(a, pltpu.HBM) for a in (part, landed, small, everyone)]
    return pl.pallas_call(
        body, name="late_exchange_start",
        out_shape=(pltpu.SemaphoreType.DMA((N_LATE,)), pltpu.SemaphoreType.DMA((N_LATE,)),
                   hbm(part), hbm(landed), hbm(small), hbm(everyone), SDS((8, BLK), F32)),
        in_specs=[HBM_SPEC] * 4,
        out_specs=(SEM_SPEC, SEM_SPEC, HBM_SPEC, HBM_SPEC, HBM_SPEC, HBM_SPEC, pl.BlockSpec(memory_space=pltpu.VMEM)),
        input_output_aliases={0: 2, 1: 3, 2: 4, 3: 5},
        compiler_params=pltpu.CompilerParams(has_side_effects=pltpu.SideEffectType.DATAFLOW_SIDE_EFFECTING),
    )(*operands)


def _late_exchange_wait(send_sems, recv_sems, part, landed, small, everyone, after):
    def body(part_ref, landed_ref, small_ref, all_ref, send_sems, recv_sems, after_ref, part_o, landed_o, small_o, all_o):
        for cp in _late_copies(part_ref, landed_ref, small_ref, all_ref, send_sems, recv_sems):
            cp.wait_send()
            cp.wait_recv()

    hbm = lambda a: pltpu.HBM(a.shape, a.dtype)
    out = pl.pallas_call(
        body, name="late_exchange_wait",
        out_shape=(hbm(part), hbm(landed), hbm(small), hbm(everyone)),
        in_specs=[HBM_SPEC] * 4 + [SEM_SPEC, SEM_SPEC, pl.BlockSpec(memory_space=pl.ANY)],
        out_specs=(HBM_SPEC,) * 4, input_output_aliases={0: 0, 1: 1, 2: 2, 3: 3},
        compiler_params=pltpu.CompilerParams(has_side_effects=pltpu.SideEffectType.DATAFLOW_SIDE_EFFECTING),
    )(part, landed, small, everyone, send_sems, recv_sems, after)
    return out[0], out[1], out[3]


def _chip_sum(parts, landed, chip_arr, name):
    rh, cc = landed.shape[1:]
    tr = rh // 2

    def body(chip_ref, own_ref, p_ref, o_ref):
        o_ref[...] = ((own_ref[0].astype(F32) + p_ref[0].astype(F32)) + p_ref[1].astype(F32)) + p_ref[2].astype(F32)

    grid_spec = pltpu.PrefetchScalarGridSpec(
        num_scalar_prefetch=1, grid=(2,),
        in_specs=[pl.BlockSpec((1, tr, cc), lambda i, chip_ref: (chip_ref[0], i, 0)),
                  pl.BlockSpec((3, tr, cc), lambda i, chip_ref: (0, i, 0))],
        out_specs=pl.BlockSpec((tr, cc), lambda i, chip_ref: (i, 0)))
    return pl.pallas_call(body, grid_spec=grid_spec, out_shape=SDS((rh, cc), F32),
                          compiler_params=_cparams(("parallel",)), name=name)(chip_arr, parts, landed)


def _device_sum(p):
    def body(p_ref, o_ref):
        acc = p_ref[0]
        for k in range(1, 8):
            acc = acc + p_ref[k]
        o_ref[...] = acc

    return pl.pallas_call(body, out_shape=SDS(p.shape[1:], F32), name="small_sum")(p)


def _join_halves(halves, name):
    n = len(halves)

    def body(*refs):
        ins, outs = refs[:n], refs[n:2 * n]
        send_sems, recv_sems = refs[2 * n:]
        x, y, c, _ = _place()
        copies = [pltpu.make_async_remote_copy(
            src_ref=ins[a], dst_ref=outs[a], send_sem=send_sems.at[a], recv_sem=recv_sems.at[a],
            device_id=(x, y, 1 - c), device_id_type=MESH) for a in range(n)]
        for cp in copies:
            cp.start()
        for cp in copies:
            cp.wait()

    return pl.pallas_call(
        body, in_specs=[HBM_SPEC] * n, out_specs=[HBM_SPEC] * n,
        out_shape=[SDS(h.shape, h.dtype) for h in halves],
        scratch_shapes=[pltpu.SemaphoreType.DMA((n,)), pltpu.SemaphoreType.DMA((n,))],
        name=name)(*halves)


def _adamw(w, g, m, v, name, tr=None):
    rows, cols = w.shape
    tr = tr or rows
    assert rows % tr == 0

    def body(w_ref, g_ref, m_ref, v_ref, d_ref, nm_ref, nv_ref):
        gg = g_ref[...]
        nm = ADAM_B1 * m_ref[...] + (1.0 - ADAM_B1) * gg
        nv = ADAM_B2 * v_ref[...] + (1.0 - ADAM_B2) * (gg * gg)
        nm_ref[...] = nm
        nv_ref[...] = nv
        m_hat = nm / (1.0 - ADAM_B1 ** ADAM_STEP)
        v_hat = nv / (1.0 - ADAM_B2 ** ADAM_STEP)
        d_ref[...] = -ADAM_LR * (m_hat / (jnp.sqrt(v_hat) + ADAM_EPS) + ADAM_WD * w_ref[...])

    blk = pl.BlockSpec((tr, cols), lambda i: (i, 0))
    return pl.pallas_call(
        body, grid=(rows // tr,), in_specs=[blk] * 4, out_specs=[blk] * 3,
        out_shape=[SDS((rows, cols), F32)] * 3,
        compiler_params=_cparams(("parallel",)), name=name)(w, g, m, v)


def _adamw_halves(w, mine, theirs, m, v, c_arr, name):
    rows, cols = w.shape
    rh = rows // 2
    tr = rh if rh <= 352 else 256
    nh = rh // tr

    def body(c_ref, w_ref, mine_ref, theirs_ref, m_ref, v_ref, g_ref, d_ref, nm_ref, nv_ref):
        own = jnp.full((tr, cols), pl.program_id(0), jnp.int32) == c_ref[0]
        gg = jnp.where(own, mine_ref[...], theirs_ref[...])
        g_ref[...] = gg
        nm = ADAM_B1 * m_ref[...] + (1.0 - ADAM_B1) * gg
        nv = ADAM_B2 * v_ref[...] + (1.0 - ADAM_B2) * (gg * gg)
        nm_ref[...] = nm
        nv_ref[...] = nv
        m_hat = nm / (1.0 - ADAM_B1 ** ADAM_STEP)
        v_hat = nv / (1.0 - ADAM_B2 ** ADAM_STEP)
        d_ref[...] = -ADAM_LR * (m_hat / (jnp.sqrt(v_hat) + ADAM_EPS) + ADAM_WD * w_ref[...])

    whole = pl.BlockSpec((tr, cols), lambda hh, i, c_ref: (hh * nh + i, 0))
    part = pl.BlockSpec((tr, cols), lambda hh, i, c_ref: (i, 0))
    grid_spec = pltpu.PrefetchScalarGridSpec(
        num_scalar_prefetch=1, grid=(2, nh), in_specs=[whole, part, part, whole, whole], out_specs=[whole] * 4)
    return pl.pallas_call(body, grid_spec=grid_spec, out_shape=[SDS((rows, cols), F32)] * 4,
                          compiler_params=_cparams(("parallel", "parallel")), name=name)(c_arr, w, mine, theirs, m, v)


def _pack_small(pre_mix, post_mix, pre_ffn, post_ffn, rel_bias, b_forget, sinks):
    def at(row, v):
        return jnp.pad(v, ((row, 7 - row), (0, D_MODEL - v.shape[1])))
    return (at(0, pre_mix) + at(1, post_mix) + at(2, pre_ffn) + at(3, post_ffn)
            + at(4, rel_bias.reshape(1, N_BUCKETS * N_HEADS)) + at(5, jnp.concatenate([b_forget, sinks], axis=1)))


def _unpack_small(p):
    return dict(ln_pre_mix=p[0:1], ln_post_mix=p[1:2], ln_pre_ffn=p[2:3], ln_post_ffn=p[3:4],
                rel_bias=p[4, :N_BUCKETS * N_HEADS].reshape(N_BUCKETS, N_HEADS),
                b_forget=p[5:6, 0:N_HEADS], sinks=p[5:6, N_HEADS:2 * N_HEADS])


WEIGHTS = ("meta_tokens", "rel_bias", "ln_pre_mix", "ln_post_mix", "ln_pre_ffn", "ln_post_ffn",
           "w_in", "b_forget", "sinks", "w_out", "w_gate_up", "w_down")


def kernel(x, meta_tokens, rel_bias, ln_pre_mix, ln_post_mix, ln_pre_ffn, ln_post_ffn, w_in, b_forget, sinks, w_out, w_gate_up, w_down, loss_target, m_meta_tokens, m_rel_bias, m_ln_pre_mix, m_ln_post_mix, m_ln_pre_ffn, m_ln_post_ffn, m_w_in, m_b_forget, m_sinks, m_w_out, m_w_gate_up, m_w_down, v_meta_tokens, v_rel_bias, v_ln_pre_mix, v_ln_post_mix, v_ln_pre_ffn, v_ln_post_ffn, v_w_in, v_b_forget, v_sinks, v_w_out, v_w_gate_up, v_w_down):
    xi, yi, ci = lax.axis_index("x"), lax.axis_index("y"), lax.axis_index("c")
    chip = 2 * xi + yi
    c_arr = jnp.reshape(ci, (1,)).astype(jnp.int32)

    def halves(w, dtype):
        return w.astype(dtype).reshape(2, w.shape[0] // 2, w.shape[1])

    def with_own(gathered, shards):
        return [lax.dynamic_update_slice(got, own[None], (chip, 0, 0, 0)) for got, own in zip(gathered, shards)]

    shards = [halves(jnp.transpose(w_in[0]), BF16), halves(meta_tokens, F32)]
    gw_in, g_meta = with_own(_run_alone(_gather_rider(shards, False), "gather_mixer_weights"), shards)
    out_shards = [halves(w_out[0], BF16)]
    ffn_shards = [halves(w_gate_up[0], BF16), halves(w_down[0], BF16)]

    def ffn_weights(carried):
        gw_gu, gw_dn = carried
        return gw_gu.reshape(D_MODEL, 2 * D_FF), gw_dn.reshape(D_FF, D_MODEL)

    early = {}

    def early_grads(dw_gu, dw_dn, dw_out):
        early["grads"] = [dw_out.reshape(4, 2, 128, D_MODEL), dw_gu.reshape(4, 2, 512, FF_T),
                          dw_dn.reshape(4, 2, 352, D_MODEL)]
        return _direct_rider(early["grads"])
    w_in_b = jnp.pad(gw_in.reshape(D_PROJ, D_MODEL), ((0, D_PROJ_P - D_PROJ), (0, 0)))
    meta_all = g_meta.reshape(4, N_META, D_MODEL // 4).transpose(1, 0, 2).reshape(N_META, D_MODEL)

    loc = _local_step(x[0], loss_target[0], meta_all, rel_bias, ln_pre_mix, ln_post_mix, ln_pre_ffn, ln_post_ffn,
                      b_forget, sinks, w_in_b, _gather_rider(out_shards, True),
                      lambda carried: carried[0].reshape(D_MODEL, D_MODEL),
                      _gather_rider(ffn_shards, True, by_columns=(0,)), ffn_weights, early_grads)

    small = jnp.concatenate(
        [_pack_small(loc["ln_pre_mix"], loc["ln_post_mix"], loc["ln_pre_ffn"], loc["ln_post_ffn"],
                     loc["rel_bias"], loc["b_forget"], loc["sinks"])
         + jnp.pad(loc["loss"].reshape(1, 1), ((LOSS_ROW, 7 - LOSS_ROW), (0, D_MODEL - 1))), loc["meta"]], axis=0)

    dw_in = loc["w_in"].reshape(1, 2, D_MODEL // 2, D_PROJ_P)
    (got_in,) = _run_alone(_swap_rider([dw_in]), "swap_halves_late")
    half_sum = _pair_sum(dw_in, got_in, c_arr, "pair_sum_late")
    part_in = half_sum[0, :, :D_PROJ].reshape(D_MODEL // 2, 4, D_PROJ // 4).transpose(1, 0, 2)
    send_sems, recv_sems, part_sent, landing, small_sent, everyone, token = _late_exchange_start(part_in, small)
    chip_arr = jnp.reshape(chip, (1,)).astype(jnp.int32)
    own_arr = jnp.stack([chip, ci]).astype(jnp.int32)
    grad, delta, new_m, new_v = {}, {}, {}, {}

    def update(names, mine):
        theirs = _join_halves(mine, "join_" + names[0])
        big = dict(w_in=(w_in, m_w_in, v_w_in), w_out=(w_out, m_w_out, v_w_out),
                   w_gate_up=(w_gate_up, m_w_gate_up, v_w_gate_up), w_down=(w_down, m_w_down, v_w_down))
        for name, g_mine, g_theirs in zip(names, mine, theirs):
            w, m, v = big[name]
            g, d, nm, nv = _adamw_halves(w[0], g_mine, g_theirs, m[0], v[0], c_arr, "adamw_" + name)
            grad[name], delta[name], new_m[name], new_v[name] = g[None], d[None], nm[None], nv[None]

    update(("w_out", "w_gate_up", "w_down"),
           [_owner_sum(g, l, own_arr, token, "owner_sum_%d" % a)
            for a, (g, l) in enumerate(zip(early["grads"], loc["landed"]))])
    part_back, landed_in, small_all = _late_exchange_wait(send_sems, recv_sems, part_sent, landing, small_sent,
                                                         everyone, new_v["w_down"])
    mine_in = _chip_sum(part_back, landed_in, chip_arr, "chip_sum_in")
    (theirs_in,) = _join_halves([mine_in], "join_w_in")
    g_w_in = jnp.where(ci == 0, jnp.concatenate([mine_in, theirs_in], axis=0),
                       jnp.concatenate([theirs_in, mine_in], axis=0))
    view = lambda a: jnp.transpose(a).reshape(D_PROJ // 4 * 8, BLK)
    back = lambda a: jnp.transpose(a.reshape(D_PROJ // 4, D_MODEL))[None]
    d, nm, nv = _adamw(view(w_in[0]), view(g_w_in), view(m_w_in[0]), view(v_w_in[0]), "adamw_w_in",
                       tr=D_PROJ // 4 * 4)
    grad["w_in"], delta["w_in"], new_m["w_in"], new_v["w_in"] = g_w_in[None], back(d), back(nm), back(nv)
    me = 4 * xi + 2 * yi + ci
    small_sum = _device_sum(lax.dynamic_update_slice(small_all, small[None], (me, 0, 0)))
    g_meta_tokens = lax.dynamic_slice(small_sum[8:N_SMALL], (0, chip * (D_MODEL // 4)), (N_META, D_MODEL // 4))
    g_small = small_sum[0:8]
    grad.update(_unpack_small(g_small))
    grad.update(meta_tokens=g_meta_tokens)
    delta["meta_tokens"], new_m["meta_tokens"], new_v["meta_tokens"] = _adamw(
        meta_tokens, g_meta_tokens, m_meta_tokens, v_meta_tokens, "adamw_meta")
    d, nm, nv = _adamw(
        _pack_small(ln_pre_mix, ln_post_mix, ln_pre_ffn, ln_post_ffn, rel_bias, b_forget, sinks), g_small,
        _pack_small(m_ln_pre_mix, m_ln_post_mix, m_ln_pre_ffn, m_ln_post_ffn, m_rel_bias, m_b_forget, m_sinks),
        _pack_small(v_ln_pre_mix, v_ln_post_mix, v_ln_pre_ffn, v_ln_post_ffn, v_rel_bias, v_b_forget, v_sinks),
        "adamw_small")
    delta.update(_unpack_small(d))
    new_m.update(_unpack_small(nm))
    new_v.update(_unpack_small(nv))

    loss = small_sum[LOSS_ROW, 0]
    return (loss,loc["grad_x"][None], *[grad[k] for k in WEIGHTS], *[delta[k] for k in WEIGHTS],
            *[new_m[k] for k in WEIGHTS], *[new_v[k] for k in WEIGHTS])
```
